```python
import jax, jax.numpy as jnp
from jax import lax
import numpy as np

D_MODEL = 1024
BATCH = 16
SEQ = 2048
DEPTH = 2

N_META = 16
EPS = 1e-6
SSM_D_INNER = 2 * D_MODEL
SSM_HEAD_DIM = 64
SSM_HEADS = SSM_D_INNER // SSM_HEAD_DIM
SSM_GROUPS = 4
SSM_STATE = 128
SSM_CONV = 4
SSM_CHUNK = 128
SSM_CONV_DIM = SSM_D_INNER + 2 * SSM_GROUPS * SSM_STATE
MLA_HEADS = 8
MLA_Q_LORA = D_MODEL // 2
MLA_KV_LORA = D_MODEL // 4
MLA_NOPE = 128
MLA_ROPE = 64
MLA_V = 128
ROPE_THETA = 10000.0
Q_BLOCK = 128
D_FF = 4 * D_MODEL
IN_SPLITS = [SSM_D_INNER, SSM_CONV_DIM, SSM_HEADS, MLA_Q_LORA, MLA_KV_LORA, MLA_ROPE, D_MODEL, D_MODEL]
IN_DIM = sum(IN_SPLITS)

kernel_name = "hybrid_ssd_mla_meta_block"


def rms_norm(x, w):
    xf = x.astype(jnp.float32)
    y = xf * lax.rsqrt(jnp.mean(xf * xf, axis=-1, keepdims=True) + EPS)
    return (y * w.astype(jnp.float32)).astype(x.dtype)


def rope_tables(n_pos, dim):
    inv = ROPE_THETA ** (-jnp.arange(0, dim, 2, dtype=jnp.float32) / dim)
    ang = jnp.arange(n_pos, dtype=jnp.float32)[:, None] * inv[None, :]
    return jnp.cos(ang), jnp.sin(ang)


def apply_rope(x, cos, sin):
    x1, x2 = jnp.split(x.astype(jnp.float32), 2, axis=-1)
    return jnp.concatenate([x1 * cos - x2 * sin, x2 * cos + x1 * sin], axis=-1).astype(x.dtype)


def causal_dwconv(x, w, b):
    k, c = w.shape
    y = lax.conv_general_dilated(x, w[:, None, :].astype(x.dtype), window_strides=(1,),
                                 padding=[(k - 1, 0)], dimension_numbers=("NWC", "WIO", "NWC"),
                                 feature_group_count=c)
    return y + b.astype(x.dtype)


def ssd_chunked(xdt, adt, bm, cm):
    b, t, h, p = xdt.shape
    g, n = bm.shape[-2:]
    e = h // g
    q = SSM_CHUNK
    c = t // q
    xc = xdt.reshape(b, c, q, g, e, p)
    a = adt.astype(jnp.float32).reshape(b, c, q, g, e).transpose(0, 3, 4, 1, 2)
    bc = bm.reshape(b, c, q, g, n)
    cc = cm.reshape(b, c, q, g, n)
    a_cs = jnp.cumsum(a, axis=-1)
    causal = np.tril(np.ones((q, q), dtype=bool))
    l_dec = jnp.exp(jnp.where(causal, a_cs[..., :, None] - a_cs[..., None, :], -jnp.inf))
    cb = jnp.einsum("bclgn,bcsgn->bcgls", cc, bc)
    y_diag = jnp.einsum("bcgls,bgecls,bcsgep->bclgep", cb, l_dec, xc)
    decay_states = jnp.exp(a_cs[..., -1:] - a_cs)
    states = jnp.einsum("bcsgn,bgecs,bcsgep->bcgepn", bc, decay_states, xc)
    chunk_decay = jnp.exp(a_cs[..., -1])

    def step(hs, inp):
        s_c, d_c = inp
        return hs * d_c[..., None, None] + s_c, hs

    h0 = jnp.zeros((b, g, e, p, n), jnp.float32)
    _, prev = lax.scan(step, h0, (states.astype(jnp.float32).transpose(1, 0, 2, 3, 4, 5),
                                  chunk_decay.transpose(3, 0, 1, 2)))
    prev = prev.transpose(1, 0, 2, 3, 4, 5)
    y_off = jnp.einsum("bclgn,bcgepn,bgecl->bclgep", cc, prev, jnp.exp(a_cs))
    return (y_diag + y_off).reshape(b, t, h, p).astype(xdt.dtype)


def ssd_mixer(z, xbc, dt, conv_w, conv_b, dt_bias, a_log, d_skip, norm_w):
    bsz, L, _ = xbc.shape
    xbc = jax.nn.silu(causal_dwconv(xbc, conv_w, conv_b))
    xs, bm, cm = jnp.split(xbc, [SSM_D_INNER, SSM_D_INNER + SSM_GROUPS * SSM_STATE], axis=-1)
    xs = xs.reshape(bsz, L, SSM_HEADS, SSM_HEAD_DIM)
    bm = bm.reshape(bsz, L, SSM_GROUPS, SSM_STATE)
    cm = cm.reshape(bsz, L, SSM_GROUPS, SSM_STATE)
    dt = jax.nn.softplus(dt.astype(jnp.float32) + dt_bias.astype(jnp.float32))
    a = -jnp.exp(a_log.astype(jnp.float32))
    pad = SSM_CHUNK - N_META

    def lpad(t):
        return jnp.pad(t, [(0, 0), (pad, 0)] + [(0, 0)] * (t.ndim - 2))

    y = ssd_chunked(lpad(xs * dt[..., None]), lpad(dt * a), lpad(bm), lpad(cm))[:, pad:]
    y = y + xs * d_skip[:, None].astype(xs.dtype)
    y = y.reshape(bsz, L, SSM_D_INNER) * jax.nn.silu(z)
    gsz = SSM_D_INNER // SSM_GROUPS
    y = rms_norm(y.reshape(bsz, L, SSM_GROUPS, gsz), norm_w.reshape(SSM_GROUPS, gsz))
    return y.reshape(bsz, L, SSM_D_INNER)


def mla_mixer(c_q, c_kv, k_rope, q_norm_w, kv_norm_w, w_uq, w_ukv, cos, sin):
    bsz, L, _ = c_q.shape
    qf = (rms_norm(c_q, q_norm_w) @ w_uq).reshape(bsz, L, MLA_HEADS, MLA_NOPE + MLA_ROPE)
    q_nope, q_pe = jnp.split(qf, [MLA_NOPE], axis=-1)
    q_pe = apply_rope(q_pe, cos[:, None, :], sin[:, None, :])
    kv = (rms_norm(c_kv, kv_norm_w) @ w_ukv).reshape(bsz, L, MLA_HEADS, MLA_NOPE + MLA_V)
    k_nope, v = jnp.split(kv, [MLA_NOPE], axis=-1)
    k_pe = apply_rope(k_rope, cos, sin)
    scale = (MLA_NOPE + MLA_ROPE) ** -0.5
    bounds = [0, N_META] + list(range(N_META + Q_BLOCK, L + 1, Q_BLOCK))
    outs = []
    for qs, qe in zip(bounds[:-1], bounds[1:]):
        s = (jnp.einsum("bqhd,bkhd->bhqk", q_nope[:, qs:qe], k_nope[:, :qe])
             + jnp.einsum("bqhr,bkr->bhqk", q_pe[:, qs:qe], k_pe[:, :qe])).astype(jnp.float32) * scale
        mask = np.arange(qs, qe)[:, None] >= np.arange(qe)[None, :]
        p = jax.nn.softmax(jnp.where(mask, s, -jnp.inf), axis=-1).astype(v.dtype)
        outs.append(jnp.einsum("bhqk,bkhv->bqhv", p, v[:, :qe]))
    o = jnp.concatenate(outs, axis=1)
    return o.reshape(bsz, L, MLA_HEADS * MLA_V)


def _fwd_setup_inputs(seed: int = 0) -> dict:
    key = jax.random.key(seed)
    ks = jax.random.split(key, 24)
    f32 = jnp.float32
    nrm = lambda k, shape, s: jax.random.normal(k, shape, f32) * s
    gain = lambda k, shape: 1.0 + 0.02 * jax.random.normal(k, shape, f32)
    res_scale = (2 * DEPTH) ** -0.5
    dt0 = jnp.exp(jax.random.uniform(ks[5], (DEPTH, SSM_HEADS), f32, np.log(1e-3), np.log(1e-1)))
    dt_bias = dt0 + jnp.log(-jnp.expm1(-dt0))
    return {
        "x": nrm(ks[0], (BATCH, SEQ, D_MODEL), 1.0),
        "meta_tokens": nrm(ks[1], (N_META, D_MODEL), 1.0),
        "norm_mix_w": gain(ks[2], (DEPTH, D_MODEL)),
        "w_in": nrm(ks[3], (DEPTH, D_MODEL, IN_DIM), D_MODEL ** -0.5),
        "conv_w": nrm(ks[4], (DEPTH, SSM_CONV, SSM_CONV_DIM), SSM_CONV ** -0.5),
        "conv_b": nrm(ks[6], (DEPTH, SSM_CONV_DIM), 0.01),
        "dt_bias": dt_bias,
        "a_log": jnp.log(jax.random.uniform(ks[7], (DEPTH, SSM_HEADS), f32, 1.0, 16.0)),
        "d_skip": 1.0 + 0.1 * jax.random.normal(ks[8], (DEPTH, SSM_HEADS), f32),
        "ssm_norm_w": gain(ks[9], (DEPTH, SSM_D_INNER)),
        "q_norm_w": gain(ks[10], (DEPTH, MLA_Q_LORA)),
        "kv_norm_w": gain(ks[11], (DEPTH, MLA_KV_LORA)),
        "w_uq": nrm(ks[12], (DEPTH, MLA_Q_LORA, MLA_HEADS * (MLA_NOPE + MLA_ROPE)), MLA_Q_LORA ** -0.5),
        "w_ukv": nrm(ks[13], (DEPTH, MLA_KV_LORA, MLA_HEADS * (MLA_NOPE + MLA_V)), MLA_KV_LORA ** -0.5),
        "w_branch_ssm": nrm(ks[14], (DEPTH, SSM_D_INNER, D_MODEL), SSM_D_INNER ** -0.5),
        "w_branch_mla": nrm(ks[15], (DEPTH, MLA_HEADS * MLA_V, D_MODEL), (MLA_HEADS * MLA_V) ** -0.5),
        "w_out": nrm(ks[16], (DEPTH, D_MODEL, D_MODEL), D_MODEL ** -0.5 * res_scale),
        "norm_mlp_w": gain(ks[17], (DEPTH, D_MODEL)),
        "w_mlp_up": nrm(ks[18], (DEPTH, D_MODEL, D_FF), D_MODEL ** -0.5),
        "w_mlp_down": nrm(ks[19], (DEPTH, D_FF, D_MODEL), D_FF ** -0.5 * res_scale),
        "final_norm_w": gain(ks[20], (D_MODEL,)),
    }


def _fwd_reference(x, meta_tokens, norm_mix_w, w_in, conv_w, conv_b, dt_bias, a_log, d_skip, ssm_norm_w,
              q_norm_w, kv_norm_w, w_uq, w_ukv, w_branch_ssm, w_branch_mla, w_out, norm_mlp_w,
              w_mlp_up, w_mlp_down, final_norm_w):
    bsz = x.shape[0]
    meta = jnp.broadcast_to(meta_tokens.astype(x.dtype)[None], (bsz, N_META, D_MODEL))
    h = jnp.concatenate([meta, x], axis=1)
    L = h.shape[1]
    cos, sin = rope_tables(L, MLA_ROPE)
    split_idx = np.cumsum(IN_SPLITS)[:-1].tolist()
    for i in range(DEPTH):
        u = rms_norm(h, norm_mix_w[i])
        z, xbc, dt, c_q, c_kv, k_rope, g_ssm, g_mla = jnp.split(u @ w_in[i], split_idx, axis=-1)
        y_ssm = ssd_mixer(z, xbc, dt, conv_w[i], conv_b[i], dt_bias[i], a_log[i], d_skip[i], ssm_norm_w[i])
        y_mla = mla_mixer(c_q, c_kv, k_rope, q_norm_w[i], kv_norm_w[i], w_uq[i], w_ukv[i], cos, sin)
        mixed = (jax.nn.sigmoid(g_ssm) * (y_ssm @ w_branch_ssm[i])
                 + jax.nn.sigmoid(g_mla) * (y_mla @ w_branch_mla[i]))
        h = h + mixed @ w_out[i]
        v = rms_norm(h, norm_mlp_w[i])
        h = h + jnp.square(jax.nn.relu(v @ w_mlp_up[i])) @ w_mlp_down[i]
    return rms_norm(h, final_norm_w)[:, N_META:]


import jax as _jax
import jax.numpy as _jnp

TWIN_FORMAT = 'train_step'
FWD_PARAMS = ['x', 'meta_tokens', 'norm_mix_w', 'w_in', 'conv_w', 'conv_b', 'dt_bias', 'a_log', 'd_skip', 'ssm_norm_w', 'q_norm_w', 'kv_norm_w', 'w_uq', 'w_ukv', 'w_branch_ssm', 'w_branch_mla', 'w_out', 'norm_mlp_w', 'w_mlp_up', 'w_mlp_down', 'final_norm_w']
TWIN_WEIGHTS = ['meta_tokens', 'norm_mix_w', 'w_in', 'conv_w', 'conv_b', 'dt_bias', 'a_log', 'd_skip', 'ssm_norm_w', 'q_norm_w', 'kv_norm_w', 'w_uq', 'w_ukv', 'w_branch_ssm', 'w_branch_mla', 'w_out', 'norm_mlp_w', 'w_mlp_up', 'w_mlp_down', 'final_norm_w']
TWIN_DIFF_INPUT = 'x'
TWIN_INPUTS = ['x', 'meta_tokens', 'norm_mix_w', 'w_in', 'conv_w', 'conv_b', 'dt_bias', 'a_log', 'd_skip', 'ssm_norm_w', 'q_norm_w', 'kv_norm_w', 'w_uq', 'w_ukv', 'w_branch_ssm', 'w_branch_mla', 'w_out', 'norm_mlp_w', 'w_mlp_up', 'w_mlp_down', 'final_norm_w', 'loss_target', 'm_meta_tokens', 'm_norm_mix_w', 'm_w_in', 'm_conv_w', 'm_conv_b', 'm_dt_bias', 'm_a_log', 'm_d_skip', 'm_ssm_norm_w', 'm_q_norm_w', 'm_kv_norm_w', 'm_w_uq', 'm_w_ukv', 'm_w_branch_ssm', 'm_w_branch_mla', 'm_w_out', 'm_norm_mlp_w', 'm_w_mlp_up', 'm_w_mlp_down', 'm_final_norm_w', 'v_meta_tokens', 'v_norm_mix_w', 'v_w_in', 'v_conv_w', 'v_conv_b', 'v_dt_bias', 'v_a_log', 'v_d_skip', 'v_ssm_norm_w', 'v_q_norm_w', 'v_kv_norm_w', 'v_w_uq', 'v_w_ukv', 'v_w_branch_ssm', 'v_w_branch_mla', 'v_w_out', 'v_norm_mlp_w', 'v_w_mlp_up', 'v_w_mlp_down', 'v_final_norm_w']
TWIN_OUTPUTS = ['loss', 'grad_x', 'grad_meta_tokens', 'grad_norm_mix_w', 'grad_w_in', 'grad_conv_w', 'grad_conv_b', 'grad_dt_bias', 'grad_a_log', 'grad_d_skip', 'grad_ssm_norm_w', 'grad_q_norm_w', 'grad_kv_norm_w', 'grad_w_uq', 'grad_w_ukv', 'grad_w_branch_ssm', 'grad_w_branch_mla', 'grad_w_out', 'grad_norm_mlp_w', 'grad_w_mlp_up', 'grad_w_mlp_down', 'grad_final_norm_w', 'delta_meta_tokens', 'delta_norm_mix_w', 'delta_w_in', 'delta_conv_w', 'delta_conv_b', 'delta_dt_bias', 'delta_a_log', 'delta_d_skip', 'delta_ssm_norm_w', 'delta_q_norm_w', 'delta_kv_norm_w', 'delta_w_uq', 'delta_w_ukv', 'delta_w_branch_ssm', 'delta_w_branch_mla', 'delta_w_out', 'delta_norm_mlp_w', 'delta_w_mlp_up', 'delta_w_mlp_down', 'delta_final_norm_w', 'new_m_meta_tokens', 'new_m_norm_mix_w', 'new_m_w_in', 'new_m_conv_w', 'new_m_conv_b', 'new_m_dt_bias', 'new_m_a_log', 'new_m_d_skip', 'new_m_ssm_norm_w', 'new_m_q_norm_w', 'new_m_kv_norm_w', 'new_m_w_uq', 'new_m_w_ukv', 'new_m_w_branch_ssm', 'new_m_w_branch_mla', 'new_m_w_out', 'new_m_norm_mlp_w', 'new_m_w_mlp_up', 'new_m_w_mlp_down', 'new_m_final_norm_w', 'new_v_meta_tokens', 'new_v_norm_mix_w', 'new_v_w_in', 'new_v_conv_w', 'new_v_conv_b', 'new_v_dt_bias', 'new_v_a_log', 'new_v_d_skip', 'new_v_ssm_norm_w', 'new_v_q_norm_w', 'new_v_kv_norm_w', 'new_v_w_uq', 'new_v_w_ukv', 'new_v_w_branch_ssm', 'new_v_w_branch_mla', 'new_v_w_out', 'new_v_norm_mlp_w', 'new_v_w_mlp_up', 'new_v_w_mlp_down', 'new_v_final_norm_w']
TWIN_LEAF_KINDS = {'loss': 'loss', 'grad_x': 'grad_x', 'grad_meta_tokens': 'grad_w', 'grad_norm_mix_w': 'grad_w', 'grad_w_in': 'grad_w', 'grad_conv_w': 'grad_w', 'grad_conv_b': 'grad_w', 'grad_dt_bias': 'grad_w', 'grad_a_log': 'grad_w', 'grad_d_skip': 'grad_w', 'grad_ssm_norm_w': 'grad_w', 'grad_q_norm_w': 'grad_w', 'grad_kv_norm_w': 'grad_w', 'grad_w_uq': 'grad_w', 'grad_w_ukv': 'grad_w', 'grad_w_branch_ssm': 'grad_w', 'grad_w_branch_mla': 'grad_w', 'grad_w_out': 'grad_w', 'grad_norm_mlp_w': 'grad_w', 'grad_w_mlp_up': 'grad_w', 'grad_w_mlp_down': 'grad_w', 'grad_final_norm_w': 'grad_w', 'delta_meta_tokens': 'delta_w', 'delta_norm_mix_w': 'delta_w', 'delta_w_in': 'delta_w', 'delta_conv_w': 'delta_w', 'delta_conv_b': 'delta_w', 'delta_dt_bias': 'delta_w', 'delta_a_log': 'delta_w', 'delta_d_skip': 'delta_w', 'delta_ssm_norm_w': 'delta_w', 'delta_q_norm_w': 'delta_w', 'delta_kv_norm_w': 'delta_w', 'delta_w_uq': 'delta_w', 'delta_w_ukv': 'delta_w', 'delta_w_branch_ssm': 'delta_w', 'delta_w_branch_mla': 'delta_w', 'delta_w_out': 'delta_w', 'delta_norm_mlp_w': 'delta_w', 'delta_w_mlp_up': 'delta_w', 'delta_w_mlp_down': 'delta_w', 'delta_final_norm_w': 'delta_w', 'new_m_meta_tokens': 'new_m', 'new_m_norm_mix_w': 'new_m', 'new_m_w_in': 'new_m', 'new_m_conv_w': 'new_m', 'new_m_conv_b': 'new_m', 'new_m_dt_bias': 'new_m', 'new_m_a_log': 'new_m', 'new_m_d_skip': 'new_m', 'new_m_ssm_norm_w': 'new_m', 'new_m_q_norm_w': 'new_m', 'new_m_kv_norm_w': 'new_m', 'new_m_w_uq': 'new_m', 'new_m_w_ukv': 'new_m', 'new_m_w_branch_ssm': 'new_m', 'new_m_w_branch_mla': 'new_m', 'new_m_w_out': 'new_m', 'new_m_norm_mlp_w': 'new_m', 'new_m_w_mlp_up': 'new_m', 'new_m_w_mlp_down': 'new_m', 'new_m_final_norm_w': 'new_m', 'new_v_meta_tokens': 'new_v', 'new_v_norm_mix_w': 'new_v', 'new_v_w_in': 'new_v', 'new_v_conv_w': 'new_v', 'new_v_conv_b': 'new_v', 'new_v_dt_bias': 'new_v', 'new_v_a_log': 'new_v', 'new_v_d_skip': 'new_v', 'new_v_ssm_norm_w': 'new_v', 'new_v_q_norm_w': 'new_v', 'new_v_kv_norm_w': 'new_v', 'new_v_w_uq': 'new_v', 'new_v_w_ukv': 'new_v', 'new_v_w_branch_ssm': 'new_v', 'new_v_w_branch_mla': 'new_v', 'new_v_w_out': 'new_v', 'new_v_norm_mlp_w': 'new_v', 'new_v_w_mlp_up': 'new_v', 'new_v_w_mlp_down': 'new_v', 'new_v_final_norm_w': 'new_v'}


def _forward(args):
    return _fwd_reference(*[args[k] for k in FWD_PARAMS])


def _output_shape():
    out = _jax.eval_shape(lambda: _forward(_fwd_setup_inputs(0)))
    return out.shape, out.dtype

N_MICROBATCH = 1
ADAM_LR = 0.001
ADAM_B1 = 0.9
ADAM_B2 = 0.999
ADAM_EPS = 1e-08
ADAM_WD = 0.01
ADAM_STEP = 10
PER_EXAMPLE_BATCH_AXIS = {'x': 0, 'loss_target': 0}
SHARED_INPUTS = []
_WEIGHT_DTYPES = {'meta_tokens': _jnp.float32, 'norm_mix_w': _jnp.float32, 'w_in': _jnp.float32, 'conv_w': _jnp.float32, 'conv_b': _jnp.float32, 'dt_bias': _jnp.float32, 'a_log': _jnp.float32, 'd_skip': _jnp.float32, 'ssm_norm_w': _jnp.float32, 'q_norm_w': _jnp.float32, 'kv_norm_w': _jnp.float32, 'w_uq': _jnp.float32, 'w_ukv': _jnp.float32, 'w_branch_ssm': _jnp.float32, 'w_branch_mla': _jnp.float32, 'w_out': _jnp.float32, 'norm_mlp_w': _jnp.float32, 'w_mlp_up': _jnp.float32, 'w_mlp_down': _jnp.float32, 'final_norm_w': _jnp.float32}
MOMENT_SCALE = {'meta_tokens': 2.728942e-03, 'norm_mix_w': 7.532652e-02, 'w_in': 2.657200e-02, 'conv_w': 2.885035e-02, 'conv_b': 3.986646e-02, 'dt_bias': 1.000839e-01, 'a_log': 1.297269e-01, 'd_skip': 1.628548e-01, 'ssm_norm_w': 3.374594e-02, 'q_norm_w': 1.015297e-02, 'kv_norm_w': 2.416259e-02, 'w_uq': 6.155350e-03, 'w_ukv': 8.685995e-03, 'w_branch_ssm': 4.693612e-02, 'w_branch_mla': 1.066514e-02, 'w_out': 9.485060e-02, 'norm_mlp_w': 1.037298e-01, 'w_mlp_up': 5.098847e-02, 'w_mlp_down': 1.883625e-01, 'final_norm_w': 3.212171e+01}


def _to_microbatches(a, axis):
    t = _jnp.moveaxis(a, axis, 0)
    t = t.reshape((N_MICROBATCH, t.shape[0] // N_MICROBATCH) + t.shape[1:])
    return _jnp.moveaxis(t, 1, axis + 1)


def setup_inputs(seed: int = 0) -> dict:
    inp = _fwd_setup_inputs(seed)
    key = _jax.random.fold_in(_jax.random.key(seed), 7919)
    shape, _ = _output_shape()
    out = dict(inp)
    out["loss_target"] = _jax.random.normal(_jax.random.fold_in(key, 0), shape, _jnp.float32)
    for i, name in enumerate(TWIN_WEIGHTS):
        w = inp[name].astype(_jnp.float32)
        if MOMENT_SCALE is None:
            s = _jnp.sqrt(_jnp.mean(_jnp.square(w)) + 1e-30)
        else:
            s = MOMENT_SCALE[name]
        km, kv = _jax.random.split(_jax.random.fold_in(key, i + 1))
        out[name] = w
        out["m_" + name] = s * _jax.random.normal(km, w.shape, _jnp.float32)
        out["v_" + name] = (s * s) * _jax.random.uniform(kv, w.shape, _jnp.float32, 0.5, 1.5)
    if N_MICROBATCH > 1:
        for name, axis in PER_EXAMPLE_BATCH_AXIS.items():
            out[name] = _to_microbatches(out[name], axis)
    return {'x': out['x'], 'meta_tokens': out['meta_tokens'], 'norm_mix_w': out['norm_mix_w'], 'w_in': out['w_in'], 'conv_w': out['conv_w'], 'conv_b': out['conv_b'], 'dt_bias': out['dt_bias'], 'a_log': out['a_log'], 'd_skip': out['d_skip'], 'ssm_norm_w': out['ssm_norm_w'], 'q_norm_w': out['q_norm_w'], 'kv_norm_w': out['kv_norm_w'], 'w_uq': out['w_uq'], 'w_ukv': out['w_ukv'], 'w_branch_ssm': out['w_branch_ssm'], 'w_branch_mla': out['w_branch_mla'], 'w_out': out['w_out'], 'norm_mlp_w': out['norm_mlp_w'], 'w_mlp_up': out['w_mlp_up'], 'w_mlp_down': out['w_mlp_down'], 'final_norm_w': out['final_norm_w'], 'loss_target': out['loss_target'], 'm_meta_tokens': out['m_meta_tokens'], 'm_norm_mix_w': out['m_norm_mix_w'], 'm_w_in': out['m_w_in'], 'm_conv_w': out['m_conv_w'], 'm_conv_b': out['m_conv_b'], 'm_dt_bias': out['m_dt_bias'], 'm_a_log': out['m_a_log'], 'm_d_skip': out['m_d_skip'], 'm_ssm_norm_w': out['m_ssm_norm_w'], 'm_q_norm_w': out['m_q_norm_w'], 'm_kv_norm_w': out['m_kv_norm_w'], 'm_w_uq': out['m_w_uq'], 'm_w_ukv': out['m_w_ukv'], 'm_w_branch_ssm': out['m_w_branch_ssm'], 'm_w_branch_mla': out['m_w_branch_mla'], 'm_w_out': out['m_w_out'], 'm_norm_mlp_w': out['m_norm_mlp_w'], 'm_w_mlp_up': out['m_w_mlp_up'], 'm_w_mlp_down': out['m_w_mlp_down'], 'm_final_norm_w': out['m_final_norm_w'], 'v_meta_tokens': out['v_meta_tokens'], 'v_norm_mix_w': out['v_norm_mix_w'], 'v_w_in': out['v_w_in'], 'v_conv_w': out['v_conv_w'], 'v_conv_b': out['v_conv_b'], 'v_dt_bias': out['v_dt_bias'], 'v_a_log': out['v_a_log'], 'v_d_skip': out['v_d_skip'], 'v_ssm_norm_w': out['v_ssm_norm_w'], 'v_q_norm_w': out['v_q_norm_w'], 'v_kv_norm_w': out['v_kv_norm_w'], 'v_w_uq': out['v_w_uq'], 'v_w_ukv': out['v_w_ukv'], 'v_w_branch_ssm': out['v_w_branch_ssm'], 'v_w_branch_mla': out['v_w_branch_mla'], 'v_w_out': out['v_w_out'], 'v_norm_mlp_w': out['v_norm_mlp_w'], 'v_w_mlp_up': out['v_w_mlp_up'], 'v_w_mlp_down': out['v_w_mlp_down'], 'v_final_norm_w': out['v_final_norm_w']}


def _loss(weights, diff, rest, loss_target):
    with _jax.named_scope("forward"):
        args = {**rest, TWIN_DIFF_INPUT: diff, **{k: w.astype(_WEIGHT_DTYPES[k]) for k, w in weights.items()}}
        y = _forward(args)
    with _jax.named_scope("loss_head"):
        err = _jnp.square(y.astype(_jnp.float32) - loss_target)
        return 0.5 * _jnp.sum(_jnp.mean(err, axis=-1)) if err.ndim else 0.5 * err


def _adamw(w, g, m, v):
    m = ADAM_B1 * m + (1.0 - ADAM_B1) * g
    v = ADAM_B2 * v + (1.0 - ADAM_B2) * _jnp.square(g)
    m_hat = m / (1.0 - ADAM_B1 ** ADAM_STEP)
    v_hat = v / (1.0 - ADAM_B2 ** ADAM_STEP)
    delta = -ADAM_LR * (m_hat / (_jnp.sqrt(v_hat) + ADAM_EPS) + ADAM_WD * w)
    return delta, m, v


def reference(x, meta_tokens, norm_mix_w, w_in, conv_w, conv_b, dt_bias, a_log, d_skip, ssm_norm_w, q_norm_w, kv_norm_w, w_uq, w_ukv, w_branch_ssm, w_branch_mla, w_out, norm_mlp_w, w_mlp_up, w_mlp_down, final_norm_w, loss_target, m_meta_tokens, m_norm_mix_w, m_w_in, m_conv_w, m_conv_b, m_dt_bias, m_a_log, m_d_skip, m_ssm_norm_w, m_q_norm_w, m_kv_norm_w, m_w_uq, m_w_ukv, m_w_branch_ssm, m_w_branch_mla, m_w_out, m_norm_mlp_w, m_w_mlp_up, m_w_mlp_down, m_final_norm_w, v_meta_tokens, v_norm_mix_w, v_w_in, v_conv_w, v_conv_b, v_dt_bias, v_a_log, v_d_skip, v_ssm_norm_w, v_q_norm_w, v_kv_norm_w, v_w_uq, v_w_ukv, v_w_branch_ssm, v_w_branch_mla, v_w_out, v_norm_mlp_w, v_w_mlp_up, v_w_mlp_down, v_final_norm_w):
    given = dict(x=x, meta_tokens=meta_tokens, norm_mix_w=norm_mix_w, w_in=w_in, conv_w=conv_w, conv_b=conv_b, dt_bias=dt_bias, a_log=a_log, d_skip=d_skip, ssm_norm_w=ssm_norm_w, q_norm_w=q_norm_w, kv_norm_w=kv_norm_w, w_uq=w_uq, w_ukv=w_ukv, w_branch_ssm=w_branch_ssm, w_branch_mla=w_branch_mla, w_out=w_out, norm_mlp_w=norm_mlp_w, w_mlp_up=w_mlp_up, w_mlp_down=w_mlp_down, final_norm_w=final_norm_w, loss_target=loss_target, m_meta_tokens=m_meta_tokens, m_norm_mix_w=m_norm_mix_w, m_w_in=m_w_in, m_conv_w=m_conv_w, m_conv_b=m_conv_b, m_dt_bias=m_dt_bias, m_a_log=m_a_log, m_d_skip=m_d_skip, m_ssm_norm_w=m_ssm_norm_w, m_q_norm_w=m_q_norm_w, m_kv_norm_w=m_kv_norm_w, m_w_uq=m_w_uq, m_w_ukv=m_w_ukv, m_w_branch_ssm=m_w_branch_ssm, m_w_branch_mla=m_w_branch_mla, m_w_out=m_w_out, m_norm_mlp_w=m_norm_mlp_w, m_w_mlp_up=m_w_mlp_up, m_w_mlp_down=m_w_mlp_down, m_final_norm_w=m_final_norm_w, v_meta_tokens=v_meta_tokens, v_norm_mix_w=v_norm_mix_w, v_w_in=v_w_in, v_conv_w=v_conv_w, v_conv_b=v_conv_b, v_dt_bias=v_dt_bias, v_a_log=v_a_log, v_d_skip=v_d_skip, v_ssm_norm_w=v_ssm_norm_w, v_q_norm_w=v_q_norm_w, v_kv_norm_w=v_kv_norm_w, v_w_uq=v_w_uq, v_w_ukv=v_w_ukv, v_w_branch_ssm=v_w_branch_ssm, v_w_branch_mla=v_w_branch_mla, v_w_out=v_w_out, v_norm_mlp_w=v_norm_mlp_w, v_w_mlp_up=v_w_mlp_up, v_w_mlp_down=v_w_mlp_down, v_final_norm_w=v_final_norm_w)
    weights = {n: given[n] for n in TWIN_WEIGHTS}
    shared = {n: given[n] for n in SHARED_INPUTS}
    per_example = {n: given[n] for n in ['x']}
    grad_fn = _jax.value_and_grad(_loss, argnums=(0, 1))

    def one_microbatch(ex, loss_target):
        ex = dict(ex)
        diff = ex.pop(TWIN_DIFF_INPUT)
        return grad_fn(weights, diff, {**shared, **ex}, loss_target)

    if N_MICROBATCH == 1:
        loss, (grad_w, grad_x) = one_microbatch(per_example, given["loss_target"])
    else:
        def body(carry, xs):
            loss_sum, grad_sum = carry
            l_k, (gw_k, gx_k) = one_microbatch(xs[0], xs[1])
            with _jax.named_scope("update"):
                return (loss_sum + l_k, _jax.tree.map(_jnp.add, grad_sum, gw_k)), gx_k

        init = (_jnp.zeros((), _jnp.float32), _jax.tree.map(_jnp.zeros_like, weights))
        (loss, grad_w), grad_x = _jax.lax.scan(body, init, (per_example, given["loss_target"]))
    with _jax.named_scope("update"):
        delta_w, new_m, new_v = {}, {}, {}
        for n in TWIN_WEIGHTS:
            delta_w[n], new_m[n], new_v[n] = _adamw(weights[n], grad_w[n], given["m_" + n], given["v_" + n])
    return (loss, grad_x, *[grad_w[n] for n in TWIN_WEIGHTS], *[delta_w[n] for n in TWIN_WEIGHTS],
            *[new_m[n] for n in TWIN_WEIGHTS], *[new_v[n] for n in TWIN_WEIGHTS])
```

```python
import functools

import numpy as np
import jax
import jax.numpy as jnp
from jax import lax
from jax.experimental import pallas as pl
from jax.experimental.pallas import tpu as pltpu

F32 = jnp.float32
MXU_DTYPE = jnp.bfloat16

D_MODEL = 1024
N_META = 16
EPS = 1e-6
SSM_D_INNER = 2048
SSM_HEAD_DIM = 64
SSM_GROUPS = 4
SSM_STATE = 128
SSM_CONV = 4
SSM_CHUNK = 128
MLA_HEADS = 8
MLA_Q_LORA = 512
MLA_KV_LORA = 256
MLA_NOPE = 128
MLA_ROPE = 64
MLA_V = 128
ROPE_THETA = 10000.0
D_FF = 4096
ADAM_LR = 0.001
ADAM_B1 = 0.9
ADAM_B2 = 0.999
ADAM_EPS = 1e-08
ADAM_WD = 0.01
ADAM_STEP = 10

N_DEV = 8
LANE = 128
PACK_W = 1024
VMEM_LIMIT = 56 * 1024 * 1024
MESH_ID = pl.DeviceIdType.MESH

BIG = (("w_in", "col"), ("w_uq", "col"), ("w_ukv", "col"), ("w_branch_ssm", "row"), ("w_branch_mla", "row"),
       ("w_out", "row"), ("w_mlp_up", "col"), ("w_mlp_down", "row"))
SHARDED_F32 = (("conv_w", "col"), ("meta_tokens", "col"))
SMALL = ("norm_mix_w", "conv_b", "dt_bias", "a_log", "d_skip", "ssm_norm_w", "q_norm_w", "kv_norm_w",
         "norm_mlp_w", "final_norm_w")
WEIGHTS = ("meta_tokens", "norm_mix_w", "w_in", "conv_w", "conv_b", "dt_bias", "a_log", "d_skip", "ssm_norm_w",
           "q_norm_w", "kv_norm_w", "w_uq", "w_ukv", "w_branch_ssm", "w_branch_mla", "w_out", "norm_mlp_w",
           "w_mlp_up", "w_mlp_down", "final_norm_w")


def _cparams(sem=None):
    return pltpu.CompilerParams(dimension_semantics=sem, vmem_limit_bytes=VMEM_LIMIT)


def _pick(n, cands):
    for c in cands:
        if n % c == 0:
            return c
    return n


def _sigmoid(x):
    return 1.0 / (1.0 + jnp.exp(-x))


def _silu(x):
    return x * _sigmoid(x)


def _softplus(x):
    return jnp.maximum(x, 0.0) + jnp.log1p(jnp.exp(-jnp.abs(x)))


def _rms(x, w):
    return x * lax.rsqrt(jnp.mean(x * x, axis=-1, keepdims=True) + EPS) * w


def _dot(a, b, ca, cb, precision=None):
    return lax.dot_general(a, b, (((ca,), (cb,)), ((), ())), preferred_element_type=F32, precision=precision)


def _mxdot(a, b, ca, cb):
    return _dot(a.astype(MXU_DTYPE), b.astype(MXU_DTYPE), ca, cb)


def _mm(name, a, b, *, ta=False, tb=False, add=None, out_dtype=F32):
    (kdim, m) = a.shape if ta else a.shape[::-1]
    (n, k2) = b.shape if tb else b.shape[::-1]
    assert kdim == k2, (name, a.shape, b.shape)
    tm = _pick(m, (1088, 544, 512, 384, 256, 128))
    tn = _pick(n, (1024, 512, 384, 256, 128))
    tk = _pick(kdim, (1024, 544, 512, 384, 256, 128))
    nk = kdim // tk
    a_spec = pl.BlockSpec((tk, tm), lambda i, j, k: (k, i)) if ta else pl.BlockSpec((tm, tk), lambda i, j, k: (i, k))
    b_spec = pl.BlockSpec((tn, tk), lambda i, j, k: (j, k)) if tb else pl.BlockSpec((tk, tn), lambda i, j, k: (k, j))
    o_spec = pl.BlockSpec((tm, tn), lambda i, j, k: (i, j))
    ca, cb = (0 if ta else 1), (1 if tb else 0)

    def body(*refs):
        a_ref, b_ref = refs[:2]
        o_ref, acc = refs[-2:]
        k = pl.program_id(2)

        @pl.when(k == 0)
        def _():
            acc[...] = jnp.zeros_like(acc)

        acc[...] += _mxdot(a_ref[...], b_ref[...], ca, cb)

        @pl.when(k == nk - 1)
        def _():
            r = acc[...]
            if add is not None:
                r = r + refs[2][...].astype(F32)
            o_ref[...] = r.astype(out_dtype)

    in_specs, args = [a_spec, b_spec], [a, b]
    if add is not None:
        in_specs.append(o_spec)
        args.append(add)
    return pl.pallas_call(
        body, name=name, grid=(m // tm, n // tn, nk), in_specs=in_specs, out_specs=o_spec,
        out_shape=jax.ShapeDtypeStruct((m, n), out_dtype), scratch_shapes=[pltpu.VMEM((tm, tn), F32)],
        compiler_params=_cparams(("parallel", "parallel", "arbitrary")))(*args)


def _rowwise(name, fn, *, nrows, tr, ncb=1, rows=(), vecs=(), tabs=(), outs=(), reds=(), tab_blocks=1):
    in_specs, args = [], []
    for arr, w, c0 in rows:
        in_specs.append(pl.BlockSpec((tr, w), lambda g, i, c0=c0: (i, c0 + g)))
        args.append(arr)
    for arr, w, c0 in vecs:
        in_specs.append(pl.BlockSpec((1, w), lambda g, i, c0=c0: (0, c0 + g)))
        args.append(arr)
    for arr, w, c0 in tabs:
        in_specs.append(pl.BlockSpec((tr, w), lambda g, i, c0=c0: (i % tab_blocks, c0)))
        args.append(arr)
    out_shape = [jax.ShapeDtypeStruct((nrows, wt), dt) for wt, w, dt in outs]
    out_shape += [jax.ShapeDtypeStruct((1, wt), F32) for wt, w in reds]
    out_specs = [pl.BlockSpec((tr, w), lambda g, i: (i, g)) for wt, w, dt in outs]
    out_specs += [pl.BlockSpec((1, w), lambda g, i: (0, g)) for wt, w in reds]
    n_in, n_out = len(args), len(outs)

    def body(*refs):
        res = fn(*[r[...] for r in refs[:n_in]])
        for o_ref, val in zip(refs[n_in:n_in + n_out], res[:n_out]):
            o_ref[...] = val.astype(o_ref.dtype)
        i = pl.program_id(1)
        for d_ref, val in zip(refs[n_in + n_out:], res[n_out:]):
            @pl.when(i == 0)
            def _(d_ref=d_ref, val=val):
                d_ref[...] = val

            @pl.when(i > 0)
            def _(d_ref=d_ref, val=val):
                d_ref[...] += val

    res = pl.pallas_call(
        body, name=name, grid=(ncb, nrows // tr), in_specs=in_specs, out_specs=out_specs, out_shape=out_shape,
        compiler_params=_cparams(("parallel", "arbitrary")))(*args)
    return res


def _peer(k):
    x, y, c = lax.axis_index("x"), lax.axis_index("y"), lax.axis_index("c")
    px = jnp.where((k >> 2) & 1, 1 - x, x)
    py = jnp.where((k >> 1) & 1, 1 - y, y)
    pc = jnp.where(k & 1, 1 - c, c)
    return (px, py, pc), 4 * px + 2 * py + pc


def _my_index():
    return 4 * lax.axis_index("x") + 2 * lax.axis_index("y") + lax.axis_index("c")


def _exchange(name, x, scatter):
    blk = x.shape[1:] if scatter else x.shape

    def body(x_ref, o_ref, send_sems, recv_sems, local_sem):
        me = _my_index()
        src_own = x_ref.at[me] if scatter else x_ref
        own = pltpu.make_async_copy(src_own, o_ref.at[me], local_sem)
        own.start()
        copies = []
        for k in range(1, N_DEV):
            dev, idx = _peer(k)
            cp = pltpu.make_async_remote_copy(
                src_ref=x_ref.at[idx] if scatter else x_ref, dst_ref=o_ref.at[me],
                send_sem=send_sems.at[k - 1], recv_sem=recv_sems.at[k - 1], device_id=dev, device_id_type=MESH_ID)
            cp.start()
            copies.append(cp)
        for k in range(1, N_DEV):
            dev, idx = _peer(k)
            pltpu.make_async_remote_copy(
                src_ref=x_ref.at[idx] if scatter else x_ref, dst_ref=o_ref.at[idx],
                send_sem=send_sems.at[k - 1], recv_sem=recv_sems.at[k - 1], device_id=dev,
                device_id_type=MESH_ID).wait_recv()
        for cp in copies:
            cp.wait_send()
        own.wait()

    return pl.pallas_call(
        body, name=name, out_shape=jax.ShapeDtypeStruct((N_DEV,) + tuple(blk), x.dtype),
        in_specs=[pl.BlockSpec(memory_space=pl.ANY)], out_specs=pl.BlockSpec(memory_space=pl.ANY),
        scratch_shapes=[pltpu.SemaphoreType.DMA((N_DEV - 1,)), pltpu.SemaphoreType.DMA((N_DEV - 1,)),
                        pltpu.SemaphoreType.DMA(())],
        compiler_params=pltpu.CompilerParams(has_side_effects=True))(x)


def _adamw(name, parts, w, m, v):
    rows = w.shape[0]
    tr = _pick(rows, (256, 128, 64, 32, 16, 8))
    spec = pl.BlockSpec((tr, PACK_W), lambda i: (i, 0))

    def body(p_ref, w_ref, m_ref, v_ref, g_ref, d_ref, nm_ref, nv_ref):
        g = p_ref[0]
        for j in range(1, N_DEV):
            g = g + p_ref[j]
        nm = ADAM_B1 * m_ref[...] + (1.0 - ADAM_B1) * g
        nv = ADAM_B2 * v_ref[...] + (1.0 - ADAM_B2) * jnp.square(g)
        m_hat = nm / (1.0 - ADAM_B1 ** ADAM_STEP)
        v_hat = nv / (1.0 - ADAM_B2 ** ADAM_STEP)
        g_ref[...] = g
        d_ref[...] = -ADAM_LR * (m_hat / (jnp.sqrt(v_hat) + ADAM_EPS) + ADAM_WD * w_ref[...])
        nm_ref[...] = nm
        nv_ref[...] = nv

    sds = jax.ShapeDtypeStruct((rows, PACK_W), F32)
    return pl.pallas_call(
        body, name=name, grid=(rows // tr,),
        in_specs=[pl.BlockSpec((N_DEV, tr, PACK_W), lambda i: (0, i, 0)), spec, spec, spec],
        out_specs=[spec] * 4, out_shape=[sds] * 4, compiler_params=_cparams(("parallel",)))(parts, w, m, v)


def _pack(arrs, dtype, row_mult=16):
    flat = jnp.concatenate([a.reshape(-1).astype(dtype) for a in arrs])
    unit = row_mult * PACK_W
    total = -(-flat.shape[0] // unit) * unit
    flat = jnp.pad(flat, (0, total - flat.shape[0]))
    return flat.reshape(-1, PACK_W)


def _pack_lead(arrs, dtype, row_mult):
    flat = jnp.concatenate([a.reshape(N_DEV, -1).astype(dtype) for a in arrs], axis=1)
    unit = row_mult * PACK_W
    total = -(-flat.shape[1] // unit) * unit
    flat = jnp.pad(flat, ((0, 0), (0, total - flat.shape[1])))
    return flat.reshape(N_DEV, -1, PACK_W)


def _unpack(buf, shapes, lead=()):
    flat = buf.reshape(lead + (-1,))
    out, off = [], 0
    for s in shapes:
        n = int(np.prod(s))
        out.append(flat[..., off:off + n].reshape(lead + tuple(s)))
        off += n
    return out


def _unshard(g, kind):
    if kind == "col":
        g = jnp.moveaxis(g, 0, -2)
        return g.reshape(g.shape[:-2] + (g.shape[-2] * g.shape[-1],))
    g = jnp.moveaxis(g, 0, 1)
    return g.reshape((g.shape[0], g.shape[1] * g.shape[2]) + g.shape[3:])


def _shard(full, kind):
    if kind == "col":
        s = full.reshape(full.shape[:-1] + (N_DEV, full.shape[-1] // N_DEV))
        return jnp.moveaxis(s, -2, 0)
    s = full.reshape((full.shape[0], N_DEV, full.shape[1] // N_DEV) + full.shape[2:])
    return jnp.moveaxis(s, 1, 0)


class _Geo:
    def __init__(self, bsz, seq):
        self.bsz, self.seq = bsz, seq
        self.pad = SSM_CHUNK - N_META
        self.lp = self.pad + N_META + seq
        assert self.lp % SSM_CHUNK == 0
        self.nrows = bsz * self.lp
        self.nc = self.lp // SSM_CHUNK
        self.nh = SSM_D_INNER // SSM_HEAD_DIM
        self.gn = SSM_GROUPS * SSM_STATE
        self.cd = SSM_D_INNER + 2 * self.gn
        self.hq = MLA_HEADS * LANE
        order = (("z", SSM_D_INNER), ("xs", SSM_D_INNER), ("g_ssm", D_MODEL), ("g_mla", D_MODEL), ("bm", self.gn),
                 ("cm", self.gn), ("c_q", MLA_Q_LORA), ("c_kv", MLA_KV_LORA), ("dt", LANE), ("k_rope", LANE))
        self.col, off = {}, 0
        for nm, w in order:
            assert off % w == 0, (nm, off, w)
            self.col[nm] = (off, w)
            off += w
        self.pw = off
        assert self.nh <= LANE and MLA_ROPE == 64 and MLA_NOPE == LANE and MLA_V == LANE
        self.tr = _pick(self.lp, (544, 384, 272, 256, 128))
        self.tr_wide = _pick(self.lp, (272, 256, 192, 128))

    def cb(self, nm):
        off, w = self.col[nm]
        return off // w


def _slot(a):
    h = MLA_ROPE // 2
    z = jnp.zeros(a.shape[:-1] + (h,), a.dtype)
    return jnp.concatenate([a[..., :h], z, a[..., h:], z], axis=-1)


def _unslot(a):
    h = MLA_ROPE // 2
    return jnp.concatenate([a[..., :h], a[..., 2 * h:3 * h]], axis=-1)


def _prep_layer(geo, wl):
    nh = geo.nh
    splits = np.cumsum([SSM_D_INNER, geo.cd, nh, MLA_Q_LORA, MLA_KV_LORA, MLA_ROPE, D_MODEL])
    z, xbc, dt, c_q, c_kv, k_rope, g_ssm, g_mla = jnp.split(wl["w_in"], splits, axis=-1)
    xs, bm, cm = jnp.split(xbc, [SSM_D_INNER, SSM_D_INNER + geo.gn], axis=-1)
    dt = jnp.pad(dt, ((0, 0), (0, LANE - nh)))
    p = {"w_in_p": jnp.concatenate([z, xs, g_ssm, g_mla, bm, cm, c_q, c_kv, dt, _slot(k_rope)], axis=-1)}
    uq = wl["w_uq"].reshape(MLA_Q_LORA, MLA_HEADS, MLA_NOPE + MLA_ROPE)
    p["w_qn"] = uq[..., :MLA_NOPE].reshape(MLA_Q_LORA, geo.hq)
    p["w_qp"] = _slot(uq[..., MLA_NOPE:]).reshape(MLA_Q_LORA, geo.hq)
    ukv = wl["w_ukv"].reshape(MLA_KV_LORA, MLA_HEADS, MLA_NOPE + MLA_V)
    p["w_k"] = ukv[..., :MLA_NOPE].reshape(MLA_KV_LORA, geo.hq)
    p["w_v"] = ukv[..., MLA_NOPE:].reshape(MLA_KV_LORA, geo.hq)
    for nm in ("w_branch_ssm", "w_branch_mla", "w_out", "w_mlp_up", "w_mlp_down"):
        p[nm] = wl[nm]
    p["conv_w"] = wl["conv_w"]
    for nm in ("norm_mix_w", "conv_b", "ssm_norm_w", "q_norm_w", "kv_norm_w", "norm_mlp_w"):
        p[nm] = wl[nm].reshape(1, -1)
    p["dt_bias"] = jnp.pad(wl["dt_bias"], (0, LANE - nh)).reshape(1, LANE)
    p["a_log"] = jnp.pad(wl["a_log"], (0, LANE - nh)).reshape(1, LANE)
    p["d_skip_full"] = jnp.repeat(wl["d_skip"], SSM_HEAD_DIM).reshape(1, SSM_D_INNER)
    return p


def _unprep_grads(geo, g):
    nh = geo.nh
    gp = g["w_in_p"]
    c = lambda nm: gp[:, geo.col[nm][0]:geo.col[nm][0] + geo.col[nm][1]]
    w_in = jnp.concatenate([c("z"), c("xs"), c("bm"), c("cm"), c("dt")[:, :nh], c("c_q"), c("c_kv"),
                            _unslot(c("k_rope")), c("g_ssm"), c("g_mla")], axis=-1)
    qn = g["w_qn"].reshape(MLA_Q_LORA, MLA_HEADS, MLA_NOPE)
    qp = _unslot(g["w_qp"].reshape(MLA_Q_LORA, MLA_HEADS, LANE))
    w_uq = jnp.concatenate([qn, qp], axis=-1).reshape(MLA_Q_LORA, -1)
    wk = g["w_k"].reshape(MLA_KV_LORA, MLA_HEADS, MLA_NOPE)
    wv = g["w_v"].reshape(MLA_KV_LORA, MLA_HEADS, MLA_V)
    w_ukv = jnp.concatenate([wk, wv], axis=-1).reshape(MLA_KV_LORA, -1)
    out = {"w_in": w_in, "w_uq": w_uq, "w_ukv": w_ukv}
    for nm in ("w_branch_ssm", "w_branch_mla", "w_out", "w_mlp_up", "w_mlp_down", "conv_w"):
        out[nm] = g[nm]
    for nm in ("norm_mix_w", "conv_b", "ssm_norm_w", "q_norm_w", "kv_norm_w", "norm_mlp_w"):
        out[nm] = g[nm].reshape(-1)
    out["dt_bias"] = g["dt_bias"].reshape(-1)[:nh]
    out["a_log"] = g["a_log"].reshape(-1)[:nh]
    out["d_skip"] = g["d_skip_full"].reshape(nh, SSM_HEAD_DIM).sum(-1)
    return out


def _tables(geo):
    pos = jnp.arange(geo.lp, dtype=F32) - geo.pad
    inv = ROPE_THETA ** (-jnp.arange(0, MLA_ROPE, 2, dtype=F32) / MLA_ROPE)
    ang = pos[:, None] * inv[None, :]
    cos, sin = jnp.cos(ang), jnp.sin(ang)
    z = jnp.zeros_like(cos)
    rows = jnp.arange(geo.lp)[:, None]
    return {"cos": jnp.concatenate([cos, z, cos, z], axis=-1), "sin": jnp.concatenate([-sin, z, sin, z], axis=-1),
            "valid": (rows >= geo.pad).astype(F32), "token": (rows >= geo.pad + N_META).astype(F32)}


def _conv_cols(geo, cbw):
    nx = SSM_D_INNER // cbw
    x0, b0 = geo.col["xs"][0] // cbw, geo.col["bm"][0] // cbw
    assert geo.col["cm"][0] == geo.col["bm"][0] + geo.gn
    return lambda j: jnp.where(j < nx, x0 + j, b0 + j - nx)


def _conv_pre(x, w_ref, b_ref):
    acc = b_ref[...] + x * w_ref[SSM_CONV - 1:SSM_CONV, :]
    for k in range(SSM_CONV - 1):
        acc = acc + pltpu.roll(x, SSM_CONV - 1 - k, axis=0) * w_ref[k:k + 1, :]
    return acc


def _conv_fwd(geo, proj, conv_w, conv_b):
    cbw = 256
    colmap = _conv_cols(geo, cbw)
    lp, pad = geo.lp, geo.pad

    def body(x_ref, w_ref, b_ref, o_ref):
        valid = (lax.broadcasted_iota(jnp.int32, (lp, 1), 0) >= pad).astype(F32)
        o_ref[...] = _silu(_conv_pre(x_ref[...], w_ref, b_ref)) * valid

    return pl.pallas_call(
        body, name="conv_fwd", grid=(geo.bsz, geo.cd // cbw),
        in_specs=[pl.BlockSpec((lp, cbw), lambda b, j: (b, colmap(j))),
                  pl.BlockSpec((SSM_CONV, cbw), lambda b, j: (0, j)), pl.BlockSpec((1, cbw), lambda b, j: (0, j))],
        out_specs=pl.BlockSpec((lp, cbw), lambda b, j: (b, j)),
        out_shape=jax.ShapeDtypeStruct((geo.nrows, geo.cd), F32),
        compiler_params=_cparams(("parallel", "parallel")))(proj, conv_w, conv_b)


def _conv_bwd(geo, proj, conv_w, conv_b, dxc):
    cbw = 256
    colmap = _conv_cols(geo, cbw)
    lp, pad = geo.lp, geo.pad

    def body(x_ref, w_ref, b_ref, dy_ref, dx_ref, gw_ref, gb_ref):
        b = pl.program_id(1)
        valid = (lax.broadcasted_iota(jnp.int32, (lp, 1), 0) >= pad).astype(F32)
        x = x_ref[...]
        pre = _conv_pre(x, w_ref, b_ref)
        sig = _sigmoid(pre)
        dpre = dy_ref[...] * (sig * (1.0 + pre * (1.0 - sig))) * valid
        dx = dpre * w_ref[SSM_CONV - 1:SSM_CONV, :]
        gws = [jnp.sum(dpre * x, axis=0, keepdims=True)]
        for k in range(SSM_CONV - 2, -1, -1):
            s = SSM_CONV - 1 - k
            dx = dx + pltpu.roll(dpre, lp - s, axis=0) * w_ref[k:k + 1, :]
            gws.insert(0, jnp.sum(dpre * pltpu.roll(x, s, axis=0), axis=0, keepdims=True))
        dx_ref[...] = (dx * valid).astype(dx_ref.dtype)

        @pl.when(b == 0)
        def _():
            gw_ref[...] = jnp.zeros_like(gw_ref)
            gb_ref[...] = jnp.zeros_like(gb_ref)

        for k in range(SSM_CONV):
            gw_ref[k:k + 1, :] += gws[k]
        gb_ref[...] += jnp.sum(dpre, axis=0, keepdims=True)

    return pl.pallas_call(
        body, name="conv_bwd", grid=(geo.cd // cbw, geo.bsz),
        in_specs=[pl.BlockSpec((lp, cbw), lambda j, b: (b, colmap(j))),
                  pl.BlockSpec((SSM_CONV, cbw), lambda j, b: (0, j)), pl.BlockSpec((1, cbw), lambda j, b: (0, j)),
                  pl.BlockSpec((lp, cbw), lambda j, b: (b, j))],
        out_specs=[pl.BlockSpec((lp, cbw), lambda j, b: (b, j)), pl.BlockSpec((SSM_CONV, cbw), lambda j, b: (0, j)),
                   pl.BlockSpec((1, cbw), lambda j, b: (0, j))],
        out_shape=[jax.ShapeDtypeStruct((geo.nrows, geo.cd), MXU_DTYPE),
                   jax.ShapeDtypeStruct((SSM_CONV, geo.cd), F32), jax.ShapeDtypeStruct((1, geo.cd), F32)],
        compiler_params=_cparams(("parallel", "arbitrary")))(proj, conv_w, conv_b, dxc)


def _tri(q):
    r = lax.broadcasted_iota(jnp.int32, (q, q), 0)
    c = lax.broadcasted_iota(jnp.int32, (q, q), 1)
    return r >= c


def _ssd_pre(dtr, dtb, alog, valid):
    dt = _softplus(dtr + dtb) * valid
    adt = dt * (-jnp.exp(alog))
    a_cs = _dot(_tri(SSM_CHUNK).astype(F32), adt, 1, 0, precision=lax.Precision.HIGHEST)
    return dt, a_cs


def _ssd_specs(geo, rev):
    nc, q = geo.nc, SSM_CHUNK
    ci = (lambda c: nc - 1 - c) if rev else (lambda c: c)
    nxb = SSM_D_INNER // geo.gn
    return [pl.BlockSpec((q, SSM_D_INNER), lambda b, c: (b * nc + ci(c), 0)),
            pl.BlockSpec((q, geo.gn), lambda b, c: (b * nc + ci(c), nxb)),
            pl.BlockSpec((q, geo.gn), lambda b, c: (b * nc + ci(c), nxb + 1)),
            pl.BlockSpec((q, LANE), lambda b, c: (b * nc + ci(c), geo.cb("dt"))),
            pl.BlockSpec((1, LANE), lambda b, c: (0, 0)), pl.BlockSpec((1, LANE), lambda b, c: (0, 0))], ci


def _ssd_fwd(geo, xc, proj, dt_bias, a_log):
    q, p, n, e = SSM_CHUNK, SSM_HEAD_DIM, SSM_STATE, geo.nh // SSM_GROUPS
    nc, pad = geo.nc, geo.pad
    in_specs, _ = _ssd_specs(geo, False)

    def body(xs_ref, b_ref, c_ref, dtr_ref, dtb_ref, alog_ref, y_ref, sp_ref, state):
        c = pl.program_id(1)

        @pl.when(c == 0)
        def _():
            state[...] = jnp.zeros_like(state)

        sp_ref[...] = state[...]
        valid = (c * q + lax.broadcasted_iota(jnp.int32, (q, 1), 0) >= pad).astype(F32)
        dt, a_cs = _ssd_pre(dtr_ref[...], dtb_ref[...], alog_ref[...], valid)
        a_cst = a_cs.T
        tri = _tri(q)
        for g in range(SSM_GROUPS):
            bg, cg = b_ref[:, g * n:(g + 1) * n], c_ref[:, g * n:(g + 1) * n]
            cb = _mxdot(cg, bg, 1, 1)
            for hh in range(e):
                h = g * e + hh
                a_col, a_row, a_last = a_cs[:, h:h + 1], a_cst[h:h + 1, :], a_cs[q - 1:q, h:h + 1]
                xdt = xs_ref[:, h * p:(h + 1) * p] * dt[:, h:h + 1]
                ldec = jnp.exp(jnp.where(tri, a_col - a_row, -jnp.inf))
                s_prev = state[h * p:(h + 1) * p, :]
                y = _mxdot(cb * ldec, xdt, 1, 0) + _mxdot(cg, s_prev, 1, 1) * jnp.exp(a_col)
                y_ref[:, h * p:(h + 1) * p] = y
                st = _mxdot(xdt, bg * jnp.exp(a_last - a_col), 0, 0)
                state[h * p:(h + 1) * p, :] = s_prev * jnp.exp(a_last) + st

    return pl.pallas_call(
        body, name="ssd_fwd", grid=(geo.bsz, nc), in_specs=in_specs,
        out_specs=[pl.BlockSpec((q, SSM_D_INNER), lambda b, c: (b * nc + c, 0)),
                   pl.BlockSpec((SSM_D_INNER, n), lambda b, c: (b * nc + c, 0))],
        out_shape=[jax.ShapeDtypeStruct((geo.nrows, SSM_D_INNER), F32),
                   jax.ShapeDtypeStruct((geo.bsz * nc * SSM_D_INNER, n), F32)],
        scratch_shapes=[pltpu.VMEM((SSM_D_INNER, n), F32)],
        compiler_params=_cparams(("parallel", "arbitrary")))(xc, xc, xc, proj, dt_bias, a_log)


def _ssd_bwd(geo, xc, proj, dt_bias, a_log, s_prev_all, dy, dxs_skip):
    q, p, n, e = SSM_CHUNK, SSM_HEAD_DIM, SSM_STATE, geo.nh // SSM_GROUPS
    nc, pad, di, gn = geo.nc, geo.pad, SSM_D_INNER, geo.gn
    in_specs, ci = _ssd_specs(geo, True)
    row_spec = pl.BlockSpec((q, di), lambda b, c: (b * nc + ci(c), 0))
    in_specs += [pl.BlockSpec((di, n), lambda b, c: (b * nc + ci(c), 0)), row_spec, row_spec]

    def body(xs_ref, b_ref, c_ref, dtr_ref, dtb_ref, alog_ref, sp_ref, dy_ref, dsk_ref,
             dxc_ref, ddt_ref, gdtb_ref, galog_ref, dstate):
        step = pl.program_id(1)
        first = jnp.logical_and(pl.program_id(0) == 0, step == 0)
        c = nc - 1 - step

        @pl.when(step == 0)
        def _():
            dstate[...] = jnp.zeros_like(dstate)

        valid = (c * q + lax.broadcasted_iota(jnp.int32, (q, 1), 0) >= pad).astype(F32)
        dtr, dtb, alog = dtr_ref[...], dtb_ref[...], alog_ref[...]
        dt, a_cs = _ssd_pre(dtr, dtb, alog, valid)
        a_cst = a_cs.T
        tri = _tri(q)
        lane = lax.broadcasted_iota(jnp.int32, (1, LANE), 1)
        sub = lax.broadcasted_iota(jnp.int32, (LANE, 1), 0)
        d_dt = jnp.zeros((q, LANE), F32)
        d_acs = jnp.zeros((q, LANE), F32)
        d_acst = jnp.zeros((LANE, q), F32)
        d_last = jnp.zeros((1, LANE), F32)
        for g in range(SSM_GROUPS):
            bg, cg = b_ref[:, g * n:(g + 1) * n], c_ref[:, g * n:(g + 1) * n]
            cb = _mxdot(cg, bg, 1, 1)
            d_cb = jnp.zeros((q, q), F32)
            d_bg = jnp.zeros((q, n), F32)
            d_cg = jnp.zeros((q, n), F32)
            for hh in range(e):
                h = g * e + hh
                hs = slice(h * p, (h + 1) * p)
                a_col, a_row, a_last = a_cs[:, h:h + 1], a_cst[h:h + 1, :], a_cs[q - 1:q, h:h + 1]
                x = xs_ref[:, hs]
                dt_col = dt[:, h:h + 1]
                xdt = x * dt_col
                ldec = jnp.exp(jnp.where(tri, a_col - a_row, -jnp.inf))
                s_prev = sp_ref[hs, :]
                d_snew = dstate[hs, :]
                dyh = dy_ref[:, hs]
                e_col, e_last = jnp.exp(a_col), jnp.exp(a_last)
                dec = jnp.exp(a_last - a_col)
                d_m = _mxdot(dyh, xdt, 1, 1)
                d_xdt = _mxdot(cb * ldec, dyh, 0, 0)
                d_cb = d_cb + d_m * ldec
                d_diff = d_m * cb * ldec
                da_col = jnp.sum(d_diff, axis=1, keepdims=True)
                da_row = -jnp.sum(d_diff, axis=0, keepdims=True)
                cs = _mxdot(cg, s_prev, 1, 1)
                d_cs = dyh * e_col
                da_col = da_col + jnp.sum(dyh * cs, axis=1, keepdims=True) * e_col
                d_cg = d_cg + _mxdot(d_cs, s_prev, 1, 0)
                d_sprev = _mxdot(d_cs, cg, 0, 0) + d_snew * e_last
                dl = jnp.sum(jnp.sum(d_snew * s_prev, axis=1, keepdims=True), axis=0, keepdims=True) * e_last
                d_xdt = d_xdt + _mxdot(bg * dec, d_snew, 1, 1)
                d_bd = _mxdot(xdt, d_snew, 1, 0)
                d_bg = d_bg + d_bd * dec
                d_dec = jnp.sum(d_bd * bg, axis=1, keepdims=True) * dec
                dl = dl + jnp.sum(d_dec, axis=0, keepdims=True)
                da_col = da_col - d_dec
                dstate[hs, :] = d_sprev
                dxc_ref[:, hs] = d_xdt * dt_col + dsk_ref[:, hs]
                onehot = (lane == h).astype(F32)
                d_dt = d_dt + jnp.sum(d_xdt * x, axis=1, keepdims=True) * onehot
                d_acs = d_acs + da_col * onehot
                d_acst = d_acst + (sub == h).astype(F32) * da_row
                d_last = d_last + dl * onehot
            dxc_ref[:, di + g * n:di + (g + 1) * n] = d_bg + _mxdot(d_cb, cg, 0, 0)
            dxc_ref[:, di + gn + g * n:di + gn + (g + 1) * n] = d_cg + _mxdot(d_cb, bg, 1, 0)
        is_last = (lax.broadcasted_iota(jnp.int32, (q, 1), 0) == q - 1).astype(F32)
        d_acs = d_acs + d_acst.T + is_last * d_last
        d_adt = _dot(_tri(q).astype(F32), d_acs, 0, 0, precision=lax.Precision.HIGHEST)
        a = -jnp.exp(alog)
        d_dt = d_dt + d_adt * a
        g_alog = jnp.sum(d_adt * dt, axis=0, keepdims=True) * a
        d_dtr = d_dt * valid * _sigmoid(dtr + dtb)
        ddt_ref[...] = d_dtr.astype(ddt_ref.dtype)
        g_dtb = jnp.sum(d_dtr, axis=0, keepdims=True)

        @pl.when(first)
        def _():
            gdtb_ref[...] = g_dtb
            galog_ref[...] = g_alog

        @pl.when(jnp.logical_not(first))
        def _():
            gdtb_ref[...] += g_dtb
            galog_ref[...] += g_alog

    vec = pl.BlockSpec((1, LANE), lambda b, c: (0, 0))
    return pl.pallas_call(
        body, name="ssd_bwd", grid=(geo.bsz, nc), in_specs=in_specs,
        out_specs=[pl.BlockSpec((q, geo.cd), lambda b, c: (b * nc + ci(c), 0)),
                   pl.BlockSpec((q, LANE), lambda b, c: (b * nc + ci(c), 0)), vec, vec],
        out_shape=[jax.ShapeDtypeStruct((geo.nrows, geo.cd), F32), jax.ShapeDtypeStruct((geo.nrows, LANE), MXU_DTYPE),
                   jax.ShapeDtypeStruct((1, LANE), F32), jax.ShapeDtypeStruct((1, LANE), F32)],
        scratch_shapes=[pltpu.VMEM((di, n), F32)],
        compiler_params=_cparams(("arbitrary", "arbitrary")))(xc, xc, xc, proj, dt_bias, a_log, s_prev_all, dy, dxs_skip)


ATT_BLK = 128


def _att_mask(qi, kj, pad):
    qidx = qi * ATT_BLK + lax.broadcasted_iota(jnp.int32, (ATT_BLK, ATT_BLK), 0)
    kidx = kj * ATT_BLK + lax.broadcasted_iota(jnp.int32, (ATT_BLK, ATT_BLK), 1)
    return jnp.logical_and(kidx <= qidx, jnp.logical_or(kidx >= pad, qidx < pad))


def _att_scores(qn, qp, kn_ref, kp_ref, qi, kj, pad):
    ks = pl.ds(pl.multiple_of(kj * ATT_BLK, ATT_BLK), ATT_BLK)
    s = (_mxdot(qn, kn_ref[ks, :], 1, 1) + _mxdot(qp, kp_ref[ks, :], 1, 1)) * ((MLA_NOPE + MLA_ROPE) ** -0.5)
    return jnp.where(_att_mask(qi, kj, pad), s, -jnp.inf), ks


def _attn_fwd(geo, qn, qp, kn, v, kp):
    nb, lp, pad, t = geo.lp // ATT_BLK, geo.lp, geo.pad, ATT_BLK

    def body(qn_ref, qp_ref, kn_ref, v_ref, kp_ref, o_ref, lse_ref):
        qi = pl.program_id(2)
        q_n, q_p = qn_ref[...], qp_ref[...]

        def step(kj, carry):
            m, l, acc = carry
            s, ks = _att_scores(q_n, q_p, kn_ref, kp_ref, qi, kj, pad)
            m_new = jnp.maximum(m, jnp.max(s, axis=1, keepdims=True))
            pr = jnp.exp(s - m_new)
            alpha = jnp.exp(m - m_new)
            return m_new, alpha * l + jnp.sum(pr, axis=1, keepdims=True), alpha * acc + _mxdot(pr, v_ref[ks, :], 1, 0)

        init = (jnp.full((t, 1), -1e30, F32), jnp.zeros((t, 1), F32), jnp.zeros((t, LANE), F32))
        m, l, acc = lax.fori_loop(0, qi + 1, step, init)
        o_ref[...] = acc / l
        lse_ref[...] = jnp.broadcast_to(m + jnp.log(l), (t, LANE))

    qspec = pl.BlockSpec((t, LANE), lambda b, h, i: (b * nb + i, h))
    kspec = pl.BlockSpec((lp, LANE), lambda b, h, i: (b, h))
    return pl.pallas_call(
        body, name="attn_fwd", grid=(geo.bsz, MLA_HEADS, nb),
        in_specs=[qspec, qspec, kspec, kspec, pl.BlockSpec((lp, LANE), lambda b, h, i: (b, 0))],
        out_specs=[qspec, qspec], out_shape=[jax.ShapeDtypeStruct((geo.nrows, geo.hq), F32)] * 2,
        compiler_params=_cparams(("parallel", "parallel", "arbitrary")))(qn, qp, kn, v, kp)


def _attn_dq(geo, qn, qp, kn, v, kp, do, lse, delta):
    nb, lp, pad, t = geo.lp // ATT_BLK, geo.lp, geo.pad, ATT_BLK
    scale = (MLA_NOPE + MLA_ROPE) ** -0.5

    def body(qn_ref, qp_ref, kn_ref, v_ref, kp_ref, do_ref, lse_ref, dl_ref, dqn_ref, dqp_ref):
        qi = pl.program_id(2)
        q_n, q_p, d_o = qn_ref[...], qp_ref[...], do_ref[...]
        lse, dlt = lse_ref[:, :1], dl_ref[:, :1]

        def step(kj, carry):
            dqn, dqp = carry
            s, ks = _att_scores(q_n, q_p, kn_ref, kp_ref, qi, kj, pad)
            pr = jnp.exp(s - lse)
            ds = pr * (_mxdot(d_o, v_ref[ks, :], 1, 1) - dlt) * scale
            return dqn + _mxdot(ds, kn_ref[ks, :], 1, 0), dqp + _mxdot(ds, kp_ref[ks, :], 1, 0)

        dqn, dqp = lax.fori_loop(0, qi + 1, step, (jnp.zeros((t, LANE), F32), jnp.zeros((t, LANE), F32)))
        dqn_ref[...] = dqn.astype(dqn_ref.dtype)
        dqp_ref[...] = dqp

    qspec = pl.BlockSpec((t, LANE), lambda b, h, i: (b * nb + i, h))
    kspec = pl.BlockSpec((lp, LANE), lambda b, h, i: (b, h))
    return pl.pallas_call(
        body, name="attn_dq", grid=(geo.bsz, MLA_HEADS, nb),
        in_specs=[qspec, qspec, kspec, kspec, pl.BlockSpec((lp, LANE), lambda b, h, i: (b, 0)), qspec, qspec, qspec],
        out_specs=[qspec, qspec],
        out_shape=[jax.ShapeDtypeStruct((geo.nrows, geo.hq), MXU_DTYPE), jax.ShapeDtypeStruct((geo.nrows, geo.hq), F32)],
        compiler_params=_cparams(("parallel", "parallel", "arbitrary")))(qn, qp, kn, v, kp, do, lse, delta)


def _attn_dkv(geo, qn, qp, kn, v, kp, do, lse, delta):
    nb, lp, pad, t = geo.lp // ATT_BLK, geo.lp, geo.pad, ATT_BLK
    scale = (MLA_NOPE + MLA_ROPE) ** -0.5

    def body(qn_ref, qp_ref, kn_ref, v_ref, kp_ref, do_ref, lse_ref, dl_ref, dkn_ref, dv_ref, dkp_ref):
        kj = pl.program_id(2)
        k_n, k_p, vv = kn_ref[...], kp_ref[...], v_ref[...]

        def step(qi, carry):
            dkn, dv, dkp = carry
            qs = pl.ds(pl.multiple_of(qi * t, t), t)
            q_n, q_p, d_o = qn_ref[qs, :], qp_ref[qs, :], do_ref[qs, :]
            s = (_mxdot(q_n, k_n, 1, 1) + _mxdot(q_p, k_p, 1, 1)) * scale
            s = jnp.where(_att_mask(qi, kj, pad), s, -jnp.inf)
            pr = jnp.exp(s - lse_ref[qs, :][:, :1])
            ds = pr * (_mxdot(d_o, vv, 1, 1) - dl_ref[qs, :][:, :1]) * scale
            return dkn + _mxdot(ds, q_n, 0, 0), dv + _mxdot(pr, d_o, 0, 0), dkp + _mxdot(ds, q_p, 0, 0)

        z = jnp.zeros((t, LANE), F32)
        dkn, dv, dkp = lax.fori_loop(kj, nb, step, (z, z, z))
        dkn_ref[...] = dkn.astype(dkn_ref.dtype)
        dv_ref[...] = dv.astype(dv_ref.dtype)
        dkp_ref[...] = dkp

    qspec = pl.BlockSpec((lp, LANE), lambda b, h, j: (b, h))
    kspec = pl.BlockSpec((t, LANE), lambda b, h, j: (b * nb + j, h))
    return pl.pallas_call(
        body, name="attn_dkv", grid=(geo.bsz, MLA_HEADS, nb),
        in_specs=[qspec, qspec, kspec, kspec, pl.BlockSpec((t, LANE), lambda b, h, j: (b * nb + j, 0)),
                  qspec, qspec, qspec],
        out_specs=[kspec, kspec, kspec],
        out_shape=[jax.ShapeDtypeStruct((geo.nrows, geo.hq), MXU_DTYPE)] * 2
        + [jax.ShapeDtypeStruct((geo.nrows, geo.hq), F32)],
        compiler_params=_cparams(("parallel", "parallel", "arbitrary")))(qn, qp, kn, v, kp, do, lse, delta)


def _rope(x, cos, sin):
    return x * cos + pltpu.roll(x, LANE // 2, axis=1) * sin


def _rope_t(dx, cos, sin):
    return dx * cos + pltpu.roll(dx * sin, LANE // 2, axis=1)


def _layer_fwd(geo, h, w, tab):
    nr, tr, trw = geo.nrows, geo.tr, geo.tr_wide
    tb = geo.lp // tr
    rw = functools.partial(_rowwise, nrows=nr)
    s = {"h": h}
    (s["u"],) = rw("rms_mix", lambda x, g: (_rms(x, g),), tr=tr, rows=[(h, D_MODEL, 0)],
                   vecs=[(w["norm_mix_w"], D_MODEL, 0)], outs=[(D_MODEL, D_MODEL, MXU_DTYPE)])
    proj = s["proj"] = _mm("mm_in", s["u"], w["w_in_p"])
    xc = s["xc"] = _conv_fwd(geo, proj, w["conv_w"], w["conv_b"])
    s["y_ssd"], s["s_prev"] = _ssd_fwd(geo, xc, proj, w["dt_bias"], w["a_log"])
    gw = SSM_D_INNER // SSM_GROUPS

    def gate_norm(y, x, z, dsk, nw):
        return (_rms((y + x * dsk) * _silu(z), nw),)

    (s["y_ssm"],) = rw("ssm_gate_norm", gate_norm, tr=tr, ncb=SSM_GROUPS,
                       rows=[(s["y_ssd"], gw, 0), (xc, gw, 0), (proj, gw, geo.col["z"][0] // gw)],
                       vecs=[(w["d_skip_full"], gw, 0), (w["ssm_norm_w"], gw, 0)], outs=[(SSM_D_INNER, gw, MXU_DTYPE)])
    (s["cq_n"],) = rw("rms_q", lambda x, g: (_rms(x, g),), tr=tr, rows=[(proj, MLA_Q_LORA, geo.cb("c_q"))],
                      vecs=[(w["q_norm_w"], MLA_Q_LORA, 0)], outs=[(MLA_Q_LORA, MLA_Q_LORA, MXU_DTYPE)])
    (s["ckv_n"],) = rw("rms_kv", lambda x, g: (_rms(x, g),), tr=tr, rows=[(proj, MLA_KV_LORA, geo.cb("c_kv"))],
                       vecs=[(w["kv_norm_w"], MLA_KV_LORA, 0)], outs=[(MLA_KV_LORA, MLA_KV_LORA, MXU_DTYPE)])
    s["qn"] = _mm("mm_qn", s["cq_n"], w["w_qn"], out_dtype=MXU_DTYPE)
    qp_raw = _mm("mm_qp", s["cq_n"], w["w_qp"])
    s["kn"] = _mm("mm_kn", s["ckv_n"], w["w_k"], out_dtype=MXU_DTYPE)
    s["v"] = _mm("mm_v", s["ckv_n"], w["w_v"], out_dtype=MXU_DTYPE)
    (s["qp"],) = rw("rope_q", lambda x, c, sn: (_rope(x, c, sn),), tr=tr, ncb=MLA_HEADS, rows=[(qp_raw, LANE, 0)],
                    tabs=[(tab["cos"], LANE, 0), (tab["sin"], LANE, 0)], outs=[(geo.hq, LANE, MXU_DTYPE)], tab_blocks=tb)
    (s["kp"],) = rw("rope_k", lambda x, c, sn: (_rope(x, c, sn),), tr=tr, rows=[(proj, LANE, geo.cb("k_rope"))],
                    tabs=[(tab["cos"], LANE, 0), (tab["sin"], LANE, 0)], outs=[(LANE, LANE, MXU_DTYPE)], tab_blocks=tb)
    s["o"], s["lse"] = _attn_fwd(geo, s["qn"], s["qp"], s["kn"], s["v"], s["kp"])
    s["ys_p"] = _mm("mm_bs", s["y_ssm"], w["w_branch_ssm"])
    s["ym_p"] = _mm("mm_bm", s["o"], w["w_branch_mla"])

    def gate(gs, gm, ys, ym):
        return (_sigmoid(gs) * ys + _sigmoid(gm) * ym,)

    (s["mixed"],) = rw("gate", gate, tr=tr, rows=[(proj, D_MODEL, geo.cb("g_ssm")), (proj, D_MODEL, geo.cb("g_mla")),
                                                  (s["ys_p"], D_MODEL, 0), (s["ym_p"], D_MODEL, 0)],
                       outs=[(D_MODEL, D_MODEL, MXU_DTYPE)])
    s["h2"] = _mm("mm_out", s["mixed"], w["w_out"], add=h)
    (s["vn"],) = rw("rms_mlp", lambda x, g: (_rms(x, g),), tr=tr, rows=[(s["h2"], D_MODEL, 0)],
                    vecs=[(w["norm_mlp_w"], D_MODEL, 0)], outs=[(D_MODEL, D_MODEL, MXU_DTYPE)])
    s["up"] = _mm("mm_up", s["vn"], w["w_mlp_up"])
    (s["act"],) = rw("relu2", lambda x: (jnp.square(jnp.maximum(x, 0.0)),), tr=trw, rows=[(s["up"], D_FF, 0)],
                     outs=[(D_FF, D_FF, MXU_DTYPE)])
    return _mm("mm_down", s["act"], w["w_mlp_down"], add=s["h2"]), s


def _rope_rows(tab, tb):
    return [(tab["cos"], LANE, 0), (tab["sin"], LANE, 0)], tb


def _layer_bwd(geo, dh3, s, w, tab):
    nr, tr, trw = geo.nrows, geo.tr, geo.tr_wide
    tb = geo.lp // tr
    rw = functools.partial(_rowwise, nrows=nr)
    g = {}
    proj = s["proj"]

    def rms_bwd(x, dy, res, gw):
        _, vjp = jax.vjp(_rms, x.astype(F32), gw)
        dx, dgw = vjp(dy.astype(F32))
        return dx + res, dgw

    def rms_bwd_nores(x, dy, gw):
        _, vjp = jax.vjp(_rms, x.astype(F32), gw)
        return vjp(dy.astype(F32))

    dact = _mm("mm_down_t", dh3, w["w_mlp_down"], tb=True)
    g["w_mlp_down"] = _mm("mm_down_g", s["act"], dh3, ta=True)
    (dup,) = rw("relu2_bwd", lambda d, x: (d * 2.0 * jnp.maximum(x, 0.0),), tr=trw,
                rows=[(dact, D_FF, 0), (s["up"], D_FF, 0)], outs=[(D_FF, D_FF, MXU_DTYPE)])
    g["w_mlp_up"] = _mm("mm_up_g", s["vn"], dup, ta=True)
    dvn = _mm("mm_up_t", dup, w["w_mlp_up"], tb=True)
    dh2, g["norm_mlp_w"] = rw("rms_mlp_bwd", rms_bwd, tr=tr,
                              rows=[(s["h2"], D_MODEL, 0), (dvn, D_MODEL, 0), (dh3, D_MODEL, 0)],
                              vecs=[(w["norm_mlp_w"], D_MODEL, 0)], outs=[(D_MODEL, D_MODEL, F32)],
                              reds=[(D_MODEL, D_MODEL)])
    dmixed = _mm("mm_out_t", dh2, w["w_out"], tb=True)
    g["w_out"] = _mm("mm_out_g", s["mixed"], dh2, ta=True)

    def gate_bwd(gs, gm, ys, ym, dm):
        f = lambda a, b, c, d: _sigmoid(a) * c + _sigmoid(b) * d
        _, vjp = jax.vjp(f, gs, gm, ys, ym)
        dgs, dgm, dys, dym = vjp(dm)
        return dys, dym, dgs, dgm

    dys_p, dym_p, dg_ssm, dg_mla = rw(
        "gate_bwd", gate_bwd, tr=tr,
        rows=[(proj, D_MODEL, geo.cb("g_ssm")), (proj, D_MODEL, geo.cb("g_mla")), (s["ys_p"], D_MODEL, 0),
              (s["ym_p"], D_MODEL, 0), (dmixed, D_MODEL, 0)], outs=[(D_MODEL, D_MODEL, MXU_DTYPE)] * 4)
    g["w_branch_ssm"] = _mm("mm_bs_g", s["y_ssm"], dys_p, ta=True)
    dy_ssm = _mm("mm_bs_t", dys_p, w["w_branch_ssm"], tb=True)
    g["w_branch_mla"] = _mm("mm_bm_g", s["o"], dym_p, ta=True)
    d_o = _mm("mm_bm_t", dym_p, w["w_branch_mla"], tb=True)
    (delta,) = rw("attn_delta", lambda a, b: (jnp.broadcast_to(jnp.sum(a * b, axis=1, keepdims=True), a.shape),),
                  tr=tr, ncb=MLA_HEADS, rows=[(d_o, LANE, 0), (s["o"], LANE, 0)], outs=[(geo.hq, LANE, F32)])
    att = (s["qn"], s["qp"], s["kn"], s["v"], s["kp"], d_o, s["lse"], delta)
    dqn, dqp = _attn_dq(geo, *att)
    dkn, dv, dkp_h = _attn_dkv(geo, *att)
    (dqp_raw,) = rw("rope_q_bwd", lambda x, c, sn: (_rope_t(x, c, sn),), tr=tr, ncb=MLA_HEADS, rows=[(dqp, LANE, 0)],
                    tabs=[(tab["cos"], LANE, 0), (tab["sin"], LANE, 0)], outs=[(geo.hq, LANE, MXU_DTYPE)], tab_blocks=tb)

    def rope_k_bwd(x, c, sn):
        tot = x[:, :LANE]
        for hd in range(1, MLA_HEADS):
            tot = tot + x[:, hd * LANE:(hd + 1) * LANE]
        return (_rope_t(tot, c, sn),)

    (dk_rope,) = rw("rope_k_bwd", rope_k_bwd, tr=tr, rows=[(dkp_h, geo.hq, 0)],
                    tabs=[(tab["cos"], LANE, 0), (tab["sin"], LANE, 0)], outs=[(LANE, LANE, MXU_DTYPE)], tab_blocks=tb)
    g["w_qn"] = _mm("mm_qn_g", s["cq_n"], dqn, ta=True)
    g["w_qp"] = _mm("mm_qp_g", s["cq_n"], dqp_raw, ta=True)
    dcq_n = _mm("mm_qp_t", dqp_raw, w["w_qp"], tb=True, add=_mm("mm_qn_t", dqn, w["w_qn"], tb=True))
    g["w_k"] = _mm("mm_kn_g", s["ckv_n"], dkn, ta=True)
    g["w_v"] = _mm("mm_v_g", s["ckv_n"], dv, ta=True)
    dckv_n = _mm("mm_v_t", dv, w["w_v"], tb=True, add=_mm("mm_kn_t", dkn, w["w_k"], tb=True))
    dc_q, g["q_norm_w"] = rw("rms_q_bwd", rms_bwd_nores, tr=tr,
                             rows=[(proj, MLA_Q_LORA, geo.cb("c_q")), (dcq_n, MLA_Q_LORA, 0)],
                             vecs=[(w["q_norm_w"], MLA_Q_LORA, 0)], outs=[(MLA_Q_LORA, MLA_Q_LORA, MXU_DTYPE)],
                             reds=[(MLA_Q_LORA, MLA_Q_LORA)])
    dc_kv, g["kv_norm_w"] = rw("rms_kv_bwd", rms_bwd_nores, tr=tr,
                               rows=[(proj, MLA_KV_LORA, geo.cb("c_kv")), (dckv_n, MLA_KV_LORA, 0)],
                               vecs=[(w["kv_norm_w"], MLA_KV_LORA, 0)], outs=[(MLA_KV_LORA, MLA_KV_LORA, MXU_DTYPE)],
                               reds=[(MLA_KV_LORA, MLA_KV_LORA)])
    gw_ = SSM_D_INNER // SSM_GROUPS

    def gate_norm_bwd(y, x, z, dy, dsk, nw):
        f = lambda y_, x_, z_, dsk_, nw_: _rms((y_ + x_ * dsk_) * _silu(z_), nw_)
        _, vjp = jax.vjp(f, y, x, z, dsk, nw)
        dy_, dx_, dz_, ddsk, dnw = vjp(dy)
        return dy_, dx_, dz_, ddsk, dnw

    dy_ssd, dxs_skip, dz, g["d_skip_full"], g["ssm_norm_w"] = rw(
        "ssm_gate_norm_bwd", gate_norm_bwd, tr=tr, ncb=SSM_GROUPS,
        rows=[(s["y_ssd"], gw_, 0), (s["xc"], gw_, 0), (proj, gw_, geo.col["z"][0] // gw_), (dy_ssm, gw_, 0)],
        vecs=[(w["d_skip_full"], gw_, 0), (w["ssm_norm_w"], gw_, 0)],
        outs=[(SSM_D_INNER, gw_, F32), (SSM_D_INNER, gw_, F32), (SSM_D_INNER, gw_, MXU_DTYPE)],
        reds=[(SSM_D_INNER, gw_), (SSM_D_INNER, gw_)])
    dxc, ddt, g["dt_bias"], g["a_log"] = _ssd_bwd(geo, s["xc"], proj, w["dt_bias"], w["a_log"], s["s_prev"],
                                                   dy_ssd, dxs_skip)
    dxbc, g["conv_w"], g["conv_b"] = _conv_bwd(geo, proj, w["conv_w"], w["conv_b"], dxc)
    di, gn = SSM_D_INNER, geo.gn
    dproj = jnp.concatenate([dz, dxbc[:, :di], dg_ssm, dg_mla, dxbc[:, di:di + gn], dxbc[:, di + gn:], dc_q, dc_kv,
                             ddt, dk_rope], axis=-1)
    g["w_in_p"] = _mm("mm_in_g", s["u"], dproj, ta=True)
    du = _mm("mm_in_t", dproj, w["w_in_p"], tb=True)
    dh, g["norm_mix_w"] = rw("rms_mix_bwd", rms_bwd, tr=tr,
                             rows=[(s["h"], D_MODEL, 0), (du, D_MODEL, 0), (dh2, D_MODEL, 0)],
                             vecs=[(w["norm_mix_w"], D_MODEL, 0)], outs=[(D_MODEL, D_MODEL, F32)],
                             reds=[(D_MODEL, D_MODEL)])
    return dh, g


def _loss_bwd(geo, h, fw, target, tab):
    tr = geo.tr

    def fn(x, tgt, gw, tok):
        def lossf(x_, gw_):
            err = jnp.square(_rms(x_, gw_) - tgt)
            return 0.5 * jnp.sum(tok * jnp.mean(err, axis=-1, keepdims=True), axis=0, keepdims=True)

        val, vjp = jax.vjp(lossf, x, gw)
        dx, dgw = vjp(jnp.ones((1, 1), F32))
        return dx, jnp.broadcast_to(val, (1, LANE)), dgw

    return _rowwise("loss", fn, nrows=geo.nrows, tr=tr, rows=[(h, D_MODEL, 0), (target, D_MODEL, 0)],
                    vecs=[(fw, D_MODEL, 0)], tabs=[(tab["token"], 1, 0)], outs=[(D_MODEL, D_MODEL, F32)],
                    reds=[(LANE, LANE), (D_MODEL, D_MODEL)], tab_blocks=geo.lp // tr)


def kernel(x, meta_tokens, norm_mix_w, w_in, conv_w, conv_b, dt_bias, a_log, d_skip, ssm_norm_w, q_norm_w, kv_norm_w, w_uq, w_ukv, w_branch_ssm, w_branch_mla, w_out, norm_mlp_w, w_mlp_up, w_mlp_down, final_norm_w, loss_target, m_meta_tokens, m_norm_mix_w, m_w_in, m_conv_w, m_conv_b, m_dt_bias, m_a_log, m_d_skip, m_ssm_norm_w, m_q_norm_w, m_kv_norm_w, m_w_uq, m_w_ukv, m_w_branch_ssm, m_w_branch_mla, m_w_out, m_norm_mlp_w, m_w_mlp_up, m_w_mlp_down, m_final_norm_w, v_meta_tokens, v_norm_mix_w, v_w_in, v_conv_w, v_conv_b, v_dt_bias, v_a_log, v_d_skip, v_ssm_norm_w, v_q_norm_w, v_kv_norm_w, v_w_uq, v_w_ukv, v_w_branch_ssm, v_w_branch_mla, v_w_out, v_norm_mlp_w, v_w_mlp_up, v_w_mlp_down, v_final_norm_w):
    args = dict(locals())
    wts = {n: args[n] for n in WEIGHTS}
    mom = {n: args["m_" + n] for n in WEIGHTS}
    var = {n: args["v_" + n] for n in WEIGHTS}
    bsz, seq, _ = x.shape
    depth = w_in.shape[0]
    geo = _Geo(bsz, seq)
    tab = _tables(geo)

    big_names = [n for n, _ in BIG]
    big_all = _exchange("gather_w", _pack([wts[n] for n in big_names], MXU_DTYPE), scatter=False)
    big_parts = _unpack(big_all, [wts[n].shape for n in big_names], lead=(N_DEV,))
    full = {n: _unshard(p, kind) for (n, kind), p in zip(BIG, big_parts)}
    f32_names = [n for n, _ in SHARDED_F32]
    f32_all = _exchange("gather_f32", _pack([wts[n] for n in f32_names], F32, row_mult=8), scatter=False)
    f32_parts = _unpack(f32_all, [wts[n].shape for n in f32_names], lead=(N_DEV,))
    full.update({n: _unshard(p, kind) for (n, kind), p in zip(SHARDED_F32, f32_parts)})
    layer_names = [n for n, _ in BIG] + ["conv_w", "norm_mix_w", "conv_b", "dt_bias", "a_log", "d_skip", "ssm_norm_w",
                                         "q_norm_w", "kv_norm_w", "norm_mlp_w"]
    layers = [_prep_layer(geo, {n: (full[n] if n in full else wts[n])[i] for n in layer_names}) for i in range(depth)]

    meta = jnp.broadcast_to(full["meta_tokens"][None], (bsz, N_META, D_MODEL))
    h = jnp.concatenate([jnp.zeros((bsz, geo.pad, D_MODEL), F32), meta, x], axis=1).reshape(geo.nrows, D_MODEL)
    target = jnp.concatenate([jnp.zeros((bsz, geo.pad + N_META, D_MODEL), F32), loss_target], axis=1)
    target = target.reshape(geo.nrows, D_MODEL)
    saved = []
    for i in range(depth):
        h, s = _layer_fwd(geo, h, layers[i], tab)
        saved.append(s)
    dh, loss_part, g_final = _loss_bwd(geo, h, final_norm_w.reshape(1, -1), target, tab)
    loss = lax.psum(loss_part[0, 0], ("x", "y", "c"))

    grads = [None] * depth
    for i in reversed(range(depth)):
        dh, gl = _layer_bwd(geo, dh, saved[i], layers[i], tab)
        grads[i] = _unprep_grads(geo, gl)
    dh = dh.reshape(bsz, geo.lp, D_MODEL)
    grad_x = dh[:, geo.pad + N_META:]
    gfull = {n: jnp.stack([grads[i][n] for i in range(depth)]) for n in grads[0]}
    gfull["meta_tokens"] = jnp.sum(dh[:, geo.pad:geo.pad + N_META], axis=0)
    gfull["final_norm_w"] = g_final.reshape(-1)

    sh_names = big_names + f32_names
    kinds = dict(BIG + SHARDED_F32)
    send = _pack_lead([_shard(gfull[n], kinds[n]) for n in sh_names], F32, row_mult=8)
    parts = _exchange("scatter_g", send, scatter=True)
    pk = lambda d: _pack([d[n] for n in sh_names], F32, row_mult=8)
    res_sh = _adamw("adamw_sharded", parts, pk(wts), pk(mom), pk(var))
    res_sh = [_unpack(r, [wts[n].shape for n in sh_names]) for r in res_sh]
    parts = _exchange("gather_g", _pack([gfull[n] for n in SMALL], F32, row_mult=8), scatter=False)
    pk = lambda d: _pack([d[n] for n in SMALL], F32, row_mult=8)
    res_sm = _adamw("adamw_small", parts, pk(wts), pk(mom), pk(var))
    res_sm = [_unpack(r, [wts[n].shape for n in SMALL]) for r in res_sm]

    out = [loss, grad_x]
    for k in range(4):
        named = dict(zip(sh_names, res_sh[k]))
        named.update(zip(SMALL, res_sm[k]))
        out += [named[n] for n in WEIGHTS]
    return tuple(out)
```

```python
import functools

import numpy as np
import jax
import jax.numpy as jnp
from jax import lax
from jax.experimental import pallas as pl
from jax.experimental.pallas import tpu as pltpu

F32 = jnp.float32
MXU_DTYPE = jnp.bfloat16

D_MODEL = 1024
N_META = 16
EPS = 1e-6
SSM_D_INNER = 2048
SSM_HEAD_DIM = 64
SSM_GROUPS = 4
SSM_STATE = 128
SSM_CONV = 4
SSM_CHUNK = 128
MLA_HEADS = 8
MLA_Q_LORA = 512
MLA_KV_LORA = 256
MLA_NOPE = 128
MLA_ROPE = 64
MLA_V = 128
ROPE_THETA = 10000.0
D_FF = 4096
ADAM_LR = 0.001
ADAM_B1 = 0.9
ADAM_B2 = 0.999
ADAM_EPS = 1e-08
ADAM_WD = 0.01
ADAM_STEP = 10

N_DEV = 8
ATT_BLK = 256
LANE = 128
PACK_W = 1024
VMEM_LIMIT = 56 * 1024 * 1024
MESH_ID = pl.DeviceIdType.MESH

BIG = (("w_in", "col"), ("w_uq", "col"), ("w_ukv", "col"), ("w_branch_ssm", "row"), ("w_branch_mla", "row"),
       ("w_out", "row"), ("w_mlp_up", "col"), ("w_mlp_down", "row"))
SHARDED_F32 = (("conv_w", "col"), ("meta_tokens", "col"))
SMALL = ("norm_mix_w", "conv_b", "dt_bias", "a_log", "d_skip", "ssm_norm_w", "q_norm_w", "kv_norm_w",
         "norm_mlp_w", "final_norm_w")
WEIGHTS = ("meta_tokens", "norm_mix_w", "w_in", "conv_w", "conv_b", "dt_bias", "a_log", "d_skip", "ssm_norm_w",
           "q_norm_w", "kv_norm_w", "w_uq", "w_ukv", "w_branch_ssm", "w_branch_mla", "w_out", "norm_mlp_w",
           "w_mlp_up", "w_mlp_down", "final_norm_w")


def _cparams(sem=None):
    return pltpu.CompilerParams(dimension_semantics=sem, vmem_limit_bytes=VMEM_LIMIT)


def _pick(n, cands):
    for c in cands:
        if n % c == 0:
            return c
    return n


def _sigmoid(x):
    return 1.0 / (1.0 + jnp.exp(-x))


def _silu(x):
    return x * _sigmoid(x)


def _softplus(x):
    return jnp.maximum(x, 0.0) + jnp.log1p(jnp.exp(-jnp.abs(x)))


def _rms(x, w):
    return x * lax.rsqrt(jnp.mean(x * x, axis=-1, keepdims=True) + EPS) * w


def _dot(a, b, ca, cb, precision=None):
    return lax.dot_general(a, b, (((ca,), (cb,)), ((), ())), preferred_element_type=F32, precision=precision)


def _mxdot(a, b, ca, cb):
    return _dot(a.astype(MXU_DTYPE), b.astype(MXU_DTYPE), ca, cb)


def _mm(name, a, b, *, ta=False, tb=False, add=None, out_dtype=F32):
    (kdim, m) = a.shape if ta else a.shape[::-1]
    (n, k2) = b.shape if tb else b.shape[::-1]
    assert kdim == k2, (name, a.shape, b.shape)
    tm = _pick(m, (1152, 1024, 768, 512, 384, 256, 128))
    tn = _pick(n, (1024, 512, 384, 256, 128))
    tk = _pick(kdim, (1152, 1024, 768, 512, 384, 256, 128))
    nk = kdim // tk
    a_spec = pl.BlockSpec((tk, tm), lambda i, j, k: (k, i)) if ta else pl.BlockSpec((tm, tk), lambda i, j, k: (i, k))
    b_spec = pl.BlockSpec((tn, tk), lambda i, j, k: (j, k)) if tb else pl.BlockSpec((tk, tn), lambda i, j, k: (k, j))
    o_spec = pl.BlockSpec((tm, tn), lambda i, j, k: (i, j))
    ca, cb = (0 if ta else 1), (1 if tb else 0)

    def body(*refs):
        a_ref, b_ref = refs[:2]
        o_ref, acc = refs[-2:]
        k = pl.program_id(2)

        @pl.when(k == 0)
        def _():
            acc[...] = jnp.zeros_like(acc)

        acc[...] += _mxdot(a_ref[...], b_ref[...], ca, cb)

        @pl.when(k == nk - 1)
        def _():
            r = acc[...]
            if add is not None:
                r = r + refs[2][...].astype(F32)
            o_ref[...] = r.astype(out_dtype)

    in_specs, args = [a_spec, b_spec], [a, b]
    if add is not None:
        in_specs.append(o_spec)
        args.append(add)
    return pl.pallas_call(
        body, name=name, grid=(m // tm, n // tn, nk), in_specs=in_specs, out_specs=o_spec,
        out_shape=jax.ShapeDtypeStruct((m, n), out_dtype), scratch_shapes=[pltpu.VMEM((tm, tn), F32)],
        compiler_params=_cparams(("parallel", "parallel", "arbitrary")))(*args)


def _rowwise(name, fn, *, nrows, tr, ncb=1, rows=(), fixed=(), vecs=(), tabs=(), outs=(), reds=(), tab_blocks=1):
    in_specs, args = [], []
    for arr, w, c0 in rows:
        in_specs.append(pl.BlockSpec((tr, w), lambda g, i, c0=c0: (i, c0 + g)))
        args.append(arr)
    for arr, w, c0 in fixed:
        in_specs.append(pl.BlockSpec((tr, w), lambda g, i, c0=c0: (i, c0)))
        args.append(arr)
    for arr, w, c0 in vecs:
        in_specs.append(pl.BlockSpec((1, w), lambda g, i, c0=c0: (0, c0 + g)))
        args.append(arr)
    for arr, w, c0 in tabs:
        in_specs.append(pl.BlockSpec((tr, w), lambda g, i, c0=c0: (i % tab_blocks, c0)))
        args.append(arr)
    out_shape = [jax.ShapeDtypeStruct((nrows, wt), dt) for wt, w, dt in outs]
    out_shape += [jax.ShapeDtypeStruct((1, wt), F32) for wt, w in reds]
    out_specs = [pl.BlockSpec((tr, w), lambda g, i: (i, g)) for wt, w, dt in outs]
    out_specs += [pl.BlockSpec((1, w), lambda g, i: (0, g)) for wt, w in reds]
    n_in, n_out = len(args), len(outs)

    def body(*refs):
        res = fn(*[r[...] for r in refs[:n_in]])
        for o_ref, val in zip(refs[n_in:n_in + n_out], res[:n_out]):
            o_ref[...] = val.astype(o_ref.dtype)
        i = pl.program_id(1)
        for d_ref, val in zip(refs[n_in + n_out:], res[n_out:]):
            @pl.when(i == 0)
            def _(d_ref=d_ref, val=val):
                d_ref[...] = val

            @pl.when(i > 0)
            def _(d_ref=d_ref, val=val):
                d_ref[...] += val

    res = pl.pallas_call(
        body, name=name, grid=(ncb, nrows // tr), in_specs=in_specs, out_specs=out_specs, out_shape=out_shape,
        compiler_params=_cparams(("parallel", "arbitrary")))(*args)
    return res


def _peer(k):
    x, y, c = lax.axis_index("x"), lax.axis_index("y"), lax.axis_index("c")
    px = jnp.where((k >> 2) & 1, 1 - x, x)
    py = jnp.where((k >> 1) & 1, 1 - y, y)
    pc = jnp.where(k & 1, 1 - c, c)
    return (px, py, pc), 4 * px + 2 * py + pc


def _my_index():
    return 4 * lax.axis_index("x") + 2 * lax.axis_index("y") + lax.axis_index("c")


def _exchange(name, ins, out_shapes, items):
    n_in, n_out, n_it = len(ins), len(out_shapes), len(items)

    def body(*refs):
        x, o = refs[:n_in], refs[n_in:n_in + n_out]
        send_sems, recv_sems, local_sems = refs[n_in + n_out:]
        me = _my_index()
        local, sends = [], []
        for t, (ii, io, src, dst) in enumerate(items):
            cp = pltpu.make_async_copy(src(x[ii], me), dst(o[io], me), local_sems.at[t])
            cp.start()
            local.append(cp)
        for k in range(1, N_DEV):
            dev, idx = _peer(k)
            for t, (ii, io, src, dst) in enumerate(items):
                s = (k - 1) * n_it + t
                cp = pltpu.make_async_remote_copy(
                    src_ref=src(x[ii], idx), dst_ref=dst(o[io], me), send_sem=send_sems.at[s],
                    recv_sem=recv_sems.at[s], device_id=dev, device_id_type=MESH_ID)
                cp.start()
                sends.append(cp)
        for k in range(1, N_DEV):
            dev, idx = _peer(k)
            for t, (ii, io, src, dst) in enumerate(items):
                s = (k - 1) * n_it + t
                pltpu.make_async_remote_copy(
                    src_ref=src(x[ii], idx), dst_ref=dst(o[io], idx), send_sem=send_sems.at[s],
                    recv_sem=recv_sems.at[s], device_id=dev, device_id_type=MESH_ID).wait_recv()
        for cp in sends:
            cp.wait_send()
        for cp in local:
            cp.wait()

    nsem = (N_DEV - 1) * n_it
    anyspec = pl.BlockSpec(memory_space=pl.ANY)
    return pl.pallas_call(
        body, name=name, out_shape=list(out_shapes), in_specs=[anyspec] * n_in, out_specs=[anyspec] * n_out,
        scratch_shapes=[pltpu.SemaphoreType.DMA((nsem,)), pltpu.SemaphoreType.DMA((nsem,)),
                        pltpu.SemaphoreType.DMA((n_it,))],
        compiler_params=pltpu.CompilerParams(has_side_effects=True))(*ins)


def _whole(ref, p):
    return ref


def _entry(ref, p):
    return ref.at[p]


def _gather_plan(shard, kind):
    l, a, b = shard.shape
    if kind == "row":
        return (l, N_DEV, a, b), (lambda ref, p: ref.at[:, p]), "row"
    if b % LANE == 0:
        return (l, a, N_DEV * b), (lambda ref, p: ref.at[:, :, pl.ds(pl.multiple_of(p * b, LANE), b)]), "col"
    return (N_DEV, l, a, b), _entry, "stack"


def _adamw_nat(name, parts, w, m, v):
    shape = w.shape
    b, c = shape[-2], shape[-1]
    a = int(np.prod(shape[:-2])) if len(shape) > 2 else 1
    tb = _pick(b, (128, 64, 32, 16, 8))
    spec = pl.BlockSpec((1, tb, c), lambda i, j: (i, j, 0))

    def body(p_ref, w_ref, m_ref, v_ref, g_ref, d_ref, nm_ref, nv_ref):
        g = p_ref[0]
        for j in range(1, N_DEV):
            g = g + p_ref[j]
        nm = ADAM_B1 * m_ref[...] + (1.0 - ADAM_B1) * g
        nv = ADAM_B2 * v_ref[...] + (1.0 - ADAM_B2) * jnp.square(g)
        m_hat = nm / (1.0 - ADAM_B1 ** ADAM_STEP)
        v_hat = nv / (1.0 - ADAM_B2 ** ADAM_STEP)
        g_ref[...] = g
        d_ref[...] = -ADAM_LR * (m_hat / (jnp.sqrt(v_hat) + ADAM_EPS) + ADAM_WD * w_ref[...])
        nm_ref[...] = nm
        nv_ref[...] = nv

    sds = jax.ShapeDtypeStruct((a, b, c), F32)
    res = pl.pallas_call(
        body, name=name, grid=(a, b // tb),
        in_specs=[pl.BlockSpec((N_DEV, 1, tb, c), lambda i, j: (0, i, j, 0)), spec, spec, spec],
        out_specs=[spec] * 4, out_shape=[sds] * 4, compiler_params=_cparams(("parallel", "parallel")))(
            parts.reshape(N_DEV, a, b, c), w.reshape(a, b, c), m.reshape(a, b, c), v.reshape(a, b, c))
    return [r.reshape(shape) for r in res]


def _adamw(name, parts, w, m, v):
    rows = w.shape[0]
    tr = _pick(rows, (256, 128, 64, 32, 16, 8))
    spec = pl.BlockSpec((tr, PACK_W), lambda i: (i, 0))

    def body(p_ref, w_ref, m_ref, v_ref, g_ref, d_ref, nm_ref, nv_ref):
        g = p_ref[0]
        for j in range(1, N_DEV):
            g = g + p_ref[j]
        nm = ADAM_B1 * m_ref[...] + (1.0 - ADAM_B1) * g
        nv = ADAM_B2 * v_ref[...] + (1.0 - ADAM_B2) * jnp.square(g)
        m_hat = nm / (1.0 - ADAM_B1 ** ADAM_STEP)
        v_hat = nv / (1.0 - ADAM_B2 ** ADAM_STEP)
        g_ref[...] = g
        d_ref[...] = -ADAM_LR * (m_hat / (jnp.sqrt(v_hat) + ADAM_EPS) + ADAM_WD * w_ref[...])
        nm_ref[...] = nm
        nv_ref[...] = nv

    sds = jax.ShapeDtypeStruct((rows, PACK_W), F32)
    return pl.pallas_call(
        body, name=name, grid=(rows // tr,),
        in_specs=[pl.BlockSpec((N_DEV, tr, PACK_W), lambda i: (0, i, 0)), spec, spec, spec],
        out_specs=[spec] * 4, out_shape=[sds] * 4, compiler_params=_cparams(("parallel",)))(parts, w, m, v)


def _pack(arrs, dtype, row_mult=16):
    flat = jnp.concatenate([a.reshape(-1).astype(dtype) for a in arrs])
    unit = row_mult * PACK_W
    total = -(-flat.shape[0] // unit) * unit
    flat = jnp.pad(flat, (0, total - flat.shape[0]))
    return flat.reshape(-1, PACK_W)


def _pack_lead(arrs, dtype, row_mult):
    flat = jnp.concatenate([a.reshape(N_DEV, -1).astype(dtype) for a in arrs], axis=1)
    unit = row_mult * PACK_W
    total = -(-flat.shape[1] // unit) * unit
    flat = jnp.pad(flat, ((0, 0), (0, total - flat.shape[1])))
    return flat.reshape(N_DEV, -1, PACK_W)


def _unpack(buf, shapes, lead=()):
    flat = buf.reshape(lead + (-1,))
    out, off = [], 0
    for s in shapes:
        n = int(np.prod(s))
        out.append(flat[..., off:off + n].reshape(lead + tuple(s)))
        off += n
    return out


def _unshard(g, kind):
    if kind == "col":
        g = jnp.moveaxis(g, 0, -2)
        return g.reshape(g.shape[:-2] + (g.shape[-2] * g.shape[-1],))
    g = jnp.moveaxis(g, 0, 1)
    return g.reshape((g.shape[0], g.shape[1] * g.shape[2]) + g.shape[3:])


def _shard(full, kind):
    if kind == "col":
        s = full.reshape(full.shape[:-1] + (N_DEV, full.shape[-1] // N_DEV))
        return jnp.moveaxis(s, -2, 0)
    s = full.reshape((full.shape[0], N_DEV, full.shape[1] // N_DEV) + full.shape[2:])
    return jnp.moveaxis(s, 1, 0)


class _Geo:
    def __init__(self, bsz, seq):
        self.bsz, self.seq = bsz, seq
        self.pad = (-(N_META + seq)) % ATT_BLK
        self.lp = self.pad + N_META + seq
        assert (self.pad + N_META) % SSM_CHUNK == 0 and self.lp % SSM_CHUNK == 0
        self.nrows = bsz * self.lp
        self.nc = self.lp // SSM_CHUNK
        self.nh = SSM_D_INNER // SSM_HEAD_DIM
        self.gn = SSM_GROUPS * SSM_STATE
        self.cd = SSM_D_INNER + 2 * self.gn
        self.hq = MLA_HEADS * LANE
        order = (("z", SSM_D_INNER), ("xs", SSM_D_INNER), ("g_ssm", D_MODEL), ("g_mla", D_MODEL), ("bm", self.gn),
                 ("cm", self.gn), ("c_q", MLA_Q_LORA), ("c_kv", MLA_KV_LORA), ("dt", LANE), ("k_rope", LANE))
        self.col, off = {}, 0
        for nm, w in order:
            assert off % w == 0, (nm, off, w)
            self.col[nm] = (off, w)
            off += w
        self.pw = off
        assert self.nh <= LANE and MLA_ROPE == 64 and MLA_NOPE == LANE and MLA_V == LANE
        self.tr = _pick(self.lp, (768, 512, 384, 256, 128))
        self.tr_wide = _pick(self.lp, (384, 256, 128))

    def cb(self, nm):
        off, w = self.col[nm]
        return off // w

    def w_in_runs(self, shard_w):
        nh, half = self.nh, MLA_ROPE // 2
        src, pieces = 0, []
        for nm, n in (("z", SSM_D_INNER), ("xs", SSM_D_INNER), ("bm", self.gn), ("cm", self.gn), ("dt", nh),
                      ("c_q", MLA_Q_LORA), ("c_kv", MLA_KV_LORA), ("k_rope", MLA_ROPE), ("g_ssm", D_MODEL),
                      ("g_mla", D_MODEL)):
            dst = self.col[nm][0]
            if nm == "k_rope":
                pieces += [(src, half, dst), (src + half, half, dst + 2 * half)]
            else:
                pieces.append((src, n, dst))
            src += n
        assert src == shard_w * N_DEV
        runs = []
        for a, n, dst in pieces:
            for j in range(N_DEV):
                lo, hi = max(a, j * shard_w), min(a + n, (j + 1) * shard_w)
                if lo < hi:
                    runs.append((j, lo - j * shard_w, hi - lo, dst + lo - a))
        return runs


def _slot(a):
    h = MLA_ROPE // 2
    z = jnp.zeros(a.shape[:-1] + (h,), a.dtype)
    return jnp.concatenate([a[..., :h], z, a[..., h:], z], axis=-1)


def _unslot(a):
    h = MLA_ROPE // 2
    return jnp.concatenate([a[..., :h], a[..., 2 * h:3 * h]], axis=-1)


def _prep_layer(geo, wl):
    nh = geo.nh
    p = {"w_in_p": wl["w_in_p"]}
    uq = wl["w_uq"].reshape(MLA_Q_LORA, MLA_HEADS, MLA_NOPE + MLA_ROPE)
    p["w_qn"] = uq[..., :MLA_NOPE].reshape(MLA_Q_LORA, geo.hq)
    p["w_qp"] = _slot(uq[..., MLA_NOPE:]).reshape(MLA_Q_LORA, geo.hq)
    ukv = wl["w_ukv"].reshape(MLA_KV_LORA, MLA_HEADS, MLA_NOPE + MLA_V)
    p["w_k"] = ukv[..., :MLA_NOPE].reshape(MLA_KV_LORA, geo.hq)
    p["w_v"] = ukv[..., MLA_NOPE:].reshape(MLA_KV_LORA, geo.hq)
    for nm in ("w_branch_ssm", "w_branch_mla", "w_out", "w_mlp_up", "w_mlp_down"):
        p[nm] = wl[nm]
    p["conv_w"] = wl["conv_w"]
    for nm in ("norm_mix_w", "conv_b", "ssm_norm_w", "q_norm_w", "kv_norm_w", "norm_mlp_w"):
        p[nm] = wl[nm].reshape(1, -1)
    p["dt_bias"] = jnp.pad(wl["dt_bias"], (0, LANE - nh)).reshape(1, LANE)
    p["a_log"] = jnp.pad(wl["a_log"], (0, LANE - nh)).reshape(1, LANE)
    p["d_skip_full"] = jnp.repeat(wl["d_skip"], SSM_HEAD_DIM).reshape(1, SSM_D_INNER)
    return p


def _unprep_grads(geo, g):
    nh = geo.nh
    qn = g["w_qn"].reshape(MLA_Q_LORA, MLA_HEADS, MLA_NOPE)
    qp = _unslot(g["w_qp"].reshape(MLA_Q_LORA, MLA_HEADS, LANE))
    w_uq = jnp.concatenate([qn, qp], axis=-1).reshape(MLA_Q_LORA, -1)
    wk = g["w_k"].reshape(MLA_KV_LORA, MLA_HEADS, MLA_NOPE)
    wv = g["w_v"].reshape(MLA_KV_LORA, MLA_HEADS, MLA_V)
    w_ukv = jnp.concatenate([wk, wv], axis=-1).reshape(MLA_KV_LORA, -1)
    out = {"w_in_p": g["w_in_p"], "w_uq": w_uq, "w_ukv": w_ukv}
    for nm in ("w_branch_ssm", "w_branch_mla", "w_out", "w_mlp_up", "w_mlp_down", "conv_w"):
        out[nm] = g[nm]
    for nm in ("norm_mix_w", "conv_b", "ssm_norm_w", "q_norm_w", "kv_norm_w", "norm_mlp_w"):
        out[nm] = g[nm].reshape(-1)
    out["dt_bias"] = g["dt_bias"].reshape(-1)[:nh]
    out["a_log"] = g["a_log"].reshape(-1)[:nh]
    out["d_skip"] = g["d_skip_full"].reshape(nh, SSM_HEAD_DIM).sum(-1)
    return out


def _tables(geo):
    pos = jnp.arange(geo.lp, dtype=F32) - geo.pad
    inv = ROPE_THETA ** (-jnp.arange(0, MLA_ROPE, 2, dtype=F32) / MLA_ROPE)
    ang = pos[:, None] * inv[None, :]
    cos, sin = jnp.cos(ang), jnp.sin(ang)
    z = jnp.zeros_like(cos)
    rows = jnp.arange(geo.lp)[:, None]
    return {"cos": jnp.concatenate([cos, z, cos, z], axis=-1), "sin": jnp.concatenate([-sin, z, sin, z], axis=-1),
            "valid": (rows >= geo.pad).astype(F32), "token": (rows >= geo.pad + N_META).astype(F32)}


def _w_in_assemble(geo, gathered):
    _, depth, d, sw = gathered.shape
    runs = geo.w_in_runs(sw)
    tr = _pick(d, (256, 128))

    def body(x_ref, o_ref):
        o_ref[...] = jnp.zeros_like(o_ref)
        for j, s0, n, d0 in runs:
            o_ref[0, :, d0:d0 + n] = x_ref[j, 0, :, s0:s0 + n]

    return pl.pallas_call(
        body, name="w_in_assemble", grid=(depth, d // tr),
        in_specs=[pl.BlockSpec((N_DEV, 1, tr, sw), lambda l, i: (0, l, i, 0))],
        out_specs=pl.BlockSpec((1, tr, geo.pw), lambda l, i: (l, i, 0)),
        out_shape=jax.ShapeDtypeStruct((depth, d, geo.pw), gathered.dtype),
        compiler_params=_cparams(("parallel", "parallel")))(gathered)


def _w_in_split(geo, g_padded, sw):
    d = g_padded.shape[0]
    runs = geo.w_in_runs(sw)
    tr = _pick(d, (128,))

    def body(x_ref, o_ref):
        for j, s0, n, d0 in runs:
            o_ref[j, :, s0:s0 + n] = x_ref[:, d0:d0 + n]

    return pl.pallas_call(
        body, name="w_in_split", grid=(d // tr,), in_specs=[pl.BlockSpec((tr, geo.pw), lambda i: (i, 0))],
        out_specs=pl.BlockSpec((N_DEV, tr, sw), lambda i: (0, i, 0)),
        out_shape=jax.ShapeDtypeStruct((N_DEV, d, sw), g_padded.dtype),
        compiler_params=_cparams(("parallel",)))(g_padded)


def _conv_cols(geo, cbw):
    nx = SSM_D_INNER // cbw
    x0, b0 = geo.col["xs"][0] // cbw, geo.col["bm"][0] // cbw
    assert geo.col["cm"][0] == geo.col["bm"][0] + geo.gn
    return lambda j: jnp.where(j < nx, x0 + j, b0 + j - nx)


def _conv_pre(x, w_ref, b_ref):
    acc = b_ref[...] + x * w_ref[SSM_CONV - 1:SSM_CONV, :]
    for k in range(SSM_CONV - 1):
        acc = acc + pltpu.roll(x, SSM_CONV - 1 - k, axis=0) * w_ref[k:k + 1, :]
    return acc


def _conv_fwd(geo, proj, conv_w, conv_b):
    cbw = 256
    colmap = _conv_cols(geo, cbw)
    lp, pad = geo.lp, geo.pad

    def body(x_ref, w_ref, b_ref, o_ref):
        valid = (lax.broadcasted_iota(jnp.int32, (lp, 1), 0) >= pad).astype(F32)
        o_ref[...] = _silu(_conv_pre(x_ref[...], w_ref, b_ref)) * valid

    return pl.pallas_call(
        body, name="conv_fwd", grid=(geo.bsz, geo.cd // cbw),
        in_specs=[pl.BlockSpec((lp, cbw), lambda b, j: (b, colmap(j))),
                  pl.BlockSpec((SSM_CONV, cbw), lambda b, j: (0, j)), pl.BlockSpec((1, cbw), lambda b, j: (0, j))],
        out_specs=pl.BlockSpec((lp, cbw), lambda b, j: (b, j)),
        out_shape=jax.ShapeDtypeStruct((geo.nrows, geo.cd), F32),
        compiler_params=_cparams(("parallel", "parallel")))(proj, conv_w, conv_b)


def _conv_bwd(geo, proj, conv_w, conv_b, dxc):
    cbw = 256
    colmap = _conv_cols(geo, cbw)
    lp, pad = geo.lp, geo.pad

    def body(x_ref, w_ref, b_ref, dy_ref, dx_ref, gw_ref, gb_ref):
        b = pl.program_id(1)
        valid = (lax.broadcasted_iota(jnp.int32, (lp, 1), 0) >= pad).astype(F32)
        x = x_ref[...]
        pre = _conv_pre(x, w_ref, b_ref)
        sig = _sigmoid(pre)
        dpre = dy_ref[...] * (sig * (1.0 + pre * (1.0 - sig))) * valid
        dx = dpre * w_ref[SSM_CONV - 1:SSM_CONV, :]
        gws = [jnp.sum(dpre * x, axis=0, keepdims=True)]
        for k in range(SSM_CONV - 2, -1, -1):
            s = SSM_CONV - 1 - k
            dx = dx + pltpu.roll(dpre, lp - s, axis=0) * w_ref[k:k + 1, :]
            gws.insert(0, jnp.sum(dpre * pltpu.roll(x, s, axis=0), axis=0, keepdims=True))
        dx_ref[...] = (dx * valid).astype(dx_ref.dtype)

        @pl.when(b == 0)
        def _():
            gw_ref[...] = jnp.zeros_like(gw_ref)
            gb_ref[...] = jnp.zeros_like(gb_ref)

        for k in range(SSM_CONV):
            gw_ref[k:k + 1, :] += gws[k]
        gb_ref[...] += jnp.sum(dpre, axis=0, keepdims=True)

    return pl.pallas_call(
        body, name="conv_bwd", grid=(geo.cd // cbw, geo.bsz),
        in_specs=[pl.BlockSpec((lp, cbw), lambda j, b: (b, colmap(j))),
                  pl.BlockSpec((SSM_CONV, cbw), lambda j, b: (0, j)), pl.BlockSpec((1, cbw), lambda j, b: (0, j)),
                  pl.BlockSpec((lp, cbw), lambda j, b: (b, j))],
        out_specs=[pl.BlockSpec((lp, cbw), lambda j, b: (b, j)), pl.BlockSpec((SSM_CONV, cbw), lambda j, b: (0, j)),
                   pl.BlockSpec((1, cbw), lambda j, b: (0, j))],
        out_shape=[jax.ShapeDtypeStruct((geo.nrows, geo.cd), MXU_DTYPE),
                   jax.ShapeDtypeStruct((SSM_CONV, geo.cd), F32), jax.ShapeDtypeStruct((1, geo.cd), F32)],
        compiler_params=_cparams(("parallel", "arbitrary")))(proj, conv_w, conv_b, dxc)


def _tri(q):
    r = lax.broadcasted_iota(jnp.int32, (q, q), 0)
    c = lax.broadcasted_iota(jnp.int32, (q, q), 1)
    return r >= c


def _ssd_pre(dtr, dtb, alog, valid):
    dt = _softplus(dtr + dtb) * valid
    adt = dt * (-jnp.exp(alog))
    a_cs = _dot(_tri(SSM_CHUNK).astype(F32), adt, 1, 0, precision=lax.Precision.HIGHEST)
    return dt, a_cs


def _ssd_specs(geo, rev):
    nc, q = geo.nc, SSM_CHUNK
    ci = (lambda c: nc - 1 - c) if rev else (lambda c: c)
    nxb = SSM_D_INNER // geo.gn
    return [pl.BlockSpec((q, SSM_D_INNER), lambda b, c: (b * nc + ci(c), 0)),
            pl.BlockSpec((q, geo.gn), lambda b, c: (b * nc + ci(c), nxb)),
            pl.BlockSpec((q, geo.gn), lambda b, c: (b * nc + ci(c), nxb + 1)),
            pl.BlockSpec((q, LANE), lambda b, c: (b * nc + ci(c), geo.cb("dt"))),
            pl.BlockSpec((1, LANE), lambda b, c: (0, 0)), pl.BlockSpec((1, LANE), lambda b, c: (0, 0))], ci


def _ssd_fwd(geo, xc, proj, dt_bias, a_log):
    q, p, n, e = SSM_CHUNK, SSM_HEAD_DIM, SSM_STATE, geo.nh // SSM_GROUPS
    nc, pad = geo.nc, geo.pad
    in_specs, _ = _ssd_specs(geo, False)

    def body(xs_ref, b_ref, c_ref, dtr_ref, dtb_ref, alog_ref, y_ref, sp_ref, state):
        c = pl.program_id(1)

        @pl.when(c == 0)
        def _():
            state[...] = jnp.zeros_like(state)

        sp_ref[...] = state[...]
        valid = (c * q + lax.broadcasted_iota(jnp.int32, (q, 1), 0) >= pad).astype(F32)
        dt, a_cs = _ssd_pre(dtr_ref[...], dtb_ref[...], alog_ref[...], valid)
        a_cst = a_cs.T
        tri = _tri(q)
        for g in range(SSM_GROUPS):
            bg, cg = b_ref[:, g * n:(g + 1) * n], c_ref[:, g * n:(g + 1) * n]
            cb = _mxdot(cg, bg, 1, 1)
            for hh in range(e):
                h = g * e + hh
                a_col, a_row, a_last = a_cs[:, h:h + 1], a_cst[h:h + 1, :], a_cs[q - 1:q, h:h + 1]
                xdt = xs_ref[:, h * p:(h + 1) * p] * dt[:, h:h + 1]
                ldec = jnp.exp(jnp.where(tri, a_col - a_row, -jnp.inf))
                s_prev = state[h * p:(h + 1) * p, :]
                y = _mxdot(cb * ldec, xdt, 1, 0) + _mxdot(cg, s_prev, 1, 1) * jnp.exp(a_col)
                y_ref[:, h * p:(h + 1) * p] = y
                st = _mxdot(xdt, bg * jnp.exp(a_last - a_col), 0, 0)
                state[h * p:(h + 1) * p, :] = s_prev * jnp.exp(a_last) + st

    return pl.pallas_call(
        body, name="ssd_fwd", grid=(geo.bsz, nc), in_specs=in_specs,
        out_specs=[pl.BlockSpec((q, SSM_D_INNER), lambda b, c: (b * nc + c, 0)),
                   pl.BlockSpec((SSM_D_INNER, n), lambda b, c: (b * nc + c, 0))],
        out_shape=[jax.ShapeDtypeStruct((geo.nrows, SSM_D_INNER), F32),
                   jax.ShapeDtypeStruct((geo.bsz * nc * SSM_D_INNER, n), F32)],
        scratch_shapes=[pltpu.VMEM((SSM_D_INNER, n), F32)],
        compiler_params=_cparams(("parallel", "arbitrary")))(xc, xc, xc, proj, dt_bias, a_log)


def _ssd_bwd(geo, xc, proj, dt_bias, a_log, s_prev_all, dy, dxs_skip):
    q, p, n, e = SSM_CHUNK, SSM_HEAD_DIM, SSM_STATE, geo.nh // SSM_GROUPS
    nc, pad, di, gn = geo.nc, geo.pad, SSM_D_INNER, geo.gn
    in_specs, ci = _ssd_specs(geo, True)
    row_spec = pl.BlockSpec((q, di), lambda b, c: (b * nc + ci(c), 0))
    in_specs += [pl.BlockSpec((di, n), lambda b, c: (b * nc + ci(c), 0)), row_spec, row_spec]

    def body(xs_ref, b_ref, c_ref, dtr_ref, dtb_ref, alog_ref, sp_ref, dy_ref, dsk_ref,
             dxc_ref, ddt_ref, gdtb_ref, galog_ref, dstate):
        step = pl.program_id(1)
        first = jnp.logical_and(pl.program_id(0) == 0, step == 0)
        c = nc - 1 - step

        @pl.when(step == 0)
        def _():
            dstate[...] = jnp.zeros_like(dstate)

        valid = (c * q + lax.broadcasted_iota(jnp.int32, (q, 1), 0) >= pad).astype(F32)
        dtr, dtb, alog = dtr_ref[...], dtb_ref[...], alog_ref[...]
        dt, a_cs = _ssd_pre(dtr, dtb, alog, valid)
        a_cst = a_cs.T
        tri = _tri(q)
        lane = lax.broadcasted_iota(jnp.int32, (1, LANE), 1)
        sub = lax.broadcasted_iota(jnp.int32, (LANE, 1), 0)
        d_dt = jnp.zeros((q, LANE), F32)
        d_acs = jnp.zeros((q, LANE), F32)
        d_acst = jnp.zeros((LANE, q), F32)
        d_last = jnp.zeros((1, LANE), F32)
        for g in range(SSM_GROUPS):
            bg, cg = b_ref[:, g * n:(g + 1) * n], c_ref[:, g * n:(g + 1) * n]
            cb = _mxdot(cg, bg, 1, 1)
            d_cb = jnp.zeros((q, q), F32)
            d_bg = jnp.zeros((q, n), F32)
            d_cg = jnp.zeros((q, n), F32)
            for hh in range(e):
                h = g * e + hh
                hs = slice(h * p, (h + 1) * p)
                a_col, a_row, a_last = a_cs[:, h:h + 1], a_cst[h:h + 1, :], a_cs[q - 1:q, h:h + 1]
                x = xs_ref[:, hs]
                dt_col = dt[:, h:h + 1]
                xdt = x * dt_col
                ldec = jnp.exp(jnp.where(tri, a_col - a_row, -jnp.inf))
                s_prev = sp_ref[hs, :]
                d_snew = dstate[hs, :]
                dyh = dy_ref[:, hs]
                e_col, e_last = jnp.exp(a_col), jnp.exp(a_last)
                dec = jnp.exp(a_last - a_col)
                d_m = _mxdot(dyh, xdt, 1, 1)
                d_xdt = _mxdot(cb * ldec, dyh, 0, 0)
                d_cb = d_cb + d_m * ldec
                d_diff = d_m * cb * ldec
                da_col = jnp.sum(d_diff, axis=1, keepdims=True)
                da_row = -jnp.sum(d_diff, axis=0, keepdims=True)
                cs = _mxdot(cg, s_prev, 1, 1)
                d_cs = dyh * e_col
                da_col = da_col + jnp.sum(dyh * cs, axis=1, keepdims=True) * e_col
                d_cg = d_cg + _mxdot(d_cs, s_prev, 1, 0)
                d_sprev = _mxdot(d_cs, cg, 0, 0) + d_snew * e_last
                dl = jnp.sum(jnp.sum(d_snew * s_prev, axis=1, keepdims=True), axis=0, keepdims=True) * e_last
                d_xdt = d_xdt + _mxdot(bg * dec, d_snew, 1, 1)
                d_bd = _mxdot(xdt, d_snew, 1, 0)
                d_bg = d_bg + d_bd * dec
                d_dec = jnp.sum(d_bd * bg, axis=1, keepdims=True) * dec
                dl = dl + jnp.sum(d_dec, axis=0, keepdims=True)
                da_col = da_col - d_dec
                dstate[hs, :] = d_sprev
                dxc_ref[:, hs] = d_xdt * dt_col + dsk_ref[:, hs]
                onehot = (lane == h).astype(F32)
                d_dt = d_dt + jnp.sum(d_xdt * x, axis=1, keepdims=True) * onehot
                d_acs = d_acs + da_col * onehot
                d_acst = d_acst + (sub == h).astype(F32) * da_row
                d_last = d_last + dl * onehot
            dxc_ref[:, di + g * n:di + (g + 1) * n] = d_bg + _mxdot(d_cb, cg, 0, 0)
            dxc_ref[:, di + gn + g * n:di + gn + (g + 1) * n] = d_cg + _mxdot(d_cb, bg, 1, 0)
        is_last = (lax.broadcasted_iota(jnp.int32, (q, 1), 0) == q - 1).astype(F32)
        d_acs = d_acs + d_acst.T + is_last * d_last
        d_adt = _dot(_tri(q).astype(F32), d_acs, 0, 0, precision=lax.Precision.HIGHEST)
        a = -jnp.exp(alog)
        d_dt = d_dt + d_adt * a
        g_alog = jnp.sum(d_adt * dt, axis=0, keepdims=True) * a
        d_dtr = d_dt * valid * _sigmoid(dtr + dtb)
        ddt_ref[...] = d_dtr.astype(ddt_ref.dtype)
        g_dtb = jnp.sum(d_dtr, axis=0, keepdims=True)

        @pl.when(first)
        def _():
            gdtb_ref[...] = g_dtb
            galog_ref[...] = g_alog

        @pl.when(jnp.logical_not(first))
        def _():
            gdtb_ref[...] += g_dtb
            galog_ref[...] += g_alog

    vec = pl.BlockSpec((1, LANE), lambda b, c: (0, 0))
    return pl.pallas_call(
        body, name="ssd_bwd", grid=(geo.bsz, nc), in_specs=in_specs,
        out_specs=[pl.BlockSpec((q, geo.cd), lambda b, c: (b * nc + ci(c), 0)),
                   pl.BlockSpec((q, LANE), lambda b, c: (b * nc + ci(c), 0)), vec, vec],
        out_shape=[jax.ShapeDtypeStruct((geo.nrows, geo.cd), F32), jax.ShapeDtypeStruct((geo.nrows, LANE), MXU_DTYPE),
                   jax.ShapeDtypeStruct((1, LANE), F32), jax.ShapeDtypeStruct((1, LANE), F32)],
        scratch_shapes=[pltpu.VMEM((di, n), F32)],
        compiler_params=_cparams(("arbitrary", "arbitrary")))(xc, xc, xc, proj, dt_bias, a_log, s_prev_all, dy, dxs_skip)


BIAS_LANE = MLA_ROPE // 2
KEY_OFF = -1e30
ATT_SCALE = (MLA_NOPE + MLA_ROPE) ** -0.5


def _row_t(col):
    return jnp.broadcast_to(col, (col.shape[0], LANE)).T[:8]


def _attn_fwd2(geo, qc, kc, v):
    t, lp = ATT_BLK, geo.lp
    nb = lp // t

    def body(q_ref, k_ref, v_ref, o_ref, lse_ref):
        qi = pl.program_id(2)
        q = q_ref[...]

        def blk(kj, carry, diag):
            m, l, acc = carry
            ks = pl.ds(pl.multiple_of(kj * t, t), t)
            s = _mxdot(q, k_ref[ks, :], 1, 1) * ATT_SCALE
            if diag:
                s = jnp.where(_tri(t), s, -jnp.inf)
            m_new = jnp.maximum(m, jnp.max(s, axis=1, keepdims=True))
            pr = jnp.exp(s - m_new)
            alpha = jnp.exp(m - m_new)
            return m_new, alpha * l + jnp.sum(pr, axis=1, keepdims=True), alpha * acc + _mxdot(pr, v_ref[ks, :], 1, 0)

        init = (jnp.full((t, 1), 2.0 * KEY_OFF, F32), jnp.zeros((t, 1), F32), jnp.zeros((t, LANE), F32))
        carry = lax.fori_loop(0, qi, lambda kj, c: blk(kj, c, False), init)
        m, l, acc = blk(qi, carry, True)
        o_ref[...] = acc / l
        lse_ref[0, 0, 0] = _row_t(m + jnp.log(l))

    return pl.pallas_call(
        body, name="attn_fwd", grid=(geo.bsz, MLA_HEADS, nb),
        in_specs=[pl.BlockSpec((t, 2 * LANE), lambda b, h, i: (b * nb + i, h)),
                  pl.BlockSpec((lp, 2 * LANE), lambda b, h, i: (b, h)), pl.BlockSpec((lp, LANE), lambda b, h, i: (b, h))],
        out_specs=[pl.BlockSpec((t, LANE), lambda b, h, i: (b * nb + i, h)),
                   pl.BlockSpec((1, 1, 1, 8, t), lambda b, h, i: (b, h, i, 0, 0))],
        out_shape=[jax.ShapeDtypeStruct((geo.nrows, geo.hq), F32),
                   jax.ShapeDtypeStruct((geo.bsz, MLA_HEADS, nb, 8, t), F32)],
        compiler_params=_cparams(("parallel", "parallel", "arbitrary")))(qc, kc, v)


def _attn_delta(geo, d_o, o):
    t, nb = ATT_BLK, geo.lp // ATT_BLK

    def body(do_ref, o_ref, dl_ref):
        dl_ref[0, 0, 0] = _row_t(jnp.sum(do_ref[...] * o_ref[...], axis=1, keepdims=True))

    spec = pl.BlockSpec((t, LANE), lambda b, h, i: (b * nb + i, h))
    return pl.pallas_call(
        body, name="attn_delta", grid=(geo.bsz, MLA_HEADS, nb), in_specs=[spec, spec],
        out_specs=pl.BlockSpec((1, 1, 1, 8, t), lambda b, h, i: (b, h, i, 0, 0)),
        out_shape=jax.ShapeDtypeStruct((geo.bsz, MLA_HEADS, nb, 8, t), F32),
        compiler_params=_cparams(("parallel", "parallel", "parallel")))(d_o, o)


def _attn_bwd2(geo, qc, kc, v, d_o, lse, delta):
    t, lp = ATT_BLK, geo.lp
    nb = lp // t

    def body(q_ref, k_ref, v_ref, do_ref, lse_ref, dl_ref, dq_ref, dk_ref, dv_ref):
        kj = pl.program_id(2)

        @pl.when(kj == 0)
        def _():
            dq_ref[...] = jnp.zeros_like(dq_ref)

        k, vv = k_ref[...], v_ref[...]

        def blk(qi, carry, diag):
            dk, dv = carry
            qs = pl.ds(pl.multiple_of(qi * t, t), t)
            q, d_o_blk = q_ref[qs, :], do_ref[qs, :]
            st = _mxdot(k, q, 1, 1) * ATT_SCALE
            if diag:
                keys = lax.broadcasted_iota(jnp.int32, (t, t), 0)
                st = jnp.where(keys <= lax.broadcasted_iota(jnp.int32, (t, t), 1), st, -jnp.inf)
            pt = jnp.exp(st - lse_ref[0, 0, qi][:1, :])
            dst = pt * (_mxdot(vv, d_o_blk, 1, 1) - dl_ref[0, 0, qi][:1, :]) * ATT_SCALE
            dq_ref[qs, :] += _mxdot(dst, k, 0, 0)
            return dk + _mxdot(dst, q, 1, 0), dv + _mxdot(pt, d_o_blk, 1, 0)

        carry = blk(kj, (jnp.zeros((t, 2 * LANE), F32), jnp.zeros((t, LANE), F32)), True)
        dk, dv = lax.fori_loop(kj + 1, nb, lambda qi, c: blk(qi, c, False), carry)
        dk_ref[...] = dk
        dv_ref[...] = dv.astype(dv_ref.dtype)

    rows = pl.BlockSpec((1, 1, nb, 8, t), lambda b, h, j: (b, h, 0, 0, 0))
    return pl.pallas_call(
        body, name="attn_bwd", grid=(geo.bsz, MLA_HEADS, nb),
        in_specs=[pl.BlockSpec((lp, 2 * LANE), lambda b, h, j: (b, h)),
                  pl.BlockSpec((t, 2 * LANE), lambda b, h, j: (b * nb + j, h)),
                  pl.BlockSpec((t, LANE), lambda b, h, j: (b * nb + j, h)),
                  pl.BlockSpec((lp, LANE), lambda b, h, j: (b, h)), rows, rows],
        out_specs=[pl.BlockSpec((lp, 2 * LANE), lambda b, h, j: (b, h)),
                   pl.BlockSpec((t, 2 * LANE), lambda b, h, j: (b * nb + j, h)),
                   pl.BlockSpec((t, LANE), lambda b, h, j: (b * nb + j, h))],
        out_shape=[jax.ShapeDtypeStruct((geo.nrows, 2 * geo.hq), F32), jax.ShapeDtypeStruct((geo.nrows, 2 * geo.hq), F32),
                   jax.ShapeDtypeStruct((geo.nrows, geo.hq), MXU_DTYPE)],
        compiler_params=_cparams(("parallel", "parallel", "arbitrary")))(qc, kc, v, d_o, lse, delta)


def _rope(x, cos, sin):
    return x * cos + pltpu.roll(x, LANE // 2, axis=1) * sin


def _rope_t(dx, cos, sin):
    return dx * cos + pltpu.roll(dx * sin, LANE // 2, axis=1)


def _layer_fwd(geo, h, w, tab):
    nr, tr, trw = geo.nrows, geo.tr, geo.tr_wide
    tb = geo.lp // tr
    rw = functools.partial(_rowwise, nrows=nr)
    s = {"h": h}
    (s["u"],) = rw("rms_mix", lambda x, g: (_rms(x, g),), tr=tr, rows=[(h, D_MODEL, 0)],
                   vecs=[(w["norm_mix_w"], D_MODEL, 0)], outs=[(D_MODEL, D_MODEL, MXU_DTYPE)])
    proj = s["proj"] = _mm("mm_in", s["u"], w["w_in_p"])
    xc = s["xc"] = _conv_fwd(geo, proj, w["conv_w"], w["conv_b"])
    s["y_ssd"], s["s_prev"] = _ssd_fwd(geo, xc, proj, w["dt_bias"], w["a_log"])
    gw = SSM_D_INNER // SSM_GROUPS

    def gate_norm(y, x, z, dsk, nw):
        return (_rms((y + x * dsk) * _silu(z), nw),)

    (s["y_ssm"],) = rw("ssm_gate_norm", gate_norm, tr=tr, ncb=SSM_GROUPS,
                       rows=[(s["y_ssd"], gw, 0), (xc, gw, 0), (proj, gw, geo.col["z"][0] // gw)],
                       vecs=[(w["d_skip_full"], gw, 0), (w["ssm_norm_w"], gw, 0)], outs=[(SSM_D_INNER, gw, MXU_DTYPE)])
    (s["cq_n"],) = rw("rms_q", lambda x, g: (_rms(x, g),), tr=tr, rows=[(proj, MLA_Q_LORA, geo.cb("c_q"))],
                      vecs=[(w["q_norm_w"], MLA_Q_LORA, 0)], outs=[(MLA_Q_LORA, MLA_Q_LORA, MXU_DTYPE)])
    (s["ckv_n"],) = rw("rms_kv", lambda x, g: (_rms(x, g),), tr=tr, rows=[(proj, MLA_KV_LORA, geo.cb("c_kv"))],
                       vecs=[(w["kv_norm_w"], MLA_KV_LORA, 0)], outs=[(MLA_KV_LORA, MLA_KV_LORA, MXU_DTYPE)])
    qn = _mm("mm_qn", s["cq_n"], w["w_qn"])
    qp_raw = _mm("mm_qp", s["cq_n"], w["w_qp"])
    kn = _mm("mm_kn", s["ckv_n"], w["w_k"])
    s["v"] = _mm("mm_v", s["ckv_n"], w["w_v"], out_dtype=MXU_DTYPE)
    bias_lane = lambda: lax.broadcasted_iota(jnp.int32, (1, LANE), 1) == BIAS_LANE

    def q_cat(x, xp, c, sn):
        return (jnp.concatenate([x, jnp.where(bias_lane(), 1.0, _rope(xp, c, sn))], axis=1),)

    def k_cat(x, xp, c, sn, valid):
        return (jnp.concatenate([x, jnp.where(bias_lane(), KEY_OFF * (1.0 - valid), _rope(xp, c, sn))], axis=1),)

    rope_tabs = [(tab["cos"], LANE, 0), (tab["sin"], LANE, 0)]
    (s["qc"],) = rw("rope_q", q_cat, tr=tr, ncb=MLA_HEADS, rows=[(qn, LANE, 0), (qp_raw, LANE, 0)], tabs=rope_tabs,
                    outs=[(2 * geo.hq, 2 * LANE, MXU_DTYPE)], tab_blocks=tb)
    (s["kc"],) = rw("rope_k", k_cat, tr=tr, ncb=MLA_HEADS, rows=[(kn, LANE, 0)], fixed=[(proj, LANE, geo.cb("k_rope"))],
                    tabs=rope_tabs + [(tab["valid"], 1, 0)], outs=[(2 * geo.hq, 2 * LANE, MXU_DTYPE)], tab_blocks=tb)
    s["o"], s["lse"] = _attn_fwd2(geo, s["qc"], s["kc"], s["v"])
    s["ys_p"] = _mm("mm_bs", s["y_ssm"], w["w_branch_ssm"])
    s["ym_p"] = _mm("mm_bm", s["o"], w["w_branch_mla"])

    def gate(gs, gm, ys, ym):
        return (_sigmoid(gs) * ys + _sigmoid(gm) * ym,)

    (s["mixed"],) = rw("gate", gate, tr=tr, rows=[(proj, D_MODEL, geo.cb("g_ssm")), (proj, D_MODEL, geo.cb("g_mla")),
                                                  (s["ys_p"], D_MODEL, 0), (s["ym_p"], D_MODEL, 0)],
                       outs=[(D_MODEL, D_MODEL, MXU_DTYPE)])
    s["h2"] = _mm("mm_out", s["mixed"], w["w_out"], add=h)
    (s["vn"],) = rw("rms_mlp", lambda x, g: (_rms(x, g),), tr=tr, rows=[(s["h2"], D_MODEL, 0)],
                    vecs=[(w["norm_mlp_w"], D_MODEL, 0)], outs=[(D_MODEL, D_MODEL, MXU_DTYPE)])
    s["up"] = _mm("mm_up", s["vn"], w["w_mlp_up"])
    (s["act"],) = rw("relu2", lambda x: (jnp.square(jnp.maximum(x, 0.0)),), tr=trw, rows=[(s["up"], D_FF, 0)],
                     outs=[(D_FF, D_FF, MXU_DTYPE)])
    return _mm("mm_down", s["act"], w["w_mlp_down"], add=s["h2"]), s


def _layer_bwd(geo, dh3, s, w, tab):
    nr, tr, trw = geo.nrows, geo.tr, geo.tr_wide
    tb = geo.lp // tr
    rw = functools.partial(_rowwise, nrows=nr)
    g = {}
    proj = s["proj"]

    def rms_bwd(x, dy, res, gw):
        _, vjp = jax.vjp(_rms, x.astype(F32), gw)
        dx, dgw = vjp(dy.astype(F32))
        return dx + res, dgw

    def rms_bwd_nores(x, dy, gw):
        _, vjp = jax.vjp(_rms, x.astype(F32), gw)
        return vjp(dy.astype(F32))

    dact = _mm("mm_down_t", dh3, w["w_mlp_down"], tb=True)
    g["w_mlp_down"] = _mm("mm_down_g", s["act"], dh3, ta=True)
    (dup,) = rw("relu2_bwd", lambda d, x: (d * 2.0 * jnp.maximum(x, 0.0),), tr=trw,
                rows=[(dact, D_FF, 0), (s["up"], D_FF, 0)], outs=[(D_FF, D_FF, MXU_DTYPE)])
    g["w_mlp_up"] = _mm("mm_up_g", s["vn"], dup, ta=True)
    dvn = _mm("mm_up_t", dup, w["w_mlp_up"], tb=True)
    dh2, g["norm_mlp_w"] = rw("rms_mlp_bwd", rms_bwd, tr=tr,
                              rows=[(s["h2"], D_MODEL, 0), (dvn, D_MODEL, 0), (dh3, D_MODEL, 0)],
                              vecs=[(w["norm_mlp_w"], D_MODEL, 0)], outs=[(D_MODEL, D_MODEL, F32)],
                              reds=[(D_MODEL, D_MODEL)])
    dmixed = _mm("mm_out_t", dh2, w["w_out"], tb=True)
    g["w_out"] = _mm("mm_out_g", s["mixed"], dh2, ta=True)

    def gate_bwd(gs, gm, ys, ym, dm):
        f = lambda a, b, c, d: _sigmoid(a) * c + _sigmoid(b) * d
        _, vjp = jax.vjp(f, gs, gm, ys, ym)
        dgs, dgm, dys, dym = vjp(dm)
        return dys, dym, dgs, dgm

    dys_p, dym_p, dg_ssm, dg_mla = rw(
        "gate_bwd", gate_bwd, tr=tr,
        rows=[(proj, D_MODEL, geo.cb("g_ssm")), (proj, D_MODEL, geo.cb("g_mla")), (s["ys_p"], D_MODEL, 0),
              (s["ym_p"], D_MODEL, 0), (dmixed, D_MODEL, 0)], outs=[(D_MODEL, D_MODEL, MXU_DTYPE)] * 4)
    g["w_branch_ssm"] = _mm("mm_bs_g", s["y_ssm"], dys_p, ta=True)
    dy_ssm = _mm("mm_bs_t", dys_p, w["w_branch_ssm"], tb=True)
    g["w_branch_mla"] = _mm("mm_bm_g", s["o"], dym_p, ta=True)
    d_o = _mm("mm_bm_t", dym_p, w["w_branch_mla"], tb=True)
    delta = _attn_delta(geo, d_o, s["o"])
    dqc, dkc, dv = _attn_bwd2(geo, s["qc"], s["kc"], s["v"], d_o, s["lse"], delta)
    rope_tabs = [(tab["cos"], LANE, 0), (tab["sin"], LANE, 0)]
    dqn, dqp_raw = rw("rope_q_bwd", lambda x, c, sn: (x[:, :LANE], _rope_t(x[:, LANE:], c, sn)), tr=tr, ncb=MLA_HEADS,
                      rows=[(dqc, 2 * LANE, 0)], tabs=rope_tabs, outs=[(geo.hq, LANE, MXU_DTYPE)] * 2, tab_blocks=tb)

    def rope_k_bwd(x, c, sn):
        tot = x[:, LANE:2 * LANE]
        for hd in range(1, MLA_HEADS):
            tot = tot + x[:, (2 * hd + 1) * LANE:(2 * hd + 2) * LANE]
        dkn_ = jnp.concatenate([x[:, 2 * hd * LANE:(2 * hd + 1) * LANE] for hd in range(MLA_HEADS)], axis=1)
        return dkn_, _rope_t(tot, c, sn)

    dkn, dk_rope = rw("rope_k_bwd", rope_k_bwd, tr=geo.tr_wide, rows=[(dkc, 2 * geo.hq, 0)], tabs=rope_tabs,
                      outs=[(geo.hq, geo.hq, MXU_DTYPE), (LANE, LANE, MXU_DTYPE)], tab_blocks=geo.lp // geo.tr_wide)
    g["w_qn"] = _mm("mm_qn_g", s["cq_n"], dqn, ta=True)
    g["w_qp"] = _mm("mm_qp_g", s["cq_n"], dqp_raw, ta=True)
    dcq_n = _mm("mm_qp_t", dqp_raw, w["w_qp"], tb=True, add=_mm("mm_qn_t", dqn, w["w_qn"], tb=True))
    g["w_k"] = _mm("mm_kn_g", s["ckv_n"], dkn, ta=True)
    g["w_v"] = _mm("mm_v_g", s["ckv_n"], dv, ta=True)
    dckv_n = _mm("mm_v_t", dv, w["w_v"], tb=True, add=_mm("mm_kn_t", dkn, w["w_k"], tb=True))
    dc_q, g["q_norm_w"] = rw("rms_q_bwd", rms_bwd_nores, tr=tr,
                             rows=[(proj, MLA_Q_LORA, geo.cb("c_q")), (dcq_n, MLA_Q_LORA, 0)],
                             vecs=[(w["q_norm_w"], MLA_Q_LORA, 0)], outs=[(MLA_Q_LORA, MLA_Q_LORA, MXU_DTYPE)],
                             reds=[(MLA_Q_LORA, MLA_Q_LORA)])
    dc_kv, g["kv_norm_w"] = rw("rms_kv_bwd", rms_bwd_nores, tr=tr,
                               rows=[(proj, MLA_KV_LORA, geo.cb("c_kv")), (dckv_n, MLA_KV_LORA, 0)],
                               vecs=[(w["kv_norm_w"], MLA_KV_LORA, 0)], outs=[(MLA_KV_LORA, MLA_KV_LORA, MXU_DTYPE)],
                               reds=[(MLA_KV_LORA, MLA_KV_LORA)])
    gw_ = SSM_D_INNER // SSM_GROUPS

    def gate_norm_bwd(y, x, z, dy, dsk, nw):
        f = lambda y_, x_, z_, dsk_, nw_: _rms((y_ + x_ * dsk_) * _silu(z_), nw_)
        _, vjp = jax.vjp(f, y, x, z, dsk, nw)
        dy_, dx_, dz_, ddsk, dnw = vjp(dy)
        return dy_, dx_, dz_, ddsk, dnw

    dy_ssd, dxs_skip, dz, g["d_skip_full"], g["ssm_norm_w"] = rw(
        "ssm_gate_norm_bwd", gate_norm_bwd, tr=tr, ncb=SSM_GROUPS,
        rows=[(s["y_ssd"], gw_, 0), (s["xc"], gw_, 0), (proj, gw_, geo.col["z"][0] // gw_), (dy_ssm, gw_, 0)],
        vecs=[(w["d_skip_full"], gw_, 0), (w["ssm_norm_w"], gw_, 0)],
        outs=[(SSM_D_INNER, gw_, F32), (SSM_D_INNER, gw_, F32), (SSM_D_INNER, gw_, MXU_DTYPE)],
        reds=[(SSM_D_INNER, gw_), (SSM_D_INNER, gw_)])
    dxc, ddt, g["dt_bias"], g["a_log"] = _ssd_bwd(geo, s["xc"], proj, w["dt_bias"], w["a_log"], s["s_prev"],
                                                   dy_ssd, dxs_skip)
    dxbc, g["conv_w"], g["conv_b"] = _conv_bwd(geo, proj, w["conv_w"], w["conv_b"], dxc)
    di, gn = SSM_D_INNER, geo.gn
    dproj = jnp.concatenate([dz, dxbc[:, :di], dg_ssm, dg_mla, dxbc[:, di:di + gn], dxbc[:, di + gn:], dc_q, dc_kv,
                             ddt, dk_rope], axis=-1)
    g["w_in_p"] = _mm("mm_in_g", s["u"], dproj, ta=True)
    du = _mm("mm_in_t", dproj, w["w_in_p"], tb=True)
    dh, g["norm_mix_w"] = rw("rms_mix_bwd", rms_bwd, tr=tr,
                             rows=[(s["h"], D_MODEL, 0), (du, D_MODEL, 0), (dh2, D_MODEL, 0)],
                             vecs=[(w["norm_mix_w"], D_MODEL, 0)], outs=[(D_MODEL, D_MODEL, F32)],
                             reds=[(D_MODEL, D_MODEL)])
    return dh, g


def _loss_bwd(geo, h, fw, target, tab):
    tr = geo.tr

    def fn(x, tgt, gw, tok):
        def lossf(x_, gw_):
            err = jnp.square(_rms(x_, gw_) - tgt)
            return 0.5 * jnp.sum(tok * jnp.mean(err, axis=-1, keepdims=True), axis=0, keepdims=True)

        val, vjp = jax.vjp(lossf, x, gw)
        dx, dgw = vjp(jnp.ones((1, 1), F32))
        return dx, jnp.broadcast_to(val, (1, LANE)), dgw

    return _rowwise("loss", fn, nrows=geo.nrows, tr=tr, rows=[(h, D_MODEL, 0), (target, D_MODEL, 0)],
                    vecs=[(fw, D_MODEL, 0)], tabs=[(tab["token"], 1, 0)], outs=[(D_MODEL, D_MODEL, F32)],
                    reds=[(LANE, LANE), (D_MODEL, D_MODEL)], tab_blocks=geo.lp // tr)


def kernel(x, meta_tokens, norm_mix_w, w_in, conv_w, conv_b, dt_bias, a_log, d_skip, ssm_norm_w, q_norm_w, kv_norm_w, w_uq, w_ukv, w_branch_ssm, w_branch_mla, w_out, norm_mlp_w, w_mlp_up, w_mlp_down, final_norm_w, loss_target, m_meta_tokens, m_norm_mix_w, m_w_in, m_conv_w, m_conv_b, m_dt_bias, m_a_log, m_d_skip, m_ssm_norm_w, m_q_norm_w, m_kv_norm_w, m_w_uq, m_w_ukv, m_w_branch_ssm, m_w_branch_mla, m_w_out, m_norm_mlp_w, m_w_mlp_up, m_w_mlp_down, m_final_norm_w, v_meta_tokens, v_norm_mix_w, v_w_in, v_conv_w, v_conv_b, v_dt_bias, v_a_log, v_d_skip, v_ssm_norm_w, v_q_norm_w, v_kv_norm_w, v_w_uq, v_w_ukv, v_w_branch_ssm, v_w_branch_mla, v_w_out, v_norm_mlp_w, v_w_mlp_up, v_w_mlp_down, v_final_norm_w):
    args = dict(locals())
    wts = {n: args[n] for n in WEIGHTS}
    mom = {n: args["m_" + n] for n in WEIGHTS}
    var = {n: args["v_" + n] for n in WEIGHTS}
    bsz, seq, _ = x.shape
    depth = w_in.shape[0]
    geo = _Geo(bsz, seq)
    tab = _tables(geo)

    big_names = [n for n, _ in BIG]
    sh_names = big_names + [n for n, _ in SHARDED_F32]
    kinds = dict(BIG + SHARDED_F32)
    shard3 = lambda a: a.reshape((1,) + a.shape) if a.ndim == 2 else a
    ins, outs, items, forms = [], [], [], {}
    for n in sh_names:
        sh = shard3(wts[n])
        sh = sh.astype(MXU_DTYPE) if n in big_names else sh
        shape, dst, forms[n] = _gather_plan(sh, kinds[n])
        items.append((len(ins), len(outs), _whole, dst))
        ins.append(sh)
        outs.append(jax.ShapeDtypeStruct(shape, sh.dtype))
    full = {}
    for n, g in zip(sh_names, _exchange("gather_w", ins, outs, items)):
        if forms[n] == "row":
            full[n] = g.reshape(g.shape[0], g.shape[1] * g.shape[2], g.shape[3])
        elif forms[n] == "col" or n == "w_in":
            full[n] = g
        else:
            full[n] = _unshard(g, "col")
    w_in_p = _w_in_assemble(geo, full.pop("w_in"))
    full["meta_tokens"] = full["meta_tokens"][0]
    layer_names = [n for n in sh_names if n not in ("w_in", "meta_tokens")] + [
        "norm_mix_w", "conv_b", "dt_bias", "a_log", "d_skip", "ssm_norm_w", "q_norm_w", "kv_norm_w", "norm_mlp_w"]
    layers = []
    for i in range(depth):
        wl = {n: (full[n] if n in full else wts[n])[i] for n in layer_names}
        wl["w_in_p"] = w_in_p[i]
        layers.append(_prep_layer(geo, wl))

    meta = jnp.broadcast_to(full["meta_tokens"][None], (bsz, N_META, D_MODEL))
    h = jnp.concatenate([jnp.zeros((bsz, geo.pad, D_MODEL), F32), meta, x], axis=1).reshape(geo.nrows, D_MODEL)
    target = jnp.concatenate([jnp.zeros((bsz, geo.pad + N_META, D_MODEL), F32), loss_target], axis=1)
    target = target.reshape(geo.nrows, D_MODEL)
    saved = []
    for i in range(depth):
        h, s = _layer_fwd(geo, h, layers[i], tab)
        saved.append(s)
    dh, loss_part, g_final = _loss_bwd(geo, h, final_norm_w.reshape(1, -1), target, tab)

    grads = [None] * depth
    for i in reversed(range(depth)):
        dh, gl = _layer_bwd(geo, dh, saved[i], layers[i], tab)
        grads[i] = _unprep_grads(geo, gl)
    dh = dh.reshape(bsz, geo.lp, D_MODEL)
    grad_x = dh[:, geo.pad + N_META:]
    g_meta = jnp.sum(dh[:, geo.pad:geo.pad + N_META], axis=0)

    ins, outs, items, slot = [], [], [], {}
    for n in sh_names:
        slot[n] = len(outs)
        outs.append(jax.ShapeDtypeStruct((N_DEV,) + shard3(wts[n]).shape, F32))

    def push(n, layer, arr):
        a, b = shard3(wts[n]).shape[1:]
        if n == "w_in":
            arr, src = _w_in_split(geo, arr, b), _entry
        elif kinds[n] == "row":
            src = lambda ref, p: ref.at[pl.ds(pl.multiple_of(p * a, a), a)]
        elif b % LANE == 0:
            src = lambda ref, p: ref.at[:, pl.ds(pl.multiple_of(p * b, b), b)]
        else:
            arr, src = _shard(arr, "col"), _entry
        items.append((len(ins), slot[n], src, lambda ref, p: ref.at[p, layer]))
        ins.append(arr)

    for i in range(depth):
        push("w_in", i, grads[i]["w_in_p"])
        for n in sh_names:
            if n not in ("w_in", "meta_tokens"):
                push(n, i, grads[i][n])
    push("meta_tokens", 0, g_meta)
    landed = _exchange("scatter_g", ins, outs, items)
    res = {}
    for n in sh_names:
        r = _adamw_nat("adamw_" + n, landed[slot[n]], shard3(wts[n]), shard3(mom[n]), shard3(var[n]))
        res[n] = [a.reshape(wts[n].shape) for a in r]
    g_small = {n: jnp.stack([grads[i][n] for i in range(depth)]) for n in SMALL if n != "final_norm_w"}
    g_small["final_norm_w"] = g_final.reshape(-1)
    zero = jnp.zeros((1,), F32)
    pk = lambda d, last: _pack([d[n] for n in SMALL] + [last], F32, row_mult=8)
    packed = pk(g_small, loss_part[0, :1])
    (parts,) = _exchange("gather_g", [packed], [jax.ShapeDtypeStruct((N_DEV,) + packed.shape, F32)],
                         [(0, 0, _whole, _entry)])
    res_sm = _adamw("adamw_small", parts, pk(wts, zero), pk(mom, zero), pk(var, zero))
    res_sm = [_unpack(r, [wts[n].shape for n in SMALL] + [(1,)]) for r in res_sm]
    loss = res_sm[0][-1][0]

    out = [loss, grad_x]
    for k in range(4):
        named = {n: res[n][k] for n in sh_names}
        named.update(zip(SMALL, res_sm[k]))
        out += [named[n] for n in WEIGHTS]
    return tuple(out)
```

```python
import functools

import numpy as np
import jax
import jax.numpy as jnp
from jax import lax
from jax.experimental import pallas as pl
from jax.experimental.pallas import tpu as pltpu

F32 = jnp.float32
MXU_DTYPE = jnp.bfloat16

D_MODEL = 1024
N_META = 16
EPS = 1e-6
SSM_D_INNER = 2048
SSM_HEAD_DIM = 64
SSM_GROUPS = 4
SSM_STATE = 128
SSM_CONV = 4
SSM_CHUNK = 128
MLA_HEADS = 8
MLA_Q_LORA = 512
MLA_KV_LORA = 256
MLA_NOPE = 128
MLA_ROPE = 64
MLA_V = 128
ROPE_THETA = 10000.0
D_FF = 4096
ADAM_LR = 0.001
ADAM_B1 = 0.9
ADAM_B2 = 0.999
ADAM_EPS = 1e-08
ADAM_WD = 0.01
ADAM_STEP = 10

N_DEV = 8
ATT_BLK = 256
LANE = 128
PACK_W = 1024
VMEM_LIMIT = 56 * 1024 * 1024
MESH_ID = pl.DeviceIdType.MESH

BIG = (("w_in", "col"), ("w_uq", "col"), ("w_ukv", "col"), ("w_branch_ssm", "row"), ("w_branch_mla", "row"),
       ("w_out", "row"), ("w_mlp_up", "col"), ("w_mlp_down", "row"))
SHARDED_F32 = (("conv_w", "col"), ("meta_tokens", "col"))
SMALL = ("norm_mix_w", "conv_b", "dt_bias", "a_log", "d_skip", "ssm_norm_w", "q_norm_w", "kv_norm_w",
         "norm_mlp_w", "final_norm_w")
WEIGHTS = ("meta_tokens", "norm_mix_w", "w_in", "conv_w", "conv_b", "dt_bias", "a_log", "d_skip", "ssm_norm_w",
           "q_norm_w", "kv_norm_w", "w_uq", "w_ukv", "w_branch_ssm", "w_branch_mla", "w_out", "norm_mlp_w",
           "w_mlp_up", "w_mlp_down", "final_norm_w")


def _cparams(sem=None):
    return pltpu.CompilerParams(dimension_semantics=sem, vmem_limit_bytes=VMEM_LIMIT)


def _pick(n, cands):
    for c in cands:
        if n % c == 0:
            return c
    return n


def _sigmoid(x):
    return 1.0 / (1.0 + jnp.exp(-x))


def _silu(x):
    return x * _sigmoid(x)


def _softplus(x):
    return jnp.maximum(x, 0.0) + jnp.log1p(jnp.exp(-jnp.abs(x)))


def _rms(x, w):
    return x * lax.rsqrt(jnp.mean(x * x, axis=-1, keepdims=True) + EPS) * w


def _dot(a, b, ca, cb, precision=None):
    return lax.dot_general(a, b, (((ca,), (cb,)), ((), ())), preferred_element_type=F32, precision=precision)


def _mxdot(a, b, ca, cb):
    return _dot(a.astype(MXU_DTYPE), b.astype(MXU_DTYPE), ca, cb)


def _mm(name, a, b, *, ta=False, tb=False, add=None, out_dtype=F32):
    (kdim, m) = a.shape if ta else a.shape[::-1]
    (n, k2) = b.shape if tb else b.shape[::-1]
    assert kdim == k2, (name, a.shape, b.shape)
    tm = _pick(m, (1152, 1024, 768, 512, 384, 256, 128))
    tn = _pick(n, (1024, 512, 384, 256, 128))
    tk = _pick(kdim, (1152, 1024, 768, 512, 384, 256, 128))
    nk = kdim // tk
    a_spec = pl.BlockSpec((tk, tm), lambda i, j, k: (k, i)) if ta else pl.BlockSpec((tm, tk), lambda i, j, k: (i, k))
    b_spec = pl.BlockSpec((tn, tk), lambda i, j, k: (j, k)) if tb else pl.BlockSpec((tk, tn), lambda i, j, k: (k, j))
    o_spec = pl.BlockSpec((tm, tn), lambda i, j, k: (i, j))
    ca, cb = (0 if ta else 1), (1 if tb else 0)

    def body(*refs):
        a_ref, b_ref = refs[:2]
        o_ref, acc = refs[-2:]
        k = pl.program_id(2)

        @pl.when(k == 0)
        def _():
            acc[...] = jnp.zeros_like(acc)

        acc[...] += _mxdot(a_ref[...], b_ref[...], ca, cb)

        @pl.when(k == nk - 1)
        def _():
            r = acc[...]
            if add is not None:
                r = r + refs[2][...].astype(F32)
            o_ref[...] = r.astype(out_dtype)

    in_specs, args = [a_spec, b_spec], [a, b]
    if add is not None:
        in_specs.append(o_spec)
        args.append(add)
    return pl.pallas_call(
        body, name=name, grid=(m // tm, n // tn, nk), in_specs=in_specs, out_specs=o_spec,
        out_shape=jax.ShapeDtypeStruct((m, n), out_dtype), scratch_shapes=[pltpu.VMEM((tm, tn), F32)],
        compiler_params=_cparams(("parallel", "parallel", "arbitrary")))(*args)


def _rowwise(name, fn, *, nrows, tr, ncb=1, rows=(), fixed=(), vecs=(), tabs=(), outs=(), reds=(), tab_blocks=1):
    in_specs, args = [], []
    for arr, w, c0 in rows:
        in_specs.append(pl.BlockSpec((tr, w), lambda g, i, c0=c0: (i, c0 + g)))
        args.append(arr)
    for arr, w, c0 in fixed:
        in_specs.append(pl.BlockSpec((tr, w), lambda g, i, c0=c0: (i, c0)))
        args.append(arr)
    for arr, w, c0 in vecs:
        in_specs.append(pl.BlockSpec((1, w), lambda g, i, c0=c0: (0, c0 + g)))
        args.append(arr)
    for arr, w, c0 in tabs:
        in_specs.append(pl.BlockSpec((tr, w), lambda g, i, c0=c0: (i % tab_blocks, c0)))
        args.append(arr)
    out_shape = [jax.ShapeDtypeStruct((nrows, wt), dt) for wt, w, dt in outs]
    out_shape += [jax.ShapeDtypeStruct((1, wt), F32) for wt, w in reds]
    out_specs = [pl.BlockSpec((tr, w), lambda g, i: (i, g)) for wt, w, dt in outs]
    out_specs += [pl.BlockSpec((1, w), lambda g, i: (0, g)) for wt, w in reds]
    n_in, n_out = len(args), len(outs)

    def body(*refs):
        res = fn(*[r[...] for r in refs[:n_in]])
        for o_ref, val in zip(refs[n_in:n_in + n_out], res[:n_out]):
            o_ref[...] = val.astype(o_ref.dtype)
        i = pl.program_id(1)
        for d_ref, val in zip(refs[n_in + n_out:], res[n_out:]):
            @pl.when(i == 0)
            def _(d_ref=d_ref, val=val):
                d_ref[...] = val

            @pl.when(i > 0)
            def _(d_ref=d_ref, val=val):
                d_ref[...] += val

    res = pl.pallas_call(
        body, name=name, grid=(ncb, nrows // tr), in_specs=in_specs, out_specs=out_specs, out_shape=out_shape,
        compiler_params=_cparams(("parallel", "arbitrary")))(*args)
    return res


def _peer(k):
    x, y, c = lax.axis_index("x"), lax.axis_index("y"), lax.axis_index("c")
    px = jnp.where((k >> 2) & 1, 1 - x, x)
    py = jnp.where((k >> 1) & 1, 1 - y, y)
    pc = jnp.where(k & 1, 1 - c, c)
    return (px, py, pc), 4 * px + 2 * py + pc


def _my_index():
    return 4 * lax.axis_index("x") + 2 * lax.axis_index("y") + lax.axis_index("c")


def _exchange(name, ins, out_shapes, items):
    n_in, n_out, n_it = len(ins), len(out_shapes), len(items)

    def body(*refs):
        x, o = refs[:n_in], refs[n_in:n_in + n_out]
        send_sems, recv_sems, local_sems = refs[n_in + n_out:]
        me = _my_index()
        local, sends = [], []
        for t, (ii, io, src, dst) in enumerate(items):
            cp = pltpu.make_async_copy(src(x[ii], me), dst(o[io], me), local_sems.at[t])
            cp.start()
            local.append(cp)
        for k in range(1, N_DEV):
            dev, idx = _peer(k)
            for t, (ii, io, src, dst) in enumerate(items):
                s = (k - 1) * n_it + t
                cp = pltpu.make_async_remote_copy(
                    src_ref=src(x[ii], idx), dst_ref=dst(o[io], me), send_sem=send_sems.at[s],
                    recv_sem=recv_sems.at[s], device_id=dev, device_id_type=MESH_ID)
                cp.start()
                sends.append(cp)
        for k in range(1, N_DEV):
            dev, idx = _peer(k)
            for t, (ii, io, src, dst) in enumerate(items):
                s = (k - 1) * n_it + t
                pltpu.make_async_remote_copy(
                    src_ref=src(x[ii], idx), dst_ref=dst(o[io], idx), send_sem=send_sems.at[s],
                    recv_sem=recv_sems.at[s], device_id=dev, device_id_type=MESH_ID).wait_recv()
        for cp in sends:
            cp.wait_send()
        for cp in local:
            cp.wait()

    nsem = (N_DEV - 1) * n_it
    anyspec = pl.BlockSpec(memory_space=pl.ANY)
    return pl.pallas_call(
        body, name=name, out_shape=list(out_shapes), in_specs=[anyspec] * n_in, out_specs=[anyspec] * n_out,
        scratch_shapes=[pltpu.SemaphoreType.DMA((nsem,)), pltpu.SemaphoreType.DMA((nsem,)),
                        pltpu.SemaphoreType.DMA((n_it,))],
        compiler_params=pltpu.CompilerParams(has_side_effects=True))(*ins)


def _local_fill(name, ins, out_shapes, items):
    n_in, n_out, n_it = len(ins), len(out_shapes), len(items)

    def body(*refs):
        x, o, sems = refs[:n_in], refs[n_in:n_in + n_out], refs[-1]
        me = _my_index()
        copies = [pltpu.make_async_copy(src(x[ii], me), dst(o[io], me), sems.at[t])
                  for t, (ii, io, src, dst) in enumerate(items)]
        for cp in copies:
            cp.start()
        for cp in copies:
            cp.wait()

    anyspec = pl.BlockSpec(memory_space=pl.ANY)
    return pl.pallas_call(
        body, name=name, out_shape=list(out_shapes), in_specs=[anyspec] * n_in, out_specs=[anyspec] * n_out,
        scratch_shapes=[pltpu.SemaphoreType.DMA((n_it,))])(*ins)


def _remote_copies(x, land, send_sems, recv_sems, items, receive):
    me = _my_index()
    out = []
    for k in range(1, N_DEV):
        dev, idx = _peer(k)
        for t, (ii, io, src, dst) in enumerate(items):
            s = (k - 1) * len(items) + t
            out.append(pltpu.make_async_remote_copy(
                src_ref=src(x[ii], idx), dst_ref=dst(land[io], idx if receive else me), send_sem=send_sems.at[s],
                recv_sem=recv_sems.at[s], device_id=dev, device_id_type=MESH_ID))
    return out


def _exchange_start(name, ins, landing, items):
    n_in, n_out, nsem = len(ins), len(landing), (N_DEV - 1) * len(items)

    def body(*refs):
        x, land = refs[:n_in], refs[n_in:n_in + n_out]
        send_sems, recv_sems, token = refs[n_in + n_out], refs[n_in + n_out + 1], refs[-1]
        for cp in _remote_copies(x, land, send_sems, recv_sems, items, False):
            cp.start()
        token[...] = jnp.zeros_like(token)

    hbm = pl.BlockSpec(memory_space=pltpu.HBM)
    sem = pl.BlockSpec(memory_space=pltpu.SEMAPHORE)
    arrs = [pltpu.with_memory_space_constraint(a, pltpu.HBM) for a in list(ins) + list(landing)]
    res = pl.pallas_call(
        body, name=name,
        out_shape=(pltpu.SemaphoreType.DMA((nsem,)), pltpu.SemaphoreType.DMA((nsem,)),
                   *[pltpu.HBM(a.shape, a.dtype) for a in arrs], jax.ShapeDtypeStruct((8, LANE), F32)),
        in_specs=[hbm] * (n_in + n_out),
        out_specs=(sem, sem, *[hbm] * (n_in + n_out), pl.BlockSpec(memory_space=pltpu.VMEM)),
        input_output_aliases={i: 2 + i for i in range(n_in + n_out)},
        compiler_params=pltpu.CompilerParams(has_side_effects=pltpu.SideEffectType.DATAFLOW_SIDE_EFFECTING))(*arrs)
    return res[:2], res[2:2 + n_in], res[2 + n_in:2 + n_in + n_out], res[-1]


def _exchange_wait(name, sems, ins, landing, items, after):
    n_in, n_out = len(ins), len(landing)

    def body(*refs):
        x, land = refs[:n_in], refs[n_in:n_in + n_out]
        send_sems, recv_sems = refs[n_in + n_out], refs[n_in + n_out + 1]
        for cp in _remote_copies(x, land, send_sems, recv_sems, items, True):
            cp.wait_send()
            cp.wait_recv()

    hbm = pl.BlockSpec(memory_space=pltpu.HBM)
    sem = pl.BlockSpec(memory_space=pltpu.SEMAPHORE)
    arrs = list(ins) + list(landing)
    res = pl.pallas_call(
        body, name=name, out_shape=tuple(pltpu.HBM(a.shape, a.dtype) for a in arrs),
        in_specs=[hbm] * (n_in + n_out) + [sem, sem, pl.BlockSpec(memory_space=pl.ANY)],
        out_specs=tuple([hbm] * (n_in + n_out)), input_output_aliases={i: i for i in range(n_in + n_out)},
        compiler_params=pltpu.CompilerParams(has_side_effects=pltpu.SideEffectType.DATAFLOW_SIDE_EFFECTING))(
            *arrs, *sems, after)
    return res[n_in:]


def _whole(ref, p):
    return ref


def _entry(ref, p):
    return ref.at[p]


def _gather_plan(a, b, kind):
    if kind == "col" and b % LANE == 0:
        return (a, N_DEV * b), (lambda ref, p: ref.at[:, pl.ds(pl.multiple_of(p * b, b), b)]), "col"
    return (N_DEV, a, b), _entry, ("row" if kind == "row" else "stack")


def _adamw_nat(name, parts, w, m, v):
    depth, b, c = w.shape
    assert len(parts) == depth
    tb = _pick(b, (128, 64, 32, 16, 8))
    spec = pl.BlockSpec((1, tb, c), lambda i, j: (i, j, 0))

    def body(*refs):
        p_refs = refs[:depth]
        w_ref, m_ref, v_ref, g_ref, d_ref, nm_ref, nv_ref = refs[depth:]
        for layer, p_ref in enumerate(p_refs):
            @pl.when(pl.program_id(0) == layer)
            def _(p_ref=p_ref):
                g = p_ref[0].astype(F32)
                for j in range(1, N_DEV):
                    g = g + p_ref[j].astype(F32)
                nm = ADAM_B1 * m_ref[0] + (1.0 - ADAM_B1) * g
                nv = ADAM_B2 * v_ref[0] + (1.0 - ADAM_B2) * jnp.square(g)
                m_hat = nm / (1.0 - ADAM_B1 ** ADAM_STEP)
                v_hat = nv / (1.0 - ADAM_B2 ** ADAM_STEP)
                g_ref[0] = g
                d_ref[0] = -ADAM_LR * (m_hat / (jnp.sqrt(v_hat) + ADAM_EPS) + ADAM_WD * w_ref[0])
                nm_ref[0] = nm
                nv_ref[0] = nv

    sds = jax.ShapeDtypeStruct((depth, b, c), F32)
    return pl.pallas_call(
        body, name=name, grid=(depth, b // tb),
        in_specs=[pl.BlockSpec((N_DEV, tb, c), lambda i, j: (0, j, 0))] * depth + [spec, spec, spec],
        out_specs=[spec] * 4, out_shape=[sds] * 4, compiler_params=_cparams(("parallel", "parallel")))(*parts, w, m, v)


def _adamw(name, parts, w, m, v):
    rows = w.shape[0]
    tr = _pick(rows, (256, 128, 64, 32, 16, 8))
    spec = pl.BlockSpec((tr, PACK_W), lambda i: (i, 0))

    def body(p_ref, w_ref, m_ref, v_ref, g_ref, d_ref, nm_ref, nv_ref):
        g = p_ref[0]
        for j in range(1, N_DEV):
            g = g + p_ref[j]
        nm = ADAM_B1 * m_ref[...] + (1.0 - ADAM_B1) * g
        nv = ADAM_B2 * v_ref[...] + (1.0 - ADAM_B2) * jnp.square(g)
        m_hat = nm / (1.0 - ADAM_B1 ** ADAM_STEP)
        v_hat = nv / (1.0 - ADAM_B2 ** ADAM_STEP)
        g_ref[...] = g
        d_ref[...] = -ADAM_LR * (m_hat / (jnp.sqrt(v_hat) + ADAM_EPS) + ADAM_WD * w_ref[...])
        nm_ref[...] = nm
        nv_ref[...] = nv

    sds = jax.ShapeDtypeStruct((rows, PACK_W), F32)
    return pl.pallas_call(
        body, name=name, grid=(rows // tr,),
        in_specs=[pl.BlockSpec((N_DEV, tr, PACK_W), lambda i: (0, i, 0)), spec, spec, spec],
        out_specs=[spec] * 4, out_shape=[sds] * 4, compiler_params=_cparams(("parallel",)))(parts, w, m, v)


def _pack(arrs, dtype, row_mult=16):
    flat = jnp.concatenate([a.reshape(-1).astype(dtype) for a in arrs])
    unit = row_mult * PACK_W
    total = -(-flat.shape[0] // unit) * unit
    flat = jnp.pad(flat, (0, total - flat.shape[0]))
    return flat.reshape(-1, PACK_W)


def _pack_lead(arrs, dtype, row_mult):
    flat = jnp.concatenate([a.reshape(N_DEV, -1).astype(dtype) for a in arrs], axis=1)
    unit = row_mult * PACK_W
    total = -(-flat.shape[1] // unit) * unit
    flat = jnp.pad(flat, ((0, 0), (0, total - flat.shape[1])))
    return flat.reshape(N_DEV, -1, PACK_W)


def _unpack(buf, shapes, lead=()):
    flat = buf.reshape(lead + (-1,))
    out, off = [], 0
    for s in shapes:
        n = int(np.prod(s))
        out.append(flat[..., off:off + n].reshape(lead + tuple(s)))
        off += n
    return out


def _unshard(g, kind):
    if kind == "col":
        g = jnp.moveaxis(g, 0, -2)
        return g.reshape(g.shape[:-2] + (g.shape[-2] * g.shape[-1],))
    g = jnp.moveaxis(g, 0, 1)
    return g.reshape((g.shape[0], g.shape[1] * g.shape[2]) + g.shape[3:])


def _shard(full, kind):
    if kind == "col":
        s = full.reshape(full.shape[:-1] + (N_DEV, full.shape[-1] // N_DEV))
        return jnp.moveaxis(s, -2, 0)
    s = full.reshape((full.shape[0], N_DEV, full.shape[1] // N_DEV) + full.shape[2:])
    return jnp.moveaxis(s, 1, 0)


class _Geo:
    def __init__(self, bsz, seq):
        self.bsz, self.seq = bsz, seq
        self.pad = (-(N_META + seq)) % ATT_BLK
        self.lp = self.pad + N_META + seq
        assert (self.pad + N_META) % SSM_CHUNK == 0 and self.lp % SSM_CHUNK == 0
        self.nrows = bsz * self.lp
        self.nc = self.lp // SSM_CHUNK
        self.nh = SSM_D_INNER // SSM_HEAD_DIM
        self.gn = SSM_GROUPS * SSM_STATE
        self.cd = SSM_D_INNER + 2 * self.gn
        self.hq = MLA_HEADS * LANE
        order = (("z", SSM_D_INNER), ("xs", SSM_D_INNER), ("g_ssm", D_MODEL), ("g_mla", D_MODEL), ("bm", self.gn),
                 ("cm", self.gn), ("c_q", MLA_Q_LORA), ("c_kv", MLA_KV_LORA), ("dt", LANE), ("k_rope", LANE))
        self.col, off = {}, 0
        for nm, w in order:
            assert off % w == 0, (nm, off, w)
            self.col[nm] = (off, w)
            off += w
        self.pw = off
        assert self.nh <= LANE and MLA_ROPE == 64 and MLA_NOPE == LANE and MLA_V == LANE
        self.tr = _pick(self.lp, (768, 512, 384, 256, 128))
        self.tr_wide = _pick(self.lp, (384, 256, 128))

    def cb(self, nm):
        off, w = self.col[nm]
        return off // w

    def w_in_runs(self, shard_w):
        nh, half = self.nh, MLA_ROPE // 2
        src, pieces = 0, []
        for nm, n in (("z", SSM_D_INNER), ("xs", SSM_D_INNER), ("bm", self.gn), ("cm", self.gn), ("dt", nh),
                      ("c_q", MLA_Q_LORA), ("c_kv", MLA_KV_LORA), ("k_rope", MLA_ROPE), ("g_ssm", D_MODEL),
                      ("g_mla", D_MODEL)):
            dst = self.col[nm][0]
            if nm == "k_rope":
                pieces += [(src, half, dst), (src + half, half, dst + 2 * half)]
            else:
                pieces.append((src, n, dst))
            src += n
        assert src == shard_w * N_DEV
        runs = []
        for a, n, dst in pieces:
            for j in range(N_DEV):
                lo, hi = max(a, j * shard_w), min(a + n, (j + 1) * shard_w)
                if lo < hi:
                    runs.append((j, lo - j * shard_w, hi - lo, dst + lo - a))
        return runs


def _slot(a):
    h = MLA_ROPE // 2
    z = jnp.zeros(a.shape[:-1] + (h,), a.dtype)
    return jnp.concatenate([a[..., :h], z, a[..., h:], z], axis=-1)


def _unslot(a):
    h = MLA_ROPE // 2
    return jnp.concatenate([a[..., :h], a[..., 2 * h:3 * h]], axis=-1)


def _prep_layer(geo, wl):
    nh = geo.nh
    p = {"w_in_p": wl["w_in_p"]}
    uq = wl["w_uq"].reshape(MLA_Q_LORA, MLA_HEADS, MLA_NOPE + MLA_ROPE)
    p["w_qn"] = uq[..., :MLA_NOPE].reshape(MLA_Q_LORA, geo.hq)
    p["w_qp"] = _slot(uq[..., MLA_NOPE:]).reshape(MLA_Q_LORA, geo.hq)
    ukv = wl["w_ukv"].reshape(MLA_KV_LORA, MLA_HEADS, MLA_NOPE + MLA_V)
    p["w_k"] = ukv[..., :MLA_NOPE].reshape(MLA_KV_LORA, geo.hq)
    p["w_v"] = ukv[..., MLA_NOPE:].reshape(MLA_KV_LORA, geo.hq)
    for nm in ("w_branch_ssm", "w_branch_mla", "w_out", "w_mlp_up", "w_mlp_down"):
        p[nm] = wl[nm]
    p["conv_w"] = wl["conv_w"]
    for nm in ("norm_mix_w", "conv_b", "ssm_norm_w", "q_norm_w", "kv_norm_w", "norm_mlp_w"):
        p[nm] = wl[nm].reshape(1, -1)
    p["dt_bias"] = jnp.pad(wl["dt_bias"], (0, LANE - nh)).reshape(1, LANE)
    p["a_log"] = jnp.pad(wl["a_log"], (0, LANE - nh)).reshape(1, LANE)
    p["d_skip_full"] = jnp.repeat(wl["d_skip"], SSM_HEAD_DIM).reshape(1, SSM_D_INNER)
    return p


def _unprep_grads(geo, g):
    nh = geo.nh
    qn = g["w_qn"].reshape(MLA_Q_LORA, MLA_HEADS, MLA_NOPE)
    qp = _unslot(g["w_qp"].reshape(MLA_Q_LORA, MLA_HEADS, LANE))
    w_uq = jnp.concatenate([qn, qp], axis=-1).reshape(MLA_Q_LORA, -1)
    wk = g["w_k"].reshape(MLA_KV_LORA, MLA_HEADS, MLA_NOPE)
    wv = g["w_v"].reshape(MLA_KV_LORA, MLA_HEADS, MLA_V)
    w_ukv = jnp.concatenate([wk, wv], axis=-1).reshape(MLA_KV_LORA, -1)
    out = {"w_in_p": g["w_in_p"], "w_uq": w_uq, "w_ukv": w_ukv}
    for nm in ("w_branch_ssm", "w_branch_mla", "w_out", "w_mlp_up", "w_mlp_down", "conv_w"):
        out[nm] = g[nm]
    for nm in ("norm_mix_w", "conv_b", "ssm_norm_w", "q_norm_w", "kv_norm_w", "norm_mlp_w"):
        out[nm] = g[nm].reshape(-1)
    out["dt_bias"] = g["dt_bias"].reshape(-1)[:nh]
    out["a_log"] = g["a_log"].reshape(-1)[:nh]
    out["d_skip"] = g["d_skip_full"].reshape(nh, SSM_HEAD_DIM).sum(-1)
    return out


def _tables(geo):
    pos = jnp.arange(geo.lp, dtype=F32) - geo.pad
    inv = ROPE_THETA ** (-jnp.arange(0, MLA_ROPE, 2, dtype=F32) / MLA_ROPE)
    ang = pos[:, None] * inv[None, :]
    cos, sin = jnp.cos(ang), jnp.sin(ang)
    z = jnp.zeros_like(cos)
    rows = jnp.arange(geo.lp)[:, None]
    return {"cos": jnp.concatenate([cos, z, cos, z], axis=-1), "sin": jnp.concatenate([-sin, z, sin, z], axis=-1),
            "valid": (rows >= geo.pad).astype(F32), "token": (rows >= geo.pad + N_META).astype(F32)}


def _w_in_assemble(geo, gathered):
    _, d, sw = gathered.shape
    runs = geo.w_in_runs(sw)
    tr = _pick(d, (256, 128))

    def body(x_ref, o_ref):
        o_ref[...] = jnp.zeros_like(o_ref)
        for j, s0, n, d0 in runs:
            o_ref[:, d0:d0 + n] = x_ref[j, :, s0:s0 + n]

    return pl.pallas_call(
        body, name="w_in_assemble", grid=(d // tr,), in_specs=[pl.BlockSpec((N_DEV, tr, sw), lambda i: (0, i, 0))],
        out_specs=pl.BlockSpec((tr, geo.pw), lambda i: (i, 0)),
        out_shape=jax.ShapeDtypeStruct((d, geo.pw), gathered.dtype), compiler_params=_cparams(("parallel",)))(gathered)


def _w_in_split(geo, g_padded, sw):
    d = g_padded.shape[0]
    runs = geo.w_in_runs(sw)
    tr = _pick(d, (128,))

    def body(x_ref, o_ref):
        for j, s0, n, d0 in runs:
            o_ref[j, :, s0:s0 + n] = x_ref[:, d0:d0 + n]

    return pl.pallas_call(
        body, name="w_in_split", grid=(d // tr,), in_specs=[pl.BlockSpec((tr, geo.pw), lambda i: (i, 0))],
        out_specs=pl.BlockSpec((N_DEV, tr, sw), lambda i: (0, i, 0)),
        out_shape=jax.ShapeDtypeStruct((N_DEV, d, sw), g_padded.dtype),
        compiler_params=_cparams(("parallel",)))(g_padded)


def _conv_cols(geo, cbw):
    nx = SSM_D_INNER // cbw
    x0, b0 = geo.col["xs"][0] // cbw, geo.col["bm"][0] // cbw
    assert geo.col["cm"][0] == geo.col["bm"][0] + geo.gn
    return lambda j: jnp.where(j < nx, x0 + j, b0 + j - nx)


def _conv_pre(x, w_ref, b_ref):
    acc = b_ref[...] + x * w_ref[SSM_CONV - 1:SSM_CONV, :]
    for k in range(SSM_CONV - 1):
        acc = acc + pltpu.roll(x, SSM_CONV - 1 - k, axis=0) * w_ref[k:k + 1, :]
    return acc


def _conv_fwd(geo, proj, conv_w, conv_b):
    cbw = 256
    colmap = _conv_cols(geo, cbw)
    lp, pad = geo.lp, geo.pad

    def body(x_ref, w_ref, b_ref, o_ref):
        valid = (lax.broadcasted_iota(jnp.int32, (lp, 1), 0) >= pad).astype(F32)
        o_ref[...] = _silu(_conv_pre(x_ref[...], w_ref, b_ref)) * valid

    return pl.pallas_call(
        body, name="conv_fwd", grid=(geo.bsz, geo.cd // cbw),
        in_specs=[pl.BlockSpec((lp, cbw), lambda b, j: (b, colmap(j))),
                  pl.BlockSpec((SSM_CONV, cbw), lambda b, j: (0, j)), pl.BlockSpec((1, cbw), lambda b, j: (0, j))],
        out_specs=pl.BlockSpec((lp, cbw), lambda b, j: (b, j)),
        out_shape=jax.ShapeDtypeStruct((geo.nrows, geo.cd), F32),
        compiler_params=_cparams(("parallel", "parallel")))(proj, conv_w, conv_b)


def _conv_bwd(geo, proj, conv_w, conv_b, dxc):
    cbw = 256
    colmap = _conv_cols(geo, cbw)
    lp, pad = geo.lp, geo.pad

    def body(x_ref, w_ref, b_ref, dy_ref, dx_ref, gw_ref, gb_ref):
        b = pl.program_id(1)
        valid = (lax.broadcasted_iota(jnp.int32, (lp, 1), 0) >= pad).astype(F32)
        x = x_ref[...]
        pre = _conv_pre(x, w_ref, b_ref)
        sig = _sigmoid(pre)
        dpre = dy_ref[...] * (sig * (1.0 + pre * (1.0 - sig))) * valid
        dx = dpre * w_ref[SSM_CONV - 1:SSM_CONV, :]
        gws = [jnp.sum(dpre * x, axis=0, keepdims=True)]
        for k in range(SSM_CONV - 2, -1, -1):
            s = SSM_CONV - 1 - k
            dx = dx + pltpu.roll(dpre, lp - s, axis=0) * w_ref[k:k + 1, :]
            gws.insert(0, jnp.sum(dpre * pltpu.roll(x, s, axis=0), axis=0, keepdims=True))
        dx_ref[...] = (dx * valid).astype(dx_ref.dtype)

        @pl.when(b == 0)
        def _():
            gw_ref[...] = jnp.zeros_like(gw_ref)
            gb_ref[...] = jnp.zeros_like(gb_ref)

        for k in range(SSM_CONV):
            gw_ref[k:k + 1, :] += gws[k]
        gb_ref[...] += jnp.sum(dpre, axis=0, keepdims=True)

    return pl.pallas_call(
        body, name="conv_bwd", grid=(geo.cd // cbw, geo.bsz),
        in_specs=[pl.BlockSpec((lp, cbw), lambda j, b: (b, colmap(j))),
                  pl.BlockSpec((SSM_CONV, cbw), lambda j, b: (0, j)), pl.BlockSpec((1, cbw), lambda j, b: (0, j)),
                  pl.BlockSpec((lp, cbw), lambda j, b: (b, j))],
        out_specs=[pl.BlockSpec((lp, cbw), lambda j, b: (b, j)), pl.BlockSpec((SSM_CONV, cbw), lambda j, b: (0, j)),
                   pl.BlockSpec((1, cbw), lambda j, b: (0, j))],
        out_shape=[jax.ShapeDtypeStruct((geo.nrows, geo.cd), MXU_DTYPE),
                   jax.ShapeDtypeStruct((SSM_CONV, geo.cd), F32), jax.ShapeDtypeStruct((1, geo.cd), F32)],
        compiler_params=_cparams(("parallel", "arbitrary")))(proj, conv_w, conv_b, dxc)


def _tri(q):
    r = lax.broadcasted_iota(jnp.int32, (q, q), 0)
    c = lax.broadcasted_iota(jnp.int32, (q, q), 1)
    return r >= c


def _ssd_pre(dtr, dtb, alog, valid):
    dt = _softplus(dtr + dtb) * valid
    adt = dt * (-jnp.exp(alog))
    a_cs = _dot(_tri(SSM_CHUNK).astype(F32), adt, 1, 0, precision=lax.Precision.HIGHEST)
    return dt, a_cs


def _ssd_specs(geo, rev):
    nc, q = geo.nc, SSM_CHUNK
    ci = (lambda c: nc - 1 - c) if rev else (lambda c: c)
    nxb = SSM_D_INNER // geo.gn
    return [pl.BlockSpec((q, SSM_D_INNER), lambda b, c: (b * nc + ci(c), 0)),
            pl.BlockSpec((q, geo.gn), lambda b, c: (b * nc + ci(c), nxb)),
            pl.BlockSpec((q, geo.gn), lambda b, c: (b * nc + ci(c), nxb + 1)),
            pl.BlockSpec((q, LANE), lambda b, c: (b * nc + ci(c), geo.cb("dt"))),
            pl.BlockSpec((1, LANE), lambda b, c: (0, 0)), pl.BlockSpec((1, LANE), lambda b, c: (0, 0))], ci


def _ssd_fwd(geo, xc, proj, dt_bias, a_log):
    q, p, n, e = SSM_CHUNK, SSM_HEAD_DIM, SSM_STATE, geo.nh // SSM_GROUPS
    nc, pad = geo.nc, geo.pad
    in_specs, _ = _ssd_specs(geo, False)

    def body(xs_ref, b_ref, c_ref, dtr_ref, dtb_ref, alog_ref, y_ref, sp_ref, state):
        c = pl.program_id(1)

        @pl.when(c == 0)
        def _():
            state[...] = jnp.zeros_like(state)

        sp_ref[...] = state[...]
        valid = (c * q + lax.broadcasted_iota(jnp.int32, (q, 1), 0) >= pad).astype(F32)
        dt, a_cs = _ssd_pre(dtr_ref[...], dtb_ref[...], alog_ref[...], valid)
        a_cst = a_cs.T
        tri = _tri(q)
        for g in range(SSM_GROUPS):
            bg, cg = b_ref[:, g * n:(g + 1) * n], c_ref[:, g * n:(g + 1) * n]
            cb = _mxdot(cg, bg, 1, 1)
            for hh in range(e):
                h = g * e + hh
                a_col, a_row, a_last = a_cs[:, h:h + 1], a_cst[h:h + 1, :], a_cs[q - 1:q, h:h + 1]
                xdt = xs_ref[:, h * p:(h + 1) * p] * dt[:, h:h + 1]
                ldec = jnp.exp(jnp.where(tri, a_col - a_row, -jnp.inf))
                s_prev = state[h * p:(h + 1) * p, :]
                y = _mxdot(cb * ldec, xdt, 1, 0) + _mxdot(cg, s_prev, 1, 1) * jnp.exp(a_col)
                y_ref[:, h * p:(h + 1) * p] = y
                st = _mxdot(xdt, bg * jnp.exp(a_last - a_col), 0, 0)
                state[h * p:(h + 1) * p, :] = s_prev * jnp.exp(a_last) + st

    return pl.pallas_call(
        body, name="ssd_fwd", grid=(geo.bsz, nc), in_specs=in_specs,
        out_specs=[pl.BlockSpec((q, SSM_D_INNER), lambda b, c: (b * nc + c, 0)),
                   pl.BlockSpec((SSM_D_INNER, n), lambda b, c: (b * nc + c, 0))],
        out_shape=[jax.ShapeDtypeStruct((geo.nrows, SSM_D_INNER), F32),
                   jax.ShapeDtypeStruct((geo.bsz * nc * SSM_D_INNER, n), F32)],
        scratch_shapes=[pltpu.VMEM((SSM_D_INNER, n), F32)],
        compiler_params=_cparams(("parallel", "arbitrary")))(xc, xc, xc, proj, dt_bias, a_log)


def _ssd_bwd(geo, xc, proj, dt_bias, a_log, s_prev_all, dy, dxs_skip):
    q, p, n, e = SSM_CHUNK, SSM_HEAD_DIM, SSM_STATE, geo.nh // SSM_GROUPS
    nc, pad, di, gn = geo.nc, geo.pad, SSM_D_INNER, geo.gn
    in_specs, ci = _ssd_specs(geo, True)
    row_spec = pl.BlockSpec((q, di), lambda b, c: (b * nc + ci(c), 0))
    in_specs += [pl.BlockSpec((di, n), lambda b, c: (b * nc + ci(c), 0)), row_spec, row_spec]

    def body(xs_ref, b_ref, c_ref, dtr_ref, dtb_ref, alog_ref, sp_ref, dy_ref, dsk_ref,
             dxc_ref, ddt_ref, gdtb_ref, galog_ref, dstate):
        step = pl.program_id(1)
        first = jnp.logical_and(pl.program_id(0) == 0, step == 0)
        c = nc - 1 - step

        @pl.when(step == 0)
        def _():
            dstate[...] = jnp.zeros_like(dstate)

        valid = (c * q + lax.broadcasted_iota(jnp.int32, (q, 1), 0) >= pad).astype(F32)
        dtr, dtb, alog = dtr_ref[...], dtb_ref[...], alog_ref[...]
        dt, a_cs = _ssd_pre(dtr, dtb, alog, valid)
        a_cst = a_cs.T
        tri = _tri(q)
        lane = lax.broadcasted_iota(jnp.int32, (1, LANE), 1)
        sub = lax.broadcasted_iota(jnp.int32, (LANE, 1), 0)
        d_dt = jnp.zeros((q, LANE), F32)
        d_acs = jnp.zeros((q, LANE), F32)
        d_acst = jnp.zeros((LANE, q), F32)
        d_last = jnp.zeros((1, LANE), F32)
        for g in range(SSM_GROUPS):
            bg, cg = b_ref[:, g * n:(g + 1) * n], c_ref[:, g * n:(g + 1) * n]
            cb = _mxdot(cg, bg, 1, 1)
            d_cb = jnp.zeros((q, q), F32)
            d_bg = jnp.zeros((q, n), F32)
            d_cg = jnp.zeros((q, n), F32)
            for hh in range(e):
                h = g * e + hh
                hs = slice(h * p, (h + 1) * p)
                a_col, a_row, a_last = a_cs[:, h:h + 1], a_cst[h:h + 1, :], a_cs[q - 1:q, h:h + 1]
                x = xs_ref[:, hs]
                dt_col = dt[:, h:h + 1]
                xdt = x * dt_col
                ldec = jnp.exp(jnp.where(tri, a_col - a_row, -jnp.inf))
                s_prev = sp_ref[hs, :]
                d_snew = dstate[hs, :]
                dyh = dy_ref[:, hs]
                e_col, e_last = jnp.exp(a_col), jnp.exp(a_last)
                dec = jnp.exp(a_last - a_col)
                d_m = _mxdot(dyh, xdt, 1, 1)
                d_xdt = _mxdot(cb * ldec, dyh, 0, 0)
                d_cb = d_cb + d_m * ldec
                d_diff = d_m * cb * ldec
                da_col = jnp.sum(d_diff, axis=1, keepdims=True)
                da_row = -jnp.sum(d_diff, axis=0, keepdims=True)
                cs = _mxdot(cg, s_prev, 1, 1)
                d_cs = dyh * e_col
                da_col = da_col + jnp.sum(dyh * cs, axis=1, keepdims=True) * e_col
                d_cg = d_cg + _mxdot(d_cs, s_prev, 1, 0)
                d_sprev = _mxdot(d_cs, cg, 0, 0) + d_snew * e_last
                dl = jnp.sum(jnp.sum(d_snew * s_prev, axis=1, keepdims=True), axis=0, keepdims=True) * e_last
                d_xdt = d_xdt + _mxdot(bg * dec, d_snew, 1, 1)
                d_bd = _mxdot(xdt, d_snew, 1, 0)
                d_bg = d_bg + d_bd * dec
                d_dec = jnp.sum(d_bd * bg, axis=1, keepdims=True) * dec
                dl = dl + jnp.sum(d_dec, axis=0, keepdims=True)
                da_col = da_col - d_dec
                dstate[hs, :] = d_sprev
                dxc_ref[:, hs] = d_xdt * dt_col + dsk_ref[:, hs]
                onehot = (lane == h).astype(F32)
                d_dt = d_dt + jnp.sum(d_xdt * x, axis=1, keepdims=True) * onehot
                d_acs = d_acs + da_col * onehot
                d_acst = d_acst + (sub == h).astype(F32) * da_row
                d_last = d_last + dl * onehot
            dxc_ref[:, di + g * n:di + (g + 1) * n] = d_bg + _mxdot(d_cb, cg, 0, 0)
            dxc_ref[:, di + gn + g * n:di + gn + (g + 1) * n] = d_cg + _mxdot(d_cb, bg, 1, 0)
        is_last = (lax.broadcasted_iota(jnp.int32, (q, 1), 0) == q - 1).astype(F32)
        d_acs = d_acs + d_acst.T + is_last * d_last
        d_adt = _dot(_tri(q).astype(F32), d_acs, 0, 0, precision=lax.Precision.HIGHEST)
        a = -jnp.exp(alog)
        d_dt = d_dt + d_adt * a
        g_alog = jnp.sum(d_adt * dt, axis=0, keepdims=True) * a
        d_dtr = d_dt * valid * _sigmoid(dtr + dtb)
        ddt_ref[...] = d_dtr.astype(ddt_ref.dtype)
        g_dtb = jnp.sum(d_dtr, axis=0, keepdims=True)

        @pl.when(first)
        def _():
            gdtb_ref[...] = g_dtb
            galog_ref[...] = g_alog

        @pl.when(jnp.logical_not(first))
        def _():
            gdtb_ref[...] += g_dtb
            galog_ref[...] += g_alog

    vec = pl.BlockSpec((1, LANE), lambda b, c: (0, 0))
    return pl.pallas_call(
        body, name="ssd_bwd", grid=(geo.bsz, nc), in_specs=in_specs,
        out_specs=[pl.BlockSpec((q, geo.cd), lambda b, c: (b * nc + ci(c), 0)),
                   pl.BlockSpec((q, LANE), lambda b, c: (b * nc + ci(c), 0)), vec, vec],
        out_shape=[jax.ShapeDtypeStruct((geo.nrows, geo.cd), F32), jax.ShapeDtypeStruct((geo.nrows, LANE), MXU_DTYPE),
                   jax.ShapeDtypeStruct((1, LANE), F32), jax.ShapeDtypeStruct((1, LANE), F32)],
        scratch_shapes=[pltpu.VMEM((di, n), F32)],
        compiler_params=_cparams(("arbitrary", "arbitrary")))(xc, xc, xc, proj, dt_bias, a_log, s_prev_all, dy, dxs_skip)


BIAS_LANE = MLA_ROPE // 2
KEY_OFF = -1e30
ATT_SCALE = (MLA_NOPE + MLA_ROPE) ** -0.5


def _row_t(col):
    return jnp.broadcast_to(col, (col.shape[0], LANE)).T[:8]


def _attn_fwd2(geo, qc, kc, v):
    t, lp = ATT_BLK, geo.lp
    nb = lp // t

    def body(q_ref, k_ref, v_ref, o_ref, lse_ref):
        qi = pl.program_id(2)
        q = q_ref[...]

        def blk(kj, carry, diag):
            m, l, acc = carry
            ks = pl.ds(pl.multiple_of(kj * t, t), t)
            s = _mxdot(q, k_ref[ks, :], 1, 1) * ATT_SCALE
            if diag:
                s = jnp.where(_tri(t), s, -jnp.inf)
            m_new = jnp.maximum(m, jnp.max(s, axis=1, keepdims=True))
            pr = jnp.exp(s - m_new)
            alpha = jnp.exp(m - m_new)
            return m_new, alpha * l + jnp.sum(pr, axis=1, keepdims=True), alpha * acc + _mxdot(pr, v_ref[ks, :], 1, 0)

        init = (jnp.full((t, 1), 2.0 * KEY_OFF, F32), jnp.zeros((t, 1), F32), jnp.zeros((t, LANE), F32))
        carry = lax.fori_loop(0, qi, lambda kj, c: blk(kj, c, False), init)
        m, l, acc = blk(qi, carry, True)
        o_ref[...] = acc / l
        lse_ref[0, 0, 0] = _row_t(m + jnp.log(l))

    return pl.pallas_call(
        body, name="attn_fwd", grid=(geo.bsz, MLA_HEADS, nb),
        in_specs=[pl.BlockSpec((t, 2 * LANE), lambda b, h, i: (b * nb + i, h)),
                  pl.BlockSpec((lp, 2 * LANE), lambda b, h, i: (b, h)), pl.BlockSpec((lp, LANE), lambda b, h, i: (b, h))],
        out_specs=[pl.BlockSpec((t, LANE), lambda b, h, i: (b * nb + i, h)),
                   pl.BlockSpec((1, 1, 1, 8, t), lambda b, h, i: (b, h, i, 0, 0))],
        out_shape=[jax.ShapeDtypeStruct((geo.nrows, geo.hq), F32),
                   jax.ShapeDtypeStruct((geo.bsz, MLA_HEADS, nb, 8, t), F32)],
        compiler_params=_cparams(("parallel", "parallel", "arbitrary")))(qc, kc, v)


def _attn_delta(geo, d_o, o):
    t, nb = ATT_BLK, geo.lp // ATT_BLK

    def body(do_ref, o_ref, dl_ref):
        dl_ref[0, 0, 0] = _row_t(jnp.sum(do_ref[...] * o_ref[...], axis=1, keepdims=True))

    spec = pl.BlockSpec((t, LANE), lambda b, h, i: (b * nb + i, h))
    return pl.pallas_call(
        body, name="attn_delta", grid=(geo.bsz, MLA_HEADS, nb), in_specs=[spec, spec],
        out_specs=pl.BlockSpec((1, 1, 1, 8, t), lambda b, h, i: (b, h, i, 0, 0)),
        out_shape=jax.ShapeDtypeStruct((geo.bsz, MLA_HEADS, nb, 8, t), F32),
        compiler_params=_cparams(("parallel", "parallel", "parallel")))(d_o, o)


def _attn_bwd2(geo, qc, kc, v, d_o, lse, delta):
    t, lp = ATT_BLK, geo.lp
    nb = lp // t

    def body(q_ref, k_ref, v_ref, do_ref, lse_ref, dl_ref, dq_ref, dk_ref, dv_ref):
        kj = pl.program_id(2)

        @pl.when(kj == 0)
        def _():
            dq_ref[...] = jnp.zeros_like(dq_ref)

        k, vv = k_ref[...], v_ref[...]

        def blk(qi, carry, diag):
            dk, dv = carry
            qs = pl.ds(pl.multiple_of(qi * t, t), t)
            q, d_o_blk = q_ref[qs, :], do_ref[qs, :]
            st = _mxdot(k, q, 1, 1) * ATT_SCALE
            if diag:
                keys = lax.broadcasted_iota(jnp.int32, (t, t), 0)
                st = jnp.where(keys <= lax.broadcasted_iota(jnp.int32, (t, t), 1), st, -jnp.inf)
            pt = jnp.exp(st - lse_ref[0, 0, qi][:1, :])
            dst = pt * (_mxdot(vv, d_o_blk, 1, 1) - dl_ref[0, 0, qi][:1, :]) * ATT_SCALE
            dq_ref[qs, :] += _mxdot(dst, k, 0, 0)
            return dk + _mxdot(dst, q, 1, 0), dv + _mxdot(pt, d_o_blk, 1, 0)

        carry = blk(kj, (jnp.zeros((t, 2 * LANE), F32), jnp.zeros((t, LANE), F32)), True)
        dk, dv = lax.fori_loop(kj + 1, nb, lambda qi, c: blk(qi, c, False), carry)
        dk_ref[...] = dk
        dv_ref[...] = dv.astype(dv_ref.dtype)

    rows = pl.BlockSpec((1, 1, nb, 8, t), lambda b, h, j: (b, h, 0, 0, 0))
    return pl.pallas_call(
        body, name="attn_bwd", grid=(geo.bsz, MLA_HEADS, nb),
        in_specs=[pl.BlockSpec((lp, 2 * LANE), lambda b, h, j: (b, h)),
                  pl.BlockSpec((t, 2 * LANE), lambda b, h, j: (b * nb + j, h)),
                  pl.BlockSpec((t, LANE), lambda b, h, j: (b * nb + j, h)),
                  pl.BlockSpec((lp, LANE), lambda b, h, j: (b, h)), rows, rows],
        out_specs=[pl.BlockSpec((lp, 2 * LANE), lambda b, h, j: (b, h)),
                   pl.BlockSpec((t, 2 * LANE), lambda b, h, j: (b * nb + j, h)),
                   pl.BlockSpec((t, LANE), lambda b, h, j: (b * nb + j, h))],
        out_shape=[jax.ShapeDtypeStruct((geo.nrows, 2 * geo.hq), F32), jax.ShapeDtypeStruct((geo.nrows, 2 * geo.hq), F32),
                   jax.ShapeDtypeStruct((geo.nrows, geo.hq), MXU_DTYPE)],
        compiler_params=_cparams(("parallel", "parallel", "arbitrary")))(qc, kc, v, d_o, lse, delta)


def _rope(x, cos, sin):
    return x * cos + pltpu.roll(x, LANE // 2, axis=1) * sin


def _rope_t(dx, cos, sin):
    return dx * cos + pltpu.roll(dx * sin, LANE // 2, axis=1)


def _layer_fwd(geo, h, w, tab):
    nr, tr, trw = geo.nrows, geo.tr, geo.tr_wide
    tb = geo.lp // tr
    rw = functools.partial(_rowwise, nrows=nr)
    s = {"h": h}
    (s["u"],) = rw("rms_mix", lambda x, g: (_rms(x, g),), tr=tr, rows=[(h, D_MODEL, 0)],
                   vecs=[(w["norm_mix_w"], D_MODEL, 0)], outs=[(D_MODEL, D_MODEL, MXU_DTYPE)])
    proj = s["proj"] = _mm("mm_in", s["u"], w["w_in_p"])
    xc = s["xc"] = _conv_fwd(geo, proj, w["conv_w"], w["conv_b"])
    s["y_ssd"], s["s_prev"] = _ssd_fwd(geo, xc, proj, w["dt_bias"], w["a_log"])
    gw = SSM_D_INNER // SSM_GROUPS

    def gate_norm(y, x, z, dsk, nw):
        return (_rms((y + x * dsk) * _silu(z), nw),)

    (s["y_ssm"],) = rw("ssm_gate_norm", gate_norm, tr=tr, ncb=SSM_GROUPS,
                       rows=[(s["y_ssd"], gw, 0), (xc, gw, 0), (proj, gw, geo.col["z"][0] // gw)],
                       vecs=[(w["d_skip_full"], gw, 0), (w["ssm_norm_w"], gw, 0)], outs=[(SSM_D_INNER, gw, MXU_DTYPE)])
    (s["cq_n"],) = rw("rms_q", lambda x, g: (_rms(x, g),), tr=tr, rows=[(proj, MLA_Q_LORA, geo.cb("c_q"))],
                      vecs=[(w["q_norm_w"], MLA_Q_LORA, 0)], outs=[(MLA_Q_LORA, MLA_Q_LORA, MXU_DTYPE)])
    (s["ckv_n"],) = rw("rms_kv", lambda x, g: (_rms(x, g),), tr=tr, rows=[(proj, MLA_KV_LORA, geo.cb("c_kv"))],
                       vecs=[(w["kv_norm_w"], MLA_KV_LORA, 0)], outs=[(MLA_KV_LORA, MLA_KV_LORA, MXU_DTYPE)])
    qn = _mm("mm_qn", s["cq_n"], w["w_qn"])
    qp_raw = _mm("mm_qp", s["cq_n"], w["w_qp"])
    kn = _mm("mm_kn", s["ckv_n"], w["w_k"])
    s["v"] = _mm("mm_v", s["ckv_n"], w["w_v"], out_dtype=MXU_DTYPE)
    bias_lane = lambda: lax.broadcasted_iota(jnp.int32, (1, LANE), 1) == BIAS_LANE

    def q_cat(x, xp, c, sn):
        return (jnp.concatenate([x, jnp.where(bias_lane(), 1.0, _rope(xp, c, sn))], axis=1),)

    def k_cat(x, xp, c, sn, valid):
        return (jnp.concatenate([x, jnp.where(bias_lane(), KEY_OFF * (1.0 - valid), _rope(xp, c, sn))], axis=1),)

    rope_tabs = [(tab["cos"], LANE, 0), (tab["sin"], LANE, 0)]
    (s["qc"],) = rw("rope_q", q_cat, tr=tr, ncb=MLA_HEADS, rows=[(qn, LANE, 0), (qp_raw, LANE, 0)], tabs=rope_tabs,
                    outs=[(2 * geo.hq, 2 * LANE, MXU_DTYPE)], tab_blocks=tb)
    (s["kc"],) = rw("rope_k", k_cat, tr=tr, ncb=MLA_HEADS, rows=[(kn, LANE, 0)], fixed=[(proj, LANE, geo.cb("k_rope"))],
                    tabs=rope_tabs + [(tab["valid"], 1, 0)], outs=[(2 * geo.hq, 2 * LANE, MXU_DTYPE)], tab_blocks=tb)
    s["o"], s["lse"] = _attn_fwd2(geo, s["qc"], s["kc"], s["v"])
    s["ys_p"] = _mm("mm_bs", s["y_ssm"], w["w_branch_ssm"])
    s["ym_p"] = _mm("mm_bm", s["o"], w["w_branch_mla"])

    def gate(gs, gm, ys, ym):
        return (_sigmoid(gs) * ys + _sigmoid(gm) * ym,)

    (s["mixed"],) = rw("gate", gate, tr=tr, rows=[(proj, D_MODEL, geo.cb("g_ssm")), (proj, D_MODEL, geo.cb("g_mla")),
                                                  (s["ys_p"], D_MODEL, 0), (s["ym_p"], D_MODEL, 0)],
                       outs=[(D_MODEL, D_MODEL, MXU_DTYPE)])
    s["h2"] = _mm("mm_out", s["mixed"], w["w_out"], add=h)
    (s["vn"],) = rw("rms_mlp", lambda x, g: (_rms(x, g),), tr=tr, rows=[(s["h2"], D_MODEL, 0)],
                    vecs=[(w["norm_mlp_w"], D_MODEL, 0)], outs=[(D_MODEL, D_MODEL, MXU_DTYPE)])
    s["up"] = _mm("mm_up", s["vn"], w["w_mlp_up"])
    (s["act"],) = rw("relu2", lambda x: (jnp.square(jnp.maximum(x, 0.0)),), tr=trw, rows=[(s["up"], D_FF, 0)],
                     outs=[(D_FF, D_FF, MXU_DTYPE)])
    return _mm("mm_down", s["act"], w["w_mlp_down"], add=s["h2"]), s


def _layer_bwd(geo, dh3, s, w, tab):
    nr, tr, trw = geo.nrows, geo.tr, geo.tr_wide
    tb = geo.lp // tr
    rw = functools.partial(_rowwise, nrows=nr)
    g = {}
    proj = s["proj"]

    def rms_bwd(x, dy, res, gw):
        _, vjp = jax.vjp(_rms, x.astype(F32), gw)
        dx, dgw = vjp(dy.astype(F32))
        return dx + res, dgw

    def rms_bwd_nores(x, dy, gw):
        _, vjp = jax.vjp(_rms, x.astype(F32), gw)
        return vjp(dy.astype(F32))

    dact = _mm("mm_down_t", dh3, w["w_mlp_down"], tb=True)
    g["w_mlp_down"] = _mm("mm_down_g", s["act"], dh3, ta=True, out_dtype=MXU_DTYPE)
    (dup,) = rw("relu2_bwd", lambda d, x: (d * 2.0 * jnp.maximum(x, 0.0),), tr=trw,
                rows=[(dact, D_FF, 0), (s["up"], D_FF, 0)], outs=[(D_FF, D_FF, MXU_DTYPE)])
    g["w_mlp_up"] = _mm("mm_up_g", s["vn"], dup, ta=True, out_dtype=MXU_DTYPE)
    dvn = _mm("mm_up_t", dup, w["w_mlp_up"], tb=True)
    dh2, g["norm_mlp_w"] = rw("rms_mlp_bwd", rms_bwd, tr=tr,
                              rows=[(s["h2"], D_MODEL, 0), (dvn, D_MODEL, 0), (dh3, D_MODEL, 0)],
                              vecs=[(w["norm_mlp_w"], D_MODEL, 0)], outs=[(D_MODEL, D_MODEL, F32)],
                              reds=[(D_MODEL, D_MODEL)])
    dmixed = _mm("mm_out_t", dh2, w["w_out"], tb=True)
    g["w_out"] = _mm("mm_out_g", s["mixed"], dh2, ta=True, out_dtype=MXU_DTYPE)

    def gate_bwd(gs, gm, ys, ym, dm):
        f = lambda a, b, c, d: _sigmoid(a) * c + _sigmoid(b) * d
        _, vjp = jax.vjp(f, gs, gm, ys, ym)
        dgs, dgm, dys, dym = vjp(dm)
        return dys, dym, dgs, dgm

    dys_p, dym_p, dg_ssm, dg_mla = rw(
        "gate_bwd", gate_bwd, tr=tr,
        rows=[(proj, D_MODEL, geo.cb("g_ssm")), (proj, D_MODEL, geo.cb("g_mla")), (s["ys_p"], D_MODEL, 0),
              (s["ym_p"], D_MODEL, 0), (dmixed, D_MODEL, 0)], outs=[(D_MODEL, D_MODEL, MXU_DTYPE)] * 4)
    g["w_branch_ssm"] = _mm("mm_bs_g", s["y_ssm"], dys_p, ta=True, out_dtype=MXU_DTYPE)
    dy_ssm = _mm("mm_bs_t", dys_p, w["w_branch_ssm"], tb=True)
    g["w_branch_mla"] = _mm("mm_bm_g", s["o"], dym_p, ta=True, out_dtype=MXU_DTYPE)
    d_o = _mm("mm_bm_t", dym_p, w["w_branch_mla"], tb=True)
    delta = _attn_delta(geo, d_o, s["o"])
    dqc, dkc, dv = _attn_bwd2(geo, s["qc"], s["kc"], s["v"], d_o, s["lse"], delta)
    rope_tabs = [(tab["cos"], LANE, 0), (tab["sin"], LANE, 0)]
    dqn, dqp_raw = rw("rope_q_bwd", lambda x, c, sn: (x[:, :LANE], _rope_t(x[:, LANE:], c, sn)), tr=tr, ncb=MLA_HEADS,
                      rows=[(dqc, 2 * LANE, 0)], tabs=rope_tabs, outs=[(geo.hq, LANE, MXU_DTYPE)] * 2, tab_blocks=tb)

    def rope_k_bwd(x, c, sn):
        tot = x[:, LANE:2 * LANE]
        for hd in range(1, MLA_HEADS):
            tot = tot + x[:, (2 * hd + 1) * LANE:(2 * hd + 2) * LANE]
        dkn_ = jnp.concatenate([x[:, 2 * hd * LANE:(2 * hd + 1) * LANE] for hd in range(MLA_HEADS)], axis=1)
        return dkn_, _rope_t(tot, c, sn)

    dkn, dk_rope = rw("rope_k_bwd", rope_k_bwd, tr=geo.tr_wide, rows=[(dkc, 2 * geo.hq, 0)], tabs=rope_tabs,
                      outs=[(geo.hq, geo.hq, MXU_DTYPE), (LANE, LANE, MXU_DTYPE)], tab_blocks=geo.lp // geo.tr_wide)
    g["w_qn"] = _mm("mm_qn_g", s["cq_n"], dqn, ta=True, out_dtype=MXU_DTYPE)
    g["w_qp"] = _mm("mm_qp_g", s["cq_n"], dqp_raw, ta=True, out_dtype=MXU_DTYPE)
    dcq_n = _mm("mm_qp_t", dqp_raw, w["w_qp"], tb=True, add=_mm("mm_qn_t", dqn, w["w_qn"], tb=True))
    g["w_k"] = _mm("mm_kn_g", s["ckv_n"], dkn, ta=True, out_dtype=MXU_DTYPE)
    g["w_v"] = _mm("mm_v_g", s["ckv_n"], dv, ta=True, out_dtype=MXU_DTYPE)
    dckv_n = _mm("mm_v_t", dv, w["w_v"], tb=True, add=_mm("mm_kn_t", dkn, w["w_k"], tb=True))
    dc_q, g["q_norm_w"] = rw("rms_q_bwd", rms_bwd_nores, tr=tr,
                             rows=[(proj, MLA_Q_LORA, geo.cb("c_q")), (dcq_n, MLA_Q_LORA, 0)],
                             vecs=[(w["q_norm_w"], MLA_Q_LORA, 0)], outs=[(MLA_Q_LORA, MLA_Q_LORA, MXU_DTYPE)],
                             reds=[(MLA_Q_LORA, MLA_Q_LORA)])
    dc_kv, g["kv_norm_w"] = rw("rms_kv_bwd", rms_bwd_nores, tr=tr,
                               rows=[(proj, MLA_KV_LORA, geo.cb("c_kv")), (dckv_n, MLA_KV_LORA, 0)],
                               vecs=[(w["kv_norm_w"], MLA_KV_LORA, 0)], outs=[(MLA_KV_LORA, MLA_KV_LORA, MXU_DTYPE)],
                               reds=[(MLA_KV_LORA, MLA_KV_LORA)])
    gw_ = SSM_D_INNER // SSM_GROUPS

    def gate_norm_bwd(y, x, z, dy, dsk, nw):
        f = lambda y_, x_, z_, dsk_, nw_: _rms((y_ + x_ * dsk_) * _silu(z_), nw_)
        _, vjp = jax.vjp(f, y, x, z, dsk, nw)
        dy_, dx_, dz_, ddsk, dnw = vjp(dy)
        return dy_, dx_, dz_, ddsk, dnw

    dy_ssd, dxs_skip, dz, g["d_skip_full"], g["ssm_norm_w"] = rw(
        "ssm_gate_norm_bwd", gate_norm_bwd, tr=tr, ncb=SSM_GROUPS,
        rows=[(s["y_ssd"], gw_, 0), (s["xc"], gw_, 0), (proj, gw_, geo.col["z"][0] // gw_), (dy_ssm, gw_, 0)],
        vecs=[(w["d_skip_full"], gw_, 0), (w["ssm_norm_w"], gw_, 0)],
        outs=[(SSM_D_INNER, gw_, F32), (SSM_D_INNER, gw_, F32), (SSM_D_INNER, gw_, MXU_DTYPE)],
        reds=[(SSM_D_INNER, gw_), (SSM_D_INNER, gw_)])
    dxc, ddt, g["dt_bias"], g["a_log"] = _ssd_bwd(geo, s["xc"], proj, w["dt_bias"], w["a_log"], s["s_prev"],
                                                   dy_ssd, dxs_skip)
    dxbc, g["conv_w"], g["conv_b"] = _conv_bwd(geo, proj, w["conv_w"], w["conv_b"], dxc)
    di, gn = SSM_D_INNER, geo.gn
    dproj = jnp.concatenate([dz, dxbc[:, :di], dg_ssm, dg_mla, dxbc[:, di:di + gn], dxbc[:, di + gn:], dc_q, dc_kv,
                             ddt, dk_rope], axis=-1)
    g["w_in_p"] = _mm("mm_in_g", s["u"], dproj, ta=True, out_dtype=MXU_DTYPE)
    du = _mm("mm_in_t", dproj, w["w_in_p"], tb=True)
    dh, g["norm_mix_w"] = rw("rms_mix_bwd", rms_bwd, tr=tr,
                             rows=[(s["h"], D_MODEL, 0), (du, D_MODEL, 0), (dh2, D_MODEL, 0)],
                             vecs=[(w["norm_mix_w"], D_MODEL, 0)], outs=[(D_MODEL, D_MODEL, F32)],
                             reds=[(D_MODEL, D_MODEL)])
    return dh, g


def _loss_bwd(geo, h, fw, target, tab):
    tr = geo.tr

    def fn(x, tgt, gw, tok):
        def lossf(x_, gw_):
            err = jnp.square(_rms(x_, gw_) - tgt)
            return 0.5 * jnp.sum(tok * jnp.mean(err, axis=-1, keepdims=True), axis=0, keepdims=True)

        val, vjp = jax.vjp(lossf, x, gw)
        dx, dgw = vjp(jnp.ones((1, 1), F32))
        return dx, jnp.broadcast_to(val, (1, LANE)), dgw

    return _rowwise("loss", fn, nrows=geo.nrows, tr=tr, rows=[(h, D_MODEL, 0), (target, D_MODEL, 0)],
                    vecs=[(fw, D_MODEL, 0)], tabs=[(tab["token"], 1, 0)], outs=[(D_MODEL, D_MODEL, F32)],
                    reds=[(LANE, LANE), (D_MODEL, D_MODEL)], tab_blocks=geo.lp // tr)


def kernel(x, meta_tokens, norm_mix_w, w_in, conv_w, conv_b, dt_bias, a_log, d_skip, ssm_norm_w, q_norm_w, kv_norm_w, w_uq, w_ukv, w_branch_ssm, w_branch_mla, w_out, norm_mlp_w, w_mlp_up, w_mlp_down, final_norm_w, loss_target, m_meta_tokens, m_norm_mix_w, m_w_in, m_conv_w, m_conv_b, m_dt_bias, m_a_log, m_d_skip, m_ssm_norm_w, m_q_norm_w, m_kv_norm_w, m_w_uq, m_w_ukv, m_w_branch_ssm, m_w_branch_mla, m_w_out, m_norm_mlp_w, m_w_mlp_up, m_w_mlp_down, m_final_norm_w, v_meta_tokens, v_norm_mix_w, v_w_in, v_conv_w, v_conv_b, v_dt_bias, v_a_log, v_d_skip, v_ssm_norm_w, v_q_norm_w, v_kv_norm_w, v_w_uq, v_w_ukv, v_w_branch_ssm, v_w_branch_mla, v_w_out, v_norm_mlp_w, v_w_mlp_up, v_w_mlp_down, v_final_norm_w):
    args = dict(locals())
    wts = {n: args[n] for n in WEIGHTS}
    mom = {n: args["m_" + n] for n in WEIGHTS}
    var = {n: args["v_" + n] for n in WEIGHTS}
    bsz, seq, _ = x.shape
    depth = w_in.shape[0]
    geo = _Geo(bsz, seq)
    tab = _tables(geo)

    big_names = [n for n, _ in BIG]
    sh_names = big_names + [n for n, _ in SHARDED_F32]
    kinds = dict(BIG + SHARDED_F32)
    shard3 = lambda a: a.reshape((1,) + a.shape) if a.ndim == 2 else a
    wire = {n: (MXU_DTYPE if n in big_names else F32) for n in sh_names}
    cast = {n: shard3(wts[n]).astype(wire[n]) for n in sh_names}
    per_layer = [n for n in sh_names if n != "meta_tokens"]
    small_names = ["norm_mix_w", "conv_b", "dt_bias", "a_log", "d_skip", "ssm_norm_w", "q_norm_w", "kv_norm_w",
                   "norm_mlp_w"]

    def gather_items(pairs):
        ins, outs, items, forms = [], [], [], []
        for n, i in pairs:
            a, b = cast[n].shape[1:]
            shape, dst, form = _gather_plan(a, b, kinds[n])
            items.append((len(ins), len(outs), (lambda ref, p, i=i: ref.at[i]), dst))
            ins.append(cast[n])
            outs.append(jax.ShapeDtypeStruct(shape, wire[n]))
            forms.append(form)
        return ins, outs, items, forms

    def whole_weights(pairs, forms, got):
        by_layer = {}
        for (n, i), form, g in zip(pairs, forms, got):
            if n == "w_in":
                n, g = "w_in_p", _w_in_assemble(geo, g)
            elif form == "row":
                g = g.reshape(g.shape[0] * g.shape[1], g.shape[2])
            elif form == "stack":
                g = _unshard(g, "col")
            by_layer.setdefault(i, {})[n] = g
        return by_layer

    def prep(i, whole, token=None):
        wl = dict(whole)
        wl.update({n: wts[n][i] for n in small_names})
        if token is not None:
            wl["norm_mix_w"] = wl["norm_mix_w"] + token[0, 0]
        return _prep_layer(geo, wl)

    pairs0 = [(n, 0) for n in per_layer] + [("meta_tokens", 0)]
    ins, outs, items, forms = gather_items(pairs0)
    whole0 = whole_weights(pairs0, forms, _exchange("gather_w0", ins, outs, items))[0]
    meta_full = whole0.pop("meta_tokens")
    pairs1 = [(n, i) for i in range(1, depth) for n in per_layer]
    token = None
    if pairs1:
        ins1, outs1, items1, forms1 = gather_items(pairs1)
        landing = _local_fill("gather_w1_own", ins1, outs1, items1)
        sems1, thru1, landing, token = _exchange_start("gather_w1_start", ins1, landing, items1)
    layers = [prep(0, whole0, token)]

    meta = jnp.broadcast_to(meta_full[None], (bsz, N_META, D_MODEL))
    h = jnp.concatenate([jnp.zeros((bsz, geo.pad, D_MODEL), F32), meta, x], axis=1).reshape(geo.nrows, D_MODEL)
    target = jnp.concatenate([jnp.zeros((bsz, geo.pad + N_META, D_MODEL), F32), loss_target], axis=1)
    target = target.reshape(geo.nrows, D_MODEL)
    saved = []
    for i in range(depth):
        if i == 1:
            got1 = _exchange_wait("gather_w1_wait", sems1, thru1, landing, items1, h)
            whole1 = whole_weights(pairs1, forms1, got1)
            layers += [prep(j, whole1[j]) for j in range(1, depth)]
        h, s = _layer_fwd(geo, h, layers[i], tab)
        saved.append(s)
    dh, loss_part, g_final = _loss_bwd(geo, h, final_norm_w.reshape(1, -1), target, tab)

    def scatter_items(pairs):
        ins, outs, items = [], [], []
        for n, i in pairs:
            a, b = cast[n].shape[1:]
            arr = g_meta if n == "meta_tokens" else grads[i]["w_in_p" if n == "w_in" else n]
            if n == "w_in":
                arr, src = _w_in_split(geo, arr, b), _entry
            elif kinds[n] == "row":
                src = lambda ref, p, a=a: ref.at[pl.ds(pl.multiple_of(p * a, a), a)]
            elif b % LANE == 0:
                src = lambda ref, p, b=b: ref.at[:, pl.ds(pl.multiple_of(p * b, b), b)]
            else:
                arr, src = _shard(arr, "col"), _entry
            items.append((len(ins), len(outs), src, _entry))
            ins.append(arr.astype(wire[n]))
            outs.append(jax.ShapeDtypeStruct((N_DEV, a, b), wire[n]))
        return ins, outs, items

    grads = [None] * depth
    landed = {}
    for i in reversed(range(depth)):
        dh, gl = _layer_bwd(geo, dh, saved[i], layers[i], tab)
        grads[i] = _unprep_grads(geo, gl)
        if i == 1:
            ins1, outs1, items1 = scatter_items(pairs1)
            landing = _local_fill("scatter_g1_own", ins1, outs1, items1)
            sems1, thru1, landing, token = _exchange_start("scatter_g1_start", ins1, landing, items1)
            dh = dh + token[0, 0]
    if pairs1:
        landed.update(zip(pairs1, _exchange_wait("scatter_g1_wait", sems1, thru1, landing, items1, dh)))
    dh = dh.reshape(bsz, geo.lp, D_MODEL)
    grad_x = dh[:, geo.pad + N_META:]
    g_meta = jnp.sum(dh[:, geo.pad:geo.pad + N_META], axis=0)
    ins, outs, items = scatter_items(pairs0)
    landed.update(zip(pairs0, _exchange("scatter_g0", ins, outs, items)))
    res = {}
    for n in sh_names:
        parts = [landed[(n, i)] for i in range(cast[n].shape[0])]
        r = _adamw_nat("adamw_" + n, parts, shard3(wts[n]), shard3(mom[n]), shard3(var[n]))
        res[n] = [a.reshape(wts[n].shape) for a in r]
    g_small = {n: jnp.stack([grads[i][n] for i in range(depth)]) for n in SMALL if n != "final_norm_w"}
    g_small["final_norm_w"] = g_final.reshape(-1)
    zero = jnp.zeros((1,), F32)
    pk = lambda d, last: _pack([d[n] for n in SMALL] + [last], F32, row_mult=8)
    packed = pk(g_small, loss_part[0, :1])
    (parts,) = _exchange("gather_g", [packed], [jax.ShapeDtypeStruct((N_DEV,) + packed.shape, F32)],
                         [(0, 0, _whole, _entry)])
    res_sm = _adamw("adamw_small", parts, pk(wts, zero), pk(mom, zero), pk(var, zero))
    res_sm = [_unpack(r, [wts[n].shape for n in SMALL] + [(1,)]) for r in res_sm]
    loss = res_sm[0][-1][0]

    out = [loss, grad_x]
    for k in range(4):
        named = {n: res[n][k] for n in sh_names}
        named.update(zip(SMALL, res_sm[k]))
        out += [named[n] for n in WEIGHTS]
    return tuple(out)
```

```python
import functools

import numpy as np
import jax
import jax.numpy as jnp
from jax import lax
from jax.experimental import pallas as pl
from jax.experimental.pallas import tpu as pltpu

F32 = jnp.float32
MXU_DTYPE = jnp.bfloat16

D_MODEL = 1024
N_META = 16
EPS = 1e-6
SSM_D_INNER = 2048
SSM_HEAD_DIM = 64
SSM_GROUPS = 4
SSM_STATE = 128
SSM_CONV = 4
SSM_CHUNK = 128
MLA_HEADS = 8
MLA_Q_LORA = 512
MLA_KV_LORA = 256
MLA_NOPE = 128
MLA_ROPE = 64
MLA_V = 128
ROPE_THETA = 10000.0
D_FF = 4096
ADAM_LR = 0.001
ADAM_B1 = 0.9
ADAM_B2 = 0.999
ADAM_EPS = 1e-08
ADAM_WD = 0.01
ADAM_STEP = 10

N_DEV = 8
ATT_BLK = 256
LANE = 128
PACK_W = 1024
VMEM_LIMIT = 56 * 1024 * 1024
MESH_ID = pl.DeviceIdType.MESH

BIG = (("w_in", "col"), ("w_uq", "col"), ("w_ukv", "col"), ("w_branch_ssm", "row"), ("w_branch_mla", "row"),
       ("w_out", "row"), ("w_mlp_up", "col"), ("w_mlp_down", "row"))
SHARDED_F32 = (("conv_w", "col"), ("meta_tokens", "col"))
SMALL = ("norm_mix_w", "conv_b", "dt_bias", "a_log", "d_skip", "ssm_norm_w", "q_norm_w", "kv_norm_w",
         "norm_mlp_w", "final_norm_w")
WEIGHTS = ("meta_tokens", "norm_mix_w", "w_in", "conv_w", "conv_b", "dt_bias", "a_log", "d_skip", "ssm_norm_w",
           "q_norm_w", "kv_norm_w", "w_uq", "w_ukv", "w_branch_ssm", "w_branch_mla", "w_out", "norm_mlp_w",
           "w_mlp_up", "w_mlp_down", "final_norm_w")


def _cparams(sem=None):
    return pltpu.CompilerParams(dimension_semantics=sem, vmem_limit_bytes=VMEM_LIMIT)


def _pick(n, cands):
    for c in cands:
        if n % c == 0:
            return c
    return n


def _sigmoid(x):
    return 1.0 / (1.0 + jnp.exp(-x))


def _silu(x):
    return x * _sigmoid(x)


def _softplus(x):
    return jnp.maximum(x, 0.0) + jnp.log1p(jnp.exp(-jnp.abs(x)))


def _rms(x, w):
    return x * lax.rsqrt(jnp.mean(x * x, axis=-1, keepdims=True) + EPS) * w


def _dot(a, b, ca, cb, precision=None):
    return lax.dot_general(a, b, (((ca,), (cb,)), ((), ())), preferred_element_type=F32, precision=precision)


def _mxdot(a, b, ca, cb):
    return _dot(a.astype(MXU_DTYPE), b.astype(MXU_DTYPE), ca, cb)


def _mm(name, a, b, *, ta=False, tb=False, add=None, out_dtype=F32):
    (kdim, m) = a.shape if ta else a.shape[::-1]
    (n, k2) = b.shape if tb else b.shape[::-1]
    assert kdim == k2, (name, a.shape, b.shape)
    tm = _pick(m, (1152, 1024, 768, 512, 384, 256, 128))
    tn = _pick(n, (1024, 512, 384, 256, 128))
    tk = _pick(kdim, (1152, 1024, 768, 512, 384, 256, 128))
    nk = kdim // tk
    a_spec = pl.BlockSpec((tk, tm), lambda i, j, k: (k, i)) if ta else pl.BlockSpec((tm, tk), lambda i, j, k: (i, k))
    b_spec = pl.BlockSpec((tn, tk), lambda i, j, k: (j, k)) if tb else pl.BlockSpec((tk, tn), lambda i, j, k: (k, j))
    o_spec = pl.BlockSpec((tm, tn), lambda i, j, k: (i, j))
    ca, cb = (0 if ta else 1), (1 if tb else 0)

    def body(*refs):
        a_ref, b_ref = refs[:2]
        o_ref, acc = refs[-2:]
        k = pl.program_id(2)

        @pl.when(k == 0)
        def _():
            acc[...] = jnp.zeros_like(acc)

        acc[...] += _mxdot(a_ref[...], b_ref[...], ca, cb)

        @pl.when(k == nk - 1)
        def _():
            r = acc[...]
            if add is not None:
                r = r + refs[2][...].astype(F32)
            o_ref[...] = r.astype(out_dtype)

    in_specs, args = [a_spec, b_spec], [a, b]
    if add is not None:
        in_specs.append(o_spec)
        args.append(add)
    return pl.pallas_call(
        body, name=name, grid=(m // tm, n // tn, nk), in_specs=in_specs, out_specs=o_spec,
        out_shape=jax.ShapeDtypeStruct((m, n), out_dtype), scratch_shapes=[pltpu.VMEM((tm, tn), F32)],
        compiler_params=_cparams(("parallel", "parallel", "arbitrary")))(*args)


def _rowwise(name, fn, *, nrows, tr, ncb=1, rows=(), fixed=(), vecs=(), tabs=(), outs=(), reds=(), tab_blocks=1):
    in_specs, args = [], []
    for arr, w, c0 in rows:
        in_specs.append(pl.BlockSpec((tr, w), lambda g, i, c0=c0: (i, c0 + g)))
        args.append(arr)
    for arr, w, c0 in fixed:
        in_specs.append(pl.BlockSpec((tr, w), lambda g, i, c0=c0: (i, c0)))
        args.append(arr)
    for arr, w, c0 in vecs:
        in_specs.append(pl.BlockSpec((1, w), lambda g, i, c0=c0: (0, c0 + g)))
        args.append(arr)
    for arr, w, c0 in tabs:
        in_specs.append(pl.BlockSpec((tr, w), lambda g, i, c0=c0: (i % tab_blocks, c0)))
        args.append(arr)
    out_shape = [jax.ShapeDtypeStruct((nrows, wt), dt) for wt, w, dt in outs]
    out_shape += [jax.ShapeDtypeStruct((1, wt), F32) for wt, w in reds]
    out_specs = [pl.BlockSpec((tr, w), lambda g, i: (i, g)) for wt, w, dt in outs]
    out_specs += [pl.BlockSpec((1, w), lambda g, i: (0, g)) for wt, w in reds]
    n_in, n_out = len(args), len(outs)

    def body(*refs):
        res = fn(*[r[...] for r in refs[:n_in]])
        for o_ref, val in zip(refs[n_in:n_in + n_out], res[:n_out]):
            o_ref[...] = val.astype(o_ref.dtype)
        i = pl.program_id(1)
        for d_ref, val in zip(refs[n_in + n_out:], res[n_out:]):
            @pl.when(i == 0)
            def _(d_ref=d_ref, val=val):
                d_ref[...] = val

            @pl.when(i > 0)
            def _(d_ref=d_ref, val=val):
                d_ref[...] += val

    res = pl.pallas_call(
        body, name=name, grid=(ncb, nrows // tr), in_specs=in_specs, out_specs=out_specs, out_shape=out_shape,
        compiler_params=_cparams(("parallel", "arbitrary")))(*args)
    return res


def _peer(k):
    x, y, c = lax.axis_index("x"), lax.axis_index("y"), lax.axis_index("c")
    px = jnp.where((k >> 2) & 1, 1 - x, x)
    py = jnp.where((k >> 1) & 1, 1 - y, y)
    pc = jnp.where(k & 1, 1 - c, c)
    return (px, py, pc), 4 * px + 2 * py + pc


def _my_index():
    return 4 * lax.axis_index("x") + 2 * lax.axis_index("y") + lax.axis_index("c")


def _exchange(name, ins, out_shapes, items):
    n_in, n_out, n_it = len(ins), len(out_shapes), len(items)

    def body(*refs):
        x, o = refs[:n_in], refs[n_in:n_in + n_out]
        send_sems, recv_sems, local_sems = refs[n_in + n_out:]
        me = _my_index()
        local, sends = [], []
        for t, (ii, io, src, dst) in enumerate(items):
            cp = pltpu.make_async_copy(src(x[ii], me), dst(o[io], me), local_sems.at[t])
            cp.start()
            local.append(cp)
        for k in range(1, N_DEV):
            dev, idx = _peer(k)
            for t, (ii, io, src, dst) in enumerate(items):
                s = (k - 1) * n_it + t
                cp = pltpu.make_async_remote_copy(
                    src_ref=src(x[ii], idx), dst_ref=dst(o[io], me), send_sem=send_sems.at[s],
                    recv_sem=recv_sems.at[s], device_id=dev, device_id_type=MESH_ID)
                cp.start()
                sends.append(cp)
        for k in range(1, N_DEV):
            dev, idx = _peer(k)
            for t, (ii, io, src, dst) in enumerate(items):
                s = (k - 1) * n_it + t
                pltpu.make_async_remote_copy(
                    src_ref=src(x[ii], idx), dst_ref=dst(o[io], idx), send_sem=send_sems.at[s],
                    recv_sem=recv_sems.at[s], device_id=dev, device_id_type=MESH_ID).wait_recv()
        for cp in sends:
            cp.wait_send()
        for cp in local:
            cp.wait()

    nsem = (N_DEV - 1) * n_it
    anyspec = pl.BlockSpec(memory_space=pl.ANY)
    return pl.pallas_call(
        body, name=name, out_shape=list(out_shapes), in_specs=[anyspec] * n_in, out_specs=[anyspec] * n_out,
        scratch_shapes=[pltpu.SemaphoreType.DMA((nsem,)), pltpu.SemaphoreType.DMA((nsem,)),
                        pltpu.SemaphoreType.DMA((n_it,))],
        compiler_params=pltpu.CompilerParams(has_side_effects=True))(*ins)


def _split_copies(x, land, send_sems, recv_sems, items, receive):
    me = _my_index()
    remote, n_it = [], len(items)
    for k in range(1, N_DEV):
        dev, idx = _peer(k)
        for t, (ii, io, src, dst) in enumerate(items):
            s = (k - 1) * n_it + t
            remote.append(pltpu.make_async_remote_copy(
                src_ref=src(x[ii], idx), dst_ref=dst(land[io], idx if receive else me), send_sem=send_sems.at[s],
                recv_sem=recv_sems.at[s], device_id=dev, device_id_type=MESH_ID))
    local = [pltpu.make_async_copy(src(x[ii], me), dst(land[io], me), send_sems.at[(N_DEV - 1) * n_it + t])
             for t, (ii, io, src, dst) in enumerate(items)]
    return remote, local


def _exchange_start(name, ins, out_shapes, items):
    n_in, n_out, n_it = len(ins), len(out_shapes), len(items)

    def body(*refs):
        x, land = refs[:n_in], refs[n_in:n_in + n_out]
        send_sems, recv_sems, token = refs[n_in + n_out], refs[n_in + n_out + 1], refs[-1]
        remote, local = _split_copies(x, land, send_sems, recv_sems, items, False)
        for cp in remote + local:
            cp.start()
        token[...] = jnp.zeros_like(token)

    hbm = pl.BlockSpec(memory_space=pltpu.HBM)
    sem = pl.BlockSpec(memory_space=pltpu.SEMAPHORE)
    arrs = [pltpu.with_memory_space_constraint(a, pltpu.HBM)
            for a in list(ins) + [lax.empty(s.shape, s.dtype) for s in out_shapes]]
    res = pl.pallas_call(
        body, name=name,
        out_shape=(pltpu.SemaphoreType.DMA((N_DEV * n_it,)), pltpu.SemaphoreType.DMA(((N_DEV - 1) * n_it,)),
                   *[pltpu.HBM(a.shape, a.dtype) for a in arrs], jax.ShapeDtypeStruct((8, LANE), F32)),
        in_specs=[hbm] * (n_in + n_out),
        out_specs=(sem, sem, *[hbm] * (n_in + n_out), pl.BlockSpec(memory_space=pltpu.VMEM)),
        input_output_aliases={i: 2 + i for i in range(n_in + n_out)},
        compiler_params=pltpu.CompilerParams(has_side_effects=pltpu.SideEffectType.DATAFLOW_SIDE_EFFECTING))(*arrs)
    return res[:2], res[2:2 + n_in], res[2 + n_in:2 + n_in + n_out], res[-1]


def _exchange_wait(name, sems, ins, landing, items, after):
    n_in, n_out = len(ins), len(landing)

    def body(*refs):
        x, land = refs[:n_in], refs[n_in:n_in + n_out]
        send_sems, recv_sems = refs[n_in + n_out], refs[n_in + n_out + 1]
        remote, local = _split_copies(x, land, send_sems, recv_sems, items, True)
        for cp in remote:
            cp.wait_send()
            cp.wait_recv()
        for cp in local:
            cp.wait()

    hbm = pl.BlockSpec(memory_space=pltpu.HBM)
    sem = pl.BlockSpec(memory_space=pltpu.SEMAPHORE)
    arrs = list(ins) + list(landing)
    res = pl.pallas_call(
        body, name=name, out_shape=tuple(pltpu.HBM(a.shape, a.dtype) for a in arrs),
        in_specs=[hbm] * (n_in + n_out) + [sem, sem, pl.BlockSpec(memory_space=pl.ANY)],
        out_specs=tuple([hbm] * (n_in + n_out)), input_output_aliases={i: i for i in range(n_in + n_out)},
        compiler_params=pltpu.CompilerParams(has_side_effects=pltpu.SideEffectType.DATAFLOW_SIDE_EFFECTING))(
            *arrs, *sems, after)
    return res[n_in:]


def _whole(ref, p):
    return ref


def _entry(ref, p):
    return ref.at[p]


def _gather_plan(a, b, kind):
    if kind == "col" and b % LANE == 0:
        return (a, N_DEV * b), (lambda ref, p: ref.at[:, pl.ds(pl.multiple_of(p * b, b), b)]), "col"
    return (N_DEV, a, b), _entry, ("row" if kind == "row" else "stack")


def _adamw_nat(name, parts, w, m, v):
    depth, b, c = w.shape
    assert len(parts) == depth
    tb = _pick(b, (128, 64, 32, 16, 8))
    spec = pl.BlockSpec((1, tb, c), lambda i, j: (i, j, 0))

    def body(*refs):
        p_refs = refs[:depth]
        w_ref, m_ref, v_ref, g_ref, d_ref, nm_ref, nv_ref = refs[depth:]
        for layer, p_ref in enumerate(p_refs):
            @pl.when(pl.program_id(0) == layer)
            def _(p_ref=p_ref):
                g = p_ref[0].astype(F32)
                for j in range(1, N_DEV):
                    g = g + p_ref[j].astype(F32)
                nm = ADAM_B1 * m_ref[0] + (1.0 - ADAM_B1) * g
                nv = ADAM_B2 * v_ref[0] + (1.0 - ADAM_B2) * jnp.square(g)
                m_hat = nm / (1.0 - ADAM_B1 ** ADAM_STEP)
                v_hat = nv / (1.0 - ADAM_B2 ** ADAM_STEP)
                g_ref[0] = g
                d_ref[0] = -ADAM_LR * (m_hat / (jnp.sqrt(v_hat) + ADAM_EPS) + ADAM_WD * w_ref[0])
                nm_ref[0] = nm
                nv_ref[0] = nv

    sds = jax.ShapeDtypeStruct((depth, b, c), F32)
    return pl.pallas_call(
        body, name=name, grid=(depth, b // tb),
        in_specs=[pl.BlockSpec((N_DEV, tb, c), lambda i, j: (0, j, 0))] * depth + [spec, spec, spec],
        out_specs=[spec] * 4, out_shape=[sds] * 4, compiler_params=_cparams(("parallel", "parallel")))(*parts, w, m, v)


def _adamw(name, parts, w, m, v):
    rows = w.shape[0]
    tr = _pick(rows, (256, 128, 64, 32, 16, 8))
    spec = pl.BlockSpec((tr, PACK_W), lambda i: (i, 0))

    def body(p_ref, w_ref, m_ref, v_ref, g_ref, d_ref, nm_ref, nv_ref):
        g = p_ref[0]
        for j in range(1, N_DEV):
            g = g + p_ref[j]
        nm = ADAM_B1 * m_ref[...] + (1.0 - ADAM_B1) * g
        nv = ADAM_B2 * v_ref[...] + (1.0 - ADAM_B2) * jnp.square(g)
        m_hat = nm / (1.0 - ADAM_B1 ** ADAM_STEP)
        v_hat = nv / (1.0 - ADAM_B2 ** ADAM_STEP)
        g_ref[...] = g
        d_ref[...] = -ADAM_LR * (m_hat / (jnp.sqrt(v_hat) + ADAM_EPS) + ADAM_WD * w_ref[...])
        nm_ref[...] = nm
        nv_ref[...] = nv

    sds = jax.ShapeDtypeStruct((rows, PACK_W), F32)
    return pl.pallas_call(
        body, name=name, grid=(rows // tr,),
        in_specs=[pl.BlockSpec((N_DEV, tr, PACK_W), lambda i: (0, i, 0)), spec, spec, spec],
        out_specs=[spec] * 4, out_shape=[sds] * 4, compiler_params=_cparams(("parallel",)))(parts, w, m, v)


def _pack(arrs, dtype, row_mult=16):
    flat = jnp.concatenate([a.reshape(-1).astype(dtype) for a in arrs])
    unit = row_mult * PACK_W
    total = -(-flat.shape[0] // unit) * unit
    flat = jnp.pad(flat, (0, total - flat.shape[0]))
    return flat.reshape(-1, PACK_W)


def _pack_lead(arrs, dtype, row_mult):
    flat = jnp.concatenate([a.reshape(N_DEV, -1).astype(dtype) for a in arrs], axis=1)
    unit = row_mult * PACK_W
    total = -(-flat.shape[1] // unit) * unit
    flat = jnp.pad(flat, ((0, 0), (0, total - flat.shape[1])))
    return flat.reshape(N_DEV, -1, PACK_W)


def _unpack(buf, shapes, lead=()):
    flat = buf.reshape(lead + (-1,))
    out, off = [], 0
    for s in shapes:
        n = int(np.prod(s))
        out.append(flat[..., off:off + n].reshape(lead + tuple(s)))
        off += n
    return out


def _unshard(g, kind):
    if kind == "col":
        g = jnp.moveaxis(g, 0, -2)
        return g.reshape(g.shape[:-2] + (g.shape[-2] * g.shape[-1],))
    g = jnp.moveaxis(g, 0, 1)
    return g.reshape((g.shape[0], g.shape[1] * g.shape[2]) + g.shape[3:])


def _shard(full, kind):
    if kind == "col":
        s = full.reshape(full.shape[:-1] + (N_DEV, full.shape[-1] // N_DEV))
        return jnp.moveaxis(s, -2, 0)
    s = full.reshape((full.shape[0], N_DEV, full.shape[1] // N_DEV) + full.shape[2:])
    return jnp.moveaxis(s, 1, 0)


class _Geo:
    def __init__(self, bsz, seq):
        self.bsz, self.seq = bsz, seq
        self.pad = (-(N_META + seq)) % ATT_BLK
        self.lp = self.pad + N_META + seq
        assert (self.pad + N_META) % SSM_CHUNK == 0 and self.lp % SSM_CHUNK == 0
        self.nrows = bsz * self.lp
        self.nc = self.lp // SSM_CHUNK
        self.nh = SSM_D_INNER // SSM_HEAD_DIM
        self.gn = SSM_GROUPS * SSM_STATE
        self.cd = SSM_D_INNER + 2 * self.gn
        self.hq = MLA_HEADS * LANE
        order = (("z", SSM_D_INNER), ("xs", SSM_D_INNER), ("g_ssm", D_MODEL), ("g_mla", D_MODEL), ("bm", self.gn),
                 ("cm", self.gn), ("c_q", MLA_Q_LORA), ("c_kv", MLA_KV_LORA), ("dt", LANE), ("k_rope", LANE))
        self.col, off = {}, 0
        for nm, w in order:
            assert off % w == 0, (nm, off, w)
            self.col[nm] = (off, w)
            off += w
        self.pw = off
        assert self.nh <= LANE and MLA_ROPE == 64 and MLA_NOPE == LANE and MLA_V == LANE
        self.tr = _pick(self.lp, (768, 512, 384, 256, 128))
        self.tr_wide = _pick(self.lp, (384, 256, 128))

    def cb(self, nm):
        off, w = self.col[nm]
        return off // w

    def w_in_runs(self, shard_w):
        nh, half = self.nh, MLA_ROPE // 2
        src, pieces = 0, []
        for nm, n in (("z", SSM_D_INNER), ("xs", SSM_D_INNER), ("bm", self.gn), ("cm", self.gn), ("dt", nh),
                      ("c_q", MLA_Q_LORA), ("c_kv", MLA_KV_LORA), ("k_rope", MLA_ROPE), ("g_ssm", D_MODEL),
                      ("g_mla", D_MODEL)):
            dst = self.col[nm][0]
            if nm == "k_rope":
                pieces += [(src, half, dst), (src + half, half, dst + 2 * half)]
            else:
                pieces.append((src, n, dst))
            src += n
        assert src == shard_w * N_DEV
        runs = []
        for a, n, dst in pieces:
            for j in range(N_DEV):
                lo, hi = max(a, j * shard_w), min(a + n, (j + 1) * shard_w)
                if lo < hi:
                    runs.append((j, lo - j * shard_w, hi - lo, dst + lo - a))
        return runs


def _slot(a):
    h = MLA_ROPE // 2
    z = jnp.zeros(a.shape[:-1] + (h,), a.dtype)
    return jnp.concatenate([a[..., :h], z, a[..., h:], z], axis=-1)


def _unslot(a):
    h = MLA_ROPE // 2
    return jnp.concatenate([a[..., :h], a[..., 2 * h:3 * h]], axis=-1)


def _prep_layer(geo, wl):
    nh = geo.nh
    p = {}
    if "w_uq" in wl:
        uq = wl["w_uq"].reshape(MLA_Q_LORA, MLA_HEADS, MLA_NOPE + MLA_ROPE)
        p["w_qn"] = uq[..., :MLA_NOPE].reshape(MLA_Q_LORA, geo.hq)
        p["w_qp"] = _slot(uq[..., MLA_NOPE:]).reshape(MLA_Q_LORA, geo.hq)
    if "w_ukv" in wl:
        ukv = wl["w_ukv"].reshape(MLA_KV_LORA, MLA_HEADS, MLA_NOPE + MLA_V)
        p["w_k"] = ukv[..., :MLA_NOPE].reshape(MLA_KV_LORA, geo.hq)
        p["w_v"] = ukv[..., MLA_NOPE:].reshape(MLA_KV_LORA, geo.hq)
    for nm in ("w_in_p", "conv_w", "w_branch_ssm", "w_branch_mla", "w_out", "w_mlp_up", "w_mlp_down"):
        if nm in wl:
            p[nm] = wl[nm]
    for nm in ("norm_mix_w", "conv_b", "ssm_norm_w", "q_norm_w", "kv_norm_w", "norm_mlp_w"):
        if nm in wl:
            p[nm] = wl[nm].reshape(1, -1)
    if "dt_bias" in wl:
        p["dt_bias"] = jnp.pad(wl["dt_bias"], (0, LANE - nh)).reshape(1, LANE)
        p["a_log"] = jnp.pad(wl["a_log"], (0, LANE - nh)).reshape(1, LANE)
        p["d_skip_full"] = jnp.repeat(wl["d_skip"], SSM_HEAD_DIM).reshape(1, SSM_D_INNER)
    return p


def _unprep_grads(geo, g):
    nh = geo.nh
    out = {}
    if "w_qn" in g:
        qn = g["w_qn"].reshape(MLA_Q_LORA, MLA_HEADS, MLA_NOPE)
        qp = _unslot(g["w_qp"].reshape(MLA_Q_LORA, MLA_HEADS, LANE))
        out["w_uq"] = jnp.concatenate([qn, qp], axis=-1).reshape(MLA_Q_LORA, -1)
    if "w_k" in g:
        wk = g["w_k"].reshape(MLA_KV_LORA, MLA_HEADS, MLA_NOPE)
        wv = g["w_v"].reshape(MLA_KV_LORA, MLA_HEADS, MLA_V)
        out["w_ukv"] = jnp.concatenate([wk, wv], axis=-1).reshape(MLA_KV_LORA, -1)
    for nm in ("w_in_p", "w_branch_ssm", "w_branch_mla", "w_out", "w_mlp_up", "w_mlp_down", "conv_w"):
        if nm in g:
            out[nm] = g[nm]
    for nm in ("norm_mix_w", "conv_b", "ssm_norm_w", "q_norm_w", "kv_norm_w", "norm_mlp_w"):
        if nm in g:
            out[nm] = g[nm].reshape(-1)
    if "dt_bias" in g:
        out["dt_bias"] = g["dt_bias"].reshape(-1)[:nh]
        out["a_log"] = g["a_log"].reshape(-1)[:nh]
        out["d_skip"] = g["d_skip_full"].reshape(nh, SSM_HEAD_DIM).sum(-1)
    return out


def _tables(geo):
    pos = jnp.arange(geo.lp, dtype=F32) - geo.pad
    inv = ROPE_THETA ** (-jnp.arange(0, MLA_ROPE, 2, dtype=F32) / MLA_ROPE)
    ang = pos[:, None] * inv[None, :]
    cos, sin = jnp.cos(ang), jnp.sin(ang)
    z = jnp.zeros_like(cos)
    rows = jnp.arange(geo.lp)[:, None]
    return {"cos": jnp.concatenate([cos, z, cos, z], axis=-1), "sin": jnp.concatenate([-sin, z, sin, z], axis=-1),
            "valid": (rows >= geo.pad).astype(F32), "token": (rows >= geo.pad + N_META).astype(F32)}


def _w_in_assemble(geo, gathered):
    _, d, sw = gathered.shape
    runs = geo.w_in_runs(sw)
    tr = _pick(d, (256, 128))

    def body(x_ref, o_ref):
        o_ref[...] = jnp.zeros_like(o_ref)
        for j, s0, n, d0 in runs:
            o_ref[:, d0:d0 + n] = x_ref[j, :, s0:s0 + n]

    return pl.pallas_call(
        body, name="w_in_assemble", grid=(d // tr,), in_specs=[pl.BlockSpec((N_DEV, tr, sw), lambda i: (0, i, 0))],
        out_specs=pl.BlockSpec((tr, geo.pw), lambda i: (i, 0)),
        out_shape=jax.ShapeDtypeStruct((d, geo.pw), gathered.dtype), compiler_params=_cparams(("parallel",)))(gathered)


def _w_in_split(geo, g_padded, sw):
    d = g_padded.shape[0]
    runs = geo.w_in_runs(sw)
    tr = _pick(d, (128,))

    def body(x_ref, o_ref):
        for j, s0, n, d0 in runs:
            o_ref[j, :, s0:s0 + n] = x_ref[:, d0:d0 + n]

    return pl.pallas_call(
        body, name="w_in_split", grid=(d // tr,), in_specs=[pl.BlockSpec((tr, geo.pw), lambda i: (i, 0))],
        out_specs=pl.BlockSpec((N_DEV, tr, sw), lambda i: (0, i, 0)),
        out_shape=jax.ShapeDtypeStruct((N_DEV, d, sw), g_padded.dtype),
        compiler_params=_cparams(("parallel",)))(g_padded)


def _conv_cols(geo, cbw):
    nx = SSM_D_INNER // cbw
    x0, b0 = geo.col["xs"][0] // cbw, geo.col["bm"][0] // cbw
    assert geo.col["cm"][0] == geo.col["bm"][0] + geo.gn
    return lambda j: jnp.where(j < nx, x0 + j, b0 + j - nx)


def _conv_pre(x, w_ref, b_ref):
    acc = b_ref[...] + x * w_ref[SSM_CONV - 1:SSM_CONV, :]
    for k in range(SSM_CONV - 1):
        acc = acc + pltpu.roll(x, SSM_CONV - 1 - k, axis=0) * w_ref[k:k + 1, :]
    return acc


def _conv_fwd(geo, proj, conv_w, conv_b):
    cbw = 256
    colmap = _conv_cols(geo, cbw)
    lp, pad = geo.lp, geo.pad

    def body(x_ref, w_ref, b_ref, o_ref):
        valid = (lax.broadcasted_iota(jnp.int32, (lp, 1), 0) >= pad).astype(F32)
        o_ref[...] = _silu(_conv_pre(x_ref[...], w_ref, b_ref)) * valid

    return pl.pallas_call(
        body, name="conv_fwd", grid=(geo.bsz, geo.cd // cbw),
        in_specs=[pl.BlockSpec((lp, cbw), lambda b, j: (b, colmap(j))),
                  pl.BlockSpec((SSM_CONV, cbw), lambda b, j: (0, j)), pl.BlockSpec((1, cbw), lambda b, j: (0, j))],
        out_specs=pl.BlockSpec((lp, cbw), lambda b, j: (b, j)),
        out_shape=jax.ShapeDtypeStruct((geo.nrows, geo.cd), F32),
        compiler_params=_cparams(("parallel", "parallel")))(proj, conv_w, conv_b)


def _conv_bwd(geo, proj, conv_w, conv_b, dxc):
    cbw = 256
    colmap = _conv_cols(geo, cbw)
    lp, pad = geo.lp, geo.pad

    def body(x_ref, w_ref, b_ref, dy_ref, dx_ref, gw_ref, gb_ref):
        b = pl.program_id(1)
        valid = (lax.broadcasted_iota(jnp.int32, (lp, 1), 0) >= pad).astype(F32)
        x = x_ref[...]
        pre = _conv_pre(x, w_ref, b_ref)
        sig = _sigmoid(pre)
        dpre = dy_ref[...] * (sig * (1.0 + pre * (1.0 - sig))) * valid
        dx = dpre * w_ref[SSM_CONV - 1:SSM_CONV, :]
        gws = [jnp.sum(dpre * x, axis=0, keepdims=True)]
        for k in range(SSM_CONV - 2, -1, -1):
            s = SSM_CONV - 1 - k
            dx = dx + pltpu.roll(dpre, lp - s, axis=0) * w_ref[k:k + 1, :]
            gws.insert(0, jnp.sum(dpre * pltpu.roll(x, s, axis=0), axis=0, keepdims=True))
        dx_ref[...] = (dx * valid).astype(dx_ref.dtype)

        @pl.when(b == 0)
        def _():
            gw_ref[...] = jnp.zeros_like(gw_ref)
            gb_ref[...] = jnp.zeros_like(gb_ref)

        for k in range(SSM_CONV):
            gw_ref[k:k + 1, :] += gws[k]
        gb_ref[...] += jnp.sum(dpre, axis=0, keepdims=True)

    return pl.pallas_call(
        body, name="conv_bwd", grid=(geo.cd // cbw, geo.bsz),
        in_specs=[pl.BlockSpec((lp, cbw), lambda j, b: (b, colmap(j))),
                  pl.BlockSpec((SSM_CONV, cbw), lambda j, b: (0, j)), pl.BlockSpec((1, cbw), lambda j, b: (0, j)),
                  pl.BlockSpec((lp, cbw), lambda j, b: (b, j))],
        out_specs=[pl.BlockSpec((lp, cbw), lambda j, b: (b, j)), pl.BlockSpec((SSM_CONV, cbw), lambda j, b: (0, j)),
                   pl.BlockSpec((1, cbw), lambda j, b: (0, j))],
        out_shape=[jax.ShapeDtypeStruct((geo.nrows, geo.cd), MXU_DTYPE),
                   jax.ShapeDtypeStruct((SSM_CONV, geo.cd), F32), jax.ShapeDtypeStruct((1, geo.cd), F32)],
        compiler_params=_cparams(("parallel", "arbitrary")))(proj, conv_w, conv_b, dxc)


def _tri(q):
    r = lax.broadcasted_iota(jnp.int32, (q, q), 0)
    c = lax.broadcasted_iota(jnp.int32, (q, q), 1)
    return r >= c


def _ssd_pre(dtr, dtb, alog, valid):
    dt = _softplus(dtr + dtb) * valid
    adt = dt * (-jnp.exp(alog))
    a_cs = _dot(_tri(SSM_CHUNK).astype(F32), adt, 1, 0, precision=lax.Precision.HIGHEST)
    return dt, a_cs


def _ssd_specs(geo, rev):
    nc, q = geo.nc, SSM_CHUNK
    ci = (lambda c: nc - 1 - c) if rev else (lambda c: c)
    nxb = SSM_D_INNER // geo.gn
    return [pl.BlockSpec((q, SSM_D_INNER), lambda b, c: (b * nc + ci(c), 0)),
            pl.BlockSpec((q, geo.gn), lambda b, c: (b * nc + ci(c), nxb)),
            pl.BlockSpec((q, geo.gn), lambda b, c: (b * nc + ci(c), nxb + 1)),
            pl.BlockSpec((q, LANE), lambda b, c: (b * nc + ci(c), geo.cb("dt"))),
            pl.BlockSpec((1, LANE), lambda b, c: (0, 0)), pl.BlockSpec((1, LANE), lambda b, c: (0, 0))], ci


def _ssd_fwd(geo, xc, proj, dt_bias, a_log):
    q, p, n, e = SSM_CHUNK, SSM_HEAD_DIM, SSM_STATE, geo.nh // SSM_GROUPS
    nc, pad = geo.nc, geo.pad
    in_specs, _ = _ssd_specs(geo, False)

    def body(xs_ref, b_ref, c_ref, dtr_ref, dtb_ref, alog_ref, y_ref, sp_ref, state):
        c = pl.program_id(1)

        @pl.when(c == 0)
        def _():
            state[...] = jnp.zeros_like(state)

        sp_ref[...] = state[...]
        valid = (c * q + lax.broadcasted_iota(jnp.int32, (q, 1), 0) >= pad).astype(F32)
        dt, a_cs = _ssd_pre(dtr_ref[...], dtb_ref[...], alog_ref[...], valid)
        a_cst = a_cs.T
        tri = _tri(q)
        for g in range(SSM_GROUPS):
            bg, cg = b_ref[:, g * n:(g + 1) * n], c_ref[:, g * n:(g + 1) * n]
            cb = _mxdot(cg, bg, 1, 1)
            for hh in range(e):
                h = g * e + hh
                a_col, a_row, a_last = a_cs[:, h:h + 1], a_cst[h:h + 1, :], a_cs[q - 1:q, h:h + 1]
                xdt = xs_ref[:, h * p:(h + 1) * p] * dt[:, h:h + 1]
                ldec = jnp.exp(jnp.where(tri, a_col - a_row, -jnp.inf))
                s_prev = state[h * p:(h + 1) * p, :]
                y = _mxdot(cb * ldec, xdt, 1, 0) + _mxdot(cg, s_prev, 1, 1) * jnp.exp(a_col)
                y_ref[:, h * p:(h + 1) * p] = y
                st = _mxdot(xdt, bg * jnp.exp(a_last - a_col), 0, 0)
                state[h * p:(h + 1) * p, :] = s_prev * jnp.exp(a_last) + st

    return pl.pallas_call(
        body, name="ssd_fwd", grid=(geo.bsz, nc), in_specs=in_specs,
        out_specs=[pl.BlockSpec((q, SSM_D_INNER), lambda b, c: (b * nc + c, 0)),
                   pl.BlockSpec((SSM_D_INNER, n), lambda b, c: (b * nc + c, 0))],
        out_shape=[jax.ShapeDtypeStruct((geo.nrows, SSM_D_INNER), F32),
                   jax.ShapeDtypeStruct((geo.bsz * nc * SSM_D_INNER, n), F32)],
        scratch_shapes=[pltpu.VMEM((SSM_D_INNER, n), F32)],
        compiler_params=_cparams(("parallel", "arbitrary")))(xc, xc, xc, proj, dt_bias, a_log)


def _ssd_bwd(geo, xc, proj, dt_bias, a_log, s_prev_all, dy, dxs_skip):
    q, p, n, e = SSM_CHUNK, SSM_HEAD_DIM, SSM_STATE, geo.nh // SSM_GROUPS
    nc, pad, di, gn = geo.nc, geo.pad, SSM_D_INNER, geo.gn
    in_specs, ci = _ssd_specs(geo, True)
    row_spec = pl.BlockSpec((q, di), lambda b, c: (b * nc + ci(c), 0))
    in_specs += [pl.BlockSpec((di, n), lambda b, c: (b * nc + ci(c), 0)), row_spec, row_spec]

    def body(xs_ref, b_ref, c_ref, dtr_ref, dtb_ref, alog_ref, sp_ref, dy_ref, dsk_ref,
             dxc_ref, ddt_ref, gdtb_ref, galog_ref, dstate):
        step = pl.program_id(1)
        first = jnp.logical_and(pl.program_id(0) == 0, step == 0)
        c = nc - 1 - step

        @pl.when(step == 0)
        def _():
            dstate[...] = jnp.zeros_like(dstate)

        valid = (c * q + lax.broadcasted_iota(jnp.int32, (q, 1), 0) >= pad).astype(F32)
        dtr, dtb, alog = dtr_ref[...], dtb_ref[...], alog_ref[...]
        dt, a_cs = _ssd_pre(dtr, dtb, alog, valid)
        a_cst = a_cs.T
        tri = _tri(q)
        lane = lax.broadcasted_iota(jnp.int32, (1, LANE), 1)
        sub = lax.broadcasted_iota(jnp.int32, (LANE, 1), 0)
        d_dt = jnp.zeros((q, LANE), F32)
        d_acs = jnp.zeros((q, LANE), F32)
        d_acst = jnp.zeros((LANE, q), F32)
        d_last = jnp.zeros((1, LANE), F32)
        for g in range(SSM_GROUPS):
            bg, cg = b_ref[:, g * n:(g + 1) * n], c_ref[:, g * n:(g + 1) * n]
            cb = _mxdot(cg, bg, 1, 1)
            d_cb = jnp.zeros((q, q), F32)
            d_bg = jnp.zeros((q, n), F32)
            d_cg = jnp.zeros((q, n), F32)
            for hh in range(e):
                h = g * e + hh
                hs = slice(h * p, (h + 1) * p)
                a_col, a_row, a_last = a_cs[:, h:h + 1], a_cst[h:h + 1, :], a_cs[q - 1:q, h:h + 1]
                x = xs_ref[:, hs]
                dt_col = dt[:, h:h + 1]
                xdt = x * dt_col
                ldec = jnp.exp(jnp.where(tri, a_col - a_row, -jnp.inf))
                s_prev = sp_ref[hs, :]
                d_snew = dstate[hs, :]
                dyh = dy_ref[:, hs]
                e_col, e_last = jnp.exp(a_col), jnp.exp(a_last)
                dec = jnp.exp(a_last - a_col)
                d_m = _mxdot(dyh, xdt, 1, 1)
                d_xdt = _mxdot(cb * ldec, dyh, 0, 0)
                d_cb = d_cb + d_m * ldec
                d_diff = d_m * cb * ldec
                da_col = jnp.sum(d_diff, axis=1, keepdims=True)
                da_row = -jnp.sum(d_diff, axis=0, keepdims=True)
                cs = _mxdot(cg, s_prev, 1, 1)
                d_cs = dyh * e_col
                da_col = da_col + jnp.sum(dyh * cs, axis=1, keepdims=True) * e_col
                d_cg = d_cg + _mxdot(d_cs, s_prev, 1, 0)
                d_sprev = _mxdot(d_cs, cg, 0, 0) + d_snew * e_last
                dl = jnp.sum(jnp.sum(d_snew * s_prev, axis=1, keepdims=True), axis=0, keepdims=True) * e_last
                d_xdt = d_xdt + _mxdot(bg * dec, d_snew, 1, 1)
                d_bd = _mxdot(xdt, d_snew, 1, 0)
                d_bg = d_bg + d_bd * dec
                d_dec = jnp.sum(d_bd * bg, axis=1, keepdims=True) * dec
                dl = dl + jnp.sum(d_dec, axis=0, keepdims=True)
                da_col = da_col - d_dec
                dstate[hs, :] = d_sprev
                dxc_ref[:, hs] = d_xdt * dt_col + dsk_ref[:, hs]
                onehot = (lane == h).astype(F32)
                d_dt = d_dt + jnp.sum(d_xdt * x, axis=1, keepdims=True) * onehot
                d_acs = d_acs + da_col * onehot
                d_acst = d_acst + (sub == h).astype(F32) * da_row
                d_last = d_last + dl * onehot
            dxc_ref[:, di + g * n:di + (g + 1) * n] = d_bg + _mxdot(d_cb, cg, 0, 0)
            dxc_ref[:, di + gn + g * n:di + gn + (g + 1) * n] = d_cg + _mxdot(d_cb, bg, 1, 0)
        is_last = (lax.broadcasted_iota(jnp.int32, (q, 1), 0) == q - 1).astype(F32)
        d_acs = d_acs + d_acst.T + is_last * d_last
        d_adt = _dot(_tri(q).astype(F32), d_acs, 0, 0, precision=lax.Precision.HIGHEST)
        a = -jnp.exp(alog)
        d_dt = d_dt + d_adt * a
        g_alog = jnp.sum(d_adt * dt, axis=0, keepdims=True) * a
        d_dtr = d_dt * valid * _sigmoid(dtr + dtb)
        ddt_ref[...] = d_dtr.astype(ddt_ref.dtype)
        g_dtb = jnp.sum(d_dtr, axis=0, keepdims=True)

        @pl.when(first)
        def _():
            gdtb_ref[...] = g_dtb
            galog_ref[...] = g_alog

        @pl.when(jnp.logical_not(first))
        def _():
            gdtb_ref[...] += g_dtb
            galog_ref[...] += g_alog

    vec = pl.BlockSpec((1, LANE), lambda b, c: (0, 0))
    return pl.pallas_call(
        body, name="ssd_bwd", grid=(geo.bsz, nc), in_specs=in_specs,
        out_specs=[pl.BlockSpec((q, geo.cd), lambda b, c: (b * nc + ci(c), 0)),
                   pl.BlockSpec((q, LANE), lambda b, c: (b * nc + ci(c), 0)), vec, vec],
        out_shape=[jax.ShapeDtypeStruct((geo.nrows, geo.cd), F32), jax.ShapeDtypeStruct((geo.nrows, LANE), MXU_DTYPE),
                   jax.ShapeDtypeStruct((1, LANE), F32), jax.ShapeDtypeStruct((1, LANE), F32)],
        scratch_shapes=[pltpu.VMEM((di, n), F32)],
        compiler_params=_cparams(("arbitrary", "arbitrary")))(xc, xc, xc, proj, dt_bias, a_log, s_prev_all, dy, dxs_skip)


BIAS_LANE = MLA_ROPE // 2
KEY_OFF = -1e30
ATT_SCALE = (MLA_NOPE + MLA_ROPE) ** -0.5


def _row_t(col):
    return jnp.broadcast_to(col, (col.shape[0], LANE)).T[:8]


def _attn_fwd2(geo, qc, kc, v):
    t, lp = ATT_BLK, geo.lp
    nb = lp // t

    def body(q_ref, k_ref, v_ref, o_ref, lse_ref):
        qi = pl.program_id(2)
        q = q_ref[...]

        def blk(kj, carry, diag):
            m, l, acc = carry
            ks = pl.ds(pl.multiple_of(kj * t, t), t)
            s = _mxdot(q, k_ref[ks, :], 1, 1) * ATT_SCALE
            if diag:
                s = jnp.where(_tri(t), s, -jnp.inf)
            m_new = jnp.maximum(m, jnp.max(s, axis=1, keepdims=True))
            pr = jnp.exp(s - m_new)
            alpha = jnp.exp(m - m_new)
            return m_new, alpha * l + jnp.sum(pr, axis=1, keepdims=True), alpha * acc + _mxdot(pr, v_ref[ks, :], 1, 0)

        init = (jnp.full((t, 1), 2.0 * KEY_OFF, F32), jnp.zeros((t, 1), F32), jnp.zeros((t, LANE), F32))
        carry = lax.fori_loop(0, qi, lambda kj, c: blk(kj, c, False), init)
        m, l, acc = blk(qi, carry, True)
        o_ref[...] = acc / l
        lse_ref[0, 0, 0] = _row_t(m + jnp.log(l))

    return pl.pallas_call(
        body, name="attn_fwd", grid=(geo.bsz, MLA_HEADS, nb),
        in_specs=[pl.BlockSpec((t, 2 * LANE), lambda b, h, i: (b * nb + i, h)),
                  pl.BlockSpec((lp, 2 * LANE), lambda b, h, i: (b, h)), pl.BlockSpec((lp, LANE), lambda b, h, i: (b, h))],
        out_specs=[pl.BlockSpec((t, LANE), lambda b, h, i: (b * nb + i, h)),
                   pl.BlockSpec((1, 1, 1, 8, t), lambda b, h, i: (b, h, i, 0, 0))],
        out_shape=[jax.ShapeDtypeStruct((geo.nrows, geo.hq), F32),
                   jax.ShapeDtypeStruct((geo.bsz, MLA_HEADS, nb, 8, t), F32)],
        compiler_params=_cparams(("parallel", "parallel", "arbitrary")))(qc, kc, v)


def _attn_delta(geo, d_o, o):
    t, nb = ATT_BLK, geo.lp // ATT_BLK

    def body(do_ref, o_ref, dl_ref):
        dl_ref[0, 0, 0] = _row_t(jnp.sum(do_ref[...] * o_ref[...], axis=1, keepdims=True))

    spec = pl.BlockSpec((t, LANE), lambda b, h, i: (b * nb + i, h))
    return pl.pallas_call(
        body, name="attn_delta", grid=(geo.bsz, MLA_HEADS, nb), in_specs=[spec, spec],
        out_specs=pl.BlockSpec((1, 1, 1, 8, t), lambda b, h, i: (b, h, i, 0, 0)),
        out_shape=jax.ShapeDtypeStruct((geo.bsz, MLA_HEADS, nb, 8, t), F32),
        compiler_params=_cparams(("parallel", "parallel", "parallel")))(d_o, o)


def _attn_bwd2(geo, qc, kc, v, d_o, lse, delta):
    t, lp = ATT_BLK, geo.lp
    nb = lp // t

    def body(q_ref, k_ref, v_ref, do_ref, lse_ref, dl_ref, dq_ref, dk_ref, dv_ref):
        kj = pl.program_id(2)

        @pl.when(kj == 0)
        def _():
            dq_ref[...] = jnp.zeros_like(dq_ref)

        k, vv = k_ref[...], v_ref[...]

        def blk(qi, carry, diag):
            dk, dv = carry
            qs = pl.ds(pl.multiple_of(qi * t, t), t)
            q, d_o_blk = q_ref[qs, :], do_ref[qs, :]
            st = _mxdot(k, q, 1, 1) * ATT_SCALE
            if diag:
                keys = lax.broadcasted_iota(jnp.int32, (t, t), 0)
                st = jnp.where(keys <= lax.broadcasted_iota(jnp.int32, (t, t), 1), st, -jnp.inf)
            pt = jnp.exp(st - lse_ref[0, 0, qi][:1, :])
            dst = pt * (_mxdot(vv, d_o_blk, 1, 1) - dl_ref[0, 0, qi][:1, :]) * ATT_SCALE
            dq_ref[qs, :] += _mxdot(dst, k, 0, 0)
            return dk + _mxdot(dst, q, 1, 0), dv + _mxdot(pt, d_o_blk, 1, 0)

        carry = blk(kj, (jnp.zeros((t, 2 * LANE), F32), jnp.zeros((t, LANE), F32)), True)
        dk, dv = lax.fori_loop(kj + 1, nb, lambda qi, c: blk(qi, c, False), carry)
        dk_ref[...] = dk
        dv_ref[...] = dv.astype(dv_ref.dtype)

    rows = pl.BlockSpec((1, 1, nb, 8, t), lambda b, h, j: (b, h, 0, 0, 0))
    return pl.pallas_call(
        body, name="attn_bwd", grid=(geo.bsz, MLA_HEADS, nb),
        in_specs=[pl.BlockSpec((lp, 2 * LANE), lambda b, h, j: (b, h)),
                  pl.BlockSpec((t, 2 * LANE), lambda b, h, j: (b * nb + j, h)),
                  pl.BlockSpec((t, LANE), lambda b, h, j: (b * nb + j, h)),
                  pl.BlockSpec((lp, LANE), lambda b, h, j: (b, h)), rows, rows],
        out_specs=[pl.BlockSpec((lp, 2 * LANE), lambda b, h, j: (b, h)),
                   pl.BlockSpec((t, 2 * LANE), lambda b, h, j: (b * nb + j, h)),
                   pl.BlockSpec((t, LANE), lambda b, h, j: (b * nb + j, h))],
        out_shape=[jax.ShapeDtypeStruct((geo.nrows, 2 * geo.hq), F32), jax.ShapeDtypeStruct((geo.nrows, 2 * geo.hq), F32),
                   jax.ShapeDtypeStruct((geo.nrows, geo.hq), MXU_DTYPE)],
        compiler_params=_cparams(("parallel", "parallel", "arbitrary")))(qc, kc, v, d_o, lse, delta)


def _rope(x, cos, sin):
    return x * cos + pltpu.roll(x, LANE // 2, axis=1) * sin


def _rope_t(dx, cos, sin):
    return dx * cos + pltpu.roll(dx * sin, LANE // 2, axis=1)


def _layer_fwd(geo, h, w, tab, late=None):
    nr, tr, trw = geo.nrows, geo.tr, geo.tr_wide
    tb = geo.lp // tr
    rw = functools.partial(_rowwise, nrows=nr)
    s = {"h": h}
    (s["u"],) = rw("rms_mix", lambda x, g: (_rms(x, g),), tr=tr, rows=[(h, D_MODEL, 0)],
                   vecs=[(w["norm_mix_w"], D_MODEL, 0)], outs=[(D_MODEL, D_MODEL, MXU_DTYPE)])
    proj = s["proj"] = _mm("mm_in", s["u"], w["w_in_p"])
    xc = s["xc"] = _conv_fwd(geo, proj, w["conv_w"], w["conv_b"])
    s["y_ssd"], s["s_prev"] = _ssd_fwd(geo, xc, proj, w["dt_bias"], w["a_log"])
    gw = SSM_D_INNER // SSM_GROUPS

    def gate_norm(y, x, z, dsk, nw):
        return (_rms((y + x * dsk) * _silu(z), nw),)

    (s["y_ssm"],) = rw("ssm_gate_norm", gate_norm, tr=tr, ncb=SSM_GROUPS,
                       rows=[(s["y_ssd"], gw, 0), (xc, gw, 0), (proj, gw, geo.col["z"][0] // gw)],
                       vecs=[(w["d_skip_full"], gw, 0), (w["ssm_norm_w"], gw, 0)], outs=[(SSM_D_INNER, gw, MXU_DTYPE)])
    if late is not None:
        w = {**w, **late(s["y_ssm"])}
    (s["cq_n"],) = rw("rms_q", lambda x, g: (_rms(x, g),), tr=tr, rows=[(proj, MLA_Q_LORA, geo.cb("c_q"))],
                      vecs=[(w["q_norm_w"], MLA_Q_LORA, 0)], outs=[(MLA_Q_LORA, MLA_Q_LORA, MXU_DTYPE)])
    (s["ckv_n"],) = rw("rms_kv", lambda x, g: (_rms(x, g),), tr=tr, rows=[(proj, MLA_KV_LORA, geo.cb("c_kv"))],
                       vecs=[(w["kv_norm_w"], MLA_KV_LORA, 0)], outs=[(MLA_KV_LORA, MLA_KV_LORA, MXU_DTYPE)])
    qn = _mm("mm_qn", s["cq_n"], w["w_qn"])
    qp_raw = _mm("mm_qp", s["cq_n"], w["w_qp"])
    kn = _mm("mm_kn", s["ckv_n"], w["w_k"])
    s["v"] = _mm("mm_v", s["ckv_n"], w["w_v"], out_dtype=MXU_DTYPE)
    bias_lane = lambda: lax.broadcasted_iota(jnp.int32, (1, LANE), 1) == BIAS_LANE

    def q_cat(x, xp, c, sn):
        return (jnp.concatenate([x, jnp.where(bias_lane(), 1.0, _rope(xp, c, sn))], axis=1),)

    def k_cat(x, xp, c, sn, valid):
        return (jnp.concatenate([x, jnp.where(bias_lane(), KEY_OFF * (1.0 - valid), _rope(xp, c, sn))], axis=1),)

    rope_tabs = [(tab["cos"], LANE, 0), (tab["sin"], LANE, 0)]
    (s["qc"],) = rw("rope_q", q_cat, tr=tr, ncb=MLA_HEADS, rows=[(qn, LANE, 0), (qp_raw, LANE, 0)], tabs=rope_tabs,
                    outs=[(2 * geo.hq, 2 * LANE, MXU_DTYPE)], tab_blocks=tb)
    (s["kc"],) = rw("rope_k", k_cat, tr=tr, ncb=MLA_HEADS, rows=[(kn, LANE, 0)], fixed=[(proj, LANE, geo.cb("k_rope"))],
                    tabs=rope_tabs + [(tab["valid"], 1, 0)], outs=[(2 * geo.hq, 2 * LANE, MXU_DTYPE)], tab_blocks=tb)
    s["o"], s["lse"] = _attn_fwd2(geo, s["qc"], s["kc"], s["v"])
    s["ys_p"] = _mm("mm_bs", s["y_ssm"], w["w_branch_ssm"])
    s["ym_p"] = _mm("mm_bm", s["o"], w["w_branch_mla"])

    def gate(gs, gm, ys, ym):
        return (_sigmoid(gs) * ys + _sigmoid(gm) * ym,)

    (s["mixed"],) = rw("gate", gate, tr=tr, rows=[(proj, D_MODEL, geo.cb("g_ssm")), (proj, D_MODEL, geo.cb("g_mla")),
                                                  (s["ys_p"], D_MODEL, 0), (s["ym_p"], D_MODEL, 0)],
                       outs=[(D_MODEL, D_MODEL, MXU_DTYPE)])
    s["h2"] = _mm("mm_out", s["mixed"], w["w_out"], add=h)
    (s["vn"],) = rw("rms_mlp", lambda x, g: (_rms(x, g),), tr=tr, rows=[(s["h2"], D_MODEL, 0)],
                    vecs=[(w["norm_mlp_w"], D_MODEL, 0)], outs=[(D_MODEL, D_MODEL, MXU_DTYPE)])
    s["up"] = _mm("mm_up", s["vn"], w["w_mlp_up"])
    (s["act"],) = rw("relu2", lambda x: (jnp.square(jnp.maximum(x, 0.0)),), tr=trw, rows=[(s["up"], D_FF, 0)],
                     outs=[(D_FF, D_FF, MXU_DTYPE)])
    return _mm("mm_down", s["act"], w["w_mlp_down"], add=s["h2"]), s, w


def _layer_bwd(geo, dh3, s, w, tab, mid=None, tail=None):
    nr, tr, trw = geo.nrows, geo.tr, geo.tr_wide
    tb = geo.lp // tr
    rw = functools.partial(_rowwise, nrows=nr)
    g = {}
    proj = s["proj"]

    def rms_bwd(x, dy, res, gw):
        _, vjp = jax.vjp(_rms, x.astype(F32), gw)
        dx, dgw = vjp(dy.astype(F32))
        return dx + res, dgw

    def rms_bwd_nores(x, dy, gw):
        _, vjp = jax.vjp(_rms, x.astype(F32), gw)
        return vjp(dy.astype(F32))

    dact = _mm("mm_down_t", dh3, w["w_mlp_down"], tb=True)
    g["w_mlp_down"] = _mm("mm_down_g", s["act"], dh3, ta=True, out_dtype=MXU_DTYPE)
    (dup,) = rw("relu2_bwd", lambda d, x: (d * 2.0 * jnp.maximum(x, 0.0),), tr=trw,
                rows=[(dact, D_FF, 0), (s["up"], D_FF, 0)], outs=[(D_FF, D_FF, MXU_DTYPE)])
    g["w_mlp_up"] = _mm("mm_up_g", s["vn"], dup, ta=True, out_dtype=MXU_DTYPE)
    dvn = _mm("mm_up_t", dup, w["w_mlp_up"], tb=True)
    dh2, g["norm_mlp_w"] = rw("rms_mlp_bwd", rms_bwd, tr=tr,
                              rows=[(s["h2"], D_MODEL, 0), (dvn, D_MODEL, 0), (dh3, D_MODEL, 0)],
                              vecs=[(w["norm_mlp_w"], D_MODEL, 0)], outs=[(D_MODEL, D_MODEL, F32)],
                              reds=[(D_MODEL, D_MODEL)])
    dmixed = _mm("mm_out_t", dh2, w["w_out"], tb=True)
    g["w_out"] = _mm("mm_out_g", s["mixed"], dh2, ta=True, out_dtype=MXU_DTYPE)

    def gate_bwd(gs, gm, ys, ym, dm):
        f = lambda a, b, c, d: _sigmoid(a) * c + _sigmoid(b) * d
        _, vjp = jax.vjp(f, gs, gm, ys, ym)
        dgs, dgm, dys, dym = vjp(dm)
        return dys, dym, dgs, dgm

    dys_p, dym_p, dg_ssm, dg_mla = rw(
        "gate_bwd", gate_bwd, tr=tr,
        rows=[(proj, D_MODEL, geo.cb("g_ssm")), (proj, D_MODEL, geo.cb("g_mla")), (s["ys_p"], D_MODEL, 0),
              (s["ym_p"], D_MODEL, 0), (dmixed, D_MODEL, 0)], outs=[(D_MODEL, D_MODEL, MXU_DTYPE)] * 4)
    g["w_branch_ssm"] = _mm("mm_bs_g", s["y_ssm"], dys_p, ta=True, out_dtype=MXU_DTYPE)
    dy_ssm = _mm("mm_bs_t", dys_p, w["w_branch_ssm"], tb=True)
    g["w_branch_mla"] = _mm("mm_bm_g", s["o"], dym_p, ta=True, out_dtype=MXU_DTYPE)
    d_o = _mm("mm_bm_t", dym_p, w["w_branch_mla"], tb=True)
    delta = _attn_delta(geo, d_o, s["o"])
    dqc, dkc, dv = _attn_bwd2(geo, s["qc"], s["kc"], s["v"], d_o, s["lse"], delta)
    rope_tabs = [(tab["cos"], LANE, 0), (tab["sin"], LANE, 0)]
    dqn, dqp_raw = rw("rope_q_bwd", lambda x, c, sn: (x[:, :LANE], _rope_t(x[:, LANE:], c, sn)), tr=tr, ncb=MLA_HEADS,
                      rows=[(dqc, 2 * LANE, 0)], tabs=rope_tabs, outs=[(geo.hq, LANE, MXU_DTYPE)] * 2, tab_blocks=tb)

    def rope_k_bwd(x, c, sn):
        tot = x[:, LANE:2 * LANE]
        for hd in range(1, MLA_HEADS):
            tot = tot + x[:, (2 * hd + 1) * LANE:(2 * hd + 2) * LANE]
        dkn_ = jnp.concatenate([x[:, 2 * hd * LANE:(2 * hd + 1) * LANE] for hd in range(MLA_HEADS)], axis=1)
        return dkn_, _rope_t(tot, c, sn)

    dkn, dk_rope = rw("rope_k_bwd", rope_k_bwd, tr=geo.tr_wide, rows=[(dkc, 2 * geo.hq, 0)], tabs=rope_tabs,
                      outs=[(geo.hq, geo.hq, MXU_DTYPE), (LANE, LANE, MXU_DTYPE)], tab_blocks=geo.lp // geo.tr_wide)
    g["w_qn"] = _mm("mm_qn_g", s["cq_n"], dqn, ta=True, out_dtype=MXU_DTYPE)
    g["w_qp"] = _mm("mm_qp_g", s["cq_n"], dqp_raw, ta=True, out_dtype=MXU_DTYPE)
    dcq_n = _mm("mm_qp_t", dqp_raw, w["w_qp"], tb=True, add=_mm("mm_qn_t", dqn, w["w_qn"], tb=True))
    g["w_k"] = _mm("mm_kn_g", s["ckv_n"], dkn, ta=True, out_dtype=MXU_DTYPE)
    g["w_v"] = _mm("mm_v_g", s["ckv_n"], dv, ta=True, out_dtype=MXU_DTYPE)
    dckv_n = _mm("mm_v_t", dv, w["w_v"], tb=True, add=_mm("mm_kn_t", dkn, w["w_k"], tb=True))
    dc_q, g["q_norm_w"] = rw("rms_q_bwd", rms_bwd_nores, tr=tr,
                             rows=[(proj, MLA_Q_LORA, geo.cb("c_q")), (dcq_n, MLA_Q_LORA, 0)],
                             vecs=[(w["q_norm_w"], MLA_Q_LORA, 0)], outs=[(MLA_Q_LORA, MLA_Q_LORA, MXU_DTYPE)],
                             reds=[(MLA_Q_LORA, MLA_Q_LORA)])
    dc_kv, g["kv_norm_w"] = rw("rms_kv_bwd", rms_bwd_nores, tr=tr,
                               rows=[(proj, MLA_KV_LORA, geo.cb("c_kv")), (dckv_n, MLA_KV_LORA, 0)],
                               vecs=[(w["kv_norm_w"], MLA_KV_LORA, 0)], outs=[(MLA_KV_LORA, MLA_KV_LORA, MXU_DTYPE)],
                               reds=[(MLA_KV_LORA, MLA_KV_LORA)])
    gw_ = SSM_D_INNER // SSM_GROUPS
    d_skip_full = w["d_skip_full"] if mid is None else w["d_skip_full"] + mid(g)[0, 0]

    def gate_norm_bwd(y, x, z, dy, dsk, nw):
        f = lambda y_, x_, z_, dsk_, nw_: _rms((y_ + x_ * dsk_) * _silu(z_), nw_)
        _, vjp = jax.vjp(f, y, x, z, dsk, nw)
        dy_, dx_, dz_, ddsk, dnw = vjp(dy)
        return dy_, dx_, dz_, ddsk, dnw

    dy_ssd, dxs_skip, dz, g["d_skip_full"], g["ssm_norm_w"] = rw(
        "ssm_gate_norm_bwd", gate_norm_bwd, tr=tr, ncb=SSM_GROUPS,
        rows=[(s["y_ssd"], gw_, 0), (s["xc"], gw_, 0), (proj, gw_, geo.col["z"][0] // gw_), (dy_ssm, gw_, 0)],
        vecs=[(d_skip_full, gw_, 0), (w["ssm_norm_w"], gw_, 0)],
        outs=[(SSM_D_INNER, gw_, F32), (SSM_D_INNER, gw_, F32), (SSM_D_INNER, gw_, MXU_DTYPE)],
        reds=[(SSM_D_INNER, gw_), (SSM_D_INNER, gw_)])
    dxc, ddt, g["dt_bias"], g["a_log"] = _ssd_bwd(geo, s["xc"], proj, w["dt_bias"], w["a_log"], s["s_prev"],
                                                   dy_ssd, dxs_skip)
    dxbc, g["conv_w"], g["conv_b"] = _conv_bwd(geo, proj, w["conv_w"], w["conv_b"], dxc)
    di, gn = SSM_D_INNER, geo.gn
    dproj = jnp.concatenate([dz, dxbc[:, :di], dg_ssm, dg_mla, dxbc[:, di:di + gn], dxbc[:, di + gn:], dc_q, dc_kv,
                             ddt, dk_rope], axis=-1)
    g["w_in_p"] = _mm("mm_in_g", s["u"], dproj, ta=True, out_dtype=MXU_DTYPE)
    if tail is not None:
        tail(g)
    du = _mm("mm_in_t", dproj, w["w_in_p"], tb=True)
    dh, g["norm_mix_w"] = rw("rms_mix_bwd", rms_bwd, tr=tr,
                             rows=[(s["h"], D_MODEL, 0), (du, D_MODEL, 0), (dh2, D_MODEL, 0)],
                             vecs=[(w["norm_mix_w"], D_MODEL, 0)], outs=[(D_MODEL, D_MODEL, F32)],
                             reds=[(D_MODEL, D_MODEL)])
    return dh, g


def _loss_bwd(geo, h, fw, target, tab):
    tr = geo.tr

    def fn(x, tgt, gw, tok):
        def lossf(x_, gw_):
            err = jnp.square(_rms(x_, gw_) - tgt)
            return 0.5 * jnp.sum(tok * jnp.mean(err, axis=-1, keepdims=True), axis=0, keepdims=True)

        val, vjp = jax.vjp(lossf, x, gw)
        dx, dgw = vjp(jnp.ones((1, 1), F32))
        return dx, jnp.broadcast_to(val, (1, LANE)), dgw

    return _rowwise("loss", fn, nrows=geo.nrows, tr=tr, rows=[(h, D_MODEL, 0), (target, D_MODEL, 0)],
                    vecs=[(fw, D_MODEL, 0)], tabs=[(tab["token"], 1, 0)], outs=[(D_MODEL, D_MODEL, F32)],
                    reds=[(LANE, LANE), (D_MODEL, D_MODEL)], tab_blocks=geo.lp // tr)


def kernel(x, meta_tokens, norm_mix_w, w_in, conv_w, conv_b, dt_bias, a_log, d_skip, ssm_norm_w, q_norm_w, kv_norm_w, w_uq, w_ukv, w_branch_ssm, w_branch_mla, w_out, norm_mlp_w, w_mlp_up, w_mlp_down, final_norm_w, loss_target, m_meta_tokens, m_norm_mix_w, m_w_in, m_conv_w, m_conv_b, m_dt_bias, m_a_log, m_d_skip, m_ssm_norm_w, m_q_norm_w, m_kv_norm_w, m_w_uq, m_w_ukv, m_w_branch_ssm, m_w_branch_mla, m_w_out, m_norm_mlp_w, m_w_mlp_up, m_w_mlp_down, m_final_norm_w, v_meta_tokens, v_norm_mix_w, v_w_in, v_conv_w, v_conv_b, v_dt_bias, v_a_log, v_d_skip, v_ssm_norm_w, v_q_norm_w, v_kv_norm_w, v_w_uq, v_w_ukv, v_w_branch_ssm, v_w_branch_mla, v_w_out, v_norm_mlp_w, v_w_mlp_up, v_w_mlp_down, v_final_norm_w):
    args = dict(locals())
    wts = {n: args[n] for n in WEIGHTS}
    mom = {n: args["m_" + n] for n in WEIGHTS}
    var = {n: args["v_" + n] for n in WEIGHTS}
    bsz, seq, _ = x.shape
    depth = w_in.shape[0]
    geo = _Geo(bsz, seq)
    tab = _tables(geo)

    big_names = [n for n, _ in BIG]
    sh_names = big_names + [n for n, _ in SHARDED_F32]
    kinds = dict(BIG + SHARDED_F32)
    shard3 = lambda a: a.reshape((1,) + a.shape) if a.ndim == 2 else a
    wire = {n: (MXU_DTYPE if n in big_names else F32) for n in sh_names}
    cast = {n: shard3(wts[n]).astype(wire[n]) for n in sh_names}
    per_layer = [n for n in sh_names if n != "meta_tokens"]
    small_names = ["norm_mix_w", "conv_b", "dt_bias", "a_log", "d_skip", "ssm_norm_w", "q_norm_w", "kv_norm_w",
                   "norm_mlp_w"]

    def gather_items(pairs):
        ins, outs, items, forms = [], [], [], []
        for n, i in pairs:
            a, b = cast[n].shape[1:]
            shape, dst, form = _gather_plan(a, b, kinds[n])
            items.append((len(ins), len(outs), (lambda ref, p, i=i: ref.at[i]), dst))
            ins.append(cast[n])
            outs.append(jax.ShapeDtypeStruct(shape, wire[n]))
            forms.append(form)
        return ins, outs, items, forms

    def whole_weights(pairs, forms, got):
        by_layer = {}
        for (n, i), form, g in zip(pairs, forms, got):
            if n == "w_in":
                n, g = "w_in_p", _w_in_assemble(geo, g)
            elif form == "row":
                g = g.reshape(g.shape[0] * g.shape[1], g.shape[2])
            elif form == "stack":
                g = _unshard(g, "col")
            by_layer.setdefault(i, {})[n] = g
        return by_layer

    def prep(i, whole, token=None):
        wl = dict(whole)
        wl.update({n: wts[n][i] for n in small_names})
        if token is not None:
            wl["norm_mix_w"] = wl["norm_mix_w"] + token[0, 0]
        return _prep_layer(geo, wl)

    early = ("w_in", "conv_w")
    late_names = [n for n in per_layer if n not in early]
    pairs1 = [(n, i) for i in range(1, depth) for n in per_layer]
    groups = [[(n, 0) for n in early] + [("meta_tokens", 0)], [(n, 0) for n in late_names]] + ([pairs1] if pairs1 else [])
    started = []
    for gi, pairs in enumerate(groups):
        ins, outs, items, forms = gather_items(pairs)
        sems, thru, landing, token = _exchange_start("gather_w%d_start" % gi, ins, outs, items)
        started.append((pairs, forms, sems, thru, landing, items))

    def gathered(gi, after):
        pairs, forms, sems, thru, landing, items = started[gi]
        return whole_weights(pairs, forms, _exchange_wait("gather_w%d_wait" % gi, sems, thru, landing, items, after))

    whole0 = gathered(0, x)[0]
    meta_full = whole0.pop("meta_tokens")

    meta = jnp.broadcast_to(meta_full[None], (bsz, N_META, D_MODEL))
    h = jnp.concatenate([jnp.zeros((bsz, geo.pad, D_MODEL), F32), meta, x], axis=1).reshape(geo.nrows, D_MODEL)
    target = jnp.concatenate([jnp.zeros((bsz, geo.pad + N_META, D_MODEL), F32), loss_target], axis=1)
    target = target.reshape(geo.nrows, D_MODEL)
    layers, saved = [], []
    for i in range(depth):
        if i == 0:
            w, late = prep(0, whole0, token), (lambda after: _prep_layer(geo, gathered(1, after)[0]))
        else:
            if i == 1:
                whole1 = gathered(2, h)
            w, late = prep(i, whole1[i]), None
        h, s, w = _layer_fwd(geo, h, w, tab, late)
        layers.append(w)
        saved.append(s)
    dh, loss_part, g_final = _loss_bwd(geo, h, final_norm_w.reshape(1, -1), target, tab)

    def scatter_items(pairs):
        ins, outs, items = [], [], []
        for n, i in pairs:
            a, b = cast[n].shape[1:]
            arr = g_meta if n == "meta_tokens" else grads[i]["w_in_p" if n == "w_in" else n]
            if n == "w_in":
                arr, src = _w_in_split(geo, arr, b), _entry
            elif kinds[n] == "row":
                src = lambda ref, p, a=a: ref.at[pl.ds(pl.multiple_of(p * a, a), a)]
            elif b % LANE == 0:
                src = lambda ref, p, b=b: ref.at[:, pl.ds(pl.multiple_of(p * b, b), b)]
            else:
                arr, src = _shard(arr, "col"), _entry
            items.append((len(ins), len(outs), src, _entry))
            ins.append(arr.astype(wire[n]))
            outs.append(jax.ShapeDtypeStruct((N_DEV, a, b), wire[n]))
        return ins, outs, items

    grads = [None] * depth
    landed, pending, res = {}, {}, {}

    def scatter_start(name, pairs):
        ins, outs, items = scatter_items(pairs)
        sems, thru, landing, token = _exchange_start(name + "_start", ins, outs, items)
        pending[name] = (pairs, sems, thru, landing, items)
        return token

    def scatter_wait(name, after):
        pairs, sems, thru, landing, items = pending[name]
        landed.update(zip(pairs, _exchange_wait(name + "_wait", sems, thru, landing, items, after)))

    def adam(n):
        parts = [landed[(n, i)] for i in range(cast[n].shape[0])]
        r = _adamw_nat("adamw_" + n, parts, shard3(wts[n]), shard3(mom[n]), shard3(var[n]))
        res[n] = [a.reshape(wts[n].shape) for a in r]

    def mid0(g):
        grads[0] = _unprep_grads(geo, g)
        return scatter_start("scatter_gb0", [(n, 0) for n in late_names])

    def tail0(g):
        grads[0] = _unprep_grads(geo, g)
        scatter_start("scatter_ga0", [(n, 0) for n in early])

    for i in reversed(range(depth)):
        dh, gl = _layer_bwd(geo, dh, saved[i], layers[i], tab, *((mid0, tail0) if i == 0 else ()))
        grads[i] = _unprep_grads(geo, gl)
        if i == 1:
            dh = dh + scatter_start("scatter_g1", pairs1)[0, 0]
    dh = dh.reshape(bsz, geo.lp, D_MODEL)
    grad_x = dh[:, geo.pad + N_META:]
    g_meta = jnp.sum(dh[:, geo.pad:geo.pad + N_META], axis=0)
    if pairs1:
        scatter_wait("scatter_g1", g_meta)
    scatter_wait("scatter_gb0", g_meta)
    for n in late_names:
        adam(n)
    g_small = {n: jnp.stack([grads[i][n] for i in range(depth)]) for n in SMALL if n != "final_norm_w"}
    g_small["final_norm_w"] = g_final.reshape(-1)
    zero = jnp.zeros((1,), F32)
    pk = lambda d, last: _pack([d[n] for n in SMALL] + [last], F32, row_mult=8)
    packed = pk(g_small, loss_part[0, :1])
    ins, outs, items = scatter_items([("meta_tokens", 0)])
    parts, landed[("meta_tokens", 0)] = _exchange(
        "gather_g", [packed] + ins, [jax.ShapeDtypeStruct((N_DEV,) + packed.shape, F32)] + outs,
        [(0, 0, _whole, _entry)] + [(1, 1, items[0][2], items[0][3])])
    adam("meta_tokens")
    scatter_wait("scatter_ga0", res["meta_tokens"][1])
    for n in early:
        adam(n)
    res_sm = _adamw("adamw_small", parts, pk(wts, zero), pk(mom, zero), pk(var, zero))
    res_sm = [_unpack(r, [wts[n].shape for n in SMALL] + [(1,)]) for r in res_sm]
    loss = res_sm[0][-1][0]

    out = [loss, grad_x]
    for k in range(4):
        named = {n: res[n][k] for n in sh_names}
        named.update(zip(SMALL, res_sm[k]))
        out += [named[n] for n in WEIGHTS]
    return tuple(out)
```

```python
import functools

import numpy as np
import jax
import jax.numpy as jnp
from jax import lax
from jax.experimental import pallas as pl
from jax.experimental.pallas import tpu as pltpu

F32 = jnp.float32
MXU_DTYPE = jnp.bfloat16

D_MODEL = 1024
N_META = 16
EPS = 1e-6
SSM_D_INNER = 2048
SSM_HEAD_DIM = 64
SSM_GROUPS = 4
SSM_STATE = 128
SSM_CONV = 4
SSM_CHUNK = 128
MLA_HEADS = 8
MLA_Q_LORA = 512
MLA_KV_LORA = 256
MLA_NOPE = 128
MLA_ROPE = 64
MLA_V = 128
ROPE_THETA = 10000.0
D_FF = 4096
ADAM_LR = 0.001
ADAM_B1 = 0.9
ADAM_B2 = 0.999
ADAM_EPS = 1e-08
ADAM_WD = 0.01
ADAM_STEP = 10

N_DEV = 8
ATT_BLK = 256
LANE = 128
PACK_W = 1024
VMEM_LIMIT = 56 * 1024 * 1024
MESH_ID = pl.DeviceIdType.MESH

BIG = (("w_in", "col"), ("w_uq", "col"), ("w_ukv", "col"), ("w_branch_ssm", "row"), ("w_branch_mla", "row"),
       ("w_out", "row"), ("w_mlp_up", "col"), ("w_mlp_down", "row"))
SHARDED_F32 = (("conv_w", "col"), ("meta_tokens", "col"))
SMALL = ("norm_mix_w", "conv_b", "dt_bias", "a_log", "d_skip", "ssm_norm_w", "q_norm_w", "kv_norm_w",
         "norm_mlp_w", "final_norm_w")
WEIGHTS = ("meta_tokens", "norm_mix_w", "w_in", "conv_w", "conv_b", "dt_bias", "a_log", "d_skip", "ssm_norm_w",
           "q_norm_w", "kv_norm_w", "w_uq", "w_ukv", "w_branch_ssm", "w_branch_mla", "w_out", "norm_mlp_w",
           "w_mlp_up", "w_mlp_down", "final_norm_w")


def _cparams(sem=None):
    return pltpu.CompilerParams(dimension_semantics=sem, vmem_limit_bytes=VMEM_LIMIT)


def _pick(n, cands):
    for c in cands:
        if n % c == 0:
            return c
    return n


def _sigmoid(x):
    return 1.0 / (1.0 + jnp.exp(-x))


def _silu(x):
    return x * _sigmoid(x)


def _softplus(x):
    return jnp.maximum(x, 0.0) + jnp.log1p(jnp.exp(-jnp.abs(x)))


def _rms(x, w):
    return x * lax.rsqrt(jnp.mean(x * x, axis=-1, keepdims=True) + EPS) * w


def _dot(a, b, ca, cb, precision=None):
    return lax.dot_general(a, b, (((ca,), (cb,)), ((), ())), preferred_element_type=F32, precision=precision)


def _mxdot(a, b, ca, cb):
    return _dot(a.astype(MXU_DTYPE), b.astype(MXU_DTYPE), ca, cb)


def _mm(name, a, b, *, ta=False, tb=False, add=None, out_dtype=F32, dep=None):
    (kdim, m) = a.shape if ta else a.shape[::-1]
    (n, k2) = b.shape if tb else b.shape[::-1]
    assert kdim == k2, (name, a.shape, b.shape)
    tm = _pick(m, (1152, 1024, 768, 512, 384, 256, 128))
    tn = _pick(n, (1024, 512, 384, 256, 128))
    tk = _pick(kdim, (1152, 1024, 768, 512, 384, 256, 128))
    nk = kdim // tk
    a_spec = pl.BlockSpec((tk, tm), lambda i, j, k: (k, i)) if ta else pl.BlockSpec((tm, tk), lambda i, j, k: (i, k))
    b_spec = pl.BlockSpec((tn, tk), lambda i, j, k: (j, k)) if tb else pl.BlockSpec((tk, tn), lambda i, j, k: (k, j))
    o_spec = pl.BlockSpec((tm, tn), lambda i, j, k: (i, j))
    ca, cb = (0 if ta else 1), (1 if tb else 0)

    def body(*refs):
        a_ref, b_ref = refs[:2]
        o_ref, acc = refs[-2:]
        k = pl.program_id(2)

        @pl.when(k == 0)
        def _():
            acc[...] = jnp.zeros_like(acc)

        acc[...] += _mxdot(a_ref[...], b_ref[...], ca, cb)

        @pl.when(k == nk - 1)
        def _():
            r = acc[...]
            if add is not None:
                r = r + refs[2][...].astype(F32)
            o_ref[...] = r.astype(out_dtype)

    in_specs, args = [a_spec, b_spec], [a, b]
    if add is not None:
        in_specs.append(o_spec)
        args.append(add)
    if dep is not None:
        in_specs.append(pl.BlockSpec((8, LANE), lambda i, j, k: (0, 0)))
        args.append(dep)
    return pl.pallas_call(
        body, name=name, grid=(m // tm, n // tn, nk), in_specs=in_specs, out_specs=o_spec,
        out_shape=jax.ShapeDtypeStruct((m, n), out_dtype), scratch_shapes=[pltpu.VMEM((tm, tn), F32)],
        compiler_params=_cparams(("parallel", "parallel", "arbitrary")))(*args)


def _rowwise(name, fn, *, nrows, tr, ncb=1, rows=(), fixed=(), vecs=(), tabs=(), outs=(), reds=(), tab_blocks=1):
    in_specs, args = [], []
    for arr, w, c0 in rows:
        in_specs.append(pl.BlockSpec((tr, w), lambda g, i, c0=c0: (i, c0 + g)))
        args.append(arr)
    for arr, w, c0 in fixed:
        in_specs.append(pl.BlockSpec((tr, w), lambda g, i, c0=c0: (i, c0)))
        args.append(arr)
    for arr, w, c0 in vecs:
        in_specs.append(pl.BlockSpec((1, w), lambda g, i, c0=c0: (0, c0 + g)))
        args.append(arr)
    for arr, w, c0 in tabs:
        in_specs.append(pl.BlockSpec((tr, w), lambda g, i, c0=c0: (i % tab_blocks, c0)))
        args.append(arr)
    out_shape = [jax.ShapeDtypeStruct((nrows, wt), dt) for wt, w, dt in outs]
    out_shape += [jax.ShapeDtypeStruct((1, wt), F32) for wt, w in reds]
    out_specs = [pl.BlockSpec((tr, w), lambda g, i: (i, g)) for wt, w, dt in outs]
    out_specs += [pl.BlockSpec((1, w), lambda g, i: (0, g)) for wt, w in reds]
    n_in, n_out = len(args), len(outs)

    def body(*refs):
        res = fn(*[r[...] for r in refs[:n_in]])
        for o_ref, val in zip(refs[n_in:n_in + n_out], res[:n_out]):
            o_ref[...] = val.astype(o_ref.dtype)
        i = pl.program_id(1)
        for d_ref, val in zip(refs[n_in + n_out:], res[n_out:]):
            @pl.when(i == 0)
            def _(d_ref=d_ref, val=val):
                d_ref[...] = val

            @pl.when(i > 0)
            def _(d_ref=d_ref, val=val):
                d_ref[...] += val

    res = pl.pallas_call(
        body, name=name, grid=(ncb, nrows // tr), in_specs=in_specs, out_specs=out_specs, out_shape=out_shape,
        compiler_params=_cparams(("parallel", "arbitrary")))(*args)
    return res


def _peer(k):
    x, y, c = lax.axis_index("x"), lax.axis_index("y"), lax.axis_index("c")
    px = jnp.where((k >> 2) & 1, 1 - x, x)
    py = jnp.where((k >> 1) & 1, 1 - y, y)
    pc = jnp.where(k & 1, 1 - c, c)
    return (px, py, pc), 4 * px + 2 * py + pc


def _my_index():
    return 4 * lax.axis_index("x") + 2 * lax.axis_index("y") + lax.axis_index("c")


def _exchange(name, ins, out_shapes, items):
    n_in, n_out, n_it = len(ins), len(out_shapes), len(items)

    def body(*refs):
        x, o = refs[:n_in], refs[n_in:n_in + n_out]
        send_sems, recv_sems, local_sems = refs[n_in + n_out:]
        me = _my_index()
        local, sends = [], []
        for t, (ii, io, src, dst) in enumerate(items):
            cp = pltpu.make_async_copy(src(x[ii], me), dst(o[io], me), local_sems.at[t])
            cp.start()
            local.append(cp)
        for k in range(1, N_DEV):
            dev, idx = _peer(k)
            for t, (ii, io, src, dst) in enumerate(items):
                s = (k - 1) * n_it + t
                cp = pltpu.make_async_remote_copy(
                    src_ref=src(x[ii], idx), dst_ref=dst(o[io], me), send_sem=send_sems.at[s],
                    recv_sem=recv_sems.at[s], device_id=dev, device_id_type=MESH_ID)
                cp.start()
                sends.append(cp)
        for k in range(1, N_DEV):
            dev, idx = _peer(k)
            for t, (ii, io, src, dst) in enumerate(items):
                s = (k - 1) * n_it + t
                pltpu.make_async_remote_copy(
                    src_ref=src(x[ii], idx), dst_ref=dst(o[io], idx), send_sem=send_sems.at[s],
                    recv_sem=recv_sems.at[s], device_id=dev, device_id_type=MESH_ID).wait_recv()
        for cp in sends:
            cp.wait_send()
        for cp in local:
            cp.wait()

    nsem = (N_DEV - 1) * n_it
    anyspec = pl.BlockSpec(memory_space=pl.ANY)
    return pl.pallas_call(
        body, name=name, out_shape=list(out_shapes), in_specs=[anyspec] * n_in, out_specs=[anyspec] * n_out,
        scratch_shapes=[pltpu.SemaphoreType.DMA((nsem,)), pltpu.SemaphoreType.DMA((nsem,)),
                        pltpu.SemaphoreType.DMA((n_it,))],
        compiler_params=pltpu.CompilerParams(has_side_effects=True))(*ins)


def _split_copies(x, land, send_sems, recv_sems, items, receive):
    me = _my_index()
    remote, n_it = [], len(items)
    for k in range(1, N_DEV):
        dev, idx = _peer(k)
        for t, (ii, io, src, dst) in enumerate(items):
            s = (k - 1) * n_it + t
            remote.append(pltpu.make_async_remote_copy(
                src_ref=src(x[ii], idx), dst_ref=dst(land[io], idx if receive else me), send_sem=send_sems.at[s],
                recv_sem=recv_sems.at[s], device_id=dev, device_id_type=MESH_ID))
    local = [pltpu.make_async_copy(src(x[ii], me), dst(land[io], me), send_sems.at[(N_DEV - 1) * n_it + t])
             for t, (ii, io, src, dst) in enumerate(items)]
    return remote, local


def _exchange_start(name, ins, out_shapes, items, dep=None):
    n_in, n_out, n_it = len(ins), len(out_shapes), len(items)

    def body(*refs):
        x, land = refs[:n_in], refs[n_in:n_in + n_out]
        first_out = n_in + n_out + (dep is not None)
        send_sems, recv_sems, token = refs[first_out], refs[first_out + 1], refs[-1]
        remote, local = _split_copies(x, land, send_sems, recv_sems, items, False)
        for cp in remote + local:
            cp.start()
        token[...] = jnp.zeros_like(token)

    hbm = pl.BlockSpec(memory_space=pltpu.HBM)
    sem = pl.BlockSpec(memory_space=pltpu.SEMAPHORE)
    arrs = [pltpu.with_memory_space_constraint(a, pltpu.HBM)
            for a in list(ins) + [lax.empty(s.shape, s.dtype) for s in out_shapes]]
    res = pl.pallas_call(
        body, name=name,
        out_shape=(pltpu.SemaphoreType.DMA((N_DEV * n_it,)), pltpu.SemaphoreType.DMA(((N_DEV - 1) * n_it,)),
                   *[pltpu.HBM(a.shape, a.dtype) for a in arrs], jax.ShapeDtypeStruct((8, LANE), F32)),
        in_specs=[hbm] * (n_in + n_out) + ([] if dep is None else [pl.BlockSpec(memory_space=pl.ANY)]),
        out_specs=(sem, sem, *[hbm] * (n_in + n_out), pl.BlockSpec(memory_space=pltpu.VMEM)),
        input_output_aliases={i: 2 + i for i in range(n_in + n_out)},
        compiler_params=pltpu.CompilerParams(has_side_effects=pltpu.SideEffectType.DATAFLOW_SIDE_EFFECTING))(
            *arrs, *([] if dep is None else [dep]))
    return res[:2], res[2:2 + n_in], res[2 + n_in:2 + n_in + n_out], res[-1]


def _exchange_wait(name, sems, ins, landing, items, after):
    n_in, n_out = len(ins), len(landing)

    def body(*refs):
        x, land = refs[:n_in], refs[n_in:n_in + n_out]
        send_sems, recv_sems = refs[n_in + n_out], refs[n_in + n_out + 1]
        remote, local = _split_copies(x, land, send_sems, recv_sems, items, True)
        for cp in remote:
            cp.wait_send()
            cp.wait_recv()
        for cp in local:
            cp.wait()

    hbm = pl.BlockSpec(memory_space=pltpu.HBM)
    sem = pl.BlockSpec(memory_space=pltpu.SEMAPHORE)
    arrs = list(ins) + list(landing)
    res = pl.pallas_call(
        body, name=name, out_shape=tuple(pltpu.HBM(a.shape, a.dtype) for a in arrs),
        in_specs=[hbm] * (n_in + n_out) + [sem, sem, pl.BlockSpec(memory_space=pl.ANY)],
        out_specs=tuple([hbm] * (n_in + n_out)), input_output_aliases={i: i for i in range(n_in + n_out)},
        compiler_params=pltpu.CompilerParams(has_side_effects=pltpu.SideEffectType.DATAFLOW_SIDE_EFFECTING))(
            *arrs, *sems, after)
    return res[n_in:]


def _whole(ref, p):
    return ref


def _entry(ref, p):
    return ref.at[p]


def _gather_plan(a, b, kind):
    if kind == "col" and b % LANE == 0:
        return (a, N_DEV * b), (lambda ref, p: ref.at[:, pl.ds(pl.multiple_of(p * b, b), b)]), "col"
    return (N_DEV, a, b), _entry, ("row" if kind == "row" else "stack")


def _adamw_nat(name, parts, w, m, v):
    depth, b, c = w.shape
    assert len(parts) == depth
    tb = _pick(b, (128, 64, 32, 16, 8))
    spec = pl.BlockSpec((1, tb, c), lambda i, j: (i, j, 0))

    def body(*refs):
        p_refs = refs[:depth]
        w_ref, m_ref, v_ref, g_ref, d_ref, nm_ref, nv_ref = refs[depth:]
        for layer, p_ref in enumerate(p_refs):
            @pl.when(pl.program_id(0) == layer)
            def _(p_ref=p_ref):
                g = p_ref[0].astype(F32)
                for j in range(1, N_DEV):
                    g = g + p_ref[j].astype(F32)
                nm = ADAM_B1 * m_ref[0] + (1.0 - ADAM_B1) * g
                nv = ADAM_B2 * v_ref[0] + (1.0 - ADAM_B2) * jnp.square(g)
                m_hat = nm / (1.0 - ADAM_B1 ** ADAM_STEP)
                v_hat = nv / (1.0 - ADAM_B2 ** ADAM_STEP)
                g_ref[0] = g
                d_ref[0] = -ADAM_LR * (m_hat / (jnp.sqrt(v_hat) + ADAM_EPS) + ADAM_WD * w_ref[0])
                nm_ref[0] = nm
                nv_ref[0] = nv

    sds = jax.ShapeDtypeStruct((depth, b, c), F32)
    return pl.pallas_call(
        body, name=name, grid=(depth, b // tb),
        in_specs=[pl.BlockSpec((N_DEV, tb, c), lambda i, j: (0, j, 0))] * depth + [spec, spec, spec],
        out_specs=[spec] * 4, out_shape=[sds] * 4, compiler_params=_cparams(("parallel", "parallel")))(*parts, w, m, v)


def _adamw(name, parts, w, m, v):
    rows = w.shape[0]
    tr = _pick(rows, (256, 128, 64, 32, 16, 8))
    spec = pl.BlockSpec((tr, PACK_W), lambda i: (i, 0))

    def body(p_ref, w_ref, m_ref, v_ref, g_ref, d_ref, nm_ref, nv_ref):
        g = p_ref[0]
        for j in range(1, N_DEV):
            g = g + p_ref[j]
        nm = ADAM_B1 * m_ref[...] + (1.0 - ADAM_B1) * g
        nv = ADAM_B2 * v_ref[...] + (1.0 - ADAM_B2) * jnp.square(g)
        m_hat = nm / (1.0 - ADAM_B1 ** ADAM_STEP)
        v_hat = nv / (1.0 - ADAM_B2 ** ADAM_STEP)
        g_ref[...] = g
        d_ref[...] = -ADAM_LR * (m_hat / (jnp.sqrt(v_hat) + ADAM_EPS) + ADAM_WD * w_ref[...])
        nm_ref[...] = nm
        nv_ref[...] = nv

    sds = jax.ShapeDtypeStruct((rows, PACK_W), F32)
    return pl.pallas_call(
        body, name=name, grid=(rows // tr,),
        in_specs=[pl.BlockSpec((N_DEV, tr, PACK_W), lambda i: (0, i, 0)), spec, spec, spec],
        out_specs=[spec] * 4, out_shape=[sds] * 4, compiler_params=_cparams(("parallel",)))(parts, w, m, v)


def _pack(arrs, dtype, row_mult=16):
    flat = jnp.concatenate([a.reshape(-1).astype(dtype) for a in arrs])
    unit = row_mult * PACK_W
    total = -(-flat.shape[0] // unit) * unit
    flat = jnp.pad(flat, (0, total - flat.shape[0]))
    return flat.reshape(-1, PACK_W)


def _pack_lead(arrs, dtype, row_mult):
    flat = jnp.concatenate([a.reshape(N_DEV, -1).astype(dtype) for a in arrs], axis=1)
    unit = row_mult * PACK_W
    total = -(-flat.shape[1] // unit) * unit
    flat = jnp.pad(flat, ((0, 0), (0, total - flat.shape[1])))
    return flat.reshape(N_DEV, -1, PACK_W)


def _unpack(buf, shapes, lead=()):
    flat = buf.reshape(lead + (-1,))
    out, off = [], 0
    for s in shapes:
        n = int(np.prod(s))
        out.append(flat[..., off:off + n].reshape(lead + tuple(s)))
        off += n
    return out


def _unshard(g, kind):
    if kind == "col":
        g = jnp.moveaxis(g, 0, -2)
        return g.reshape(g.shape[:-2] + (g.shape[-2] * g.shape[-1],))
    g = jnp.moveaxis(g, 0, 1)
    return g.reshape((g.shape[0], g.shape[1] * g.shape[2]) + g.shape[3:])


def _shard(full, kind):
    if kind == "col":
        s = full.reshape(full.shape[:-1] + (N_DEV, full.shape[-1] // N_DEV))
        return jnp.moveaxis(s, -2, 0)
    s = full.reshape((full.shape[0], N_DEV, full.shape[1] // N_DEV) + full.shape[2:])
    return jnp.moveaxis(s, 1, 0)


class _Geo:
    def __init__(self, bsz, seq):
        self.bsz, self.seq = bsz, seq
        self.pad = (-(N_META + seq)) % ATT_BLK
        self.lp = self.pad + N_META + seq
        assert (self.pad + N_META) % SSM_CHUNK == 0 and self.lp % SSM_CHUNK == 0
        self.nrows = bsz * self.lp
        self.nc = self.lp // SSM_CHUNK
        self.nh = SSM_D_INNER // SSM_HEAD_DIM
        self.gn = SSM_GROUPS * SSM_STATE
        self.cd = SSM_D_INNER + 2 * self.gn
        self.hq = MLA_HEADS * LANE
        order = (("z", SSM_D_INNER), ("xs", SSM_D_INNER), ("g_ssm", D_MODEL), ("g_mla", D_MODEL), ("bm", self.gn),
                 ("cm", self.gn), ("c_q", MLA_Q_LORA), ("c_kv", MLA_KV_LORA), ("dt", LANE), ("k_rope", LANE))
        self.col, off = {}, 0
        for nm, w in order:
            assert off % w == 0, (nm, off, w)
            self.col[nm] = (off, w)
            off += w
        self.pw = off
        assert self.nh <= LANE and MLA_ROPE == 64 and MLA_NOPE == LANE and MLA_V == LANE
        self.tr = _pick(self.lp, (768, 512, 384, 256, 128))
        self.tr_wide = _pick(self.lp, (384, 256, 128))

    def cb(self, nm):
        off, w = self.col[nm]
        return off // w

    def w_in_runs(self, shard_w):
        nh, half = self.nh, MLA_ROPE // 2
        src, pieces = 0, []
        for nm, n in (("z", SSM_D_INNER), ("xs", SSM_D_INNER), ("bm", self.gn), ("cm", self.gn), ("dt", nh),
                      ("c_q", MLA_Q_LORA), ("c_kv", MLA_KV_LORA), ("k_rope", MLA_ROPE), ("g_ssm", D_MODEL),
                      ("g_mla", D_MODEL)):
            dst = self.col[nm][0]
            if nm == "k_rope":
                pieces += [(src, half, dst), (src + half, half, dst + 2 * half)]
            else:
                pieces.append((src, n, dst))
            src += n
        assert src == shard_w * N_DEV
        runs = []
        for a, n, dst in pieces:
            for j in range(N_DEV):
                lo, hi = max(a, j * shard_w), min(a + n, (j + 1) * shard_w)
                if lo < hi:
                    runs.append((j, lo - j * shard_w, hi - lo, dst + lo - a))
        return runs


def _slot(a):
    h = MLA_ROPE // 2
    z = jnp.zeros(a.shape[:-1] + (h,), a.dtype)
    return jnp.concatenate([a[..., :h], z, a[..., h:], z], axis=-1)


def _unslot(a):
    h = MLA_ROPE // 2
    return jnp.concatenate([a[..., :h], a[..., 2 * h:3 * h]], axis=-1)


def _prep_layer(geo, wl):
    nh = geo.nh
    p = {}
    if "w_uq" in wl:
        uq = wl["w_uq"].reshape(MLA_Q_LORA, MLA_HEADS, MLA_NOPE + MLA_ROPE)
        p["w_qn"] = uq[..., :MLA_NOPE].reshape(MLA_Q_LORA, geo.hq)
        p["w_qp"] = _slot(uq[..., MLA_NOPE:]).reshape(MLA_Q_LORA, geo.hq)
    if "w_ukv" in wl:
        ukv = wl["w_ukv"].reshape(MLA_KV_LORA, MLA_HEADS, MLA_NOPE + MLA_V)
        p["w_k"] = ukv[..., :MLA_NOPE].reshape(MLA_KV_LORA, geo.hq)
        p["w_v"] = ukv[..., MLA_NOPE:].reshape(MLA_KV_LORA, geo.hq)
    for nm in ("w_in_p", "conv_w", "w_branch_ssm", "w_branch_mla", "w_out", "w_mlp_up", "w_mlp_down"):
        if nm in wl:
            p[nm] = wl[nm]
    for nm in ("norm_mix_w", "conv_b", "ssm_norm_w", "q_norm_w", "kv_norm_w", "norm_mlp_w"):
        if nm in wl:
            p[nm] = wl[nm].reshape(1, -1)
    if "dt_bias" in wl:
        p["dt_bias"] = jnp.pad(wl["dt_bias"], (0, LANE - nh)).reshape(1, LANE)
        p["a_log"] = jnp.pad(wl["a_log"], (0, LANE - nh)).reshape(1, LANE)
        p["d_skip_full"] = jnp.repeat(wl["d_skip"], SSM_HEAD_DIM).reshape(1, SSM_D_INNER)
    return p


def _unprep_grads(geo, g):
    nh = geo.nh
    out = {}
    if "w_qn" in g:
        qn = g["w_qn"].reshape(MLA_Q_LORA, MLA_HEADS, MLA_NOPE)
        qp = _unslot(g["w_qp"].reshape(MLA_Q_LORA, MLA_HEADS, LANE))
        out["w_uq"] = jnp.concatenate([qn, qp], axis=-1).reshape(MLA_Q_LORA, -1)
    if "w_k" in g:
        wk = g["w_k"].reshape(MLA_KV_LORA, MLA_HEADS, MLA_NOPE)
        wv = g["w_v"].reshape(MLA_KV_LORA, MLA_HEADS, MLA_V)
        out["w_ukv"] = jnp.concatenate([wk, wv], axis=-1).reshape(MLA_KV_LORA, -1)
    for nm in ("w_in_p", "w_branch_ssm", "w_branch_mla", "w_out", "w_mlp_up", "w_mlp_down", "conv_w"):
        if nm in g:
            out[nm] = g[nm]
    for nm in ("norm_mix_w", "conv_b", "ssm_norm_w", "q_norm_w", "kv_norm_w", "norm_mlp_w"):
        if nm in g:
            out[nm] = g[nm].reshape(-1)
    if "dt_bias" in g:
        out["dt_bias"] = g["dt_bias"].reshape(-1)[:nh]
        out["a_log"] = g["a_log"].reshape(-1)[:nh]
        out["d_skip"] = g["d_skip_full"].reshape(nh, SSM_HEAD_DIM).sum(-1)
    return out


def _tables(geo):
    pos = jnp.arange(geo.lp, dtype=F32) - geo.pad
    inv = ROPE_THETA ** (-jnp.arange(0, MLA_ROPE, 2, dtype=F32) / MLA_ROPE)
    ang = pos[:, None] * inv[None, :]
    cos, sin = jnp.cos(ang), jnp.sin(ang)
    z = jnp.zeros_like(cos)
    rows = jnp.arange(geo.lp)[:, None]
    return {"cos": jnp.concatenate([cos, z, cos, z], axis=-1), "sin": jnp.concatenate([-sin, z, sin, z], axis=-1),
            "valid": (rows >= geo.pad).astype(F32), "token": (rows >= geo.pad + N_META).astype(F32)}


def _w_in_assemble(geo, gathered):
    _, d, sw = gathered.shape
    runs = geo.w_in_runs(sw)
    tr = _pick(d, (256, 128))

    def body(x_ref, o_ref):
        o_ref[...] = jnp.zeros_like(o_ref)
        for j, s0, n, d0 in runs:
            o_ref[:, d0:d0 + n] = x_ref[j, :, s0:s0 + n]

    return pl.pallas_call(
        body, name="w_in_assemble", grid=(d // tr,), in_specs=[pl.BlockSpec((N_DEV, tr, sw), lambda i: (0, i, 0))],
        out_specs=pl.BlockSpec((tr, geo.pw), lambda i: (i, 0)),
        out_shape=jax.ShapeDtypeStruct((d, geo.pw), gathered.dtype), compiler_params=_cparams(("parallel",)))(gathered)


def _w_in_split(geo, g_padded, sw):
    d = g_padded.shape[0]
    runs = geo.w_in_runs(sw)
    tr = _pick(d, (128,))

    def body(x_ref, o_ref):
        for j, s0, n, d0 in runs:
            o_ref[j, :, s0:s0 + n] = x_ref[:, d0:d0 + n]

    return pl.pallas_call(
        body, name="w_in_split", grid=(d // tr,), in_specs=[pl.BlockSpec((tr, geo.pw), lambda i: (i, 0))],
        out_specs=pl.BlockSpec((N_DEV, tr, sw), lambda i: (0, i, 0)),
        out_shape=jax.ShapeDtypeStruct((N_DEV, d, sw), g_padded.dtype),
        compiler_params=_cparams(("parallel",)))(g_padded)


def _conv_cols(geo, cbw):
    nx = SSM_D_INNER // cbw
    x0, b0 = geo.col["xs"][0] // cbw, geo.col["bm"][0] // cbw
    assert geo.col["cm"][0] == geo.col["bm"][0] + geo.gn
    return lambda j: jnp.where(j < nx, x0 + j, b0 + j - nx)


def _conv_pre(x, w_ref, b_ref):
    acc = b_ref[...] + x * w_ref[SSM_CONV - 1:SSM_CONV, :]
    for k in range(SSM_CONV - 1):
        acc = acc + pltpu.roll(x, SSM_CONV - 1 - k, axis=0) * w_ref[k:k + 1, :]
    return acc


def _conv_fwd(geo, proj, conv_w, conv_b):
    cbw = 256
    colmap = _conv_cols(geo, cbw)
    lp, pad = geo.lp, geo.pad

    def body(x_ref, w_ref, b_ref, o_ref):
        valid = (lax.broadcasted_iota(jnp.int32, (lp, 1), 0) >= pad).astype(F32)
        o_ref[...] = _silu(_conv_pre(x_ref[...], w_ref, b_ref)) * valid

    return pl.pallas_call(
        body, name="conv_fwd", grid=(geo.bsz, geo.cd // cbw),
        in_specs=[pl.BlockSpec((lp, cbw), lambda b, j: (b, colmap(j))),
                  pl.BlockSpec((SSM_CONV, cbw), lambda b, j: (0, j)), pl.BlockSpec((1, cbw), lambda b, j: (0, j))],
        out_specs=pl.BlockSpec((lp, cbw), lambda b, j: (b, j)),
        out_shape=jax.ShapeDtypeStruct((geo.nrows, geo.cd), F32),
        compiler_params=_cparams(("parallel", "parallel")))(proj, conv_w, conv_b)


def _conv_bwd(geo, proj, conv_w, conv_b, dxc):
    cbw = 256
    colmap = _conv_cols(geo, cbw)
    lp, pad = geo.lp, geo.pad

    def body(x_ref, w_ref, b_ref, dy_ref, dx_ref, gw_ref, gb_ref):
        b = pl.program_id(1)
        valid = (lax.broadcasted_iota(jnp.int32, (lp, 1), 0) >= pad).astype(F32)
        x = x_ref[...]
        pre = _conv_pre(x, w_ref, b_ref)
        sig = _sigmoid(pre)
        dpre = dy_ref[...] * (sig * (1.0 + pre * (1.0 - sig))) * valid
        dx = dpre * w_ref[SSM_CONV - 1:SSM_CONV, :]
        gws = [jnp.sum(dpre * x, axis=0, keepdims=True)]
        for k in range(SSM_CONV - 2, -1, -1):
            s = SSM_CONV - 1 - k
            dx = dx + pltpu.roll(dpre, lp - s, axis=0) * w_ref[k:k + 1, :]
            gws.insert(0, jnp.sum(dpre * pltpu.roll(x, s, axis=0), axis=0, keepdims=True))
        dx_ref[...] = (dx * valid).astype(dx_ref.dtype)

        @pl.when(b == 0)
        def _():
            gw_ref[...] = jnp.zeros_like(gw_ref)
            gb_ref[...] = jnp.zeros_like(gb_ref)

        for k in range(SSM_CONV):
            gw_ref[k:k + 1, :] += gws[k]
        gb_ref[...] += jnp.sum(dpre, axis=0, keepdims=True)

    return pl.pallas_call(
        body, name="conv_bwd", grid=(geo.cd // cbw, geo.bsz),
        in_specs=[pl.BlockSpec((lp, cbw), lambda j, b: (b, colmap(j))),
                  pl.BlockSpec((SSM_CONV, cbw), lambda j, b: (0, j)), pl.BlockSpec((1, cbw), lambda j, b: (0, j)),
                  pl.BlockSpec((lp, cbw), lambda j, b: (b, j))],
        out_specs=[pl.BlockSpec((lp, cbw), lambda j, b: (b, j)), pl.BlockSpec((SSM_CONV, cbw), lambda j, b: (0, j)),
                   pl.BlockSpec((1, cbw), lambda j, b: (0, j))],
        out_shape=[jax.ShapeDtypeStruct((geo.nrows, geo.cd), MXU_DTYPE),
                   jax.ShapeDtypeStruct((SSM_CONV, geo.cd), F32), jax.ShapeDtypeStruct((1, geo.cd), F32)],
        compiler_params=_cparams(("parallel", "arbitrary")))(proj, conv_w, conv_b, dxc)


def _tri(q):
    r = lax.broadcasted_iota(jnp.int32, (q, q), 0)
    c = lax.broadcasted_iota(jnp.int32, (q, q), 1)
    return r >= c


def _ssd_pre(dtr, dtb, alog, valid):
    dt = _softplus(dtr + dtb) * valid
    adt = dt * (-jnp.exp(alog))
    a_cs = _dot(_tri(SSM_CHUNK).astype(F32), adt, 1, 0, precision=lax.Precision.HIGHEST)
    return dt, a_cs


def _ssd_specs(geo, rev):
    nc, q = geo.nc, SSM_CHUNK
    ci = (lambda c: nc - 1 - c) if rev else (lambda c: c)
    nxb = SSM_D_INNER // geo.gn
    return [pl.BlockSpec((q, SSM_D_INNER), lambda b, c: (b * nc + ci(c), 0)),
            pl.BlockSpec((q, geo.gn), lambda b, c: (b * nc + ci(c), nxb)),
            pl.BlockSpec((q, geo.gn), lambda b, c: (b * nc + ci(c), nxb + 1)),
            pl.BlockSpec((q, LANE), lambda b, c: (b * nc + ci(c), geo.cb("dt"))),
            pl.BlockSpec((1, LANE), lambda b, c: (0, 0)), pl.BlockSpec((1, LANE), lambda b, c: (0, 0))], ci


def _ssd_fwd(geo, xc, proj, dt_bias, a_log):
    q, p, n, e = SSM_CHUNK, SSM_HEAD_DIM, SSM_STATE, geo.nh // SSM_GROUPS
    nc, pad = geo.nc, geo.pad
    in_specs, _ = _ssd_specs(geo, False)

    def body(xs_ref, b_ref, c_ref, dtr_ref, dtb_ref, alog_ref, y_ref, sp_ref, state):
        c = pl.program_id(1)

        @pl.when(c == 0)
        def _():
            state[...] = jnp.zeros_like(state)

        sp_ref[...] = state[...]
        valid = (c * q + lax.broadcasted_iota(jnp.int32, (q, 1), 0) >= pad).astype(F32)
        dt, a_cs = _ssd_pre(dtr_ref[...], dtb_ref[...], alog_ref[...], valid)
        a_cst = a_cs.T
        tri = _tri(q)
        for g in range(SSM_GROUPS):
            bg, cg = b_ref[:, g * n:(g + 1) * n], c_ref[:, g * n:(g + 1) * n]
            cb = _mxdot(cg, bg, 1, 1)
            for hh in range(e):
                h = g * e + hh
                a_col, a_row, a_last = a_cs[:, h:h + 1], a_cst[h:h + 1, :], a_cs[q - 1:q, h:h + 1]
                xdt = xs_ref[:, h * p:(h + 1) * p] * dt[:, h:h + 1]
                ldec = jnp.exp(jnp.where(tri, a_col - a_row, -jnp.inf))
                s_prev = state[h * p:(h + 1) * p, :]
                y = _mxdot(cb * ldec, xdt, 1, 0) + _mxdot(cg, s_prev, 1, 1) * jnp.exp(a_col)
                y_ref[:, h * p:(h + 1) * p] = y
                st = _mxdot(xdt, bg * jnp.exp(a_last - a_col), 0, 0)
                state[h * p:(h + 1) * p, :] = s_prev * jnp.exp(a_last) + st

    return pl.pallas_call(
        body, name="ssd_fwd", grid=(geo.bsz, nc), in_specs=in_specs,
        out_specs=[pl.BlockSpec((q, SSM_D_INNER), lambda b, c: (b * nc + c, 0)),
                   pl.BlockSpec((SSM_D_INNER, n), lambda b, c: (b * nc + c, 0))],
        out_shape=[jax.ShapeDtypeStruct((geo.nrows, SSM_D_INNER), F32),
                   jax.ShapeDtypeStruct((geo.bsz * nc * SSM_D_INNER, n), F32)],
        scratch_shapes=[pltpu.VMEM((SSM_D_INNER, n), F32)],
        compiler_params=_cparams(("parallel", "arbitrary")))(xc, xc, xc, proj, dt_bias, a_log)


def _ssd_bwd(geo, xc, proj, dt_bias, a_log, s_prev_all, dy, dxs_skip):
    q, p, n, e = SSM_CHUNK, SSM_HEAD_DIM, SSM_STATE, geo.nh // SSM_GROUPS
    nc, pad, di, gn = geo.nc, geo.pad, SSM_D_INNER, geo.gn
    in_specs, ci = _ssd_specs(geo, True)
    row_spec = pl.BlockSpec((q, di), lambda b, c: (b * nc + ci(c), 0))
    in_specs += [pl.BlockSpec((di, n), lambda b, c: (b * nc + ci(c), 0)), row_spec, row_spec]

    def body(xs_ref, b_ref, c_ref, dtr_ref, dtb_ref, alog_ref, sp_ref, dy_ref, dsk_ref,
             dxc_ref, ddt_ref, gdtb_ref, galog_ref, dstate):
        step = pl.program_id(1)
        first = jnp.logical_and(pl.program_id(0) == 0, step == 0)
        c = nc - 1 - step

        @pl.when(step == 0)
        def _():
            dstate[...] = jnp.zeros_like(dstate)

        valid = (c * q + lax.broadcasted_iota(jnp.int32, (q, 1), 0) >= pad).astype(F32)
        dtr, dtb, alog = dtr_ref[...], dtb_ref[...], alog_ref[...]
        dt, a_cs = _ssd_pre(dtr, dtb, alog, valid)
        a_cst = a_cs.T
        tri = _tri(q)
        lane = lax.broadcasted_iota(jnp.int32, (1, LANE), 1)
        sub = lax.broadcasted_iota(jnp.int32, (LANE, 1), 0)
        d_dt = jnp.zeros((q, LANE), F32)
        d_acs = jnp.zeros((q, LANE), F32)
        d_acst = jnp.zeros((LANE, q), F32)
        d_last = jnp.zeros((1, LANE), F32)
        for g in range(SSM_GROUPS):
            bg, cg = b_ref[:, g * n:(g + 1) * n], c_ref[:, g * n:(g + 1) * n]
            cb = _mxdot(cg, bg, 1, 1)
            d_cb = jnp.zeros((q, q), F32)
            d_bg = jnp.zeros((q, n), F32)
            d_cg = jnp.zeros((q, n), F32)
            for hh in range(e):
                h = g * e + hh
                hs = slice(h * p, (h + 1) * p)
                a_col, a_row, a_last = a_cs[:, h:h + 1], a_cst[h:h + 1, :], a_cs[q - 1:q, h:h + 1]
                x = xs_ref[:, hs]
                dt_col = dt[:, h:h + 1]
                xdt = x * dt_col
                ldec = jnp.exp(jnp.where(tri, a_col - a_row, -jnp.inf))
                s_prev = sp_ref[hs, :]
                d_snew = dstate[hs, :]
                dyh = dy_ref[:, hs]
                e_col, e_last = jnp.exp(a_col), jnp.exp(a_last)
                dec = jnp.exp(a_last - a_col)
                d_m = _mxdot(dyh, xdt, 1, 1)
                d_xdt = _mxdot(cb * ldec, dyh, 0, 0)
                d_cb = d_cb + d_m * ldec
                d_diff = d_m * cb * ldec
                da_col = jnp.sum(d_diff, axis=1, keepdims=True)
                da_row = -jnp.sum(d_diff, axis=0, keepdims=True)
                cs = _mxdot(cg, s_prev, 1, 1)
                d_cs = dyh * e_col
                da_col = da_col + jnp.sum(dyh * cs, axis=1, keepdims=True) * e_col
                d_cg = d_cg + _mxdot(d_cs, s_prev, 1, 0)
                d_sprev = _mxdot(d_cs, cg, 0, 0) + d_snew * e_last
                dl = jnp.sum(jnp.sum(d_snew * s_prev, axis=1, keepdims=True), axis=0, keepdims=True) * e_last
                d_xdt = d_xdt + _mxdot(bg * dec, d_snew, 1, 1)
                d_bd = _mxdot(xdt, d_snew, 1, 0)
                d_bg = d_bg + d_bd * dec
                d_dec = jnp.sum(d_bd * bg, axis=1, keepdims=True) * dec
                dl = dl + jnp.sum(d_dec, axis=0, keepdims=True)
                da_col = da_col - d_dec
                dstate[hs, :] = d_sprev
                dxc_ref[:, hs] = d_xdt * dt_col + dsk_ref[:, hs]
                onehot = (lane == h).astype(F32)
                d_dt = d_dt + jnp.sum(d_xdt * x, axis=1, keepdims=True) * onehot
                d_acs = d_acs + da_col * onehot
                d_acst = d_acst + (sub == h).astype(F32) * da_row
                d_last = d_last + dl * onehot
            dxc_ref[:, di + g * n:di + (g + 1) * n] = d_bg + _mxdot(d_cb, cg, 0, 0)
            dxc_ref[:, di + gn + g * n:di + gn + (g + 1) * n] = d_cg + _mxdot(d_cb, bg, 1, 0)
        is_last = (lax.broadcasted_iota(jnp.int32, (q, 1), 0) == q - 1).astype(F32)
        d_acs = d_acs + d_acst.T + is_last * d_last
        d_adt = _dot(_tri(q).astype(F32), d_acs, 0, 0, precision=lax.Precision.HIGHEST)
        a = -jnp.exp(alog)
        d_dt = d_dt + d_adt * a
        g_alog = jnp.sum(d_adt * dt, axis=0, keepdims=True) * a
        d_dtr = d_dt * valid * _sigmoid(dtr + dtb)
        ddt_ref[...] = d_dtr.astype(ddt_ref.dtype)
        g_dtb = jnp.sum(d_dtr, axis=0, keepdims=True)

        @pl.when(first)
        def _():
            gdtb_ref[...] = g_dtb
            galog_ref[...] = g_alog

        @pl.when(jnp.logical_not(first))
        def _():
            gdtb_ref[...] += g_dtb
            galog_ref[...] += g_alog

    vec = pl.BlockSpec((1, LANE), lambda b, c: (0, 0))
    return pl.pallas_call(
        body, name="ssd_bwd", grid=(geo.bsz, nc), in_specs=in_specs,
        out_specs=[pl.BlockSpec((q, geo.cd), lambda b, c: (b * nc + ci(c), 0)),
                   pl.BlockSpec((q, LANE), lambda b, c: (b * nc + ci(c), 0)), vec, vec],
        out_shape=[jax.ShapeDtypeStruct((geo.nrows, geo.cd), F32), jax.ShapeDtypeStruct((geo.nrows, LANE), MXU_DTYPE),
                   jax.ShapeDtypeStruct((1, LANE), F32), jax.ShapeDtypeStruct((1, LANE), F32)],
        scratch_shapes=[pltpu.VMEM((di, n), F32)],
        compiler_params=_cparams(("arbitrary", "arbitrary")))(xc, xc, xc, proj, dt_bias, a_log, s_prev_all, dy, dxs_skip)


BIAS_LANE = MLA_ROPE // 2
KEY_OFF = -1e30
ATT_SCALE = (MLA_NOPE + MLA_ROPE) ** -0.5


def _row_t(col):
    return jnp.broadcast_to(col, (col.shape[0], LANE)).T[:8]


def _attn_fwd2(geo, qc, kc, v):
    t, lp = ATT_BLK, geo.lp
    nb = lp // t

    def body(q_ref, k_ref, v_ref, o_ref, lse_ref):
        qi = pl.program_id(2)
        q = q_ref[...]

        def blk(kj, carry, diag):
            m, l, acc = carry
            ks = pl.ds(pl.multiple_of(kj * t, t), t)
            s = _mxdot(q, k_ref[ks, :], 1, 1) * ATT_SCALE
            if diag:
                s = jnp.where(_tri(t), s, -jnp.inf)
            m_new = jnp.maximum(m, jnp.max(s, axis=1, keepdims=True))
            pr = jnp.exp(s - m_new)
            alpha = jnp.exp(m - m_new)
            return m_new, alpha * l + jnp.sum(pr, axis=1, keepdims=True), alpha * acc + _mxdot(pr, v_ref[ks, :], 1, 0)

        init = (jnp.full((t, 1), 2.0 * KEY_OFF, F32), jnp.zeros((t, 1), F32), jnp.zeros((t, LANE), F32))
        carry = lax.fori_loop(0, qi, lambda kj, c: blk(kj, c, False), init)
        m, l, acc = blk(qi, carry, True)
        o_ref[...] = acc / l
        lse_ref[0, 0, 0] = _row_t(m + jnp.log(l))

    return pl.pallas_call(
        body, name="attn_fwd", grid=(geo.bsz, MLA_HEADS, nb),
        in_specs=[pl.BlockSpec((t, 2 * LANE), lambda b, h, i: (b * nb + i, h)),
                  pl.BlockSpec((lp, 2 * LANE), lambda b, h, i: (b, h)), pl.BlockSpec((lp, LANE), lambda b, h, i: (b, h))],
        out_specs=[pl.BlockSpec((t, LANE), lambda b, h, i: (b * nb + i, h)),
                   pl.BlockSpec((1, 1, 1, 8, t), lambda b, h, i: (b, h, i, 0, 0))],
        out_shape=[jax.ShapeDtypeStruct((geo.nrows, geo.hq), F32),
                   jax.ShapeDtypeStruct((geo.bsz, MLA_HEADS, nb, 8, t), F32)],
        compiler_params=_cparams(("parallel", "parallel", "arbitrary")))(qc, kc, v)


def _attn_delta(geo, d_o, o):
    t, nb = ATT_BLK, geo.lp // ATT_BLK

    def body(do_ref, o_ref, dl_ref):
        dl_ref[0, 0, 0] = _row_t(jnp.sum(do_ref[...] * o_ref[...], axis=1, keepdims=True))

    spec = pl.BlockSpec((t, LANE), lambda b, h, i: (b * nb + i, h))
    return pl.pallas_call(
        body, name="attn_delta", grid=(geo.bsz, MLA_HEADS, nb), in_specs=[spec, spec],
        out_specs=pl.BlockSpec((1, 1, 1, 8, t), lambda b, h, i: (b, h, i, 0, 0)),
        out_shape=jax.ShapeDtypeStruct((geo.bsz, MLA_HEADS, nb, 8, t), F32),
        compiler_params=_cparams(("parallel", "parallel", "parallel")))(d_o, o)


def _attn_bwd2(geo, qc, kc, v, d_o, lse, delta):
    t, lp = ATT_BLK, geo.lp
    nb = lp // t

    def body(q_ref, k_ref, v_ref, do_ref, lse_ref, dl_ref, dq_ref, dk_ref, dv_ref):
        kj = pl.program_id(2)

        @pl.when(kj == 0)
        def _():
            dq_ref[...] = jnp.zeros_like(dq_ref)

        k, vv = k_ref[...], v_ref[...]

        def blk(qi, carry, diag):
            dk, dv = carry
            qs = pl.ds(pl.multiple_of(qi * t, t), t)
            q, d_o_blk = q_ref[qs, :], do_ref[qs, :]
            st = _mxdot(k, q, 1, 1) * ATT_SCALE
            if diag:
                keys = lax.broadcasted_iota(jnp.int32, (t, t), 0)
                st = jnp.where(keys <= lax.broadcasted_iota(jnp.int32, (t, t), 1), st, -jnp.inf)
            pt = jnp.exp(st - lse_ref[0, 0, qi][:1, :])
            dst = pt * (_mxdot(vv, d_o_blk, 1, 1) - dl_ref[0, 0, qi][:1, :]) * ATT_SCALE
            dq_ref[qs, :] += _mxdot(dst, k, 0, 0)
            return dk + _mxdot(dst, q, 1, 0), dv + _mxdot(pt, d_o_blk, 1, 0)

        carry = blk(kj, (jnp.zeros((t, 2 * LANE), F32), jnp.zeros((t, LANE), F32)), True)
        dk, dv = lax.fori_loop(kj + 1, nb, lambda qi, c: blk(qi, c, False), carry)
        dk_ref[...] = dk
        dv_ref[...] = dv.astype(dv_ref.dtype)

    rows = pl.BlockSpec((1, 1, nb, 8, t), lambda b, h, j: (b, h, 0, 0, 0))
    return pl.pallas_call(
        body, name="attn_bwd", grid=(geo.bsz, MLA_HEADS, nb),
        in_specs=[pl.BlockSpec((lp, 2 * LANE), lambda b, h, j: (b, h)),
                  pl.BlockSpec((t, 2 * LANE), lambda b, h, j: (b * nb + j, h)),
                  pl.BlockSpec((t, LANE), lambda b, h, j: (b * nb + j, h)),
                  pl.BlockSpec((lp, LANE), lambda b, h, j: (b, h)), rows, rows],
        out_specs=[pl.BlockSpec((lp, 2 * LANE), lambda b, h, j: (b, h)),
                   pl.BlockSpec((t, 2 * LANE), lambda b, h, j: (b * nb + j, h)),
                   pl.BlockSpec((t, LANE), lambda b, h, j: (b * nb + j, h))],
        out_shape=[jax.ShapeDtypeStruct((geo.nrows, 2 * geo.hq), F32), jax.ShapeDtypeStruct((geo.nrows, 2 * geo.hq), F32),
                   jax.ShapeDtypeStruct((geo.nrows, geo.hq), MXU_DTYPE)],
        compiler_params=_cparams(("parallel", "parallel", "arbitrary")))(qc, kc, v, d_o, lse, delta)


def _rope(x, cos, sin):
    return x * cos + pltpu.roll(x, LANE // 2, axis=1) * sin


def _rope_t(dx, cos, sin):
    return dx * cos + pltpu.roll(dx * sin, LANE // 2, axis=1)


def _layer_fwd(geo, h, w, tab, late=None):
    nr, tr, trw = geo.nrows, geo.tr, geo.tr_wide
    tb = geo.lp // tr
    rw = functools.partial(_rowwise, nrows=nr)
    s = {"h": h}
    (s["u"],) = rw("rms_mix", lambda x, g: (_rms(x, g),), tr=tr, rows=[(h, D_MODEL, 0)],
                   vecs=[(w["norm_mix_w"], D_MODEL, 0)], outs=[(D_MODEL, D_MODEL, MXU_DTYPE)])
    proj = s["proj"] = _mm("mm_in", s["u"], w["w_in_p"])
    xc = s["xc"] = _conv_fwd(geo, proj, w["conv_w"], w["conv_b"])
    s["y_ssd"], s["s_prev"] = _ssd_fwd(geo, xc, proj, w["dt_bias"], w["a_log"])
    gw = SSM_D_INNER // SSM_GROUPS

    def gate_norm(y, x, z, dsk, nw):
        return (_rms((y + x * dsk) * _silu(z), nw),)

    (s["y_ssm"],) = rw("ssm_gate_norm", gate_norm, tr=tr, ncb=SSM_GROUPS,
                       rows=[(s["y_ssd"], gw, 0), (xc, gw, 0), (proj, gw, geo.col["z"][0] // gw)],
                       vecs=[(w["d_skip_full"], gw, 0), (w["ssm_norm_w"], gw, 0)], outs=[(SSM_D_INNER, gw, MXU_DTYPE)])
    if late is not None:
        w = {**w, **late(s["y_ssm"])}
    (s["cq_n"],) = rw("rms_q", lambda x, g: (_rms(x, g),), tr=tr, rows=[(proj, MLA_Q_LORA, geo.cb("c_q"))],
                      vecs=[(w["q_norm_w"], MLA_Q_LORA, 0)], outs=[(MLA_Q_LORA, MLA_Q_LORA, MXU_DTYPE)])
    (s["ckv_n"],) = rw("rms_kv", lambda x, g: (_rms(x, g),), tr=tr, rows=[(proj, MLA_KV_LORA, geo.cb("c_kv"))],
                       vecs=[(w["kv_norm_w"], MLA_KV_LORA, 0)], outs=[(MLA_KV_LORA, MLA_KV_LORA, MXU_DTYPE)])
    qn = _mm("mm_qn", s["cq_n"], w["w_qn"])
    qp_raw = _mm("mm_qp", s["cq_n"], w["w_qp"])
    kn = _mm("mm_kn", s["ckv_n"], w["w_k"])
    s["v"] = _mm("mm_v", s["ckv_n"], w["w_v"], out_dtype=MXU_DTYPE)
    bias_lane = lambda: lax.broadcasted_iota(jnp.int32, (1, LANE), 1) == BIAS_LANE

    def q_cat(x, xp, c, sn):
        return (jnp.concatenate([x, jnp.where(bias_lane(), 1.0, _rope(xp, c, sn))], axis=1),)

    def k_cat(x, xp, c, sn, valid):
        return (jnp.concatenate([x, jnp.where(bias_lane(), KEY_OFF * (1.0 - valid), _rope(xp, c, sn))], axis=1),)

    rope_tabs = [(tab["cos"], LANE, 0), (tab["sin"], LANE, 0)]
    (s["qc"],) = rw("rope_q", q_cat, tr=tr, ncb=MLA_HEADS, rows=[(qn, LANE, 0), (qp_raw, LANE, 0)], tabs=rope_tabs,
                    outs=[(2 * geo.hq, 2 * LANE, MXU_DTYPE)], tab_blocks=tb)
    (s["kc"],) = rw("rope_k", k_cat, tr=tr, ncb=MLA_HEADS, rows=[(kn, LANE, 0)], fixed=[(proj, LANE, geo.cb("k_rope"))],
                    tabs=rope_tabs + [(tab["valid"], 1, 0)], outs=[(2 * geo.hq, 2 * LANE, MXU_DTYPE)], tab_blocks=tb)
    s["o"], s["lse"] = _attn_fwd2(geo, s["qc"], s["kc"], s["v"])
    s["ys_p"] = _mm("mm_bs", s["y_ssm"], w["w_branch_ssm"])
    s["ym_p"] = _mm("mm_bm", s["o"], w["w_branch_mla"])

    def gate(gs, gm, ys, ym):
        return (_sigmoid(gs) * ys + _sigmoid(gm) * ym,)

    (s["mixed"],) = rw("gate", gate, tr=tr, rows=[(proj, D_MODEL, geo.cb("g_ssm")), (proj, D_MODEL, geo.cb("g_mla")),
                                                  (s["ys_p"], D_MODEL, 0), (s["ym_p"], D_MODEL, 0)],
                       outs=[(D_MODEL, D_MODEL, MXU_DTYPE)])
    s["h2"] = _mm("mm_out", s["mixed"], w["w_out"], add=h)
    (s["vn"],) = rw("rms_mlp", lambda x, g: (_rms(x, g),), tr=tr, rows=[(s["h2"], D_MODEL, 0)],
                    vecs=[(w["norm_mlp_w"], D_MODEL, 0)], outs=[(D_MODEL, D_MODEL, MXU_DTYPE)])
    s["up"] = _mm("mm_up", s["vn"], w["w_mlp_up"])
    (s["act"],) = rw("relu2", lambda x: (jnp.square(jnp.maximum(x, 0.0)),), tr=trw, rows=[(s["up"], D_FF, 0)],
                     outs=[(D_FF, D_FF, MXU_DTYPE)])
    return _mm("mm_down", s["act"], w["w_mlp_down"], add=s["h2"]), s, w


def _layer_bwd(geo, dh3, s, w, tab, mid=None, tail=None):
    nr, tr, trw = geo.nrows, geo.tr, geo.tr_wide
    tb = geo.lp // tr
    rw = functools.partial(_rowwise, nrows=nr)
    g = {}
    proj = s["proj"]

    def rms_bwd(x, dy, res, gw):
        _, vjp = jax.vjp(_rms, x.astype(F32), gw)
        dx, dgw = vjp(dy.astype(F32))
        return dx + res, dgw

    def rms_bwd_nores(x, dy, gw):
        _, vjp = jax.vjp(_rms, x.astype(F32), gw)
        return vjp(dy.astype(F32))

    dact = _mm("mm_down_t", dh3, w["w_mlp_down"], tb=True)
    g["w_mlp_down"] = _mm("mm_down_g", s["act"], dh3, ta=True, out_dtype=MXU_DTYPE)
    (dup,) = rw("relu2_bwd", lambda d, x: (d * 2.0 * jnp.maximum(x, 0.0),), tr=trw,
                rows=[(dact, D_FF, 0), (s["up"], D_FF, 0)], outs=[(D_FF, D_FF, MXU_DTYPE)])
    g["w_mlp_up"] = _mm("mm_up_g", s["vn"], dup, ta=True, out_dtype=MXU_DTYPE)
    dvn = _mm("mm_up_t", dup, w["w_mlp_up"], tb=True)
    dh2, g["norm_mlp_w"] = rw("rms_mlp_bwd", rms_bwd, tr=tr,
                              rows=[(s["h2"], D_MODEL, 0), (dvn, D_MODEL, 0), (dh3, D_MODEL, 0)],
                              vecs=[(w["norm_mlp_w"], D_MODEL, 0)], outs=[(D_MODEL, D_MODEL, F32)],
                              reds=[(D_MODEL, D_MODEL)])
    dmixed = _mm("mm_out_t", dh2, w["w_out"], tb=True)
    g["w_out"] = _mm("mm_out_g", s["mixed"], dh2, ta=True, out_dtype=MXU_DTYPE)

    def gate_bwd(gs, gm, ys, ym, dm):
        f = lambda a, b, c, d: _sigmoid(a) * c + _sigmoid(b) * d
        _, vjp = jax.vjp(f, gs, gm, ys, ym)
        dgs, dgm, dys, dym = vjp(dm)
        return dys, dym, dgs, dgm

    dys_p, dym_p, dg_ssm, dg_mla = rw(
        "gate_bwd", gate_bwd, tr=tr,
        rows=[(proj, D_MODEL, geo.cb("g_ssm")), (proj, D_MODEL, geo.cb("g_mla")), (s["ys_p"], D_MODEL, 0),
              (s["ym_p"], D_MODEL, 0), (dmixed, D_MODEL, 0)], outs=[(D_MODEL, D_MODEL, MXU_DTYPE)] * 4)
    g["w_branch_ssm"] = _mm("mm_bs_g", s["y_ssm"], dys_p, ta=True, out_dtype=MXU_DTYPE)
    dy_ssm = _mm("mm_bs_t", dys_p, w["w_branch_ssm"], tb=True)
    g["w_branch_mla"] = _mm("mm_bm_g", s["o"], dym_p, ta=True, out_dtype=MXU_DTYPE)
    d_o = _mm("mm_bm_t", dym_p, w["w_branch_mla"], tb=True)
    delta = _attn_delta(geo, d_o, s["o"])
    dqc, dkc, dv = _attn_bwd2(geo, s["qc"], s["kc"], s["v"], d_o, s["lse"], delta)
    rope_tabs = [(tab["cos"], LANE, 0), (tab["sin"], LANE, 0)]
    dqn, dqp_raw = rw("rope_q_bwd", lambda x, c, sn: (x[:, :LANE], _rope_t(x[:, LANE:], c, sn)), tr=tr, ncb=MLA_HEADS,
                      rows=[(dqc, 2 * LANE, 0)], tabs=rope_tabs, outs=[(geo.hq, LANE, MXU_DTYPE)] * 2, tab_blocks=tb)

    def rope_k_bwd(x, c, sn):
        tot = x[:, LANE:2 * LANE]
        for hd in range(1, MLA_HEADS):
            tot = tot + x[:, (2 * hd + 1) * LANE:(2 * hd + 2) * LANE]
        dkn_ = jnp.concatenate([x[:, 2 * hd * LANE:(2 * hd + 1) * LANE] for hd in range(MLA_HEADS)], axis=1)
        return dkn_, _rope_t(tot, c, sn)

    dkn, dk_rope = rw("rope_k_bwd", rope_k_bwd, tr=geo.tr_wide, rows=[(dkc, 2 * geo.hq, 0)], tabs=rope_tabs,
                      outs=[(geo.hq, geo.hq, MXU_DTYPE), (LANE, LANE, MXU_DTYPE)], tab_blocks=geo.lp // geo.tr_wide)
    g["w_qn"] = _mm("mm_qn_g", s["cq_n"], dqn, ta=True, out_dtype=MXU_DTYPE)
    g["w_qp"] = _mm("mm_qp_g", s["cq_n"], dqp_raw, ta=True, out_dtype=MXU_DTYPE)
    dcq_n = _mm("mm_qp_t", dqp_raw, w["w_qp"], tb=True, add=_mm("mm_qn_t", dqn, w["w_qn"], tb=True))
    g["w_k"] = _mm("mm_kn_g", s["ckv_n"], dkn, ta=True, out_dtype=MXU_DTYPE)
    g["w_v"] = _mm("mm_v_g", s["ckv_n"], dv, ta=True, out_dtype=MXU_DTYPE)
    dckv_n = _mm("mm_v_t", dv, w["w_v"], tb=True, add=_mm("mm_kn_t", dkn, w["w_k"], tb=True))
    dc_q, g["q_norm_w"] = rw("rms_q_bwd", rms_bwd_nores, tr=tr,
                             rows=[(proj, MLA_Q_LORA, geo.cb("c_q")), (dcq_n, MLA_Q_LORA, 0)],
                             vecs=[(w["q_norm_w"], MLA_Q_LORA, 0)], outs=[(MLA_Q_LORA, MLA_Q_LORA, MXU_DTYPE)],
                             reds=[(MLA_Q_LORA, MLA_Q_LORA)])
    dc_kv, g["kv_norm_w"] = rw("rms_kv_bwd", rms_bwd_nores, tr=tr,
                               rows=[(proj, MLA_KV_LORA, geo.cb("c_kv")), (dckv_n, MLA_KV_LORA, 0)],
                               vecs=[(w["kv_norm_w"], MLA_KV_LORA, 0)], outs=[(MLA_KV_LORA, MLA_KV_LORA, MXU_DTYPE)],
                               reds=[(MLA_KV_LORA, MLA_KV_LORA)])
    gw_ = SSM_D_INNER // SSM_GROUPS
    d_skip_full = w["d_skip_full"] if mid is None else w["d_skip_full"] + mid(g)[0, 0]

    def gate_norm_bwd(y, x, z, dy, dsk, nw):
        f = lambda y_, x_, z_, dsk_, nw_: _rms((y_ + x_ * dsk_) * _silu(z_), nw_)
        _, vjp = jax.vjp(f, y, x, z, dsk, nw)
        dy_, dx_, dz_, ddsk, dnw = vjp(dy)
        return dy_, dx_, dz_, ddsk, dnw

    dy_ssd, dxs_skip, dz, g["d_skip_full"], g["ssm_norm_w"] = rw(
        "ssm_gate_norm_bwd", gate_norm_bwd, tr=tr, ncb=SSM_GROUPS,
        rows=[(s["y_ssd"], gw_, 0), (s["xc"], gw_, 0), (proj, gw_, geo.col["z"][0] // gw_), (dy_ssm, gw_, 0)],
        vecs=[(d_skip_full, gw_, 0), (w["ssm_norm_w"], gw_, 0)],
        outs=[(SSM_D_INNER, gw_, F32), (SSM_D_INNER, gw_, F32), (SSM_D_INNER, gw_, MXU_DTYPE)],
        reds=[(SSM_D_INNER, gw_), (SSM_D_INNER, gw_)])
    dxc, ddt, g["dt_bias"], g["a_log"] = _ssd_bwd(geo, s["xc"], proj, w["dt_bias"], w["a_log"], s["s_prev"],
                                                   dy_ssd, dxs_skip)
    dxbc, g["conv_w"], g["conv_b"] = _conv_bwd(geo, proj, w["conv_w"], w["conv_b"], dxc)
    di, gn = SSM_D_INNER, geo.gn
    dproj = jnp.concatenate([dz, dxbc[:, :di], dg_ssm, dg_mla, dxbc[:, di:di + gn], dxbc[:, di + gn:], dc_q, dc_kv,
                             ddt, dk_rope], axis=-1)
    g["w_in_p"] = _mm("mm_in_g", s["u"], dproj, ta=True, out_dtype=MXU_DTYPE)
    du = _mm("mm_in_t", dproj, w["w_in_p"], tb=True, dep=None if tail is None else tail(g))
    dh, g["norm_mix_w"] = rw("rms_mix_bwd", rms_bwd, tr=tr,
                             rows=[(s["h"], D_MODEL, 0), (du, D_MODEL, 0), (dh2, D_MODEL, 0)],
                             vecs=[(w["norm_mix_w"], D_MODEL, 0)], outs=[(D_MODEL, D_MODEL, F32)],
                             reds=[(D_MODEL, D_MODEL)])
    return dh, g


def _loss_bwd(geo, h, fw, target, tab):
    tr = geo.tr

    def fn(x, tgt, gw, tok):
        def lossf(x_, gw_):
            err = jnp.square(_rms(x_, gw_) - tgt)
            return 0.5 * jnp.sum(tok * jnp.mean(err, axis=-1, keepdims=True), axis=0, keepdims=True)

        val, vjp = jax.vjp(lossf, x, gw)
        dx, dgw = vjp(jnp.ones((1, 1), F32))
        return dx, jnp.broadcast_to(val, (1, LANE)), dgw

    return _rowwise("loss", fn, nrows=geo.nrows, tr=tr, rows=[(h, D_MODEL, 0), (target, D_MODEL, 0)],
                    vecs=[(fw, D_MODEL, 0)], tabs=[(tab["token"], 1, 0)], outs=[(D_MODEL, D_MODEL, F32)],
                    reds=[(LANE, LANE), (D_MODEL, D_MODEL)], tab_blocks=geo.lp // tr)


def kernel(x, meta_tokens, norm_mix_w, w_in, conv_w, conv_b, dt_bias, a_log, d_skip, ssm_norm_w, q_norm_w, kv_norm_w, w_uq, w_ukv, w_branch_ssm, w_branch_mla, w_out, norm_mlp_w, w_mlp_up, w_mlp_down, final_norm_w, loss_target, m_meta_tokens, m_norm_mix_w, m_w_in, m_conv_w, m_conv_b, m_dt_bias, m_a_log, m_d_skip, m_ssm_norm_w, m_q_norm_w, m_kv_norm_w, m_w_uq, m_w_ukv, m_w_branch_ssm, m_w_branch_mla, m_w_out, m_norm_mlp_w, m_w_mlp_up, m_w_mlp_down, m_final_norm_w, v_meta_tokens, v_norm_mix_w, v_w_in, v_conv_w, v_conv_b, v_dt_bias, v_a_log, v_d_skip, v_ssm_norm_w, v_q_norm_w, v_kv_norm_w, v_w_uq, v_w_ukv, v_w_branch_ssm, v_w_branch_mla, v_w_out, v_norm_mlp_w, v_w_mlp_up, v_w_mlp_down, v_final_norm_w):
    args = dict(locals())
    wts = {n: args[n] for n in WEIGHTS}
    mom = {n: args["m_" + n] for n in WEIGHTS}
    var = {n: args["v_" + n] for n in WEIGHTS}
    bsz, seq, _ = x.shape
    depth = w_in.shape[0]
    geo = _Geo(bsz, seq)
    tab = _tables(geo)

    big_names = [n for n, _ in BIG]
    sh_names = big_names + [n for n, _ in SHARDED_F32]
    kinds = dict(BIG + SHARDED_F32)
    shard3 = lambda a: a.reshape((1,) + a.shape) if a.ndim == 2 else a
    wire = {n: (MXU_DTYPE if n in big_names else F32) for n in sh_names}
    cast = {n: shard3(wts[n]).astype(wire[n]) for n in sh_names}
    per_layer = [n for n in sh_names if n != "meta_tokens"]
    small_names = ["norm_mix_w", "conv_b", "dt_bias", "a_log", "d_skip", "ssm_norm_w", "q_norm_w", "kv_norm_w",
                   "norm_mlp_w"]

    def gather_items(pairs):
        ins, outs, items, forms = [], [], [], []
        for n, i in pairs:
            a, b = cast[n].shape[1:]
            shape, dst, form = _gather_plan(a, b, kinds[n])
            items.append((len(ins), len(outs), (lambda ref, p, i=i: ref.at[i]), dst))
            ins.append(cast[n])
            outs.append(jax.ShapeDtypeStruct(shape, wire[n]))
            forms.append(form)
        return ins, outs, items, forms

    def whole_weights(pairs, forms, got):
        by_layer = {}
        for (n, i), form, g in zip(pairs, forms, got):
            if n == "w_in":
                n, g = "w_in_p", _w_in_assemble(geo, g)
            elif form == "row":
                g = g.reshape(g.shape[0] * g.shape[1], g.shape[2])
            elif form == "stack":
                g = _unshard(g, "col")
            by_layer.setdefault(i, {})[n] = g
        return by_layer

    def prep(i, whole, token=None):
        wl = dict(whole)
        wl.update({n: wts[n][i] for n in small_names})
        if token is not None:
            wl["norm_mix_w"] = wl["norm_mix_w"] + token[0, 0]
        return _prep_layer(geo, wl)

    early = ("w_in", "conv_w")
    late_names = [n for n in per_layer if n not in early]
    pairs1 = [(n, i) for i in range(1, depth) for n in per_layer]
    groups = [[(n, 0) for n in early] + [("meta_tokens", 0)], [(n, 0) for n in late_names]] + ([pairs1] if pairs1 else [])
    started = {}

    def gather_start(gi, dep=None):
        ins, outs, items, forms = gather_items(groups[gi])
        sems, thru, landing, token = _exchange_start("gather_w%d_start" % gi, ins, outs, items, dep)
        started[gi] = (groups[gi], forms, sems, thru, landing, items)
        return token

    def gathered(gi, after):
        pairs, forms, sems, thru, landing, items = started[gi]
        return whole_weights(pairs, forms, _exchange_wait("gather_w%d_wait" % gi, sems, thru, landing, items, after))

    def late0(after):
        whole = gathered(1, after)[0]
        if pairs1:
            gather_start(2, whole["w_out"])
        return _prep_layer(geo, whole)

    gather_start(0)
    token = gather_start(1)
    whole0 = gathered(0, x)[0]
    meta_full = whole0.pop("meta_tokens")

    meta = jnp.broadcast_to(meta_full[None], (bsz, N_META, D_MODEL))
    h = jnp.concatenate([jnp.zeros((bsz, geo.pad, D_MODEL), F32), meta, x], axis=1).reshape(geo.nrows, D_MODEL)
    target = jnp.concatenate([jnp.zeros((bsz, geo.pad + N_META, D_MODEL), F32), loss_target], axis=1)
    target = target.reshape(geo.nrows, D_MODEL)
    layers, saved = [], []
    for i in range(depth):
        if i == 0:
            w, late = prep(0, whole0, token), late0
        else:
            if i == 1:
                whole1 = gathered(2, h)
            w, late = prep(i, whole1[i]), None
        h, s, w = _layer_fwd(geo, h, w, tab, late)
        layers.append(w)
        saved.append(s)
    dh, loss_part, g_final = _loss_bwd(geo, h, final_norm_w.reshape(1, -1), target, tab)

    def scatter_items(pairs):
        ins, outs, items = [], [], []
        for n, i in pairs:
            a, b = cast[n].shape[1:]
            arr = g_meta if n == "meta_tokens" else grads[i]["w_in_p" if n == "w_in" else n]
            if n == "w_in":
                arr, src = _w_in_split(geo, arr, b), _entry
            elif kinds[n] == "row":
                src = lambda ref, p, a=a: ref.at[pl.ds(pl.multiple_of(p * a, a), a)]
            elif b % LANE == 0:
                src = lambda ref, p, b=b: ref.at[:, pl.ds(pl.multiple_of(p * b, b), b)]
            else:
                arr, src = _shard(arr, "col"), _entry
            items.append((len(ins), len(outs), src, _entry))
            ins.append(arr.astype(wire[n]))
            outs.append(jax.ShapeDtypeStruct((N_DEV, a, b), wire[n]))
        return ins, outs, items

    grads = [None] * depth
    landed, pending, res = {}, {}, {}

    def scatter_start(name, pairs):
        ins, outs, items = scatter_items(pairs)
        sems, thru, landing, token = _exchange_start(name + "_start", ins, outs, items)
        pending[name] = (pairs, sems, thru, landing, items)
        return token

    def scatter_wait(name, after):
        pairs, sems, thru, landing, items = pending[name]
        landed.update(zip(pairs, _exchange_wait(name + "_wait", sems, thru, landing, items, after)))

    def adam(n):
        parts = [landed[(n, i)] for i in range(cast[n].shape[0])]
        r = _adamw_nat("adamw_" + n, parts, shard3(wts[n]), shard3(mom[n]), shard3(var[n]))
        res[n] = [a.reshape(wts[n].shape) for a in r]

    def mid0(g):
        grads[0] = _unprep_grads(geo, g)
        return scatter_start("scatter_gb0", [(n, 0) for n in late_names])

    def tail0(g):
        grads[0] = _unprep_grads(geo, g)
        return scatter_start("scatter_ga0", [(n, 0) for n in early])

    for i in reversed(range(depth)):
        dh, gl = _layer_bwd(geo, dh, saved[i], layers[i], tab, *((mid0, tail0) if i == 0 else ()))
        grads[i] = _unprep_grads(geo, gl)
        if i == 1:
            dh = dh + scatter_start("scatter_g1", pairs1)[0, 0]
    dh = dh.reshape(bsz, geo.lp, D_MODEL)
    grad_x = dh[:, geo.pad + N_META:]
    g_meta = jnp.sum(dh[:, geo.pad:geo.pad + N_META], axis=0)
    if pairs1:
        scatter_wait("scatter_g1", g_meta)
    scatter_wait("scatter_gb0", g_meta)
    for n in late_names:
        adam(n)
    g_small = {n: jnp.stack([grads[i][n] for i in range(depth)]) for n in SMALL if n != "final_norm_w"}
    g_small["final_norm_w"] = g_final.reshape(-1)
    zero = jnp.zeros((1,), F32)
    pk = lambda d, last: _pack([d[n] for n in SMALL] + [last], F32, row_mult=8)
    packed = pk(g_small, loss_part[0, :1])
    ins, outs, items = scatter_items([("meta_tokens", 0)])
    parts, landed[("meta_tokens", 0)] = _exchange(
        "gather_g", [packed] + ins, [jax.ShapeDtypeStruct((N_DEV,) + packed.shape, F32)] + outs,
        [(0, 0, _whole, _entry)] + [(1, 1, items[0][2], items[0][3])])
    adam("meta_tokens")
    scatter_wait("scatter_ga0", res["meta_tokens"][1])
    for n in early:
        adam(n)
    res_sm = _adamw("adamw_small", parts, pk(wts, zero), pk(mom, zero), pk(var, zero))
    res_sm = [_unpack(r, [wts[n].shape for n in SMALL] + [(1,)]) for r in res_sm]
    loss = res_sm[0][-1][0]

    out = [loss, grad_x]
    for k in range(4):
        named = {n: res[n][k] for n in sh_names}
        named.update(zip(SMALL, res_sm[k]))
        out += [named[n] for n in WEIGHTS]
    return tuple(out)
```

```python
import functools

import numpy as np
import jax
import jax.numpy as jnp
from jax import lax
from jax.experimental import pallas as pl
from jax.experimental.pallas import tpu as pltpu

F32 = jnp.float32
MXU_DTYPE = jnp.bfloat16

D_MODEL = 1024
N_META = 16
EPS = 1e-6
SSM_D_INNER = 2048
SSM_HEAD_DIM = 64
SSM_GROUPS = 4
SSM_STATE = 128
SSM_CONV = 4
SSM_CHUNK = 128
MLA_HEADS = 8
MLA_Q_LORA = 512
MLA_KV_LORA = 256
MLA_NOPE = 128
MLA_ROPE = 64
MLA_V = 128
ROPE_THETA = 10000.0
D_FF = 4096
ADAM_LR = 0.001
ADAM_B1 = 0.9
ADAM_B2 = 0.999
ADAM_EPS = 1e-08
ADAM_WD = 0.01
ADAM_STEP = 10

N_DEV = 8
ATT_BLK = 256
LANE = 128
PACK_W = 1024
VMEM_LIMIT = 56 * 1024 * 1024
MESH_ID = pl.DeviceIdType.MESH

BIG = (("w_in", "col"), ("w_uq", "col"), ("w_ukv", "col"), ("w_branch_ssm", "row"), ("w_branch_mla", "row"),
       ("w_out", "row"), ("w_mlp_up", "col"), ("w_mlp_down", "row"))
SHARDED_F32 = (("conv_w", "col"), ("meta_tokens", "col"))
SMALL = ("norm_mix_w", "conv_b", "dt_bias", "a_log", "d_skip", "ssm_norm_w", "q_norm_w", "kv_norm_w",
         "norm_mlp_w", "final_norm_w")
WEIGHTS = ("meta_tokens", "norm_mix_w", "w_in", "conv_w", "conv_b", "dt_bias", "a_log", "d_skip", "ssm_norm_w",
           "q_norm_w", "kv_norm_w", "w_uq", "w_ukv", "w_branch_ssm", "w_branch_mla", "w_out", "norm_mlp_w",
           "w_mlp_up", "w_mlp_down", "final_norm_w")


def _cparams(sem=None):
    return pltpu.CompilerParams(dimension_semantics=sem, vmem_limit_bytes=VMEM_LIMIT)


def _pick(n, cands):
    for c in cands:
        if n % c == 0:
            return c
    return n


def _sigmoid(x):
    return 1.0 / (1.0 + jnp.exp(-x))


def _silu(x):
    return x * _sigmoid(x)


def _softplus(x):
    return jnp.maximum(x, 0.0) + jnp.log1p(jnp.exp(-jnp.abs(x)))


def _rms(x, w):
    return x * lax.rsqrt(jnp.mean(x * x, axis=-1, keepdims=True) + EPS) * w


def _dot(a, b, ca, cb, precision=None):
    return lax.dot_general(a, b, (((ca,), (cb,)), ((), ())), preferred_element_type=F32, precision=precision)


def _mxdot(a, b, ca, cb):
    return _dot(a.astype(MXU_DTYPE), b.astype(MXU_DTYPE), ca, cb)


def _mm(name, a, b, *, ta=False, tb=False, add=None, out_dtype=F32, dep=None):
    (kdim, m) = a.shape if ta else a.shape[::-1]
    (n, k2) = b.shape if tb else b.shape[::-1]
    assert kdim == k2, (name, a.shape, b.shape)
    tm = _pick(m, (1152, 1024, 768, 512, 384, 256, 128))
    tn = _pick(n, (1024, 512, 384, 256, 128))
    tk = _pick(kdim, (1152, 1024, 768, 512, 384, 256, 128))
    nk = kdim // tk
    a_spec = pl.BlockSpec((tk, tm), lambda i, j, k: (k, i)) if ta else pl.BlockSpec((tm, tk), lambda i, j, k: (i, k))
    b_spec = pl.BlockSpec((tn, tk), lambda i, j, k: (j, k)) if tb else pl.BlockSpec((tk, tn), lambda i, j, k: (k, j))
    o_spec = pl.BlockSpec((tm, tn), lambda i, j, k: (i, j))
    ca, cb = (0 if ta else 1), (1 if tb else 0)

    def body(*refs):
        a_ref, b_ref = refs[:2]
        o_ref, acc = refs[-2:]
        k = pl.program_id(2)

        @pl.when(k == 0)
        def _():
            acc[...] = jnp.zeros_like(acc)

        acc[...] += _mxdot(a_ref[...], b_ref[...], ca, cb)

        @pl.when(k == nk - 1)
        def _():
            r = acc[...]
            if add is not None:
                r = r + refs[2][...].astype(F32)
            o_ref[...] = r.astype(out_dtype)

    in_specs, args = [a_spec, b_spec], [a, b]
    if add is not None:
        in_specs.append(o_spec)
        args.append(add)
    if dep is not None:
        in_specs.append(pl.BlockSpec((8, LANE), lambda i, j, k: (0, 0)))
        args.append(dep)
    return pl.pallas_call(
        body, name=name, grid=(m // tm, n // tn, nk), in_specs=in_specs, out_specs=o_spec,
        out_shape=jax.ShapeDtypeStruct((m, n), out_dtype), scratch_shapes=[pltpu.VMEM((tm, tn), F32)],
        compiler_params=_cparams(("parallel", "parallel", "arbitrary")))(*args)


def _rowwise(name, fn, *, nrows, tr, ncb=1, rows=(), fixed=(), vecs=(), tabs=(), outs=(), reds=(), tab_blocks=1):
    in_specs, args = [], []
    for arr, w, c0 in rows:
        in_specs.append(pl.BlockSpec((tr, w), lambda g, i, c0=c0: (i, c0 + g)))
        args.append(arr)
    for arr, w, c0 in fixed:
        in_specs.append(pl.BlockSpec((tr, w), lambda g, i, c0=c0: (i, c0)))
        args.append(arr)
    for arr, w, c0 in vecs:
        in_specs.append(pl.BlockSpec((1, w), lambda g, i, c0=c0: (0, c0 + g)))
        args.append(arr)
    for arr, w, c0 in tabs:
        in_specs.append(pl.BlockSpec((tr, w), lambda g, i, c0=c0: (i % tab_blocks, c0)))
        args.append(arr)
    out_shape = [jax.ShapeDtypeStruct((nrows, wt), dt) for wt, w, dt in outs]
    out_shape += [jax.ShapeDtypeStruct((1, wt), F32) for wt, w in reds]
    out_specs = [pl.BlockSpec((tr, w), lambda g, i: (i, g)) for wt, w, dt in outs]
    out_specs += [pl.BlockSpec((1, w), lambda g, i: (0, g)) for wt, w in reds]
    n_in, n_out = len(args), len(outs)

    def body(*refs):
        res = fn(*[r[...] for r in refs[:n_in]])
        for o_ref, val in zip(refs[n_in:n_in + n_out], res[:n_out]):
            o_ref[...] = val.astype(o_ref.dtype)
        i = pl.program_id(1)
        for d_ref, val in zip(refs[n_in + n_out:], res[n_out:]):
            @pl.when(i == 0)
            def _(d_ref=d_ref, val=val):
                d_ref[...] = val

            @pl.when(i > 0)
            def _(d_ref=d_ref, val=val):
                d_ref[...] += val

    res = pl.pallas_call(
        body, name=name, grid=(ncb, nrows // tr), in_specs=in_specs, out_specs=out_specs, out_shape=out_shape,
        compiler_params=_cparams(("parallel", "arbitrary")))(*args)
    return res


def _peer(k):
    x, y, c = lax.axis_index("x"), lax.axis_index("y"), lax.axis_index("c")
    px = jnp.where((k >> 2) & 1, 1 - x, x)
    py = jnp.where((k >> 1) & 1, 1 - y, y)
    pc = jnp.where(k & 1, 1 - c, c)
    return (px, py, pc), 4 * px + 2 * py + pc


def _my_index():
    return 4 * lax.axis_index("x") + 2 * lax.axis_index("y") + lax.axis_index("c")


def _exchange(name, ins, out_shapes, items):
    n_in, n_out, n_it = len(ins), len(out_shapes), len(items)

    def body(*refs):
        x, o = refs[:n_in], refs[n_in:n_in + n_out]
        send_sems, recv_sems, local_sems = refs[n_in + n_out:]
        me = _my_index()
        local, sends = [], []
        for t, (ii, io, src, dst) in enumerate(items):
            cp = pltpu.make_async_copy(src(x[ii], me), dst(o[io], me), local_sems.at[t])
            cp.start()
            local.append(cp)
        for k in range(1, N_DEV):
            dev, idx = _peer(k)
            for t, (ii, io, src, dst) in enumerate(items):
                s = (k - 1) * n_it + t
                cp = pltpu.make_async_remote_copy(
                    src_ref=src(x[ii], idx), dst_ref=dst(o[io], me), send_sem=send_sems.at[s],
                    recv_sem=recv_sems.at[s], device_id=dev, device_id_type=MESH_ID)
                cp.start()
                sends.append(cp)
        for k in range(1, N_DEV):
            dev, idx = _peer(k)
            for t, (ii, io, src, dst) in enumerate(items):
                s = (k - 1) * n_it + t
                pltpu.make_async_remote_copy(
                    src_ref=src(x[ii], idx), dst_ref=dst(o[io], idx), send_sem=send_sems.at[s],
                    recv_sem=recv_sems.at[s], device_id=dev, device_id_type=MESH_ID).wait_recv()
        for cp in sends:
            cp.wait_send()
        for cp in local:
            cp.wait()

    nsem = (N_DEV - 1) * n_it
    anyspec = pl.BlockSpec(memory_space=pl.ANY)
    return pl.pallas_call(
        body, name=name, out_shape=list(out_shapes), in_specs=[anyspec] * n_in, out_specs=[anyspec] * n_out,
        scratch_shapes=[pltpu.SemaphoreType.DMA((nsem,)), pltpu.SemaphoreType.DMA((nsem,)),
                        pltpu.SemaphoreType.DMA((n_it,))],
        compiler_params=pltpu.CompilerParams(has_side_effects=True))(*ins)


def _split_copies(x, land, send_sems, recv_sems, items, receive):
    me = _my_index()
    remote, n_it = [], len(items)
    for k in range(1, N_DEV):
        dev, idx = _peer(k)
        for t, (ii, io, src, dst) in enumerate(items):
            s = (k - 1) * n_it + t
            remote.append(pltpu.make_async_remote_copy(
                src_ref=src(x[ii], idx), dst_ref=dst(land[io], idx if receive else me), send_sem=send_sems.at[s],
                recv_sem=recv_sems.at[s], device_id=dev, device_id_type=MESH_ID))
    local = [pltpu.make_async_copy(src(x[ii], me), dst(land[io], me), send_sems.at[(N_DEV - 1) * n_it + t])
             for t, (ii, io, src, dst) in enumerate(items)]
    return remote, local


def _exchange_start(name, ins, out_shapes, items, dep=None):
    n_in, n_out, n_it = len(ins), len(out_shapes), len(items)

    def body(*refs):
        x, land = refs[:n_in], refs[n_in:n_in + n_out]
        first_out = n_in + n_out + (dep is not None)
        send_sems, recv_sems, token = refs[first_out], refs[first_out + 1], refs[-1]
        remote, local = _split_copies(x, land, send_sems, recv_sems, items, False)
        for cp in remote + local:
            cp.start()
        token[...] = jnp.zeros_like(token)

    hbm = pl.BlockSpec(memory_space=pltpu.HBM)
    sem = pl.BlockSpec(memory_space=pltpu.SEMAPHORE)
    arrs = [pltpu.with_memory_space_constraint(a, pltpu.HBM)
            for a in list(ins) + [lax.empty(s.shape, s.dtype) for s in out_shapes]]
    res = pl.pallas_call(
        body, name=name,
        out_shape=(pltpu.SemaphoreType.DMA((N_DEV * n_it,)), pltpu.SemaphoreType.DMA(((N_DEV - 1) * n_it,)),
                   *[pltpu.HBM(a.shape, a.dtype) for a in arrs], jax.ShapeDtypeStruct((8, LANE), F32)),
        in_specs=[hbm] * (n_in + n_out) + ([] if dep is None else [pl.BlockSpec(memory_space=pl.ANY)]),
        out_specs=(sem, sem, *[hbm] * (n_in + n_out), pl.BlockSpec(memory_space=pltpu.VMEM)),
        input_output_aliases={i: 2 + i for i in range(n_in + n_out)},
        compiler_params=pltpu.CompilerParams(has_side_effects=pltpu.SideEffectType.DATAFLOW_SIDE_EFFECTING))(
            *arrs, *([] if dep is None else [dep]))
    return res[:2], res[2:2 + n_in], res[2 + n_in:2 + n_in + n_out], res[-1]


def _exchange_wait(name, sems, ins, landing, items, after):
    n_in, n_out = len(ins), len(landing)

    def body(*refs):
        x, land = refs[:n_in], refs[n_in:n_in + n_out]
        send_sems, recv_sems = refs[n_in + n_out], refs[n_in + n_out + 1]
        remote, local = _split_copies(x, land, send_sems, recv_sems, items, True)
        for cp in remote:
            cp.wait_send()
            cp.wait_recv()
        for cp in local:
            cp.wait()

    hbm = pl.BlockSpec(memory_space=pltpu.HBM)
    sem = pl.BlockSpec(memory_space=pltpu.SEMAPHORE)
    arrs = list(ins) + list(landing)
    res = pl.pallas_call(
        body, name=name, out_shape=tuple(pltpu.HBM(a.shape, a.dtype) for a in arrs),
        in_specs=[hbm] * (n_in + n_out) + [sem, sem, pl.BlockSpec(memory_space=pl.ANY)],
        out_specs=tuple([hbm] * (n_in + n_out)), input_output_aliases={i: i for i in range(n_in + n_out)},
        compiler_params=pltpu.CompilerParams(has_side_effects=pltpu.SideEffectType.DATAFLOW_SIDE_EFFECTING))(
            *arrs, *sems, after)
    return res[n_in:]


def _whole(ref, p):
    return ref


def _entry(ref, p):
    return ref.at[p]


def _gather_plan(a, b, kind):
    if kind == "col" and b % LANE == 0:
        return (a, N_DEV * b), (lambda ref, p: ref.at[:, pl.ds(pl.multiple_of(p * b, b), b)]), "col"
    return (N_DEV, a, b), _entry, ("row" if kind == "row" else "stack")


def _adamw_nat(name, parts, w, m, v):
    depth, b, c = w.shape
    assert len(parts) == depth
    tb = _pick(b, (128, 64, 32, 16, 8))
    spec = pl.BlockSpec((1, tb, c), lambda i, j: (i, j, 0))

    def body(*refs):
        p_refs = refs[:depth]
        w_ref, m_ref, v_ref, g_ref, d_ref, nm_ref, nv_ref = refs[depth:]
        for layer, p_ref in enumerate(p_refs):
            @pl.when(pl.program_id(0) == layer)
            def _(p_ref=p_ref):
                g = p_ref[0].astype(F32)
                for j in range(1, N_DEV):
                    g = g + p_ref[j].astype(F32)
                nm = ADAM_B1 * m_ref[0] + (1.0 - ADAM_B1) * g
                nv = ADAM_B2 * v_ref[0] + (1.0 - ADAM_B2) * jnp.square(g)
                m_hat = nm / (1.0 - ADAM_B1 ** ADAM_STEP)
                v_hat = nv / (1.0 - ADAM_B2 ** ADAM_STEP)
                g_ref[0] = g
                d_ref[0] = -ADAM_LR * (m_hat / (jnp.sqrt(v_hat) + ADAM_EPS) + ADAM_WD * w_ref[0])
                nm_ref[0] = nm
                nv_ref[0] = nv

    sds = jax.ShapeDtypeStruct((depth, b, c), F32)
    return pl.pallas_call(
        body, name=name, grid=(depth, b // tb),
        in_specs=[pl.BlockSpec((N_DEV, tb, c), lambda i, j: (0, j, 0))] * depth + [spec, spec, spec],
        out_specs=[spec] * 4, out_shape=[sds] * 4, compiler_params=_cparams(("parallel", "parallel")))(*parts, w, m, v)


def _adamw(name, parts, w, m, v):
    rows = w.shape[0]
    tr = _pick(rows, (256, 128, 64, 32, 16, 8))
    spec = pl.BlockSpec((tr, PACK_W), lambda i: (i, 0))

    def body(p_ref, w_ref, m_ref, v_ref, g_ref, d_ref, nm_ref, nv_ref):
        g = p_ref[0]
        for j in range(1, N_DEV):
            g = g + p_ref[j]
        nm = ADAM_B1 * m_ref[...] + (1.0 - ADAM_B1) * g
        nv = ADAM_B2 * v_ref[...] + (1.0 - ADAM_B2) * jnp.square(g)
        m_hat = nm / (1.0 - ADAM_B1 ** ADAM_STEP)
        v_hat = nv / (1.0 - ADAM_B2 ** ADAM_STEP)
        g_ref[...] = g
        d_ref[...] = -ADAM_LR * (m_hat / (jnp.sqrt(v_hat) + ADAM_EPS) + ADAM_WD * w_ref[...])
        nm_ref[...] = nm
        nv_ref[...] = nv

    sds = jax.ShapeDtypeStruct((rows, PACK_W), F32)
    return pl.pallas_call(
        body, name=name, grid=(rows // tr,),
        in_specs=[pl.BlockSpec((N_DEV, tr, PACK_W), lambda i: (0, i, 0)), spec, spec, spec],
        out_specs=[spec] * 4, out_shape=[sds] * 4, compiler_params=_cparams(("parallel",)))(parts, w, m, v)


def _pack(arrs, dtype, row_mult=16):
    flat = jnp.concatenate([a.reshape(-1).astype(dtype) for a in arrs])
    unit = row_mult * PACK_W
    total = -(-flat.shape[0] // unit) * unit
    flat = jnp.pad(flat, (0, total - flat.shape[0]))
    return flat.reshape(-1, PACK_W)


def _pack_lead(arrs, dtype, row_mult):
    flat = jnp.concatenate([a.reshape(N_DEV, -1).astype(dtype) for a in arrs], axis=1)
    unit = row_mult * PACK_W
    total = -(-flat.shape[1] // unit) * unit
    flat = jnp.pad(flat, ((0, 0), (0, total - flat.shape[1])))
    return flat.reshape(N_DEV, -1, PACK_W)


def _unpack(buf, shapes, lead=()):
    flat = buf.reshape(lead + (-1,))
    out, off = [], 0
    for s in shapes:
        n = int(np.prod(s))
        out.append(flat[..., off:off + n].reshape(lead + tuple(s)))
        off += n
    return out


def _unshard(g, kind):
    if kind == "col":
        g = jnp.moveaxis(g, 0, -2)
        return g.reshape(g.shape[:-2] + (g.shape[-2] * g.shape[-1],))
    g = jnp.moveaxis(g, 0, 1)
    return g.reshape((g.shape[0], g.shape[1] * g.shape[2]) + g.shape[3:])


def _shard(full, kind):
    if kind == "col":
        s = full.reshape(full.shape[:-1] + (N_DEV, full.shape[-1] // N_DEV))
        return jnp.moveaxis(s, -2, 0)
    s = full.reshape((full.shape[0], N_DEV, full.shape[1] // N_DEV) + full.shape[2:])
    return jnp.moveaxis(s, 1, 0)


class _Geo:
    def __init__(self, bsz, seq):
        self.bsz, self.seq = bsz, seq
        self.pad = (-(N_META + seq)) % ATT_BLK
        self.lp = self.pad + N_META + seq
        assert (self.pad + N_META) % SSM_CHUNK == 0 and self.lp % SSM_CHUNK == 0
        self.nrows = bsz * self.lp
        self.nc = self.lp // SSM_CHUNK
        self.nh = SSM_D_INNER // SSM_HEAD_DIM
        self.gn = SSM_GROUPS * SSM_STATE
        self.cd = SSM_D_INNER + 2 * self.gn
        self.hq = MLA_HEADS * LANE
        order = (("z", SSM_D_INNER), ("xs", SSM_D_INNER), ("g_ssm", D_MODEL), ("g_mla", D_MODEL), ("bm", self.gn),
                 ("cm", self.gn), ("c_q", MLA_Q_LORA), ("c_kv", MLA_KV_LORA), ("dt", LANE), ("k_rope", LANE))
        self.col, off = {}, 0
        for nm, w in order:
            assert off % w == 0, (nm, off, w)
            self.col[nm] = (off, w)
            off += w
        self.pw = off
        assert self.nh <= LANE and MLA_ROPE == 64 and MLA_NOPE == LANE and MLA_V == LANE
        self.tr = _pick(self.lp, (768, 512, 384, 256, 128))
        self.tr_wide = _pick(self.lp, (384, 256, 128))

    def cb(self, nm):
        off, w = self.col[nm]
        return off // w

    def w_in_runs(self, shard_w):
        nh, half = self.nh, MLA_ROPE // 2
        src, pieces = 0, []
        for nm, n in (("z", SSM_D_INNER), ("xs", SSM_D_INNER), ("bm", self.gn), ("cm", self.gn), ("dt", nh),
                      ("c_q", MLA_Q_LORA), ("c_kv", MLA_KV_LORA), ("k_rope", MLA_ROPE), ("g_ssm", D_MODEL),
                      ("g_mla", D_MODEL)):
            dst = self.col[nm][0]
            if nm == "k_rope":
                pieces += [(src, half, dst), (src + half, half, dst + 2 * half)]
            else:
                pieces.append((src, n, dst))
            src += n
        assert src == shard_w * N_DEV
        runs = []
        for a, n, dst in pieces:
            for j in range(N_DEV):
                lo, hi = max(a, j * shard_w), min(a + n, (j + 1) * shard_w)
                if lo < hi:
                    runs.append((j, lo - j * shard_w, hi - lo, dst + lo - a))
        return runs


def _slot(a):
    h = MLA_ROPE // 2
    z = jnp.zeros(a.shape[:-1] + (h,), a.dtype)
    return jnp.concatenate([a[..., :h], z, a[..., h:], z], axis=-1)


def _unslot(a):
    h = MLA_ROPE // 2
    return jnp.concatenate([a[..., :h], a[..., 2 * h:3 * h]], axis=-1)


def _prep_layer(geo, wl):
    nh = geo.nh
    p = {}
    if "w_uq" in wl:
        uq = wl["w_uq"].reshape(MLA_Q_LORA, MLA_HEADS, MLA_NOPE + MLA_ROPE)
        p["w_qn"] = uq[..., :MLA_NOPE].reshape(MLA_Q_LORA, geo.hq)
        p["w_qp"] = _slot(uq[..., MLA_NOPE:]).reshape(MLA_Q_LORA, geo.hq)
    if "w_ukv" in wl:
        ukv = wl["w_ukv"].reshape(MLA_KV_LORA, MLA_HEADS, MLA_NOPE + MLA_V)
        p["w_k"] = ukv[..., :MLA_NOPE].reshape(MLA_KV_LORA, geo.hq)
        p["w_v"] = ukv[..., MLA_NOPE:].reshape(MLA_KV_LORA, geo.hq)
    for nm in ("w_in_p", "conv_w", "w_branch_ssm", "w_branch_mla", "w_out", "w_mlp_up", "w_mlp_down"):
        if nm in wl:
            p[nm] = wl[nm]
    for nm in ("norm_mix_w", "conv_b", "ssm_norm_w", "q_norm_w", "kv_norm_w", "norm_mlp_w"):
        if nm in wl:
            p[nm] = wl[nm].reshape(1, -1)
    if "dt_bias" in wl:
        p["dt_bias"] = jnp.pad(wl["dt_bias"], (0, LANE - nh)).reshape(1, LANE)
        p["a_log"] = jnp.pad(wl["a_log"], (0, LANE - nh)).reshape(1, LANE)
        p["d_skip_full"] = jnp.repeat(wl["d_skip"], SSM_HEAD_DIM).reshape(1, SSM_D_INNER)
    return p


def _unprep_grads(geo, g):
    nh = geo.nh
    out = {}
    if "w_qn" in g:
        qn = g["w_qn"].reshape(MLA_Q_LORA, MLA_HEADS, MLA_NOPE)
        qp = _unslot(g["w_qp"].reshape(MLA_Q_LORA, MLA_HEADS, LANE))
        out["w_uq"] = jnp.concatenate([qn, qp], axis=-1).reshape(MLA_Q_LORA, -1)
    if "w_k" in g:
        wk = g["w_k"].reshape(MLA_KV_LORA, MLA_HEADS, MLA_NOPE)
        wv = g["w_v"].reshape(MLA_KV_LORA, MLA_HEADS, MLA_V)
        out["w_ukv"] = jnp.concatenate([wk, wv], axis=-1).reshape(MLA_KV_LORA, -1)
    for nm in ("w_in_p", "w_branch_ssm", "w_branch_mla", "w_out", "w_mlp_up", "w_mlp_down", "conv_w"):
        if nm in g:
            out[nm] = g[nm]
    for nm in ("norm_mix_w", "conv_b", "ssm_norm_w", "q_norm_w", "kv_norm_w", "norm_mlp_w"):
        if nm in g:
            out[nm] = g[nm].reshape(-1)
    if "dt_bias" in g:
        out["dt_bias"] = g["dt_bias"].reshape(-1)[:nh]
        out["a_log"] = g["a_log"].reshape(-1)[:nh]
        out["d_skip"] = g["d_skip_full"].reshape(nh, SSM_HEAD_DIM).sum(-1)
    return out


def _tables(geo):
    pos = jnp.arange(geo.lp, dtype=F32) - geo.pad
    inv = ROPE_THETA ** (-jnp.arange(0, MLA_ROPE, 2, dtype=F32) / MLA_ROPE)
    ang = pos[:, None] * inv[None, :]
    cos, sin = jnp.cos(ang), jnp.sin(ang)
    z = jnp.zeros_like(cos)
    rows = jnp.arange(geo.lp)[:, None]
    return {"cos": jnp.concatenate([cos, z, cos, z], axis=-1), "sin": jnp.concatenate([-sin, z, sin, z], axis=-1),
            "valid": (rows >= geo.pad).astype(F32), "token": (rows >= geo.pad + N_META).astype(F32)}


def _w_in_assemble(geo, gathered):
    _, d, sw = gathered.shape
    runs = geo.w_in_runs(sw)
    tr = _pick(d, (256, 128))

    def body(x_ref, o_ref):
        o_ref[...] = jnp.zeros_like(o_ref)
        for j, s0, n, d0 in runs:
            o_ref[:, d0:d0 + n] = x_ref[j, :, s0:s0 + n]

    return pl.pallas_call(
        body, name="w_in_assemble", grid=(d // tr,), in_specs=[pl.BlockSpec((N_DEV, tr, sw), lambda i: (0, i, 0))],
        out_specs=pl.BlockSpec((tr, geo.pw), lambda i: (i, 0)),
        out_shape=jax.ShapeDtypeStruct((d, geo.pw), gathered.dtype), compiler_params=_cparams(("parallel",)))(gathered)


def _w_in_split(geo, g_padded, sw):
    d = g_padded.shape[0]
    runs = geo.w_in_runs(sw)
    tr = _pick(d, (128,))

    def body(x_ref, o_ref):
        for j, s0, n, d0 in runs:
            o_ref[j, :, s0:s0 + n] = x_ref[:, d0:d0 + n]

    return pl.pallas_call(
        body, name="w_in_split", grid=(d // tr,), in_specs=[pl.BlockSpec((tr, geo.pw), lambda i: (i, 0))],
        out_specs=pl.BlockSpec((N_DEV, tr, sw), lambda i: (0, i, 0)),
        out_shape=jax.ShapeDtypeStruct((N_DEV, d, sw), g_padded.dtype),
        compiler_params=_cparams(("parallel",)))(g_padded)


def _conv_cols(geo, cbw):
    nx = SSM_D_INNER // cbw
    x0, b0 = geo.col["xs"][0] // cbw, geo.col["bm"][0] // cbw
    assert geo.col["cm"][0] == geo.col["bm"][0] + geo.gn
    return lambda j: jnp.where(j < nx, x0 + j, b0 + j - nx)


def _conv_pre(x, w_ref, b_ref):
    acc = b_ref[...] + x * w_ref[SSM_CONV - 1:SSM_CONV, :]
    for k in range(SSM_CONV - 1):
        acc = acc + pltpu.roll(x, SSM_CONV - 1 - k, axis=0) * w_ref[k:k + 1, :]
    return acc


def _conv_fwd(geo, proj, conv_w, conv_b):
    cbw = 256
    colmap = _conv_cols(geo, cbw)
    lp, pad = geo.lp, geo.pad

    def body(x_ref, w_ref, b_ref, o_ref):
        valid = (lax.broadcasted_iota(jnp.int32, (lp, 1), 0) >= pad).astype(F32)
        o_ref[...] = _silu(_conv_pre(x_ref[...], w_ref, b_ref)) * valid

    return pl.pallas_call(
        body, name="conv_fwd", grid=(geo.bsz, geo.cd // cbw),
        in_specs=[pl.BlockSpec((lp, cbw), lambda b, j: (b, colmap(j))),
                  pl.BlockSpec((SSM_CONV, cbw), lambda b, j: (0, j)), pl.BlockSpec((1, cbw), lambda b, j: (0, j))],
        out_specs=pl.BlockSpec((lp, cbw), lambda b, j: (b, j)),
        out_shape=jax.ShapeDtypeStruct((geo.nrows, geo.cd), F32),
        compiler_params=_cparams(("parallel", "parallel")))(proj, conv_w, conv_b)


def _conv_bwd(geo, proj, conv_w, conv_b, dxc):
    cbw = 256
    colmap = _conv_cols(geo, cbw)
    lp, pad = geo.lp, geo.pad

    def body(x_ref, w_ref, b_ref, dy_ref, dx_ref, gw_ref, gb_ref):
        b = pl.program_id(1)
        valid = (lax.broadcasted_iota(jnp.int32, (lp, 1), 0) >= pad).astype(F32)
        x = x_ref[...]
        pre = _conv_pre(x, w_ref, b_ref)
        sig = _sigmoid(pre)
        dpre = dy_ref[...] * (sig * (1.0 + pre * (1.0 - sig))) * valid
        dx = dpre * w_ref[SSM_CONV - 1:SSM_CONV, :]
        gws = [jnp.sum(dpre * x, axis=0, keepdims=True)]
        for k in range(SSM_CONV - 2, -1, -1):
            s = SSM_CONV - 1 - k
            dx = dx + pltpu.roll(dpre, lp - s, axis=0) * w_ref[k:k + 1, :]
            gws.insert(0, jnp.sum(dpre * pltpu.roll(x, s, axis=0), axis=0, keepdims=True))
        dx_ref[...] = (dx * valid).astype(dx_ref.dtype)

        @pl.when(b == 0)
        def _():
            gw_ref[...] = jnp.zeros_like(gw_ref)
            gb_ref[...] = jnp.zeros_like(gb_ref)

        for k in range(SSM_CONV):
            gw_ref[k:k + 1, :] += gws[k]
        gb_ref[...] += jnp.sum(dpre, axis=0, keepdims=True)

    return pl.pallas_call(
        body, name="conv_bwd", grid=(geo.cd // cbw, geo.bsz),
        in_specs=[pl.BlockSpec((lp, cbw), lambda j, b: (b, colmap(j))),
                  pl.BlockSpec((SSM_CONV, cbw), lambda j, b: (0, j)), pl.BlockSpec((1, cbw), lambda j, b: (0, j)),
                  pl.BlockSpec((lp, cbw), lambda j, b: (b, j))],
        out_specs=[pl.BlockSpec((lp, cbw), lambda j, b: (b, j)), pl.BlockSpec((SSM_CONV, cbw), lambda j, b: (0, j)),
                   pl.BlockSpec((1, cbw), lambda j, b: (0, j))],
        out_shape=[jax.ShapeDtypeStruct((geo.nrows, geo.cd), MXU_DTYPE),
                   jax.ShapeDtypeStruct((SSM_CONV, geo.cd), F32), jax.ShapeDtypeStruct((1, geo.cd), F32)],
        compiler_params=_cparams(("parallel", "arbitrary")))(proj, conv_w, conv_b, dxc)


def _tri(q):
    r = lax.broadcasted_iota(jnp.int32, (q, q), 0)
    c = lax.broadcasted_iota(jnp.int32, (q, q), 1)
    return r >= c


def _ssd_pre(dtr, dtb, alog, valid):
    dt = _softplus(dtr + dtb) * valid
    adt = dt * (-jnp.exp(alog))
    a_cs = _dot(_tri(SSM_CHUNK).astype(F32), adt, 1, 0, precision=lax.Precision.HIGHEST)
    return dt, a_cs


def _ssd_specs(geo, rev):
    nc, q = geo.nc, SSM_CHUNK
    ci = (lambda c: nc - 1 - c) if rev else (lambda c: c)
    nxb = SSM_D_INNER // geo.gn
    return [pl.BlockSpec((q, SSM_D_INNER), lambda b, c: (b * nc + ci(c), 0)),
            pl.BlockSpec((q, geo.gn), lambda b, c: (b * nc + ci(c), nxb)),
            pl.BlockSpec((q, geo.gn), lambda b, c: (b * nc + ci(c), nxb + 1)),
            pl.BlockSpec((q, LANE), lambda b, c: (b * nc + ci(c), geo.cb("dt"))),
            pl.BlockSpec((1, LANE), lambda b, c: (0, 0)), pl.BlockSpec((1, LANE), lambda b, c: (0, 0))], ci


def _ssd_fwd(geo, xc, proj, dt_bias, a_log):
    q, p, n, e = SSM_CHUNK, SSM_HEAD_DIM, SSM_STATE, geo.nh // SSM_GROUPS
    nc, pad = geo.nc, geo.pad
    in_specs, _ = _ssd_specs(geo, False)

    def body(xs_ref, b_ref, c_ref, dtr_ref, dtb_ref, alog_ref, y_ref, sp_ref, state):
        c = pl.program_id(1)

        @pl.when(c == 0)
        def _():
            state[...] = jnp.zeros_like(state)

        sp_ref[...] = state[...]
        valid = (c * q + lax.broadcasted_iota(jnp.int32, (q, 1), 0) >= pad).astype(F32)
        dt, a_cs = _ssd_pre(dtr_ref[...], dtb_ref[...], alog_ref[...], valid)
        a_cst = a_cs.T
        tri = _tri(q)
        for g in range(SSM_GROUPS):
            bg, cg = b_ref[:, g * n:(g + 1) * n], c_ref[:, g * n:(g + 1) * n]
            cb = _mxdot(cg, bg, 1, 1)
            for hh in range(e):
                h = g * e + hh
                a_col, a_row, a_last = a_cs[:, h:h + 1], a_cst[h:h + 1, :], a_cs[q - 1:q, h:h + 1]
                xdt = xs_ref[:, h * p:(h + 1) * p] * dt[:, h:h + 1]
                ldec = jnp.exp(jnp.where(tri, a_col - a_row, -jnp.inf))
                s_prev = state[h * p:(h + 1) * p, :]
                y = _mxdot(cb * ldec, xdt, 1, 0) + _mxdot(cg, s_prev, 1, 1) * jnp.exp(a_col)
                y_ref[:, h * p:(h + 1) * p] = y
                st = _mxdot(xdt, bg * jnp.exp(a_last - a_col), 0, 0)
                state[h * p:(h + 1) * p, :] = s_prev * jnp.exp(a_last) + st

    return pl.pallas_call(
        body, name="ssd_fwd", grid=(geo.bsz, nc), in_specs=in_specs,
        out_specs=[pl.BlockSpec((q, SSM_D_INNER), lambda b, c: (b * nc + c, 0)),
                   pl.BlockSpec((SSM_D_INNER, n), lambda b, c: (b * nc + c, 0))],
        out_shape=[jax.ShapeDtypeStruct((geo.nrows, SSM_D_INNER), F32),
                   jax.ShapeDtypeStruct((geo.bsz * nc * SSM_D_INNER, n), F32)],
        scratch_shapes=[pltpu.VMEM((SSM_D_INNER, n), F32)],
        compiler_params=_cparams(("parallel", "arbitrary")))(xc, xc, xc, proj, dt_bias, a_log)


def _ssd_bwd(geo, xc, proj, dt_bias, a_log, s_prev_all, dy, dxs_skip):
    q, p, n, e = SSM_CHUNK, SSM_HEAD_DIM, SSM_STATE, geo.nh // SSM_GROUPS
    nc, pad, di, gn = geo.nc, geo.pad, SSM_D_INNER, geo.gn
    in_specs, ci = _ssd_specs(geo, True)
    row_spec = pl.BlockSpec((q, di), lambda b, c: (b * nc + ci(c), 0))
    in_specs += [pl.BlockSpec((di, n), lambda b, c: (b * nc + ci(c), 0)), row_spec, row_spec]

    def body(xs_ref, b_ref, c_ref, dtr_ref, dtb_ref, alog_ref, sp_ref, dy_ref, dsk_ref,
             dxc_ref, ddt_ref, gdtb_ref, galog_ref, dstate):
        step = pl.program_id(1)
        first = jnp.logical_and(pl.program_id(0) == 0, step == 0)
        c = nc - 1 - step

        @pl.when(step == 0)
        def _():
            dstate[...] = jnp.zeros_like(dstate)

        valid = (c * q + lax.broadcasted_iota(jnp.int32, (q, 1), 0) >= pad).astype(F32)
        dtr, dtb, alog = dtr_ref[...], dtb_ref[...], alog_ref[...]
        dt, a_cs = _ssd_pre(dtr, dtb, alog, valid)
        a_cst = a_cs.T
        tri = _tri(q)
        lane = lax.broadcasted_iota(jnp.int32, (1, LANE), 1)
        sub = lax.broadcasted_iota(jnp.int32, (LANE, 1), 0)
        d_dt = jnp.zeros((q, LANE), F32)
        d_acs = jnp.zeros((q, LANE), F32)
        d_acst = jnp.zeros((LANE, q), F32)
        d_last = jnp.zeros((1, LANE), F32)
        for g in range(SSM_GROUPS):
            bg, cg = b_ref[:, g * n:(g + 1) * n], c_ref[:, g * n:(g + 1) * n]
            cb = _mxdot(cg, bg, 1, 1)
            d_cb = jnp.zeros((q, q), F32)
            d_bg = jnp.zeros((q, n), F32)
            d_cg = jnp.zeros((q, n), F32)
            for hh in range(e):
                h = g * e + hh
                hs = slice(h * p, (h + 1) * p)
                a_col, a_row, a_last = a_cs[:, h:h + 1], a_cst[h:h + 1, :], a_cs[q - 1:q, h:h + 1]
                x = xs_ref[:, hs]
                dt_col = dt[:, h:h + 1]
                xdt = x * dt_col
                ldec = jnp.exp(jnp.where(tri, a_col - a_row, -jnp.inf))
                s_prev = sp_ref[hs, :]
                d_snew = dstate[hs, :]
                dyh = dy_ref[:, hs]
                e_col, e_last = jnp.exp(a_col), jnp.exp(a_last)
                dec = jnp.exp(a_last - a_col)
                d_m = _mxdot(dyh, xdt, 1, 1)
                d_xdt = _mxdot(cb * ldec, dyh, 0, 0)
                d_cb = d_cb + d_m * ldec
                d_diff = d_m * cb * ldec
                da_col = jnp.sum(d_diff, axis=1, keepdims=True)
                da_row = -jnp.sum(d_diff, axis=0, keepdims=True)
                cs = _mxdot(cg, s_prev, 1, 1)
                d_cs = dyh * e_col
                da_col = da_col + jnp.sum(dyh * cs, axis=1, keepdims=True) * e_col
                d_cg = d_cg + _mxdot(d_cs, s_prev, 1, 0)
                d_sprev = _mxdot(d_cs, cg, 0, 0) + d_snew * e_last
                dl = jnp.sum(jnp.sum(d_snew * s_prev, axis=1, keepdims=True), axis=0, keepdims=True) * e_last
                d_xdt = d_xdt + _mxdot(bg * dec, d_snew, 1, 1)
                d_bd = _mxdot(xdt, d_snew, 1, 0)
                d_bg = d_bg + d_bd * dec
                d_dec = jnp.sum(d_bd * bg, axis=1, keepdims=True) * dec
                dl = dl + jnp.sum(d_dec, axis=0, keepdims=True)
                da_col = da_col - d_dec
                dstate[hs, :] = d_sprev
                dxc_ref[:, hs] = d_xdt * dt_col + dsk_ref[:, hs]
                onehot = (lane == h).astype(F32)
                d_dt = d_dt + jnp.sum(d_xdt * x, axis=1, keepdims=True) * onehot
                d_acs = d_acs + da_col * onehot
                d_acst = d_acst + (sub == h).astype(F32) * da_row
                d_last = d_last + dl * onehot
            dxc_ref[:, di + g * n:di + (g + 1) * n] = d_bg + _mxdot(d_cb, cg, 0, 0)
            dxc_ref[:, di + gn + g * n:di + gn + (g + 1) * n] = d_cg + _mxdot(d_cb, bg, 1, 0)
        is_last = (lax.broadcasted_iota(jnp.int32, (q, 1), 0) == q - 1).astype(F32)
        d_acs = d_acs + d_acst.T + is_last * d_last
        d_adt = _dot(_tri(q).astype(F32), d_acs, 0, 0, precision=lax.Precision.HIGHEST)
        a = -jnp.exp(alog)
        d_dt = d_dt + d_adt * a
        g_alog = jnp.sum(d_adt * dt, axis=0, keepdims=True) * a
        d_dtr = d_dt * valid * _sigmoid(dtr + dtb)
        ddt_ref[...] = d_dtr.astype(ddt_ref.dtype)
        g_dtb = jnp.sum(d_dtr, axis=0, keepdims=True)

        @pl.when(first)
        def _():
            gdtb_ref[...] = g_dtb
            galog_ref[...] = g_alog

        @pl.when(jnp.logical_not(first))
        def _():
            gdtb_ref[...] += g_dtb
            galog_ref[...] += g_alog

    vec = pl.BlockSpec((1, LANE), lambda b, c: (0, 0))
    return pl.pallas_call(
        body, name="ssd_bwd", grid=(geo.bsz, nc), in_specs=in_specs,
        out_specs=[pl.BlockSpec((q, geo.cd), lambda b, c: (b * nc + ci(c), 0)),
                   pl.BlockSpec((q, LANE), lambda b, c: (b * nc + ci(c), 0)), vec, vec],
        out_shape=[jax.ShapeDtypeStruct((geo.nrows, geo.cd), F32), jax.ShapeDtypeStruct((geo.nrows, LANE), MXU_DTYPE),
                   jax.ShapeDtypeStruct((1, LANE), F32), jax.ShapeDtypeStruct((1, LANE), F32)],
        scratch_shapes=[pltpu.VMEM((di, n), F32)],
        compiler_params=_cparams(("arbitrary", "arbitrary")))(xc, xc, xc, proj, dt_bias, a_log, s_prev_all, dy, dxs_skip)


BIAS_LANE = MLA_ROPE // 2
KEY_OFF = -1e30
ATT_SCALE = (MLA_NOPE + MLA_ROPE) ** -0.5


def _row_t(col):
    return jnp.broadcast_to(col, (col.shape[0], LANE)).T[:8]


def _attn_fwd2(geo, qc, kc, v):
    t, lp = ATT_BLK, geo.lp
    nb = lp // t

    def body(q_ref, k_ref, v_ref, o_ref, lse_ref):
        qi = pl.program_id(2)
        q = q_ref[...]

        def blk(kj, carry, diag):
            m, l, acc = carry
            ks = pl.ds(pl.multiple_of(kj * t, t), t)
            s = _mxdot(q, k_ref[ks, :], 1, 1) * ATT_SCALE
            if diag:
                s = jnp.where(_tri(t), s, -jnp.inf)
            m_new = jnp.maximum(m, jnp.max(s, axis=1, keepdims=True))
            pr = jnp.exp(s - m_new)
            alpha = jnp.exp(m - m_new)
            return m_new, alpha * l + jnp.sum(pr, axis=1, keepdims=True), alpha * acc + _mxdot(pr, v_ref[ks, :], 1, 0)

        init = (jnp.full((t, 1), 2.0 * KEY_OFF, F32), jnp.zeros((t, 1), F32), jnp.zeros((t, LANE), F32))
        carry = lax.fori_loop(0, qi, lambda kj, c: blk(kj, c, False), init)
        m, l, acc = blk(qi, carry, True)
        o_ref[...] = acc / l
        lse_ref[0, 0, 0] = _row_t(m + jnp.log(l))

    return pl.pallas_call(
        body, name="attn_fwd", grid=(geo.bsz, MLA_HEADS, nb),
        in_specs=[pl.BlockSpec((t, 2 * LANE), lambda b, h, i: (b * nb + i, h)),
                  pl.BlockSpec((lp, 2 * LANE), lambda b, h, i: (b, h)), pl.BlockSpec((lp, LANE), lambda b, h, i: (b, h))],
        out_specs=[pl.BlockSpec((t, LANE), lambda b, h, i: (b * nb + i, h)),
                   pl.BlockSpec((1, 1, 1, 8, t), lambda b, h, i: (b, h, i, 0, 0))],
        out_shape=[jax.ShapeDtypeStruct((geo.nrows, geo.hq), F32),
                   jax.ShapeDtypeStruct((geo.bsz, MLA_HEADS, nb, 8, t), F32)],
        compiler_params=_cparams(("parallel", "parallel", "arbitrary")))(qc, kc, v)


def _attn_delta(geo, d_o, o):
    t, nb = ATT_BLK, geo.lp // ATT_BLK

    def body(do_ref, o_ref, dl_ref):
        dl_ref[0, 0, 0] = _row_t(jnp.sum(do_ref[...] * o_ref[...], axis=1, keepdims=True))

    spec = pl.BlockSpec((t, LANE), lambda b, h, i: (b * nb + i, h))
    return pl.pallas_call(
        body, name="attn_delta", grid=(geo.bsz, MLA_HEADS, nb), in_specs=[spec, spec],
        out_specs=pl.BlockSpec((1, 1, 1, 8, t), lambda b, h, i: (b, h, i, 0, 0)),
        out_shape=jax.ShapeDtypeStruct((geo.bsz, MLA_HEADS, nb, 8, t), F32),
        compiler_params=_cparams(("parallel", "parallel", "parallel")))(d_o, o)


def _attn_bwd2(geo, qc, kc, v, d_o, lse, delta):
    t, lp = ATT_BLK, geo.lp
    nb = lp // t

    def body(q_ref, k_ref, v_ref, do_ref, lse_ref, dl_ref, dq_ref, dk_ref, dv_ref):
        kj = pl.program_id(2)

        @pl.when(kj == 0)
        def _():
            dq_ref[...] = jnp.zeros_like(dq_ref)

        k, vv = k_ref[...], v_ref[...]

        def blk(qi, carry, diag):
            dk, dv = carry
            qs = pl.ds(pl.multiple_of(qi * t, t), t)
            q, d_o_blk = q_ref[qs, :], do_ref[qs, :]
            st = _mxdot(k, q, 1, 1) * ATT_SCALE
            if diag:
                keys = lax.broadcasted_iota(jnp.int32, (t, t), 0)
                st = jnp.where(keys <= lax.broadcasted_iota(jnp.int32, (t, t), 1), st, -jnp.inf)
            pt = jnp.exp(st - lse_ref[0, 0, qi][:1, :])
            dst = pt * (_mxdot(vv, d_o_blk, 1, 1) - dl_ref[0, 0, qi][:1, :]) * ATT_SCALE
            dq_ref[qs, :] += _mxdot(dst, k, 0, 0)
            return dk + _mxdot(dst, q, 1, 0), dv + _mxdot(pt, d_o_blk, 1, 0)

        carry = blk(kj, (jnp.zeros((t, 2 * LANE), F32), jnp.zeros((t, LANE), F32)), True)
        dk, dv = lax.fori_loop(kj + 1, nb, lambda qi, c: blk(qi, c, False), carry)
        dk_ref[...] = dk
        dv_ref[...] = dv.astype(dv_ref.dtype)

    rows = pl.BlockSpec((1, 1, nb, 8, t), lambda b, h, j: (b, h, 0, 0, 0))
    return pl.pallas_call(
        body, name="attn_bwd", grid=(geo.bsz, MLA_HEADS, nb),
        in_specs=[pl.BlockSpec((lp, 2 * LANE), lambda b, h, j: (b, h)),
                  pl.BlockSpec((t, 2 * LANE), lambda b, h, j: (b * nb + j, h)),
                  pl.BlockSpec((t, LANE), lambda b, h, j: (b * nb + j, h)),
                  pl.BlockSpec((lp, LANE), lambda b, h, j: (b, h)), rows, rows],
        out_specs=[pl.BlockSpec((lp, 2 * LANE), lambda b, h, j: (b, h)),
                   pl.BlockSpec((t, 2 * LANE), lambda b, h, j: (b * nb + j, h)),
                   pl.BlockSpec((t, LANE), lambda b, h, j: (b * nb + j, h))],
        out_shape=[jax.ShapeDtypeStruct((geo.nrows, 2 * geo.hq), F32), jax.ShapeDtypeStruct((geo.nrows, 2 * geo.hq), F32),
                   jax.ShapeDtypeStruct((geo.nrows, geo.hq), MXU_DTYPE)],
        compiler_params=_cparams(("parallel", "parallel", "arbitrary")))(qc, kc, v, d_o, lse, delta)


def _rope(x, cos, sin):
    return x * cos + pltpu.roll(x, LANE // 2, axis=1) * sin


def _rope_t(dx, cos, sin):
    return dx * cos + pltpu.roll(dx * sin, LANE // 2, axis=1)


def _layer_fwd(geo, h, w, tab, late=None):
    nr, tr, trw = geo.nrows, geo.tr, geo.tr_wide
    tb = geo.lp // tr
    rw = functools.partial(_rowwise, nrows=nr)
    s = {"h": h}
    (s["u"],) = rw("rms_mix", lambda x, g: (_rms(x, g),), tr=tr, rows=[(h, D_MODEL, 0)],
                   vecs=[(w["norm_mix_w"], D_MODEL, 0)], outs=[(D_MODEL, D_MODEL, MXU_DTYPE)])
    proj = s["proj"] = _mm("mm_in", s["u"], w["w_in_p"])
    xc = s["xc"] = _conv_fwd(geo, proj, w["conv_w"], w["conv_b"])
    s["y_ssd"], s["s_prev"] = _ssd_fwd(geo, xc, proj, w["dt_bias"], w["a_log"])
    gw = SSM_D_INNER // SSM_GROUPS

    def gate_norm(y, x, z, dsk, nw):
        return (_rms((y + x * dsk) * _silu(z), nw),)

    (s["y_ssm"],) = rw("ssm_gate_norm", gate_norm, tr=tr, ncb=SSM_GROUPS,
                       rows=[(s["y_ssd"], gw, 0), (xc, gw, 0), (proj, gw, geo.col["z"][0] // gw)],
                       vecs=[(w["d_skip_full"], gw, 0), (w["ssm_norm_w"], gw, 0)], outs=[(SSM_D_INNER, gw, MXU_DTYPE)])
    if late is not None:
        w = {**w, **late(s["y_ssm"])}
    (s["cq_n"],) = rw("rms_q", lambda x, g: (_rms(x, g),), tr=tr, rows=[(proj, MLA_Q_LORA, geo.cb("c_q"))],
                      vecs=[(w["q_norm_w"], MLA_Q_LORA, 0)], outs=[(MLA_Q_LORA, MLA_Q_LORA, MXU_DTYPE)])
    (s["ckv_n"],) = rw("rms_kv", lambda x, g: (_rms(x, g),), tr=tr, rows=[(proj, MLA_KV_LORA, geo.cb("c_kv"))],
                       vecs=[(w["kv_norm_w"], MLA_KV_LORA, 0)], outs=[(MLA_KV_LORA, MLA_KV_LORA, MXU_DTYPE)])
    qn = _mm("mm_qn", s["cq_n"], w["w_qn"])
    qp_raw = _mm("mm_qp", s["cq_n"], w["w_qp"])
    kn = _mm("mm_kn", s["ckv_n"], w["w_k"])
    s["v"] = _mm("mm_v", s["ckv_n"], w["w_v"], out_dtype=MXU_DTYPE)
    bias_lane = lambda: lax.broadcasted_iota(jnp.int32, (1, LANE), 1) == BIAS_LANE

    def q_cat(x, xp, c, sn):
        return (jnp.concatenate([x, jnp.where(bias_lane(), 1.0, _rope(xp, c, sn))], axis=1),)

    def k_cat(x, xp, c, sn, valid):
        return (jnp.concatenate([x, jnp.where(bias_lane(), KEY_OFF * (1.0 - valid), _rope(xp, c, sn))], axis=1),)

    rope_tabs = [(tab["cos"], LANE, 0), (tab["sin"], LANE, 0)]
    (s["qc"],) = rw("rope_q", q_cat, tr=tr, ncb=MLA_HEADS, rows=[(qn, LANE, 0), (qp_raw, LANE, 0)], tabs=rope_tabs,
                    outs=[(2 * geo.hq, 2 * LANE, MXU_DTYPE)], tab_blocks=tb)
    (s["kc"],) = rw("rope_k", k_cat, tr=tr, ncb=MLA_HEADS, rows=[(kn, LANE, 0)], fixed=[(proj, LANE, geo.cb("k_rope"))],
                    tabs=rope_tabs + [(tab["valid"], 1, 0)], outs=[(2 * geo.hq, 2 * LANE, MXU_DTYPE)], tab_blocks=tb)
    s["o"], s["lse"] = _attn_fwd2(geo, s["qc"], s["kc"], s["v"])
    s["ys_p"] = _mm("mm_bs", s["y_ssm"], w["w_branch_ssm"])
    s["ym_p"] = _mm("mm_bm", s["o"], w["w_branch_mla"])

    def gate(gs, gm, ys, ym):
        return (_sigmoid(gs) * ys + _sigmoid(gm) * ym,)

    (s["mixed"],) = rw("gate", gate, tr=tr, rows=[(proj, D_MODEL, geo.cb("g_ssm")), (proj, D_MODEL, geo.cb("g_mla")),
                                                  (s["ys_p"], D_MODEL, 0), (s["ym_p"], D_MODEL, 0)],
                       outs=[(D_MODEL, D_MODEL, MXU_DTYPE)])
    s["h2"] = _mm("mm_out", s["mixed"], w["w_out"], add=h)
    (s["vn"],) = rw("rms_mlp", lambda x, g: (_rms(x, g),), tr=tr, rows=[(s["h2"], D_MODEL, 0)],
                    vecs=[(w["norm_mlp_w"], D_MODEL, 0)], outs=[(D_MODEL, D_MODEL, MXU_DTYPE)])
    s["up"] = _mm("mm_up", s["vn"], w["w_mlp_up"])
    (s["act"],) = rw("relu2", lambda x: (jnp.square(jnp.maximum(x, 0.0)),), tr=trw, rows=[(s["up"], D_FF, 0)],
                     outs=[(D_FF, D_FF, MXU_DTYPE)])
    return _mm("mm_down", s["act"], w["w_mlp_down"], add=s["h2"]), s, w


def _layer_bwd(geo, dh3, s, w, tab, mid=None, tail=None):
    nr, tr, trw = geo.nrows, geo.tr, geo.tr_wide
    tb = geo.lp // tr
    rw = functools.partial(_rowwise, nrows=nr)
    g = {}
    proj = s["proj"]

    def rms_bwd(x, dy, res, gw):
        _, vjp = jax.vjp(_rms, x.astype(F32), gw)
        dx, dgw = vjp(dy.astype(F32))
        return dx + res, dgw

    def rms_bwd_nores(x, dy, gw):
        _, vjp = jax.vjp(_rms, x.astype(F32), gw)
        return vjp(dy.astype(F32))

    dact = _mm("mm_down_t", dh3, w["w_mlp_down"], tb=True)
    g["w_mlp_down"] = _mm("mm_down_g", s["act"], dh3, ta=True, out_dtype=MXU_DTYPE)
    (dup,) = rw("relu2_bwd", lambda d, x: (d * 2.0 * jnp.maximum(x, 0.0),), tr=trw,
                rows=[(dact, D_FF, 0), (s["up"], D_FF, 0)], outs=[(D_FF, D_FF, MXU_DTYPE)])
    g["w_mlp_up"] = _mm("mm_up_g", s["vn"], dup, ta=True, out_dtype=MXU_DTYPE)
    dvn = _mm("mm_up_t", dup, w["w_mlp_up"], tb=True)
    dh2, g["norm_mlp_w"] = rw("rms_mlp_bwd", rms_bwd, tr=tr,
                              rows=[(s["h2"], D_MODEL, 0), (dvn, D_MODEL, 0), (dh3, D_MODEL, 0)],
                              vecs=[(w["norm_mlp_w"], D_MODEL, 0)], outs=[(D_MODEL, D_MODEL, F32)],
                              reds=[(D_MODEL, D_MODEL)])
    dmixed = _mm("mm_out_t", dh2, w["w_out"], tb=True)
    g["w_out"] = _mm("mm_out_g", s["mixed"], dh2, ta=True, out_dtype=MXU_DTYPE)

    def gate_bwd(gs, gm, ys, ym, dm):
        f = lambda a, b, c, d: _sigmoid(a) * c + _sigmoid(b) * d
        _, vjp = jax.vjp(f, gs, gm, ys, ym)
        dgs, dgm, dys, dym = vjp(dm)
        return dys, dym, dgs, dgm

    dys_p, dym_p, dg_ssm, dg_mla = rw(
        "gate_bwd", gate_bwd, tr=tr,
        rows=[(proj, D_MODEL, geo.cb("g_ssm")), (proj, D_MODEL, geo.cb("g_mla")), (s["ys_p"], D_MODEL, 0),
              (s["ym_p"], D_MODEL, 0), (dmixed, D_MODEL, 0)], outs=[(D_MODEL, D_MODEL, MXU_DTYPE)] * 4)
    g["w_branch_ssm"] = _mm("mm_bs_g", s["y_ssm"], dys_p, ta=True, out_dtype=MXU_DTYPE)
    dy_ssm = _mm("mm_bs_t", dys_p, w["w_branch_ssm"], tb=True)
    g["w_branch_mla"] = _mm("mm_bm_g", s["o"], dym_p, ta=True, out_dtype=MXU_DTYPE)
    d_o = _mm("mm_bm_t", dym_p, w["w_branch_mla"], tb=True)
    delta = _attn_delta(geo, d_o, s["o"])
    dqc, dkc, dv = _attn_bwd2(geo, s["qc"], s["kc"], s["v"], d_o, s["lse"], delta)
    rope_tabs = [(tab["cos"], LANE, 0), (tab["sin"], LANE, 0)]
    dqn, dqp_raw = rw("rope_q_bwd", lambda x, c, sn: (x[:, :LANE], _rope_t(x[:, LANE:], c, sn)), tr=tr, ncb=MLA_HEADS,
                      rows=[(dqc, 2 * LANE, 0)], tabs=rope_tabs, outs=[(geo.hq, LANE, MXU_DTYPE)] * 2, tab_blocks=tb)

    def rope_k_bwd(x, c, sn):
        tot = x[:, LANE:2 * LANE]
        for hd in range(1, MLA_HEADS):
            tot = tot + x[:, (2 * hd + 1) * LANE:(2 * hd + 2) * LANE]
        dkn_ = jnp.concatenate([x[:, 2 * hd * LANE:(2 * hd + 1) * LANE] for hd in range(MLA_HEADS)], axis=1)
        return dkn_, _rope_t(tot, c, sn)

    dkn, dk_rope = rw("rope_k_bwd", rope_k_bwd, tr=geo.tr_wide, rows=[(dkc, 2 * geo.hq, 0)], tabs=rope_tabs,
                      outs=[(geo.hq, geo.hq, MXU_DTYPE), (LANE, LANE, MXU_DTYPE)], tab_blocks=geo.lp // geo.tr_wide)
    g["w_qn"] = _mm("mm_qn_g", s["cq_n"], dqn, ta=True, out_dtype=MXU_DTYPE)
    g["w_qp"] = _mm("mm_qp_g", s["cq_n"], dqp_raw, ta=True, out_dtype=MXU_DTYPE)
    dcq_n = _mm("mm_qp_t", dqp_raw, w["w_qp"], tb=True, add=_mm("mm_qn_t", dqn, w["w_qn"], tb=True))
    g["w_k"] = _mm("mm_kn_g", s["ckv_n"], dkn, ta=True, out_dtype=MXU_DTYPE)
    g["w_v"] = _mm("mm_v_g", s["ckv_n"], dv, ta=True, out_dtype=MXU_DTYPE)
    dckv_n = _mm("mm_v_t", dv, w["w_v"], tb=True, add=_mm("mm_kn_t", dkn, w["w_k"], tb=True))
    dc_q, g["q_norm_w"] = rw("rms_q_bwd", rms_bwd_nores, tr=tr,
                             rows=[(proj, MLA_Q_LORA, geo.cb("c_q")), (dcq_n, MLA_Q_LORA, 0)],
                             vecs=[(w["q_norm_w"], MLA_Q_LORA, 0)], outs=[(MLA_Q_LORA, MLA_Q_LORA, MXU_DTYPE)],
                             reds=[(MLA_Q_LORA, MLA_Q_LORA)])
    dc_kv, g["kv_norm_w"] = rw("rms_kv_bwd", rms_bwd_nores, tr=tr,
                               rows=[(proj, MLA_KV_LORA, geo.cb("c_kv")), (dckv_n, MLA_KV_LORA, 0)],
                               vecs=[(w["kv_norm_w"], MLA_KV_LORA, 0)], outs=[(MLA_KV_LORA, MLA_KV_LORA, MXU_DTYPE)],
                               reds=[(MLA_KV_LORA, MLA_KV_LORA)])
    gw_ = SSM_D_INNER // SSM_GROUPS
    d_skip_full = w["d_skip_full"] if mid is None else w["d_skip_full"] + mid(g)[0, 0]

    def gate_norm_bwd(y, x, z, dy, dsk, nw):
        f = lambda y_, x_, z_, dsk_, nw_: _rms((y_ + x_ * dsk_) * _silu(z_), nw_)
        _, vjp = jax.vjp(f, y, x, z, dsk, nw)
        dy_, dx_, dz_, ddsk, dnw = vjp(dy)
        return dy_, dx_, dz_, ddsk, dnw

    dy_ssd, dxs_skip, dz, g["d_skip_full"], g["ssm_norm_w"] = rw(
        "ssm_gate_norm_bwd", gate_norm_bwd, tr=tr, ncb=SSM_GROUPS,
        rows=[(s["y_ssd"], gw_, 0), (s["xc"], gw_, 0), (proj, gw_, geo.col["z"][0] // gw_), (dy_ssm, gw_, 0)],
        vecs=[(d_skip_full, gw_, 0), (w["ssm_norm_w"], gw_, 0)],
        outs=[(SSM_D_INNER, gw_, F32), (SSM_D_INNER, gw_, F32), (SSM_D_INNER, gw_, MXU_DTYPE)],
        reds=[(SSM_D_INNER, gw_), (SSM_D_INNER, gw_)])
    dxc, ddt, g["dt_bias"], g["a_log"] = _ssd_bwd(geo, s["xc"], proj, w["dt_bias"], w["a_log"], s["s_prev"],
                                                   dy_ssd, dxs_skip)
    dxbc, g["conv_w"], g["conv_b"] = _conv_bwd(geo, proj, w["conv_w"], w["conv_b"], dxc)
    di, gn = SSM_D_INNER, geo.gn
    dproj = jnp.concatenate([dz, dxbc[:, :di], dg_ssm, dg_mla, dxbc[:, di:di + gn], dxbc[:, di + gn:], dc_q, dc_kv,
                             ddt, dk_rope], axis=-1)
    g["w_in_p"] = _mm("mm_in_g", s["u"], dproj, ta=True, out_dtype=MXU_DTYPE)
    du = _mm("mm_in_t", dproj, w["w_in_p"], tb=True, dep=None if tail is None else tail(g))
    dh, g["norm_mix_w"] = rw("rms_mix_bwd", rms_bwd, tr=tr,
                             rows=[(s["h"], D_MODEL, 0), (du, D_MODEL, 0), (dh2, D_MODEL, 0)],
                             vecs=[(w["norm_mix_w"], D_MODEL, 0)], outs=[(D_MODEL, D_MODEL, F32)],
                             reds=[(D_MODEL, D_MODEL)])
    return dh, g


def _loss_bwd(geo, h, fw, target, tab):
    tr = geo.tr

    def fn(x, tgt, gw, tok):
        def lossf(x_, gw_):
            err = jnp.square(_rms(x_, gw_) - tgt)
            return 0.5 * jnp.sum(tok * jnp.mean(err, axis=-1, keepdims=True), axis=0, keepdims=True)

        val, vjp = jax.vjp(lossf, x, gw)
        dx, dgw = vjp(jnp.ones((1, 1), F32))
        return dx, jnp.broadcast_to(val, (1, LANE)), dgw

    return _rowwise("loss", fn, nrows=geo.nrows, tr=tr, rows=[(h, D_MODEL, 0), (target, D_MODEL, 0)],
                    vecs=[(fw, D_MODEL, 0)], tabs=[(tab["token"], 1, 0)], outs=[(D_MODEL, D_MODEL, F32)],
                    reds=[(LANE, LANE), (D_MODEL, D_MODEL)], tab_blocks=geo.lp // tr)


def kernel(x, meta_tokens, norm_mix_w, w_in, conv_w, conv_b, dt_bias, a_log, d_skip, ssm_norm_w, q_norm_w, kv_norm_w, w_uq, w_ukv, w_branch_ssm, w_branch_mla, w_out, norm_mlp_w, w_mlp_up, w_mlp_down, final_norm_w, loss_target, m_meta_tokens, m_norm_mix_w, m_w_in, m_conv_w, m_conv_b, m_dt_bias, m_a_log, m_d_skip, m_ssm_norm_w, m_q_norm_w, m_kv_norm_w, m_w_uq, m_w_ukv, m_w_branch_ssm, m_w_branch_mla, m_w_out, m_norm_mlp_w, m_w_mlp_up, m_w_mlp_down, m_final_norm_w, v_meta_tokens, v_norm_mix_w, v_w_in, v_conv_w, v_conv_b, v_dt_bias, v_a_log, v_d_skip, v_ssm_norm_w, v_q_norm_w, v_kv_norm_w, v_w_uq, v_w_ukv, v_w_branch_ssm, v_w_branch_mla, v_w_out, v_norm_mlp_w, v_w_mlp_up, v_w_mlp_down, v_final_norm_w):
    args = dict(locals())
    wts = {n: args[n] for n in WEIGHTS}
    mom = {n: args["m_" + n] for n in WEIGHTS}
    var = {n: args["v_" + n] for n in WEIGHTS}
    bsz, seq, _ = x.shape
    depth = w_in.shape[0]
    geo = _Geo(bsz, seq)
    tab = _tables(geo)

    big_names = [n for n, _ in BIG]
    sh_names = big_names + [n for n, _ in SHARDED_F32]
    kinds = dict(BIG + SHARDED_F32)
    shard3 = lambda a: a.reshape((1,) + a.shape) if a.ndim == 2 else a
    wire = {n: (MXU_DTYPE if n in big_names else F32) for n in sh_names}
    cast = {n: shard3(wts[n]).astype(wire[n]) for n in sh_names}
    per_layer = [n for n in sh_names if n != "meta_tokens"]
    small_names = ["norm_mix_w", "conv_b", "dt_bias", "a_log", "d_skip", "ssm_norm_w", "q_norm_w", "kv_norm_w",
                   "norm_mlp_w"]

    def gather_items(pairs):
        ins, outs, items, forms = [], [], [], []
        for n, i in pairs:
            a, b = cast[n].shape[1:]
            shape, dst, form = _gather_plan(a, b, kinds[n])
            items.append((len(ins), len(outs), (lambda ref, p, i=i: ref.at[i]), dst))
            ins.append(cast[n])
            outs.append(jax.ShapeDtypeStruct(shape, wire[n]))
            forms.append(form)
        return ins, outs, items, forms

    def whole_weights(pairs, forms, got):
        by_layer = {}
        for (n, i), form, g in zip(pairs, forms, got):
            if n == "w_in":
                n, g = "w_in_p", _w_in_assemble(geo, g)
            elif form == "row":
                g = g.reshape(g.shape[0] * g.shape[1], g.shape[2])
            elif form == "stack":
                g = _unshard(g, "col")
            by_layer.setdefault(i, {})[n] = g
        return by_layer

    def prep(i, whole, token=None):
        wl = dict(whole)
        wl.update({n: wts[n][i] for n in small_names})
        if token is not None:
            wl["norm_mix_w"] = wl["norm_mix_w"] + token[0, 0]
        return _prep_layer(geo, wl)

    early = ("w_in", "conv_w")
    late_names = [n for n in per_layer if n not in early]
    pairs1 = [(n, i) for i in range(1, depth) for n in per_layer]
    groups = [[(n, 0) for n in early] + [("meta_tokens", 0)], [(n, 0) for n in late_names]] + ([pairs1] if pairs1 else [])
    started = {}

    def gather_start(gi, dep=None):
        ins, outs, items, forms = gather_items(groups[gi])
        sems, thru, landing, token = _exchange_start("gather_w%d_start" % gi, ins, outs, items, dep)
        started[gi] = (groups[gi], forms, sems, thru, landing, items)
        return token

    def gathered(gi, after):
        pairs, forms, sems, thru, landing, items = started[gi]
        return whole_weights(pairs, forms, _exchange_wait("gather_w%d_wait" % gi, sems, thru, landing, items, after))

    def late0(after):
        whole = gathered(1, after)[0]
        if pairs1:
            whole["q_norm_w"] = wts["q_norm_w"][0] + gather_start(2, whole["w_out"])[0, 0]
        return _prep_layer(geo, whole)

    gather_start(0)
    token = gather_start(1)
    whole0 = gathered(0, token)[0]
    meta_full = whole0.pop("meta_tokens")

    meta = jnp.broadcast_to(meta_full[None], (bsz, N_META, D_MODEL))
    h = jnp.concatenate([jnp.zeros((bsz, geo.pad, D_MODEL), F32), meta, x], axis=1).reshape(geo.nrows, D_MODEL)
    target = jnp.concatenate([jnp.zeros((bsz, geo.pad + N_META, D_MODEL), F32), loss_target], axis=1)
    target = target.reshape(geo.nrows, D_MODEL)
    layers, saved = [], []
    for i in range(depth):
        if i == 0:
            w, late = prep(0, whole0, token), late0
        else:
            if i == 1:
                whole1 = gathered(2, h)
            w, late = prep(i, whole1[i]), None
        h, s, w = _layer_fwd(geo, h, w, tab, late)
        layers.append(w)
        saved.append(s)
    dh, loss_part, g_final = _loss_bwd(geo, h, final_norm_w.reshape(1, -1), target, tab)

    def scatter_items(pairs):
        ins, outs, items = [], [], []
        for n, i in pairs:
            a, b = cast[n].shape[1:]
            arr = g_meta if n == "meta_tokens" else grads[i]["w_in_p" if n == "w_in" else n]
            if n == "w_in":
                arr, src = _w_in_split(geo, arr, b), _entry
            elif kinds[n] == "row":
                src = lambda ref, p, a=a: ref.at[pl.ds(pl.multiple_of(p * a, a), a)]
            elif b % LANE == 0:
                src = lambda ref, p, b=b: ref.at[:, pl.ds(pl.multiple_of(p * b, b), b)]
            else:
                arr, src = _shard(arr, "col"), _entry
            items.append((len(ins), len(outs), src, _entry))
            ins.append(arr.astype(wire[n]))
            outs.append(jax.ShapeDtypeStruct((N_DEV, a, b), wire[n]))
        return ins, outs, items

    grads = [None] * depth
    landed, pending, res = {}, {}, {}

    def scatter_start(name, pairs):
        ins, outs, items = scatter_items(pairs)
        sems, thru, landing, token = _exchange_start(name + "_start", ins, outs, items)
        pending[name] = (pairs, sems, thru, landing, items)
        return token

    def scatter_wait(name, after):
        pairs, sems, thru, landing, items = pending[name]
        landed.update(zip(pairs, _exchange_wait(name + "_wait", sems, thru, landing, items, after)))

    def adam(n):
        parts = [landed[(n, i)] for i in range(cast[n].shape[0])]
        r = _adamw_nat("adamw_" + n, parts, shard3(wts[n]), shard3(mom[n]), shard3(var[n]))
        res[n] = [a.reshape(wts[n].shape) for a in r]

    def mid0(g):
        grads[0] = _unprep_grads(geo, g)
        return scatter_start("scatter_gb0", [(n, 0) for n in late_names])

    def tail0(g):
        grads[0] = _unprep_grads(geo, g)
        return scatter_start("scatter_ga0", [(n, 0) for n in early])

    for i in reversed(range(depth)):
        dh, gl = _layer_bwd(geo, dh, saved[i], layers[i], tab, *((mid0, tail0) if i == 0 else ()))
        grads[i] = _unprep_grads(geo, gl)
        if i == 1:
            dh = dh + scatter_start("scatter_g1", pairs1)[0, 0]
    dh = dh.reshape(bsz, geo.lp, D_MODEL)
    grad_x = dh[:, geo.pad + N_META:]
    g_meta = jnp.sum(dh[:, geo.pad:geo.pad + N_META], axis=0)
    if pairs1:
        scatter_wait("scatter_g1", g_meta)
    scatter_wait("scatter_gb0", g_meta)
    for n in late_names:
        adam(n)
    g_small = {n: jnp.stack([grads[i][n] for i in range(depth)]) for n in SMALL if n != "final_norm_w"}
    g_small["final_norm_w"] = g_final.reshape(-1)
    zero = jnp.zeros((1,), F32)
    pk = lambda d, last: _pack([d[n] for n in SMALL] + [last], F32, row_mult=8)
    packed = pk(g_small, loss_part[0, :1])
    ins, outs, items = scatter_items([("meta_tokens", 0)])
    parts, landed[("meta_tokens", 0)] = _exchange(
        "gather_g", [packed] + ins, [jax.ShapeDtypeStruct((N_DEV,) + packed.shape, F32)] + outs,
        [(0, 0, _whole, _entry)] + [(1, 1, items[0][2], items[0][3])])
    adam("meta_tokens")
    scatter_wait("scatter_ga0", res["meta_tokens"][1])
    for n in early:
        adam(n)
    res_sm = _adamw("adamw_small", parts, pk(wts, zero), pk(mom, zero), pk(var, zero))
    res_sm = [_unpack(r, [wts[n].shape for n in SMALL] + [(1,)]) for r in res_sm]
    loss = res_sm[0][-1][0]

    out = [loss, grad_x]
    for k in range(4):
        named = {n: res[n][k] for n in sh_names}
        named.update(zip(SMALL, res_sm[k]))
        out += [named[n] for n in WEIGHTS]
    return tuple(out)
```

```python
import functools

import numpy as np
import jax
import jax.numpy as jnp
from jax import lax
from jax.experimental import pallas as pl
from jax.experimental.pallas import tpu as pltpu

F32 = jnp.float32
MXU_DTYPE = jnp.bfloat16

D_MODEL = 1024
N_META = 16
EPS = 1e-6
SSM_D_INNER = 2048
SSM_HEAD_DIM = 64
SSM_GROUPS = 4
SSM_STATE = 128
SSM_CONV = 4
SSM_CHUNK = 128
MLA_HEADS = 8
MLA_Q_LORA = 512
MLA_KV_LORA = 256
MLA_NOPE = 128
MLA_ROPE = 64
MLA_V = 128
ROPE_THETA = 10000.0
D_FF = 4096
ADAM_LR = 0.001
ADAM_B1 = 0.9
ADAM_B2 = 0.999
ADAM_EPS = 1e-08
ADAM_WD = 0.01
ADAM_STEP = 10

N_DEV = 8
ATT_BLK = 256
LANE = 128
PACK_W = 1024
VMEM_LIMIT = 56 * 1024 * 1024
MESH_ID = pl.DeviceIdType.MESH

BIG = (("w_in", "col"), ("w_uq", "col"), ("w_ukv", "col"), ("w_branch_ssm", "row"), ("w_branch_mla", "row"),
       ("w_out", "row"), ("w_mlp_up", "col"), ("w_mlp_down", "row"))
SHARDED_F32 = (("conv_w", "col"), ("meta_tokens", "col"))
SMALL = ("norm_mix_w", "conv_b", "dt_bias", "a_log", "d_skip", "ssm_norm_w", "q_norm_w", "kv_norm_w",
         "norm_mlp_w", "final_norm_w")
WEIGHTS = ("meta_tokens", "norm_mix_w", "w_in", "conv_w", "conv_b", "dt_bias", "a_log", "d_skip", "ssm_norm_w",
           "q_norm_w", "kv_norm_w", "w_uq", "w_ukv", "w_branch_ssm", "w_branch_mla", "w_out", "norm_mlp_w",
           "w_mlp_up", "w_mlp_down", "final_norm_w")


def _cparams(sem=None):
    return pltpu.CompilerParams(dimension_semantics=sem, vmem_limit_bytes=VMEM_LIMIT)


def _pick(n, cands):
    for c in cands:
        if n % c == 0:
            return c
    return n


def _sigmoid(x):
    return 1.0 / (1.0 + jnp.exp(-x))


def _silu(x):
    return x * _sigmoid(x)


def _softplus(x):
    return jnp.maximum(x, 0.0) + jnp.log1p(jnp.exp(-jnp.abs(x)))


def _rms(x, w):
    return x * lax.rsqrt(jnp.mean(x * x, axis=-1, keepdims=True) + EPS) * w


def _dot(a, b, ca, cb, precision=None):
    return lax.dot_general(a, b, (((ca,), (cb,)), ((), ())), preferred_element_type=F32, precision=precision)


def _mxdot(a, b, ca, cb):
    return _dot(a.astype(MXU_DTYPE), b.astype(MXU_DTYPE), ca, cb)


def _mm(name, a, b, *, ta=False, tb=False, add=None, out_dtype=F32, dep=None):
    (kdim, m) = a.shape if ta else a.shape[::-1]
    (n, k2) = b.shape if tb else b.shape[::-1]
    assert kdim == k2, (name, a.shape, b.shape)
    tm = _pick(m, (1152, 1024, 768, 512, 384, 256, 128))
    tn = _pick(n, (1024, 512, 384, 256, 128))
    tk = _pick(kdim, (1152, 1024, 768, 512, 384, 256, 128))
    nk = kdim // tk
    a_spec = pl.BlockSpec((tk, tm), lambda i, j, k: (k, i)) if ta else pl.BlockSpec((tm, tk), lambda i, j, k: (i, k))
    b_spec = pl.BlockSpec((tn, tk), lambda i, j, k: (j, k)) if tb else pl.BlockSpec((tk, tn), lambda i, j, k: (k, j))
    o_spec = pl.BlockSpec((tm, tn), lambda i, j, k: (i, j))
    ca, cb = (0 if ta else 1), (1 if tb else 0)

    def body(*refs):
        a_ref, b_ref = refs[:2]
        o_ref, acc = refs[-2:]
        k = pl.program_id(2)

        @pl.when(k == 0)
        def _():
            acc[...] = jnp.zeros_like(acc)

        acc[...] += _mxdot(a_ref[...], b_ref[...], ca, cb)

        @pl.when(k == nk - 1)
        def _():
            r = acc[...]
            if add is not None:
                r = r + refs[2][...].astype(F32)
            o_ref[...] = r.astype(out_dtype)

    in_specs, args = [a_spec, b_spec], [a, b]
    if add is not None:
        in_specs.append(o_spec)
        args.append(add)
    if dep is not None:
        in_specs.append(pl.BlockSpec((8, LANE), lambda i, j, k: (0, 0)))
        args.append(dep)
    return pl.pallas_call(
        body, name=name, grid=(m // tm, n // tn, nk), in_specs=in_specs, out_specs=o_spec,
        out_shape=jax.ShapeDtypeStruct((m, n), out_dtype), scratch_shapes=[pltpu.VMEM((tm, tn), F32)],
        compiler_params=_cparams(("parallel", "parallel", "arbitrary")))(*args)


def _rowwise(name, fn, *, nrows, tr, ncb=1, rows=(), fixed=(), vecs=(), tabs=(), outs=(), reds=(), tab_blocks=1):
    in_specs, args = [], []
    for arr, w, c0 in rows:
        in_specs.append(pl.BlockSpec((tr, w), lambda g, i, c0=c0: (i, c0 + g)))
        args.append(arr)
    for arr, w, c0 in fixed:
        in_specs.append(pl.BlockSpec((tr, w), lambda g, i, c0=c0: (i, c0)))
        args.append(arr)
    for arr, w, c0 in vecs:
        in_specs.append(pl.BlockSpec((1, w), lambda g, i, c0=c0: (0, c0 + g)))
        args.append(arr)
    for arr, w, c0 in tabs:
        in_specs.append(pl.BlockSpec((tr, w), lambda g, i, c0=c0: (i % tab_blocks, c0)))
        args.append(arr)
    out_shape = [jax.ShapeDtypeStruct((nrows, wt), dt) for wt, w, dt in outs]
    out_shape += [jax.ShapeDtypeStruct((1, wt), F32) for wt, w in reds]
    out_specs = [pl.BlockSpec((tr, w), lambda g, i: (i, g)) for wt, w, dt in outs]
    out_specs += [pl.BlockSpec((1, w), lambda g, i: (0, g)) for wt, w in reds]
    n_in, n_out = len(args), len(outs)

    def body(*refs):
        res = fn(*[r[...] for r in refs[:n_in]])
        for o_ref, val in zip(refs[n_in:n_in + n_out], res[:n_out]):
            o_ref[...] = val.astype(o_ref.dtype)
        i = pl.program_id(1)
        for d_ref, val in zip(refs[n_in + n_out:], res[n_out:]):
            @pl.when(i == 0)
            def _(d_ref=d_ref, val=val):
                d_ref[...] = val

            @pl.when(i > 0)
            def _(d_ref=d_ref, val=val):
                d_ref[...] += val

    res = pl.pallas_call(
        body, name=name, grid=(ncb, nrows // tr), in_specs=in_specs, out_specs=out_specs, out_shape=out_shape,
        compiler_params=_cparams(("parallel", "arbitrary")))(*args)
    return res


def _peer(k):
    x, y, c = lax.axis_index("x"), lax.axis_index("y"), lax.axis_index("c")
    px = jnp.where((k >> 2) & 1, 1 - x, x)
    py = jnp.where((k >> 1) & 1, 1 - y, y)
    pc = jnp.where(k & 1, 1 - c, c)
    return (px, py, pc), 4 * px + 2 * py + pc


def _my_index():
    return 4 * lax.axis_index("x") + 2 * lax.axis_index("y") + lax.axis_index("c")


def _exchange(name, ins, out_shapes, items):
    n_in, n_out, n_it = len(ins), len(out_shapes), len(items)

    def body(*refs):
        x, o = refs[:n_in], refs[n_in:n_in + n_out]
        send_sems, recv_sems, local_sems = refs[n_in + n_out:]
        me = _my_index()
        local, sends = [], []
        for t, (ii, io, src, dst) in enumerate(items):
            cp = pltpu.make_async_copy(src(x[ii], me), dst(o[io], me), local_sems.at[t])
            cp.start()
            local.append(cp)
        for k in range(1, N_DEV):
            dev, idx = _peer(k)
            for t, (ii, io, src, dst) in enumerate(items):
                s = (k - 1) * n_it + t
                cp = pltpu.make_async_remote_copy(
                    src_ref=src(x[ii], idx), dst_ref=dst(o[io], me), send_sem=send_sems.at[s],
                    recv_sem=recv_sems.at[s], device_id=dev, device_id_type=MESH_ID)
                cp.start()
                sends.append(cp)
        for k in range(1, N_DEV):
            dev, idx = _peer(k)
            for t, (ii, io, src, dst) in enumerate(items):
                s = (k - 1) * n_it + t
                pltpu.make_async_remote_copy(
                    src_ref=src(x[ii], idx), dst_ref=dst(o[io], idx), send_sem=send_sems.at[s],
                    recv_sem=recv_sems.at[s], device_id=dev, device_id_type=MESH_ID).wait_recv()
        for cp in sends:
            cp.wait_send()
        for cp in local:
            cp.wait()

    nsem = (N_DEV - 1) * n_it
    anyspec = pl.BlockSpec(memory_space=pl.ANY)
    return pl.pallas_call(
        body, name=name, out_shape=list(out_shapes), in_specs=[anyspec] * n_in, out_specs=[anyspec] * n_out,
        scratch_shapes=[pltpu.SemaphoreType.DMA((nsem,)), pltpu.SemaphoreType.DMA((nsem,)),
                        pltpu.SemaphoreType.DMA((n_it,))],
        compiler_params=pltpu.CompilerParams(has_side_effects=True))(*ins)


def _split_copies(x, land, send_sems, recv_sems, items, receive):
    me = _my_index()
    remote, n_it = [], len(items)
    for k in range(1, N_DEV):
        dev, idx = _peer(k)
        for t, (ii, io, src, dst) in enumerate(items):
            s = (k - 1) * n_it + t
            remote.append(pltpu.make_async_remote_copy(
                src_ref=src(x[ii], idx), dst_ref=dst(land[io], idx if receive else me), send_sem=send_sems.at[s],
                recv_sem=recv_sems.at[s], device_id=dev, device_id_type=MESH_ID))
    local = [pltpu.make_async_copy(src(x[ii], me), dst(land[io], me), send_sems.at[(N_DEV - 1) * n_it + t])
             for t, (ii, io, src, dst) in enumerate(items)]
    return remote, local


def _exchange_start(name, ins, out_shapes, items, dep=None):
    n_in, n_out, n_it = len(ins), len(out_shapes), len(items)

    def body(*refs):
        x, land = refs[:n_in], refs[n_in:n_in + n_out]
        first_out = n_in + n_out + (dep is not None)
        send_sems, recv_sems, token = refs[first_out], refs[first_out + 1], refs[-1]
        remote, local = _split_copies(x, land, send_sems, recv_sems, items, False)
        for cp in remote + local:
            cp.start()
        token[...] = jnp.zeros_like(token)

    hbm = pl.BlockSpec(memory_space=pltpu.HBM)
    sem = pl.BlockSpec(memory_space=pltpu.SEMAPHORE)
    arrs = [pltpu.with_memory_space_constraint(a, pltpu.HBM)
            for a in list(ins) + [lax.empty(s.shape, s.dtype) for s in out_shapes]]
    res = pl.pallas_call(
        body, name=name,
        out_shape=(pltpu.SemaphoreType.DMA((N_DEV * n_it,)), pltpu.SemaphoreType.DMA(((N_DEV - 1) * n_it,)),
                   *[pltpu.HBM(a.shape, a.dtype) for a in arrs], jax.ShapeDtypeStruct((8, LANE), F32)),
        in_specs=[hbm] * (n_in + n_out) + ([] if dep is None else [pl.BlockSpec(memory_space=pl.ANY)]),
        out_specs=(sem, sem, *[hbm] * (n_in + n_out), pl.BlockSpec(memory_space=pltpu.VMEM)),
        input_output_aliases={i: 2 + i for i in range(n_in + n_out)},
        compiler_params=pltpu.CompilerParams(has_side_effects=pltpu.SideEffectType.DATAFLOW_SIDE_EFFECTING))(
            *arrs, *([] if dep is None else [dep]))
    return res[:2], res[2:2 + n_in], res[2 + n_in:2 + n_in + n_out], res[-1]


def _exchange_wait(name, sems, ins, landing, items, after):
    n_in, n_out = len(ins), len(landing)

    def body(*refs):
        x, land = refs[:n_in], refs[n_in:n_in + n_out]
        send_sems, recv_sems = refs[n_in + n_out], refs[n_in + n_out + 1]
        remote, local = _split_copies(x, land, send_sems, recv_sems, items, True)
        for cp in remote:
            cp.wait_send()
            cp.wait_recv()
        for cp in local:
            cp.wait()

    hbm = pl.BlockSpec(memory_space=pltpu.HBM)
    sem = pl.BlockSpec(memory_space=pltpu.SEMAPHORE)
    arrs = list(ins) + list(landing)
    res = pl.pallas_call(
        body, name=name, out_shape=tuple(pltpu.HBM(a.shape, a.dtype) for a in arrs),
        in_specs=[hbm] * (n_in + n_out) + [sem, sem, pl.BlockSpec(memory_space=pl.ANY)],
        out_specs=tuple([hbm] * (n_in + n_out)), input_output_aliases={i: i for i in range(n_in + n_out)},
        compiler_params=pltpu.CompilerParams(has_side_effects=pltpu.SideEffectType.DATAFLOW_SIDE_EFFECTING))(
            *arrs, *sems, after)
    return res[n_in:]


def _whole(ref, p):
    return ref


def _entry(ref, p):
    return ref.at[p]


def _gather_plan(a, b, kind):
    if kind == "col" and b % LANE == 0:
        return (a, N_DEV * b), (lambda ref, p: ref.at[:, pl.ds(pl.multiple_of(p * b, b), b)]), "col"
    return (N_DEV, a, b), _entry, ("row" if kind == "row" else "stack")


def _adamw_nat(name, parts, w, m, v):
    depth, b, c = w.shape
    assert len(parts) == depth
    tb = _pick(b, (128, 64, 32, 16, 8))
    spec = pl.BlockSpec((1, tb, c), lambda i, j: (i, j, 0))

    def body(*refs):
        p_refs = refs[:depth]
        w_ref, m_ref, v_ref, g_ref, d_ref, nm_ref, nv_ref = refs[depth:]
        for layer, p_ref in enumerate(p_refs):
            @pl.when(pl.program_id(0) == layer)
            def _(p_ref=p_ref):
                g = p_ref[0].astype(F32)
                for j in range(1, N_DEV):
                    g = g + p_ref[j].astype(F32)
                nm = ADAM_B1 * m_ref[0] + (1.0 - ADAM_B1) * g
                nv = ADAM_B2 * v_ref[0] + (1.0 - ADAM_B2) * jnp.square(g)
                m_hat = nm / (1.0 - ADAM_B1 ** ADAM_STEP)
                v_hat = nv / (1.0 - ADAM_B2 ** ADAM_STEP)
                g_ref[0] = g
                d_ref[0] = -ADAM_LR * (m_hat / (jnp.sqrt(v_hat) + ADAM_EPS) + ADAM_WD * w_ref[0])
                nm_ref[0] = nm
                nv_ref[0] = nv

    sds = jax.ShapeDtypeStruct((depth, b, c), F32)
    return pl.pallas_call(
        body, name=name, grid=(depth, b // tb),
        in_specs=[pl.BlockSpec((N_DEV, tb, c), lambda i, j: (0, j, 0))] * depth + [spec, spec, spec],
        out_specs=[spec] * 4, out_shape=[sds] * 4, compiler_params=_cparams(("parallel", "parallel")))(*parts, w, m, v)


def _adamw(name, parts, w, m, v):
    rows = w.shape[0]
    tr = _pick(rows, (256, 128, 64, 32, 16, 8))
    spec = pl.BlockSpec((tr, PACK_W), lambda i: (i, 0))

    def body(p_ref, w_ref, m_ref, v_ref, g_ref, d_ref, nm_ref, nv_ref):
        g = p_ref[0]
        for j in range(1, N_DEV):
            g = g + p_ref[j]
        nm = ADAM_B1 * m_ref[...] + (1.0 - ADAM_B1) * g
        nv = ADAM_B2 * v_ref[...] + (1.0 - ADAM_B2) * jnp.square(g)
        m_hat = nm / (1.0 - ADAM_B1 ** ADAM_STEP)
        v_hat = nv / (1.0 - ADAM_B2 ** ADAM_STEP)
        g_ref[...] = g
        d_ref[...] = -ADAM_LR * (m_hat / (jnp.sqrt(v_hat) + ADAM_EPS) + ADAM_WD * w_ref[...])
        nm_ref[...] = nm
        nv_ref[...] = nv

    sds = jax.ShapeDtypeStruct((rows, PACK_W), F32)
    return pl.pallas_call(
        body, name=name, grid=(rows // tr,),
        in_specs=[pl.BlockSpec((N_DEV, tr, PACK_W), lambda i: (0, i, 0)), spec, spec, spec],
        out_specs=[spec] * 4, out_shape=[sds] * 4, compiler_params=_cparams(("parallel",)))(parts, w, m, v)


def _pack(arrs, dtype, row_mult=16):
    flat = jnp.concatenate([a.reshape(-1).astype(dtype) for a in arrs])
    unit = row_mult * PACK_W
    total = -(-flat.shape[0] // unit) * unit
    flat = jnp.pad(flat, (0, total - flat.shape[0]))
    return flat.reshape(-1, PACK_W)


def _pack_lead(arrs, dtype, row_mult):
    flat = jnp.concatenate([a.reshape(N_DEV, -1).astype(dtype) for a in arrs], axis=1)
    unit = row_mult * PACK_W
    total = -(-flat.shape[1] // unit) * unit
    flat = jnp.pad(flat, ((0, 0), (0, total - flat.shape[1])))
    return flat.reshape(N_DEV, -1, PACK_W)


def _unpack(buf, shapes, lead=()):
    flat = buf.reshape(lead + (-1,))
    out, off = [], 0
    for s in shapes:
        n = int(np.prod(s))
        out.append(flat[..., off:off + n].reshape(lead + tuple(s)))
        off += n
    return out


def _unshard(g, kind):
    if kind == "col":
        g = jnp.moveaxis(g, 0, -2)
        return g.reshape(g.shape[:-2] + (g.shape[-2] * g.shape[-1],))
    g = jnp.moveaxis(g, 0, 1)
    return g.reshape((g.shape[0], g.shape[1] * g.shape[2]) + g.shape[3:])


def _shard(full, kind):
    if kind == "col":
        s = full.reshape(full.shape[:-1] + (N_DEV, full.shape[-1] // N_DEV))
        return jnp.moveaxis(s, -2, 0)
    s = full.reshape((full.shape[0], N_DEV, full.shape[1] // N_DEV) + full.shape[2:])
    return jnp.moveaxis(s, 1, 0)


class _Geo:
    def __init__(self, bsz, seq):
        self.bsz, self.seq = bsz, seq
        self.pad = (-(N_META + seq)) % ATT_BLK
        self.lp = self.pad + N_META + seq
        assert (self.pad + N_META) % SSM_CHUNK == 0 and self.lp % SSM_CHUNK == 0
        self.nrows = bsz * self.lp
        self.nc = self.lp // SSM_CHUNK
        self.nh = SSM_D_INNER // SSM_HEAD_DIM
        self.gn = SSM_GROUPS * SSM_STATE
        self.cd = SSM_D_INNER + 2 * self.gn
        self.hq = MLA_HEADS * LANE
        order = (("z", SSM_D_INNER), ("xs", SSM_D_INNER), ("g_ssm", D_MODEL), ("g_mla", D_MODEL), ("bm", self.gn),
                 ("cm", self.gn), ("c_q", MLA_Q_LORA), ("c_kv", MLA_KV_LORA), ("dt", LANE), ("k_rope", LANE))
        self.col, off = {}, 0
        for nm, w in order:
            assert off % w == 0, (nm, off, w)
            self.col[nm] = (off, w)
            off += w
        self.pw = off
        assert self.nh <= LANE and MLA_ROPE == 64 and MLA_NOPE == LANE and MLA_V == LANE
        self.tr = _pick(self.lp, (768, 512, 384, 256, 128))
        self.tr_wide = _pick(self.lp, (384, 256, 128))

    def cb(self, nm):
        off, w = self.col[nm]
        return off // w

    def w_in_runs(self, shard_w):
        nh, half = self.nh, MLA_ROPE // 2
        src, pieces = 0, []
        for nm, n in (("z", SSM_D_INNER), ("xs", SSM_D_INNER), ("bm", self.gn), ("cm", self.gn), ("dt", nh),
                      ("c_q", MLA_Q_LORA), ("c_kv", MLA_KV_LORA), ("k_rope", MLA_ROPE), ("g_ssm", D_MODEL),
                      ("g_mla", D_MODEL)):
            dst = self.col[nm][0]
            if nm == "k_rope":
                pieces += [(src, half, dst), (src + half, half, dst + 2 * half)]
            else:
                pieces.append((src, n, dst))
            src += n
        assert src == shard_w * N_DEV
        runs = []
        for a, n, dst in pieces:
            for j in range(N_DEV):
                lo, hi = max(a, j * shard_w), min(a + n, (j + 1) * shard_w)
                if lo < hi:
                    runs.append((j, lo - j * shard_w, hi - lo, dst + lo - a))
        return runs


def _slot(a):
    h = MLA_ROPE // 2
    z = jnp.zeros(a.shape[:-1] + (h,), a.dtype)
    return jnp.concatenate([a[..., :h], z, a[..., h:], z], axis=-1)


def _unslot(a):
    h = MLA_ROPE // 2
    return jnp.concatenate([a[..., :h], a[..., 2 * h:3 * h]], axis=-1)


def _prep_layer(geo, wl):
    nh = geo.nh
    p = {}
    if "w_uq" in wl:
        uq = wl["w_uq"].reshape(MLA_Q_LORA, MLA_HEADS, MLA_NOPE + MLA_ROPE)
        p["w_qn"] = uq[..., :MLA_NOPE].reshape(MLA_Q_LORA, geo.hq)
        p["w_qp"] = _slot(uq[..., MLA_NOPE:]).reshape(MLA_Q_LORA, geo.hq)
    if "w_ukv" in wl:
        ukv = wl["w_ukv"].reshape(MLA_KV_LORA, MLA_HEADS, MLA_NOPE + MLA_V)
        p["w_k"] = ukv[..., :MLA_NOPE].reshape(MLA_KV_LORA, geo.hq)
        p["w_v"] = ukv[..., MLA_NOPE:].reshape(MLA_KV_LORA, geo.hq)
    for nm in ("w_in_p", "conv_w", "w_branch_ssm", "w_branch_mla", "w_out", "w_mlp_up", "w_mlp_down"):
        if nm in wl:
            p[nm] = wl[nm]
    for nm in ("norm_mix_w", "conv_b", "ssm_norm_w", "q_norm_w", "kv_norm_w", "norm_mlp_w"):
        if nm in wl:
            p[nm] = wl[nm].reshape(1, -1)
    if "dt_bias" in wl:
        p["dt_bias"] = jnp.pad(wl["dt_bias"], (0, LANE - nh)).reshape(1, LANE)
        p["a_log"] = jnp.pad(wl["a_log"], (0, LANE - nh)).reshape(1, LANE)
        p["d_skip_full"] = jnp.repeat(wl["d_skip"], SSM_HEAD_DIM).reshape(1, SSM_D_INNER)
    return p


def _unprep_grads(geo, g):
    nh = geo.nh
    out = {}
    if "w_qn" in g:
        qn = g["w_qn"].reshape(MLA_Q_LORA, MLA_HEADS, MLA_NOPE)
        qp = _unslot(g["w_qp"].reshape(MLA_Q_LORA, MLA_HEADS, LANE))
        out["w_uq"] = jnp.concatenate([qn, qp], axis=-1).reshape(MLA_Q_LORA, -1)
    if "w_k" in g:
        wk = g["w_k"].reshape(MLA_KV_LORA, MLA_HEADS, MLA_NOPE)
        wv = g["w_v"].reshape(MLA_KV_LORA, MLA_HEADS, MLA_V)
        out["w_ukv"] = jnp.concatenate([wk, wv], axis=-1).reshape(MLA_KV_LORA, -1)
    for nm in ("w_in_p", "w_branch_ssm", "w_branch_mla", "w_out", "w_mlp_up", "w_mlp_down", "conv_w"):
        if nm in g:
            out[nm] = g[nm]
    for nm in ("norm_mix_w", "conv_b", "ssm_norm_w", "q_norm_w", "kv_norm_w", "norm_mlp_w"):
        if nm in g:
            out[nm] = g[nm].reshape(-1)
    if "dt_bias" in g:
        out["dt_bias"] = g["dt_bias"].reshape(-1)[:nh]
        out["a_log"] = g["a_log"].reshape(-1)[:nh]
        out["d_skip"] = g["d_skip_full"].reshape(nh, SSM_HEAD_DIM).sum(-1)
    return out


def _tables(geo):
    pos = jnp.arange(geo.lp, dtype=F32) - geo.pad
    inv = ROPE_THETA ** (-jnp.arange(0, MLA_ROPE, 2, dtype=F32) / MLA_ROPE)
    ang = pos[:, None] * inv[None, :]
    cos, sin = jnp.cos(ang), jnp.sin(ang)
    z = jnp.zeros_like(cos)
    rows = jnp.arange(geo.lp)[:, None]
    return {"cos": jnp.concatenate([cos, z, cos, z], axis=-1), "sin": jnp.concatenate([-sin, z, sin, z], axis=-1),
            "valid": (rows >= geo.pad).astype(F32), "token": (rows >= geo.pad + N_META).astype(F32)}


def _w_in_assemble(geo, gathered):
    _, d, sw = gathered.shape
    runs = geo.w_in_runs(sw)
    tr = _pick(d, (256, 128))

    def body(x_ref, o_ref):
        o_ref[...] = jnp.zeros_like(o_ref)
        for j, s0, n, d0 in runs:
            o_ref[:, d0:d0 + n] = x_ref[j, :, s0:s0 + n]

    return pl.pallas_call(
        body, name="w_in_assemble", grid=(d // tr,), in_specs=[pl.BlockSpec((N_DEV, tr, sw), lambda i: (0, i, 0))],
        out_specs=pl.BlockSpec((tr, geo.pw), lambda i: (i, 0)),
        out_shape=jax.ShapeDtypeStruct((d, geo.pw), gathered.dtype), compiler_params=_cparams(("parallel",)))(gathered)


def _w_in_split(geo, g_padded, sw):
    d = g_padded.shape[0]
    runs = geo.w_in_runs(sw)
    tr = _pick(d, (128,))

    def body(x_ref, o_ref):
        for j, s0, n, d0 in runs:
            o_ref[j, :, s0:s0 + n] = x_ref[:, d0:d0 + n]

    return pl.pallas_call(
        body, name="w_in_split", grid=(d // tr,), in_specs=[pl.BlockSpec((tr, geo.pw), lambda i: (i, 0))],
        out_specs=pl.BlockSpec((N_DEV, tr, sw), lambda i: (0, i, 0)),
        out_shape=jax.ShapeDtypeStruct((N_DEV, d, sw), g_padded.dtype),
        compiler_params=_cparams(("parallel",)))(g_padded)


def _conv_cols(geo, cbw):
    nx = SSM_D_INNER // cbw
    x0, b0 = geo.col["xs"][0] // cbw, geo.col["bm"][0] // cbw
    assert geo.col["cm"][0] == geo.col["bm"][0] + geo.gn
    return lambda j: jnp.where(j < nx, x0 + j, b0 + j - nx)


def _conv_pre(x, w_ref, b_ref):
    acc = b_ref[...] + x * w_ref[SSM_CONV - 1:SSM_CONV, :]
    for k in range(SSM_CONV - 1):
        acc = acc + pltpu.roll(x, SSM_CONV - 1 - k, axis=0) * w_ref[k:k + 1, :]
    return acc


def _conv_fwd(geo, proj, conv_w, conv_b):
    cbw = 256
    colmap = _conv_cols(geo, cbw)
    lp, pad = geo.lp, geo.pad

    def body(x_ref, w_ref, b_ref, o_ref):
        valid = (lax.broadcasted_iota(jnp.int32, (lp, 1), 0) >= pad).astype(F32)
        o_ref[...] = _silu(_conv_pre(x_ref[...], w_ref, b_ref)) * valid

    return pl.pallas_call(
        body, name="conv_fwd", grid=(geo.bsz, geo.cd // cbw),
        in_specs=[pl.BlockSpec((lp, cbw), lambda b, j: (b, colmap(j))),
                  pl.BlockSpec((SSM_CONV, cbw), lambda b, j: (0, j)), pl.BlockSpec((1, cbw), lambda b, j: (0, j))],
        out_specs=pl.BlockSpec((lp, cbw), lambda b, j: (b, j)),
        out_shape=jax.ShapeDtypeStruct((geo.nrows, geo.cd), F32),
        compiler_params=_cparams(("parallel", "parallel")))(proj, conv_w, conv_b)


def _conv_bwd(geo, proj, conv_w, conv_b, dxc):
    cbw = 256
    colmap = _conv_cols(geo, cbw)
    lp, pad = geo.lp, geo.pad

    def body(x_ref, w_ref, b_ref, dy_ref, dx_ref, gw_ref, gb_ref):
        b = pl.program_id(1)
        valid = (lax.broadcasted_iota(jnp.int32, (lp, 1), 0) >= pad).astype(F32)
        x = x_ref[...]
        pre = _conv_pre(x, w_ref, b_ref)
        sig = _sigmoid(pre)
        dpre = dy_ref[...] * (sig * (1.0 + pre * (1.0 - sig))) * valid
        dx = dpre * w_ref[SSM_CONV - 1:SSM_CONV, :]
        gws = [jnp.sum(dpre * x, axis=0, keepdims=True)]
        for k in range(SSM_CONV - 2, -1, -1):
            s = SSM_CONV - 1 - k
            dx = dx + pltpu.roll(dpre, lp - s, axis=0) * w_ref[k:k + 1, :]
            gws.insert(0, jnp.sum(dpre * pltpu.roll(x, s, axis=0), axis=0, keepdims=True))
        dx_ref[...] = (dx * valid).astype(dx_ref.dtype)

        @pl.when(b == 0)
        def _():
            gw_ref[...] = jnp.zeros_like(gw_ref)
            gb_ref[...] = jnp.zeros_like(gb_ref)

        for k in range(SSM_CONV):
            gw_ref[k:k + 1, :] += gws[k]
        gb_ref[...] += jnp.sum(dpre, axis=0, keepdims=True)

    return pl.pallas_call(
        body, name="conv_bwd", grid=(geo.cd // cbw, geo.bsz),
        in_specs=[pl.BlockSpec((lp, cbw), lambda j, b: (b, colmap(j))),
                  pl.BlockSpec((SSM_CONV, cbw), lambda j, b: (0, j)), pl.BlockSpec((1, cbw), lambda j, b: (0, j)),
                  pl.BlockSpec((lp, cbw), lambda j, b: (b, j))],
        out_specs=[pl.BlockSpec((lp, cbw), lambda j, b: (b, j)), pl.BlockSpec((SSM_CONV, cbw), lambda j, b: (0, j)),
                   pl.BlockSpec((1, cbw), lambda j, b: (0, j))],
        out_shape=[jax.ShapeDtypeStruct((geo.nrows, geo.cd), MXU_DTYPE),
                   jax.ShapeDtypeStruct((SSM_CONV, geo.cd), F32), jax.ShapeDtypeStruct((1, geo.cd), F32)],
        compiler_params=_cparams(("parallel", "arbitrary")))(proj, conv_w, conv_b, dxc)


def _tri(q):
    r = lax.broadcasted_iota(jnp.int32, (q, q), 0)
    c = lax.broadcasted_iota(jnp.int32, (q, q), 1)
    return r >= c


def _ssd_pre(dtr, dtb, alog, valid):
    dt = _softplus(dtr + dtb) * valid
    adt = dt * (-jnp.exp(alog))
    a_cs = _dot(_tri(SSM_CHUNK).astype(F32), adt, 1, 0, precision=lax.Precision.HIGHEST)
    return dt, a_cs


def _ssd_specs(geo, rev):
    nc, q = geo.nc, SSM_CHUNK
    ci = (lambda c: nc - 1 - c) if rev else (lambda c: c)
    nxb = SSM_D_INNER // geo.gn
    return [pl.BlockSpec((q, SSM_D_INNER), lambda b, c: (b * nc + ci(c), 0)),
            pl.BlockSpec((q, geo.gn), lambda b, c: (b * nc + ci(c), nxb)),
            pl.BlockSpec((q, geo.gn), lambda b, c: (b * nc + ci(c), nxb + 1)),
            pl.BlockSpec((q, LANE), lambda b, c: (b * nc + ci(c), geo.cb("dt"))),
            pl.BlockSpec((1, LANE), lambda b, c: (0, 0)), pl.BlockSpec((1, LANE), lambda b, c: (0, 0))], ci


def _expand_heads(cols, nh):
    per = LANE // SSM_HEAD_DIM
    lane = lax.broadcasted_iota(jnp.int32, (1, LANE), 1)
    blocks = []
    for j in range(nh // per):
        blk = jnp.broadcast_to(cols[:, j * per:j * per + 1], (cols.shape[0], LANE))
        for k in range(1, per):
            blk = jnp.where(lane >= k * SSM_HEAD_DIM, cols[:, j * per + k:j * per + k + 1], blk)
        blocks.append(blk)
    return jnp.concatenate(blocks, axis=1)


def _head_maps(geo):
    e = (jnp.arange(SSM_D_INNER)[None, :] // SSM_HEAD_DIM == jnp.arange(LANE)[:, None]).astype(F32)
    return e, e.T


def _ssd_fwd_g(geo, xc, proj, dt_bias, a_log):
    q, p, n, e = SSM_CHUNK, SSM_HEAD_DIM, SSM_STATE, geo.nh // SSM_GROUPS
    nc, pad, gw = geo.nc, geo.pad, SSM_D_INNER // SSM_GROUPS
    in_specs, _ = _ssd_specs(geo, False)
    _, e_t = _head_maps(geo)
    in_specs.append(pl.BlockSpec((SSM_D_INNER, LANE), lambda b, c: (0, 0)))

    def body(xs_ref, b_ref, c_ref, dtr_ref, dtb_ref, alog_ref, et_ref, y_ref, sp_ref, state, xdt_s):
        c = pl.program_id(1)

        @pl.when(c == 0)
        def _():
            state[...] = jnp.zeros_like(state)

        sp_ref[...] = state[...]
        inert = (c + 1) * q <= pad

        @pl.when(inert)
        def _():
            y_ref[...] = jnp.zeros_like(y_ref)

        @pl.when(jnp.logical_not(inert))
        def _():
            valid = (c * q + lax.broadcasted_iota(jnp.int32, (q, 1), 0) >= pad).astype(F32)
            dt, a_cs = _ssd_pre(dtr_ref[...], dtb_ref[...], alog_ref[...], valid)
            a_cst = a_cs.T
            dt_x, a_x = _expand_heads(dt, geo.nh), _expand_heads(a_cs, geo.nh)
            e_last = jnp.exp(a_cs[q - 1:q, :])
            tri = _tri(q)
            for g in range(SSM_GROUPS):
                gs = slice(g * gw, (g + 1) * gw)
                bg, cg = b_ref[:, g * n:(g + 1) * n], c_ref[:, g * n:(g + 1) * n]
                a_g = a_x[:, gs]
                xdt_g = xs_ref[:, gs] * dt_x[:, gs]
                xdt_s[:, gs] = xdt_g
                s_g = state[gs, :]
                y_ref[:, gs] = _mxdot(cg, s_g, 1, 1) * jnp.exp(a_g)
                e_last_rows = jnp.sum(et_ref[gs, :] * e_last, axis=1, keepdims=True)
                state[gs, :] = s_g * e_last_rows + _mxdot(xdt_g * jnp.exp(a_g[q - 1:q, :] - a_g), bg, 0, 0)
                cb = _mxdot(cg, bg, 1, 1)
                for hh in range(e):
                    h = g * e + hh
                    hs = slice(h * p, (h + 1) * p)
                    ldec = jnp.exp(jnp.where(tri, a_cs[:, h:h + 1] - a_cst[h:h + 1, :], -jnp.inf))
                    y_ref[:, hs] += _mxdot(cb * ldec, xdt_s[:, hs], 1, 0)

    return pl.pallas_call(
        body, name="ssd_fwd", grid=(geo.bsz, nc), in_specs=in_specs,
        out_specs=[pl.BlockSpec((q, SSM_D_INNER), lambda b, c: (b * nc + c, 0)),
                   pl.BlockSpec((SSM_D_INNER, n), lambda b, c: (b * nc + c, 0))],
        out_shape=[jax.ShapeDtypeStruct((geo.nrows, SSM_D_INNER), F32),
                   jax.ShapeDtypeStruct((geo.bsz * nc * SSM_D_INNER, n), F32)],
        scratch_shapes=[pltpu.VMEM((SSM_D_INNER, n), F32), pltpu.VMEM((q, SSM_D_INNER), F32)],
        compiler_params=_cparams(("parallel", "arbitrary")))(xc, xc, xc, proj, dt_bias, a_log, e_t)


def _ssd_bwd_g(geo, xc, proj, dt_bias, a_log, s_prev_all, dy, dxs_skip):
    q, p, n, e = SSM_CHUNK, SSM_HEAD_DIM, SSM_STATE, geo.nh // SSM_GROUPS
    nc, pad, di, gn, gw = geo.nc, geo.pad, SSM_D_INNER, geo.gn, SSM_D_INNER // SSM_GROUPS
    in_specs, ci = _ssd_specs(geo, True)
    row_spec = pl.BlockSpec((q, di), lambda b, c: (b * nc + ci(c), 0))
    e_map, e_t = _head_maps(geo)
    in_specs += [pl.BlockSpec((di, n), lambda b, c: (b * nc + ci(c), 0)), row_spec, row_spec,
                 pl.BlockSpec((LANE, di), lambda b, c: (0, 0)), pl.BlockSpec((di, LANE), lambda b, c: (0, 0))]

    def body(xs_ref, b_ref, c_ref, dtr_ref, dtb_ref, alog_ref, sp_ref, dy_ref, dsk_ref, e_ref, et_ref,
             dxc_ref, ddt_ref, gdtb_ref, galog_ref, dstate, xdt_s, dxdt_s):
        step = pl.program_id(1)
        first = jnp.logical_and(pl.program_id(0) == 0, step == 0)
        c = nc - 1 - step

        @pl.when(step == 0)
        def _():
            dstate[...] = jnp.zeros_like(dstate)

        @pl.when(first)
        def _():
            gdtb_ref[...] = jnp.zeros_like(gdtb_ref)
            galog_ref[...] = jnp.zeros_like(galog_ref)

        inert = (c + 1) * q <= pad

        @pl.when(inert)
        def _():
            dxc_ref[...] = jnp.zeros_like(dxc_ref)
            ddt_ref[...] = jnp.zeros_like(ddt_ref)

        @pl.when(jnp.logical_not(inert))
        def _():
            valid = (c * q + lax.broadcasted_iota(jnp.int32, (q, 1), 0) >= pad).astype(F32)
            dtr, dtb, alog = dtr_ref[...], dtb_ref[...], alog_ref[...]
            dt, a_cs = _ssd_pre(dtr, dtb, alog, valid)
            a_cst = a_cs.T
            dt_x, a_x = _expand_heads(dt, geo.nh), _expand_heads(a_cs, geo.nh)
            e_last = jnp.exp(a_cs[q - 1:q, :])
            tri = _tri(q)
            lane = lax.broadcasted_iota(jnp.int32, (1, LANE), 1)
            sub = lax.broadcasted_iota(jnp.int32, (LANE, 1), 0)
            d_dt = jnp.zeros((q, LANE), F32)
            d_acs = jnp.zeros((q, LANE), F32)
            d_acst = jnp.zeros((LANE, q), F32)
            d_last = jnp.zeros((1, LANE), F32)
            for g in range(SSM_GROUPS):
                gs = slice(g * gw, (g + 1) * gw)
                bg, cg = b_ref[:, g * n:(g + 1) * n], c_ref[:, g * n:(g + 1) * n]
                seg = lambda v: _mxdot(v, e_ref[:, gs], 1, 1)
                a_g, dt_g, x_g, dy_g = a_x[:, gs], dt_x[:, gs], xs_ref[:, gs], dy_ref[:, gs]
                e_col, dec = jnp.exp(a_g), jnp.exp(a_g[q - 1:q, :] - a_g)
                xdt_g = x_g * dt_g
                xdt_s[:, gs] = xdt_g
                s_g, ds_g, et_g = sp_ref[gs, :], dstate[gs, :], et_ref[gs, :]
                cs = _mxdot(cg, s_g, 1, 1)
                d_cs = dy_g * e_col
                d_acs = d_acs + seg(d_cs * cs)
                d_cg = _mxdot(d_cs, s_g, 1, 0)
                dstate[gs, :] = _mxdot(d_cs, cg, 0, 0) + ds_g * jnp.sum(et_g * e_last, axis=1, keepdims=True)
                d_last = d_last + jnp.sum(jnp.sum(ds_g * s_g, axis=1, keepdims=True) * et_g, axis=0,
                                          keepdims=True) * e_last
                gmat = _mxdot(bg, ds_g, 1, 1)
                xd = xdt_g * dec
                d_bg = _mxdot(xd, ds_g, 1, 0)
                d_dec = seg(xd * gmat)
                d_acs = d_acs - d_dec
                d_last = d_last + jnp.sum(d_dec, axis=0, keepdims=True)
                dxdt_s[:, gs] = dec * gmat
                cb = _mxdot(cg, bg, 1, 1)
                d_cb = jnp.zeros((q, q), F32)
                for hh in range(e):
                    h = g * e + hh
                    hs = slice(h * p, (h + 1) * p)
                    ldec = jnp.exp(jnp.where(tri, a_cs[:, h:h + 1] - a_cst[h:h + 1, :], -jnp.inf))
                    dyh = dy_ref[:, hs]
                    d_m = _mxdot(dyh, xdt_s[:, hs], 1, 1)
                    dxdt_s[:, hs] += _mxdot(cb * ldec, dyh, 0, 0)
                    d_cb = d_cb + d_m * ldec
                    d_diff = d_m * cb * ldec
                    d_acs = d_acs + jnp.sum(d_diff, axis=1, keepdims=True) * (lane == h).astype(F32)
                    d_acst = d_acst - (sub == h).astype(F32) * jnp.sum(d_diff, axis=0, keepdims=True)
                d_xdt = dxdt_s[:, gs]
                dxc_ref[:, gs] = d_xdt * dt_g + dsk_ref[:, gs]
                d_dt = d_dt + seg(d_xdt * x_g)
                dxc_ref[:, di + g * n:di + (g + 1) * n] = d_bg + _mxdot(d_cb, cg, 0, 0)
                dxc_ref[:, di + gn + g * n:di + gn + (g + 1) * n] = d_cg + _mxdot(d_cb, bg, 1, 0)
            is_last = (lax.broadcasted_iota(jnp.int32, (q, 1), 0) == q - 1).astype(F32)
            d_acs = d_acs + d_acst.T + is_last * d_last
            d_adt = _dot(_tri(q).astype(F32), d_acs, 0, 0, precision=lax.Precision.HIGHEST)
            a = -jnp.exp(alog)
            d_dt = d_dt + d_adt * a
            d_dtr = d_dt * valid * _sigmoid(dtr + dtb)
            ddt_ref[...] = d_dtr.astype(ddt_ref.dtype)
            gdtb_ref[...] += jnp.sum(d_dtr, axis=0, keepdims=True)
            galog_ref[...] += jnp.sum(d_adt * dt, axis=0, keepdims=True) * a

    vec = pl.BlockSpec((1, LANE), lambda b, c: (0, 0))
    return pl.pallas_call(
        body, name="ssd_bwd", grid=(geo.bsz, nc), in_specs=in_specs,
        out_specs=[pl.BlockSpec((q, geo.cd), lambda b, c: (b * nc + ci(c), 0)),
                   pl.BlockSpec((q, LANE), lambda b, c: (b * nc + ci(c), 0)), vec, vec],
        out_shape=[jax.ShapeDtypeStruct((geo.nrows, geo.cd), F32), jax.ShapeDtypeStruct((geo.nrows, LANE), MXU_DTYPE),
                   jax.ShapeDtypeStruct((1, LANE), F32), jax.ShapeDtypeStruct((1, LANE), F32)],
        scratch_shapes=[pltpu.VMEM((di, n), F32), pltpu.VMEM((q, di), F32), pltpu.VMEM((q, di), F32)],
        compiler_params=_cparams(("arbitrary", "arbitrary")))(
            xc, xc, xc, proj, dt_bias, a_log, s_prev_all, dy, dxs_skip, e_map, e_t)


BIAS_LANE = MLA_ROPE // 2
KEY_OFF = -1e30
ATT_SCALE = (MLA_NOPE + MLA_ROPE) ** -0.5


def _row_t(col):
    return jnp.broadcast_to(col, (col.shape[0], LANE)).T[:8]


def _attn_fwd2(geo, qc, kc, v):
    t, lp = ATT_BLK, geo.lp
    nb = lp // t

    def body(q_ref, k_ref, v_ref, o_ref, lse_ref):
        qi = pl.program_id(2)
        q = q_ref[...]

        def blk(kj, carry, diag):
            m, l, acc = carry
            ks = pl.ds(pl.multiple_of(kj * t, t), t)
            s = _mxdot(q, k_ref[ks, :], 1, 1) * ATT_SCALE
            if diag:
                s = jnp.where(_tri(t), s, -jnp.inf)
            m_new = jnp.maximum(m, jnp.max(s, axis=1, keepdims=True))
            pr = jnp.exp(s - m_new)
            alpha = jnp.exp(m - m_new)
            return m_new, alpha * l + jnp.sum(pr, axis=1, keepdims=True), alpha * acc + _mxdot(pr, v_ref[ks, :], 1, 0)

        init = (jnp.full((t, 1), 2.0 * KEY_OFF, F32), jnp.zeros((t, 1), F32), jnp.zeros((t, LANE), F32))
        carry = lax.fori_loop(0, qi, lambda kj, c: blk(kj, c, False), init)
        m, l, acc = blk(qi, carry, True)
        o_ref[...] = acc / l
        lse_ref[0, 0, 0] = _row_t(m + jnp.log(l))

    return pl.pallas_call(
        body, name="attn_fwd", grid=(geo.bsz, MLA_HEADS, nb),
        in_specs=[pl.BlockSpec((t, 2 * LANE), lambda b, h, i: (b * nb + i, h)),
                  pl.BlockSpec((lp, 2 * LANE), lambda b, h, i: (b, h)), pl.BlockSpec((lp, LANE), lambda b, h, i: (b, h))],
        out_specs=[pl.BlockSpec((t, LANE), lambda b, h, i: (b * nb + i, h)),
                   pl.BlockSpec((1, 1, 1, 8, t), lambda b, h, i: (b, h, i, 0, 0))],
        out_shape=[jax.ShapeDtypeStruct((geo.nrows, geo.hq), F32),
                   jax.ShapeDtypeStruct((geo.bsz, MLA_HEADS, nb, 8, t), F32)],
        compiler_params=_cparams(("parallel", "parallel", "arbitrary")))(qc, kc, v)


def _attn_delta(geo, d_o, o):
    t, nb = ATT_BLK, geo.lp // ATT_BLK

    def body(do_ref, o_ref, dl_ref):
        dl_ref[0, 0, 0] = _row_t(jnp.sum(do_ref[...] * o_ref[...], axis=1, keepdims=True))

    spec = pl.BlockSpec((t, LANE), lambda b, h, i: (b * nb + i, h))
    return pl.pallas_call(
        body, name="attn_delta", grid=(geo.bsz, MLA_HEADS, nb), in_specs=[spec, spec],
        out_specs=pl.BlockSpec((1, 1, 1, 8, t), lambda b, h, i: (b, h, i, 0, 0)),
        out_shape=jax.ShapeDtypeStruct((geo.bsz, MLA_HEADS, nb, 8, t), F32),
        compiler_params=_cparams(("parallel", "parallel", "parallel")))(d_o, o)


def _attn_bwd2(geo, qc, kc, v, d_o, lse, delta):
    t, lp = ATT_BLK, geo.lp
    nb = lp // t

    def body(q_ref, k_ref, v_ref, do_ref, lse_ref, dl_ref, dq_ref, dk_ref, dv_ref):
        kj = pl.program_id(2)

        @pl.when(kj == 0)
        def _():
            dq_ref[...] = jnp.zeros_like(dq_ref)

        k, vv = k_ref[...], v_ref[...]

        def blk(qi, carry, diag):
            dk, dv = carry
            qs = pl.ds(pl.multiple_of(qi * t, t), t)
            q, d_o_blk = q_ref[qs, :], do_ref[qs, :]
            st = _mxdot(k, q, 1, 1) * ATT_SCALE
            if diag:
                keys = lax.broadcasted_iota(jnp.int32, (t, t), 0)
                st = jnp.where(keys <= lax.broadcasted_iota(jnp.int32, (t, t), 1), st, -jnp.inf)
            pt = jnp.exp(st - lse_ref[0, 0, qi][:1, :])
            dst = pt * (_mxdot(vv, d_o_blk, 1, 1) - dl_ref[0, 0, qi][:1, :]) * ATT_SCALE
            dq_ref[qs, :] += _mxdot(dst, k, 0, 0)
            return dk + _mxdot(dst, q, 1, 0), dv + _mxdot(pt, d_o_blk, 1, 0)

        carry = blk(kj, (jnp.zeros((t, 2 * LANE), F32), jnp.zeros((t, LANE), F32)), True)
        dk, dv = lax.fori_loop(kj + 1, nb, lambda qi, c: blk(qi, c, False), carry)
        dk_ref[...] = dk
        dv_ref[...] = dv.astype(dv_ref.dtype)

    rows = pl.BlockSpec((1, 1, nb, 8, t), lambda b, h, j: (b, h, 0, 0, 0))
    return pl.pallas_call(
        body, name="attn_bwd", grid=(geo.bsz, MLA_HEADS, nb),
        in_specs=[pl.BlockSpec((lp, 2 * LANE), lambda b, h, j: (b, h)),
                  pl.BlockSpec((t, 2 * LANE), lambda b, h, j: (b * nb + j, h)),
                  pl.BlockSpec((t, LANE), lambda b, h, j: (b * nb + j, h)),
                  pl.BlockSpec((lp, LANE), lambda b, h, j: (b, h)), rows, rows],
        out_specs=[pl.BlockSpec((lp, 2 * LANE), lambda b, h, j: (b, h)),
                   pl.BlockSpec((t, 2 * LANE), lambda b, h, j: (b * nb + j, h)),
                   pl.BlockSpec((t, LANE), lambda b, h, j: (b * nb + j, h))],
        out_shape=[jax.ShapeDtypeStruct((geo.nrows, 2 * geo.hq), F32), jax.ShapeDtypeStruct((geo.nrows, 2 * geo.hq), F32),
                   jax.ShapeDtypeStruct((geo.nrows, geo.hq), MXU_DTYPE)],
        compiler_params=_cparams(("parallel", "parallel", "arbitrary")))(qc, kc, v, d_o, lse, delta)


def _rope(x, cos, sin):
    return x * cos + pltpu.roll(x, LANE // 2, axis=1) * sin


def _rope_t(dx, cos, sin):
    return dx * cos + pltpu.roll(dx * sin, LANE // 2, axis=1)


def _layer_fwd(geo, h, w, tab, late=None):
    nr, tr, trw = geo.nrows, geo.tr, geo.tr_wide
    tb = geo.lp // tr
    rw = functools.partial(_rowwise, nrows=nr)
    s = {"h": h}
    (s["u"],) = rw("rms_mix", lambda x, g: (_rms(x, g),), tr=tr, rows=[(h, D_MODEL, 0)],
                   vecs=[(w["norm_mix_w"], D_MODEL, 0)], outs=[(D_MODEL, D_MODEL, MXU_DTYPE)])
    proj = s["proj"] = _mm("mm_in", s["u"], w["w_in_p"])
    xc = s["xc"] = _conv_fwd(geo, proj, w["conv_w"], w["conv_b"])
    s["y_ssd"], s["s_prev"] = _ssd_fwd_g(geo, xc, proj, w["dt_bias"], w["a_log"])
    gw = SSM_D_INNER // SSM_GROUPS

    def gate_norm(y, x, z, dsk, nw):
        return (_rms((y + x * dsk) * _silu(z), nw),)

    (s["y_ssm"],) = rw("ssm_gate_norm", gate_norm, tr=tr, ncb=SSM_GROUPS,
                       rows=[(s["y_ssd"], gw, 0), (xc, gw, 0), (proj, gw, geo.col["z"][0] // gw)],
                       vecs=[(w["d_skip_full"], gw, 0), (w["ssm_norm_w"], gw, 0)], outs=[(SSM_D_INNER, gw, MXU_DTYPE)])
    if late is not None:
        w = {**w, **late(s["y_ssm"])}
    (s["cq_n"],) = rw("rms_q", lambda x, g: (_rms(x, g),), tr=tr, rows=[(proj, MLA_Q_LORA, geo.cb("c_q"))],
                      vecs=[(w["q_norm_w"], MLA_Q_LORA, 0)], outs=[(MLA_Q_LORA, MLA_Q_LORA, MXU_DTYPE)])
    (s["ckv_n"],) = rw("rms_kv", lambda x, g: (_rms(x, g),), tr=tr, rows=[(proj, MLA_KV_LORA, geo.cb("c_kv"))],
                       vecs=[(w["kv_norm_w"], MLA_KV_LORA, 0)], outs=[(MLA_KV_LORA, MLA_KV_LORA, MXU_DTYPE)])
    qn = _mm("mm_qn", s["cq_n"], w["w_qn"])
    qp_raw = _mm("mm_qp", s["cq_n"], w["w_qp"])
    kn = _mm("mm_kn", s["ckv_n"], w["w_k"])
    s["v"] = _mm("mm_v", s["ckv_n"], w["w_v"], out_dtype=MXU_DTYPE)
    bias_lane = lambda: lax.broadcasted_iota(jnp.int32, (1, LANE), 1) == BIAS_LANE

    def q_cat(x, xp, c, sn):
        return (jnp.concatenate([x, jnp.where(bias_lane(), 1.0, _rope(xp, c, sn))], axis=1),)

    def k_cat(x, xp, c, sn, valid):
        return (jnp.concatenate([x, jnp.where(bias_lane(), KEY_OFF * (1.0 - valid), _rope(xp, c, sn))], axis=1),)

    rope_tabs = [(tab["cos"], LANE, 0), (tab["sin"], LANE, 0)]
    (s["qc"],) = rw("rope_q", q_cat, tr=tr, ncb=MLA_HEADS, rows=[(qn, LANE, 0), (qp_raw, LANE, 0)], tabs=rope_tabs,
                    outs=[(2 * geo.hq, 2 * LANE, MXU_DTYPE)], tab_blocks=tb)
    (s["kc"],) = rw("rope_k", k_cat, tr=tr, ncb=MLA_HEADS, rows=[(kn, LANE, 0)], fixed=[(proj, LANE, geo.cb("k_rope"))],
                    tabs=rope_tabs + [(tab["valid"], 1, 0)], outs=[(2 * geo.hq, 2 * LANE, MXU_DTYPE)], tab_blocks=tb)
    s["o"], s["lse"] = _attn_fwd2(geo, s["qc"], s["kc"], s["v"])
    s["ys_p"] = _mm("mm_bs", s["y_ssm"], w["w_branch_ssm"])
    s["ym_p"] = _mm("mm_bm", s["o"], w["w_branch_mla"])

    def gate(gs, gm, ys, ym):
        return (_sigmoid(gs) * ys + _sigmoid(gm) * ym,)

    (s["mixed"],) = rw("gate", gate, tr=tr, rows=[(proj, D_MODEL, geo.cb("g_ssm")), (proj, D_MODEL, geo.cb("g_mla")),
                                                  (s["ys_p"], D_MODEL, 0), (s["ym_p"], D_MODEL, 0)],
                       outs=[(D_MODEL, D_MODEL, MXU_DTYPE)])
    s["h2"] = _mm("mm_out", s["mixed"], w["w_out"], add=h)
    (s["vn"],) = rw("rms_mlp", lambda x, g: (_rms(x, g),), tr=tr, rows=[(s["h2"], D_MODEL, 0)],
                    vecs=[(w["norm_mlp_w"], D_MODEL, 0)], outs=[(D_MODEL, D_MODEL, MXU_DTYPE)])
    s["up"] = _mm("mm_up", s["vn"], w["w_mlp_up"])
    (s["act"],) = rw("relu2", lambda x: (jnp.square(jnp.maximum(x, 0.0)),), tr=trw, rows=[(s["up"], D_FF, 0)],
                     outs=[(D_FF, D_FF, MXU_DTYPE)])
    return _mm("mm_down", s["act"], w["w_mlp_down"], add=s["h2"]), s, w


def _layer_bwd(geo, dh3, s, w, tab, mid=None, tail=None):
    nr, tr, trw = geo.nrows, geo.tr, geo.tr_wide
    tb = geo.lp // tr
    rw = functools.partial(_rowwise, nrows=nr)
    g = {}
    proj = s["proj"]

    def rms_bwd(x, dy, res, gw):
        _, vjp = jax.vjp(_rms, x.astype(F32), gw)
        dx, dgw = vjp(dy.astype(F32))
        return dx + res, dgw

    def rms_bwd_nores(x, dy, gw):
        _, vjp = jax.vjp(_rms, x.astype(F32), gw)
        return vjp(dy.astype(F32))

    dact = _mm("mm_down_t", dh3, w["w_mlp_down"], tb=True)
    g["w_mlp_down"] = _mm("mm_down_g", s["act"], dh3, ta=True, out_dtype=MXU_DTYPE)
    (dup,) = rw("relu2_bwd", lambda d, x: (d * 2.0 * jnp.maximum(x, 0.0),), tr=trw,
                rows=[(dact, D_FF, 0), (s["up"], D_FF, 0)], outs=[(D_FF, D_FF, MXU_DTYPE)])
    g["w_mlp_up"] = _mm("mm_up_g", s["vn"], dup, ta=True, out_dtype=MXU_DTYPE)
    dvn = _mm("mm_up_t", dup, w["w_mlp_up"], tb=True)
    dh2, g["norm_mlp_w"] = rw("rms_mlp_bwd", rms_bwd, tr=tr,
                              rows=[(s["h2"], D_MODEL, 0), (dvn, D_MODEL, 0), (dh3, D_MODEL, 0)],
                              vecs=[(w["norm_mlp_w"], D_MODEL, 0)], outs=[(D_MODEL, D_MODEL, F32)],
                              reds=[(D_MODEL, D_MODEL)])
    dmixed = _mm("mm_out_t", dh2, w["w_out"], tb=True)
    g["w_out"] = _mm("mm_out_g", s["mixed"], dh2, ta=True, out_dtype=MXU_DTYPE)

    def gate_bwd(gs, gm, ys, ym, dm):
        f = lambda a, b, c, d: _sigmoid(a) * c + _sigmoid(b) * d
        _, vjp = jax.vjp(f, gs, gm, ys, ym)
        dgs, dgm, dys, dym = vjp(dm)
        return dys, dym, dgs, dgm

    dys_p, dym_p, dg_ssm, dg_mla = rw(
        "gate_bwd", gate_bwd, tr=tr,
        rows=[(proj, D_MODEL, geo.cb("g_ssm")), (proj, D_MODEL, geo.cb("g_mla")), (s["ys_p"], D_MODEL, 0),
              (s["ym_p"], D_MODEL, 0), (dmixed, D_MODEL, 0)], outs=[(D_MODEL, D_MODEL, MXU_DTYPE)] * 4)
    g["w_branch_ssm"] = _mm("mm_bs_g", s["y_ssm"], dys_p, ta=True, out_dtype=MXU_DTYPE)
    dy_ssm = _mm("mm_bs_t", dys_p, w["w_branch_ssm"], tb=True)
    g["w_branch_mla"] = _mm("mm_bm_g", s["o"], dym_p, ta=True, out_dtype=MXU_DTYPE)
    d_o = _mm("mm_bm_t", dym_p, w["w_branch_mla"], tb=True)
    delta = _attn_delta(geo, d_o, s["o"])
    dqc, dkc, dv = _attn_bwd2(geo, s["qc"], s["kc"], s["v"], d_o, s["lse"], delta)
    rope_tabs = [(tab["cos"], LANE, 0), (tab["sin"], LANE, 0)]
    dqn, dqp_raw = rw("rope_q_bwd", lambda x, c, sn: (x[:, :LANE], _rope_t(x[:, LANE:], c, sn)), tr=tr, ncb=MLA_HEADS,
                      rows=[(dqc, 2 * LANE, 0)], tabs=rope_tabs, outs=[(geo.hq, LANE, MXU_DTYPE)] * 2, tab_blocks=tb)

    def rope_k_bwd(x, c, sn):
        tot = x[:, LANE:2 * LANE]
        for hd in range(1, MLA_HEADS):
            tot = tot + x[:, (2 * hd + 1) * LANE:(2 * hd + 2) * LANE]
        dkn_ = jnp.concatenate([x[:, 2 * hd * LANE:(2 * hd + 1) * LANE] for hd in range(MLA_HEADS)], axis=1)
        return dkn_, _rope_t(tot, c, sn)

    dkn, dk_rope = rw("rope_k_bwd", rope_k_bwd, tr=geo.tr_wide, rows=[(dkc, 2 * geo.hq, 0)], tabs=rope_tabs,
                      outs=[(geo.hq, geo.hq, MXU_DTYPE), (LANE, LANE, MXU_DTYPE)], tab_blocks=geo.lp // geo.tr_wide)
    g["w_qn"] = _mm("mm_qn_g", s["cq_n"], dqn, ta=True, out_dtype=MXU_DTYPE)
    g["w_qp"] = _mm("mm_qp_g", s["cq_n"], dqp_raw, ta=True, out_dtype=MXU_DTYPE)
    dcq_n = _mm("mm_qp_t", dqp_raw, w["w_qp"], tb=True, add=_mm("mm_qn_t", dqn, w["w_qn"], tb=True))
    g["w_k"] = _mm("mm_kn_g", s["ckv_n"], dkn, ta=True, out_dtype=MXU_DTYPE)
    g["w_v"] = _mm("mm_v_g", s["ckv_n"], dv, ta=True, out_dtype=MXU_DTYPE)
    dckv_n = _mm("mm_v_t", dv, w["w_v"], tb=True, add=_mm("mm_kn_t", dkn, w["w_k"], tb=True))
    dc_q, g["q_norm_w"] = rw("rms_q_bwd", rms_bwd_nores, tr=tr,
                             rows=[(proj, MLA_Q_LORA, geo.cb("c_q")), (dcq_n, MLA_Q_LORA, 0)],
                             vecs=[(w["q_norm_w"], MLA_Q_LORA, 0)], outs=[(MLA_Q_LORA, MLA_Q_LORA, MXU_DTYPE)],
                             reds=[(MLA_Q_LORA, MLA_Q_LORA)])
    dc_kv, g["kv_norm_w"] = rw("rms_kv_bwd", rms_bwd_nores, tr=tr,
                               rows=[(proj, MLA_KV_LORA, geo.cb("c_kv")), (dckv_n, MLA_KV_LORA, 0)],
                               vecs=[(w["kv_norm_w"], MLA_KV_LORA, 0)], outs=[(MLA_KV_LORA, MLA_KV_LORA, MXU_DTYPE)],
                               reds=[(MLA_KV_LORA, MLA_KV_LORA)])
    gw_ = SSM_D_INNER // SSM_GROUPS
    d_skip_full = w["d_skip_full"] if mid is None else w["d_skip_full"] + mid(g)[0, 0]

    def gate_norm_bwd(y, x, z, dy, dsk, nw):
        f = lambda y_, x_, z_, dsk_, nw_: _rms((y_ + x_ * dsk_) * _silu(z_), nw_)
        _, vjp = jax.vjp(f, y, x, z, dsk, nw)
        dy_, dx_, dz_, ddsk, dnw = vjp(dy)
        return dy_, dx_, dz_, ddsk, dnw

    dy_ssd, dxs_skip, dz, g["d_skip_full"], g["ssm_norm_w"] = rw(
        "ssm_gate_norm_bwd", gate_norm_bwd, tr=tr, ncb=SSM_GROUPS,
        rows=[(s["y_ssd"], gw_, 0), (s["xc"], gw_, 0), (proj, gw_, geo.col["z"][0] // gw_), (dy_ssm, gw_, 0)],
        vecs=[(d_skip_full, gw_, 0), (w["ssm_norm_w"], gw_, 0)],
        outs=[(SSM_D_INNER, gw_, F32), (SSM_D_INNER, gw_, F32), (SSM_D_INNER, gw_, MXU_DTYPE)],
        reds=[(SSM_D_INNER, gw_), (SSM_D_INNER, gw_)])
    dxc, ddt, g["dt_bias"], g["a_log"] = _ssd_bwd_g(geo, s["xc"], proj, w["dt_bias"], w["a_log"], s["s_prev"],
                                                   dy_ssd, dxs_skip)
    dxbc, g["conv_w"], g["conv_b"] = _conv_bwd(geo, proj, w["conv_w"], w["conv_b"], dxc)
    di, gn = SSM_D_INNER, geo.gn
    dproj = jnp.concatenate([dz, dxbc[:, :di], dg_ssm, dg_mla, dxbc[:, di:di + gn], dxbc[:, di + gn:], dc_q, dc_kv,
                             ddt, dk_rope], axis=-1)
    g["w_in_p"] = _mm("mm_in_g", s["u"], dproj, ta=True, out_dtype=MXU_DTYPE)
    du = _mm("mm_in_t", dproj, w["w_in_p"], tb=True, dep=None if tail is None else tail(g))
    dh, g["norm_mix_w"] = rw("rms_mix_bwd", rms_bwd, tr=tr,
                             rows=[(s["h"], D_MODEL, 0), (du, D_MODEL, 0), (dh2, D_MODEL, 0)],
                             vecs=[(w["norm_mix_w"], D_MODEL, 0)], outs=[(D_MODEL, D_MODEL, F32)],
                             reds=[(D_MODEL, D_MODEL)])
    return dh, g


def _loss_bwd(geo, h, fw, target, tab):
    tr = geo.tr

    def fn(x, tgt, gw, tok):
        def lossf(x_, gw_):
            err = jnp.square(_rms(x_, gw_) - tgt)
            return 0.5 * jnp.sum(tok * jnp.mean(err, axis=-1, keepdims=True), axis=0, keepdims=True)

        val, vjp = jax.vjp(lossf, x, gw)
        dx, dgw = vjp(jnp.ones((1, 1), F32))
        return dx, jnp.broadcast_to(val, (1, LANE)), dgw

    return _rowwise("loss", fn, nrows=geo.nrows, tr=tr, rows=[(h, D_MODEL, 0), (target, D_MODEL, 0)],
                    vecs=[(fw, D_MODEL, 0)], tabs=[(tab["token"], 1, 0)], outs=[(D_MODEL, D_MODEL, F32)],
                    reds=[(LANE, LANE), (D_MODEL, D_MODEL)], tab_blocks=geo.lp // tr)


def kernel(x, meta_tokens, norm_mix_w, w_in, conv_w, conv_b, dt_bias, a_log, d_skip, ssm_norm_w, q_norm_w, kv_norm_w, w_uq, w_ukv, w_branch_ssm, w_branch_mla, w_out, norm_mlp_w, w_mlp_up, w_mlp_down, final_norm_w, loss_target, m_meta_tokens, m_norm_mix_w, m_w_in, m_conv_w, m_conv_b, m_dt_bias, m_a_log, m_d_skip, m_ssm_norm_w, m_q_norm_w, m_kv_norm_w, m_w_uq, m_w_ukv, m_w_branch_ssm, m_w_branch_mla, m_w_out, m_norm_mlp_w, m_w_mlp_up, m_w_mlp_down, m_final_norm_w, v_meta_tokens, v_norm_mix_w, v_w_in, v_conv_w, v_conv_b, v_dt_bias, v_a_log, v_d_skip, v_ssm_norm_w, v_q_norm_w, v_kv_norm_w, v_w_uq, v_w_ukv, v_w_branch_ssm, v_w_branch_mla, v_w_out, v_norm_mlp_w, v_w_mlp_up, v_w_mlp_down, v_final_norm_w):
    args = dict(locals())
    wts = {n: args[n] for n in WEIGHTS}
    mom = {n: args["m_" + n] for n in WEIGHTS}
    var = {n: args["v_" + n] for n in WEIGHTS}
    bsz, seq, _ = x.shape
    depth = w_in.shape[0]
    geo = _Geo(bsz, seq)
    tab = _tables(geo)

    big_names = [n for n, _ in BIG]
    sh_names = big_names + [n for n, _ in SHARDED_F32]
    kinds = dict(BIG + SHARDED_F32)
    shard3 = lambda a: a.reshape((1,) + a.shape) if a.ndim == 2 else a
    wire = {n: (MXU_DTYPE if n in big_names else F32) for n in sh_names}
    cast = {n: shard3(wts[n]).astype(wire[n]) for n in sh_names}
    per_layer = [n for n in sh_names if n != "meta_tokens"]
    small_names = ["norm_mix_w", "conv_b", "dt_bias", "a_log", "d_skip", "ssm_norm_w", "q_norm_w", "kv_norm_w",
                   "norm_mlp_w"]

    def gather_items(pairs):
        ins, outs, items, forms = [], [], [], []
        for n, i in pairs:
            a, b = cast[n].shape[1:]
            shape, dst, form = _gather_plan(a, b, kinds[n])
            items.append((len(ins), len(outs), (lambda ref, p, i=i: ref.at[i]), dst))
            ins.append(cast[n])
            outs.append(jax.ShapeDtypeStruct(shape, wire[n]))
            forms.append(form)
        return ins, outs, items, forms

    def whole_weights(pairs, forms, got):
        by_layer = {}
        for (n, i), form, g in zip(pairs, forms, got):
            if n == "w_in":
                n, g = "w_in_p", _w_in_assemble(geo, g)
            elif form == "row":
                g = g.reshape(g.shape[0] * g.shape[1], g.shape[2])
            elif form == "stack":
                g = _unshard(g, "col")
            by_layer.setdefault(i, {})[n] = g
        return by_layer

    def prep(i, whole, token=None):
        wl = dict(whole)
        wl.update({n: wts[n][i] for n in small_names})
        if token is not None:
            wl["norm_mix_w"] = wl["norm_mix_w"] + token[0, 0]
        return _prep_layer(geo, wl)

    early = ("w_in", "conv_w")
    late_names = [n for n in per_layer if n not in early]
    pairs1 = [(n, i) for i in range(1, depth) for n in per_layer]
    groups = [[(n, 0) for n in early] + [("meta_tokens", 0)], [(n, 0) for n in late_names]] + ([pairs1] if pairs1 else [])
    started = {}

    def gather_start(gi, dep=None):
        ins, outs, items, forms = gather_items(groups[gi])
        sems, thru, landing, token = _exchange_start("gather_w%d_start" % gi, ins, outs, items, dep)
        started[gi] = (groups[gi], forms, sems, thru, landing, items)
        return token

    def gathered(gi, after):
        pairs, forms, sems, thru, landing, items = started[gi]
        return whole_weights(pairs, forms, _exchange_wait("gather_w%d_wait" % gi, sems, thru, landing, items, after))

    def late0(after):
        whole = gathered(1, after)[0]
        if pairs1:
            whole["q_norm_w"] = wts["q_norm_w"][0] + gather_start(2, whole["w_out"])[0, 0]
        return _prep_layer(geo, whole)

    token = gather_start(1, gather_start(0))
    whole0 = gathered(0, token)[0]
    meta_full = whole0.pop("meta_tokens")

    meta = jnp.broadcast_to(meta_full[None], (bsz, N_META, D_MODEL))
    h = jnp.concatenate([jnp.zeros((bsz, geo.pad, D_MODEL), F32), meta, x], axis=1).reshape(geo.nrows, D_MODEL)
    target = jnp.concatenate([jnp.zeros((bsz, geo.pad + N_META, D_MODEL), F32), loss_target], axis=1)
    target = target.reshape(geo.nrows, D_MODEL)
    layers, saved = [], []
    for i in range(depth):
        if i == 0:
            w, late = prep(0, whole0, token), late0
        else:
            if i == 1:
                whole1 = gathered(2, h)
            w, late = prep(i, whole1[i]), None
        h, s, w = _layer_fwd(geo, h, w, tab, late)
        layers.append(w)
        saved.append(s)
    dh, loss_part, g_final = _loss_bwd(geo, h, final_norm_w.reshape(1, -1), target, tab)

    def scatter_items(pairs):
        ins, outs, items = [], [], []
        for n, i in pairs:
            a, b = cast[n].shape[1:]
            arr = g_meta if n == "meta_tokens" else grads[i]["w_in_p" if n == "w_in" else n]
            if n == "w_in":
                arr, src = _w_in_split(geo, arr, b), _entry
            elif kinds[n] == "row":
                src = lambda ref, p, a=a: ref.at[pl.ds(pl.multiple_of(p * a, a), a)]
            elif b % LANE == 0:
                src = lambda ref, p, b=b: ref.at[:, pl.ds(pl.multiple_of(p * b, b), b)]
            else:
                arr, src = _shard(arr, "col"), _entry
            items.append((len(ins), len(outs), src, _entry))
            ins.append(arr.astype(wire[n]))
            outs.append(jax.ShapeDtypeStruct((N_DEV, a, b), wire[n]))
        return ins, outs, items

    grads = [None] * depth
    landed, pending, res = {}, {}, {}

    def scatter_start(name, pairs):
        ins, outs, items = scatter_items(pairs)
        sems, thru, landing, token = _exchange_start(name + "_start", ins, outs, items)
        pending[name] = (pairs, sems, thru, landing, items)
        return token

    def scatter_wait(name, after):
        pairs, sems, thru, landing, items = pending[name]
        landed.update(zip(pairs, _exchange_wait(name + "_wait", sems, thru, landing, items, after)))

    def adam(n):
        parts = [landed[(n, i)] for i in range(cast[n].shape[0])]
        r = _adamw_nat("adamw_" + n, parts, shard3(wts[n]), shard3(mom[n]), shard3(var[n]))
        res[n] = [a.reshape(wts[n].shape) for a in r]

    def mid0(g):
        grads[0] = _unprep_grads(geo, g)
        return scatter_start("scatter_gb0", [(n, 0) for n in late_names])

    def tail0(g):
        grads[0] = _unprep_grads(geo, g)
        return scatter_start("scatter_ga0", [(n, 0) for n in early])

    for i in reversed(range(depth)):
        dh, gl = _layer_bwd(geo, dh, saved[i], layers[i], tab, *((mid0, tail0) if i == 0 else ()))
        grads[i] = _unprep_grads(geo, gl)
        if i == 1:
            dh = dh + scatter_start("scatter_g1", pairs1)[0, 0]
    dh = dh.reshape(bsz, geo.lp, D_MODEL)
    grad_x = dh[:, geo.pad + N_META:]
    g_meta = jnp.sum(dh[:, geo.pad:geo.pad + N_META], axis=0)
    if pairs1:
        scatter_wait("scatter_g1", g_meta)
    scatter_wait("scatter_gb0", g_meta)
    for n in late_names:
        adam(n)
    g_small = {n: jnp.stack([grads[i][n] for i in range(depth)]) for n in SMALL if n != "final_norm_w"}
    g_small["final_norm_w"] = g_final.reshape(-1)
    zero = jnp.zeros((1,), F32)
    pk = lambda d, last: _pack([d[n] for n in SMALL] + [last], F32, row_mult=8)
    packed = pk(g_small, loss_part[0, :1])
    ins, outs, items = scatter_items([("meta_tokens", 0)])
    parts, landed[("meta_tokens", 0)] = _exchange(
        "gather_g", [packed] + ins, [jax.ShapeDtypeStruct((N_DEV,) + packed.shape, F32)] + outs,
        [(0, 0, _whole, _entry)] + [(1, 1, items[0][2], items[0][3])])
    adam("meta_tokens")
    scatter_wait("scatter_ga0", res["meta_tokens"][1])
    for n in early:
        adam(n)
    res_sm = _adamw("adamw_small", parts, pk(wts, zero), pk(mom, zero), pk(var, zero))
    res_sm = [_unpack(r, [wts[n].shape for n in SMALL] + [(1,)]) for r in res_sm]
    loss = res_sm[0][-1][0]

    out = [loss, grad_x]
    for k in range(4):
        named = {n: res[n][k] for n in sh_names}
        named.update(zip(SMALL, res_sm[k]))
        out += [named[n] for n in WEIGHTS]
    return tuple(out)
```

```python
import functools

import numpy as np
import jax
import jax.numpy as jnp
from jax import lax
from jax.experimental import pallas as pl
from jax.experimental.pallas import tpu as pltpu

F32 = jnp.float32
MXU_DTYPE = jnp.bfloat16

D_MODEL = 1024
N_META = 16
EPS = 1e-6
SSM_D_INNER = 2048
SSM_HEAD_DIM = 64
SSM_GROUPS = 4
SSM_STATE = 128
SSM_CONV = 4
SSM_CHUNK = 128
MLA_HEADS = 8
MLA_Q_LORA = 512
MLA_KV_LORA = 256
MLA_NOPE = 128
MLA_ROPE = 64
MLA_V = 128
ROPE_THETA = 10000.0
D_FF = 4096
ADAM_LR = 0.001
ADAM_B1 = 0.9
ADAM_B2 = 0.999
ADAM_EPS = 1e-08
ADAM_WD = 0.01
ADAM_STEP = 10

N_DEV = 8
ATT_BLK = 256
LANE = 128
PACK_W = 1024
VMEM_LIMIT = 56 * 1024 * 1024
MESH_ID = pl.DeviceIdType.MESH

BIG = (("w_in", "col"), ("w_uq", "col"), ("w_ukv", "col"), ("w_branch_ssm", "row"), ("w_branch_mla", "row"),
       ("w_out", "row"), ("w_mlp_up", "col"), ("w_mlp_down", "row"))
SHARDED_F32 = (("conv_w", "col"), ("meta_tokens", "col"))
SMALL = ("norm_mix_w", "conv_b", "dt_bias", "a_log", "d_skip", "ssm_norm_w", "q_norm_w", "kv_norm_w",
         "norm_mlp_w", "final_norm_w")
WEIGHTS = ("meta_tokens", "norm_mix_w", "w_in", "conv_w", "conv_b", "dt_bias", "a_log", "d_skip", "ssm_norm_w",
           "q_norm_w", "kv_norm_w", "w_uq", "w_ukv", "w_branch_ssm", "w_branch_mla", "w_out", "norm_mlp_w",
           "w_mlp_up", "w_mlp_down", "final_norm_w")


def _cparams(sem=None):
    return pltpu.CompilerParams(dimension_semantics=sem, vmem_limit_bytes=VMEM_LIMIT)


def _pick(n, cands):
    for c in cands:
        if n % c == 0:
            return c
    return n


def _sigmoid(x):
    return 1.0 / (1.0 + jnp.exp(-x))


def _silu(x):
    return x * _sigmoid(x)


def _softplus(x):
    return jnp.maximum(x, 0.0) + jnp.log1p(jnp.exp(-jnp.abs(x)))


def _rms(x, w):
    return x * lax.rsqrt(jnp.mean(x * x, axis=-1, keepdims=True) + EPS) * w


def _dot(a, b, ca, cb, precision=None):
    return lax.dot_general(a, b, (((ca,), (cb,)), ((), ())), preferred_element_type=F32, precision=precision)


def _mxdot(a, b, ca, cb):
    return _dot(a.astype(MXU_DTYPE), b.astype(MXU_DTYPE), ca, cb)


def _mm(name, a, b, *, ta=False, tb=False, add=None, out_dtype=F32, dep=None, epi=None):
    (kdim, m) = a.shape if ta else a.shape[::-1]
    (n, k2) = b.shape if tb else b.shape[::-1]
    assert kdim == k2, (name, a.shape, b.shape)
    tm = _pick(m, (1152, 1024, 768, 512, 384, 256, 128))
    tn = _pick(n, (1024, 512, 384, 256, 128))
    tk = _pick(kdim, (1152, 1024, 768, 512, 384, 256, 128))
    nk = kdim // tk
    a_spec = pl.BlockSpec((tk, tm), lambda i, j, k: (k, i)) if ta else pl.BlockSpec((tm, tk), lambda i, j, k: (i, k))
    b_spec = pl.BlockSpec((tn, tk), lambda i, j, k: (j, k)) if tb else pl.BlockSpec((tk, tn), lambda i, j, k: (k, j))
    o_spec = pl.BlockSpec((tm, tn), lambda i, j, k: (i, j))
    ca, cb = (0 if ta else 1), (1 if tb else 0)

    out_dtypes = [out_dtype] if epi is None else list(epi[1])
    n_out = len(out_dtypes)

    def body(*refs):
        a_ref, b_ref = refs[:2]
        o_refs, acc = refs[-1 - n_out:-1], refs[-1]
        k = pl.program_id(2)

        @pl.when(k == 0)
        def _():
            acc[...] = jnp.zeros_like(acc)

        acc[...] += _mxdot(a_ref[...], b_ref[...], ca, cb)

        @pl.when(k == nk - 1)
        def _():
            r = acc[...]
            if epi is not None:
                res = epi[0](r, refs[2][...]) if add is not None else epi[0](r)
            else:
                res = (r + refs[2][...].astype(F32) if add is not None else r,)
            for o_ref, val in zip(o_refs, res):
                o_ref[...] = val.astype(o_ref.dtype)

    in_specs, args = [a_spec, b_spec], [a, b]
    if add is not None:
        in_specs.append(o_spec)
        args.append(add)
    if dep is not None:
        in_specs.append(pl.BlockSpec((8, LANE), lambda i, j, k: (0, 0)))
        args.append(dep)
    res = pl.pallas_call(
        body, name=name, grid=(m // tm, n // tn, nk), in_specs=in_specs, out_specs=[o_spec] * n_out,
        out_shape=[jax.ShapeDtypeStruct((m, n), dt) for dt in out_dtypes], scratch_shapes=[pltpu.VMEM((tm, tn), F32)],
        compiler_params=_cparams(("parallel", "parallel", "arbitrary")))(*args)
    return res[0] if epi is None else res


def _rowwise(name, fn, *, nrows, tr, ncb=1, rows=(), fixed=(), vecs=(), tabs=(), outs=(), reds=(), tab_blocks=1):
    in_specs, args = [], []
    for arr, w, c0 in rows:
        in_specs.append(pl.BlockSpec((tr, w), lambda g, i, c0=c0: (i, c0 + g)))
        args.append(arr)
    for arr, w, c0 in fixed:
        in_specs.append(pl.BlockSpec((tr, w), lambda g, i, c0=c0: (i, c0)))
        args.append(arr)
    for arr, w, c0 in vecs:
        in_specs.append(pl.BlockSpec((1, w), lambda g, i, c0=c0: (0, c0 + g)))
        args.append(arr)
    for arr, w, c0 in tabs:
        in_specs.append(pl.BlockSpec((tr, w), lambda g, i, c0=c0: (i % tab_blocks, c0)))
        args.append(arr)
    out_shape = [jax.ShapeDtypeStruct((nrows, wt), dt) for wt, w, dt in outs]
    out_shape += [jax.ShapeDtypeStruct((1, wt), F32) for wt, w in reds]
    out_specs = [pl.BlockSpec((tr, w), lambda g, i: (i, g)) for wt, w, dt in outs]
    out_specs += [pl.BlockSpec((1, w), lambda g, i: (0, g)) for wt, w in reds]
    n_in, n_out = len(args), len(outs)

    def body(*refs):
        res = fn(*[r[...] for r in refs[:n_in]])
        for o_ref, val in zip(refs[n_in:n_in + n_out], res[:n_out]):
            o_ref[...] = val.astype(o_ref.dtype)
        i = pl.program_id(1)
        for d_ref, val in zip(refs[n_in + n_out:], res[n_out:]):
            @pl.when(i == 0)
            def _(d_ref=d_ref, val=val):
                d_ref[...] = val

            @pl.when(i > 0)
            def _(d_ref=d_ref, val=val):
                d_ref[...] += val

    res = pl.pallas_call(
        body, name=name, grid=(ncb, nrows // tr), in_specs=in_specs, out_specs=out_specs, out_shape=out_shape,
        compiler_params=_cparams(("parallel", "arbitrary")))(*args)
    return res


def _peer(k):
    x, y, c = lax.axis_index("x"), lax.axis_index("y"), lax.axis_index("c")
    px = jnp.where((k >> 2) & 1, 1 - x, x)
    py = jnp.where((k >> 1) & 1, 1 - y, y)
    pc = jnp.where(k & 1, 1 - c, c)
    return (px, py, pc), 4 * px + 2 * py + pc


def _my_index():
    return 4 * lax.axis_index("x") + 2 * lax.axis_index("y") + lax.axis_index("c")


def _exchange(name, ins, out_shapes, items):
    n_in, n_out, n_it = len(ins), len(out_shapes), len(items)

    def body(*refs):
        x, o = refs[:n_in], refs[n_in:n_in + n_out]
        send_sems, recv_sems, local_sems = refs[n_in + n_out:]
        me = _my_index()
        local, sends = [], []
        for t, (ii, io, src, dst) in enumerate(items):
            cp = pltpu.make_async_copy(src(x[ii], me), dst(o[io], me), local_sems.at[t])
            cp.start()
            local.append(cp)
        for k in range(1, N_DEV):
            dev, idx = _peer(k)
            for t, (ii, io, src, dst) in enumerate(items):
                s = (k - 1) * n_it + t
                cp = pltpu.make_async_remote_copy(
                    src_ref=src(x[ii], idx), dst_ref=dst(o[io], me), send_sem=send_sems.at[s],
                    recv_sem=recv_sems.at[s], device_id=dev, device_id_type=MESH_ID)
                cp.start()
                sends.append(cp)
        for k in range(1, N_DEV):
            dev, idx = _peer(k)
            for t, (ii, io, src, dst) in enumerate(items):
                s = (k - 1) * n_it + t
                pltpu.make_async_remote_copy(
                    src_ref=src(x[ii], idx), dst_ref=dst(o[io], idx), send_sem=send_sems.at[s],
                    recv_sem=recv_sems.at[s], device_id=dev, device_id_type=MESH_ID).wait_recv()
        for cp in sends:
            cp.wait_send()
        for cp in local:
            cp.wait()

    nsem = (N_DEV - 1) * n_it
    anyspec = pl.BlockSpec(memory_space=pl.ANY)
    return pl.pallas_call(
        body, name=name, out_shape=list(out_shapes), in_specs=[anyspec] * n_in, out_specs=[anyspec] * n_out,
        scratch_shapes=[pltpu.SemaphoreType.DMA((nsem,)), pltpu.SemaphoreType.DMA((nsem,)),
                        pltpu.SemaphoreType.DMA((n_it,))],
        compiler_params=pltpu.CompilerParams(has_side_effects=True))(*ins)


def _split_copies(x, land, send_sems, recv_sems, items, receive):
    me = _my_index()
    remote, n_it = [], len(items)
    for k in range(1, N_DEV):
        dev, idx = _peer(k)
        for t, (ii, io, src, dst) in enumerate(items):
            s = (k - 1) * n_it + t
            remote.append(pltpu.make_async_remote_copy(
                src_ref=src(x[ii], idx), dst_ref=dst(land[io], idx if receive else me), send_sem=send_sems.at[s],
                recv_sem=recv_sems.at[s], device_id=dev, device_id_type=MESH_ID))
    local = [pltpu.make_async_copy(src(x[ii], me), dst(land[io], me), send_sems.at[(N_DEV - 1) * n_it + t])
             for t, (ii, io, src, dst) in enumerate(items)]
    return remote, local


def _exchange_start(name, ins, out_shapes, items, dep=None):
    n_in, n_out, n_it = len(ins), len(out_shapes), len(items)

    def body(*refs):
        x, land = refs[:n_in], refs[n_in:n_in + n_out]
        first_out = n_in + n_out + (dep is not None)
        send_sems, recv_sems, token = refs[first_out], refs[first_out + 1], refs[-1]
        remote, local = _split_copies(x, land, send_sems, recv_sems, items, False)
        for cp in remote + local:
            cp.start()
        token[...] = jnp.zeros_like(token)

    hbm = pl.BlockSpec(memory_space=pltpu.HBM)
    sem = pl.BlockSpec(memory_space=pltpu.SEMAPHORE)
    arrs = [pltpu.with_memory_space_constraint(a, pltpu.HBM)
            for a in list(ins) + [lax.empty(s.shape, s.dtype) for s in out_shapes]]
    res = pl.pallas_call(
        body, name=name,
        out_shape=(pltpu.SemaphoreType.DMA((N_DEV * n_it,)), pltpu.SemaphoreType.DMA(((N_DEV - 1) * n_it,)),
                   *[pltpu.HBM(a.shape, a.dtype) for a in arrs], jax.ShapeDtypeStruct((8, LANE), F32)),
        in_specs=[hbm] * (n_in + n_out) + ([] if dep is None else [pl.BlockSpec(memory_space=pl.ANY)]),
        out_specs=(sem, sem, *[hbm] * (n_in + n_out), pl.BlockSpec(memory_space=pltpu.VMEM)),
        input_output_aliases={i: 2 + i for i in range(n_in + n_out)},
        compiler_params=pltpu.CompilerParams(has_side_effects=pltpu.SideEffectType.DATAFLOW_SIDE_EFFECTING))(
            *arrs, *([] if dep is None else [dep]))
    return res[:2], res[2:2 + n_in], res[2 + n_in:2 + n_in + n_out], res[-1]


def _exchange_wait(name, sems, ins, landing, items, after):
    n_in, n_out = len(ins), len(landing)

    def body(*refs):
        x, land = refs[:n_in], refs[n_in:n_in + n_out]
        send_sems, recv_sems = refs[n_in + n_out], refs[n_in + n_out + 1]
        remote, local = _split_copies(x, land, send_sems, recv_sems, items, True)
        for cp in remote:
            cp.wait_send()
            cp.wait_recv()
        for cp in local:
            cp.wait()

    hbm = pl.BlockSpec(memory_space=pltpu.HBM)
    sem = pl.BlockSpec(memory_space=pltpu.SEMAPHORE)
    arrs = list(ins) + list(landing)
    res = pl.pallas_call(
        body, name=name, out_shape=tuple(pltpu.HBM(a.shape, a.dtype) for a in arrs),
        in_specs=[hbm] * (n_in + n_out) + [sem, sem, pl.BlockSpec(memory_space=pl.ANY)],
        out_specs=tuple([hbm] * (n_in + n_out)), input_output_aliases={i: i for i in range(n_in + n_out)},
        compiler_params=pltpu.CompilerParams(has_side_effects=pltpu.SideEffectType.DATAFLOW_SIDE_EFFECTING))(
            *arrs, *sems, after)
    return res[n_in:]


def _whole(ref, p):
    return ref


def _entry(ref, p):
    return ref.at[p]


def _gather_plan(a, b, kind):
    if kind == "col" and b % LANE == 0:
        return (a, N_DEV * b), (lambda ref, p: ref.at[:, pl.ds(pl.multiple_of(p * b, b), b)]), "col"
    return (N_DEV, a, b), _entry, ("row" if kind == "row" else "stack")


def _adamw_nat(name, parts, w, m, v):
    depth, b, c = w.shape
    assert len(parts) == depth
    tb = _pick(b, (128, 64, 32, 16, 8))
    spec = pl.BlockSpec((1, tb, c), lambda i, j: (i, j, 0))

    def body(*refs):
        p_refs = refs[:depth]
        w_ref, m_ref, v_ref, g_ref, d_ref, nm_ref, nv_ref = refs[depth:]
        for layer, p_ref in enumerate(p_refs):
            @pl.when(pl.program_id(0) == layer)
            def _(p_ref=p_ref):
                g = p_ref[0].astype(F32)
                for j in range(1, N_DEV):
                    g = g + p_ref[j].astype(F32)
                nm = ADAM_B1 * m_ref[0] + (1.0 - ADAM_B1) * g
                nv = ADAM_B2 * v_ref[0] + (1.0 - ADAM_B2) * jnp.square(g)
                m_hat = nm / (1.0 - ADAM_B1 ** ADAM_STEP)
                v_hat = nv / (1.0 - ADAM_B2 ** ADAM_STEP)
                g_ref[0] = g
                d_ref[0] = -ADAM_LR * (m_hat / (jnp.sqrt(v_hat) + ADAM_EPS) + ADAM_WD * w_ref[0])
                nm_ref[0] = nm
                nv_ref[0] = nv

    sds = jax.ShapeDtypeStruct((depth, b, c), F32)
    return pl.pallas_call(
        body, name=name, grid=(depth, b // tb),
        in_specs=[pl.BlockSpec((N_DEV, tb, c), lambda i, j: (0, j, 0))] * depth + [spec, spec, spec],
        out_specs=[spec] * 4, out_shape=[sds] * 4, compiler_params=_cparams(("parallel", "parallel")))(*parts, w, m, v)


def _adamw(name, parts, w, m, v):
    rows = w.shape[0]
    tr = _pick(rows, (256, 128, 64, 32, 16, 8))
    spec = pl.BlockSpec((tr, PACK_W), lambda i: (i, 0))

    def body(p_ref, w_ref, m_ref, v_ref, g_ref, d_ref, nm_ref, nv_ref):
        g = p_ref[0]
        for j in range(1, N_DEV):
            g = g + p_ref[j]
        nm = ADAM_B1 * m_ref[...] + (1.0 - ADAM_B1) * g
        nv = ADAM_B2 * v_ref[...] + (1.0 - ADAM_B2) * jnp.square(g)
        m_hat = nm / (1.0 - ADAM_B1 ** ADAM_STEP)
        v_hat = nv / (1.0 - ADAM_B2 ** ADAM_STEP)
        g_ref[...] = g
        d_ref[...] = -ADAM_LR * (m_hat / (jnp.sqrt(v_hat) + ADAM_EPS) + ADAM_WD * w_ref[...])
        nm_ref[...] = nm
        nv_ref[...] = nv

    sds = jax.ShapeDtypeStruct((rows, PACK_W), F32)
    return pl.pallas_call(
        body, name=name, grid=(rows // tr,),
        in_specs=[pl.BlockSpec((N_DEV, tr, PACK_W), lambda i: (0, i, 0)), spec, spec, spec],
        out_specs=[spec] * 4, out_shape=[sds] * 4, compiler_params=_cparams(("parallel",)))(parts, w, m, v)


def _pack(arrs, dtype, row_mult=16):
    flat = jnp.concatenate([a.reshape(-1).astype(dtype) for a in arrs])
    unit = row_mult * PACK_W
    total = -(-flat.shape[0] // unit) * unit
    flat = jnp.pad(flat, (0, total - flat.shape[0]))
    return flat.reshape(-1, PACK_W)


def _pack_lead(arrs, dtype, row_mult):
    flat = jnp.concatenate([a.reshape(N_DEV, -1).astype(dtype) for a in arrs], axis=1)
    unit = row_mult * PACK_W
    total = -(-flat.shape[1] // unit) * unit
    flat = jnp.pad(flat, ((0, 0), (0, total - flat.shape[1])))
    return flat.reshape(N_DEV, -1, PACK_W)


def _unpack(buf, shapes, lead=()):
    flat = buf.reshape(lead + (-1,))
    out, off = [], 0
    for s in shapes:
        n = int(np.prod(s))
        out.append(flat[..., off:off + n].reshape(lead + tuple(s)))
        off += n
    return out


def _unshard(g, kind):
    if kind == "col":
        g = jnp.moveaxis(g, 0, -2)
        return g.reshape(g.shape[:-2] + (g.shape[-2] * g.shape[-1],))
    g = jnp.moveaxis(g, 0, 1)
    return g.reshape((g.shape[0], g.shape[1] * g.shape[2]) + g.shape[3:])


def _shard(full, kind):
    if kind == "col":
        s = full.reshape(full.shape[:-1] + (N_DEV, full.shape[-1] // N_DEV))
        return jnp.moveaxis(s, -2, 0)
    s = full.reshape((full.shape[0], N_DEV, full.shape[1] // N_DEV) + full.shape[2:])
    return jnp.moveaxis(s, 1, 0)


class _Geo:
    def __init__(self, bsz, seq):
        self.bsz, self.seq = bsz, seq
        self.pad = (-(N_META + seq)) % ATT_BLK
        self.lp = self.pad + N_META + seq
        assert (self.pad + N_META) % SSM_CHUNK == 0 and self.lp % SSM_CHUNK == 0
        self.nrows = bsz * self.lp
        self.nc = self.lp // SSM_CHUNK
        self.nh = SSM_D_INNER // SSM_HEAD_DIM
        self.gn = SSM_GROUPS * SSM_STATE
        self.cd = SSM_D_INNER + 2 * self.gn
        self.hq = MLA_HEADS * LANE
        order = (("z", SSM_D_INNER), ("xs", SSM_D_INNER), ("g_ssm", D_MODEL), ("g_mla", D_MODEL), ("bm", self.gn),
                 ("cm", self.gn), ("c_q", MLA_Q_LORA), ("c_kv", MLA_KV_LORA), ("dt", LANE), ("k_rope", LANE))
        self.col, off = {}, 0
        for nm, w in order:
            assert off % w == 0, (nm, off, w)
            self.col[nm] = (off, w)
            off += w
        self.pw = off
        assert self.nh <= LANE and MLA_ROPE == 64 and MLA_NOPE == LANE and MLA_V == LANE
        self.tr = _pick(self.lp, (768, 512, 384, 256, 128))
        self.tr_wide = _pick(self.lp, (384, 256, 128))

    def cb(self, nm):
        off, w = self.col[nm]
        return off // w

    def w_in_runs(self, shard_w):
        nh, half = self.nh, MLA_ROPE // 2
        src, pieces = 0, []
        for nm, n in (("z", SSM_D_INNER), ("xs", SSM_D_INNER), ("bm", self.gn), ("cm", self.gn), ("dt", nh),
                      ("c_q", MLA_Q_LORA), ("c_kv", MLA_KV_LORA), ("k_rope", MLA_ROPE), ("g_ssm", D_MODEL),
                      ("g_mla", D_MODEL)):
            dst = self.col[nm][0]
            if nm == "k_rope":
                pieces += [(src, half, dst), (src + half, half, dst + 2 * half)]
            else:
                pieces.append((src, n, dst))
            src += n
        assert src == shard_w * N_DEV
        runs = []
        for a, n, dst in pieces:
            for j in range(N_DEV):
                lo, hi = max(a, j * shard_w), min(a + n, (j + 1) * shard_w)
                if lo < hi:
                    runs.append((j, lo - j * shard_w, hi - lo, dst + lo - a))
        return runs


def _slot(a):
    h = MLA_ROPE // 2
    z = jnp.zeros(a.shape[:-1] + (h,), a.dtype)
    return jnp.concatenate([a[..., :h], z, a[..., h:], z], axis=-1)


def _unslot(a):
    h = MLA_ROPE // 2
    return jnp.concatenate([a[..., :h], a[..., 2 * h:3 * h]], axis=-1)


def _prep_layer(geo, wl):
    nh = geo.nh
    p = {}
    if "w_uq" in wl:
        uq = wl["w_uq"].reshape(MLA_Q_LORA, MLA_HEADS, MLA_NOPE + MLA_ROPE)
        p["w_qn"] = uq[..., :MLA_NOPE].reshape(MLA_Q_LORA, geo.hq)
        p["w_qp"] = _slot(uq[..., MLA_NOPE:]).reshape(MLA_Q_LORA, geo.hq)
    if "w_ukv" in wl:
        ukv = wl["w_ukv"].reshape(MLA_KV_LORA, MLA_HEADS, MLA_NOPE + MLA_V)
        p["w_k"] = ukv[..., :MLA_NOPE].reshape(MLA_KV_LORA, geo.hq)
        p["w_v"] = ukv[..., MLA_NOPE:].reshape(MLA_KV_LORA, geo.hq)
    for nm in ("w_in_p", "conv_w", "w_branch_ssm", "w_branch_mla", "w_out", "w_mlp_up", "w_mlp_down"):
        if nm in wl:
            p[nm] = wl[nm]
    for nm in ("norm_mix_w", "conv_b", "ssm_norm_w", "q_norm_w", "kv_norm_w", "norm_mlp_w"):
        if nm in wl:
            p[nm] = wl[nm].reshape(1, -1)
    if "dt_bias" in wl:
        p["dt_bias"] = jnp.pad(wl["dt_bias"], (0, LANE - nh)).reshape(1, LANE)
        p["a_log"] = jnp.pad(wl["a_log"], (0, LANE - nh)).reshape(1, LANE)
        p["d_skip_full"] = jnp.repeat(wl["d_skip"], SSM_HEAD_DIM).reshape(1, SSM_D_INNER)
    return p


def _unprep_grads(geo, g):
    nh = geo.nh
    out = {}
    if "w_qn" in g:
        qn = g["w_qn"].reshape(MLA_Q_LORA, MLA_HEADS, MLA_NOPE)
        qp = _unslot(g["w_qp"].reshape(MLA_Q_LORA, MLA_HEADS, LANE))
        out["w_uq"] = jnp.concatenate([qn, qp], axis=-1).reshape(MLA_Q_LORA, -1)
    if "w_k" in g:
        wk = g["w_k"].reshape(MLA_KV_LORA, MLA_HEADS, MLA_NOPE)
        wv = g["w_v"].reshape(MLA_KV_LORA, MLA_HEADS, MLA_V)
        out["w_ukv"] = jnp.concatenate([wk, wv], axis=-1).reshape(MLA_KV_LORA, -1)
    for nm in ("w_in_p", "w_branch_ssm", "w_branch_mla", "w_out", "w_mlp_up", "w_mlp_down", "conv_w"):
        if nm in g:
            out[nm] = g[nm]
    for nm in ("norm_mix_w", "conv_b", "ssm_norm_w", "q_norm_w", "kv_norm_w", "norm_mlp_w"):
        if nm in g:
            out[nm] = g[nm].reshape(-1)
    if "dt_bias" in g:
        out["dt_bias"] = g["dt_bias"].reshape(-1)[:nh]
        out["a_log"] = g["a_log"].reshape(-1)[:nh]
        out["d_skip"] = g["d_skip_full"].reshape(nh, SSM_HEAD_DIM).sum(-1)
    return out


def _tables(geo):
    pos = jnp.arange(geo.lp, dtype=F32) - geo.pad
    inv = ROPE_THETA ** (-jnp.arange(0, MLA_ROPE, 2, dtype=F32) / MLA_ROPE)
    ang = pos[:, None] * inv[None, :]
    cos, sin = jnp.cos(ang), jnp.sin(ang)
    z = jnp.zeros_like(cos)
    rows = jnp.arange(geo.lp)[:, None]
    return {"cos": jnp.concatenate([cos, z, cos, z], axis=-1), "sin": jnp.concatenate([-sin, z, sin, z], axis=-1),
            "valid": (rows >= geo.pad).astype(F32), "token": (rows >= geo.pad + N_META).astype(F32)}


def _w_in_assemble(geo, gathered):
    _, d, sw = gathered.shape
    runs = geo.w_in_runs(sw)
    tr = _pick(d, (256, 128))

    def body(x_ref, o_ref):
        o_ref[...] = jnp.zeros_like(o_ref)
        for j, s0, n, d0 in runs:
            o_ref[:, d0:d0 + n] = x_ref[j, :, s0:s0 + n]

    return pl.pallas_call(
        body, name="w_in_assemble", grid=(d // tr,), in_specs=[pl.BlockSpec((N_DEV, tr, sw), lambda i: (0, i, 0))],
        out_specs=pl.BlockSpec((tr, geo.pw), lambda i: (i, 0)),
        out_shape=jax.ShapeDtypeStruct((d, geo.pw), gathered.dtype), compiler_params=_cparams(("parallel",)))(gathered)


def _w_in_split(geo, g_padded, sw):
    d = g_padded.shape[0]
    runs = geo.w_in_runs(sw)
    tr = _pick(d, (128,))

    def body(x_ref, o_ref):
        for j, s0, n, d0 in runs:
            o_ref[j, :, s0:s0 + n] = x_ref[:, d0:d0 + n]

    return pl.pallas_call(
        body, name="w_in_split", grid=(d // tr,), in_specs=[pl.BlockSpec((tr, geo.pw), lambda i: (i, 0))],
        out_specs=pl.BlockSpec((N_DEV, tr, sw), lambda i: (0, i, 0)),
        out_shape=jax.ShapeDtypeStruct((N_DEV, d, sw), g_padded.dtype),
        compiler_params=_cparams(("parallel",)))(g_padded)


def _conv_cols(geo, cbw):
    nx = SSM_D_INNER // cbw
    x0, b0 = geo.col["xs"][0] // cbw, geo.col["bm"][0] // cbw
    assert geo.col["cm"][0] == geo.col["bm"][0] + geo.gn
    return lambda j: jnp.where(j < nx, x0 + j, b0 + j - nx)


def _conv_pre(x, w_ref, b_ref):
    acc = b_ref[...] + x * w_ref[SSM_CONV - 1:SSM_CONV, :]
    for k in range(SSM_CONV - 1):
        acc = acc + pltpu.roll(x, SSM_CONV - 1 - k, axis=0) * w_ref[k:k + 1, :]
    return acc


def _conv_fwd(geo, proj, conv_w, conv_b):
    cbw = 256
    colmap = _conv_cols(geo, cbw)
    lp, pad = geo.lp, geo.pad

    def body(x_ref, w_ref, b_ref, o_ref):
        valid = (lax.broadcasted_iota(jnp.int32, (lp, 1), 0) >= pad).astype(F32)
        o_ref[...] = _silu(_conv_pre(x_ref[...], w_ref, b_ref)) * valid

    return pl.pallas_call(
        body, name="conv_fwd", grid=(geo.bsz, geo.cd // cbw),
        in_specs=[pl.BlockSpec((lp, cbw), lambda b, j: (b, colmap(j))),
                  pl.BlockSpec((SSM_CONV, cbw), lambda b, j: (0, j)), pl.BlockSpec((1, cbw), lambda b, j: (0, j))],
        out_specs=pl.BlockSpec((lp, cbw), lambda b, j: (b, j)),
        out_shape=jax.ShapeDtypeStruct((geo.nrows, geo.cd), F32),
        compiler_params=_cparams(("parallel", "parallel")))(proj, conv_w, conv_b)


def _conv_bwd(geo, proj, conv_w, conv_b, dxc):
    cbw = 256
    colmap = _conv_cols(geo, cbw)
    lp, pad = geo.lp, geo.pad

    def body(x_ref, w_ref, b_ref, dy_ref, dx_ref, gw_ref, gb_ref):
        b = pl.program_id(1)
        valid = (lax.broadcasted_iota(jnp.int32, (lp, 1), 0) >= pad).astype(F32)
        x = x_ref[...]
        pre = _conv_pre(x, w_ref, b_ref)
        sig = _sigmoid(pre)
        dpre = dy_ref[...] * (sig * (1.0 + pre * (1.0 - sig))) * valid
        dx = dpre * w_ref[SSM_CONV - 1:SSM_CONV, :]
        gws = [jnp.sum(dpre * x, axis=0, keepdims=True)]
        for k in range(SSM_CONV - 2, -1, -1):
            s = SSM_CONV - 1 - k
            dx = dx + pltpu.roll(dpre, lp - s, axis=0) * w_ref[k:k + 1, :]
            gws.insert(0, jnp.sum(dpre * pltpu.roll(x, s, axis=0), axis=0, keepdims=True))
        dx_ref[...] = (dx * valid).astype(dx_ref.dtype)

        @pl.when(b == 0)
        def _():
            gw_ref[...] = jnp.zeros_like(gw_ref)
            gb_ref[...] = jnp.zeros_like(gb_ref)

        for k in range(SSM_CONV):
            gw_ref[k:k + 1, :] += gws[k]
        gb_ref[...] += jnp.sum(dpre, axis=0, keepdims=True)

    return pl.pallas_call(
        body, name="conv_bwd", grid=(geo.cd // cbw, geo.bsz),
        in_specs=[pl.BlockSpec((lp, cbw), lambda j, b: (b, colmap(j))),
                  pl.BlockSpec((SSM_CONV, cbw), lambda j, b: (0, j)), pl.BlockSpec((1, cbw), lambda j, b: (0, j)),
                  pl.BlockSpec((lp, cbw), lambda j, b: (b, j))],
        out_specs=[pl.BlockSpec((lp, cbw), lambda j, b: (b, j)), pl.BlockSpec((SSM_CONV, cbw), lambda j, b: (0, j)),
                   pl.BlockSpec((1, cbw), lambda j, b: (0, j))],
        out_shape=[jax.ShapeDtypeStruct((geo.nrows, geo.cd), MXU_DTYPE),
                   jax.ShapeDtypeStruct((SSM_CONV, geo.cd), F32), jax.ShapeDtypeStruct((1, geo.cd), F32)],
        compiler_params=_cparams(("parallel", "arbitrary")))(proj, conv_w, conv_b, dxc)


def _tri(q):
    r = lax.broadcasted_iota(jnp.int32, (q, q), 0)
    c = lax.broadcasted_iota(jnp.int32, (q, q), 1)
    return r >= c


def _ssd_pre(dtr, dtb, alog, valid):
    dt = _softplus(dtr + dtb) * valid
    adt = dt * (-jnp.exp(alog))
    a_cs = _dot(_tri(SSM_CHUNK).astype(F32), adt, 1, 0, precision=lax.Precision.HIGHEST)
    return dt, a_cs


def _ssd_specs(geo, rev):
    nc, q = geo.nc, SSM_CHUNK
    ci = (lambda c: nc - 1 - c) if rev else (lambda c: c)
    nxb = SSM_D_INNER // geo.gn
    return [pl.BlockSpec((q, SSM_D_INNER), lambda b, c: (b * nc + ci(c), 0)),
            pl.BlockSpec((q, geo.gn), lambda b, c: (b * nc + ci(c), nxb)),
            pl.BlockSpec((q, geo.gn), lambda b, c: (b * nc + ci(c), nxb + 1)),
            pl.BlockSpec((q, LANE), lambda b, c: (b * nc + ci(c), geo.cb("dt"))),
            pl.BlockSpec((1, LANE), lambda b, c: (0, 0)), pl.BlockSpec((1, LANE), lambda b, c: (0, 0))], ci


def _expand_heads(cols, nh):
    per = LANE // SSM_HEAD_DIM
    lane = lax.broadcasted_iota(jnp.int32, (1, LANE), 1)
    blocks = []
    for j in range(nh // per):
        blk = jnp.broadcast_to(cols[:, j * per:j * per + 1], (cols.shape[0], LANE))
        for k in range(1, per):
            blk = jnp.where(lane >= k * SSM_HEAD_DIM, cols[:, j * per + k:j * per + k + 1], blk)
        blocks.append(blk)
    return jnp.concatenate(blocks, axis=1)


def _head_maps(geo):
    e = (jnp.arange(SSM_D_INNER)[None, :] // SSM_HEAD_DIM == jnp.arange(LANE)[:, None]).astype(F32)
    return e, e.T


def _ssd_fwd_g(geo, xc, proj, dt_bias, a_log):
    q, p, n, e = SSM_CHUNK, SSM_HEAD_DIM, SSM_STATE, geo.nh // SSM_GROUPS
    nc, pad, gw = geo.nc, geo.pad, SSM_D_INNER // SSM_GROUPS
    in_specs, _ = _ssd_specs(geo, False)
    _, e_t = _head_maps(geo)
    in_specs.append(pl.BlockSpec((SSM_D_INNER, LANE), lambda b, c: (0, 0)))

    def body(xs_ref, b_ref, c_ref, dtr_ref, dtb_ref, alog_ref, et_ref, y_ref, sp_ref, state, xdt_s):
        c = pl.program_id(1)

        @pl.when(c == 0)
        def _():
            state[...] = jnp.zeros_like(state)

        sp_ref[...] = state[...]
        inert = (c + 1) * q <= pad

        @pl.when(inert)
        def _():
            y_ref[...] = jnp.zeros_like(y_ref)

        @pl.when(jnp.logical_not(inert))
        def _():
            valid = (c * q + lax.broadcasted_iota(jnp.int32, (q, 1), 0) >= pad).astype(F32)
            dt, a_cs = _ssd_pre(dtr_ref[...], dtb_ref[...], alog_ref[...], valid)
            a_cst = a_cs.T
            dt_x, a_x = _expand_heads(dt, geo.nh), _expand_heads(a_cs, geo.nh)
            e_last = jnp.exp(a_cs[q - 1:q, :])
            tri = _tri(q)
            for g in range(SSM_GROUPS):
                gs = slice(g * gw, (g + 1) * gw)
                bg, cg = b_ref[:, g * n:(g + 1) * n], c_ref[:, g * n:(g + 1) * n]
                a_g = a_x[:, gs]
                xdt_g = xs_ref[:, gs] * dt_x[:, gs]
                xdt_s[:, gs] = xdt_g
                s_g = state[gs, :]
                y_ref[:, gs] = _mxdot(cg, s_g, 1, 1) * jnp.exp(a_g)
                e_last_rows = jnp.sum(et_ref[gs, :] * e_last, axis=1, keepdims=True)
                state[gs, :] = s_g * e_last_rows + _mxdot(xdt_g * jnp.exp(a_g[q - 1:q, :] - a_g), bg, 0, 0)
                cb = _mxdot(cg, bg, 1, 1)
                for hh in range(e):
                    h = g * e + hh
                    hs = slice(h * p, (h + 1) * p)
                    ldec = jnp.exp(jnp.where(tri, a_cs[:, h:h + 1] - a_cst[h:h + 1, :], -jnp.inf))
                    y_ref[:, hs] += _mxdot(cb * ldec, xdt_s[:, hs], 1, 0)

    return pl.pallas_call(
        body, name="ssd_fwd", grid=(geo.bsz, nc), in_specs=in_specs,
        out_specs=[pl.BlockSpec((q, SSM_D_INNER), lambda b, c: (b * nc + c, 0)),
                   pl.BlockSpec((SSM_D_INNER, n), lambda b, c: (b * nc + c, 0))],
        out_shape=[jax.ShapeDtypeStruct((geo.nrows, SSM_D_INNER), F32),
                   jax.ShapeDtypeStruct((geo.bsz * nc * SSM_D_INNER, n), F32)],
        scratch_shapes=[pltpu.VMEM((SSM_D_INNER, n), F32), pltpu.VMEM((q, SSM_D_INNER), F32)],
        compiler_params=_cparams(("parallel", "arbitrary")))(xc, xc, xc, proj, dt_bias, a_log, e_t)


def _ssd_bwd_g(geo, xc, proj, dt_bias, a_log, s_prev_all, dy, dxs_skip):
    q, p, n, e = SSM_CHUNK, SSM_HEAD_DIM, SSM_STATE, geo.nh // SSM_GROUPS
    nc, pad, di, gn, gw = geo.nc, geo.pad, SSM_D_INNER, geo.gn, SSM_D_INNER // SSM_GROUPS
    in_specs, ci = _ssd_specs(geo, True)
    row_spec = pl.BlockSpec((q, di), lambda b, c: (b * nc + ci(c), 0))
    e_map, e_t = _head_maps(geo)
    in_specs += [pl.BlockSpec((di, n), lambda b, c: (b * nc + ci(c), 0)), row_spec, row_spec,
                 pl.BlockSpec((LANE, di), lambda b, c: (0, 0)), pl.BlockSpec((di, LANE), lambda b, c: (0, 0))]

    def body(xs_ref, b_ref, c_ref, dtr_ref, dtb_ref, alog_ref, sp_ref, dy_ref, dsk_ref, e_ref, et_ref,
             dxc_ref, ddt_ref, gdtb_ref, galog_ref, dstate, xdt_s, dxdt_s):
        step = pl.program_id(1)
        first = jnp.logical_and(pl.program_id(0) == 0, step == 0)
        c = nc - 1 - step

        @pl.when(step == 0)
        def _():
            dstate[...] = jnp.zeros_like(dstate)

        @pl.when(first)
        def _():
            gdtb_ref[...] = jnp.zeros_like(gdtb_ref)
            galog_ref[...] = jnp.zeros_like(galog_ref)

        inert = (c + 1) * q <= pad

        @pl.when(inert)
        def _():
            dxc_ref[...] = jnp.zeros_like(dxc_ref)
            ddt_ref[...] = jnp.zeros_like(ddt_ref)

        @pl.when(jnp.logical_not(inert))
        def _():
            valid = (c * q + lax.broadcasted_iota(jnp.int32, (q, 1), 0) >= pad).astype(F32)
            dtr, dtb, alog = dtr_ref[...], dtb_ref[...], alog_ref[...]
            dt, a_cs = _ssd_pre(dtr, dtb, alog, valid)
            a_cst = a_cs.T
            dt_x, a_x = _expand_heads(dt, geo.nh), _expand_heads(a_cs, geo.nh)
            e_last = jnp.exp(a_cs[q - 1:q, :])
            tri = _tri(q)
            lane = lax.broadcasted_iota(jnp.int32, (1, LANE), 1)
            sub = lax.broadcasted_iota(jnp.int32, (LANE, 1), 0)
            d_dt = jnp.zeros((q, LANE), F32)
            d_acs = jnp.zeros((q, LANE), F32)
            d_acst = jnp.zeros((LANE, q), F32)
            d_last = jnp.zeros((1, LANE), F32)
            for g in range(SSM_GROUPS):
                gs = slice(g * gw, (g + 1) * gw)
                bg, cg = b_ref[:, g * n:(g + 1) * n], c_ref[:, g * n:(g + 1) * n]
                seg = lambda v: _mxdot(v, e_ref[:, gs], 1, 1)
                a_g, dt_g, x_g, dy_g = a_x[:, gs], dt_x[:, gs], xs_ref[:, gs], dy_ref[:, gs]
                e_col, dec = jnp.exp(a_g), jnp.exp(a_g[q - 1:q, :] - a_g)
                xdt_g = x_g * dt_g
                xdt_s[:, gs] = xdt_g
                s_g, ds_g, et_g = sp_ref[gs, :], dstate[gs, :], et_ref[gs, :]
                cs = _mxdot(cg, s_g, 1, 1)
                d_cs = dy_g * e_col
                d_acs = d_acs + seg(d_cs * cs)
                d_cg = _mxdot(d_cs, s_g, 1, 0)
                dstate[gs, :] = _mxdot(d_cs, cg, 0, 0) + ds_g * jnp.sum(et_g * e_last, axis=1, keepdims=True)
                d_last = d_last + jnp.sum(jnp.sum(ds_g * s_g, axis=1, keepdims=True) * et_g, axis=0,
                                          keepdims=True) * e_last
                gmat = _mxdot(bg, ds_g, 1, 1)
                xd = xdt_g * dec
                d_bg = _mxdot(xd, ds_g, 1, 0)
                d_dec = seg(xd * gmat)
                d_acs = d_acs - d_dec
                d_last = d_last + jnp.sum(d_dec, axis=0, keepdims=True)
                dxdt_s[:, gs] = dec * gmat
                cb = _mxdot(cg, bg, 1, 1)
                d_cb = jnp.zeros((q, q), F32)
                for hh in range(e):
                    h = g * e + hh
                    hs = slice(h * p, (h + 1) * p)
                    ldec = jnp.exp(jnp.where(tri, a_cs[:, h:h + 1] - a_cst[h:h + 1, :], -jnp.inf))
                    dyh = dy_ref[:, hs]
                    d_m = _mxdot(dyh, xdt_s[:, hs], 1, 1)
                    dxdt_s[:, hs] += _mxdot(cb * ldec, dyh, 0, 0)
                    d_cb = d_cb + d_m * ldec
                    d_diff = d_m * cb * ldec
                    d_acs = d_acs + jnp.sum(d_diff, axis=1, keepdims=True) * (lane == h).astype(F32)
                    d_acst = d_acst - (sub == h).astype(F32) * jnp.sum(d_diff, axis=0, keepdims=True)
                d_xdt = dxdt_s[:, gs]
                dxc_ref[:, gs] = d_xdt * dt_g + dsk_ref[:, gs]
                d_dt = d_dt + seg(d_xdt * x_g)
                dxc_ref[:, di + g * n:di + (g + 1) * n] = d_bg + _mxdot(d_cb, cg, 0, 0)
                dxc_ref[:, di + gn + g * n:di + gn + (g + 1) * n] = d_cg + _mxdot(d_cb, bg, 1, 0)
            is_last = (lax.broadcasted_iota(jnp.int32, (q, 1), 0) == q - 1).astype(F32)
            d_acs = d_acs + d_acst.T + is_last * d_last
            d_adt = _dot(_tri(q).astype(F32), d_acs, 0, 0, precision=lax.Precision.HIGHEST)
            a = -jnp.exp(alog)
            d_dt = d_dt + d_adt * a
            d_dtr = d_dt * valid * _sigmoid(dtr + dtb)
            ddt_ref[...] = d_dtr.astype(ddt_ref.dtype)
            gdtb_ref[...] += jnp.sum(d_dtr, axis=0, keepdims=True)
            galog_ref[...] += jnp.sum(d_adt * dt, axis=0, keepdims=True) * a

    vec = pl.BlockSpec((1, LANE), lambda b, c: (0, 0))
    return pl.pallas_call(
        body, name="ssd_bwd", grid=(geo.bsz, nc), in_specs=in_specs,
        out_specs=[pl.BlockSpec((q, geo.cd), lambda b, c: (b * nc + ci(c), 0)),
                   pl.BlockSpec((q, LANE), lambda b, c: (b * nc + ci(c), 0)), vec, vec],
        out_shape=[jax.ShapeDtypeStruct((geo.nrows, geo.cd), F32), jax.ShapeDtypeStruct((geo.nrows, LANE), MXU_DTYPE),
                   jax.ShapeDtypeStruct((1, LANE), F32), jax.ShapeDtypeStruct((1, LANE), F32)],
        scratch_shapes=[pltpu.VMEM((di, n), F32), pltpu.VMEM((q, di), F32), pltpu.VMEM((q, di), F32)],
        compiler_params=_cparams(("arbitrary", "arbitrary")))(
            xc, xc, xc, proj, dt_bias, a_log, s_prev_all, dy, dxs_skip, e_map, e_t)


BIAS_LANE = MLA_ROPE // 2
KEY_OFF = -1e30
ATT_SCALE = (MLA_NOPE + MLA_ROPE) ** -0.5


def _row_t(col):
    return jnp.broadcast_to(col, (col.shape[0], LANE)).T[:8]


def _attn_fwd2(geo, qc, kc, v):
    t, lp = ATT_BLK, geo.lp
    nb = lp // t

    def body(q_ref, k_ref, v_ref, o_ref, lse_ref):
        qi = pl.program_id(2)
        q = q_ref[...]

        def blk(kj, ntile, carry, diag):
            m, l, acc = carry
            ks = pl.ds(pl.multiple_of(kj * t, t), ntile * t)
            s = _mxdot(q, k_ref[ks, :], 1, 1) * ATT_SCALE
            if diag:
                s = jnp.where(_tri(t), s, -jnp.inf)
            m_new = jnp.maximum(m, jnp.max(s, axis=1, keepdims=True))
            pr = jnp.exp(s - m_new)
            alpha = jnp.exp(m - m_new)
            return m_new, alpha * l + jnp.sum(pr, axis=1, keepdims=True), alpha * acc + _mxdot(pr, v_ref[ks, :], 1, 0)

        init = (jnp.full((t, 1), 2.0 * KEY_OFF, F32), jnp.zeros((t, 1), F32), jnp.zeros((t, LANE), F32))
        pairs = qi // 2
        carry = lax.fori_loop(0, pairs, lambda j, c: blk(2 * j, 2, c, False), init)
        carry = lax.fori_loop(2 * pairs, qi, lambda kj, c: blk(kj, 1, c, False), carry)
        m, l, acc = blk(qi, 1, carry, True)
        o_ref[...] = acc / l
        lse_ref[0, 0, 0] = _row_t(m + jnp.log(l))

    return pl.pallas_call(
        body, name="attn_fwd", grid=(geo.bsz, MLA_HEADS, nb),
        in_specs=[pl.BlockSpec((t, 2 * LANE), lambda b, h, i: (b * nb + i, h)),
                  pl.BlockSpec((lp, 2 * LANE), lambda b, h, i: (b, h)), pl.BlockSpec((lp, LANE), lambda b, h, i: (b, h))],
        out_specs=[pl.BlockSpec((t, LANE), lambda b, h, i: (b * nb + i, h)),
                   pl.BlockSpec((1, 1, 1, 8, t), lambda b, h, i: (b, h, i, 0, 0))],
        out_shape=[jax.ShapeDtypeStruct((geo.nrows, geo.hq), F32),
                   jax.ShapeDtypeStruct((geo.bsz, MLA_HEADS, nb, 8, t), F32)],
        compiler_params=_cparams(("parallel", "parallel", "arbitrary")))(qc, kc, v)


def _attn_bwd2(geo, qc, kc, v, d_o, o, lse):
    t, lp = ATT_BLK, geo.lp
    nb = lp // t

    def body(q_ref, k_ref, v_ref, do_ref, o_ref, lse_ref, dq_ref, dk_ref, dv_ref, dl_s):
        kj = pl.program_id(2)

        @pl.when(kj == 0)
        def _():
            dq_ref[...] = jnp.zeros_like(dq_ref)
            for i in range(nb):
                rows = slice(i * t, (i + 1) * t)
                dl_s[i] = _row_t(jnp.sum(do_ref[rows, :] * o_ref[rows, :], axis=1, keepdims=True))

        k, vv = k_ref[...], v_ref[...]

        def row(ref, qi, ntile):
            return jnp.concatenate([ref[qi + i][:1, :] for i in range(ntile)], axis=1)

        def blk(qi, ntile, carry, diag):
            dk, dv = carry
            qs = pl.ds(pl.multiple_of(qi * t, t), ntile * t)
            q, d_o_blk = q_ref[qs, :], do_ref[qs, :]
            st = _mxdot(k, q, 1, 1) * ATT_SCALE
            if diag:
                keys = lax.broadcasted_iota(jnp.int32, (t, t), 0)
                st = jnp.where(keys <= lax.broadcasted_iota(jnp.int32, (t, t), 1), st, -jnp.inf)
            pt = jnp.exp(st - row(lse_ref.at[0, 0], qi, ntile))
            dst = pt * (_mxdot(vv, d_o_blk, 1, 1) - row(dl_s, qi, ntile)) * ATT_SCALE
            dq_ref[qs, :] += _mxdot(dst, k, 0, 0)
            return dk + _mxdot(dst, q, 1, 0), dv + _mxdot(pt, d_o_blk, 1, 0)

        carry = blk(kj, 1, (jnp.zeros((t, 2 * LANE), F32), jnp.zeros((t, LANE), F32)), True)
        pairs = (nb - 1 - kj) // 2
        carry = lax.fori_loop(0, pairs, lambda j, c: blk(kj + 1 + 2 * j, 2, c, False), carry)
        dk, dv = lax.fori_loop(kj + 1 + 2 * pairs, nb, lambda qi, c: blk(qi, 1, c, False), carry)
        dk_ref[...] = dk
        dv_ref[...] = dv.astype(dv_ref.dtype)

    seq = pl.BlockSpec((lp, LANE), lambda b, h, j: (b, h))
    return pl.pallas_call(
        body, name="attn_bwd", grid=(geo.bsz, MLA_HEADS, nb),
        in_specs=[pl.BlockSpec((lp, 2 * LANE), lambda b, h, j: (b, h)),
                  pl.BlockSpec((t, 2 * LANE), lambda b, h, j: (b * nb + j, h)),
                  pl.BlockSpec((t, LANE), lambda b, h, j: (b * nb + j, h)), seq, seq,
                  pl.BlockSpec((1, 1, nb, 8, t), lambda b, h, j: (b, h, 0, 0, 0))],
        out_specs=[pl.BlockSpec((lp, 2 * LANE), lambda b, h, j: (b, h)),
                   pl.BlockSpec((t, 2 * LANE), lambda b, h, j: (b * nb + j, h)),
                   pl.BlockSpec((t, LANE), lambda b, h, j: (b * nb + j, h))],
        out_shape=[jax.ShapeDtypeStruct((geo.nrows, 2 * geo.hq), F32), jax.ShapeDtypeStruct((geo.nrows, 2 * geo.hq), F32),
                   jax.ShapeDtypeStruct((geo.nrows, geo.hq), MXU_DTYPE)],
        scratch_shapes=[pltpu.VMEM((nb, 8, t), F32)],
        compiler_params=_cparams(("parallel", "parallel", "arbitrary")))(qc, kc, v, d_o, o, lse)


def _rope(x, cos, sin):
    return x * cos + pltpu.roll(x, LANE // 2, axis=1) * sin


def _rope_t(dx, cos, sin):
    return dx * cos + pltpu.roll(dx * sin, LANE // 2, axis=1)


def _layer_fwd(geo, h, w, tab, late=None):
    nr, tr, trw = geo.nrows, geo.tr, geo.tr_wide
    tb = geo.lp // tr
    rw = functools.partial(_rowwise, nrows=nr)
    s = {"h": h}
    (s["u"],) = rw("rms_mix", lambda x, g: (_rms(x, g),), tr=tr, rows=[(h, D_MODEL, 0)],
                   vecs=[(w["norm_mix_w"], D_MODEL, 0)], outs=[(D_MODEL, D_MODEL, MXU_DTYPE)])
    proj = s["proj"] = _mm("mm_in", s["u"], w["w_in_p"])
    xc = s["xc"] = _conv_fwd(geo, proj, w["conv_w"], w["conv_b"])
    s["y_ssd"], s["s_prev"] = _ssd_fwd_g(geo, xc, proj, w["dt_bias"], w["a_log"])
    gw = SSM_D_INNER // SSM_GROUPS

    def gate_norm(y, x, z, dsk, nw):
        return (_rms((y + x * dsk) * _silu(z), nw),)

    (s["y_ssm"],) = rw("ssm_gate_norm", gate_norm, tr=tr, ncb=SSM_GROUPS,
                       rows=[(s["y_ssd"], gw, 0), (xc, gw, 0), (proj, gw, geo.col["z"][0] // gw)],
                       vecs=[(w["d_skip_full"], gw, 0), (w["ssm_norm_w"], gw, 0)], outs=[(SSM_D_INNER, gw, MXU_DTYPE)])
    if late is not None:
        w = {**w, **late(s["y_ssm"])}
    (s["cq_n"],) = rw("rms_q", lambda x, g: (_rms(x, g),), tr=tr, rows=[(proj, MLA_Q_LORA, geo.cb("c_q"))],
                      vecs=[(w["q_norm_w"], MLA_Q_LORA, 0)], outs=[(MLA_Q_LORA, MLA_Q_LORA, MXU_DTYPE)])
    (s["ckv_n"],) = rw("rms_kv", lambda x, g: (_rms(x, g),), tr=tr, rows=[(proj, MLA_KV_LORA, geo.cb("c_kv"))],
                       vecs=[(w["kv_norm_w"], MLA_KV_LORA, 0)], outs=[(MLA_KV_LORA, MLA_KV_LORA, MXU_DTYPE)])
    qn = _mm("mm_qn", s["cq_n"], w["w_qn"])
    qp_raw = _mm("mm_qp", s["cq_n"], w["w_qp"])
    kn = _mm("mm_kn", s["ckv_n"], w["w_k"])
    s["v"] = _mm("mm_v", s["ckv_n"], w["w_v"], out_dtype=MXU_DTYPE)
    bias_lane = lambda: lax.broadcasted_iota(jnp.int32, (1, LANE), 1) == BIAS_LANE

    def q_cat(x, xp, c, sn):
        return (jnp.concatenate([x, jnp.where(bias_lane(), 1.0, _rope(xp, c, sn))], axis=1),)

    def k_cat(x, xp, c, sn, valid):
        return (jnp.concatenate([x, jnp.where(bias_lane(), KEY_OFF * (1.0 - valid), _rope(xp, c, sn))], axis=1),)

    rope_tabs = [(tab["cos"], LANE, 0), (tab["sin"], LANE, 0)]
    (s["qc"],) = rw("rope_q", q_cat, tr=tr, ncb=MLA_HEADS, rows=[(qn, LANE, 0), (qp_raw, LANE, 0)], tabs=rope_tabs,
                    outs=[(2 * geo.hq, 2 * LANE, MXU_DTYPE)], tab_blocks=tb)
    (s["kc"],) = rw("rope_k", k_cat, tr=tr, ncb=MLA_HEADS, rows=[(kn, LANE, 0)], fixed=[(proj, LANE, geo.cb("k_rope"))],
                    tabs=rope_tabs + [(tab["valid"], 1, 0)], outs=[(2 * geo.hq, 2 * LANE, MXU_DTYPE)], tab_blocks=tb)
    s["o"], s["lse"] = _attn_fwd2(geo, s["qc"], s["kc"], s["v"])
    s["ys_p"] = _mm("mm_bs", s["y_ssm"], w["w_branch_ssm"])
    s["ym_p"] = _mm("mm_bm", s["o"], w["w_branch_mla"])

    def gate(gs, gm, ys, ym):
        return (_sigmoid(gs) * ys + _sigmoid(gm) * ym,)

    (s["mixed"],) = rw("gate", gate, tr=tr, rows=[(proj, D_MODEL, geo.cb("g_ssm")), (proj, D_MODEL, geo.cb("g_mla")),
                                                  (s["ys_p"], D_MODEL, 0), (s["ym_p"], D_MODEL, 0)],
                       outs=[(D_MODEL, D_MODEL, MXU_DTYPE)])
    s["h2"] = _mm("mm_out", s["mixed"], w["w_out"], add=h)
    (s["vn"],) = rw("rms_mlp", lambda x, g: (_rms(x, g),), tr=tr, rows=[(s["h2"], D_MODEL, 0)],
                    vecs=[(w["norm_mlp_w"], D_MODEL, 0)], outs=[(D_MODEL, D_MODEL, MXU_DTYPE)])
    s["up"], s["act"] = _mm("mm_up", s["vn"], w["w_mlp_up"],
                            epi=(lambda r: (r, jnp.square(jnp.maximum(r, 0.0))), (F32, MXU_DTYPE)))
    return _mm("mm_down", s["act"], w["w_mlp_down"], add=s["h2"]), s, w


def _layer_bwd(geo, dh3, s, w, tab, mid=None, tail=None):
    nr, tr, trw = geo.nrows, geo.tr, geo.tr_wide
    tb = geo.lp // tr
    rw = functools.partial(_rowwise, nrows=nr)
    g = {}
    proj = s["proj"]

    def rms_bwd(x, dy, res, gw):
        _, vjp = jax.vjp(_rms, x.astype(F32), gw)
        dx, dgw = vjp(dy.astype(F32))
        return dx + res, dgw

    def rms_bwd_nores(x, dy, gw):
        _, vjp = jax.vjp(_rms, x.astype(F32), gw)
        return vjp(dy.astype(F32))

    (dup,) = _mm("mm_down_t", dh3, w["w_mlp_down"], tb=True, add=s["up"],
                 epi=(lambda r, up: (r * 2.0 * jnp.maximum(up, 0.0),), (MXU_DTYPE,)))
    g["w_mlp_down"] = _mm("mm_down_g", s["act"], dh3, ta=True, out_dtype=MXU_DTYPE)
    g["w_mlp_up"] = _mm("mm_up_g", s["vn"], dup, ta=True, out_dtype=MXU_DTYPE)
    dvn = _mm("mm_up_t", dup, w["w_mlp_up"], tb=True)
    dh2, g["norm_mlp_w"] = rw("rms_mlp_bwd", rms_bwd, tr=tr,
                              rows=[(s["h2"], D_MODEL, 0), (dvn, D_MODEL, 0), (dh3, D_MODEL, 0)],
                              vecs=[(w["norm_mlp_w"], D_MODEL, 0)], outs=[(D_MODEL, D_MODEL, F32)],
                              reds=[(D_MODEL, D_MODEL)])
    dmixed = _mm("mm_out_t", dh2, w["w_out"], tb=True)
    g["w_out"] = _mm("mm_out_g", s["mixed"], dh2, ta=True, out_dtype=MXU_DTYPE)

    def gate_bwd(gs, gm, ys, ym, dm):
        f = lambda a, b, c, d: _sigmoid(a) * c + _sigmoid(b) * d
        _, vjp = jax.vjp(f, gs, gm, ys, ym)
        dgs, dgm, dys, dym = vjp(dm)
        return dys, dym, dgs, dgm

    dys_p, dym_p, dg_ssm, dg_mla = rw(
        "gate_bwd", gate_bwd, tr=tr,
        rows=[(proj, D_MODEL, geo.cb("g_ssm")), (proj, D_MODEL, geo.cb("g_mla")), (s["ys_p"], D_MODEL, 0),
              (s["ym_p"], D_MODEL, 0), (dmixed, D_MODEL, 0)], outs=[(D_MODEL, D_MODEL, MXU_DTYPE)] * 4)
    g["w_branch_ssm"] = _mm("mm_bs_g", s["y_ssm"], dys_p, ta=True, out_dtype=MXU_DTYPE)
    dy_ssm = _mm("mm_bs_t", dys_p, w["w_branch_ssm"], tb=True)
    g["w_branch_mla"] = _mm("mm_bm_g", s["o"], dym_p, ta=True, out_dtype=MXU_DTYPE)
    d_o = _mm("mm_bm_t", dym_p, w["w_branch_mla"], tb=True)
    dqc, dkc, dv = _attn_bwd2(geo, s["qc"], s["kc"], s["v"], d_o, s["o"], s["lse"])
    rope_tabs = [(tab["cos"], LANE, 0), (tab["sin"], LANE, 0)]
    dqn, dqp_raw = rw("rope_q_bwd", lambda x, c, sn: (x[:, :LANE], _rope_t(x[:, LANE:], c, sn)), tr=tr, ncb=MLA_HEADS,
                      rows=[(dqc, 2 * LANE, 0)], tabs=rope_tabs, outs=[(geo.hq, LANE, MXU_DTYPE)] * 2, tab_blocks=tb)

    def rope_k_bwd(x, c, sn):
        tot = x[:, LANE:2 * LANE]
        for hd in range(1, MLA_HEADS):
            tot = tot + x[:, (2 * hd + 1) * LANE:(2 * hd + 2) * LANE]
        dkn_ = jnp.concatenate([x[:, 2 * hd * LANE:(2 * hd + 1) * LANE] for hd in range(MLA_HEADS)], axis=1)
        return dkn_, _rope_t(tot, c, sn)

    dkn, dk_rope = rw("rope_k_bwd", rope_k_bwd, tr=geo.tr_wide, rows=[(dkc, 2 * geo.hq, 0)], tabs=rope_tabs,
                      outs=[(geo.hq, geo.hq, MXU_DTYPE), (LANE, LANE, MXU_DTYPE)], tab_blocks=geo.lp // geo.tr_wide)
    g["w_qn"] = _mm("mm_qn_g", s["cq_n"], dqn, ta=True, out_dtype=MXU_DTYPE)
    g["w_qp"] = _mm("mm_qp_g", s["cq_n"], dqp_raw, ta=True, out_dtype=MXU_DTYPE)
    dcq_n = _mm("mm_qp_t", dqp_raw, w["w_qp"], tb=True, add=_mm("mm_qn_t", dqn, w["w_qn"], tb=True))
    g["w_k"] = _mm("mm_kn_g", s["ckv_n"], dkn, ta=True, out_dtype=MXU_DTYPE)
    g["w_v"] = _mm("mm_v_g", s["ckv_n"], dv, ta=True, out_dtype=MXU_DTYPE)
    dckv_n = _mm("mm_v_t", dv, w["w_v"], tb=True, add=_mm("mm_kn_t", dkn, w["w_k"], tb=True))
    dc_q, g["q_norm_w"] = rw("rms_q_bwd", rms_bwd_nores, tr=tr,
                             rows=[(proj, MLA_Q_LORA, geo.cb("c_q")), (dcq_n, MLA_Q_LORA, 0)],
                             vecs=[(w["q_norm_w"], MLA_Q_LORA, 0)], outs=[(MLA_Q_LORA, MLA_Q_LORA, MXU_DTYPE)],
                             reds=[(MLA_Q_LORA, MLA_Q_LORA)])
    dc_kv, g["kv_norm_w"] = rw("rms_kv_bwd", rms_bwd_nores, tr=tr,
                               rows=[(proj, MLA_KV_LORA, geo.cb("c_kv")), (dckv_n, MLA_KV_LORA, 0)],
                               vecs=[(w["kv_norm_w"], MLA_KV_LORA, 0)], outs=[(MLA_KV_LORA, MLA_KV_LORA, MXU_DTYPE)],
                               reds=[(MLA_KV_LORA, MLA_KV_LORA)])
    gw_ = SSM_D_INNER // SSM_GROUPS
    d_skip_full = w["d_skip_full"] if mid is None else w["d_skip_full"] + mid(g)[0, 0]

    def gate_norm_bwd(y, x, z, dy, dsk, nw):
        f = lambda y_, x_, z_, dsk_, nw_: _rms((y_ + x_ * dsk_) * _silu(z_), nw_)
        _, vjp = jax.vjp(f, y, x, z, dsk, nw)
        dy_, dx_, dz_, ddsk, dnw = vjp(dy)
        return dy_, dx_, dz_, ddsk, dnw

    dy_ssd, dxs_skip, dz, g["d_skip_full"], g["ssm_norm_w"] = rw(
        "ssm_gate_norm_bwd", gate_norm_bwd, tr=tr, ncb=SSM_GROUPS,
        rows=[(s["y_ssd"], gw_, 0), (s["xc"], gw_, 0), (proj, gw_, geo.col["z"][0] // gw_), (dy_ssm, gw_, 0)],
        vecs=[(d_skip_full, gw_, 0), (w["ssm_norm_w"], gw_, 0)],
        outs=[(SSM_D_INNER, gw_, F32), (SSM_D_INNER, gw_, F32), (SSM_D_INNER, gw_, MXU_DTYPE)],
        reds=[(SSM_D_INNER, gw_), (SSM_D_INNER, gw_)])
    dxc, ddt, g["dt_bias"], g["a_log"] = _ssd_bwd_g(geo, s["xc"], proj, w["dt_bias"], w["a_log"], s["s_prev"],
                                                   dy_ssd, dxs_skip)
    dxbc, g["conv_w"], g["conv_b"] = _conv_bwd(geo, proj, w["conv_w"], w["conv_b"], dxc)
    di, gn = SSM_D_INNER, geo.gn
    dproj = jnp.concatenate([dz, dxbc[:, :di], dg_ssm, dg_mla, dxbc[:, di:di + gn], dxbc[:, di + gn:], dc_q, dc_kv,
                             ddt, dk_rope], axis=-1)
    g["w_in_p"] = _mm("mm_in_g", s["u"], dproj, ta=True, out_dtype=MXU_DTYPE)
    du = _mm("mm_in_t", dproj, w["w_in_p"], tb=True, dep=None if tail is None else tail(g))
    dh, g["norm_mix_w"] = rw("rms_mix_bwd", rms_bwd, tr=tr,
                             rows=[(s["h"], D_MODEL, 0), (du, D_MODEL, 0), (dh2, D_MODEL, 0)],
                             vecs=[(w["norm_mix_w"], D_MODEL, 0)], outs=[(D_MODEL, D_MODEL, F32)],
                             reds=[(D_MODEL, D_MODEL)])
    return dh, g


def _loss_bwd(geo, h, fw, target, tab):
    tr = geo.tr

    def fn(x, tgt, gw, tok):
        def lossf(x_, gw_):
            err = jnp.square(_rms(x_, gw_) - tgt)
            return 0.5 * jnp.sum(tok * jnp.mean(err, axis=-1, keepdims=True), axis=0, keepdims=True)

        val, vjp = jax.vjp(lossf, x, gw)
        dx, dgw = vjp(jnp.ones((1, 1), F32))
        return dx, jnp.broadcast_to(val, (1, LANE)), dgw

    return _rowwise("loss", fn, nrows=geo.nrows, tr=tr, rows=[(h, D_MODEL, 0), (target, D_MODEL, 0)],
                    vecs=[(fw, D_MODEL, 0)], tabs=[(tab["token"], 1, 0)], outs=[(D_MODEL, D_MODEL, F32)],
                    reds=[(LANE, LANE), (D_MODEL, D_MODEL)], tab_blocks=geo.lp // tr)


def kernel(x, meta_tokens, norm_mix_w, w_in, conv_w, conv_b, dt_bias, a_log, d_skip, ssm_norm_w, q_norm_w, kv_norm_w, w_uq, w_ukv, w_branch_ssm, w_branch_mla, w_out, norm_mlp_w, w_mlp_up, w_mlp_down, final_norm_w, loss_target, m_meta_tokens, m_norm_mix_w, m_w_in, m_conv_w, m_conv_b, m_dt_bias, m_a_log, m_d_skip, m_ssm_norm_w, m_q_norm_w, m_kv_norm_w, m_w_uq, m_w_ukv, m_w_branch_ssm, m_w_branch_mla, m_w_out, m_norm_mlp_w, m_w_mlp_up, m_w_mlp_down, m_final_norm_w, v_meta_tokens, v_norm_mix_w, v_w_in, v_conv_w, v_conv_b, v_dt_bias, v_a_log, v_d_skip, v_ssm_norm_w, v_q_norm_w, v_kv_norm_w, v_w_uq, v_w_ukv, v_w_branch_ssm, v_w_branch_mla, v_w_out, v_norm_mlp_w, v_w_mlp_up, v_w_mlp_down, v_final_norm_w):
    args = dict(locals())
    wts = {n: args[n] for n in WEIGHTS}
    mom = {n: args["m_" + n] for n in WEIGHTS}
    var = {n: args["v_" + n] for n in WEIGHTS}
    bsz, seq, _ = x.shape
    depth = w_in.shape[0]
    geo = _Geo(bsz, seq)
    tab = _tables(geo)

    big_names = [n for n, _ in BIG]
    sh_names = big_names + [n for n, _ in SHARDED_F32]
    kinds = dict(BIG + SHARDED_F32)
    shard3 = lambda a: a.reshape((1,) + a.shape) if a.ndim == 2 else a
    wire = {n: (MXU_DTYPE if n in big_names else F32) for n in sh_names}
    cast = {n: shard3(wts[n]).astype(wire[n]) for n in sh_names}
    per_layer = [n for n in sh_names if n != "meta_tokens"]
    small_names = ["norm_mix_w", "conv_b", "dt_bias", "a_log", "d_skip", "ssm_norm_w", "q_norm_w", "kv_norm_w",
                   "norm_mlp_w"]

    def gather_items(pairs):
        ins, outs, items, forms = [], [], [], []
        for n, i in pairs:
            a, b = cast[n].shape[1:]
            shape, dst, form = _gather_plan(a, b, kinds[n])
            items.append((len(ins), len(outs), (lambda ref, p, i=i: ref.at[i]), dst))
            ins.append(cast[n])
            outs.append(jax.ShapeDtypeStruct(shape, wire[n]))
            forms.append(form)
        return ins, outs, items, forms

    def whole_weights(pairs, forms, got):
        by_layer = {}
        for (n, i), form, g in zip(pairs, forms, got):
            if n == "w_in":
                n, g = "w_in_p", _w_in_assemble(geo, g)
            elif form == "row":
                g = g.reshape(g.shape[0] * g.shape[1], g.shape[2])
            elif form == "stack":
                g = _unshard(g, "col")
            by_layer.setdefault(i, {})[n] = g
        return by_layer

    def prep(i, whole, token=None):
        wl = dict(whole)
        wl.update({n: wts[n][i] for n in small_names})
        if token is not None:
            wl["norm_mix_w"] = wl["norm_mix_w"] + token[0, 0]
        return _prep_layer(geo, wl)

    early = ("w_in", "conv_w")
    late_names = [n for n in per_layer if n not in early]
    pairs1 = [(n, i) for i in range(1, depth) for n in per_layer]
    groups = [[(n, 0) for n in early] + [("meta_tokens", 0)], [(n, 0) for n in late_names]] + ([pairs1] if pairs1 else [])
    started = {}

    def gather_start(gi, dep=None):
        ins, outs, items, forms = gather_items(groups[gi])
        sems, thru, landing, token = _exchange_start("gather_w%d_start" % gi, ins, outs, items, dep)
        started[gi] = (groups[gi], forms, sems, thru, landing, items)
        return token

    def gathered(gi, after):
        pairs, forms, sems, thru, landing, items = started[gi]
        return whole_weights(pairs, forms, _exchange_wait("gather_w%d_wait" % gi, sems, thru, landing, items, after))

    def late0(after):
        whole = gathered(1, after)[0]
        if pairs1:
            whole["q_norm_w"] = wts["q_norm_w"][0] + gather_start(2, whole["w_out"])[0, 0]
        return _prep_layer(geo, whole)

    token = gather_start(1, gather_start(0))
    whole0 = gathered(0, token)[0]
    meta_full = whole0.pop("meta_tokens")

    meta = jnp.broadcast_to(meta_full[None], (bsz, N_META, D_MODEL))
    h = jnp.concatenate([jnp.zeros((bsz, geo.pad, D_MODEL), F32), meta, x], axis=1).reshape(geo.nrows, D_MODEL)
    target = jnp.concatenate([jnp.zeros((bsz, geo.pad + N_META, D_MODEL), F32), loss_target], axis=1)
    target = target.reshape(geo.nrows, D_MODEL)
    layers, saved = [], []
    for i in range(depth):
        if i == 0:
            w, late = prep(0, whole0, token), late0
        else:
            if i == 1:
                whole1 = gathered(2, h)
            w, late = prep(i, whole1[i]), None
        h, s, w = _layer_fwd(geo, h, w, tab, late)
        layers.append(w)
        saved.append(s)
    dh, loss_part, g_final = _loss_bwd(geo, h, final_norm_w.reshape(1, -1), target, tab)

    def scatter_items(pairs):
        ins, outs, items = [], [], []
        for n, i in pairs:
            a, b = cast[n].shape[1:]
            arr = g_meta if n == "meta_tokens" else grads[i]["w_in_p" if n == "w_in" else n]
            if n == "w_in":
                arr, src = _w_in_split(geo, arr, b), _entry
            elif kinds[n] == "row":
                src = lambda ref, p, a=a: ref.at[pl.ds(pl.multiple_of(p * a, a), a)]
            elif b % LANE == 0:
                src = lambda ref, p, b=b: ref.at[:, pl.ds(pl.multiple_of(p * b, b), b)]
            else:
                arr, src = _shard(arr, "col"), _entry
            items.append((len(ins), len(outs), src, _entry))
            ins.append(arr.astype(wire[n]))
            outs.append(jax.ShapeDtypeStruct((N_DEV, a, b), wire[n]))
        return ins, outs, items

    grads = [None] * depth
    landed, pending, res = {}, {}, {}

    def scatter_start(name, pairs):
        ins, outs, items = scatter_items(pairs)
        sems, thru, landing, token = _exchange_start(name + "_start", ins, outs, items)
        pending[name] = (pairs, sems, thru, landing, items)
        return token

    def scatter_wait(name, after):
        pairs, sems, thru, landing, items = pending[name]
        landed.update(zip(pairs, _exchange_wait(name + "_wait", sems, thru, landing, items, after)))

    def adam(n):
        parts = [landed[(n, i)] for i in range(cast[n].shape[0])]
        r = _adamw_nat("adamw_" + n, parts, shard3(wts[n]), shard3(mom[n]), shard3(var[n]))
        res[n] = [a.reshape(wts[n].shape) for a in r]

    def mid0(g):
        grads[0] = _unprep_grads(geo, g)
        return scatter_start("scatter_gb0", [(n, 0) for n in late_names])

    def tail0(g):
        grads[0] = _unprep_grads(geo, g)
        return scatter_start("scatter_ga0", [(n, 0) for n in early])

    for i in reversed(range(depth)):
        dh, gl = _layer_bwd(geo, dh, saved[i], layers[i], tab, *((mid0, tail0) if i == 0 else ()))
        grads[i] = _unprep_grads(geo, gl)
        if i == 1:
            dh = dh + scatter_start("scatter_g1", pairs1)[0, 0]
    dh = dh.reshape(bsz, geo.lp, D_MODEL)
    grad_x = dh[:, geo.pad + N_META:]
    g_meta = jnp.sum(dh[:, geo.pad:geo.pad + N_META], axis=0)
    if pairs1:
        scatter_wait("scatter_g1", g_meta)
    scatter_wait("scatter_gb0", g_meta)
    for n in late_names:
        adam(n)
    g_small = {n: jnp.stack([grads[i][n] for i in range(depth)]) for n in SMALL if n != "final_norm_w"}
    g_small["final_norm_w"] = g_final.reshape(-1)
    zero = jnp.zeros((1,), F32)
    pk = lambda d, last: _pack([d[n] for n in SMALL] + [last], F32, row_mult=8)
    packed = pk(g_small, loss_part[0, :1])
    ins, outs, items = scatter_items([("meta_tokens", 0)])
    parts, landed[("meta_tokens", 0)] = _exchange(
        "gather_g", [packed] + ins, [jax.ShapeDtypeStruct((N_DEV,) + packed.shape, F32)] + outs,
        [(0, 0, _whole, _entry)] + [(1, 1, items[0][2], items[0][3])])
    adam("meta_tokens")
    scatter_wait("scatter_ga0", res["meta_tokens"][1])
    for n in early:
        adam(n)
    res_sm = _adamw("adamw_small", parts, pk(wts, zero), pk(mom, zero), pk(var, zero))
    res_sm = [_unpack(r, [wts[n].shape for n in SMALL] + [(1,)]) for r in res_sm]
    loss = res_sm[0][-1][0]

    out = [loss, grad_x]
    for k in range(4):
        named = {n: res[n][k] for n in sh_names}
        named.update(zip(SMALL, res_sm[k]))
        out += [named[n] for n in WEIGHTS]
    return tuple(out)
```

```python
import functools

import numpy as np
import jax
import jax.numpy as jnp
from jax import lax
from jax.experimental import pallas as pl
from jax.experimental.pallas import tpu as pltpu

F32 = jnp.float32
MXU_DTYPE = jnp.bfloat16

D_MODEL = 1024
N_META = 16
EPS = 1e-6
SSM_D_INNER = 2048
SSM_HEAD_DIM = 64
SSM_GROUPS = 4
SSM_STATE = 128
SSM_CONV = 4
SSM_CHUNK = 128
MLA_HEADS = 8
MLA_Q_LORA = 512
MLA_KV_LORA = 256
MLA_NOPE = 128
MLA_ROPE = 64
MLA_V = 128
ROPE_THETA = 10000.0
D_FF = 4096
ADAM_LR = 0.001
ADAM_B1 = 0.9
ADAM_B2 = 0.999
ADAM_EPS = 1e-08
ADAM_WD = 0.01
ADAM_STEP = 10

N_DEV = 8
ATT_BLK = 256
LANE = 128
PACK_W = 1024
VMEM_LIMIT = 56 * 1024 * 1024
MESH_ID = pl.DeviceIdType.MESH

BIG = (("w_in", "col"), ("w_uq", "col"), ("w_ukv", "col"), ("w_branch_ssm", "row"), ("w_branch_mla", "row"),
       ("w_out", "row"), ("w_mlp_up", "col"), ("w_mlp_down", "row"))
SHARDED_F32 = (("conv_w", "col"), ("meta_tokens", "col"))
SMALL = ("norm_mix_w", "conv_b", "dt_bias", "a_log", "d_skip", "ssm_norm_w", "q_norm_w", "kv_norm_w",
         "norm_mlp_w", "final_norm_w")
WEIGHTS = ("meta_tokens", "norm_mix_w", "w_in", "conv_w", "conv_b", "dt_bias", "a_log", "d_skip", "ssm_norm_w",
           "q_norm_w", "kv_norm_w", "w_uq", "w_ukv", "w_branch_ssm", "w_branch_mla", "w_out", "norm_mlp_w",
           "w_mlp_up", "w_mlp_down", "final_norm_w")


def _cparams(sem=None):
    return pltpu.CompilerParams(dimension_semantics=sem, vmem_limit_bytes=VMEM_LIMIT)


def _pick(n, cands):
    for c in cands:
        if n % c == 0:
            return c
    return n


def _sigmoid(x):
    return 1.0 / (1.0 + jnp.exp(-x))


def _silu(x):
    return x * _sigmoid(x)


def _softplus(x):
    return jnp.maximum(x, 0.0) + jnp.log1p(jnp.exp(-jnp.abs(x)))


def _rms(x, w):
    return x * lax.rsqrt(jnp.mean(x * x, axis=-1, keepdims=True) + EPS) * w


def _dot(a, b, ca, cb, precision=None):
    return lax.dot_general(a, b, (((ca,), (cb,)), ((), ())), preferred_element_type=F32, precision=precision)


def _mxdot(a, b, ca, cb):
    return _dot(a.astype(MXU_DTYPE), b.astype(MXU_DTYPE), ca, cb)


def _mm(name, a, b, *, ta=False, tb=False, add=None, out_dtype=F32, dep=None, epi=None):
    (kdim, m) = a.shape if ta else a.shape[::-1]
    (n, k2) = b.shape if tb else b.shape[::-1]
    assert kdim == k2, (name, a.shape, b.shape)
    tm = _pick(m, (1152, 1024, 768, 512, 384, 256, 128))
    tn = _pick(n, (1024, 512, 384, 256, 128))
    tk = _pick(kdim, (1152, 1024, 768, 512, 384, 256, 128))
    nk = kdim // tk
    a_spec = pl.BlockSpec((tk, tm), lambda i, j, k: (k, i)) if ta else pl.BlockSpec((tm, tk), lambda i, j, k: (i, k))
    b_spec = pl.BlockSpec((tn, tk), lambda i, j, k: (j, k)) if tb else pl.BlockSpec((tk, tn), lambda i, j, k: (k, j))
    o_spec = pl.BlockSpec((tm, tn), lambda i, j, k: (i, j))
    ca, cb = (0 if ta else 1), (1 if tb else 0)

    out_dtypes = [out_dtype] if epi is None else list(epi[1])
    n_out = len(out_dtypes)

    def body(*refs):
        a_ref, b_ref = refs[:2]
        o_refs, acc = refs[-1 - n_out:-1], refs[-1]
        k = pl.program_id(2)

        @pl.when(k == 0)
        def _():
            acc[...] = jnp.zeros_like(acc)

        acc[...] += _mxdot(a_ref[...], b_ref[...], ca, cb)

        @pl.when(k == nk - 1)
        def _():
            r = acc[...]
            if epi is not None:
                res = epi[0](r, refs[2][...]) if add is not None else epi[0](r)
            else:
                res = (r + refs[2][...].astype(F32) if add is not None else r,)
            for o_ref, val in zip(o_refs, res):
                o_ref[...] = val.astype(o_ref.dtype)

    in_specs, args = [a_spec, b_spec], [a, b]
    if add is not None:
        in_specs.append(o_spec)
        args.append(add)
    if dep is not None:
        in_specs.append(pl.BlockSpec((8, LANE), lambda i, j, k: (0, 0)))
        args.append(dep)
    res = pl.pallas_call(
        body, name=name, grid=(m // tm, n // tn, nk), in_specs=in_specs, out_specs=[o_spec] * n_out,
        out_shape=[jax.ShapeDtypeStruct((m, n), dt) for dt in out_dtypes], scratch_shapes=[pltpu.VMEM((tm, tn), F32)],
        compiler_params=_cparams(("parallel", "parallel", "arbitrary")))(*args)
    return res[0] if epi is None else res


def _rowwise(name, fn, *, nrows, tr, ncb=1, rows=(), fixed=(), vecs=(), tabs=(), outs=(), reds=(), tab_blocks=1):
    in_specs, args = [], []
    for arr, w, c0 in rows:
        in_specs.append(pl.BlockSpec((tr, w), lambda g, i, c0=c0: (i, c0 + g)))
        args.append(arr)
    for arr, w, c0 in fixed:
        in_specs.append(pl.BlockSpec((tr, w), lambda g, i, c0=c0: (i, c0)))
        args.append(arr)
    for arr, w, c0 in vecs:
        in_specs.append(pl.BlockSpec((1, w), lambda g, i, c0=c0: (0, c0 + g)))
        args.append(arr)
    for arr, w, c0 in tabs:
        in_specs.append(pl.BlockSpec((tr, w), lambda g, i, c0=c0: (i % tab_blocks, c0)))
        args.append(arr)
    out_shape = [jax.ShapeDtypeStruct((nrows, wt), dt) for wt, w, dt in outs]
    out_shape += [jax.ShapeDtypeStruct((1, wt), F32) for wt, w in reds]
    out_specs = [pl.BlockSpec((tr, w), lambda g, i: (i, g)) for wt, w, dt in outs]
    out_specs += [pl.BlockSpec((1, w), lambda g, i: (0, g)) for wt, w in reds]
    n_in, n_out = len(args), len(outs)

    def body(*refs):
        res = fn(*[r[...] for r in refs[:n_in]])
        for o_ref, val in zip(refs[n_in:n_in + n_out], res[:n_out]):
            o_ref[...] = val.astype(o_ref.dtype)
        i = pl.program_id(1)
        for d_ref, val in zip(refs[n_in + n_out:], res[n_out:]):
            @pl.when(i == 0)
            def _(d_ref=d_ref, val=val):
                d_ref[...] = val

            @pl.when(i > 0)
            def _(d_ref=d_ref, val=val):
                d_ref[...] += val

    res = pl.pallas_call(
        body, name=name, grid=(ncb, nrows // tr), in_specs=in_specs, out_specs=out_specs, out_shape=out_shape,
        compiler_params=_cparams(("parallel", "arbitrary")))(*args)
    return res


def _peer(k):
    x, y, c = lax.axis_index("x"), lax.axis_index("y"), lax.axis_index("c")
    px = jnp.where((k >> 2) & 1, 1 - x, x)
    py = jnp.where((k >> 1) & 1, 1 - y, y)
    pc = jnp.where(k & 1, 1 - c, c)
    return (px, py, pc), 4 * px + 2 * py + pc


def _my_index():
    return 4 * lax.axis_index("x") + 2 * lax.axis_index("y") + lax.axis_index("c")


def _exchange(name, ins, out_shapes, items):
    n_in, n_out, n_it = len(ins), len(out_shapes), len(items)

    def body(*refs):
        x, o = refs[:n_in], refs[n_in:n_in + n_out]
        send_sems, recv_sems, local_sems = refs[n_in + n_out:]
        me = _my_index()
        local, sends = [], []
        for t, (ii, io, src, dst) in enumerate(items):
            cp = pltpu.make_async_copy(src(x[ii], me), dst(o[io], me), local_sems.at[t])
            cp.start()
            local.append(cp)
        for k in range(1, N_DEV):
            dev, idx = _peer(k)
            for t, (ii, io, src, dst) in enumerate(items):
                s = (k - 1) * n_it + t
                cp = pltpu.make_async_remote_copy(
                    src_ref=src(x[ii], idx), dst_ref=dst(o[io], me), send_sem=send_sems.at[s],
                    recv_sem=recv_sems.at[s], device_id=dev, device_id_type=MESH_ID)
                cp.start()
                sends.append(cp)
        for k in range(1, N_DEV):
            dev, idx = _peer(k)
            for t, (ii, io, src, dst) in enumerate(items):
                s = (k - 1) * n_it + t
                pltpu.make_async_remote_copy(
                    src_ref=src(x[ii], idx), dst_ref=dst(o[io], idx), send_sem=send_sems.at[s],
                    recv_sem=recv_sems.at[s], device_id=dev, device_id_type=MESH_ID).wait_recv()
        for cp in sends:
            cp.wait_send()
        for cp in local:
            cp.wait()

    nsem = (N_DEV - 1) * n_it
    anyspec = pl.BlockSpec(memory_space=pl.ANY)
    return pl.pallas_call(
        body, name=name, out_shape=list(out_shapes), in_specs=[anyspec] * n_in, out_specs=[anyspec] * n_out,
        scratch_shapes=[pltpu.SemaphoreType.DMA((nsem,)), pltpu.SemaphoreType.DMA((nsem,)),
                        pltpu.SemaphoreType.DMA((n_it,))],
        compiler_params=pltpu.CompilerParams(has_side_effects=True))(*ins)


def _split_copies(x, land, send_sems, recv_sems, items, receive):
    me = _my_index()
    remote, n_it = [], len(items)
    for k in range(1, N_DEV):
        dev, idx = _peer(k)
        for t, (ii, io, src, dst) in enumerate(items):
            s = (k - 1) * n_it + t
            remote.append(pltpu.make_async_remote_copy(
                src_ref=src(x[ii], idx), dst_ref=dst(land[io], idx if receive else me), send_sem=send_sems.at[s],
                recv_sem=recv_sems.at[s], device_id=dev, device_id_type=MESH_ID))
    local = [pltpu.make_async_copy(src(x[ii], me), dst(land[io], me), send_sems.at[(N_DEV - 1) * n_it + t])
             for t, (ii, io, src, dst) in enumerate(items)]
    return remote, local


def _exchange_start(name, ins, out_shapes, items, dep=None):
    n_in, n_out, n_it = len(ins), len(out_shapes), len(items)

    def body(*refs):
        x, land = refs[:n_in], refs[n_in:n_in + n_out]
        first_out = n_in + n_out + (dep is not None)
        send_sems, recv_sems, token = refs[first_out], refs[first_out + 1], refs[-1]
        remote, local = _split_copies(x, land, send_sems, recv_sems, items, False)
        for cp in remote + local:
            cp.start()
        token[...] = jnp.zeros_like(token)

    hbm = pl.BlockSpec(memory_space=pltpu.HBM)
    sem = pl.BlockSpec(memory_space=pltpu.SEMAPHORE)
    arrs = [pltpu.with_memory_space_constraint(a, pltpu.HBM)
            for a in list(ins) + [lax.empty(s.shape, s.dtype) for s in out_shapes]]
    res = pl.pallas_call(
        body, name=name,
        out_shape=(pltpu.SemaphoreType.DMA((N_DEV * n_it,)), pltpu.SemaphoreType.DMA(((N_DEV - 1) * n_it,)),
                   *[pltpu.HBM(a.shape, a.dtype) for a in arrs], jax.ShapeDtypeStruct((8, LANE), F32)),
        in_specs=[hbm] * (n_in + n_out) + ([] if dep is None else [pl.BlockSpec(memory_space=pl.ANY)]),
        out_specs=(sem, sem, *[hbm] * (n_in + n_out), pl.BlockSpec(memory_space=pltpu.VMEM)),
        input_output_aliases={i: 2 + i for i in range(n_in + n_out)},
        compiler_params=pltpu.CompilerParams(has_side_effects=pltpu.SideEffectType.DATAFLOW_SIDE_EFFECTING))(
            *arrs, *([] if dep is None else [dep]))
    return res[:2], res[2:2 + n_in], res[2 + n_in:2 + n_in + n_out], res[-1]


def _exchange_wait(name, sems, ins, landing, items, after):
    n_in, n_out = len(ins), len(landing)

    def body(*refs):
        x, land = refs[:n_in], refs[n_in:n_in + n_out]
        send_sems, recv_sems = refs[n_in + n_out], refs[n_in + n_out + 1]
        remote, local = _split_copies(x, land, send_sems, recv_sems, items, True)
        for cp in remote:
            cp.wait_send()
            cp.wait_recv()
        for cp in local:
            cp.wait()

    hbm = pl.BlockSpec(memory_space=pltpu.HBM)
    sem = pl.BlockSpec(memory_space=pltpu.SEMAPHORE)
    arrs = list(ins) + list(landing)
    res = pl.pallas_call(
        body, name=name, out_shape=tuple(pltpu.HBM(a.shape, a.dtype) for a in arrs),
        in_specs=[hbm] * (n_in + n_out) + [sem, sem, pl.BlockSpec(memory_space=pl.ANY)],
        out_specs=tuple([hbm] * (n_in + n_out)), input_output_aliases={i: i for i in range(n_in + n_out)},
        compiler_params=pltpu.CompilerParams(has_side_effects=pltpu.SideEffectType.DATAFLOW_SIDE_EFFECTING))(
            *arrs, *sems, after)
    return res[n_in:]


def _whole(ref, p):
    return ref


def _entry(ref, p):
    return ref.at[p]


def _gather_plan(a, b, kind):
    if kind == "col" and b % LANE == 0:
        return (a, N_DEV * b), (lambda ref, p: ref.at[:, pl.ds(pl.multiple_of(p * b, b), b)]), "col"
    return (N_DEV, a, b), _entry, ("row" if kind == "row" else "stack")


def _adamw_nat(name, parts, w, m, v):
    depth, b, c = w.shape
    assert len(parts) == depth
    tb = _pick(b, (128, 64, 32, 16, 8))
    spec = pl.BlockSpec((1, tb, c), lambda i, j: (i, j, 0))

    def body(*refs):
        p_refs = refs[:depth]
        w_ref, m_ref, v_ref, g_ref, d_ref, nm_ref, nv_ref = refs[depth:]
        for layer, p_ref in enumerate(p_refs):
            @pl.when(pl.program_id(0) == layer)
            def _(p_ref=p_ref):
                g = p_ref[0].astype(F32)
                for j in range(1, N_DEV):
                    g = g + p_ref[j].astype(F32)
                nm = ADAM_B1 * m_ref[0] + (1.0 - ADAM_B1) * g
                nv = ADAM_B2 * v_ref[0] + (1.0 - ADAM_B2) * jnp.square(g)
                m_hat = nm / (1.0 - ADAM_B1 ** ADAM_STEP)
                v_hat = nv / (1.0 - ADAM_B2 ** ADAM_STEP)
                g_ref[0] = g
                d_ref[0] = -ADAM_LR * (m_hat / (jnp.sqrt(v_hat) + ADAM_EPS) + ADAM_WD * w_ref[0])
                nm_ref[0] = nm
                nv_ref[0] = nv

    sds = jax.ShapeDtypeStruct((depth, b, c), F32)
    return pl.pallas_call(
        body, name=name, grid=(depth, b // tb),
        in_specs=[pl.BlockSpec((N_DEV, tb, c), lambda i, j: (0, j, 0))] * depth + [spec, spec, spec],
        out_specs=[spec] * 4, out_shape=[sds] * 4, compiler_params=_cparams(("parallel", "parallel")))(*parts, w, m, v)


def _adamw(name, parts, w, m, v):
    rows = w.shape[0]
    tr = _pick(rows, (256, 128, 64, 32, 16, 8))
    spec = pl.BlockSpec((tr, PACK_W), lambda i: (i, 0))

    def body(p_ref, w_ref, m_ref, v_ref, g_ref, d_ref, nm_ref, nv_ref):
        g = p_ref[0]
        for j in range(1, N_DEV):
            g = g + p_ref[j]
        nm = ADAM_B1 * m_ref[...] + (1.0 - ADAM_B1) * g
        nv = ADAM_B2 * v_ref[...] + (1.0 - ADAM_B2) * jnp.square(g)
        m_hat = nm / (1.0 - ADAM_B1 ** ADAM_STEP)
        v_hat = nv / (1.0 - ADAM_B2 ** ADAM_STEP)
        g_ref[...] = g
        d_ref[...] = -ADAM_LR * (m_hat / (jnp.sqrt(v_hat) + ADAM_EPS) + ADAM_WD * w_ref[...])
        nm_ref[...] = nm
        nv_ref[...] = nv

    sds = jax.ShapeDtypeStruct((rows, PACK_W), F32)
    return pl.pallas_call(
        body, name=name, grid=(rows // tr,),
        in_specs=[pl.BlockSpec((N_DEV, tr, PACK_W), lambda i: (0, i, 0)), spec, spec, spec],
        out_specs=[spec] * 4, out_shape=[sds] * 4, compiler_params=_cparams(("parallel",)))(parts, w, m, v)


def _pack(arrs, dtype, row_mult=16):
    flat = jnp.concatenate([a.reshape(-1).astype(dtype) for a in arrs])
    unit = row_mult * PACK_W
    total = -(-flat.shape[0] // unit) * unit
    flat = jnp.pad(flat, (0, total - flat.shape[0]))
    return flat.reshape(-1, PACK_W)


def _pack_lead(arrs, dtype, row_mult):
    flat = jnp.concatenate([a.reshape(N_DEV, -1).astype(dtype) for a in arrs], axis=1)
    unit = row_mult * PACK_W
    total = -(-flat.shape[1] // unit) * unit
    flat = jnp.pad(flat, ((0, 0), (0, total - flat.shape[1])))
    return flat.reshape(N_DEV, -1, PACK_W)


def _unpack(buf, shapes, lead=()):
    flat = buf.reshape(lead + (-1,))
    out, off = [], 0
    for s in shapes:
        n = int(np.prod(s))
        out.append(flat[..., off:off + n].reshape(lead + tuple(s)))
        off += n
    return out


def _unshard(g, kind):
    if kind == "col":
        g = jnp.moveaxis(g, 0, -2)
        return g.reshape(g.shape[:-2] + (g.shape[-2] * g.shape[-1],))
    g = jnp.moveaxis(g, 0, 1)
    return g.reshape((g.shape[0], g.shape[1] * g.shape[2]) + g.shape[3:])


def _shard(full, kind):
    if kind == "col":
        s = full.reshape(full.shape[:-1] + (N_DEV, full.shape[-1] // N_DEV))
        return jnp.moveaxis(s, -2, 0)
    s = full.reshape((full.shape[0], N_DEV, full.shape[1] // N_DEV) + full.shape[2:])
    return jnp.moveaxis(s, 1, 0)


class _Geo:
    def __init__(self, bsz, seq):
        self.bsz, self.seq = bsz, seq
        self.pad = (-(N_META + seq)) % ATT_BLK
        self.lp = self.pad + N_META + seq
        assert (self.pad + N_META) % SSM_CHUNK == 0 and self.lp % SSM_CHUNK == 0
        self.nrows = bsz * self.lp
        self.nc = self.lp // SSM_CHUNK
        self.nh = SSM_D_INNER // SSM_HEAD_DIM
        self.gn = SSM_GROUPS * SSM_STATE
        self.cd = SSM_D_INNER + 2 * self.gn
        self.hq = MLA_HEADS * LANE
        order = (("z", SSM_D_INNER), ("xs", SSM_D_INNER), ("g_ssm", D_MODEL), ("g_mla", D_MODEL), ("bm", self.gn),
                 ("cm", self.gn), ("c_q", MLA_Q_LORA), ("c_kv", MLA_KV_LORA), ("dt", LANE), ("k_rope", LANE))
        self.col, off = {}, 0
        for nm, w in order:
            assert off % w == 0, (nm, off, w)
            self.col[nm] = (off, w)
            off += w
        self.pw = off
        assert self.nh <= LANE and MLA_ROPE == 64 and MLA_NOPE == LANE and MLA_V == LANE
        self.tr = _pick(self.lp, (768, 512, 384, 256, 128))
        self.tr_wide = _pick(self.lp, (384, 256, 128))

    def cb(self, nm):
        off, w = self.col[nm]
        return off // w

    def w_in_runs(self, shard_w):
        nh, half = self.nh, MLA_ROPE // 2
        src, pieces = 0, []
        for nm, n in (("z", SSM_D_INNER), ("xs", SSM_D_INNER), ("bm", self.gn), ("cm", self.gn), ("dt", nh),
                      ("c_q", MLA_Q_LORA), ("c_kv", MLA_KV_LORA), ("k_rope", MLA_ROPE), ("g_ssm", D_MODEL),
                      ("g_mla", D_MODEL)):
            dst = self.col[nm][0]
            if nm == "k_rope":
                pieces += [(src, half, dst), (src + half, half, dst + 2 * half)]
            else:
                pieces.append((src, n, dst))
            src += n
        assert src == shard_w * N_DEV
        runs = []
        for a, n, dst in pieces:
            for j in range(N_DEV):
                lo, hi = max(a, j * shard_w), min(a + n, (j + 1) * shard_w)
                if lo < hi:
                    runs.append((j, lo - j * shard_w, hi - lo, dst + lo - a))
        return runs


def _slot(a):
    h = MLA_ROPE // 2
    z = jnp.zeros(a.shape[:-1] + (h,), a.dtype)
    return jnp.concatenate([a[..., :h], z, a[..., h:], z], axis=-1)


def _unslot(a):
    h = MLA_ROPE // 2
    return jnp.concatenate([a[..., :h], a[..., 2 * h:3 * h]], axis=-1)


def _prep_layer(geo, wl):
    nh = geo.nh
    p = {}
    if "w_uq" in wl:
        uq = wl["w_uq"].reshape(MLA_Q_LORA, MLA_HEADS, MLA_NOPE + MLA_ROPE)
        p["w_qn"] = uq[..., :MLA_NOPE].reshape(MLA_Q_LORA, geo.hq)
        p["w_qp"] = _slot(uq[..., MLA_NOPE:]).reshape(MLA_Q_LORA, geo.hq)
    if "w_ukv" in wl:
        ukv = wl["w_ukv"].reshape(MLA_KV_LORA, MLA_HEADS, MLA_NOPE + MLA_V)
        p["w_k"] = ukv[..., :MLA_NOPE].reshape(MLA_KV_LORA, geo.hq)
        p["w_v"] = ukv[..., MLA_NOPE:].reshape(MLA_KV_LORA, geo.hq)
    for nm in ("w_in_p", "conv_w", "w_branch_ssm", "w_branch_mla", "w_out", "w_mlp_up", "w_mlp_down"):
        if nm in wl:
            p[nm] = wl[nm]
    for nm in ("norm_mix_w", "conv_b", "ssm_norm_w", "q_norm_w", "kv_norm_w", "norm_mlp_w"):
        if nm in wl:
            p[nm] = wl[nm].reshape(1, -1)
    if "dt_bias" in wl:
        p["dt_bias"] = jnp.pad(wl["dt_bias"], (0, LANE - nh)).reshape(1, LANE)
        p["a_log"] = jnp.pad(wl["a_log"], (0, LANE - nh)).reshape(1, LANE)
        p["d_skip_full"] = jnp.repeat(wl["d_skip"], SSM_HEAD_DIM).reshape(1, SSM_D_INNER)
    return p


def _unprep_grads(geo, g):
    nh = geo.nh
    out = {}
    if "w_qn" in g:
        qn = g["w_qn"].reshape(MLA_Q_LORA, MLA_HEADS, MLA_NOPE)
        qp = _unslot(g["w_qp"].reshape(MLA_Q_LORA, MLA_HEADS, LANE))
        out["w_uq"] = jnp.concatenate([qn, qp], axis=-1).reshape(MLA_Q_LORA, -1)
    if "w_k" in g:
        wk = g["w_k"].reshape(MLA_KV_LORA, MLA_HEADS, MLA_NOPE)
        wv = g["w_v"].reshape(MLA_KV_LORA, MLA_HEADS, MLA_V)
        out["w_ukv"] = jnp.concatenate([wk, wv], axis=-1).reshape(MLA_KV_LORA, -1)
    for nm in ("w_in_p", "w_branch_ssm", "w_branch_mla", "w_out", "w_mlp_up", "w_mlp_down", "conv_w"):
        if nm in g:
            out[nm] = g[nm]
    for nm in ("norm_mix_w", "conv_b", "ssm_norm_w", "q_norm_w", "kv_norm_w", "norm_mlp_w"):
        if nm in g:
            out[nm] = g[nm].reshape(-1)
    if "dt_bias" in g:
        out["dt_bias"] = g["dt_bias"].reshape(-1)[:nh]
        out["a_log"] = g["a_log"].reshape(-1)[:nh]
        out["d_skip"] = g["d_skip_full"].reshape(nh, SSM_HEAD_DIM).sum(-1)
    return out


def _tables(geo):
    pos = jnp.arange(geo.lp, dtype=F32) - geo.pad
    inv = ROPE_THETA ** (-jnp.arange(0, MLA_ROPE, 2, dtype=F32) / MLA_ROPE)
    ang = pos[:, None] * inv[None, :]
    cos, sin = jnp.cos(ang), jnp.sin(ang)
    z = jnp.zeros_like(cos)
    rows = jnp.arange(geo.lp)[:, None]
    return {"cos": jnp.concatenate([cos, z, cos, z], axis=-1), "sin": jnp.concatenate([-sin, z, sin, z], axis=-1),
            "valid": (rows >= geo.pad).astype(F32), "token": (rows >= geo.pad + N_META).astype(F32)}


def _w_in_assemble(geo, gathered):
    _, d, sw = gathered.shape
    runs = geo.w_in_runs(sw)
    tr = _pick(d, (256, 128))

    def body(x_ref, o_ref):
        o_ref[...] = jnp.zeros_like(o_ref)
        for j, s0, n, d0 in runs:
            o_ref[:, d0:d0 + n] = x_ref[j, :, s0:s0 + n]

    return pl.pallas_call(
        body, name="w_in_assemble", grid=(d // tr,), in_specs=[pl.BlockSpec((N_DEV, tr, sw), lambda i: (0, i, 0))],
        out_specs=pl.BlockSpec((tr, geo.pw), lambda i: (i, 0)),
        out_shape=jax.ShapeDtypeStruct((d, geo.pw), gathered.dtype), compiler_params=_cparams(("parallel",)))(gathered)


def _w_in_split(geo, g_padded, sw):
    d = g_padded.shape[0]
    runs = geo.w_in_runs(sw)
    tr = _pick(d, (128,))

    def body(x_ref, o_ref):
        for j, s0, n, d0 in runs:
            o_ref[j, :, s0:s0 + n] = x_ref[:, d0:d0 + n]

    return pl.pallas_call(
        body, name="w_in_split", grid=(d // tr,), in_specs=[pl.BlockSpec((tr, geo.pw), lambda i: (i, 0))],
        out_specs=pl.BlockSpec((N_DEV, tr, sw), lambda i: (0, i, 0)),
        out_shape=jax.ShapeDtypeStruct((N_DEV, d, sw), g_padded.dtype),
        compiler_params=_cparams(("parallel",)))(g_padded)


def _conv_cols(geo, cbw):
    nx = SSM_D_INNER // cbw
    x0, b0 = geo.col["xs"][0] // cbw, geo.col["bm"][0] // cbw
    assert geo.col["cm"][0] == geo.col["bm"][0] + geo.gn
    return lambda j: jnp.where(j < nx, x0 + j, b0 + j - nx)


def _conv_pre(x, w_ref, b_ref):
    acc = b_ref[...] + x * w_ref[SSM_CONV - 1:SSM_CONV, :]
    for k in range(SSM_CONV - 1):
        acc = acc + pltpu.roll(x, SSM_CONV - 1 - k, axis=0) * w_ref[k:k + 1, :]
    return acc


def _conv_fwd(geo, proj, conv_w, conv_b):
    cbw = 256
    colmap = _conv_cols(geo, cbw)
    lp, pad = geo.lp, geo.pad

    def body(x_ref, w_ref, b_ref, o_ref):
        valid = (lax.broadcasted_iota(jnp.int32, (lp, 1), 0) >= pad).astype(F32)
        o_ref[...] = _silu(_conv_pre(x_ref[...], w_ref, b_ref)) * valid

    return pl.pallas_call(
        body, name="conv_fwd", grid=(geo.bsz, geo.cd // cbw),
        in_specs=[pl.BlockSpec((lp, cbw), lambda b, j: (b, colmap(j))),
                  pl.BlockSpec((SSM_CONV, cbw), lambda b, j: (0, j)), pl.BlockSpec((1, cbw), lambda b, j: (0, j))],
        out_specs=pl.BlockSpec((lp, cbw), lambda b, j: (b, j)),
        out_shape=jax.ShapeDtypeStruct((geo.nrows, geo.cd), F32),
        compiler_params=_cparams(("parallel", "parallel")))(proj, conv_w, conv_b)


def _conv_bwd(geo, proj, conv_w, conv_b, dxc):
    cbw = 256
    colmap = _conv_cols(geo, cbw)
    lp, pad = geo.lp, geo.pad

    def body(x_ref, w_ref, b_ref, dy_ref, dx_ref, gw_ref, gb_ref):
        b = pl.program_id(1)
        valid = (lax.broadcasted_iota(jnp.int32, (lp, 1), 0) >= pad).astype(F32)
        x = x_ref[...]
        pre = _conv_pre(x, w_ref, b_ref)
        sig = _sigmoid(pre)
        dpre = dy_ref[...] * (sig * (1.0 + pre * (1.0 - sig))) * valid
        dx = dpre * w_ref[SSM_CONV - 1:SSM_CONV, :]
        gws = [jnp.sum(dpre * x, axis=0, keepdims=True)]
        for k in range(SSM_CONV - 2, -1, -1):
            s = SSM_CONV - 1 - k
            dx = dx + pltpu.roll(dpre, lp - s, axis=0) * w_ref[k:k + 1, :]
            gws.insert(0, jnp.sum(dpre * pltpu.roll(x, s, axis=0), axis=0, keepdims=True))
        dx_ref[...] = (dx * valid).astype(dx_ref.dtype)

        @pl.when(b == 0)
        def _():
            gw_ref[...] = jnp.zeros_like(gw_ref)
            gb_ref[...] = jnp.zeros_like(gb_ref)

        for k in range(SSM_CONV):
            gw_ref[k:k + 1, :] += gws[k]
        gb_ref[...] += jnp.sum(dpre, axis=0, keepdims=True)

    return pl.pallas_call(
        body, name="conv_bwd", grid=(geo.cd // cbw, geo.bsz),
        in_specs=[pl.BlockSpec((lp, cbw), lambda j, b: (b, colmap(j))),
                  pl.BlockSpec((SSM_CONV, cbw), lambda j, b: (0, j)), pl.BlockSpec((1, cbw), lambda j, b: (0, j)),
                  pl.BlockSpec((lp, cbw), lambda j, b: (b, j))],
        out_specs=[pl.BlockSpec((lp, cbw), lambda j, b: (b, j)), pl.BlockSpec((SSM_CONV, cbw), lambda j, b: (0, j)),
                   pl.BlockSpec((1, cbw), lambda j, b: (0, j))],
        out_shape=[jax.ShapeDtypeStruct((geo.nrows, geo.cd), MXU_DTYPE),
                   jax.ShapeDtypeStruct((SSM_CONV, geo.cd), F32), jax.ShapeDtypeStruct((1, geo.cd), F32)],
        compiler_params=_cparams(("parallel", "arbitrary")))(proj, conv_w, conv_b, dxc)


def _tri(q):
    r = lax.broadcasted_iota(jnp.int32, (q, q), 0)
    c = lax.broadcasted_iota(jnp.int32, (q, q), 1)
    return r >= c


def _ssd_pre(dtr, dtb, alog, valid):
    dt = _softplus(dtr + dtb) * valid
    adt = dt * (-jnp.exp(alog))
    a_cs = _dot(_tri(SSM_CHUNK).astype(F32), adt, 1, 0, precision=lax.Precision.HIGHEST)
    return dt, a_cs


def _ssd_specs(geo, rev):
    nc, q = geo.nc, SSM_CHUNK
    ci = (lambda c: nc - 1 - c) if rev else (lambda c: c)
    nxb = SSM_D_INNER // geo.gn
    return [pl.BlockSpec((q, SSM_D_INNER), lambda b, c: (b * nc + ci(c), 0)),
            pl.BlockSpec((q, geo.gn), lambda b, c: (b * nc + ci(c), nxb)),
            pl.BlockSpec((q, geo.gn), lambda b, c: (b * nc + ci(c), nxb + 1)),
            pl.BlockSpec((q, LANE), lambda b, c: (b * nc + ci(c), geo.cb("dt"))),
            pl.BlockSpec((1, LANE), lambda b, c: (0, 0)), pl.BlockSpec((1, LANE), lambda b, c: (0, 0))], ci


def _expand_heads(cols, nh):
    per = LANE // SSM_HEAD_DIM
    lane = lax.broadcasted_iota(jnp.int32, (1, LANE), 1)
    blocks = []
    for j in range(nh // per):
        blk = jnp.broadcast_to(cols[:, j * per:j * per + 1], (cols.shape[0], LANE))
        for k in range(1, per):
            blk = jnp.where(lane >= k * SSM_HEAD_DIM, cols[:, j * per + k:j * per + k + 1], blk)
        blocks.append(blk)
    return jnp.concatenate(blocks, axis=1)


def _head_maps(geo):
    e = (jnp.arange(SSM_D_INNER)[None, :] // SSM_HEAD_DIM == jnp.arange(LANE)[:, None]).astype(F32)
    return e, e.T


def _ssd_fwd_g(geo, xc, proj, dt_bias, a_log):
    q, p, n, e = SSM_CHUNK, SSM_HEAD_DIM, SSM_STATE, geo.nh // SSM_GROUPS
    nc, pad, gw = geo.nc, geo.pad, SSM_D_INNER // SSM_GROUPS
    in_specs, _ = _ssd_specs(geo, False)
    _, e_t = _head_maps(geo)
    in_specs.append(pl.BlockSpec((SSM_D_INNER, LANE), lambda b, c: (0, 0)))

    def body(xs_ref, b_ref, c_ref, dtr_ref, dtb_ref, alog_ref, et_ref, y_ref, sp_ref, state, xdt_s):
        c = pl.program_id(1)

        @pl.when(c == 0)
        def _():
            state[...] = jnp.zeros_like(state)

        sp_ref[...] = state[...]
        inert = (c + 1) * q <= pad

        @pl.when(inert)
        def _():
            y_ref[...] = jnp.zeros_like(y_ref)

        @pl.when(jnp.logical_not(inert))
        def _():
            valid = (c * q + lax.broadcasted_iota(jnp.int32, (q, 1), 0) >= pad).astype(F32)
            dt, a_cs = _ssd_pre(dtr_ref[...], dtb_ref[...], alog_ref[...], valid)
            a_cst = a_cs.T
            dt_x, a_x = _expand_heads(dt, geo.nh), _expand_heads(a_cs, geo.nh)
            e_last = jnp.exp(a_cs[q - 1:q, :])
            tri = _tri(q)
            for g in range(SSM_GROUPS):
                gs = slice(g * gw, (g + 1) * gw)
                bg, cg = b_ref[:, g * n:(g + 1) * n], c_ref[:, g * n:(g + 1) * n]
                a_g = a_x[:, gs]
                xdt_g = xs_ref[:, gs] * dt_x[:, gs]
                xdt_s[:, gs] = xdt_g
                s_g = state[gs, :]
                y_ref[:, gs] = _mxdot(cg, s_g, 1, 1) * jnp.exp(a_g)
                e_last_rows = jnp.sum(et_ref[gs, :] * e_last, axis=1, keepdims=True)
                state[gs, :] = s_g * e_last_rows + _mxdot(xdt_g * jnp.exp(a_g[q - 1:q, :] - a_g), bg, 0, 0)
                cb = _mxdot(cg, bg, 1, 1)
                for hh in range(e):
                    h = g * e + hh
                    hs = slice(h * p, (h + 1) * p)
                    ldec = jnp.exp(jnp.where(tri, a_cs[:, h:h + 1] - a_cst[h:h + 1, :], -jnp.inf))
                    y_ref[:, hs] += _mxdot(cb * ldec, xdt_s[:, hs], 1, 0)

    return pl.pallas_call(
        body, name="ssd_fwd", grid=(geo.bsz, nc), in_specs=in_specs,
        out_specs=[pl.BlockSpec((q, SSM_D_INNER), lambda b, c: (b * nc + c, 0)),
                   pl.BlockSpec((SSM_D_INNER, n), lambda b, c: (b * nc + c, 0))],
        out_shape=[jax.ShapeDtypeStruct((geo.nrows, SSM_D_INNER), F32),
                   jax.ShapeDtypeStruct((geo.bsz * nc * SSM_D_INNER, n), F32)],
        scratch_shapes=[pltpu.VMEM((SSM_D_INNER, n), F32), pltpu.VMEM((q, SSM_D_INNER), F32)],
        compiler_params=_cparams(("parallel", "arbitrary")))(xc, xc, xc, proj, dt_bias, a_log, e_t)


def _ssd_bwd_g(geo, xc, proj, dt_bias, a_log, s_prev_all, dy, dxs_skip):
    q, p, n, e = SSM_CHUNK, SSM_HEAD_DIM, SSM_STATE, geo.nh // SSM_GROUPS
    nc, pad, di, gn, gw = geo.nc, geo.pad, SSM_D_INNER, geo.gn, SSM_D_INNER // SSM_GROUPS
    in_specs, ci = _ssd_specs(geo, True)
    row_spec = pl.BlockSpec((q, di), lambda b, c: (b * nc + ci(c), 0))
    e_map, e_t = _head_maps(geo)
    in_specs += [pl.BlockSpec((di, n), lambda b, c: (b * nc + ci(c), 0)), row_spec, row_spec,
                 pl.BlockSpec((LANE, di), lambda b, c: (0, 0)), pl.BlockSpec((di, LANE), lambda b, c: (0, 0))]

    def body(xs_ref, b_ref, c_ref, dtr_ref, dtb_ref, alog_ref, sp_ref, dy_ref, dsk_ref, e_ref, et_ref,
             dxc_ref, ddt_ref, gdtb_ref, galog_ref, dstate, xdt_s, dxdt_s):
        step = pl.program_id(1)
        first = jnp.logical_and(pl.program_id(0) == 0, step == 0)
        c = nc - 1 - step

        @pl.when(step == 0)
        def _():
            dstate[...] = jnp.zeros_like(dstate)

        @pl.when(first)
        def _():
            gdtb_ref[...] = jnp.zeros_like(gdtb_ref)
            galog_ref[...] = jnp.zeros_like(galog_ref)

        inert = (c + 1) * q <= pad

        @pl.when(inert)
        def _():
            dxc_ref[...] = jnp.zeros_like(dxc_ref)
            ddt_ref[...] = jnp.zeros_like(ddt_ref)

        @pl.when(jnp.logical_not(inert))
        def _():
            valid = (c * q + lax.broadcasted_iota(jnp.int32, (q, 1), 0) >= pad).astype(F32)
            dtr, dtb, alog = dtr_ref[...], dtb_ref[...], alog_ref[...]
            dt, a_cs = _ssd_pre(dtr, dtb, alog, valid)
            a_cst = a_cs.T
            dt_x, a_x = _expand_heads(dt, geo.nh), _expand_heads(a_cs, geo.nh)
            e_last = jnp.exp(a_cs[q - 1:q, :])
            tri = _tri(q)
            lane = lax.broadcasted_iota(jnp.int32, (1, LANE), 1)
            sub = lax.broadcasted_iota(jnp.int32, (LANE, 1), 0)
            d_dt = jnp.zeros((q, LANE), F32)
            d_acs = jnp.zeros((q, LANE), F32)
            d_acst = jnp.zeros((LANE, q), F32)
            d_last = jnp.zeros((1, LANE), F32)
            for g in range(SSM_GROUPS):
                gs = slice(g * gw, (g + 1) * gw)
                bg, cg = b_ref[:, g * n:(g + 1) * n], c_ref[:, g * n:(g + 1) * n]
                seg = lambda v: _mxdot(v, e_ref[:, gs], 1, 1)
                a_g, dt_g, x_g, dy_g = a_x[:, gs], dt_x[:, gs], xs_ref[:, gs], dy_ref[:, gs]
                e_col, dec = jnp.exp(a_g), jnp.exp(a_g[q - 1:q, :] - a_g)
                xdt_g = x_g * dt_g
                xdt_s[:, gs] = xdt_g
                s_g, ds_g, et_g = sp_ref[gs, :], dstate[gs, :], et_ref[gs, :]
                cs = _mxdot(cg, s_g, 1, 1)
                d_cs = dy_g * e_col
                d_acs = d_acs + seg(d_cs * cs)
                d_cg = _mxdot(d_cs, s_g, 1, 0)
                dstate[gs, :] = _mxdot(d_cs, cg, 0, 0) + ds_g * jnp.sum(et_g * e_last, axis=1, keepdims=True)
                d_last = d_last + jnp.sum(jnp.sum(ds_g * s_g, axis=1, keepdims=True) * et_g, axis=0,
                                          keepdims=True) * e_last
                gmat = _mxdot(bg, ds_g, 1, 1)
                xd = xdt_g * dec
                d_bg = _mxdot(xd, ds_g, 1, 0)
                d_dec = seg(xd * gmat)
                d_acs = d_acs - d_dec
                d_last = d_last + jnp.sum(d_dec, axis=0, keepdims=True)
                dxdt_s[:, gs] = dec * gmat
                cb = _mxdot(cg, bg, 1, 1)
                d_cb = jnp.zeros((q, q), F32)
                for hh in range(e):
                    h = g * e + hh
                    hs = slice(h * p, (h + 1) * p)
                    ldec = jnp.exp(jnp.where(tri, a_cs[:, h:h + 1] - a_cst[h:h + 1, :], -jnp.inf))
                    dyh = dy_ref[:, hs]
                    d_m = _mxdot(dyh, xdt_s[:, hs], 1, 1)
                    dxdt_s[:, hs] += _mxdot(cb * ldec, dyh, 0, 0)
                    d_cb = d_cb + d_m * ldec
                    d_diff = d_m * cb * ldec
                    d_acs = d_acs + jnp.sum(d_diff, axis=1, keepdims=True) * (lane == h).astype(F32)
                    d_acst = d_acst - (sub == h).astype(F32) * jnp.sum(d_diff, axis=0, keepdims=True)
                d_xdt = dxdt_s[:, gs]
                dxc_ref[:, gs] = d_xdt * dt_g + dsk_ref[:, gs]
                d_dt = d_dt + seg(d_xdt * x_g)
                dxc_ref[:, di + g * n:di + (g + 1) * n] = d_bg + _mxdot(d_cb, cg, 0, 0)
                dxc_ref[:, di + gn + g * n:di + gn + (g + 1) * n] = d_cg + _mxdot(d_cb, bg, 1, 0)
            is_last = (lax.broadcasted_iota(jnp.int32, (q, 1), 0) == q - 1).astype(F32)
            d_acs = d_acs + d_acst.T + is_last * d_last
            d_adt = _dot(_tri(q).astype(F32), d_acs, 0, 0, precision=lax.Precision.HIGHEST)
            a = -jnp.exp(alog)
            d_dt = d_dt + d_adt * a
            d_dtr = d_dt * valid * _sigmoid(dtr + dtb)
            ddt_ref[...] = d_dtr.astype(ddt_ref.dtype)
            gdtb_ref[...] += jnp.sum(d_dtr, axis=0, keepdims=True)
            galog_ref[...] += jnp.sum(d_adt * dt, axis=0, keepdims=True) * a

    vec = pl.BlockSpec((1, LANE), lambda b, c: (0, 0))
    return pl.pallas_call(
        body, name="ssd_bwd", grid=(geo.bsz, nc), in_specs=in_specs,
        out_specs=[pl.BlockSpec((q, geo.cd), lambda b, c: (b * nc + ci(c), 0)),
                   pl.BlockSpec((q, LANE), lambda b, c: (b * nc + ci(c), 0)), vec, vec],
        out_shape=[jax.ShapeDtypeStruct((geo.nrows, geo.cd), F32), jax.ShapeDtypeStruct((geo.nrows, LANE), MXU_DTYPE),
                   jax.ShapeDtypeStruct((1, LANE), F32), jax.ShapeDtypeStruct((1, LANE), F32)],
        scratch_shapes=[pltpu.VMEM((di, n), F32), pltpu.VMEM((q, di), F32), pltpu.VMEM((q, di), F32)],
        compiler_params=_cparams(("arbitrary", "arbitrary")))(
            xc, xc, xc, proj, dt_bias, a_log, s_prev_all, dy, dxs_skip, e_map, e_t)


BIAS_LANE = MLA_ROPE // 2
KEY_OFF = -1e30
ATT_SCALE = (MLA_NOPE + MLA_ROPE) ** -0.5


def _row_t(col):
    return jnp.broadcast_to(col, (col.shape[0], LANE)).T[:8]


def _attn_fwd2(geo, qn, qp, kn, kp, v):
    t, lp = ATT_BLK, geo.lp
    nb = lp // t

    def body(qn_ref, qp_ref, kn_ref, kp_ref, v_ref, o_ref, lse_ref, k_ref):
        qi = pl.program_id(2)

        @pl.when(qi == 0)
        def _():
            k_ref[:, :LANE] = kn_ref[...]
            k_ref[:, LANE:] = kp_ref[...]

        q = jnp.concatenate([qn_ref[...], qp_ref[...]], axis=1)

        def blk(kj, ntile, carry, diag):
            m, l, acc = carry
            ks = pl.ds(pl.multiple_of(kj * t, t), ntile * t)
            s = _mxdot(q, k_ref[ks, :], 1, 1) * ATT_SCALE
            if diag:
                s = jnp.where(_tri(t), s, -jnp.inf)
            m_new = jnp.maximum(m, jnp.max(s, axis=1, keepdims=True))
            pr = jnp.exp(s - m_new)
            alpha = jnp.exp(m - m_new)
            return m_new, alpha * l + jnp.sum(pr, axis=1, keepdims=True), alpha * acc + _mxdot(pr, v_ref[ks, :], 1, 0)

        init = (jnp.full((t, 1), 2.0 * KEY_OFF, F32), jnp.zeros((t, 1), F32), jnp.zeros((t, LANE), F32))
        pairs = qi // 2
        carry = lax.fori_loop(0, pairs, lambda j, c: blk(2 * j, 2, c, False), init)
        carry = lax.fori_loop(2 * pairs, qi, lambda kj, c: blk(kj, 1, c, False), carry)
        m, l, acc = blk(qi, 1, carry, True)
        o_ref[...] = acc / l
        lse_ref[0, 0, 0] = _row_t(m + jnp.log(l))

    tile = pl.BlockSpec((t, LANE), lambda b, h, i: (b * nb + i, h))
    seq = pl.BlockSpec((lp, LANE), lambda b, h, i: (b, h))
    return pl.pallas_call(
        body, name="attn_fwd", grid=(geo.bsz, MLA_HEADS, nb),
        in_specs=[tile, tile, seq, pl.BlockSpec((lp, LANE), lambda b, h, i: (b, 0)), seq],
        out_specs=[tile, pl.BlockSpec((1, 1, 1, 8, t), lambda b, h, i: (b, h, i, 0, 0))],
        out_shape=[jax.ShapeDtypeStruct((geo.nrows, geo.hq), F32),
                   jax.ShapeDtypeStruct((geo.bsz, MLA_HEADS, nb, 8, t), F32)],
        scratch_shapes=[pltpu.VMEM((lp, 2 * LANE), MXU_DTYPE)],
        compiler_params=_cparams(("parallel", "parallel", "arbitrary")))(qn, qp, kn, kp, v)


def _attn_bwd2(geo, qn, qp, kn, kp, v, d_o, o, lse):
    t, lp = ATT_BLK, geo.lp
    nb = lp // t

    def body(qn_ref, qp_ref, kn_ref, kp_ref, v_ref, do_ref, o_ref, lse_ref,
             dqn_ref, dqp_ref, dkn_ref, dkp_ref, dv_ref, q_ref, dl_s):
        kj = pl.program_id(2)

        @pl.when(kj == 0)
        def _():
            q_ref[:, :LANE] = qn_ref[...]
            q_ref[:, LANE:] = qp_ref[...]
            dqn_ref[...] = jnp.zeros_like(dqn_ref)
            dqp_ref[...] = jnp.zeros_like(dqp_ref)
            for i in range(nb):
                rows = slice(i * t, (i + 1) * t)
                dl_s[i] = _row_t(jnp.sum(do_ref[rows, :] * o_ref[rows, :], axis=1, keepdims=True))

        k, vv = jnp.concatenate([kn_ref[...], kp_ref[...]], axis=1), v_ref[...]

        def row(ref, qi, ntile):
            return jnp.concatenate([ref[qi + i][:1, :] for i in range(ntile)], axis=1)

        def blk(qi, ntile, carry, diag):
            dk, dv = carry
            qs = pl.ds(pl.multiple_of(qi * t, t), ntile * t)
            q, d_o_blk = q_ref[qs, :], do_ref[qs, :]
            st = _mxdot(k, q, 1, 1) * ATT_SCALE
            if diag:
                keys = lax.broadcasted_iota(jnp.int32, (t, t), 0)
                st = jnp.where(keys <= lax.broadcasted_iota(jnp.int32, (t, t), 1), st, -jnp.inf)
            pt = jnp.exp(st - row(lse_ref.at[0, 0], qi, ntile))
            dst = pt * (_mxdot(vv, d_o_blk, 1, 1) - row(dl_s, qi, ntile)) * ATT_SCALE
            dq = _mxdot(dst, k, 0, 0)
            dqn_ref[qs, :] += dq[:, :LANE]
            dqp_ref[qs, :] += dq[:, LANE:]
            return dk + _mxdot(dst, q, 1, 0), dv + _mxdot(pt, d_o_blk, 1, 0)

        carry = blk(kj, 1, (jnp.zeros((t, 2 * LANE), F32), jnp.zeros((t, LANE), F32)), True)
        pairs = (nb - 1 - kj) // 2
        carry = lax.fori_loop(0, pairs, lambda j, c: blk(kj + 1 + 2 * j, 2, c, False), carry)
        dk, dv = lax.fori_loop(kj + 1 + 2 * pairs, nb, lambda qi, c: blk(qi, 1, c, False), carry)
        dkn_ref[...] = dk[:, :LANE].astype(dkn_ref.dtype)
        dkp_ref[...] = dk[:, LANE:]
        dv_ref[...] = dv.astype(dv_ref.dtype)

    seq = pl.BlockSpec((lp, LANE), lambda b, h, j: (b, h))
    tile = pl.BlockSpec((t, LANE), lambda b, h, j: (b * nb + j, h))
    return pl.pallas_call(
        body, name="attn_bwd", grid=(geo.bsz, MLA_HEADS, nb),
        in_specs=[seq, seq, tile, pl.BlockSpec((t, LANE), lambda b, h, j: (b * nb + j, 0)), tile, seq, seq,
                  pl.BlockSpec((1, 1, nb, 8, t), lambda b, h, j: (b, h, 0, 0, 0))],
        out_specs=[seq, seq, tile, tile, tile],
        out_shape=[jax.ShapeDtypeStruct((geo.nrows, geo.hq), F32), jax.ShapeDtypeStruct((geo.nrows, geo.hq), F32),
                   jax.ShapeDtypeStruct((geo.nrows, geo.hq), MXU_DTYPE), jax.ShapeDtypeStruct((geo.nrows, geo.hq), F32),
                   jax.ShapeDtypeStruct((geo.nrows, geo.hq), MXU_DTYPE)],
        scratch_shapes=[pltpu.VMEM((lp, 2 * LANE), MXU_DTYPE), pltpu.VMEM((nb, 8, t), F32)],
        compiler_params=_cparams(("parallel", "parallel", "arbitrary")))(qn, qp, kn, kp, v, d_o, o, lse)


def _rope(x, cos, sin):
    return x * cos + pltpu.roll(x, LANE // 2, axis=1) * sin


def _rope_t(dx, cos, sin):
    return dx * cos + pltpu.roll(dx * sin, LANE // 2, axis=1)


def _layer_fwd(geo, h, w, tab, late=None):
    nr, tr, trw = geo.nrows, geo.tr, geo.tr_wide
    tb = geo.lp // tr
    rw = functools.partial(_rowwise, nrows=nr)
    s = {"h": h}
    (s["u"],) = rw("rms_mix", lambda x, g: (_rms(x, g),), tr=tr, rows=[(h, D_MODEL, 0)],
                   vecs=[(w["norm_mix_w"], D_MODEL, 0)], outs=[(D_MODEL, D_MODEL, MXU_DTYPE)])
    proj = s["proj"] = _mm("mm_in", s["u"], w["w_in_p"])
    xc = s["xc"] = _conv_fwd(geo, proj, w["conv_w"], w["conv_b"])
    s["y_ssd"], s["s_prev"] = _ssd_fwd_g(geo, xc, proj, w["dt_bias"], w["a_log"])
    gw = SSM_D_INNER // SSM_GROUPS

    def gate_norm(y, x, z, dsk, nw):
        return (_rms((y + x * dsk) * _silu(z), nw),)

    (s["y_ssm"],) = rw("ssm_gate_norm", gate_norm, tr=tr, ncb=SSM_GROUPS,
                       rows=[(s["y_ssd"], gw, 0), (xc, gw, 0), (proj, gw, geo.col["z"][0] // gw)],
                       vecs=[(w["d_skip_full"], gw, 0), (w["ssm_norm_w"], gw, 0)], outs=[(SSM_D_INNER, gw, MXU_DTYPE)])
    if late is not None:
        w = {**w, **late(s["y_ssm"])}
    (s["cq_n"],) = rw("rms_q", lambda x, g: (_rms(x, g),), tr=tr, rows=[(proj, MLA_Q_LORA, geo.cb("c_q"))],
                      vecs=[(w["q_norm_w"], MLA_Q_LORA, 0)], outs=[(MLA_Q_LORA, MLA_Q_LORA, MXU_DTYPE)])
    (s["ckv_n"],) = rw("rms_kv", lambda x, g: (_rms(x, g),), tr=tr, rows=[(proj, MLA_KV_LORA, geo.cb("c_kv"))],
                       vecs=[(w["kv_norm_w"], MLA_KV_LORA, 0)], outs=[(MLA_KV_LORA, MLA_KV_LORA, MXU_DTYPE)])
    s["qn"] = _mm("mm_qn", s["cq_n"], w["w_qn"], out_dtype=MXU_DTYPE)
    qp_raw = _mm("mm_qp", s["cq_n"], w["w_qp"])
    s["kn"] = _mm("mm_kn", s["ckv_n"], w["w_k"], out_dtype=MXU_DTYPE)
    s["v"] = _mm("mm_v", s["ckv_n"], w["w_v"], out_dtype=MXU_DTYPE)
    bias_lane = lambda: lax.broadcasted_iota(jnp.int32, (1, LANE), 1) == BIAS_LANE
    rope_tabs = [(tab["cos"], LANE, 0), (tab["sin"], LANE, 0)]
    (s["qp"],) = rw("rope_q", lambda xp, c, sn: (jnp.where(bias_lane(), 1.0, _rope(xp, c, sn)),), tr=tr, ncb=MLA_HEADS,
                    rows=[(qp_raw, LANE, 0)], tabs=rope_tabs, outs=[(geo.hq, LANE, MXU_DTYPE)], tab_blocks=tb)
    (s["kp"],) = rw("rope_k", lambda xp, c, sn, valid: (jnp.where(bias_lane(), KEY_OFF * (1.0 - valid), _rope(xp, c, sn)),),
                    tr=tr, rows=[(proj, LANE, geo.cb("k_rope"))], tabs=rope_tabs + [(tab["valid"], 1, 0)],
                    outs=[(LANE, LANE, MXU_DTYPE)], tab_blocks=tb)
    s["o"], s["lse"] = _attn_fwd2(geo, s["qn"], s["qp"], s["kn"], s["kp"], s["v"])
    s["ys_p"] = _mm("mm_bs", s["y_ssm"], w["w_branch_ssm"])
    s["ym_p"] = _mm("mm_bm", s["o"], w["w_branch_mla"])

    def gate(gs, gm, ys, ym):
        return (_sigmoid(gs) * ys + _sigmoid(gm) * ym,)

    (s["mixed"],) = rw("gate", gate, tr=tr, rows=[(proj, D_MODEL, geo.cb("g_ssm")), (proj, D_MODEL, geo.cb("g_mla")),
                                                  (s["ys_p"], D_MODEL, 0), (s["ym_p"], D_MODEL, 0)],
                       outs=[(D_MODEL, D_MODEL, MXU_DTYPE)])
    s["h2"] = _mm("mm_out", s["mixed"], w["w_out"], add=h)
    (s["vn"],) = rw("rms_mlp", lambda x, g: (_rms(x, g),), tr=tr, rows=[(s["h2"], D_MODEL, 0)],
                    vecs=[(w["norm_mlp_w"], D_MODEL, 0)], outs=[(D_MODEL, D_MODEL, MXU_DTYPE)])
    s["up"], s["act"] = _mm("mm_up", s["vn"], w["w_mlp_up"],
                            epi=(lambda r: (r, jnp.square(jnp.maximum(r, 0.0))), (F32, MXU_DTYPE)))
    return _mm("mm_down", s["act"], w["w_mlp_down"], add=s["h2"]), s, w


def _layer_bwd(geo, dh3, s, w, tab, mid=None, tail=None):
    nr, tr, trw = geo.nrows, geo.tr, geo.tr_wide
    tb = geo.lp // tr
    rw = functools.partial(_rowwise, nrows=nr)
    g = {}
    proj = s["proj"]

    def rms_bwd(x, dy, res, gw):
        _, vjp = jax.vjp(_rms, x.astype(F32), gw)
        dx, dgw = vjp(dy.astype(F32))
        return dx + res, dgw

    def rms_bwd_nores(x, dy, gw):
        _, vjp = jax.vjp(_rms, x.astype(F32), gw)
        return vjp(dy.astype(F32))

    (dup,) = _mm("mm_down_t", dh3, w["w_mlp_down"], tb=True, add=s["up"],
                 epi=(lambda r, up: (r * 2.0 * jnp.maximum(up, 0.0),), (MXU_DTYPE,)))
    g["w_mlp_down"] = _mm("mm_down_g", s["act"], dh3, ta=True, out_dtype=MXU_DTYPE)
    g["w_mlp_up"] = _mm("mm_up_g", s["vn"], dup, ta=True, out_dtype=MXU_DTYPE)
    dvn = _mm("mm_up_t", dup, w["w_mlp_up"], tb=True)
    dh2, g["norm_mlp_w"] = rw("rms_mlp_bwd", rms_bwd, tr=tr,
                              rows=[(s["h2"], D_MODEL, 0), (dvn, D_MODEL, 0), (dh3, D_MODEL, 0)],
                              vecs=[(w["norm_mlp_w"], D_MODEL, 0)], outs=[(D_MODEL, D_MODEL, F32)],
                              reds=[(D_MODEL, D_MODEL)])
    dmixed = _mm("mm_out_t", dh2, w["w_out"], tb=True)
    g["w_out"] = _mm("mm_out_g", s["mixed"], dh2, ta=True, out_dtype=MXU_DTYPE)

    def gate_bwd(gs, gm, ys, ym, dm):
        f = lambda a, b, c, d: _sigmoid(a) * c + _sigmoid(b) * d
        _, vjp = jax.vjp(f, gs, gm, ys, ym)
        dgs, dgm, dys, dym = vjp(dm)
        return dys, dym, dgs, dgm

    dys_p, dym_p, dg_ssm, dg_mla = rw(
        "gate_bwd", gate_bwd, tr=tr,
        rows=[(proj, D_MODEL, geo.cb("g_ssm")), (proj, D_MODEL, geo.cb("g_mla")), (s["ys_p"], D_MODEL, 0),
              (s["ym_p"], D_MODEL, 0), (dmixed, D_MODEL, 0)], outs=[(D_MODEL, D_MODEL, MXU_DTYPE)] * 4)
    g["w_branch_ssm"] = _mm("mm_bs_g", s["y_ssm"], dys_p, ta=True, out_dtype=MXU_DTYPE)
    dy_ssm = _mm("mm_bs_t", dys_p, w["w_branch_ssm"], tb=True)
    g["w_branch_mla"] = _mm("mm_bm_g", s["o"], dym_p, ta=True, out_dtype=MXU_DTYPE)
    d_o = _mm("mm_bm_t", dym_p, w["w_branch_mla"], tb=True)
    dqn, dqp, dkn, dkp_h, dv = _attn_bwd2(geo, s["qn"], s["qp"], s["kn"], s["kp"], s["v"], d_o, s["o"], s["lse"])
    rope_tabs = [(tab["cos"], LANE, 0), (tab["sin"], LANE, 0)]
    (dqp_raw,) = rw("rope_q_bwd", lambda x, c, sn: (_rope_t(x, c, sn),), tr=tr, ncb=MLA_HEADS, rows=[(dqp, LANE, 0)],
                    tabs=rope_tabs, outs=[(geo.hq, LANE, MXU_DTYPE)], tab_blocks=tb)

    def rope_k_bwd(x, c, sn):
        tot = x[:, :LANE]
        for hd in range(1, MLA_HEADS):
            tot = tot + x[:, hd * LANE:(hd + 1) * LANE]
        return (_rope_t(tot, c, sn),)

    (dk_rope,) = rw("rope_k_bwd", rope_k_bwd, tr=tr, rows=[(dkp_h, geo.hq, 0)], tabs=rope_tabs,
                    outs=[(LANE, LANE, MXU_DTYPE)], tab_blocks=tb)
    g["w_qn"] = _mm("mm_qn_g", s["cq_n"], dqn, ta=True, out_dtype=MXU_DTYPE)
    g["w_qp"] = _mm("mm_qp_g", s["cq_n"], dqp_raw, ta=True, out_dtype=MXU_DTYPE)
    dcq_n = _mm("mm_qp_t", dqp_raw, w["w_qp"], tb=True, add=_mm("mm_qn_t", dqn, w["w_qn"], tb=True))
    g["w_k"] = _mm("mm_kn_g", s["ckv_n"], dkn, ta=True, out_dtype=MXU_DTYPE)
    g["w_v"] = _mm("mm_v_g", s["ckv_n"], dv, ta=True, out_dtype=MXU_DTYPE)
    dckv_n = _mm("mm_v_t", dv, w["w_v"], tb=True, add=_mm("mm_kn_t", dkn, w["w_k"], tb=True))
    dc_q, g["q_norm_w"] = rw("rms_q_bwd", rms_bwd_nores, tr=tr,
                             rows=[(proj, MLA_Q_LORA, geo.cb("c_q")), (dcq_n, MLA_Q_LORA, 0)],
                             vecs=[(w["q_norm_w"], MLA_Q_LORA, 0)], outs=[(MLA_Q_LORA, MLA_Q_LORA, MXU_DTYPE)],
                             reds=[(MLA_Q_LORA, MLA_Q_LORA)])
    dc_kv, g["kv_norm_w"] = rw("rms_kv_bwd", rms_bwd_nores, tr=tr,
                               rows=[(proj, MLA_KV_LORA, geo.cb("c_kv")), (dckv_n, MLA_KV_LORA, 0)],
                               vecs=[(w["kv_norm_w"], MLA_KV_LORA, 0)], outs=[(MLA_KV_LORA, MLA_KV_LORA, MXU_DTYPE)],
                               reds=[(MLA_KV_LORA, MLA_KV_LORA)])
    gw_ = SSM_D_INNER // SSM_GROUPS
    d_skip_full = w["d_skip_full"] if mid is None else w["d_skip_full"] + mid(g)[0, 0]

    def gate_norm_bwd(y, x, z, dy, dsk, nw):
        f = lambda y_, x_, z_, dsk_, nw_: _rms((y_ + x_ * dsk_) * _silu(z_), nw_)
        _, vjp = jax.vjp(f, y, x, z, dsk, nw)
        dy_, dx_, dz_, ddsk, dnw = vjp(dy)
        return dy_, dx_, dz_, ddsk, dnw

    dy_ssd, dxs_skip, dz, g["d_skip_full"], g["ssm_norm_w"] = rw(
        "ssm_gate_norm_bwd", gate_norm_bwd, tr=tr, ncb=SSM_GROUPS,
        rows=[(s["y_ssd"], gw_, 0), (s["xc"], gw_, 0), (proj, gw_, geo.col["z"][0] // gw_), (dy_ssm, gw_, 0)],
        vecs=[(d_skip_full, gw_, 0), (w["ssm_norm_w"], gw_, 0)],
        outs=[(SSM_D_INNER, gw_, F32), (SSM_D_INNER, gw_, F32), (SSM_D_INNER, gw_, MXU_DTYPE)],
        reds=[(SSM_D_INNER, gw_), (SSM_D_INNER, gw_)])
    dxc, ddt, g["dt_bias"], g["a_log"] = _ssd_bwd_g(geo, s["xc"], proj, w["dt_bias"], w["a_log"], s["s_prev"],
                                                   dy_ssd, dxs_skip)
    dxbc, g["conv_w"], g["conv_b"] = _conv_bwd(geo, proj, w["conv_w"], w["conv_b"], dxc)
    di, gn = SSM_D_INNER, geo.gn
    dproj = jnp.concatenate([dz, dxbc[:, :di], dg_ssm, dg_mla, dxbc[:, di:di + gn], dxbc[:, di + gn:], dc_q, dc_kv,
                             ddt, dk_rope], axis=-1)
    g["w_in_p"] = _mm("mm_in_g", s["u"], dproj, ta=True, out_dtype=MXU_DTYPE)
    du = _mm("mm_in_t", dproj, w["w_in_p"], tb=True, dep=None if tail is None else tail(g))
    dh, g["norm_mix_w"] = rw("rms_mix_bwd", rms_bwd, tr=tr,
                             rows=[(s["h"], D_MODEL, 0), (du, D_MODEL, 0), (dh2, D_MODEL, 0)],
                             vecs=[(w["norm_mix_w"], D_MODEL, 0)], outs=[(D_MODEL, D_MODEL, F32)],
                             reds=[(D_MODEL, D_MODEL)])
    return dh, g


def _loss_bwd(geo, h, fw, target, tab):
    tr = geo.tr

    def fn(x, tgt, gw, tok):
        def lossf(x_, gw_):
            err = jnp.square(_rms(x_, gw_) - tgt)
            return 0.5 * jnp.sum(tok * jnp.mean(err, axis=-1, keepdims=True), axis=0, keepdims=True)

        val, vjp = jax.vjp(lossf, x, gw)
        dx, dgw = vjp(jnp.ones((1, 1), F32))
        return dx, jnp.broadcast_to(val, (1, LANE)), dgw

    return _rowwise("loss", fn, nrows=geo.nrows, tr=tr, rows=[(h, D_MODEL, 0), (target, D_MODEL, 0)],
                    vecs=[(fw, D_MODEL, 0)], tabs=[(tab["token"], 1, 0)], outs=[(D_MODEL, D_MODEL, F32)],
                    reds=[(LANE, LANE), (D_MODEL, D_MODEL)], tab_blocks=geo.lp // tr)


def kernel(x, meta_tokens, norm_mix_w, w_in, conv_w, conv_b, dt_bias, a_log, d_skip, ssm_norm_w, q_norm_w, kv_norm_w, w_uq, w_ukv, w_branch_ssm, w_branch_mla, w_out, norm_mlp_w, w_mlp_up, w_mlp_down, final_norm_w, loss_target, m_meta_tokens, m_norm_mix_w, m_w_in, m_conv_w, m_conv_b, m_dt_bias, m_a_log, m_d_skip, m_ssm_norm_w, m_q_norm_w, m_kv_norm_w, m_w_uq, m_w_ukv, m_w_branch_ssm, m_w_branch_mla, m_w_out, m_norm_mlp_w, m_w_mlp_up, m_w_mlp_down, m_final_norm_w, v_meta_tokens, v_norm_mix_w, v_w_in, v_conv_w, v_conv_b, v_dt_bias, v_a_log, v_d_skip, v_ssm_norm_w, v_q_norm_w, v_kv_norm_w, v_w_uq, v_w_ukv, v_w_branch_ssm, v_w_branch_mla, v_w_out, v_norm_mlp_w, v_w_mlp_up, v_w_mlp_down, v_final_norm_w):
    args = dict(locals())
    wts = {n: args[n] for n in WEIGHTS}
    mom = {n: args["m_" + n] for n in WEIGHTS}
    var = {n: args["v_" + n] for n in WEIGHTS}
    bsz, seq, _ = x.shape
    depth = w_in.shape[0]
    geo = _Geo(bsz, seq)
    tab = _tables(geo)

    big_names = [n for n, _ in BIG]
    sh_names = big_names + [n for n, _ in SHARDED_F32]
    kinds = dict(BIG + SHARDED_F32)
    shard3 = lambda a: a.reshape((1,) + a.shape) if a.ndim == 2 else a
    wire = {n: (MXU_DTYPE if n in big_names else F32) for n in sh_names}
    cast = {n: shard3(wts[n]).astype(wire[n]) for n in sh_names}
    per_layer = [n for n in sh_names if n != "meta_tokens"]
    small_names = ["norm_mix_w", "conv_b", "dt_bias", "a_log", "d_skip", "ssm_norm_w", "q_norm_w", "kv_norm_w",
                   "norm_mlp_w"]

    def gather_items(pairs):
        ins, outs, items, forms = [], [], [], []
        for n, i in pairs:
            a, b = cast[n].shape[1:]
            shape, dst, form = _gather_plan(a, b, kinds[n])
            items.append((len(ins), len(outs), (lambda ref, p, i=i: ref.at[i]), dst))
            ins.append(cast[n])
            outs.append(jax.ShapeDtypeStruct(shape, wire[n]))
            forms.append(form)
        return ins, outs, items, forms

    def whole_weights(pairs, forms, got):
        by_layer = {}
        for (n, i), form, g in zip(pairs, forms, got):
            if n == "w_in":
                n, g = "w_in_p", _w_in_assemble(geo, g)
            elif form == "row":
                g = g.reshape(g.shape[0] * g.shape[1], g.shape[2])
            elif form == "stack":
                g = _unshard(g, "col")
            by_layer.setdefault(i, {})[n] = g
        return by_layer

    def prep(i, whole, token=None):
        wl = dict(whole)
        wl.update({n: wts[n][i] for n in small_names})
        if token is not None:
            wl["norm_mix_w"] = wl["norm_mix_w"] + token[0, 0]
        return _prep_layer(geo, wl)

    early = ("w_in", "conv_w")
    late_names = [n for n in per_layer if n not in early]
    pairs1 = [(n, i) for i in range(1, depth) for n in per_layer]
    groups = [[(n, 0) for n in early] + [("meta_tokens", 0)], [(n, 0) for n in late_names]] + ([pairs1] if pairs1 else [])
    started = {}

    def gather_start(gi, dep=None):
        ins, outs, items, forms = gather_items(groups[gi])
        sems, thru, landing, token = _exchange_start("gather_w%d_start" % gi, ins, outs, items, dep)
        started[gi] = (groups[gi], forms, sems, thru, landing, items)
        return token

    def gathered(gi, after):
        pairs, forms, sems, thru, landing, items = started[gi]
        return whole_weights(pairs, forms, _exchange_wait("gather_w%d_wait" % gi, sems, thru, landing, items, after))

    def late0(after):
        whole = gathered(1, after)[0]
        if pairs1:
            whole["q_norm_w"] = wts["q_norm_w"][0] + gather_start(2, whole["w_out"])[0, 0]
        return _prep_layer(geo, whole)

    token = gather_start(1, gather_start(0))
    whole0 = gathered(0, token)[0]
    meta_full = whole0.pop("meta_tokens")

    meta = jnp.broadcast_to(meta_full[None], (bsz, N_META, D_MODEL))
    h = jnp.concatenate([jnp.zeros((bsz, geo.pad, D_MODEL), F32), meta, x], axis=1).reshape(geo.nrows, D_MODEL)
    target = jnp.concatenate([jnp.zeros((bsz, geo.pad + N_META, D_MODEL), F32), loss_target], axis=1)
    target = target.reshape(geo.nrows, D_MODEL)
    layers, saved = [], []
    for i in range(depth):
        if i == 0:
            w, late = prep(0, whole0, token), late0
        else:
            if i == 1:
                whole1 = gathered(2, h)
            w, late = prep(i, whole1[i]), None
        h, s, w = _layer_fwd(geo, h, w, tab, late)
        layers.append(w)
        saved.append(s)
    dh, loss_part, g_final = _loss_bwd(geo, h, final_norm_w.reshape(1, -1), target, tab)

    def scatter_items(pairs):
        ins, outs, items = [], [], []
        for n, i in pairs:
            a, b = cast[n].shape[1:]
            arr = g_meta if n == "meta_tokens" else grads[i]["w_in_p" if n == "w_in" else n]
            if n == "w_in":
                arr, src = _w_in_split(geo, arr, b), _entry
            elif kinds[n] == "row":
                src = lambda ref, p, a=a: ref.at[pl.ds(pl.multiple_of(p * a, a), a)]
            elif b % LANE == 0:
                src = lambda ref, p, b=b: ref.at[:, pl.ds(pl.multiple_of(p * b, b), b)]
            else:
                arr, src = _shard(arr, "col"), _entry
            items.append((len(ins), len(outs), src, _entry))
            ins.append(arr.astype(wire[n]))
            outs.append(jax.ShapeDtypeStruct((N_DEV, a, b), wire[n]))
        return ins, outs, items

    grads = [None] * depth
    landed, pending, res = {}, {}, {}

    def scatter_start(name, pairs):
        ins, outs, items = scatter_items(pairs)
        sems, thru, landing, token = _exchange_start(name + "_start", ins, outs, items)
        pending[name] = (pairs, sems, thru, landing, items)
        return token

    def scatter_wait(name, after):
        pairs, sems, thru, landing, items = pending[name]
        landed.update(zip(pairs, _exchange_wait(name + "_wait", sems, thru, landing, items, after)))

    def adam(n):
        parts = [landed[(n, i)] for i in range(cast[n].shape[0])]
        r = _adamw_nat("adamw_" + n, parts, shard3(wts[n]), shard3(mom[n]), shard3(var[n]))
        res[n] = [a.reshape(wts[n].shape) for a in r]

    def mid0(g):
        grads[0] = _unprep_grads(geo, g)
        return scatter_start("scatter_gb0", [(n, 0) for n in late_names])

    def tail0(g):
        grads[0] = _unprep_grads(geo, g)
        return scatter_start("scatter_ga0", [(n, 0) for n in early])

    for i in reversed(range(depth)):
        dh, gl = _layer_bwd(geo, dh, saved[i], layers[i], tab, *((mid0, tail0) if i == 0 else ()))
        grads[i] = _unprep_grads(geo, gl)
        if i == 1:
            dh = dh + scatter_start("scatter_g1", pairs1)[0, 0]
    dh = dh.reshape(bsz, geo.lp, D_MODEL)
    grad_x = dh[:, geo.pad + N_META:]
    g_meta = jnp.sum(dh[:, geo.pad:geo.pad + N_META], axis=0)
    if pairs1:
        scatter_wait("scatter_g1", g_meta)
    scatter_wait("scatter_gb0", g_meta)
    for n in late_names:
        adam(n)
    g_small = {n: jnp.stack([grads[i][n] for i in range(depth)]) for n in SMALL if n != "final_norm_w"}
    g_small["final_norm_w"] = g_final.reshape(-1)
    zero = jnp.zeros((1,), F32)
    pk = lambda d, last: _pack([d[n] for n in SMALL] + [last], F32, row_mult=8)
    packed = pk(g_small, loss_part[0, :1])
    ins, outs, items = scatter_items([("meta_tokens", 0)])
    parts, landed[("meta_tokens", 0)] = _exchange(
        "gather_g", [packed] + ins, [jax.ShapeDtypeStruct((N_DEV,) + packed.shape, F32)] + outs,
        [(0, 0, _whole, _entry)] + [(1, 1, items[0][2], items[0][3])])
    adam("meta_tokens")
    scatter_wait("scatter_ga0", res["meta_tokens"][1])
    for n in early:
        adam(n)
    res_sm = _adamw("adamw_small", parts, pk(wts, zero), pk(mom, zero), pk(var, zero))
    res_sm = [_unpack(r, [wts[n].shape for n in SMALL] + [(1,)]) for r in res_sm]
    loss = res_sm[0][-1][0]

    out = [loss, grad_x]
    for k in range(4):
        named = {n: res[n][k] for n in sh_names}
        named.update(zip(SMALL, res_sm[k]))
        out += [named[n] for n in WEIGHTS]
    return tuple(out)
```

```python
import functools

import numpy as np
import jax
import jax.numpy as jnp
from jax import lax
from jax.experimental import pallas as pl
from jax.experimental.pallas import tpu as pltpu

F32 = jnp.float32
MXU_DTYPE = jnp.bfloat16

D_MODEL = 1024
N_META = 16
EPS = 1e-6
SSM_D_INNER = 2048
SSM_HEAD_DIM = 64
SSM_GROUPS = 4
SSM_STATE = 128
SSM_CONV = 4
SSM_CHUNK = 128
MLA_HEADS = 8
MLA_Q_LORA = 512
MLA_KV_LORA = 256
MLA_NOPE = 128
MLA_ROPE = 64
MLA_V = 128
ROPE_THETA = 10000.0
D_FF = 4096
ADAM_LR = 0.001
ADAM_B1 = 0.9
ADAM_B2 = 0.999
ADAM_EPS = 1e-08
ADAM_WD = 0.01
ADAM_STEP = 10

N_DEV = 8
ATT_BLK = 256
LANE = 128
PACK_W = 1024
VMEM_LIMIT = 56 * 1024 * 1024
MESH_ID = pl.DeviceIdType.MESH

BIG = (("w_in", "col"), ("w_uq", "col"), ("w_ukv", "col"), ("w_branch_ssm", "row"), ("w_branch_mla", "row"),
       ("w_out", "row"), ("w_mlp_up", "col"), ("w_mlp_down", "row"))
SHARDED_F32 = (("conv_w", "col"), ("meta_tokens", "col"))
SMALL = ("norm_mix_w", "conv_b", "dt_bias", "a_log", "d_skip", "ssm_norm_w", "q_norm_w", "kv_norm_w",
         "norm_mlp_w", "final_norm_w")
WEIGHTS = ("meta_tokens", "norm_mix_w", "w_in", "conv_w", "conv_b", "dt_bias", "a_log", "d_skip", "ssm_norm_w",
           "q_norm_w", "kv_norm_w", "w_uq", "w_ukv", "w_branch_ssm", "w_branch_mla", "w_out", "norm_mlp_w",
           "w_mlp_up", "w_mlp_down", "final_norm_w")


def _cparams(sem=None):
    return pltpu.CompilerParams(dimension_semantics=sem, vmem_limit_bytes=VMEM_LIMIT)


def _pick(n, cands):
    for c in cands:
        if n % c == 0:
            return c
    return n


def _sigmoid(x):
    return 1.0 / (1.0 + jnp.exp(-x))


def _silu(x):
    return x * _sigmoid(x)


def _softplus(x):
    return jnp.maximum(x, 0.0) + jnp.log1p(jnp.exp(-jnp.abs(x)))


def _rms(x, w):
    return x * lax.rsqrt(jnp.mean(x * x, axis=-1, keepdims=True) + EPS) * w


def _dot(a, b, ca, cb, precision=None):
    return lax.dot_general(a, b, (((ca,), (cb,)), ((), ())), preferred_element_type=F32, precision=precision)


def _mxdot(a, b, ca, cb):
    return _dot(a.astype(MXU_DTYPE), b.astype(MXU_DTYPE), ca, cb)


def _mm(name, a, b, *, ta=False, tb=False, add=None, out_dtype=F32, dep=None, epi=None):
    (kdim, m) = a.shape if ta else a.shape[::-1]
    (n, k2) = b.shape if tb else b.shape[::-1]
    assert kdim == k2, (name, a.shape, b.shape)
    tm = _pick(m, (1152, 1024, 768, 512, 384, 256, 128))
    tn = _pick(n, (1024, 512, 384, 256, 128))
    tk = _pick(kdim, (1152, 1024, 768, 512, 384, 256, 128))
    nk = kdim // tk
    a_spec = pl.BlockSpec((tk, tm), lambda i, j, k: (k, i)) if ta else pl.BlockSpec((tm, tk), lambda i, j, k: (i, k))
    b_spec = pl.BlockSpec((tn, tk), lambda i, j, k: (j, k)) if tb else pl.BlockSpec((tk, tn), lambda i, j, k: (k, j))
    o_spec = pl.BlockSpec((tm, tn), lambda i, j, k: (i, j))
    ca, cb = (0 if ta else 1), (1 if tb else 0)

    out_dtypes = [out_dtype] if epi is None else list(epi[1])
    n_out = len(out_dtypes)

    def body(*refs):
        a_ref, b_ref = refs[:2]
        o_refs, acc = refs[-1 - n_out:-1], refs[-1]
        k = pl.program_id(2)

        @pl.when(k == 0)
        def _():
            acc[...] = jnp.zeros_like(acc)

        acc[...] += _mxdot(a_ref[...], b_ref[...], ca, cb)

        @pl.when(k == nk - 1)
        def _():
            r = acc[...]
            if epi is not None:
                res = epi[0](r, refs[2][...]) if add is not None else epi[0](r)
            else:
                res = (r + refs[2][...].astype(F32) if add is not None else r,)
            for o_ref, val in zip(o_refs, res):
                o_ref[...] = val.astype(o_ref.dtype)

    in_specs, args = [a_spec, b_spec], [a, b]
    if add is not None:
        in_specs.append(o_spec)
        args.append(add)
    if dep is not None:
        in_specs.append(pl.BlockSpec((8, LANE), lambda i, j, k: (0, 0)))
        args.append(dep)
    res = pl.pallas_call(
        body, name=name, grid=(m // tm, n // tn, nk), in_specs=in_specs, out_specs=[o_spec] * n_out,
        out_shape=[jax.ShapeDtypeStruct((m, n), dt) for dt in out_dtypes], scratch_shapes=[pltpu.VMEM((tm, tn), F32)],
        compiler_params=_cparams(("parallel", "parallel", "arbitrary")))(*args)
    return res[0] if epi is None else res


def _rowwise(name, fn, *, nrows, tr, ncb=1, rows=(), fixed=(), vecs=(), tabs=(), outs=(), reds=(), tab_blocks=1):
    in_specs, args = [], []
    for arr, w, c0 in rows:
        in_specs.append(pl.BlockSpec((tr, w), lambda g, i, c0=c0: (i, c0 + g)))
        args.append(arr)
    for arr, w, c0 in fixed:
        in_specs.append(pl.BlockSpec((tr, w), lambda g, i, c0=c0: (i, c0)))
        args.append(arr)
    for arr, w, c0 in vecs:
        in_specs.append(pl.BlockSpec((1, w), lambda g, i, c0=c0: (0, c0 + g)))
        args.append(arr)
    for arr, w, c0 in tabs:
        in_specs.append(pl.BlockSpec((tr, w), lambda g, i, c0=c0: (i % tab_blocks, c0)))
        args.append(arr)
    out_shape = [jax.ShapeDtypeStruct((nrows, wt), dt) for wt, w, dt in outs]
    out_shape += [jax.ShapeDtypeStruct((1, wt), F32) for wt, w in reds]
    out_specs = [pl.BlockSpec((tr, w), lambda g, i: (i, g)) for wt, w, dt in outs]
    out_specs += [pl.BlockSpec((1, w), lambda g, i: (0, g)) for wt, w in reds]
    n_in, n_out = len(args), len(outs)

    def body(*refs):
        res = fn(*[r[...] for r in refs[:n_in]])
        for o_ref, val in zip(refs[n_in:n_in + n_out], res[:n_out]):
            o_ref[...] = val.astype(o_ref.dtype)
        i = pl.program_id(1)
        for d_ref, val in zip(refs[n_in + n_out:], res[n_out:]):
            @pl.when(i == 0)
            def _(d_ref=d_ref, val=val):
                d_ref[...] = val

            @pl.when(i > 0)
            def _(d_ref=d_ref, val=val):
                d_ref[...] += val

    res = pl.pallas_call(
        body, name=name, grid=(ncb, nrows // tr), in_specs=in_specs, out_specs=out_specs, out_shape=out_shape,
        compiler_params=_cparams(("parallel", "arbitrary")))(*args)
    return res


def _peer(k):
    x, y, c = lax.axis_index("x"), lax.axis_index("y"), lax.axis_index("c")
    px = jnp.where((k >> 2) & 1, 1 - x, x)
    py = jnp.where((k >> 1) & 1, 1 - y, y)
    pc = jnp.where(k & 1, 1 - c, c)
    return (px, py, pc), 4 * px + 2 * py + pc


def _my_index():
    return 4 * lax.axis_index("x") + 2 * lax.axis_index("y") + lax.axis_index("c")


def _exchange(name, ins, out_shapes, items):
    n_in, n_out, n_it = len(ins), len(out_shapes), len(items)

    def body(*refs):
        x, o = refs[:n_in], refs[n_in:n_in + n_out]
        send_sems, recv_sems, local_sems = refs[n_in + n_out:]
        me = _my_index()
        local, sends = [], []
        for t, (ii, io, src, dst) in enumerate(items):
            cp = pltpu.make_async_copy(src(x[ii], me), dst(o[io], me), local_sems.at[t])
            cp.start()
            local.append(cp)
        for k in range(1, N_DEV):
            dev, idx = _peer(k)
            for t, (ii, io, src, dst) in enumerate(items):
                s = (k - 1) * n_it + t
                cp = pltpu.make_async_remote_copy(
                    src_ref=src(x[ii], idx), dst_ref=dst(o[io], me), send_sem=send_sems.at[s],
                    recv_sem=recv_sems.at[s], device_id=dev, device_id_type=MESH_ID)
                cp.start()
                sends.append(cp)
        for k in range(1, N_DEV):
            dev, idx = _peer(k)
            for t, (ii, io, src, dst) in enumerate(items):
                s = (k - 1) * n_it + t
                pltpu.make_async_remote_copy(
                    src_ref=src(x[ii], idx), dst_ref=dst(o[io], idx), send_sem=send_sems.at[s],
                    recv_sem=recv_sems.at[s], device_id=dev, device_id_type=MESH_ID).wait_recv()
        for cp in sends:
            cp.wait_send()
        for cp in local:
            cp.wait()

    nsem = (N_DEV - 1) * n_it
    anyspec = pl.BlockSpec(memory_space=pl.ANY)
    return pl.pallas_call(
        body, name=name, out_shape=list(out_shapes), in_specs=[anyspec] * n_in, out_specs=[anyspec] * n_out,
        scratch_shapes=[pltpu.SemaphoreType.DMA((nsem,)), pltpu.SemaphoreType.DMA((nsem,)),
                        pltpu.SemaphoreType.DMA((n_it,))],
        compiler_params=pltpu.CompilerParams(has_side_effects=True))(*ins)


def _split_copies(x, land, send_sems, recv_sems, items, receive):
    me = _my_index()
    remote, n_it = [], len(items)
    for k in range(1, N_DEV):
        dev, idx = _peer(k)
        for t, (ii, io, src, dst) in enumerate(items):
            s = (k - 1) * n_it + t
            remote.append(pltpu.make_async_remote_copy(
                src_ref=src(x[ii], idx), dst_ref=dst(land[io], idx if receive else me), send_sem=send_sems.at[s],
                recv_sem=recv_sems.at[s], device_id=dev, device_id_type=MESH_ID))
    local = [pltpu.make_async_copy(src(x[ii], me), dst(land[io], me), send_sems.at[(N_DEV - 1) * n_it + t])
             for t, (ii, io, src, dst) in enumerate(items)]
    return remote, local


def _exchange_start(name, ins, out_shapes, items, dep=None):
    n_in, n_out, n_it = len(ins), len(out_shapes), len(items)

    def body(*refs):
        x, land = refs[:n_in], refs[n_in:n_in + n_out]
        first_out = n_in + n_out + (dep is not None)
        send_sems, recv_sems, token = refs[first_out], refs[first_out + 1], refs[-1]
        remote, local = _split_copies(x, land, send_sems, recv_sems, items, False)
        for cp in remote + local:
            cp.start()
        token[...] = jnp.zeros_like(token)

    hbm = pl.BlockSpec(memory_space=pltpu.HBM)
    sem = pl.BlockSpec(memory_space=pltpu.SEMAPHORE)
    arrs = [pltpu.with_memory_space_constraint(a, pltpu.HBM)
            for a in list(ins) + [lax.empty(s.shape, s.dtype) for s in out_shapes]]
    res = pl.pallas_call(
        body, name=name,
        out_shape=(pltpu.SemaphoreType.DMA((N_DEV * n_it,)), pltpu.SemaphoreType.DMA(((N_DEV - 1) * n_it,)),
                   *[pltpu.HBM(a.shape, a.dtype) for a in arrs], jax.ShapeDtypeStruct((8, LANE), F32)),
        in_specs=[hbm] * (n_in + n_out) + ([] if dep is None else [pl.BlockSpec(memory_space=pl.ANY)]),
        out_specs=(sem, sem, *[hbm] * (n_in + n_out), pl.BlockSpec(memory_space=pltpu.VMEM)),
        input_output_aliases={i: 2 + i for i in range(n_in + n_out)},
        compiler_params=pltpu.CompilerParams(has_side_effects=pltpu.SideEffectType.DATAFLOW_SIDE_EFFECTING))(
            *arrs, *([] if dep is None else [dep]))
    return res[:2], res[2:2 + n_in], res[2 + n_in:2 + n_in + n_out], res[-1]


def _exchange_wait(name, sems, ins, landing, items, after):
    n_in, n_out = len(ins), len(landing)

    def body(*refs):
        x, land = refs[:n_in], refs[n_in:n_in + n_out]
        send_sems, recv_sems = refs[n_in + n_out], refs[n_in + n_out + 1]
        remote, local = _split_copies(x, land, send_sems, recv_sems, items, True)
        for cp in remote:
            cp.wait_send()
            cp.wait_recv()
        for cp in local:
            cp.wait()

    hbm = pl.BlockSpec(memory_space=pltpu.HBM)
    sem = pl.BlockSpec(memory_space=pltpu.SEMAPHORE)
    arrs = list(ins) + list(landing)
    res = pl.pallas_call(
        body, name=name, out_shape=tuple(pltpu.HBM(a.shape, a.dtype) for a in arrs),
        in_specs=[hbm] * (n_in + n_out) + [sem, sem, pl.BlockSpec(memory_space=pl.ANY)],
        out_specs=tuple([hbm] * (n_in + n_out)), input_output_aliases={i: i for i in range(n_in + n_out)},
        compiler_params=pltpu.CompilerParams(has_side_effects=pltpu.SideEffectType.DATAFLOW_SIDE_EFFECTING))(
            *arrs, *sems, after)
    return res[n_in:]


def _whole(ref, p):
    return ref


def _entry(ref, p):
    return ref.at[p]


def _gather_plan(a, b, kind):
    if kind == "col" and b % LANE == 0:
        return (a, N_DEV * b), (lambda ref, p: ref.at[:, pl.ds(pl.multiple_of(p * b, b), b)]), "col"
    return (N_DEV, a, b), _entry, ("row" if kind == "row" else "stack")


def _adamw_nat(name, parts, w, m, v):
    depth, b, c = w.shape
    assert len(parts) == depth
    tb = _pick(b, (128, 64, 32, 16, 8))
    spec = pl.BlockSpec((1, tb, c), lambda i, j: (i, j, 0))

    def body(*refs):
        p_refs = refs[:depth]
        w_ref, m_ref, v_ref, g_ref, d_ref, nm_ref, nv_ref = refs[depth:]
        for layer, p_ref in enumerate(p_refs):
            @pl.when(pl.program_id(0) == layer)
            def _(p_ref=p_ref):
                g = p_ref[0].astype(F32)
                for j in range(1, N_DEV):
                    g = g + p_ref[j].astype(F32)
                nm = ADAM_B1 * m_ref[0] + (1.0 - ADAM_B1) * g
                nv = ADAM_B2 * v_ref[0] + (1.0 - ADAM_B2) * jnp.square(g)
                m_hat = nm / (1.0 - ADAM_B1 ** ADAM_STEP)
                v_hat = nv / (1.0 - ADAM_B2 ** ADAM_STEP)
                g_ref[0] = g
                d_ref[0] = -ADAM_LR * (m_hat / (jnp.sqrt(v_hat) + ADAM_EPS) + ADAM_WD * w_ref[0])
                nm_ref[0] = nm
                nv_ref[0] = nv

    sds = jax.ShapeDtypeStruct((depth, b, c), F32)
    return pl.pallas_call(
        body, name=name, grid=(depth, b // tb),
        in_specs=[pl.BlockSpec((N_DEV, tb, c), lambda i, j: (0, j, 0))] * depth + [spec, spec, spec],
        out_specs=[spec] * 4, out_shape=[sds] * 4, compiler_params=_cparams(("parallel", "parallel")))(*parts, w, m, v)


def _adamw(name, parts, w, m, v):
    rows = w.shape[0]
    tr = _pick(rows, (256, 128, 64, 32, 16, 8))
    spec = pl.BlockSpec((tr, PACK_W), lambda i: (i, 0))

    def body(p_ref, w_ref, m_ref, v_ref, g_ref, d_ref, nm_ref, nv_ref):
        g = p_ref[0]
        for j in range(1, N_DEV):
            g = g + p_ref[j]
        nm = ADAM_B1 * m_ref[...] + (1.0 - ADAM_B1) * g
        nv = ADAM_B2 * v_ref[...] + (1.0 - ADAM_B2) * jnp.square(g)
        m_hat = nm / (1.0 - ADAM_B1 ** ADAM_STEP)
        v_hat = nv / (1.0 - ADAM_B2 ** ADAM_STEP)
        g_ref[...] = g
        d_ref[...] = -ADAM_LR * (m_hat / (jnp.sqrt(v_hat) + ADAM_EPS) + ADAM_WD * w_ref[...])
        nm_ref[...] = nm
        nv_ref[...] = nv

    sds = jax.ShapeDtypeStruct((rows, PACK_W), F32)
    return pl.pallas_call(
        body, name=name, grid=(rows // tr,),
        in_specs=[pl.BlockSpec((N_DEV, tr, PACK_W), lambda i: (0, i, 0)), spec, spec, spec],
        out_specs=[spec] * 4, out_shape=[sds] * 4, compiler_params=_cparams(("parallel",)))(parts, w, m, v)


def _pack(arrs, dtype, row_mult=16):
    flat = jnp.concatenate([a.reshape(-1).astype(dtype) for a in arrs])
    unit = row_mult * PACK_W
    total = -(-flat.shape[0] // unit) * unit
    flat = jnp.pad(flat, (0, total - flat.shape[0]))
    return flat.reshape(-1, PACK_W)


def _pack_lead(arrs, dtype, row_mult):
    flat = jnp.concatenate([a.reshape(N_DEV, -1).astype(dtype) for a in arrs], axis=1)
    unit = row_mult * PACK_W
    total = -(-flat.shape[1] // unit) * unit
    flat = jnp.pad(flat, ((0, 0), (0, total - flat.shape[1])))
    return flat.reshape(N_DEV, -1, PACK_W)


def _unpack(buf, shapes, lead=()):
    flat = buf.reshape(lead + (-1,))
    out, off = [], 0
    for s in shapes:
        n = int(np.prod(s))
        out.append(flat[..., off:off + n].reshape(lead + tuple(s)))
        off += n
    return out


def _unshard(g, kind):
    if kind == "col":
        g = jnp.moveaxis(g, 0, -2)
        return g.reshape(g.shape[:-2] + (g.shape[-2] * g.shape[-1],))
    g = jnp.moveaxis(g, 0, 1)
    return g.reshape((g.shape[0], g.shape[1] * g.shape[2]) + g.shape[3:])


def _shard(full, kind):
    if kind == "col":
        s = full.reshape(full.shape[:-1] + (N_DEV, full.shape[-1] // N_DEV))
        return jnp.moveaxis(s, -2, 0)
    s = full.reshape((full.shape[0], N_DEV, full.shape[1] // N_DEV) + full.shape[2:])
    return jnp.moveaxis(s, 1, 0)


class _Geo:
    def __init__(self, bsz, seq):
        self.bsz, self.seq = bsz, seq
        self.pad = (-(N_META + seq)) % ATT_BLK
        self.lp = self.pad + N_META + seq
        assert (self.pad + N_META) % SSM_CHUNK == 0 and self.lp % SSM_CHUNK == 0
        self.nrows = bsz * self.lp
        self.nc = self.lp // SSM_CHUNK
        self.nh = SSM_D_INNER // SSM_HEAD_DIM
        self.gn = SSM_GROUPS * SSM_STATE
        self.cd = SSM_D_INNER + 2 * self.gn
        self.hq = MLA_HEADS * LANE
        order = (("z", SSM_D_INNER), ("xs", SSM_D_INNER), ("g_ssm", D_MODEL), ("g_mla", D_MODEL), ("bm", self.gn),
                 ("cm", self.gn), ("c_q", MLA_Q_LORA), ("c_kv", MLA_KV_LORA), ("dt", LANE), ("k_rope", LANE))
        self.col, off = {}, 0
        for nm, w in order:
            assert off % w == 0, (nm, off, w)
            self.col[nm] = (off, w)
            off += w
        self.pw = off
        assert self.nh <= LANE and MLA_ROPE == 64 and MLA_NOPE == LANE and MLA_V == LANE
        self.tr = _pick(self.lp, (768, 512, 384, 256, 128))
        self.tr_wide = _pick(self.lp, (384, 256, 128))

    def cb(self, nm):
        off, w = self.col[nm]
        return off // w

    def w_in_runs(self, shard_w):
        nh, half = self.nh, MLA_ROPE // 2
        src, pieces = 0, []
        for nm, n in (("z", SSM_D_INNER), ("xs", SSM_D_INNER), ("bm", self.gn), ("cm", self.gn), ("dt", nh),
                      ("c_q", MLA_Q_LORA), ("c_kv", MLA_KV_LORA), ("k_rope", MLA_ROPE), ("g_ssm", D_MODEL),
                      ("g_mla", D_MODEL)):
            dst = self.col[nm][0]
            if nm == "k_rope":
                pieces += [(src, half, dst), (src + half, half, dst + 2 * half)]
            else:
                pieces.append((src, n, dst))
            src += n
        assert src == shard_w * N_DEV
        runs = []
        for a, n, dst in pieces:
            for j in range(N_DEV):
                lo, hi = max(a, j * shard_w), min(a + n, (j + 1) * shard_w)
                if lo < hi:
                    runs.append((j, lo - j * shard_w, hi - lo, dst + lo - a))
        return runs


def _slot(a):
    h = MLA_ROPE // 2
    z = jnp.zeros(a.shape[:-1] + (h,), a.dtype)
    return jnp.concatenate([a[..., :h], z, a[..., h:], z], axis=-1)


def _unslot(a):
    h = MLA_ROPE // 2
    return jnp.concatenate([a[..., :h], a[..., 2 * h:3 * h]], axis=-1)


def _prep_layer(geo, wl):
    nh = geo.nh
    p = {}
    if "w_uq" in wl:
        uq = wl["w_uq"].reshape(MLA_Q_LORA, MLA_HEADS, MLA_NOPE + MLA_ROPE)
        p["w_qn"] = uq[..., :MLA_NOPE].reshape(MLA_Q_LORA, geo.hq)
        p["w_qp"] = _slot(uq[..., MLA_NOPE:]).reshape(MLA_Q_LORA, geo.hq)
    if "w_ukv" in wl:
        ukv = wl["w_ukv"].reshape(MLA_KV_LORA, MLA_HEADS, MLA_NOPE + MLA_V)
        p["w_k"] = ukv[..., :MLA_NOPE].reshape(MLA_KV_LORA, geo.hq)
        p["w_v"] = ukv[..., MLA_NOPE:].reshape(MLA_KV_LORA, geo.hq)
    for nm in ("w_in_p", "conv_w", "w_branch_ssm", "w_branch_mla", "w_out", "w_mlp_up", "w_mlp_down"):
        if nm in wl:
            p[nm] = wl[nm]
    for nm in ("norm_mix_w", "conv_b", "ssm_norm_w", "q_norm_w", "kv_norm_w", "norm_mlp_w"):
        if nm in wl:
            p[nm] = wl[nm].reshape(1, -1)
    if "dt_bias" in wl:
        p["dt_bias"] = jnp.pad(wl["dt_bias"], (0, LANE - nh)).reshape(1, LANE)
        p["a_log"] = jnp.pad(wl["a_log"], (0, LANE - nh)).reshape(1, LANE)
        p["d_skip_full"] = jnp.repeat(wl["d_skip"], SSM_HEAD_DIM).reshape(1, SSM_D_INNER)
    return p


def _unprep_grads(geo, g):
    nh = geo.nh
    out = {}
    if "w_qn" in g:
        qn = g["w_qn"].reshape(MLA_Q_LORA, MLA_HEADS, MLA_NOPE)
        qp = _unslot(g["w_qp"].reshape(MLA_Q_LORA, MLA_HEADS, LANE))
        out["w_uq"] = jnp.concatenate([qn, qp], axis=-1).reshape(MLA_Q_LORA, -1)
    if "w_k" in g:
        wk = g["w_k"].reshape(MLA_KV_LORA, MLA_HEADS, MLA_NOPE)
        wv = g["w_v"].reshape(MLA_KV_LORA, MLA_HEADS, MLA_V)
        out["w_ukv"] = jnp.concatenate([wk, wv], axis=-1).reshape(MLA_KV_LORA, -1)
    for nm in ("w_in_p", "w_branch_ssm", "w_branch_mla", "w_out", "w_mlp_up", "w_mlp_down", "conv_w"):
        if nm in g:
            out[nm] = g[nm]
    for nm in ("norm_mix_w", "conv_b", "ssm_norm_w", "q_norm_w", "kv_norm_w", "norm_mlp_w"):
        if nm in g:
            out[nm] = g[nm].reshape(-1)
    if "dt_bias" in g:
        out["dt_bias"] = g["dt_bias"].reshape(-1)[:nh]
        out["a_log"] = g["a_log"].reshape(-1)[:nh]
        out["d_skip"] = g["d_skip_full"].reshape(nh, SSM_HEAD_DIM).sum(-1)
    return out


def _tables(geo):
    pos = jnp.arange(geo.lp, dtype=F32) - geo.pad
    inv = ROPE_THETA ** (-jnp.arange(0, MLA_ROPE, 2, dtype=F32) / MLA_ROPE)
    ang = pos[:, None] * inv[None, :]
    cos, sin = jnp.cos(ang), jnp.sin(ang)
    z = jnp.zeros_like(cos)
    rows = jnp.arange(geo.lp)[:, None]
    return {"cos": jnp.concatenate([cos, z, cos, z], axis=-1), "sin": jnp.concatenate([-sin, z, sin, z], axis=-1),
            "valid": (rows >= geo.pad).astype(F32), "token": (rows >= geo.pad + N_META).astype(F32)}


def _w_in_assemble(geo, gathered):
    _, d, sw = gathered.shape
    runs = geo.w_in_runs(sw)
    tr = _pick(d, (256, 128))

    def body(x_ref, o_ref):
        o_ref[...] = jnp.zeros_like(o_ref)
        for j, s0, n, d0 in runs:
            o_ref[:, d0:d0 + n] = x_ref[j, :, s0:s0 + n]

    return pl.pallas_call(
        body, name="w_in_assemble", grid=(d // tr,), in_specs=[pl.BlockSpec((N_DEV, tr, sw), lambda i: (0, i, 0))],
        out_specs=pl.BlockSpec((tr, geo.pw), lambda i: (i, 0)),
        out_shape=jax.ShapeDtypeStruct((d, geo.pw), gathered.dtype), compiler_params=_cparams(("parallel",)))(gathered)


def _w_in_split(geo, g_padded, sw):
    d = g_padded.shape[0]
    runs = geo.w_in_runs(sw)
    tr = _pick(d, (128,))

    def body(x_ref, o_ref):
        for j, s0, n, d0 in runs:
            o_ref[j, :, s0:s0 + n] = x_ref[:, d0:d0 + n]

    return pl.pallas_call(
        body, name="w_in_split", grid=(d // tr,), in_specs=[pl.BlockSpec((tr, geo.pw), lambda i: (i, 0))],
        out_specs=pl.BlockSpec((N_DEV, tr, sw), lambda i: (0, i, 0)),
        out_shape=jax.ShapeDtypeStruct((N_DEV, d, sw), g_padded.dtype),
        compiler_params=_cparams(("parallel",)))(g_padded)


def _conv_cols(geo, cbw):
    nx = SSM_D_INNER // cbw
    x0, b0 = geo.col["xs"][0] // cbw, geo.col["bm"][0] // cbw
    assert geo.col["cm"][0] == geo.col["bm"][0] + geo.gn
    return lambda j: jnp.where(j < nx, x0 + j, b0 + j - nx)


def _conv_taps(x):
    return [pltpu.roll(x, SSM_CONV - 1 - k, axis=0) for k in range(SSM_CONV - 1)] + [x]


def _conv_pre(x, w_ref, b_ref, taps=None):
    taps = _conv_taps(x) if taps is None else taps
    acc = b_ref[...]
    for k in range(SSM_CONV):
        acc = acc + taps[k] * w_ref[k:k + 1, :]
    return acc


def _conv_fwd(geo, proj, conv_w, conv_b):
    cbw = 256
    colmap = _conv_cols(geo, cbw)
    lp, pad = geo.lp, geo.pad

    def body(x_ref, w_ref, b_ref, o_ref):
        valid = (lax.broadcasted_iota(jnp.int32, (lp, 1), 0) >= pad).astype(F32)
        o_ref[...] = _silu(_conv_pre(x_ref[...], w_ref, b_ref)) * valid

    return pl.pallas_call(
        body, name="conv_fwd", grid=(geo.bsz, geo.cd // cbw),
        in_specs=[pl.BlockSpec((lp, cbw), lambda b, j: (b, colmap(j))),
                  pl.BlockSpec((SSM_CONV, cbw), lambda b, j: (0, j)), pl.BlockSpec((1, cbw), lambda b, j: (0, j))],
        out_specs=pl.BlockSpec((lp, cbw), lambda b, j: (b, j)),
        out_shape=jax.ShapeDtypeStruct((geo.nrows, geo.cd), F32),
        compiler_params=_cparams(("parallel", "parallel")))(proj, conv_w, conv_b)


def _conv_bwd(geo, proj, conv_w, conv_b, dxc):
    cbw = 256
    colmap = _conv_cols(geo, cbw)
    lp, pad = geo.lp, geo.pad

    def body(x_ref, w_ref, b_ref, dy_ref, dx_ref, gw_ref, gb_ref):
        b = pl.program_id(1)
        valid = (lax.broadcasted_iota(jnp.int32, (lp, 1), 0) >= pad).astype(F32)
        taps = _conv_taps(x_ref[...])
        pre = _conv_pre(None, w_ref, b_ref, taps)
        sig = _sigmoid(pre)
        dpre = dy_ref[...] * (sig * (1.0 + pre * (1.0 - sig))) * valid
        dx = dpre * w_ref[SSM_CONV - 1:SSM_CONV, :]
        for k in range(SSM_CONV - 1):
            dx = dx + pltpu.roll(dpre, lp - (SSM_CONV - 1 - k), axis=0) * w_ref[k:k + 1, :]
        gws = [jnp.sum(dpre * taps[k], axis=0, keepdims=True) for k in range(SSM_CONV)]
        dx_ref[...] = (dx * valid).astype(dx_ref.dtype)

        @pl.when(b == 0)
        def _():
            gw_ref[...] = jnp.zeros_like(gw_ref)
            gb_ref[...] = jnp.zeros_like(gb_ref)

        for k in range(SSM_CONV):
            gw_ref[k:k + 1, :] += gws[k]
        gb_ref[...] += jnp.sum(dpre, axis=0, keepdims=True)

    return pl.pallas_call(
        body, name="conv_bwd", grid=(geo.cd // cbw, geo.bsz),
        in_specs=[pl.BlockSpec((lp, cbw), lambda j, b: (b, colmap(j))),
                  pl.BlockSpec((SSM_CONV, cbw), lambda j, b: (0, j)), pl.BlockSpec((1, cbw), lambda j, b: (0, j)),
                  pl.BlockSpec((lp, cbw), lambda j, b: (b, j))],
        out_specs=[pl.BlockSpec((lp, cbw), lambda j, b: (b, j)), pl.BlockSpec((SSM_CONV, cbw), lambda j, b: (0, j)),
                   pl.BlockSpec((1, cbw), lambda j, b: (0, j))],
        out_shape=[jax.ShapeDtypeStruct((geo.nrows, geo.cd), MXU_DTYPE),
                   jax.ShapeDtypeStruct((SSM_CONV, geo.cd), F32), jax.ShapeDtypeStruct((1, geo.cd), F32)],
        compiler_params=_cparams(("parallel", "arbitrary")))(proj, conv_w, conv_b, dxc)


def _tri(q):
    r = lax.broadcasted_iota(jnp.int32, (q, q), 0)
    c = lax.broadcasted_iota(jnp.int32, (q, q), 1)
    return r >= c


def _ssd_pre(dtr, dtb, alog, valid):
    dt = _softplus(dtr + dtb) * valid
    adt = dt * (-jnp.exp(alog))
    a_cs = _dot(_tri(SSM_CHUNK).astype(F32), adt, 1, 0, precision=lax.Precision.HIGHEST)
    return dt, a_cs


def _ssd_specs(geo, rev):
    nc, q = geo.nc, SSM_CHUNK
    ci = (lambda c: nc - 1 - c) if rev else (lambda c: c)
    nxb = SSM_D_INNER // geo.gn
    return [pl.BlockSpec((q, SSM_D_INNER), lambda b, c: (b * nc + ci(c), 0)),
            pl.BlockSpec((q, geo.gn), lambda b, c: (b * nc + ci(c), nxb)),
            pl.BlockSpec((q, geo.gn), lambda b, c: (b * nc + ci(c), nxb + 1)),
            pl.BlockSpec((q, LANE), lambda b, c: (b * nc + ci(c), geo.cb("dt"))),
            pl.BlockSpec((1, LANE), lambda b, c: (0, 0)), pl.BlockSpec((1, LANE), lambda b, c: (0, 0))], ci


def _expand_heads(cols, nh):
    per = LANE // SSM_HEAD_DIM
    lane = lax.broadcasted_iota(jnp.int32, (1, LANE), 1)
    blocks = []
    for j in range(nh // per):
        blk = jnp.broadcast_to(cols[:, j * per:j * per + 1], (cols.shape[0], LANE))
        for k in range(1, per):
            blk = jnp.where(lane >= k * SSM_HEAD_DIM, cols[:, j * per + k:j * per + k + 1], blk)
        blocks.append(blk)
    return jnp.concatenate(blocks, axis=1)


def _head_maps(geo):
    e = (jnp.arange(SSM_D_INNER)[None, :] // SSM_HEAD_DIM == jnp.arange(LANE)[:, None]).astype(F32)
    return e, e.T


def _ssd_fwd_g(geo, xc, proj, dt_bias, a_log):
    q, p, n, e = SSM_CHUNK, SSM_HEAD_DIM, SSM_STATE, geo.nh // SSM_GROUPS
    nc, pad, gw = geo.nc, geo.pad, SSM_D_INNER // SSM_GROUPS
    in_specs, _ = _ssd_specs(geo, False)
    _, e_t = _head_maps(geo)
    in_specs.append(pl.BlockSpec((SSM_D_INNER, LANE), lambda b, c: (0, 0)))

    def body(xs_ref, b_ref, c_ref, dtr_ref, dtb_ref, alog_ref, et_ref, y_ref, sp_ref, state, xdt_s):
        c = pl.program_id(1)

        @pl.when(c == 0)
        def _():
            state[...] = jnp.zeros_like(state)

        sp_ref[...] = state[...]
        inert = (c + 1) * q <= pad

        @pl.when(inert)
        def _():
            y_ref[...] = jnp.zeros_like(y_ref)

        @pl.when(jnp.logical_not(inert))
        def _():
            valid = (c * q + lax.broadcasted_iota(jnp.int32, (q, 1), 0) >= pad).astype(F32)
            dt, a_cs = _ssd_pre(dtr_ref[...], dtb_ref[...], alog_ref[...], valid)
            a_cst = a_cs.T
            dt_x, a_x = _expand_heads(dt, geo.nh), _expand_heads(a_cs, geo.nh)
            e_last = jnp.exp(a_cs[q - 1:q, :])
            tri = _tri(q)
            for g in range(SSM_GROUPS):
                gs = slice(g * gw, (g + 1) * gw)
                bg, cg = b_ref[:, g * n:(g + 1) * n], c_ref[:, g * n:(g + 1) * n]
                a_g = a_x[:, gs]
                xdt_g = xs_ref[:, gs] * dt_x[:, gs]
                xdt_s[:, gs] = xdt_g
                s_g = state[gs, :]
                y_ref[:, gs] = _mxdot(cg, s_g, 1, 1) * jnp.exp(a_g)
                e_last_rows = jnp.sum(et_ref[gs, :] * e_last, axis=1, keepdims=True)
                state[gs, :] = s_g * e_last_rows + _mxdot(xdt_g * jnp.exp(a_g[q - 1:q, :] - a_g), bg, 0, 0)
                cb = _mxdot(cg, bg, 1, 1)
                for hh in range(e):
                    h = g * e + hh
                    hs = slice(h * p, (h + 1) * p)
                    ldec = jnp.exp(jnp.where(tri, a_cs[:, h:h + 1] - a_cst[h:h + 1, :], -jnp.inf))
                    y_ref[:, hs] += _mxdot(cb * ldec, xdt_s[:, hs], 1, 0)

    return pl.pallas_call(
        body, name="ssd_fwd", grid=(geo.bsz, nc), in_specs=in_specs,
        out_specs=[pl.BlockSpec((q, SSM_D_INNER), lambda b, c: (b * nc + c, 0)),
                   pl.BlockSpec((SSM_D_INNER, n), lambda b, c: (b * nc + c, 0))],
        out_shape=[jax.ShapeDtypeStruct((geo.nrows, SSM_D_INNER), F32),
                   jax.ShapeDtypeStruct((geo.bsz * nc * SSM_D_INNER, n), F32)],
        scratch_shapes=[pltpu.VMEM((SSM_D_INNER, n), F32), pltpu.VMEM((q, SSM_D_INNER), F32)],
        compiler_params=_cparams(("parallel", "arbitrary")))(xc, xc, xc, proj, dt_bias, a_log, e_t)


def _ssd_bwd_g(geo, xc, proj, dt_bias, a_log, s_prev_all, dy, dxs_skip):
    q, p, n, e = SSM_CHUNK, SSM_HEAD_DIM, SSM_STATE, geo.nh // SSM_GROUPS
    nc, pad, di, gn, gw = geo.nc, geo.pad, SSM_D_INNER, geo.gn, SSM_D_INNER // SSM_GROUPS
    in_specs, ci = _ssd_specs(geo, True)
    row_spec = pl.BlockSpec((q, di), lambda b, c: (b * nc + ci(c), 0))
    e_map, e_t = _head_maps(geo)
    in_specs += [pl.BlockSpec((di, n), lambda b, c: (b * nc + ci(c), 0)), row_spec, row_spec,
                 pl.BlockSpec((LANE, di), lambda b, c: (0, 0)), pl.BlockSpec((di, LANE), lambda b, c: (0, 0))]

    def body(xs_ref, b_ref, c_ref, dtr_ref, dtb_ref, alog_ref, sp_ref, dy_ref, dsk_ref, e_ref, et_ref,
             dxc_ref, ddt_ref, gdtb_ref, galog_ref, dstate, xdt_s, dxdt_s):
        step = pl.program_id(1)
        first = jnp.logical_and(pl.program_id(0) == 0, step == 0)
        c = nc - 1 - step

        @pl.when(step == 0)
        def _():
            dstate[...] = jnp.zeros_like(dstate)

        @pl.when(first)
        def _():
            gdtb_ref[...] = jnp.zeros_like(gdtb_ref)
            galog_ref[...] = jnp.zeros_like(galog_ref)

        inert = (c + 1) * q <= pad

        @pl.when(inert)
        def _():
            dxc_ref[...] = jnp.zeros_like(dxc_ref)
            ddt_ref[...] = jnp.zeros_like(ddt_ref)

        @pl.when(jnp.logical_not(inert))
        def _():
            valid = (c * q + lax.broadcasted_iota(jnp.int32, (q, 1), 0) >= pad).astype(F32)
            dtr, dtb, alog = dtr_ref[...], dtb_ref[...], alog_ref[...]
            dt, a_cs = _ssd_pre(dtr, dtb, alog, valid)
            a_cst = a_cs.T
            dt_x, a_x = _expand_heads(dt, geo.nh), _expand_heads(a_cs, geo.nh)
            e_last = jnp.exp(a_cs[q - 1:q, :])
            tri = _tri(q)
            lane = lax.broadcasted_iota(jnp.int32, (1, LANE), 1)
            sub = lax.broadcasted_iota(jnp.int32, (LANE, 1), 0)
            d_dt = jnp.zeros((q, LANE), F32)
            d_acs = jnp.zeros((q, LANE), F32)
            d_acst = jnp.zeros((LANE, q), F32)
            d_last = jnp.zeros((1, LANE), F32)
            for g in range(SSM_GROUPS):
                gs = slice(g * gw, (g + 1) * gw)
                bg, cg = b_ref[:, g * n:(g + 1) * n], c_ref[:, g * n:(g + 1) * n]
                seg = lambda v: _mxdot(v, e_ref[:, gs], 1, 1)
                a_g, dt_g, x_g, dy_g = a_x[:, gs], dt_x[:, gs], xs_ref[:, gs], dy_ref[:, gs]
                e_col, dec = jnp.exp(a_g), jnp.exp(a_g[q - 1:q, :] - a_g)
                xdt_g = x_g * dt_g
                xdt_s[:, gs] = xdt_g
                s_g, ds_g, et_g = sp_ref[gs, :], dstate[gs, :], et_ref[gs, :]
                cs = _mxdot(cg, s_g, 1, 1)
                d_cs = dy_g * e_col
                d_acs = d_acs + seg(d_cs * cs)
                d_cg = _mxdot(d_cs, s_g, 1, 0)
                dstate[gs, :] = _mxdot(d_cs, cg, 0, 0) + ds_g * jnp.sum(et_g * e_last, axis=1, keepdims=True)
                d_last = d_last + jnp.sum(jnp.sum(ds_g * s_g, axis=1, keepdims=True) * et_g, axis=0,
                                          keepdims=True) * e_last
                gmat = _mxdot(bg, ds_g, 1, 1)
                xd = xdt_g * dec
                d_bg = _mxdot(xd, ds_g, 1, 0)
                d_dec = seg(xd * gmat)
                d_acs = d_acs - d_dec
                d_last = d_last + jnp.sum(d_dec, axis=0, keepdims=True)
                dxdt_s[:, gs] = dec * gmat
                cb = _mxdot(cg, bg, 1, 1)
                d_cb = jnp.zeros((q, q), F32)
                for hh in range(e):
                    h = g * e + hh
                    hs = slice(h * p, (h + 1) * p)
                    ldec = jnp.exp(jnp.where(tri, a_cs[:, h:h + 1] - a_cst[h:h + 1, :], -jnp.inf))
                    dyh = dy_ref[:, hs]
                    d_m = _mxdot(dyh, xdt_s[:, hs], 1, 1)
                    dxdt_s[:, hs] += _mxdot(cb * ldec, dyh, 0, 0)
                    d_cb = d_cb + d_m * ldec
                    d_diff = d_m * cb * ldec
                    d_acs = d_acs + jnp.sum(d_diff, axis=1, keepdims=True) * (lane == h).astype(F32)
                    d_acst = d_acst - (sub == h).astype(F32) * jnp.sum(d_diff, axis=0, keepdims=True)
                d_xdt = dxdt_s[:, gs]
                dxc_ref[:, gs] = d_xdt * dt_g + dsk_ref[:, gs]
                d_dt = d_dt + seg(d_xdt * x_g)
                dxc_ref[:, di + g * n:di + (g + 1) * n] = d_bg + _mxdot(d_cb, cg, 0, 0)
                dxc_ref[:, di + gn + g * n:di + gn + (g + 1) * n] = d_cg + _mxdot(d_cb, bg, 1, 0)
            is_last = (lax.broadcasted_iota(jnp.int32, (q, 1), 0) == q - 1).astype(F32)
            d_acs = d_acs + d_acst.T + is_last * d_last
            d_adt = _dot(_tri(q).astype(F32), d_acs, 0, 0, precision=lax.Precision.HIGHEST)
            a = -jnp.exp(alog)
            d_dt = d_dt + d_adt * a
            d_dtr = d_dt * valid * _sigmoid(dtr + dtb)
            ddt_ref[...] = d_dtr.astype(ddt_ref.dtype)
            gdtb_ref[...] += jnp.sum(d_dtr, axis=0, keepdims=True)
            galog_ref[...] += jnp.sum(d_adt * dt, axis=0, keepdims=True) * a

    vec = pl.BlockSpec((1, LANE), lambda b, c: (0, 0))
    return pl.pallas_call(
        body, name="ssd_bwd", grid=(geo.bsz, nc), in_specs=in_specs,
        out_specs=[pl.BlockSpec((q, geo.cd), lambda b, c: (b * nc + ci(c), 0)),
                   pl.BlockSpec((q, LANE), lambda b, c: (b * nc + ci(c), 0)), vec, vec],
        out_shape=[jax.ShapeDtypeStruct((geo.nrows, geo.cd), F32), jax.ShapeDtypeStruct((geo.nrows, LANE), MXU_DTYPE),
                   jax.ShapeDtypeStruct((1, LANE), F32), jax.ShapeDtypeStruct((1, LANE), F32)],
        scratch_shapes=[pltpu.VMEM((di, n), F32), pltpu.VMEM((q, di), F32), pltpu.VMEM((q, di), F32)],
        compiler_params=_cparams(("arbitrary", "arbitrary")))(
            xc, xc, xc, proj, dt_bias, a_log, s_prev_all, dy, dxs_skip, e_map, e_t)


BIAS_LANE = MLA_ROPE // 2
KEY_OFF = -1e30
ATT_SCALE = (MLA_NOPE + MLA_ROPE) ** -0.5


def _row_t(col):
    return jnp.broadcast_to(col, (col.shape[0], LANE)).T[:8]


def _attn_fwd2(geo, qn, qp, kn, kp, v):
    t, lp = ATT_BLK, geo.lp
    nb = lp // t

    def body(qn_ref, qp_ref, kn_ref, kp_ref, v_ref, o_ref, lse_ref, k_ref):
        qi = pl.program_id(2)

        @pl.when(qi == 0)
        def _():
            k_ref[:, :LANE] = kn_ref[...]
            k_ref[:, LANE:] = kp_ref[...]

        q = jnp.concatenate([qn_ref[...], qp_ref[...]], axis=1)

        def blk(kj, ntile, carry, diag):
            m, l, acc = carry
            ks = pl.ds(pl.multiple_of(kj * t, t), ntile * t)
            s = _mxdot(q, k_ref[ks, :], 1, 1) * ATT_SCALE
            if diag:
                s = jnp.where(_tri(t), s, -jnp.inf)
            m_new = jnp.maximum(m, jnp.max(s, axis=1, keepdims=True))
            pr = jnp.exp(s - m_new)
            alpha = jnp.exp(m - m_new)
            return m_new, alpha * l + jnp.sum(pr, axis=1, keepdims=True), alpha * acc + _mxdot(pr, v_ref[ks, :], 1, 0)

        carry = (jnp.full((t, 1), 2.0 * KEY_OFF, F32), jnp.zeros((t, 1), F32), jnp.zeros((t, LANE), F32))
        done = 0
        for ntile in (4, 2, 1):
            steps = (qi - done) // ntile
            carry = lax.fori_loop(0, steps, lambda j, c, d=done, n=ntile: blk(d + n * j, n, c, False), carry)
            done = done + steps * ntile
        m, l, acc = blk(qi, 1, carry, True)
        o_ref[...] = acc / l
        lse_ref[0, 0, 0] = _row_t(m + jnp.log(l))

    tile = pl.BlockSpec((t, LANE), lambda b, h, i: (b * nb + i, h))
    seq = pl.BlockSpec((lp, LANE), lambda b, h, i: (b, h))
    return pl.pallas_call(
        body, name="attn_fwd", grid=(geo.bsz, MLA_HEADS, nb),
        in_specs=[tile, tile, seq, pl.BlockSpec((lp, LANE), lambda b, h, i: (b, 0)), seq],
        out_specs=[tile, pl.BlockSpec((1, 1, 1, 8, t), lambda b, h, i: (b, h, i, 0, 0))],
        out_shape=[jax.ShapeDtypeStruct((geo.nrows, geo.hq), F32),
                   jax.ShapeDtypeStruct((geo.bsz, MLA_HEADS, nb, 8, t), F32)],
        scratch_shapes=[pltpu.VMEM((lp, 2 * LANE), MXU_DTYPE)],
        compiler_params=_cparams(("parallel", "parallel", "arbitrary")))(qn, qp, kn, kp, v)


def _attn_bwd2(geo, qn, qp, kn, kp, v, d_o, o, lse):
    t, lp = ATT_BLK, geo.lp
    nb = lp // t

    def body(qn_ref, qp_ref, kn_ref, kp_ref, v_ref, do_ref, o_ref, lse_ref,
             dqn_ref, dqp_ref, dkn_ref, dkp_ref, dv_ref, q_ref, dl_s):
        kj = pl.program_id(2)

        @pl.when(kj == 0)
        def _():
            q_ref[:, :LANE] = qn_ref[...]
            q_ref[:, LANE:] = qp_ref[...]
            dqn_ref[...] = jnp.zeros_like(dqn_ref)
            dqp_ref[...] = jnp.zeros_like(dqp_ref)
            for i in range(nb):
                rows = slice(i * t, (i + 1) * t)
                dl_s[i] = _row_t(jnp.sum(do_ref[rows, :] * o_ref[rows, :], axis=1, keepdims=True))

        k, vv = jnp.concatenate([kn_ref[...], kp_ref[...]], axis=1), v_ref[...]

        def row(ref, qi, ntile):
            return jnp.concatenate([ref[qi + i][:1, :] for i in range(ntile)], axis=1)

        def blk(qi, ntile, carry, diag):
            dk, dv = carry
            qs = pl.ds(pl.multiple_of(qi * t, t), ntile * t)
            q, d_o_blk = q_ref[qs, :], do_ref[qs, :]
            st = _mxdot(k, q, 1, 1) * ATT_SCALE
            if diag:
                keys = lax.broadcasted_iota(jnp.int32, (t, t), 0)
                st = jnp.where(keys <= lax.broadcasted_iota(jnp.int32, (t, t), 1), st, -jnp.inf)
            pt = jnp.exp(st - row(lse_ref.at[0, 0], qi, ntile))
            dst = pt * (_mxdot(vv, d_o_blk, 1, 1) - row(dl_s, qi, ntile)) * ATT_SCALE
            dq = _mxdot(dst, k, 0, 0)
            dqn_ref[qs, :] += dq[:, :LANE]
            dqp_ref[qs, :] += dq[:, LANE:]
            return dk + _mxdot(dst, q, 1, 0), dv + _mxdot(pt, d_o_blk, 1, 0)

        carry = blk(kj, 1, (jnp.zeros((t, 2 * LANE), F32), jnp.zeros((t, LANE), F32)), True)
        done = kj + 1
        for ntile in (4, 2, 1):
            steps = (nb - done) // ntile
            carry = lax.fori_loop(0, steps, lambda j, c, d=done, n=ntile: blk(d + n * j, n, c, False), carry)
            done = done + steps * ntile
        dk, dv = carry
        dkn_ref[...] = dk[:, :LANE].astype(dkn_ref.dtype)
        dkp_ref[...] = dk[:, LANE:]
        dv_ref[...] = dv.astype(dv_ref.dtype)

    seq = pl.BlockSpec((lp, LANE), lambda b, h, j: (b, h))
    tile = pl.BlockSpec((t, LANE), lambda b, h, j: (b * nb + j, h))
    return pl.pallas_call(
        body, name="attn_bwd", grid=(geo.bsz, MLA_HEADS, nb),
        in_specs=[seq, seq, tile, pl.BlockSpec((t, LANE), lambda b, h, j: (b * nb + j, 0)), tile, seq, seq,
                  pl.BlockSpec((1, 1, nb, 8, t), lambda b, h, j: (b, h, 0, 0, 0))],
        out_specs=[seq, seq, tile, tile, tile],
        out_shape=[jax.ShapeDtypeStruct((geo.nrows, geo.hq), F32), jax.ShapeDtypeStruct((geo.nrows, geo.hq), F32),
                   jax.ShapeDtypeStruct((geo.nrows, geo.hq), MXU_DTYPE), jax.ShapeDtypeStruct((geo.nrows, geo.hq), F32),
                   jax.ShapeDtypeStruct((geo.nrows, geo.hq), MXU_DTYPE)],
        scratch_shapes=[pltpu.VMEM((lp, 2 * LANE), MXU_DTYPE), pltpu.VMEM((nb, 8, t), F32)],
        compiler_params=_cparams(("parallel", "parallel", "arbitrary")))(qn, qp, kn, kp, v, d_o, o, lse)


def _rope(x, cos, sin):
    return x * cos + pltpu.roll(x, LANE // 2, axis=1) * sin


def _rope_t(dx, cos, sin):
    return dx * cos + pltpu.roll(dx * sin, LANE // 2, axis=1)


def _per_head(f):
    def fn(x, cos, sin):
        return (jnp.concatenate([f(x[:, h * LANE:(h + 1) * LANE], cos, sin) for h in range(MLA_HEADS)], axis=1),)
    return fn


def _layer_fwd(geo, h, w, tab, late=None):
    nr, tr, trw = geo.nrows, geo.tr, geo.tr_wide
    tb = geo.lp // tr
    rw = functools.partial(_rowwise, nrows=nr)
    s = {"h": h}
    (s["u"],) = rw("rms_mix", lambda x, g: (_rms(x, g),), tr=tr, rows=[(h, D_MODEL, 0)],
                   vecs=[(w["norm_mix_w"], D_MODEL, 0)], outs=[(D_MODEL, D_MODEL, MXU_DTYPE)])
    proj = s["proj"] = _mm("mm_in", s["u"], w["w_in_p"])
    xc = s["xc"] = _conv_fwd(geo, proj, w["conv_w"], w["conv_b"])
    s["y_ssd"], s["s_prev"] = _ssd_fwd_g(geo, xc, proj, w["dt_bias"], w["a_log"])
    gw = SSM_D_INNER // SSM_GROUPS

    def gate_norm(y, x, z, dsk, nw):
        return (_rms((y + x * dsk) * _silu(z), nw),)

    (s["y_ssm"],) = rw("ssm_gate_norm", gate_norm, tr=tr, ncb=SSM_GROUPS,
                       rows=[(s["y_ssd"], gw, 0), (xc, gw, 0), (proj, gw, geo.col["z"][0] // gw)],
                       vecs=[(w["d_skip_full"], gw, 0), (w["ssm_norm_w"], gw, 0)], outs=[(SSM_D_INNER, gw, MXU_DTYPE)])
    if late is not None:
        w = {**w, **late(s["y_ssm"])}
    (s["cq_n"],) = rw("rms_q", lambda x, g: (_rms(x, g),), tr=tr, rows=[(proj, MLA_Q_LORA, geo.cb("c_q"))],
                      vecs=[(w["q_norm_w"], MLA_Q_LORA, 0)], outs=[(MLA_Q_LORA, MLA_Q_LORA, MXU_DTYPE)])
    (s["ckv_n"],) = rw("rms_kv", lambda x, g: (_rms(x, g),), tr=tr, rows=[(proj, MLA_KV_LORA, geo.cb("c_kv"))],
                       vecs=[(w["kv_norm_w"], MLA_KV_LORA, 0)], outs=[(MLA_KV_LORA, MLA_KV_LORA, MXU_DTYPE)])
    s["qn"] = _mm("mm_qn", s["cq_n"], w["w_qn"], out_dtype=MXU_DTYPE)
    qp_raw = _mm("mm_qp", s["cq_n"], w["w_qp"])
    s["kn"] = _mm("mm_kn", s["ckv_n"], w["w_k"], out_dtype=MXU_DTYPE)
    s["v"] = _mm("mm_v", s["ckv_n"], w["w_v"], out_dtype=MXU_DTYPE)
    bias_lane = lambda: lax.broadcasted_iota(jnp.int32, (1, LANE), 1) == BIAS_LANE
    rope_tabs = [(tab["cos"], LANE, 0), (tab["sin"], LANE, 0)]
    (s["qp"],) = rw("rope_q", _per_head(lambda xp, c, sn: jnp.where(bias_lane(), 1.0, _rope(xp, c, sn))), tr=tr,
                    rows=[(qp_raw, geo.hq, 0)], tabs=rope_tabs, outs=[(geo.hq, geo.hq, MXU_DTYPE)], tab_blocks=tb)
    (s["kp"],) = rw("rope_k", lambda xp, c, sn, valid: (jnp.where(bias_lane(), KEY_OFF * (1.0 - valid), _rope(xp, c, sn)),),
                    tr=tr, rows=[(proj, LANE, geo.cb("k_rope"))], tabs=rope_tabs + [(tab["valid"], 1, 0)],
                    outs=[(LANE, LANE, MXU_DTYPE)], tab_blocks=tb)
    s["o"], s["lse"] = _attn_fwd2(geo, s["qn"], s["qp"], s["kn"], s["kp"], s["v"])
    s["ys_p"] = _mm("mm_bs", s["y_ssm"], w["w_branch_ssm"])
    s["ym_p"] = _mm("mm_bm", s["o"], w["w_branch_mla"])

    def gate(gs, gm, ys, ym):
        return (_sigmoid(gs) * ys + _sigmoid(gm) * ym,)

    (s["mixed"],) = rw("gate", gate, tr=tr, rows=[(proj, D_MODEL, geo.cb("g_ssm")), (proj, D_MODEL, geo.cb("g_mla")),
                                                  (s["ys_p"], D_MODEL, 0), (s["ym_p"], D_MODEL, 0)],
                       outs=[(D_MODEL, D_MODEL, MXU_DTYPE)])
    s["h2"] = _mm("mm_out", s["mixed"], w["w_out"], add=h)
    (s["vn"],) = rw("rms_mlp", lambda x, g: (_rms(x, g),), tr=tr, rows=[(s["h2"], D_MODEL, 0)],
                    vecs=[(w["norm_mlp_w"], D_MODEL, 0)], outs=[(D_MODEL, D_MODEL, MXU_DTYPE)])
    s["up"], s["act"] = _mm("mm_up", s["vn"], w["w_mlp_up"],
                            epi=(lambda r: (r, jnp.square(jnp.maximum(r, 0.0))), (F32, MXU_DTYPE)))
    return _mm("mm_down", s["act"], w["w_mlp_down"], add=s["h2"]), s, w


def _layer_bwd(geo, dh3, s, w, tab, mid=None, tail=None):
    nr, tr, trw = geo.nrows, geo.tr, geo.tr_wide
    tb = geo.lp // tr
    rw = functools.partial(_rowwise, nrows=nr)
    g = {}
    proj = s["proj"]

    def rms_bwd(x, dy, res, gw):
        _, vjp = jax.vjp(_rms, x.astype(F32), gw)
        dx, dgw = vjp(dy.astype(F32))
        return dx + res, dgw

    def rms_bwd_nores(x, dy, gw):
        _, vjp = jax.vjp(_rms, x.astype(F32), gw)
        return vjp(dy.astype(F32))

    (dup,) = _mm("mm_down_t", dh3, w["w_mlp_down"], tb=True, add=s["up"],
                 epi=(lambda r, up: (r * 2.0 * jnp.maximum(up, 0.0),), (MXU_DTYPE,)))
    g["w_mlp_down"] = _mm("mm_down_g", s["act"], dh3, ta=True, out_dtype=MXU_DTYPE)
    g["w_mlp_up"] = _mm("mm_up_g", s["vn"], dup, ta=True, out_dtype=MXU_DTYPE)
    dvn = _mm("mm_up_t", dup, w["w_mlp_up"], tb=True)
    dh2, g["norm_mlp_w"] = rw("rms_mlp_bwd", rms_bwd, tr=tr,
                              rows=[(s["h2"], D_MODEL, 0), (dvn, D_MODEL, 0), (dh3, D_MODEL, 0)],
                              vecs=[(w["norm_mlp_w"], D_MODEL, 0)], outs=[(D_MODEL, D_MODEL, F32)],
                              reds=[(D_MODEL, D_MODEL)])
    dmixed = _mm("mm_out_t", dh2, w["w_out"], tb=True)
    g["w_out"] = _mm("mm_out_g", s["mixed"], dh2, ta=True, out_dtype=MXU_DTYPE)

    def gate_bwd(gs, gm, ys, ym, dm):
        f = lambda a, b, c, d: _sigmoid(a) * c + _sigmoid(b) * d
        _, vjp = jax.vjp(f, gs, gm, ys, ym)
        dgs, dgm, dys, dym = vjp(dm)
        return dys, dym, dgs, dgm

    dys_p, dym_p, dg_ssm, dg_mla = rw(
        "gate_bwd", gate_bwd, tr=tr,
        rows=[(proj, D_MODEL, geo.cb("g_ssm")), (proj, D_MODEL, geo.cb("g_mla")), (s["ys_p"], D_MODEL, 0),
              (s["ym_p"], D_MODEL, 0), (dmixed, D_MODEL, 0)], outs=[(D_MODEL, D_MODEL, MXU_DTYPE)] * 4)
    g["w_branch_ssm"] = _mm("mm_bs_g", s["y_ssm"], dys_p, ta=True, out_dtype=MXU_DTYPE)
    dy_ssm = _mm("mm_bs_t", dys_p, w["w_branch_ssm"], tb=True)
    g["w_branch_mla"] = _mm("mm_bm_g", s["o"], dym_p, ta=True, out_dtype=MXU_DTYPE)
    d_o = _mm("mm_bm_t", dym_p, w["w_branch_mla"], tb=True)
    dqn, dqp, dkn, dkp_h, dv = _attn_bwd2(geo, s["qn"], s["qp"], s["kn"], s["kp"], s["v"], d_o, s["o"], s["lse"])
    rope_tabs = [(tab["cos"], LANE, 0), (tab["sin"], LANE, 0)]
    (dqp_raw,) = rw("rope_q_bwd", _per_head(_rope_t), tr=tr, rows=[(dqp, geo.hq, 0)], tabs=rope_tabs,
                    outs=[(geo.hq, geo.hq, MXU_DTYPE)], tab_blocks=tb)

    def rope_k_bwd(x, c, sn):
        tot = x[:, :LANE]
        for hd in range(1, MLA_HEADS):
            tot = tot + x[:, hd * LANE:(hd + 1) * LANE]
        return (_rope_t(tot, c, sn),)

    (dk_rope,) = rw("rope_k_bwd", rope_k_bwd, tr=tr, rows=[(dkp_h, geo.hq, 0)], tabs=rope_tabs,
                    outs=[(LANE, LANE, MXU_DTYPE)], tab_blocks=tb)
    g["w_qn"] = _mm("mm_qn_g", s["cq_n"], dqn, ta=True, out_dtype=MXU_DTYPE)
    g["w_qp"] = _mm("mm_qp_g", s["cq_n"], dqp_raw, ta=True, out_dtype=MXU_DTYPE)
    dcq_n = _mm("mm_qp_t", dqp_raw, w["w_qp"], tb=True, add=_mm("mm_qn_t", dqn, w["w_qn"], tb=True))
    g["w_k"] = _mm("mm_kn_g", s["ckv_n"], dkn, ta=True, out_dtype=MXU_DTYPE)
    g["w_v"] = _mm("mm_v_g", s["ckv_n"], dv, ta=True, out_dtype=MXU_DTYPE)
    dckv_n = _mm("mm_v_t", dv, w["w_v"], tb=True, add=_mm("mm_kn_t", dkn, w["w_k"], tb=True))
    dc_q, g["q_norm_w"] = rw("rms_q_bwd", rms_bwd_nores, tr=tr,
                             rows=[(proj, MLA_Q_LORA, geo.cb("c_q")), (dcq_n, MLA_Q_LORA, 0)],
                             vecs=[(w["q_norm_w"], MLA_Q_LORA, 0)], outs=[(MLA_Q_LORA, MLA_Q_LORA, MXU_DTYPE)],
                             reds=[(MLA_Q_LORA, MLA_Q_LORA)])
    dc_kv, g["kv_norm_w"] = rw("rms_kv_bwd", rms_bwd_nores, tr=tr,
                               rows=[(proj, MLA_KV_LORA, geo.cb("c_kv")), (dckv_n, MLA_KV_LORA, 0)],
                               vecs=[(w["kv_norm_w"], MLA_KV_LORA, 0)], outs=[(MLA_KV_LORA, MLA_KV_LORA, MXU_DTYPE)],
                               reds=[(MLA_KV_LORA, MLA_KV_LORA)])
    gw_ = SSM_D_INNER // SSM_GROUPS
    d_skip_full = w["d_skip_full"] if mid is None else w["d_skip_full"] + mid(g)[0, 0]

    def gate_norm_bwd(y, x, z, dy, dsk, nw):
        f = lambda y_, x_, z_, dsk_, nw_: _rms((y_ + x_ * dsk_) * _silu(z_), nw_)
        _, vjp = jax.vjp(f, y, x, z, dsk, nw)
        dy_, dx_, dz_, ddsk, dnw = vjp(dy)
        return dy_, dx_, dz_, ddsk, dnw

    dy_ssd, dxs_skip, dz, g["d_skip_full"], g["ssm_norm_w"] = rw(
        "ssm_gate_norm_bwd", gate_norm_bwd, tr=tr, ncb=SSM_GROUPS,
        rows=[(s["y_ssd"], gw_, 0), (s["xc"], gw_, 0), (proj, gw_, geo.col["z"][0] // gw_), (dy_ssm, gw_, 0)],
        vecs=[(d_skip_full, gw_, 0), (w["ssm_norm_w"], gw_, 0)],
        outs=[(SSM_D_INNER, gw_, F32), (SSM_D_INNER, gw_, F32), (SSM_D_INNER, gw_, MXU_DTYPE)],
        reds=[(SSM_D_INNER, gw_), (SSM_D_INNER, gw_)])
    dxc, ddt, g["dt_bias"], g["a_log"] = _ssd_bwd_g(geo, s["xc"], proj, w["dt_bias"], w["a_log"], s["s_prev"],
                                                   dy_ssd, dxs_skip)
    dxbc, g["conv_w"], g["conv_b"] = _conv_bwd(geo, proj, w["conv_w"], w["conv_b"], dxc)
    di, gn = SSM_D_INNER, geo.gn
    dproj = jnp.concatenate([dz, dxbc[:, :di], dg_ssm, dg_mla, dxbc[:, di:di + gn], dxbc[:, di + gn:], dc_q, dc_kv,
                             ddt, dk_rope], axis=-1)
    g["w_in_p"] = _mm("mm_in_g", s["u"], dproj, ta=True, out_dtype=MXU_DTYPE)
    du = _mm("mm_in_t", dproj, w["w_in_p"], tb=True, dep=None if tail is None else tail(g))
    dh, g["norm_mix_w"] = rw("rms_mix_bwd", rms_bwd, tr=tr,
                             rows=[(s["h"], D_MODEL, 0), (du, D_MODEL, 0), (dh2, D_MODEL, 0)],
                             vecs=[(w["norm_mix_w"], D_MODEL, 0)], outs=[(D_MODEL, D_MODEL, F32)],
                             reds=[(D_MODEL, D_MODEL)])
    return dh, g


def _loss_bwd(geo, h, fw, target, tab):
    tr = geo.tr

    def fn(x, tgt, gw, tok):
        def lossf(x_, gw_):
            err = jnp.square(_rms(x_, gw_) - tgt)
            return 0.5 * jnp.sum(tok * jnp.mean(err, axis=-1, keepdims=True), axis=0, keepdims=True)

        val, vjp = jax.vjp(lossf, x, gw)
        dx, dgw = vjp(jnp.ones((1, 1), F32))
        return dx, jnp.broadcast_to(val, (1, LANE)), dgw

    return _rowwise("loss", fn, nrows=geo.nrows, tr=tr, rows=[(h, D_MODEL, 0), (target, D_MODEL, 0)],
                    vecs=[(fw, D_MODEL, 0)], tabs=[(tab["token"], 1, 0)], outs=[(D_MODEL, D_MODEL, F32)],
                    reds=[(LANE, LANE), (D_MODEL, D_MODEL)], tab_blocks=geo.lp // tr)


def kernel(x, meta_tokens, norm_mix_w, w_in, conv_w, conv_b, dt_bias, a_log, d_skip, ssm_norm_w, q_norm_w, kv_norm_w, w_uq, w_ukv, w_branch_ssm, w_branch_mla, w_out, norm_mlp_w, w_mlp_up, w_mlp_down, final_norm_w, loss_target, m_meta_tokens, m_norm_mix_w, m_w_in, m_conv_w, m_conv_b, m_dt_bias, m_a_log, m_d_skip, m_ssm_norm_w, m_q_norm_w, m_kv_norm_w, m_w_uq, m_w_ukv, m_w_branch_ssm, m_w_branch_mla, m_w_out, m_norm_mlp_w, m_w_mlp_up, m_w_mlp_down, m_final_norm_w, v_meta_tokens, v_norm_mix_w, v_w_in, v_conv_w, v_conv_b, v_dt_bias, v_a_log, v_d_skip, v_ssm_norm_w, v_q_norm_w, v_kv_norm_w, v_w_uq, v_w_ukv, v_w_branch_ssm, v_w_branch_mla, v_w_out, v_norm_mlp_w, v_w_mlp_up, v_w_mlp_down, v_final_norm_w):
    args = dict(locals())
    wts = {n: args[n] for n in WEIGHTS}
    mom = {n: args["m_" + n] for n in WEIGHTS}
    var = {n: args["v_" + n] for n in WEIGHTS}
    bsz, seq, _ = x.shape
    depth = w_in.shape[0]
    geo = _Geo(bsz, seq)
    tab = _tables(geo)

    big_names = [n for n, _ in BIG]
    sh_names = big_names + [n for n, _ in SHARDED_F32]
    kinds = dict(BIG + SHARDED_F32)
    shard3 = lambda a: a.reshape((1,) + a.shape) if a.ndim == 2 else a
    wire = {n: (MXU_DTYPE if n in big_names else F32) for n in sh_names}
    cast = {n: shard3(wts[n]).astype(wire[n]) for n in sh_names}
    per_layer = [n for n in sh_names if n != "meta_tokens"]
    small_names = ["norm_mix_w", "conv_b", "dt_bias", "a_log", "d_skip", "ssm_norm_w", "q_norm_w", "kv_norm_w",
                   "norm_mlp_w"]

    def gather_items(pairs):
        ins, outs, items, forms = [], [], [], []
        for n, i in pairs:
            a, b = cast[n].shape[1:]
            shape, dst, form = _gather_plan(a, b, kinds[n])
            items.append((len(ins), len(outs), (lambda ref, p, i=i: ref.at[i]), dst))
            ins.append(cast[n])
            outs.append(jax.ShapeDtypeStruct(shape, wire[n]))
            forms.append(form)
        return ins, outs, items, forms

    def whole_weights(pairs, forms, got):
        by_layer = {}
        for (n, i), form, g in zip(pairs, forms, got):
            if n == "w_in":
                n, g = "w_in_p", _w_in_assemble(geo, g)
            elif form == "row":
                g = g.reshape(g.shape[0] * g.shape[1], g.shape[2])
            elif form == "stack":
                g = _unshard(g, "col")
            by_layer.setdefault(i, {})[n] = g
        return by_layer

    def prep(i, whole, token=None):
        wl = dict(whole)
        wl.update({n: wts[n][i] for n in small_names})
        if token is not None:
            wl["norm_mix_w"] = wl["norm_mix_w"] + token[0, 0]
        return _prep_layer(geo, wl)

    early = ("w_in", "conv_w")
    late_names = [n for n in per_layer if n not in early]
    pairs1 = [(n, i) for i in range(1, depth) for n in per_layer]
    groups = [[(n, 0) for n in early] + [("meta_tokens", 0)], [(n, 0) for n in late_names]] + ([pairs1] if pairs1 else [])
    started = {}

    def gather_start(gi, dep=None):
        ins, outs, items, forms = gather_items(groups[gi])
        sems, thru, landing, token = _exchange_start("gather_w%d_start" % gi, ins, outs, items, dep)
        started[gi] = (groups[gi], forms, sems, thru, landing, items)
        return token

    def gathered(gi, after):
        pairs, forms, sems, thru, landing, items = started[gi]
        return whole_weights(pairs, forms, _exchange_wait("gather_w%d_wait" % gi, sems, thru, landing, items, after))

    def late0(after):
        whole = gathered(1, after)[0]
        if pairs1:
            whole["q_norm_w"] = wts["q_norm_w"][0] + gather_start(2, whole["w_out"])[0, 0]
        return _prep_layer(geo, whole)

    token = gather_start(1, gather_start(0))
    whole0 = gathered(0, token)[0]
    meta_full = whole0.pop("meta_tokens")

    meta = jnp.broadcast_to(meta_full[None], (bsz, N_META, D_MODEL))
    h = jnp.concatenate([jnp.zeros((bsz, geo.pad, D_MODEL), F32), meta, x], axis=1).reshape(geo.nrows, D_MODEL)
    target = jnp.concatenate([jnp.zeros((bsz, geo.pad + N_META, D_MODEL), F32), loss_target], axis=1)
    target = target.reshape(geo.nrows, D_MODEL)
    layers, saved = [], []
    for i in range(depth):
        if i == 0:
            w, late = prep(0, whole0, token), late0
        else:
            if i == 1:
                whole1 = gathered(2, h)
            w, late = prep(i, whole1[i]), None
        h, s, w = _layer_fwd(geo, h, w, tab, late)
        layers.append(w)
        saved.append(s)
    dh, loss_part, g_final = _loss_bwd(geo, h, final_norm_w.reshape(1, -1), target, tab)

    def scatter_items(pairs):
        ins, outs, items = [], [], []
        for n, i in pairs:
            a, b = cast[n].shape[1:]
            arr = g_meta if n == "meta_tokens" else grads[i]["w_in_p" if n == "w_in" else n]
            if n == "w_in":
                arr, src = _w_in_split(geo, arr, b), _entry
            elif kinds[n] == "row":
                src = lambda ref, p, a=a: ref.at[pl.ds(pl.multiple_of(p * a, a), a)]
            elif b % LANE == 0:
                src = lambda ref, p, b=b: ref.at[:, pl.ds(pl.multiple_of(p * b, b), b)]
            else:
                arr, src = _shard(arr, "col"), _entry
            items.append((len(ins), len(outs), src, _entry))
            ins.append(arr.astype(wire[n]))
            outs.append(jax.ShapeDtypeStruct((N_DEV, a, b), wire[n]))
        return ins, outs, items

    grads = [None] * depth
    landed, pending, res = {}, {}, {}

    def scatter_start(name, pairs):
        ins, outs, items = scatter_items(pairs)
        sems, thru, landing, token = _exchange_start(name + "_start", ins, outs, items)
        pending[name] = (pairs, sems, thru, landing, items)
        return token

    def scatter_wait(name, after):
        pairs, sems, thru, landing, items = pending[name]
        landed.update(zip(pairs, _exchange_wait(name + "_wait", sems, thru, landing, items, after)))

    def adam(n):
        parts = [landed[(n, i)] for i in range(cast[n].shape[0])]
        r = _adamw_nat("adamw_" + n, parts, shard3(wts[n]), shard3(mom[n]), shard3(var[n]))
        res[n] = [a.reshape(wts[n].shape) for a in r]

    def mid0(g):
        grads[0] = _unprep_grads(geo, g)
        return scatter_start("scatter_gb0", [(n, 0) for n in late_names])

    def tail0(g):
        grads[0] = _unprep_grads(geo, g)
        return scatter_start("scatter_ga0", [(n, 0) for n in early])

    for i in reversed(range(depth)):
        dh, gl = _layer_bwd(geo, dh, saved[i], layers[i], tab, *((mid0, tail0) if i == 0 else ()))
        grads[i] = _unprep_grads(geo, gl)
        if i == 1:
            dh = dh + scatter_start("scatter_g1", pairs1)[0, 0]
    dh = dh.reshape(bsz, geo.lp, D_MODEL)
    grad_x = dh[:, geo.pad + N_META:]
    g_meta = jnp.sum(dh[:, geo.pad:geo.pad + N_META], axis=0)
    if pairs1:
        scatter_wait("scatter_g1", g_meta)
    scatter_wait("scatter_gb0", g_meta)
    for n in late_names:
        adam(n)
    g_small = {n: jnp.stack([grads[i][n] for i in range(depth)]) for n in SMALL if n != "final_norm_w"}
    g_small["final_norm_w"] = g_final.reshape(-1)
    zero = jnp.zeros((1,), F32)
    pk = lambda d, last: _pack([d[n] for n in SMALL] + [last], F32, row_mult=8)
    packed = pk(g_small, loss_part[0, :1])
    ins, outs, items = scatter_items([("meta_tokens", 0)])
    parts, landed[("meta_tokens", 0)] = _exchange(
        "gather_g", [packed] + ins, [jax.ShapeDtypeStruct((N_DEV,) + packed.shape, F32)] + outs,
        [(0, 0, _whole, _entry)] + [(1, 1, items[0][2], items[0][3])])
    adam("meta_tokens")
    scatter_wait("scatter_ga0", res["meta_tokens"][1])
    for n in early:
        adam(n)
    res_sm = _adamw("adamw_small", parts, pk(wts, zero), pk(mom, zero), pk(var, zero))
    res_sm = [_unpack(r, [wts[n].shape for n in SMALL] + [(1,)]) for r in res_sm]
    loss = res_sm[0][-1][0]

    out = [loss, grad_x]
    for k in range(4):
        named = {n: res[n][k] for n in sh_names}
        named.update(zip(SMALL, res_sm[k]))
        out += [named[n] for n in WEIGHTS]
    return tuple(out)
```

```python
import functools

import numpy as np
import jax
import jax.numpy as jnp
from jax import lax
from jax.experimental import pallas as pl
from jax.experimental.pallas import tpu as pltpu

F32 = jnp.float32
MXU_DTYPE = jnp.bfloat16

D_MODEL = 1024
N_META = 16
EPS = 1e-6
SSM_D_INNER = 2048
SSM_HEAD_DIM = 64
SSM_GROUPS = 4
SSM_STATE = 128
SSM_CONV = 4
SSM_CHUNK = 128
MLA_HEADS = 8
MLA_Q_LORA = 512
MLA_KV_LORA = 256
MLA_NOPE = 128
MLA_ROPE = 64
MLA_V = 128
ROPE_THETA = 10000.0
D_FF = 4096
ADAM_LR = 0.001
ADAM_B1 = 0.9
ADAM_B2 = 0.999
ADAM_EPS = 1e-08
ADAM_WD = 0.01
ADAM_STEP = 10

N_DEV = 8
ATT_BLK = 256
LANE = 128
PACK_W = 1024
VMEM_LIMIT = 56 * 1024 * 1024
MESH_ID = pl.DeviceIdType.MESH

BIG = (("w_in", "col"), ("w_uq", "col"), ("w_ukv", "col"), ("w_branch_ssm", "row"), ("w_branch_mla", "row"),
       ("w_out", "row"), ("w_mlp_up", "col"), ("w_mlp_down", "row"))
SHARDED_F32 = (("conv_w", "col"), ("meta_tokens", "col"))
SMALL = ("norm_mix_w", "conv_b", "dt_bias", "a_log", "d_skip", "ssm_norm_w", "q_norm_w", "kv_norm_w",
         "norm_mlp_w", "final_norm_w")
WEIGHTS = ("meta_tokens", "norm_mix_w", "w_in", "conv_w", "conv_b", "dt_bias", "a_log", "d_skip", "ssm_norm_w",
           "q_norm_w", "kv_norm_w", "w_uq", "w_ukv", "w_branch_ssm", "w_branch_mla", "w_out", "norm_mlp_w",
           "w_mlp_up", "w_mlp_down", "final_norm_w")


def _cparams(sem=None):
    return pltpu.CompilerParams(dimension_semantics=sem, vmem_limit_bytes=VMEM_LIMIT)


def _pick(n, cands):
    for c in cands:
        if n % c == 0:
            return c
    return n


def _sigmoid(x):
    return 1.0 / (1.0 + jnp.exp(-x))


def _silu(x):
    return x * _sigmoid(x)


def _softplus(x):
    return jnp.maximum(x, 0.0) + jnp.log1p(jnp.exp(-jnp.abs(x)))


def _rms(x, w):
    return x * lax.rsqrt(jnp.mean(x * x, axis=-1, keepdims=True) + EPS) * w


def _dot(a, b, ca, cb, precision=None):
    return lax.dot_general(a, b, (((ca,), (cb,)), ((), ())), preferred_element_type=F32, precision=precision)


def _mxdot(a, b, ca, cb):
    return _dot(a.astype(MXU_DTYPE), b.astype(MXU_DTYPE), ca, cb)


def _mm(name, a, b, *, ta=False, tb=False, add=None, out_dtype=F32, dep=None, epi=None):
    (kdim, m) = a.shape if ta else a.shape[::-1]
    (n, k2) = b.shape if tb else b.shape[::-1]
    assert kdim == k2, (name, a.shape, b.shape)
    tm = _pick(m, (1152, 1024, 768, 512, 384, 256, 128))
    tn = _pick(n, (1024, 512, 384, 256, 128))
    tk = _pick(kdim, (1152, 1024, 768, 512, 384, 256, 128))
    nk = kdim // tk
    a_spec = pl.BlockSpec((tk, tm), lambda i, j, k: (k, i)) if ta else pl.BlockSpec((tm, tk), lambda i, j, k: (i, k))
    b_spec = pl.BlockSpec((tn, tk), lambda i, j, k: (j, k)) if tb else pl.BlockSpec((tk, tn), lambda i, j, k: (k, j))
    o_spec = pl.BlockSpec((tm, tn), lambda i, j, k: (i, j))
    ca, cb = (0 if ta else 1), (1 if tb else 0)

    out_dtypes = [out_dtype] if epi is None else list(epi[1])
    n_out = len(out_dtypes)

    def body(*refs):
        a_ref, b_ref = refs[:2]
        o_refs, acc = refs[-1 - n_out:-1], refs[-1]
        k = pl.program_id(2)

        @pl.when(k == 0)
        def _():
            acc[...] = jnp.zeros_like(acc)

        acc[...] += _mxdot(a_ref[...], b_ref[...], ca, cb)

        @pl.when(k == nk - 1)
        def _():
            r = acc[...]
            if epi is not None:
                res = epi[0](r, refs[2][...]) if add is not None else epi[0](r)
            else:
                res = (r + refs[2][...].astype(F32) if add is not None else r,)
            for o_ref, val in zip(o_refs, res):
                o_ref[...] = val.astype(o_ref.dtype)

    in_specs, args = [a_spec, b_spec], [a, b]
    if add is not None:
        in_specs.append(o_spec)
        args.append(add)
    if dep is not None:
        in_specs.append(pl.BlockSpec((8, LANE), lambda i, j, k: (0, 0)))
        args.append(dep)
    res = pl.pallas_call(
        body, name=name, grid=(m // tm, n // tn, nk), in_specs=in_specs, out_specs=[o_spec] * n_out,
        out_shape=[jax.ShapeDtypeStruct((m, n), dt) for dt in out_dtypes], scratch_shapes=[pltpu.VMEM((tm, tn), F32)],
        compiler_params=_cparams(("parallel", "parallel", "arbitrary")))(*args)
    return res[0] if epi is None else res


def _rowwise(name, fn, *, nrows, tr, ncb=1, rows=(), fixed=(), vecs=(), tabs=(), outs=(), reds=(), tab_blocks=1):
    in_specs, args = [], []
    for arr, w, c0 in rows:
        in_specs.append(pl.BlockSpec((tr, w), lambda g, i, c0=c0: (i, c0 + g)))
        args.append(arr)
    for arr, w, c0 in fixed:
        in_specs.append(pl.BlockSpec((tr, w), lambda g, i, c0=c0: (i, c0)))
        args.append(arr)
    for arr, w, c0 in vecs:
        in_specs.append(pl.BlockSpec((1, w), lambda g, i, c0=c0: (0, c0 + g)))
        args.append(arr)
    for arr, w, c0 in tabs:
        in_specs.append(pl.BlockSpec((tr, w), lambda g, i, c0=c0: (i % tab_blocks, c0)))
        args.append(arr)
    n_in, n_out = len(args), len(outs)
    out_shape, out_specs, aliases = [], [], {}
    for k, o in enumerate(outs):
        c0 = o[3] if len(o) > 3 else 0
        out_shape.append(jax.ShapeDtypeStruct((nrows, o[0]), o[2]))
        out_specs.append(pl.BlockSpec((tr, o[1]), lambda g, i, c0=c0: (i, c0 + g)))
        if len(o) > 4:
            aliases[len(args)] = k
            in_specs.append(pl.BlockSpec(memory_space=pl.ANY))
            args.append(o[4])
    out_shape += [jax.ShapeDtypeStruct((1, wt), F32) for wt, w in reds]
    out_specs += [pl.BlockSpec((1, w), lambda g, i: (0, g)) for wt, w in reds]
    first_out = len(args)

    def body(*refs):
        res = fn(*[r[...] for r in refs[:n_in]])
        for o_ref, val in zip(refs[first_out:first_out + n_out], res[:n_out]):
            o_ref[...] = val.astype(o_ref.dtype)
        i = pl.program_id(1)
        for d_ref, val in zip(refs[first_out + n_out:], res[n_out:]):
            @pl.when(i == 0)
            def _(d_ref=d_ref, val=val):
                d_ref[...] = val

            @pl.when(i > 0)
            def _(d_ref=d_ref, val=val):
                d_ref[...] += val

    return pl.pallas_call(
        body, name=name, grid=(ncb, nrows // tr), in_specs=in_specs, out_specs=out_specs, out_shape=out_shape,
        input_output_aliases=aliases, compiler_params=_cparams(("parallel", "arbitrary")))(*args)


def _peer(k):
    x, y, c = lax.axis_index("x"), lax.axis_index("y"), lax.axis_index("c")
    px = jnp.where((k >> 2) & 1, 1 - x, x)
    py = jnp.where((k >> 1) & 1, 1 - y, y)
    pc = jnp.where(k & 1, 1 - c, c)
    return (px, py, pc), 4 * px + 2 * py + pc


def _my_index():
    return 4 * lax.axis_index("x") + 2 * lax.axis_index("y") + lax.axis_index("c")


def _exchange(name, ins, out_shapes, items):
    n_in, n_out, n_it = len(ins), len(out_shapes), len(items)

    def body(*refs):
        x, o = refs[:n_in], refs[n_in:n_in + n_out]
        send_sems, recv_sems, local_sems = refs[n_in + n_out:]
        me = _my_index()
        local, sends = [], []
        for t, (ii, io, src, dst) in enumerate(items):
            cp = pltpu.make_async_copy(src(x[ii], me), dst(o[io], me), local_sems.at[t])
            cp.start()
            local.append(cp)
        for k in range(1, N_DEV):
            dev, idx = _peer(k)
            for t, (ii, io, src, dst) in enumerate(items):
                s = (k - 1) * n_it + t
                cp = pltpu.make_async_remote_copy(
                    src_ref=src(x[ii], idx), dst_ref=dst(o[io], me), send_sem=send_sems.at[s],
                    recv_sem=recv_sems.at[s], device_id=dev, device_id_type=MESH_ID)
                cp.start()
                sends.append(cp)
        for k in range(1, N_DEV):
            dev, idx = _peer(k)
            for t, (ii, io, src, dst) in enumerate(items):
                s = (k - 1) * n_it + t
                pltpu.make_async_remote_copy(
                    src_ref=src(x[ii], idx), dst_ref=dst(o[io], idx), send_sem=send_sems.at[s],
                    recv_sem=recv_sems.at[s], device_id=dev, device_id_type=MESH_ID).wait_recv()
        for cp in sends:
            cp.wait_send()
        for cp in local:
            cp.wait()

    nsem = (N_DEV - 1) * n_it
    anyspec = pl.BlockSpec(memory_space=pl.ANY)
    return pl.pallas_call(
        body, name=name, out_shape=list(out_shapes), in_specs=[anyspec] * n_in, out_specs=[anyspec] * n_out,
        scratch_shapes=[pltpu.SemaphoreType.DMA((nsem,)), pltpu.SemaphoreType.DMA((nsem,)),
                        pltpu.SemaphoreType.DMA((n_it,))],
        compiler_params=pltpu.CompilerParams(has_side_effects=True))(*ins)


def _split_copies(x, land, send_sems, recv_sems, items, receive):
    me = _my_index()
    remote, n_it = [], len(items)
    for k in range(1, N_DEV):
        dev, idx = _peer(k)
        for t, (ii, io, src, dst) in enumerate(items):
            s = (k - 1) * n_it + t
            remote.append(pltpu.make_async_remote_copy(
                src_ref=src(x[ii], idx), dst_ref=dst(land[io], idx if receive else me), send_sem=send_sems.at[s],
                recv_sem=recv_sems.at[s], device_id=dev, device_id_type=MESH_ID))
    local = [pltpu.make_async_copy(src(x[ii], me), dst(land[io], me), send_sems.at[(N_DEV - 1) * n_it + t])
             for t, (ii, io, src, dst) in enumerate(items)]
    return remote, local


def _exchange_start(name, ins, out_shapes, items, dep=None):
    n_in, n_out, n_it = len(ins), len(out_shapes), len(items)

    def body(*refs):
        x, land = refs[:n_in], refs[n_in:n_in + n_out]
        first_out = n_in + n_out + (dep is not None)
        send_sems, recv_sems, token = refs[first_out], refs[first_out + 1], refs[-1]
        remote, local = _split_copies(x, land, send_sems, recv_sems, items, False)
        for cp in remote + local:
            cp.start()
        token[...] = jnp.zeros_like(token)

    hbm = pl.BlockSpec(memory_space=pltpu.HBM)
    sem = pl.BlockSpec(memory_space=pltpu.SEMAPHORE)
    arrs = [pltpu.with_memory_space_constraint(a, pltpu.HBM)
            for a in list(ins) + [lax.empty(s.shape, s.dtype) for s in out_shapes]]
    res = pl.pallas_call(
        body, name=name,
        out_shape=(pltpu.SemaphoreType.DMA((N_DEV * n_it,)), pltpu.SemaphoreType.DMA(((N_DEV - 1) * n_it,)),
                   *[pltpu.HBM(a.shape, a.dtype) for a in arrs], jax.ShapeDtypeStruct((8, LANE), F32)),
        in_specs=[hbm] * (n_in + n_out) + ([] if dep is None else [pl.BlockSpec(memory_space=pl.ANY)]),
        out_specs=(sem, sem, *[hbm] * (n_in + n_out), pl.BlockSpec(memory_space=pltpu.VMEM)),
        input_output_aliases={i: 2 + i for i in range(n_in + n_out)},
        compiler_params=pltpu.CompilerParams(has_side_effects=pltpu.SideEffectType.DATAFLOW_SIDE_EFFECTING))(
            *arrs, *([] if dep is None else [dep]))
    return res[:2], res[2:2 + n_in], res[2 + n_in:2 + n_in + n_out], res[-1]


def _exchange_wait(name, sems, ins, landing, items, after):
    n_in, n_out = len(ins), len(landing)

    def body(*refs):
        x, land = refs[:n_in], refs[n_in:n_in + n_out]
        send_sems, recv_sems = refs[n_in + n_out], refs[n_in + n_out + 1]
        remote, local = _split_copies(x, land, send_sems, recv_sems, items, True)
        for cp in remote:
            cp.wait_send()
            cp.wait_recv()
        for cp in local:
            cp.wait()

    hbm = pl.BlockSpec(memory_space=pltpu.HBM)
    sem = pl.BlockSpec(memory_space=pltpu.SEMAPHORE)
    arrs = list(ins) + list(landing)
    res = pl.pallas_call(
        body, name=name, out_shape=tuple(pltpu.HBM(a.shape, a.dtype) for a in arrs),
        in_specs=[hbm] * (n_in + n_out) + [sem, sem, pl.BlockSpec(memory_space=pl.ANY)],
        out_specs=tuple([hbm] * (n_in + n_out)), input_output_aliases={i: i for i in range(n_in + n_out)},
        compiler_params=pltpu.CompilerParams(has_side_effects=pltpu.SideEffectType.DATAFLOW_SIDE_EFFECTING))(
            *arrs, *sems, after)
    return res[n_in:]


def _whole(ref, p):
    return ref


def _entry(ref, p):
    return ref.at[p]


def _gather_plan(a, b, kind):
    if kind == "col" and b % LANE == 0:
        return (a, N_DEV * b), (lambda ref, p: ref.at[:, pl.ds(pl.multiple_of(p * b, b), b)]), "col"
    return (N_DEV, a, b), _entry, ("row" if kind == "row" else "stack")


def _adamw_nat(name, parts, w, m, v):
    depth, b, c = w.shape
    assert len(parts) == depth
    tb = _pick(b, (128, 64, 32, 16, 8))
    spec = pl.BlockSpec((1, tb, c), lambda i, j: (i, j, 0))

    def body(*refs):
        p_refs = refs[:depth]
        w_ref, m_ref, v_ref, g_ref, d_ref, nm_ref, nv_ref = refs[depth:]
        for layer, p_ref in enumerate(p_refs):
            @pl.when(pl.program_id(0) == layer)
            def _(p_ref=p_ref):
                g = p_ref[0].astype(F32)
                for j in range(1, N_DEV):
                    g = g + p_ref[j].astype(F32)
                nm = ADAM_B1 * m_ref[0] + (1.0 - ADAM_B1) * g
                nv = ADAM_B2 * v_ref[0] + (1.0 - ADAM_B2) * jnp.square(g)
                m_hat = nm / (1.0 - ADAM_B1 ** ADAM_STEP)
                v_hat = nv / (1.0 - ADAM_B2 ** ADAM_STEP)
                g_ref[0] = g
                d_ref[0] = -ADAM_LR * (m_hat / (jnp.sqrt(v_hat) + ADAM_EPS) + ADAM_WD * w_ref[0])
                nm_ref[0] = nm
                nv_ref[0] = nv

    sds = jax.ShapeDtypeStruct((depth, b, c), F32)
    return pl.pallas_call(
        body, name=name, grid=(depth, b // tb),
        in_specs=[pl.BlockSpec((N_DEV, tb, c), lambda i, j: (0, j, 0))] * depth + [spec, spec, spec],
        out_specs=[spec] * 4, out_shape=[sds] * 4, compiler_params=_cparams(("parallel", "parallel")))(*parts, w, m, v)


def _adamw(name, parts, w, m, v):
    rows = w.shape[0]
    tr = _pick(rows, (256, 128, 64, 32, 16, 8))
    spec = pl.BlockSpec((tr, PACK_W), lambda i: (i, 0))

    def body(p_ref, w_ref, m_ref, v_ref, g_ref, d_ref, nm_ref, nv_ref):
        g = p_ref[0]
        for j in range(1, N_DEV):
            g = g + p_ref[j]
        nm = ADAM_B1 * m_ref[...] + (1.0 - ADAM_B1) * g
        nv = ADAM_B2 * v_ref[...] + (1.0 - ADAM_B2) * jnp.square(g)
        m_hat = nm / (1.0 - ADAM_B1 ** ADAM_STEP)
        v_hat = nv / (1.0 - ADAM_B2 ** ADAM_STEP)
        g_ref[...] = g
        d_ref[...] = -ADAM_LR * (m_hat / (jnp.sqrt(v_hat) + ADAM_EPS) + ADAM_WD * w_ref[...])
        nm_ref[...] = nm
        nv_ref[...] = nv

    sds = jax.ShapeDtypeStruct((rows, PACK_W), F32)
    return pl.pallas_call(
        body, name=name, grid=(rows // tr,),
        in_specs=[pl.BlockSpec((N_DEV, tr, PACK_W), lambda i: (0, i, 0)), spec, spec, spec],
        out_specs=[spec] * 4, out_shape=[sds] * 4, compiler_params=_cparams(("parallel",)))(parts, w, m, v)


def _pack(arrs, dtype, row_mult=16):
    flat = jnp.concatenate([a.reshape(-1).astype(dtype) for a in arrs])
    unit = row_mult * PACK_W
    total = -(-flat.shape[0] // unit) * unit
    flat = jnp.pad(flat, (0, total - flat.shape[0]))
    return flat.reshape(-1, PACK_W)


def _pack_lead(arrs, dtype, row_mult):
    flat = jnp.concatenate([a.reshape(N_DEV, -1).astype(dtype) for a in arrs], axis=1)
    unit = row_mult * PACK_W
    total = -(-flat.shape[1] // unit) * unit
    flat = jnp.pad(flat, ((0, 0), (0, total - flat.shape[1])))
    return flat.reshape(N_DEV, -1, PACK_W)


def _unpack(buf, shapes, lead=()):
    flat = buf.reshape(lead + (-1,))
    out, off = [], 0
    for s in shapes:
        n = int(np.prod(s))
        out.append(flat[..., off:off + n].reshape(lead + tuple(s)))
        off += n
    return out


def _unshard(g, kind):
    if kind == "col":
        g = jnp.moveaxis(g, 0, -2)
        return g.reshape(g.shape[:-2] + (g.shape[-2] * g.shape[-1],))
    g = jnp.moveaxis(g, 0, 1)
    return g.reshape((g.shape[0], g.shape[1] * g.shape[2]) + g.shape[3:])


def _shard(full, kind):
    if kind == "col":
        s = full.reshape(full.shape[:-1] + (N_DEV, full.shape[-1] // N_DEV))
        return jnp.moveaxis(s, -2, 0)
    s = full.reshape((full.shape[0], N_DEV, full.shape[1] // N_DEV) + full.shape[2:])
    return jnp.moveaxis(s, 1, 0)


class _Geo:
    def __init__(self, bsz, seq):
        self.bsz, self.seq = bsz, seq
        self.pad = (-(N_META + seq)) % ATT_BLK
        self.lp = self.pad + N_META + seq
        assert (self.pad + N_META) % SSM_CHUNK == 0 and self.lp % SSM_CHUNK == 0
        self.nrows = bsz * self.lp
        self.nc = self.lp // SSM_CHUNK
        self.nh = SSM_D_INNER // SSM_HEAD_DIM
        self.gn = SSM_GROUPS * SSM_STATE
        self.cd = SSM_D_INNER + 2 * self.gn
        self.hq = MLA_HEADS * LANE
        order = (("z", SSM_D_INNER), ("g_ssm", D_MODEL), ("g_mla", D_MODEL), ("xs", SSM_D_INNER), ("bm", self.gn),
                 ("cm", self.gn), ("c_q", MLA_Q_LORA), ("c_kv", MLA_KV_LORA), ("dt", LANE), ("k_rope", LANE))
        self.col, off = {}, 0
        for nm, w in order:
            assert off % w == 0, (nm, off, w)
            self.col[nm] = (off, w)
            off += w
        self.pw = off
        assert self.nh <= LANE and MLA_ROPE == 64 and MLA_NOPE == LANE and MLA_V == LANE
        self.tr = _pick(self.lp, (768, 512, 384, 256, 128))
        self.tr_wide = _pick(self.lp, (384, 256, 128))

    def cb(self, nm):
        off, w = self.col[nm]
        return off // w

    def w_in_runs(self, shard_w):
        nh, half = self.nh, MLA_ROPE // 2
        src, pieces = 0, []
        for nm, n in (("z", SSM_D_INNER), ("xs", SSM_D_INNER), ("bm", self.gn), ("cm", self.gn), ("dt", nh),
                      ("c_q", MLA_Q_LORA), ("c_kv", MLA_KV_LORA), ("k_rope", MLA_ROPE), ("g_ssm", D_MODEL),
                      ("g_mla", D_MODEL)):
            dst = self.col[nm][0]
            if nm == "k_rope":
                pieces += [(src, half, dst), (src + half, half, dst + 2 * half)]
            else:
                pieces.append((src, n, dst))
            src += n
        assert src == shard_w * N_DEV
        runs = []
        for a, n, dst in pieces:
            for j in range(N_DEV):
                lo, hi = max(a, j * shard_w), min(a + n, (j + 1) * shard_w)
                if lo < hi:
                    runs.append((j, lo - j * shard_w, hi - lo, dst + lo - a))
        return runs


def _slot(a):
    h = MLA_ROPE // 2
    z = jnp.zeros(a.shape[:-1] + (h,), a.dtype)
    return jnp.concatenate([a[..., :h], z, a[..., h:], z], axis=-1)


def _unslot(a):
    h = MLA_ROPE // 2
    return jnp.concatenate([a[..., :h], a[..., 2 * h:3 * h]], axis=-1)


def _prep_layer(geo, wl):
    nh = geo.nh
    p = {}
    if "w_uq" in wl:
        uq = wl["w_uq"].reshape(MLA_Q_LORA, MLA_HEADS, MLA_NOPE + MLA_ROPE)
        p["w_qn"] = uq[..., :MLA_NOPE].reshape(MLA_Q_LORA, geo.hq)
        p["w_qp"] = _slot(uq[..., MLA_NOPE:]).reshape(MLA_Q_LORA, geo.hq)
    if "w_ukv" in wl:
        ukv = wl["w_ukv"].reshape(MLA_KV_LORA, MLA_HEADS, MLA_NOPE + MLA_V)
        p["w_k"] = ukv[..., :MLA_NOPE].reshape(MLA_KV_LORA, geo.hq)
        p["w_v"] = ukv[..., MLA_NOPE:].reshape(MLA_KV_LORA, geo.hq)
    for nm in ("w_in_p", "conv_w", "w_branch_ssm", "w_branch_mla", "w_out", "w_mlp_up", "w_mlp_down"):
        if nm in wl:
            p[nm] = wl[nm]
    for nm in ("norm_mix_w", "conv_b", "ssm_norm_w", "q_norm_w", "kv_norm_w", "norm_mlp_w"):
        if nm in wl:
            p[nm] = wl[nm].reshape(1, -1)
    if "dt_bias" in wl:
        p["dt_bias"] = jnp.pad(wl["dt_bias"], (0, LANE - nh)).reshape(1, LANE)
        p["a_log"] = jnp.pad(wl["a_log"], (0, LANE - nh)).reshape(1, LANE)
        p["d_skip_full"] = jnp.repeat(wl["d_skip"], SSM_HEAD_DIM).reshape(1, SSM_D_INNER)
    return p


def _unprep_grads(geo, g):
    nh = geo.nh
    out = {}
    if "w_qn" in g:
        qn = g["w_qn"].reshape(MLA_Q_LORA, MLA_HEADS, MLA_NOPE)
        qp = _unslot(g["w_qp"].reshape(MLA_Q_LORA, MLA_HEADS, LANE))
        out["w_uq"] = jnp.concatenate([qn, qp], axis=-1).reshape(MLA_Q_LORA, -1)
    if "w_k" in g:
        wk = g["w_k"].reshape(MLA_KV_LORA, MLA_HEADS, MLA_NOPE)
        wv = g["w_v"].reshape(MLA_KV_LORA, MLA_HEADS, MLA_V)
        out["w_ukv"] = jnp.concatenate([wk, wv], axis=-1).reshape(MLA_KV_LORA, -1)
    for nm in ("w_in_p", "w_branch_ssm", "w_branch_mla", "w_out", "w_mlp_up", "w_mlp_down", "conv_w"):
        if nm in g:
            out[nm] = g[nm]
    for nm in ("norm_mix_w", "conv_b", "ssm_norm_w", "q_norm_w", "kv_norm_w", "norm_mlp_w"):
        if nm in g:
            out[nm] = g[nm].reshape(-1)
    if "dt_bias" in g:
        out["dt_bias"] = g["dt_bias"].reshape(-1)[:nh]
        out["a_log"] = g["a_log"].reshape(-1)[:nh]
        out["d_skip"] = g["d_skip_full"].reshape(nh, SSM_HEAD_DIM).sum(-1)
    return out


def _tables(geo):
    pos = jnp.arange(geo.lp, dtype=F32) - geo.pad
    inv = ROPE_THETA ** (-jnp.arange(0, MLA_ROPE, 2, dtype=F32) / MLA_ROPE)
    ang = pos[:, None] * inv[None, :]
    cos, sin = jnp.cos(ang), jnp.sin(ang)
    z = jnp.zeros_like(cos)
    rows = jnp.arange(geo.lp)[:, None]
    return {"cos": jnp.concatenate([cos, z, cos, z], axis=-1), "sin": jnp.concatenate([-sin, z, sin, z], axis=-1),
            "valid": (rows >= geo.pad).astype(F32), "token": (rows >= geo.pad + N_META).astype(F32)}


def _w_in_assemble(geo, gathered):
    _, d, sw = gathered.shape
    runs = geo.w_in_runs(sw)
    tr = _pick(d, (256, 128))

    def body(x_ref, o_ref):
        o_ref[...] = jnp.zeros_like(o_ref)
        for j, s0, n, d0 in runs:
            o_ref[:, d0:d0 + n] = x_ref[j, :, s0:s0 + n]

    return pl.pallas_call(
        body, name="w_in_assemble", grid=(d // tr,), in_specs=[pl.BlockSpec((N_DEV, tr, sw), lambda i: (0, i, 0))],
        out_specs=pl.BlockSpec((tr, geo.pw), lambda i: (i, 0)),
        out_shape=jax.ShapeDtypeStruct((d, geo.pw), gathered.dtype), compiler_params=_cparams(("parallel",)))(gathered)


def _w_in_split(geo, g_padded, sw):
    d = g_padded.shape[0]
    runs = geo.w_in_runs(sw)
    tr = _pick(d, (128,))

    def body(x_ref, o_ref):
        for j, s0, n, d0 in runs:
            o_ref[j, :, s0:s0 + n] = x_ref[:, d0:d0 + n]

    return pl.pallas_call(
        body, name="w_in_split", grid=(d // tr,), in_specs=[pl.BlockSpec((tr, geo.pw), lambda i: (i, 0))],
        out_specs=pl.BlockSpec((N_DEV, tr, sw), lambda i: (0, i, 0)),
        out_shape=jax.ShapeDtypeStruct((N_DEV, d, sw), g_padded.dtype),
        compiler_params=_cparams(("parallel",)))(g_padded)


def _conv_cols(geo, cbw):
    x0 = geo.col["xs"][0]
    assert geo.col["bm"][0] == x0 + SSM_D_INNER and geo.col["cm"][0] == geo.col["bm"][0] + geo.gn and x0 % cbw == 0
    return lambda j: x0 // cbw + j


def _conv_taps(x):
    return [pltpu.roll(x, SSM_CONV - 1 - k, axis=0) for k in range(SSM_CONV - 1)] + [x]


def _conv_pre(x, w_ref, b_ref, taps=None):
    taps = _conv_taps(x) if taps is None else taps
    acc = b_ref[...]
    for k in range(SSM_CONV):
        acc = acc + taps[k] * w_ref[k:k + 1, :]
    return acc


def _conv_fwd(geo, proj, conv_w, conv_b):
    cbw = 256
    colmap = _conv_cols(geo, cbw)
    lp, pad = geo.lp, geo.pad

    def body(x_ref, w_ref, b_ref, o_ref):
        valid = (lax.broadcasted_iota(jnp.int32, (lp, 1), 0) >= pad).astype(F32)
        o_ref[...] = _silu(_conv_pre(x_ref[...], w_ref, b_ref)) * valid

    return pl.pallas_call(
        body, name="conv_fwd", grid=(geo.bsz, geo.cd // cbw),
        in_specs=[pl.BlockSpec((lp, cbw), lambda b, j: (b, colmap(j))),
                  pl.BlockSpec((SSM_CONV, cbw), lambda b, j: (0, j)), pl.BlockSpec((1, cbw), lambda b, j: (0, j))],
        out_specs=pl.BlockSpec((lp, cbw), lambda b, j: (b, j)),
        out_shape=jax.ShapeDtypeStruct((geo.nrows, geo.cd), F32),
        compiler_params=_cparams(("parallel", "parallel")))(proj, conv_w, conv_b)


def _conv_bwd(geo, proj, conv_w, conv_b, dxc, dproj):
    cbw = 256
    colmap = _conv_cols(geo, cbw)
    lp, pad = geo.lp, geo.pad

    def body(x_ref, w_ref, b_ref, dy_ref, _, dx_ref, gw_ref, gb_ref):
        b = pl.program_id(1)
        valid = (lax.broadcasted_iota(jnp.int32, (lp, 1), 0) >= pad).astype(F32)
        taps = _conv_taps(x_ref[...])
        pre = _conv_pre(None, w_ref, b_ref, taps)
        sig = _sigmoid(pre)
        dpre = dy_ref[...] * (sig * (1.0 + pre * (1.0 - sig))) * valid
        dx = dpre * w_ref[SSM_CONV - 1:SSM_CONV, :]
        for k in range(SSM_CONV - 1):
            dx = dx + pltpu.roll(dpre, lp - (SSM_CONV - 1 - k), axis=0) * w_ref[k:k + 1, :]
        gws = [jnp.sum(dpre * taps[k], axis=0, keepdims=True) for k in range(SSM_CONV)]
        dx_ref[...] = (dx * valid).astype(dx_ref.dtype)

        @pl.when(b == 0)
        def _():
            gw_ref[...] = jnp.zeros_like(gw_ref)
            gb_ref[...] = jnp.zeros_like(gb_ref)

        for k in range(SSM_CONV):
            gw_ref[k:k + 1, :] += gws[k]
        gb_ref[...] += jnp.sum(dpre, axis=0, keepdims=True)

    return pl.pallas_call(
        body, name="conv_bwd", grid=(geo.cd // cbw, geo.bsz),
        in_specs=[pl.BlockSpec((lp, cbw), lambda j, b: (b, colmap(j))),
                  pl.BlockSpec((SSM_CONV, cbw), lambda j, b: (0, j)), pl.BlockSpec((1, cbw), lambda j, b: (0, j)),
                  pl.BlockSpec((lp, cbw), lambda j, b: (b, j)), pl.BlockSpec(memory_space=pl.ANY)],
        out_specs=[pl.BlockSpec((lp, cbw), lambda j, b: (b, colmap(j))),
                   pl.BlockSpec((SSM_CONV, cbw), lambda j, b: (0, j)), pl.BlockSpec((1, cbw), lambda j, b: (0, j))],
        out_shape=[jax.ShapeDtypeStruct(dproj.shape, dproj.dtype),
                   jax.ShapeDtypeStruct((SSM_CONV, geo.cd), F32), jax.ShapeDtypeStruct((1, geo.cd), F32)],
        input_output_aliases={4: 0},
        compiler_params=_cparams(("parallel", "arbitrary")))(proj, conv_w, conv_b, dxc, dproj)


def _tri(q):
    r = lax.broadcasted_iota(jnp.int32, (q, q), 0)
    c = lax.broadcasted_iota(jnp.int32, (q, q), 1)
    return r >= c


def _ssd_pre(dtr, dtb, alog, valid):
    dt = _softplus(dtr + dtb) * valid
    adt = dt * (-jnp.exp(alog))
    a_cs = _dot(_tri(SSM_CHUNK).astype(F32), adt, 1, 0, precision=lax.Precision.HIGHEST)
    return dt, a_cs


def _ssd_specs(geo, rev):
    nc, q = geo.nc, SSM_CHUNK
    ci = (lambda c: nc - 1 - c) if rev else (lambda c: c)
    nxb = SSM_D_INNER // geo.gn
    return [pl.BlockSpec((q, SSM_D_INNER), lambda b, c: (b * nc + ci(c), 0)),
            pl.BlockSpec((q, geo.gn), lambda b, c: (b * nc + ci(c), nxb)),
            pl.BlockSpec((q, geo.gn), lambda b, c: (b * nc + ci(c), nxb + 1)),
            pl.BlockSpec((q, LANE), lambda b, c: (b * nc + ci(c), geo.cb("dt"))),
            pl.BlockSpec((1, LANE), lambda b, c: (0, 0)), pl.BlockSpec((1, LANE), lambda b, c: (0, 0))], ci


def _expand_heads(cols, nh):
    per = LANE // SSM_HEAD_DIM
    lane = lax.broadcasted_iota(jnp.int32, (1, LANE), 1)
    blocks = []
    for j in range(nh // per):
        blk = jnp.broadcast_to(cols[:, j * per:j * per + 1], (cols.shape[0], LANE))
        for k in range(1, per):
            blk = jnp.where(lane >= k * SSM_HEAD_DIM, cols[:, j * per + k:j * per + k + 1], blk)
        blocks.append(blk)
    return jnp.concatenate(blocks, axis=1)


def _head_maps(geo):
    e = (jnp.arange(SSM_D_INNER)[None, :] // SSM_HEAD_DIM == jnp.arange(LANE)[:, None]).astype(F32)
    return e, e.T


def _ssd_fwd_g(geo, xc, proj, dt_bias, a_log):
    q, p, n, e = SSM_CHUNK, SSM_HEAD_DIM, SSM_STATE, geo.nh // SSM_GROUPS
    nc, pad, gw = geo.nc, geo.pad, SSM_D_INNER // SSM_GROUPS
    in_specs, _ = _ssd_specs(geo, False)
    _, e_t = _head_maps(geo)
    in_specs.append(pl.BlockSpec((SSM_D_INNER, LANE), lambda b, c: (0, 0)))

    def body(xs_ref, b_ref, c_ref, dtr_ref, dtb_ref, alog_ref, et_ref, y_ref, sp_ref, state, xdt_s):
        c = pl.program_id(1)

        @pl.when(c == 0)
        def _():
            state[...] = jnp.zeros_like(state)

        sp_ref[...] = state[...]
        inert = (c + 1) * q <= pad

        @pl.when(inert)
        def _():
            y_ref[...] = jnp.zeros_like(y_ref)

        @pl.when(jnp.logical_not(inert))
        def _():
            valid = (c * q + lax.broadcasted_iota(jnp.int32, (q, 1), 0) >= pad).astype(F32)
            dt, a_cs = _ssd_pre(dtr_ref[...], dtb_ref[...], alog_ref[...], valid)
            a_cst = a_cs.T
            dt_x, a_x = _expand_heads(dt, geo.nh), _expand_heads(a_cs, geo.nh)
            e_last = jnp.exp(a_cs[q - 1:q, :])
            tri = _tri(q)
            for g in range(SSM_GROUPS):
                gs = slice(g * gw, (g + 1) * gw)
                bg, cg = b_ref[:, g * n:(g + 1) * n], c_ref[:, g * n:(g + 1) * n]
                a_g = a_x[:, gs]
                xdt_g = xs_ref[:, gs] * dt_x[:, gs]
                xdt_s[:, gs] = xdt_g
                s_g = state[gs, :]
                y_ref[:, gs] = _mxdot(cg, s_g, 1, 1) * jnp.exp(a_g)
                e_last_rows = jnp.sum(et_ref[gs, :] * e_last, axis=1, keepdims=True)
                state[gs, :] = s_g * e_last_rows + _mxdot(xdt_g * jnp.exp(a_g[q - 1:q, :] - a_g), bg, 0, 0)
                cb = _mxdot(cg, bg, 1, 1)
                for hh in range(e):
                    h = g * e + hh
                    hs = slice(h * p, (h + 1) * p)
                    ldec = jnp.exp(jnp.where(tri, a_cs[:, h:h + 1] - a_cst[h:h + 1, :], -jnp.inf))
                    y_ref[:, hs] += _mxdot(cb * ldec, xdt_s[:, hs], 1, 0)

    return pl.pallas_call(
        body, name="ssd_fwd", grid=(geo.bsz, nc), in_specs=in_specs,
        out_specs=[pl.BlockSpec((q, SSM_D_INNER), lambda b, c: (b * nc + c, 0)),
                   pl.BlockSpec((SSM_D_INNER, n), lambda b, c: (b * nc + c, 0))],
        out_shape=[jax.ShapeDtypeStruct((geo.nrows, SSM_D_INNER), F32),
                   jax.ShapeDtypeStruct((geo.bsz * nc * SSM_D_INNER, n), F32)],
        scratch_shapes=[pltpu.VMEM((SSM_D_INNER, n), F32), pltpu.VMEM((q, SSM_D_INNER), F32)],
        compiler_params=_cparams(("parallel", "arbitrary")))(xc, xc, xc, proj, dt_bias, a_log, e_t)


def _ssd_bwd_g(geo, xc, proj, dt_bias, a_log, s_prev_all, dy, dxs_skip, dproj):
    q, p, n, e = SSM_CHUNK, SSM_HEAD_DIM, SSM_STATE, geo.nh // SSM_GROUPS
    nc, pad, di, gn, gw = geo.nc, geo.pad, SSM_D_INNER, geo.gn, SSM_D_INNER // SSM_GROUPS
    in_specs, ci = _ssd_specs(geo, True)
    row_spec = pl.BlockSpec((q, di), lambda b, c: (b * nc + ci(c), 0))
    e_map, e_t = _head_maps(geo)
    in_specs += [pl.BlockSpec((di, n), lambda b, c: (b * nc + ci(c), 0)), row_spec, row_spec,
                 pl.BlockSpec((LANE, di), lambda b, c: (0, 0)), pl.BlockSpec((di, LANE), lambda b, c: (0, 0)),
                 pl.BlockSpec(memory_space=pl.ANY)]

    def body(xs_ref, b_ref, c_ref, dtr_ref, dtb_ref, alog_ref, sp_ref, dy_ref, dsk_ref, e_ref, et_ref, _,
             dxc_ref, ddt_ref, gdtb_ref, galog_ref, dstate, xdt_s, dxdt_s):
        step = pl.program_id(1)
        first = jnp.logical_and(pl.program_id(0) == 0, step == 0)
        c = nc - 1 - step

        @pl.when(step == 0)
        def _():
            dstate[...] = jnp.zeros_like(dstate)

        @pl.when(first)
        def _():
            gdtb_ref[...] = jnp.zeros_like(gdtb_ref)
            galog_ref[...] = jnp.zeros_like(galog_ref)

        inert = (c + 1) * q <= pad

        @pl.when(inert)
        def _():
            dxc_ref[...] = jnp.zeros_like(dxc_ref)
            ddt_ref[...] = jnp.zeros_like(ddt_ref)

        @pl.when(jnp.logical_not(inert))
        def _():
            valid = (c * q + lax.broadcasted_iota(jnp.int32, (q, 1), 0) >= pad).astype(F32)
            dtr, dtb, alog = dtr_ref[...], dtb_ref[...], alog_ref[...]
            dt, a_cs = _ssd_pre(dtr, dtb, alog, valid)
            a_cst = a_cs.T
            dt_x, a_x = _expand_heads(dt, geo.nh), _expand_heads(a_cs, geo.nh)
            e_last = jnp.exp(a_cs[q - 1:q, :])
            tri = _tri(q)
            lane = lax.broadcasted_iota(jnp.int32, (1, LANE), 1)
            sub = lax.broadcasted_iota(jnp.int32, (LANE, 1), 0)
            d_dt = jnp.zeros((q, LANE), F32)
            d_acs = jnp.zeros((q, LANE), F32)
            d_acst = jnp.zeros((LANE, q), F32)
            d_last = jnp.zeros((1, LANE), F32)
            for g in range(SSM_GROUPS):
                gs = slice(g * gw, (g + 1) * gw)
                bg, cg = b_ref[:, g * n:(g + 1) * n], c_ref[:, g * n:(g + 1) * n]
                seg = lambda v: _mxdot(v, e_ref[:, gs], 1, 1)
                a_g, dt_g, x_g, dy_g = a_x[:, gs], dt_x[:, gs], xs_ref[:, gs], dy_ref[:, gs]
                e_col, dec = jnp.exp(a_g), jnp.exp(a_g[q - 1:q, :] - a_g)
                xdt_g = x_g * dt_g
                xdt_s[:, gs] = xdt_g
                s_g, ds_g, et_g = sp_ref[gs, :], dstate[gs, :], et_ref[gs, :]
                cs = _mxdot(cg, s_g, 1, 1)
                d_cs = dy_g * e_col
                d_acs = d_acs + seg(d_cs * cs)
                d_cg = _mxdot(d_cs, s_g, 1, 0)
                dstate[gs, :] = _mxdot(d_cs, cg, 0, 0) + ds_g * jnp.sum(et_g * e_last, axis=1, keepdims=True)
                d_last = d_last + jnp.sum(jnp.sum(ds_g * s_g, axis=1, keepdims=True) * et_g, axis=0,
                                          keepdims=True) * e_last
                gmat = _mxdot(bg, ds_g, 1, 1)
                xd = xdt_g * dec
                d_bg = _mxdot(xd, ds_g, 1, 0)
                d_dec = seg(xd * gmat)
                d_acs = d_acs - d_dec
                d_last = d_last + jnp.sum(d_dec, axis=0, keepdims=True)
                dxdt_s[:, gs] = dec * gmat
                cb = _mxdot(cg, bg, 1, 1)
                d_cb = jnp.zeros((q, q), F32)
                for hh in range(e):
                    h = g * e + hh
                    hs = slice(h * p, (h + 1) * p)
                    ldec = jnp.exp(jnp.where(tri, a_cs[:, h:h + 1] - a_cst[h:h + 1, :], -jnp.inf))
                    dyh = dy_ref[:, hs]
                    d_m = _mxdot(dyh, xdt_s[:, hs], 1, 1)
                    dxdt_s[:, hs] += _mxdot(cb * ldec, dyh, 0, 0)
                    d_cb = d_cb + d_m * ldec
                    d_diff = d_m * cb * ldec
                    d_acs = d_acs + jnp.sum(d_diff, axis=1, keepdims=True) * (lane == h).astype(F32)
                    d_acst = d_acst - (sub == h).astype(F32) * jnp.sum(d_diff, axis=0, keepdims=True)
                d_xdt = dxdt_s[:, gs]
                dxc_ref[:, gs] = d_xdt * dt_g + dsk_ref[:, gs]
                d_dt = d_dt + seg(d_xdt * x_g)
                dxc_ref[:, di + g * n:di + (g + 1) * n] = d_bg + _mxdot(d_cb, cg, 0, 0)
                dxc_ref[:, di + gn + g * n:di + gn + (g + 1) * n] = d_cg + _mxdot(d_cb, bg, 1, 0)
            is_last = (lax.broadcasted_iota(jnp.int32, (q, 1), 0) == q - 1).astype(F32)
            d_acs = d_acs + d_acst.T + is_last * d_last
            d_adt = _dot(_tri(q).astype(F32), d_acs, 0, 0, precision=lax.Precision.HIGHEST)
            a = -jnp.exp(alog)
            d_dt = d_dt + d_adt * a
            d_dtr = d_dt * valid * _sigmoid(dtr + dtb)
            ddt_ref[...] = d_dtr.astype(ddt_ref.dtype)
            gdtb_ref[...] += jnp.sum(d_dtr, axis=0, keepdims=True)
            galog_ref[...] += jnp.sum(d_adt * dt, axis=0, keepdims=True) * a

    vec = pl.BlockSpec((1, LANE), lambda b, c: (0, 0))
    return pl.pallas_call(
        body, name="ssd_bwd", grid=(geo.bsz, nc), in_specs=in_specs,
        out_specs=[pl.BlockSpec((q, geo.cd), lambda b, c: (b * nc + ci(c), 0)),
                   pl.BlockSpec((q, LANE), lambda b, c: (b * nc + ci(c), geo.cb("dt"))), vec, vec],
        out_shape=[jax.ShapeDtypeStruct((geo.nrows, geo.cd), F32), jax.ShapeDtypeStruct(dproj.shape, dproj.dtype),
                   jax.ShapeDtypeStruct((1, LANE), F32), jax.ShapeDtypeStruct((1, LANE), F32)],
        scratch_shapes=[pltpu.VMEM((di, n), F32), pltpu.VMEM((q, di), F32), pltpu.VMEM((q, di), F32)],
        input_output_aliases={11: 1},
        compiler_params=_cparams(("arbitrary", "arbitrary")))(
            xc, xc, xc, proj, dt_bias, a_log, s_prev_all, dy, dxs_skip, e_map, e_t, dproj)


BIAS_LANE = MLA_ROPE // 2
KEY_OFF = -1e30
ATT_SCALE = (MLA_NOPE + MLA_ROPE) ** -0.5


def _row_t(col):
    return jnp.broadcast_to(col, (col.shape[0], LANE)).T[:8]


def _attn_fwd2(geo, qn, qp, kn, kp, v):
    t, lp = ATT_BLK, geo.lp
    nb = lp // t

    def body(qn_ref, qp_ref, kn_ref, kp_ref, v_ref, o_ref, lse_ref, k_ref):
        qi = pl.program_id(2)

        @pl.when(qi == 0)
        def _():
            k_ref[:, :LANE] = kn_ref[...]
            k_ref[:, LANE:] = kp_ref[...]

        q = jnp.concatenate([qn_ref[...], qp_ref[...]], axis=1)

        def blk(kj, ntile, carry, diag):
            m, l, acc = carry
            ks = pl.ds(pl.multiple_of(kj * t, t), ntile * t)
            s = _mxdot(q, k_ref[ks, :], 1, 1) * ATT_SCALE
            if diag:
                s = jnp.where(_tri(t), s, -jnp.inf)
            m_new = jnp.maximum(m, jnp.max(s, axis=1, keepdims=True))
            pr = jnp.exp(s - m_new)
            alpha = jnp.exp(m - m_new)
            return m_new, alpha * l + jnp.sum(pr, axis=1, keepdims=True), alpha * acc + _mxdot(pr, v_ref[ks, :], 1, 0)

        carry = (jnp.full((t, 1), 2.0 * KEY_OFF, F32), jnp.zeros((t, 1), F32), jnp.zeros((t, LANE), F32))
        done = 0
        for ntile in (4, 2, 1):
            steps = (qi - done) // ntile
            carry = lax.fori_loop(0, steps, lambda j, c, d=done, n=ntile: blk(d + n * j, n, c, False), carry)
            done = done + steps * ntile
        m, l, acc = blk(qi, 1, carry, True)
        o_ref[...] = acc / l
        lse_ref[0, 0, 0] = _row_t(m + jnp.log(l))

    tile = pl.BlockSpec((t, LANE), lambda b, h, i: (b * nb + i, h))
    seq = pl.BlockSpec((lp, LANE), lambda b, h, i: (b, h))
    return pl.pallas_call(
        body, name="attn_fwd", grid=(geo.bsz, MLA_HEADS, nb),
        in_specs=[tile, tile, seq, pl.BlockSpec((lp, LANE), lambda b, h, i: (b, 0)), seq],
        out_specs=[tile, pl.BlockSpec((1, 1, 1, 8, t), lambda b, h, i: (b, h, i, 0, 0))],
        out_shape=[jax.ShapeDtypeStruct((geo.nrows, geo.hq), F32),
                   jax.ShapeDtypeStruct((geo.bsz, MLA_HEADS, nb, 8, t), F32)],
        scratch_shapes=[pltpu.VMEM((lp, 2 * LANE), MXU_DTYPE)],
        compiler_params=_cparams(("parallel", "parallel", "arbitrary")))(qn, qp, kn, kp, v)


def _attn_bwd2(geo, qn, qp, kn, kp, v, d_o, o, lse):
    t, lp = ATT_BLK, geo.lp
    nb = lp // t

    def body(qn_ref, qp_ref, kn_ref, kp_ref, v_ref, do_ref, o_ref, lse_ref,
             dqn_ref, dqp_ref, dkn_ref, dkp_ref, dv_ref, q_ref, dl_s):
        kj = pl.program_id(2)

        @pl.when(kj == 0)
        def _():
            q_ref[:, :LANE] = qn_ref[...]
            q_ref[:, LANE:] = qp_ref[...]
            dqn_ref[...] = jnp.zeros_like(dqn_ref)
            dqp_ref[...] = jnp.zeros_like(dqp_ref)
            for i in range(nb):
                rows = slice(i * t, (i + 1) * t)
                dl_s[i] = _row_t(jnp.sum(do_ref[rows, :] * o_ref[rows, :], axis=1, keepdims=True))

        k, vv = jnp.concatenate([kn_ref[...], kp_ref[...]], axis=1), v_ref[...]

        def row(ref, qi, ntile):
            return jnp.concatenate([ref[qi + i][:1, :] for i in range(ntile)], axis=1)

        def blk(qi, ntile, carry, diag):
            dk, dv = carry
            qs = pl.ds(pl.multiple_of(qi * t, t), ntile * t)
            q, d_o_blk = q_ref[qs, :], do_ref[qs, :]
            st = _mxdot(k, q, 1, 1) * ATT_SCALE
            if diag:
                keys = lax.broadcasted_iota(jnp.int32, (t, t), 0)
                st = jnp.where(keys <= lax.broadcasted_iota(jnp.int32, (t, t), 1), st, -jnp.inf)
            pt = jnp.exp(st - row(lse_ref.at[0, 0], qi, ntile))
            dst = pt * (_mxdot(vv, d_o_blk, 1, 1) - row(dl_s, qi, ntile)) * ATT_SCALE
            dq = _mxdot(dst, k, 0, 0)
            dqn_ref[qs, :] += dq[:, :LANE]
            dqp_ref[qs, :] += dq[:, LANE:]
            return dk + _mxdot(dst, q, 1, 0), dv + _mxdot(pt, d_o_blk, 1, 0)

        carry = blk(kj, 1, (jnp.zeros((t, 2 * LANE), F32), jnp.zeros((t, LANE), F32)), True)
        done = kj + 1
        for ntile in (4, 2, 1):
            steps = (nb - done) // ntile
            carry = lax.fori_loop(0, steps, lambda j, c, d=done, n=ntile: blk(d + n * j, n, c, False), carry)
            done = done + steps * ntile
        dk, dv = carry
        dkn_ref[...] = dk[:, :LANE].astype(dkn_ref.dtype)
        dkp_ref[...] = dk[:, LANE:]
        dv_ref[...] = dv.astype(dv_ref.dtype)

    seq = pl.BlockSpec((lp, LANE), lambda b, h, j: (b, h))
    tile = pl.BlockSpec((t, LANE), lambda b, h, j: (b * nb + j, h))
    return pl.pallas_call(
        body, name="attn_bwd", grid=(geo.bsz, MLA_HEADS, nb),
        in_specs=[seq, seq, tile, pl.BlockSpec((t, LANE), lambda b, h, j: (b * nb + j, 0)), tile, seq, seq,
                  pl.BlockSpec((1, 1, nb, 8, t), lambda b, h, j: (b, h, 0, 0, 0))],
        out_specs=[seq, seq, tile, tile, tile],
        out_shape=[jax.ShapeDtypeStruct((geo.nrows, geo.hq), F32), jax.ShapeDtypeStruct((geo.nrows, geo.hq), F32),
                   jax.ShapeDtypeStruct((geo.nrows, geo.hq), MXU_DTYPE), jax.ShapeDtypeStruct((geo.nrows, geo.hq), F32),
                   jax.ShapeDtypeStruct((geo.nrows, geo.hq), MXU_DTYPE)],
        scratch_shapes=[pltpu.VMEM((lp, 2 * LANE), MXU_DTYPE), pltpu.VMEM((nb, 8, t), F32)],
        compiler_params=_cparams(("parallel", "parallel", "arbitrary")))(qn, qp, kn, kp, v, d_o, o, lse)


def _rope(x, cos, sin):
    return x * cos + pltpu.roll(x, LANE // 2, axis=1) * sin


def _rope_t(dx, cos, sin):
    return dx * cos + pltpu.roll(dx * sin, LANE // 2, axis=1)


def _per_head(f):
    def fn(x, cos, sin):
        return (jnp.concatenate([f(x[:, h * LANE:(h + 1) * LANE], cos, sin) for h in range(MLA_HEADS)], axis=1),)
    return fn


def _layer_fwd(geo, h, w, tab, late=None):
    nr, tr, trw = geo.nrows, geo.tr, geo.tr_wide
    tb = geo.lp // tr
    rw = functools.partial(_rowwise, nrows=nr)
    s = {"h": h}
    (s["u"],) = rw("rms_mix", lambda x, g: (_rms(x, g),), tr=tr, rows=[(h, D_MODEL, 0)],
                   vecs=[(w["norm_mix_w"], D_MODEL, 0)], outs=[(D_MODEL, D_MODEL, MXU_DTYPE)])
    proj = s["proj"] = _mm("mm_in", s["u"], w["w_in_p"])
    xc = s["xc"] = _conv_fwd(geo, proj, w["conv_w"], w["conv_b"])
    s["y_ssd"], s["s_prev"] = _ssd_fwd_g(geo, xc, proj, w["dt_bias"], w["a_log"])
    gw = SSM_D_INNER // SSM_GROUPS

    def gate_norm(y, x, z, dsk, nw):
        return (_rms((y + x * dsk) * _silu(z), nw),)

    (s["y_ssm"],) = rw("ssm_gate_norm", gate_norm, tr=tr, ncb=SSM_GROUPS,
                       rows=[(s["y_ssd"], gw, 0), (xc, gw, 0), (proj, gw, geo.col["z"][0] // gw)],
                       vecs=[(w["d_skip_full"], gw, 0), (w["ssm_norm_w"], gw, 0)], outs=[(SSM_D_INNER, gw, MXU_DTYPE)])
    if late is not None:
        w = {**w, **late(s["y_ssm"])}
    (s["cq_n"],) = rw("rms_q", lambda x, g: (_rms(x, g),), tr=tr, rows=[(proj, MLA_Q_LORA, geo.cb("c_q"))],
                      vecs=[(w["q_norm_w"], MLA_Q_LORA, 0)], outs=[(MLA_Q_LORA, MLA_Q_LORA, MXU_DTYPE)])
    (s["ckv_n"],) = rw("rms_kv", lambda x, g: (_rms(x, g),), tr=tr, rows=[(proj, MLA_KV_LORA, geo.cb("c_kv"))],
                       vecs=[(w["kv_norm_w"], MLA_KV_LORA, 0)], outs=[(MLA_KV_LORA, MLA_KV_LORA, MXU_DTYPE)])
    s["qn"] = _mm("mm_qn", s["cq_n"], w["w_qn"], out_dtype=MXU_DTYPE)
    qp_raw = _mm("mm_qp", s["cq_n"], w["w_qp"])
    s["kn"] = _mm("mm_kn", s["ckv_n"], w["w_k"], out_dtype=MXU_DTYPE)
    s["v"] = _mm("mm_v", s["ckv_n"], w["w_v"], out_dtype=MXU_DTYPE)
    bias_lane = lambda: lax.broadcasted_iota(jnp.int32, (1, LANE), 1) == BIAS_LANE
    rope_tabs = [(tab["cos"], LANE, 0), (tab["sin"], LANE, 0)]
    (s["qp"],) = rw("rope_q", _per_head(lambda xp, c, sn: jnp.where(bias_lane(), 1.0, _rope(xp, c, sn))), tr=tr,
                    rows=[(qp_raw, geo.hq, 0)], tabs=rope_tabs, outs=[(geo.hq, geo.hq, MXU_DTYPE)], tab_blocks=tb)
    (s["kp"],) = rw("rope_k", lambda xp, c, sn, valid: (jnp.where(bias_lane(), KEY_OFF * (1.0 - valid), _rope(xp, c, sn)),),
                    tr=tr, rows=[(proj, LANE, geo.cb("k_rope"))], tabs=rope_tabs + [(tab["valid"], 1, 0)],
                    outs=[(LANE, LANE, MXU_DTYPE)], tab_blocks=tb)
    s["o"], s["lse"] = _attn_fwd2(geo, s["qn"], s["qp"], s["kn"], s["kp"], s["v"])
    s["ys_p"] = _mm("mm_bs", s["y_ssm"], w["w_branch_ssm"])
    s["ym_p"] = _mm("mm_bm", s["o"], w["w_branch_mla"])

    def gate(gs, gm, ys, ym):
        return (_sigmoid(gs) * ys + _sigmoid(gm) * ym,)

    (s["mixed"],) = rw("gate", gate, tr=tr, rows=[(proj, D_MODEL, geo.cb("g_ssm")), (proj, D_MODEL, geo.cb("g_mla")),
                                                  (s["ys_p"], D_MODEL, 0), (s["ym_p"], D_MODEL, 0)],
                       outs=[(D_MODEL, D_MODEL, MXU_DTYPE)])
    s["h2"] = _mm("mm_out", s["mixed"], w["w_out"], add=h)
    (s["vn"],) = rw("rms_mlp", lambda x, g: (_rms(x, g),), tr=tr, rows=[(s["h2"], D_MODEL, 0)],
                    vecs=[(w["norm_mlp_w"], D_MODEL, 0)], outs=[(D_MODEL, D_MODEL, MXU_DTYPE)])
    s["up"], s["act"] = _mm("mm_up", s["vn"], w["w_mlp_up"],
                            epi=(lambda r: (r, jnp.square(jnp.maximum(r, 0.0))), (F32, MXU_DTYPE)))
    return _mm("mm_down", s["act"], w["w_mlp_down"], add=s["h2"]), s, w


def _layer_bwd(geo, dh3, s, w, tab, mid=None, tail=None, dep=None):
    nr, tr, trw = geo.nrows, geo.tr, geo.tr_wide
    tb = geo.lp // tr
    rw = functools.partial(_rowwise, nrows=nr)
    g = {}
    proj = s["proj"]

    def rms_bwd(x, dy, res, gw):
        _, vjp = jax.vjp(_rms, x.astype(F32), gw)
        dx, dgw = vjp(dy.astype(F32))
        return dx + res, dgw

    def rms_bwd_nores(x, dy, gw):
        _, vjp = jax.vjp(_rms, x.astype(F32), gw)
        return vjp(dy.astype(F32))

    (dup,) = _mm("mm_down_t", dh3, w["w_mlp_down"], tb=True, add=s["up"], dep=dep,
                 epi=(lambda r, up: (r * 2.0 * jnp.maximum(up, 0.0),), (MXU_DTYPE,)))
    g["w_mlp_down"] = _mm("mm_down_g", s["act"], dh3, ta=True, out_dtype=MXU_DTYPE)
    g["w_mlp_up"] = _mm("mm_up_g", s["vn"], dup, ta=True, out_dtype=MXU_DTYPE)
    dvn = _mm("mm_up_t", dup, w["w_mlp_up"], tb=True)
    dh2, g["norm_mlp_w"] = rw("rms_mlp_bwd", rms_bwd, tr=tr,
                              rows=[(s["h2"], D_MODEL, 0), (dvn, D_MODEL, 0), (dh3, D_MODEL, 0)],
                              vecs=[(w["norm_mlp_w"], D_MODEL, 0)], outs=[(D_MODEL, D_MODEL, F32)],
                              reds=[(D_MODEL, D_MODEL)])
    dmixed = _mm("mm_out_t", dh2, w["w_out"], tb=True)
    g["w_out"] = _mm("mm_out_g", s["mixed"], dh2, ta=True, out_dtype=MXU_DTYPE)

    def gate_bwd(gs, gm, ys, ym, dm):
        f = lambda a, b, c, d: _sigmoid(a) * c + _sigmoid(b) * d
        _, vjp = jax.vjp(f, gs, gm, ys, ym)
        dgs, dgm, dys, dym = vjp(dm)
        return dys, dym, jnp.concatenate([dgs, dgm], axis=1)

    assert geo.col["g_mla"][0] == geo.col["g_ssm"][0] + D_MODEL and geo.col["g_ssm"][0] % (2 * D_MODEL) == 0
    dys_p, dym_p, dproj = rw(
        "gate_bwd", gate_bwd, tr=tr,
        rows=[(proj, D_MODEL, geo.cb("g_ssm")), (proj, D_MODEL, geo.cb("g_mla")), (s["ys_p"], D_MODEL, 0),
              (s["ym_p"], D_MODEL, 0), (dmixed, D_MODEL, 0)],
        outs=[(D_MODEL, D_MODEL, MXU_DTYPE)] * 2 + [(geo.pw, 2 * D_MODEL, MXU_DTYPE, geo.col["g_ssm"][0] // (2 * D_MODEL))])
    g["w_branch_ssm"] = _mm("mm_bs_g", s["y_ssm"], dys_p, ta=True, out_dtype=MXU_DTYPE)
    dy_ssm = _mm("mm_bs_t", dys_p, w["w_branch_ssm"], tb=True)
    g["w_branch_mla"] = _mm("mm_bm_g", s["o"], dym_p, ta=True, out_dtype=MXU_DTYPE)
    d_o = _mm("mm_bm_t", dym_p, w["w_branch_mla"], tb=True)
    dqn, dqp, dkn, dkp_h, dv = _attn_bwd2(geo, s["qn"], s["qp"], s["kn"], s["kp"], s["v"], d_o, s["o"], s["lse"])
    rope_tabs = [(tab["cos"], LANE, 0), (tab["sin"], LANE, 0)]
    (dqp_raw,) = rw("rope_q_bwd", _per_head(_rope_t), tr=tr, rows=[(dqp, geo.hq, 0)], tabs=rope_tabs,
                    outs=[(geo.hq, geo.hq, MXU_DTYPE)], tab_blocks=tb)

    def rope_k_bwd(x, c, sn):
        tot = x[:, :LANE]
        for hd in range(1, MLA_HEADS):
            tot = tot + x[:, hd * LANE:(hd + 1) * LANE]
        return (_rope_t(tot, c, sn),)

    (dproj,) = rw("rope_k_bwd", rope_k_bwd, tr=tr, rows=[(dkp_h, geo.hq, 0)], tabs=rope_tabs,
                  outs=[(geo.pw, LANE, MXU_DTYPE, geo.cb("k_rope"), dproj)], tab_blocks=tb)
    g["w_qn"] = _mm("mm_qn_g", s["cq_n"], dqn, ta=True, out_dtype=MXU_DTYPE)
    g["w_qp"] = _mm("mm_qp_g", s["cq_n"], dqp_raw, ta=True, out_dtype=MXU_DTYPE)
    dcq_n = _mm("mm_qp_t", dqp_raw, w["w_qp"], tb=True, add=_mm("mm_qn_t", dqn, w["w_qn"], tb=True))
    g["w_k"] = _mm("mm_kn_g", s["ckv_n"], dkn, ta=True, out_dtype=MXU_DTYPE)
    g["w_v"] = _mm("mm_v_g", s["ckv_n"], dv, ta=True, out_dtype=MXU_DTYPE)
    dckv_n = _mm("mm_v_t", dv, w["w_v"], tb=True, add=_mm("mm_kn_t", dkn, w["w_k"], tb=True))
    dproj, g["q_norm_w"] = rw("rms_q_bwd", rms_bwd_nores, tr=tr,
                              rows=[(proj, MLA_Q_LORA, geo.cb("c_q")), (dcq_n, MLA_Q_LORA, 0)],
                              vecs=[(w["q_norm_w"], MLA_Q_LORA, 0)],
                              outs=[(geo.pw, MLA_Q_LORA, MXU_DTYPE, geo.cb("c_q"), dproj)], reds=[(MLA_Q_LORA, MLA_Q_LORA)])
    dproj, g["kv_norm_w"] = rw("rms_kv_bwd", rms_bwd_nores, tr=tr,
                               rows=[(proj, MLA_KV_LORA, geo.cb("c_kv")), (dckv_n, MLA_KV_LORA, 0)],
                               vecs=[(w["kv_norm_w"], MLA_KV_LORA, 0)],
                               outs=[(geo.pw, MLA_KV_LORA, MXU_DTYPE, geo.cb("c_kv"), dproj)],
                               reds=[(MLA_KV_LORA, MLA_KV_LORA)])
    gw_ = SSM_D_INNER // SSM_GROUPS
    d_skip_full = w["d_skip_full"] if mid is None else w["d_skip_full"] + mid(g)[0, 0]

    def gate_norm_bwd(y, x, z, dy, dsk, nw):
        f = lambda y_, x_, z_, dsk_, nw_: _rms((y_ + x_ * dsk_) * _silu(z_), nw_)
        _, vjp = jax.vjp(f, y, x, z, dsk, nw)
        dy_, dx_, dz_, ddsk, dnw = vjp(dy)
        return dy_, dx_, dz_, ddsk, dnw

    dy_ssd, dxs_skip, dproj, g["d_skip_full"], g["ssm_norm_w"] = rw(
        "ssm_gate_norm_bwd", gate_norm_bwd, tr=tr, ncb=SSM_GROUPS,
        rows=[(s["y_ssd"], gw_, 0), (s["xc"], gw_, 0), (proj, gw_, geo.col["z"][0] // gw_), (dy_ssm, gw_, 0)],
        vecs=[(d_skip_full, gw_, 0), (w["ssm_norm_w"], gw_, 0)],
        outs=[(SSM_D_INNER, gw_, F32), (SSM_D_INNER, gw_, F32),
              (geo.pw, gw_, MXU_DTYPE, geo.col["z"][0] // gw_, dproj)],
        reds=[(SSM_D_INNER, gw_), (SSM_D_INNER, gw_)])
    dxc, dproj, g["dt_bias"], g["a_log"] = _ssd_bwd_g(geo, s["xc"], proj, w["dt_bias"], w["a_log"], s["s_prev"],
                                                     dy_ssd, dxs_skip, dproj)
    dproj, g["conv_w"], g["conv_b"] = _conv_bwd(geo, proj, w["conv_w"], w["conv_b"], dxc, dproj)
    g["w_in_p"] = _mm("mm_in_g", s["u"], dproj, ta=True, out_dtype=MXU_DTYPE)
    du = _mm("mm_in_t", dproj, w["w_in_p"], tb=True, dep=None if tail is None else tail(g))
    dh, g["norm_mix_w"] = rw("rms_mix_bwd", rms_bwd, tr=tr,
                             rows=[(s["h"], D_MODEL, 0), (du, D_MODEL, 0), (dh2, D_MODEL, 0)],
                             vecs=[(w["norm_mix_w"], D_MODEL, 0)], outs=[(D_MODEL, D_MODEL, F32)],
                             reds=[(D_MODEL, D_MODEL)])
    return dh, g


def _loss_bwd(geo, h, fw, target, tab):
    tr = geo.tr

    def fn(x, tgt, gw, tok):
        def lossf(x_, gw_):
            err = jnp.square(_rms(x_, gw_) - tgt)
            return 0.5 * jnp.sum(tok * jnp.mean(err, axis=-1, keepdims=True), axis=0, keepdims=True)

        val, vjp = jax.vjp(lossf, x, gw)
        dx, dgw = vjp(jnp.ones((1, 1), F32))
        return dx, jnp.broadcast_to(val, (1, LANE)), dgw

    return _rowwise("loss", fn, nrows=geo.nrows, tr=tr, rows=[(h, D_MODEL, 0), (target, D_MODEL, 0)],
                    vecs=[(fw, D_MODEL, 0)], tabs=[(tab["token"], 1, 0)], outs=[(D_MODEL, D_MODEL, F32)],
                    reds=[(LANE, LANE), (D_MODEL, D_MODEL)], tab_blocks=geo.lp // tr)


def kernel(x, meta_tokens, norm_mix_w, w_in, conv_w, conv_b, dt_bias, a_log, d_skip, ssm_norm_w, q_norm_w, kv_norm_w, w_uq, w_ukv, w_branch_ssm, w_branch_mla, w_out, norm_mlp_w, w_mlp_up, w_mlp_down, final_norm_w, loss_target, m_meta_tokens, m_norm_mix_w, m_w_in, m_conv_w, m_conv_b, m_dt_bias, m_a_log, m_d_skip, m_ssm_norm_w, m_q_norm_w, m_kv_norm_w, m_w_uq, m_w_ukv, m_w_branch_ssm, m_w_branch_mla, m_w_out, m_norm_mlp_w, m_w_mlp_up, m_w_mlp_down, m_final_norm_w, v_meta_tokens, v_norm_mix_w, v_w_in, v_conv_w, v_conv_b, v_dt_bias, v_a_log, v_d_skip, v_ssm_norm_w, v_q_norm_w, v_kv_norm_w, v_w_uq, v_w_ukv, v_w_branch_ssm, v_w_branch_mla, v_w_out, v_norm_mlp_w, v_w_mlp_up, v_w_mlp_down, v_final_norm_w):
    args = dict(locals())
    wts = {n: args[n] for n in WEIGHTS}
    mom = {n: args["m_" + n] for n in WEIGHTS}
    var = {n: args["v_" + n] for n in WEIGHTS}
    bsz, seq, _ = x.shape
    depth = w_in.shape[0]
    geo = _Geo(bsz, seq)
    tab = _tables(geo)

    big_names = [n for n, _ in BIG]
    sh_names = big_names + [n for n, _ in SHARDED_F32]
    kinds = dict(BIG + SHARDED_F32)
    shard3 = lambda a: a.reshape((1,) + a.shape) if a.ndim == 2 else a
    wire = {n: (MXU_DTYPE if n in big_names else F32) for n in sh_names}
    cast = {n: shard3(wts[n]).astype(wire[n]) for n in sh_names}
    per_layer = [n for n in sh_names if n != "meta_tokens"]
    small_names = ["norm_mix_w", "conv_b", "dt_bias", "a_log", "d_skip", "ssm_norm_w", "q_norm_w", "kv_norm_w",
                   "norm_mlp_w"]

    def gather_items(pairs):
        ins, outs, items, forms = [], [], [], []
        for n, i in pairs:
            a, b = cast[n].shape[1:]
            shape, dst, form = _gather_plan(a, b, kinds[n])
            items.append((len(ins), len(outs), (lambda ref, p, i=i: ref.at[i]), dst))
            ins.append(cast[n])
            outs.append(jax.ShapeDtypeStruct(shape, wire[n]))
            forms.append(form)
        return ins, outs, items, forms

    def whole_weights(pairs, forms, got):
        by_layer = {}
        for (n, i), form, g in zip(pairs, forms, got):
            if n == "w_in":
                n, g = "w_in_p", _w_in_assemble(geo, g)
            elif form == "row":
                g = g.reshape(g.shape[0] * g.shape[1], g.shape[2])
            elif form == "stack":
                g = _unshard(g, "col")
            by_layer.setdefault(i, {})[n] = g
        return by_layer

    def prep(i, whole, token=None):
        wl = dict(whole)
        wl.update({n: wts[n][i] for n in small_names})
        if token is not None:
            wl["norm_mix_w"] = wl["norm_mix_w"] + token[0, 0]
        return _prep_layer(geo, wl)

    early = ("w_in", "conv_w")
    late_names = [n for n in per_layer if n not in early]
    pairs1 = [(n, i) for i in range(1, depth) for n in per_layer]
    groups = [[(n, 0) for n in early] + [("meta_tokens", 0)], [(n, 0) for n in late_names]] + ([pairs1] if pairs1 else [])
    started = {}

    def gather_start(gi, dep=None):
        ins, outs, items, forms = gather_items(groups[gi])
        sems, thru, landing, token = _exchange_start("gather_w%d_start" % gi, ins, outs, items, dep)
        started[gi] = (groups[gi], forms, sems, thru, landing, items)
        return token

    def gathered(gi, after):
        pairs, forms, sems, thru, landing, items = started[gi]
        return whole_weights(pairs, forms, _exchange_wait("gather_w%d_wait" % gi, sems, thru, landing, items, after))

    def late0(after):
        whole = gathered(1, after)[0]
        if pairs1:
            whole["q_norm_w"] = wts["q_norm_w"][0] + gather_start(2, whole["w_out"])[0, 0]
        return _prep_layer(geo, whole)

    token = gather_start(1, gather_start(0))
    whole0 = gathered(0, token)[0]
    meta_full = whole0.pop("meta_tokens")

    meta = jnp.broadcast_to(meta_full[None], (bsz, N_META, D_MODEL))
    h = jnp.concatenate([jnp.zeros((bsz, geo.pad, D_MODEL), F32), meta, x], axis=1).reshape(geo.nrows, D_MODEL)
    target = jnp.concatenate([jnp.zeros((bsz, geo.pad + N_META, D_MODEL), F32), loss_target], axis=1)
    target = target.reshape(geo.nrows, D_MODEL)
    layers, saved = [], []
    for i in range(depth):
        if i == 0:
            w, late = prep(0, whole0, token), late0
        else:
            if i == 1:
                whole1 = gathered(2, h)
            w, late = prep(i, whole1[i]), None
        h, s, w = _layer_fwd(geo, h, w, tab, late)
        layers.append(w)
        saved.append(s)
    dh, loss_part, g_final = _loss_bwd(geo, h, final_norm_w.reshape(1, -1), target, tab)

    def scatter_items(pairs):
        ins, outs, items = [], [], []
        for n, i in pairs:
            a, b = cast[n].shape[1:]
            arr = g_meta if n == "meta_tokens" else grads[i]["w_in_p" if n == "w_in" else n]
            if n == "w_in":
                arr, src = _w_in_split(geo, arr, b), _entry
            elif kinds[n] == "row":
                src = lambda ref, p, a=a: ref.at[pl.ds(pl.multiple_of(p * a, a), a)]
            elif b % LANE == 0:
                src = lambda ref, p, b=b: ref.at[:, pl.ds(pl.multiple_of(p * b, b), b)]
            else:
                arr, src = _shard(arr, "col"), _entry
            items.append((len(ins), len(outs), src, _entry))
            ins.append(arr.astype(wire[n]))
            outs.append(jax.ShapeDtypeStruct((N_DEV, a, b), wire[n]))
        return ins, outs, items

    grads = [None] * depth
    landed, pending, res = {}, {}, {}

    def scatter_start(name, pairs):
        ins, outs, items = scatter_items(pairs)
        sems, thru, landing, token = _exchange_start(name + "_start", ins, outs, items)
        pending[name] = (pairs, sems, thru, landing, items)
        return token

    def scatter_wait(name, after):
        pairs, sems, thru, landing, items = pending[name]
        landed.update(zip(pairs, _exchange_wait(name + "_wait", sems, thru, landing, items, after)))

    def adam(n):
        parts = [landed[(n, i)] for i in range(cast[n].shape[0])]
        r = _adamw_nat("adamw_" + n, parts, shard3(wts[n]), shard3(mom[n]), shard3(var[n]))
        res[n] = [a.reshape(wts[n].shape) for a in r]

    def mid0(g):
        grads[0] = _unprep_grads(geo, g)
        return scatter_start("scatter_gb0", [(n, 0) for n in late_names])

    def tail0(g):
        grads[0] = _unprep_grads(geo, g)
        return scatter_start("scatter_ga0", [(n, 0) for n in early])

    dep = None
    for i in reversed(range(depth)):
        dh, gl = _layer_bwd(geo, dh, saved[i], layers[i], tab, *((mid0, tail0) if i == 0 else (None, None)), dep)
        grads[i] = _unprep_grads(geo, gl)
        if i == 1:
            dep = scatter_start("scatter_g1", pairs1)
    dh = dh.reshape(bsz, geo.lp, D_MODEL)
    grad_x = dh[:, geo.pad + N_META:]
    g_meta = jnp.sum(dh[:, geo.pad:geo.pad + N_META], axis=0)
    if pairs1:
        scatter_wait("scatter_g1", g_meta)
    scatter_wait("scatter_gb0", g_meta)
    for n in late_names:
        adam(n)
    g_small = {n: jnp.stack([grads[i][n] for i in range(depth)]) for n in SMALL if n != "final_norm_w"}
    g_small["final_norm_w"] = g_final.reshape(-1)
    zero = jnp.zeros((1,), F32)
    pk = lambda d, last: _pack([d[n] for n in SMALL] + [last], F32, row_mult=8)
    packed = pk(g_small, loss_part[0, :1])
    ins, outs, items = scatter_items([("meta_tokens", 0)])
    parts, landed[("meta_tokens", 0)] = _exchange(
        "gather_g", [packed] + ins, [jax.ShapeDtypeStruct((N_DEV,) + packed.shape, F32)] + outs,
        [(0, 0, _whole, _entry)] + [(1, 1, items[0][2], items[0][3])])
    adam("meta_tokens")
    scatter_wait("scatter_ga0", res["meta_tokens"][1])
    for n in early:
        adam(n)
    res_sm = _adamw("adamw_small", parts, pk(wts, zero), pk(mom, zero), pk(var, zero))
    res_sm = [_unpack(r, [wts[n].shape for n in SMALL] + [(1,)]) for r in res_sm]
    loss = res_sm[0][-1][0]

    out = [loss, grad_x]
    for k in range(4):
        named = {n: res[n][k] for n in sh_names}
        named.update(zip(SMALL, res_sm[k]))
        out += [named[n] for n in WEIGHTS]
    return tuple(out)
```

```python
import functools

import numpy as np
import jax
import jax.numpy as jnp
from jax import lax
from jax.experimental import pallas as pl
from jax.experimental.pallas import tpu as pltpu

F32 = jnp.float32
MXU_DTYPE = jnp.bfloat16

D_MODEL = 1024
N_META = 16
EPS = 1e-6
SSM_D_INNER = 2048
SSM_HEAD_DIM = 64
SSM_GROUPS = 4
SSM_STATE = 128
SSM_CONV = 4
SSM_CHUNK = 128
MLA_HEADS = 8
MLA_Q_LORA = 512
MLA_KV_LORA = 256
MLA_NOPE = 128
MLA_ROPE = 64
MLA_V = 128
ROPE_THETA = 10000.0
D_FF = 4096
ADAM_LR = 0.001
ADAM_B1 = 0.9
ADAM_B2 = 0.999
ADAM_EPS = 1e-08
ADAM_WD = 0.01
ADAM_STEP = 10

N_DEV = 8
ATT_BLK = 256
LANE = 128
PACK_W = 1024
VMEM_LIMIT = 56 * 1024 * 1024
MESH_ID = pl.DeviceIdType.MESH

BIG = (("w_in", "col"), ("w_uq", "col"), ("w_ukv", "col"), ("w_branch_ssm", "row"), ("w_branch_mla", "row"),
       ("w_out", "row"), ("w_mlp_up", "col"), ("w_mlp_down", "row"))
SHARDED_F32 = (("conv_w", "col"), ("meta_tokens", "col"))
SMALL = ("norm_mix_w", "conv_b", "dt_bias", "a_log", "d_skip", "ssm_norm_w", "q_norm_w", "kv_norm_w",
         "norm_mlp_w", "final_norm_w")
WEIGHTS = ("meta_tokens", "norm_mix_w", "w_in", "conv_w", "conv_b", "dt_bias", "a_log", "d_skip", "ssm_norm_w",
           "q_norm_w", "kv_norm_w", "w_uq", "w_ukv", "w_branch_ssm", "w_branch_mla", "w_out", "norm_mlp_w",
           "w_mlp_up", "w_mlp_down", "final_norm_w")


def _cparams(sem=None):
    return pltpu.CompilerParams(dimension_semantics=sem, vmem_limit_bytes=VMEM_LIMIT)


def _pick(n, cands):
    for c in cands:
        if n % c == 0:
            return c
    return n


def _sigmoid(x):
    return 1.0 / (1.0 + jnp.exp(-x))


def _silu(x):
    return x * _sigmoid(x)


def _softplus(x):
    t = jnp.exp(-jnp.abs(x))
    return jnp.maximum(x, 0.0) + jnp.where(t < 0.01, t * (1.0 - t * (0.5 - t * (1.0 / 3.0))), jnp.log(1.0 + t))


def _rms(x, w):
    return x * lax.rsqrt(jnp.mean(x * x, axis=-1, keepdims=True) + EPS) * w


def _dot(a, b, ca, cb, precision=None):
    return lax.dot_general(a, b, (((ca,), (cb,)), ((), ())), preferred_element_type=F32, precision=precision)


def _mxdot(a, b, ca, cb):
    return _dot(a.astype(MXU_DTYPE), b.astype(MXU_DTYPE), ca, cb)


def _mm(name, a, b, *, ta=False, tb=False, add=None, out_dtype=F32, dep=None, epi=None):
    (kdim, m) = a.shape if ta else a.shape[::-1]
    (n, k2) = b.shape if tb else b.shape[::-1]
    assert kdim == k2, (name, a.shape, b.shape)
    tm = _pick(m, (1152, 1024, 768, 512, 384, 256, 128))
    tn = _pick(n, (1024, 512, 384, 256, 128))
    tk = _pick(kdim, (1152, 1024, 768, 512, 384, 256, 128))
    nk = kdim // tk
    a_spec = pl.BlockSpec((tk, tm), lambda i, j, k: (k, i)) if ta else pl.BlockSpec((tm, tk), lambda i, j, k: (i, k))
    b_spec = pl.BlockSpec((tn, tk), lambda i, j, k: (j, k)) if tb else pl.BlockSpec((tk, tn), lambda i, j, k: (k, j))
    o_spec = pl.BlockSpec((tm, tn), lambda i, j, k: (i, j))
    ca, cb = (0 if ta else 1), (1 if tb else 0)

    out_dtypes = [out_dtype] if epi is None else list(epi[1])
    n_out = len(out_dtypes)

    def body(*refs):
        a_ref, b_ref = refs[:2]
        o_refs, acc = refs[-1 - n_out:-1], refs[-1]
        k = pl.program_id(2)

        @pl.when(k == 0)
        def _():
            acc[...] = jnp.zeros_like(acc)

        acc[...] += _mxdot(a_ref[...], b_ref[...], ca, cb)

        @pl.when(k == nk - 1)
        def _():
            r = acc[...]
            if epi is not None:
                res = epi[0](r, refs[2][...]) if add is not None else epi[0](r)
            else:
                res = (r + refs[2][...].astype(F32) if add is not None else r,)
            for o_ref, val in zip(o_refs, res):
                o_ref[...] = val.astype(o_ref.dtype)

    in_specs, args = [a_spec, b_spec], [a, b]
    if add is not None:
        in_specs.append(o_spec)
        args.append(add)
    if dep is not None:
        in_specs.append(pl.BlockSpec((8, LANE), lambda i, j, k: (0, 0)))
        args.append(dep)
    res = pl.pallas_call(
        body, name=name, grid=(m // tm, n // tn, nk), in_specs=in_specs, out_specs=[o_spec] * n_out,
        out_shape=[jax.ShapeDtypeStruct((m, n), dt) for dt in out_dtypes], scratch_shapes=[pltpu.VMEM((tm, tn), F32)],
        compiler_params=_cparams(("parallel", "parallel", "arbitrary")))(*args)
    return res[0] if epi is None else res


def _rowwise(name, fn, *, nrows, tr, ncb=1, rows=(), fixed=(), vecs=(), tabs=(), outs=(), reds=(), tab_blocks=1):
    in_specs, args = [], []
    for arr, w, c0 in rows:
        in_specs.append(pl.BlockSpec((tr, w), lambda g, i, c0=c0: (i, c0 + g)))
        args.append(arr)
    for arr, w, c0 in fixed:
        in_specs.append(pl.BlockSpec((tr, w), lambda g, i, c0=c0: (i, c0)))
        args.append(arr)
    for arr, w, c0 in vecs:
        in_specs.append(pl.BlockSpec((1, w), lambda g, i, c0=c0: (0, c0 + g)))
        args.append(arr)
    for arr, w, c0 in tabs:
        in_specs.append(pl.BlockSpec((tr, w), lambda g, i, c0=c0: (i % tab_blocks, c0)))
        args.append(arr)
    n_in, n_out = len(args), len(outs)
    out_shape, out_specs, aliases = [], [], {}
    for k, o in enumerate(outs):
        c0 = o[3] if len(o) > 3 else 0
        out_shape.append(jax.ShapeDtypeStruct((nrows, o[0]), o[2]))
        out_specs.append(pl.BlockSpec((tr, o[1]), lambda g, i, c0=c0: (i, c0 + g)))
        if len(o) > 4:
            aliases[len(args)] = k
            in_specs.append(pl.BlockSpec(memory_space=pl.ANY))
            args.append(o[4])
    out_shape += [jax.ShapeDtypeStruct((1, wt), F32) for wt, w in reds]
    out_specs += [pl.BlockSpec((1, w), lambda g, i: (0, g)) for wt, w in reds]
    first_out = len(args)

    def body(*refs):
        res = fn(*[r[...] for r in refs[:n_in]])
        for o_ref, val in zip(refs[first_out:first_out + n_out], res[:n_out]):
            o_ref[...] = val.astype(o_ref.dtype)
        i = pl.program_id(1)
        for d_ref, val in zip(refs[first_out + n_out:], res[n_out:]):
            @pl.when(i == 0)
            def _(d_ref=d_ref, val=val):
                d_ref[...] = val

            @pl.when(i > 0)
            def _(d_ref=d_ref, val=val):
                d_ref[...] += val

    return pl.pallas_call(
        body, name=name, grid=(ncb, nrows // tr), in_specs=in_specs, out_specs=out_specs, out_shape=out_shape,
        input_output_aliases=aliases, compiler_params=_cparams(("parallel", "arbitrary")))(*args)


def _peer(k):
    x, y, c = lax.axis_index("x"), lax.axis_index("y"), lax.axis_index("c")
    px = jnp.where((k >> 2) & 1, 1 - x, x)
    py = jnp.where((k >> 1) & 1, 1 - y, y)
    pc = jnp.where(k & 1, 1 - c, c)
    return (px, py, pc), 4 * px + 2 * py + pc


def _my_index():
    return 4 * lax.axis_index("x") + 2 * lax.axis_index("y") + lax.axis_index("c")


def _exchange(name, ins, out_shapes, items):
    n_in, n_out, n_it = len(ins), len(out_shapes), len(items)

    def body(*refs):
        x, o = refs[:n_in], refs[n_in:n_in + n_out]
        send_sems, recv_sems, local_sems = refs[n_in + n_out:]
        me = _my_index()
        local, sends = [], []
        for t, (ii, io, src, dst) in enumerate(items):
            cp = pltpu.make_async_copy(src(x[ii], me), dst(o[io], me), local_sems.at[t])
            cp.start()
            local.append(cp)
        for k in range(1, N_DEV):
            dev, idx = _peer(k)
            for t, (ii, io, src, dst) in enumerate(items):
                s = (k - 1) * n_it + t
                cp = pltpu.make_async_remote_copy(
                    src_ref=src(x[ii], idx), dst_ref=dst(o[io], me), send_sem=send_sems.at[s],
                    recv_sem=recv_sems.at[s], device_id=dev, device_id_type=MESH_ID)
                cp.start()
                sends.append(cp)
        for k in range(1, N_DEV):
            dev, idx = _peer(k)
            for t, (ii, io, src, dst) in enumerate(items):
                s = (k - 1) * n_it + t
                pltpu.make_async_remote_copy(
                    src_ref=src(x[ii], idx), dst_ref=dst(o[io], idx), send_sem=send_sems.at[s],
                    recv_sem=recv_sems.at[s], device_id=dev, device_id_type=MESH_ID).wait_recv()
        for cp in sends:
            cp.wait_send()
        for cp in local:
            cp.wait()

    nsem = (N_DEV - 1) * n_it
    anyspec = pl.BlockSpec(memory_space=pl.ANY)
    return pl.pallas_call(
        body, name=name, out_shape=list(out_shapes), in_specs=[anyspec] * n_in, out_specs=[anyspec] * n_out,
        scratch_shapes=[pltpu.SemaphoreType.DMA((nsem,)), pltpu.SemaphoreType.DMA((nsem,)),
                        pltpu.SemaphoreType.DMA((n_it,))],
        compiler_params=pltpu.CompilerParams(has_side_effects=True))(*ins)


def _split_copies(x, land, send_sems, recv_sems, items, receive):
    me = _my_index()
    remote, n_it = [], len(items)
    for k in range(1, N_DEV):
        dev, idx = _peer(k)
        for t, (ii, io, src, dst) in enumerate(items):
            s = (k - 1) * n_it + t
            remote.append(pltpu.make_async_remote_copy(
                src_ref=src(x[ii], idx), dst_ref=dst(land[io], idx if receive else me), send_sem=send_sems.at[s],
                recv_sem=recv_sems.at[s], device_id=dev, device_id_type=MESH_ID))
    local = [pltpu.make_async_copy(src(x[ii], me), dst(land[io], me), send_sems.at[(N_DEV - 1) * n_it + t])
             for t, (ii, io, src, dst) in enumerate(items)]
    return remote, local


def _exchange_start(name, ins, out_shapes, items, dep=None):
    n_in, n_out, n_it = len(ins), len(out_shapes), len(items)

    def body(*refs):
        x, land = refs[:n_in], refs[n_in:n_in + n_out]
        first_out = n_in + n_out + (dep is not None)
        send_sems, recv_sems, token = refs[first_out], refs[first_out + 1], refs[-1]
        remote, local = _split_copies(x, land, send_sems, recv_sems, items, False)
        for cp in remote + local:
            cp.start()
        token[...] = jnp.zeros_like(token)

    hbm = pl.BlockSpec(memory_space=pltpu.HBM)
    sem = pl.BlockSpec(memory_space=pltpu.SEMAPHORE)
    arrs = [pltpu.with_memory_space_constraint(a, pltpu.HBM)
            for a in list(ins) + [lax.empty(s.shape, s.dtype) for s in out_shapes]]
    res = pl.pallas_call(
        body, name=name,
        out_shape=(pltpu.SemaphoreType.DMA((N_DEV * n_it,)), pltpu.SemaphoreType.DMA(((N_DEV - 1) * n_it,)),
                   *[pltpu.HBM(a.shape, a.dtype) for a in arrs], jax.ShapeDtypeStruct((8, LANE), F32)),
        in_specs=[hbm] * (n_in + n_out) + ([] if dep is None else [pl.BlockSpec(memory_space=pl.ANY)]),
        out_specs=(sem, sem, *[hbm] * (n_in + n_out), pl.BlockSpec(memory_space=pltpu.VMEM)),
        input_output_aliases={i: 2 + i for i in range(n_in + n_out)},
        compiler_params=pltpu.CompilerParams(has_side_effects=pltpu.SideEffectType.DATAFLOW_SIDE_EFFECTING))(
            *arrs, *([] if dep is None else [dep]))
    return res[:2], res[2:2 + n_in], res[2 + n_in:2 + n_in + n_out], res[-1]


def _exchange_wait(name, sems, ins, landing, items, after):
    n_in, n_out = len(ins), len(landing)

    def body(*refs):
        x, land = refs[:n_in], refs[n_in:n_in + n_out]
        send_sems, recv_sems = refs[n_in + n_out], refs[n_in + n_out + 1]
        remote, local = _split_copies(x, land, send_sems, recv_sems, items, True)
        for cp in remote:
            cp.wait_send()
            cp.wait_recv()
        for cp in local:
            cp.wait()

    hbm = pl.BlockSpec(memory_space=pltpu.HBM)
    sem = pl.BlockSpec(memory_space=pltpu.SEMAPHORE)
    arrs = list(ins) + list(landing)
    res = pl.pallas_call(
        body, name=name, out_shape=tuple(pltpu.HBM(a.shape, a.dtype) for a in arrs),
        in_specs=[hbm] * (n_in + n_out) + [sem, sem, pl.BlockSpec(memory_space=pl.ANY)],
        out_specs=tuple([hbm] * (n_in + n_out)), input_output_aliases={i: i for i in range(n_in + n_out)},
        compiler_params=pltpu.CompilerParams(has_side_effects=pltpu.SideEffectType.DATAFLOW_SIDE_EFFECTING))(
            *arrs, *sems, after)
    return res[n_in:]


def _whole(ref, p):
    return ref


def _entry(ref, p):
    return ref.at[p]


def _gather_plan(a, b, kind):
    if kind == "col" and b % LANE == 0:
        return (a, N_DEV * b), (lambda ref, p: ref.at[:, pl.ds(pl.multiple_of(p * b, b), b)]), "col"
    return (N_DEV, a, b), _entry, ("row" if kind == "row" else "stack")


def _adamw_nat(name, parts, w, m, v):
    depth, b, c = w.shape
    assert len(parts) == depth
    tb = _pick(b, (128, 64, 32, 16, 8))
    spec = pl.BlockSpec((1, tb, c), lambda i, j: (i, j, 0))

    def body(*refs):
        p_refs = refs[:depth]
        w_ref, m_ref, v_ref, g_ref, d_ref, nm_ref, nv_ref = refs[depth:]
        for layer, p_ref in enumerate(p_refs):
            @pl.when(pl.program_id(0) == layer)
            def _(p_ref=p_ref):
                g = p_ref[0].astype(F32)
                for j in range(1, N_DEV):
                    g = g + p_ref[j].astype(F32)
                nm = ADAM_B1 * m_ref[0] + (1.0 - ADAM_B1) * g
                nv = ADAM_B2 * v_ref[0] + (1.0 - ADAM_B2) * jnp.square(g)
                m_hat = nm / (1.0 - ADAM_B1 ** ADAM_STEP)
                v_hat = nv / (1.0 - ADAM_B2 ** ADAM_STEP)
                g_ref[0] = g
                d_ref[0] = -ADAM_LR * (m_hat / (jnp.sqrt(v_hat) + ADAM_EPS) + ADAM_WD * w_ref[0])
                nm_ref[0] = nm
                nv_ref[0] = nv

    sds = jax.ShapeDtypeStruct((depth, b, c), F32)
    return pl.pallas_call(
        body, name=name, grid=(depth, b // tb),
        in_specs=[pl.BlockSpec((N_DEV, tb, c), lambda i, j: (0, j, 0))] * depth + [spec, spec, spec],
        out_specs=[spec] * 4, out_shape=[sds] * 4, compiler_params=_cparams(("parallel", "parallel")))(*parts, w, m, v)


def _adamw(name, parts, w, m, v):
    rows = w.shape[0]
    tr = _pick(rows, (256, 128, 64, 32, 16, 8))
    spec = pl.BlockSpec((tr, PACK_W), lambda i: (i, 0))

    def body(p_ref, w_ref, m_ref, v_ref, g_ref, d_ref, nm_ref, nv_ref):
        g = p_ref[0]
        for j in range(1, N_DEV):
            g = g + p_ref[j]
        nm = ADAM_B1 * m_ref[...] + (1.0 - ADAM_B1) * g
        nv = ADAM_B2 * v_ref[...] + (1.0 - ADAM_B2) * jnp.square(g)
        m_hat = nm / (1.0 - ADAM_B1 ** ADAM_STEP)
        v_hat = nv / (1.0 - ADAM_B2 ** ADAM_STEP)
        g_ref[...] = g
        d_ref[...] = -ADAM_LR * (m_hat / (jnp.sqrt(v_hat) + ADAM_EPS) + ADAM_WD * w_ref[...])
        nm_ref[...] = nm
        nv_ref[...] = nv

    sds = jax.ShapeDtypeStruct((rows, PACK_W), F32)
    return pl.pallas_call(
        body, name=name, grid=(rows // tr,),
        in_specs=[pl.BlockSpec((N_DEV, tr, PACK_W), lambda i: (0, i, 0)), spec, spec, spec],
        out_specs=[spec] * 4, out_shape=[sds] * 4, compiler_params=_cparams(("parallel",)))(parts, w, m, v)


def _pack(arrs, dtype, row_mult=16):
    flat = jnp.concatenate([a.reshape(-1).astype(dtype) for a in arrs])
    unit = row_mult * PACK_W
    total = -(-flat.shape[0] // unit) * unit
    flat = jnp.pad(flat, (0, total - flat.shape[0]))
    return flat.reshape(-1, PACK_W)


def _pack_lead(arrs, dtype, row_mult):
    flat = jnp.concatenate([a.reshape(N_DEV, -1).astype(dtype) for a in arrs], axis=1)
    unit = row_mult * PACK_W
    total = -(-flat.shape[1] // unit) * unit
    flat = jnp.pad(flat, ((0, 0), (0, total - flat.shape[1])))
    return flat.reshape(N_DEV, -1, PACK_W)


def _unpack(buf, shapes, lead=()):
    flat = buf.reshape(lead + (-1,))
    out, off = [], 0
    for s in shapes:
        n = int(np.prod(s))
        out.append(flat[..., off:off + n].reshape(lead + tuple(s)))
        off += n
    return out


def _unshard(g, kind):
    if kind == "col":
        g = jnp.moveaxis(g, 0, -2)
        return g.reshape(g.shape[:-2] + (g.shape[-2] * g.shape[-1],))
    g = jnp.moveaxis(g, 0, 1)
    return g.reshape((g.shape[0], g.shape[1] * g.shape[2]) + g.shape[3:])


def _shard(full, kind):
    if kind == "col":
        s = full.reshape(full.shape[:-1] + (N_DEV, full.shape[-1] // N_DEV))
        return jnp.moveaxis(s, -2, 0)
    s = full.reshape((full.shape[0], N_DEV, full.shape[1] // N_DEV) + full.shape[2:])
    return jnp.moveaxis(s, 1, 0)


class _Geo:
    def __init__(self, bsz, seq):
        self.bsz, self.seq = bsz, seq
        self.pad = (-(N_META + seq)) % ATT_BLK
        self.lp = self.pad + N_META + seq
        assert (self.pad + N_META) % SSM_CHUNK == 0 and self.lp % SSM_CHUNK == 0
        self.nrows = bsz * self.lp
        self.nc = self.lp // SSM_CHUNK
        self.nh = SSM_D_INNER // SSM_HEAD_DIM
        self.gn = SSM_GROUPS * SSM_STATE
        self.cd = SSM_D_INNER + 2 * self.gn
        self.hq = MLA_HEADS * LANE
        order = (("z", SSM_D_INNER), ("g_ssm", D_MODEL), ("g_mla", D_MODEL), ("xs", SSM_D_INNER), ("bm", self.gn),
                 ("cm", self.gn), ("c_q", MLA_Q_LORA), ("c_kv", MLA_KV_LORA), ("dt", LANE), ("k_rope", LANE))
        self.col, off = {}, 0
        for nm, w in order:
            assert off % w == 0, (nm, off, w)
            self.col[nm] = (off, w)
            off += w
        self.pw = off
        assert self.nh <= LANE and MLA_ROPE == 64 and MLA_NOPE == LANE and MLA_V == LANE
        self.tr = _pick(self.lp, (768, 512, 384, 256, 128))
        self.tr_wide = _pick(self.lp, (384, 256, 128))

    def cb(self, nm):
        off, w = self.col[nm]
        return off // w

    def w_in_runs(self, shard_w):
        nh, half = self.nh, MLA_ROPE // 2
        src, pieces = 0, []
        for nm, n in (("z", SSM_D_INNER), ("xs", SSM_D_INNER), ("bm", self.gn), ("cm", self.gn), ("dt", nh),
                      ("c_q", MLA_Q_LORA), ("c_kv", MLA_KV_LORA), ("k_rope", MLA_ROPE), ("g_ssm", D_MODEL),
                      ("g_mla", D_MODEL)):
            dst = self.col[nm][0]
            if nm == "k_rope":
                pieces += [(src, half, dst), (src + half, half, dst + 2 * half)]
            else:
                pieces.append((src, n, dst))
            src += n
        assert src == shard_w * N_DEV
        runs = []
        for a, n, dst in pieces:
            for j in range(N_DEV):
                lo, hi = max(a, j * shard_w), min(a + n, (j + 1) * shard_w)
                if lo < hi:
                    runs.append((j, lo - j * shard_w, hi - lo, dst + lo - a))
        return runs


def _slot(a):
    h = MLA_ROPE // 2
    z = jnp.zeros(a.shape[:-1] + (h,), a.dtype)
    return jnp.concatenate([a[..., :h], z, a[..., h:], z], axis=-1)


def _unslot(a):
    h = MLA_ROPE // 2
    return jnp.concatenate([a[..., :h], a[..., 2 * h:3 * h]], axis=-1)


def _prep_layer(geo, wl):
    nh = geo.nh
    p = {}
    if "w_uq" in wl:
        uq = wl["w_uq"].reshape(MLA_Q_LORA, MLA_HEADS, MLA_NOPE + MLA_ROPE)
        p["w_qn"] = uq[..., :MLA_NOPE].reshape(MLA_Q_LORA, geo.hq)
        p["w_qp"] = _slot(uq[..., MLA_NOPE:]).reshape(MLA_Q_LORA, geo.hq)
    if "w_ukv" in wl:
        ukv = wl["w_ukv"].reshape(MLA_KV_LORA, MLA_HEADS, MLA_NOPE + MLA_V)
        p["w_k"] = ukv[..., :MLA_NOPE].reshape(MLA_KV_LORA, geo.hq)
        p["w_v"] = ukv[..., MLA_NOPE:].reshape(MLA_KV_LORA, geo.hq)
    for nm in ("w_in_p", "conv_w", "w_branch_ssm", "w_branch_mla", "w_out", "w_mlp_up", "w_mlp_down"):
        if nm in wl:
            p[nm] = wl[nm]
    for nm in ("norm_mix_w", "conv_b", "ssm_norm_w", "q_norm_w", "kv_norm_w", "norm_mlp_w"):
        if nm in wl:
            p[nm] = wl[nm].reshape(1, -1)
    if "dt_bias" in wl:
        p["dt_bias"] = jnp.pad(wl["dt_bias"], (0, LANE - nh)).reshape(1, LANE)
        p["a_log"] = jnp.pad(wl["a_log"], (0, LANE - nh)).reshape(1, LANE)
        p["d_skip_full"] = jnp.repeat(wl["d_skip"], SSM_HEAD_DIM).reshape(1, SSM_D_INNER)
    return p


def _unprep_grads(geo, g):
    nh = geo.nh
    out = {}
    if "w_qn" in g:
        qn = g["w_qn"].reshape(MLA_Q_LORA, MLA_HEADS, MLA_NOPE)
        qp = _unslot(g["w_qp"].reshape(MLA_Q_LORA, MLA_HEADS, LANE))
        out["w_uq"] = jnp.concatenate([qn, qp], axis=-1).reshape(MLA_Q_LORA, -1)
    if "w_k" in g:
        wk = g["w_k"].reshape(MLA_KV_LORA, MLA_HEADS, MLA_NOPE)
        wv = g["w_v"].reshape(MLA_KV_LORA, MLA_HEADS, MLA_V)
        out["w_ukv"] = jnp.concatenate([wk, wv], axis=-1).reshape(MLA_KV_LORA, -1)
    for nm in ("w_in_p", "w_branch_ssm", "w_branch_mla", "w_out", "w_mlp_up", "w_mlp_down", "conv_w"):
        if nm in g:
            out[nm] = g[nm]
    for nm in ("norm_mix_w", "conv_b", "ssm_norm_w", "q_norm_w", "kv_norm_w", "norm_mlp_w"):
        if nm in g:
            out[nm] = g[nm].reshape(-1)
    if "dt_bias" in g:
        out["dt_bias"] = g["dt_bias"].reshape(-1)[:nh]
        out["a_log"] = g["a_log"].reshape(-1)[:nh]
        out["d_skip"] = g["d_skip_full"].reshape(nh, SSM_HEAD_DIM).sum(-1)
    return out


def _tables(geo):
    pos = jnp.arange(geo.lp, dtype=F32) - geo.pad
    inv = ROPE_THETA ** (-jnp.arange(0, MLA_ROPE, 2, dtype=F32) / MLA_ROPE)
    ang = pos[:, None] * inv[None, :]
    cos, sin = jnp.cos(ang), jnp.sin(ang)
    z = jnp.zeros_like(cos)
    rows = jnp.arange(geo.lp)[:, None]
    return {"cos": jnp.concatenate([cos, z, cos, z], axis=-1), "sin": jnp.concatenate([-sin, z, sin, z], axis=-1),
            "valid": (rows >= geo.pad).astype(F32), "token": (rows >= geo.pad + N_META).astype(F32)}


def _w_in_assemble(geo, gathered):
    _, d, sw = gathered.shape
    runs = geo.w_in_runs(sw)
    tr = _pick(d, (256, 128))

    def body(x_ref, o_ref):
        o_ref[...] = jnp.zeros_like(o_ref)
        for j, s0, n, d0 in runs:
            o_ref[:, d0:d0 + n] = x_ref[j, :, s0:s0 + n]

    return pl.pallas_call(
        body, name="w_in_assemble", grid=(d // tr,), in_specs=[pl.BlockSpec((N_DEV, tr, sw), lambda i: (0, i, 0))],
        out_specs=pl.BlockSpec((tr, geo.pw), lambda i: (i, 0)),
        out_shape=jax.ShapeDtypeStruct((d, geo.pw), gathered.dtype), compiler_params=_cparams(("parallel",)))(gathered)


def _w_in_split(geo, g_padded, sw):
    d = g_padded.shape[0]
    runs = geo.w_in_runs(sw)
    tr = _pick(d, (128,))

    def body(x_ref, o_ref):
        for j, s0, n, d0 in runs:
            o_ref[j, :, s0:s0 + n] = x_ref[:, d0:d0 + n]

    return pl.pallas_call(
        body, name="w_in_split", grid=(d // tr,), in_specs=[pl.BlockSpec((tr, geo.pw), lambda i: (i, 0))],
        out_specs=pl.BlockSpec((N_DEV, tr, sw), lambda i: (0, i, 0)),
        out_shape=jax.ShapeDtypeStruct((N_DEV, d, sw), g_padded.dtype),
        compiler_params=_cparams(("parallel",)))(g_padded)


def _conv_cols(geo, cbw):
    x0 = geo.col["xs"][0]
    assert geo.col["bm"][0] == x0 + SSM_D_INNER and geo.col["cm"][0] == geo.col["bm"][0] + geo.gn and x0 % cbw == 0
    return lambda j: x0 // cbw + j


def _conv_taps(x):
    return [pltpu.roll(x, SSM_CONV - 1 - k, axis=0) for k in range(SSM_CONV - 1)] + [x]


def _conv_pre(x, w_ref, b_ref, taps=None):
    taps = _conv_taps(x) if taps is None else taps
    acc = b_ref[...]
    for k in range(SSM_CONV):
        acc = acc + taps[k] * w_ref[k:k + 1, :]
    return acc


def _conv_fwd(geo, proj, conv_w, conv_b):
    cbw = 256
    colmap = _conv_cols(geo, cbw)
    lp, pad = geo.lp, geo.pad

    def body(x_ref, w_ref, b_ref, o_ref):
        valid = (lax.broadcasted_iota(jnp.int32, (lp, 1), 0) >= pad).astype(F32)
        o_ref[...] = _silu(_conv_pre(x_ref[...], w_ref, b_ref)) * valid

    return pl.pallas_call(
        body, name="conv_fwd", grid=(geo.bsz, geo.cd // cbw),
        in_specs=[pl.BlockSpec((lp, cbw), lambda b, j: (b, colmap(j))),
                  pl.BlockSpec((SSM_CONV, cbw), lambda b, j: (0, j)), pl.BlockSpec((1, cbw), lambda b, j: (0, j))],
        out_specs=pl.BlockSpec((lp, cbw), lambda b, j: (b, j)),
        out_shape=jax.ShapeDtypeStruct((geo.nrows, geo.cd), F32),
        compiler_params=_cparams(("parallel", "parallel")))(proj, conv_w, conv_b)


def _conv_bwd(geo, proj, conv_w, conv_b, dxc, dproj):
    cbw = 256
    colmap = _conv_cols(geo, cbw)
    lp, pad = geo.lp, geo.pad

    def body(x_ref, w_ref, b_ref, dy_ref, _, dx_ref, gw_ref, gb_ref):
        b = pl.program_id(1)
        valid = (lax.broadcasted_iota(jnp.int32, (lp, 1), 0) >= pad).astype(F32)
        taps = _conv_taps(x_ref[...])
        pre = _conv_pre(None, w_ref, b_ref, taps)
        sig = _sigmoid(pre)
        dpre = dy_ref[...] * (sig * (1.0 + pre * (1.0 - sig))) * valid
        dx = dpre * w_ref[SSM_CONV - 1:SSM_CONV, :]
        for k in range(SSM_CONV - 1):
            dx = dx + pltpu.roll(dpre, lp - (SSM_CONV - 1 - k), axis=0) * w_ref[k:k + 1, :]
        gws = [jnp.sum(dpre * taps[k], axis=0, keepdims=True) for k in range(SSM_CONV)]
        dx_ref[...] = (dx * valid).astype(dx_ref.dtype)

        @pl.when(b == 0)
        def _():
            gw_ref[...] = jnp.zeros_like(gw_ref)
            gb_ref[...] = jnp.zeros_like(gb_ref)

        for k in range(SSM_CONV):
            gw_ref[k:k + 1, :] += gws[k]
        gb_ref[...] += jnp.sum(dpre, axis=0, keepdims=True)

    return pl.pallas_call(
        body, name="conv_bwd", grid=(geo.cd // cbw, geo.bsz),
        in_specs=[pl.BlockSpec((lp, cbw), lambda j, b: (b, colmap(j))),
                  pl.BlockSpec((SSM_CONV, cbw), lambda j, b: (0, j)), pl.BlockSpec((1, cbw), lambda j, b: (0, j)),
                  pl.BlockSpec((lp, cbw), lambda j, b: (b, j)), pl.BlockSpec(memory_space=pl.ANY)],
        out_specs=[pl.BlockSpec((lp, cbw), lambda j, b: (b, colmap(j))),
                   pl.BlockSpec((SSM_CONV, cbw), lambda j, b: (0, j)), pl.BlockSpec((1, cbw), lambda j, b: (0, j))],
        out_shape=[jax.ShapeDtypeStruct(dproj.shape, dproj.dtype),
                   jax.ShapeDtypeStruct((SSM_CONV, geo.cd), F32), jax.ShapeDtypeStruct((1, geo.cd), F32)],
        input_output_aliases={4: 0},
        compiler_params=_cparams(("parallel", "arbitrary")))(proj, conv_w, conv_b, dxc, dproj)


def _tri(q):
    r = lax.broadcasted_iota(jnp.int32, (q, q), 0)
    c = lax.broadcasted_iota(jnp.int32, (q, q), 1)
    return r >= c


def _ssd_pre(dtr, dtb, alog, valid):
    dt = _softplus(dtr + dtb) * valid
    adt = dt * (-jnp.exp(alog))
    a_cs = _dot(_tri(SSM_CHUNK).astype(F32), adt, 1, 0, precision=lax.Precision.HIGHEST)
    return dt, a_cs


def _ssd_specs(geo, rev):
    nc, q = geo.nc, SSM_CHUNK
    ci = (lambda c: nc - 1 - c) if rev else (lambda c: c)
    nxb = SSM_D_INNER // geo.gn
    return [pl.BlockSpec((q, SSM_D_INNER), lambda b, c: (b * nc + ci(c), 0)),
            pl.BlockSpec((q, geo.gn), lambda b, c: (b * nc + ci(c), nxb)),
            pl.BlockSpec((q, geo.gn), lambda b, c: (b * nc + ci(c), nxb + 1)),
            pl.BlockSpec((q, LANE), lambda b, c: (b * nc + ci(c), geo.cb("dt"))),
            pl.BlockSpec((1, LANE), lambda b, c: (0, 0)), pl.BlockSpec((1, LANE), lambda b, c: (0, 0))], ci


def _expand_heads(cols, nh):
    per = LANE // SSM_HEAD_DIM
    lane = lax.broadcasted_iota(jnp.int32, (1, LANE), 1)
    blocks = []
    for j in range(nh // per):
        blk = jnp.broadcast_to(cols[:, j * per:j * per + 1], (cols.shape[0], LANE))
        for k in range(1, per):
            blk = jnp.where(lane >= k * SSM_HEAD_DIM, cols[:, j * per + k:j * per + k + 1], blk)
        blocks.append(blk)
    return jnp.concatenate(blocks, axis=1)


def _head_maps(geo):
    e = (jnp.arange(SSM_D_INNER)[None, :] // SSM_HEAD_DIM == jnp.arange(LANE)[:, None]).astype(F32)
    return e, e.T


def _ssd_fwd_g(geo, xc, proj, dt_bias, a_log):
    q, p, n, e = SSM_CHUNK, SSM_HEAD_DIM, SSM_STATE, geo.nh // SSM_GROUPS
    nc, pad, gw = geo.nc, geo.pad, SSM_D_INNER // SSM_GROUPS
    in_specs, _ = _ssd_specs(geo, False)

    def body(xs_ref, b_ref, c_ref, dtr_ref, dtb_ref, alog_ref, y_ref, sp_ref, state, xdt_s):
        c = pl.program_id(1)

        @pl.when(c == 0)
        def _():
            state[...] = jnp.zeros_like(state)

        sp_ref[...] = state[...]
        inert = (c + 1) * q <= pad

        @pl.when(inert)
        def _():
            y_ref[...] = jnp.zeros_like(y_ref)

        @pl.when(jnp.logical_not(inert))
        def _():
            valid = (c * q + lax.broadcasted_iota(jnp.int32, (q, 1), 0) >= pad).astype(F32)
            dt, a_cs = _ssd_pre(dtr_ref[...], dtb_ref[...], alog_ref[...], valid)
            a_cst = a_cs.T
            dt_x, a_x = _expand_heads(dt, geo.nh), _expand_heads(a_cs, geo.nh)
            tri = _tri(q)
            for g in range(SSM_GROUPS):
                gs = slice(g * gw, (g + 1) * gw)
                bg, cg = b_ref[:, g * n:(g + 1) * n], c_ref[:, g * n:(g + 1) * n]
                a_g = a_x[:, gs]
                a_last = a_g[q - 1:q, :]
                xdt_g = xs_ref[:, gs] * dt_x[:, gs]
                xdt_s[:, gs] = xdt_g
                s_g = state[:, gs]
                y_ref[:, gs] = _mxdot(cg, s_g, 1, 0) * jnp.exp(a_g)
                state[:, gs] = s_g * jnp.exp(a_last) + _mxdot(bg, xdt_g * jnp.exp(a_last - a_g), 0, 0)
                cb = _mxdot(cg, bg, 1, 1)
                for hh in range(e):
                    h = g * e + hh
                    hs = slice(h * p, (h + 1) * p)
                    ldec = jnp.exp(jnp.where(tri, a_cs[:, h:h + 1] - a_cst[h:h + 1, :], -jnp.inf))
                    y_ref[:, hs] += _mxdot(cb * ldec, xdt_s[:, hs], 1, 0)

    return pl.pallas_call(
        body, name="ssd_fwd", grid=(geo.bsz, nc), in_specs=in_specs,
        out_specs=[pl.BlockSpec((q, SSM_D_INNER), lambda b, c: (b * nc + c, 0)),
                   pl.BlockSpec((n, SSM_D_INNER), lambda b, c: (b * nc + c, 0))],
        out_shape=[jax.ShapeDtypeStruct((geo.nrows, SSM_D_INNER), F32),
                   jax.ShapeDtypeStruct((geo.bsz * nc * n, SSM_D_INNER), F32)],
        scratch_shapes=[pltpu.VMEM((n, SSM_D_INNER), F32), pltpu.VMEM((q, SSM_D_INNER), F32)],
        compiler_params=_cparams(("parallel", "arbitrary")))(xc, xc, xc, proj, dt_bias, a_log)


def _ssd_bwd_g(geo, xc, proj, dt_bias, a_log, s_prev_all, dy, dxs_skip, dproj):
    q, p, n, e = SSM_CHUNK, SSM_HEAD_DIM, SSM_STATE, geo.nh // SSM_GROUPS
    nc, pad, di, gn, gw = geo.nc, geo.pad, SSM_D_INNER, geo.gn, SSM_D_INNER // SSM_GROUPS
    in_specs, ci = _ssd_specs(geo, True)
    row_spec = pl.BlockSpec((q, di), lambda b, c: (b * nc + ci(c), 0))
    e_map, _ = _head_maps(geo)
    in_specs += [pl.BlockSpec((n, di), lambda b, c: (b * nc + ci(c), 0)), row_spec, row_spec,
                 pl.BlockSpec((LANE, di), lambda b, c: (0, 0)), pl.BlockSpec(memory_space=pl.ANY)]

    def body(xs_ref, b_ref, c_ref, dtr_ref, dtb_ref, alog_ref, sp_ref, dy_ref, dsk_ref, e_ref, _,
             dxc_ref, ddt_ref, gdtb_ref, galog_ref, dstate, xdt_s, dxdt_s):
        step = pl.program_id(1)
        first = jnp.logical_and(pl.program_id(0) == 0, step == 0)
        c = nc - 1 - step

        @pl.when(step == 0)
        def _():
            dstate[...] = jnp.zeros_like(dstate)

        @pl.when(first)
        def _():
            gdtb_ref[...] = jnp.zeros_like(gdtb_ref)
            galog_ref[...] = jnp.zeros_like(galog_ref)

        inert = (c + 1) * q <= pad

        @pl.when(inert)
        def _():
            dxc_ref[...] = jnp.zeros_like(dxc_ref)
            ddt_ref[...] = jnp.zeros_like(ddt_ref)

        @pl.when(jnp.logical_not(inert))
        def _():
            valid = (c * q + lax.broadcasted_iota(jnp.int32, (q, 1), 0) >= pad).astype(F32)
            dtr, dtb, alog = dtr_ref[...], dtb_ref[...], alog_ref[...]
            dt, a_cs = _ssd_pre(dtr, dtb, alog, valid)
            a_cst = a_cs.T
            dt_x, a_x = _expand_heads(dt, geo.nh), _expand_heads(a_cs, geo.nh)
            tri = _tri(q)
            lane = lax.broadcasted_iota(jnp.int32, (1, LANE), 1)
            sub = lax.broadcasted_iota(jnp.int32, (LANE, 1), 0)
            d_dt = jnp.zeros((q, LANE), F32)
            d_acs = jnp.zeros((q, LANE), F32)
            d_acst = jnp.zeros((LANE, q), F32)
            d_last = jnp.zeros((1, LANE), F32)
            for g in range(SSM_GROUPS):
                gs = slice(g * gw, (g + 1) * gw)
                bg, cg = b_ref[:, g * n:(g + 1) * n], c_ref[:, g * n:(g + 1) * n]
                seg = lambda v: _mxdot(v, e_ref[:, gs], 1, 1)
                a_g, dt_g, x_g, dy_g = a_x[:, gs], dt_x[:, gs], xs_ref[:, gs], dy_ref[:, gs]
                e_col, e_last, dec = jnp.exp(a_g), jnp.exp(a_g[q - 1:q, :]), jnp.exp(a_g[q - 1:q, :] - a_g)
                xdt_g = x_g * dt_g
                xdt_s[:, gs] = xdt_g
                s_g, ds_g = sp_ref[:, gs], dstate[:, gs]
                cs = _mxdot(cg, s_g, 1, 0)
                d_cs = dy_g * e_col
                d_acs = d_acs + seg(d_cs * cs)
                d_cg = _mxdot(d_cs, s_g, 1, 1)
                dstate[:, gs] = _mxdot(cg, d_cs, 0, 0) + ds_g * e_last
                dl_x = jnp.sum(ds_g * s_g, axis=0, keepdims=True) * e_last
                d_last = d_last + seg(jnp.broadcast_to(dl_x, (8, gw)))[:1]
                gmat = _mxdot(bg, ds_g, 1, 0)
                xd = xdt_g * dec
                d_bg = _mxdot(xd, ds_g, 1, 1)
                d_dec = seg(xd * gmat)
                d_acs = d_acs - d_dec
                d_last = d_last + jnp.sum(d_dec, axis=0, keepdims=True)
                dxdt_s[:, gs] = dec * gmat
                cb = _mxdot(cg, bg, 1, 1)
                d_cb = jnp.zeros((q, q), F32)
                for hh in range(e):
                    h = g * e + hh
                    hs = slice(h * p, (h + 1) * p)
                    ldec = jnp.exp(jnp.where(tri, a_cs[:, h:h + 1] - a_cst[h:h + 1, :], -jnp.inf))
                    dyh = dy_ref[:, hs]
                    d_m = _mxdot(dyh, xdt_s[:, hs], 1, 1)
                    dxdt_s[:, hs] += _mxdot(cb * ldec, dyh, 0, 0)
                    d_cb = d_cb + d_m * ldec
                    d_diff = d_m * cb * ldec
                    d_acs = d_acs + jnp.sum(d_diff, axis=1, keepdims=True) * (lane == h).astype(F32)
                    d_acst = d_acst - (sub == h).astype(F32) * jnp.sum(d_diff, axis=0, keepdims=True)
                d_xdt = dxdt_s[:, gs]
                dxc_ref[:, gs] = d_xdt * dt_g + dsk_ref[:, gs]
                d_dt = d_dt + seg(d_xdt * x_g)
                dxc_ref[:, di + g * n:di + (g + 1) * n] = d_bg + _mxdot(d_cb, cg, 0, 0)
                dxc_ref[:, di + gn + g * n:di + gn + (g + 1) * n] = d_cg + _mxdot(d_cb, bg, 1, 0)
            is_last = (lax.broadcasted_iota(jnp.int32, (q, 1), 0) == q - 1).astype(F32)
            d_acs = d_acs + d_acst.T + is_last * d_last
            d_adt = _dot(_tri(q).astype(F32), d_acs, 0, 0, precision=lax.Precision.HIGHEST)
            a = -jnp.exp(alog)
            d_dt = d_dt + d_adt * a
            d_dtr = d_dt * valid * _sigmoid(dtr + dtb)
            ddt_ref[...] = d_dtr.astype(ddt_ref.dtype)
            gdtb_ref[...] += jnp.sum(d_dtr, axis=0, keepdims=True)
            galog_ref[...] += jnp.sum(d_adt * dt, axis=0, keepdims=True) * a

    vec = pl.BlockSpec((1, LANE), lambda b, c: (0, 0))
    return pl.pallas_call(
        body, name="ssd_bwd", grid=(geo.bsz, nc), in_specs=in_specs,
        out_specs=[pl.BlockSpec((q, geo.cd), lambda b, c: (b * nc + ci(c), 0)),
                   pl.BlockSpec((q, LANE), lambda b, c: (b * nc + ci(c), geo.cb("dt"))), vec, vec],
        out_shape=[jax.ShapeDtypeStruct((geo.nrows, geo.cd), F32), jax.ShapeDtypeStruct(dproj.shape, dproj.dtype),
                   jax.ShapeDtypeStruct((1, LANE), F32), jax.ShapeDtypeStruct((1, LANE), F32)],
        scratch_shapes=[pltpu.VMEM((n, di), F32), pltpu.VMEM((q, di), F32), pltpu.VMEM((q, di), F32)],
        input_output_aliases={10: 1},
        compiler_params=_cparams(("arbitrary", "arbitrary")))(
            xc, xc, xc, proj, dt_bias, a_log, s_prev_all, dy, dxs_skip, e_map, dproj)


BIAS_LANE = MLA_ROPE // 2
KEY_OFF = -1e30
ATT_SCALE = (MLA_NOPE + MLA_ROPE) ** -0.5


def _row_t(col):
    return jnp.broadcast_to(col, (col.shape[0], LANE)).T[:8]


def _attn_fwd2(geo, qn, qp, kn, kp, v):
    t, lp = ATT_BLK, geo.lp
    nb = lp // t

    def body(qn_ref, qp_ref, kn_ref, kp_ref, v_ref, o_ref, lse_ref, k_ref):
        qi = pl.program_id(2)

        @pl.when(qi == 0)
        def _():
            k_ref[:, :LANE] = kn_ref[...]
            k_ref[:, LANE:] = kp_ref[...]

        q = jnp.concatenate([qn_ref[...], qp_ref[...]], axis=1)

        def blk(kj, ntile, carry, diag):
            m, l, acc = carry
            ks = pl.ds(pl.multiple_of(kj * t, t), ntile * t)
            s = _mxdot(q, k_ref[ks, :], 1, 1) * ATT_SCALE
            if diag:
                s = jnp.where(_tri(t), s, -jnp.inf)
            m_new = jnp.maximum(m, jnp.max(s, axis=1, keepdims=True))
            pr = jnp.exp(s - m_new)
            alpha = jnp.exp(m - m_new)
            return m_new, alpha * l + jnp.sum(pr, axis=1, keepdims=True), alpha * acc + _mxdot(pr, v_ref[ks, :], 1, 0)

        carry = (jnp.full((t, 1), 2.0 * KEY_OFF, F32), jnp.zeros((t, 1), F32), jnp.zeros((t, LANE), F32))
        done = 0
        for ntile in (4, 2, 1):
            steps = (qi - done) // ntile
            carry = lax.fori_loop(0, steps, lambda j, c, d=done, n=ntile: blk(d + n * j, n, c, False), carry)
            done = done + steps * ntile
        m, l, acc = blk(qi, 1, carry, True)
        o_ref[...] = acc / l
        lse_ref[0, 0, 0] = _row_t(m + jnp.log(l))

    tile = pl.BlockSpec((t, LANE), lambda b, h, i: (b * nb + i, h))
    seq = pl.BlockSpec((lp, LANE), lambda b, h, i: (b, h))
    return pl.pallas_call(
        body, name="attn_fwd", grid=(geo.bsz, MLA_HEADS, nb),
        in_specs=[tile, tile, seq, pl.BlockSpec((lp, LANE), lambda b, h, i: (b, 0)), seq],
        out_specs=[tile, pl.BlockSpec((1, 1, 1, 8, t), lambda b, h, i: (b, h, i, 0, 0))],
        out_shape=[jax.ShapeDtypeStruct((geo.nrows, geo.hq), F32),
                   jax.ShapeDtypeStruct((geo.bsz, MLA_HEADS, nb, 8, t), F32)],
        scratch_shapes=[pltpu.VMEM((lp, 2 * LANE), MXU_DTYPE)],
        compiler_params=_cparams(("parallel", "parallel", "arbitrary")))(qn, qp, kn, kp, v)


def _attn_bwd2(geo, qn, qp, kn, kp, v, d_o, o, lse):
    t, lp = ATT_BLK, geo.lp
    nb = lp // t

    def body(qn_ref, qp_ref, kn_ref, kp_ref, v_ref, do_ref, o_ref, lse_ref,
             dqn_ref, dqp_ref, dkn_ref, dkp_ref, dv_ref, q_ref, dl_s):
        kj = pl.program_id(2)

        @pl.when(kj == 0)
        def _():
            q_ref[:, :LANE] = qn_ref[...]
            q_ref[:, LANE:] = qp_ref[...]
            dqn_ref[...] = jnp.zeros_like(dqn_ref)
            dqp_ref[...] = jnp.zeros_like(dqp_ref)
            for i in range(nb):
                rows = slice(i * t, (i + 1) * t)
                dl_s[i] = _row_t(jnp.sum(do_ref[rows, :] * o_ref[rows, :], axis=1, keepdims=True))

        k, vv = jnp.concatenate([kn_ref[...], kp_ref[...]], axis=1), v_ref[...]

        def row(ref, qi, ntile):
            return jnp.concatenate([ref[qi + i][:1, :] for i in range(ntile)], axis=1)

        def blk(qi, ntile, carry, diag):
            dk, dv = carry
            qs = pl.ds(pl.multiple_of(qi * t, t), ntile * t)
            q, d_o_blk = q_ref[qs, :], do_ref[qs, :]
            st = _mxdot(k, q, 1, 1) * ATT_SCALE
            if diag:
                keys = lax.broadcasted_iota(jnp.int32, (t, t), 0)
                st = jnp.where(keys <= lax.broadcasted_iota(jnp.int32, (t, t), 1), st, -jnp.inf)
            pt = jnp.exp(st - row(lse_ref.at[0, 0], qi, ntile))
            dst = pt * (_mxdot(vv, d_o_blk, 1, 1) - row(dl_s, qi, ntile)) * ATT_SCALE
            dq = _mxdot(dst, k, 0, 0)
            dqn_ref[qs, :] += dq[:, :LANE]
            dqp_ref[qs, :] += dq[:, LANE:]
            return dk + _mxdot(dst, q, 1, 0), dv + _mxdot(pt, d_o_blk, 1, 0)

        carry = blk(kj, 1, (jnp.zeros((t, 2 * LANE), F32), jnp.zeros((t, LANE), F32)), True)
        done = kj + 1
        for ntile in (4, 2, 1):
            steps = (nb - done) // ntile
            carry = lax.fori_loop(0, steps, lambda j, c, d=done, n=ntile: blk(d + n * j, n, c, False), carry)
            done = done + steps * ntile
        dk, dv = carry
        dkn_ref[...] = dk[:, :LANE].astype(dkn_ref.dtype)
        dkp_ref[...] = dk[:, LANE:]
        dv_ref[...] = dv.astype(dv_ref.dtype)

    seq = pl.BlockSpec((lp, LANE), lambda b, h, j: (b, h))
    tile = pl.BlockSpec((t, LANE), lambda b, h, j: (b * nb + j, h))
    return pl.pallas_call(
        body, name="attn_bwd", grid=(geo.bsz, MLA_HEADS, nb),
        in_specs=[seq, seq, tile, pl.BlockSpec((t, LANE), lambda b, h, j: (b * nb + j, 0)), tile, seq, seq,
                  pl.BlockSpec((1, 1, nb, 8, t), lambda b, h, j: (b, h, 0, 0, 0))],
        out_specs=[seq, seq, tile, tile, tile],
        out_shape=[jax.ShapeDtypeStruct((geo.nrows, geo.hq), F32), jax.ShapeDtypeStruct((geo.nrows, geo.hq), F32),
                   jax.ShapeDtypeStruct((geo.nrows, geo.hq), MXU_DTYPE), jax.ShapeDtypeStruct((geo.nrows, geo.hq), F32),
                   jax.ShapeDtypeStruct((geo.nrows, geo.hq), MXU_DTYPE)],
        scratch_shapes=[pltpu.VMEM((lp, 2 * LANE), MXU_DTYPE), pltpu.VMEM((nb, 8, t), F32)],
        compiler_params=_cparams(("parallel", "parallel", "arbitrary")))(qn, qp, kn, kp, v, d_o, o, lse)


def _rope(x, cos, sin):
    return x * cos + pltpu.roll(x, LANE // 2, axis=1) * sin


def _rope_t(dx, cos, sin):
    return dx * cos + pltpu.roll(dx * sin, LANE // 2, axis=1)


def _per_head(f):
    def fn(x, cos, sin):
        return (jnp.concatenate([f(x[:, h * LANE:(h + 1) * LANE], cos, sin) for h in range(MLA_HEADS)], axis=1),)
    return fn


def _layer_fwd(geo, h, w, tab, late=None):
    nr, tr, trw = geo.nrows, geo.tr, geo.tr_wide
    tb = geo.lp // tr
    rw = functools.partial(_rowwise, nrows=nr)
    s = {"h": h}
    (s["u"],) = rw("rms_mix", lambda x, g: (_rms(x, g),), tr=tr, rows=[(h, D_MODEL, 0)],
                   vecs=[(w["norm_mix_w"], D_MODEL, 0)], outs=[(D_MODEL, D_MODEL, MXU_DTYPE)])
    proj = s["proj"] = _mm("mm_in", s["u"], w["w_in_p"])
    xc = s["xc"] = _conv_fwd(geo, proj, w["conv_w"], w["conv_b"])
    s["y_ssd"], s["s_prev"] = _ssd_fwd_g(geo, xc, proj, w["dt_bias"], w["a_log"])
    gw = SSM_D_INNER // SSM_GROUPS

    def gate_norm(y, x, z, dsk, nw):
        return (_rms((y + x * dsk) * _silu(z), nw),)

    (s["y_ssm"],) = rw("ssm_gate_norm", gate_norm, tr=tr, ncb=SSM_GROUPS,
                       rows=[(s["y_ssd"], gw, 0), (xc, gw, 0), (proj, gw, geo.col["z"][0] // gw)],
                       vecs=[(w["d_skip_full"], gw, 0), (w["ssm_norm_w"], gw, 0)], outs=[(SSM_D_INNER, gw, MXU_DTYPE)])
    if late is not None:
        w = {**w, **late(s["y_ssm"])}
    (s["cq_n"],) = rw("rms_q", lambda x, g: (_rms(x, g),), tr=tr, rows=[(proj, MLA_Q_LORA, geo.cb("c_q"))],
                      vecs=[(w["q_norm_w"], MLA_Q_LORA, 0)], outs=[(MLA_Q_LORA, MLA_Q_LORA, MXU_DTYPE)])
    (s["ckv_n"],) = rw("rms_kv", lambda x, g: (_rms(x, g),), tr=tr, rows=[(proj, MLA_KV_LORA, geo.cb("c_kv"))],
                       vecs=[(w["kv_norm_w"], MLA_KV_LORA, 0)], outs=[(MLA_KV_LORA, MLA_KV_LORA, MXU_DTYPE)])
    s["qn"] = _mm("mm_qn", s["cq_n"], w["w_qn"], out_dtype=MXU_DTYPE)
    qp_raw = _mm("mm_qp", s["cq_n"], w["w_qp"])
    s["kn"] = _mm("mm_kn", s["ckv_n"], w["w_k"], out_dtype=MXU_DTYPE)
    s["v"] = _mm("mm_v", s["ckv_n"], w["w_v"], out_dtype=MXU_DTYPE)
    bias_lane = lambda: lax.broadcasted_iota(jnp.int32, (1, LANE), 1) == BIAS_LANE
    rope_tabs = [(tab["cos"], LANE, 0), (tab["sin"], LANE, 0)]
    (s["qp"],) = rw("rope_q", _per_head(lambda xp, c, sn: jnp.where(bias_lane(), 1.0, _rope(xp, c, sn))), tr=tr,
                    rows=[(qp_raw, geo.hq, 0)], tabs=rope_tabs, outs=[(geo.hq, geo.hq, MXU_DTYPE)], tab_blocks=tb)
    (s["kp"],) = rw("rope_k", lambda xp, c, sn, valid: (jnp.where(bias_lane(), KEY_OFF * (1.0 - valid), _rope(xp, c, sn)),),
                    tr=tr, rows=[(proj, LANE, geo.cb("k_rope"))], tabs=rope_tabs + [(tab["valid"], 1, 0)],
                    outs=[(LANE, LANE, MXU_DTYPE)], tab_blocks=tb)
    s["o"], s["lse"] = _attn_fwd2(geo, s["qn"], s["qp"], s["kn"], s["kp"], s["v"])
    s["ys_p"] = _mm("mm_bs", s["y_ssm"], w["w_branch_ssm"])
    s["ym_p"] = _mm("mm_bm", s["o"], w["w_branch_mla"])

    def gate(gs, gm, ys, ym):
        return (_sigmoid(gs) * ys + _sigmoid(gm) * ym,)

    (s["mixed"],) = rw("gate", gate, tr=tr, rows=[(proj, D_MODEL, geo.cb("g_ssm")), (proj, D_MODEL, geo.cb("g_mla")),
                                                  (s["ys_p"], D_MODEL, 0), (s["ym_p"], D_MODEL, 0)],
                       outs=[(D_MODEL, D_MODEL, MXU_DTYPE)])
    s["h2"] = _mm("mm_out", s["mixed"], w["w_out"], add=h)
    (s["vn"],) = rw("rms_mlp", lambda x, g: (_rms(x, g),), tr=tr, rows=[(s["h2"], D_MODEL, 0)],
                    vecs=[(w["norm_mlp_w"], D_MODEL, 0)], outs=[(D_MODEL, D_MODEL, MXU_DTYPE)])
    s["up"], s["act"] = _mm("mm_up", s["vn"], w["w_mlp_up"],
                            epi=(lambda r: (r, jnp.square(jnp.maximum(r, 0.0))), (F32, MXU_DTYPE)))
    return _mm("mm_down", s["act"], w["w_mlp_down"], add=s["h2"]), s, w


def _layer_bwd(geo, dh3, s, w, tab, mid=None, tail=None, dep=None):
    nr, tr, trw = geo.nrows, geo.tr, geo.tr_wide
    tb = geo.lp // tr
    rw = functools.partial(_rowwise, nrows=nr)
    g = {}
    proj = s["proj"]

    def rms_bwd(x, dy, res, gw):
        _, vjp = jax.vjp(_rms, x.astype(F32), gw)
        dx, dgw = vjp(dy.astype(F32))
        return dx + res, dgw

    def rms_bwd_nores(x, dy, gw):
        _, vjp = jax.vjp(_rms, x.astype(F32), gw)
        return vjp(dy.astype(F32))

    (dup,) = _mm("mm_down_t", dh3, w["w_mlp_down"], tb=True, add=s["up"], dep=dep,
                 epi=(lambda r, up: (r * 2.0 * jnp.maximum(up, 0.0),), (MXU_DTYPE,)))
    g["w_mlp_down"] = _mm("mm_down_g", s["act"], dh3, ta=True, out_dtype=MXU_DTYPE)
    g["w_mlp_up"] = _mm("mm_up_g", s["vn"], dup, ta=True, out_dtype=MXU_DTYPE)
    dvn = _mm("mm_up_t", dup, w["w_mlp_up"], tb=True)
    dh2, g["norm_mlp_w"] = rw("rms_mlp_bwd", rms_bwd, tr=tr,
                              rows=[(s["h2"], D_MODEL, 0), (dvn, D_MODEL, 0), (dh3, D_MODEL, 0)],
                              vecs=[(w["norm_mlp_w"], D_MODEL, 0)], outs=[(D_MODEL, D_MODEL, F32)],
                              reds=[(D_MODEL, D_MODEL)])
    dmixed = _mm("mm_out_t", dh2, w["w_out"], tb=True)
    g["w_out"] = _mm("mm_out_g", s["mixed"], dh2, ta=True, out_dtype=MXU_DTYPE)

    def gate_bwd(gs, gm, ys, ym, dm):
        f = lambda a, b, c, d: _sigmoid(a) * c + _sigmoid(b) * d
        _, vjp = jax.vjp(f, gs, gm, ys, ym)
        dgs, dgm, dys, dym = vjp(dm)
        return dys, dym, jnp.concatenate([dgs, dgm], axis=1)

    assert geo.col["g_mla"][0] == geo.col["g_ssm"][0] + D_MODEL and geo.col["g_ssm"][0] % (2 * D_MODEL) == 0
    dys_p, dym_p, dproj = rw(
        "gate_bwd", gate_bwd, tr=tr,
        rows=[(proj, D_MODEL, geo.cb("g_ssm")), (proj, D_MODEL, geo.cb("g_mla")), (s["ys_p"], D_MODEL, 0),
              (s["ym_p"], D_MODEL, 0), (dmixed, D_MODEL, 0)],
        outs=[(D_MODEL, D_MODEL, MXU_DTYPE)] * 2 + [(geo.pw, 2 * D_MODEL, MXU_DTYPE, geo.col["g_ssm"][0] // (2 * D_MODEL))])
    g["w_branch_ssm"] = _mm("mm_bs_g", s["y_ssm"], dys_p, ta=True, out_dtype=MXU_DTYPE)
    dy_ssm = _mm("mm_bs_t", dys_p, w["w_branch_ssm"], tb=True)
    g["w_branch_mla"] = _mm("mm_bm_g", s["o"], dym_p, ta=True, out_dtype=MXU_DTYPE)
    d_o = _mm("mm_bm_t", dym_p, w["w_branch_mla"], tb=True)
    dqn, dqp, dkn, dkp_h, dv = _attn_bwd2(geo, s["qn"], s["qp"], s["kn"], s["kp"], s["v"], d_o, s["o"], s["lse"])
    rope_tabs = [(tab["cos"], LANE, 0), (tab["sin"], LANE, 0)]
    (dqp_raw,) = rw("rope_q_bwd", _per_head(_rope_t), tr=tr, rows=[(dqp, geo.hq, 0)], tabs=rope_tabs,
                    outs=[(geo.hq, geo.hq, MXU_DTYPE)], tab_blocks=tb)

    def rope_k_bwd(x, c, sn):
        tot = x[:, :LANE]
        for hd in range(1, MLA_HEADS):
            tot = tot + x[:, hd * LANE:(hd + 1) * LANE]
        return (_rope_t(tot, c, sn),)

    (dproj,) = rw("rope_k_bwd", rope_k_bwd, tr=tr, rows=[(dkp_h, geo.hq, 0)], tabs=rope_tabs,
                  outs=[(geo.pw, LANE, MXU_DTYPE, geo.cb("k_rope"), dproj)], tab_blocks=tb)
    g["w_qn"] = _mm("mm_qn_g", s["cq_n"], dqn, ta=True, out_dtype=MXU_DTYPE)
    g["w_qp"] = _mm("mm_qp_g", s["cq_n"], dqp_raw, ta=True, out_dtype=MXU_DTYPE)
    dcq_n = _mm("mm_qp_t", dqp_raw, w["w_qp"], tb=True, add=_mm("mm_qn_t", dqn, w["w_qn"], tb=True))
    g["w_k"] = _mm("mm_kn_g", s["ckv_n"], dkn, ta=True, out_dtype=MXU_DTYPE)
    g["w_v"] = _mm("mm_v_g", s["ckv_n"], dv, ta=True, out_dtype=MXU_DTYPE)
    dckv_n = _mm("mm_v_t", dv, w["w_v"], tb=True, add=_mm("mm_kn_t", dkn, w["w_k"], tb=True))
    dproj, g["q_norm_w"] = rw("rms_q_bwd", rms_bwd_nores, tr=tr,
                              rows=[(proj, MLA_Q_LORA, geo.cb("c_q")), (dcq_n, MLA_Q_LORA, 0)],
                              vecs=[(w["q_norm_w"], MLA_Q_LORA, 0)],
                              outs=[(geo.pw, MLA_Q_LORA, MXU_DTYPE, geo.cb("c_q"), dproj)], reds=[(MLA_Q_LORA, MLA_Q_LORA)])
    dproj, g["kv_norm_w"] = rw("rms_kv_bwd", rms_bwd_nores, tr=tr,
                               rows=[(proj, MLA_KV_LORA, geo.cb("c_kv")), (dckv_n, MLA_KV_LORA, 0)],
                               vecs=[(w["kv_norm_w"], MLA_KV_LORA, 0)],
                               outs=[(geo.pw, MLA_KV_LORA, MXU_DTYPE, geo.cb("c_kv"), dproj)],
                               reds=[(MLA_KV_LORA, MLA_KV_LORA)])
    gw_ = SSM_D_INNER // SSM_GROUPS
    d_skip_full = w["d_skip_full"] if mid is None else w["d_skip_full"] + mid(g)[0, 0]

    def gate_norm_bwd(y, x, z, dy, dsk, nw):
        f = lambda y_, x_, z_, dsk_, nw_: _rms((y_ + x_ * dsk_) * _silu(z_), nw_)
        _, vjp = jax.vjp(f, y, x, z, dsk, nw)
        dy_, dx_, dz_, ddsk, dnw = vjp(dy)
        return dy_, dx_, dz_, ddsk, dnw

    dy_ssd, dxs_skip, dproj, g["d_skip_full"], g["ssm_norm_w"] = rw(
        "ssm_gate_norm_bwd", gate_norm_bwd, tr=tr, ncb=SSM_GROUPS,
        rows=[(s["y_ssd"], gw_, 0), (s["xc"], gw_, 0), (proj, gw_, geo.col["z"][0] // gw_), (dy_ssm, gw_, 0)],
        vecs=[(d_skip_full, gw_, 0), (w["ssm_norm_w"], gw_, 0)],
        outs=[(SSM_D_INNER, gw_, F32), (SSM_D_INNER, gw_, F32),
              (geo.pw, gw_, MXU_DTYPE, geo.col["z"][0] // gw_, dproj)],
        reds=[(SSM_D_INNER, gw_), (SSM_D_INNER, gw_)])
    dxc, dproj, g["dt_bias"], g["a_log"] = _ssd_bwd_g(geo, s["xc"], proj, w["dt_bias"], w["a_log"], s["s_prev"],
                                                     dy_ssd, dxs_skip, dproj)
    dproj, g["conv_w"], g["conv_b"] = _conv_bwd(geo, proj, w["conv_w"], w["conv_b"], dxc, dproj)
    g["w_in_p"] = _mm("mm_in_g", s["u"], dproj, ta=True, out_dtype=MXU_DTYPE)
    du = _mm("mm_in_t", dproj, w["w_in_p"], tb=True, dep=None if tail is None else tail(g))
    dh, g["norm_mix_w"] = rw("rms_mix_bwd", rms_bwd, tr=tr,
                             rows=[(s["h"], D_MODEL, 0), (du, D_MODEL, 0), (dh2, D_MODEL, 0)],
                             vecs=[(w["norm_mix_w"], D_MODEL, 0)], outs=[(D_MODEL, D_MODEL, F32)],
                             reds=[(D_MODEL, D_MODEL)])
    return dh, g


def _loss_bwd(geo, h, fw, target, tab):
    tr = geo.tr

    def fn(x, tgt, gw, tok):
        def lossf(x_, gw_):
            err = jnp.square(_rms(x_, gw_) - tgt)
            return 0.5 * jnp.sum(tok * jnp.mean(err, axis=-1, keepdims=True), axis=0, keepdims=True)

        val, vjp = jax.vjp(lossf, x, gw)
        dx, dgw = vjp(jnp.ones((1, 1), F32))
        return dx, jnp.broadcast_to(val, (1, LANE)), dgw

    return _rowwise("loss", fn, nrows=geo.nrows, tr=tr, rows=[(h, D_MODEL, 0), (target, D_MODEL, 0)],
                    vecs=[(fw, D_MODEL, 0)], tabs=[(tab["token"], 1, 0)], outs=[(D_MODEL, D_MODEL, F32)],
                    reds=[(LANE, LANE), (D_MODEL, D_MODEL)], tab_blocks=geo.lp // tr)


def kernel(x, meta_tokens, norm_mix_w, w_in, conv_w, conv_b, dt_bias, a_log, d_skip, ssm_norm_w, q_norm_w, kv_norm_w, w_uq, w_ukv, w_branch_ssm, w_branch_mla, w_out, norm_mlp_w, w_mlp_up, w_mlp_down, final_norm_w, loss_target, m_meta_tokens, m_norm_mix_w, m_w_in, m_conv_w, m_conv_b, m_dt_bias, m_a_log, m_d_skip, m_ssm_norm_w, m_q_norm_w, m_kv_norm_w, m_w_uq, m_w_ukv, m_w_branch_ssm, m_w_branch_mla, m_w_out, m_norm_mlp_w, m_w_mlp_up, m_w_mlp_down, m_final_norm_w, v_meta_tokens, v_norm_mix_w, v_w_in, v_conv_w, v_conv_b, v_dt_bias, v_a_log, v_d_skip, v_ssm_norm_w, v_q_norm_w, v_kv_norm_w, v_w_uq, v_w_ukv, v_w_branch_ssm, v_w_branch_mla, v_w_out, v_norm_mlp_w, v_w_mlp_up, v_w_mlp_down, v_final_norm_w):
    args = dict(locals())
    wts = {n: args[n] for n in WEIGHTS}
    mom = {n: args["m_" + n] for n in WEIGHTS}
    var = {n: args["v_" + n] for n in WEIGHTS}
    bsz, seq, _ = x.shape
    depth = w_in.shape[0]
    geo = _Geo(bsz, seq)
    tab = _tables(geo)

    big_names = [n for n, _ in BIG]
    sh_names = big_names + [n for n, _ in SHARDED_F32]
    kinds = dict(BIG + SHARDED_F32)
    shard3 = lambda a: a.reshape((1,) + a.shape) if a.ndim == 2 else a
    wire = {n: (MXU_DTYPE if n in big_names else F32) for n in sh_names}
    cast = {n: shard3(wts[n]).astype(wire[n]) for n in sh_names}
    per_layer = [n for n in sh_names if n != "meta_tokens"]
    small_names = ["norm_mix_w", "conv_b", "dt_bias", "a_log", "d_skip", "ssm_norm_w", "q_norm_w", "kv_norm_w",
                   "norm_mlp_w"]

    def gather_items(pairs):
        ins, outs, items, forms = [], [], [], []
        for n, i in pairs:
            a, b = cast[n].shape[1:]
            shape, dst, form = _gather_plan(a, b, kinds[n])
            items.append((len(ins), len(outs), (lambda ref, p, i=i: ref.at[i]), dst))
            ins.append(cast[n])
            outs.append(jax.ShapeDtypeStruct(shape, wire[n]))
            forms.append(form)
        return ins, outs, items, forms

    def whole_weights(pairs, forms, got):
        by_layer = {}
        for (n, i), form, g in zip(pairs, forms, got):
            if n == "w_in":
                n, g = "w_in_p", _w_in_assemble(geo, g)
            elif form == "row":
                g = g.reshape(g.shape[0] * g.shape[1], g.shape[2])
            elif form == "stack":
                g = _unshard(g, "col")
            by_layer.setdefault(i, {})[n] = g
        return by_layer

    def prep(i, whole, token=None):
        wl = dict(whole)
        wl.update({n: wts[n][i] for n in small_names})
        if token is not None:
            wl["norm_mix_w"] = wl["norm_mix_w"] + token[0, 0]
        return _prep_layer(geo, wl)

    early = ("w_in", "conv_w")
    late_names = [n for n in per_layer if n not in early]
    pairs1 = [(n, i) for i in range(1, depth) for n in per_layer]
    groups = [[(n, 0) for n in early] + [("meta_tokens", 0)], [(n, 0) for n in late_names]] + ([pairs1] if pairs1 else [])
    started = {}

    def gather_start(gi, dep=None):
        ins, outs, items, forms = gather_items(groups[gi])
        sems, thru, landing, token = _exchange_start("gather_w%d_start" % gi, ins, outs, items, dep)
        started[gi] = (groups[gi], forms, sems, thru, landing, items)
        return token

    def gathered(gi, after):
        pairs, forms, sems, thru, landing, items = started[gi]
        return whole_weights(pairs, forms, _exchange_wait("gather_w%d_wait" % gi, sems, thru, landing, items, after))

    def late0(after):
        whole = gathered(1, after)[0]
        if pairs1:
            whole["q_norm_w"] = wts["q_norm_w"][0] + gather_start(2, whole["w_out"])[0, 0]
        return _prep_layer(geo, whole)

    token = gather_start(1, gather_start(0))
    whole0 = gathered(0, token)[0]
    meta_full = whole0.pop("meta_tokens")

    meta = jnp.broadcast_to(meta_full[None], (bsz, N_META, D_MODEL))
    h = jnp.concatenate([jnp.zeros((bsz, geo.pad, D_MODEL), F32), meta, x], axis=1).reshape(geo.nrows, D_MODEL)
    target = jnp.concatenate([jnp.zeros((bsz, geo.pad + N_META, D_MODEL), F32), loss_target], axis=1)
    target = target.reshape(geo.nrows, D_MODEL)
    layers, saved = [], []
    for i in range(depth):
        if i == 0:
            w, late = prep(0, whole0, token), late0
        else:
            if i == 1:
                whole1 = gathered(2, h)
            w, late = prep(i, whole1[i]), None
        h, s, w = _layer_fwd(geo, h, w, tab, late)
        layers.append(w)
        saved.append(s)
    dh, loss_part, g_final = _loss_bwd(geo, h, final_norm_w.reshape(1, -1), target, tab)

    def scatter_items(pairs):
        ins, outs, items = [], [], []
        for n, i in pairs:
            a, b = cast[n].shape[1:]
            arr = g_meta if n == "meta_tokens" else grads[i]["w_in_p" if n == "w_in" else n]
            if n == "w_in":
                arr, src = _w_in_split(geo, arr, b), _entry
            elif kinds[n] == "row":
                src = lambda ref, p, a=a: ref.at[pl.ds(pl.multiple_of(p * a, a), a)]
            elif b % LANE == 0:
                src = lambda ref, p, b=b: ref.at[:, pl.ds(pl.multiple_of(p * b, b), b)]
            else:
                arr, src = _shard(arr, "col"), _entry
            items.append((len(ins), len(outs), src, _entry))
            ins.append(arr.astype(wire[n]))
            outs.append(jax.ShapeDtypeStruct((N_DEV, a, b), wire[n]))
        return ins, outs, items

    grads = [None] * depth
    landed, pending, res = {}, {}, {}

    def scatter_start(name, pairs):
        ins, outs, items = scatter_items(pairs)
        sems, thru, landing, token = _exchange_start(name + "_start", ins, outs, items)
        pending[name] = (pairs, sems, thru, landing, items)
        return token

    def scatter_wait(name, after):
        pairs, sems, thru, landing, items = pending[name]
        landed.update(zip(pairs, _exchange_wait(name + "_wait", sems, thru, landing, items, after)))

    def adam(n):
        parts = [landed[(n, i)] for i in range(cast[n].shape[0])]
        r = _adamw_nat("adamw_" + n, parts, shard3(wts[n]), shard3(mom[n]), shard3(var[n]))
        res[n] = [a.reshape(wts[n].shape) for a in r]

    def mid0(g):
        grads[0] = _unprep_grads(geo, g)
        return scatter_start("scatter_gb0", [(n, 0) for n in late_names])

    def tail0(g):
        grads[0] = _unprep_grads(geo, g)
        return scatter_start("scatter_ga0", [(n, 0) for n in early])

    dep = None
    for i in reversed(range(depth)):
        dh, gl = _layer_bwd(geo, dh, saved[i], layers[i], tab, *((mid0, tail0) if i == 0 else (None, None)), dep)
        grads[i] = _unprep_grads(geo, gl)
        if i == 1:
            dep = scatter_start("scatter_g1", pairs1)
    dh = dh.reshape(bsz, geo.lp, D_MODEL)
    grad_x = dh[:, geo.pad + N_META:]
    g_meta = jnp.sum(dh[:, geo.pad:geo.pad + N_META], axis=0)
    if pairs1:
        scatter_wait("scatter_g1", g_meta)
    scatter_wait("scatter_gb0", g_meta)
    for n in late_names:
        adam(n)
    g_small = {n: jnp.stack([grads[i][n] for i in range(depth)]) for n in SMALL if n != "final_norm_w"}
    g_small["final_norm_w"] = g_final.reshape(-1)
    zero = jnp.zeros((1,), F32)
    pk = lambda d, last: _pack([d[n] for n in SMALL] + [last], F32, row_mult=8)
    packed = pk(g_small, loss_part[0, :1])
    ins, outs, items = scatter_items([("meta_tokens", 0)])
    parts, landed[("meta_tokens", 0)] = _exchange(
        "gather_g", [packed] + ins, [jax.ShapeDtypeStruct((N_DEV,) + packed.shape, F32)] + outs,
        [(0, 0, _whole, _entry)] + [(1, 1, items[0][2], items[0][3])])
    adam("meta_tokens")
    scatter_wait("scatter_ga0", res["meta_tokens"][1])
    for n in early:
        adam(n)
    res_sm = _adamw("adamw_small", parts, pk(wts, zero), pk(mom, zero), pk(var, zero))
    res_sm = [_unpack(r, [wts[n].shape for n in SMALL] + [(1,)]) for r in res_sm]
    loss = res_sm[0][-1][0]

    out = [loss, grad_x]
    for k in range(4):
        named = {n: res[n][k] for n in sh_names}
        named.update(zip(SMALL, res_sm[k]))
        out += [named[n] for n in WEIGHTS]
    return tuple(out)
```

```python
import functools

import numpy as np
import jax
import jax.numpy as jnp
from jax import lax
from jax.experimental import pallas as pl
from jax.experimental.pallas import tpu as pltpu

F32 = jnp.float32
MXU_DTYPE = jnp.bfloat16

D_MODEL = 1024
N_META = 16
EPS = 1e-6
SSM_D_INNER = 2048
SSM_HEAD_DIM = 64
SSM_GROUPS = 4
SSM_STATE = 128
SSM_CONV = 4
SSM_CHUNK = 128
MLA_HEADS = 8
MLA_Q_LORA = 512
MLA_KV_LORA = 256
MLA_NOPE = 128
MLA_ROPE = 64
MLA_V = 128
ROPE_THETA = 10000.0
D_FF = 4096
ADAM_LR = 0.001
ADAM_B1 = 0.9
ADAM_B2 = 0.999
ADAM_EPS = 1e-08
ADAM_WD = 0.01
ADAM_STEP = 10

N_DEV = 8
ATT_BLK = 256
LANE = 128
PACK_W = 1024
VMEM_LIMIT = 56 * 1024 * 1024
MESH_ID = pl.DeviceIdType.MESH

BIG = (("w_in", "col"), ("w_uq", "col"), ("w_ukv", "col"), ("w_branch_ssm", "row"), ("w_branch_mla", "row"),
       ("w_out", "row"), ("w_mlp_up", "col"), ("w_mlp_down", "row"))
SHARDED_F32 = (("conv_w", "col"), ("meta_tokens", "col"))
SMALL = ("norm_mix_w", "conv_b", "dt_bias", "a_log", "d_skip", "ssm_norm_w", "q_norm_w", "kv_norm_w",
         "norm_mlp_w", "final_norm_w")
WEIGHTS = ("meta_tokens", "norm_mix_w", "w_in", "conv_w", "conv_b", "dt_bias", "a_log", "d_skip", "ssm_norm_w",
           "q_norm_w", "kv_norm_w", "w_uq", "w_ukv", "w_branch_ssm", "w_branch_mla", "w_out", "norm_mlp_w",
           "w_mlp_up", "w_mlp_down", "final_norm_w")


def _cparams(sem=None):
    return pltpu.CompilerParams(dimension_semantics=sem, vmem_limit_bytes=VMEM_LIMIT)


def _pick(n, cands):
    for c in cands:
        if n % c == 0:
            return c
    return n


def _sigmoid(x):
    return 1.0 / (1.0 + jnp.exp(-x))


def _silu(x):
    return x * _sigmoid(x)


def _softplus(x):
    t = jnp.exp(-jnp.abs(x))
    return jnp.maximum(x, 0.0) + jnp.where(t < 0.01, t * (1.0 - t * (0.5 - t * (1.0 / 3.0))), jnp.log(1.0 + t))


def _rms(x, w):
    return x * lax.rsqrt(jnp.mean(x * x, axis=-1, keepdims=True) + EPS) * w


def _dot(a, b, ca, cb, precision=None):
    return lax.dot_general(a, b, (((ca,), (cb,)), ((), ())), preferred_element_type=F32, precision=precision)


def _mxdot(a, b, ca, cb):
    return _dot(a.astype(MXU_DTYPE), b.astype(MXU_DTYPE), ca, cb)


def _mm(name, a, b, *, ta=False, tb=False, add=None, out_dtype=F32, dep=None, epi=None):
    (kdim, m) = a.shape if ta else a.shape[::-1]
    (n, k2) = b.shape if tb else b.shape[::-1]
    assert kdim == k2, (name, a.shape, b.shape)
    tm = _pick(m, (1152, 1024, 768, 512, 384, 256, 128))
    tn = _pick(n, (1024, 512, 384, 256, 128))
    tk = _pick(kdim, (1152, 1024, 768, 512, 384, 256, 128))
    nk = kdim // tk
    a_spec = pl.BlockSpec((tk, tm), lambda i, j, k: (k, i)) if ta else pl.BlockSpec((tm, tk), lambda i, j, k: (i, k))
    b_spec = pl.BlockSpec((tn, tk), lambda i, j, k: (j, k)) if tb else pl.BlockSpec((tk, tn), lambda i, j, k: (k, j))
    o_spec = pl.BlockSpec((tm, tn), lambda i, j, k: (i, j))
    ca, cb = (0 if ta else 1), (1 if tb else 0)

    out_dtypes = [out_dtype] if epi is None else list(epi[1])
    n_out = len(out_dtypes)

    def body(*refs):
        a_ref, b_ref = refs[:2]
        o_refs, acc = refs[-1 - n_out:-1], refs[-1]
        k = pl.program_id(2)

        @pl.when(k == 0)
        def _():
            acc[...] = jnp.zeros_like(acc)

        acc[...] += _mxdot(a_ref[...], b_ref[...], ca, cb)

        @pl.when(k == nk - 1)
        def _():
            r = acc[...]
            if epi is not None:
                res = epi[0](r, refs[2][...]) if add is not None else epi[0](r)
            else:
                res = (r + refs[2][...].astype(F32) if add is not None else r,)
            for o_ref, val in zip(o_refs, res):
                o_ref[...] = val.astype(o_ref.dtype)

    in_specs, args = [a_spec, b_spec], [a, b]
    if add is not None:
        in_specs.append(o_spec)
        args.append(add)
    if dep is not None:
        in_specs.append(pl.BlockSpec((8, LANE), lambda i, j, k: (0, 0)))
        args.append(dep)
    res = pl.pallas_call(
        body, name=name, grid=(m // tm, n // tn, nk), in_specs=in_specs, out_specs=[o_spec] * n_out,
        out_shape=[jax.ShapeDtypeStruct((m, n), dt) for dt in out_dtypes], scratch_shapes=[pltpu.VMEM((tm, tn), F32)],
        compiler_params=_cparams(("parallel", "parallel", "arbitrary")))(*args)
    return res[0] if epi is None else res


def _rowwise(name, fn, *, nrows, tr, ncb=1, rows=(), fixed=(), vecs=(), tabs=(), outs=(), reds=(), tab_blocks=1):
    in_specs, args = [], []
    for arr, w, c0 in rows:
        in_specs.append(pl.BlockSpec((tr, w), lambda g, i, c0=c0: (i, c0 + g)))
        args.append(arr)
    for arr, w, c0 in fixed:
        in_specs.append(pl.BlockSpec((tr, w), lambda g, i, c0=c0: (i, c0)))
        args.append(arr)
    for arr, w, c0 in vecs:
        in_specs.append(pl.BlockSpec((1, w), lambda g, i, c0=c0: (0, c0 + g)))
        args.append(arr)
    for arr, w, c0 in tabs:
        in_specs.append(pl.BlockSpec((tr, w), lambda g, i, c0=c0: (i % tab_blocks, c0)))
        args.append(arr)
    n_in, n_out = len(args), len(outs)
    out_shape, out_specs, aliases = [], [], {}
    for k, o in enumerate(outs):
        c0 = o[3] if len(o) > 3 else 0
        out_shape.append(jax.ShapeDtypeStruct((nrows, o[0]), o[2]))
        out_specs.append(pl.BlockSpec((tr, o[1]), lambda g, i, c0=c0: (i, c0 + g)))
        if len(o) > 4:
            aliases[len(args)] = k
            in_specs.append(pl.BlockSpec(memory_space=pl.ANY))
            args.append(o[4])
    out_shape += [jax.ShapeDtypeStruct((1, wt), F32) for wt, w in reds]
    out_specs += [pl.BlockSpec((1, w), lambda g, i: (0, g)) for wt, w in reds]
    first_out = len(args)

    def body(*refs):
        res = fn(*[r[...] for r in refs[:n_in]])
        for o_ref, val in zip(refs[first_out:first_out + n_out], res[:n_out]):
            o_ref[...] = val.astype(o_ref.dtype)
        i = pl.program_id(1)
        for d_ref, val in zip(refs[first_out + n_out:], res[n_out:]):
            @pl.when(i == 0)
            def _(d_ref=d_ref, val=val):
                d_ref[...] = val

            @pl.when(i > 0)
            def _(d_ref=d_ref, val=val):
                d_ref[...] += val

    return pl.pallas_call(
        body, name=name, grid=(ncb, nrows // tr), in_specs=in_specs, out_specs=out_specs, out_shape=out_shape,
        input_output_aliases=aliases, compiler_params=_cparams(("parallel", "arbitrary")))(*args)


def _peer(k):
    x, y, c = lax.axis_index("x"), lax.axis_index("y"), lax.axis_index("c")
    px = jnp.where((k >> 2) & 1, 1 - x, x)
    py = jnp.where((k >> 1) & 1, 1 - y, y)
    pc = jnp.where(k & 1, 1 - c, c)
    return (px, py, pc), 4 * px + 2 * py + pc


def _my_index():
    return 4 * lax.axis_index("x") + 2 * lax.axis_index("y") + lax.axis_index("c")


def _exchange(name, ins, out_shapes, items):
    n_in, n_out, n_it = len(ins), len(out_shapes), len(items)

    def body(*refs):
        x, o = refs[:n_in], refs[n_in:n_in + n_out]
        send_sems, recv_sems, local_sems = refs[n_in + n_out:]
        me = _my_index()
        local, sends = [], []
        for t, (ii, io, src, dst) in enumerate(items):
            cp = pltpu.make_async_copy(src(x[ii], me), dst(o[io], me), local_sems.at[t])
            cp.start()
            local.append(cp)
        for k in range(1, N_DEV):
            dev, idx = _peer(k)
            for t, (ii, io, src, dst) in enumerate(items):
                s = (k - 1) * n_it + t
                cp = pltpu.make_async_remote_copy(
                    src_ref=src(x[ii], idx), dst_ref=dst(o[io], me), send_sem=send_sems.at[s],
                    recv_sem=recv_sems.at[s], device_id=dev, device_id_type=MESH_ID)
                cp.start()
                sends.append(cp)
        for k in range(1, N_DEV):
            dev, idx = _peer(k)
            for t, (ii, io, src, dst) in enumerate(items):
                s = (k - 1) * n_it + t
                pltpu.make_async_remote_copy(
                    src_ref=src(x[ii], idx), dst_ref=dst(o[io], idx), send_sem=send_sems.at[s],
                    recv_sem=recv_sems.at[s], device_id=dev, device_id_type=MESH_ID).wait_recv()
        for cp in sends:
            cp.wait_send()
        for cp in local:
            cp.wait()

    nsem = (N_DEV - 1) * n_it
    anyspec = pl.BlockSpec(memory_space=pl.ANY)
    return pl.pallas_call(
        body, name=name, out_shape=list(out_shapes), in_specs=[anyspec] * n_in, out_specs=[anyspec] * n_out,
        scratch_shapes=[pltpu.SemaphoreType.DMA((nsem,)), pltpu.SemaphoreType.DMA((nsem,)),
                        pltpu.SemaphoreType.DMA((n_it,))],
        compiler_params=pltpu.CompilerParams(has_side_effects=True))(*ins)


def _split_copies(x, land, send_sems, recv_sems, items, receive):
    me = _my_index()
    remote, n_it = [], len(items)
    for k in range(1, N_DEV):
        dev, idx = _peer(k)
        for t, (ii, io, src, dst) in enumerate(items):
            s = (k - 1) * n_it + t
            remote.append(pltpu.make_async_remote_copy(
                src_ref=src(x[ii], idx), dst_ref=dst(land[io], idx if receive else me), send_sem=send_sems.at[s],
                recv_sem=recv_sems.at[s], device_id=dev, device_id_type=MESH_ID))
    local = [pltpu.make_async_copy(src(x[ii], me), dst(land[io], me), send_sems.at[(N_DEV - 1) * n_it + t])
             for t, (ii, io, src, dst) in enumerate(items)]
    return remote, local


def _exchange_start(name, ins, out_shapes, items, dep=None):
    n_in, n_out, n_it = len(ins), len(out_shapes), len(items)

    def body(*refs):
        x, land = refs[:n_in], refs[n_in:n_in + n_out]
        first_out = n_in + n_out + (dep is not None)
        send_sems, recv_sems, token = refs[first_out], refs[first_out + 1], refs[-1]
        remote, local = _split_copies(x, land, send_sems, recv_sems, items, False)
        for cp in remote + local:
            cp.start()
        token[...] = jnp.zeros_like(token)

    hbm = pl.BlockSpec(memory_space=pltpu.HBM)
    sem = pl.BlockSpec(memory_space=pltpu.SEMAPHORE)
    arrs = [pltpu.with_memory_space_constraint(a, pltpu.HBM)
            for a in list(ins) + [lax.empty(s.shape, s.dtype) for s in out_shapes]]
    res = pl.pallas_call(
        body, name=name,
        out_shape=(pltpu.SemaphoreType.DMA((N_DEV * n_it,)), pltpu.SemaphoreType.DMA(((N_DEV - 1) * n_it,)),
                   *[pltpu.HBM(a.shape, a.dtype) for a in arrs], jax.ShapeDtypeStruct((8, LANE), F32)),
        in_specs=[hbm] * (n_in + n_out) + ([] if dep is None else [pl.BlockSpec(memory_space=pl.ANY)]),
        out_specs=(sem, sem, *[hbm] * (n_in + n_out), pl.BlockSpec(memory_space=pltpu.VMEM)),
        input_output_aliases={i: 2 + i for i in range(n_in + n_out)},
        compiler_params=pltpu.CompilerParams(has_side_effects=pltpu.SideEffectType.DATAFLOW_SIDE_EFFECTING))(
            *arrs, *([] if dep is None else [dep]))
    return res[:2], res[2:2 + n_in], res[2 + n_in:2 + n_in + n_out], res[-1]


def _exchange_wait(name, sems, ins, landing, items, after):
    n_in, n_out = len(ins), len(landing)

    def body(*refs):
        x, land = refs[:n_in], refs[n_in:n_in + n_out]
        send_sems, recv_sems = refs[n_in + n_out], refs[n_in + n_out + 1]
        remote, local = _split_copies(x, land, send_sems, recv_sems, items, True)
        for cp in remote:
            cp.wait_send()
            cp.wait_recv()
        for cp in local:
            cp.wait()

    hbm = pl.BlockSpec(memory_space=pltpu.HBM)
    sem = pl.BlockSpec(memory_space=pltpu.SEMAPHORE)
    arrs = list(ins) + list(landing)
    res = pl.pallas_call(
        body, name=name, out_shape=tuple(pltpu.HBM(a.shape, a.dtype) for a in arrs),
        in_specs=[hbm] * (n_in + n_out) + [sem, sem, pl.BlockSpec(memory_space=pl.ANY)],
        out_specs=tuple([hbm] * (n_in + n_out)), input_output_aliases={i: i for i in range(n_in + n_out)},
        compiler_params=pltpu.CompilerParams(has_side_effects=pltpu.SideEffectType.DATAFLOW_SIDE_EFFECTING))(
            *arrs, *sems, after)
    return res[n_in:]


def _whole(ref, p):
    return ref


def _entry(ref, p):
    return ref.at[p]


def _gather_plan(a, b, kind):
    if kind == "col" and b % LANE == 0:
        return (a, N_DEV * b), (lambda ref, p: ref.at[:, pl.ds(pl.multiple_of(p * b, b), b)]), "col"
    return (N_DEV, a, b), _entry, ("row" if kind == "row" else "stack")


def _adamw_nat(name, parts, w, m, v):
    depth, b, c = w.shape
    assert len(parts) == depth
    tb = _pick(b, (128, 64, 32, 16, 8))
    spec = pl.BlockSpec((1, tb, c), lambda i, j: (i, j, 0))

    def body(*refs):
        p_refs = refs[:depth]
        w_ref, m_ref, v_ref, g_ref, d_ref, nm_ref, nv_ref = refs[depth:]
        for layer, p_ref in enumerate(p_refs):
            @pl.when(pl.program_id(0) == layer)
            def _(p_ref=p_ref):
                g = p_ref[0].astype(F32)
                for j in range(1, N_DEV):
                    g = g + p_ref[j].astype(F32)
                nm = ADAM_B1 * m_ref[0] + (1.0 - ADAM_B1) * g
                nv = ADAM_B2 * v_ref[0] + (1.0 - ADAM_B2) * jnp.square(g)
                m_hat = nm / (1.0 - ADAM_B1 ** ADAM_STEP)
                v_hat = nv / (1.0 - ADAM_B2 ** ADAM_STEP)
                g_ref[0] = g
                d_ref[0] = -ADAM_LR * (m_hat / (jnp.sqrt(v_hat) + ADAM_EPS) + ADAM_WD * w_ref[0])
                nm_ref[0] = nm
                nv_ref[0] = nv

    sds = jax.ShapeDtypeStruct((depth, b, c), F32)
    return pl.pallas_call(
        body, name=name, grid=(depth, b // tb),
        in_specs=[pl.BlockSpec((N_DEV, tb, c), lambda i, j: (0, j, 0))] * depth + [spec, spec, spec],
        out_specs=[spec] * 4, out_shape=[sds] * 4, compiler_params=_cparams(("parallel", "parallel")))(*parts, w, m, v)


def _adamw(name, parts, w, m, v):
    rows = w.shape[0]
    tr = _pick(rows, (256, 128, 64, 32, 16, 8))
    spec = pl.BlockSpec((tr, PACK_W), lambda i: (i, 0))

    def body(p_ref, w_ref, m_ref, v_ref, g_ref, d_ref, nm_ref, nv_ref):
        g = p_ref[0]
        for j in range(1, N_DEV):
            g = g + p_ref[j]
        nm = ADAM_B1 * m_ref[...] + (1.0 - ADAM_B1) * g
        nv = ADAM_B2 * v_ref[...] + (1.0 - ADAM_B2) * jnp.square(g)
        m_hat = nm / (1.0 - ADAM_B1 ** ADAM_STEP)
        v_hat = nv / (1.0 - ADAM_B2 ** ADAM_STEP)
        g_ref[...] = g
        d_ref[...] = -ADAM_LR * (m_hat / (jnp.sqrt(v_hat) + ADAM_EPS) + ADAM_WD * w_ref[...])
        nm_ref[...] = nm
        nv_ref[...] = nv

    sds = jax.ShapeDtypeStruct((rows, PACK_W), F32)
    return pl.pallas_call(
        body, name=name, grid=(rows // tr,),
        in_specs=[pl.BlockSpec((N_DEV, tr, PACK_W), lambda i: (0, i, 0)), spec, spec, spec],
        out_specs=[spec] * 4, out_shape=[sds] * 4, compiler_params=_cparams(("parallel",)))(parts, w, m, v)


def _pack(arrs, dtype, row_mult=16):
    flat = jnp.concatenate([a.reshape(-1).astype(dtype) for a in arrs])
    unit = row_mult * PACK_W
    total = -(-flat.shape[0] // unit) * unit
    flat = jnp.pad(flat, (0, total - flat.shape[0]))
    return flat.reshape(-1, PACK_W)


def _pack_lead(arrs, dtype, row_mult):
    flat = jnp.concatenate([a.reshape(N_DEV, -1).astype(dtype) for a in arrs], axis=1)
    unit = row_mult * PACK_W
    total = -(-flat.shape[1] // unit) * unit
    flat = jnp.pad(flat, ((0, 0), (0, total - flat.shape[1])))
    return flat.reshape(N_DEV, -1, PACK_W)


def _unpack(buf, shapes, lead=()):
    flat = buf.reshape(lead + (-1,))
    out, off = [], 0
    for s in shapes:
        n = int(np.prod(s))
        out.append(flat[..., off:off + n].reshape(lead + tuple(s)))
        off += n
    return out


def _unshard(g, kind):
    if kind == "col":
        g = jnp.moveaxis(g, 0, -2)
        return g.reshape(g.shape[:-2] + (g.shape[-2] * g.shape[-1],))
    g = jnp.moveaxis(g, 0, 1)
    return g.reshape((g.shape[0], g.shape[1] * g.shape[2]) + g.shape[3:])


def _shard(full, kind):
    if kind == "col":
        s = full.reshape(full.shape[:-1] + (N_DEV, full.shape[-1] // N_DEV))
        return jnp.moveaxis(s, -2, 0)
    s = full.reshape((full.shape[0], N_DEV, full.shape[1] // N_DEV) + full.shape[2:])
    return jnp.moveaxis(s, 1, 0)


class _Geo:
    def __init__(self, bsz, seq):
        self.bsz, self.seq = bsz, seq
        self.pad = (-(N_META + seq)) % ATT_BLK
        self.lp = self.pad + N_META + seq
        assert (self.pad + N_META) % SSM_CHUNK == 0 and self.lp % SSM_CHUNK == 0
        self.nrows = bsz * self.lp
        self.nc = self.lp // SSM_CHUNK
        self.nh = SSM_D_INNER // SSM_HEAD_DIM
        self.gn = SSM_GROUPS * SSM_STATE
        self.cd = SSM_D_INNER + 2 * self.gn
        self.hq = MLA_HEADS * LANE
        order = (("z", SSM_D_INNER), ("g_ssm", D_MODEL), ("g_mla", D_MODEL), ("xs", SSM_D_INNER), ("bm", self.gn),
                 ("cm", self.gn), ("c_q", MLA_Q_LORA), ("c_kv", MLA_KV_LORA), ("dt", LANE), ("k_rope", LANE))
        self.col, off = {}, 0
        for nm, w in order:
            assert off % w == 0, (nm, off, w)
            self.col[nm] = (off, w)
            off += w
        self.pw = off
        assert self.nh <= LANE and MLA_ROPE == 64 and MLA_NOPE == LANE and MLA_V == LANE
        self.tr = _pick(self.lp, (768, 512, 384, 256, 128))
        self.tr_wide = _pick(self.lp, (384, 256, 128))

    def cb(self, nm):
        off, w = self.col[nm]
        return off // w

    def w_in_runs(self, shard_w):
        nh, half = self.nh, MLA_ROPE // 2
        src, pieces = 0, []
        for nm, n in (("z", SSM_D_INNER), ("xs", SSM_D_INNER), ("bm", self.gn), ("cm", self.gn), ("dt", nh),
                      ("c_q", MLA_Q_LORA), ("c_kv", MLA_KV_LORA), ("k_rope", MLA_ROPE), ("g_ssm", D_MODEL),
                      ("g_mla", D_MODEL)):
            dst = self.col[nm][0]
            if nm == "k_rope":
                pieces += [(src, half, dst), (src + half, half, dst + 2 * half)]
            else:
                pieces.append((src, n, dst))
            src += n
        assert src == shard_w * N_DEV
        runs = []
        for a, n, dst in pieces:
            for j in range(N_DEV):
                lo, hi = max(a, j * shard_w), min(a + n, (j + 1) * shard_w)
                if lo < hi:
                    runs.append((j, lo - j * shard_w, hi - lo, dst + lo - a))
        return runs


def _slot(a):
    h = MLA_ROPE // 2
    z = jnp.zeros(a.shape[:-1] + (h,), a.dtype)
    return jnp.concatenate([a[..., :h], z, a[..., h:], z], axis=-1)


def _unslot(a):
    h = MLA_ROPE // 2
    return jnp.concatenate([a[..., :h], a[..., 2 * h:3 * h]], axis=-1)


def _prep_layer(geo, wl):
    nh = geo.nh
    p = {}
    if "w_uq" in wl:
        uq = wl["w_uq"].reshape(MLA_Q_LORA, MLA_HEADS, MLA_NOPE + MLA_ROPE)
        p["w_qn"] = uq[..., :MLA_NOPE].reshape(MLA_Q_LORA, geo.hq)
        p["w_qp"] = _slot(uq[..., MLA_NOPE:]).reshape(MLA_Q_LORA, geo.hq)
    if "w_ukv" in wl:
        ukv = wl["w_ukv"].reshape(MLA_KV_LORA, MLA_HEADS, MLA_NOPE + MLA_V)
        p["w_k"] = ukv[..., :MLA_NOPE].reshape(MLA_KV_LORA, geo.hq)
        p["w_v"] = ukv[..., MLA_NOPE:].reshape(MLA_KV_LORA, geo.hq)
    for nm in ("w_in_p", "conv_w", "w_branch_ssm", "w_branch_mla", "w_out", "w_mlp_up", "w_mlp_down"):
        if nm in wl:
            p[nm] = wl[nm]
    for nm in ("norm_mix_w", "conv_b", "ssm_norm_w", "q_norm_w", "kv_norm_w", "norm_mlp_w"):
        if nm in wl:
            p[nm] = wl[nm].reshape(1, -1)
    if "dt_bias" in wl:
        p["dt_bias"] = jnp.pad(wl["dt_bias"], (0, LANE - nh)).reshape(1, LANE)
        p["a_log"] = jnp.pad(wl["a_log"], (0, LANE - nh)).reshape(1, LANE)
        p["d_skip_full"] = jnp.repeat(wl["d_skip"], SSM_HEAD_DIM).reshape(1, SSM_D_INNER)
    return p


def _unprep_grads(geo, g):
    nh = geo.nh
    out = {}
    if "w_qn" in g:
        qn = g["w_qn"].reshape(MLA_Q_LORA, MLA_HEADS, MLA_NOPE)
        qp = _unslot(g["w_qp"].reshape(MLA_Q_LORA, MLA_HEADS, LANE))
        out["w_uq"] = jnp.concatenate([qn, qp], axis=-1).reshape(MLA_Q_LORA, -1)
    if "w_k" in g:
        wk = g["w_k"].reshape(MLA_KV_LORA, MLA_HEADS, MLA_NOPE)
        wv = g["w_v"].reshape(MLA_KV_LORA, MLA_HEADS, MLA_V)
        out["w_ukv"] = jnp.concatenate([wk, wv], axis=-1).reshape(MLA_KV_LORA, -1)
    for nm in ("w_in_p", "w_branch_ssm", "w_branch_mla", "w_out", "w_mlp_up", "w_mlp_down", "conv_w"):
        if nm in g:
            out[nm] = g[nm]
    for nm in ("norm_mix_w", "conv_b", "ssm_norm_w", "q_norm_w", "kv_norm_w", "norm_mlp_w"):
        if nm in g:
            out[nm] = g[nm].reshape(-1)
    if "dt_bias" in g:
        out["dt_bias"] = g["dt_bias"].reshape(-1)[:nh]
        out["a_log"] = g["a_log"].reshape(-1)[:nh]
        out["d_skip"] = g["d_skip_full"].reshape(nh, SSM_HEAD_DIM).sum(-1)
    return out


def _tables(geo):
    pos = jnp.arange(geo.lp, dtype=F32) - geo.pad
    inv = ROPE_THETA ** (-jnp.arange(0, MLA_ROPE, 2, dtype=F32) / MLA_ROPE)
    ang = pos[:, None] * inv[None, :]
    cos, sin = jnp.cos(ang), jnp.sin(ang)
    z = jnp.zeros_like(cos)
    rows = jnp.arange(geo.lp)[:, None]
    return {"cos": jnp.concatenate([cos, z, cos, z], axis=-1), "sin": jnp.concatenate([-sin, z, sin, z], axis=-1),
            "valid": (rows >= geo.pad).astype(F32), "token": (rows >= geo.pad + N_META).astype(F32)}


def _w_in_assemble(geo, gathered):
    _, d, sw = gathered.shape
    runs = geo.w_in_runs(sw)
    tr = _pick(d, (256, 128))

    def body(x_ref, o_ref):
        o_ref[...] = jnp.zeros_like(o_ref)
        for j, s0, n, d0 in runs:
            o_ref[:, d0:d0 + n] = x_ref[j, :, s0:s0 + n]

    return pl.pallas_call(
        body, name="w_in_assemble", grid=(d // tr,), in_specs=[pl.BlockSpec((N_DEV, tr, sw), lambda i: (0, i, 0))],
        out_specs=pl.BlockSpec((tr, geo.pw), lambda i: (i, 0)),
        out_shape=jax.ShapeDtypeStruct((d, geo.pw), gathered.dtype), compiler_params=_cparams(("parallel",)))(gathered)


def _w_in_split(geo, g_padded, sw):
    d = g_padded.shape[0]
    runs = geo.w_in_runs(sw)
    tr = _pick(d, (128,))

    def body(x_ref, o_ref):
        for j, s0, n, d0 in runs:
            o_ref[j, :, s0:s0 + n] = x_ref[:, d0:d0 + n]

    return pl.pallas_call(
        body, name="w_in_split", grid=(d // tr,), in_specs=[pl.BlockSpec((tr, geo.pw), lambda i: (i, 0))],
        out_specs=pl.BlockSpec((N_DEV, tr, sw), lambda i: (0, i, 0)),
        out_shape=jax.ShapeDtypeStruct((N_DEV, d, sw), g_padded.dtype),
        compiler_params=_cparams(("parallel",)))(g_padded)


def _conv_cols(geo, cbw):
    x0 = geo.col["xs"][0]
    assert geo.col["bm"][0] == x0 + SSM_D_INNER and geo.col["cm"][0] == geo.col["bm"][0] + geo.gn and x0 % cbw == 0
    return lambda j: x0 // cbw + j


def _conv_taps(x):
    return [pltpu.roll(x, SSM_CONV - 1 - k, axis=0) for k in range(SSM_CONV - 1)] + [x]


def _conv_pre(x, w_ref, b_ref, taps=None):
    taps = _conv_taps(x) if taps is None else taps
    acc = b_ref[...]
    for k in range(SSM_CONV):
        acc = acc + taps[k] * w_ref[k:k + 1, :]
    return acc


def _conv_fwd(geo, proj, conv_w, conv_b):
    cbw = 256
    colmap = _conv_cols(geo, cbw)
    lp, pad = geo.lp, geo.pad

    def body(x_ref, w_ref, b_ref, o_ref):
        valid = (lax.broadcasted_iota(jnp.int32, (lp, 1), 0) >= pad).astype(F32)
        o_ref[...] = (_silu(_conv_pre(x_ref[...], w_ref, b_ref)) * valid).astype(o_ref.dtype)

    return pl.pallas_call(
        body, name="conv_fwd", grid=(geo.bsz, geo.cd // cbw),
        in_specs=[pl.BlockSpec((lp, cbw), lambda b, j: (b, colmap(j))),
                  pl.BlockSpec((SSM_CONV, cbw), lambda b, j: (0, j)), pl.BlockSpec((1, cbw), lambda b, j: (0, j))],
        out_specs=pl.BlockSpec((lp, cbw), lambda b, j: (b, j)),
        out_shape=jax.ShapeDtypeStruct((geo.nrows, geo.cd), MXU_DTYPE),
        compiler_params=_cparams(("parallel", "parallel")))(proj, conv_w, conv_b)


def _conv_bwd(geo, proj, conv_w, conv_b, dxc, dproj):
    cbw = 256
    colmap = _conv_cols(geo, cbw)
    lp, pad = geo.lp, geo.pad

    def body(x_ref, w_ref, b_ref, dy_ref, _, dx_ref, gw_ref, gb_ref):
        b = pl.program_id(1)
        valid = (lax.broadcasted_iota(jnp.int32, (lp, 1), 0) >= pad).astype(F32)
        taps = _conv_taps(x_ref[...])
        pre = _conv_pre(None, w_ref, b_ref, taps)
        sig = _sigmoid(pre)
        dpre = dy_ref[...] * (sig * (1.0 + pre * (1.0 - sig))) * valid
        dx = dpre * w_ref[SSM_CONV - 1:SSM_CONV, :]
        for k in range(SSM_CONV - 1):
            dx = dx + pltpu.roll(dpre, lp - (SSM_CONV - 1 - k), axis=0) * w_ref[k:k + 1, :]
        gws = [jnp.sum(dpre * taps[k], axis=0, keepdims=True) for k in range(SSM_CONV)]
        dx_ref[...] = (dx * valid).astype(dx_ref.dtype)

        @pl.when(b == 0)
        def _():
            gw_ref[...] = jnp.zeros_like(gw_ref)
            gb_ref[...] = jnp.zeros_like(gb_ref)

        for k in range(SSM_CONV):
            gw_ref[k:k + 1, :] += gws[k]
        gb_ref[...] += jnp.sum(dpre, axis=0, keepdims=True)

    return pl.pallas_call(
        body, name="conv_bwd", grid=(geo.cd // cbw, geo.bsz),
        in_specs=[pl.BlockSpec((lp, cbw), lambda j, b: (b, colmap(j))),
                  pl.BlockSpec((SSM_CONV, cbw), lambda j, b: (0, j)), pl.BlockSpec((1, cbw), lambda j, b: (0, j)),
                  pl.BlockSpec((lp, cbw), lambda j, b: (b, j)), pl.BlockSpec(memory_space=pl.ANY)],
        out_specs=[pl.BlockSpec((lp, cbw), lambda j, b: (b, colmap(j))),
                   pl.BlockSpec((SSM_CONV, cbw), lambda j, b: (0, j)), pl.BlockSpec((1, cbw), lambda j, b: (0, j))],
        out_shape=[jax.ShapeDtypeStruct(dproj.shape, dproj.dtype),
                   jax.ShapeDtypeStruct((SSM_CONV, geo.cd), F32), jax.ShapeDtypeStruct((1, geo.cd), F32)],
        input_output_aliases={4: 0},
        compiler_params=_cparams(("parallel", "arbitrary")))(proj, conv_w, conv_b, dxc, dproj)


def _tri(q):
    r = lax.broadcasted_iota(jnp.int32, (q, q), 0)
    c = lax.broadcasted_iota(jnp.int32, (q, q), 1)
    return r >= c


def _ssd_pre(dtr, dtb, alog, valid):
    dt = _softplus(dtr + dtb) * valid
    adt = dt * (-jnp.exp(alog))
    a_cs = _dot(_tri(SSM_CHUNK).astype(F32), adt, 1, 0, precision=lax.Precision.HIGHEST)
    return dt, a_cs


def _ssd_specs(geo, rev):
    nc, q = geo.nc, SSM_CHUNK
    ci = (lambda c: nc - 1 - c) if rev else (lambda c: c)
    nxb = SSM_D_INNER // geo.gn
    return [pl.BlockSpec((q, SSM_D_INNER), lambda b, c: (b * nc + ci(c), 0)),
            pl.BlockSpec((q, geo.gn), lambda b, c: (b * nc + ci(c), nxb)),
            pl.BlockSpec((q, geo.gn), lambda b, c: (b * nc + ci(c), nxb + 1)),
            pl.BlockSpec((q, LANE), lambda b, c: (b * nc + ci(c), geo.cb("dt"))),
            pl.BlockSpec((1, LANE), lambda b, c: (0, 0)), pl.BlockSpec((1, LANE), lambda b, c: (0, 0))], ci


def _expand_heads(cols, nh):
    per = LANE // SSM_HEAD_DIM
    lane = lax.broadcasted_iota(jnp.int32, (1, LANE), 1)
    blocks = []
    for j in range(nh // per):
        blk = jnp.broadcast_to(cols[:, j * per:j * per + 1], (cols.shape[0], LANE))
        for k in range(1, per):
            blk = jnp.where(lane >= k * SSM_HEAD_DIM, cols[:, j * per + k:j * per + k + 1], blk)
        blocks.append(blk)
    return jnp.concatenate(blocks, axis=1)


def _head_maps(geo):
    e = (jnp.arange(SSM_D_INNER)[None, :] // SSM_HEAD_DIM == jnp.arange(LANE)[:, None]).astype(F32)
    return e, e.T


def _ssd_fwd_g(geo, xc, proj, dt_bias, a_log):
    q, p, n, e = SSM_CHUNK, SSM_HEAD_DIM, SSM_STATE, geo.nh // SSM_GROUPS
    nc, pad, gw = geo.nc, geo.pad, SSM_D_INNER // SSM_GROUPS
    in_specs, _ = _ssd_specs(geo, False)

    def body(xs_ref, b_ref, c_ref, dtr_ref, dtb_ref, alog_ref, y_ref, sp_ref, state, xdt_s, y_s):
        c = pl.program_id(1)

        @pl.when(c == 0)
        def _():
            state[...] = jnp.zeros_like(state)

        sp_ref[...] = state[...]
        inert = (c + 1) * q <= pad

        @pl.when(inert)
        def _():
            y_ref[...] = jnp.zeros_like(y_ref)

        @pl.when(jnp.logical_not(inert))
        def _():
            valid = (c * q + lax.broadcasted_iota(jnp.int32, (q, 1), 0) >= pad).astype(F32)
            dt, a_cs = _ssd_pre(dtr_ref[...], dtb_ref[...], alog_ref[...], valid)
            a_cst = a_cs.T
            dt_x, a_x = _expand_heads(dt, geo.nh), _expand_heads(a_cs, geo.nh)
            tri = _tri(q)
            for g in range(SSM_GROUPS):
                gs = slice(g * gw, (g + 1) * gw)
                bg, cg = b_ref[:, g * n:(g + 1) * n], c_ref[:, g * n:(g + 1) * n]
                a_g = a_x[:, gs]
                a_last = a_g[q - 1:q, :]
                xdt_g = xs_ref[:, gs] * dt_x[:, gs]
                xdt_s[:, gs] = xdt_g
                s_g = state[:, gs]
                y_s[:, gs] = _mxdot(cg, s_g, 1, 0) * jnp.exp(a_g)
                state[:, gs] = s_g * jnp.exp(a_last) + _mxdot(bg, xdt_g * jnp.exp(a_last - a_g), 0, 0)
                cb = _mxdot(cg, bg, 1, 1)
                for hh in range(e):
                    h = g * e + hh
                    hs = slice(h * p, (h + 1) * p)
                    ldec = jnp.exp(jnp.where(tri, a_cs[:, h:h + 1] - a_cst[h:h + 1, :], -jnp.inf))
                    y_s[:, hs] += _mxdot(cb * ldec, xdt_s[:, hs], 1, 0)
            y_ref[...] = y_s[...].astype(y_ref.dtype)

    return pl.pallas_call(
        body, name="ssd_fwd", grid=(geo.bsz, nc), in_specs=in_specs,
        out_specs=[pl.BlockSpec((q, SSM_D_INNER), lambda b, c: (b * nc + c, 0)),
                   pl.BlockSpec((n, SSM_D_INNER), lambda b, c: (b * nc + c, 0))],
        out_shape=[jax.ShapeDtypeStruct((geo.nrows, SSM_D_INNER), MXU_DTYPE),
                   jax.ShapeDtypeStruct((geo.bsz * nc * n, SSM_D_INNER), F32)],
        scratch_shapes=[pltpu.VMEM((n, SSM_D_INNER), F32), pltpu.VMEM((q, SSM_D_INNER), F32),
                        pltpu.VMEM((q, SSM_D_INNER), F32)],
        compiler_params=_cparams(("parallel", "arbitrary")))(xc, xc, xc, proj, dt_bias, a_log)


def _ssd_bwd_g(geo, xc, proj, dt_bias, a_log, s_prev_all, dy, dxs_skip, dproj):
    q, p, n, e = SSM_CHUNK, SSM_HEAD_DIM, SSM_STATE, geo.nh // SSM_GROUPS
    nc, pad, di, gn, gw = geo.nc, geo.pad, SSM_D_INNER, geo.gn, SSM_D_INNER // SSM_GROUPS
    in_specs, ci = _ssd_specs(geo, True)
    row_spec = pl.BlockSpec((q, di), lambda b, c: (b * nc + ci(c), 0))
    e_map, _ = _head_maps(geo)
    in_specs += [pl.BlockSpec((n, di), lambda b, c: (b * nc + ci(c), 0)), row_spec, row_spec,
                 pl.BlockSpec((LANE, di), lambda b, c: (0, 0)), pl.BlockSpec(memory_space=pl.ANY)]

    def body(xs_ref, b_ref, c_ref, dtr_ref, dtb_ref, alog_ref, sp_ref, dy_ref, dsk_ref, e_ref, _,
             dxc_ref, ddt_ref, gdtb_ref, galog_ref, dstate, xdt_s, dxdt_s):
        step = pl.program_id(1)
        first = jnp.logical_and(pl.program_id(0) == 0, step == 0)
        c = nc - 1 - step

        @pl.when(step == 0)
        def _():
            dstate[...] = jnp.zeros_like(dstate)

        @pl.when(first)
        def _():
            gdtb_ref[...] = jnp.zeros_like(gdtb_ref)
            galog_ref[...] = jnp.zeros_like(galog_ref)

        inert = (c + 1) * q <= pad

        @pl.when(inert)
        def _():
            dxc_ref[...] = jnp.zeros_like(dxc_ref)
            ddt_ref[...] = jnp.zeros_like(ddt_ref)

        @pl.when(jnp.logical_not(inert))
        def _():
            valid = (c * q + lax.broadcasted_iota(jnp.int32, (q, 1), 0) >= pad).astype(F32)
            dtr, dtb, alog = dtr_ref[...], dtb_ref[...], alog_ref[...]
            dt, a_cs = _ssd_pre(dtr, dtb, alog, valid)
            a_cst = a_cs.T
            dt_x, a_x = _expand_heads(dt, geo.nh), _expand_heads(a_cs, geo.nh)
            tri = _tri(q)
            lane = lax.broadcasted_iota(jnp.int32, (1, LANE), 1)
            sub = lax.broadcasted_iota(jnp.int32, (LANE, 1), 0)
            d_dt = jnp.zeros((q, LANE), F32)
            d_acs = jnp.zeros((q, LANE), F32)
            d_acst = jnp.zeros((LANE, q), F32)
            d_last = jnp.zeros((1, LANE), F32)
            for g in range(SSM_GROUPS):
                gs = slice(g * gw, (g + 1) * gw)
                bg, cg = b_ref[:, g * n:(g + 1) * n], c_ref[:, g * n:(g + 1) * n]
                seg = lambda v: _mxdot(v, e_ref[:, gs], 1, 1)
                a_g, dt_g, x_g, dy_g = a_x[:, gs], dt_x[:, gs], xs_ref[:, gs], dy_ref[:, gs]
                e_col, e_last, dec = jnp.exp(a_g), jnp.exp(a_g[q - 1:q, :]), jnp.exp(a_g[q - 1:q, :] - a_g)
                xdt_g = x_g * dt_g
                xdt_s[:, gs] = xdt_g
                s_g, ds_g = sp_ref[:, gs], dstate[:, gs]
                cs = _mxdot(cg, s_g, 1, 0)
                d_cs = dy_g * e_col
                d_acs = d_acs + seg(d_cs * cs)
                d_cg = _mxdot(d_cs, s_g, 1, 1)
                dstate[:, gs] = _mxdot(cg, d_cs, 0, 0) + ds_g * e_last
                dl_x = jnp.sum(ds_g * s_g, axis=0, keepdims=True) * e_last
                d_last = d_last + seg(jnp.broadcast_to(dl_x, (8, gw)))[:1]
                gmat = _mxdot(bg, ds_g, 1, 0)
                xd = xdt_g * dec
                d_bg = _mxdot(xd, ds_g, 1, 1)
                d_dec = seg(xd * gmat)
                d_acs = d_acs - d_dec
                d_last = d_last + jnp.sum(d_dec, axis=0, keepdims=True)
                dxdt_s[:, gs] = dec * gmat
                cb = _mxdot(cg, bg, 1, 1)
                d_cb = jnp.zeros((q, q), F32)
                for hh in range(e):
                    h = g * e + hh
                    hs = slice(h * p, (h + 1) * p)
                    ldec = jnp.exp(jnp.where(tri, a_cs[:, h:h + 1] - a_cst[h:h + 1, :], -jnp.inf))
                    dyh = dy_ref[:, hs]
                    d_m = _mxdot(dyh, xdt_s[:, hs], 1, 1)
                    dxdt_s[:, hs] += _mxdot(cb * ldec, dyh, 0, 0)
                    d_cb = d_cb + d_m * ldec
                    d_diff = d_m * cb * ldec
                    d_acs = d_acs + jnp.sum(d_diff, axis=1, keepdims=True) * (lane == h).astype(F32)
                    d_acst = d_acst - (sub == h).astype(F32) * jnp.sum(d_diff, axis=0, keepdims=True)
                d_xdt = dxdt_s[:, gs]
                dxc_ref[:, gs] = d_xdt * dt_g + dsk_ref[:, gs]
                d_dt = d_dt + seg(d_xdt * x_g)
                dxc_ref[:, di + g * n:di + (g + 1) * n] = d_bg + _mxdot(d_cb, cg, 0, 0)
                dxc_ref[:, di + gn + g * n:di + gn + (g + 1) * n] = d_cg + _mxdot(d_cb, bg, 1, 0)
            is_last = (lax.broadcasted_iota(jnp.int32, (q, 1), 0) == q - 1).astype(F32)
            d_acs = d_acs + d_acst.T + is_last * d_last
            d_adt = _dot(_tri(q).astype(F32), d_acs, 0, 0, precision=lax.Precision.HIGHEST)
            a = -jnp.exp(alog)
            d_dt = d_dt + d_adt * a
            d_dtr = d_dt * valid * _sigmoid(dtr + dtb)
            ddt_ref[...] = d_dtr.astype(ddt_ref.dtype)
            gdtb_ref[...] += jnp.sum(d_dtr, axis=0, keepdims=True)
            galog_ref[...] += jnp.sum(d_adt * dt, axis=0, keepdims=True) * a

    vec = pl.BlockSpec((1, LANE), lambda b, c: (0, 0))
    return pl.pallas_call(
        body, name="ssd_bwd", grid=(geo.bsz, nc), in_specs=in_specs,
        out_specs=[pl.BlockSpec((q, geo.cd), lambda b, c: (b * nc + ci(c), 0)),
                   pl.BlockSpec((q, LANE), lambda b, c: (b * nc + ci(c), geo.cb("dt"))), vec, vec],
        out_shape=[jax.ShapeDtypeStruct((geo.nrows, geo.cd), F32), jax.ShapeDtypeStruct(dproj.shape, dproj.dtype),
                   jax.ShapeDtypeStruct((1, LANE), F32), jax.ShapeDtypeStruct((1, LANE), F32)],
        scratch_shapes=[pltpu.VMEM((n, di), F32), pltpu.VMEM((q, di), F32), pltpu.VMEM((q, di), F32)],
        input_output_aliases={10: 1},
        compiler_params=_cparams(("arbitrary", "arbitrary")))(
            xc, xc, xc, proj, dt_bias, a_log, s_prev_all, dy, dxs_skip, e_map, dproj)


BIAS_LANE = MLA_ROPE // 2
KEY_OFF = -1e30
ATT_SCALE = (MLA_NOPE + MLA_ROPE) ** -0.5


def _row_t(col):
    return jnp.broadcast_to(col, (col.shape[0], LANE)).T[:8]


def _attn_fwd2(geo, qn, qp, kn, kp, v):
    t, lp = ATT_BLK, geo.lp
    nb = lp // t

    def body(qn_ref, qp_ref, kn_ref, kp_ref, v_ref, o_ref, lse_ref, k_ref):
        qi = pl.program_id(2)

        @pl.when(qi == 0)
        def _():
            k_ref[:, :LANE] = kn_ref[...]
            k_ref[:, LANE:] = kp_ref[...]

        q = jnp.concatenate([qn_ref[...], qp_ref[...]], axis=1)

        def blk(kj, ntile, carry, diag):
            m, l, acc = carry
            ks = pl.ds(pl.multiple_of(kj * t, t), ntile * t)
            s = _mxdot(q, k_ref[ks, :], 1, 1) * ATT_SCALE
            if diag:
                s = jnp.where(_tri(t), s, -jnp.inf)
            m_new = jnp.maximum(m, jnp.max(s, axis=1, keepdims=True))
            pr = jnp.exp(s - m_new)
            alpha = jnp.exp(m - m_new)
            return m_new, alpha * l + jnp.sum(pr, axis=1, keepdims=True), alpha * acc + _mxdot(pr, v_ref[ks, :], 1, 0)

        carry = (jnp.full((t, 1), 2.0 * KEY_OFF, F32), jnp.zeros((t, 1), F32), jnp.zeros((t, LANE), F32))
        done = 0
        for ntile in (4, 2, 1):
            steps = (qi - done) // ntile
            carry = lax.fori_loop(0, steps, lambda j, c, d=done, n=ntile: blk(d + n * j, n, c, False), carry)
            done = done + steps * ntile
        m, l, acc = blk(qi, 1, carry, True)
        o_ref[...] = (acc / l).astype(o_ref.dtype)
        lse_ref[0, 0, 0] = _row_t(m + jnp.log(l))

    tile = pl.BlockSpec((t, LANE), lambda b, h, i: (b * nb + i, h))
    seq = pl.BlockSpec((lp, LANE), lambda b, h, i: (b, h))
    return pl.pallas_call(
        body, name="attn_fwd", grid=(geo.bsz, MLA_HEADS, nb),
        in_specs=[tile, tile, seq, pl.BlockSpec((lp, LANE), lambda b, h, i: (b, 0)), seq],
        out_specs=[tile, pl.BlockSpec((1, 1, 1, 8, t), lambda b, h, i: (b, h, i, 0, 0))],
        out_shape=[jax.ShapeDtypeStruct((geo.nrows, geo.hq), MXU_DTYPE),
                   jax.ShapeDtypeStruct((geo.bsz, MLA_HEADS, nb, 8, t), F32)],
        scratch_shapes=[pltpu.VMEM((lp, 2 * LANE), MXU_DTYPE)],
        compiler_params=_cparams(("parallel", "parallel", "arbitrary")))(qn, qp, kn, kp, v)


def _attn_bwd2(geo, qn, qp, kn, kp, v, d_o, o, lse):
    t, lp = ATT_BLK, geo.lp
    nb = lp // t

    def body(qn_ref, qp_ref, kn_ref, kp_ref, v_ref, do_ref, o_ref, lse_ref,
             dqn_ref, dqp_ref, dkn_ref, dkp_ref, dv_ref, q_ref, dl_s):
        kj = pl.program_id(2)

        @pl.when(kj == 0)
        def _():
            q_ref[:, :LANE] = qn_ref[...]
            q_ref[:, LANE:] = qp_ref[...]
            dqn_ref[...] = jnp.zeros_like(dqn_ref)
            dqp_ref[...] = jnp.zeros_like(dqp_ref)
            for i in range(nb):
                rows = slice(i * t, (i + 1) * t)
                dl_s[i] = _row_t(jnp.sum(do_ref[rows, :].astype(F32) * o_ref[rows, :].astype(F32), axis=1, keepdims=True))

        k, vv = jnp.concatenate([kn_ref[...], kp_ref[...]], axis=1), v_ref[...]

        def row(ref, qi, ntile):
            return jnp.concatenate([ref[qi + i][:1, :] for i in range(ntile)], axis=1)

        def blk(qi, ntile, carry, diag):
            dk, dv = carry
            qs = pl.ds(pl.multiple_of(qi * t, t), ntile * t)
            q, d_o_blk = q_ref[qs, :], do_ref[qs, :]
            st = _mxdot(k, q, 1, 1) * ATT_SCALE
            if diag:
                keys = lax.broadcasted_iota(jnp.int32, (t, t), 0)
                st = jnp.where(keys <= lax.broadcasted_iota(jnp.int32, (t, t), 1), st, -jnp.inf)
            pt = jnp.exp(st - row(lse_ref.at[0, 0], qi, ntile))
            dst = pt * (_mxdot(vv, d_o_blk, 1, 1) - row(dl_s, qi, ntile)) * ATT_SCALE
            dq = _mxdot(dst, k, 0, 0)
            dqn_ref[qs, :] += dq[:, :LANE]
            dqp_ref[qs, :] += dq[:, LANE:]
            return dk + _mxdot(dst, q, 1, 0), dv + _mxdot(pt, d_o_blk, 1, 0)

        carry = blk(kj, 1, (jnp.zeros((t, 2 * LANE), F32), jnp.zeros((t, LANE), F32)), True)
        done = kj + 1
        for ntile in (4, 2, 1):
            steps = (nb - done) // ntile
            carry = lax.fori_loop(0, steps, lambda j, c, d=done, n=ntile: blk(d + n * j, n, c, False), carry)
            done = done + steps * ntile
        dk, dv = carry
        dkn_ref[...] = dk[:, :LANE].astype(dkn_ref.dtype)
        dkp_ref[...] = dk[:, LANE:]
        dv_ref[...] = dv.astype(dv_ref.dtype)

    seq = pl.BlockSpec((lp, LANE), lambda b, h, j: (b, h))
    tile = pl.BlockSpec((t, LANE), lambda b, h, j: (b * nb + j, h))
    return pl.pallas_call(
        body, name="attn_bwd", grid=(geo.bsz, MLA_HEADS, nb),
        in_specs=[seq, seq, tile, pl.BlockSpec((t, LANE), lambda b, h, j: (b * nb + j, 0)), tile, seq, seq,
                  pl.BlockSpec((1, 1, nb, 8, t), lambda b, h, j: (b, h, 0, 0, 0))],
        out_specs=[seq, seq, tile, tile, tile],
        out_shape=[jax.ShapeDtypeStruct((geo.nrows, geo.hq), F32), jax.ShapeDtypeStruct((geo.nrows, geo.hq), F32),
                   jax.ShapeDtypeStruct((geo.nrows, geo.hq), MXU_DTYPE), jax.ShapeDtypeStruct((geo.nrows, geo.hq), F32),
                   jax.ShapeDtypeStruct((geo.nrows, geo.hq), MXU_DTYPE)],
        scratch_shapes=[pltpu.VMEM((lp, 2 * LANE), MXU_DTYPE), pltpu.VMEM((nb, 8, t), F32)],
        compiler_params=_cparams(("parallel", "parallel", "arbitrary")))(qn, qp, kn, kp, v, d_o, o, lse)


def _rope(x, cos, sin):
    return x * cos + pltpu.roll(x, LANE // 2, axis=1) * sin


def _rope_t(dx, cos, sin):
    return dx * cos + pltpu.roll(dx * sin, LANE // 2, axis=1)


def _per_head(f):
    def fn(x, cos, sin):
        return (jnp.concatenate([f(x[:, h * LANE:(h + 1) * LANE], cos, sin) for h in range(MLA_HEADS)], axis=1),)
    return fn


def _layer_fwd(geo, h, w, tab, late=None):
    nr, tr, trw = geo.nrows, geo.tr, geo.tr_wide
    tb = geo.lp // tr
    rw = functools.partial(_rowwise, nrows=nr)
    s = {"h": h}
    (s["u"],) = rw("rms_mix", lambda x, g: (_rms(x, g),), tr=tr, rows=[(h, D_MODEL, 0)],
                   vecs=[(w["norm_mix_w"], D_MODEL, 0)], outs=[(D_MODEL, D_MODEL, MXU_DTYPE)])
    proj = s["proj"] = _mm("mm_in", s["u"], w["w_in_p"])
    xc = s["xc"] = _conv_fwd(geo, proj, w["conv_w"], w["conv_b"])
    s["y_ssd"], s["s_prev"] = _ssd_fwd_g(geo, xc, proj, w["dt_bias"], w["a_log"])
    gw = SSM_D_INNER // SSM_GROUPS

    def gate_norm(y, x, z, dsk, nw):
        return (_rms((y + x * dsk) * _silu(z), nw),)

    (s["y_ssm"],) = rw("ssm_gate_norm", gate_norm, tr=tr, ncb=SSM_GROUPS,
                       rows=[(s["y_ssd"], gw, 0), (xc, gw, 0), (proj, gw, geo.col["z"][0] // gw)],
                       vecs=[(w["d_skip_full"], gw, 0), (w["ssm_norm_w"], gw, 0)], outs=[(SSM_D_INNER, gw, MXU_DTYPE)])
    if late is not None:
        w = {**w, **late(s["y_ssm"])}
    (s["cq_n"],) = rw("rms_q", lambda x, g: (_rms(x, g),), tr=tr, rows=[(proj, MLA_Q_LORA, geo.cb("c_q"))],
                      vecs=[(w["q_norm_w"], MLA_Q_LORA, 0)], outs=[(MLA_Q_LORA, MLA_Q_LORA, MXU_DTYPE)])
    (s["ckv_n"],) = rw("rms_kv", lambda x, g: (_rms(x, g),), tr=tr, rows=[(proj, MLA_KV_LORA, geo.cb("c_kv"))],
                       vecs=[(w["kv_norm_w"], MLA_KV_LORA, 0)], outs=[(MLA_KV_LORA, MLA_KV_LORA, MXU_DTYPE)])
    s["qn"] = _mm("mm_qn", s["cq_n"], w["w_qn"], out_dtype=MXU_DTYPE)
    qp_raw = _mm("mm_qp", s["cq_n"], w["w_qp"])
    s["kn"] = _mm("mm_kn", s["ckv_n"], w["w_k"], out_dtype=MXU_DTYPE)
    s["v"] = _mm("mm_v", s["ckv_n"], w["w_v"], out_dtype=MXU_DTYPE)
    bias_lane = lambda: lax.broadcasted_iota(jnp.int32, (1, LANE), 1) == BIAS_LANE
    rope_tabs = [(tab["cos"], LANE, 0), (tab["sin"], LANE, 0)]
    (s["qp"],) = rw("rope_q", _per_head(lambda xp, c, sn: jnp.where(bias_lane(), 1.0, _rope(xp, c, sn))), tr=tr,
                    rows=[(qp_raw, geo.hq, 0)], tabs=rope_tabs, outs=[(geo.hq, geo.hq, MXU_DTYPE)], tab_blocks=tb)
    (s["kp"],) = rw("rope_k", lambda xp, c, sn, valid: (jnp.where(bias_lane(), KEY_OFF * (1.0 - valid), _rope(xp, c, sn)),),
                    tr=tr, rows=[(proj, LANE, geo.cb("k_rope"))], tabs=rope_tabs + [(tab["valid"], 1, 0)],
                    outs=[(LANE, LANE, MXU_DTYPE)], tab_blocks=tb)
    s["o"], s["lse"] = _attn_fwd2(geo, s["qn"], s["qp"], s["kn"], s["kp"], s["v"])
    s["ys_p"] = _mm("mm_bs", s["y_ssm"], w["w_branch_ssm"], out_dtype=MXU_DTYPE)
    s["ym_p"] = _mm("mm_bm", s["o"], w["w_branch_mla"], out_dtype=MXU_DTYPE)

    def gate(gs, gm, ys, ym):
        return (_sigmoid(gs) * ys + _sigmoid(gm) * ym,)

    (s["mixed"],) = rw("gate", gate, tr=tr, rows=[(proj, D_MODEL, geo.cb("g_ssm")), (proj, D_MODEL, geo.cb("g_mla")),
                                                  (s["ys_p"], D_MODEL, 0), (s["ym_p"], D_MODEL, 0)],
                       outs=[(D_MODEL, D_MODEL, MXU_DTYPE)])
    s["h2"] = _mm("mm_out", s["mixed"], w["w_out"], add=h)
    (s["vn"],) = rw("rms_mlp", lambda x, g: (_rms(x, g),), tr=tr, rows=[(s["h2"], D_MODEL, 0)],
                    vecs=[(w["norm_mlp_w"], D_MODEL, 0)], outs=[(D_MODEL, D_MODEL, MXU_DTYPE)])
    s["up"], s["act"] = _mm("mm_up", s["vn"], w["w_mlp_up"],
                            epi=(lambda r: (r, jnp.square(jnp.maximum(r, 0.0))), (MXU_DTYPE, MXU_DTYPE)))
    return _mm("mm_down", s["act"], w["w_mlp_down"], add=s["h2"]), s, w


def _layer_bwd(geo, dh3, s, w, tab, mid=None, tail=None, dep=None):
    nr, tr, trw = geo.nrows, geo.tr, geo.tr_wide
    tb = geo.lp // tr
    rw = functools.partial(_rowwise, nrows=nr)
    g = {}
    proj = s["proj"]

    def rms_bwd(x, dy, res, gw):
        _, vjp = jax.vjp(_rms, x.astype(F32), gw)
        dx, dgw = vjp(dy.astype(F32))
        return dx + res, dgw

    def rms_bwd_nores(x, dy, gw):
        _, vjp = jax.vjp(_rms, x.astype(F32), gw)
        return vjp(dy.astype(F32))

    (dup,) = _mm("mm_down_t", dh3, w["w_mlp_down"], tb=True, add=s["up"], dep=dep,
                 epi=(lambda r, up: (r * 2.0 * jnp.maximum(up, 0.0),), (MXU_DTYPE,)))
    g["w_mlp_down"] = _mm("mm_down_g", s["act"], dh3, ta=True, out_dtype=MXU_DTYPE)
    g["w_mlp_up"] = _mm("mm_up_g", s["vn"], dup, ta=True, out_dtype=MXU_DTYPE)
    dvn = _mm("mm_up_t", dup, w["w_mlp_up"], tb=True)
    dh2, g["norm_mlp_w"] = rw("rms_mlp_bwd", rms_bwd, tr=tr,
                              rows=[(s["h2"], D_MODEL, 0), (dvn, D_MODEL, 0), (dh3, D_MODEL, 0)],
                              vecs=[(w["norm_mlp_w"], D_MODEL, 0)], outs=[(D_MODEL, D_MODEL, F32)],
                              reds=[(D_MODEL, D_MODEL)])
    dmixed = _mm("mm_out_t", dh2, w["w_out"], tb=True, out_dtype=MXU_DTYPE)
    g["w_out"] = _mm("mm_out_g", s["mixed"], dh2, ta=True, out_dtype=MXU_DTYPE)

    def gate_bwd(gs, gm, ys, ym, dm):
        f = lambda a, b, c, d: _sigmoid(a) * c + _sigmoid(b) * d
        _, vjp = jax.vjp(f, gs, gm, ys.astype(F32), ym.astype(F32))
        dgs, dgm, dys, dym = vjp(dm.astype(F32))
        return dys, dym, jnp.concatenate([dgs, dgm], axis=1)

    assert geo.col["g_mla"][0] == geo.col["g_ssm"][0] + D_MODEL and geo.col["g_ssm"][0] % (2 * D_MODEL) == 0
    dys_p, dym_p, dproj = rw(
        "gate_bwd", gate_bwd, tr=tr,
        rows=[(proj, D_MODEL, geo.cb("g_ssm")), (proj, D_MODEL, geo.cb("g_mla")), (s["ys_p"], D_MODEL, 0),
              (s["ym_p"], D_MODEL, 0), (dmixed, D_MODEL, 0)],
        outs=[(D_MODEL, D_MODEL, MXU_DTYPE)] * 2 + [(geo.pw, 2 * D_MODEL, MXU_DTYPE, geo.col["g_ssm"][0] // (2 * D_MODEL))])
    g["w_branch_ssm"] = _mm("mm_bs_g", s["y_ssm"], dys_p, ta=True, out_dtype=MXU_DTYPE)
    dy_ssm = _mm("mm_bs_t", dys_p, w["w_branch_ssm"], tb=True, out_dtype=MXU_DTYPE)
    g["w_branch_mla"] = _mm("mm_bm_g", s["o"], dym_p, ta=True, out_dtype=MXU_DTYPE)
    d_o = _mm("mm_bm_t", dym_p, w["w_branch_mla"], tb=True, out_dtype=MXU_DTYPE)
    dqn, dqp, dkn, dkp_h, dv = _attn_bwd2(geo, s["qn"], s["qp"], s["kn"], s["kp"], s["v"], d_o, s["o"], s["lse"])
    rope_tabs = [(tab["cos"], LANE, 0), (tab["sin"], LANE, 0)]
    (dqp_raw,) = rw("rope_q_bwd", _per_head(_rope_t), tr=tr, rows=[(dqp, geo.hq, 0)], tabs=rope_tabs,
                    outs=[(geo.hq, geo.hq, MXU_DTYPE)], tab_blocks=tb)

    def rope_k_bwd(x, c, sn):
        tot = x[:, :LANE]
        for hd in range(1, MLA_HEADS):
            tot = tot + x[:, hd * LANE:(hd + 1) * LANE]
        return (_rope_t(tot, c, sn),)

    (dproj,) = rw("rope_k_bwd", rope_k_bwd, tr=tr, rows=[(dkp_h, geo.hq, 0)], tabs=rope_tabs,
                  outs=[(geo.pw, LANE, MXU_DTYPE, geo.cb("k_rope"), dproj)], tab_blocks=tb)
    g["w_qn"] = _mm("mm_qn_g", s["cq_n"], dqn, ta=True, out_dtype=MXU_DTYPE)
    g["w_qp"] = _mm("mm_qp_g", s["cq_n"], dqp_raw, ta=True, out_dtype=MXU_DTYPE)
    dcq_n = _mm("mm_qp_t", dqp_raw, w["w_qp"], tb=True, add=_mm("mm_qn_t", dqn, w["w_qn"], tb=True))
    g["w_k"] = _mm("mm_kn_g", s["ckv_n"], dkn, ta=True, out_dtype=MXU_DTYPE)
    g["w_v"] = _mm("mm_v_g", s["ckv_n"], dv, ta=True, out_dtype=MXU_DTYPE)
    dckv_n = _mm("mm_v_t", dv, w["w_v"], tb=True, add=_mm("mm_kn_t", dkn, w["w_k"], tb=True))
    dproj, g["q_norm_w"] = rw("rms_q_bwd", rms_bwd_nores, tr=tr,
                              rows=[(proj, MLA_Q_LORA, geo.cb("c_q")), (dcq_n, MLA_Q_LORA, 0)],
                              vecs=[(w["q_norm_w"], MLA_Q_LORA, 0)],
                              outs=[(geo.pw, MLA_Q_LORA, MXU_DTYPE, geo.cb("c_q"), dproj)], reds=[(MLA_Q_LORA, MLA_Q_LORA)])
    dproj, g["kv_norm_w"] = rw("rms_kv_bwd", rms_bwd_nores, tr=tr,
                               rows=[(proj, MLA_KV_LORA, geo.cb("c_kv")), (dckv_n, MLA_KV_LORA, 0)],
                               vecs=[(w["kv_norm_w"], MLA_KV_LORA, 0)],
                               outs=[(geo.pw, MLA_KV_LORA, MXU_DTYPE, geo.cb("c_kv"), dproj)],
                               reds=[(MLA_KV_LORA, MLA_KV_LORA)])
    gw_ = SSM_D_INNER // SSM_GROUPS
    d_skip_full = w["d_skip_full"] if mid is None else w["d_skip_full"] + mid(g)[0, 0]

    def gate_norm_bwd(y, x, z, dy, dsk, nw):
        f = lambda y_, x_, z_, dsk_, nw_: _rms((y_ + x_ * dsk_) * _silu(z_), nw_)
        _, vjp = jax.vjp(f, y.astype(F32), x.astype(F32), z, dsk, nw)
        dy_, dx_, dz_, ddsk, dnw = vjp(dy.astype(F32))
        return dy_, dx_, dz_, ddsk, dnw

    dy_ssd, dxs_skip, dproj, g["d_skip_full"], g["ssm_norm_w"] = rw(
        "ssm_gate_norm_bwd", gate_norm_bwd, tr=tr, ncb=SSM_GROUPS,
        rows=[(s["y_ssd"], gw_, 0), (s["xc"], gw_, 0), (proj, gw_, geo.col["z"][0] // gw_), (dy_ssm, gw_, 0)],
        vecs=[(d_skip_full, gw_, 0), (w["ssm_norm_w"], gw_, 0)],
        outs=[(SSM_D_INNER, gw_, MXU_DTYPE), (SSM_D_INNER, gw_, MXU_DTYPE),
              (geo.pw, gw_, MXU_DTYPE, geo.col["z"][0] // gw_, dproj)],
        reds=[(SSM_D_INNER, gw_), (SSM_D_INNER, gw_)])
    dxc, dproj, g["dt_bias"], g["a_log"] = _ssd_bwd_g(geo, s["xc"], proj, w["dt_bias"], w["a_log"], s["s_prev"],
                                                     dy_ssd, dxs_skip, dproj)
    dproj, g["conv_w"], g["conv_b"] = _conv_bwd(geo, proj, w["conv_w"], w["conv_b"], dxc, dproj)
    g["w_in_p"] = _mm("mm_in_g", s["u"], dproj, ta=True, out_dtype=MXU_DTYPE)
    du = _mm("mm_in_t", dproj, w["w_in_p"], tb=True, dep=None if tail is None else tail(g))
    dh, g["norm_mix_w"] = rw("rms_mix_bwd", rms_bwd, tr=tr,
                             rows=[(s["h"], D_MODEL, 0), (du, D_MODEL, 0), (dh2, D_MODEL, 0)],
                             vecs=[(w["norm_mix_w"], D_MODEL, 0)], outs=[(D_MODEL, D_MODEL, F32)],
                             reds=[(D_MODEL, D_MODEL)])
    return dh, g


def _loss_bwd(geo, h, fw, target, tab):
    tr = geo.tr

    def fn(x, tgt, gw, tok):
        def lossf(x_, gw_):
            err = jnp.square(_rms(x_, gw_) - tgt)
            return 0.5 * jnp.sum(tok * jnp.mean(err, axis=-1, keepdims=True), axis=0, keepdims=True)

        val, vjp = jax.vjp(lossf, x, gw)
        dx, dgw = vjp(jnp.ones((1, 1), F32))
        return dx, jnp.broadcast_to(val, (1, LANE)), dgw

    return _rowwise("loss", fn, nrows=geo.nrows, tr=tr, rows=[(h, D_MODEL, 0), (target, D_MODEL, 0)],
                    vecs=[(fw, D_MODEL, 0)], tabs=[(tab["token"], 1, 0)], outs=[(D_MODEL, D_MODEL, F32)],
                    reds=[(LANE, LANE), (D_MODEL, D_MODEL)], tab_blocks=geo.lp // tr)


def kernel(x, meta_tokens, norm_mix_w, w_in, conv_w, conv_b, dt_bias, a_log, d_skip, ssm_norm_w, q_norm_w, kv_norm_w, w_uq, w_ukv, w_branch_ssm, w_branch_mla, w_out, norm_mlp_w, w_mlp_up, w_mlp_down, final_norm_w, loss_target, m_meta_tokens, m_norm_mix_w, m_w_in, m_conv_w, m_conv_b, m_dt_bias, m_a_log, m_d_skip, m_ssm_norm_w, m_q_norm_w, m_kv_norm_w, m_w_uq, m_w_ukv, m_w_branch_ssm, m_w_branch_mla, m_w_out, m_norm_mlp_w, m_w_mlp_up, m_w_mlp_down, m_final_norm_w, v_meta_tokens, v_norm_mix_w, v_w_in, v_conv_w, v_conv_b, v_dt_bias, v_a_log, v_d_skip, v_ssm_norm_w, v_q_norm_w, v_kv_norm_w, v_w_uq, v_w_ukv, v_w_branch_ssm, v_w_branch_mla, v_w_out, v_norm_mlp_w, v_w_mlp_up, v_w_mlp_down, v_final_norm_w):
    args = dict(locals())
    wts = {n: args[n] for n in WEIGHTS}
    mom = {n: args["m_" + n] for n in WEIGHTS}
    var = {n: args["v_" + n] for n in WEIGHTS}
    bsz, seq, _ = x.shape
    depth = w_in.shape[0]
    geo = _Geo(bsz, seq)
    tab = _tables(geo)

    big_names = [n for n, _ in BIG]
    sh_names = big_names + [n for n, _ in SHARDED_F32]
    kinds = dict(BIG + SHARDED_F32)
    shard3 = lambda a: a.reshape((1,) + a.shape) if a.ndim == 2 else a
    wire = {n: (MXU_DTYPE if n in big_names else F32) for n in sh_names}
    cast = {n: shard3(wts[n]).astype(wire[n]) for n in sh_names}
    per_layer = [n for n in sh_names if n != "meta_tokens"]
    small_names = ["norm_mix_w", "conv_b", "dt_bias", "a_log", "d_skip", "ssm_norm_w", "q_norm_w", "kv_norm_w",
                   "norm_mlp_w"]

    def gather_items(pairs):
        ins, outs, items, forms = [], [], [], []
        for n, i in pairs:
            a, b = cast[n].shape[1:]
            shape, dst, form = _gather_plan(a, b, kinds[n])
            items.append((len(ins), len(outs), (lambda ref, p, i=i: ref.at[i]), dst))
            ins.append(cast[n])
            outs.append(jax.ShapeDtypeStruct(shape, wire[n]))
            forms.append(form)
        return ins, outs, items, forms

    def whole_weights(pairs, forms, got):
        by_layer = {}
        for (n, i), form, g in zip(pairs, forms, got):
            if n == "w_in":
                n, g = "w_in_p", _w_in_assemble(geo, g)
            elif form == "row":
                g = g.reshape(g.shape[0] * g.shape[1], g.shape[2])
            elif form == "stack":
                g = _unshard(g, "col")
            by_layer.setdefault(i, {})[n] = g
        return by_layer

    def prep(i, whole, token=None):
        wl = dict(whole)
        wl.update({n: wts[n][i] for n in small_names})
        if token is not None:
            wl["norm_mix_w"] = wl["norm_mix_w"] + token[0, 0]
        return _prep_layer(geo, wl)

    early = ("w_in", "conv_w")
    late_names = [n for n in per_layer if n not in early]
    pairs1 = [(n, i) for i in range(1, depth) for n in per_layer]
    groups = [[(n, 0) for n in early] + [("meta_tokens", 0)], [(n, 0) for n in late_names]] + ([pairs1] if pairs1 else [])
    started = {}

    def gather_start(gi, dep=None):
        ins, outs, items, forms = gather_items(groups[gi])
        sems, thru, landing, token = _exchange_start("gather_w%d_start" % gi, ins, outs, items, dep)
        started[gi] = (groups[gi], forms, sems, thru, landing, items)
        return token

    def gathered(gi, after):
        pairs, forms, sems, thru, landing, items = started[gi]
        return whole_weights(pairs, forms, _exchange_wait("gather_w%d_wait" % gi, sems, thru, landing, items, after))

    def late0(after):
        whole = gathered(1, after)[0]
        if pairs1:
            whole["q_norm_w"] = wts["q_norm_w"][0] + gather_start(2, whole["w_out"])[0, 0]
        return _prep_layer(geo, whole)

    token = gather_start(1, gather_start(0))
    whole0 = gathered(0, token)[0]
    meta_full = whole0.pop("meta_tokens")

    meta = jnp.broadcast_to(meta_full[None], (bsz, N_META, D_MODEL))
    h = jnp.concatenate([jnp.zeros((bsz, geo.pad, D_MODEL), F32), meta, x], axis=1).reshape(geo.nrows, D_MODEL)
    target = jnp.concatenate([jnp.zeros((bsz, geo.pad + N_META, D_MODEL), F32), loss_target], axis=1)
    target = target.reshape(geo.nrows, D_MODEL)
    layers, saved = [], []
    for i in range(depth):
        if i == 0:
            w, late = prep(0, whole0, token), late0
        else:
            if i == 1:
                whole1 = gathered(2, h)
            w, late = prep(i, whole1[i]), None
        h, s, w = _layer_fwd(geo, h, w, tab, late)
        layers.append(w)
        saved.append(s)
    dh, loss_part, g_final = _loss_bwd(geo, h, final_norm_w.reshape(1, -1), target, tab)

    def scatter_items(pairs):
        ins, outs, items = [], [], []
        for n, i in pairs:
            a, b = cast[n].shape[1:]
            arr = g_meta if n == "meta_tokens" else grads[i]["w_in_p" if n == "w_in" else n]
            if n == "w_in":
                arr, src = _w_in_split(geo, arr, b), _entry
            elif kinds[n] == "row":
                src = lambda ref, p, a=a: ref.at[pl.ds(pl.multiple_of(p * a, a), a)]
            elif b % LANE == 0:
                src = lambda ref, p, b=b: ref.at[:, pl.ds(pl.multiple_of(p * b, b), b)]
            else:
                arr, src = _shard(arr, "col"), _entry
            items.append((len(ins), len(outs), src, _entry))
            ins.append(arr.astype(wire[n]))
            outs.append(jax.ShapeDtypeStruct((N_DEV, a, b), wire[n]))
        return ins, outs, items

    grads = [None] * depth
    landed, pending, res = {}, {}, {}

    def scatter_start(name, pairs):
        ins, outs, items = scatter_items(pairs)
        sems, thru, landing, token = _exchange_start(name + "_start", ins, outs, items)
        pending[name] = (pairs, sems, thru, landing, items)
        return token

    def scatter_wait(name, after):
        pairs, sems, thru, landing, items = pending[name]
        landed.update(zip(pairs, _exchange_wait(name + "_wait", sems, thru, landing, items, after)))

    def adam(n):
        parts = [landed[(n, i)] for i in range(cast[n].shape[0])]
        r = _adamw_nat("adamw_" + n, parts, shard3(wts[n]), shard3(mom[n]), shard3(var[n]))
        res[n] = [a.reshape(wts[n].shape) for a in r]

    def mid0(g):
        grads[0] = _unprep_grads(geo, g)
        return scatter_start("scatter_gb0", [(n, 0) for n in late_names])

    def tail0(g):
        grads[0] = _unprep_grads(geo, g)
        return scatter_start("scatter_ga0", [(n, 0) for n in early])

    dep = None
    for i in reversed(range(depth)):
        dh, gl = _layer_bwd(geo, dh, saved[i], layers[i], tab, *((mid0, tail0) if i == 0 else (None, None)), dep)
        grads[i] = _unprep_grads(geo, gl)
        if i == 1:
            dep = scatter_start("scatter_g1", pairs1)
    dh = dh.reshape(bsz, geo.lp, D_MODEL)
    grad_x = dh[:, geo.pad + N_META:]
    g_meta = jnp.sum(dh[:, geo.pad:geo.pad + N_META], axis=0)
    if pairs1:
        scatter_wait("scatter_g1", g_meta)
    scatter_wait("scatter_gb0", g_meta)
    for n in late_names:
        adam(n)
    g_small = {n: jnp.stack([grads[i][n] for i in range(depth)]) for n in SMALL if n != "final_norm_w"}
    g_small["final_norm_w"] = g_final.reshape(-1)
    zero = jnp.zeros((1,), F32)
    pk = lambda d, last: _pack([d[n] for n in SMALL] + [last], F32, row_mult=8)
    packed = pk(g_small, loss_part[0, :1])
    ins, outs, items = scatter_items([("meta_tokens", 0)])
    parts, landed[("meta_tokens", 0)] = _exchange(
        "gather_g", [packed] + ins, [jax.ShapeDtypeStruct((N_DEV,) + packed.shape, F32)] + outs,
        [(0, 0, _whole, _entry)] + [(1, 1, items[0][2], items[0][3])])
    adam("meta_tokens")
    scatter_wait("scatter_ga0", res["meta_tokens"][1])
    for n in early:
        adam(n)
    res_sm = _adamw("adamw_small", parts, pk(wts, zero), pk(mom, zero), pk(var, zero))
    res_sm = [_unpack(r, [wts[n].shape for n in SMALL] + [(1,)]) for r in res_sm]
    loss = res_sm[0][-1][0]

    out = [loss, grad_x]
    for k in range(4):
        named = {n: res[n][k] for n in sh_names}
        named.update(zip(SMALL, res_sm[k]))
        out += [named[n] for n in WEIGHTS]
    return tuple(out)
```

```python
import functools

import numpy as np
import jax
import jax.numpy as jnp
from jax import lax
from jax.experimental import pallas as pl
from jax.experimental.pallas import tpu as pltpu

F32 = jnp.float32
MXU_DTYPE = jnp.bfloat16

D_MODEL = 1024
N_META = 16
EPS = 1e-6
SSM_D_INNER = 2048
SSM_HEAD_DIM = 64
SSM_GROUPS = 4
SSM_STATE = 128
SSM_CONV = 4
SSM_CHUNK = 128
MLA_HEADS = 8
MLA_Q_LORA = 512
MLA_KV_LORA = 256
MLA_NOPE = 128
MLA_ROPE = 64
MLA_V = 128
ROPE_THETA = 10000.0
D_FF = 4096
ADAM_LR = 0.001
ADAM_B1 = 0.9
ADAM_B2 = 0.999
ADAM_EPS = 1e-08
ADAM_WD = 0.01
ADAM_STEP = 10

N_DEV = 8
ATT_BLK = 256
LANE = 128
PACK_W = 1024
VMEM_LIMIT = 56 * 1024 * 1024
MESH_ID = pl.DeviceIdType.MESH

BIG = (("w_in", "col"), ("w_uq", "col"), ("w_ukv", "col"), ("w_branch_ssm", "row"), ("w_branch_mla", "row"),
       ("w_out", "row"), ("w_mlp_up", "col"), ("w_mlp_down", "row"))
SHARDED_F32 = (("conv_w", "col"), ("meta_tokens", "col"))
SMALL = ("norm_mix_w", "conv_b", "dt_bias", "a_log", "d_skip", "ssm_norm_w", "q_norm_w", "kv_norm_w",
         "norm_mlp_w", "final_norm_w")
WEIGHTS = ("meta_tokens", "norm_mix_w", "w_in", "conv_w", "conv_b", "dt_bias", "a_log", "d_skip", "ssm_norm_w",
           "q_norm_w", "kv_norm_w", "w_uq", "w_ukv", "w_branch_ssm", "w_branch_mla", "w_out", "norm_mlp_w",
           "w_mlp_up", "w_mlp_down", "final_norm_w")


def _cparams(sem=None):
    return pltpu.CompilerParams(dimension_semantics=sem, vmem_limit_bytes=VMEM_LIMIT)


def _pick(n, cands):
    for c in cands:
        if n % c == 0:
            return c
    return n


def _sigmoid(x):
    return 1.0 / (1.0 + jnp.exp(-x))


def _silu(x):
    return x * _sigmoid(x)


def _softplus(x):
    t = jnp.exp(-jnp.abs(x))
    return jnp.maximum(x, 0.0) + jnp.where(t < 0.01, t * (1.0 - t * (0.5 - t * (1.0 / 3.0))), jnp.log(1.0 + t))


def _rms(x, w):
    x = x.astype(F32)
    return x * lax.rsqrt(jnp.mean(x * x, axis=-1, keepdims=True) + EPS) * w


def _dot(a, b, ca, cb, precision=None):
    return lax.dot_general(a, b, (((ca,), (cb,)), ((), ())), preferred_element_type=F32, precision=precision)


def _mxdot(a, b, ca, cb):
    return _dot(a.astype(MXU_DTYPE), b.astype(MXU_DTYPE), ca, cb)


def _mm(name, a, b, *, ta=False, tb=False, add=None, out_dtype=F32, dep=None, epi=None, side=None):
    (kdim, m) = a.shape if ta else a.shape[::-1]
    (n, k2) = b.shape if tb else b.shape[::-1]
    assert kdim == k2, (name, a.shape, b.shape)
    tm = _pick(m, (1152, 1024, 768, 512, 384, 256, 128))
    tn = _pick(n, (1024, 512, 384, 256, 128))
    tk = _pick(kdim, (1152, 1024, 768, 512, 384, 256, 128))
    nk = kdim // tk
    a_spec = pl.BlockSpec((tk, tm), lambda i, j, k: (k, i)) if ta else pl.BlockSpec((tm, tk), lambda i, j, k: (i, k))
    b_spec = pl.BlockSpec((tn, tk), lambda i, j, k: (j, k)) if tb else pl.BlockSpec((tk, tn), lambda i, j, k: (k, j))
    o_spec = pl.BlockSpec((tm, tn), lambda i, j, k: (i, j))
    ca, cb = (0 if ta else 1), (1 if tb else 0)

    out_dtypes = [out_dtype] if epi is None else list(epi[1])
    n_out = len(out_dtypes)
    n_side = 0 if side is None else 1

    def body(*refs):
        a_ref, b_ref = refs[:2]
        o_refs, acc = refs[-1 - n_side - n_out:-1 - n_side], refs[-1]
        k = pl.program_id(2)

        @pl.when(k == 0)
        def _():
            acc[...] = jnp.zeros_like(acc)

        acc[...] += _mxdot(a_ref[...], b_ref[...], ca, cb)

        @pl.when(k == nk - 1)
        def _():
            r = acc[...]
            if epi is not None:
                res = epi[0](r, refs[2][...]) if add is not None else epi[0](r)
            else:
                res = (r + refs[2][...].astype(F32) if add is not None else r,)
            for o_ref, val in zip(o_refs, res):
                o_ref[...] = val.astype(o_ref.dtype)

        if side is not None:
            @pl.when(jnp.logical_and(k == nk - 1, pl.program_id(1) == side[0] // tn))
            def _():
                refs[-2][...] = acc[:, side[0] % tn:side[0] % tn + side[1]]

    in_specs, args = [a_spec, b_spec], [a, b]
    if add is not None:
        in_specs.append(o_spec)
        args.append(add)
    if dep is not None:
        in_specs.append(pl.BlockSpec((8, LANE), lambda i, j, k: (0, 0)))
        args.append(dep)
    out_specs = [o_spec] * n_out
    out_shape = [jax.ShapeDtypeStruct((m, n), dt) for dt in out_dtypes]
    if side is not None:
        assert side[0] % tn + side[1] <= tn
        out_specs.append(pl.BlockSpec((tm, side[1]), lambda i, j, k: (i, 0)))
        out_shape.append(jax.ShapeDtypeStruct((m, side[1]), F32))
    res = pl.pallas_call(
        body, name=name, grid=(m // tm, n // tn, nk), in_specs=in_specs, out_specs=out_specs, out_shape=out_shape,
        scratch_shapes=[pltpu.VMEM((tm, tn), F32)],
        compiler_params=_cparams(("parallel", "arbitrary" if side is not None else "parallel", "arbitrary")))(*args)
    return res[0] if epi is None and side is None else res


def _rowwise(name, fn, *, nrows, tr, ncb=1, rows=(), fixed=(), vecs=(), tabs=(), outs=(), reds=(), tab_blocks=1):
    in_specs, args = [], []
    for arr, w, c0 in rows:
        in_specs.append(pl.BlockSpec((tr, w), lambda g, i, c0=c0: (i, c0 + g)))
        args.append(arr)
    for arr, w, c0 in fixed:
        in_specs.append(pl.BlockSpec((tr, w), lambda g, i, c0=c0: (i, c0)))
        args.append(arr)
    for arr, w, c0 in vecs:
        in_specs.append(pl.BlockSpec((1, w), lambda g, i, c0=c0: (0, c0 + g)))
        args.append(arr)
    for arr, w, c0 in tabs:
        in_specs.append(pl.BlockSpec((tr, w), lambda g, i, c0=c0: (i % tab_blocks, c0)))
        args.append(arr)
    n_in, n_out = len(args), len(outs)
    out_shape, out_specs, aliases = [], [], {}
    for k, o in enumerate(outs):
        c0 = o[3] if len(o) > 3 else 0
        out_shape.append(jax.ShapeDtypeStruct((nrows, o[0]), o[2]))
        out_specs.append(pl.BlockSpec((tr, o[1]), lambda g, i, c0=c0: (i, c0 + g)))
        if len(o) > 4:
            aliases[len(args)] = k
            in_specs.append(pl.BlockSpec(memory_space=pl.ANY))
            args.append(o[4])
    out_shape += [jax.ShapeDtypeStruct((1, wt), F32) for wt, w in reds]
    out_specs += [pl.BlockSpec((1, w), lambda g, i: (0, g)) for wt, w in reds]
    first_out = len(args)

    def body(*refs):
        res = fn(*[r[...] for r in refs[:n_in]])
        for o_ref, val in zip(refs[first_out:first_out + n_out], res[:n_out]):
            o_ref[...] = val.astype(o_ref.dtype)
        i = pl.program_id(1)
        for d_ref, val in zip(refs[first_out + n_out:], res[n_out:]):
            @pl.when(i == 0)
            def _(d_ref=d_ref, val=val):
                d_ref[...] = val

            @pl.when(i > 0)
            def _(d_ref=d_ref, val=val):
                d_ref[...] += val

    return pl.pallas_call(
        body, name=name, grid=(ncb, nrows // tr), in_specs=in_specs, out_specs=out_specs, out_shape=out_shape,
        input_output_aliases=aliases, compiler_params=_cparams(("parallel", "arbitrary")))(*args)


def _peer(k):
    x, y, c = lax.axis_index("x"), lax.axis_index("y"), lax.axis_index("c")
    px = jnp.where((k >> 2) & 1, 1 - x, x)
    py = jnp.where((k >> 1) & 1, 1 - y, y)
    pc = jnp.where(k & 1, 1 - c, c)
    return (px, py, pc), 4 * px + 2 * py + pc


def _my_index():
    return 4 * lax.axis_index("x") + 2 * lax.axis_index("y") + lax.axis_index("c")


def _exchange(name, ins, out_shapes, items):
    n_in, n_out, n_it = len(ins), len(out_shapes), len(items)

    def body(*refs):
        x, o = refs[:n_in], refs[n_in:n_in + n_out]
        send_sems, recv_sems, local_sems = refs[n_in + n_out:]
        me = _my_index()
        local, sends = [], []
        for t, (ii, io, src, dst) in enumerate(items):
            cp = pltpu.make_async_copy(src(x[ii], me), dst(o[io], me), local_sems.at[t])
            cp.start()
            local.append(cp)
        for k in range(1, N_DEV):
            dev, idx = _peer(k)
            for t, (ii, io, src, dst) in enumerate(items):
                s = (k - 1) * n_it + t
                cp = pltpu.make_async_remote_copy(
                    src_ref=src(x[ii], idx), dst_ref=dst(o[io], me), send_sem=send_sems.at[s],
                    recv_sem=recv_sems.at[s], device_id=dev, device_id_type=MESH_ID)
                cp.start()
                sends.append(cp)
        for k in range(1, N_DEV):
            dev, idx = _peer(k)
            for t, (ii, io, src, dst) in enumerate(items):
                s = (k - 1) * n_it + t
                pltpu.make_async_remote_copy(
                    src_ref=src(x[ii], idx), dst_ref=dst(o[io], idx), send_sem=send_sems.at[s],
                    recv_sem=recv_sems.at[s], device_id=dev, device_id_type=MESH_ID).wait_recv()
        for cp in sends:
            cp.wait_send()
        for cp in local:
            cp.wait()

    nsem = (N_DEV - 1) * n_it
    anyspec = pl.BlockSpec(memory_space=pl.ANY)
    return pl.pallas_call(
        body, name=name, out_shape=list(out_shapes), in_specs=[anyspec] * n_in, out_specs=[anyspec] * n_out,
        scratch_shapes=[pltpu.SemaphoreType.DMA((nsem,)), pltpu.SemaphoreType.DMA((nsem,)),
                        pltpu.SemaphoreType.DMA((n_it,))],
        compiler_params=pltpu.CompilerParams(has_side_effects=True))(*ins)


def _split_copies(x, land, send_sems, recv_sems, items, receive):
    me = _my_index()
    remote, n_it = [], len(items)
    for k in range(1, N_DEV):
        dev, idx = _peer(k)
        for t, (ii, io, src, dst) in enumerate(items):
            s = (k - 1) * n_it + t
            remote.append(pltpu.make_async_remote_copy(
                src_ref=src(x[ii], idx), dst_ref=dst(land[io], idx if receive else me), send_sem=send_sems.at[s],
                recv_sem=recv_sems.at[s], device_id=dev, device_id_type=MESH_ID))
    local = [pltpu.make_async_copy(src(x[ii], me), dst(land[io], me), send_sems.at[(N_DEV - 1) * n_it + t])
             for t, (ii, io, src, dst) in enumerate(items)]
    return remote, local


def _exchange_start(name, ins, out_shapes, items, dep=None):
    n_in, n_out, n_it = len(ins), len(out_shapes), len(items)

    def body(*refs):
        x, land = refs[:n_in], refs[n_in:n_in + n_out]
        first_out = n_in + n_out + (dep is not None)
        send_sems, recv_sems, token = refs[first_out], refs[first_out + 1], refs[-1]
        remote, local = _split_copies(x, land, send_sems, recv_sems, items, False)
        for cp in remote + local:
            cp.start()
        token[...] = jnp.zeros_like(token)

    hbm = pl.BlockSpec(memory_space=pltpu.HBM)
    sem = pl.BlockSpec(memory_space=pltpu.SEMAPHORE)
    arrs = [pltpu.with_memory_space_constraint(a, pltpu.HBM)
            for a in list(ins) + [lax.empty(s.shape, s.dtype) for s in out_shapes]]
    res = pl.pallas_call(
        body, name=name,
        out_shape=(pltpu.SemaphoreType.DMA((N_DEV * n_it,)), pltpu.SemaphoreType.DMA(((N_DEV - 1) * n_it,)),
                   *[pltpu.HBM(a.shape, a.dtype) for a in arrs], jax.ShapeDtypeStruct((8, LANE), F32)),
        in_specs=[hbm] * (n_in + n_out) + ([] if dep is None else [pl.BlockSpec(memory_space=pl.ANY)]),
        out_specs=(sem, sem, *[hbm] * (n_in + n_out), pl.BlockSpec(memory_space=pltpu.VMEM)),
        input_output_aliases={i: 2 + i for i in range(n_in + n_out)},
        compiler_params=pltpu.CompilerParams(has_side_effects=pltpu.SideEffectType.DATAFLOW_SIDE_EFFECTING))(
            *arrs, *([] if dep is None else [dep]))
    return res[:2], res[2:2 + n_in], res[2 + n_in:2 + n_in + n_out], res[-1]


def _exchange_wait(name, sems, ins, landing, items, after):
    n_in, n_out = len(ins), len(landing)

    def body(*refs):
        x, land = refs[:n_in], refs[n_in:n_in + n_out]
        send_sems, recv_sems = refs[n_in + n_out], refs[n_in + n_out + 1]
        remote, local = _split_copies(x, land, send_sems, recv_sems, items, True)
        for cp in remote:
            cp.wait_send()
            cp.wait_recv()
        for cp in local:
            cp.wait()

    hbm = pl.BlockSpec(memory_space=pltpu.HBM)
    sem = pl.BlockSpec(memory_space=pltpu.SEMAPHORE)
    arrs = list(ins) + list(landing)
    res = pl.pallas_call(
        body, name=name, out_shape=tuple(pltpu.HBM(a.shape, a.dtype) for a in arrs),
        in_specs=[hbm] * (n_in + n_out) + [sem, sem, pl.BlockSpec(memory_space=pl.ANY)],
        out_specs=tuple([hbm] * (n_in + n_out)), input_output_aliases={i: i for i in range(n_in + n_out)},
        compiler_params=pltpu.CompilerParams(has_side_effects=pltpu.SideEffectType.DATAFLOW_SIDE_EFFECTING))(
            *arrs, *sems, after)
    return res[n_in:]


def _whole(ref, p):
    return ref


def _entry(ref, p):
    return ref.at[p]


def _gather_plan(a, b, kind):
    if kind == "col" and b % LANE == 0:
        return (a, N_DEV * b), (lambda ref, p: ref.at[:, pl.ds(pl.multiple_of(p * b, b), b)]), "col"
    return (N_DEV, a, b), _entry, ("row" if kind == "row" else "stack")


def _adamw_nat(name, parts, w, m, v):
    depth, b, c = w.shape
    assert len(parts) == depth
    tb = _pick(b, (128, 64, 32, 16, 8))
    spec = pl.BlockSpec((1, tb, c), lambda i, j: (i, j, 0))

    def body(*refs):
        p_refs = refs[:depth]
        w_ref, m_ref, v_ref, g_ref, d_ref, nm_ref, nv_ref = refs[depth:]
        for layer, p_ref in enumerate(p_refs):
            @pl.when(pl.program_id(0) == layer)
            def _(p_ref=p_ref):
                g = p_ref[0].astype(F32)
                for j in range(1, N_DEV):
                    g = g + p_ref[j].astype(F32)
                nm = ADAM_B1 * m_ref[0] + (1.0 - ADAM_B1) * g
                nv = ADAM_B2 * v_ref[0] + (1.0 - ADAM_B2) * jnp.square(g)
                m_hat = nm / (1.0 - ADAM_B1 ** ADAM_STEP)
                v_hat = nv / (1.0 - ADAM_B2 ** ADAM_STEP)
                g_ref[0] = g
                d_ref[0] = -ADAM_LR * (m_hat / (jnp.sqrt(v_hat) + ADAM_EPS) + ADAM_WD * w_ref[0])
                nm_ref[0] = nm
                nv_ref[0] = nv

    sds = jax.ShapeDtypeStruct((depth, b, c), F32)
    return pl.pallas_call(
        body, name=name, grid=(depth, b // tb),
        in_specs=[pl.BlockSpec((N_DEV, tb, c), lambda i, j: (0, j, 0))] * depth + [spec, spec, spec],
        out_specs=[spec] * 4, out_shape=[sds] * 4, compiler_params=_cparams(("parallel", "parallel")))(*parts, w, m, v)


def _adamw(name, parts, w, m, v):
    rows = w.shape[0]
    tr = _pick(rows, (256, 128, 64, 32, 16, 8))
    spec = pl.BlockSpec((tr, PACK_W), lambda i: (i, 0))

    def body(p_ref, w_ref, m_ref, v_ref, g_ref, d_ref, nm_ref, nv_ref):
        g = p_ref[0]
        for j in range(1, N_DEV):
            g = g + p_ref[j]
        nm = ADAM_B1 * m_ref[...] + (1.0 - ADAM_B1) * g
        nv = ADAM_B2 * v_ref[...] + (1.0 - ADAM_B2) * jnp.square(g)
        m_hat = nm / (1.0 - ADAM_B1 ** ADAM_STEP)
        v_hat = nv / (1.0 - ADAM_B2 ** ADAM_STEP)
        g_ref[...] = g
        d_ref[...] = -ADAM_LR * (m_hat / (jnp.sqrt(v_hat) + ADAM_EPS) + ADAM_WD * w_ref[...])
        nm_ref[...] = nm
        nv_ref[...] = nv

    sds = jax.ShapeDtypeStruct((rows, PACK_W), F32)
    return pl.pallas_call(
        body, name=name, grid=(rows // tr,),
        in_specs=[pl.BlockSpec((N_DEV, tr, PACK_W), lambda i: (0, i, 0)), spec, spec, spec],
        out_specs=[spec] * 4, out_shape=[sds] * 4, compiler_params=_cparams(("parallel",)))(parts, w, m, v)


def _pack(arrs, dtype, row_mult=16):
    flat = jnp.concatenate([a.reshape(-1).astype(dtype) for a in arrs])
    unit = row_mult * PACK_W
    total = -(-flat.shape[0] // unit) * unit
    flat = jnp.pad(flat, (0, total - flat.shape[0]))
    return flat.reshape(-1, PACK_W)


def _pack_lead(arrs, dtype, row_mult):
    flat = jnp.concatenate([a.reshape(N_DEV, -1).astype(dtype) for a in arrs], axis=1)
    unit = row_mult * PACK_W
    total = -(-flat.shape[1] // unit) * unit
    flat = jnp.pad(flat, ((0, 0), (0, total - flat.shape[1])))
    return flat.reshape(N_DEV, -1, PACK_W)


def _unpack(buf, shapes, lead=()):
    flat = buf.reshape(lead + (-1,))
    out, off = [], 0
    for s in shapes:
        n = int(np.prod(s))
        out.append(flat[..., off:off + n].reshape(lead + tuple(s)))
        off += n
    return out


def _unshard(g, kind):
    if kind == "col":
        g = jnp.moveaxis(g, 0, -2)
        return g.reshape(g.shape[:-2] + (g.shape[-2] * g.shape[-1],))
    g = jnp.moveaxis(g, 0, 1)
    return g.reshape((g.shape[0], g.shape[1] * g.shape[2]) + g.shape[3:])


def _shard(full, kind):
    if kind == "col":
        s = full.reshape(full.shape[:-1] + (N_DEV, full.shape[-1] // N_DEV))
        return jnp.moveaxis(s, -2, 0)
    s = full.reshape((full.shape[0], N_DEV, full.shape[1] // N_DEV) + full.shape[2:])
    return jnp.moveaxis(s, 1, 0)


class _Geo:
    def __init__(self, bsz, seq):
        self.bsz, self.seq = bsz, seq
        self.pad = (-(N_META + seq)) % ATT_BLK
        self.lp = self.pad + N_META + seq
        assert (self.pad + N_META) % SSM_CHUNK == 0 and self.lp % SSM_CHUNK == 0
        self.nrows = bsz * self.lp
        self.nc = self.lp // SSM_CHUNK
        self.nh = SSM_D_INNER // SSM_HEAD_DIM
        self.gn = SSM_GROUPS * SSM_STATE
        self.cd = SSM_D_INNER + 2 * self.gn
        self.hq = MLA_HEADS * LANE
        order = (("z", SSM_D_INNER), ("g_ssm", D_MODEL), ("g_mla", D_MODEL), ("xs", SSM_D_INNER), ("bm", self.gn),
                 ("cm", self.gn), ("c_q", MLA_Q_LORA), ("c_kv", MLA_KV_LORA), ("dt", LANE), ("k_rope", LANE))
        self.col, off = {}, 0
        for nm, w in order:
            assert off % w == 0, (nm, off, w)
            self.col[nm] = (off, w)
            off += w
        self.pw = off
        assert self.nh <= LANE and MLA_ROPE == 64 and MLA_NOPE == LANE and MLA_V == LANE
        self.tr = _pick(self.lp, (768, 512, 384, 256, 128))
        self.tr_wide = _pick(self.lp, (384, 256, 128))

    def cb(self, nm):
        off, w = self.col[nm]
        return off // w

    def w_in_runs(self, shard_w):
        nh, half = self.nh, MLA_ROPE // 2
        src, pieces = 0, []
        for nm, n in (("z", SSM_D_INNER), ("xs", SSM_D_INNER), ("bm", self.gn), ("cm", self.gn), ("dt", nh),
                      ("c_q", MLA_Q_LORA), ("c_kv", MLA_KV_LORA), ("k_rope", MLA_ROPE), ("g_ssm", D_MODEL),
                      ("g_mla", D_MODEL)):
            dst = self.col[nm][0]
            if nm == "k_rope":
                pieces += [(src, half, dst), (src + half, half, dst + 2 * half)]
            else:
                pieces.append((src, n, dst))
            src += n
        assert src == shard_w * N_DEV
        runs = []
        for a, n, dst in pieces:
            for j in range(N_DEV):
                lo, hi = max(a, j * shard_w), min(a + n, (j + 1) * shard_w)
                if lo < hi:
                    runs.append((j, lo - j * shard_w, hi - lo, dst + lo - a))
        return runs


def _slot(a):
    h = MLA_ROPE // 2
    z = jnp.zeros(a.shape[:-1] + (h,), a.dtype)
    return jnp.concatenate([a[..., :h], z, a[..., h:], z], axis=-1)


def _unslot(a):
    h = MLA_ROPE // 2
    return jnp.concatenate([a[..., :h], a[..., 2 * h:3 * h]], axis=-1)


def _prep_layer(geo, wl):
    nh = geo.nh
    p = {}
    if "w_uq" in wl:
        uq = wl["w_uq"].reshape(MLA_Q_LORA, MLA_HEADS, MLA_NOPE + MLA_ROPE)
        p["w_qn"] = uq[..., :MLA_NOPE].reshape(MLA_Q_LORA, geo.hq)
        p["w_qp"] = _slot(uq[..., MLA_NOPE:]).reshape(MLA_Q_LORA, geo.hq)
    if "w_ukv" in wl:
        ukv = wl["w_ukv"].reshape(MLA_KV_LORA, MLA_HEADS, MLA_NOPE + MLA_V)
        p["w_k"] = ukv[..., :MLA_NOPE].reshape(MLA_KV_LORA, geo.hq)
        p["w_v"] = ukv[..., MLA_NOPE:].reshape(MLA_KV_LORA, geo.hq)
    for nm in ("w_in_p", "conv_w", "w_branch_ssm", "w_branch_mla", "w_out", "w_mlp_up", "w_mlp_down"):
        if nm in wl:
            p[nm] = wl[nm]
    for nm in ("norm_mix_w", "conv_b", "ssm_norm_w", "q_norm_w", "kv_norm_w", "norm_mlp_w"):
        if nm in wl:
            p[nm] = wl[nm].reshape(1, -1)
    if "dt_bias" in wl:
        p["dt_bias"] = jnp.pad(wl["dt_bias"], (0, LANE - nh)).reshape(1, LANE)
        p["a_log"] = jnp.pad(wl["a_log"], (0, LANE - nh)).reshape(1, LANE)
        p["d_skip_full"] = jnp.repeat(wl["d_skip"], SSM_HEAD_DIM).reshape(1, SSM_D_INNER)
    return p


def _unprep_grads(geo, g):
    nh = geo.nh
    out = {}
    if "w_qn" in g:
        qn = g["w_qn"].reshape(MLA_Q_LORA, MLA_HEADS, MLA_NOPE)
        qp = _unslot(g["w_qp"].reshape(MLA_Q_LORA, MLA_HEADS, LANE))
        out["w_uq"] = jnp.concatenate([qn, qp], axis=-1).reshape(MLA_Q_LORA, -1)
    if "w_k" in g:
        wk = g["w_k"].reshape(MLA_KV_LORA, MLA_HEADS, MLA_NOPE)
        wv = g["w_v"].reshape(MLA_KV_LORA, MLA_HEADS, MLA_V)
        out["w_ukv"] = jnp.concatenate([wk, wv], axis=-1).reshape(MLA_KV_LORA, -1)
    for nm in ("w_in_p", "w_branch_ssm", "w_branch_mla", "w_out", "w_mlp_up", "w_mlp_down", "conv_w"):
        if nm in g:
            out[nm] = g[nm]
    for nm in ("norm_mix_w", "conv_b", "ssm_norm_w", "q_norm_w", "kv_norm_w", "norm_mlp_w"):
        if nm in g:
            out[nm] = g[nm].reshape(-1)
    if "dt_bias" in g:
        out["dt_bias"] = g["dt_bias"].reshape(-1)[:nh]
        out["a_log"] = g["a_log"].reshape(-1)[:nh]
        out["d_skip"] = g["d_skip_full"].reshape(nh, SSM_HEAD_DIM).sum(-1)
    return out


def _tables(geo):
    pos = jnp.arange(geo.lp, dtype=F32) - geo.pad
    inv = ROPE_THETA ** (-jnp.arange(0, MLA_ROPE, 2, dtype=F32) / MLA_ROPE)
    ang = pos[:, None] * inv[None, :]
    cos, sin = jnp.cos(ang), jnp.sin(ang)
    z = jnp.zeros_like(cos)
    rows = jnp.arange(geo.lp)[:, None]
    return {"cos": jnp.concatenate([cos, z, cos, z], axis=-1), "sin": jnp.concatenate([-sin, z, sin, z], axis=-1),
            "valid": (rows >= geo.pad).astype(F32), "token": (rows >= geo.pad + N_META).astype(F32)}


def _w_in_assemble(geo, gathered):
    _, d, sw = gathered.shape
    runs = geo.w_in_runs(sw)
    tr = _pick(d, (256, 128))

    def body(x_ref, o_ref):
        o_ref[...] = jnp.zeros_like(o_ref)
        for j, s0, n, d0 in runs:
            o_ref[:, d0:d0 + n] = x_ref[j, :, s0:s0 + n]

    return pl.pallas_call(
        body, name="w_in_assemble", grid=(d // tr,), in_specs=[pl.BlockSpec((N_DEV, tr, sw), lambda i: (0, i, 0))],
        out_specs=pl.BlockSpec((tr, geo.pw), lambda i: (i, 0)),
        out_shape=jax.ShapeDtypeStruct((d, geo.pw), gathered.dtype), compiler_params=_cparams(("parallel",)))(gathered)


def _w_in_split(geo, g_padded, sw):
    d = g_padded.shape[0]
    runs = geo.w_in_runs(sw)
    tr = _pick(d, (128,))

    def body(x_ref, o_ref):
        for j, s0, n, d0 in runs:
            o_ref[j, :, s0:s0 + n] = x_ref[:, d0:d0 + n]

    return pl.pallas_call(
        body, name="w_in_split", grid=(d // tr,), in_specs=[pl.BlockSpec((tr, geo.pw), lambda i: (i, 0))],
        out_specs=pl.BlockSpec((N_DEV, tr, sw), lambda i: (0, i, 0)),
        out_shape=jax.ShapeDtypeStruct((N_DEV, d, sw), g_padded.dtype),
        compiler_params=_cparams(("parallel",)))(g_padded)


def _conv_cols(geo, cbw):
    x0 = geo.col["xs"][0]
    assert geo.col["bm"][0] == x0 + SSM_D_INNER and geo.col["cm"][0] == geo.col["bm"][0] + geo.gn and x0 % cbw == 0
    return lambda j: x0 // cbw + j


def _conv_taps(x):
    return [pltpu.roll(x, SSM_CONV - 1 - k, axis=0) for k in range(SSM_CONV - 1)] + [x]


def _conv_pre(x, w_ref, b_ref, taps=None):
    taps = _conv_taps(x) if taps is None else taps
    acc = b_ref[...]
    for k in range(SSM_CONV):
        acc = acc + taps[k] * w_ref[k:k + 1, :]
    return acc


def _conv_fwd(geo, proj, conv_w, conv_b):
    cbw = 256
    colmap = _conv_cols(geo, cbw)
    lp, pad = geo.lp, geo.pad

    def body(x_ref, w_ref, b_ref, o_ref):
        valid = (lax.broadcasted_iota(jnp.int32, (lp, 1), 0) >= pad).astype(F32)
        o_ref[...] = (_silu(_conv_pre(x_ref[...].astype(F32), w_ref, b_ref)) * valid).astype(o_ref.dtype)

    return pl.pallas_call(
        body, name="conv_fwd", grid=(geo.bsz, geo.cd // cbw),
        in_specs=[pl.BlockSpec((lp, cbw), lambda b, j: (b, colmap(j))),
                  pl.BlockSpec((SSM_CONV, cbw), lambda b, j: (0, j)), pl.BlockSpec((1, cbw), lambda b, j: (0, j))],
        out_specs=pl.BlockSpec((lp, cbw), lambda b, j: (b, j)),
        out_shape=jax.ShapeDtypeStruct((geo.nrows, geo.cd), MXU_DTYPE),
        compiler_params=_cparams(("parallel", "parallel")))(proj, conv_w, conv_b)


def _conv_bwd(geo, proj, conv_w, conv_b, dxc, dproj):
    cbw = 256
    colmap = _conv_cols(geo, cbw)
    lp, pad = geo.lp, geo.pad

    def body(x_ref, w_ref, b_ref, dy_ref, _, dx_ref, gw_ref, gb_ref):
        b = pl.program_id(1)
        valid = (lax.broadcasted_iota(jnp.int32, (lp, 1), 0) >= pad).astype(F32)
        taps = _conv_taps(x_ref[...].astype(F32))
        pre = _conv_pre(None, w_ref, b_ref, taps)
        sig = _sigmoid(pre)
        dpre = dy_ref[...] * (sig * (1.0 + pre * (1.0 - sig))) * valid
        dx = dpre * w_ref[SSM_CONV - 1:SSM_CONV, :]
        for k in range(SSM_CONV - 1):
            dx = dx + pltpu.roll(dpre, lp - (SSM_CONV - 1 - k), axis=0) * w_ref[k:k + 1, :]
        gws = [jnp.sum(dpre * taps[k], axis=0, keepdims=True) for k in range(SSM_CONV)]
        dx_ref[...] = (dx * valid).astype(dx_ref.dtype)

        @pl.when(b == 0)
        def _():
            gw_ref[...] = jnp.zeros_like(gw_ref)
            gb_ref[...] = jnp.zeros_like(gb_ref)

        for k in range(SSM_CONV):
            gw_ref[k:k + 1, :] += gws[k]
        gb_ref[...] += jnp.sum(dpre, axis=0, keepdims=True)

    return pl.pallas_call(
        body, name="conv_bwd", grid=(geo.cd // cbw, geo.bsz),
        in_specs=[pl.BlockSpec((lp, cbw), lambda j, b: (b, colmap(j))),
                  pl.BlockSpec((SSM_CONV, cbw), lambda j, b: (0, j)), pl.BlockSpec((1, cbw), lambda j, b: (0, j)),
                  pl.BlockSpec((lp, cbw), lambda j, b: (b, j)), pl.BlockSpec(memory_space=pl.ANY)],
        out_specs=[pl.BlockSpec((lp, cbw), lambda j, b: (b, colmap(j))),
                   pl.BlockSpec((SSM_CONV, cbw), lambda j, b: (0, j)), pl.BlockSpec((1, cbw), lambda j, b: (0, j))],
        out_shape=[jax.ShapeDtypeStruct(dproj.shape, dproj.dtype),
                   jax.ShapeDtypeStruct((SSM_CONV, geo.cd), F32), jax.ShapeDtypeStruct((1, geo.cd), F32)],
        input_output_aliases={4: 0},
        compiler_params=_cparams(("parallel", "arbitrary")))(proj, conv_w, conv_b, dxc, dproj)


def _tri(q):
    r = lax.broadcasted_iota(jnp.int32, (q, q), 0)
    c = lax.broadcasted_iota(jnp.int32, (q, q), 1)
    return r >= c


def _ssd_pre(dtr, dtb, alog, valid):
    dt = _softplus(dtr + dtb) * valid
    adt = dt * (-jnp.exp(alog))
    a_cs = _dot(_tri(SSM_CHUNK).astype(F32), adt, 1, 0, precision=lax.Precision.HIGHEST)
    return dt, a_cs


def _ssd_specs(geo, rev):
    nc, q = geo.nc, SSM_CHUNK
    ci = (lambda c: nc - 1 - c) if rev else (lambda c: c)
    nxb = SSM_D_INNER // geo.gn
    return [pl.BlockSpec((q, SSM_D_INNER), lambda b, c: (b * nc + ci(c), 0)),
            pl.BlockSpec((q, geo.gn), lambda b, c: (b * nc + ci(c), nxb)),
            pl.BlockSpec((q, geo.gn), lambda b, c: (b * nc + ci(c), nxb + 1)),
            pl.BlockSpec((q, LANE), lambda b, c: (b * nc + ci(c), 0)),
            pl.BlockSpec((1, LANE), lambda b, c: (0, 0)), pl.BlockSpec((1, LANE), lambda b, c: (0, 0))], ci


def _expand_heads(cols, nh):
    per = LANE // SSM_HEAD_DIM
    lane = lax.broadcasted_iota(jnp.int32, (1, LANE), 1)
    blocks = []
    for j in range(nh // per):
        blk = jnp.broadcast_to(cols[:, j * per:j * per + 1], (cols.shape[0], LANE))
        for k in range(1, per):
            blk = jnp.where(lane >= k * SSM_HEAD_DIM, cols[:, j * per + k:j * per + k + 1], blk)
        blocks.append(blk)
    return jnp.concatenate(blocks, axis=1)


def _head_maps(geo):
    e = (jnp.arange(SSM_D_INNER)[None, :] // SSM_HEAD_DIM == jnp.arange(LANE)[:, None]).astype(F32)
    return e, e.T


def _ssd_fwd_g(geo, xc, proj, dt_bias, a_log):
    q, p, n, e = SSM_CHUNK, SSM_HEAD_DIM, SSM_STATE, geo.nh // SSM_GROUPS
    nc, pad, gw = geo.nc, geo.pad, SSM_D_INNER // SSM_GROUPS
    in_specs, _ = _ssd_specs(geo, False)

    def body(xs_ref, b_ref, c_ref, dtr_ref, dtb_ref, alog_ref, y_ref, sp_ref, state, xdt_s, y_s):
        c = pl.program_id(1)

        @pl.when(c == 0)
        def _():
            state[...] = jnp.zeros_like(state)

        sp_ref[...] = state[...]
        inert = (c + 1) * q <= pad

        @pl.when(inert)
        def _():
            y_ref[...] = jnp.zeros_like(y_ref)

        @pl.when(jnp.logical_not(inert))
        def _():
            valid = (c * q + lax.broadcasted_iota(jnp.int32, (q, 1), 0) >= pad).astype(F32)
            dt, a_cs = _ssd_pre(dtr_ref[...], dtb_ref[...], alog_ref[...], valid)
            a_cst = a_cs.T
            dt_x, a_x = _expand_heads(dt, geo.nh), _expand_heads(a_cs, geo.nh)
            tri = _tri(q)
            for g in range(SSM_GROUPS):
                gs = slice(g * gw, (g + 1) * gw)
                bg, cg = b_ref[:, g * n:(g + 1) * n], c_ref[:, g * n:(g + 1) * n]
                a_g = a_x[:, gs]
                a_last = a_g[q - 1:q, :]
                xdt_g = xs_ref[:, gs] * dt_x[:, gs]
                xdt_s[:, gs] = xdt_g
                s_g = state[:, gs]
                y_s[:, gs] = _mxdot(cg, s_g, 1, 0) * jnp.exp(a_g)
                state[:, gs] = s_g * jnp.exp(a_last) + _mxdot(bg, xdt_g * jnp.exp(a_last - a_g), 0, 0)
                cb = _mxdot(cg, bg, 1, 1)
                for hh in range(e):
                    h = g * e + hh
                    hs = slice(h * p, (h + 1) * p)
                    ldec = jnp.exp(jnp.where(tri, a_cs[:, h:h + 1] - a_cst[h:h + 1, :], -jnp.inf))
                    y_s[:, hs] += _mxdot(cb * ldec, xdt_s[:, hs], 1, 0)
            y_ref[...] = y_s[...].astype(y_ref.dtype)

    return pl.pallas_call(
        body, name="ssd_fwd", grid=(geo.bsz, nc), in_specs=in_specs,
        out_specs=[pl.BlockSpec((q, SSM_D_INNER), lambda b, c: (b * nc + c, 0)),
                   pl.BlockSpec((n, SSM_D_INNER), lambda b, c: (b * nc + c, 0))],
        out_shape=[jax.ShapeDtypeStruct((geo.nrows, SSM_D_INNER), MXU_DTYPE),
                   jax.ShapeDtypeStruct((geo.bsz * nc * n, SSM_D_INNER), F32)],
        scratch_shapes=[pltpu.VMEM((n, SSM_D_INNER), F32), pltpu.VMEM((q, SSM_D_INNER), F32),
                        pltpu.VMEM((q, SSM_D_INNER), F32)],
        compiler_params=_cparams(("parallel", "arbitrary")))(xc, xc, xc, proj, dt_bias, a_log)


def _ssd_bwd_g(geo, xc, proj, dt_bias, a_log, s_prev_all, dy, dxs_skip, dproj):
    q, p, n, e = SSM_CHUNK, SSM_HEAD_DIM, SSM_STATE, geo.nh // SSM_GROUPS
    nc, pad, di, gn, gw = geo.nc, geo.pad, SSM_D_INNER, geo.gn, SSM_D_INNER // SSM_GROUPS
    in_specs, ci = _ssd_specs(geo, True)
    row_spec = pl.BlockSpec((q, di), lambda b, c: (b * nc + ci(c), 0))
    e_map, _ = _head_maps(geo)
    in_specs += [pl.BlockSpec((n, di), lambda b, c: (b * nc + ci(c), 0)), row_spec, row_spec,
                 pl.BlockSpec((LANE, di), lambda b, c: (0, 0)), pl.BlockSpec(memory_space=pl.ANY)]

    def body(xs_ref, b_ref, c_ref, dtr_ref, dtb_ref, alog_ref, sp_ref, dy_ref, dsk_ref, e_ref, _,
             dxc_ref, ddt_ref, gdtb_ref, galog_ref, dstate, xdt_s, dxdt_s):
        step = pl.program_id(1)
        first = jnp.logical_and(pl.program_id(0) == 0, step == 0)
        c = nc - 1 - step

        @pl.when(step == 0)
        def _():
            dstate[...] = jnp.zeros_like(dstate)

        @pl.when(first)
        def _():
            gdtb_ref[...] = jnp.zeros_like(gdtb_ref)
            galog_ref[...] = jnp.zeros_like(galog_ref)

        inert = (c + 1) * q <= pad

        @pl.when(inert)
        def _():
            dxc_ref[...] = jnp.zeros_like(dxc_ref)
            ddt_ref[...] = jnp.zeros_like(ddt_ref)

        @pl.when(jnp.logical_not(inert))
        def _():
            valid = (c * q + lax.broadcasted_iota(jnp.int32, (q, 1), 0) >= pad).astype(F32)
            dtr, dtb, alog = dtr_ref[...], dtb_ref[...], alog_ref[...]
            dt, a_cs = _ssd_pre(dtr, dtb, alog, valid)
            a_cst = a_cs.T
            dt_x, a_x = _expand_heads(dt, geo.nh), _expand_heads(a_cs, geo.nh)
            tri = _tri(q)
            lane = lax.broadcasted_iota(jnp.int32, (1, LANE), 1)
            sub = lax.broadcasted_iota(jnp.int32, (LANE, 1), 0)
            d_dt = jnp.zeros((q, LANE), F32)
            d_acs = jnp.zeros((q, LANE), F32)
            d_acst = jnp.zeros((LANE, q), F32)
            d_last = jnp.zeros((1, LANE), F32)
            for g in range(SSM_GROUPS):
                gs = slice(g * gw, (g + 1) * gw)
                bg, cg = b_ref[:, g * n:(g + 1) * n], c_ref[:, g * n:(g + 1) * n]
                seg = lambda v: _mxdot(v, e_ref[:, gs], 1, 1)
                a_g, dt_g, x_g, dy_g = a_x[:, gs], dt_x[:, gs], xs_ref[:, gs], dy_ref[:, gs]
                e_col, e_last, dec = jnp.exp(a_g), jnp.exp(a_g[q - 1:q, :]), jnp.exp(a_g[q - 1:q, :] - a_g)
                xdt_g = x_g * dt_g
                xdt_s[:, gs] = xdt_g
                s_g, ds_g = sp_ref[:, gs], dstate[:, gs]
                cs = _mxdot(cg, s_g, 1, 0)
                d_cs = dy_g * e_col
                d_acs = d_acs + seg(d_cs * cs)
                d_cg = _mxdot(d_cs, s_g, 1, 1)
                dstate[:, gs] = _mxdot(cg, d_cs, 0, 0) + ds_g * e_last
                dl_x = jnp.sum(ds_g * s_g, axis=0, keepdims=True) * e_last
                d_last = d_last + seg(jnp.broadcast_to(dl_x, (8, gw)))[:1]
                gmat = _mxdot(bg, ds_g, 1, 0)
                xd = xdt_g * dec
                d_bg = _mxdot(xd, ds_g, 1, 1)
                d_dec = seg(xd * gmat)
                d_acs = d_acs - d_dec
                d_last = d_last + jnp.sum(d_dec, axis=0, keepdims=True)
                dxdt_s[:, gs] = dec * gmat
                cb = _mxdot(cg, bg, 1, 1)
                d_cb = jnp.zeros((q, q), F32)
                for hh in range(e):
                    h = g * e + hh
                    hs = slice(h * p, (h + 1) * p)
                    ldec = jnp.exp(jnp.where(tri, a_cs[:, h:h + 1] - a_cst[h:h + 1, :], -jnp.inf))
                    dyh = dy_ref[:, hs]
                    d_m = _mxdot(dyh, xdt_s[:, hs], 1, 1)
                    dxdt_s[:, hs] += _mxdot(cb * ldec, dyh, 0, 0)
                    d_cb = d_cb + d_m * ldec
                    d_diff = d_m * cb * ldec
                    d_acs = d_acs + jnp.sum(d_diff, axis=1, keepdims=True) * (lane == h).astype(F32)
                    d_acst = d_acst - (sub == h).astype(F32) * jnp.sum(d_diff, axis=0, keepdims=True)
                d_xdt = dxdt_s[:, gs]
                dxc_ref[:, gs] = d_xdt * dt_g + dsk_ref[:, gs]
                d_dt = d_dt + seg(d_xdt * x_g)
                dxc_ref[:, di + g * n:di + (g + 1) * n] = d_bg + _mxdot(d_cb, cg, 0, 0)
                dxc_ref[:, di + gn + g * n:di + gn + (g + 1) * n] = d_cg + _mxdot(d_cb, bg, 1, 0)
            is_last = (lax.broadcasted_iota(jnp.int32, (q, 1), 0) == q - 1).astype(F32)
            d_acs = d_acs + d_acst.T + is_last * d_last
            d_adt = _dot(_tri(q).astype(F32), d_acs, 0, 0, precision=lax.Precision.HIGHEST)
            a = -jnp.exp(alog)
            d_dt = d_dt + d_adt * a
            d_dtr = d_dt * valid * _sigmoid(dtr + dtb)
            ddt_ref[...] = d_dtr.astype(ddt_ref.dtype)
            gdtb_ref[...] += jnp.sum(d_dtr, axis=0, keepdims=True)
            galog_ref[...] += jnp.sum(d_adt * dt, axis=0, keepdims=True) * a

    vec = pl.BlockSpec((1, LANE), lambda b, c: (0, 0))
    return pl.pallas_call(
        body, name="ssd_bwd", grid=(geo.bsz, nc), in_specs=in_specs,
        out_specs=[pl.BlockSpec((q, geo.cd), lambda b, c: (b * nc + ci(c), 0)),
                   pl.BlockSpec((q, LANE), lambda b, c: (b * nc + ci(c), geo.cb("dt"))), vec, vec],
        out_shape=[jax.ShapeDtypeStruct((geo.nrows, geo.cd), F32), jax.ShapeDtypeStruct(dproj.shape, dproj.dtype),
                   jax.ShapeDtypeStruct((1, LANE), F32), jax.ShapeDtypeStruct((1, LANE), F32)],
        scratch_shapes=[pltpu.VMEM((n, di), F32), pltpu.VMEM((q, di), F32), pltpu.VMEM((q, di), F32)],
        input_output_aliases={10: 1},
        compiler_params=_cparams(("arbitrary", "arbitrary")))(
            xc, xc, xc, proj, dt_bias, a_log, s_prev_all, dy, dxs_skip, e_map, dproj)


BIAS_LANE = MLA_ROPE // 2
KEY_OFF = -1e30
ATT_SCALE = (MLA_NOPE + MLA_ROPE) ** -0.5


def _row_t(col):
    return jnp.broadcast_to(col, (col.shape[0], LANE)).T[:8]


def _attn_fwd2(geo, qn, qp, kn, kp, v):
    t, lp = ATT_BLK, geo.lp
    nb = lp // t

    def body(qn_ref, qp_ref, kn_ref, kp_ref, v_ref, o_ref, lse_ref, k_ref):
        qi = pl.program_id(2)

        @pl.when(qi == 0)
        def _():
            k_ref[:, :LANE] = kn_ref[...]
            k_ref[:, LANE:] = kp_ref[...]

        q = jnp.concatenate([qn_ref[...], qp_ref[...]], axis=1)

        def blk(kj, ntile, carry, diag):
            m, l, acc = carry
            ks = pl.ds(pl.multiple_of(kj * t, t), ntile * t)
            s = _mxdot(q, k_ref[ks, :], 1, 1) * ATT_SCALE
            if diag:
                s = jnp.where(_tri(t), s, -jnp.inf)
            m_new = jnp.maximum(m, jnp.max(s, axis=1, keepdims=True))
            pr = jnp.exp(s - m_new)
            alpha = jnp.exp(m - m_new)
            return m_new, alpha * l + jnp.sum(pr, axis=1, keepdims=True), alpha * acc + _mxdot(pr, v_ref[ks, :], 1, 0)

        carry = (jnp.full((t, 1), 2.0 * KEY_OFF, F32), jnp.zeros((t, 1), F32), jnp.zeros((t, LANE), F32))
        done = 0
        for ntile in (4, 2, 1):
            steps = (qi - done) // ntile
            carry = lax.fori_loop(0, steps, lambda j, c, d=done, n=ntile: blk(d + n * j, n, c, False), carry)
            done = done + steps * ntile
        m, l, acc = blk(qi, 1, carry, True)
        o_ref[...] = (acc / l).astype(o_ref.dtype)
        lse_ref[0, 0, 0] = _row_t(m + jnp.log(l))

    tile = pl.BlockSpec((t, LANE), lambda b, h, i: (b * nb + i, h))
    seq = pl.BlockSpec((lp, LANE), lambda b, h, i: (b, h))
    return pl.pallas_call(
        body, name="attn_fwd", grid=(geo.bsz, MLA_HEADS, nb),
        in_specs=[tile, tile, seq, pl.BlockSpec((lp, LANE), lambda b, h, i: (b, 0)), seq],
        out_specs=[tile, pl.BlockSpec((1, 1, 1, 8, t), lambda b, h, i: (b, h, i, 0, 0))],
        out_shape=[jax.ShapeDtypeStruct((geo.nrows, geo.hq), MXU_DTYPE),
                   jax.ShapeDtypeStruct((geo.bsz, MLA_HEADS, nb, 8, t), F32)],
        scratch_shapes=[pltpu.VMEM((lp, 2 * LANE), MXU_DTYPE)],
        compiler_params=_cparams(("parallel", "parallel", "arbitrary")))(qn, qp, kn, kp, v)


def _attn_bwd2(geo, qn, qp, kn, kp, v, d_o, o, lse):
    t, lp = ATT_BLK, geo.lp
    nb = lp // t

    def body(qn_ref, qp_ref, kn_ref, kp_ref, v_ref, do_ref, o_ref, lse_ref,
             dqn_ref, dqp_ref, dkn_ref, dkp_ref, dv_ref, q_ref, dl_s):
        kj = pl.program_id(2)

        @pl.when(kj == 0)
        def _():
            q_ref[:, :LANE] = qn_ref[...]
            q_ref[:, LANE:] = qp_ref[...]
            dqn_ref[...] = jnp.zeros_like(dqn_ref)
            dqp_ref[...] = jnp.zeros_like(dqp_ref)
            for i in range(nb):
                rows = slice(i * t, (i + 1) * t)
                dl_s[i] = _row_t(jnp.sum(do_ref[rows, :].astype(F32) * o_ref[rows, :].astype(F32), axis=1, keepdims=True))

        k, vv = jnp.concatenate([kn_ref[...], kp_ref[...]], axis=1), v_ref[...]

        def row(ref, qi, ntile):
            return jnp.concatenate([ref[qi + i][:1, :] for i in range(ntile)], axis=1)

        def blk(qi, ntile, carry, diag):
            dk, dv = carry
            qs = pl.ds(pl.multiple_of(qi * t, t), ntile * t)
            q, d_o_blk = q_ref[qs, :], do_ref[qs, :]
            st = _mxdot(k, q, 1, 1) * ATT_SCALE
            if diag:
                keys = lax.broadcasted_iota(jnp.int32, (t, t), 0)
                st = jnp.where(keys <= lax.broadcasted_iota(jnp.int32, (t, t), 1), st, -jnp.inf)
            pt = jnp.exp(st - row(lse_ref.at[0, 0], qi, ntile))
            dst = pt * (_mxdot(vv, d_o_blk, 1, 1) - row(dl_s, qi, ntile)) * ATT_SCALE
            dq = _mxdot(dst, k, 0, 0)
            dqn_ref[qs, :] += dq[:, :LANE]
            dqp_ref[qs, :] += dq[:, LANE:]
            return dk + _mxdot(dst, q, 1, 0), dv + _mxdot(pt, d_o_blk, 1, 0)

        carry = blk(kj, 1, (jnp.zeros((t, 2 * LANE), F32), jnp.zeros((t, LANE), F32)), True)
        done = kj + 1
        for ntile in (4, 2, 1):
            steps = (nb - done) // ntile
            carry = lax.fori_loop(0, steps, lambda j, c, d=done, n=ntile: blk(d + n * j, n, c, False), carry)
            done = done + steps * ntile
        dk, dv = carry
        dkn_ref[...] = dk[:, :LANE].astype(dkn_ref.dtype)
        dkp_ref[...] = dk[:, LANE:]
        dv_ref[...] = dv.astype(dv_ref.dtype)

    seq = pl.BlockSpec((lp, LANE), lambda b, h, j: (b, h))
    tile = pl.BlockSpec((t, LANE), lambda b, h, j: (b * nb + j, h))
    return pl.pallas_call(
        body, name="attn_bwd", grid=(geo.bsz, MLA_HEADS, nb),
        in_specs=[seq, seq, tile, pl.BlockSpec((t, LANE), lambda b, h, j: (b * nb + j, 0)), tile, seq, seq,
                  pl.BlockSpec((1, 1, nb, 8, t), lambda b, h, j: (b, h, 0, 0, 0))],
        out_specs=[seq, seq, tile, tile, tile],
        out_shape=[jax.ShapeDtypeStruct((geo.nrows, geo.hq), F32), jax.ShapeDtypeStruct((geo.nrows, geo.hq), F32),
                   jax.ShapeDtypeStruct((geo.nrows, geo.hq), MXU_DTYPE), jax.ShapeDtypeStruct((geo.nrows, geo.hq), F32),
                   jax.ShapeDtypeStruct((geo.nrows, geo.hq), MXU_DTYPE)],
        scratch_shapes=[pltpu.VMEM((lp, 2 * LANE), MXU_DTYPE), pltpu.VMEM((nb, 8, t), F32)],
        compiler_params=_cparams(("parallel", "parallel", "arbitrary")))(qn, qp, kn, kp, v, d_o, o, lse)


def _rope(x, cos, sin):
    return x * cos + pltpu.roll(x, LANE // 2, axis=1) * sin


def _rope_t(dx, cos, sin):
    return dx * cos + pltpu.roll(dx * sin, LANE // 2, axis=1)


def _per_head(f):
    def fn(x, cos, sin):
        return (jnp.concatenate([f(x[:, h * LANE:(h + 1) * LANE], cos, sin) for h in range(MLA_HEADS)], axis=1),)
    return fn


def _layer_fwd(geo, h, w, tab, late=None):
    nr, tr, trw = geo.nrows, geo.tr, geo.tr_wide
    tb = geo.lp // tr
    rw = functools.partial(_rowwise, nrows=nr)
    s = {"h": h}
    (s["u"],) = rw("rms_mix", lambda x, g: (_rms(x, g),), tr=tr, rows=[(h, D_MODEL, 0)],
                   vecs=[(w["norm_mix_w"], D_MODEL, 0)], outs=[(D_MODEL, D_MODEL, MXU_DTYPE)])
    proj, s["proj_dt"] = _mm("mm_in", s["u"], w["w_in_p"], out_dtype=MXU_DTYPE, side=(geo.col["dt"][0], LANE))
    s["proj"] = proj
    xc = s["xc"] = _conv_fwd(geo, proj, w["conv_w"], w["conv_b"])
    s["y_ssd"], s["s_prev"] = _ssd_fwd_g(geo, xc, s["proj_dt"], w["dt_bias"], w["a_log"])
    gw = SSM_D_INNER // SSM_GROUPS

    def gate_norm(y, x, z, dsk, nw):
        return (_rms((y + x * dsk) * _silu(z.astype(F32)), nw),)

    (s["y_ssm"],) = rw("ssm_gate_norm", gate_norm, tr=tr, ncb=SSM_GROUPS,
                       rows=[(s["y_ssd"], gw, 0), (xc, gw, 0), (proj, gw, geo.col["z"][0] // gw)],
                       vecs=[(w["d_skip_full"], gw, 0), (w["ssm_norm_w"], gw, 0)], outs=[(SSM_D_INNER, gw, MXU_DTYPE)])
    if late is not None:
        w = {**w, **late(s["y_ssm"])}
    (s["cq_n"],) = rw("rms_q", lambda x, g: (_rms(x, g),), tr=tr, rows=[(proj, MLA_Q_LORA, geo.cb("c_q"))],
                      vecs=[(w["q_norm_w"], MLA_Q_LORA, 0)], outs=[(MLA_Q_LORA, MLA_Q_LORA, MXU_DTYPE)])
    (s["ckv_n"],) = rw("rms_kv", lambda x, g: (_rms(x, g),), tr=tr, rows=[(proj, MLA_KV_LORA, geo.cb("c_kv"))],
                       vecs=[(w["kv_norm_w"], MLA_KV_LORA, 0)], outs=[(MLA_KV_LORA, MLA_KV_LORA, MXU_DTYPE)])
    s["qn"] = _mm("mm_qn", s["cq_n"], w["w_qn"], out_dtype=MXU_DTYPE)
    qp_raw = _mm("mm_qp", s["cq_n"], w["w_qp"])
    s["kn"] = _mm("mm_kn", s["ckv_n"], w["w_k"], out_dtype=MXU_DTYPE)
    s["v"] = _mm("mm_v", s["ckv_n"], w["w_v"], out_dtype=MXU_DTYPE)
    bias_lane = lambda: lax.broadcasted_iota(jnp.int32, (1, LANE), 1) == BIAS_LANE
    rope_tabs = [(tab["cos"], LANE, 0), (tab["sin"], LANE, 0)]
    (s["qp"],) = rw("rope_q", _per_head(lambda xp, c, sn: jnp.where(bias_lane(), 1.0, _rope(xp, c, sn))), tr=tr,
                    rows=[(qp_raw, geo.hq, 0)], tabs=rope_tabs, outs=[(geo.hq, geo.hq, MXU_DTYPE)], tab_blocks=tb)
    (s["kp"],) = rw("rope_k", lambda xp, c, sn, valid: (jnp.where(bias_lane(), KEY_OFF * (1.0 - valid),
                                                                 _rope(xp.astype(F32), c, sn)),),
                    tr=tr, rows=[(proj, LANE, geo.cb("k_rope"))], tabs=rope_tabs + [(tab["valid"], 1, 0)],
                    outs=[(LANE, LANE, MXU_DTYPE)], tab_blocks=tb)
    s["o"], s["lse"] = _attn_fwd2(geo, s["qn"], s["qp"], s["kn"], s["kp"], s["v"])
    s["ys_p"] = _mm("mm_bs", s["y_ssm"], w["w_branch_ssm"], out_dtype=MXU_DTYPE)
    s["ym_p"] = _mm("mm_bm", s["o"], w["w_branch_mla"], out_dtype=MXU_DTYPE)

    def gate(gs, gm, ys, ym):
        return (_sigmoid(gs.astype(F32)) * ys + _sigmoid(gm.astype(F32)) * ym,)

    (s["mixed"],) = rw("gate", gate, tr=tr, rows=[(proj, D_MODEL, geo.cb("g_ssm")), (proj, D_MODEL, geo.cb("g_mla")),
                                                  (s["ys_p"], D_MODEL, 0), (s["ym_p"], D_MODEL, 0)],
                       outs=[(D_MODEL, D_MODEL, MXU_DTYPE)])
    s["h2"] = _mm("mm_out", s["mixed"], w["w_out"], add=h)
    (s["vn"],) = rw("rms_mlp", lambda x, g: (_rms(x, g),), tr=tr, rows=[(s["h2"], D_MODEL, 0)],
                    vecs=[(w["norm_mlp_w"], D_MODEL, 0)], outs=[(D_MODEL, D_MODEL, MXU_DTYPE)])
    s["up"], s["act"] = _mm("mm_up", s["vn"], w["w_mlp_up"],
                            epi=(lambda r: (r, jnp.square(jnp.maximum(r, 0.0))), (MXU_DTYPE, MXU_DTYPE)))
    return _mm("mm_down", s["act"], w["w_mlp_down"], add=s["h2"]), s, w


def _layer_bwd(geo, dh3, s, w, tab, mid=None, tail=None, dep=None):
    nr, tr, trw = geo.nrows, geo.tr, geo.tr_wide
    tb = geo.lp // tr
    rw = functools.partial(_rowwise, nrows=nr)
    g = {}
    proj = s["proj"]

    def rms_bwd(x, dy, res, gw):
        _, vjp = jax.vjp(_rms, x.astype(F32), gw)
        dx, dgw = vjp(dy.astype(F32))
        return dx + res, dgw

    def rms_bwd_nores(x, dy, gw):
        _, vjp = jax.vjp(_rms, x.astype(F32), gw)
        return vjp(dy.astype(F32))

    (dup,) = _mm("mm_down_t", dh3, w["w_mlp_down"], tb=True, add=s["up"], dep=dep,
                 epi=(lambda r, up: (r * 2.0 * jnp.maximum(up, 0.0),), (MXU_DTYPE,)))
    g["w_mlp_down"] = _mm("mm_down_g", s["act"], dh3, ta=True, out_dtype=MXU_DTYPE)
    g["w_mlp_up"] = _mm("mm_up_g", s["vn"], dup, ta=True, out_dtype=MXU_DTYPE)
    dvn = _mm("mm_up_t", dup, w["w_mlp_up"], tb=True)
    dh2, g["norm_mlp_w"] = rw("rms_mlp_bwd", rms_bwd, tr=tr,
                              rows=[(s["h2"], D_MODEL, 0), (dvn, D_MODEL, 0), (dh3, D_MODEL, 0)],
                              vecs=[(w["norm_mlp_w"], D_MODEL, 0)], outs=[(D_MODEL, D_MODEL, F32)],
                              reds=[(D_MODEL, D_MODEL)])
    dmixed = _mm("mm_out_t", dh2, w["w_out"], tb=True, out_dtype=MXU_DTYPE)
    g["w_out"] = _mm("mm_out_g", s["mixed"], dh2, ta=True, out_dtype=MXU_DTYPE)

    def gate_bwd(gs, gm, ys, ym, dm):
        f = lambda a, b, c, d: _sigmoid(a) * c + _sigmoid(b) * d
        _, vjp = jax.vjp(f, gs.astype(F32), gm.astype(F32), ys.astype(F32), ym.astype(F32))
        dgs, dgm, dys, dym = vjp(dm.astype(F32))
        return dys, dym, jnp.concatenate([dgs, dgm], axis=1)

    assert geo.col["g_mla"][0] == geo.col["g_ssm"][0] + D_MODEL and geo.col["g_ssm"][0] % (2 * D_MODEL) == 0
    dys_p, dym_p, dproj = rw(
        "gate_bwd", gate_bwd, tr=tr,
        rows=[(proj, D_MODEL, geo.cb("g_ssm")), (proj, D_MODEL, geo.cb("g_mla")), (s["ys_p"], D_MODEL, 0),
              (s["ym_p"], D_MODEL, 0), (dmixed, D_MODEL, 0)],
        outs=[(D_MODEL, D_MODEL, MXU_DTYPE)] * 2 + [(geo.pw, 2 * D_MODEL, MXU_DTYPE, geo.col["g_ssm"][0] // (2 * D_MODEL))])
    g["w_branch_ssm"] = _mm("mm_bs_g", s["y_ssm"], dys_p, ta=True, out_dtype=MXU_DTYPE)
    dy_ssm = _mm("mm_bs_t", dys_p, w["w_branch_ssm"], tb=True, out_dtype=MXU_DTYPE)
    g["w_branch_mla"] = _mm("mm_bm_g", s["o"], dym_p, ta=True, out_dtype=MXU_DTYPE)
    d_o = _mm("mm_bm_t", dym_p, w["w_branch_mla"], tb=True, out_dtype=MXU_DTYPE)
    dqn, dqp, dkn, dkp_h, dv = _attn_bwd2(geo, s["qn"], s["qp"], s["kn"], s["kp"], s["v"], d_o, s["o"], s["lse"])
    rope_tabs = [(tab["cos"], LANE, 0), (tab["sin"], LANE, 0)]
    (dqp_raw,) = rw("rope_q_bwd", _per_head(_rope_t), tr=tr, rows=[(dqp, geo.hq, 0)], tabs=rope_tabs,
                    outs=[(geo.hq, geo.hq, MXU_DTYPE)], tab_blocks=tb)

    def rope_k_bwd(x, c, sn):
        tot = x[:, :LANE]
        for hd in range(1, MLA_HEADS):
            tot = tot + x[:, hd * LANE:(hd + 1) * LANE]
        return (_rope_t(tot, c, sn),)

    (dproj,) = rw("rope_k_bwd", rope_k_bwd, tr=tr, rows=[(dkp_h, geo.hq, 0)], tabs=rope_tabs,
                  outs=[(geo.pw, LANE, MXU_DTYPE, geo.cb("k_rope"), dproj)], tab_blocks=tb)
    g["w_qn"] = _mm("mm_qn_g", s["cq_n"], dqn, ta=True, out_dtype=MXU_DTYPE)
    g["w_qp"] = _mm("mm_qp_g", s["cq_n"], dqp_raw, ta=True, out_dtype=MXU_DTYPE)
    dcq_n = _mm("mm_qp_t", dqp_raw, w["w_qp"], tb=True, add=_mm("mm_qn_t", dqn, w["w_qn"], tb=True))
    g["w_k"] = _mm("mm_kn_g", s["ckv_n"], dkn, ta=True, out_dtype=MXU_DTYPE)
    g["w_v"] = _mm("mm_v_g", s["ckv_n"], dv, ta=True, out_dtype=MXU_DTYPE)
    dckv_n = _mm("mm_v_t", dv, w["w_v"], tb=True, add=_mm("mm_kn_t", dkn, w["w_k"], tb=True))
    dproj, g["q_norm_w"] = rw("rms_q_bwd", rms_bwd_nores, tr=tr,
                              rows=[(proj, MLA_Q_LORA, geo.cb("c_q")), (dcq_n, MLA_Q_LORA, 0)],
                              vecs=[(w["q_norm_w"], MLA_Q_LORA, 0)],
                              outs=[(geo.pw, MLA_Q_LORA, MXU_DTYPE, geo.cb("c_q"), dproj)], reds=[(MLA_Q_LORA, MLA_Q_LORA)])
    dproj, g["kv_norm_w"] = rw("rms_kv_bwd", rms_bwd_nores, tr=tr,
                               rows=[(proj, MLA_KV_LORA, geo.cb("c_kv")), (dckv_n, MLA_KV_LORA, 0)],
                               vecs=[(w["kv_norm_w"], MLA_KV_LORA, 0)],
                               outs=[(geo.pw, MLA_KV_LORA, MXU_DTYPE, geo.cb("c_kv"), dproj)],
                               reds=[(MLA_KV_LORA, MLA_KV_LORA)])
    gw_ = SSM_D_INNER // SSM_GROUPS
    d_skip_full = w["d_skip_full"] if mid is None else w["d_skip_full"] + mid(g)[0, 0]

    def gate_norm_bwd(y, x, z, dy, dsk, nw):
        f = lambda y_, x_, z_, dsk_, nw_: _rms((y_ + x_ * dsk_) * _silu(z_), nw_)
        _, vjp = jax.vjp(f, y.astype(F32), x.astype(F32), z.astype(F32), dsk, nw)
        dy_, dx_, dz_, ddsk, dnw = vjp(dy.astype(F32))
        return dy_, dx_, dz_, ddsk, dnw

    dy_ssd, dxs_skip, dproj, g["d_skip_full"], g["ssm_norm_w"] = rw(
        "ssm_gate_norm_bwd", gate_norm_bwd, tr=tr, ncb=SSM_GROUPS,
        rows=[(s["y_ssd"], gw_, 0), (s["xc"], gw_, 0), (proj, gw_, geo.col["z"][0] // gw_), (dy_ssm, gw_, 0)],
        vecs=[(d_skip_full, gw_, 0), (w["ssm_norm_w"], gw_, 0)],
        outs=[(SSM_D_INNER, gw_, MXU_DTYPE), (SSM_D_INNER, gw_, MXU_DTYPE),
              (geo.pw, gw_, MXU_DTYPE, geo.col["z"][0] // gw_, dproj)],
        reds=[(SSM_D_INNER, gw_), (SSM_D_INNER, gw_)])
    dxc, dproj, g["dt_bias"], g["a_log"] = _ssd_bwd_g(geo, s["xc"], s["proj_dt"], w["dt_bias"], w["a_log"], s["s_prev"],
                                                     dy_ssd, dxs_skip, dproj)
    dproj, g["conv_w"], g["conv_b"] = _conv_bwd(geo, proj, w["conv_w"], w["conv_b"], dxc, dproj)
    g["w_in_p"] = _mm("mm_in_g", s["u"], dproj, ta=True, out_dtype=MXU_DTYPE)
    du = _mm("mm_in_t", dproj, w["w_in_p"], tb=True, dep=None if tail is None else tail(g))
    dh, g["norm_mix_w"] = rw("rms_mix_bwd", rms_bwd, tr=tr,
                             rows=[(s["h"], D_MODEL, 0), (du, D_MODEL, 0), (dh2, D_MODEL, 0)],
                             vecs=[(w["norm_mix_w"], D_MODEL, 0)], outs=[(D_MODEL, D_MODEL, F32)],
                             reds=[(D_MODEL, D_MODEL)])
    return dh, g


def _loss_bwd(geo, h, fw, target, tab):
    tr = geo.tr

    def fn(x, tgt, gw, tok):
        def lossf(x_, gw_):
            err = jnp.square(_rms(x_, gw_) - tgt)
            return 0.5 * jnp.sum(tok * jnp.mean(err, axis=-1, keepdims=True), axis=0, keepdims=True)

        val, vjp = jax.vjp(lossf, x, gw)
        dx, dgw = vjp(jnp.ones((1, 1), F32))
        return dx, jnp.broadcast_to(val, (1, LANE)), dgw

    return _rowwise("loss", fn, nrows=geo.nrows, tr=tr, rows=[(h, D_MODEL, 0), (target, D_MODEL, 0)],
                    vecs=[(fw, D_MODEL, 0)], tabs=[(tab["token"], 1, 0)], outs=[(D_MODEL, D_MODEL, F32)],
                    reds=[(LANE, LANE), (D_MODEL, D_MODEL)], tab_blocks=geo.lp // tr)


def kernel(x, meta_tokens, norm_mix_w, w_in, conv_w, conv_b, dt_bias, a_log, d_skip, ssm_norm_w, q_norm_w, kv_norm_w, w_uq, w_ukv, w_branch_ssm, w_branch_mla, w_out, norm_mlp_w, w_mlp_up, w_mlp_down, final_norm_w, loss_target, m_meta_tokens, m_norm_mix_w, m_w_in, m_conv_w, m_conv_b, m_dt_bias, m_a_log, m_d_skip, m_ssm_norm_w, m_q_norm_w, m_kv_norm_w, m_w_uq, m_w_ukv, m_w_branch_ssm, m_w_branch_mla, m_w_out, m_norm_mlp_w, m_w_mlp_up, m_w_mlp_down, m_final_norm_w, v_meta_tokens, v_norm_mix_w, v_w_in, v_conv_w, v_conv_b, v_dt_bias, v_a_log, v_d_skip, v_ssm_norm_w, v_q_norm_w, v_kv_norm_w, v_w_uq, v_w_ukv, v_w_branch_ssm, v_w_branch_mla, v_w_out, v_norm_mlp_w, v_w_mlp_up, v_w_mlp_down, v_final_norm_w):
    args = dict(locals())
    wts = {n: args[n] for n in WEIGHTS}
    mom = {n: args["m_" + n] for n in WEIGHTS}
    var = {n: args["v_" + n] for n in WEIGHTS}
    bsz, seq, _ = x.shape
    depth = w_in.shape[0]
    geo = _Geo(bsz, seq)
    tab = _tables(geo)

    big_names = [n for n, _ in BIG]
    sh_names = big_names + [n for n, _ in SHARDED_F32]
    kinds = dict(BIG + SHARDED_F32)
    shard3 = lambda a: a.reshape((1,) + a.shape) if a.ndim == 2 else a
    wire = {n: (MXU_DTYPE if n in big_names else F32) for n in sh_names}
    cast = {n: shard3(wts[n]).astype(wire[n]) for n in sh_names}
    per_layer = [n for n in sh_names if n != "meta_tokens"]
    small_names = ["norm_mix_w", "conv_b", "dt_bias", "a_log", "d_skip", "ssm_norm_w", "q_norm_w", "kv_norm_w",
                   "norm_mlp_w"]

    def gather_items(pairs):
        ins, outs, items, forms = [], [], [], []
        for n, i in pairs:
            a, b = cast[n].shape[1:]
            shape, dst, form = _gather_plan(a, b, kinds[n])
            items.append((len(ins), len(outs), (lambda ref, p, i=i: ref.at[i]), dst))
            ins.append(cast[n])
            outs.append(jax.ShapeDtypeStruct(shape, wire[n]))
            forms.append(form)
        return ins, outs, items, forms

    def whole_weights(pairs, forms, got):
        by_layer = {}
        for (n, i), form, g in zip(pairs, forms, got):
            if n == "w_in":
                n, g = "w_in_p", _w_in_assemble(geo, g)
            elif form == "row":
                g = g.reshape(g.shape[0] * g.shape[1], g.shape[2])
            elif form == "stack":
                g = _unshard(g, "col")
            by_layer.setdefault(i, {})[n] = g
        return by_layer

    def prep(i, whole, token=None):
        wl = dict(whole)
        wl.update({n: wts[n][i] for n in small_names})
        if token is not None:
            wl["norm_mix_w"] = wl["norm_mix_w"] + token[0, 0]
        return _prep_layer(geo, wl)

    early = ("w_in", "conv_w")
    late_names = [n for n in per_layer if n not in early]
    pairs1 = [(n, i) for i in range(1, depth) for n in per_layer]
    groups = [[(n, 0) for n in early] + [("meta_tokens", 0)], [(n, 0) for n in late_names]] + ([pairs1] if pairs1 else [])
    started = {}

    def gather_start(gi, dep=None):
        ins, outs, items, forms = gather_items(groups[gi])
        sems, thru, landing, token = _exchange_start("gather_w%d_start" % gi, ins, outs, items, dep)
        started[gi] = (groups[gi], forms, sems, thru, landing, items)
        return token

    def gathered(gi, after):
        pairs, forms, sems, thru, landing, items = started[gi]
        return whole_weights(pairs, forms, _exchange_wait("gather_w%d_wait" % gi, sems, thru, landing, items, after))

    def late0(after):
        whole = gathered(1, after)[0]
        if pairs1:
            whole["q_norm_w"] = wts["q_norm_w"][0] + gather_start(2, whole["w_out"])[0, 0]
        return _prep_layer(geo, whole)

    token = gather_start(1, gather_start(0))
    whole0 = gathered(0, token)[0]
    meta_full = whole0.pop("meta_tokens")

    meta = jnp.broadcast_to(meta_full[None], (bsz, N_META, D_MODEL))
    h = jnp.concatenate([jnp.zeros((bsz, geo.pad, D_MODEL), F32), meta, x], axis=1).reshape(geo.nrows, D_MODEL)
    target = jnp.concatenate([jnp.zeros((bsz, geo.pad + N_META, D_MODEL), F32), loss_target], axis=1)
    target = target.reshape(geo.nrows, D_MODEL)
    layers, saved = [], []
    for i in range(depth):
        if i == 0:
            w, late = prep(0, whole0, token), late0
        else:
            if i == 1:
                whole1 = gathered(2, h)
            w, late = prep(i, whole1[i]), None
        h, s, w = _layer_fwd(geo, h, w, tab, late)
        layers.append(w)
        saved.append(s)
    dh, loss_part, g_final = _loss_bwd(geo, h, final_norm_w.reshape(1, -1), target, tab)

    def scatter_items(pairs):
        ins, outs, items = [], [], []
        for n, i in pairs:
            a, b = cast[n].shape[1:]
            arr = g_meta if n == "meta_tokens" else grads[i]["w_in_p" if n == "w_in" else n]
            if n == "w_in":
                arr, src = _w_in_split(geo, arr, b), _entry
            elif kinds[n] == "row":
                src = lambda ref, p, a=a: ref.at[pl.ds(pl.multiple_of(p * a, a), a)]
            elif b % LANE == 0:
                src = lambda ref, p, b=b: ref.at[:, pl.ds(pl.multiple_of(p * b, b), b)]
            else:
                arr, src = _shard(arr, "col"), _entry
            items.append((len(ins), len(outs), src, _entry))
            ins.append(arr.astype(wire[n]))
            outs.append(jax.ShapeDtypeStruct((N_DEV, a, b), wire[n]))
        return ins, outs, items

    grads = [None] * depth
    landed, pending, res = {}, {}, {}

    def scatter_start(name, pairs):
        ins, outs, items = scatter_items(pairs)
        sems, thru, landing, token = _exchange_start(name + "_start", ins, outs, items)
        pending[name] = (pairs, sems, thru, landing, items)
        return token

    def scatter_wait(name, after):
        pairs, sems, thru, landing, items = pending[name]
        landed.update(zip(pairs, _exchange_wait(name + "_wait", sems, thru, landing, items, after)))

    def adam(n):
        parts = [landed[(n, i)] for i in range(cast[n].shape[0])]
        r = _adamw_nat("adamw_" + n, parts, shard3(wts[n]), shard3(mom[n]), shard3(var[n]))
        res[n] = [a.reshape(wts[n].shape) for a in r]

    def mid0(g):
        grads[0] = _unprep_grads(geo, g)
        return scatter_start("scatter_gb0", [(n, 0) for n in late_names])

    def tail0(g):
        grads[0] = _unprep_grads(geo, g)
        return scatter_start("scatter_ga0", [(n, 0) for n in early])

    dep = None
    for i in reversed(range(depth)):
        dh, gl = _layer_bwd(geo, dh, saved[i], layers[i], tab, *((mid0, tail0) if i == 0 else (None, None)), dep)
        grads[i] = _unprep_grads(geo, gl)
        if i == 1:
            dep = scatter_start("scatter_g1", pairs1)
    dh = dh.reshape(bsz, geo.lp, D_MODEL)
    grad_x = dh[:, geo.pad + N_META:]
    g_meta = jnp.sum(dh[:, geo.pad:geo.pad + N_META], axis=0)
    if pairs1:
        scatter_wait("scatter_g1", g_meta)
    scatter_wait("scatter_gb0", g_meta)
    for n in late_names:
        adam(n)
    g_small = {n: jnp.stack([grads[i][n] for i in range(depth)]) for n in SMALL if n != "final_norm_w"}
    g_small["final_norm_w"] = g_final.reshape(-1)
    zero = jnp.zeros((1,), F32)
    pk = lambda d, last: _pack([d[n] for n in SMALL] + [last], F32, row_mult=8)
    packed = pk(g_small, loss_part[0, :1])
    ins, outs, items = scatter_items([("meta_tokens", 0)])
    parts, landed[("meta_tokens", 0)] = _exchange(
        "gather_g", [packed] + ins, [jax.ShapeDtypeStruct((N_DEV,) + packed.shape, F32)] + outs,
        [(0, 0, _whole, _entry)] + [(1, 1, items[0][2], items[0][3])])
    adam("meta_tokens")
    scatter_wait("scatter_ga0", res["meta_tokens"][1])
    for n in early:
        adam(n)
    res_sm = _adamw("adamw_small", parts, pk(wts, zero), pk(mom, zero), pk(var, zero))
    res_sm = [_unpack(r, [wts[n].shape for n in SMALL] + [(1,)]) for r in res_sm]
    loss = res_sm[0][-1][0]

    out = [loss, grad_x]
    for k in range(4):
        named = {n: res[n][k] for n in sh_names}
        named.update(zip(SMALL, res_sm[k]))
        out += [named[n] for n in WEIGHTS]
    return tuple(out)
```

```python
import functools

import numpy as np
import jax
import jax.numpy as jnp
from jax import lax
from jax.experimental import pallas as pl
from jax.experimental.pallas import tpu as pltpu

F32 = jnp.float32
MXU_DTYPE = jnp.bfloat16

D_MODEL = 1024
N_META = 16
EPS = 1e-6
SSM_D_INNER = 2048
SSM_HEAD_DIM = 64
SSM_GROUPS = 4
SSM_STATE = 128
SSM_CONV = 4
SSM_CHUNK = 128
MLA_HEADS = 8
MLA_Q_LORA = 512
MLA_KV_LORA = 256
MLA_NOPE = 128
MLA_ROPE = 64
MLA_V = 128
ROPE_THETA = 10000.0
D_FF = 4096
ADAM_LR = 0.001
ADAM_B1 = 0.9
ADAM_B2 = 0.999
ADAM_EPS = 1e-08
ADAM_WD = 0.01
ADAM_STEP = 10

N_DEV = 8
ATT_BLK = 256
LANE = 128
PACK_W = 1024
VMEM_LIMIT = 56 * 1024 * 1024
MESH_ID = pl.DeviceIdType.MESH

BIG = (("w_in", "col"), ("w_uq", "col"), ("w_ukv", "col"), ("w_branch_ssm", "row"), ("w_branch_mla", "row"),
       ("w_out", "row"), ("w_mlp_up", "col"), ("w_mlp_down", "row"))
SHARDED_F32 = (("conv_w", "col"), ("meta_tokens", "col"))
SMALL = ("norm_mix_w", "conv_b", "dt_bias", "a_log", "d_skip", "ssm_norm_w", "q_norm_w", "kv_norm_w",
         "norm_mlp_w", "final_norm_w")
WEIGHTS = ("meta_tokens", "norm_mix_w", "w_in", "conv_w", "conv_b", "dt_bias", "a_log", "d_skip", "ssm_norm_w",
           "q_norm_w", "kv_norm_w", "w_uq", "w_ukv", "w_branch_ssm", "w_branch_mla", "w_out", "norm_mlp_w",
           "w_mlp_up", "w_mlp_down", "final_norm_w")


def _cparams(sem=None):
    return pltpu.CompilerParams(dimension_semantics=sem, vmem_limit_bytes=VMEM_LIMIT)


def _pick(n, cands):
    for c in cands:
        if n % c == 0:
            return c
    return n


def _sigmoid(x):
    return 1.0 / (1.0 + jnp.exp(-x))


def _silu(x):
    return x * _sigmoid(x)


def _softplus(x):
    t = jnp.exp(-jnp.abs(x))
    return jnp.maximum(x, 0.0) + jnp.where(t < 0.01, t * (1.0 - t * (0.5 - t * (1.0 / 3.0))), jnp.log(1.0 + t))


def _rms(x, w):
    x = x.astype(F32)
    return x * lax.rsqrt(jnp.mean(x * x, axis=-1, keepdims=True) + EPS) * w


def _dot(a, b, ca, cb, precision=None):
    return lax.dot_general(a, b, (((ca,), (cb,)), ((), ())), preferred_element_type=F32, precision=precision)


def _mxdot(a, b, ca, cb):
    return _dot(a.astype(MXU_DTYPE), b.astype(MXU_DTYPE), ca, cb)


def _mm(name, a, b, *, ta=False, tb=False, add=None, out_dtype=F32, dep=None, epi=None, side=None):
    (kdim, m) = a.shape if ta else a.shape[::-1]
    (n, k2) = b.shape if tb else b.shape[::-1]
    assert kdim == k2, (name, a.shape, b.shape)
    tm = _pick(m, (1152, 1024, 768, 512, 384, 256, 128))
    tn = _pick(n, (1024, 512, 384, 256, 128))
    tk = _pick(kdim, (1152, 1024, 768, 512, 384, 256, 128))
    nk = kdim // tk
    a_spec = pl.BlockSpec((tk, tm), lambda i, j, k: (k, i)) if ta else pl.BlockSpec((tm, tk), lambda i, j, k: (i, k))
    b_spec = pl.BlockSpec((tn, tk), lambda i, j, k: (j, k)) if tb else pl.BlockSpec((tk, tn), lambda i, j, k: (k, j))
    o_spec = pl.BlockSpec((tm, tn), lambda i, j, k: (i, j))
    ca, cb = (0 if ta else 1), (1 if tb else 0)

    out_dtypes = [out_dtype] if epi is None else list(epi[1])
    n_out = len(out_dtypes)
    n_side = 0 if side is None else 1

    def body(*refs):
        a_ref, b_ref = refs[:2]
        o_refs, acc = refs[-1 - n_side - n_out:-1 - n_side], refs[-1]
        k = pl.program_id(2)

        @pl.when(k == 0)
        def _():
            acc[...] = jnp.zeros_like(acc)

        acc[...] += _mxdot(a_ref[...], b_ref[...], ca, cb)

        @pl.when(k == nk - 1)
        def _():
            r = acc[...]
            if epi is not None:
                res = epi[0](r, refs[2][...]) if add is not None else epi[0](r)
            else:
                res = (r + refs[2][...].astype(F32) if add is not None else r,)
            for o_ref, val in zip(o_refs, res):
                o_ref[...] = val.astype(o_ref.dtype)

        if side is not None:
            @pl.when(jnp.logical_and(k == nk - 1, pl.program_id(1) == side[0] // tn))
            def _():
                refs[-2][...] = acc[:, side[0] % tn:side[0] % tn + side[1]]

    in_specs, args = [a_spec, b_spec], [a, b]
    if add is not None:
        in_specs.append(o_spec)
        args.append(add)
    if dep is not None:
        in_specs.append(pl.BlockSpec((8, LANE), lambda i, j, k: (0, 0)))
        args.append(dep)
    out_specs = [o_spec] * n_out
    out_shape = [jax.ShapeDtypeStruct((m, n), dt) for dt in out_dtypes]
    if side is not None:
        assert side[0] % tn + side[1] <= tn
        out_specs.append(pl.BlockSpec((tm, side[1]), lambda i, j, k: (i, 0)))
        out_shape.append(jax.ShapeDtypeStruct((m, side[1]), F32))
    res = pl.pallas_call(
        body, name=name, grid=(m // tm, n // tn, nk), in_specs=in_specs, out_specs=out_specs, out_shape=out_shape,
        scratch_shapes=[pltpu.VMEM((tm, tn), F32)],
        compiler_params=_cparams(("parallel", "arbitrary" if side is not None else "parallel", "arbitrary")))(*args)
    return res[0] if epi is None and side is None else res


def _rowwise(name, fn, *, nrows, tr, ncb=1, rows=(), fixed=(), vecs=(), tabs=(), outs=(), reds=(), tab_blocks=1):
    in_specs, args = [], []
    for arr, w, c0 in rows:
        in_specs.append(pl.BlockSpec((tr, w), lambda g, i, c0=c0: (i, c0 + g)))
        args.append(arr)
    for arr, w, c0 in fixed:
        in_specs.append(pl.BlockSpec((tr, w), lambda g, i, c0=c0: (i, c0)))
        args.append(arr)
    for arr, w, c0 in vecs:
        in_specs.append(pl.BlockSpec((1, w), lambda g, i, c0=c0: (0, c0 + g)))
        args.append(arr)
    for arr, w, c0 in tabs:
        in_specs.append(pl.BlockSpec((tr, w), lambda g, i, c0=c0: (i % tab_blocks, c0)))
        args.append(arr)
    n_in, n_out = len(args), len(outs)
    out_shape, out_specs, aliases = [], [], {}
    for k, o in enumerate(outs):
        c0 = o[3] if len(o) > 3 else 0
        out_shape.append(jax.ShapeDtypeStruct((nrows, o[0]), o[2]))
        out_specs.append(pl.BlockSpec((tr, o[1]), lambda g, i, c0=c0: (i, c0 + g)))
        if len(o) > 4:
            aliases[len(args)] = k
            in_specs.append(pl.BlockSpec(memory_space=pl.ANY))
            args.append(o[4])
    out_shape += [jax.ShapeDtypeStruct((1, wt), F32) for wt, w in reds]
    out_specs += [pl.BlockSpec((1, w), lambda g, i: (0, g)) for wt, w in reds]
    first_out = len(args)

    def body(*refs):
        res = fn(*[r[...] for r in refs[:n_in]])
        for o_ref, val in zip(refs[first_out:first_out + n_out], res[:n_out]):
            o_ref[...] = val.astype(o_ref.dtype)
        i = pl.program_id(1)
        for d_ref, val in zip(refs[first_out + n_out:], res[n_out:]):
            @pl.when(i == 0)
            def _(d_ref=d_ref, val=val):
                d_ref[...] = val

            @pl.when(i > 0)
            def _(d_ref=d_ref, val=val):
                d_ref[...] += val

    return pl.pallas_call(
        body, name=name, grid=(ncb, nrows // tr), in_specs=in_specs, out_specs=out_specs, out_shape=out_shape,
        input_output_aliases=aliases, compiler_params=_cparams(("parallel", "arbitrary")))(*args)


def _peer(k):
    x, y, c = lax.axis_index("x"), lax.axis_index("y"), lax.axis_index("c")
    px = jnp.where((k >> 2) & 1, 1 - x, x)
    py = jnp.where((k >> 1) & 1, 1 - y, y)
    pc = jnp.where(k & 1, 1 - c, c)
    return (px, py, pc), 4 * px + 2 * py + pc


def _my_index():
    return 4 * lax.axis_index("x") + 2 * lax.axis_index("y") + lax.axis_index("c")


def _exchange(name, ins, out_shapes, items):
    n_in, n_out, n_it = len(ins), len(out_shapes), len(items)

    def body(*refs):
        x, o = refs[:n_in], refs[n_in:n_in + n_out]
        send_sems, recv_sems, local_sems = refs[n_in + n_out:]
        me = _my_index()
        local, sends = [], []
        for t, (ii, io, src, dst) in enumerate(items):
            cp = pltpu.make_async_copy(src(x[ii], me), dst(o[io], me), local_sems.at[t])
            cp.start()
            local.append(cp)
        for k in range(1, N_DEV):
            dev, idx = _peer(k)
            for t, (ii, io, src, dst) in enumerate(items):
                s = (k - 1) * n_it + t
                cp = pltpu.make_async_remote_copy(
                    src_ref=src(x[ii], idx), dst_ref=dst(o[io], me), send_sem=send_sems.at[s],
                    recv_sem=recv_sems.at[s], device_id=dev, device_id_type=MESH_ID)
                cp.start()
                sends.append(cp)
        for k in range(1, N_DEV):
            dev, idx = _peer(k)
            for t, (ii, io, src, dst) in enumerate(items):
                s = (k - 1) * n_it + t
                pltpu.make_async_remote_copy(
                    src_ref=src(x[ii], idx), dst_ref=dst(o[io], idx), send_sem=send_sems.at[s],
                    recv_sem=recv_sems.at[s], device_id=dev, device_id_type=MESH_ID).wait_recv()
        for cp in sends:
            cp.wait_send()
        for cp in local:
            cp.wait()

    nsem = (N_DEV - 1) * n_it
    anyspec = pl.BlockSpec(memory_space=pl.ANY)
    return pl.pallas_call(
        body, name=name, out_shape=list(out_shapes), in_specs=[anyspec] * n_in, out_specs=[anyspec] * n_out,
        scratch_shapes=[pltpu.SemaphoreType.DMA((nsem,)), pltpu.SemaphoreType.DMA((nsem,)),
                        pltpu.SemaphoreType.DMA((n_it,))],
        compiler_params=pltpu.CompilerParams(has_side_effects=True))(*ins)


def _split_copies(x, land, send_sems, recv_sems, items, receive):
    me = _my_index()
    remote, n_it = [], len(items)
    for k in range(1, N_DEV):
        dev, idx = _peer(k)
        for t, (ii, io, src, dst) in enumerate(items):
            s = (k - 1) * n_it + t
            remote.append(pltpu.make_async_remote_copy(
                src_ref=src(x[ii], idx), dst_ref=dst(land[io], idx if receive else me), send_sem=send_sems.at[s],
                recv_sem=recv_sems.at[s], device_id=dev, device_id_type=MESH_ID))
    local = [pltpu.make_async_copy(src(x[ii], me), dst(land[io], me), send_sems.at[(N_DEV - 1) * n_it + t])
             for t, (ii, io, src, dst) in enumerate(items)]
    return remote, local


def _exchange_start(name, ins, out_shapes, items, dep=None):
    n_in, n_out, n_it = len(ins), len(out_shapes), len(items)

    def body(*refs):
        x, land = refs[:n_in], refs[n_in:n_in + n_out]
        first_out = n_in + n_out + (dep is not None)
        send_sems, recv_sems, token = refs[first_out], refs[first_out + 1], refs[-1]
        remote, local = _split_copies(x, land, send_sems, recv_sems, items, False)
        for cp in remote + local:
            cp.start()
        token[...] = jnp.zeros_like(token)

    hbm = pl.BlockSpec(memory_space=pltpu.HBM)
    sem = pl.BlockSpec(memory_space=pltpu.SEMAPHORE)
    arrs = [pltpu.with_memory_space_constraint(a, pltpu.HBM)
            for a in list(ins) + [lax.empty(s.shape, s.dtype) for s in out_shapes]]
    res = pl.pallas_call(
        body, name=name,
        out_shape=(pltpu.SemaphoreType.DMA((N_DEV * n_it,)), pltpu.SemaphoreType.DMA(((N_DEV - 1) * n_it,)),
                   *[pltpu.HBM(a.shape, a.dtype) for a in arrs], jax.ShapeDtypeStruct((8, LANE), F32)),
        in_specs=[hbm] * (n_in + n_out) + ([] if dep is None else [pl.BlockSpec(memory_space=pl.ANY)]),
        out_specs=(sem, sem, *[hbm] * (n_in + n_out), pl.BlockSpec(memory_space=pltpu.VMEM)),
        input_output_aliases={i: 2 + i for i in range(n_in + n_out)},
        compiler_params=pltpu.CompilerParams(has_side_effects=pltpu.SideEffectType.DATAFLOW_SIDE_EFFECTING))(
            *arrs, *([] if dep is None else [dep]))
    return res[:2], res[2:2 + n_in], res[2 + n_in:2 + n_in + n_out], res[-1]


def _exchange_wait(name, sems, ins, landing, items, after):
    n_in, n_out = len(ins), len(landing)

    def body(*refs):
        x, land = refs[:n_in], refs[n_in:n_in + n_out]
        send_sems, recv_sems = refs[n_in + n_out], refs[n_in + n_out + 1]
        remote, local = _split_copies(x, land, send_sems, recv_sems, items, True)
        for cp in remote:
            cp.wait_send()
            cp.wait_recv()
        for cp in local:
            cp.wait()

    hbm = pl.BlockSpec(memory_space=pltpu.HBM)
    sem = pl.BlockSpec(memory_space=pltpu.SEMAPHORE)
    arrs = list(ins) + list(landing)
    res = pl.pallas_call(
        body, name=name, out_shape=tuple(pltpu.HBM(a.shape, a.dtype) for a in arrs),
        in_specs=[hbm] * (n_in + n_out) + [sem, sem, pl.BlockSpec(memory_space=pl.ANY)],
        out_specs=tuple([hbm] * (n_in + n_out)), input_output_aliases={i: i for i in range(n_in + n_out)},
        compiler_params=pltpu.CompilerParams(has_side_effects=pltpu.SideEffectType.DATAFLOW_SIDE_EFFECTING))(
            *arrs, *sems, after)
    return res[n_in:]


def _whole(ref, p):
    return ref


def _entry(ref, p):
    return ref.at[p]


def _gather_plan(a, b, kind):
    if kind == "col" and b % LANE == 0:
        return (a, N_DEV * b), (lambda ref, p: ref.at[:, pl.ds(pl.multiple_of(p * b, b), b)]), "col"
    return (N_DEV, a, b), _entry, ("row" if kind == "row" else "stack")


def _adamw_nat(name, parts, w, m, v):
    depth, b, c = w.shape
    assert len(parts) == depth
    tb = _pick(b, (128, 64, 32, 16, 8))
    if tb == b and b > 256:
        tb = 256
    spec = pl.BlockSpec((1, tb, c), lambda i, j: (i, j, 0))

    def body(*refs):
        p_refs = refs[:depth]
        w_ref, m_ref, v_ref, g_ref, d_ref, nm_ref, nv_ref = refs[depth:]
        for layer, p_ref in enumerate(p_refs):
            @pl.when(pl.program_id(0) == layer)
            def _(p_ref=p_ref):
                g = p_ref[0].astype(F32)
                for j in range(1, N_DEV):
                    g = g + p_ref[j].astype(F32)
                nm = ADAM_B1 * m_ref[0] + (1.0 - ADAM_B1) * g
                nv = ADAM_B2 * v_ref[0] + (1.0 - ADAM_B2) * jnp.square(g)
                m_hat = nm / (1.0 - ADAM_B1 ** ADAM_STEP)
                v_hat = nv / (1.0 - ADAM_B2 ** ADAM_STEP)
                g_ref[0] = g
                d_ref[0] = -ADAM_LR * (m_hat / (jnp.sqrt(v_hat) + ADAM_EPS) + ADAM_WD * w_ref[0])
                nm_ref[0] = nm
                nv_ref[0] = nv

    sds = jax.ShapeDtypeStruct((depth, b, c), F32)
    return pl.pallas_call(
        body, name=name, grid=(depth, pl.cdiv(b, tb)),
        in_specs=[pl.BlockSpec((N_DEV, tb, c), lambda i, j: (0, j, 0))] * depth + [spec, spec, spec],
        out_specs=[spec] * 4, out_shape=[sds] * 4, compiler_params=_cparams(("parallel", "parallel")))(*parts, w, m, v)


def _adamw(name, parts, w, m, v):
    rows = w.shape[0]
    tr = _pick(rows, (256, 128, 64, 32, 16, 8))
    spec = pl.BlockSpec((tr, PACK_W), lambda i: (i, 0))

    def body(p_ref, w_ref, m_ref, v_ref, g_ref, d_ref, nm_ref, nv_ref):
        g = p_ref[0]
        for j in range(1, N_DEV):
            g = g + p_ref[j]
        nm = ADAM_B1 * m_ref[...] + (1.0 - ADAM_B1) * g
        nv = ADAM_B2 * v_ref[...] + (1.0 - ADAM_B2) * jnp.square(g)
        m_hat = nm / (1.0 - ADAM_B1 ** ADAM_STEP)
        v_hat = nv / (1.0 - ADAM_B2 ** ADAM_STEP)
        g_ref[...] = g
        d_ref[...] = -ADAM_LR * (m_hat / (jnp.sqrt(v_hat) + ADAM_EPS) + ADAM_WD * w_ref[...])
        nm_ref[...] = nm
        nv_ref[...] = nv

    sds = jax.ShapeDtypeStruct((rows, PACK_W), F32)
    return pl.pallas_call(
        body, name=name, grid=(rows // tr,),
        in_specs=[pl.BlockSpec((N_DEV, tr, PACK_W), lambda i: (0, i, 0)), spec, spec, spec],
        out_specs=[spec] * 4, out_shape=[sds] * 4, compiler_params=_cparams(("parallel",)))(parts, w, m, v)


def _pack(arrs, dtype, row_mult=16):
    flat = jnp.concatenate([a.reshape(-1).astype(dtype) for a in arrs])
    unit = row_mult * PACK_W
    total = -(-flat.shape[0] // unit) * unit
    flat = jnp.pad(flat, (0, total - flat.shape[0]))
    return flat.reshape(-1, PACK_W)


def _pack_lead(arrs, dtype, row_mult):
    flat = jnp.concatenate([a.reshape(N_DEV, -1).astype(dtype) for a in arrs], axis=1)
    unit = row_mult * PACK_W
    total = -(-flat.shape[1] // unit) * unit
    flat = jnp.pad(flat, ((0, 0), (0, total - flat.shape[1])))
    return flat.reshape(N_DEV, -1, PACK_W)


def _unpack(buf, shapes, lead=()):
    flat = buf.reshape(lead + (-1,))
    out, off = [], 0
    for s in shapes:
        n = int(np.prod(s))
        out.append(flat[..., off:off + n].reshape(lead + tuple(s)))
        off += n
    return out


def _unshard(g, kind):
    if kind == "col":
        g = jnp.moveaxis(g, 0, -2)
        return g.reshape(g.shape[:-2] + (g.shape[-2] * g.shape[-1],))
    g = jnp.moveaxis(g, 0, 1)
    return g.reshape((g.shape[0], g.shape[1] * g.shape[2]) + g.shape[3:])


def _shard(full, kind):
    if kind == "col":
        s = full.reshape(full.shape[:-1] + (N_DEV, full.shape[-1] // N_DEV))
        return jnp.moveaxis(s, -2, 0)
    s = full.reshape((full.shape[0], N_DEV, full.shape[1] // N_DEV) + full.shape[2:])
    return jnp.moveaxis(s, 1, 0)


class _Geo:
    def __init__(self, bsz, seq):
        self.bsz, self.seq = bsz, seq
        self.pad = (-(N_META + seq)) % ATT_BLK
        self.lp = self.pad + N_META + seq
        assert (self.pad + N_META) % SSM_CHUNK == 0 and self.lp % SSM_CHUNK == 0
        self.nrows = bsz * self.lp
        self.nc = self.lp // SSM_CHUNK
        self.nh = SSM_D_INNER // SSM_HEAD_DIM
        self.gn = SSM_GROUPS * SSM_STATE
        self.cd = SSM_D_INNER + 2 * self.gn
        self.hq = MLA_HEADS * LANE
        order = (("z", SSM_D_INNER), ("g_ssm", D_MODEL), ("g_mla", D_MODEL), ("xs", SSM_D_INNER), ("bm", self.gn),
                 ("cm", self.gn), ("c_q", MLA_Q_LORA), ("c_kv", MLA_KV_LORA), ("dt", LANE), ("k_rope", LANE))
        self.col, off = {}, 0
        for nm, w in order:
            assert off % w == 0, (nm, off, w)
            self.col[nm] = (off, w)
            off += w
        self.pw = off
        assert self.nh <= LANE and MLA_ROPE == 64 and MLA_NOPE == LANE and MLA_V == LANE
        self.tr = _pick(self.lp, (768, 512, 384, 256, 128))
        self.tr_wide = _pick(self.lp, (384, 256, 128))

    def cb(self, nm):
        off, w = self.col[nm]
        return off // w

    def w_in_runs(self, shard_w):
        nh, half = self.nh, MLA_ROPE // 2
        src, pieces = 0, []
        for nm, n in (("z", SSM_D_INNER), ("xs", SSM_D_INNER), ("bm", self.gn), ("cm", self.gn), ("dt", nh),
                      ("c_q", MLA_Q_LORA), ("c_kv", MLA_KV_LORA), ("k_rope", MLA_ROPE), ("g_ssm", D_MODEL),
                      ("g_mla", D_MODEL)):
            dst = self.col[nm][0]
            if nm == "k_rope":
                pieces += [(src, half, dst), (src + half, half, dst + 2 * half)]
            else:
                pieces.append((src, n, dst))
            src += n
        assert src == shard_w * N_DEV
        runs = []
        for a, n, dst in pieces:
            for j in range(N_DEV):
                lo, hi = max(a, j * shard_w), min(a + n, (j + 1) * shard_w)
                if lo < hi:
                    runs.append((j, lo - j * shard_w, hi - lo, dst + lo - a))
        return runs


def _slot(a):
    h = MLA_ROPE // 2
    z = jnp.zeros(a.shape[:-1] + (h,), a.dtype)
    return jnp.concatenate([a[..., :h], z, a[..., h:], z], axis=-1)


def _unslot(a):
    h = MLA_ROPE // 2
    return jnp.concatenate([a[..., :h], a[..., 2 * h:3 * h]], axis=-1)


def _prep_layer(geo, wl):
    nh = geo.nh
    p = {}
    if "w_uq" in wl:
        uq = wl["w_uq"].reshape(MLA_Q_LORA, MLA_HEADS, MLA_NOPE + MLA_ROPE)
        p["w_qn"] = uq[..., :MLA_NOPE].reshape(MLA_Q_LORA, geo.hq)
        p["w_qp"] = _slot(uq[..., MLA_NOPE:]).reshape(MLA_Q_LORA, geo.hq)
    if "w_ukv" in wl:
        ukv = wl["w_ukv"].reshape(MLA_KV_LORA, MLA_HEADS, MLA_NOPE + MLA_V)
        p["w_k"] = ukv[..., :MLA_NOPE].reshape(MLA_KV_LORA, geo.hq)
        p["w_v"] = ukv[..., MLA_NOPE:].reshape(MLA_KV_LORA, geo.hq)
    for nm in ("w_in_pt", "conv_w", "w_branch_ssm", "w_branch_mla", "w_out", "w_mlp_up", "w_mlp_down"):
        if nm in wl:
            p[nm] = wl[nm]
    for nm in ("norm_mix_w", "conv_b", "ssm_norm_w", "q_norm_w", "kv_norm_w", "norm_mlp_w"):
        if nm in wl:
            p[nm] = wl[nm].reshape(1, -1)
    if "dt_bias" in wl:
        p["dt_bias"] = jnp.pad(wl["dt_bias"], (0, LANE - nh)).reshape(1, LANE)
        p["a_log"] = jnp.pad(wl["a_log"], (0, LANE - nh)).reshape(1, LANE)
        p["d_skip_full"] = jnp.repeat(wl["d_skip"], SSM_HEAD_DIM).reshape(1, SSM_D_INNER)
    return p


def _unprep_grads(geo, g):
    nh = geo.nh
    out = {}
    if "w_qn" in g:
        qn = g["w_qn"].reshape(MLA_Q_LORA, MLA_HEADS, MLA_NOPE)
        qp = _unslot(g["w_qp"].reshape(MLA_Q_LORA, MLA_HEADS, LANE))
        out["w_uq"] = jnp.concatenate([qn, qp], axis=-1).reshape(MLA_Q_LORA, -1)
    if "w_k" in g:
        wk = g["w_k"].reshape(MLA_KV_LORA, MLA_HEADS, MLA_NOPE)
        wv = g["w_v"].reshape(MLA_KV_LORA, MLA_HEADS, MLA_V)
        out["w_ukv"] = jnp.concatenate([wk, wv], axis=-1).reshape(MLA_KV_LORA, -1)
    for nm in ("w_in_pt", "w_branch_ssm", "w_branch_mla", "w_out", "w_mlp_up", "w_mlp_down", "conv_w"):
        if nm in g:
            out[nm] = g[nm]
    for nm in ("norm_mix_w", "conv_b", "ssm_norm_w", "q_norm_w", "kv_norm_w", "norm_mlp_w"):
        if nm in g:
            out[nm] = g[nm].reshape(-1)
    if "dt_bias" in g:
        out["dt_bias"] = g["dt_bias"].reshape(-1)[:nh]
        out["a_log"] = g["a_log"].reshape(-1)[:nh]
        out["d_skip"] = g["d_skip_full"].reshape(nh, SSM_HEAD_DIM).sum(-1)
    return out


def _tables(geo):
    pos = jnp.arange(geo.lp, dtype=F32) - geo.pad
    inv = ROPE_THETA ** (-jnp.arange(0, MLA_ROPE, 2, dtype=F32) / MLA_ROPE)
    ang = pos[:, None] * inv[None, :]
    cos, sin = jnp.cos(ang), jnp.sin(ang)
    z = jnp.zeros_like(cos)
    rows = jnp.arange(geo.lp)[:, None]
    return {"cos": jnp.concatenate([cos, z, cos, z], axis=-1), "sin": jnp.concatenate([-sin, z, sin, z], axis=-1),
            "valid": (rows >= geo.pad).astype(F32), "token": (rows >= geo.pad + N_META).astype(F32)}


def _w_in_assemble(geo, gathered):
    _, sw, d = gathered.shape
    runs = geo.w_in_runs(sw)
    tl = _pick(d, (256, 128))

    def body(x_ref, o_ref):
        o_ref[...] = jnp.zeros_like(o_ref)
        for j, s0, n, d0 in runs:
            o_ref[d0:d0 + n, :] = x_ref[j, s0:s0 + n, :]

    return pl.pallas_call(
        body, name="w_in_assemble", grid=(d // tl,), in_specs=[pl.BlockSpec((N_DEV, sw, tl), lambda i: (0, 0, i))],
        out_specs=pl.BlockSpec((geo.pw, tl), lambda i: (0, i)),
        out_shape=jax.ShapeDtypeStruct((geo.pw, d), gathered.dtype), compiler_params=_cparams(("parallel",)))(gathered)


def _w_in_split(geo, g_padded, sw):
    d = g_padded.shape[1]
    runs = geo.w_in_runs(sw)
    tl = _pick(d, (256, 128))

    def body(x_ref, o_ref):
        for j, s0, n, d0 in runs:
            o_ref[j, s0:s0 + n, :] = x_ref[d0:d0 + n, :]

    return pl.pallas_call(
        body, name="w_in_split", grid=(d // tl,), in_specs=[pl.BlockSpec((geo.pw, tl), lambda i: (0, i))],
        out_specs=pl.BlockSpec((N_DEV, sw, tl), lambda i: (0, 0, i)),
        out_shape=jax.ShapeDtypeStruct((N_DEV, sw, d), g_padded.dtype),
        compiler_params=_cparams(("parallel",)))(g_padded)


def _conv_cols(geo, cbw):
    x0 = geo.col["xs"][0]
    assert geo.col["bm"][0] == x0 + SSM_D_INNER and geo.col["cm"][0] == geo.col["bm"][0] + geo.gn and x0 % cbw == 0
    return lambda j: x0 // cbw + j


def _conv_taps(x):
    return [pltpu.roll(x, SSM_CONV - 1 - k, axis=0) for k in range(SSM_CONV - 1)] + [x]


def _conv_pre(x, w_ref, b_ref, taps=None):
    taps = _conv_taps(x) if taps is None else taps
    acc = b_ref[...]
    for k in range(SSM_CONV):
        acc = acc + taps[k] * w_ref[k:k + 1, :]
    return acc


def _conv_fwd(geo, proj, conv_w, conv_b):
    cbw = 256
    colmap = _conv_cols(geo, cbw)
    lp, pad = geo.lp, geo.pad

    def body(x_ref, w_ref, b_ref, o_ref):
        valid = (lax.broadcasted_iota(jnp.int32, (lp, 1), 0) >= pad).astype(F32)
        o_ref[...] = (_silu(_conv_pre(x_ref[...].astype(F32), w_ref, b_ref)) * valid).astype(o_ref.dtype)

    return pl.pallas_call(
        body, name="conv_fwd", grid=(geo.bsz, geo.cd // cbw),
        in_specs=[pl.BlockSpec((lp, cbw), lambda b, j: (b, colmap(j))),
                  pl.BlockSpec((SSM_CONV, cbw), lambda b, j: (0, j)), pl.BlockSpec((1, cbw), lambda b, j: (0, j))],
        out_specs=pl.BlockSpec((lp, cbw), lambda b, j: (b, j)),
        out_shape=jax.ShapeDtypeStruct((geo.nrows, geo.cd), MXU_DTYPE),
        compiler_params=_cparams(("parallel", "parallel")))(proj, conv_w, conv_b)


def _conv_bwd(geo, proj, conv_w, conv_b, dxc, dproj):
    cbw = 256
    colmap = _conv_cols(geo, cbw)
    lp, pad = geo.lp, geo.pad

    def body(x_ref, w_ref, b_ref, dy_ref, _, dx_ref, gw_ref, gb_ref):
        b = pl.program_id(1)
        valid = (lax.broadcasted_iota(jnp.int32, (lp, 1), 0) >= pad).astype(F32)
        taps = _conv_taps(x_ref[...].astype(F32))
        pre = _conv_pre(None, w_ref, b_ref, taps)
        sig = _sigmoid(pre)
        dpre = dy_ref[...] * (sig * (1.0 + pre * (1.0 - sig))) * valid
        dx = dpre * w_ref[SSM_CONV - 1:SSM_CONV, :]
        for k in range(SSM_CONV - 1):
            dx = dx + pltpu.roll(dpre, lp - (SSM_CONV - 1 - k), axis=0) * w_ref[k:k + 1, :]
        gws = [jnp.sum(dpre * taps[k], axis=0, keepdims=True) for k in range(SSM_CONV)]
        dx_ref[...] = (dx * valid).astype(dx_ref.dtype)

        @pl.when(b == 0)
        def _():
            gw_ref[...] = jnp.zeros_like(gw_ref)
            gb_ref[...] = jnp.zeros_like(gb_ref)

        for k in range(SSM_CONV):
            gw_ref[k:k + 1, :] += gws[k]
        gb_ref[...] += jnp.sum(dpre, axis=0, keepdims=True)

    return pl.pallas_call(
        body, name="conv_bwd", grid=(geo.cd // cbw, geo.bsz),
        in_specs=[pl.BlockSpec((lp, cbw), lambda j, b: (b, colmap(j))),
                  pl.BlockSpec((SSM_CONV, cbw), lambda j, b: (0, j)), pl.BlockSpec((1, cbw), lambda j, b: (0, j)),
                  pl.BlockSpec((lp, cbw), lambda j, b: (b, j)), pl.BlockSpec(memory_space=pl.ANY)],
        out_specs=[pl.BlockSpec((lp, cbw), lambda j, b: (b, colmap(j))),
                   pl.BlockSpec((SSM_CONV, cbw), lambda j, b: (0, j)), pl.BlockSpec((1, cbw), lambda j, b: (0, j))],
        out_shape=[jax.ShapeDtypeStruct(dproj.shape, dproj.dtype),
                   jax.ShapeDtypeStruct((SSM_CONV, geo.cd), F32), jax.ShapeDtypeStruct((1, geo.cd), F32)],
        input_output_aliases={4: 0},
        compiler_params=_cparams(("parallel", "arbitrary")))(proj, conv_w, conv_b, dxc, dproj)


def _tri(q):
    r = lax.broadcasted_iota(jnp.int32, (q, q), 0)
    c = lax.broadcasted_iota(jnp.int32, (q, q), 1)
    return r >= c


def _ssd_pre(dtr, dtb, alog, valid):
    dt = _softplus(dtr + dtb) * valid
    adt = dt * (-jnp.exp(alog))
    a_cs = _dot(_tri(SSM_CHUNK).astype(F32), adt, 1, 0, precision=lax.Precision.HIGHEST)
    return dt, a_cs


def _ssd_specs(geo, rev):
    nc, q = geo.nc, SSM_CHUNK
    ci = (lambda c: nc - 1 - c) if rev else (lambda c: c)
    nxb = SSM_D_INNER // geo.gn
    return [pl.BlockSpec((q, SSM_D_INNER), lambda b, c: (b * nc + ci(c), 0)),
            pl.BlockSpec((q, geo.gn), lambda b, c: (b * nc + ci(c), nxb)),
            pl.BlockSpec((q, geo.gn), lambda b, c: (b * nc + ci(c), nxb + 1)),
            pl.BlockSpec((q, LANE), lambda b, c: (b * nc + ci(c), 0)),
            pl.BlockSpec((1, LANE), lambda b, c: (0, 0)), pl.BlockSpec((1, LANE), lambda b, c: (0, 0))], ci


def _expand_heads(cols, nh):
    per = LANE // SSM_HEAD_DIM
    lane = lax.broadcasted_iota(jnp.int32, (1, LANE), 1)
    blocks = []
    for j in range(nh // per):
        blk = jnp.broadcast_to(cols[:, j * per:j * per + 1], (cols.shape[0], LANE))
        for k in range(1, per):
            blk = jnp.where(lane >= k * SSM_HEAD_DIM, cols[:, j * per + k:j * per + k + 1], blk)
        blocks.append(blk)
    return jnp.concatenate(blocks, axis=1)


def _head_maps(geo):
    e = (jnp.arange(SSM_D_INNER)[None, :] // SSM_HEAD_DIM == jnp.arange(LANE)[:, None]).astype(F32)
    return e, e.T


def _ssd_fwd_g(geo, xc, proj, dt_bias, a_log):
    q, p, n, e = SSM_CHUNK, SSM_HEAD_DIM, SSM_STATE, geo.nh // SSM_GROUPS
    nc, pad, gw = geo.nc, geo.pad, SSM_D_INNER // SSM_GROUPS
    in_specs, _ = _ssd_specs(geo, False)

    def body(xs_ref, b_ref, c_ref, dtr_ref, dtb_ref, alog_ref, y_ref, sp_ref, state, xdt_s, y_s):
        c = pl.program_id(1)

        @pl.when(c == 0)
        def _():
            state[...] = jnp.zeros_like(state)

        sp_ref[...] = state[...]
        inert = (c + 1) * q <= pad

        @pl.when(inert)
        def _():
            y_ref[...] = jnp.zeros_like(y_ref)

        @pl.when(jnp.logical_not(inert))
        def _():
            valid = (c * q + lax.broadcasted_iota(jnp.int32, (q, 1), 0) >= pad).astype(F32)
            dt, a_cs = _ssd_pre(dtr_ref[...], dtb_ref[...], alog_ref[...], valid)
            a_cst = a_cs.T
            dt_x, a_x = _expand_heads(dt, geo.nh), _expand_heads(a_cs, geo.nh)
            tri = _tri(q)
            for g in range(SSM_GROUPS):
                gs = slice(g * gw, (g + 1) * gw)
                bg, cg = b_ref[:, g * n:(g + 1) * n], c_ref[:, g * n:(g + 1) * n]
                a_g = a_x[:, gs]
                a_last = a_g[q - 1:q, :]
                xdt_g = xs_ref[:, gs] * dt_x[:, gs]
                xdt_s[:, gs] = xdt_g
                s_g = state[:, gs]
                y_s[:, gs] = _mxdot(cg, s_g, 1, 0) * jnp.exp(a_g)
                state[:, gs] = s_g * jnp.exp(a_last) + _mxdot(bg, xdt_g * jnp.exp(a_last - a_g), 0, 0)
                cb = _mxdot(cg, bg, 1, 1)
                for hh in range(e):
                    h = g * e + hh
                    hs = slice(h * p, (h + 1) * p)
                    ldec = jnp.exp(jnp.where(tri, a_cs[:, h:h + 1] - a_cst[h:h + 1, :], -jnp.inf))
                    y_s[:, hs] += _mxdot(cb * ldec, xdt_s[:, hs], 1, 0)
            y_ref[...] = y_s[...].astype(y_ref.dtype)

    return pl.pallas_call(
        body, name="ssd_fwd", grid=(geo.bsz, nc), in_specs=in_specs,
        out_specs=[pl.BlockSpec((q, SSM_D_INNER), lambda b, c: (b * nc + c, 0)),
                   pl.BlockSpec((n, SSM_D_INNER), lambda b, c: (b * nc + c, 0))],
        out_shape=[jax.ShapeDtypeStruct((geo.nrows, SSM_D_INNER), MXU_DTYPE),
                   jax.ShapeDtypeStruct((geo.bsz * nc * n, SSM_D_INNER), F32)],
        scratch_shapes=[pltpu.VMEM((n, SSM_D_INNER), F32), pltpu.VMEM((q, SSM_D_INNER), F32),
                        pltpu.VMEM((q, SSM_D_INNER), F32)],
        compiler_params=_cparams(("parallel", "arbitrary")))(xc, xc, xc, proj, dt_bias, a_log)


def _ssd_bwd_g(geo, xc, proj, dt_bias, a_log, s_prev_all, dy, dxs_skip, dproj):
    q, p, n, e = SSM_CHUNK, SSM_HEAD_DIM, SSM_STATE, geo.nh // SSM_GROUPS
    nc, pad, di, gn, gw = geo.nc, geo.pad, SSM_D_INNER, geo.gn, SSM_D_INNER // SSM_GROUPS
    in_specs, ci = _ssd_specs(geo, True)
    row_spec = pl.BlockSpec((q, di), lambda b, c: (b * nc + ci(c), 0))
    e_map, _ = _head_maps(geo)
    in_specs += [pl.BlockSpec((n, di), lambda b, c: (b * nc + ci(c), 0)), row_spec, row_spec,
                 pl.BlockSpec((LANE, di), lambda b, c: (0, 0)), pl.BlockSpec(memory_space=pl.ANY)]

    def body(xs_ref, b_ref, c_ref, dtr_ref, dtb_ref, alog_ref, sp_ref, dy_ref, dsk_ref, e_ref, _,
             dxc_ref, ddt_ref, gdtb_ref, galog_ref, dstate, xdt_s, dxdt_s):
        step = pl.program_id(1)
        first = jnp.logical_and(pl.program_id(0) == 0, step == 0)
        c = nc - 1 - step

        @pl.when(step == 0)
        def _():
            dstate[...] = jnp.zeros_like(dstate)

        @pl.when(first)
        def _():
            gdtb_ref[...] = jnp.zeros_like(gdtb_ref)
            galog_ref[...] = jnp.zeros_like(galog_ref)

        inert = (c + 1) * q <= pad

        @pl.when(inert)
        def _():
            dxc_ref[...] = jnp.zeros_like(dxc_ref)
            ddt_ref[...] = jnp.zeros_like(ddt_ref)

        @pl.when(jnp.logical_not(inert))
        def _():
            valid = (c * q + lax.broadcasted_iota(jnp.int32, (q, 1), 0) >= pad).astype(F32)
            dtr, dtb, alog = dtr_ref[...], dtb_ref[...], alog_ref[...]
            dt, a_cs = _ssd_pre(dtr, dtb, alog, valid)
            a_cst = a_cs.T
            dt_x, a_x = _expand_heads(dt, geo.nh), _expand_heads(a_cs, geo.nh)
            tri = _tri(q)
            lane = lax.broadcasted_iota(jnp.int32, (1, LANE), 1)
            sub = lax.broadcasted_iota(jnp.int32, (LANE, 1), 0)
            d_dt = jnp.zeros((q, LANE), F32)
            d_acs = jnp.zeros((q, LANE), F32)
            d_acst = jnp.zeros((LANE, q), F32)
            d_last = jnp.zeros((1, LANE), F32)
            for g in range(SSM_GROUPS):
                gs = slice(g * gw, (g + 1) * gw)
                bg, cg = b_ref[:, g * n:(g + 1) * n], c_ref[:, g * n:(g + 1) * n]
                seg = lambda v: _mxdot(v, e_ref[:, gs], 1, 1)
                a_g, dt_g, x_g, dy_g = a_x[:, gs], dt_x[:, gs], xs_ref[:, gs], dy_ref[:, gs]
                e_col, e_last, dec = jnp.exp(a_g), jnp.exp(a_g[q - 1:q, :]), jnp.exp(a_g[q - 1:q, :] - a_g)
                xdt_g = x_g * dt_g
                xdt_s[:, gs] = xdt_g
                s_g, ds_g = sp_ref[:, gs], dstate[:, gs]
                cs = _mxdot(cg, s_g, 1, 0)
                d_cs = dy_g * e_col
                d_acs = d_acs + seg(d_cs * cs)
                d_cg = _mxdot(d_cs, s_g, 1, 1)
                dstate[:, gs] = _mxdot(cg, d_cs, 0, 0) + ds_g * e_last
                dl_x = jnp.sum(ds_g * s_g, axis=0, keepdims=True) * e_last
                d_last = d_last + seg(jnp.broadcast_to(dl_x, (8, gw)))[:1]
                gmat = _mxdot(bg, ds_g, 1, 0)
                xd = xdt_g * dec
                d_bg = _mxdot(xd, ds_g, 1, 1)
                d_dec = seg(xd * gmat)
                d_acs = d_acs - d_dec
                d_last = d_last + jnp.sum(d_dec, axis=0, keepdims=True)
                dxdt_s[:, gs] = dec * gmat
                cb = _mxdot(cg, bg, 1, 1)
                d_cb = jnp.zeros((q, q), F32)
                for hh in range(e):
                    h = g * e + hh
                    hs = slice(h * p, (h + 1) * p)
                    ldec = jnp.exp(jnp.where(tri, a_cs[:, h:h + 1] - a_cst[h:h + 1, :], -jnp.inf))
                    dyh = dy_ref[:, hs]
                    d_m = _mxdot(dyh, xdt_s[:, hs], 1, 1)
                    dxdt_s[:, hs] += _mxdot(cb * ldec, dyh, 0, 0)
                    d_cb = d_cb + d_m * ldec
                    d_diff = d_m * cb * ldec
                    d_acs = d_acs + jnp.sum(d_diff, axis=1, keepdims=True) * (lane == h).astype(F32)
                    d_acst = d_acst - (sub == h).astype(F32) * jnp.sum(d_diff, axis=0, keepdims=True)
                d_xdt = dxdt_s[:, gs]
                dxc_ref[:, gs] = d_xdt * dt_g + dsk_ref[:, gs]
                d_dt = d_dt + seg(d_xdt * x_g)
                dxc_ref[:, di + g * n:di + (g + 1) * n] = d_bg + _mxdot(d_cb, cg, 0, 0)
                dxc_ref[:, di + gn + g * n:di + gn + (g + 1) * n] = d_cg + _mxdot(d_cb, bg, 1, 0)
            is_last = (lax.broadcasted_iota(jnp.int32, (q, 1), 0) == q - 1).astype(F32)
            d_acs = d_acs + d_acst.T + is_last * d_last
            d_adt = _dot(_tri(q).astype(F32), d_acs, 0, 0, precision=lax.Precision.HIGHEST)
            a = -jnp.exp(alog)
            d_dt = d_dt + d_adt * a
            d_dtr = d_dt * valid * _sigmoid(dtr + dtb)
            ddt_ref[...] = d_dtr.astype(ddt_ref.dtype)
            gdtb_ref[...] += jnp.sum(d_dtr, axis=0, keepdims=True)
            galog_ref[...] += jnp.sum(d_adt * dt, axis=0, keepdims=True) * a

    vec = pl.BlockSpec((1, LANE), lambda b, c: (0, 0))
    return pl.pallas_call(
        body, name="ssd_bwd", grid=(geo.bsz, nc), in_specs=in_specs,
        out_specs=[pl.BlockSpec((q, geo.cd), lambda b, c: (b * nc + ci(c), 0)),
                   pl.BlockSpec((q, LANE), lambda b, c: (b * nc + ci(c), geo.cb("dt"))), vec, vec],
        out_shape=[jax.ShapeDtypeStruct((geo.nrows, geo.cd), F32), jax.ShapeDtypeStruct(dproj.shape, dproj.dtype),
                   jax.ShapeDtypeStruct((1, LANE), F32), jax.ShapeDtypeStruct((1, LANE), F32)],
        scratch_shapes=[pltpu.VMEM((n, di), F32), pltpu.VMEM((q, di), F32), pltpu.VMEM((q, di), F32)],
        input_output_aliases={10: 1},
        compiler_params=_cparams(("arbitrary", "arbitrary")))(
            xc, xc, xc, proj, dt_bias, a_log, s_prev_all, dy, dxs_skip, e_map, dproj)


BIAS_LANE = MLA_ROPE // 2
KEY_OFF = -1e30
ATT_SCALE = (MLA_NOPE + MLA_ROPE) ** -0.5


def _row_t(col):
    return jnp.broadcast_to(col, (col.shape[0], LANE)).T[:8]


def _attn_fwd2(geo, qn, qp, kn, kp, v):
    t, lp = ATT_BLK, geo.lp
    nb = lp // t

    def body(qn_ref, qp_ref, kn_ref, kp_ref, v_ref, o_ref, lse_ref, k_ref):
        qi = pl.program_id(2)

        @pl.when(qi == 0)
        def _():
            k_ref[:, :LANE] = kn_ref[...]
            k_ref[:, LANE:] = kp_ref[...]

        q = jnp.concatenate([qn_ref[...], qp_ref[...]], axis=1)

        def blk(kj, ntile, carry, diag):
            m, l, acc = carry
            ks = pl.ds(pl.multiple_of(kj * t, t), ntile * t)
            s = _mxdot(q, k_ref[ks, :], 1, 1) * ATT_SCALE
            if diag:
                s = jnp.where(_tri(t), s, -jnp.inf)
            m_new = jnp.maximum(m, jnp.max(s, axis=1, keepdims=True))
            pr = jnp.exp(s - m_new)
            alpha = jnp.exp(m - m_new)
            return m_new, alpha * l + jnp.sum(pr, axis=1, keepdims=True), alpha * acc + _mxdot(pr, v_ref[ks, :], 1, 0)

        carry = (jnp.full((t, 1), 2.0 * KEY_OFF, F32), jnp.zeros((t, 1), F32), jnp.zeros((t, LANE), F32))
        done = 0
        for ntile in (4, 2, 1):
            steps = (qi - done) // ntile
            carry = lax.fori_loop(0, steps, lambda j, c, d=done, n=ntile: blk(d + n * j, n, c, False), carry)
            done = done + steps * ntile
        m, l, acc = blk(qi, 1, carry, True)
        o_ref[...] = (acc / l).astype(o_ref.dtype)
        lse_ref[0, 0, 0] = _row_t(m + jnp.log(l))

    tile = pl.BlockSpec((t, LANE), lambda b, h, i: (b * nb + i, h))
    seq = pl.BlockSpec((lp, LANE), lambda b, h, i: (b, h))
    return pl.pallas_call(
        body, name="attn_fwd", grid=(geo.bsz, MLA_HEADS, nb),
        in_specs=[tile, tile, seq, pl.BlockSpec((lp, LANE), lambda b, h, i: (b, 0)), seq],
        out_specs=[tile, pl.BlockSpec((1, 1, 1, 8, t), lambda b, h, i: (b, h, i, 0, 0))],
        out_shape=[jax.ShapeDtypeStruct((geo.nrows, geo.hq), MXU_DTYPE),
                   jax.ShapeDtypeStruct((geo.bsz, MLA_HEADS, nb, 8, t), F32)],
        scratch_shapes=[pltpu.VMEM((lp, 2 * LANE), MXU_DTYPE)],
        compiler_params=_cparams(("parallel", "parallel", "arbitrary")))(qn, qp, kn, kp, v)


def _attn_bwd2(geo, qn, qp, kn, kp, v, d_o, o, lse):
    t, lp = ATT_BLK, geo.lp
    nb = lp // t

    def body(qn_ref, qp_ref, kn_ref, kp_ref, v_ref, do_ref, o_ref, lse_ref,
             dqn_ref, dqp_ref, dkn_ref, dkp_ref, dv_ref, q_ref, dl_s):
        kj = pl.program_id(2)

        @pl.when(kj == 0)
        def _():
            q_ref[:, :LANE] = qn_ref[...]
            q_ref[:, LANE:] = qp_ref[...]
            dqn_ref[...] = jnp.zeros_like(dqn_ref)
            dqp_ref[...] = jnp.zeros_like(dqp_ref)
            for i in range(nb):
                rows = slice(i * t, (i + 1) * t)
                dl_s[i] = _row_t(jnp.sum(do_ref[rows, :].astype(F32) * o_ref[rows, :].astype(F32), axis=1, keepdims=True))

        k, vv = jnp.concatenate([kn_ref[...], kp_ref[...]], axis=1), v_ref[...]

        def row(ref, qi, ntile):
            return jnp.concatenate([ref[qi + i][:1, :] for i in range(ntile)], axis=1)

        def blk(qi, ntile, carry, diag):
            dk, dv = carry
            qs = pl.ds(pl.multiple_of(qi * t, t), ntile * t)
            q, d_o_blk = q_ref[qs, :], do_ref[qs, :]
            st = _mxdot(k, q, 1, 1) * ATT_SCALE
            if diag:
                keys = lax.broadcasted_iota(jnp.int32, (t, t), 0)
                st = jnp.where(keys <= lax.broadcasted_iota(jnp.int32, (t, t), 1), st, -jnp.inf)
            pt = jnp.exp(st - row(lse_ref.at[0, 0], qi, ntile))
            dst = pt * (_mxdot(vv, d_o_blk, 1, 1) - row(dl_s, qi, ntile)) * ATT_SCALE
            dq = _mxdot(dst, k, 0, 0)
            dqn_ref[qs, :] += dq[:, :LANE]
            dqp_ref[qs, :] += dq[:, LANE:]
            return dk + _mxdot(dst, q, 1, 0), dv + _mxdot(pt, d_o_blk, 1, 0)

        carry = blk(kj, 1, (jnp.zeros((t, 2 * LANE), F32), jnp.zeros((t, LANE), F32)), True)
        done = kj + 1
        for ntile in (4, 2, 1):
            steps = (nb - done) // ntile
            carry = lax.fori_loop(0, steps, lambda j, c, d=done, n=ntile: blk(d + n * j, n, c, False), carry)
            done = done + steps * ntile
        dk, dv = carry
        dkn_ref[...] = dk[:, :LANE].astype(dkn_ref.dtype)
        dkp_ref[...] = dk[:, LANE:]
        dv_ref[...] = dv.astype(dv_ref.dtype)

    seq = pl.BlockSpec((lp, LANE), lambda b, h, j: (b, h))
    tile = pl.BlockSpec((t, LANE), lambda b, h, j: (b * nb + j, h))
    return pl.pallas_call(
        body, name="attn_bwd", grid=(geo.bsz, MLA_HEADS, nb),
        in_specs=[seq, seq, tile, pl.BlockSpec((t, LANE), lambda b, h, j: (b * nb + j, 0)), tile, seq, seq,
                  pl.BlockSpec((1, 1, nb, 8, t), lambda b, h, j: (b, h, 0, 0, 0))],
        out_specs=[seq, seq, tile, tile, tile],
        out_shape=[jax.ShapeDtypeStruct((geo.nrows, geo.hq), F32), jax.ShapeDtypeStruct((geo.nrows, geo.hq), F32),
                   jax.ShapeDtypeStruct((geo.nrows, geo.hq), MXU_DTYPE), jax.ShapeDtypeStruct((geo.nrows, geo.hq), F32),
                   jax.ShapeDtypeStruct((geo.nrows, geo.hq), MXU_DTYPE)],
        scratch_shapes=[pltpu.VMEM((lp, 2 * LANE), MXU_DTYPE), pltpu.VMEM((nb, 8, t), F32)],
        compiler_params=_cparams(("parallel", "parallel", "arbitrary")))(qn, qp, kn, kp, v, d_o, o, lse)


def _rope(x, cos, sin):
    return x * cos + pltpu.roll(x, LANE // 2, axis=1) * sin


def _rope_t(dx, cos, sin):
    return dx * cos + pltpu.roll(dx * sin, LANE // 2, axis=1)


def _per_head(f):
    def fn(x, cos, sin):
        return (jnp.concatenate([f(x[:, h * LANE:(h + 1) * LANE], cos, sin) for h in range(MLA_HEADS)], axis=1),)
    return fn


def _layer_fwd(geo, h, w, tab, late=None):
    nr, tr, trw = geo.nrows, geo.tr, geo.tr_wide
    tb = geo.lp // tr
    rw = functools.partial(_rowwise, nrows=nr)
    s = {"h": h}
    (s["u"],) = rw("rms_mix", lambda x, g: (_rms(x, g),), tr=tr, rows=[(h, D_MODEL, 0)],
                   vecs=[(w["norm_mix_w"], D_MODEL, 0)], outs=[(D_MODEL, D_MODEL, MXU_DTYPE)])
    proj, s["proj_dt"] = _mm("mm_in", s["u"], w["w_in_pt"], tb=True, out_dtype=MXU_DTYPE,
                             side=(geo.col["dt"][0], LANE))
    s["proj"] = proj
    xc = s["xc"] = _conv_fwd(geo, proj, w["conv_w"], w["conv_b"])
    s["y_ssd"], s["s_prev"] = _ssd_fwd_g(geo, xc, s["proj_dt"], w["dt_bias"], w["a_log"])
    gw = SSM_D_INNER // SSM_GROUPS

    def gate_norm(y, x, z, dsk, nw):
        return (_rms((y + x * dsk) * _silu(z.astype(F32)), nw),)

    (s["y_ssm"],) = rw("ssm_gate_norm", gate_norm, tr=tr, ncb=SSM_GROUPS,
                       rows=[(s["y_ssd"], gw, 0), (xc, gw, 0), (proj, gw, geo.col["z"][0] // gw)],
                       vecs=[(w["d_skip_full"], gw, 0), (w["ssm_norm_w"], gw, 0)], outs=[(SSM_D_INNER, gw, MXU_DTYPE)])
    if late is not None:
        w = {**w, **late(s["y_ssm"])}
    (s["cq_n"],) = rw("rms_q", lambda x, g: (_rms(x, g),), tr=tr, rows=[(proj, MLA_Q_LORA, geo.cb("c_q"))],
                      vecs=[(w["q_norm_w"], MLA_Q_LORA, 0)], outs=[(MLA_Q_LORA, MLA_Q_LORA, MXU_DTYPE)])
    (s["ckv_n"],) = rw("rms_kv", lambda x, g: (_rms(x, g),), tr=tr, rows=[(proj, MLA_KV_LORA, geo.cb("c_kv"))],
                       vecs=[(w["kv_norm_w"], MLA_KV_LORA, 0)], outs=[(MLA_KV_LORA, MLA_KV_LORA, MXU_DTYPE)])
    s["qn"] = _mm("mm_qn", s["cq_n"], w["w_qn"], out_dtype=MXU_DTYPE)
    qp_raw = _mm("mm_qp", s["cq_n"], w["w_qp"])
    s["kn"] = _mm("mm_kn", s["ckv_n"], w["w_k"], out_dtype=MXU_DTYPE)
    s["v"] = _mm("mm_v", s["ckv_n"], w["w_v"], out_dtype=MXU_DTYPE)
    bias_lane = lambda: lax.broadcasted_iota(jnp.int32, (1, LANE), 1) == BIAS_LANE
    rope_tabs = [(tab["cos"], LANE, 0), (tab["sin"], LANE, 0)]
    (s["qp"],) = rw("rope_q", _per_head(lambda xp, c, sn: jnp.where(bias_lane(), 1.0, _rope(xp, c, sn))), tr=tr,
                    rows=[(qp_raw, geo.hq, 0)], tabs=rope_tabs, outs=[(geo.hq, geo.hq, MXU_DTYPE)], tab_blocks=tb)
    (s["kp"],) = rw("rope_k", lambda xp, c, sn, valid: (jnp.where(bias_lane(), KEY_OFF * (1.0 - valid),
                                                                 _rope(xp.astype(F32), c, sn)),),
                    tr=tr, rows=[(proj, LANE, geo.cb("k_rope"))], tabs=rope_tabs + [(tab["valid"], 1, 0)],
                    outs=[(LANE, LANE, MXU_DTYPE)], tab_blocks=tb)
    s["o"], s["lse"] = _attn_fwd2(geo, s["qn"], s["qp"], s["kn"], s["kp"], s["v"])
    s["ys_p"] = _mm("mm_bs", s["y_ssm"], w["w_branch_ssm"], out_dtype=MXU_DTYPE)
    s["ym_p"] = _mm("mm_bm", s["o"], w["w_branch_mla"], out_dtype=MXU_DTYPE)

    def gate(gs, gm, ys, ym):
        return (_sigmoid(gs.astype(F32)) * ys + _sigmoid(gm.astype(F32)) * ym,)

    (s["mixed"],) = rw("gate", gate, tr=tr, rows=[(proj, D_MODEL, geo.cb("g_ssm")), (proj, D_MODEL, geo.cb("g_mla")),
                                                  (s["ys_p"], D_MODEL, 0), (s["ym_p"], D_MODEL, 0)],
                       outs=[(D_MODEL, D_MODEL, MXU_DTYPE)])
    s["h2"] = _mm("mm_out", s["mixed"], w["w_out"], add=h)
    (s["vn"],) = rw("rms_mlp", lambda x, g: (_rms(x, g),), tr=tr, rows=[(s["h2"], D_MODEL, 0)],
                    vecs=[(w["norm_mlp_w"], D_MODEL, 0)], outs=[(D_MODEL, D_MODEL, MXU_DTYPE)])
    s["up"], s["act"] = _mm("mm_up", s["vn"], w["w_mlp_up"],
                            epi=(lambda r: (r, jnp.square(jnp.maximum(r, 0.0))), (MXU_DTYPE, MXU_DTYPE)))
    return _mm("mm_down", s["act"], w["w_mlp_down"], add=s["h2"]), s, w


def _layer_bwd(geo, dh3, s, w, tab, mid=None, tail=None, dep=None):
    nr, tr, trw = geo.nrows, geo.tr, geo.tr_wide
    tb = geo.lp // tr
    rw = functools.partial(_rowwise, nrows=nr)
    g = {}
    proj = s["proj"]

    def rms_bwd(x, dy, res, gw):
        _, vjp = jax.vjp(_rms, x.astype(F32), gw)
        dx, dgw = vjp(dy.astype(F32))
        return dx + res, dgw

    def rms_bwd_nores(x, dy, gw):
        _, vjp = jax.vjp(_rms, x.astype(F32), gw)
        return vjp(dy.astype(F32))

    (dup,) = _mm("mm_down_t", dh3, w["w_mlp_down"], tb=True, add=s["up"], dep=dep,
                 epi=(lambda r, up: (r * 2.0 * jnp.maximum(up, 0.0),), (MXU_DTYPE,)))
    g["w_mlp_down"] = _mm("mm_down_g", s["act"], dh3, ta=True, out_dtype=MXU_DTYPE)
    g["w_mlp_up"] = _mm("mm_up_g", s["vn"], dup, ta=True, out_dtype=MXU_DTYPE)
    dvn = _mm("mm_up_t", dup, w["w_mlp_up"], tb=True)
    dh2, g["norm_mlp_w"] = rw("rms_mlp_bwd", rms_bwd, tr=tr,
                              rows=[(s["h2"], D_MODEL, 0), (dvn, D_MODEL, 0), (dh3, D_MODEL, 0)],
                              vecs=[(w["norm_mlp_w"], D_MODEL, 0)], outs=[(D_MODEL, D_MODEL, F32)],
                              reds=[(D_MODEL, D_MODEL)])
    dmixed = _mm("mm_out_t", dh2, w["w_out"], tb=True, out_dtype=MXU_DTYPE)
    g["w_out"] = _mm("mm_out_g", s["mixed"], dh2, ta=True, out_dtype=MXU_DTYPE)

    def gate_bwd(gs, gm, ys, ym, dm):
        f = lambda a, b, c, d: _sigmoid(a) * c + _sigmoid(b) * d
        _, vjp = jax.vjp(f, gs.astype(F32), gm.astype(F32), ys.astype(F32), ym.astype(F32))
        dgs, dgm, dys, dym = vjp(dm.astype(F32))
        return dys, dym, jnp.concatenate([dgs, dgm], axis=1)

    assert geo.col["g_mla"][0] == geo.col["g_ssm"][0] + D_MODEL and geo.col["g_ssm"][0] % (2 * D_MODEL) == 0
    dys_p, dym_p, dproj = rw(
        "gate_bwd", gate_bwd, tr=tr,
        rows=[(proj, D_MODEL, geo.cb("g_ssm")), (proj, D_MODEL, geo.cb("g_mla")), (s["ys_p"], D_MODEL, 0),
              (s["ym_p"], D_MODEL, 0), (dmixed, D_MODEL, 0)],
        outs=[(D_MODEL, D_MODEL, MXU_DTYPE)] * 2 + [(geo.pw, 2 * D_MODEL, MXU_DTYPE, geo.col["g_ssm"][0] // (2 * D_MODEL))])
    g["w_branch_ssm"] = _mm("mm_bs_g", s["y_ssm"], dys_p, ta=True, out_dtype=MXU_DTYPE)
    dy_ssm = _mm("mm_bs_t", dys_p, w["w_branch_ssm"], tb=True, out_dtype=MXU_DTYPE)
    g["w_branch_mla"] = _mm("mm_bm_g", s["o"], dym_p, ta=True, out_dtype=MXU_DTYPE)
    d_o = _mm("mm_bm_t", dym_p, w["w_branch_mla"], tb=True, out_dtype=MXU_DTYPE)
    dqn, dqp, dkn, dkp_h, dv = _attn_bwd2(geo, s["qn"], s["qp"], s["kn"], s["kp"], s["v"], d_o, s["o"], s["lse"])
    rope_tabs = [(tab["cos"], LANE, 0), (tab["sin"], LANE, 0)]
    (dqp_raw,) = rw("rope_q_bwd", _per_head(_rope_t), tr=tr, rows=[(dqp, geo.hq, 0)], tabs=rope_tabs,
                    outs=[(geo.hq, geo.hq, MXU_DTYPE)], tab_blocks=tb)

    def rope_k_bwd(x, c, sn):
        tot = x[:, :LANE]
        for hd in range(1, MLA_HEADS):
            tot = tot + x[:, hd * LANE:(hd + 1) * LANE]
        return (_rope_t(tot, c, sn),)

    (dproj,) = rw("rope_k_bwd", rope_k_bwd, tr=tr, rows=[(dkp_h, geo.hq, 0)], tabs=rope_tabs,
                  outs=[(geo.pw, LANE, MXU_DTYPE, geo.cb("k_rope"), dproj)], tab_blocks=tb)
    g["w_qn"] = _mm("mm_qn_g", s["cq_n"], dqn, ta=True, out_dtype=MXU_DTYPE)
    g["w_qp"] = _mm("mm_qp_g", s["cq_n"], dqp_raw, ta=True, out_dtype=MXU_DTYPE)
    dcq_n = _mm("mm_qp_t", dqp_raw, w["w_qp"], tb=True, add=_mm("mm_qn_t", dqn, w["w_qn"], tb=True))
    g["w_k"] = _mm("mm_kn_g", s["ckv_n"], dkn, ta=True, out_dtype=MXU_DTYPE)
    g["w_v"] = _mm("mm_v_g", s["ckv_n"], dv, ta=True, out_dtype=MXU_DTYPE)
    dckv_n = _mm("mm_v_t", dv, w["w_v"], tb=True, add=_mm("mm_kn_t", dkn, w["w_k"], tb=True))
    dproj, g["q_norm_w"] = rw("rms_q_bwd", rms_bwd_nores, tr=tr,
                              rows=[(proj, MLA_Q_LORA, geo.cb("c_q")), (dcq_n, MLA_Q_LORA, 0)],
                              vecs=[(w["q_norm_w"], MLA_Q_LORA, 0)],
                              outs=[(geo.pw, MLA_Q_LORA, MXU_DTYPE, geo.cb("c_q"), dproj)], reds=[(MLA_Q_LORA, MLA_Q_LORA)])
    dproj, g["kv_norm_w"] = rw("rms_kv_bwd", rms_bwd_nores, tr=tr,
                               rows=[(proj, MLA_KV_LORA, geo.cb("c_kv")), (dckv_n, MLA_KV_LORA, 0)],
                               vecs=[(w["kv_norm_w"], MLA_KV_LORA, 0)],
                               outs=[(geo.pw, MLA_KV_LORA, MXU_DTYPE, geo.cb("c_kv"), dproj)],
                               reds=[(MLA_KV_LORA, MLA_KV_LORA)])
    gw_ = SSM_D_INNER // SSM_GROUPS
    d_skip_full = w["d_skip_full"] if mid is None else w["d_skip_full"] + mid(g)[0, 0]

    def gate_norm_bwd(y, x, z, dy, dsk, nw):
        f = lambda y_, x_, z_, dsk_, nw_: _rms((y_ + x_ * dsk_) * _silu(z_), nw_)
        _, vjp = jax.vjp(f, y.astype(F32), x.astype(F32), z.astype(F32), dsk, nw)
        dy_, dx_, dz_, ddsk, dnw = vjp(dy.astype(F32))
        return dy_, dx_, dz_, ddsk, dnw

    dy_ssd, dxs_skip, dproj, g["d_skip_full"], g["ssm_norm_w"] = rw(
        "ssm_gate_norm_bwd", gate_norm_bwd, tr=tr, ncb=SSM_GROUPS,
        rows=[(s["y_ssd"], gw_, 0), (s["xc"], gw_, 0), (proj, gw_, geo.col["z"][0] // gw_), (dy_ssm, gw_, 0)],
        vecs=[(d_skip_full, gw_, 0), (w["ssm_norm_w"], gw_, 0)],
        outs=[(SSM_D_INNER, gw_, MXU_DTYPE), (SSM_D_INNER, gw_, MXU_DTYPE),
              (geo.pw, gw_, MXU_DTYPE, geo.col["z"][0] // gw_, dproj)],
        reds=[(SSM_D_INNER, gw_), (SSM_D_INNER, gw_)])
    dxc, dproj, g["dt_bias"], g["a_log"] = _ssd_bwd_g(geo, s["xc"], s["proj_dt"], w["dt_bias"], w["a_log"], s["s_prev"],
                                                     dy_ssd, dxs_skip, dproj)
    dproj, g["conv_w"], g["conv_b"] = _conv_bwd(geo, proj, w["conv_w"], w["conv_b"], dxc, dproj)
    g["w_in_pt"] = _mm("mm_in_g", dproj, s["u"], ta=True, out_dtype=MXU_DTYPE)
    du = _mm("mm_in_t", dproj, w["w_in_pt"], dep=None if tail is None else tail(g))
    dh, g["norm_mix_w"] = rw("rms_mix_bwd", rms_bwd, tr=tr,
                             rows=[(s["h"], D_MODEL, 0), (du, D_MODEL, 0), (dh2, D_MODEL, 0)],
                             vecs=[(w["norm_mix_w"], D_MODEL, 0)], outs=[(D_MODEL, D_MODEL, F32)],
                             reds=[(D_MODEL, D_MODEL)])
    return dh, g


def _loss_bwd(geo, h, fw, target, tab):
    tr = geo.tr

    def fn(x, tgt, gw, tok):
        def lossf(x_, gw_):
            err = jnp.square(_rms(x_, gw_) - tgt)
            return 0.5 * jnp.sum(tok * jnp.mean(err, axis=-1, keepdims=True), axis=0, keepdims=True)

        val, vjp = jax.vjp(lossf, x, gw)
        dx, dgw = vjp(jnp.ones((1, 1), F32))
        return dx, jnp.broadcast_to(val, (1, LANE)), dgw

    return _rowwise("loss", fn, nrows=geo.nrows, tr=tr, rows=[(h, D_MODEL, 0), (target, D_MODEL, 0)],
                    vecs=[(fw, D_MODEL, 0)], tabs=[(tab["token"], 1, 0)], outs=[(D_MODEL, D_MODEL, F32)],
                    reds=[(LANE, LANE), (D_MODEL, D_MODEL)], tab_blocks=geo.lp // tr)


def kernel(x, meta_tokens, norm_mix_w, w_in, conv_w, conv_b, dt_bias, a_log, d_skip, ssm_norm_w, q_norm_w, kv_norm_w, w_uq, w_ukv, w_branch_ssm, w_branch_mla, w_out, norm_mlp_w, w_mlp_up, w_mlp_down, final_norm_w, loss_target, m_meta_tokens, m_norm_mix_w, m_w_in, m_conv_w, m_conv_b, m_dt_bias, m_a_log, m_d_skip, m_ssm_norm_w, m_q_norm_w, m_kv_norm_w, m_w_uq, m_w_ukv, m_w_branch_ssm, m_w_branch_mla, m_w_out, m_norm_mlp_w, m_w_mlp_up, m_w_mlp_down, m_final_norm_w, v_meta_tokens, v_norm_mix_w, v_w_in, v_conv_w, v_conv_b, v_dt_bias, v_a_log, v_d_skip, v_ssm_norm_w, v_q_norm_w, v_kv_norm_w, v_w_uq, v_w_ukv, v_w_branch_ssm, v_w_branch_mla, v_w_out, v_norm_mlp_w, v_w_mlp_up, v_w_mlp_down, v_final_norm_w):
    args = dict(locals())
    turn = lambda n, a: jnp.swapaxes(a, 1, 2) if n == "w_in" else a
    wts = {n: turn(n, args[n]) for n in WEIGHTS}
    mom = {n: turn(n, args["m_" + n]) for n in WEIGHTS}
    var = {n: turn(n, args["v_" + n]) for n in WEIGHTS}
    bsz, seq, _ = x.shape
    depth = w_in.shape[0]
    geo = _Geo(bsz, seq)
    tab = _tables(geo)

    big_names = [n for n, _ in BIG]
    sh_names = big_names + [n for n, _ in SHARDED_F32]
    kinds = dict(BIG + SHARDED_F32, w_in="row")
    shard3 = lambda a: a.reshape((1,) + a.shape) if a.ndim == 2 else a
    wire = {n: (MXU_DTYPE if n in big_names else F32) for n in sh_names}
    cast = {n: shard3(wts[n]).astype(wire[n]) for n in sh_names}
    per_layer = [n for n in sh_names if n != "meta_tokens"]
    small_names = ["norm_mix_w", "conv_b", "dt_bias", "a_log", "d_skip", "ssm_norm_w", "q_norm_w", "kv_norm_w",
                   "norm_mlp_w"]

    def gather_items(pairs):
        ins, outs, items, forms = [], [], [], []
        for n, i in pairs:
            a, b = cast[n].shape[1:]
            shape, dst, form = _gather_plan(a, b, kinds[n])
            items.append((len(ins), len(outs), (lambda ref, p, i=i: ref.at[i]), dst))
            ins.append(cast[n])
            outs.append(jax.ShapeDtypeStruct(shape, wire[n]))
            forms.append(form)
        return ins, outs, items, forms

    def whole_weights(pairs, forms, got):
        by_layer = {}
        for (n, i), form, g in zip(pairs, forms, got):
            if n == "w_in":
                n, g = "w_in_pt", _w_in_assemble(geo, g)
            elif form == "row":
                g = g.reshape(g.shape[0] * g.shape[1], g.shape[2])
            elif form == "stack":
                g = _unshard(g, "col")
            by_layer.setdefault(i, {})[n] = g
        return by_layer

    def prep(i, whole, token=None):
        wl = dict(whole)
        wl.update({n: wts[n][i] for n in small_names})
        if token is not None:
            wl["norm_mix_w"] = wl["norm_mix_w"] + token[0, 0]
        return _prep_layer(geo, wl)

    early = ("w_in", "conv_w")
    late_names = [n for n in per_layer if n not in early]
    pairs1 = [(n, i) for i in range(1, depth) for n in per_layer]
    groups = [[(n, 0) for n in early] + [("meta_tokens", 0)], [(n, 0) for n in late_names]] + ([pairs1] if pairs1 else [])
    started = {}

    def gather_start(gi, dep=None):
        ins, outs, items, forms = gather_items(groups[gi])
        sems, thru, landing, token = _exchange_start("gather_w%d_start" % gi, ins, outs, items, dep)
        started[gi] = (groups[gi], forms, sems, thru, landing, items)
        return token

    def gathered(gi, after):
        pairs, forms, sems, thru, landing, items = started[gi]
        return whole_weights(pairs, forms, _exchange_wait("gather_w%d_wait" % gi, sems, thru, landing, items, after))

    def late0(after):
        whole = gathered(1, after)[0]
        if pairs1:
            whole["q_norm_w"] = wts["q_norm_w"][0] + gather_start(2, whole["w_out"])[0, 0]
        return _prep_layer(geo, whole)

    token = gather_start(1, gather_start(0))
    whole0 = gathered(0, token)[0]
    meta_full = whole0.pop("meta_tokens")

    meta = jnp.broadcast_to(meta_full[None], (bsz, N_META, D_MODEL))
    h = jnp.concatenate([jnp.zeros((bsz, geo.pad, D_MODEL), F32), meta, x], axis=1).reshape(geo.nrows, D_MODEL)
    target = jnp.concatenate([jnp.zeros((bsz, geo.pad + N_META, D_MODEL), F32), loss_target], axis=1)
    target = target.reshape(geo.nrows, D_MODEL)
    layers, saved = [], []
    for i in range(depth):
        if i == 0:
            w, late = prep(0, whole0, token), late0
        else:
            if i == 1:
                whole1 = gathered(2, h)
            w, late = prep(i, whole1[i]), None
        h, s, w = _layer_fwd(geo, h, w, tab, late)
        layers.append(w)
        saved.append(s)
    dh, loss_part, g_final = _loss_bwd(geo, h, final_norm_w.reshape(1, -1), target, tab)

    def scatter_items(pairs):
        ins, outs, items = [], [], []
        for n, i in pairs:
            a, b = cast[n].shape[1:]
            arr = g_meta if n == "meta_tokens" else grads[i]["w_in_pt" if n == "w_in" else n]
            if n == "w_in":
                arr, src = _w_in_split(geo, arr, a), _entry
            elif kinds[n] == "row":
                src = lambda ref, p, a=a: ref.at[pl.ds(pl.multiple_of(p * a, a), a)]
            elif b % LANE == 0:
                src = lambda ref, p, b=b: ref.at[:, pl.ds(pl.multiple_of(p * b, b), b)]
            else:
                arr, src = _shard(arr, "col"), _entry
            items.append((len(ins), len(outs), src, _entry))
            ins.append(arr.astype(wire[n]))
            outs.append(jax.ShapeDtypeStruct((N_DEV, a, b), wire[n]))
        return ins, outs, items

    grads = [None] * depth
    landed, pending, res = {}, {}, {}

    def scatter_start(name, pairs):
        ins, outs, items = scatter_items(pairs)
        sems, thru, landing, token = _exchange_start(name + "_start", ins, outs, items)
        pending[name] = (pairs, sems, thru, landing, items)
        return token

    def scatter_wait(name, after):
        pairs, sems, thru, landing, items = pending[name]
        landed.update(zip(pairs, _exchange_wait(name + "_wait", sems, thru, landing, items, after)))

    def adam(n):
        parts = [landed[(n, i)] for i in range(cast[n].shape[0])]
        r = _adamw_nat("adamw_" + n, parts, shard3(wts[n]), shard3(mom[n]), shard3(var[n]))
        res[n] = [a.reshape(wts[n].shape) for a in r]

    def mid0(g):
        grads[0] = _unprep_grads(geo, g)
        return scatter_start("scatter_gb0", [(n, 0) for n in late_names])

    def tail0(g):
        grads[0] = _unprep_grads(geo, g)
        return scatter_start("scatter_ga0", [(n, 0) for n in early])

    dep = None
    for i in reversed(range(depth)):
        dh, gl = _layer_bwd(geo, dh, saved[i], layers[i], tab, *((mid0, tail0) if i == 0 else (None, None)), dep)
        grads[i] = _unprep_grads(geo, gl)
        if i == 1:
            dep = scatter_start("scatter_g1", pairs1)
    dh = dh.reshape(bsz, geo.lp, D_MODEL)
    grad_x = dh[:, geo.pad + N_META:]
    g_meta = jnp.sum(dh[:, geo.pad:geo.pad + N_META], axis=0)
    if pairs1:
        scatter_wait("scatter_g1", g_meta)
    scatter_wait("scatter_gb0", g_meta)
    for n in late_names:
        adam(n)
    g_small = {n: jnp.stack([grads[i][n] for i in range(depth)]) for n in SMALL if n != "final_norm_w"}
    g_small["final_norm_w"] = g_final.reshape(-1)
    zero = jnp.zeros((1,), F32)
    pk = lambda d, last: _pack([d[n] for n in SMALL] + [last], F32, row_mult=8)
    packed = pk(g_small, loss_part[0, :1])
    ins, outs, items = scatter_items([("meta_tokens", 0)])
    parts, landed[("meta_tokens", 0)] = _exchange(
        "gather_g", [packed] + ins, [jax.ShapeDtypeStruct((N_DEV,) + packed.shape, F32)] + outs,
        [(0, 0, _whole, _entry)] + [(1, 1, items[0][2], items[0][3])])
    adam("meta_tokens")
    scatter_wait("scatter_ga0", res["meta_tokens"][1])
    for n in early:
        adam(n)
    res_sm = _adamw("adamw_small", parts, pk(wts, zero), pk(mom, zero), pk(var, zero))
    res_sm = [_unpack(r, [wts[n].shape for n in SMALL] + [(1,)]) for r in res_sm]
    loss = res_sm[0][-1][0]

    out = [loss, grad_x]
    for k in range(4):
        named = {n: res[n][k] for n in sh_names}
        named.update(zip(SMALL, res_sm[k]))
        out += [turn(n, named[n]) for n in WEIGHTS]
    return tuple(out)
```

```python
import functools

import numpy as np
import jax
import jax.numpy as jnp
from jax import lax
from jax.experimental import pallas as pl
from jax.experimental.pallas import tpu as pltpu

F32 = jnp.float32
MXU_DTYPE = jnp.bfloat16

D_MODEL = 1024
N_META = 16
EPS = 1e-6
SSM_D_INNER = 2048
SSM_HEAD_DIM = 64
SSM_GROUPS = 4
SSM_STATE = 128
SSM_CONV = 4
SSM_CHUNK = 128
MLA_HEADS = 8
MLA_Q_LORA = 512
MLA_KV_LORA = 256
MLA_NOPE = 128
MLA_ROPE = 64
MLA_V = 128
ROPE_THETA = 10000.0
D_FF = 4096
ADAM_LR = 0.001
ADAM_B1 = 0.9
ADAM_B2 = 0.999
ADAM_EPS = 1e-08
ADAM_WD = 0.01
ADAM_STEP = 10

N_DEV = 8
ATT_BLK = 256
LANE = 128
PACK_W = 1024
VMEM_LIMIT = 56 * 1024 * 1024
MESH_ID = pl.DeviceIdType.MESH

BIG = (("w_in", "col"), ("w_uq", "col"), ("w_ukv", "col"), ("w_branch_ssm", "row"), ("w_branch_mla", "row"),
       ("w_out", "row"), ("w_mlp_up", "col"), ("w_mlp_down", "row"))
SHARDED_F32 = (("conv_w", "col"), ("meta_tokens", "col"))
SMALL = ("norm_mix_w", "conv_b", "dt_bias", "a_log", "d_skip", "ssm_norm_w", "q_norm_w", "kv_norm_w",
         "norm_mlp_w", "final_norm_w")
WEIGHTS = ("meta_tokens", "norm_mix_w", "w_in", "conv_w", "conv_b", "dt_bias", "a_log", "d_skip", "ssm_norm_w",
           "q_norm_w", "kv_norm_w", "w_uq", "w_ukv", "w_branch_ssm", "w_branch_mla", "w_out", "norm_mlp_w",
           "w_mlp_up", "w_mlp_down", "final_norm_w")


def _cparams(sem=None):
    return pltpu.CompilerParams(dimension_semantics=sem, vmem_limit_bytes=VMEM_LIMIT)


def _pick(n, cands):
    for c in cands:
        if n % c == 0:
            return c
    return n


def _sigmoid(x):
    return 1.0 / (1.0 + jnp.exp(-x))


def _silu(x):
    return x * _sigmoid(x)


def _softplus(x):
    t = jnp.exp(-jnp.abs(x))
    return jnp.maximum(x, 0.0) + jnp.where(t < 0.01, t * (1.0 - t * (0.5 - t * (1.0 / 3.0))), jnp.log(1.0 + t))


def _rms(x, w):
    x = x.astype(F32)
    return x * lax.rsqrt(jnp.mean(x * x, axis=-1, keepdims=True) + EPS) * w


def _dot(a, b, ca, cb, precision=None):
    return lax.dot_general(a, b, (((ca,), (cb,)), ((), ())), preferred_element_type=F32, precision=precision)


def _mxdot(a, b, ca, cb):
    return _dot(a.astype(MXU_DTYPE), b.astype(MXU_DTYPE), ca, cb)


def _mm(name, a, b, *, ta=False, tb=False, add=None, out_dtype=F32, dep=None, epi=None, side=None):
    (kdim, m) = a.shape if ta else a.shape[::-1]
    (n, k2) = b.shape if tb else b.shape[::-1]
    assert kdim == k2, (name, a.shape, b.shape)
    tm = _pick(m, (1152, 1024, 768, 512, 384, 256, 128))
    tn = _pick(n, (1024, 512, 384, 256, 128))
    tk = _pick(kdim, (1152, 1024, 768, 512, 384, 256, 128))
    nk = kdim // tk
    a_spec = pl.BlockSpec((tk, tm), lambda i, j, k: (k, i)) if ta else pl.BlockSpec((tm, tk), lambda i, j, k: (i, k))
    b_spec = pl.BlockSpec((tn, tk), lambda i, j, k: (j, k)) if tb else pl.BlockSpec((tk, tn), lambda i, j, k: (k, j))
    o_spec = pl.BlockSpec((tm, tn), lambda i, j, k: (i, j))
    ca, cb = (0 if ta else 1), (1 if tb else 0)

    out_dtypes = [out_dtype] if epi is None else list(epi[1])
    n_out = len(out_dtypes)
    n_side = 0 if side is None else 1

    def body(*refs):
        a_ref, b_ref = refs[:2]
        o_refs, acc = refs[-1 - n_side - n_out:-1 - n_side], refs[-1]
        k = pl.program_id(2)

        @pl.when(k == 0)
        def _():
            acc[...] = jnp.zeros_like(acc)

        acc[...] += _mxdot(a_ref[...], b_ref[...], ca, cb)

        @pl.when(k == nk - 1)
        def _():
            r = acc[...]
            if epi is not None:
                res = epi[0](r, refs[2][...]) if add is not None else epi[0](r)
            else:
                res = (r + refs[2][...].astype(F32) if add is not None else r,)
            for o_ref, val in zip(o_refs, res):
                o_ref[...] = val.astype(o_ref.dtype)

        if side is not None:
            @pl.when(jnp.logical_and(k == nk - 1, pl.program_id(1) == side[0] // tn))
            def _():
                refs[-2][...] = acc[:, side[0] % tn:side[0] % tn + side[1]]

    in_specs, args = [a_spec, b_spec], [a, b]
    if add is not None:
        in_specs.append(o_spec)
        args.append(add)
    if dep is not None:
        in_specs.append(pl.BlockSpec((8, LANE), lambda i, j, k: (0, 0)))
        args.append(dep)
    out_specs = [o_spec] * n_out
    out_shape = [jax.ShapeDtypeStruct((m, n), dt) for dt in out_dtypes]
    if side is not None:
        assert side[0] % tn + side[1] <= tn
        out_specs.append(pl.BlockSpec((tm, side[1]), lambda i, j, k: (i, 0)))
        out_shape.append(jax.ShapeDtypeStruct((m, side[1]), F32))
    res = pl.pallas_call(
        body, name=name, grid=(m // tm, n // tn, nk), in_specs=in_specs, out_specs=out_specs, out_shape=out_shape,
        scratch_shapes=[pltpu.VMEM((tm, tn), F32)],
        compiler_params=_cparams(("parallel", "arbitrary" if side is not None else "parallel", "arbitrary")))(*args)
    return res[0] if epi is None and side is None else res


def _rowwise(name, fn, *, nrows, tr, ncb=1, rows=(), fixed=(), vecs=(), tabs=(), outs=(), reds=(), tab_blocks=1):
    in_specs, args = [], []
    for arr, w, c0 in rows:
        in_specs.append(pl.BlockSpec((tr, w), lambda g, i, c0=c0: (i, c0 + g)))
        args.append(arr)
    for arr, w, c0 in fixed:
        in_specs.append(pl.BlockSpec((tr, w), lambda g, i, c0=c0: (i, c0)))
        args.append(arr)
    for arr, w, c0 in vecs:
        in_specs.append(pl.BlockSpec((1, w), lambda g, i, c0=c0: (0, c0 + g)))
        args.append(arr)
    for arr, w, c0 in tabs:
        in_specs.append(pl.BlockSpec((tr, w), lambda g, i, c0=c0: (i % tab_blocks, c0)))
        args.append(arr)
    n_in, n_out = len(args), len(outs)
    out_shape, out_specs, aliases = [], [], {}
    for k, o in enumerate(outs):
        c0 = o[3] if len(o) > 3 else 0
        out_shape.append(jax.ShapeDtypeStruct((nrows, o[0]), o[2]))
        out_specs.append(pl.BlockSpec((tr, o[1]), lambda g, i, c0=c0: (i, c0 + g)))
        if len(o) > 4:
            aliases[len(args)] = k
            in_specs.append(pl.BlockSpec(memory_space=pl.ANY))
            args.append(o[4])
    out_shape += [jax.ShapeDtypeStruct((1, wt), F32) for wt, w in reds]
    out_specs += [pl.BlockSpec((1, w), lambda g, i: (0, g)) for wt, w in reds]
    first_out = len(args)

    def body(*refs):
        res = fn(*[r[...] for r in refs[:n_in]])
        for o_ref, val in zip(refs[first_out:first_out + n_out], res[:n_out]):
            o_ref[...] = val.astype(o_ref.dtype)
        i = pl.program_id(1)
        for d_ref, val in zip(refs[first_out + n_out:], res[n_out:]):
            @pl.when(i == 0)
            def _(d_ref=d_ref, val=val):
                d_ref[...] = val

            @pl.when(i > 0)
            def _(d_ref=d_ref, val=val):
                d_ref[...] += val

    return pl.pallas_call(
        body, name=name, grid=(ncb, nrows // tr), in_specs=in_specs, out_specs=out_specs, out_shape=out_shape,
        input_output_aliases=aliases, compiler_params=_cparams(("parallel", "arbitrary")))(*args)


def _peer(k):
    x, y, c = lax.axis_index("x"), lax.axis_index("y"), lax.axis_index("c")
    px = jnp.where((k >> 2) & 1, 1 - x, x)
    py = jnp.where((k >> 1) & 1, 1 - y, y)
    pc = jnp.where(k & 1, 1 - c, c)
    return (px, py, pc), 4 * px + 2 * py + pc


def _my_index():
    return 4 * lax.axis_index("x") + 2 * lax.axis_index("y") + lax.axis_index("c")


def _exchange(name, ins, out_shapes, items):
    n_in, n_out, n_it = len(ins), len(out_shapes), len(items)

    def body(*refs):
        x, o = refs[:n_in], refs[n_in:n_in + n_out]
        send_sems, recv_sems, local_sems = refs[n_in + n_out:]
        me = _my_index()
        local, sends = [], []
        for t, (ii, io, src, dst) in enumerate(items):
            cp = pltpu.make_async_copy(src(x[ii], me), dst(o[io], me), local_sems.at[t])
            cp.start()
            local.append(cp)
        for k in range(1, N_DEV):
            dev, idx = _peer(k)
            for t, (ii, io, src, dst) in enumerate(items):
                s = (k - 1) * n_it + t
                cp = pltpu.make_async_remote_copy(
                    src_ref=src(x[ii], idx), dst_ref=dst(o[io], me), send_sem=send_sems.at[s],
                    recv_sem=recv_sems.at[s], device_id=dev, device_id_type=MESH_ID)
                cp.start()
                sends.append(cp)
        for k in range(1, N_DEV):
            dev, idx = _peer(k)
            for t, (ii, io, src, dst) in enumerate(items):
                s = (k - 1) * n_it + t
                pltpu.make_async_remote_copy(
                    src_ref=src(x[ii], idx), dst_ref=dst(o[io], idx), send_sem=send_sems.at[s],
                    recv_sem=recv_sems.at[s], device_id=dev, device_id_type=MESH_ID).wait_recv()
        for cp in sends:
            cp.wait_send()
        for cp in local:
            cp.wait()

    nsem = (N_DEV - 1) * n_it
    anyspec = pl.BlockSpec(memory_space=pl.ANY)
    return pl.pallas_call(
        body, name=name, out_shape=list(out_shapes), in_specs=[anyspec] * n_in, out_specs=[anyspec] * n_out,
        scratch_shapes=[pltpu.SemaphoreType.DMA((nsem,)), pltpu.SemaphoreType.DMA((nsem,)),
                        pltpu.SemaphoreType.DMA((n_it,))],
        compiler_params=pltpu.CompilerParams(has_side_effects=True))(*ins)


def _split_copies(x, land, send_sems, recv_sems, items, receive):
    me = _my_index()
    remote, n_it = [], len(items)
    for k in range(1, N_DEV):
        dev, idx = _peer(k)
        for t, (ii, io, src, dst) in enumerate(items):
            s = (k - 1) * n_it + t
            remote.append(pltpu.make_async_remote_copy(
                src_ref=src(x[ii], idx), dst_ref=dst(land[io], idx if receive else me), send_sem=send_sems.at[s],
                recv_sem=recv_sems.at[s], device_id=dev, device_id_type=MESH_ID))
    local = [pltpu.make_async_copy(src(x[ii], me), dst(land[io], me), send_sems.at[(N_DEV - 1) * n_it + t])
             for t, (ii, io, src, dst) in enumerate(items)]
    return remote, local


def _exchange_start(name, ins, out_shapes, items, dep=None):
    n_in, n_out, n_it = len(ins), len(out_shapes), len(items)

    def body(*refs):
        x, land = refs[:n_in], refs[n_in:n_in + n_out]
        first_out = n_in + n_out + (dep is not None)
        send_sems, recv_sems, token = refs[first_out], refs[first_out + 1], refs[-1]
        remote, local = _split_copies(x, land, send_sems, recv_sems, items, False)
        for cp in remote + local:
            cp.start()
        token[...] = jnp.zeros_like(token)

    hbm = pl.BlockSpec(memory_space=pltpu.HBM)
    sem = pl.BlockSpec(memory_space=pltpu.SEMAPHORE)
    arrs = [pltpu.with_memory_space_constraint(a, pltpu.HBM)
            for a in list(ins) + [lax.empty(s.shape, s.dtype) for s in out_shapes]]
    res = pl.pallas_call(
        body, name=name,
        out_shape=(pltpu.SemaphoreType.DMA((N_DEV * n_it,)), pltpu.SemaphoreType.DMA(((N_DEV - 1) * n_it,)),
                   *[pltpu.HBM(a.shape, a.dtype) for a in arrs], jax.ShapeDtypeStruct((8, LANE), F32)),
        in_specs=[hbm] * (n_in + n_out) + ([] if dep is None else [pl.BlockSpec(memory_space=pl.ANY)]),
        out_specs=(sem, sem, *[hbm] * (n_in + n_out), pl.BlockSpec(memory_space=pltpu.VMEM)),
        input_output_aliases={i: 2 + i for i in range(n_in + n_out)},
        compiler_params=pltpu.CompilerParams(has_side_effects=pltpu.SideEffectType.DATAFLOW_SIDE_EFFECTING))(
            *arrs, *([] if dep is None else [dep]))
    return res[:2], res[2:2 + n_in], res[2 + n_in:2 + n_in + n_out], res[-1]


def _exchange_wait(name, sems, ins, landing, items, after):
    n_in, n_out = len(ins), len(landing)

    def body(*refs):
        x, land = refs[:n_in], refs[n_in:n_in + n_out]
        send_sems, recv_sems = refs[n_in + n_out], refs[n_in + n_out + 1]
        remote, local = _split_copies(x, land, send_sems, recv_sems, items, True)
        for cp in remote:
            cp.wait_send()
            cp.wait_recv()
        for cp in local:
            cp.wait()

    hbm = pl.BlockSpec(memory_space=pltpu.HBM)
    sem = pl.BlockSpec(memory_space=pltpu.SEMAPHORE)
    arrs = list(ins) + list(landing)
    res = pl.pallas_call(
        body, name=name, out_shape=tuple(pltpu.HBM(a.shape, a.dtype) for a in arrs),
        in_specs=[hbm] * (n_in + n_out) + [sem, sem, pl.BlockSpec(memory_space=pl.ANY)],
        out_specs=tuple([hbm] * (n_in + n_out)), input_output_aliases={i: i for i in range(n_in + n_out)},
        compiler_params=pltpu.CompilerParams(has_side_effects=pltpu.SideEffectType.DATAFLOW_SIDE_EFFECTING))(
            *arrs, *sems, after)
    return res[n_in:]


def _whole(ref, p):
    return ref


def _entry(ref, p):
    return ref.at[p]


def _gather_plan(a, b, kind):
    if kind == "col" and b % LANE == 0:
        return (a, N_DEV * b), (lambda ref, p: ref.at[:, pl.ds(pl.multiple_of(p * b, b), b)]), "col"
    return (N_DEV, a, b), _entry, ("row" if kind == "row" else "stack")


def _adamw_nat(name, parts, w, m, v):
    depth, b, c = w.shape
    assert len(parts) == depth
    tb = _pick(b, (128, 64, 32, 16, 8))
    if tb == b and b > 256:
        tb = 256
    spec = pl.BlockSpec((1, tb, c), lambda i, j: (i, j, 0))

    def body(*refs):
        p_refs = refs[:depth]
        w_ref, m_ref, v_ref, g_ref, d_ref, nm_ref, nv_ref = refs[depth:]
        for layer, p_ref in enumerate(p_refs):
            @pl.when(pl.program_id(0) == layer)
            def _(p_ref=p_ref):
                g = p_ref[0].astype(F32)
                for j in range(1, N_DEV):
                    g = g + p_ref[j].astype(F32)
                nm = ADAM_B1 * m_ref[0] + (1.0 - ADAM_B1) * g
                nv = ADAM_B2 * v_ref[0] + (1.0 - ADAM_B2) * jnp.square(g)
                m_hat = nm / (1.0 - ADAM_B1 ** ADAM_STEP)
                v_hat = nv / (1.0 - ADAM_B2 ** ADAM_STEP)
                g_ref[0] = g
                d_ref[0] = -ADAM_LR * (m_hat / (jnp.sqrt(v_hat) + ADAM_EPS) + ADAM_WD * w_ref[0])
                nm_ref[0] = nm
                nv_ref[0] = nv

    sds = jax.ShapeDtypeStruct((depth, b, c), F32)
    return pl.pallas_call(
        body, name=name, grid=(depth, pl.cdiv(b, tb)),
        in_specs=[pl.BlockSpec((N_DEV, tb, c), lambda i, j: (0, j, 0))] * depth + [spec, spec, spec],
        out_specs=[spec] * 4, out_shape=[sds] * 4, compiler_params=_cparams(("parallel", "parallel")))(*parts, w, m, v)


def _adamw(name, parts, w, m, v):
    rows = w.shape[0]
    tr = _pick(rows, (256, 128, 64, 32, 16, 8))
    spec = pl.BlockSpec((tr, PACK_W), lambda i: (i, 0))

    def body(p_ref, w_ref, m_ref, v_ref, g_ref, d_ref, nm_ref, nv_ref):
        g = p_ref[0]
        for j in range(1, N_DEV):
            g = g + p_ref[j]
        nm = ADAM_B1 * m_ref[...] + (1.0 - ADAM_B1) * g
        nv = ADAM_B2 * v_ref[...] + (1.0 - ADAM_B2) * jnp.square(g)
        m_hat = nm / (1.0 - ADAM_B1 ** ADAM_STEP)
        v_hat = nv / (1.0 - ADAM_B2 ** ADAM_STEP)
        g_ref[...] = g
        d_ref[...] = -ADAM_LR * (m_hat / (jnp.sqrt(v_hat) + ADAM_EPS) + ADAM_WD * w_ref[...])
        nm_ref[...] = nm
        nv_ref[...] = nv

    sds = jax.ShapeDtypeStruct((rows, PACK_W), F32)
    return pl.pallas_call(
        body, name=name, grid=(rows // tr,),
        in_specs=[pl.BlockSpec((N_DEV, tr, PACK_W), lambda i: (0, i, 0)), spec, spec, spec],
        out_specs=[spec] * 4, out_shape=[sds] * 4, compiler_params=_cparams(("parallel",)))(parts, w, m, v)


def _pack(arrs, dtype, row_mult=16):
    flat = jnp.concatenate([a.reshape(-1).astype(dtype) for a in arrs])
    unit = row_mult * PACK_W
    total = -(-flat.shape[0] // unit) * unit
    flat = jnp.pad(flat, (0, total - flat.shape[0]))
    return flat.reshape(-1, PACK_W)


def _pack_lead(arrs, dtype, row_mult):
    flat = jnp.concatenate([a.reshape(N_DEV, -1).astype(dtype) for a in arrs], axis=1)
    unit = row_mult * PACK_W
    total = -(-flat.shape[1] // unit) * unit
    flat = jnp.pad(flat, ((0, 0), (0, total - flat.shape[1])))
    return flat.reshape(N_DEV, -1, PACK_W)


def _unpack(buf, shapes, lead=()):
    flat = buf.reshape(lead + (-1,))
    out, off = [], 0
    for s in shapes:
        n = int(np.prod(s))
        out.append(flat[..., off:off + n].reshape(lead + tuple(s)))
        off += n
    return out


def _unshard(g, kind):
    if kind == "col":
        g = jnp.moveaxis(g, 0, -2)
        return g.reshape(g.shape[:-2] + (g.shape[-2] * g.shape[-1],))
    g = jnp.moveaxis(g, 0, 1)
    return g.reshape((g.shape[0], g.shape[1] * g.shape[2]) + g.shape[3:])


def _shard(full, kind):
    if kind == "col":
        s = full.reshape(full.shape[:-1] + (N_DEV, full.shape[-1] // N_DEV))
        return jnp.moveaxis(s, -2, 0)
    s = full.reshape((full.shape[0], N_DEV, full.shape[1] // N_DEV) + full.shape[2:])
    return jnp.moveaxis(s, 1, 0)


class _Geo:
    def __init__(self, bsz, seq):
        self.bsz, self.seq = bsz, seq
        self.pad = (-(N_META + seq)) % ATT_BLK
        self.lp = self.pad + N_META + seq
        assert (self.pad + N_META) % SSM_CHUNK == 0 and self.lp % SSM_CHUNK == 0
        self.nrows = bsz * self.lp
        self.nc = self.lp // SSM_CHUNK
        self.nh = SSM_D_INNER // SSM_HEAD_DIM
        self.gn = SSM_GROUPS * SSM_STATE
        self.cd = SSM_D_INNER + 2 * self.gn
        self.hq = MLA_HEADS * LANE
        order = (("z", SSM_D_INNER), ("g_ssm", D_MODEL), ("g_mla", D_MODEL), ("xs", SSM_D_INNER), ("bm", self.gn),
                 ("cm", self.gn), ("c_q", MLA_Q_LORA), ("c_kv", MLA_KV_LORA), ("dt", LANE), ("k_rope", LANE))
        self.col, off = {}, 0
        for nm, w in order:
            assert off % w == 0, (nm, off, w)
            self.col[nm] = (off, w)
            off += w
        self.pw = off
        assert self.nh <= LANE and MLA_ROPE == 64 and MLA_NOPE == LANE and MLA_V == LANE
        self.tr = _pick(self.lp, (768, 512, 384, 256, 128))
        self.tr_wide = _pick(self.lp, (384, 256, 128))

    def cb(self, nm):
        off, w = self.col[nm]
        return off // w

    def w_in_runs(self, shard_w):
        nh, half = self.nh, MLA_ROPE // 2
        src, pieces = 0, []
        for nm, n in (("z", SSM_D_INNER), ("xs", SSM_D_INNER), ("bm", self.gn), ("cm", self.gn), ("dt", nh),
                      ("c_q", MLA_Q_LORA), ("c_kv", MLA_KV_LORA), ("k_rope", MLA_ROPE), ("g_ssm", D_MODEL),
                      ("g_mla", D_MODEL)):
            dst = self.col[nm][0]
            if nm == "k_rope":
                pieces += [(src, half, dst), (src + half, half, dst + 2 * half)]
            else:
                pieces.append((src, n, dst))
            src += n
        assert src == shard_w * N_DEV
        runs = []
        for a, n, dst in pieces:
            for j in range(N_DEV):
                lo, hi = max(a, j * shard_w), min(a + n, (j + 1) * shard_w)
                if lo < hi:
                    runs.append((j, lo - j * shard_w, hi - lo, dst + lo - a))
        return runs


def _slot(a):
    h = MLA_ROPE // 2
    z = jnp.zeros(a.shape[:-1] + (h,), a.dtype)
    return jnp.concatenate([a[..., :h], z, a[..., h:], z], axis=-1)


def _unslot(a):
    h = MLA_ROPE // 2
    return jnp.concatenate([a[..., :h], a[..., 2 * h:3 * h]], axis=-1)


def _prep_layer(geo, wl):
    nh = geo.nh
    p = {}
    if "w_uq" in wl:
        uq = wl["w_uq"].reshape(MLA_Q_LORA, MLA_HEADS, MLA_NOPE + MLA_ROPE)
        p["w_qn"] = uq[..., :MLA_NOPE].reshape(MLA_Q_LORA, geo.hq)
        p["w_qp"] = _slot(uq[..., MLA_NOPE:]).reshape(MLA_Q_LORA, geo.hq)
    if "w_ukv" in wl:
        ukv = wl["w_ukv"].reshape(MLA_KV_LORA, MLA_HEADS, MLA_NOPE + MLA_V)
        p["w_k"] = ukv[..., :MLA_NOPE].reshape(MLA_KV_LORA, geo.hq)
        p["w_v"] = ukv[..., MLA_NOPE:].reshape(MLA_KV_LORA, geo.hq)
    for nm in ("w_in_pt", "conv_w", "w_branch_ssm", "w_branch_mla", "w_out", "w_mlp_up", "w_mlp_down"):
        if nm in wl:
            p[nm] = wl[nm]
    for nm in ("norm_mix_w", "conv_b", "ssm_norm_w", "q_norm_w", "kv_norm_w", "norm_mlp_w"):
        if nm in wl:
            p[nm] = wl[nm].reshape(1, -1)
    if "dt_bias" in wl:
        p["dt_bias"] = jnp.pad(wl["dt_bias"], (0, LANE - nh)).reshape(1, LANE)
        p["a_log"] = jnp.pad(wl["a_log"], (0, LANE - nh)).reshape(1, LANE)
        p["d_skip_full"] = jnp.repeat(wl["d_skip"], SSM_HEAD_DIM).reshape(1, SSM_D_INNER)
    return p


def _unprep_grads(geo, g):
    nh = geo.nh
    out = {}
    if "w_qn" in g:
        qn = g["w_qn"].reshape(MLA_Q_LORA, MLA_HEADS, MLA_NOPE)
        qp = _unslot(g["w_qp"].reshape(MLA_Q_LORA, MLA_HEADS, LANE))
        out["w_uq"] = jnp.concatenate([qn, qp], axis=-1).reshape(MLA_Q_LORA, -1)
    if "w_k" in g:
        wk = g["w_k"].reshape(MLA_KV_LORA, MLA_HEADS, MLA_NOPE)
        wv = g["w_v"].reshape(MLA_KV_LORA, MLA_HEADS, MLA_V)
        out["w_ukv"] = jnp.concatenate([wk, wv], axis=-1).reshape(MLA_KV_LORA, -1)
    for nm in ("w_in_pt", "w_branch_ssm", "w_branch_mla", "w_out", "w_mlp_up", "w_mlp_down", "conv_w"):
        if nm in g:
            out[nm] = g[nm]
    for nm in ("norm_mix_w", "conv_b", "ssm_norm_w", "q_norm_w", "kv_norm_w", "norm_mlp_w"):
        if nm in g:
            out[nm] = g[nm].reshape(-1)
    if "dt_bias" in g:
        out["dt_bias"] = g["dt_bias"].reshape(-1)[:nh]
        out["a_log"] = g["a_log"].reshape(-1)[:nh]
        out["d_skip"] = g["d_skip_full"].reshape(nh, SSM_HEAD_DIM).sum(-1)
    return out


def _tables(geo):
    pos = jnp.arange(geo.lp, dtype=F32) - geo.pad
    inv = ROPE_THETA ** (-jnp.arange(0, MLA_ROPE, 2, dtype=F32) / MLA_ROPE)
    ang = pos[:, None] * inv[None, :]
    cos, sin = jnp.cos(ang), jnp.sin(ang)
    z = jnp.zeros_like(cos)
    rows = jnp.arange(geo.lp)[:, None]
    return {"cos": jnp.concatenate([cos, z, cos, z], axis=-1), "sin": jnp.concatenate([-sin, z, sin, z], axis=-1),
            "valid": (rows >= geo.pad).astype(F32), "token": (rows >= geo.pad + N_META).astype(F32)}


def _w_in_assemble(geo, gathered):
    _, sw, d = gathered.shape
    runs = geo.w_in_runs(sw)
    tl = _pick(d, (256, 128))

    def body(x_ref, o_ref):
        o_ref[...] = jnp.zeros_like(o_ref)
        for j, s0, n, d0 in runs:
            o_ref[d0:d0 + n, :] = x_ref[j, s0:s0 + n, :]

    return pl.pallas_call(
        body, name="w_in_assemble", grid=(d // tl,), in_specs=[pl.BlockSpec((N_DEV, sw, tl), lambda i: (0, 0, i))],
        out_specs=pl.BlockSpec((geo.pw, tl), lambda i: (0, i)),
        out_shape=jax.ShapeDtypeStruct((geo.pw, d), gathered.dtype), compiler_params=_cparams(("parallel",)))(gathered)


def _w_in_split(geo, g_padded, sw):
    d = g_padded.shape[1]
    runs = geo.w_in_runs(sw)
    tl = _pick(d, (256, 128))

    def body(x_ref, o_ref):
        for j, s0, n, d0 in runs:
            o_ref[j, s0:s0 + n, :] = x_ref[d0:d0 + n, :]

    return pl.pallas_call(
        body, name="w_in_split", grid=(d // tl,), in_specs=[pl.BlockSpec((geo.pw, tl), lambda i: (0, i))],
        out_specs=pl.BlockSpec((N_DEV, sw, tl), lambda i: (0, 0, i)),
        out_shape=jax.ShapeDtypeStruct((N_DEV, sw, d), g_padded.dtype),
        compiler_params=_cparams(("parallel",)))(g_padded)


def _conv_cols(geo, cbw):
    x0 = geo.col["xs"][0]
    assert geo.col["bm"][0] == x0 + SSM_D_INNER and geo.col["cm"][0] == geo.col["bm"][0] + geo.gn and x0 % cbw == 0
    return lambda j: x0 // cbw + j


def _conv_taps(x):
    return [pltpu.roll(x, SSM_CONV - 1 - k, axis=0) for k in range(SSM_CONV - 1)] + [x]


def _conv_pre(x, w_ref, b_ref, taps=None):
    taps = _conv_taps(x) if taps is None else taps
    acc = b_ref[...]
    for k in range(SSM_CONV):
        acc = acc + taps[k] * w_ref[k:k + 1, :]
    return acc


def _conv_fwd(geo, proj, conv_w, conv_b):
    cbw = 256
    colmap = _conv_cols(geo, cbw)
    lp, pad = geo.lp, geo.pad

    def body(x_ref, w_ref, b_ref, o_ref):
        valid = (lax.broadcasted_iota(jnp.int32, (lp, 1), 0) >= pad).astype(F32)
        o_ref[...] = (_silu(_conv_pre(x_ref[...].astype(F32), w_ref, b_ref)) * valid).astype(o_ref.dtype)

    return pl.pallas_call(
        body, name="conv_fwd", grid=(geo.bsz, geo.cd // cbw),
        in_specs=[pl.BlockSpec((lp, cbw), lambda b, j: (b, colmap(j))),
                  pl.BlockSpec((SSM_CONV, cbw), lambda b, j: (0, j)), pl.BlockSpec((1, cbw), lambda b, j: (0, j))],
        out_specs=pl.BlockSpec((lp, cbw), lambda b, j: (b, j)),
        out_shape=jax.ShapeDtypeStruct((geo.nrows, geo.cd), MXU_DTYPE),
        compiler_params=_cparams(("parallel", "parallel")))(proj, conv_w, conv_b)


def _conv_bwd(geo, proj, conv_w, conv_b, dxc, dproj):
    cbw = 256
    colmap = _conv_cols(geo, cbw)
    lp, pad = geo.lp, geo.pad

    def body(x_ref, w_ref, b_ref, dy_ref, _, dx_ref, gw_ref, gb_ref):
        b = pl.program_id(1)
        valid = (lax.broadcasted_iota(jnp.int32, (lp, 1), 0) >= pad).astype(F32)
        taps = _conv_taps(x_ref[...].astype(F32))
        pre = _conv_pre(None, w_ref, b_ref, taps)
        sig = _sigmoid(pre)
        dpre = dy_ref[...] * (sig * (1.0 + pre * (1.0 - sig))) * valid
        dx = dpre * w_ref[SSM_CONV - 1:SSM_CONV, :]
        for k in range(SSM_CONV - 1):
            dx = dx + pltpu.roll(dpre, lp - (SSM_CONV - 1 - k), axis=0) * w_ref[k:k + 1, :]
        gws = [jnp.sum(dpre * taps[k], axis=0, keepdims=True) for k in range(SSM_CONV)]
        dx_ref[...] = (dx * valid).astype(dx_ref.dtype)

        @pl.when(b == 0)
        def _():
            gw_ref[...] = jnp.zeros_like(gw_ref)
            gb_ref[...] = jnp.zeros_like(gb_ref)

        for k in range(SSM_CONV):
            gw_ref[k:k + 1, :] += gws[k]
        gb_ref[...] += jnp.sum(dpre, axis=0, keepdims=True)

    return pl.pallas_call(
        body, name="conv_bwd", grid=(geo.cd // cbw, geo.bsz),
        in_specs=[pl.BlockSpec((lp, cbw), lambda j, b: (b, colmap(j))),
                  pl.BlockSpec((SSM_CONV, cbw), lambda j, b: (0, j)), pl.BlockSpec((1, cbw), lambda j, b: (0, j)),
                  pl.BlockSpec((lp, cbw), lambda j, b: (b, j)), pl.BlockSpec(memory_space=pl.ANY)],
        out_specs=[pl.BlockSpec((lp, cbw), lambda j, b: (b, colmap(j))),
                   pl.BlockSpec((SSM_CONV, cbw), lambda j, b: (0, j)), pl.BlockSpec((1, cbw), lambda j, b: (0, j))],
        out_shape=[jax.ShapeDtypeStruct(dproj.shape, dproj.dtype),
                   jax.ShapeDtypeStruct((SSM_CONV, geo.cd), F32), jax.ShapeDtypeStruct((1, geo.cd), F32)],
        input_output_aliases={4: 0},
        compiler_params=_cparams(("parallel", "arbitrary")))(proj, conv_w, conv_b, dxc, dproj)


def _tri(q):
    r = lax.broadcasted_iota(jnp.int32, (q, q), 0)
    c = lax.broadcasted_iota(jnp.int32, (q, q), 1)
    return r >= c


def _ssd_pre(dtr, dtb, alog, valid):
    dt = _softplus(dtr + dtb) * valid
    adt = dt * (-jnp.exp(alog))
    a_cs = _dot(_tri(SSM_CHUNK).astype(F32), adt, 1, 0, precision=lax.Precision.HIGHEST)
    return dt, a_cs


def _ssd_specs(geo, rev):
    nc, q = geo.nc, SSM_CHUNK
    ci = (lambda c: nc - 1 - c) if rev else (lambda c: c)
    nxb = SSM_D_INNER // geo.gn
    return [pl.BlockSpec((q, SSM_D_INNER), lambda b, c: (b * nc + ci(c), 0)),
            pl.BlockSpec((q, geo.gn), lambda b, c: (b * nc + ci(c), nxb)),
            pl.BlockSpec((q, geo.gn), lambda b, c: (b * nc + ci(c), nxb + 1)),
            pl.BlockSpec((q, LANE), lambda b, c: (b * nc + ci(c), 0)),
            pl.BlockSpec((1, LANE), lambda b, c: (0, 0)), pl.BlockSpec((1, LANE), lambda b, c: (0, 0))], ci


def _expand_heads(cols, nh):
    per = LANE // SSM_HEAD_DIM
    lane = lax.broadcasted_iota(jnp.int32, (1, LANE), 1)
    blocks = []
    for j in range(nh // per):
        blk = jnp.broadcast_to(cols[:, j * per:j * per + 1], (cols.shape[0], LANE))
        for k in range(1, per):
            blk = jnp.where(lane >= k * SSM_HEAD_DIM, cols[:, j * per + k:j * per + k + 1], blk)
        blocks.append(blk)
    return jnp.concatenate(blocks, axis=1)


def _head_maps(geo):
    e = (jnp.arange(SSM_D_INNER)[None, :] // SSM_HEAD_DIM == jnp.arange(LANE)[:, None]).astype(F32)
    return e, e.T


def _ssd_fwd_g(geo, xc, proj, dt_bias, a_log):
    q, p, n, e = SSM_CHUNK, SSM_HEAD_DIM, SSM_STATE, geo.nh // SSM_GROUPS
    nc, pad, gw = geo.nc, geo.pad, SSM_D_INNER // SSM_GROUPS
    in_specs, _ = _ssd_specs(geo, False)

    def body(xs_ref, b_ref, c_ref, dtr_ref, dtb_ref, alog_ref, y_ref, sp_ref, state, xdt_s, y_s):
        c = pl.program_id(1)

        @pl.when(c == 0)
        def _():
            state[...] = jnp.zeros_like(state)

        sp_ref[...] = state[...]
        inert = (c + 1) * q <= pad

        @pl.when(inert)
        def _():
            y_ref[...] = jnp.zeros_like(y_ref)

        @pl.when(jnp.logical_not(inert))
        def _():
            valid = (c * q + lax.broadcasted_iota(jnp.int32, (q, 1), 0) >= pad).astype(F32)
            dt, a_cs = _ssd_pre(dtr_ref[...], dtb_ref[...], alog_ref[...], valid)
            a_cst = a_cs.T
            dt_x, a_x = _expand_heads(dt, geo.nh), _expand_heads(a_cs, geo.nh)
            tri = _tri(q)
            for g in range(SSM_GROUPS):
                gs = slice(g * gw, (g + 1) * gw)
                bg, cg = b_ref[:, g * n:(g + 1) * n], c_ref[:, g * n:(g + 1) * n]
                a_g = a_x[:, gs]
                a_last = a_g[q - 1:q, :]
                xdt_g = xs_ref[:, gs] * dt_x[:, gs]
                xdt_s[:, gs] = xdt_g
                s_g = state[:, gs]
                y_s[:, gs] = _mxdot(cg, s_g, 1, 0) * jnp.exp(a_g)
                state[:, gs] = s_g * jnp.exp(a_last) + _mxdot(bg, xdt_g * jnp.exp(a_last - a_g), 0, 0)
                cb = _mxdot(cg, bg, 1, 1)
                for hh in range(e):
                    h = g * e + hh
                    hs = slice(h * p, (h + 1) * p)
                    ldec = jnp.exp(jnp.where(tri, a_cs[:, h:h + 1] - a_cst[h:h + 1, :], -jnp.inf))
                    y_s[:, hs] += _mxdot(cb * ldec, xdt_s[:, hs], 1, 0)
            y_ref[...] = y_s[...].astype(y_ref.dtype)

    return pl.pallas_call(
        body, name="ssd_fwd", grid=(geo.bsz, nc), in_specs=in_specs,
        out_specs=[pl.BlockSpec((q, SSM_D_INNER), lambda b, c: (b * nc + c, 0)),
                   pl.BlockSpec((n, SSM_D_INNER), lambda b, c: (b * nc + c, 0))],
        out_shape=[jax.ShapeDtypeStruct((geo.nrows, SSM_D_INNER), MXU_DTYPE),
                   jax.ShapeDtypeStruct((geo.bsz * nc * n, SSM_D_INNER), F32)],
        scratch_shapes=[pltpu.VMEM((n, SSM_D_INNER), F32), pltpu.VMEM((q, SSM_D_INNER), F32),
                        pltpu.VMEM((q, SSM_D_INNER), F32)],
        compiler_params=_cparams(("parallel", "arbitrary")))(xc, xc, xc, proj, dt_bias, a_log)


def _ssd_bwd_g(geo, xc, proj, dt_bias, a_log, s_prev_all, dy, dxs_skip, dproj):
    q, p, n, e = SSM_CHUNK, SSM_HEAD_DIM, SSM_STATE, geo.nh // SSM_GROUPS
    nc, pad, di, gn, gw = geo.nc, geo.pad, SSM_D_INNER, geo.gn, SSM_D_INNER // SSM_GROUPS
    in_specs, ci = _ssd_specs(geo, True)
    row_spec = pl.BlockSpec((q, di), lambda b, c: (b * nc + ci(c), 0))
    e_map, _ = _head_maps(geo)
    in_specs += [pl.BlockSpec((n, di), lambda b, c: (b * nc + ci(c), 0)), row_spec, row_spec,
                 pl.BlockSpec((LANE, di), lambda b, c: (0, 0)), pl.BlockSpec(memory_space=pl.ANY)]

    def body(xs_ref, b_ref, c_ref, dtr_ref, dtb_ref, alog_ref, sp_ref, dy_ref, dsk_ref, e_ref, _,
             dxc_ref, ddt_ref, gdtb_ref, galog_ref, dstate, xdt_s, dxdt_s):
        step = pl.program_id(1)
        first = jnp.logical_and(pl.program_id(0) == 0, step == 0)
        c = nc - 1 - step

        @pl.when(step == 0)
        def _():
            dstate[...] = jnp.zeros_like(dstate)

        @pl.when(first)
        def _():
            gdtb_ref[...] = jnp.zeros_like(gdtb_ref)
            galog_ref[...] = jnp.zeros_like(galog_ref)

        inert = (c + 1) * q <= pad

        @pl.when(inert)
        def _():
            dxc_ref[...] = jnp.zeros_like(dxc_ref)
            ddt_ref[...] = jnp.zeros_like(ddt_ref)

        @pl.when(jnp.logical_not(inert))
        def _():
            valid = (c * q + lax.broadcasted_iota(jnp.int32, (q, 1), 0) >= pad).astype(F32)
            dtr, dtb, alog = dtr_ref[...], dtb_ref[...], alog_ref[...]
            dt, a_cs = _ssd_pre(dtr, dtb, alog, valid)
            a_cst = a_cs.T
            dt_x, a_x = _expand_heads(dt, geo.nh), _expand_heads(a_cs, geo.nh)
            tri = _tri(q)
            lane = lax.broadcasted_iota(jnp.int32, (1, LANE), 1)
            sub = lax.broadcasted_iota(jnp.int32, (LANE, 1), 0)
            d_dt = jnp.zeros((q, LANE), F32)
            d_acs = jnp.zeros((q, LANE), F32)
            d_acst = jnp.zeros((LANE, q), F32)
            d_last = jnp.zeros((1, LANE), F32)
            for g in range(SSM_GROUPS):
                gs = slice(g * gw, (g + 1) * gw)
                bg, cg = b_ref[:, g * n:(g + 1) * n], c_ref[:, g * n:(g + 1) * n]
                seg = lambda v: _mxdot(v, e_ref[:, gs], 1, 1)
                a_g, dt_g, x_g, dy_g = a_x[:, gs], dt_x[:, gs], xs_ref[:, gs], dy_ref[:, gs]
                e_col, e_last, dec = jnp.exp(a_g), jnp.exp(a_g[q - 1:q, :]), jnp.exp(a_g[q - 1:q, :] - a_g)
                xdt_g = x_g * dt_g
                xdt_s[:, gs] = xdt_g
                s_g, ds_g = sp_ref[:, gs], dstate[:, gs]
                cs = _mxdot(cg, s_g, 1, 0)
                d_cs = dy_g * e_col
                d_acs = d_acs + seg(d_cs * cs)
                d_cg = _mxdot(d_cs, s_g, 1, 1)
                dstate[:, gs] = _mxdot(cg, d_cs, 0, 0) + ds_g * e_last
                dl_x = jnp.sum(ds_g * s_g, axis=0, keepdims=True) * e_last
                d_last = d_last + seg(jnp.broadcast_to(dl_x, (8, gw)))[:1]
                gmat = _mxdot(bg, ds_g, 1, 0)
                xd = xdt_g * dec
                d_bg = _mxdot(xd, ds_g, 1, 1)
                d_dec = seg(xd * gmat)
                d_acs = d_acs - d_dec
                d_last = d_last + jnp.sum(d_dec, axis=0, keepdims=True)
                dxdt_s[:, gs] = dec * gmat
                cb = _mxdot(cg, bg, 1, 1)
                d_cb = jnp.zeros((q, q), F32)
                for hh in range(e):
                    h = g * e + hh
                    hs = slice(h * p, (h + 1) * p)
                    ldec = jnp.exp(jnp.where(tri, a_cs[:, h:h + 1] - a_cst[h:h + 1, :], -jnp.inf))
                    dyh = dy_ref[:, hs]
                    d_m = _mxdot(dyh, xdt_s[:, hs], 1, 1)
                    dxdt_s[:, hs] += _mxdot(cb * ldec, dyh, 0, 0)
                    d_cb = d_cb + d_m * ldec
                    d_diff = d_m * cb * ldec
                    d_acs = d_acs + jnp.sum(d_diff, axis=1, keepdims=True) * (lane == h).astype(F32)
                    d_acst = d_acst - (sub == h).astype(F32) * jnp.sum(d_diff, axis=0, keepdims=True)
                d_xdt = dxdt_s[:, gs]
                dxc_ref[:, gs] = d_xdt * dt_g + dsk_ref[:, gs]
                d_dt = d_dt + seg(d_xdt * x_g)
                dxc_ref[:, di + g * n:di + (g + 1) * n] = d_bg + _mxdot(d_cb, cg, 0, 0)
                dxc_ref[:, di + gn + g * n:di + gn + (g + 1) * n] = d_cg + _mxdot(d_cb, bg, 1, 0)
            is_last = (lax.broadcasted_iota(jnp.int32, (q, 1), 0) == q - 1).astype(F32)
            d_acs = d_acs + d_acst.T + is_last * d_last
            d_adt = _dot(_tri(q).astype(F32), d_acs, 0, 0, precision=lax.Precision.HIGHEST)
            a = -jnp.exp(alog)
            d_dt = d_dt + d_adt * a
            d_dtr = d_dt * valid * _sigmoid(dtr + dtb)
            ddt_ref[...] = d_dtr.astype(ddt_ref.dtype)
            gdtb_ref[...] += jnp.sum(d_dtr, axis=0, keepdims=True)
            galog_ref[...] += jnp.sum(d_adt * dt, axis=0, keepdims=True) * a

    vec = pl.BlockSpec((1, LANE), lambda b, c: (0, 0))
    return pl.pallas_call(
        body, name="ssd_bwd", grid=(geo.bsz, nc), in_specs=in_specs,
        out_specs=[pl.BlockSpec((q, geo.cd), lambda b, c: (b * nc + ci(c), 0)),
                   pl.BlockSpec((q, LANE), lambda b, c: (b * nc + ci(c), geo.cb("dt"))), vec, vec],
        out_shape=[jax.ShapeDtypeStruct((geo.nrows, geo.cd), F32), jax.ShapeDtypeStruct(dproj.shape, dproj.dtype),
                   jax.ShapeDtypeStruct((1, LANE), F32), jax.ShapeDtypeStruct((1, LANE), F32)],
        scratch_shapes=[pltpu.VMEM((n, di), F32), pltpu.VMEM((q, di), F32), pltpu.VMEM((q, di), F32)],
        input_output_aliases={10: 1},
        compiler_params=_cparams(("arbitrary", "arbitrary")))(
            xc, xc, xc, proj, dt_bias, a_log, s_prev_all, dy, dxs_skip, e_map, dproj)


BIAS_LANE = MLA_ROPE // 2
KEY_OFF = -1e30
ATT_SCALE = (MLA_NOPE + MLA_ROPE) ** -0.5


def _row_t(col):
    return jnp.broadcast_to(col, (col.shape[0], LANE)).T[:8]


def _attn_fwd2(geo, qn, qp, kn, kp, v):
    t, lp = ATT_BLK, geo.lp
    nb = lp // t

    def body(qn_ref, qp_ref, kn_ref, kp_ref, v_ref, o_ref, lse_ref, k_ref):
        qi = pl.program_id(2)

        @pl.when(qi == 0)
        def _():
            k_ref[:, :LANE] = kn_ref[...]
            k_ref[:, LANE:] = kp_ref[...]

        q = jnp.concatenate([qn_ref[...], qp_ref[...]], axis=1)

        def blk(kj, ntile, carry, diag):
            m, l, acc = carry
            ks = pl.ds(pl.multiple_of(kj * t, t), ntile * t)
            s = _mxdot(q, k_ref[ks, :], 1, 1) * ATT_SCALE
            if diag:
                s = jnp.where(_tri(t), s, -jnp.inf)
            m_new = jnp.maximum(m, jnp.max(s, axis=1, keepdims=True))
            pr = jnp.exp(s - m_new)
            alpha = jnp.exp(m - m_new)
            return m_new, alpha * l + jnp.sum(pr, axis=1, keepdims=True), alpha * acc + _mxdot(pr, v_ref[ks, :], 1, 0)

        carry = (jnp.full((t, 1), 2.0 * KEY_OFF, F32), jnp.zeros((t, 1), F32), jnp.zeros((t, LANE), F32))
        done = 0
        for ntile in (4, 2, 1):
            steps = (qi - done) // ntile
            carry = lax.fori_loop(0, steps, lambda j, c, d=done, n=ntile: blk(d + n * j, n, c, False), carry)
            done = done + steps * ntile
        m, l, acc = blk(qi, 1, carry, True)
        o_ref[...] = (acc / l).astype(o_ref.dtype)
        lse_ref[0, 0, 0] = _row_t(m + jnp.log(l))

    tile = pl.BlockSpec((t, LANE), lambda b, h, i: (b * nb + i, h))
    seq = pl.BlockSpec((lp, LANE), lambda b, h, i: (b, h))
    return pl.pallas_call(
        body, name="attn_fwd", grid=(geo.bsz, MLA_HEADS, nb),
        in_specs=[tile, tile, seq, pl.BlockSpec((lp, LANE), lambda b, h, i: (b, 0)), seq],
        out_specs=[tile, pl.BlockSpec((1, 1, 1, 8, t), lambda b, h, i: (b, h, i, 0, 0))],
        out_shape=[jax.ShapeDtypeStruct((geo.nrows, geo.hq), MXU_DTYPE),
                   jax.ShapeDtypeStruct((geo.bsz, MLA_HEADS, nb, 8, t), F32)],
        scratch_shapes=[pltpu.VMEM((lp, 2 * LANE), MXU_DTYPE)],
        compiler_params=_cparams(("parallel", "parallel", "arbitrary")))(qn, qp, kn, kp, v)


def _attn_bwd2(geo, qn, qp, kn, kp, v, d_o, o, lse):
    t, lp = ATT_BLK, geo.lp
    nb = lp // t

    def body(qn_ref, qp_ref, kn_ref, kp_ref, v_ref, do_ref, o_ref, lse_ref,
             dqn_ref, dqp_ref, dkn_ref, dkp_ref, dv_ref, q_ref, dl_s):
        kj = pl.program_id(2)

        @pl.when(kj == 0)
        def _():
            q_ref[:, :LANE] = qn_ref[...]
            q_ref[:, LANE:] = qp_ref[...]
            dqn_ref[...] = jnp.zeros_like(dqn_ref)
            dqp_ref[...] = jnp.zeros_like(dqp_ref)
            for i in range(nb):
                rows = slice(i * t, (i + 1) * t)
                dl_s[i] = _row_t(jnp.sum(do_ref[rows, :].astype(F32) * o_ref[rows, :].astype(F32), axis=1, keepdims=True))

        k, vv = jnp.concatenate([kn_ref[...], kp_ref[...]], axis=1), v_ref[...]

        def row(ref, qi, ntile):
            return jnp.concatenate([ref[qi + i][:1, :] for i in range(ntile)], axis=1)

        def blk(qi, ntile, carry, diag):
            dk, dv = carry
            qs = pl.ds(pl.multiple_of(qi * t, t), ntile * t)
            q, d_o_blk = q_ref[qs, :], do_ref[qs, :]
            st = _mxdot(k, q, 1, 1) * ATT_SCALE
            if diag:
                keys = lax.broadcasted_iota(jnp.int32, (t, t), 0)
                st = jnp.where(keys <= lax.broadcasted_iota(jnp.int32, (t, t), 1), st, -jnp.inf)
            pt = jnp.exp(st - row(lse_ref.at[0, 0], qi, ntile))
            dst = pt * (_mxdot(vv, d_o_blk, 1, 1) - row(dl_s, qi, ntile)) * ATT_SCALE
            dq = _mxdot(dst, k, 0, 0)
            dqn_ref[qs, :] += dq[:, :LANE]
            dqp_ref[qs, :] += dq[:, LANE:]
            return dk + _mxdot(dst, q, 1, 0), dv + _mxdot(pt, d_o_blk, 1, 0)

        carry = blk(kj, 1, (jnp.zeros((t, 2 * LANE), F32), jnp.zeros((t, LANE), F32)), True)
        done = kj + 1
        for ntile in (4, 2, 1):
            steps = (nb - done) // ntile
            carry = lax.fori_loop(0, steps, lambda j, c, d=done, n=ntile: blk(d + n * j, n, c, False), carry)
            done = done + steps * ntile
        dk, dv = carry
        dkn_ref[...] = dk[:, :LANE].astype(dkn_ref.dtype)
        dkp_ref[...] = dk[:, LANE:]
        dv_ref[...] = dv.astype(dv_ref.dtype)

    seq = pl.BlockSpec((lp, LANE), lambda b, h, j: (b, h))
    tile = pl.BlockSpec((t, LANE), lambda b, h, j: (b * nb + j, h))
    return pl.pallas_call(
        body, name="attn_bwd", grid=(geo.bsz, MLA_HEADS, nb),
        in_specs=[seq, seq, tile, pl.BlockSpec((t, LANE), lambda b, h, j: (b * nb + j, 0)), tile, seq, seq,
                  pl.BlockSpec((1, 1, nb, 8, t), lambda b, h, j: (b, h, 0, 0, 0))],
        out_specs=[seq, seq, tile, tile, tile],
        out_shape=[jax.ShapeDtypeStruct((geo.nrows, geo.hq), F32), jax.ShapeDtypeStruct((geo.nrows, geo.hq), F32),
                   jax.ShapeDtypeStruct((geo.nrows, geo.hq), MXU_DTYPE), jax.ShapeDtypeStruct((geo.nrows, geo.hq), F32),
                   jax.ShapeDtypeStruct((geo.nrows, geo.hq), MXU_DTYPE)],
        scratch_shapes=[pltpu.VMEM((lp, 2 * LANE), MXU_DTYPE), pltpu.VMEM((nb, 8, t), F32)],
        compiler_params=_cparams(("parallel", "parallel", "arbitrary")))(qn, qp, kn, kp, v, d_o, o, lse)


def _rope(x, cos, sin):
    return x * cos + pltpu.roll(x, LANE // 2, axis=1) * sin


def _rope_t(dx, cos, sin):
    return dx * cos + pltpu.roll(dx * sin, LANE // 2, axis=1)


def _per_head(f):
    def fn(x, cos, sin):
        return (jnp.concatenate([f(x[:, h * LANE:(h + 1) * LANE], cos, sin) for h in range(MLA_HEADS)], axis=1),)
    return fn


def _layer_fwd(geo, h, w, tab, late=None):
    nr, tr, trw = geo.nrows, geo.tr, geo.tr_wide
    tb = geo.lp // tr
    rw = functools.partial(_rowwise, nrows=nr)
    s = {"h": h}
    (s["u"],) = rw("rms_mix", lambda x, g: (_rms(x, g),), tr=tr, rows=[(h, D_MODEL, 0)],
                   vecs=[(w["norm_mix_w"], D_MODEL, 0)], outs=[(D_MODEL, D_MODEL, MXU_DTYPE)])
    proj, s["proj_dt"] = _mm("mm_in", s["u"], w["w_in_pt"], tb=True, out_dtype=MXU_DTYPE,
                             side=(geo.col["dt"][0], LANE))
    s["proj"] = proj
    xc = s["xc"] = _conv_fwd(geo, proj, w["conv_w"], w["conv_b"])
    s["y_ssd"], s["s_prev"] = _ssd_fwd_g(geo, xc, s["proj_dt"], w["dt_bias"], w["a_log"])
    gw = SSM_D_INNER // SSM_GROUPS

    def gate_norm(y, x, z, dsk, nw):
        return (_rms((y + x * dsk) * _silu(z.astype(F32)), nw),)

    (s["y_ssm"],) = rw("ssm_gate_norm", gate_norm, tr=tr, ncb=SSM_GROUPS,
                       rows=[(s["y_ssd"], gw, 0), (xc, gw, 0), (proj, gw, geo.col["z"][0] // gw)],
                       vecs=[(w["d_skip_full"], gw, 0), (w["ssm_norm_w"], gw, 0)], outs=[(SSM_D_INNER, gw, MXU_DTYPE)])
    if late is not None:
        w = {**w, **late(s["y_ssm"])}
    (s["cq_n"],) = rw("rms_q", lambda x, g: (_rms(x, g),), tr=tr, rows=[(proj, MLA_Q_LORA, geo.cb("c_q"))],
                      vecs=[(w["q_norm_w"], MLA_Q_LORA, 0)], outs=[(MLA_Q_LORA, MLA_Q_LORA, MXU_DTYPE)])
    (s["ckv_n"],) = rw("rms_kv", lambda x, g: (_rms(x, g),), tr=tr, rows=[(proj, MLA_KV_LORA, geo.cb("c_kv"))],
                       vecs=[(w["kv_norm_w"], MLA_KV_LORA, 0)], outs=[(MLA_KV_LORA, MLA_KV_LORA, MXU_DTYPE)])
    s["qn"] = _mm("mm_qn", s["cq_n"], w["w_qn"], out_dtype=MXU_DTYPE)
    qp_raw = _mm("mm_qp", s["cq_n"], w["w_qp"])
    s["kn"] = _mm("mm_kn", s["ckv_n"], w["w_k"], out_dtype=MXU_DTYPE)
    s["v"] = _mm("mm_v", s["ckv_n"], w["w_v"], out_dtype=MXU_DTYPE)
    bias_lane = lambda: lax.broadcasted_iota(jnp.int32, (1, LANE), 1) == BIAS_LANE
    rope_tabs = [(tab["cos"], LANE, 0), (tab["sin"], LANE, 0)]
    (s["qp"],) = rw("rope_q", _per_head(lambda xp, c, sn: jnp.where(bias_lane(), 1.0, _rope(xp, c, sn))), tr=tr,
                    rows=[(qp_raw, geo.hq, 0)], tabs=rope_tabs, outs=[(geo.hq, geo.hq, MXU_DTYPE)], tab_blocks=tb)
    (s["kp"],) = rw("rope_k", lambda xp, c, sn, valid: (jnp.where(bias_lane(), KEY_OFF * (1.0 - valid),
                                                                 _rope(xp.astype(F32), c, sn)),),
                    tr=tr, rows=[(proj, LANE, geo.cb("k_rope"))], tabs=rope_tabs + [(tab["valid"], 1, 0)],
                    outs=[(LANE, LANE, MXU_DTYPE)], tab_blocks=tb)
    s["o"], s["lse"] = _attn_fwd2(geo, s["qn"], s["qp"], s["kn"], s["kp"], s["v"])
    s["ys_p"] = _mm("mm_bs", s["y_ssm"], w["w_branch_ssm"], out_dtype=MXU_DTYPE)
    s["ym_p"] = _mm("mm_bm", s["o"], w["w_branch_mla"], out_dtype=MXU_DTYPE)

    def gate(gs, gm, ys, ym):
        return (_sigmoid(gs.astype(F32)) * ys + _sigmoid(gm.astype(F32)) * ym,)

    (s["mixed"],) = rw("gate", gate, tr=tr, rows=[(proj, D_MODEL, geo.cb("g_ssm")), (proj, D_MODEL, geo.cb("g_mla")),
                                                  (s["ys_p"], D_MODEL, 0), (s["ym_p"], D_MODEL, 0)],
                       outs=[(D_MODEL, D_MODEL, MXU_DTYPE)])
    s["h2"] = _mm("mm_out", s["mixed"], w["w_out"], add=h)
    (s["vn"],) = rw("rms_mlp", lambda x, g: (_rms(x, g),), tr=tr, rows=[(s["h2"], D_MODEL, 0)],
                    vecs=[(w["norm_mlp_w"], D_MODEL, 0)], outs=[(D_MODEL, D_MODEL, MXU_DTYPE)])
    s["up"], s["act"] = _mm("mm_up", s["vn"], w["w_mlp_up"],
                            epi=(lambda r: (r, jnp.square(jnp.maximum(r, 0.0))), (MXU_DTYPE, MXU_DTYPE)))
    return _mm("mm_down", s["act"], w["w_mlp_down"], add=s["h2"]), s, w


def _layer_bwd(geo, dh3, s, w, tab, mid=None, tail=None, dep=None):
    nr, tr, trw = geo.nrows, geo.tr, geo.tr_wide
    tb = geo.lp // tr
    rw = functools.partial(_rowwise, nrows=nr)
    g = {}
    proj = s["proj"]

    def rms_bwd(x, dy, res, gw):
        _, vjp = jax.vjp(_rms, x.astype(F32), gw)
        dx, dgw = vjp(dy.astype(F32))
        return dx + res, dgw

    def rms_bwd_nores(x, dy, gw):
        _, vjp = jax.vjp(_rms, x.astype(F32), gw)
        return vjp(dy.astype(F32))

    (dup,) = _mm("mm_down_t", dh3, w["w_mlp_down"], tb=True, add=s["up"], dep=dep,
                 epi=(lambda r, up: (r * 2.0 * jnp.maximum(up, 0.0),), (MXU_DTYPE,)))
    g["w_mlp_down"] = _mm("mm_down_g", s["act"], dh3, ta=True, out_dtype=MXU_DTYPE)
    g["w_mlp_up"] = _mm("mm_up_g", s["vn"], dup, ta=True, out_dtype=MXU_DTYPE)
    dvn = _mm("mm_up_t", dup, w["w_mlp_up"], tb=True)
    dh2, g["norm_mlp_w"] = rw("rms_mlp_bwd", rms_bwd, tr=tr,
                              rows=[(s["h2"], D_MODEL, 0), (dvn, D_MODEL, 0), (dh3, D_MODEL, 0)],
                              vecs=[(w["norm_mlp_w"], D_MODEL, 0)], outs=[(D_MODEL, D_MODEL, F32)],
                              reds=[(D_MODEL, D_MODEL)])
    dmixed = _mm("mm_out_t", dh2, w["w_out"], tb=True, out_dtype=MXU_DTYPE)
    g["w_out"] = _mm("mm_out_g", s["mixed"], dh2, ta=True, out_dtype=MXU_DTYPE)

    def gate_bwd(gs, gm, ys, ym, dm):
        f = lambda a, b, c, d: _sigmoid(a) * c + _sigmoid(b) * d
        _, vjp = jax.vjp(f, gs.astype(F32), gm.astype(F32), ys.astype(F32), ym.astype(F32))
        dgs, dgm, dys, dym = vjp(dm.astype(F32))
        return dys, dym, jnp.concatenate([dgs, dgm], axis=1)

    assert geo.col["g_mla"][0] == geo.col["g_ssm"][0] + D_MODEL and geo.col["g_ssm"][0] % (2 * D_MODEL) == 0
    dys_p, dym_p, dproj = rw(
        "gate_bwd", gate_bwd, tr=tr,
        rows=[(proj, D_MODEL, geo.cb("g_ssm")), (proj, D_MODEL, geo.cb("g_mla")), (s["ys_p"], D_MODEL, 0),
              (s["ym_p"], D_MODEL, 0), (dmixed, D_MODEL, 0)],
        outs=[(D_MODEL, D_MODEL, MXU_DTYPE)] * 2 + [(geo.pw, 2 * D_MODEL, MXU_DTYPE, geo.col["g_ssm"][0] // (2 * D_MODEL))])
    g["w_branch_ssm"] = _mm("mm_bs_g", s["y_ssm"], dys_p, ta=True, out_dtype=MXU_DTYPE)
    dy_ssm = _mm("mm_bs_t", dys_p, w["w_branch_ssm"], tb=True, out_dtype=MXU_DTYPE)
    g["w_branch_mla"] = _mm("mm_bm_g", s["o"], dym_p, ta=True, out_dtype=MXU_DTYPE)
    d_o = _mm("mm_bm_t", dym_p, w["w_branch_mla"], tb=True, out_dtype=MXU_DTYPE)
    dqn, dqp, dkn, dkp_h, dv = _attn_bwd2(geo, s["qn"], s["qp"], s["kn"], s["kp"], s["v"], d_o, s["o"], s["lse"])
    rope_tabs = [(tab["cos"], LANE, 0), (tab["sin"], LANE, 0)]
    (dqp_raw,) = rw("rope_q_bwd", _per_head(_rope_t), tr=tr, rows=[(dqp, geo.hq, 0)], tabs=rope_tabs,
                    outs=[(geo.hq, geo.hq, MXU_DTYPE)], tab_blocks=tb)

    def rope_k_bwd(x, c, sn):
        tot = x[:, :LANE]
        for hd in range(1, MLA_HEADS):
            tot = tot + x[:, hd * LANE:(hd + 1) * LANE]
        return (_rope_t(tot, c, sn),)

    (dproj,) = rw("rope_k_bwd", rope_k_bwd, tr=tr, rows=[(dkp_h, geo.hq, 0)], tabs=rope_tabs,
                  outs=[(geo.pw, LANE, MXU_DTYPE, geo.cb("k_rope"), dproj)], tab_blocks=tb)
    g["w_qn"] = _mm("mm_qn_g", s["cq_n"], dqn, ta=True, out_dtype=MXU_DTYPE)
    g["w_qp"] = _mm("mm_qp_g", s["cq_n"], dqp_raw, ta=True, out_dtype=MXU_DTYPE)
    dcq_n = _mm("mm_qp_t", dqp_raw, w["w_qp"], tb=True, add=_mm("mm_qn_t", dqn, w["w_qn"], tb=True))
    g["w_k"] = _mm("mm_kn_g", s["ckv_n"], dkn, ta=True, out_dtype=MXU_DTYPE)
    g["w_v"] = _mm("mm_v_g", s["ckv_n"], dv, ta=True, out_dtype=MXU_DTYPE)
    dckv_n = _mm("mm_v_t", dv, w["w_v"], tb=True, add=_mm("mm_kn_t", dkn, w["w_k"], tb=True))
    dproj, g["q_norm_w"] = rw("rms_q_bwd", rms_bwd_nores, tr=tr,
                              rows=[(proj, MLA_Q_LORA, geo.cb("c_q")), (dcq_n, MLA_Q_LORA, 0)],
                              vecs=[(w["q_norm_w"], MLA_Q_LORA, 0)],
                              outs=[(geo.pw, MLA_Q_LORA, MXU_DTYPE, geo.cb("c_q"), dproj)], reds=[(MLA_Q_LORA, MLA_Q_LORA)])
    dproj, g["kv_norm_w"] = rw("rms_kv_bwd", rms_bwd_nores, tr=tr,
                               rows=[(proj, MLA_KV_LORA, geo.cb("c_kv")), (dckv_n, MLA_KV_LORA, 0)],
                               vecs=[(w["kv_norm_w"], MLA_KV_LORA, 0)],
                               outs=[(geo.pw, MLA_KV_LORA, MXU_DTYPE, geo.cb("c_kv"), dproj)],
                               reds=[(MLA_KV_LORA, MLA_KV_LORA)])
    gw_ = SSM_D_INNER // SSM_GROUPS
    d_skip_full = w["d_skip_full"] if mid is None else w["d_skip_full"] + mid(g)[0, 0]

    def gate_norm_bwd(y, x, z, dy, dsk, nw):
        f = lambda y_, x_, z_, dsk_, nw_: _rms((y_ + x_ * dsk_) * _silu(z_), nw_)
        _, vjp = jax.vjp(f, y.astype(F32), x.astype(F32), z.astype(F32), dsk, nw)
        dy_, dx_, dz_, ddsk, dnw = vjp(dy.astype(F32))
        return dy_, dx_, dz_, ddsk, dnw

    dy_ssd, dxs_skip, dproj, g["d_skip_full"], g["ssm_norm_w"] = rw(
        "ssm_gate_norm_bwd", gate_norm_bwd, tr=tr, ncb=SSM_GROUPS,
        rows=[(s["y_ssd"], gw_, 0), (s["xc"], gw_, 0), (proj, gw_, geo.col["z"][0] // gw_), (dy_ssm, gw_, 0)],
        vecs=[(d_skip_full, gw_, 0), (w["ssm_norm_w"], gw_, 0)],
        outs=[(SSM_D_INNER, gw_, MXU_DTYPE), (SSM_D_INNER, gw_, MXU_DTYPE),
              (geo.pw, gw_, MXU_DTYPE, geo.col["z"][0] // gw_, dproj)],
        reds=[(SSM_D_INNER, gw_), (SSM_D_INNER, gw_)])
    dxc, dproj, g["dt_bias"], g["a_log"] = _ssd_bwd_g(geo, s["xc"], s["proj_dt"], w["dt_bias"], w["a_log"], s["s_prev"],
                                                     dy_ssd, dxs_skip, dproj)
    dproj, g["conv_w"], g["conv_b"] = _conv_bwd(geo, proj, w["conv_w"], w["conv_b"], dxc, dproj)
    g["w_in_pt"] = _mm("mm_in_g", dproj, s["u"], ta=True, out_dtype=MXU_DTYPE)
    du = _mm("mm_in_t", dproj, w["w_in_pt"], dep=None if tail is None else tail(g))
    dh, g["norm_mix_w"] = rw("rms_mix_bwd", rms_bwd, tr=tr,
                             rows=[(s["h"], D_MODEL, 0), (du, D_MODEL, 0), (dh2, D_MODEL, 0)],
                             vecs=[(w["norm_mix_w"], D_MODEL, 0)], outs=[(D_MODEL, D_MODEL, F32)],
                             reds=[(D_MODEL, D_MODEL)])
    return dh, g


def _loss_bwd(geo, h, fw, target, tab):
    tr = geo.tr

    def fn(x, tgt, gw, tok):
        def lossf(x_, gw_):
            err = jnp.square(_rms(x_, gw_) - tgt)
            return 0.5 * jnp.sum(tok * jnp.mean(err, axis=-1, keepdims=True), axis=0, keepdims=True)

        val, vjp = jax.vjp(lossf, x, gw)
        dx, dgw = vjp(jnp.ones((1, 1), F32))
        return dx, jnp.broadcast_to(val, (1, LANE)), dgw

    return _rowwise("loss", fn, nrows=geo.nrows, tr=tr, rows=[(h, D_MODEL, 0), (target, D_MODEL, 0)],
                    vecs=[(fw, D_MODEL, 0)], tabs=[(tab["token"], 1, 0)], outs=[(D_MODEL, D_MODEL, F32)],
                    reds=[(LANE, LANE), (D_MODEL, D_MODEL)], tab_blocks=geo.lp // tr)


def kernel(x, meta_tokens, norm_mix_w, w_in, conv_w, conv_b, dt_bias, a_log, d_skip, ssm_norm_w, q_norm_w, kv_norm_w, w_uq, w_ukv, w_branch_ssm, w_branch_mla, w_out, norm_mlp_w, w_mlp_up, w_mlp_down, final_norm_w, loss_target, m_meta_tokens, m_norm_mix_w, m_w_in, m_conv_w, m_conv_b, m_dt_bias, m_a_log, m_d_skip, m_ssm_norm_w, m_q_norm_w, m_kv_norm_w, m_w_uq, m_w_ukv, m_w_branch_ssm, m_w_branch_mla, m_w_out, m_norm_mlp_w, m_w_mlp_up, m_w_mlp_down, m_final_norm_w, v_meta_tokens, v_norm_mix_w, v_w_in, v_conv_w, v_conv_b, v_dt_bias, v_a_log, v_d_skip, v_ssm_norm_w, v_q_norm_w, v_kv_norm_w, v_w_uq, v_w_ukv, v_w_branch_ssm, v_w_branch_mla, v_w_out, v_norm_mlp_w, v_w_mlp_up, v_w_mlp_down, v_final_norm_w):
    args = dict(locals())
    turn = lambda n, a: jnp.swapaxes(a, 1, 2) if n == "w_in" else a
    wts = {n: turn(n, args[n]) for n in WEIGHTS}
    mom = {n: turn(n, args["m_" + n]) for n in WEIGHTS}
    var = {n: turn(n, args["v_" + n]) for n in WEIGHTS}
    bsz, seq, _ = x.shape
    depth = w_in.shape[0]
    geo = _Geo(bsz, seq)
    tab = _tables(geo)

    big_names = [n for n, _ in BIG]
    sh_names = big_names + [n for n, _ in SHARDED_F32]
    kinds = dict(BIG + SHARDED_F32, w_in="row")
    shard3 = lambda a: a.reshape((1,) + a.shape) if a.ndim == 2 else a
    wire = {n: (MXU_DTYPE if n in big_names else F32) for n in sh_names}
    cast = {n: shard3(wts[n]).astype(wire[n]) for n in sh_names}
    per_layer = [n for n in sh_names if n != "meta_tokens"]
    small_names = ["norm_mix_w", "conv_b", "dt_bias", "a_log", "d_skip", "ssm_norm_w", "q_norm_w", "kv_norm_w",
                   "norm_mlp_w"]

    def gather_items(pairs):
        ins, outs, items, forms = [], [], [], []
        for n, i in pairs:
            a, b = cast[n].shape[1:]
            shape, dst, form = _gather_plan(a, b, kinds[n])
            items.append((len(ins), len(outs), (lambda ref, p, i=i: ref.at[i]), dst))
            ins.append(cast[n])
            outs.append(jax.ShapeDtypeStruct(shape, wire[n]))
            forms.append(form)
        return ins, outs, items, forms

    def whole_weights(pairs, forms, got):
        by_layer = {}
        for (n, i), form, g in zip(pairs, forms, got):
            if n == "w_in":
                n, g = "w_in_pt", _w_in_assemble(geo, g)
            elif form == "row":
                g = g.reshape(g.shape[0] * g.shape[1], g.shape[2])
            elif form == "stack":
                g = _unshard(g, "col")
            by_layer.setdefault(i, {})[n] = g
        return by_layer

    def prep(i, whole, token=None):
        wl = dict(whole)
        wl.update({n: wts[n][i] for n in small_names})
        if token is not None:
            wl["norm_mix_w"] = wl["norm_mix_w"] + token[0, 0]
        return _prep_layer(geo, wl)

    early = ("w_in", "conv_w")
    late_names = [n for n in per_layer if n not in early]
    pairs1 = [(n, i) for i in range(1, depth) for n in per_layer]
    groups = [[(n, 0) for n in early] + [("meta_tokens", 0)], [(n, 0) for n in late_names]] + ([pairs1] if pairs1 else [])
    started = {}

    def gather_start(gi, dep=None):
        ins, outs, items, forms = gather_items(groups[gi])
        sems, thru, landing, token = _exchange_start("gather_w%d_start" % gi, ins, outs, items, dep)
        started[gi] = (groups[gi], forms, sems, thru, landing, items)
        return token

    def gathered(gi, after):
        pairs, forms, sems, thru, landing, items = started[gi]
        return whole_weights(pairs, forms, _exchange_wait("gather_w%d_wait" % gi, sems, thru, landing, items, after))

    def late0(after):
        whole = gathered(1, after)[0]
        if pairs1:
            whole["q_norm_w"] = wts["q_norm_w"][0] + gather_start(2, whole["w_out"])[0, 0]
        return _prep_layer(geo, whole)

    token = gather_start(1, gather_start(0))
    whole0 = gathered(0, token)[0]
    meta_full = whole0.pop("meta_tokens")

    meta = jnp.broadcast_to(meta_full[None], (bsz, N_META, D_MODEL))
    h = jnp.concatenate([jnp.zeros((bsz, geo.pad, D_MODEL), F32), meta, x], axis=1).reshape(geo.nrows, D_MODEL)
    target = jnp.concatenate([jnp.zeros((bsz, geo.pad + N_META, D_MODEL), F32), loss_target], axis=1)
    target = target.reshape(geo.nrows, D_MODEL)
    layers, saved = [], []
    for i in range(depth):
        if i == 0:
            w, late = prep(0, whole0, token), late0
        else:
            if i == 1:
                whole1 = gathered(2, h)
            w, late = prep(i, whole1[i]), None
        h, s, w = _layer_fwd(geo, h, w, tab, late)
        layers.append(w)
        saved.append(s)
    dh, loss_part, g_final = _loss_bwd(geo, h, final_norm_w.reshape(1, -1), target, tab)

    def scatter_items(pairs):
        ins, outs, items = [], [], []
        for n, i in pairs:
            a, b = cast[n].shape[1:]
            arr = g_meta if n == "meta_tokens" else grads[i]["w_in_pt" if n == "w_in" else n]
            if n == "w_in":
                arr, src = _w_in_split(geo, arr, a), _entry
            elif kinds[n] == "row":
                src = lambda ref, p, a=a: ref.at[pl.ds(pl.multiple_of(p * a, a), a)]
            elif b % LANE == 0:
                src = lambda ref, p, b=b: ref.at[:, pl.ds(pl.multiple_of(p * b, b), b)]
            else:
                arr, src = _shard(arr, "col"), _entry
            items.append((len(ins), len(outs), src, _entry))
            ins.append(arr.astype(wire[n]))
            outs.append(jax.ShapeDtypeStruct((N_DEV, a, b), wire[n]))
        return ins, outs, items

    grads = [None] * depth
    landed, pending, res = {}, {}, {}

    def scatter_start(name, pairs):
        ins, outs, items = scatter_items(pairs)
        sems, thru, landing, token = _exchange_start(name + "_start", ins, outs, items)
        pending[name] = (pairs, sems, thru, landing, items)
        return token

    def scatter_wait(name, after):
        pairs, sems, thru, landing, items = pending[name]
        landed.update(zip(pairs, _exchange_wait(name + "_wait", sems, thru, landing, items, after)))

    def adam(n):
        parts = [landed[(n, i)] for i in range(cast[n].shape[0])]
        r = _adamw_nat("adamw_" + n, parts, shard3(wts[n]), shard3(mom[n]), shard3(var[n]))
        res[n] = [a.reshape(wts[n].shape) for a in r]

    def mid0(g):
        grads[0] = _unprep_grads(geo, g)
        return scatter_start("scatter_gb0", [(n, 0) for n in late_names])

    def tail0(g):
        grads[0] = _unprep_grads(geo, g)
        return scatter_start("scatter_ga0", [(n, 0) for n in early])

    dep = None
    for i in reversed(range(depth)):
        dh, gl = _layer_bwd(geo, dh, saved[i], layers[i], tab, *((mid0, tail0) if i == 0 else (None, None)), dep)
        grads[i] = _unprep_grads(geo, gl)
        if i == 1:
            dep = scatter_start("scatter_g1", pairs1)
    dh = dh.reshape(bsz, geo.lp, D_MODEL)
    grad_x = dh[:, geo.pad + N_META:]
    g_meta = jnp.sum(dh[:, geo.pad:geo.pad + N_META], axis=0)
    if pairs1:
        scatter_wait("scatter_g1", g_meta)
    scatter_wait("scatter_gb0", g_meta)
    for n in late_names:
        adam(n)
    g_small = {n: jnp.stack([grads[i][n] for i in range(depth)]) for n in SMALL if n != "final_norm_w"}
    g_small["final_norm_w"] = g_final.reshape(-1)
    zero = jnp.zeros((1,), F32)
    pk = lambda d, last: _pack([d[n] for n in SMALL] + [last], F32, row_mult=8)
    packed = pk(g_small, loss_part[0, :1])
    ins, outs, items = scatter_items([("meta_tokens", 0)])
    parts, landed[("meta_tokens", 0)] = _exchange(
        "gather_g", [packed] + ins + [res[n][1] for n in late_names],
        [jax.ShapeDtypeStruct((N_DEV,) + packed.shape, F32)] + outs,
        [(0, 0, _whole, _entry)] + [(1, 1, items[0][2], items[0][3])])
    adam("meta_tokens")
    scatter_wait("scatter_ga0", res["meta_tokens"][1])
    for n in early:
        adam(n)
    res_sm = _adamw("adamw_small", parts, pk(wts, zero), pk(mom, zero), pk(var, zero))
    res_sm = [_unpack(r, [wts[n].shape for n in SMALL] + [(1,)]) for r in res_sm]
    loss = res_sm[0][-1][0]

    out = [loss, grad_x]
    for k in range(4):
        named = {n: res[n][k] for n in sh_names}
        named.update(zip(SMALL, res_sm[k]))
        out += [turn(n, named[n]) for n in WEIGHTS]
    return tuple(out)
```

```python
import functools

import numpy as np
import jax
import jax.numpy as jnp
from jax import lax
from jax.experimental import pallas as pl
from jax.experimental.pallas import tpu as pltpu

F32 = jnp.float32
MXU_DTYPE = jnp.bfloat16

D_MODEL = 1024
N_META = 16
EPS = 1e-6
SSM_D_INNER = 2048
SSM_HEAD_DIM = 64
SSM_GROUPS = 4
SSM_STATE = 128
SSM_CONV = 4
SSM_CHUNK = 128
MLA_HEADS = 8
MLA_Q_LORA = 512
MLA_KV_LORA = 256
MLA_NOPE = 128
MLA_ROPE = 64
MLA_V = 128
ROPE_THETA = 10000.0
D_FF = 4096
ADAM_LR = 0.001
ADAM_B1 = 0.9
ADAM_B2 = 0.999
ADAM_EPS = 1e-08
ADAM_WD = 0.01
ADAM_STEP = 10

N_DEV = 8
ATT_BLK = 256
LANE = 128
PACK_W = 1024
VMEM_LIMIT = 56 * 1024 * 1024
MESH_ID = pl.DeviceIdType.MESH

BIG = (("w_in", "col"), ("w_uq", "col"), ("w_ukv", "col"), ("w_branch_ssm", "row"), ("w_branch_mla", "row"),
       ("w_out", "row"), ("w_mlp_up", "col"), ("w_mlp_down", "row"))
SHARDED_F32 = (("conv_w", "col"), ("meta_tokens", "col"))
SMALL = ("norm_mix_w", "conv_b", "dt_bias", "a_log", "d_skip", "ssm_norm_w", "q_norm_w", "kv_norm_w",
         "norm_mlp_w", "final_norm_w")
WEIGHTS = ("meta_tokens", "norm_mix_w", "w_in", "conv_w", "conv_b", "dt_bias", "a_log", "d_skip", "ssm_norm_w",
           "q_norm_w", "kv_norm_w", "w_uq", "w_ukv", "w_branch_ssm", "w_branch_mla", "w_out", "norm_mlp_w",
           "w_mlp_up", "w_mlp_down", "final_norm_w")


def _cparams(sem=None):
    return pltpu.CompilerParams(dimension_semantics=sem, vmem_limit_bytes=VMEM_LIMIT)


def _pick(n, cands):
    for c in cands:
        if n % c == 0:
            return c
    return n


def _sigmoid(x):
    return 1.0 / (1.0 + jnp.exp(-x))


def _silu(x):
    return x * _sigmoid(x)


def _softplus(x):
    t = jnp.exp(-jnp.abs(x))
    return jnp.maximum(x, 0.0) + jnp.where(t < 0.01, t * (1.0 - t * (0.5 - t * (1.0 / 3.0))), jnp.log(1.0 + t))


def _rms(x, w):
    x = x.astype(F32)
    return x * lax.rsqrt(jnp.mean(x * x, axis=-1, keepdims=True) + EPS) * w


def _dot(a, b, ca, cb, precision=None):
    return lax.dot_general(a, b, (((ca,), (cb,)), ((), ())), preferred_element_type=F32, precision=precision)


def _mxdot(a, b, ca, cb):
    return _dot(a.astype(MXU_DTYPE), b.astype(MXU_DTYPE), ca, cb)


def _mm(name, a, b, *, ta=False, tb=False, add=None, out_dtype=F32, dep=None, epi=None, side=None):
    (kdim, m) = a.shape if ta else a.shape[::-1]
    (n, k2) = b.shape if tb else b.shape[::-1]
    assert kdim == k2, (name, a.shape, b.shape)
    tm = _pick(m, (1152, 1088, 1024, 768, 544, 512, 384, 256, 128))
    tn = _pick(n, (1024, 512, 384, 256, 128))
    tk = _pick(kdim, (1152, 1088, 1024, 768, 544, 512, 384, 256, 128))
    nk = kdim // tk
    a_spec = pl.BlockSpec((tk, tm), lambda i, j, k: (k, i)) if ta else pl.BlockSpec((tm, tk), lambda i, j, k: (i, k))
    b_spec = pl.BlockSpec((tn, tk), lambda i, j, k: (j, k)) if tb else pl.BlockSpec((tk, tn), lambda i, j, k: (k, j))
    o_spec = pl.BlockSpec((tm, tn), lambda i, j, k: (i, j))
    ca, cb = (0 if ta else 1), (1 if tb else 0)

    out_dtypes = [out_dtype] if epi is None else list(epi[1])
    n_out = len(out_dtypes)
    n_side = 0 if side is None else 1

    def body(*refs):
        a_ref, b_ref = refs[:2]
        o_refs, acc = refs[-1 - n_side - n_out:-1 - n_side], refs[-1]
        k = pl.program_id(2)

        @pl.when(k == 0)
        def _():
            acc[...] = jnp.zeros_like(acc)

        acc[...] += _mxdot(a_ref[...], b_ref[...], ca, cb)

        @pl.when(k == nk - 1)
        def _():
            r = acc[...]
            if epi is not None:
                res = epi[0](r, refs[2][...]) if add is not None else epi[0](r)
            else:
                res = (r + refs[2][...].astype(F32) if add is not None else r,)
            for o_ref, val in zip(o_refs, res):
                o_ref[...] = val.astype(o_ref.dtype)

        if side is not None:
            @pl.when(jnp.logical_and(k == nk - 1, pl.program_id(1) == side[0] // tn))
            def _():
                refs[-2][...] = acc[:, side[0] % tn:side[0] % tn + side[1]]

    in_specs, args = [a_spec, b_spec], [a, b]
    if add is not None:
        in_specs.append(o_spec)
        args.append(add)
    if dep is not None:
        in_specs.append(pl.BlockSpec((8, LANE), lambda i, j, k: (0, 0)))
        args.append(dep)
    out_specs = [o_spec] * n_out
    out_shape = [jax.ShapeDtypeStruct((m, n), dt) for dt in out_dtypes]
    if side is not None:
        assert side[0] % tn + side[1] <= tn
        out_specs.append(pl.BlockSpec((tm, side[1]), lambda i, j, k: (i, 0)))
        out_shape.append(jax.ShapeDtypeStruct((m, side[1]), F32))
    res = pl.pallas_call(
        body, name=name, grid=(m // tm, n // tn, nk), in_specs=in_specs, out_specs=out_specs, out_shape=out_shape,
        scratch_shapes=[pltpu.VMEM((tm, tn), F32)],
        compiler_params=_cparams(("parallel", "arbitrary" if side is not None else "parallel", "arbitrary")))(*args)
    return res[0] if epi is None and side is None else res


def _rowwise(name, fn, *, nrows, tr, ncb=1, rows=(), fixed=(), vecs=(), tabs=(), outs=(), reds=(), tab_blocks=1):
    in_specs, args = [], []
    for arr, w, c0 in rows:
        in_specs.append(pl.BlockSpec((tr, w), lambda g, i, c0=c0: (i, c0 + g)))
        args.append(arr)
    for arr, w, c0 in fixed:
        in_specs.append(pl.BlockSpec((tr, w), lambda g, i, c0=c0: (i, c0)))
        args.append(arr)
    for arr, w, c0 in vecs:
        in_specs.append(pl.BlockSpec((1, w), lambda g, i, c0=c0: (0, c0 + g)))
        args.append(arr)
    for arr, w, c0 in tabs:
        in_specs.append(pl.BlockSpec((tr, w), lambda g, i, c0=c0: (i % tab_blocks, c0)))
        args.append(arr)
    n_in, n_out = len(args), len(outs)
    out_shape, out_specs, aliases = [], [], {}
    for k, o in enumerate(outs):
        c0 = o[3] if len(o) > 3 else 0
        out_shape.append(jax.ShapeDtypeStruct((nrows, o[0]), o[2]))
        out_specs.append(pl.BlockSpec((tr, o[1]), lambda g, i, c0=c0: (i, c0 + g)))
        if len(o) > 4:
            aliases[len(args)] = k
            in_specs.append(pl.BlockSpec(memory_space=pl.ANY))
            args.append(o[4])
    out_shape += [jax.ShapeDtypeStruct((1, wt), F32) for wt, w in reds]
    out_specs += [pl.BlockSpec((1, w), lambda g, i: (0, g)) for wt, w in reds]
    first_out = len(args)

    def body(*refs):
        res = fn(*[r[...] for r in refs[:n_in]])
        for o_ref, val in zip(refs[first_out:first_out + n_out], res[:n_out]):
            o_ref[...] = val.astype(o_ref.dtype)
        i = pl.program_id(1)
        for d_ref, val in zip(refs[first_out + n_out:], res[n_out:]):
            @pl.when(i == 0)
            def _(d_ref=d_ref, val=val):
                d_ref[...] = val

            @pl.when(i > 0)
            def _(d_ref=d_ref, val=val):
                d_ref[...] += val

    return pl.pallas_call(
        body, name=name, grid=(ncb, nrows // tr), in_specs=in_specs, out_specs=out_specs, out_shape=out_shape,
        input_output_aliases=aliases, compiler_params=_cparams(("parallel", "arbitrary")))(*args)


def _peer(k):
    x, y, c = lax.axis_index("x"), lax.axis_index("y"), lax.axis_index("c")
    px = jnp.where((k >> 2) & 1, 1 - x, x)
    py = jnp.where((k >> 1) & 1, 1 - y, y)
    pc = jnp.where(k & 1, 1 - c, c)
    return (px, py, pc), 4 * px + 2 * py + pc


def _my_index():
    return 4 * lax.axis_index("x") + 2 * lax.axis_index("y") + lax.axis_index("c")


def _exchange(name, ins, out_shapes, items):
    n_in, n_out, n_it = len(ins), len(out_shapes), len(items)

    def body(*refs):
        x, o = refs[:n_in], refs[n_in:n_in + n_out]
        send_sems, recv_sems, local_sems = refs[n_in + n_out:]
        me = _my_index()
        local, sends = [], []
        for t, (ii, io, src, dst) in enumerate(items):
            cp = pltpu.make_async_copy(src(x[ii], me), dst(o[io], me), local_sems.at[t])
            cp.start()
            local.append(cp)
        for k in range(1, N_DEV):
            dev, idx = _peer(k)
            for t, (ii, io, src, dst) in enumerate(items):
                s = (k - 1) * n_it + t
                cp = pltpu.make_async_remote_copy(
                    src_ref=src(x[ii], idx), dst_ref=dst(o[io], me), send_sem=send_sems.at[s],
                    recv_sem=recv_sems.at[s], device_id=dev, device_id_type=MESH_ID)
                cp.start()
                sends.append(cp)
        for k in range(1, N_DEV):
            dev, idx = _peer(k)
            for t, (ii, io, src, dst) in enumerate(items):
                s = (k - 1) * n_it + t
                pltpu.make_async_remote_copy(
                    src_ref=src(x[ii], idx), dst_ref=dst(o[io], idx), send_sem=send_sems.at[s],
                    recv_sem=recv_sems.at[s], device_id=dev, device_id_type=MESH_ID).wait_recv()
        for cp in sends:
            cp.wait_send()
        for cp in local:
            cp.wait()

    nsem = (N_DEV - 1) * n_it
    anyspec = pl.BlockSpec(memory_space=pl.ANY)
    return pl.pallas_call(
        body, name=name, out_shape=list(out_shapes), in_specs=[anyspec] * n_in, out_specs=[anyspec] * n_out,
        scratch_shapes=[pltpu.SemaphoreType.DMA((nsem,)), pltpu.SemaphoreType.DMA((nsem,)),
                        pltpu.SemaphoreType.DMA((n_it,))],
        compiler_params=pltpu.CompilerParams(has_side_effects=True))(*ins)


def _split_copies(x, land, send_sems, recv_sems, items, receive):
    me = _my_index()
    remote, n_it = [], len(items)
    for k in range(1, N_DEV):
        dev, idx = _peer(k)
        for t, (ii, io, src, dst) in enumerate(items):
            s = (k - 1) * n_it + t
            remote.append(pltpu.make_async_remote_copy(
                src_ref=src(x[ii], idx), dst_ref=dst(land[io], idx if receive else me), send_sem=send_sems.at[s],
                recv_sem=recv_sems.at[s], device_id=dev, device_id_type=MESH_ID))
    local = [pltpu.make_async_copy(src(x[ii], me), dst(land[io], me), send_sems.at[(N_DEV - 1) * n_it + t])
             for t, (ii, io, src, dst) in enumerate(items)]
    return remote, local


def _exchange_start(name, ins, out_shapes, items, dep=None):
    n_in, n_out, n_it = len(ins), len(out_shapes), len(items)

    def body(*refs):
        x, land = refs[:n_in], refs[n_in:n_in + n_out]
        first_out = n_in + n_out + (dep is not None)
        send_sems, recv_sems, token = refs[first_out], refs[first_out + 1], refs[-1]
        remote, local = _split_copies(x, land, send_sems, recv_sems, items, False)
        for cp in remote + local:
            cp.start()
        token[...] = jnp.zeros_like(token)

    hbm = pl.BlockSpec(memory_space=pltpu.HBM)
    sem = pl.BlockSpec(memory_space=pltpu.SEMAPHORE)
    arrs = [pltpu.with_memory_space_constraint(a, pltpu.HBM)
            for a in list(ins) + [lax.empty(s.shape, s.dtype) for s in out_shapes]]
    res = pl.pallas_call(
        body, name=name,
        out_shape=(pltpu.SemaphoreType.DMA((N_DEV * n_it,)), pltpu.SemaphoreType.DMA(((N_DEV - 1) * n_it,)),
                   *[pltpu.HBM(a.shape, a.dtype) for a in arrs], jax.ShapeDtypeStruct((8, LANE), F32)),
        in_specs=[hbm] * (n_in + n_out) + ([] if dep is None else [pl.BlockSpec(memory_space=pl.ANY)]),
        out_specs=(sem, sem, *[hbm] * (n_in + n_out), pl.BlockSpec(memory_space=pltpu.VMEM)),
        input_output_aliases={i: 2 + i for i in range(n_in + n_out)},
        compiler_params=pltpu.CompilerParams(has_side_effects=pltpu.SideEffectType.DATAFLOW_SIDE_EFFECTING))(
            *arrs, *([] if dep is None else [dep]))
    return res[:2], res[2:2 + n_in], res[2 + n_in:2 + n_in + n_out], res[-1]


def _exchange_wait(name, sems, ins, landing, items, after):
    n_in, n_out = len(ins), len(landing)

    def body(*refs):
        x, land = refs[:n_in], refs[n_in:n_in + n_out]
        send_sems, recv_sems = refs[n_in + n_out], refs[n_in + n_out + 1]
        remote, local = _split_copies(x, land, send_sems, recv_sems, items, True)
        for cp in remote:
            cp.wait_send()
            cp.wait_recv()
        for cp in local:
            cp.wait()

    hbm = pl.BlockSpec(memory_space=pltpu.HBM)
    sem = pl.BlockSpec(memory_space=pltpu.SEMAPHORE)
    arrs = list(ins) + list(landing)
    res = pl.pallas_call(
        body, name=name, out_shape=tuple(pltpu.HBM(a.shape, a.dtype) for a in arrs),
        in_specs=[hbm] * (n_in + n_out) + [sem, sem, pl.BlockSpec(memory_space=pl.ANY)],
        out_specs=tuple([hbm] * (n_in + n_out)), input_output_aliases={i: i for i in range(n_in + n_out)},
        compiler_params=pltpu.CompilerParams(has_side_effects=pltpu.SideEffectType.DATAFLOW_SIDE_EFFECTING))(
            *arrs, *sems, after)
    return res[n_in:]


def _whole(ref, p):
    return ref


def _entry(ref, p):
    return ref.at[p]


def _gather_plan(a, b, kind):
    if kind == "col" and b % LANE == 0:
        return (a, N_DEV * b), (lambda ref, p: ref.at[:, pl.ds(pl.multiple_of(p * b, b), b)]), "col"
    return (N_DEV, a, b), _entry, ("row" if kind == "row" else "stack")


def _adamw_nat(name, parts, w, m, v):
    depth, b, c = w.shape
    assert len(parts) == depth
    tb = _pick(b, (128, 64, 32, 16, 8))
    if tb == b and b > 256:
        tb = 256
    spec = pl.BlockSpec((1, tb, c), lambda i, j: (i, j, 0))

    def body(*refs):
        p_refs = refs[:depth]
        w_ref, m_ref, v_ref, g_ref, d_ref, nm_ref, nv_ref = refs[depth:]
        for layer, p_ref in enumerate(p_refs):
            @pl.when(pl.program_id(0) == layer)
            def _(p_ref=p_ref):
                g = p_ref[0].astype(F32)
                for j in range(1, N_DEV):
                    g = g + p_ref[j].astype(F32)
                nm = ADAM_B1 * m_ref[0] + (1.0 - ADAM_B1) * g
                nv = ADAM_B2 * v_ref[0] + (1.0 - ADAM_B2) * jnp.square(g)
                m_hat = nm / (1.0 - ADAM_B1 ** ADAM_STEP)
                v_hat = nv / (1.0 - ADAM_B2 ** ADAM_STEP)
                g_ref[0] = g
                d_ref[0] = -ADAM_LR * (m_hat / (jnp.sqrt(v_hat) + ADAM_EPS) + ADAM_WD * w_ref[0])
                nm_ref[0] = nm
                nv_ref[0] = nv

    sds = jax.ShapeDtypeStruct((depth, b, c), F32)
    return pl.pallas_call(
        body, name=name, grid=(depth, pl.cdiv(b, tb)),
        in_specs=[pl.BlockSpec((N_DEV, tb, c), lambda i, j: (0, j, 0))] * depth + [spec, spec, spec],
        out_specs=[spec] * 4, out_shape=[sds] * 4, compiler_params=_cparams(("parallel", "parallel")))(*parts, w, m, v)


def _adamw(name, parts, w, m, v):
    rows = w.shape[0]
    tr = _pick(rows, (256, 128, 64, 32, 16, 8))
    spec = pl.BlockSpec((tr, PACK_W), lambda i: (i, 0))

    def body(p_ref, w_ref, m_ref, v_ref, g_ref, d_ref, nm_ref, nv_ref):
        g = p_ref[0]
        for j in range(1, N_DEV):
            g = g + p_ref[j]
        nm = ADAM_B1 * m_ref[...] + (1.0 - ADAM_B1) * g
        nv = ADAM_B2 * v_ref[...] + (1.0 - ADAM_B2) * jnp.square(g)
        m_hat = nm / (1.0 - ADAM_B1 ** ADAM_STEP)
        v_hat = nv / (1.0 - ADAM_B2 ** ADAM_STEP)
        g_ref[...] = g
        d_ref[...] = -ADAM_LR * (m_hat / (jnp.sqrt(v_hat) + ADAM_EPS) + ADAM_WD * w_ref[...])
        nm_ref[...] = nm
        nv_ref[...] = nv

    sds = jax.ShapeDtypeStruct((rows, PACK_W), F32)
    return pl.pallas_call(
        body, name=name, grid=(rows // tr,),
        in_specs=[pl.BlockSpec((N_DEV, tr, PACK_W), lambda i: (0, i, 0)), spec, spec, spec],
        out_specs=[spec] * 4, out_shape=[sds] * 4, compiler_params=_cparams(("parallel",)))(parts, w, m, v)


def _pack(arrs, dtype, row_mult=16):
    flat = jnp.concatenate([a.reshape(-1).astype(dtype) for a in arrs])
    unit = row_mult * PACK_W
    total = -(-flat.shape[0] // unit) * unit
    flat = jnp.pad(flat, (0, total - flat.shape[0]))
    return flat.reshape(-1, PACK_W)


def _pack_lead(arrs, dtype, row_mult):
    flat = jnp.concatenate([a.reshape(N_DEV, -1).astype(dtype) for a in arrs], axis=1)
    unit = row_mult * PACK_W
    total = -(-flat.shape[1] // unit) * unit
    flat = jnp.pad(flat, ((0, 0), (0, total - flat.shape[1])))
    return flat.reshape(N_DEV, -1, PACK_W)


def _unpack(buf, shapes, lead=()):
    flat = buf.reshape(lead + (-1,))
    out, off = [], 0
    for s in shapes:
        n = int(np.prod(s))
        out.append(flat[..., off:off + n].reshape(lead + tuple(s)))
        off += n
    return out


def _unshard(g, kind):
    if kind == "col":
        g = jnp.moveaxis(g, 0, -2)
        return g.reshape(g.shape[:-2] + (g.shape[-2] * g.shape[-1],))
    g = jnp.moveaxis(g, 0, 1)
    return g.reshape((g.shape[0], g.shape[1] * g.shape[2]) + g.shape[3:])


def _shard(full, kind):
    if kind == "col":
        s = full.reshape(full.shape[:-1] + (N_DEV, full.shape[-1] // N_DEV))
        return jnp.moveaxis(s, -2, 0)
    s = full.reshape((full.shape[0], N_DEV, full.shape[1] // N_DEV) + full.shape[2:])
    return jnp.moveaxis(s, 1, 0)


class _Geo:
    def __init__(self, bsz, seq):
        self.bsz, self.seq = bsz, seq
        self.pad = (-N_META) % SSM_CHUNK
        self.lp = self.pad + N_META + seq
        self.t0 = self.pad + N_META
        self.nq = 1 + seq // ATT_BLK
        assert self.t0 == LANE and seq % ATT_BLK == 0 and self.lp % SSM_CHUNK == 0
        self.nrows = bsz * self.lp
        self.nc = self.lp // SSM_CHUNK
        self.nh = SSM_D_INNER // SSM_HEAD_DIM
        self.gn = SSM_GROUPS * SSM_STATE
        self.cd = SSM_D_INNER + 2 * self.gn
        self.hq = MLA_HEADS * LANE
        order = (("z", SSM_D_INNER), ("g_ssm", D_MODEL), ("g_mla", D_MODEL), ("xs", SSM_D_INNER), ("bm", self.gn),
                 ("cm", self.gn), ("c_q", MLA_Q_LORA), ("c_kv", MLA_KV_LORA), ("dt", LANE), ("k_rope", LANE))
        self.col, off = {}, 0
        for nm, w in order:
            assert off % w == 0, (nm, off, w)
            self.col[nm] = (off, w)
            off += w
        self.pw = off
        assert self.nh <= LANE and MLA_ROPE == 64 and MLA_NOPE == LANE and MLA_V == LANE
        self.tr = _pick(self.lp, (1088, 768, 544, 512, 384, 272, 256, 128))
        self.tr_wide = _pick(self.lp, (544, 384, 272, 256, 128))

    def cb(self, nm):
        off, w = self.col[nm]
        return off // w

    def w_in_runs(self, shard_w):
        nh, half = self.nh, MLA_ROPE // 2
        src, pieces = 0, []
        for nm, n in (("z", SSM_D_INNER), ("xs", SSM_D_INNER), ("bm", self.gn), ("cm", self.gn), ("dt", nh),
                      ("c_q", MLA_Q_LORA), ("c_kv", MLA_KV_LORA), ("k_rope", MLA_ROPE), ("g_ssm", D_MODEL),
                      ("g_mla", D_MODEL)):
            dst = self.col[nm][0]
            if nm == "k_rope":
                pieces += [(src, half, dst), (src + half, half, dst + 2 * half)]
            else:
                pieces.append((src, n, dst))
            src += n
        assert src == shard_w * N_DEV
        runs = []
        for a, n, dst in pieces:
            for j in range(N_DEV):
                lo, hi = max(a, j * shard_w), min(a + n, (j + 1) * shard_w)
                if lo < hi:
                    runs.append((j, lo - j * shard_w, hi - lo, dst + lo - a))
        return runs


def _slot(a):
    h = MLA_ROPE // 2
    z = jnp.zeros(a.shape[:-1] + (h,), a.dtype)
    return jnp.concatenate([a[..., :h], z, a[..., h:], z], axis=-1)


def _unslot(a):
    h = MLA_ROPE // 2
    return jnp.concatenate([a[..., :h], a[..., 2 * h:3 * h]], axis=-1)


def _prep_layer(geo, wl):
    nh = geo.nh
    p = {}
    if "w_uq" in wl:
        uq = wl["w_uq"].reshape(MLA_Q_LORA, MLA_HEADS, MLA_NOPE + MLA_ROPE)
        p["w_qn"] = uq[..., :MLA_NOPE].reshape(MLA_Q_LORA, geo.hq)
        p["w_qp"] = _slot(uq[..., MLA_NOPE:]).reshape(MLA_Q_LORA, geo.hq)
    if "w_ukv" in wl:
        ukv = wl["w_ukv"].reshape(MLA_KV_LORA, MLA_HEADS, MLA_NOPE + MLA_V)
        p["w_k"] = ukv[..., :MLA_NOPE].reshape(MLA_KV_LORA, geo.hq)
        p["w_v"] = ukv[..., MLA_NOPE:].reshape(MLA_KV_LORA, geo.hq)
    for nm in ("w_in_pt", "conv_w", "w_branch_ssm", "w_branch_mla", "w_out", "w_mlp_up", "w_mlp_down"):
        if nm in wl:
            p[nm] = wl[nm]
    for nm in ("norm_mix_w", "conv_b", "ssm_norm_w", "q_norm_w", "kv_norm_w", "norm_mlp_w"):
        if nm in wl:
            p[nm] = wl[nm].reshape(1, -1)
    if "dt_bias" in wl:
        p["dt_bias"] = jnp.pad(wl["dt_bias"], (0, LANE - nh)).reshape(1, LANE)
        p["a_log"] = jnp.pad(wl["a_log"], (0, LANE - nh)).reshape(1, LANE)
        p["d_skip_full"] = jnp.repeat(wl["d_skip"], SSM_HEAD_DIM).reshape(1, SSM_D_INNER)
    return p


def _unprep_grads(geo, g):
    nh = geo.nh
    out = {}
    if "w_qn" in g:
        qn = g["w_qn"].reshape(MLA_Q_LORA, MLA_HEADS, MLA_NOPE)
        qp = _unslot(g["w_qp"].reshape(MLA_Q_LORA, MLA_HEADS, LANE))
        out["w_uq"] = jnp.concatenate([qn, qp], axis=-1).reshape(MLA_Q_LORA, -1)
    if "w_k" in g:
        wk = g["w_k"].reshape(MLA_KV_LORA, MLA_HEADS, MLA_NOPE)
        wv = g["w_v"].reshape(MLA_KV_LORA, MLA_HEADS, MLA_V)
        out["w_ukv"] = jnp.concatenate([wk, wv], axis=-1).reshape(MLA_KV_LORA, -1)
    for nm in ("w_in_pt", "w_branch_ssm", "w_branch_mla", "w_out", "w_mlp_up", "w_mlp_down", "conv_w"):
        if nm in g:
            out[nm] = g[nm]
    for nm in ("norm_mix_w", "conv_b", "ssm_norm_w", "q_norm_w", "kv_norm_w", "norm_mlp_w"):
        if nm in g:
            out[nm] = g[nm].reshape(-1)
    if "dt_bias" in g:
        out["dt_bias"] = g["dt_bias"].reshape(-1)[:nh]
        out["a_log"] = g["a_log"].reshape(-1)[:nh]
        out["d_skip"] = g["d_skip_full"].reshape(nh, SSM_HEAD_DIM).sum(-1)
    return out


def _tables(geo):
    pos = jnp.arange(geo.lp, dtype=F32) - geo.pad
    inv = ROPE_THETA ** (-jnp.arange(0, MLA_ROPE, 2, dtype=F32) / MLA_ROPE)
    ang = pos[:, None] * inv[None, :]
    cos, sin = jnp.cos(ang), jnp.sin(ang)
    z = jnp.zeros_like(cos)
    rows = jnp.arange(geo.lp)[:, None]
    return {"cos": jnp.concatenate([cos, z, cos, z], axis=-1), "sin": jnp.concatenate([-sin, z, sin, z], axis=-1),
            "valid": (rows >= geo.pad).astype(F32), "token": (rows >= geo.pad + N_META).astype(F32)}


def _w_in_assemble(geo, gathered):
    _, sw, d = gathered.shape
    runs = geo.w_in_runs(sw)
    tl = _pick(d, (256, 128))

    def body(x_ref, o_ref):
        o_ref[...] = jnp.zeros_like(o_ref)
        for j, s0, n, d0 in runs:
            o_ref[d0:d0 + n, :] = x_ref[j, s0:s0 + n, :]

    return pl.pallas_call(
        body, name="w_in_assemble", grid=(d // tl,), in_specs=[pl.BlockSpec((N_DEV, sw, tl), lambda i: (0, 0, i))],
        out_specs=pl.BlockSpec((geo.pw, tl), lambda i: (0, i)),
        out_shape=jax.ShapeDtypeStruct((geo.pw, d), gathered.dtype), compiler_params=_cparams(("parallel",)))(gathered)


def _w_in_split(geo, g_padded, sw):
    d = g_padded.shape[1]
    runs = geo.w_in_runs(sw)
    tl = _pick(d, (256, 128))

    def body(x_ref, o_ref):
        for j, s0, n, d0 in runs:
            o_ref[j, s0:s0 + n, :] = x_ref[d0:d0 + n, :]

    return pl.pallas_call(
        body, name="w_in_split", grid=(d // tl,), in_specs=[pl.BlockSpec((geo.pw, tl), lambda i: (0, i))],
        out_specs=pl.BlockSpec((N_DEV, sw, tl), lambda i: (0, 0, i)),
        out_shape=jax.ShapeDtypeStruct((N_DEV, sw, d), g_padded.dtype),
        compiler_params=_cparams(("parallel",)))(g_padded)


def _conv_cols(geo, cbw):
    x0 = geo.col["xs"][0]
    assert geo.col["bm"][0] == x0 + SSM_D_INNER and geo.col["cm"][0] == geo.col["bm"][0] + geo.gn and x0 % cbw == 0
    return lambda j: x0 // cbw + j


def _conv_taps(x):
    return [pltpu.roll(x, SSM_CONV - 1 - k, axis=0) for k in range(SSM_CONV - 1)] + [x]


def _conv_pre(x, w_ref, b_ref, taps=None):
    taps = _conv_taps(x) if taps is None else taps
    acc = b_ref[...]
    for k in range(SSM_CONV):
        acc = acc + taps[k] * w_ref[k:k + 1, :]
    return acc


def _conv_fwd(geo, proj, conv_w, conv_b):
    cbw = 256
    colmap = _conv_cols(geo, cbw)
    lp, pad = geo.lp, geo.pad

    def body(x_ref, w_ref, b_ref, o_ref):
        valid = (lax.broadcasted_iota(jnp.int32, (lp, 1), 0) >= pad).astype(F32)
        o_ref[...] = (_silu(_conv_pre(x_ref[...].astype(F32), w_ref, b_ref)) * valid).astype(o_ref.dtype)

    return pl.pallas_call(
        body, name="conv_fwd", grid=(geo.bsz, geo.cd // cbw),
        in_specs=[pl.BlockSpec((lp, cbw), lambda b, j: (b, colmap(j))),
                  pl.BlockSpec((SSM_CONV, cbw), lambda b, j: (0, j)), pl.BlockSpec((1, cbw), lambda b, j: (0, j))],
        out_specs=pl.BlockSpec((lp, cbw), lambda b, j: (b, j)),
        out_shape=jax.ShapeDtypeStruct((geo.nrows, geo.cd), MXU_DTYPE),
        compiler_params=_cparams(("parallel", "parallel")))(proj, conv_w, conv_b)


def _conv_bwd(geo, proj, conv_w, conv_b, dxc, dproj):
    cbw = 256
    colmap = _conv_cols(geo, cbw)
    lp, pad = geo.lp, geo.pad

    def body(x_ref, w_ref, b_ref, dy_ref, _, dx_ref, gw_ref, gb_ref):
        b = pl.program_id(1)
        valid = (lax.broadcasted_iota(jnp.int32, (lp, 1), 0) >= pad).astype(F32)
        taps = _conv_taps(x_ref[...].astype(F32))
        pre = _conv_pre(None, w_ref, b_ref, taps)
        sig = _sigmoid(pre)
        dpre = dy_ref[...] * (sig * (1.0 + pre * (1.0 - sig))) * valid
        dx = dpre * w_ref[SSM_CONV - 1:SSM_CONV, :]
        for k in range(SSM_CONV - 1):
            dx = dx + pltpu.roll(dpre, lp - (SSM_CONV - 1 - k), axis=0) * w_ref[k:k + 1, :]
        gws = [jnp.sum(dpre * taps[k], axis=0, keepdims=True) for k in range(SSM_CONV)]
        dx_ref[...] = (dx * valid).astype(dx_ref.dtype)

        @pl.when(b == 0)
        def _():
            gw_ref[...] = jnp.zeros_like(gw_ref)
            gb_ref[...] = jnp.zeros_like(gb_ref)

        for k in range(SSM_CONV):
            gw_ref[k:k + 1, :] += gws[k]
        gb_ref[...] += jnp.sum(dpre, axis=0, keepdims=True)

    return pl.pallas_call(
        body, name="conv_bwd", grid=(geo.cd // cbw, geo.bsz),
        in_specs=[pl.BlockSpec((lp, cbw), lambda j, b: (b, colmap(j))),
                  pl.BlockSpec((SSM_CONV, cbw), lambda j, b: (0, j)), pl.BlockSpec((1, cbw), lambda j, b: (0, j)),
                  pl.BlockSpec((lp, cbw), lambda j, b: (b, j)), pl.BlockSpec(memory_space=pl.ANY)],
        out_specs=[pl.BlockSpec((lp, cbw), lambda j, b: (b, colmap(j))),
                   pl.BlockSpec((SSM_CONV, cbw), lambda j, b: (0, j)), pl.BlockSpec((1, cbw), lambda j, b: (0, j))],
        out_shape=[jax.ShapeDtypeStruct(dproj.shape, dproj.dtype),
                   jax.ShapeDtypeStruct((SSM_CONV, geo.cd), F32), jax.ShapeDtypeStruct((1, geo.cd), F32)],
        input_output_aliases={4: 0},
        compiler_params=_cparams(("parallel", "arbitrary")))(proj, conv_w, conv_b, dxc, dproj)


def _tri(q):
    r = lax.broadcasted_iota(jnp.int32, (q, q), 0)
    c = lax.broadcasted_iota(jnp.int32, (q, q), 1)
    return r >= c


def _ssd_pre(dtr, dtb, alog, valid):
    dt = _softplus(dtr + dtb) * valid
    adt = dt * (-jnp.exp(alog))
    a_cs = _dot(_tri(SSM_CHUNK).astype(F32), adt, 1, 0, precision=lax.Precision.HIGHEST)
    return dt, a_cs


def _ssd_specs(geo, rev):
    nc, q = geo.nc, SSM_CHUNK
    ci = (lambda c: nc - 1 - c) if rev else (lambda c: c)
    nxb = SSM_D_INNER // geo.gn
    return [pl.BlockSpec((q, SSM_D_INNER), lambda b, c: (b * nc + ci(c), 0)),
            pl.BlockSpec((q, geo.gn), lambda b, c: (b * nc + ci(c), nxb)),
            pl.BlockSpec((q, geo.gn), lambda b, c: (b * nc + ci(c), nxb + 1)),
            pl.BlockSpec((q, LANE), lambda b, c: (b * nc + ci(c), 0)),
            pl.BlockSpec((1, LANE), lambda b, c: (0, 0)), pl.BlockSpec((1, LANE), lambda b, c: (0, 0))], ci


def _expand_heads(cols, nh):
    per = LANE // SSM_HEAD_DIM
    lane = lax.broadcasted_iota(jnp.int32, (1, LANE), 1)
    blocks = []
    for j in range(nh // per):
        blk = jnp.broadcast_to(cols[:, j * per:j * per + 1], (cols.shape[0], LANE))
        for k in range(1, per):
            blk = jnp.where(lane >= k * SSM_HEAD_DIM, cols[:, j * per + k:j * per + k + 1], blk)
        blocks.append(blk)
    return jnp.concatenate(blocks, axis=1)


def _head_maps(geo):
    e = (jnp.arange(SSM_D_INNER)[None, :] // SSM_HEAD_DIM == jnp.arange(LANE)[:, None]).astype(F32)
    return e, e.T


def _ssd_fwd_g(geo, xc, proj, dt_bias, a_log):
    q, p, n, e = SSM_CHUNK, SSM_HEAD_DIM, SSM_STATE, geo.nh // SSM_GROUPS
    nc, pad, gw = geo.nc, geo.pad, SSM_D_INNER // SSM_GROUPS
    in_specs, _ = _ssd_specs(geo, False)

    def body(xs_ref, b_ref, c_ref, dtr_ref, dtb_ref, alog_ref, y_ref, sp_ref, state, xdt_s, y_s):
        c = pl.program_id(1)

        @pl.when(c == 0)
        def _():
            state[...] = jnp.zeros_like(state)

        sp_ref[...] = state[...]
        inert = (c + 1) * q <= pad

        @pl.when(inert)
        def _():
            y_ref[...] = jnp.zeros_like(y_ref)

        @pl.when(jnp.logical_not(inert))
        def _():
            valid = (c * q + lax.broadcasted_iota(jnp.int32, (q, 1), 0) >= pad).astype(F32)
            dt, a_cs = _ssd_pre(dtr_ref[...], dtb_ref[...], alog_ref[...], valid)
            a_cst = a_cs.T
            dt_x, a_x = _expand_heads(dt, geo.nh), _expand_heads(a_cs, geo.nh)
            tri = _tri(q)
            for g in range(SSM_GROUPS):
                gs = slice(g * gw, (g + 1) * gw)
                bg, cg = b_ref[:, g * n:(g + 1) * n], c_ref[:, g * n:(g + 1) * n]
                a_g = a_x[:, gs]
                a_last = a_g[q - 1:q, :]
                xdt_g = xs_ref[:, gs] * dt_x[:, gs]
                xdt_s[:, gs] = xdt_g
                s_g = state[:, gs]
                y_s[:, gs] = _mxdot(cg, s_g, 1, 0) * jnp.exp(a_g)
                state[:, gs] = s_g * jnp.exp(a_last) + _mxdot(bg, xdt_g * jnp.exp(a_last - a_g), 0, 0)
                cb = _mxdot(cg, bg, 1, 1)
                for hh in range(e):
                    h = g * e + hh
                    hs = slice(h * p, (h + 1) * p)
                    ldec = jnp.exp(jnp.where(tri, a_cs[:, h:h + 1] - a_cst[h:h + 1, :], -jnp.inf))
                    y_s[:, hs] += _mxdot(cb * ldec, xdt_s[:, hs], 1, 0)
            y_ref[...] = y_s[...].astype(y_ref.dtype)

    return pl.pallas_call(
        body, name="ssd_fwd", grid=(geo.bsz, nc), in_specs=in_specs,
        out_specs=[pl.BlockSpec((q, SSM_D_INNER), lambda b, c: (b * nc + c, 0)),
                   pl.BlockSpec((n, SSM_D_INNER), lambda b, c: (b * nc + c, 0))],
        out_shape=[jax.ShapeDtypeStruct((geo.nrows, SSM_D_INNER), MXU_DTYPE),
                   jax.ShapeDtypeStruct((geo.bsz * nc * n, SSM_D_INNER), F32)],
        scratch_shapes=[pltpu.VMEM((n, SSM_D_INNER), F32), pltpu.VMEM((q, SSM_D_INNER), F32),
                        pltpu.VMEM((q, SSM_D_INNER), F32)],
        compiler_params=_cparams(("parallel", "arbitrary")))(xc, xc, xc, proj, dt_bias, a_log)


def _ssd_bwd_g(geo, xc, proj, dt_bias, a_log, s_prev_all, dy, dxs_skip, dproj):
    q, p, n, e = SSM_CHUNK, SSM_HEAD_DIM, SSM_STATE, geo.nh // SSM_GROUPS
    nc, pad, di, gn, gw = geo.nc, geo.pad, SSM_D_INNER, geo.gn, SSM_D_INNER // SSM_GROUPS
    in_specs, ci = _ssd_specs(geo, True)
    row_spec = pl.BlockSpec((q, di), lambda b, c: (b * nc + ci(c), 0))
    e_map, _ = _head_maps(geo)
    in_specs += [pl.BlockSpec((n, di), lambda b, c: (b * nc + ci(c), 0)), row_spec, row_spec,
                 pl.BlockSpec((LANE, di), lambda b, c: (0, 0)), pl.BlockSpec(memory_space=pl.ANY)]

    def body(xs_ref, b_ref, c_ref, dtr_ref, dtb_ref, alog_ref, sp_ref, dy_ref, dsk_ref, e_ref, _,
             dxc_ref, ddt_ref, gdtb_ref, galog_ref, dstate, xdt_s, dxdt_s):
        step = pl.program_id(1)
        first = jnp.logical_and(pl.program_id(0) == 0, step == 0)
        c = nc - 1 - step

        @pl.when(step == 0)
        def _():
            dstate[...] = jnp.zeros_like(dstate)

        @pl.when(first)
        def _():
            gdtb_ref[...] = jnp.zeros_like(gdtb_ref)
            galog_ref[...] = jnp.zeros_like(galog_ref)

        inert = (c + 1) * q <= pad

        @pl.when(inert)
        def _():
            dxc_ref[...] = jnp.zeros_like(dxc_ref)
            ddt_ref[...] = jnp.zeros_like(ddt_ref)

        @pl.when(jnp.logical_not(inert))
        def _():
            valid = (c * q + lax.broadcasted_iota(jnp.int32, (q, 1), 0) >= pad).astype(F32)
            dtr, dtb, alog = dtr_ref[...], dtb_ref[...], alog_ref[...]
            dt, a_cs = _ssd_pre(dtr, dtb, alog, valid)
            a_cst = a_cs.T
            dt_x, a_x = _expand_heads(dt, geo.nh), _expand_heads(a_cs, geo.nh)
            tri = _tri(q)
            lane = lax.broadcasted_iota(jnp.int32, (1, LANE), 1)
            sub = lax.broadcasted_iota(jnp.int32, (LANE, 1), 0)
            d_dt = jnp.zeros((q, LANE), F32)
            d_acs = jnp.zeros((q, LANE), F32)
            d_acst = jnp.zeros((LANE, q), F32)
            d_last = jnp.zeros((1, LANE), F32)
            for g in range(SSM_GROUPS):
                gs = slice(g * gw, (g + 1) * gw)
                bg, cg = b_ref[:, g * n:(g + 1) * n], c_ref[:, g * n:(g + 1) * n]
                seg = lambda v: _mxdot(v, e_ref[:, gs], 1, 1)
                a_g, dt_g, x_g, dy_g = a_x[:, gs], dt_x[:, gs], xs_ref[:, gs], dy_ref[:, gs]
                e_col, e_last, dec = jnp.exp(a_g), jnp.exp(a_g[q - 1:q, :]), jnp.exp(a_g[q - 1:q, :] - a_g)
                xdt_g = x_g * dt_g
                xdt_s[:, gs] = xdt_g
                s_g, ds_g = sp_ref[:, gs], dstate[:, gs]
                cs = _mxdot(cg, s_g, 1, 0)
                d_cs = dy_g * e_col
                d_acs = d_acs + seg(d_cs * cs)
                d_cg = _mxdot(d_cs, s_g, 1, 1)
                dstate[:, gs] = _mxdot(cg, d_cs, 0, 0) + ds_g * e_last
                dl_x = jnp.sum(ds_g * s_g, axis=0, keepdims=True) * e_last
                d_last = d_last + seg(jnp.broadcast_to(dl_x, (8, gw)))[:1]
                gmat = _mxdot(bg, ds_g, 1, 0)
                xd = xdt_g * dec
                d_bg = _mxdot(xd, ds_g, 1, 1)
                d_dec = seg(xd * gmat)
                d_acs = d_acs - d_dec
                d_last = d_last + jnp.sum(d_dec, axis=0, keepdims=True)
                dxdt_s[:, gs] = dec * gmat
                cb = _mxdot(cg, bg, 1, 1)
                d_cb = jnp.zeros((q, q), F32)
                for hh in range(e):
                    h = g * e + hh
                    hs = slice(h * p, (h + 1) * p)
                    ldec = jnp.exp(jnp.where(tri, a_cs[:, h:h + 1] - a_cst[h:h + 1, :], -jnp.inf))
                    dyh = dy_ref[:, hs]
                    d_m = _mxdot(dyh, xdt_s[:, hs], 1, 1)
                    dxdt_s[:, hs] += _mxdot(cb * ldec, dyh, 0, 0)
                    d_cb = d_cb + d_m * ldec
                    d_diff = d_m * cb * ldec
                    d_acs = d_acs + jnp.sum(d_diff, axis=1, keepdims=True) * (lane == h).astype(F32)
                    d_acst = d_acst - (sub == h).astype(F32) * jnp.sum(d_diff, axis=0, keepdims=True)
                d_xdt = dxdt_s[:, gs]
                dxc_ref[:, gs] = d_xdt * dt_g + dsk_ref[:, gs]
                d_dt = d_dt + seg(d_xdt * x_g)
                dxc_ref[:, di + g * n:di + (g + 1) * n] = d_bg + _mxdot(d_cb, cg, 0, 0)
                dxc_ref[:, di + gn + g * n:di + gn + (g + 1) * n] = d_cg + _mxdot(d_cb, bg, 1, 0)
            is_last = (lax.broadcasted_iota(jnp.int32, (q, 1), 0) == q - 1).astype(F32)
            d_acs = d_acs + d_acst.T + is_last * d_last
            d_adt = _dot(_tri(q).astype(F32), d_acs, 0, 0, precision=lax.Precision.HIGHEST)
            a = -jnp.exp(alog)
            d_dt = d_dt + d_adt * a
            d_dtr = d_dt * valid * _sigmoid(dtr + dtb)
            ddt_ref[...] = d_dtr.astype(ddt_ref.dtype)
            gdtb_ref[...] += jnp.sum(d_dtr, axis=0, keepdims=True)
            galog_ref[...] += jnp.sum(d_adt * dt, axis=0, keepdims=True) * a

    vec = pl.BlockSpec((1, LANE), lambda b, c: (0, 0))
    return pl.pallas_call(
        body, name="ssd_bwd", grid=(geo.bsz, nc), in_specs=in_specs,
        out_specs=[pl.BlockSpec((q, geo.cd), lambda b, c: (b * nc + ci(c), 0)),
                   pl.BlockSpec((q, LANE), lambda b, c: (b * nc + ci(c), geo.cb("dt"))), vec, vec],
        out_shape=[jax.ShapeDtypeStruct((geo.nrows, geo.cd), F32), jax.ShapeDtypeStruct(dproj.shape, dproj.dtype),
                   jax.ShapeDtypeStruct((1, LANE), F32), jax.ShapeDtypeStruct((1, LANE), F32)],
        scratch_shapes=[pltpu.VMEM((n, di), F32), pltpu.VMEM((q, di), F32), pltpu.VMEM((q, di), F32)],
        input_output_aliases={10: 1},
        compiler_params=_cparams(("arbitrary", "arbitrary")))(
            xc, xc, xc, proj, dt_bias, a_log, s_prev_all, dy, dxs_skip, e_map, dproj)


BIAS_LANE = MLA_ROPE // 2
KEY_OFF = -1e30
ATT_SCALE = (MLA_NOPE + MLA_ROPE) ** -0.5


def _row_t(col):
    return jnp.broadcast_to(col, (col.shape[0], LANE)).T[:8]


def _att_row0(geo, i):
    return pl.multiple_of(geo.t0 + (i - 1) * ATT_BLK, geo.t0)


def _attn_fwd3(geo, qn, qp, kn, kp, v):
    t, lp, t0, nq = ATT_BLK, geo.lp, geo.t0, geo.nq

    def body(qn_ref, qp_ref, kn_ref, kp_ref, v_ref, o_ref, lse_ref, k_ref):
        qi = pl.program_id(2)

        def blk(q, start, width, carry, diag):
            m, l, acc = carry
            ks = pl.ds(start, width)
            s = _mxdot(q, k_ref[ks, :], 1, 1) * ATT_SCALE
            if diag:
                s = jnp.where(_tri(width), s, -jnp.inf)
            m_new = jnp.maximum(m, jnp.max(s, axis=1, keepdims=True))
            pr = jnp.exp(s - m_new)
            alpha = jnp.exp(m - m_new)
            return m_new, alpha * l + jnp.sum(pr, axis=1, keepdims=True), alpha * acc + _mxdot(pr, v_ref[ks, :], 1, 0)

        def init(rows):
            return (jnp.full((rows, 1), 2.0 * KEY_OFF, F32), jnp.zeros((rows, 1), F32), jnp.zeros((rows, LANE), F32))

        @pl.when(qi == 0)
        def _():
            k_ref[:, :LANE] = kn_ref[...]
            k_ref[:, LANE:] = kp_ref[...]
            q = jnp.concatenate([qn_ref[0:t0, :], qp_ref[0:t0, :]], axis=1)
            m, l, acc = blk(q, 0, t0, init(t0), True)
            o_ref[0:t0, :] = (acc / l).astype(o_ref.dtype)
            lse_ref[0, 0, 0, :, 0:t0] = _row_t(m + jnp.log(l))

        @pl.when(qi > 0)
        def _():
            qs = pl.ds(_att_row0(geo, qi), t)
            q = jnp.concatenate([qn_ref[qs, :], qp_ref[qs, :]], axis=1)
            carry = blk(q, 0, t0, init(t), False)
            done = 1
            for ntile in (4, 2, 1):
                steps = (qi - done) // ntile
                carry = lax.fori_loop(
                    0, steps, lambda j, c, d=done, n=ntile: blk(q, _att_row0(geo, d + n * j), n * t, c, False), carry)
                done = done + steps * ntile
            m, l, acc = blk(q, _att_row0(geo, qi), t, carry, True)
            o_ref[qs, :] = (acc / l).astype(o_ref.dtype)
            lse_ref[0, 0, 0] = _row_t(m + jnp.log(l))

    seq = pl.BlockSpec((lp, LANE), lambda b, h, i: (b, h))
    return pl.pallas_call(
        body, name="attn_fwd", grid=(geo.bsz, MLA_HEADS, nq),
        in_specs=[seq, seq, seq, pl.BlockSpec((lp, LANE), lambda b, h, i: (b, 0)), seq],
        out_specs=[seq, pl.BlockSpec((1, 1, 1, 8, t), lambda b, h, i: (b, h, i, 0, 0))],
        out_shape=[jax.ShapeDtypeStruct((geo.nrows, geo.hq), MXU_DTYPE),
                   jax.ShapeDtypeStruct((geo.bsz, MLA_HEADS, nq, 8, t), F32)],
        scratch_shapes=[pltpu.VMEM((lp, 2 * LANE), MXU_DTYPE)],
        compiler_params=_cparams(("parallel", "parallel", "arbitrary")))(qn, qp, kn, kp, v)


def _attn_bwd3(geo, qn, qp, kn, kp, v, d_o, o, lse):
    t, lp, t0, nq = ATT_BLK, geo.lp, geo.t0, geo.nq

    def body(qn_ref, qp_ref, kn_ref, kp_ref, v_ref, do_ref, o_ref, lse_ref,
             dqn_ref, dqp_ref, dkn_ref, dkp_ref, dv_ref, q_ref, dl_s):
        kj = pl.program_id(2)
        lse = lse_ref.at[0, 0]

        @pl.when(kj == 0)
        def _():
            q_ref[:, :LANE] = qn_ref[...]
            q_ref[:, LANE:] = qp_ref[...]
            dqn_ref[...] = jnp.zeros_like(dqn_ref)
            dqp_ref[...] = jnp.zeros_like(dqp_ref)
            for i in range(nq):
                rows = slice(0, t0) if i == 0 else slice(t0 + (i - 1) * t, t0 + i * t)
                dl = _row_t(jnp.sum(do_ref[rows, :].astype(F32) * o_ref[rows, :].astype(F32), axis=1, keepdims=True))
                if i == 0:
                    dl_s[0, :, 0:t0] = dl
                else:
                    dl_s[i] = dl

        def rows_of(ref, qi, ntile):
            return jnp.concatenate([ref[qi + i][:1, :] for i in range(ntile)], axis=1)

        def qblk(k, vv, start, width, lrow, drow, carry, diag):
            dk, dv = carry
            qs = pl.ds(start, width)
            q, d_o_blk = q_ref[qs, :], do_ref[qs, :]
            st = _mxdot(k, q, 1, 1) * ATT_SCALE
            if diag:
                keys = lax.broadcasted_iota(jnp.int32, (width, width), 0)
                st = jnp.where(keys <= lax.broadcasted_iota(jnp.int32, (width, width), 1), st, -jnp.inf)
            pt = jnp.exp(st - lrow)
            dst = pt * (_mxdot(vv, d_o_blk, 1, 1) - drow) * ATT_SCALE
            dq = _mxdot(dst, k, 0, 0)
            dqn_ref[qs, :] += dq[:, :LANE]
            dqp_ref[qs, :] += dq[:, LANE:]
            return dk + _mxdot(dst, q, 1, 0), dv + _mxdot(pt, d_o_blk, 1, 0)

        def upper(k, vv, first, carry):
            done = first
            for ntile in (4, 2, 1):
                steps = (nq - done) // ntile
                carry = lax.fori_loop(
                    0, steps, lambda j, c, d=done, n=ntile: qblk(
                        k, vv, _att_row0(geo, d + n * j), n * t, rows_of(lse, d + n * j, n),
                        rows_of(dl_s, d + n * j, n), c, False), carry)
                done = done + steps * ntile
            return carry

        def zeros(rows):
            return jnp.zeros((rows, 2 * LANE), F32), jnp.zeros((rows, LANE), F32)

        @pl.when(kj == 0)
        def _():
            k = jnp.concatenate([kn_ref[0:t0, :], kp_ref[0:t0, :]], axis=1)
            vv = v_ref[0:t0, :]
            carry = qblk(k, vv, 0, t0, lse[0][:1, 0:t0], dl_s[0][:1, 0:t0], zeros(t0), True)
            dk, dv = upper(k, vv, 1, carry)
            dkn_ref[0:t0, :] = dk[:, :LANE].astype(dkn_ref.dtype)
            dkp_ref[0:t0, :] = dk[:, LANE:]
            dv_ref[0:t0, :] = dv.astype(dv_ref.dtype)

        @pl.when(kj > 0)
        def _():
            ks = pl.ds(_att_row0(geo, kj), t)
            k = jnp.concatenate([kn_ref[ks, :], kp_ref[ks, :]], axis=1)
            vv = v_ref[ks, :]
            carry = qblk(k, vv, _att_row0(geo, kj), t, rows_of(lse, kj, 1), rows_of(dl_s, kj, 1), zeros(t), True)
            dk, dv = upper(k, vv, kj + 1, carry)
            dkn_ref[ks, :] = dk[:, :LANE].astype(dkn_ref.dtype)
            dkp_ref[ks, :] = dk[:, LANE:]
            dv_ref[ks, :] = dv.astype(dv_ref.dtype)

    seq = pl.BlockSpec((lp, LANE), lambda b, h, j: (b, h))
    return pl.pallas_call(
        body, name="attn_bwd", grid=(geo.bsz, MLA_HEADS, nq),
        in_specs=[seq, seq, seq, pl.BlockSpec((lp, LANE), lambda b, h, j: (b, 0)), seq, seq, seq,
                  pl.BlockSpec((1, 1, nq, 8, t), lambda b, h, j: (b, h, 0, 0, 0))],
        out_specs=[seq, seq, seq, seq, seq],
        out_shape=[jax.ShapeDtypeStruct((geo.nrows, geo.hq), F32), jax.ShapeDtypeStruct((geo.nrows, geo.hq), F32),
                   jax.ShapeDtypeStruct((geo.nrows, geo.hq), MXU_DTYPE), jax.ShapeDtypeStruct((geo.nrows, geo.hq), F32),
                   jax.ShapeDtypeStruct((geo.nrows, geo.hq), MXU_DTYPE)],
        scratch_shapes=[pltpu.VMEM((lp, 2 * LANE), MXU_DTYPE), pltpu.VMEM((nq, 8, t), F32)],
        compiler_params=_cparams(("parallel", "parallel", "arbitrary")))(qn, qp, kn, kp, v, d_o, o, lse)


def _rope(x, cos, sin):
    return x * cos + pltpu.roll(x, LANE // 2, axis=1) * sin


def _rope_t(dx, cos, sin):
    return dx * cos + pltpu.roll(dx * sin, LANE // 2, axis=1)


def _per_head(f):
    def fn(x, cos, sin):
        return (jnp.concatenate([f(x[:, h * LANE:(h + 1) * LANE], cos, sin) for h in range(MLA_HEADS)], axis=1),)
    return fn


def _layer_fwd(geo, h, w, tab, late=None):
    nr, tr, trw = geo.nrows, geo.tr, geo.tr_wide
    tb = geo.lp // tr
    rw = functools.partial(_rowwise, nrows=nr)
    s = {"h": h}
    (s["u"],) = rw("rms_mix", lambda x, g: (_rms(x, g),), tr=tr, rows=[(h, D_MODEL, 0)],
                   vecs=[(w["norm_mix_w"], D_MODEL, 0)], outs=[(D_MODEL, D_MODEL, MXU_DTYPE)])
    proj, s["proj_dt"] = _mm("mm_in", s["u"], w["w_in_pt"], tb=True, out_dtype=MXU_DTYPE,
                             side=(geo.col["dt"][0], LANE))
    s["proj"] = proj
    xc = s["xc"] = _conv_fwd(geo, proj, w["conv_w"], w["conv_b"])
    s["y_ssd"], s["s_prev"] = _ssd_fwd_g(geo, xc, s["proj_dt"], w["dt_bias"], w["a_log"])
    gw = SSM_D_INNER // SSM_GROUPS

    def gate_norm(y, x, z, dsk, nw):
        return (_rms((y + x * dsk) * _silu(z.astype(F32)), nw),)

    (s["y_ssm"],) = rw("ssm_gate_norm", gate_norm, tr=tr, ncb=SSM_GROUPS,
                       rows=[(s["y_ssd"], gw, 0), (xc, gw, 0), (proj, gw, geo.col["z"][0] // gw)],
                       vecs=[(w["d_skip_full"], gw, 0), (w["ssm_norm_w"], gw, 0)], outs=[(SSM_D_INNER, gw, MXU_DTYPE)])
    if late is not None:
        w = {**w, **late(s["y_ssm"])}
    (s["cq_n"],) = rw("rms_q", lambda x, g: (_rms(x, g),), tr=tr, rows=[(proj, MLA_Q_LORA, geo.cb("c_q"))],
                      vecs=[(w["q_norm_w"], MLA_Q_LORA, 0)], outs=[(MLA_Q_LORA, MLA_Q_LORA, MXU_DTYPE)])
    (s["ckv_n"],) = rw("rms_kv", lambda x, g: (_rms(x, g),), tr=tr, rows=[(proj, MLA_KV_LORA, geo.cb("c_kv"))],
                       vecs=[(w["kv_norm_w"], MLA_KV_LORA, 0)], outs=[(MLA_KV_LORA, MLA_KV_LORA, MXU_DTYPE)])
    s["qn"] = _mm("mm_qn", s["cq_n"], w["w_qn"], out_dtype=MXU_DTYPE)
    qp_raw = _mm("mm_qp", s["cq_n"], w["w_qp"])
    s["kn"] = _mm("mm_kn", s["ckv_n"], w["w_k"], out_dtype=MXU_DTYPE)
    s["v"] = _mm("mm_v", s["ckv_n"], w["w_v"], out_dtype=MXU_DTYPE)
    bias_lane = lambda: lax.broadcasted_iota(jnp.int32, (1, LANE), 1) == BIAS_LANE
    rope_tabs = [(tab["cos"], LANE, 0), (tab["sin"], LANE, 0)]
    (s["qp"],) = rw("rope_q", _per_head(lambda xp, c, sn: jnp.where(bias_lane(), 1.0, _rope(xp, c, sn))), tr=tr,
                    rows=[(qp_raw, geo.hq, 0)], tabs=rope_tabs, outs=[(geo.hq, geo.hq, MXU_DTYPE)], tab_blocks=tb)
    (s["kp"],) = rw("rope_k", lambda xp, c, sn, valid: (jnp.where(bias_lane(), KEY_OFF * (1.0 - valid),
                                                                 _rope(xp.astype(F32), c, sn)),),
                    tr=tr, rows=[(proj, LANE, geo.cb("k_rope"))], tabs=rope_tabs + [(tab["valid"], 1, 0)],
                    outs=[(LANE, LANE, MXU_DTYPE)], tab_blocks=tb)
    s["o"], s["lse"] = _attn_fwd3(geo, s["qn"], s["qp"], s["kn"], s["kp"], s["v"])
    s["ys_p"] = _mm("mm_bs", s["y_ssm"], w["w_branch_ssm"], out_dtype=MXU_DTYPE)
    s["ym_p"] = _mm("mm_bm", s["o"], w["w_branch_mla"], out_dtype=MXU_DTYPE)

    def gate(gs, gm, ys, ym):
        return (_sigmoid(gs.astype(F32)) * ys + _sigmoid(gm.astype(F32)) * ym,)

    (s["mixed"],) = rw("gate", gate, tr=tr, rows=[(proj, D_MODEL, geo.cb("g_ssm")), (proj, D_MODEL, geo.cb("g_mla")),
                                                  (s["ys_p"], D_MODEL, 0), (s["ym_p"], D_MODEL, 0)],
                       outs=[(D_MODEL, D_MODEL, MXU_DTYPE)])
    s["h2"] = _mm("mm_out", s["mixed"], w["w_out"], add=h)
    (s["vn"],) = rw("rms_mlp", lambda x, g: (_rms(x, g),), tr=tr, rows=[(s["h2"], D_MODEL, 0)],
                    vecs=[(w["norm_mlp_w"], D_MODEL, 0)], outs=[(D_MODEL, D_MODEL, MXU_DTYPE)])
    s["up"], s["act"] = _mm("mm_up", s["vn"], w["w_mlp_up"],
                            epi=(lambda r: (r, jnp.square(jnp.maximum(r, 0.0))), (MXU_DTYPE, MXU_DTYPE)))
    return _mm("mm_down", s["act"], w["w_mlp_down"], add=s["h2"]), s, w


def _layer_bwd(geo, dh3, s, w, tab, mid=None, tail=None, dep=None):
    nr, tr, trw = geo.nrows, geo.tr, geo.tr_wide
    tb = geo.lp // tr
    rw = functools.partial(_rowwise, nrows=nr)
    g = {}
    proj = s["proj"]

    def rms_bwd(x, dy, res, gw):
        _, vjp = jax.vjp(_rms, x.astype(F32), gw)
        dx, dgw = vjp(dy.astype(F32))
        return dx + res, dgw

    def rms_bwd_nores(x, dy, gw):
        _, vjp = jax.vjp(_rms, x.astype(F32), gw)
        return vjp(dy.astype(F32))

    (dup,) = _mm("mm_down_t", dh3, w["w_mlp_down"], tb=True, add=s["up"], dep=dep,
                 epi=(lambda r, up: (r * 2.0 * jnp.maximum(up, 0.0),), (MXU_DTYPE,)))
    g["w_mlp_down"] = _mm("mm_down_g", s["act"], dh3, ta=True, out_dtype=MXU_DTYPE)
    g["w_mlp_up"] = _mm("mm_up_g", s["vn"], dup, ta=True, out_dtype=MXU_DTYPE)
    dvn = _mm("mm_up_t", dup, w["w_mlp_up"], tb=True)
    dh2, g["norm_mlp_w"] = rw("rms_mlp_bwd", rms_bwd, tr=tr,
                              rows=[(s["h2"], D_MODEL, 0), (dvn, D_MODEL, 0), (dh3, D_MODEL, 0)],
                              vecs=[(w["norm_mlp_w"], D_MODEL, 0)], outs=[(D_MODEL, D_MODEL, F32)],
                              reds=[(D_MODEL, D_MODEL)])
    dmixed = _mm("mm_out_t", dh2, w["w_out"], tb=True, out_dtype=MXU_DTYPE)
    g["w_out"] = _mm("mm_out_g", s["mixed"], dh2, ta=True, out_dtype=MXU_DTYPE)

    def gate_bwd(gs, gm, ys, ym, dm):
        f = lambda a, b, c, d: _sigmoid(a) * c + _sigmoid(b) * d
        _, vjp = jax.vjp(f, gs.astype(F32), gm.astype(F32), ys.astype(F32), ym.astype(F32))
        dgs, dgm, dys, dym = vjp(dm.astype(F32))
        return dys, dym, jnp.concatenate([dgs, dgm], axis=1)

    assert geo.col["g_mla"][0] == geo.col["g_ssm"][0] + D_MODEL and geo.col["g_ssm"][0] % (2 * D_MODEL) == 0
    dys_p, dym_p, dproj = rw(
        "gate_bwd", gate_bwd, tr=tr,
        rows=[(proj, D_MODEL, geo.cb("g_ssm")), (proj, D_MODEL, geo.cb("g_mla")), (s["ys_p"], D_MODEL, 0),
              (s["ym_p"], D_MODEL, 0), (dmixed, D_MODEL, 0)],
        outs=[(D_MODEL, D_MODEL, MXU_DTYPE)] * 2 + [(geo.pw, 2 * D_MODEL, MXU_DTYPE, geo.col["g_ssm"][0] // (2 * D_MODEL))])
    g["w_branch_ssm"] = _mm("mm_bs_g", s["y_ssm"], dys_p, ta=True, out_dtype=MXU_DTYPE)
    dy_ssm = _mm("mm_bs_t", dys_p, w["w_branch_ssm"], tb=True, out_dtype=MXU_DTYPE)
    g["w_branch_mla"] = _mm("mm_bm_g", s["o"], dym_p, ta=True, out_dtype=MXU_DTYPE)
    d_o = _mm("mm_bm_t", dym_p, w["w_branch_mla"], tb=True, out_dtype=MXU_DTYPE)
    dqn, dqp, dkn, dkp_h, dv = _attn_bwd3(geo, s["qn"], s["qp"], s["kn"], s["kp"], s["v"], d_o, s["o"], s["lse"])
    rope_tabs = [(tab["cos"], LANE, 0), (tab["sin"], LANE, 0)]
    (dqp_raw,) = rw("rope_q_bwd", _per_head(_rope_t), tr=tr, rows=[(dqp, geo.hq, 0)], tabs=rope_tabs,
                    outs=[(geo.hq, geo.hq, MXU_DTYPE)], tab_blocks=tb)

    def rope_k_bwd(x, c, sn):
        tot = x[:, :LANE]
        for hd in range(1, MLA_HEADS):
            tot = tot + x[:, hd * LANE:(hd + 1) * LANE]
        return (_rope_t(tot, c, sn),)

    (dproj,) = rw("rope_k_bwd", rope_k_bwd, tr=tr, rows=[(dkp_h, geo.hq, 0)], tabs=rope_tabs,
                  outs=[(geo.pw, LANE, MXU_DTYPE, geo.cb("k_rope"), dproj)], tab_blocks=tb)
    g["w_qn"] = _mm("mm_qn_g", s["cq_n"], dqn, ta=True, out_dtype=MXU_DTYPE)
    g["w_qp"] = _mm("mm_qp_g", s["cq_n"], dqp_raw, ta=True, out_dtype=MXU_DTYPE)
    dcq_n = _mm("mm_qp_t", dqp_raw, w["w_qp"], tb=True, add=_mm("mm_qn_t", dqn, w["w_qn"], tb=True))
    g["w_k"] = _mm("mm_kn_g", s["ckv_n"], dkn, ta=True, out_dtype=MXU_DTYPE)
    g["w_v"] = _mm("mm_v_g", s["ckv_n"], dv, ta=True, out_dtype=MXU_DTYPE)
    dckv_n = _mm("mm_v_t", dv, w["w_v"], tb=True, add=_mm("mm_kn_t", dkn, w["w_k"], tb=True))
    dproj, g["q_norm_w"] = rw("rms_q_bwd", rms_bwd_nores, tr=tr,
                              rows=[(proj, MLA_Q_LORA, geo.cb("c_q")), (dcq_n, MLA_Q_LORA, 0)],
                              vecs=[(w["q_norm_w"], MLA_Q_LORA, 0)],
                              outs=[(geo.pw, MLA_Q_LORA, MXU_DTYPE, geo.cb("c_q"), dproj)], reds=[(MLA_Q_LORA, MLA_Q_LORA)])
    dproj, g["kv_norm_w"] = rw("rms_kv_bwd", rms_bwd_nores, tr=tr,
                               rows=[(proj, MLA_KV_LORA, geo.cb("c_kv")), (dckv_n, MLA_KV_LORA, 0)],
                               vecs=[(w["kv_norm_w"], MLA_KV_LORA, 0)],
                               outs=[(geo.pw, MLA_KV_LORA, MXU_DTYPE, geo.cb("c_kv"), dproj)],
                               reds=[(MLA_KV_LORA, MLA_KV_LORA)])
    gw_ = SSM_D_INNER // SSM_GROUPS
    d_skip_full = w["d_skip_full"] if mid is None else w["d_skip_full"] + mid(g)[0, 0]

    def gate_norm_bwd(y, x, z, dy, dsk, nw):
        f = lambda y_, x_, z_, dsk_, nw_: _rms((y_ + x_ * dsk_) * _silu(z_), nw_)
        _, vjp = jax.vjp(f, y.astype(F32), x.astype(F32), z.astype(F32), dsk, nw)
        dy_, dx_, dz_, ddsk, dnw = vjp(dy.astype(F32))
        return dy_, dx_, dz_, ddsk, dnw

    dy_ssd, dxs_skip, dproj, g["d_skip_full"], g["ssm_norm_w"] = rw(
        "ssm_gate_norm_bwd", gate_norm_bwd, tr=tr, ncb=SSM_GROUPS,
        rows=[(s["y_ssd"], gw_, 0), (s["xc"], gw_, 0), (proj, gw_, geo.col["z"][0] // gw_), (dy_ssm, gw_, 0)],
        vecs=[(d_skip_full, gw_, 0), (w["ssm_norm_w"], gw_, 0)],
        outs=[(SSM_D_INNER, gw_, MXU_DTYPE), (SSM_D_INNER, gw_, MXU_DTYPE),
              (geo.pw, gw_, MXU_DTYPE, geo.col["z"][0] // gw_, dproj)],
        reds=[(SSM_D_INNER, gw_), (SSM_D_INNER, gw_)])
    dxc, dproj, g["dt_bias"], g["a_log"] = _ssd_bwd_g(geo, s["xc"], s["proj_dt"], w["dt_bias"], w["a_log"], s["s_prev"],
                                                     dy_ssd, dxs_skip, dproj)
    dproj, g["conv_w"], g["conv_b"] = _conv_bwd(geo, proj, w["conv_w"], w["conv_b"], dxc, dproj)
    g["w_in_pt"] = _mm("mm_in_g", dproj, s["u"], ta=True, out_dtype=MXU_DTYPE)
    du = _mm("mm_in_t", dproj, w["w_in_pt"], dep=None if tail is None else tail(g))
    dh, g["norm_mix_w"] = rw("rms_mix_bwd", rms_bwd, tr=tr,
                             rows=[(s["h"], D_MODEL, 0), (du, D_MODEL, 0), (dh2, D_MODEL, 0)],
                             vecs=[(w["norm_mix_w"], D_MODEL, 0)], outs=[(D_MODEL, D_MODEL, F32)],
                             reds=[(D_MODEL, D_MODEL)])
    return dh, g


def _loss_bwd(geo, h, fw, target, tab):
    tr = geo.tr

    def fn(x, tgt, gw, tok):
        def lossf(x_, gw_):
            err = jnp.square(_rms(x_, gw_) - tgt)
            return 0.5 * jnp.sum(tok * jnp.mean(err, axis=-1, keepdims=True), axis=0, keepdims=True)

        val, vjp = jax.vjp(lossf, x, gw)
        dx, dgw = vjp(jnp.ones((1, 1), F32))
        return dx, jnp.broadcast_to(val, (1, LANE)), dgw

    return _rowwise("loss", fn, nrows=geo.nrows, tr=tr, rows=[(h, D_MODEL, 0), (target, D_MODEL, 0)],
                    vecs=[(fw, D_MODEL, 0)], tabs=[(tab["token"], 1, 0)], outs=[(D_MODEL, D_MODEL, F32)],
                    reds=[(LANE, LANE), (D_MODEL, D_MODEL)], tab_blocks=geo.lp // tr)


def kernel(x, meta_tokens, norm_mix_w, w_in, conv_w, conv_b, dt_bias, a_log, d_skip, ssm_norm_w, q_norm_w, kv_norm_w, w_uq, w_ukv, w_branch_ssm, w_branch_mla, w_out, norm_mlp_w, w_mlp_up, w_mlp_down, final_norm_w, loss_target, m_meta_tokens, m_norm_mix_w, m_w_in, m_conv_w, m_conv_b, m_dt_bias, m_a_log, m_d_skip, m_ssm_norm_w, m_q_norm_w, m_kv_norm_w, m_w_uq, m_w_ukv, m_w_branch_ssm, m_w_branch_mla, m_w_out, m_norm_mlp_w, m_w_mlp_up, m_w_mlp_down, m_final_norm_w, v_meta_tokens, v_norm_mix_w, v_w_in, v_conv_w, v_conv_b, v_dt_bias, v_a_log, v_d_skip, v_ssm_norm_w, v_q_norm_w, v_kv_norm_w, v_w_uq, v_w_ukv, v_w_branch_ssm, v_w_branch_mla, v_w_out, v_norm_mlp_w, v_w_mlp_up, v_w_mlp_down, v_final_norm_w):
    args = dict(locals())
    turn = lambda n, a: jnp.swapaxes(a, 1, 2) if n == "w_in" else a
    wts = {n: turn(n, args[n]) for n in WEIGHTS}
    mom = {n: turn(n, args["m_" + n]) for n in WEIGHTS}
    var = {n: turn(n, args["v_" + n]) for n in WEIGHTS}
    bsz, seq, _ = x.shape
    depth = w_in.shape[0]
    geo = _Geo(bsz, seq)
    tab = _tables(geo)

    big_names = [n for n, _ in BIG]
    sh_names = big_names + [n for n, _ in SHARDED_F32]
    kinds = dict(BIG + SHARDED_F32, w_in="row")
    shard3 = lambda a: a.reshape((1,) + a.shape) if a.ndim == 2 else a
    wire = {n: (MXU_DTYPE if n in big_names else F32) for n in sh_names}
    cast = {n: shard3(wts[n]).astype(wire[n]) for n in sh_names}
    per_layer = [n for n in sh_names if n != "meta_tokens"]
    small_names = ["norm_mix_w", "conv_b", "dt_bias", "a_log", "d_skip", "ssm_norm_w", "q_norm_w", "kv_norm_w",
                   "norm_mlp_w"]

    def gather_items(pairs):
        ins, outs, items, forms = [], [], [], []
        for n, i in pairs:
            a, b = cast[n].shape[1:]
            shape, dst, form = _gather_plan(a, b, kinds[n])
            items.append((len(ins), len(outs), (lambda ref, p, i=i: ref.at[i]), dst))
            ins.append(cast[n])
            outs.append(jax.ShapeDtypeStruct(shape, wire[n]))
            forms.append(form)
        return ins, outs, items, forms

    def whole_weights(pairs, forms, got):
        by_layer = {}
        for (n, i), form, g in zip(pairs, forms, got):
            if n == "w_in":
                n, g = "w_in_pt", _w_in_assemble(geo, g)
            elif form == "row":
                g = g.reshape(g.shape[0] * g.shape[1], g.shape[2])
            elif form == "stack":
                g = _unshard(g, "col")
            by_layer.setdefault(i, {})[n] = g
        return by_layer

    def prep(i, whole, token=None):
        wl = dict(whole)
        wl.update({n: wts[n][i] for n in small_names})
        if token is not None:
            wl["norm_mix_w"] = wl["norm_mix_w"] + token[0, 0]
        return _prep_layer(geo, wl)

    early = ("w_in", "conv_w")
    late_names = [n for n in per_layer if n not in early]
    pairs1 = [(n, i) for i in range(1, depth) for n in per_layer]
    groups = [[(n, 0) for n in early] + [("meta_tokens", 0)], [(n, 0) for n in late_names]] + ([pairs1] if pairs1 else [])
    started = {}

    def gather_start(gi, dep=None):
        ins, outs, items, forms = gather_items(groups[gi])
        sems, thru, landing, token = _exchange_start("gather_w%d_start" % gi, ins, outs, items, dep)
        started[gi] = (groups[gi], forms, sems, thru, landing, items)
        return token

    def gathered(gi, after):
        pairs, forms, sems, thru, landing, items = started[gi]
        return whole_weights(pairs, forms, _exchange_wait("gather_w%d_wait" % gi, sems, thru, landing, items, after))

    def late0(after):
        whole = gathered(1, after)[0]
        if pairs1:
            whole["q_norm_w"] = wts["q_norm_w"][0] + gather_start(2, whole["w_out"])[0, 0]
        return _prep_layer(geo, whole)

    token = gather_start(1, gather_start(0))
    whole0 = gathered(0, token)[0]
    meta_full = whole0.pop("meta_tokens")

    meta = jnp.broadcast_to(meta_full[None], (bsz, N_META, D_MODEL))
    h = jnp.concatenate([jnp.zeros((bsz, geo.pad, D_MODEL), F32), meta, x], axis=1).reshape(geo.nrows, D_MODEL)
    target = jnp.concatenate([jnp.zeros((bsz, geo.pad + N_META, D_MODEL), F32), loss_target], axis=1)
    target = target.reshape(geo.nrows, D_MODEL)
    layers, saved = [], []
    for i in range(depth):
        if i == 0:
            w, late = prep(0, whole0, token), late0
        else:
            if i == 1:
                whole1 = gathered(2, h)
            w, late = prep(i, whole1[i]), None
        h, s, w = _layer_fwd(geo, h, w, tab, late)
        layers.append(w)
        saved.append(s)
    dh, loss_part, g_final = _loss_bwd(geo, h, final_norm_w.reshape(1, -1), target, tab)

    def scatter_items(pairs):
        ins, outs, items = [], [], []
        for n, i in pairs:
            a, b = cast[n].shape[1:]
            arr = g_meta if n == "meta_tokens" else grads[i]["w_in_pt" if n == "w_in" else n]
            if n == "w_in":
                arr, src = _w_in_split(geo, arr, a), _entry
            elif kinds[n] == "row":
                src = lambda ref, p, a=a: ref.at[pl.ds(pl.multiple_of(p * a, a), a)]
            elif b % LANE == 0:
                src = lambda ref, p, b=b: ref.at[:, pl.ds(pl.multiple_of(p * b, b), b)]
            else:
                arr, src = _shard(arr, "col"), _entry
            items.append((len(ins), len(outs), src, _entry))
            ins.append(arr.astype(wire[n]))
            outs.append(jax.ShapeDtypeStruct((N_DEV, a, b), wire[n]))
        return ins, outs, items

    grads = [None] * depth
    landed, pending, res = {}, {}, {}

    def scatter_start(name, pairs):
        ins, outs, items = scatter_items(pairs)
        sems, thru, landing, token = _exchange_start(name + "_start", ins, outs, items)
        pending[name] = (pairs, sems, thru, landing, items)
        return token

    def scatter_wait(name, after):
        pairs, sems, thru, landing, items = pending[name]
        landed.update(zip(pairs, _exchange_wait(name + "_wait", sems, thru, landing, items, after)))

    def adam(n):
        parts = [landed[(n, i)] for i in range(cast[n].shape[0])]
        r = _adamw_nat("adamw_" + n, parts, shard3(wts[n]), shard3(mom[n]), shard3(var[n]))
        res[n] = [a.reshape(wts[n].shape) for a in r]

    def mid0(g):
        grads[0] = _unprep_grads(geo, g)
        return scatter_start("scatter_gb0", [(n, 0) for n in late_names])

    def tail0(g):
        grads[0] = _unprep_grads(geo, g)
        return scatter_start("scatter_ga0", [(n, 0) for n in early])

    dep = None
    for i in reversed(range(depth)):
        dh, gl = _layer_bwd(geo, dh, saved[i], layers[i], tab, *((mid0, tail0) if i == 0 else (None, None)), dep)
        grads[i] = _unprep_grads(geo, gl)
        if i == 1:
            dep = scatter_start("scatter_g1", pairs1)
    dh = dh.reshape(bsz, geo.lp, D_MODEL)
    grad_x = dh[:, geo.pad + N_META:]
    g_meta = jnp.sum(dh[:, geo.pad:geo.pad + N_META], axis=0)
    if pairs1:
        scatter_wait("scatter_g1", g_meta)
    scatter_wait("scatter_gb0", g_meta)
    for n in late_names:
        adam(n)
    g_small = {n: jnp.stack([grads[i][n] for i in range(depth)]) for n in SMALL if n != "final_norm_w"}
    g_small["final_norm_w"] = g_final.reshape(-1)
    zero = jnp.zeros((1,), F32)
    pk = lambda d, last: _pack([d[n] for n in SMALL] + [last], F32, row_mult=8)
    packed = pk(g_small, loss_part[0, :1])
    ins, outs, items = scatter_items([("meta_tokens", 0)])
    parts, landed[("meta_tokens", 0)] = _exchange(
        "gather_g", [packed] + ins + [res[n][1] for n in late_names],
        [jax.ShapeDtypeStruct((N_DEV,) + packed.shape, F32)] + outs,
        [(0, 0, _whole, _entry)] + [(1, 1, items[0][2], items[0][3])])
    adam("meta_tokens")
    scatter_wait("scatter_ga0", res["meta_tokens"][1])
    for n in early:
        adam(n)
    res_sm = _adamw("adamw_small", parts, pk(wts, zero), pk(mom, zero), pk(var, zero))
    res_sm = [_unpack(r, [wts[n].shape for n in SMALL] + [(1,)]) for r in res_sm]
    loss = res_sm[0][-1][0]

    out = [loss, grad_x]
    for k in range(4):
        named = {n: res[n][k] for n in sh_names}
        named.update(zip(SMALL, res_sm[k]))
        out += [turn(n, named[n]) for n in WEIGHTS]
    return tuple(out)
```

```python
import functools

import numpy as np
import jax
import jax.numpy as jnp
from jax import lax
from jax.experimental import pallas as pl
from jax.experimental.pallas import tpu as pltpu

F32 = jnp.float32
MXU_DTYPE = jnp.bfloat16

D_MODEL = 1024
N_META = 16
EPS = 1e-6
SSM_D_INNER = 2048
SSM_HEAD_DIM = 64
SSM_GROUPS = 4
SSM_STATE = 128
SSM_CONV = 4
SSM_CHUNK = 128
MLA_HEADS = 8
MLA_Q_LORA = 512
MLA_KV_LORA = 256
MLA_NOPE = 128
MLA_ROPE = 64
MLA_V = 128
ROPE_THETA = 10000.0
D_FF = 4096
ADAM_LR = 0.001
ADAM_B1 = 0.9
ADAM_B2 = 0.999
ADAM_EPS = 1e-08
ADAM_WD = 0.01
ADAM_STEP = 10

N_DEV = 8
ATT_BLK = 256
LANE = 128
PACK_W = 1024
VMEM_LIMIT = 56 * 1024 * 1024
MESH_ID = pl.DeviceIdType.MESH

BIG = (("w_in", "col"), ("w_uq", "col"), ("w_ukv", "col"), ("w_branch_ssm", "row"), ("w_branch_mla", "row"),
       ("w_out", "row"), ("w_mlp_up", "col"), ("w_mlp_down", "row"))
SHARDED_F32 = (("conv_w", "col"), ("meta_tokens", "col"))
SMALL = ("norm_mix_w", "conv_b", "dt_bias", "a_log", "d_skip", "ssm_norm_w", "q_norm_w", "kv_norm_w",
         "norm_mlp_w", "final_norm_w")
WEIGHTS = ("meta_tokens", "norm_mix_w", "w_in", "conv_w", "conv_b", "dt_bias", "a_log", "d_skip", "ssm_norm_w",
           "q_norm_w", "kv_norm_w", "w_uq", "w_ukv", "w_branch_ssm", "w_branch_mla", "w_out", "norm_mlp_w",
           "w_mlp_up", "w_mlp_down", "final_norm_w")


def _cparams(sem=None):
    return pltpu.CompilerParams(dimension_semantics=sem, vmem_limit_bytes=VMEM_LIMIT)


def _pick(n, cands):
    for c in cands:
        if n % c == 0:
            return c
    return n


def _sigmoid(x):
    return 1.0 / (1.0 + jnp.exp(-x))


def _silu(x):
    return x * _sigmoid(x)


def _softplus(x):
    t = jnp.exp(-jnp.abs(x))
    return jnp.maximum(x, 0.0) + jnp.where(t < 0.01, t * (1.0 - t * (0.5 - t * (1.0 / 3.0))), jnp.log(1.0 + t))


def _rms(x, w):
    x = x.astype(F32)
    return x * lax.rsqrt(jnp.mean(x * x, axis=-1, keepdims=True) + EPS) * w


def _dot(a, b, ca, cb, precision=None):
    return lax.dot_general(a, b, (((ca,), (cb,)), ((), ())), preferred_element_type=F32, precision=precision)


def _mxdot(a, b, ca, cb):
    return _dot(a.astype(MXU_DTYPE), b.astype(MXU_DTYPE), ca, cb)


def _mm(name, a, b, *, ta=False, tb=False, add=None, out_dtype=F32, dep=None, epi=None, side=None):
    (kdim, m) = a.shape if ta else a.shape[::-1]
    (n, k2) = b.shape if tb else b.shape[::-1]
    assert kdim == k2, (name, a.shape, b.shape)
    tm = _pick(m, (1152, 1088, 1024, 768, 544, 512, 384, 256, 128))
    tn = _pick(n, (1024, 512, 384, 256, 128))
    tk = _pick(kdim, (1152, 1088, 1024, 768, 544, 512, 384, 256, 128))
    nk = kdim // tk
    a_spec = pl.BlockSpec((tk, tm), lambda i, j, k: (k, i)) if ta else pl.BlockSpec((tm, tk), lambda i, j, k: (i, k))
    b_spec = pl.BlockSpec((tn, tk), lambda i, j, k: (j, k)) if tb else pl.BlockSpec((tk, tn), lambda i, j, k: (k, j))
    o_spec = pl.BlockSpec((tm, tn), lambda i, j, k: (i, j))
    ca, cb = (0 if ta else 1), (1 if tb else 0)

    out_dtypes = [out_dtype] if epi is None else list(epi[1])
    n_out = len(out_dtypes)
    n_side = 0 if side is None else 1

    def body(*refs):
        a_ref, b_ref = refs[:2]
        o_refs, acc = refs[-1 - n_side - n_out:-1 - n_side], refs[-1]
        k = pl.program_id(2)

        @pl.when(k == 0)
        def _():
            acc[...] = jnp.zeros_like(acc)

        acc[...] += _mxdot(a_ref[...], b_ref[...], ca, cb)

        @pl.when(k == nk - 1)
        def _():
            r = acc[...]
            if epi is not None:
                res = epi[0](r, refs[2][...]) if add is not None else epi[0](r)
            else:
                res = (r + refs[2][...].astype(F32) if add is not None else r,)
            for o_ref, val in zip(o_refs, res):
                o_ref[...] = val.astype(o_ref.dtype)

        if side is not None:
            @pl.when(jnp.logical_and(k == nk - 1, pl.program_id(1) == side[0] // tn))
            def _():
                refs[-2][...] = acc[:, side[0] % tn:side[0] % tn + side[1]]

    in_specs, args = [a_spec, b_spec], [a, b]
    if add is not None:
        in_specs.append(o_spec)
        args.append(add)
    if dep is not None:
        in_specs.append(pl.BlockSpec((8, LANE), lambda i, j, k: (0, 0)))
        args.append(dep)
    out_specs = [o_spec] * n_out
    out_shape = [jax.ShapeDtypeStruct((m, n), dt) for dt in out_dtypes]
    if side is not None:
        assert side[0] % tn + side[1] <= tn
        out_specs.append(pl.BlockSpec((tm, side[1]), lambda i, j, k: (i, 0)))
        out_shape.append(jax.ShapeDtypeStruct((m, side[1]), F32))
    res = pl.pallas_call(
        body, name=name, grid=(m // tm, n // tn, nk), in_specs=in_specs, out_specs=out_specs, out_shape=out_shape,
        scratch_shapes=[pltpu.VMEM((tm, tn), F32)],
        compiler_params=_cparams(("parallel", "arbitrary" if side is not None else "parallel", "arbitrary")))(*args)
    return res[0] if epi is None and side is None else res


def _rowwise(name, fn, *, nrows, tr, ncb=1, rows=(), fixed=(), vecs=(), tabs=(), outs=(), reds=(), tab_blocks=1):
    in_specs, args = [], []
    for arr, w, c0 in rows:
        in_specs.append(pl.BlockSpec((tr, w), lambda g, i, c0=c0: (i, c0 + g)))
        args.append(arr)
    for arr, w, c0 in fixed:
        in_specs.append(pl.BlockSpec((tr, w), lambda g, i, c0=c0: (i, c0)))
        args.append(arr)
    for arr, w, c0 in vecs:
        in_specs.append(pl.BlockSpec((1, w), lambda g, i, c0=c0: (0, c0 + g)))
        args.append(arr)
    for arr, w, c0 in tabs:
        in_specs.append(pl.BlockSpec((tr, w), lambda g, i, c0=c0: (i % tab_blocks, c0)))
        args.append(arr)
    n_in, n_out = len(args), len(outs)
    out_shape, out_specs, aliases = [], [], {}
    for k, o in enumerate(outs):
        c0 = o[3] if len(o) > 3 else 0
        out_shape.append(jax.ShapeDtypeStruct((nrows, o[0]), o[2]))
        out_specs.append(pl.BlockSpec((tr, o[1]), lambda g, i, c0=c0: (i, c0 + g)))
        if len(o) > 4:
            aliases[len(args)] = k
            in_specs.append(pl.BlockSpec(memory_space=pl.ANY))
            args.append(o[4])
    out_shape += [jax.ShapeDtypeStruct((1, wt), F32) for wt, w in reds]
    out_specs += [pl.BlockSpec((1, w), lambda g, i: (0, g)) for wt, w in reds]
    first_out = len(args)

    def body(*refs):
        res = fn(*[r[...] for r in refs[:n_in]])
        for o_ref, val in zip(refs[first_out:first_out + n_out], res[:n_out]):
            o_ref[...] = val.astype(o_ref.dtype)
        i = pl.program_id(1)
        for d_ref, val in zip(refs[first_out + n_out:], res[n_out:]):
            @pl.when(i == 0)
            def _(d_ref=d_ref, val=val):
                d_ref[...] = val

            @pl.when(i > 0)
            def _(d_ref=d_ref, val=val):
                d_ref[...] += val

    return pl.pallas_call(
        body, name=name, grid=(ncb, nrows // tr), in_specs=in_specs, out_specs=out_specs, out_shape=out_shape,
        input_output_aliases=aliases, compiler_params=_cparams(("parallel", "arbitrary")))(*args)


def _peer(k):
    x, y, c = lax.axis_index("x"), lax.axis_index("y"), lax.axis_index("c")
    px = jnp.where((k >> 2) & 1, 1 - x, x)
    py = jnp.where((k >> 1) & 1, 1 - y, y)
    pc = jnp.where(k & 1, 1 - c, c)
    return (px, py, pc), 4 * px + 2 * py + pc


def _my_index():
    return 4 * lax.axis_index("x") + 2 * lax.axis_index("y") + lax.axis_index("c")


def _exchange(name, ins, out_shapes, items):
    n_in, n_out, n_it = len(ins), len(out_shapes), len(items)

    def body(*refs):
        x, o = refs[:n_in], refs[n_in:n_in + n_out]
        send_sems, recv_sems, local_sems = refs[n_in + n_out:]
        me = _my_index()
        local, sends = [], []
        for t, (ii, io, src, dst) in enumerate(items):
            cp = pltpu.make_async_copy(src(x[ii], me), dst(o[io], me), local_sems.at[t])
            cp.start()
            local.append(cp)
        for k in range(1, N_DEV):
            dev, idx = _peer(k)
            for t, (ii, io, src, dst) in enumerate(items):
                s = (k - 1) * n_it + t
                cp = pltpu.make_async_remote_copy(
                    src_ref=src(x[ii], idx), dst_ref=dst(o[io], me), send_sem=send_sems.at[s],
                    recv_sem=recv_sems.at[s], device_id=dev, device_id_type=MESH_ID)
                cp.start()
                sends.append(cp)
        for k in range(1, N_DEV):
            dev, idx = _peer(k)
            for t, (ii, io, src, dst) in enumerate(items):
                s = (k - 1) * n_it + t
                pltpu.make_async_remote_copy(
                    src_ref=src(x[ii], idx), dst_ref=dst(o[io], idx), send_sem=send_sems.at[s],
                    recv_sem=recv_sems.at[s], device_id=dev, device_id_type=MESH_ID).wait_recv()
        for cp in sends:
            cp.wait_send()
        for cp in local:
            cp.wait()

    nsem = (N_DEV - 1) * n_it
    anyspec = pl.BlockSpec(memory_space=pl.ANY)
    return pl.pallas_call(
        body, name=name, out_shape=list(out_shapes), in_specs=[anyspec] * n_in, out_specs=[anyspec] * n_out,
        scratch_shapes=[pltpu.SemaphoreType.DMA((nsem,)), pltpu.SemaphoreType.DMA((nsem,)),
                        pltpu.SemaphoreType.DMA((n_it,))],
        compiler_params=pltpu.CompilerParams(has_side_effects=True))(*ins)


def _split_copies(x, land, send_sems, recv_sems, items, receive):
    me = _my_index()
    remote, n_it = [], len(items)
    for k in range(1, N_DEV):
        dev, idx = _peer(k)
        for t, (ii, io, src, dst) in enumerate(items):
            s = (k - 1) * n_it + t
            remote.append(pltpu.make_async_remote_copy(
                src_ref=src(x[ii], idx), dst_ref=dst(land[io], idx if receive else me), send_sem=send_sems.at[s],
                recv_sem=recv_sems.at[s], device_id=dev, device_id_type=MESH_ID))
    local = [pltpu.make_async_copy(src(x[ii], me), dst(land[io], me), send_sems.at[(N_DEV - 1) * n_it + t])
             for t, (ii, io, src, dst) in enumerate(items)]
    return remote, local


def _exchange_start(name, ins, out_shapes, items, dep=None):
    n_in, n_out, n_it = len(ins), len(out_shapes), len(items)

    def body(*refs):
        x, land = refs[:n_in], refs[n_in:n_in + n_out]
        first_out = n_in + n_out + (dep is not None)
        send_sems, recv_sems, token = refs[first_out], refs[first_out + 1], refs[-1]
        remote, local = _split_copies(x, land, send_sems, recv_sems, items, False)
        for cp in remote + local:
            cp.start()
        token[...] = jnp.zeros_like(token)

    hbm = pl.BlockSpec(memory_space=pltpu.HBM)
    sem = pl.BlockSpec(memory_space=pltpu.SEMAPHORE)
    arrs = [pltpu.with_memory_space_constraint(a, pltpu.HBM)
            for a in list(ins) + [lax.empty(s.shape, s.dtype) for s in out_shapes]]
    res = pl.pallas_call(
        body, name=name,
        out_shape=(pltpu.SemaphoreType.DMA((N_DEV * n_it,)), pltpu.SemaphoreType.DMA(((N_DEV - 1) * n_it,)),
                   *[pltpu.HBM(a.shape, a.dtype) for a in arrs], jax.ShapeDtypeStruct((8, LANE), F32)),
        in_specs=[hbm] * (n_in + n_out) + ([] if dep is None else [pl.BlockSpec(memory_space=pl.ANY)]),
        out_specs=(sem, sem, *[hbm] * (n_in + n_out), pl.BlockSpec(memory_space=pltpu.VMEM)),
        input_output_aliases={i: 2 + i for i in range(n_in + n_out)},
        compiler_params=pltpu.CompilerParams(has_side_effects=pltpu.SideEffectType.DATAFLOW_SIDE_EFFECTING))(
            *arrs, *([] if dep is None else [dep]))
    return res[:2], res[2:2 + n_in], res[2 + n_in:2 + n_in + n_out], res[-1]


def _exchange_wait(name, sems, ins, landing, items, after):
    n_in, n_out = len(ins), len(landing)

    def body(*refs):
        x, land = refs[:n_in], refs[n_in:n_in + n_out]
        send_sems, recv_sems = refs[n_in + n_out], refs[n_in + n_out + 1]
        remote, local = _split_copies(x, land, send_sems, recv_sems, items, True)
        for cp in remote:
            cp.wait_send()
            cp.wait_recv()
        for cp in local:
            cp.wait()

    hbm = pl.BlockSpec(memory_space=pltpu.HBM)
    sem = pl.BlockSpec(memory_space=pltpu.SEMAPHORE)
    arrs = list(ins) + list(landing)
    res = pl.pallas_call(
        body, name=name, out_shape=tuple(pltpu.HBM(a.shape, a.dtype) for a in arrs),
        in_specs=[hbm] * (n_in + n_out) + [sem, sem, pl.BlockSpec(memory_space=pl.ANY)],
        out_specs=tuple([hbm] * (n_in + n_out)), input_output_aliases={i: i for i in range(n_in + n_out)},
        compiler_params=pltpu.CompilerParams(has_side_effects=pltpu.SideEffectType.DATAFLOW_SIDE_EFFECTING))(
            *arrs, *sems, after)
    return res[n_in:]


def _whole(ref, p):
    return ref


def _entry(ref, p):
    return ref.at[p]


def _gather_plan(a, b, kind):
    if kind == "col" and b % LANE == 0:
        return (a, N_DEV * b), (lambda ref, p: ref.at[:, pl.ds(pl.multiple_of(p * b, b), b)]), "col"
    return (N_DEV, a, b), _entry, ("row" if kind == "row" else "stack")


def _adamw_nat(name, parts, w, m, v):
    depth, b, c = w.shape
    assert len(parts) == depth
    tb = _pick(b, (128, 64, 32, 16, 8))
    if tb == b and b > 256:
        tb = 256
    spec = pl.BlockSpec((1, tb, c), lambda i, j: (i, j, 0))

    def body(*refs):
        p_refs = refs[:depth]
        w_ref, m_ref, v_ref, g_ref, d_ref, nm_ref, nv_ref = refs[depth:]
        for layer, p_ref in enumerate(p_refs):
            @pl.when(pl.program_id(0) == layer)
            def _(p_ref=p_ref):
                g = p_ref[0].astype(F32)
                for j in range(1, N_DEV):
                    g = g + p_ref[j].astype(F32)
                nm = ADAM_B1 * m_ref[0] + (1.0 - ADAM_B1) * g
                nv = ADAM_B2 * v_ref[0] + (1.0 - ADAM_B2) * jnp.square(g)
                m_hat = nm / (1.0 - ADAM_B1 ** ADAM_STEP)
                v_hat = nv / (1.0 - ADAM_B2 ** ADAM_STEP)
                g_ref[0] = g
                d_ref[0] = -ADAM_LR * (m_hat / (jnp.sqrt(v_hat) + ADAM_EPS) + ADAM_WD * w_ref[0])
                nm_ref[0] = nm
                nv_ref[0] = nv

    sds = jax.ShapeDtypeStruct((depth, b, c), F32)
    return pl.pallas_call(
        body, name=name, grid=(depth, pl.cdiv(b, tb)),
        in_specs=[pl.BlockSpec((N_DEV, tb, c), lambda i, j: (0, j, 0))] * depth + [spec, spec, spec],
        out_specs=[spec] * 4, out_shape=[sds] * 4, compiler_params=_cparams(("parallel", "parallel")))(*parts, w, m, v)


def _adamw(name, parts, w, m, v):
    rows = w.shape[0]
    tr = _pick(rows, (256, 128, 64, 32, 16, 8))
    spec = pl.BlockSpec((tr, PACK_W), lambda i: (i, 0))

    def body(p_ref, w_ref, m_ref, v_ref, g_ref, d_ref, nm_ref, nv_ref):
        g = p_ref[0]
        for j in range(1, N_DEV):
            g = g + p_ref[j]
        nm = ADAM_B1 * m_ref[...] + (1.0 - ADAM_B1) * g
        nv = ADAM_B2 * v_ref[...] + (1.0 - ADAM_B2) * jnp.square(g)
        m_hat = nm / (1.0 - ADAM_B1 ** ADAM_STEP)
        v_hat = nv / (1.0 - ADAM_B2 ** ADAM_STEP)
        g_ref[...] = g
        d_ref[...] = -ADAM_LR * (m_hat / (jnp.sqrt(v_hat) + ADAM_EPS) + ADAM_WD * w_ref[...])
        nm_ref[...] = nm
        nv_ref[...] = nv

    sds = jax.ShapeDtypeStruct((rows, PACK_W), F32)
    return pl.pallas_call(
        body, name=name, grid=(rows // tr,),
        in_specs=[pl.BlockSpec((N_DEV, tr, PACK_W), lambda i: (0, i, 0)), spec, spec, spec],
        out_specs=[spec] * 4, out_shape=[sds] * 4, compiler_params=_cparams(("parallel",)))(parts, w, m, v)


def _pack(arrs, dtype, row_mult=16):
    flat = jnp.concatenate([a.reshape(-1).astype(dtype) for a in arrs])
    unit = row_mult * PACK_W
    total = -(-flat.shape[0] // unit) * unit
    flat = jnp.pad(flat, (0, total - flat.shape[0]))
    return flat.reshape(-1, PACK_W)


def _pack_lead(arrs, dtype, row_mult):
    flat = jnp.concatenate([a.reshape(N_DEV, -1).astype(dtype) for a in arrs], axis=1)
    unit = row_mult * PACK_W
    total = -(-flat.shape[1] // unit) * unit
    flat = jnp.pad(flat, ((0, 0), (0, total - flat.shape[1])))
    return flat.reshape(N_DEV, -1, PACK_W)


def _unpack(buf, shapes, lead=()):
    flat = buf.reshape(lead + (-1,))
    out, off = [], 0
    for s in shapes:
        n = int(np.prod(s))
        out.append(flat[..., off:off + n].reshape(lead + tuple(s)))
        off += n
    return out


def _unshard(g, kind):
    if kind == "col":
        g = jnp.moveaxis(g, 0, -2)
        return g.reshape(g.shape[:-2] + (g.shape[-2] * g.shape[-1],))
    g = jnp.moveaxis(g, 0, 1)
    return g.reshape((g.shape[0], g.shape[1] * g.shape[2]) + g.shape[3:])


def _shard(full, kind):
    if kind == "col":
        s = full.reshape(full.shape[:-1] + (N_DEV, full.shape[-1] // N_DEV))
        return jnp.moveaxis(s, -2, 0)
    s = full.reshape((full.shape[0], N_DEV, full.shape[1] // N_DEV) + full.shape[2:])
    return jnp.moveaxis(s, 1, 0)


class _Geo:
    def __init__(self, bsz, seq):
        self.bsz, self.seq = bsz, seq
        self.pad = (-N_META) % SSM_CHUNK
        self.lp = self.pad + N_META + seq
        self.t0 = self.pad + N_META
        self.nq = 1 + seq // ATT_BLK
        assert self.t0 == LANE and seq % ATT_BLK == 0 and self.lp % SSM_CHUNK == 0
        self.nrows = bsz * self.lp
        self.nc = self.lp // SSM_CHUNK
        self.nh = SSM_D_INNER // SSM_HEAD_DIM
        self.gn = SSM_GROUPS * SSM_STATE
        self.cd = SSM_D_INNER + 2 * self.gn
        self.hq = MLA_HEADS * LANE
        order = (("z", SSM_D_INNER), ("g_ssm", D_MODEL), ("g_mla", D_MODEL), ("xs", SSM_D_INNER), ("bm", self.gn),
                 ("cm", self.gn), ("c_q", MLA_Q_LORA), ("c_kv", MLA_KV_LORA), ("dt", LANE), ("k_rope", LANE))
        self.col, off = {}, 0
        for nm, w in order:
            assert off % w == 0, (nm, off, w)
            self.col[nm] = (off, w)
            off += w
        self.pw = off
        assert self.nh <= LANE and MLA_ROPE == 64 and MLA_NOPE == LANE and MLA_V == LANE
        self.tr = _pick(self.lp, (1088, 768, 544, 512, 384, 272, 256, 128))
        self.tr_wide = _pick(self.lp, (544, 384, 272, 256, 128))

    def cb(self, nm):
        off, w = self.col[nm]
        return off // w

    def w_in_runs(self, shard_w):
        nh, half = self.nh, MLA_ROPE // 2
        src, pieces = 0, []
        for nm, n in (("z", SSM_D_INNER), ("xs", SSM_D_INNER), ("bm", self.gn), ("cm", self.gn), ("dt", nh),
                      ("c_q", MLA_Q_LORA), ("c_kv", MLA_KV_LORA), ("k_rope", MLA_ROPE), ("g_ssm", D_MODEL),
                      ("g_mla", D_MODEL)):
            dst = self.col[nm][0]
            if nm == "k_rope":
                pieces += [(src, half, dst), (src + half, half, dst + 2 * half)]
            else:
                pieces.append((src, n, dst))
            src += n
        assert src == shard_w * N_DEV
        runs = []
        for a, n, dst in pieces:
            for j in range(N_DEV):
                lo, hi = max(a, j * shard_w), min(a + n, (j + 1) * shard_w)
                if lo < hi:
                    runs.append((j, lo - j * shard_w, hi - lo, dst + lo - a))
        return runs


def _slot(a):
    h = MLA_ROPE // 2
    z = jnp.zeros(a.shape[:-1] + (h,), a.dtype)
    return jnp.concatenate([a[..., :h], z, a[..., h:], z], axis=-1)


def _unslot(a):
    h = MLA_ROPE // 2
    return jnp.concatenate([a[..., :h], a[..., 2 * h:3 * h]], axis=-1)


def _prep_layer(geo, wl):
    nh = geo.nh
    p = {}
    if "w_uq" in wl:
        uq = wl["w_uq"].reshape(MLA_Q_LORA, MLA_HEADS, MLA_NOPE + MLA_ROPE)
        p["w_qn"] = uq[..., :MLA_NOPE].reshape(MLA_Q_LORA, geo.hq)
        p["w_qp"] = _slot(uq[..., MLA_NOPE:]).reshape(MLA_Q_LORA, geo.hq)
    if "w_ukv" in wl:
        ukv = wl["w_ukv"].reshape(MLA_KV_LORA, MLA_HEADS, MLA_NOPE + MLA_V)
        p["w_k"] = ukv[..., :MLA_NOPE].reshape(MLA_KV_LORA, geo.hq)
        p["w_v"] = ukv[..., MLA_NOPE:].reshape(MLA_KV_LORA, geo.hq)
    for nm in ("w_in_pt", "conv_w", "w_branch_ssm", "w_branch_mla", "w_out", "w_mlp_up", "w_mlp_down"):
        if nm in wl:
            p[nm] = wl[nm]
    for nm in ("norm_mix_w", "conv_b", "ssm_norm_w", "q_norm_w", "kv_norm_w", "norm_mlp_w"):
        if nm in wl:
            p[nm] = wl[nm].reshape(1, -1)
    if "dt_bias" in wl:
        p["dt_bias"] = jnp.pad(wl["dt_bias"], (0, LANE - nh)).reshape(1, LANE)
        p["a_log"] = jnp.pad(wl["a_log"], (0, LANE - nh)).reshape(1, LANE)
        p["d_skip_full"] = jnp.repeat(wl["d_skip"], SSM_HEAD_DIM).reshape(1, SSM_D_INNER)
    return p


def _unprep_grads(geo, g):
    nh = geo.nh
    out = {}
    if "w_qn" in g:
        qn = g["w_qn"].reshape(MLA_Q_LORA, MLA_HEADS, MLA_NOPE)
        qp = _unslot(g["w_qp"].reshape(MLA_Q_LORA, MLA_HEADS, LANE))
        out["w_uq"] = jnp.concatenate([qn, qp], axis=-1).reshape(MLA_Q_LORA, -1)
    if "w_k" in g:
        wk = g["w_k"].reshape(MLA_KV_LORA, MLA_HEADS, MLA_NOPE)
        wv = g["w_v"].reshape(MLA_KV_LORA, MLA_HEADS, MLA_V)
        out["w_ukv"] = jnp.concatenate([wk, wv], axis=-1).reshape(MLA_KV_LORA, -1)
    for nm in ("w_in_pt", "w_branch_ssm", "w_branch_mla", "w_out", "w_mlp_up", "w_mlp_down", "conv_w"):
        if nm in g:
            out[nm] = g[nm]
    for nm in ("norm_mix_w", "conv_b", "ssm_norm_w", "q_norm_w", "kv_norm_w", "norm_mlp_w"):
        if nm in g:
            out[nm] = g[nm].reshape(-1)
    if "dt_bias" in g:
        out["dt_bias"] = g["dt_bias"].reshape(-1)[:nh]
        out["a_log"] = g["a_log"].reshape(-1)[:nh]
        out["d_skip"] = g["d_skip_full"].reshape(nh, SSM_HEAD_DIM).sum(-1)
    return out


def _tables(geo):
    pos = jnp.arange(geo.lp, dtype=F32) - geo.pad
    inv = ROPE_THETA ** (-jnp.arange(0, MLA_ROPE, 2, dtype=F32) / MLA_ROPE)
    ang = pos[:, None] * inv[None, :]
    cos, sin = jnp.cos(ang), jnp.sin(ang)
    z = jnp.zeros_like(cos)
    rows = jnp.arange(geo.lp)[:, None]
    return {"cos": jnp.concatenate([cos, z, cos, z], axis=-1), "sin": jnp.concatenate([-sin, z, sin, z], axis=-1),
            "valid": (rows >= geo.pad).astype(F32), "token": (rows >= geo.pad + N_META).astype(F32)}


def _w_in_assemble(geo, gathered):
    _, sw, d = gathered.shape
    runs = geo.w_in_runs(sw)
    tl = _pick(d, (256, 128))

    def body(x_ref, o_ref):
        o_ref[...] = jnp.zeros_like(o_ref)
        for j, s0, n, d0 in runs:
            o_ref[d0:d0 + n, :] = x_ref[j, s0:s0 + n, :]

    return pl.pallas_call(
        body, name="w_in_assemble", grid=(d // tl,), in_specs=[pl.BlockSpec((N_DEV, sw, tl), lambda i: (0, 0, i))],
        out_specs=pl.BlockSpec((geo.pw, tl), lambda i: (0, i)),
        out_shape=jax.ShapeDtypeStruct((geo.pw, d), gathered.dtype), compiler_params=_cparams(("parallel",)))(gathered)


def _w_in_split(geo, g_padded, sw):
    d = g_padded.shape[1]
    runs = geo.w_in_runs(sw)
    tl = _pick(d, (256, 128))

    def body(x_ref, o_ref):
        for j, s0, n, d0 in runs:
            o_ref[j, s0:s0 + n, :] = x_ref[d0:d0 + n, :]

    return pl.pallas_call(
        body, name="w_in_split", grid=(d // tl,), in_specs=[pl.BlockSpec((geo.pw, tl), lambda i: (0, i))],
        out_specs=pl.BlockSpec((N_DEV, sw, tl), lambda i: (0, 0, i)),
        out_shape=jax.ShapeDtypeStruct((N_DEV, sw, d), g_padded.dtype),
        compiler_params=_cparams(("parallel",)))(g_padded)


def _conv_cols(geo, cbw):
    x0 = geo.col["xs"][0]
    assert geo.col["bm"][0] == x0 + SSM_D_INNER and geo.col["cm"][0] == geo.col["bm"][0] + geo.gn and x0 % cbw == 0
    return lambda j: x0 // cbw + j


def _conv_taps(x):
    return [pltpu.roll(x, SSM_CONV - 1 - k, axis=0) for k in range(SSM_CONV - 1)] + [x]


def _conv_pre(x, w_ref, b_ref, taps=None):
    taps = _conv_taps(x) if taps is None else taps
    acc = b_ref[...]
    for k in range(SSM_CONV):
        acc = acc + taps[k] * w_ref[k:k + 1, :]
    return acc


def _conv_fwd(geo, proj, conv_w, conv_b):
    cbw = 256
    colmap = _conv_cols(geo, cbw)
    lp, pad = geo.lp, geo.pad

    def body(x_ref, w_ref, b_ref, o_ref):
        valid = (lax.broadcasted_iota(jnp.int32, (lp, 1), 0) >= pad).astype(F32)
        o_ref[...] = (_silu(_conv_pre(x_ref[...].astype(F32), w_ref, b_ref)) * valid).astype(o_ref.dtype)

    return pl.pallas_call(
        body, name="conv_fwd", grid=(geo.bsz, geo.cd // cbw),
        in_specs=[pl.BlockSpec((lp, cbw), lambda b, j: (b, colmap(j))),
                  pl.BlockSpec((SSM_CONV, cbw), lambda b, j: (0, j)), pl.BlockSpec((1, cbw), lambda b, j: (0, j))],
        out_specs=pl.BlockSpec((lp, cbw), lambda b, j: (b, j)),
        out_shape=jax.ShapeDtypeStruct((geo.nrows, geo.cd), MXU_DTYPE),
        compiler_params=_cparams(("parallel", "parallel")))(proj, conv_w, conv_b)


def _conv_bwd(geo, proj, conv_w, conv_b, dxc, dproj):
    cbw = 256
    colmap = _conv_cols(geo, cbw)
    lp, pad = geo.lp, geo.pad

    def body(x_ref, w_ref, b_ref, dy_ref, _, dx_ref, gw_ref, gb_ref):
        b = pl.program_id(1)
        valid = (lax.broadcasted_iota(jnp.int32, (lp, 1), 0) >= pad).astype(F32)
        taps = _conv_taps(x_ref[...].astype(F32))
        pre = _conv_pre(None, w_ref, b_ref, taps)
        sig = _sigmoid(pre)
        dpre = dy_ref[...] * (sig * (1.0 + pre * (1.0 - sig))) * valid
        dx = dpre * w_ref[SSM_CONV - 1:SSM_CONV, :]
        for k in range(SSM_CONV - 1):
            dx = dx + pltpu.roll(dpre, lp - (SSM_CONV - 1 - k), axis=0) * w_ref[k:k + 1, :]
        gws = [jnp.sum(dpre * taps[k], axis=0, keepdims=True) for k in range(SSM_CONV)]
        dx_ref[...] = (dx * valid).astype(dx_ref.dtype)

        @pl.when(b == 0)
        def _():
            gw_ref[...] = jnp.zeros_like(gw_ref)
            gb_ref[...] = jnp.zeros_like(gb_ref)

        for k in range(SSM_CONV):
            gw_ref[k:k + 1, :] += gws[k]
        gb_ref[...] += jnp.sum(dpre, axis=0, keepdims=True)

    return pl.pallas_call(
        body, name="conv_bwd", grid=(geo.cd // cbw, geo.bsz),
        in_specs=[pl.BlockSpec((lp, cbw), lambda j, b: (b, colmap(j))),
                  pl.BlockSpec((SSM_CONV, cbw), lambda j, b: (0, j)), pl.BlockSpec((1, cbw), lambda j, b: (0, j)),
                  pl.BlockSpec((lp, cbw), lambda j, b: (b, j)), pl.BlockSpec(memory_space=pl.ANY)],
        out_specs=[pl.BlockSpec((lp, cbw), lambda j, b: (b, colmap(j))),
                   pl.BlockSpec((SSM_CONV, cbw), lambda j, b: (0, j)), pl.BlockSpec((1, cbw), lambda j, b: (0, j))],
        out_shape=[jax.ShapeDtypeStruct(dproj.shape, dproj.dtype),
                   jax.ShapeDtypeStruct((SSM_CONV, geo.cd), F32), jax.ShapeDtypeStruct((1, geo.cd), F32)],
        input_output_aliases={4: 0},
        compiler_params=_cparams(("parallel", "arbitrary")))(proj, conv_w, conv_b, dxc, dproj)


def _tri(q):
    r = lax.broadcasted_iota(jnp.int32, (q, q), 0)
    c = lax.broadcasted_iota(jnp.int32, (q, q), 1)
    return r >= c


def _ssd_pre(dtr, dtb, alog, valid):
    dt = _softplus(dtr + dtb) * valid
    adt = dt * (-jnp.exp(alog))
    a_cs = _dot(_tri(SSM_CHUNK).astype(F32), adt, 1, 0, precision=lax.Precision.HIGHEST)
    return dt, a_cs


def _ssd_specs(geo, rev):
    nc, q = geo.nc, SSM_CHUNK
    ci = (lambda c: nc - 1 - c) if rev else (lambda c: c)
    nxb = SSM_D_INNER // geo.gn
    return [pl.BlockSpec((q, SSM_D_INNER), lambda b, c: (b * nc + ci(c), 0)),
            pl.BlockSpec((q, geo.gn), lambda b, c: (b * nc + ci(c), nxb)),
            pl.BlockSpec((q, geo.gn), lambda b, c: (b * nc + ci(c), nxb + 1)),
            pl.BlockSpec((q, LANE), lambda b, c: (b * nc + ci(c), 0)),
            pl.BlockSpec((1, LANE), lambda b, c: (0, 0)), pl.BlockSpec((1, LANE), lambda b, c: (0, 0))], ci


def _expand_heads(cols, nh):
    per = LANE // SSM_HEAD_DIM
    lane = lax.broadcasted_iota(jnp.int32, (1, LANE), 1)
    blocks = []
    for j in range(nh // per):
        blk = jnp.broadcast_to(cols[:, j * per:j * per + 1], (cols.shape[0], LANE))
        for k in range(1, per):
            blk = jnp.where(lane >= k * SSM_HEAD_DIM, cols[:, j * per + k:j * per + k + 1], blk)
        blocks.append(blk)
    return jnp.concatenate(blocks, axis=1)


def _head_maps(geo):
    e = (jnp.arange(SSM_D_INNER)[None, :] // SSM_HEAD_DIM == jnp.arange(LANE)[:, None]).astype(F32)
    return e, e.T


def _ssd_fwd_g(geo, xc, proj, dt_bias, a_log):
    q, p, n, e = SSM_CHUNK, SSM_HEAD_DIM, SSM_STATE, geo.nh // SSM_GROUPS
    nc, pad, gw = geo.nc, geo.pad, SSM_D_INNER // SSM_GROUPS
    in_specs, _ = _ssd_specs(geo, False)

    def body(xs_ref, b_ref, c_ref, dtr_ref, dtb_ref, alog_ref, y_ref, sp_ref, state, xdt_s, y_s):
        c = pl.program_id(1)

        @pl.when(c == 0)
        def _():
            state[...] = jnp.zeros_like(state)

        sp_ref[...] = state[...]
        inert = (c + 1) * q <= pad

        @pl.when(inert)
        def _():
            y_ref[...] = jnp.zeros_like(y_ref)

        @pl.when(jnp.logical_not(inert))
        def _():
            valid = (c * q + lax.broadcasted_iota(jnp.int32, (q, 1), 0) >= pad).astype(F32)
            dt, a_cs = _ssd_pre(dtr_ref[...], dtb_ref[...], alog_ref[...], valid)
            a_cst = a_cs.T
            dt_x, a_x = _expand_heads(dt, geo.nh), _expand_heads(a_cs, geo.nh)
            tri = _tri(q)
            for g in range(SSM_GROUPS):
                gs = slice(g * gw, (g + 1) * gw)
                bg, cg = b_ref[:, g * n:(g + 1) * n], c_ref[:, g * n:(g + 1) * n]
                a_g = a_x[:, gs]
                a_last = a_g[q - 1:q, :]
                xdt_g = xs_ref[:, gs] * dt_x[:, gs]
                xdt_s[:, gs] = xdt_g
                s_g = state[:, gs]
                y_s[:, gs] = _mxdot(cg, s_g, 1, 0) * jnp.exp(a_g)
                state[:, gs] = s_g * jnp.exp(a_last) + _mxdot(bg, xdt_g * jnp.exp(a_last - a_g), 0, 0)
                cb = _mxdot(cg, bg, 1, 1)
                for hh in range(e):
                    h = g * e + hh
                    hs = slice(h * p, (h + 1) * p)
                    ldec = jnp.exp(jnp.where(tri, a_cs[:, h:h + 1] - a_cst[h:h + 1, :], -jnp.inf))
                    y_s[:, hs] += _mxdot(cb * ldec, xdt_s[:, hs], 1, 0)
            y_ref[...] = y_s[...].astype(y_ref.dtype)

    return pl.pallas_call(
        body, name="ssd_fwd", grid=(geo.bsz, nc), in_specs=in_specs,
        out_specs=[pl.BlockSpec((q, SSM_D_INNER), lambda b, c: (b * nc + c, 0)),
                   pl.BlockSpec((n, SSM_D_INNER), lambda b, c: (b * nc + c, 0))],
        out_shape=[jax.ShapeDtypeStruct((geo.nrows, SSM_D_INNER), MXU_DTYPE),
                   jax.ShapeDtypeStruct((geo.bsz * nc * n, SSM_D_INNER), F32)],
        scratch_shapes=[pltpu.VMEM((n, SSM_D_INNER), F32), pltpu.VMEM((q, SSM_D_INNER), F32),
                        pltpu.VMEM((q, SSM_D_INNER), F32)],
        compiler_params=_cparams(("parallel", "arbitrary")))(xc, xc, xc, proj, dt_bias, a_log)


def _ssd_bwd_g(geo, xc, proj, dt_bias, a_log, s_prev_all, dy, dxs_skip, dproj):
    q, p, n, e = SSM_CHUNK, SSM_HEAD_DIM, SSM_STATE, geo.nh // SSM_GROUPS
    nc, pad, di, gn, gw = geo.nc, geo.pad, SSM_D_INNER, geo.gn, SSM_D_INNER // SSM_GROUPS
    in_specs, ci = _ssd_specs(geo, True)
    row_spec = pl.BlockSpec((q, di), lambda b, c: (b * nc + ci(c), 0))
    e_map, _ = _head_maps(geo)
    in_specs += [pl.BlockSpec((n, di), lambda b, c: (b * nc + ci(c), 0)), row_spec, row_spec,
                 pl.BlockSpec((LANE, di), lambda b, c: (0, 0)), pl.BlockSpec(memory_space=pl.ANY)]

    def body(xs_ref, b_ref, c_ref, dtr_ref, dtb_ref, alog_ref, sp_ref, dy_ref, dsk_ref, e_ref, _,
             dxc_ref, ddt_ref, gdtb_ref, galog_ref, dstate, xdt_s, dxdt_s):
        step = pl.program_id(1)
        first = jnp.logical_and(pl.program_id(0) == 0, step == 0)
        c = nc - 1 - step

        @pl.when(step == 0)
        def _():
            dstate[...] = jnp.zeros_like(dstate)

        @pl.when(first)
        def _():
            gdtb_ref[...] = jnp.zeros_like(gdtb_ref)
            galog_ref[...] = jnp.zeros_like(galog_ref)

        inert = (c + 1) * q <= pad

        @pl.when(inert)
        def _():
            dxc_ref[...] = jnp.zeros_like(dxc_ref)
            ddt_ref[...] = jnp.zeros_like(ddt_ref)

        @pl.when(jnp.logical_not(inert))
        def _():
            valid = (c * q + lax.broadcasted_iota(jnp.int32, (q, 1), 0) >= pad).astype(F32)
            dtr, dtb, alog = dtr_ref[...], dtb_ref[...], alog_ref[...]
            dt, a_cs = _ssd_pre(dtr, dtb, alog, valid)
            a_cst = a_cs.T
            dt_x, a_x = _expand_heads(dt, geo.nh), _expand_heads(a_cs, geo.nh)
            tri = _tri(q)
            lane = lax.broadcasted_iota(jnp.int32, (1, LANE), 1)
            sub = lax.broadcasted_iota(jnp.int32, (LANE, 1), 0)
            d_dt = jnp.zeros((q, LANE), F32)
            d_acs = jnp.zeros((q, LANE), F32)
            d_acst = jnp.zeros((LANE, q), F32)
            d_last = jnp.zeros((1, LANE), F32)
            for g in range(SSM_GROUPS):
                gs = slice(g * gw, (g + 1) * gw)
                bg, cg = b_ref[:, g * n:(g + 1) * n], c_ref[:, g * n:(g + 1) * n]
                seg = lambda v: _mxdot(v, e_ref[:, gs], 1, 1)
                a_g, dt_g, x_g, dy_g = a_x[:, gs], dt_x[:, gs], xs_ref[:, gs], dy_ref[:, gs]
                e_col, e_last, dec = jnp.exp(a_g), jnp.exp(a_g[q - 1:q, :]), jnp.exp(a_g[q - 1:q, :] - a_g)
                xdt_g = x_g * dt_g
                xdt_s[:, gs] = xdt_g
                s_g, ds_g = sp_ref[:, gs], dstate[:, gs]
                cs = _mxdot(cg, s_g, 1, 0)
                d_cs = dy_g * e_col
                d_acs = d_acs + seg(d_cs * cs)
                d_cg = _mxdot(d_cs, s_g, 1, 1)
                dstate[:, gs] = _mxdot(cg, d_cs, 0, 0) + ds_g * e_last
                dl_x = jnp.sum(ds_g * s_g, axis=0, keepdims=True) * e_last
                d_last = d_last + seg(jnp.broadcast_to(dl_x, (8, gw)))[:1]
                gmat = _mxdot(bg, ds_g, 1, 0)
                xd = xdt_g * dec
                d_bg = _mxdot(xd, ds_g, 1, 1)
                d_dec = seg(xd * gmat)
                d_acs = d_acs - d_dec
                d_last = d_last + jnp.sum(d_dec, axis=0, keepdims=True)
                dxdt_s[:, gs] = dec * gmat
                cb = _mxdot(cg, bg, 1, 1)
                d_cb = jnp.zeros((q, q), F32)
                for hh in range(e):
                    h = g * e + hh
                    hs = slice(h * p, (h + 1) * p)
                    ldec = jnp.exp(jnp.where(tri, a_cs[:, h:h + 1] - a_cst[h:h + 1, :], -jnp.inf))
                    dyh = dy_ref[:, hs]
                    d_m = _mxdot(dyh, xdt_s[:, hs], 1, 1)
                    dxdt_s[:, hs] += _mxdot(cb * ldec, dyh, 0, 0)
                    d_cb = d_cb + d_m * ldec
                    d_diff = d_m * cb * ldec
                    d_acs = d_acs + jnp.sum(d_diff, axis=1, keepdims=True) * (lane == h).astype(F32)
                    d_acst = d_acst - (sub == h).astype(F32) * jnp.sum(d_diff, axis=0, keepdims=True)
                d_xdt = dxdt_s[:, gs]
                dxc_ref[:, gs] = d_xdt * dt_g + dsk_ref[:, gs]
                d_dt = d_dt + seg(d_xdt * x_g)
                dxc_ref[:, di + g * n:di + (g + 1) * n] = d_bg + _mxdot(d_cb, cg, 0, 0)
                dxc_ref[:, di + gn + g * n:di + gn + (g + 1) * n] = d_cg + _mxdot(d_cb, bg, 1, 0)
            is_last = (lax.broadcasted_iota(jnp.int32, (q, 1), 0) == q - 1).astype(F32)
            d_acs = d_acs + d_acst.T + is_last * d_last
            d_adt = _dot(_tri(q).astype(F32), d_acs, 0, 0, precision=lax.Precision.HIGHEST)
            a = -jnp.exp(alog)
            d_dt = d_dt + d_adt * a
            d_dtr = d_dt * valid * _sigmoid(dtr + dtb)
            ddt_ref[...] = d_dtr.astype(ddt_ref.dtype)
            gdtb_ref[...] += jnp.sum(d_dtr, axis=0, keepdims=True)
            galog_ref[...] += jnp.sum(d_adt * dt, axis=0, keepdims=True) * a

    vec = pl.BlockSpec((1, LANE), lambda b, c: (0, 0))
    return pl.pallas_call(
        body, name="ssd_bwd", grid=(geo.bsz, nc), in_specs=in_specs,
        out_specs=[pl.BlockSpec((q, geo.cd), lambda b, c: (b * nc + ci(c), 0)),
                   pl.BlockSpec((q, LANE), lambda b, c: (b * nc + ci(c), geo.cb("dt"))), vec, vec],
        out_shape=[jax.ShapeDtypeStruct((geo.nrows, geo.cd), F32), jax.ShapeDtypeStruct(dproj.shape, dproj.dtype),
                   jax.ShapeDtypeStruct((1, LANE), F32), jax.ShapeDtypeStruct((1, LANE), F32)],
        scratch_shapes=[pltpu.VMEM((n, di), F32), pltpu.VMEM((q, di), F32), pltpu.VMEM((q, di), F32)],
        input_output_aliases={10: 1},
        compiler_params=_cparams(("arbitrary", "arbitrary")))(
            xc, xc, xc, proj, dt_bias, a_log, s_prev_all, dy, dxs_skip, e_map, dproj)


BIAS_LANE = MLA_ROPE // 2
KEY_OFF = -1e30
ATT_SCALE = (MLA_NOPE + MLA_ROPE) ** -0.5


def _row_t(col):
    return jnp.broadcast_to(col, (col.shape[0], LANE)).T[:8]


def _att_row0(geo, i):
    return pl.multiple_of(geo.t0 + (i - 1) * ATT_BLK, geo.t0)


def _attn_fwd3(geo, qn, qp, kn, kp, v):
    t, lp, t0, nq = ATT_BLK, geo.lp, geo.t0, geo.nq

    def body(qn_ref, qp_ref, kn_ref, kp_ref, v_ref, o_ref, lse_ref, k_ref):
        qi = pl.program_id(2)

        def blk(q, start, width, carry, diag):
            m, l, acc = carry
            ks = pl.ds(start, width)
            s = _mxdot(q, k_ref[ks, :], 1, 1) * ATT_SCALE
            if diag:
                s = jnp.where(_tri(width), s, -jnp.inf)
            m_new = jnp.maximum(m, jnp.max(s, axis=1, keepdims=True))
            pr = jnp.exp(s - m_new)
            alpha = jnp.exp(m - m_new)
            return m_new, alpha * l + jnp.sum(pr, axis=1, keepdims=True), alpha * acc + _mxdot(pr, v_ref[ks, :], 1, 0)

        def init(rows):
            return (jnp.full((rows, 1), 2.0 * KEY_OFF, F32), jnp.zeros((rows, 1), F32), jnp.zeros((rows, LANE), F32))

        @pl.when(qi == 0)
        def _():
            k_ref[:, :LANE] = kn_ref[...]
            k_ref[:, LANE:] = kp_ref[...]
            q = jnp.concatenate([qn_ref[0:t0, :], qp_ref[0:t0, :]], axis=1)
            m, l, acc = blk(q, 0, t0, init(t0), True)
            o_ref[0:t0, :] = (acc / l).astype(o_ref.dtype)
            lse_ref[0, 0, 0, :, 0:t0] = _row_t(m + jnp.log(l))

        @pl.when(qi > 0)
        def _():
            qs = pl.ds(_att_row0(geo, qi), t)
            q = jnp.concatenate([qn_ref[qs, :], qp_ref[qs, :]], axis=1)
            both = jnp.where(qi >= 2, 1, 0)
            carry = lax.fori_loop(0, both, lambda _, c: blk(q, 0, t0 + t, c, False), init(t))
            carry = lax.fori_loop(0, 1 - both, lambda _, c: blk(q, 0, t0, c, False), carry)
            done = 1 + both
            for ntile in (4, 2, 1):
                steps = (qi - done) // ntile
                carry = lax.fori_loop(
                    0, steps, lambda j, c, d=done, n=ntile: blk(q, _att_row0(geo, d + n * j), n * t, c, False), carry)
                done = done + steps * ntile
            m, l, acc = blk(q, _att_row0(geo, qi), t, carry, True)
            o_ref[qs, :] = (acc / l).astype(o_ref.dtype)
            lse_ref[0, 0, 0] = _row_t(m + jnp.log(l))

    seq = pl.BlockSpec((lp, LANE), lambda b, h, i: (b, h))
    return pl.pallas_call(
        body, name="attn_fwd", grid=(geo.bsz, MLA_HEADS, nq),
        in_specs=[seq, seq, seq, pl.BlockSpec((lp, LANE), lambda b, h, i: (b, 0)), seq],
        out_specs=[seq, pl.BlockSpec((1, 1, 1, 8, t), lambda b, h, i: (b, h, i, 0, 0))],
        out_shape=[jax.ShapeDtypeStruct((geo.nrows, geo.hq), MXU_DTYPE),
                   jax.ShapeDtypeStruct((geo.bsz, MLA_HEADS, nq, 8, t), F32)],
        scratch_shapes=[pltpu.VMEM((lp, 2 * LANE), MXU_DTYPE)],
        compiler_params=_cparams(("parallel", "parallel", "arbitrary")))(qn, qp, kn, kp, v)


def _attn_bwd3(geo, qn, qp, kn, kp, v, d_o, o, lse):
    t, lp, t0, nq = ATT_BLK, geo.lp, geo.t0, geo.nq

    def body(qn_ref, qp_ref, kn_ref, kp_ref, v_ref, do_ref, o_ref, lse_ref,
             dqn_ref, dqp_ref, dkn_ref, dkp_ref, dv_ref, q_ref, dl_s):
        kj = pl.program_id(2)
        lse = lse_ref.at[0, 0]

        @pl.when(kj == 0)
        def _():
            q_ref[:, :LANE] = qn_ref[...]
            q_ref[:, LANE:] = qp_ref[...]
            dqn_ref[...] = jnp.zeros_like(dqn_ref)
            dqp_ref[...] = jnp.zeros_like(dqp_ref)
            for i in range(nq):
                rows = slice(0, t0) if i == 0 else slice(t0 + (i - 1) * t, t0 + i * t)
                dl = _row_t(jnp.sum(do_ref[rows, :].astype(F32) * o_ref[rows, :].astype(F32), axis=1, keepdims=True))
                if i == 0:
                    dl_s[0, :, 0:t0] = dl
                else:
                    dl_s[i] = dl

        def rows_of(ref, qi, ntile):
            return jnp.concatenate([ref[qi + i][:1, :] for i in range(ntile)], axis=1)

        def qblk(k, vv, start, width, lrow, drow, carry, diag):
            dk, dv = carry
            qs = pl.ds(start, width)
            q, d_o_blk = q_ref[qs, :], do_ref[qs, :]
            st = _mxdot(k, q, 1, 1) * ATT_SCALE
            if diag:
                keys = lax.broadcasted_iota(jnp.int32, (width, width), 0)
                st = jnp.where(keys <= lax.broadcasted_iota(jnp.int32, (width, width), 1), st, -jnp.inf)
            pt = jnp.exp(st - lrow)
            dst = pt * (_mxdot(vv, d_o_blk, 1, 1) - drow) * ATT_SCALE
            dq = _mxdot(dst, k, 0, 0)
            dqn_ref[qs, :] += dq[:, :LANE]
            dqp_ref[qs, :] += dq[:, LANE:]
            return dk + _mxdot(dst, q, 1, 0), dv + _mxdot(pt, d_o_blk, 1, 0)

        def upper(k, vv, first, carry):
            done = first
            for ntile in (4, 2, 1):
                steps = (nq - done) // ntile
                carry = lax.fori_loop(
                    0, steps, lambda j, c, d=done, n=ntile: qblk(
                        k, vv, _att_row0(geo, d + n * j), n * t, rows_of(lse, d + n * j, n),
                        rows_of(dl_s, d + n * j, n), c, False), carry)
                done = done + steps * ntile
            return carry

        def zeros(rows):
            return jnp.zeros((rows, 2 * LANE), F32), jnp.zeros((rows, LANE), F32)

        @pl.when(kj == 0)
        def _():
            k = jnp.concatenate([kn_ref[0:t0, :], kp_ref[0:t0, :]], axis=1)
            vv = v_ref[0:t0, :]
            carry = qblk(k, vv, 0, t0, lse[0][:1, 0:t0], dl_s[0][:1, 0:t0], zeros(t0), True)
            dk, dv = upper(k, vv, 1, carry)
            dkn_ref[0:t0, :] = dk[:, :LANE].astype(dkn_ref.dtype)
            dkp_ref[0:t0, :] = dk[:, LANE:]
            dv_ref[0:t0, :] = dv.astype(dv_ref.dtype)

        @pl.when(kj > 0)
        def _():
            ks = pl.ds(_att_row0(geo, kj), t)
            k = jnp.concatenate([kn_ref[ks, :], kp_ref[ks, :]], axis=1)
            vv = v_ref[ks, :]
            carry = qblk(k, vv, _att_row0(geo, kj), t, rows_of(lse, kj, 1), rows_of(dl_s, kj, 1), zeros(t), True)
            dk, dv = upper(k, vv, kj + 1, carry)
            dkn_ref[ks, :] = dk[:, :LANE].astype(dkn_ref.dtype)
            dkp_ref[ks, :] = dk[:, LANE:]
            dv_ref[ks, :] = dv.astype(dv_ref.dtype)

    seq = pl.BlockSpec((lp, LANE), lambda b, h, j: (b, h))
    return pl.pallas_call(
        body, name="attn_bwd", grid=(geo.bsz, MLA_HEADS, nq),
        in_specs=[seq, seq, seq, pl.BlockSpec((lp, LANE), lambda b, h, j: (b, 0)), seq, seq, seq,
                  pl.BlockSpec((1, 1, nq, 8, t), lambda b, h, j: (b, h, 0, 0, 0))],
        out_specs=[seq, seq, seq, seq, seq],
        out_shape=[jax.ShapeDtypeStruct((geo.nrows, geo.hq), F32), jax.ShapeDtypeStruct((geo.nrows, geo.hq), F32),
                   jax.ShapeDtypeStruct((geo.nrows, geo.hq), MXU_DTYPE), jax.ShapeDtypeStruct((geo.nrows, geo.hq), F32),
                   jax.ShapeDtypeStruct((geo.nrows, geo.hq), MXU_DTYPE)],
        scratch_shapes=[pltpu.VMEM((lp, 2 * LANE), MXU_DTYPE), pltpu.VMEM((nq, 8, t), F32)],
        compiler_params=_cparams(("parallel", "parallel", "arbitrary")))(qn, qp, kn, kp, v, d_o, o, lse)


def _rope(x, cos, sin):
    return x * cos + pltpu.roll(x, LANE // 2, axis=1) * sin


def _rope_t(dx, cos, sin):
    return dx * cos + pltpu.roll(dx * sin, LANE // 2, axis=1)


def _per_head(f):
    def fn(x, cos, sin):
        return (jnp.concatenate([f(x[:, h * LANE:(h + 1) * LANE], cos, sin) for h in range(MLA_HEADS)], axis=1),)
    return fn


def _layer_fwd(geo, h, w, tab, late=None):
    nr, tr, trw = geo.nrows, geo.tr, geo.tr_wide
    tb = geo.lp // tr
    rw = functools.partial(_rowwise, nrows=nr)
    s = {"h": h}
    (s["u"],) = rw("rms_mix", lambda x, g: (_rms(x, g),), tr=tr, rows=[(h, D_MODEL, 0)],
                   vecs=[(w["norm_mix_w"], D_MODEL, 0)], outs=[(D_MODEL, D_MODEL, MXU_DTYPE)])
    proj, s["proj_dt"] = _mm("mm_in", s["u"], w["w_in_pt"], tb=True, out_dtype=MXU_DTYPE,
                             side=(geo.col["dt"][0], LANE))
    s["proj"] = proj
    xc = s["xc"] = _conv_fwd(geo, proj, w["conv_w"], w["conv_b"])
    s["y_ssd"], s["s_prev"] = _ssd_fwd_g(geo, xc, s["proj_dt"], w["dt_bias"], w["a_log"])
    gw = SSM_D_INNER // SSM_GROUPS

    def gate_norm(y, x, z, dsk, nw):
        return (_rms((y + x * dsk) * _silu(z.astype(F32)), nw),)

    (s["y_ssm"],) = rw("ssm_gate_norm", gate_norm, tr=tr, ncb=SSM_GROUPS,
                       rows=[(s["y_ssd"], gw, 0), (xc, gw, 0), (proj, gw, geo.col["z"][0] // gw)],
                       vecs=[(w["d_skip_full"], gw, 0), (w["ssm_norm_w"], gw, 0)], outs=[(SSM_D_INNER, gw, MXU_DTYPE)])
    if late is not None:
        w = {**w, **late(s["y_ssm"])}
    (s["cq_n"],) = rw("rms_q", lambda x, g: (_rms(x, g),), tr=tr, rows=[(proj, MLA_Q_LORA, geo.cb("c_q"))],
                      vecs=[(w["q_norm_w"], MLA_Q_LORA, 0)], outs=[(MLA_Q_LORA, MLA_Q_LORA, MXU_DTYPE)])
    (s["ckv_n"],) = rw("rms_kv", lambda x, g: (_rms(x, g),), tr=tr, rows=[(proj, MLA_KV_LORA, geo.cb("c_kv"))],
                       vecs=[(w["kv_norm_w"], MLA_KV_LORA, 0)], outs=[(MLA_KV_LORA, MLA_KV_LORA, MXU_DTYPE)])
    s["qn"] = _mm("mm_qn", s["cq_n"], w["w_qn"], out_dtype=MXU_DTYPE)
    qp_raw = _mm("mm_qp", s["cq_n"], w["w_qp"])
    s["kn"] = _mm("mm_kn", s["ckv_n"], w["w_k"], out_dtype=MXU_DTYPE)
    s["v"] = _mm("mm_v", s["ckv_n"], w["w_v"], out_dtype=MXU_DTYPE)
    bias_lane = lambda: lax.broadcasted_iota(jnp.int32, (1, LANE), 1) == BIAS_LANE
    rope_tabs = [(tab["cos"], LANE, 0), (tab["sin"], LANE, 0)]
    (s["qp"],) = rw("rope_q", _per_head(lambda xp, c, sn: jnp.where(bias_lane(), 1.0, _rope(xp, c, sn))), tr=tr,
                    rows=[(qp_raw, geo.hq, 0)], tabs=rope_tabs, outs=[(geo.hq, geo.hq, MXU_DTYPE)], tab_blocks=tb)
    (s["kp"],) = rw("rope_k", lambda xp, c, sn, valid: (jnp.where(bias_lane(), KEY_OFF * (1.0 - valid),
                                                                 _rope(xp.astype(F32), c, sn)),),
                    tr=tr, rows=[(proj, LANE, geo.cb("k_rope"))], tabs=rope_tabs + [(tab["valid"], 1, 0)],
                    outs=[(LANE, LANE, MXU_DTYPE)], tab_blocks=tb)
    s["o"], s["lse"] = _attn_fwd3(geo, s["qn"], s["qp"], s["kn"], s["kp"], s["v"])
    s["ys_p"] = _mm("mm_bs", s["y_ssm"], w["w_branch_ssm"], out_dtype=MXU_DTYPE)
    s["ym_p"] = _mm("mm_bm", s["o"], w["w_branch_mla"], out_dtype=MXU_DTYPE)

    def gate(gs, gm, ys, ym):
        return (_sigmoid(gs.astype(F32)) * ys + _sigmoid(gm.astype(F32)) * ym,)

    (s["mixed"],) = rw("gate", gate, tr=tr, rows=[(proj, D_MODEL, geo.cb("g_ssm")), (proj, D_MODEL, geo.cb("g_mla")),
                                                  (s["ys_p"], D_MODEL, 0), (s["ym_p"], D_MODEL, 0)],
                       outs=[(D_MODEL, D_MODEL, MXU_DTYPE)])
    s["h2"] = _mm("mm_out", s["mixed"], w["w_out"], add=h)
    (s["vn"],) = rw("rms_mlp", lambda x, g: (_rms(x, g),), tr=tr, rows=[(s["h2"], D_MODEL, 0)],
                    vecs=[(w["norm_mlp_w"], D_MODEL, 0)], outs=[(D_MODEL, D_MODEL, MXU_DTYPE)])
    s["up"], s["act"] = _mm("mm_up", s["vn"], w["w_mlp_up"],
                            epi=(lambda r: (r, jnp.square(jnp.maximum(r, 0.0))), (MXU_DTYPE, MXU_DTYPE)))
    return _mm("mm_down", s["act"], w["w_mlp_down"], add=s["h2"]), s, w


def _layer_bwd(geo, dh3, s, w, tab, mid=None, tail=None, dep=None):
    nr, tr, trw = geo.nrows, geo.tr, geo.tr_wide
    tb = geo.lp // tr
    rw = functools.partial(_rowwise, nrows=nr)
    g = {}
    proj = s["proj"]

    def rms_bwd(x, dy, res, gw):
        _, vjp = jax.vjp(_rms, x.astype(F32), gw)
        dx, dgw = vjp(dy.astype(F32))
        return dx + res, dgw

    def rms_bwd_nores(x, dy, gw):
        _, vjp = jax.vjp(_rms, x.astype(F32), gw)
        return vjp(dy.astype(F32))

    (dup,) = _mm("mm_down_t", dh3, w["w_mlp_down"], tb=True, add=s["up"], dep=dep,
                 epi=(lambda r, up: (r * 2.0 * jnp.maximum(up, 0.0),), (MXU_DTYPE,)))
    g["w_mlp_down"] = _mm("mm_down_g", s["act"], dh3, ta=True, out_dtype=MXU_DTYPE)
    g["w_mlp_up"] = _mm("mm_up_g", s["vn"], dup, ta=True, out_dtype=MXU_DTYPE)
    dvn = _mm("mm_up_t", dup, w["w_mlp_up"], tb=True)
    dh2, g["norm_mlp_w"] = rw("rms_mlp_bwd", rms_bwd, tr=tr,
                              rows=[(s["h2"], D_MODEL, 0), (dvn, D_MODEL, 0), (dh3, D_MODEL, 0)],
                              vecs=[(w["norm_mlp_w"], D_MODEL, 0)], outs=[(D_MODEL, D_MODEL, F32)],
                              reds=[(D_MODEL, D_MODEL)])
    dmixed = _mm("mm_out_t", dh2, w["w_out"], tb=True, out_dtype=MXU_DTYPE)
    g["w_out"] = _mm("mm_out_g", s["mixed"], dh2, ta=True, out_dtype=MXU_DTYPE)

    def gate_bwd(gs, gm, ys, ym, dm):
        f = lambda a, b, c, d: _sigmoid(a) * c + _sigmoid(b) * d
        _, vjp = jax.vjp(f, gs.astype(F32), gm.astype(F32), ys.astype(F32), ym.astype(F32))
        dgs, dgm, dys, dym = vjp(dm.astype(F32))
        return dys, dym, jnp.concatenate([dgs, dgm], axis=1)

    assert geo.col["g_mla"][0] == geo.col["g_ssm"][0] + D_MODEL and geo.col["g_ssm"][0] % (2 * D_MODEL) == 0
    dys_p, dym_p, dproj = rw(
        "gate_bwd", gate_bwd, tr=tr,
        rows=[(proj, D_MODEL, geo.cb("g_ssm")), (proj, D_MODEL, geo.cb("g_mla")), (s["ys_p"], D_MODEL, 0),
              (s["ym_p"], D_MODEL, 0), (dmixed, D_MODEL, 0)],
        outs=[(D_MODEL, D_MODEL, MXU_DTYPE)] * 2 + [(geo.pw, 2 * D_MODEL, MXU_DTYPE, geo.col["g_ssm"][0] // (2 * D_MODEL))])
    g["w_branch_ssm"] = _mm("mm_bs_g", s["y_ssm"], dys_p, ta=True, out_dtype=MXU_DTYPE)
    dy_ssm = _mm("mm_bs_t", dys_p, w["w_branch_ssm"], tb=True, out_dtype=MXU_DTYPE)
    g["w_branch_mla"] = _mm("mm_bm_g", s["o"], dym_p, ta=True, out_dtype=MXU_DTYPE)
    d_o = _mm("mm_bm_t", dym_p, w["w_branch_mla"], tb=True, out_dtype=MXU_DTYPE)
    dqn, dqp, dkn, dkp_h, dv = _attn_bwd3(geo, s["qn"], s["qp"], s["kn"], s["kp"], s["v"], d_o, s["o"], s["lse"])
    rope_tabs = [(tab["cos"], LANE, 0), (tab["sin"], LANE, 0)]
    (dqp_raw,) = rw("rope_q_bwd", _per_head(_rope_t), tr=tr, rows=[(dqp, geo.hq, 0)], tabs=rope_tabs,
                    outs=[(geo.hq, geo.hq, MXU_DTYPE)], tab_blocks=tb)

    def rope_k_bwd(x, c, sn):
        tot = x[:, :LANE]
        for hd in range(1, MLA_HEADS):
            tot = tot + x[:, hd * LANE:(hd + 1) * LANE]
        return (_rope_t(tot, c, sn),)

    (dproj,) = rw("rope_k_bwd", rope_k_bwd, tr=tr, rows=[(dkp_h, geo.hq, 0)], tabs=rope_tabs,
                  outs=[(geo.pw, LANE, MXU_DTYPE, geo.cb("k_rope"), dproj)], tab_blocks=tb)
    g["w_qn"] = _mm("mm_qn_g", s["cq_n"], dqn, ta=True, out_dtype=MXU_DTYPE)
    g["w_qp"] = _mm("mm_qp_g", s["cq_n"], dqp_raw, ta=True, out_dtype=MXU_DTYPE)
    dcq_n = _mm("mm_qp_t", dqp_raw, w["w_qp"], tb=True, add=_mm("mm_qn_t", dqn, w["w_qn"], tb=True))
    g["w_k"] = _mm("mm_kn_g", s["ckv_n"], dkn, ta=True, out_dtype=MXU_DTYPE)
    g["w_v"] = _mm("mm_v_g", s["ckv_n"], dv, ta=True, out_dtype=MXU_DTYPE)
    dckv_n = _mm("mm_v_t", dv, w["w_v"], tb=True, add=_mm("mm_kn_t", dkn, w["w_k"], tb=True))
    dproj, g["q_norm_w"] = rw("rms_q_bwd", rms_bwd_nores, tr=tr,
                              rows=[(proj, MLA_Q_LORA, geo.cb("c_q")), (dcq_n, MLA_Q_LORA, 0)],
                              vecs=[(w["q_norm_w"], MLA_Q_LORA, 0)],
                              outs=[(geo.pw, MLA_Q_LORA, MXU_DTYPE, geo.cb("c_q"), dproj)], reds=[(MLA_Q_LORA, MLA_Q_LORA)])
    dproj, g["kv_norm_w"] = rw("rms_kv_bwd", rms_bwd_nores, tr=tr,
                               rows=[(proj, MLA_KV_LORA, geo.cb("c_kv")), (dckv_n, MLA_KV_LORA, 0)],
                               vecs=[(w["kv_norm_w"], MLA_KV_LORA, 0)],
                               outs=[(geo.pw, MLA_KV_LORA, MXU_DTYPE, geo.cb("c_kv"), dproj)],
                               reds=[(MLA_KV_LORA, MLA_KV_LORA)])
    gw_ = SSM_D_INNER // SSM_GROUPS
    d_skip_full = w["d_skip_full"] if mid is None else w["d_skip_full"] + mid(g)[0, 0]

    def gate_norm_bwd(y, x, z, dy, dsk, nw):
        f = lambda y_, x_, z_, dsk_, nw_: _rms((y_ + x_ * dsk_) * _silu(z_), nw_)
        _, vjp = jax.vjp(f, y.astype(F32), x.astype(F32), z.astype(F32), dsk, nw)
        dy_, dx_, dz_, ddsk, dnw = vjp(dy.astype(F32))
        return dy_, dx_, dz_, ddsk, dnw

    dy_ssd, dxs_skip, dproj, g["d_skip_full"], g["ssm_norm_w"] = rw(
        "ssm_gate_norm_bwd", gate_norm_bwd, tr=tr, ncb=SSM_GROUPS,
        rows=[(s["y_ssd"], gw_, 0), (s["xc"], gw_, 0), (proj, gw_, geo.col["z"][0] // gw_), (dy_ssm, gw_, 0)],
        vecs=[(d_skip_full, gw_, 0), (w["ssm_norm_w"], gw_, 0)],
        outs=[(SSM_D_INNER, gw_, MXU_DTYPE), (SSM_D_INNER, gw_, MXU_DTYPE),
              (geo.pw, gw_, MXU_DTYPE, geo.col["z"][0] // gw_, dproj)],
        reds=[(SSM_D_INNER, gw_), (SSM_D_INNER, gw_)])
    dxc, dproj, g["dt_bias"], g["a_log"] = _ssd_bwd_g(geo, s["xc"], s["proj_dt"], w["dt_bias"], w["a_log"], s["s_prev"],
                                                     dy_ssd, dxs_skip, dproj)
    dproj, g["conv_w"], g["conv_b"] = _conv_bwd(geo, proj, w["conv_w"], w["conv_b"], dxc, dproj)
    g["w_in_pt"] = _mm("mm_in_g", dproj, s["u"], ta=True, out_dtype=MXU_DTYPE)
    du = _mm("mm_in_t", dproj, w["w_in_pt"], dep=None if tail is None else tail(g))
    dh, g["norm_mix_w"] = rw("rms_mix_bwd", rms_bwd, tr=tr,
                             rows=[(s["h"], D_MODEL, 0), (du, D_MODEL, 0), (dh2, D_MODEL, 0)],
                             vecs=[(w["norm_mix_w"], D_MODEL, 0)], outs=[(D_MODEL, D_MODEL, F32)],
                             reds=[(D_MODEL, D_MODEL)])
    return dh, g


def _loss_bwd(geo, h, fw, target, tab):
    tr = geo.tr

    def fn(x, tgt, gw, tok):
        def lossf(x_, gw_):
            err = jnp.square(_rms(x_, gw_) - tgt)
            return 0.5 * jnp.sum(tok * jnp.mean(err, axis=-1, keepdims=True), axis=0, keepdims=True)

        val, vjp = jax.vjp(lossf, x, gw)
        dx, dgw = vjp(jnp.ones((1, 1), F32))
        return dx, jnp.broadcast_to(val, (1, LANE)), dgw

    return _rowwise("loss", fn, nrows=geo.nrows, tr=tr, rows=[(h, D_MODEL, 0), (target, D_MODEL, 0)],
                    vecs=[(fw, D_MODEL, 0)], tabs=[(tab["token"], 1, 0)], outs=[(D_MODEL, D_MODEL, F32)],
                    reds=[(LANE, LANE), (D_MODEL, D_MODEL)], tab_blocks=geo.lp // tr)


def kernel(x, meta_tokens, norm_mix_w, w_in, conv_w, conv_b, dt_bias, a_log, d_skip, ssm_norm_w, q_norm_w, kv_norm_w, w_uq, w_ukv, w_branch_ssm, w_branch_mla, w_out, norm_mlp_w, w_mlp_up, w_mlp_down, final_norm_w, loss_target, m_meta_tokens, m_norm_mix_w, m_w_in, m_conv_w, m_conv_b, m_dt_bias, m_a_log, m_d_skip, m_ssm_norm_w, m_q_norm_w, m_kv_norm_w, m_w_uq, m_w_ukv, m_w_branch_ssm, m_w_branch_mla, m_w_out, m_norm_mlp_w, m_w_mlp_up, m_w_mlp_down, m_final_norm_w, v_meta_tokens, v_norm_mix_w, v_w_in, v_conv_w, v_conv_b, v_dt_bias, v_a_log, v_d_skip, v_ssm_norm_w, v_q_norm_w, v_kv_norm_w, v_w_uq, v_w_ukv, v_w_branch_ssm, v_w_branch_mla, v_w_out, v_norm_mlp_w, v_w_mlp_up, v_w_mlp_down, v_final_norm_w):
    args = dict(locals())
    turn = lambda n, a: jnp.swapaxes(a, 1, 2) if n == "w_in" else a
    wts = {n: turn(n, args[n]) for n in WEIGHTS}
    mom = {n: turn(n, args["m_" + n]) for n in WEIGHTS}
    var = {n: turn(n, args["v_" + n]) for n in WEIGHTS}
    bsz, seq, _ = x.shape
    depth = w_in.shape[0]
    geo = _Geo(bsz, seq)
    tab = _tables(geo)

    big_names = [n for n, _ in BIG]
    sh_names = big_names + [n for n, _ in SHARDED_F32]
    kinds = dict(BIG + SHARDED_F32, w_in="row")
    shard3 = lambda a: a.reshape((1,) + a.shape) if a.ndim == 2 else a
    wire = {n: (MXU_DTYPE if n in big_names else F32) for n in sh_names}
    cast = {n: shard3(wts[n]).astype(wire[n]) for n in sh_names}
    per_layer = [n for n in sh_names if n != "meta_tokens"]
    small_names = ["norm_mix_w", "conv_b", "dt_bias", "a_log", "d_skip", "ssm_norm_w", "q_norm_w", "kv_norm_w",
                   "norm_mlp_w"]

    def gather_items(pairs):
        ins, outs, items, forms = [], [], [], []
        for n, i in pairs:
            a, b = cast[n].shape[1:]
            shape, dst, form = _gather_plan(a, b, kinds[n])
            items.append((len(ins), len(outs), (lambda ref, p, i=i: ref.at[i]), dst))
            ins.append(cast[n])
            outs.append(jax.ShapeDtypeStruct(shape, wire[n]))
            forms.append(form)
        return ins, outs, items, forms

    def whole_weights(pairs, forms, got):
        by_layer = {}
        for (n, i), form, g in zip(pairs, forms, got):
            if n == "w_in":
                n, g = "w_in_pt", _w_in_assemble(geo, g)
            elif form == "row":
                g = g.reshape(g.shape[0] * g.shape[1], g.shape[2])
            elif form == "stack":
                g = _unshard(g, "col")
            by_layer.setdefault(i, {})[n] = g
        return by_layer

    def prep(i, whole, token=None):
        wl = dict(whole)
        wl.update({n: wts[n][i] for n in small_names})
        if token is not None:
            wl["norm_mix_w"] = wl["norm_mix_w"] + token[0, 0]
        return _prep_layer(geo, wl)

    early = ("w_in", "conv_w")
    late_names = [n for n in per_layer if n not in early]
    pairs1 = [(n, i) for i in range(1, depth) for n in per_layer]
    groups = [[(n, 0) for n in early] + [("meta_tokens", 0)], [(n, 0) for n in late_names]] + ([pairs1] if pairs1 else [])
    started = {}

    def gather_start(gi, dep=None):
        ins, outs, items, forms = gather_items(groups[gi])
        sems, thru, landing, token = _exchange_start("gather_w%d_start" % gi, ins, outs, items, dep)
        started[gi] = (groups[gi], forms, sems, thru, landing, items)
        return token

    def gathered(gi, after):
        pairs, forms, sems, thru, landing, items = started[gi]
        return whole_weights(pairs, forms, _exchange_wait("gather_w%d_wait" % gi, sems, thru, landing, items, after))

    def late0(after):
        whole = gathered(1, after)[0]
        if pairs1:
            whole["q_norm_w"] = wts["q_norm_w"][0] + gather_start(2, whole["w_out"])[0, 0]
        return _prep_layer(geo, whole)

    token = gather_start(1, gather_start(0))
    whole0 = gathered(0, token)[0]
    meta_full = whole0.pop("meta_tokens")

    meta = jnp.broadcast_to(meta_full[None], (bsz, N_META, D_MODEL))
    h = jnp.concatenate([jnp.zeros((bsz, geo.pad, D_MODEL), F32), meta, x], axis=1).reshape(geo.nrows, D_MODEL)
    target = jnp.concatenate([jnp.zeros((bsz, geo.pad + N_META, D_MODEL), F32), loss_target], axis=1)
    target = target.reshape(geo.nrows, D_MODEL)
    layers, saved = [], []
    for i in range(depth):
        if i == 0:
            w, late = prep(0, whole0, token), late0
        else:
            if i == 1:
                whole1 = gathered(2, h)
            w, late = prep(i, whole1[i]), None
        h, s, w = _layer_fwd(geo, h, w, tab, late)
        layers.append(w)
        saved.append(s)
    dh, loss_part, g_final = _loss_bwd(geo, h, final_norm_w.reshape(1, -1), target, tab)

    def scatter_items(pairs):
        ins, outs, items = [], [], []
        for n, i in pairs:
            a, b = cast[n].shape[1:]
            arr = g_meta if n == "meta_tokens" else grads[i]["w_in_pt" if n == "w_in" else n]
            if n == "w_in":
                arr, src = _w_in_split(geo, arr, a), _entry
            elif kinds[n] == "row":
                src = lambda ref, p, a=a: ref.at[pl.ds(pl.multiple_of(p * a, a), a)]
            elif b % LANE == 0:
                src = lambda ref, p, b=b: ref.at[:, pl.ds(pl.multiple_of(p * b, b), b)]
            else:
                arr, src = _shard(arr, "col"), _entry
            items.append((len(ins), len(outs), src, _entry))
            ins.append(arr.astype(wire[n]))
            outs.append(jax.ShapeDtypeStruct((N_DEV, a, b), wire[n]))
        return ins, outs, items

    grads = [None] * depth
    landed, pending, res = {}, {}, {}

    def scatter_start(name, pairs):
        ins, outs, items = scatter_items(pairs)
        sems, thru, landing, token = _exchange_start(name + "_start", ins, outs, items)
        pending[name] = (pairs, sems, thru, landing, items)
        return token

    def scatter_wait(name, after):
        pairs, sems, thru, landing, items = pending[name]
        landed.update(zip(pairs, _exchange_wait(name + "_wait", sems, thru, landing, items, after)))

    def adam(n):
        parts = [landed[(n, i)] for i in range(cast[n].shape[0])]
        r = _adamw_nat("adamw_" + n, parts, shard3(wts[n]), shard3(mom[n]), shard3(var[n]))
        res[n] = [a.reshape(wts[n].shape) for a in r]

    def mid0(g):
        grads[0] = _unprep_grads(geo, g)
        return scatter_start("scatter_gb0", [(n, 0) for n in late_names])

    def tail0(g):
        grads[0] = _unprep_grads(geo, g)
        return scatter_start("scatter_ga0", [(n, 0) for n in early])

    dep = None
    for i in reversed(range(depth)):
        dh, gl = _layer_bwd(geo, dh, saved[i], layers[i], tab, *((mid0, tail0) if i == 0 else (None, None)), dep)
        grads[i] = _unprep_grads(geo, gl)
        if i == 1:
            dep = scatter_start("scatter_g1", pairs1)
    dh = dh.reshape(bsz, geo.lp, D_MODEL)
    grad_x = dh[:, geo.pad + N_META:]
    g_meta = jnp.sum(dh[:, geo.pad:geo.pad + N_META], axis=0)
    if pairs1:
        scatter_wait("scatter_g1", g_meta)
    scatter_wait("scatter_gb0", g_meta)
    for n in late_names:
        adam(n)
    g_small = {n: jnp.stack([grads[i][n] for i in range(depth)]) for n in SMALL if n != "final_norm_w"}
    g_small["final_norm_w"] = g_final.reshape(-1)
    zero = jnp.zeros((1,), F32)
    pk = lambda d, last: _pack([d[n] for n in SMALL] + [last], F32, row_mult=8)
    packed = pk(g_small, loss_part[0, :1])
    ins, outs, items = scatter_items([("meta_tokens", 0)])
    parts, landed[("meta_tokens", 0)] = _exchange(
        "gather_g", [packed] + ins + [res[n][1] for n in late_names],
        [jax.ShapeDtypeStruct((N_DEV,) + packed.shape, F32)] + outs,
        [(0, 0, _whole, _entry)] + [(1, 1, items[0][2], items[0][3])])
    adam("meta_tokens")
    scatter_wait("scatter_ga0", res["meta_tokens"][1])
    for n in early:
        adam(n)
    res_sm = _adamw("adamw_small", parts, pk(wts, zero), pk(mom, zero), pk(var, zero))
    res_sm = [_unpack(r, [wts[n].shape for n in SMALL] + [(1,)]) for r in res_sm]
    loss = res_sm[0][-1][0]

    out = [loss, grad_x]
    for k in range(4):
        named = {n: res[n][k] for n in sh_names}
        named.update(zip(SMALL, res_sm[k]))
        out += [turn(n, named[n]) for n in WEIGHTS]
    return tuple(out)
```

```python
import functools

import numpy as np
import jax
import jax.numpy as jnp
from jax import lax
from jax.experimental import pallas as pl
from jax.experimental.pallas import tpu as pltpu

F32 = jnp.float32
MXU_DTYPE = jnp.bfloat16

D_MODEL = 1024
N_META = 16
EPS = 1e-6
SSM_D_INNER = 2048
SSM_HEAD_DIM = 64
SSM_GROUPS = 4
SSM_STATE = 128
SSM_CONV = 4
SSM_CHUNK = 128
MLA_HEADS = 8
MLA_Q_LORA = 512
MLA_KV_LORA = 256
MLA_NOPE = 128
MLA_ROPE = 64
MLA_V = 128
ROPE_THETA = 10000.0
D_FF = 4096
ADAM_LR = 0.001
ADAM_B1 = 0.9
ADAM_B2 = 0.999
ADAM_EPS = 1e-08
ADAM_WD = 0.01
ADAM_STEP = 10

N_DEV = 8
ATT_BLK = 256
LANE = 128
PACK_W = 1024
VMEM_LIMIT = 56 * 1024 * 1024
MESH_ID = pl.DeviceIdType.MESH

BIG = (("w_in", "col"), ("w_uq", "col"), ("w_ukv", "col"), ("w_branch_ssm", "row"), ("w_branch_mla", "row"),
       ("w_out", "row"), ("w_mlp_up", "col"), ("w_mlp_down", "row"))
SHARDED_F32 = (("conv_w", "col"), ("meta_tokens", "col"))
SMALL = ("norm_mix_w", "conv_b", "dt_bias", "a_log", "d_skip", "ssm_norm_w", "q_norm_w", "kv_norm_w",
         "norm_mlp_w", "final_norm_w")
WEIGHTS = ("meta_tokens", "norm_mix_w", "w_in", "conv_w", "conv_b", "dt_bias", "a_log", "d_skip", "ssm_norm_w",
           "q_norm_w", "kv_norm_w", "w_uq", "w_ukv", "w_branch_ssm", "w_branch_mla", "w_out", "norm_mlp_w",
           "w_mlp_up", "w_mlp_down", "final_norm_w")


def _cparams(sem=None):
    return pltpu.CompilerParams(dimension_semantics=sem, vmem_limit_bytes=VMEM_LIMIT)


def _pick(n, cands):
    for c in cands:
        if n % c == 0:
            return c
    return n


def _sigmoid(x):
    return 1.0 / (1.0 + jnp.exp(-x))


def _silu(x):
    return x * _sigmoid(x)


def _softplus(x):
    t = jnp.exp(-jnp.abs(x))
    return jnp.maximum(x, 0.0) + jnp.where(t < 0.01, t * (1.0 - t * (0.5 - t * (1.0 / 3.0))), jnp.log(1.0 + t))


def _rms(x, w):
    x = x.astype(F32)
    return x * lax.rsqrt(jnp.mean(x * x, axis=-1, keepdims=True) + EPS) * w


def _dot(a, b, ca, cb, precision=None):
    return lax.dot_general(a, b, (((ca,), (cb,)), ((), ())), preferred_element_type=F32, precision=precision)


def _mxdot(a, b, ca, cb):
    return _dot(a.astype(MXU_DTYPE), b.astype(MXU_DTYPE), ca, cb)


def _mm(name, a, b, *, ta=False, tb=False, add=None, out_dtype=F32, dep=None, epi=None, side=None):
    (kdim, m) = a.shape if ta else a.shape[::-1]
    (n, k2) = b.shape if tb else b.shape[::-1]
    assert kdim == k2, (name, a.shape, b.shape)
    tm = _pick(m, (1152, 1088, 1024, 768, 544, 512, 384, 256, 128))
    tn = _pick(n, (1024, 512, 384, 256, 128))
    tk = _pick(kdim, (1152, 1088, 1024, 768, 544, 512, 384, 256, 128))
    nk = kdim // tk
    a_spec = pl.BlockSpec((tk, tm), lambda i, j, k: (k, i)) if ta else pl.BlockSpec((tm, tk), lambda i, j, k: (i, k))
    b_spec = pl.BlockSpec((tn, tk), lambda i, j, k: (j, k)) if tb else pl.BlockSpec((tk, tn), lambda i, j, k: (k, j))
    o_spec = pl.BlockSpec((tm, tn), lambda i, j, k: (i, j))
    ca, cb = (0 if ta else 1), (1 if tb else 0)

    out_dtypes = [out_dtype] if epi is None else list(epi[1])
    n_out = len(out_dtypes)
    n_side = 0 if side is None else 1

    def body(*refs):
        a_ref, b_ref = refs[:2]
        o_refs, acc = refs[-1 - n_side - n_out:-1 - n_side], refs[-1]
        k = pl.program_id(2)

        @pl.when(k == 0)
        def _():
            acc[...] = jnp.zeros_like(acc)

        acc[...] += _mxdot(a_ref[...], b_ref[...], ca, cb)

        @pl.when(k == nk - 1)
        def _():
            r = acc[...]
            if epi is not None:
                res = epi[0](r, refs[2][...]) if add is not None else epi[0](r)
            else:
                res = (r + refs[2][...].astype(F32) if add is not None else r,)
            for o_ref, val in zip(o_refs, res):
                o_ref[...] = val.astype(o_ref.dtype)

        if side is not None:
            @pl.when(jnp.logical_and(k == nk - 1, pl.program_id(1) == side[0] // tn))
            def _():
                refs[-2][...] = acc[:, side[0] % tn:side[0] % tn + side[1]]

    in_specs, args = [a_spec, b_spec], [a, b]
    if add is not None:
        in_specs.append(o_spec)
        args.append(add)
    if dep is not None:
        in_specs.append(pl.BlockSpec((8, LANE), lambda i, j, k: (0, 0)))
        args.append(dep)
    out_specs = [o_spec] * n_out
    out_shape = [jax.ShapeDtypeStruct((m, n), dt) for dt in out_dtypes]
    if side is not None:
        assert side[0] % tn + side[1] <= tn
        out_specs.append(pl.BlockSpec((tm, side[1]), lambda i, j, k: (i, 0)))
        out_shape.append(jax.ShapeDtypeStruct((m, side[1]), F32))
    res = pl.pallas_call(
        body, name=name, grid=(m // tm, n // tn, nk), in_specs=in_specs, out_specs=out_specs, out_shape=out_shape,
        scratch_shapes=[pltpu.VMEM((tm, tn), F32)],
        compiler_params=_cparams(("parallel", "arbitrary" if side is not None else "parallel", "arbitrary")))(*args)
    return res[0] if epi is None and side is None else res


def _rowwise(name, fn, *, nrows, tr, ncb=1, rows=(), fixed=(), vecs=(), tabs=(), outs=(), reds=(), tab_blocks=1):
    in_specs, args = [], []
    for arr, w, c0 in rows:
        in_specs.append(pl.BlockSpec((tr, w), lambda g, i, c0=c0: (i, c0 + g)))
        args.append(arr)
    for arr, w, c0 in fixed:
        in_specs.append(pl.BlockSpec((tr, w), lambda g, i, c0=c0: (i, c0)))
        args.append(arr)
    for arr, w, c0 in vecs:
        in_specs.append(pl.BlockSpec((1, w), lambda g, i, c0=c0: (0, c0 + g)))
        args.append(arr)
    for arr, w, c0 in tabs:
        in_specs.append(pl.BlockSpec((tr, w), lambda g, i, c0=c0: (i % tab_blocks, c0)))
        args.append(arr)
    n_in, n_out = len(args), len(outs)
    out_shape, out_specs, aliases = [], [], {}
    for k, o in enumerate(outs):
        c0 = o[3] if len(o) > 3 else 0
        out_shape.append(jax.ShapeDtypeStruct((nrows, o[0]), o[2]))
        out_specs.append(pl.BlockSpec((tr, o[1]), lambda g, i, c0=c0: (i, c0 + g)))
        if len(o) > 4:
            aliases[len(args)] = k
            in_specs.append(pl.BlockSpec(memory_space=pl.ANY))
            args.append(o[4])
    out_shape += [jax.ShapeDtypeStruct((1, wt), F32) for wt, w in reds]
    out_specs += [pl.BlockSpec((1, w), lambda g, i: (0, g)) for wt, w in reds]
    first_out = len(args)

    def body(*refs):
        res = fn(*[r[...] for r in refs[:n_in]])
        for o_ref, val in zip(refs[first_out:first_out + n_out], res[:n_out]):
            o_ref[...] = val.astype(o_ref.dtype)
        i = pl.program_id(1)
        for d_ref, val in zip(refs[first_out + n_out:], res[n_out:]):
            @pl.when(i == 0)
            def _(d_ref=d_ref, val=val):
                d_ref[...] = val

            @pl.when(i > 0)
            def _(d_ref=d_ref, val=val):
                d_ref[...] += val

    return pl.pallas_call(
        body, name=name, grid=(ncb, nrows // tr), in_specs=in_specs, out_specs=out_specs, out_shape=out_shape,
        input_output_aliases=aliases, compiler_params=_cparams(("parallel", "arbitrary")))(*args)


def _peer(k):
    x, y, c = lax.axis_index("x"), lax.axis_index("y"), lax.axis_index("c")
    px = jnp.where((k >> 2) & 1, 1 - x, x)
    py = jnp.where((k >> 1) & 1, 1 - y, y)
    pc = jnp.where(k & 1, 1 - c, c)
    return (px, py, pc), 4 * px + 2 * py + pc


def _my_index():
    return 4 * lax.axis_index("x") + 2 * lax.axis_index("y") + lax.axis_index("c")


def _exchange(name, ins, out_shapes, items):
    n_in, n_out, n_it = len(ins), len(out_shapes), len(items)

    def body(*refs):
        x, o = refs[:n_in], refs[n_in:n_in + n_out]
        send_sems, recv_sems, local_sems = refs[n_in + n_out:]
        me = _my_index()
        local, sends = [], []
        for t, (ii, io, src, dst) in enumerate(items):
            cp = pltpu.make_async_copy(src(x[ii], me), dst(o[io], me), local_sems.at[t])
            cp.start()
            local.append(cp)
        for k in range(1, N_DEV):
            dev, idx = _peer(k)
            for t, (ii, io, src, dst) in enumerate(items):
                s = (k - 1) * n_it + t
                cp = pltpu.make_async_remote_copy(
                    src_ref=src(x[ii], idx), dst_ref=dst(o[io], me), send_sem=send_sems.at[s],
                    recv_sem=recv_sems.at[s], device_id=dev, device_id_type=MESH_ID)
                cp.start()
                sends.append(cp)
        for k in range(1, N_DEV):
            dev, idx = _peer(k)
            for t, (ii, io, src, dst) in enumerate(items):
                s = (k - 1) * n_it + t
                pltpu.make_async_remote_copy(
                    src_ref=src(x[ii], idx), dst_ref=dst(o[io], idx), send_sem=send_sems.at[s],
                    recv_sem=recv_sems.at[s], device_id=dev, device_id_type=MESH_ID).wait_recv()
        for cp in sends:
            cp.wait_send()
        for cp in local:
            cp.wait()

    nsem = (N_DEV - 1) * n_it
    anyspec = pl.BlockSpec(memory_space=pl.ANY)
    return pl.pallas_call(
        body, name=name, out_shape=list(out_shapes), in_specs=[anyspec] * n_in, out_specs=[anyspec] * n_out,
        scratch_shapes=[pltpu.SemaphoreType.DMA((nsem,)), pltpu.SemaphoreType.DMA((nsem,)),
                        pltpu.SemaphoreType.DMA((n_it,))],
        compiler_params=pltpu.CompilerParams(has_side_effects=True))(*ins)


def _split_copies(x, land, send_sems, recv_sems, items, receive):
    me = _my_index()
    remote, n_it = [], len(items)
    for k in range(1, N_DEV):
        dev, idx = _peer(k)
        for t, (ii, io, src, dst) in enumerate(items):
            s = (k - 1) * n_it + t
            remote.append(pltpu.make_async_remote_copy(
                src_ref=src(x[ii], idx), dst_ref=dst(land[io], idx if receive else me), send_sem=send_sems.at[s],
                recv_sem=recv_sems.at[s], device_id=dev, device_id_type=MESH_ID))
    local = [pltpu.make_async_copy(src(x[ii], me), dst(land[io], me), send_sems.at[(N_DEV - 1) * n_it + t])
             for t, (ii, io, src, dst) in enumerate(items)]
    return remote, local


def _exchange_start(name, ins, out_shapes, items, dep=None):
    n_in, n_out, n_it = len(ins), len(out_shapes), len(items)

    def body(*refs):
        x, land = refs[:n_in], refs[n_in:n_in + n_out]
        first_out = n_in + n_out + (dep is not None)
        send_sems, recv_sems, token = refs[first_out], refs[first_out + 1], refs[-1]
        remote, local = _split_copies(x, land, send_sems, recv_sems, items, False)
        for cp in remote + local:
            cp.start()
        token[...] = jnp.zeros_like(token)

    hbm = pl.BlockSpec(memory_space=pltpu.HBM)
    sem = pl.BlockSpec(memory_space=pltpu.SEMAPHORE)
    arrs = [pltpu.with_memory_space_constraint(a, pltpu.HBM)
            for a in list(ins) + [lax.empty(s.shape, s.dtype) for s in out_shapes]]
    res = pl.pallas_call(
        body, name=name,
        out_shape=(pltpu.SemaphoreType.DMA((N_DEV * n_it,)), pltpu.SemaphoreType.DMA(((N_DEV - 1) * n_it,)),
                   *[pltpu.HBM(a.shape, a.dtype) for a in arrs], jax.ShapeDtypeStruct((8, LANE), F32)),
        in_specs=[hbm] * (n_in + n_out) + ([] if dep is None else [pl.BlockSpec(memory_space=pl.ANY)]),
        out_specs=(sem, sem, *[hbm] * (n_in + n_out), pl.BlockSpec(memory_space=pltpu.VMEM)),
        input_output_aliases={i: 2 + i for i in range(n_in + n_out)},
        compiler_params=pltpu.CompilerParams(has_side_effects=pltpu.SideEffectType.DATAFLOW_SIDE_EFFECTING))(
            *arrs, *([] if dep is None else [dep]))
    return res[:2], res[2:2 + n_in], res[2 + n_in:2 + n_in + n_out], res[-1]


def _exchange_wait(name, sems, ins, landing, items, after):
    n_in, n_out = len(ins), len(landing)

    def body(*refs):
        x, land = refs[:n_in], refs[n_in:n_in + n_out]
        send_sems, recv_sems = refs[n_in + n_out], refs[n_in + n_out + 1]
        remote, local = _split_copies(x, land, send_sems, recv_sems, items, True)
        for cp in remote:
            cp.wait_send()
            cp.wait_recv()
        for cp in local:
            cp.wait()

    hbm = pl.BlockSpec(memory_space=pltpu.HBM)
    sem = pl.BlockSpec(memory_space=pltpu.SEMAPHORE)
    arrs = list(ins) + list(landing)
    res = pl.pallas_call(
        body, name=name, out_shape=tuple(pltpu.HBM(a.shape, a.dtype) for a in arrs),
        in_specs=[hbm] * (n_in + n_out) + [sem, sem, pl.BlockSpec(memory_space=pl.ANY)],
        out_specs=tuple([hbm] * (n_in + n_out)), input_output_aliases={i: i for i in range(n_in + n_out)},
        compiler_params=pltpu.CompilerParams(has_side_effects=pltpu.SideEffectType.DATAFLOW_SIDE_EFFECTING))(
            *arrs, *sems, after)
    return res[n_in:]


def _whole(ref, p):
    return ref


def _entry(ref, p):
    return ref.at[p]


def _gather_plan(a, b, kind):
    if kind == "col" and b % LANE == 0:
        return (a, N_DEV * b), (lambda ref, p: ref.at[:, pl.ds(pl.multiple_of(p * b, b), b)]), "col"
    return (N_DEV, a, b), _entry, ("row" if kind == "row" else "stack")


def _adamw_nat(name, parts, w, m, v):
    depth, b, c = w.shape
    assert len(parts) == depth
    tb = _pick(b, (128, 64, 32, 16, 8))
    if tb == b and b > 256:
        tb = 256
    spec = pl.BlockSpec((1, tb, c), lambda i, j: (i, j, 0))

    def body(*refs):
        p_refs = refs[:depth]
        w_ref, m_ref, v_ref, g_ref, d_ref, nm_ref, nv_ref = refs[depth:]
        for layer, p_ref in enumerate(p_refs):
            @pl.when(pl.program_id(0) == layer)
            def _(p_ref=p_ref):
                g = p_ref[0].astype(F32)
                for j in range(1, N_DEV):
                    g = g + p_ref[j].astype(F32)
                nm = ADAM_B1 * m_ref[0] + (1.0 - ADAM_B1) * g
                nv = ADAM_B2 * v_ref[0] + (1.0 - ADAM_B2) * jnp.square(g)
                m_hat = nm / (1.0 - ADAM_B1 ** ADAM_STEP)
                v_hat = nv / (1.0 - ADAM_B2 ** ADAM_STEP)
                g_ref[0] = g
                d_ref[0] = -ADAM_LR * (m_hat / (jnp.sqrt(v_hat) + ADAM_EPS) + ADAM_WD * w_ref[0])
                nm_ref[0] = nm
                nv_ref[0] = nv

    sds = jax.ShapeDtypeStruct((depth, b, c), F32)
    return pl.pallas_call(
        body, name=name, grid=(depth, pl.cdiv(b, tb)),
        in_specs=[pl.BlockSpec((N_DEV, tb, c), lambda i, j: (0, j, 0))] * depth + [spec, spec, spec],
        out_specs=[spec] * 4, out_shape=[sds] * 4, compiler_params=_cparams(("parallel", "parallel")))(*parts, w, m, v)


def _adamw(name, parts, w, m, v):
    rows = w.shape[0]
    tr = _pick(rows, (256, 128, 64, 32, 16, 8))
    spec = pl.BlockSpec((tr, PACK_W), lambda i: (i, 0))

    def body(p_ref, w_ref, m_ref, v_ref, g_ref, d_ref, nm_ref, nv_ref):
        g = p_ref[0]
        for j in range(1, N_DEV):
            g = g + p_ref[j]
        nm = ADAM_B1 * m_ref[...] + (1.0 - ADAM_B1) * g
        nv = ADAM_B2 * v_ref[...] + (1.0 - ADAM_B2) * jnp.square(g)
        m_hat = nm / (1.0 - ADAM_B1 ** ADAM_STEP)
        v_hat = nv / (1.0 - ADAM_B2 ** ADAM_STEP)
        g_ref[...] = g
        d_ref[...] = -ADAM_LR * (m_hat / (jnp.sqrt(v_hat) + ADAM_EPS) + ADAM_WD * w_ref[...])
        nm_ref[...] = nm
        nv_ref[...] = nv

    sds = jax.ShapeDtypeStruct((rows, PACK_W), F32)
    return pl.pallas_call(
        body, name=name, grid=(rows // tr,),
        in_specs=[pl.BlockSpec((N_DEV, tr, PACK_W), lambda i: (0, i, 0)), spec, spec, spec],
        out_specs=[spec] * 4, out_shape=[sds] * 4, compiler_params=_cparams(("parallel",)))(parts, w, m, v)


def _pack(arrs, dtype, row_mult=16):
    flat = jnp.concatenate([a.reshape(-1).astype(dtype) for a in arrs])
    unit = row_mult * PACK_W
    total = -(-flat.shape[0] // unit) * unit
    flat = jnp.pad(flat, (0, total - flat.shape[0]))
    return flat.reshape(-1, PACK_W)


def _pack_lead(arrs, dtype, row_mult):
    flat = jnp.concatenate([a.reshape(N_DEV, -1).astype(dtype) for a in arrs], axis=1)
    unit = row_mult * PACK_W
    total = -(-flat.shape[1] // unit) * unit
    flat = jnp.pad(flat, ((0, 0), (0, total - flat.shape[1])))
    return flat.reshape(N_DEV, -1, PACK_W)


def _unpack(buf, shapes, lead=()):
    flat = buf.reshape(lead + (-1,))
    out, off = [], 0
    for s in shapes:
        n = int(np.prod(s))
        out.append(flat[..., off:off + n].reshape(lead + tuple(s)))
        off += n
    return out


def _unshard(g, kind):
    if kind == "col":
        g = jnp.moveaxis(g, 0, -2)
        return g.reshape(g.shape[:-2] + (g.shape[-2] * g.shape[-1],))
    g = jnp.moveaxis(g, 0, 1)
    return g.reshape((g.shape[0], g.shape[1] * g.shape[2]) + g.shape[3:])


def _shard(full, kind):
    if kind == "col":
        s = full.reshape(full.shape[:-1] + (N_DEV, full.shape[-1] // N_DEV))
        return jnp.moveaxis(s, -2, 0)
    s = full.reshape((full.shape[0], N_DEV, full.shape[1] // N_DEV) + full.shape[2:])
    return jnp.moveaxis(s, 1, 0)


class _Geo:
    def __init__(self, bsz, seq):
        self.bsz, self.seq = bsz, seq
        self.pad = (-N_META) % SSM_CHUNK
        self.lp = self.pad + N_META + seq
        self.t0 = self.pad + N_META
        self.nq = 1 + seq // ATT_BLK
        assert self.t0 == LANE and seq % ATT_BLK == 0 and self.lp % SSM_CHUNK == 0
        self.nrows = bsz * self.lp
        self.nc = self.lp // SSM_CHUNK
        self.nh = SSM_D_INNER // SSM_HEAD_DIM
        self.gn = SSM_GROUPS * SSM_STATE
        self.cd = SSM_D_INNER + 2 * self.gn
        self.hq = MLA_HEADS * LANE
        order = (("z", SSM_D_INNER), ("g_ssm", D_MODEL), ("g_mla", D_MODEL), ("xs", SSM_D_INNER), ("bm", self.gn),
                 ("cm", self.gn), ("c_q", MLA_Q_LORA), ("c_kv", MLA_KV_LORA), ("dt", LANE), ("k_rope", LANE))
        self.col, off = {}, 0
        for nm, w in order:
            assert off % w == 0, (nm, off, w)
            self.col[nm] = (off, w)
            off += w
        self.pw = off
        assert self.nh <= LANE and MLA_ROPE == 64 and MLA_NOPE == LANE and MLA_V == LANE
        self.tr = _pick(self.lp, (1088, 768, 544, 512, 384, 272, 256, 128))
        self.tr_wide = _pick(self.lp, (544, 384, 272, 256, 128))

    def cb(self, nm):
        off, w = self.col[nm]
        return off // w

    def w_in_runs(self, shard_w):
        nh, half = self.nh, MLA_ROPE // 2
        src, pieces = 0, []
        for nm, n in (("z", SSM_D_INNER), ("xs", SSM_D_INNER), ("bm", self.gn), ("cm", self.gn), ("dt", nh),
                      ("c_q", MLA_Q_LORA), ("c_kv", MLA_KV_LORA), ("k_rope", MLA_ROPE), ("g_ssm", D_MODEL),
                      ("g_mla", D_MODEL)):
            dst = self.col[nm][0]
            if nm == "k_rope":
                pieces += [(src, half, dst), (src + half, half, dst + 2 * half)]
            else:
                pieces.append((src, n, dst))
            src += n
        assert src == shard_w * N_DEV
        runs = []
        for a, n, dst in pieces:
            for j in range(N_DEV):
                lo, hi = max(a, j * shard_w), min(a + n, (j + 1) * shard_w)
                if lo < hi:
                    runs.append((j, lo - j * shard_w, hi - lo, dst + lo - a))
        return runs


def _slot(a):
    h = MLA_ROPE // 2
    z = jnp.zeros(a.shape[:-1] + (h,), a.dtype)
    return jnp.concatenate([a[..., :h], z, a[..., h:], z], axis=-1)


def _unslot(a):
    h = MLA_ROPE // 2
    return jnp.concatenate([a[..., :h], a[..., 2 * h:3 * h]], axis=-1)


def _prep_layer(geo, wl):
    nh = geo.nh
    p = {}
    if "w_uq" in wl:
        uq = wl["w_uq"].reshape(MLA_Q_LORA, MLA_HEADS, MLA_NOPE + MLA_ROPE)
        p["w_qn"] = uq[..., :MLA_NOPE].reshape(MLA_Q_LORA, geo.hq)
        p["w_qp"] = _slot(uq[..., MLA_NOPE:]).reshape(MLA_Q_LORA, geo.hq)
    if "w_ukv" in wl:
        ukv = wl["w_ukv"].reshape(MLA_KV_LORA, MLA_HEADS, MLA_NOPE + MLA_V)
        p["w_k"] = ukv[..., :MLA_NOPE].reshape(MLA_KV_LORA, geo.hq)
        p["w_v"] = ukv[..., MLA_NOPE:].reshape(MLA_KV_LORA, geo.hq)
    for nm in ("w_in_pt", "conv_w", "w_branch_ssm", "w_branch_mla", "w_out", "w_mlp_up", "w_mlp_down"):
        if nm in wl:
            p[nm] = wl[nm]
    for nm in ("norm_mix_w", "conv_b", "ssm_norm_w", "q_norm_w", "kv_norm_w", "norm_mlp_w"):
        if nm in wl:
            p[nm] = wl[nm].reshape(1, -1)
    if "dt_bias" in wl:
        p["dt_bias"] = jnp.pad(wl["dt_bias"], (0, LANE - nh)).reshape(1, LANE)
        p["a_log"] = jnp.pad(wl["a_log"], (0, LANE - nh)).reshape(1, LANE)
        p["d_skip_full"] = jnp.repeat(wl["d_skip"], SSM_HEAD_DIM).reshape(1, SSM_D_INNER)
    return p


def _unprep_grads(geo, g):
    nh = geo.nh
    out = {}
    if "w_qn" in g:
        qn = g["w_qn"].reshape(MLA_Q_LORA, MLA_HEADS, MLA_NOPE)
        qp = _unslot(g["w_qp"].reshape(MLA_Q_LORA, MLA_HEADS, LANE))
        out["w_uq"] = jnp.concatenate([qn, qp], axis=-1).reshape(MLA_Q_LORA, -1)
    if "w_k" in g:
        wk = g["w_k"].reshape(MLA_KV_LORA, MLA_HEADS, MLA_NOPE)
        wv = g["w_v"].reshape(MLA_KV_LORA, MLA_HEADS, MLA_V)
        out["w_ukv"] = jnp.concatenate([wk, wv], axis=-1).reshape(MLA_KV_LORA, -1)
    for nm in ("w_in_pt", "w_branch_ssm", "w_branch_mla", "w_out", "w_mlp_up", "w_mlp_down", "conv_w"):
        if nm in g:
            out[nm] = g[nm]
    for nm in ("norm_mix_w", "conv_b", "ssm_norm_w", "q_norm_w", "kv_norm_w", "norm_mlp_w"):
        if nm in g:
            out[nm] = g[nm].reshape(-1)
    if "dt_bias" in g:
        out["dt_bias"] = g["dt_bias"].reshape(-1)[:nh]
        out["a_log"] = g["a_log"].reshape(-1)[:nh]
        out["d_skip"] = g["d_skip_full"].reshape(nh, SSM_HEAD_DIM).sum(-1)
    return out


def _tables(geo):
    pos = jnp.arange(geo.lp, dtype=F32) - geo.pad
    inv = ROPE_THETA ** (-jnp.arange(0, MLA_ROPE, 2, dtype=F32) / MLA_ROPE)
    ang = pos[:, None] * inv[None, :]
    cos, sin = jnp.cos(ang), jnp.sin(ang)
    z = jnp.zeros_like(cos)
    rows = jnp.arange(geo.lp)[:, None]
    return {"cos": jnp.concatenate([cos, z, cos, z], axis=-1), "sin": jnp.concatenate([-sin, z, sin, z], axis=-1),
            "valid": (rows >= geo.pad).astype(F32), "token": (rows >= geo.pad + N_META).astype(F32)}


def _w_in_assemble(geo, gathered):
    _, sw, d = gathered.shape
    runs = geo.w_in_runs(sw)
    tl = _pick(d, (256, 128))

    def body(x_ref, o_ref):
        o_ref[...] = jnp.zeros_like(o_ref)
        for j, s0, n, d0 in runs:
            o_ref[d0:d0 + n, :] = x_ref[j, s0:s0 + n, :]

    return pl.pallas_call(
        body, name="w_in_assemble", grid=(d // tl,), in_specs=[pl.BlockSpec((N_DEV, sw, tl), lambda i: (0, 0, i))],
        out_specs=pl.BlockSpec((geo.pw, tl), lambda i: (0, i)),
        out_shape=jax.ShapeDtypeStruct((geo.pw, d), gathered.dtype), compiler_params=_cparams(("parallel",)))(gathered)


def _w_in_split(geo, g_padded, sw):
    d = g_padded.shape[1]
    runs = geo.w_in_runs(sw)
    tl = _pick(d, (256, 128))

    def body(x_ref, o_ref):
        for j, s0, n, d0 in runs:
            o_ref[j, s0:s0 + n, :] = x_ref[d0:d0 + n, :]

    return pl.pallas_call(
        body, name="w_in_split", grid=(d // tl,), in_specs=[pl.BlockSpec((geo.pw, tl), lambda i: (0, i))],
        out_specs=pl.BlockSpec((N_DEV, sw, tl), lambda i: (0, 0, i)),
        out_shape=jax.ShapeDtypeStruct((N_DEV, sw, d), g_padded.dtype),
        compiler_params=_cparams(("parallel",)))(g_padded)


def _conv_cols(geo, cbw):
    x0 = geo.col["xs"][0]
    assert geo.col["bm"][0] == x0 + SSM_D_INNER and geo.col["cm"][0] == geo.col["bm"][0] + geo.gn and x0 % cbw == 0
    return lambda j: x0 // cbw + j


def _conv_taps(x):
    return [pltpu.roll(x, SSM_CONV - 1 - k, axis=0) for k in range(SSM_CONV - 1)] + [x]


def _conv_pre(x, w_ref, b_ref, taps=None):
    taps = _conv_taps(x) if taps is None else taps
    acc = b_ref[...]
    for k in range(SSM_CONV):
        acc = acc + taps[k] * w_ref[k:k + 1, :]
    return acc


def _conv_fwd(geo, proj, conv_w, conv_b):
    cbw = 256
    colmap = _conv_cols(geo, cbw)
    lp, pad = geo.lp, geo.pad

    def body(x_ref, w_ref, b_ref, o_ref):
        valid = (lax.broadcasted_iota(jnp.int32, (lp, 1), 0) >= pad).astype(F32)
        o_ref[...] = (_silu(_conv_pre(x_ref[...].astype(F32), w_ref, b_ref)) * valid).astype(o_ref.dtype)

    return pl.pallas_call(
        body, name="conv_fwd", grid=(geo.bsz, geo.cd // cbw),
        in_specs=[pl.BlockSpec((lp, cbw), lambda b, j: (b, colmap(j))),
                  pl.BlockSpec((SSM_CONV, cbw), lambda b, j: (0, j)), pl.BlockSpec((1, cbw), lambda b, j: (0, j))],
        out_specs=pl.BlockSpec((lp, cbw), lambda b, j: (b, j)),
        out_shape=jax.ShapeDtypeStruct((geo.nrows, geo.cd), MXU_DTYPE),
        compiler_params=_cparams(("parallel", "parallel")))(proj, conv_w, conv_b)


def _conv_bwd(geo, proj, conv_w, conv_b, dxc, dproj):
    cbw = 256
    colmap = _conv_cols(geo, cbw)
    lp, pad = geo.lp, geo.pad

    def body(x_ref, w_ref, b_ref, dy_ref, _, dx_ref, gw_ref, gb_ref):
        b = pl.program_id(1)
        valid = (lax.broadcasted_iota(jnp.int32, (lp, 1), 0) >= pad).astype(F32)
        taps = _conv_taps(x_ref[...].astype(F32))
        pre = _conv_pre(None, w_ref, b_ref, taps)
        sig = _sigmoid(pre)
        dpre = dy_ref[...] * (sig * (1.0 + pre * (1.0 - sig))) * valid
        dx = dpre * w_ref[SSM_CONV - 1:SSM_CONV, :]
        for k in range(SSM_CONV - 1):
            dx = dx + pltpu.roll(dpre, lp - (SSM_CONV - 1 - k), axis=0) * w_ref[k:k + 1, :]
        gws = [jnp.sum(dpre * taps[k], axis=0, keepdims=True) for k in range(SSM_CONV)]
        dx_ref[...] = (dx * valid).astype(dx_ref.dtype)

        @pl.when(b == 0)
        def _():
            gw_ref[...] = jnp.zeros_like(gw_ref)
            gb_ref[...] = jnp.zeros_like(gb_ref)

        for k in range(SSM_CONV):
            gw_ref[k:k + 1, :] += gws[k]
        gb_ref[...] += jnp.sum(dpre, axis=0, keepdims=True)

    return pl.pallas_call(
        body, name="conv_bwd", grid=(geo.cd // cbw, geo.bsz),
        in_specs=[pl.BlockSpec((lp, cbw), lambda j, b: (b, colmap(j))),
                  pl.BlockSpec((SSM_CONV, cbw), lambda j, b: (0, j)), pl.BlockSpec((1, cbw), lambda j, b: (0, j)),
                  pl.BlockSpec((lp, cbw), lambda j, b: (b, j)), pl.BlockSpec(memory_space=pl.ANY)],
        out_specs=[pl.BlockSpec((lp, cbw), lambda j, b: (b, colmap(j))),
                   pl.BlockSpec((SSM_CONV, cbw), lambda j, b: (0, j)), pl.BlockSpec((1, cbw), lambda j, b: (0, j))],
        out_shape=[jax.ShapeDtypeStruct(dproj.shape, dproj.dtype),
                   jax.ShapeDtypeStruct((SSM_CONV, geo.cd), F32), jax.ShapeDtypeStruct((1, geo.cd), F32)],
        input_output_aliases={4: 0},
        compiler_params=_cparams(("parallel", "arbitrary")))(proj, conv_w, conv_b, dxc, dproj)


def _tri(q):
    r = lax.broadcasted_iota(jnp.int32, (q, q), 0)
    c = lax.broadcasted_iota(jnp.int32, (q, q), 1)
    return r >= c


def _ssd_pre(dtr, dtb, alog, valid):
    dt = _softplus(dtr + dtb) * valid
    adt = dt * (-jnp.exp(alog))
    a_cs = _dot(_tri(SSM_CHUNK).astype(F32), adt, 1, 0, precision=lax.Precision.HIGHEST)
    return dt, a_cs


def _ssd_specs(geo, rev):
    nc, q = geo.nc, SSM_CHUNK
    ci = (lambda c: nc - 1 - c) if rev else (lambda c: c)
    nxb = SSM_D_INNER // geo.gn
    return [pl.BlockSpec((q, SSM_D_INNER), lambda b, c: (b * nc + ci(c), 0)),
            pl.BlockSpec((q, geo.gn), lambda b, c: (b * nc + ci(c), nxb)),
            pl.BlockSpec((q, geo.gn), lambda b, c: (b * nc + ci(c), nxb + 1)),
            pl.BlockSpec((q, LANE), lambda b, c: (b * nc + ci(c), 0)),
            pl.BlockSpec((1, LANE), lambda b, c: (0, 0)), pl.BlockSpec((1, LANE), lambda b, c: (0, 0))], ci


def _expand_heads(cols, nh):
    per = LANE // SSM_HEAD_DIM
    lane = lax.broadcasted_iota(jnp.int32, (1, LANE), 1)
    blocks = []
    for j in range(nh // per):
        blk = jnp.broadcast_to(cols[:, j * per:j * per + 1], (cols.shape[0], LANE))
        for k in range(1, per):
            blk = jnp.where(lane >= k * SSM_HEAD_DIM, cols[:, j * per + k:j * per + k + 1], blk)
        blocks.append(blk)
    return jnp.concatenate(blocks, axis=1)


def _head_maps(geo):
    e = (jnp.arange(SSM_D_INNER)[None, :] // SSM_HEAD_DIM == jnp.arange(LANE)[:, None]).astype(F32)
    return e, e.T


def _ssd_fwd_g(geo, xc, proj, dt_bias, a_log):
    q, p, n, e = SSM_CHUNK, SSM_HEAD_DIM, SSM_STATE, geo.nh // SSM_GROUPS
    nc, pad, gw = geo.nc, geo.pad, SSM_D_INNER // SSM_GROUPS
    in_specs, _ = _ssd_specs(geo, False)

    def body(xs_ref, b_ref, c_ref, dtr_ref, dtb_ref, alog_ref, y_ref, sp_ref, state, xdt_s, y_s):
        c = pl.program_id(1)

        @pl.when(c == 0)
        def _():
            state[...] = jnp.zeros_like(state)

        sp_ref[...] = state[...]
        inert = (c + 1) * q <= pad

        @pl.when(inert)
        def _():
            y_ref[...] = jnp.zeros_like(y_ref)

        @pl.when(jnp.logical_not(inert))
        def _():
            valid = (c * q + lax.broadcasted_iota(jnp.int32, (q, 1), 0) >= pad).astype(F32)
            dt, a_cs = _ssd_pre(dtr_ref[...], dtb_ref[...], alog_ref[...], valid)
            a_cst = a_cs.T
            dt_x, a_x = _expand_heads(dt, geo.nh), _expand_heads(a_cs, geo.nh)
            tri = _tri(q)
            for g in range(SSM_GROUPS):
                gs = slice(g * gw, (g + 1) * gw)
                bg, cg = b_ref[:, g * n:(g + 1) * n], c_ref[:, g * n:(g + 1) * n]
                a_g = a_x[:, gs]
                a_last = a_g[q - 1:q, :]
                xdt_g = xs_ref[:, gs] * dt_x[:, gs]
                xdt_s[:, gs] = xdt_g
                s_g = state[:, gs]
                y_s[:, gs] = _mxdot(cg, s_g, 1, 0) * jnp.exp(a_g)
                state[:, gs] = s_g * jnp.exp(a_last) + _mxdot(bg, xdt_g * jnp.exp(a_last - a_g), 0, 0)
                cb = _mxdot(cg, bg, 1, 1)
                for hh in range(e):
                    h = g * e + hh
                    hs = slice(h * p, (h + 1) * p)
                    ldec = jnp.exp(jnp.where(tri, a_cs[:, h:h + 1] - a_cst[h:h + 1, :], -jnp.inf))
                    y_s[:, hs] += _mxdot(cb * ldec, xdt_s[:, hs], 1, 0)
            y_ref[...] = y_s[...].astype(y_ref.dtype)

    return pl.pallas_call(
        body, name="ssd_fwd", grid=(geo.bsz, nc), in_specs=in_specs,
        out_specs=[pl.BlockSpec((q, SSM_D_INNER), lambda b, c: (b * nc + c, 0)),
                   pl.BlockSpec((n, SSM_D_INNER), lambda b, c: (b * nc + c, 0))],
        out_shape=[jax.ShapeDtypeStruct((geo.nrows, SSM_D_INNER), MXU_DTYPE),
                   jax.ShapeDtypeStruct((geo.bsz * nc * n, SSM_D_INNER), F32)],
        scratch_shapes=[pltpu.VMEM((n, SSM_D_INNER), F32), pltpu.VMEM((q, SSM_D_INNER), F32),
                        pltpu.VMEM((q, SSM_D_INNER), F32)],
        compiler_params=_cparams(("parallel", "arbitrary")))(xc, xc, xc, proj, dt_bias, a_log)


def _ssd_bwd_g(geo, xc, proj, dt_bias, a_log, s_prev_all, dy, dxs_skip, dproj):
    q, p, n, e = SSM_CHUNK, SSM_HEAD_DIM, SSM_STATE, geo.nh // SSM_GROUPS
    nc, pad, di, gn, gw = geo.nc, geo.pad, SSM_D_INNER, geo.gn, SSM_D_INNER // SSM_GROUPS
    in_specs, ci = _ssd_specs(geo, True)
    row_spec = pl.BlockSpec((q, di), lambda b, c: (b * nc + ci(c), 0))
    e_map, _ = _head_maps(geo)
    in_specs += [pl.BlockSpec((n, di), lambda b, c: (b * nc + ci(c), 0)), row_spec, row_spec,
                 pl.BlockSpec((LANE, di), lambda b, c: (0, 0)), pl.BlockSpec(memory_space=pl.ANY)]

    def body(xs_ref, b_ref, c_ref, dtr_ref, dtb_ref, alog_ref, sp_ref, dy_ref, dsk_ref, e_ref, _,
             dxc_ref, ddt_ref, gdtb_ref, galog_ref, dstate, xdt_s, dxdt_s):
        step = pl.program_id(1)
        first = jnp.logical_and(pl.program_id(0) == 0, step == 0)
        c = nc - 1 - step

        @pl.when(step == 0)
        def _():
            dstate[...] = jnp.zeros_like(dstate)

        @pl.when(first)
        def _():
            gdtb_ref[...] = jnp.zeros_like(gdtb_ref)
            galog_ref[...] = jnp.zeros_like(galog_ref)

        inert = (c + 1) * q <= pad

        @pl.when(inert)
        def _():
            dxc_ref[...] = jnp.zeros_like(dxc_ref)
            ddt_ref[...] = jnp.zeros_like(ddt_ref)

        @pl.when(jnp.logical_not(inert))
        def _():
            valid = (c * q + lax.broadcasted_iota(jnp.int32, (q, 1), 0) >= pad).astype(F32)
            dtr, dtb, alog = dtr_ref[...], dtb_ref[...], alog_ref[...]
            dt, a_cs = _ssd_pre(dtr, dtb, alog, valid)
            a_cst = a_cs.T
            dt_x, a_x = _expand_heads(dt, geo.nh), _expand_heads(a_cs, geo.nh)
            tri = _tri(q)
            lane = lax.broadcasted_iota(jnp.int32, (1, LANE), 1)
            sub = lax.broadcasted_iota(jnp.int32, (LANE, 1), 0)
            d_dt = jnp.zeros((q, LANE), F32)
            d_acs = jnp.zeros((q, LANE), F32)
            d_acst = jnp.zeros((LANE, q), F32)
            d_last = jnp.zeros((1, LANE), F32)
            for g in range(SSM_GROUPS):
                gs = slice(g * gw, (g + 1) * gw)
                bg, cg = b_ref[:, g * n:(g + 1) * n], c_ref[:, g * n:(g + 1) * n]
                seg = lambda v: _mxdot(v, e_ref[:, gs], 1, 1)
                a_g, dt_g, x_g, dy_g = a_x[:, gs], dt_x[:, gs], xs_ref[:, gs], dy_ref[:, gs]
                e_col, e_last, dec = jnp.exp(a_g), jnp.exp(a_g[q - 1:q, :]), jnp.exp(a_g[q - 1:q, :] - a_g)
                xdt_g = x_g * dt_g
                xdt_s[:, gs] = xdt_g
                s_g, ds_g = sp_ref[:, gs], dstate[:, gs]
                cs = _mxdot(cg, s_g, 1, 0)
                d_cs = dy_g * e_col
                d_acs = d_acs + seg(d_cs * cs)
                d_cg = _mxdot(d_cs, s_g, 1, 1)
                dstate[:, gs] = _mxdot(cg, d_cs, 0, 0) + ds_g * e_last
                dl_x = jnp.sum(ds_g * s_g, axis=0, keepdims=True) * e_last
                d_last = d_last + seg(jnp.broadcast_to(dl_x, (8, gw)))[:1]
                gmat = _mxdot(bg, ds_g, 1, 0)
                xd = xdt_g * dec
                d_bg = _mxdot(xd, ds_g, 1, 1)
                d_dec = seg(xd * gmat)
                d_acs = d_acs - d_dec
                d_last = d_last + jnp.sum(d_dec, axis=0, keepdims=True)
                dxdt_s[:, gs] = dec * gmat
                cb = _mxdot(cg, bg, 1, 1)
                d_cb = jnp.zeros((q, q), F32)
                for hh in range(e):
                    h = g * e + hh
                    hs = slice(h * p, (h + 1) * p)
                    ldec = jnp.exp(jnp.where(tri, a_cs[:, h:h + 1] - a_cst[h:h + 1, :], -jnp.inf))
                    dyh = dy_ref[:, hs]
                    d_m = _mxdot(dyh, xdt_s[:, hs], 1, 1)
                    dxdt_s[:, hs] += _mxdot(cb * ldec, dyh, 0, 0)
                    d_cb = d_cb + d_m * ldec
                    d_diff = d_m * cb * ldec
                    d_acs = d_acs + jnp.sum(d_diff, axis=1, keepdims=True) * (lane == h).astype(F32)
                    d_acst = d_acst - (sub == h).astype(F32) * jnp.sum(d_diff, axis=0, keepdims=True)
                d_xdt = dxdt_s[:, gs]
                dxc_ref[:, gs] = d_xdt * dt_g + dsk_ref[:, gs]
                d_dt = d_dt + seg(d_xdt * x_g)
                dxc_ref[:, di + g * n:di + (g + 1) * n] = d_bg + _mxdot(d_cb, cg, 0, 0)
                dxc_ref[:, di + gn + g * n:di + gn + (g + 1) * n] = d_cg + _mxdot(d_cb, bg, 1, 0)
            is_last = (lax.broadcasted_iota(jnp.int32, (q, 1), 0) == q - 1).astype(F32)
            d_acs = d_acs + d_acst.T + is_last * d_last
            d_adt = _dot(_tri(q).astype(F32), d_acs, 0, 0, precision=lax.Precision.HIGHEST)
            a = -jnp.exp(alog)
            d_dt = d_dt + d_adt * a
            d_dtr = d_dt * valid * _sigmoid(dtr + dtb)
            ddt_ref[...] = d_dtr.astype(ddt_ref.dtype)
            gdtb_ref[...] += jnp.sum(d_dtr, axis=0, keepdims=True)
            galog_ref[...] += jnp.sum(d_adt * dt, axis=0, keepdims=True) * a

    vec = pl.BlockSpec((1, LANE), lambda b, c: (0, 0))
    return pl.pallas_call(
        body, name="ssd_bwd", grid=(geo.bsz, nc), in_specs=in_specs,
        out_specs=[pl.BlockSpec((q, geo.cd), lambda b, c: (b * nc + ci(c), 0)),
                   pl.BlockSpec((q, LANE), lambda b, c: (b * nc + ci(c), geo.cb("dt"))), vec, vec],
        out_shape=[jax.ShapeDtypeStruct((geo.nrows, geo.cd), F32), jax.ShapeDtypeStruct(dproj.shape, dproj.dtype),
                   jax.ShapeDtypeStruct((1, LANE), F32), jax.ShapeDtypeStruct((1, LANE), F32)],
        scratch_shapes=[pltpu.VMEM((n, di), F32), pltpu.VMEM((q, di), F32), pltpu.VMEM((q, di), F32)],
        input_output_aliases={10: 1},
        compiler_params=_cparams(("arbitrary", "arbitrary")))(
            xc, xc, xc, proj, dt_bias, a_log, s_prev_all, dy, dxs_skip, e_map, dproj)


BIAS_LANE = MLA_ROPE // 2
KEY_OFF = -1e30
ATT_SCALE = (MLA_NOPE + MLA_ROPE) ** -0.5


def _row_t(col):
    return jnp.broadcast_to(col, (col.shape[0], LANE)).T[:8]


def _att_tile(geo, i):
    return (0, geo.t0) if i == 0 else (geo.t0 + (i - 1) * ATT_BLK, ATT_BLK)


def _attn_fwd4(geo, qn, qp, kn, kp, v):
    t, lp, nq = ATT_BLK, geo.lp, geo.nq

    def body(qn_ref, qp_ref, kn_ref, kp_ref, v_ref, o_ref, lse_ref, k_ref):
        qi = pl.program_id(2)

        @pl.when(qi == 0)
        def _():
            k_ref[:, :LANE] = kn_ref[...]
            k_ref[:, LANE:] = kp_ref[...]

        for i in range(nq):
            @pl.when(qi == i)
            def _(i=i):
                r0, rows = _att_tile(geo, i)
                w = r0 + rows
                q = jnp.concatenate([qn_ref[r0:w, :], qp_ref[r0:w, :]], axis=1)
                s = _mxdot(q, k_ref[0:w, :], 1, 1) * ATT_SCALE
                keys = lax.broadcasted_iota(jnp.int32, (rows, w), 1)
                s = jnp.where(keys <= r0 + lax.broadcasted_iota(jnp.int32, (rows, w), 0), s, -jnp.inf)
                m = jnp.max(s, axis=1, keepdims=True)
                pr = jnp.exp(s - m)
                l = jnp.sum(pr, axis=1, keepdims=True)
                o_ref[r0:w, :] = (_mxdot(pr, v_ref[0:w, :], 1, 0) / l).astype(o_ref.dtype)
                lse_ref[0, 0, 0, :, 0:rows] = _row_t(m + jnp.log(l))

    seq = pl.BlockSpec((lp, LANE), lambda b, h, i: (b, h))
    return pl.pallas_call(
        body, name="attn_fwd", grid=(geo.bsz, MLA_HEADS, nq),
        in_specs=[seq, seq, seq, pl.BlockSpec((lp, LANE), lambda b, h, i: (b, 0)), seq],
        out_specs=[seq, pl.BlockSpec((1, 1, 1, 8, t), lambda b, h, i: (b, h, i, 0, 0))],
        out_shape=[jax.ShapeDtypeStruct((geo.nrows, geo.hq), MXU_DTYPE),
                   jax.ShapeDtypeStruct((geo.bsz, MLA_HEADS, nq, 8, t), F32)],
        scratch_shapes=[pltpu.VMEM((lp, 2 * LANE), MXU_DTYPE)],
        compiler_params=_cparams(("parallel", "parallel", "arbitrary")))(qn, qp, kn, kp, v)


def _attn_bwd4(geo, qn, qp, kn, kp, v, d_o, o, lse):
    t, lp, nq = ATT_BLK, geo.lp, geo.nq

    def body(qn_ref, qp_ref, kn_ref, kp_ref, v_ref, do_ref, o_ref, lse_ref,
             dqn_ref, dqp_ref, dkn_ref, dkp_ref, dv_ref, q_ref, dl_s):
        kj = pl.program_id(2)

        @pl.when(kj == 0)
        def _():
            q_ref[:, :LANE] = qn_ref[...]
            q_ref[:, LANE:] = qp_ref[...]
            dqn_ref[...] = jnp.zeros_like(dqn_ref)
            dqp_ref[...] = jnp.zeros_like(dqp_ref)
            for i in range(nq):
                r0, rows = _att_tile(geo, i)
                dl_s[i, :, 0:rows] = _row_t(jnp.sum(do_ref[r0:r0 + rows, :].astype(F32) * o_ref[r0:r0 + rows, :].astype(F32),
                                                    axis=1, keepdims=True))

        def per_query(ref, first):
            return jnp.concatenate([ref[i][:1, 0:_att_tile(geo, i)[1]] for i in range(first, nq)], axis=1)

        for i in range(nq):
            @pl.when(kj == i)
            def _(i=i):
                k0, kw = _att_tile(geo, i)
                k = jnp.concatenate([kn_ref[k0:k0 + kw, :], kp_ref[k0:k0 + kw, :]], axis=1)
                vv = v_ref[k0:k0 + kw, :]
                q, d_o_blk = q_ref[k0:lp, :], do_ref[k0:lp, :]
                wq = lp - k0
                st = _mxdot(k, q, 1, 1) * ATT_SCALE
                keys = lax.broadcasted_iota(jnp.int32, (kw, wq), 0)
                st = jnp.where(keys <= lax.broadcasted_iota(jnp.int32, (kw, wq), 1), st, -jnp.inf)
                pt = jnp.exp(st - per_query(lse_ref.at[0, 0], i))
                dst = pt * (_mxdot(vv, d_o_blk, 1, 1) - per_query(dl_s, i)) * ATT_SCALE
                dq = _mxdot(dst, k, 0, 0)
                dqn_ref[k0:lp, :] += dq[:, :LANE]
                dqp_ref[k0:lp, :] += dq[:, LANE:]
                dk = _mxdot(dst, q, 1, 0)
                dkn_ref[k0:k0 + kw, :] = dk[:, :LANE].astype(dkn_ref.dtype)
                dkp_ref[k0:k0 + kw, :] = dk[:, LANE:]
                dv_ref[k0:k0 + kw, :] = _mxdot(pt, d_o_blk, 1, 0).astype(dv_ref.dtype)

    seq = pl.BlockSpec((lp, LANE), lambda b, h, j: (b, h))
    return pl.pallas_call(
        body, name="attn_bwd", grid=(geo.bsz, MLA_HEADS, nq),
        in_specs=[seq, seq, seq, pl.BlockSpec((lp, LANE), lambda b, h, j: (b, 0)), seq, seq, seq,
                  pl.BlockSpec((1, 1, nq, 8, t), lambda b, h, j: (b, h, 0, 0, 0))],
        out_specs=[seq, seq, seq, seq, seq],
        out_shape=[jax.ShapeDtypeStruct((geo.nrows, geo.hq), F32), jax.ShapeDtypeStruct((geo.nrows, geo.hq), F32),
                   jax.ShapeDtypeStruct((geo.nrows, geo.hq), MXU_DTYPE), jax.ShapeDtypeStruct((geo.nrows, geo.hq), F32),
                   jax.ShapeDtypeStruct((geo.nrows, geo.hq), MXU_DTYPE)],
        scratch_shapes=[pltpu.VMEM((lp, 2 * LANE), MXU_DTYPE), pltpu.VMEM((nq, 8, t), F32)],
        compiler_params=_cparams(("parallel", "parallel", "arbitrary")))(qn, qp, kn, kp, v, d_o, o, lse)


def _rope(x, cos, sin):
    return x * cos + pltpu.roll(x, LANE // 2, axis=1) * sin


def _rope_t(dx, cos, sin):
    return dx * cos + pltpu.roll(dx * sin, LANE // 2, axis=1)


def _per_head(f):
    def fn(x, cos, sin):
        return (jnp.concatenate([f(x[:, h * LANE:(h + 1) * LANE], cos, sin) for h in range(MLA_HEADS)], axis=1),)
    return fn


def _layer_fwd(geo, h, w, tab, late=None):
    nr, tr, trw = geo.nrows, geo.tr, geo.tr_wide
    tb = geo.lp // tr
    rw = functools.partial(_rowwise, nrows=nr)
    s = {"h": h}
    (s["u"],) = rw("rms_mix", lambda x, g: (_rms(x, g),), tr=tr, rows=[(h, D_MODEL, 0)],
                   vecs=[(w["norm_mix_w"], D_MODEL, 0)], outs=[(D_MODEL, D_MODEL, MXU_DTYPE)])
    proj, s["proj_dt"] = _mm("mm_in", s["u"], w["w_in_pt"], tb=True, out_dtype=MXU_DTYPE,
                             side=(geo.col["dt"][0], LANE))
    s["proj"] = proj
    xc = s["xc"] = _conv_fwd(geo, proj, w["conv_w"], w["conv_b"])
    s["y_ssd"], s["s_prev"] = _ssd_fwd_g(geo, xc, s["proj_dt"], w["dt_bias"], w["a_log"])
    gw = SSM_D_INNER // SSM_GROUPS

    def gate_norm(y, x, z, dsk, nw):
        return (_rms((y + x * dsk) * _silu(z.astype(F32)), nw),)

    (s["y_ssm"],) = rw("ssm_gate_norm", gate_norm, tr=tr, ncb=SSM_GROUPS,
                       rows=[(s["y_ssd"], gw, 0), (xc, gw, 0), (proj, gw, geo.col["z"][0] // gw)],
                       vecs=[(w["d_skip_full"], gw, 0), (w["ssm_norm_w"], gw, 0)], outs=[(SSM_D_INNER, gw, MXU_DTYPE)])
    if late is not None:
        w = {**w, **late(s["y_ssm"])}
    (s["cq_n"],) = rw("rms_q", lambda x, g: (_rms(x, g),), tr=tr, rows=[(proj, MLA_Q_LORA, geo.cb("c_q"))],
                      vecs=[(w["q_norm_w"], MLA_Q_LORA, 0)], outs=[(MLA_Q_LORA, MLA_Q_LORA, MXU_DTYPE)])
    (s["ckv_n"],) = rw("rms_kv", lambda x, g: (_rms(x, g),), tr=tr, rows=[(proj, MLA_KV_LORA, geo.cb("c_kv"))],
                       vecs=[(w["kv_norm_w"], MLA_KV_LORA, 0)], outs=[(MLA_KV_LORA, MLA_KV_LORA, MXU_DTYPE)])
    s["qn"] = _mm("mm_qn", s["cq_n"], w["w_qn"], out_dtype=MXU_DTYPE)
    qp_raw = _mm("mm_qp", s["cq_n"], w["w_qp"])
    s["kn"] = _mm("mm_kn", s["ckv_n"], w["w_k"], out_dtype=MXU_DTYPE)
    s["v"] = _mm("mm_v", s["ckv_n"], w["w_v"], out_dtype=MXU_DTYPE)
    bias_lane = lambda: lax.broadcasted_iota(jnp.int32, (1, LANE), 1) == BIAS_LANE
    rope_tabs = [(tab["cos"], LANE, 0), (tab["sin"], LANE, 0)]
    (s["qp"],) = rw("rope_q", _per_head(lambda xp, c, sn: jnp.where(bias_lane(), 1.0, _rope(xp, c, sn))), tr=tr,
                    rows=[(qp_raw, geo.hq, 0)], tabs=rope_tabs, outs=[(geo.hq, geo.hq, MXU_DTYPE)], tab_blocks=tb)
    (s["kp"],) = rw("rope_k", lambda xp, c, sn, valid: (jnp.where(bias_lane(), KEY_OFF * (1.0 - valid),
                                                                 _rope(xp.astype(F32), c, sn)),),
                    tr=tr, rows=[(proj, LANE, geo.cb("k_rope"))], tabs=rope_tabs + [(tab["valid"], 1, 0)],
                    outs=[(LANE, LANE, MXU_DTYPE)], tab_blocks=tb)
    s["o"], s["lse"] = _attn_fwd4(geo, s["qn"], s["qp"], s["kn"], s["kp"], s["v"])
    s["ys_p"] = _mm("mm_bs", s["y_ssm"], w["w_branch_ssm"], out_dtype=MXU_DTYPE)
    s["ym_p"] = _mm("mm_bm", s["o"], w["w_branch_mla"], out_dtype=MXU_DTYPE)

    def gate(gs, gm, ys, ym):
        return (_sigmoid(gs.astype(F32)) * ys + _sigmoid(gm.astype(F32)) * ym,)

    (s["mixed"],) = rw("gate", gate, tr=tr, rows=[(proj, D_MODEL, geo.cb("g_ssm")), (proj, D_MODEL, geo.cb("g_mla")),
                                                  (s["ys_p"], D_MODEL, 0), (s["ym_p"], D_MODEL, 0)],
                       outs=[(D_MODEL, D_MODEL, MXU_DTYPE)])
    s["h2"] = _mm("mm_out", s["mixed"], w["w_out"], add=h)
    (s["vn"],) = rw("rms_mlp", lambda x, g: (_rms(x, g),), tr=tr, rows=[(s["h2"], D_MODEL, 0)],
                    vecs=[(w["norm_mlp_w"], D_MODEL, 0)], outs=[(D_MODEL, D_MODEL, MXU_DTYPE)])
    s["up"], s["act"] = _mm("mm_up", s["vn"], w["w_mlp_up"],
                            epi=(lambda r: (r, jnp.square(jnp.maximum(r, 0.0))), (MXU_DTYPE, MXU_DTYPE)))
    return _mm("mm_down", s["act"], w["w_mlp_down"], add=s["h2"]), s, w


def _layer_bwd(geo, dh3, s, w, tab, mid=None, tail=None, dep=None):
    nr, tr, trw = geo.nrows, geo.tr, geo.tr_wide
    tb = geo.lp // tr
    rw = functools.partial(_rowwise, nrows=nr)
    g = {}
    proj = s["proj"]

    def rms_bwd(x, dy, res, gw):
        _, vjp = jax.vjp(_rms, x.astype(F32), gw)
        dx, dgw = vjp(dy.astype(F32))
        return dx + res, dgw

    def rms_bwd_nores(x, dy, gw):
        _, vjp = jax.vjp(_rms, x.astype(F32), gw)
        return vjp(dy.astype(F32))

    (dup,) = _mm("mm_down_t", dh3, w["w_mlp_down"], tb=True, add=s["up"], dep=dep,
                 epi=(lambda r, up: (r * 2.0 * jnp.maximum(up, 0.0),), (MXU_DTYPE,)))
    g["w_mlp_down"] = _mm("mm_down_g", s["act"], dh3, ta=True, out_dtype=MXU_DTYPE)
    g["w_mlp_up"] = _mm("mm_up_g", s["vn"], dup, ta=True, out_dtype=MXU_DTYPE)
    dvn = _mm("mm_up_t", dup, w["w_mlp_up"], tb=True)
    dh2, g["norm_mlp_w"] = rw("rms_mlp_bwd", rms_bwd, tr=tr,
                              rows=[(s["h2"], D_MODEL, 0), (dvn, D_MODEL, 0), (dh3, D_MODEL, 0)],
                              vecs=[(w["norm_mlp_w"], D_MODEL, 0)], outs=[(D_MODEL, D_MODEL, F32)],
                              reds=[(D_MODEL, D_MODEL)])
    dmixed = _mm("mm_out_t", dh2, w["w_out"], tb=True, out_dtype=MXU_DTYPE)
    g["w_out"] = _mm("mm_out_g", s["mixed"], dh2, ta=True, out_dtype=MXU_DTYPE)

    def gate_bwd(gs, gm, ys, ym, dm):
        f = lambda a, b, c, d: _sigmoid(a) * c + _sigmoid(b) * d
        _, vjp = jax.vjp(f, gs.astype(F32), gm.astype(F32), ys.astype(F32), ym.astype(F32))
        dgs, dgm, dys, dym = vjp(dm.astype(F32))
        return dys, dym, jnp.concatenate([dgs, dgm], axis=1)

    assert geo.col["g_mla"][0] == geo.col["g_ssm"][0] + D_MODEL and geo.col["g_ssm"][0] % (2 * D_MODEL) == 0
    dys_p, dym_p, dproj = rw(
        "gate_bwd", gate_bwd, tr=tr,
        rows=[(proj, D_MODEL, geo.cb("g_ssm")), (proj, D_MODEL, geo.cb("g_mla")), (s["ys_p"], D_MODEL, 0),
              (s["ym_p"], D_MODEL, 0), (dmixed, D_MODEL, 0)],
        outs=[(D_MODEL, D_MODEL, MXU_DTYPE)] * 2 + [(geo.pw, 2 * D_MODEL, MXU_DTYPE, geo.col["g_ssm"][0] // (2 * D_MODEL))])
    g["w_branch_ssm"] = _mm("mm_bs_g", s["y_ssm"], dys_p, ta=True, out_dtype=MXU_DTYPE)
    dy_ssm = _mm("mm_bs_t", dys_p, w["w_branch_ssm"], tb=True, out_dtype=MXU_DTYPE)
    g["w_branch_mla"] = _mm("mm_bm_g", s["o"], dym_p, ta=True, out_dtype=MXU_DTYPE)
    d_o = _mm("mm_bm_t", dym_p, w["w_branch_mla"], tb=True, out_dtype=MXU_DTYPE)
    dqn, dqp, dkn, dkp_h, dv = _attn_bwd4(geo, s["qn"], s["qp"], s["kn"], s["kp"], s["v"], d_o, s["o"], s["lse"])
    rope_tabs = [(tab["cos"], LANE, 0), (tab["sin"], LANE, 0)]
    (dqp_raw,) = rw("rope_q_bwd", _per_head(_rope_t), tr=tr, rows=[(dqp, geo.hq, 0)], tabs=rope_tabs,
                    outs=[(geo.hq, geo.hq, MXU_DTYPE)], tab_blocks=tb)

    def rope_k_bwd(x, c, sn):
        tot = x[:, :LANE]
        for hd in range(1, MLA_HEADS):
            tot = tot + x[:, hd * LANE:(hd + 1) * LANE]
        return (_rope_t(tot, c, sn),)

    (dproj,) = rw("rope_k_bwd", rope_k_bwd, tr=tr, rows=[(dkp_h, geo.hq, 0)], tabs=rope_tabs,
                  outs=[(geo.pw, LANE, MXU_DTYPE, geo.cb("k_rope"), dproj)], tab_blocks=tb)
    g["w_qn"] = _mm("mm_qn_g", s["cq_n"], dqn, ta=True, out_dtype=MXU_DTYPE)
    g["w_qp"] = _mm("mm_qp_g", s["cq_n"], dqp_raw, ta=True, out_dtype=MXU_DTYPE)
    dcq_n = _mm("mm_qp_t", dqp_raw, w["w_qp"], tb=True, add=_mm("mm_qn_t", dqn, w["w_qn"], tb=True))
    g["w_k"] = _mm("mm_kn_g", s["ckv_n"], dkn, ta=True, out_dtype=MXU_DTYPE)
    g["w_v"] = _mm("mm_v_g", s["ckv_n"], dv, ta=True, out_dtype=MXU_DTYPE)
    dckv_n = _mm("mm_v_t", dv, w["w_v"], tb=True, add=_mm("mm_kn_t", dkn, w["w_k"], tb=True))
    dproj, g["q_norm_w"] = rw("rms_q_bwd", rms_bwd_nores, tr=tr,
                              rows=[(proj, MLA_Q_LORA, geo.cb("c_q")), (dcq_n, MLA_Q_LORA, 0)],
                              vecs=[(w["q_norm_w"], MLA_Q_LORA, 0)],
                              outs=[(geo.pw, MLA_Q_LORA, MXU_DTYPE, geo.cb("c_q"), dproj)], reds=[(MLA_Q_LORA, MLA_Q_LORA)])
    dproj, g["kv_norm_w"] = rw("rms_kv_bwd", rms_bwd_nores, tr=tr,
                               rows=[(proj, MLA_KV_LORA, geo.cb("c_kv")), (dckv_n, MLA_KV_LORA, 0)],
                               vecs=[(w["kv_norm_w"], MLA_KV_LORA, 0)],
                               outs=[(geo.pw, MLA_KV_LORA, MXU_DTYPE, geo.cb("c_kv"), dproj)],
                               reds=[(MLA_KV_LORA, MLA_KV_LORA)])
    gw_ = SSM_D_INNER // SSM_GROUPS
    d_skip_full = w["d_skip_full"] if mid is None else w["d_skip_full"] + mid(g)[0, 0]

    def gate_norm_bwd(y, x, z, dy, dsk, nw):
        f = lambda y_, x_, z_, dsk_, nw_: _rms((y_ + x_ * dsk_) * _silu(z_), nw_)
        _, vjp = jax.vjp(f, y.astype(F32), x.astype(F32), z.astype(F32), dsk, nw)
        dy_, dx_, dz_, ddsk, dnw = vjp(dy.astype(F32))
        return dy_, dx_, dz_, ddsk, dnw

    dy_ssd, dxs_skip, dproj, g["d_skip_full"], g["ssm_norm_w"] = rw(
        "ssm_gate_norm_bwd", gate_norm_bwd, tr=tr, ncb=SSM_GROUPS,
        rows=[(s["y_ssd"], gw_, 0), (s["xc"], gw_, 0), (proj, gw_, geo.col["z"][0] // gw_), (dy_ssm, gw_, 0)],
        vecs=[(d_skip_full, gw_, 0), (w["ssm_norm_w"], gw_, 0)],
        outs=[(SSM_D_INNER, gw_, MXU_DTYPE), (SSM_D_INNER, gw_, MXU_DTYPE),
              (geo.pw, gw_, MXU_DTYPE, geo.col["z"][0] // gw_, dproj)],
        reds=[(SSM_D_INNER, gw_), (SSM_D_INNER, gw_)])
    dxc, dproj, g["dt_bias"], g["a_log"] = _ssd_bwd_g(geo, s["xc"], s["proj_dt"], w["dt_bias"], w["a_log"], s["s_prev"],
                                                     dy_ssd, dxs_skip, dproj)
    dproj, g["conv_w"], g["conv_b"] = _conv_bwd(geo, proj, w["conv_w"], w["conv_b"], dxc, dproj)
    g["w_in_pt"] = _mm("mm_in_g", dproj, s["u"], ta=True, out_dtype=MXU_DTYPE)
    du = _mm("mm_in_t", dproj, w["w_in_pt"], dep=None if tail is None else tail(g))
    dh, g["norm_mix_w"] = rw("rms_mix_bwd", rms_bwd, tr=tr,
                             rows=[(s["h"], D_MODEL, 0), (du, D_MODEL, 0), (dh2, D_MODEL, 0)],
                             vecs=[(w["norm_mix_w"], D_MODEL, 0)], outs=[(D_MODEL, D_MODEL, F32)],
                             reds=[(D_MODEL, D_MODEL)])
    return dh, g


def _loss_bwd(geo, h, fw, target, tab):
    tr = geo.tr

    def fn(x, tgt, gw, tok):
        def lossf(x_, gw_):
            err = jnp.square(_rms(x_, gw_) - tgt)
            return 0.5 * jnp.sum(tok * jnp.mean(err, axis=-1, keepdims=True), axis=0, keepdims=True)

        val, vjp = jax.vjp(lossf, x, gw)
        dx, dgw = vjp(jnp.ones((1, 1), F32))
        return dx, jnp.broadcast_to(val, (1, LANE)), dgw

    return _rowwise("loss", fn, nrows=geo.nrows, tr=tr, rows=[(h, D_MODEL, 0), (target, D_MODEL, 0)],
                    vecs=[(fw, D_MODEL, 0)], tabs=[(tab["token"], 1, 0)], outs=[(D_MODEL, D_MODEL, F32)],
                    reds=[(LANE, LANE), (D_MODEL, D_MODEL)], tab_blocks=geo.lp // tr)


def kernel(x, meta_tokens, norm_mix_w, w_in, conv_w, conv_b, dt_bias, a_log, d_skip, ssm_norm_w, q_norm_w, kv_norm_w, w_uq, w_ukv, w_branch_ssm, w_branch_mla, w_out, norm_mlp_w, w_mlp_up, w_mlp_down, final_norm_w, loss_target, m_meta_tokens, m_norm_mix_w, m_w_in, m_conv_w, m_conv_b, m_dt_bias, m_a_log, m_d_skip, m_ssm_norm_w, m_q_norm_w, m_kv_norm_w, m_w_uq, m_w_ukv, m_w_branch_ssm, m_w_branch_mla, m_w_out, m_norm_mlp_w, m_w_mlp_up, m_w_mlp_down, m_final_norm_w, v_meta_tokens, v_norm_mix_w, v_w_in, v_conv_w, v_conv_b, v_dt_bias, v_a_log, v_d_skip, v_ssm_norm_w, v_q_norm_w, v_kv_norm_w, v_w_uq, v_w_ukv, v_w_branch_ssm, v_w_branch_mla, v_w_out, v_norm_mlp_w, v_w_mlp_up, v_w_mlp_down, v_final_norm_w):
    args = dict(locals())
    turn = lambda n, a: jnp.swapaxes(a, 1, 2) if n == "w_in" else a
    wts = {n: turn(n, args[n]) for n in WEIGHTS}
    mom = {n: turn(n, args["m_" + n]) for n in WEIGHTS}
    var = {n: turn(n, args["v_" + n]) for n in WEIGHTS}
    bsz, seq, _ = x.shape
    depth = w_in.shape[0]
    geo = _Geo(bsz, seq)
    tab = _tables(geo)

    big_names = [n for n, _ in BIG]
    sh_names = big_names + [n for n, _ in SHARDED_F32]
    kinds = dict(BIG + SHARDED_F32, w_in="row")
    shard3 = lambda a: a.reshape((1,) + a.shape) if a.ndim == 2 else a
    wire = {n: (MXU_DTYPE if n in big_names else F32) for n in sh_names}
    cast = {n: shard3(wts[n]).astype(wire[n]) for n in sh_names}
    per_layer = [n for n in sh_names if n != "meta_tokens"]
    small_names = ["norm_mix_w", "conv_b", "dt_bias", "a_log", "d_skip", "ssm_norm_w", "q_norm_w", "kv_norm_w",
                   "norm_mlp_w"]

    def gather_items(pairs):
        ins, outs, items, forms = [], [], [], []
        for n, i in pairs:
            a, b = cast[n].shape[1:]
            shape, dst, form = _gather_plan(a, b, kinds[n])
            items.append((len(ins), len(outs), (lambda ref, p, i=i: ref.at[i]), dst))
            ins.append(cast[n])
            outs.append(jax.ShapeDtypeStruct(shape, wire[n]))
            forms.append(form)
        return ins, outs, items, forms

    def whole_weights(pairs, forms, got):
        by_layer = {}
        for (n, i), form, g in zip(pairs, forms, got):
            if n == "w_in":
                n, g = "w_in_pt", _w_in_assemble(geo, g)
            elif form == "row":
                g = g.reshape(g.shape[0] * g.shape[1], g.shape[2])
            elif form == "stack":
                g = _unshard(g, "col")
            by_layer.setdefault(i, {})[n] = g
        return by_layer

    def prep(i, whole, token=None):
        wl = dict(whole)
        wl.update({n: wts[n][i] for n in small_names})
        if token is not None:
            wl["norm_mix_w"] = wl["norm_mix_w"] + token[0, 0]
        return _prep_layer(geo, wl)

    early = ("w_in", "conv_w")
    late_names = [n for n in per_layer if n not in early]
    pairs1 = [(n, i) for i in range(1, depth) for n in per_layer]
    groups = [[(n, 0) for n in early] + [("meta_tokens", 0)], [(n, 0) for n in late_names]] + ([pairs1] if pairs1 else [])
    started = {}

    def gather_start(gi, dep=None):
        ins, outs, items, forms = gather_items(groups[gi])
        sems, thru, landing, token = _exchange_start("gather_w%d_start" % gi, ins, outs, items, dep)
        started[gi] = (groups[gi], forms, sems, thru, landing, items)
        return token

    def gathered(gi, after):
        pairs, forms, sems, thru, landing, items = started[gi]
        return whole_weights(pairs, forms, _exchange_wait("gather_w%d_wait" % gi, sems, thru, landing, items, after))

    def late0(after):
        whole = gathered(1, after)[0]
        if pairs1:
            whole["q_norm_w"] = wts["q_norm_w"][0] + gather_start(2, whole["w_out"])[0, 0]
        return _prep_layer(geo, whole)

    token = gather_start(1, gather_start(0))
    whole0 = gathered(0, token)[0]
    meta_full = whole0.pop("meta_tokens")

    meta = jnp.broadcast_to(meta_full[None], (bsz, N_META, D_MODEL))
    h = jnp.concatenate([jnp.zeros((bsz, geo.pad, D_MODEL), F32), meta, x], axis=1).reshape(geo.nrows, D_MODEL)
    target = jnp.concatenate([jnp.zeros((bsz, geo.pad + N_META, D_MODEL), F32), loss_target], axis=1)
    target = target.reshape(geo.nrows, D_MODEL)
    layers, saved = [], []
    for i in range(depth):
        if i == 0:
            w, late = prep(0, whole0, token), late0
        else:
            if i == 1:
                whole1 = gathered(2, h)
            w, late = prep(i, whole1[i]), None
        h, s, w = _layer_fwd(geo, h, w, tab, late)
        layers.append(w)
        saved.append(s)
    dh, loss_part, g_final = _loss_bwd(geo, h, final_norm_w.reshape(1, -1), target, tab)

    def scatter_items(pairs):
        ins, outs, items = [], [], []
        for n, i in pairs:
            a, b = cast[n].shape[1:]
            arr = g_meta if n == "meta_tokens" else grads[i]["w_in_pt" if n == "w_in" else n]
            if n == "w_in":
                arr, src = _w_in_split(geo, arr, a), _entry
            elif kinds[n] == "row":
                src = lambda ref, p, a=a: ref.at[pl.ds(pl.multiple_of(p * a, a), a)]
            elif b % LANE == 0:
                src = lambda ref, p, b=b: ref.at[:, pl.ds(pl.multiple_of(p * b, b), b)]
            else:
                arr, src = _shard(arr, "col"), _entry
            items.append((len(ins), len(outs), src, _entry))
            ins.append(arr.astype(wire[n]))
            outs.append(jax.ShapeDtypeStruct((N_DEV, a, b), wire[n]))
        return ins, outs, items

    grads = [None] * depth
    landed, pending, res = {}, {}, {}

    def scatter_start(name, pairs):
        ins, outs, items = scatter_items(pairs)
        sems, thru, landing, token = _exchange_start(name + "_start", ins, outs, items)
        pending[name] = (pairs, sems, thru, landing, items)
        return token

    def scatter_wait(name, after):
        pairs, sems, thru, landing, items = pending[name]
        landed.update(zip(pairs, _exchange_wait(name + "_wait", sems, thru, landing, items, after)))

    def adam(n):
        parts = [landed[(n, i)] for i in range(cast[n].shape[0])]
        r = _adamw_nat("adamw_" + n, parts, shard3(wts[n]), shard3(mom[n]), shard3(var[n]))
        res[n] = [a.reshape(wts[n].shape) for a in r]

    def mid0(g):
        grads[0] = _unprep_grads(geo, g)
        return scatter_start("scatter_gb0", [(n, 0) for n in late_names])

    def tail0(g):
        grads[0] = _unprep_grads(geo, g)
        return scatter_start("scatter_ga0", [(n, 0) for n in early])

    dep = None
    for i in reversed(range(depth)):
        dh, gl = _layer_bwd(geo, dh, saved[i], layers[i], tab, *((mid0, tail0) if i == 0 else (None, None)), dep)
        grads[i] = _unprep_grads(geo, gl)
        if i == 1:
            dep = scatter_start("scatter_g1", pairs1)
    dh = dh.reshape(bsz, geo.lp, D_MODEL)
    grad_x = dh[:, geo.pad + N_META:]
    g_meta = jnp.sum(dh[:, geo.pad:geo.pad + N_META], axis=0)
    if pairs1:
        scatter_wait("scatter_g1", g_meta)
    scatter_wait("scatter_gb0", g_meta)
    for n in late_names:
        adam(n)
    g_small = {n: jnp.stack([grads[i][n] for i in range(depth)]) for n in SMALL if n != "final_norm_w"}
    g_small["final_norm_w"] = g_final.reshape(-1)
    zero = jnp.zeros((1,), F32)
    pk = lambda d, last: _pack([d[n] for n in SMALL] + [last], F32, row_mult=8)
    packed = pk(g_small, loss_part[0, :1])
    ins, outs, items = scatter_items([("meta_tokens", 0)])
    parts, landed[("meta_tokens", 0)] = _exchange(
        "gather_g", [packed] + ins + [res[n][1] for n in late_names],
        [jax.ShapeDtypeStruct((N_DEV,) + packed.shape, F32)] + outs,
        [(0, 0, _whole, _entry)] + [(1, 1, items[0][2], items[0][3])])
    adam("meta_tokens")
    scatter_wait("scatter_ga0", res["meta_tokens"][1])
    for n in early:
        adam(n)
    res_sm = _adamw("adamw_small", parts, pk(wts, zero), pk(mom, zero), pk(var, zero))
    res_sm = [_unpack(r, [wts[n].shape for n in SMALL] + [(1,)]) for r in res_sm]
    loss = res_sm[0][-1][0]

    out = [loss, grad_x]
    for k in range(4):
        named = {n: res[n][k] for n in sh_names}
        named.update(zip(SMALL, res_sm[k]))
        out += [turn(n, named[n]) for n in WEIGHTS]
    return tuple(out)
```

```python
import functools

import numpy as np
import jax
import jax.numpy as jnp
from jax import lax
from jax.experimental import pallas as pl
from jax.experimental.pallas import tpu as pltpu

F32 = jnp.float32
MXU_DTYPE = jnp.bfloat16

D_MODEL = 1024
N_META = 16
EPS = 1e-6
SSM_D_INNER = 2048
SSM_HEAD_DIM = 64
SSM_GROUPS = 4
SSM_STATE = 128
SSM_CONV = 4
SSM_CHUNK = 128
MLA_HEADS = 8
MLA_Q_LORA = 512
MLA_KV_LORA = 256
MLA_NOPE = 128
MLA_ROPE = 64
MLA_V = 128
ROPE_THETA = 10000.0
D_FF = 4096
ADAM_LR = 0.001
ADAM_B1 = 0.9
ADAM_B2 = 0.999
ADAM_EPS = 1e-08
ADAM_WD = 0.01
ADAM_STEP = 10

N_DEV = 8
ATT_BLK = 256
LANE = 128
PACK_W = 1024
VMEM_LIMIT = 56 * 1024 * 1024
MESH_ID = pl.DeviceIdType.MESH

BIG = (("w_in", "col"), ("w_uq", "col"), ("w_ukv", "col"), ("w_branch_ssm", "row"), ("w_branch_mla", "row"),
       ("w_out", "row"), ("w_mlp_up", "col"), ("w_mlp_down", "row"))
SHARDED_F32 = (("conv_w", "col"), ("meta_tokens", "col"))
SMALL = ("norm_mix_w", "conv_b", "dt_bias", "a_log", "d_skip", "ssm_norm_w", "q_norm_w", "kv_norm_w",
         "norm_mlp_w", "final_norm_w")
WEIGHTS = ("meta_tokens", "norm_mix_w", "w_in", "conv_w", "conv_b", "dt_bias", "a_log", "d_skip", "ssm_norm_w",
           "q_norm_w", "kv_norm_w", "w_uq", "w_ukv", "w_branch_ssm", "w_branch_mla", "w_out", "norm_mlp_w",
           "w_mlp_up", "w_mlp_down", "final_norm_w")


def _cparams(sem=None):
    return pltpu.CompilerParams(dimension_semantics=sem, vmem_limit_bytes=VMEM_LIMIT)


def _pick(n, cands):
    for c in cands:
        if n % c == 0:
            return c
    return n


def _sigmoid(x):
    return 1.0 / (1.0 + jnp.exp(-x))


def _silu(x):
    return x * _sigmoid(x)


def _softplus(x):
    t = jnp.exp(-jnp.abs(x))
    return jnp.maximum(x, 0.0) + jnp.where(t < 0.01, t * (1.0 - t * (0.5 - t * (1.0 / 3.0))), jnp.log(1.0 + t))


def _rms(x, w):
    x = x.astype(F32)
    return x * lax.rsqrt(jnp.mean(x * x, axis=-1, keepdims=True) + EPS) * w


def _dot(a, b, ca, cb, precision=None):
    return lax.dot_general(a, b, (((ca,), (cb,)), ((), ())), preferred_element_type=F32, precision=precision)


def _mxdot(a, b, ca, cb):
    return _dot(a.astype(MXU_DTYPE), b.astype(MXU_DTYPE), ca, cb)


def _mm(name, a, b, *, ta=False, tb=False, add=None, out_dtype=F32, dep=None, epi=None, side=None):
    (kdim, m) = a.shape if ta else a.shape[::-1]
    (n, k2) = b.shape if tb else b.shape[::-1]
    assert kdim == k2, (name, a.shape, b.shape)
    tm = _pick(m, (1152, 1088, 1024, 768, 544, 512, 384, 256, 128))
    tn = _pick(n, (1024, 512, 384, 256, 128))
    tk = _pick(kdim, (1152, 1088, 1024, 768, 544, 512, 384, 256, 128))
    nk = kdim // tk
    a_spec = pl.BlockSpec((tk, tm), lambda i, j, k: (k, i)) if ta else pl.BlockSpec((tm, tk), lambda i, j, k: (i, k))
    b_spec = pl.BlockSpec((tn, tk), lambda i, j, k: (j, k)) if tb else pl.BlockSpec((tk, tn), lambda i, j, k: (k, j))
    o_spec = pl.BlockSpec((tm, tn), lambda i, j, k: (i, j))
    ca, cb = (0 if ta else 1), (1 if tb else 0)

    out_dtypes = [out_dtype] if epi is None else list(epi[1])
    n_out = len(out_dtypes)
    n_side = 0 if side is None else 1

    def body(*refs):
        a_ref, b_ref = refs[:2]
        o_refs, acc = refs[-1 - n_side - n_out:-1 - n_side], refs[-1]
        k = pl.program_id(2)

        @pl.when(k == 0)
        def _():
            acc[...] = jnp.zeros_like(acc)

        acc[...] += _mxdot(a_ref[...], b_ref[...], ca, cb)

        @pl.when(k == nk - 1)
        def _():
            r = acc[...]
            if epi is not None:
                res = epi[0](r, refs[2][...]) if add is not None else epi[0](r)
            else:
                res = (r + refs[2][...].astype(F32) if add is not None else r,)
            for o_ref, val in zip(o_refs, res):
                o_ref[...] = val.astype(o_ref.dtype)

        if side is not None:
            @pl.when(jnp.logical_and(k == nk - 1, pl.program_id(1) == side[0] // tn))
            def _():
                refs[-2][...] = acc[:, side[0] % tn:side[0] % tn + side[1]]

    in_specs, args = [a_spec, b_spec], [a, b]
    if add is not None:
        in_specs.append(o_spec)
        args.append(add)
    if dep is not None:
        in_specs.append(pl.BlockSpec((8, LANE), lambda i, j, k: (0, 0)))
        args.append(dep)
    out_specs = [o_spec] * n_out
    out_shape = [jax.ShapeDtypeStruct((m, n), dt) for dt in out_dtypes]
    if side is not None:
        assert side[0] % tn + side[1] <= tn
        out_specs.append(pl.BlockSpec((tm, side[1]), lambda i, j, k: (i, 0)))
        out_shape.append(jax.ShapeDtypeStruct((m, side[1]), F32))
    res = pl.pallas_call(
        body, name=name, grid=(m // tm, n // tn, nk), in_specs=in_specs, out_specs=out_specs, out_shape=out_shape,
        scratch_shapes=[pltpu.VMEM((tm, tn), F32)],
        compiler_params=_cparams(("parallel", "arbitrary" if side is not None else "parallel", "arbitrary")))(*args)
    return res[0] if epi is None and side is None else res


def _rowwise(name, fn, *, nrows, tr, ncb=1, rows=(), fixed=(), vecs=(), tabs=(), outs=(), reds=(), tab_blocks=1):
    in_specs, args = [], []
    for arr, w, c0 in rows:
        in_specs.append(pl.BlockSpec((tr, w), lambda g, i, c0=c0: (i, c0 + g)))
        args.append(arr)
    for arr, w, c0 in fixed:
        in_specs.append(pl.BlockSpec((tr, w), lambda g, i, c0=c0: (i, c0)))
        args.append(arr)
    for arr, w, c0 in vecs:
        in_specs.append(pl.BlockSpec((1, w), lambda g, i, c0=c0: (0, c0 + g)))
        args.append(arr)
    for arr, w, c0 in tabs:
        in_specs.append(pl.BlockSpec((tr, w), lambda g, i, c0=c0: (i % tab_blocks, c0)))
        args.append(arr)
    n_in, n_out = len(args), len(outs)
    out_shape, out_specs, aliases = [], [], {}
    for k, o in enumerate(outs):
        c0 = o[3] if len(o) > 3 else 0
        out_shape.append(jax.ShapeDtypeStruct((nrows, o[0]), o[2]))
        out_specs.append(pl.BlockSpec((tr, o[1]), lambda g, i, c0=c0: (i, c0 + g)))
        if len(o) > 4:
            aliases[len(args)] = k
            in_specs.append(pl.BlockSpec(memory_space=pl.ANY))
            args.append(o[4])
    out_shape += [jax.ShapeDtypeStruct((1, wt), F32) for wt, w in reds]
    out_specs += [pl.BlockSpec((1, w), lambda g, i: (0, g)) for wt, w in reds]
    first_out = len(args)

    def body(*refs):
        res = fn(*[r[...] for r in refs[:n_in]])
        for o_ref, val in zip(refs[first_out:first_out + n_out], res[:n_out]):
            o_ref[...] = val.astype(o_ref.dtype)
        i = pl.program_id(1)
        for d_ref, val in zip(refs[first_out + n_out:], res[n_out:]):
            @pl.when(i == 0)
            def _(d_ref=d_ref, val=val):
                d_ref[...] = val

            @pl.when(i > 0)
            def _(d_ref=d_ref, val=val):
                d_ref[...] += val

    return pl.pallas_call(
        body, name=name, grid=(ncb, nrows // tr), in_specs=in_specs, out_specs=out_specs, out_shape=out_shape,
        input_output_aliases=aliases, compiler_params=_cparams(("parallel", "arbitrary")))(*args)


def _peer(k):
    x, y, c = lax.axis_index("x"), lax.axis_index("y"), lax.axis_index("c")
    px = jnp.where((k >> 2) & 1, 1 - x, x)
    py = jnp.where((k >> 1) & 1, 1 - y, y)
    pc = jnp.where(k & 1, 1 - c, c)
    return (px, py, pc), 4 * px + 2 * py + pc


def _my_index():
    return 4 * lax.axis_index("x") + 2 * lax.axis_index("y") + lax.axis_index("c")


def _exchange(name, ins, out_shapes, items):
    n_in, n_out, n_it = len(ins), len(out_shapes), len(items)

    def body(*refs):
        x, o = refs[:n_in], refs[n_in:n_in + n_out]
        send_sems, recv_sems, local_sems = refs[n_in + n_out:]
        me = _my_index()
        local, sends = [], []
        for t, (ii, io, src, dst) in enumerate(items):
            cp = pltpu.make_async_copy(src(x[ii], me), dst(o[io], me), local_sems.at[t])
            cp.start()
            local.append(cp)
        for k in range(1, N_DEV):
            dev, idx = _peer(k)
            for t, (ii, io, src, dst) in enumerate(items):
                s = (k - 1) * n_it + t
                cp = pltpu.make_async_remote_copy(
                    src_ref=src(x[ii], idx), dst_ref=dst(o[io], me), send_sem=send_sems.at[s],
                    recv_sem=recv_sems.at[s], device_id=dev, device_id_type=MESH_ID)
                cp.start()
                sends.append(cp)
        for k in range(1, N_DEV):
            dev, idx = _peer(k)
            for t, (ii, io, src, dst) in enumerate(items):
                s = (k - 1) * n_it + t
                pltpu.make_async_remote_copy(
                    src_ref=src(x[ii], idx), dst_ref=dst(o[io], idx), send_sem=send_sems.at[s],
                    recv_sem=recv_sems.at[s], device_id=dev, device_id_type=MESH_ID).wait_recv()
        for cp in sends:
            cp.wait_send()
        for cp in local:
            cp.wait()

    nsem = (N_DEV - 1) * n_it
    anyspec = pl.BlockSpec(memory_space=pl.ANY)
    return pl.pallas_call(
        body, name=name, out_shape=list(out_shapes), in_specs=[anyspec] * n_in, out_specs=[anyspec] * n_out,
        scratch_shapes=[pltpu.SemaphoreType.DMA((nsem,)), pltpu.SemaphoreType.DMA((nsem,)),
                        pltpu.SemaphoreType.DMA((n_it,))],
        compiler_params=pltpu.CompilerParams(has_side_effects=True))(*ins)


def _split_copies(x, land, send_sems, recv_sems, items, receive):
    me = _my_index()
    remote, n_it = [], len(items)
    for k in range(1, N_DEV):
        dev, idx = _peer(k)
        for t, (ii, io, src, dst) in enumerate(items):
            s = (k - 1) * n_it + t
            remote.append(pltpu.make_async_remote_copy(
                src_ref=src(x[ii], idx), dst_ref=dst(land[io], idx if receive else me), send_sem=send_sems.at[s],
                recv_sem=recv_sems.at[s], device_id=dev, device_id_type=MESH_ID))
    local = [pltpu.make_async_copy(src(x[ii], me), dst(land[io], me), send_sems.at[(N_DEV - 1) * n_it + t])
             for t, (ii, io, src, dst) in enumerate(items)]
    return remote, local


def _exchange_start(name, ins, out_shapes, items, dep=None):
    n_in, n_out, n_it = len(ins), len(out_shapes), len(items)

    def body(*refs):
        x, land = refs[:n_in], refs[n_in:n_in + n_out]
        first_out = n_in + n_out + (dep is not None)
        send_sems, recv_sems, token = refs[first_out], refs[first_out + 1], refs[-1]
        remote, local = _split_copies(x, land, send_sems, recv_sems, items, False)
        for cp in remote + local:
            cp.start()
        token[...] = jnp.zeros_like(token)

    hbm = pl.BlockSpec(memory_space=pltpu.HBM)
    sem = pl.BlockSpec(memory_space=pltpu.SEMAPHORE)
    arrs = [pltpu.with_memory_space_constraint(a, pltpu.HBM)
            for a in list(ins) + [lax.empty(s.shape, s.dtype) for s in out_shapes]]
    res = pl.pallas_call(
        body, name=name,
        out_shape=(pltpu.SemaphoreType.DMA((N_DEV * n_it,)), pltpu.SemaphoreType.DMA(((N_DEV - 1) * n_it,)),
                   *[pltpu.HBM(a.shape, a.dtype) for a in arrs], jax.ShapeDtypeStruct((8, LANE), F32)),
        in_specs=[hbm] * (n_in + n_out) + ([] if dep is None else [pl.BlockSpec(memory_space=pl.ANY)]),
        out_specs=(sem, sem, *[hbm] * (n_in + n_out), pl.BlockSpec(memory_space=pltpu.VMEM)),
        input_output_aliases={i: 2 + i for i in range(n_in + n_out)},
        compiler_params=pltpu.CompilerParams(has_side_effects=pltpu.SideEffectType.DATAFLOW_SIDE_EFFECTING))(
            *arrs, *([] if dep is None else [dep]))
    return res[:2], res[2:2 + n_in], res[2 + n_in:2 + n_in + n_out], res[-1]


def _exchange_wait(name, sems, ins, landing, items, after):
    n_in, n_out = len(ins), len(landing)

    def body(*refs):
        x, land = refs[:n_in], refs[n_in:n_in + n_out]
        send_sems, recv_sems = refs[n_in + n_out], refs[n_in + n_out + 1]
        remote, local = _split_copies(x, land, send_sems, recv_sems, items, True)
        for cp in remote:
            cp.wait_send()
            cp.wait_recv()
        for cp in local:
            cp.wait()

    hbm = pl.BlockSpec(memory_space=pltpu.HBM)
    sem = pl.BlockSpec(memory_space=pltpu.SEMAPHORE)
    arrs = list(ins) + list(landing)
    res = pl.pallas_call(
        body, name=name, out_shape=tuple(pltpu.HBM(a.shape, a.dtype) for a in arrs),
        in_specs=[hbm] * (n_in + n_out) + [sem, sem, pl.BlockSpec(memory_space=pl.ANY)],
        out_specs=tuple([hbm] * (n_in + n_out)), input_output_aliases={i: i for i in range(n_in + n_out)},
        compiler_params=pltpu.CompilerParams(has_side_effects=pltpu.SideEffectType.DATAFLOW_SIDE_EFFECTING))(
            *arrs, *sems, after)
    return res[n_in:]


def _whole(ref, p):
    return ref


def _entry(ref, p):
    return ref.at[p]


def _gather_plan(a, b, kind):
    if kind == "col" and b % LANE == 0:
        return (a, N_DEV * b), (lambda ref, p: ref.at[:, pl.ds(pl.multiple_of(p * b, b), b)]), "col"
    return (N_DEV, a, b), _entry, ("row" if kind == "row" else "stack")


def _adamw_nat(name, parts, w, m, v):
    depth, b, c = w.shape
    assert len(parts) == depth
    tb = _pick(b, (128, 64, 32, 16, 8))
    if tb == b and b > 256:
        tb = 256
    spec = pl.BlockSpec((1, tb, c), lambda i, j: (i, j, 0))

    def body(*refs):
        p_refs = refs[:depth]
        w_ref, m_ref, v_ref, g_ref, d_ref, nm_ref, nv_ref = refs[depth:]
        for layer, p_ref in enumerate(p_refs):
            @pl.when(pl.program_id(0) == layer)
            def _(p_ref=p_ref):
                g = p_ref[0].astype(F32)
                for j in range(1, N_DEV):
                    g = g + p_ref[j].astype(F32)
                nm = ADAM_B1 * m_ref[0] + (1.0 - ADAM_B1) * g
                nv = ADAM_B2 * v_ref[0] + (1.0 - ADAM_B2) * jnp.square(g)
                m_hat = nm / (1.0 - ADAM_B1 ** ADAM_STEP)
                v_hat = nv / (1.0 - ADAM_B2 ** ADAM_STEP)
                g_ref[0] = g
                d_ref[0] = -ADAM_LR * (m_hat / (jnp.sqrt(v_hat) + ADAM_EPS) + ADAM_WD * w_ref[0])
                nm_ref[0] = nm
                nv_ref[0] = nv

    sds = jax.ShapeDtypeStruct((depth, b, c), F32)
    return pl.pallas_call(
        body, name=name, grid=(depth, pl.cdiv(b, tb)),
        in_specs=[pl.BlockSpec((N_DEV, tb, c), lambda i, j: (0, j, 0))] * depth + [spec, spec, spec],
        out_specs=[spec] * 4, out_shape=[sds] * 4, compiler_params=_cparams(("parallel", "parallel")))(*parts, w, m, v)


def _adamw(name, parts, w, m, v):
    rows = w.shape[0]
    tr = _pick(rows, (256, 128, 64, 32, 16, 8))
    spec = pl.BlockSpec((tr, PACK_W), lambda i: (i, 0))

    def body(p_ref, w_ref, m_ref, v_ref, g_ref, d_ref, nm_ref, nv_ref):
        g = p_ref[0]
        for j in range(1, N_DEV):
            g = g + p_ref[j]
        nm = ADAM_B1 * m_ref[...] + (1.0 - ADAM_B1) * g
        nv = ADAM_B2 * v_ref[...] + (1.0 - ADAM_B2) * jnp.square(g)
        m_hat = nm / (1.0 - ADAM_B1 ** ADAM_STEP)
        v_hat = nv / (1.0 - ADAM_B2 ** ADAM_STEP)
        g_ref[...] = g
        d_ref[...] = -ADAM_LR * (m_hat / (jnp.sqrt(v_hat) + ADAM_EPS) + ADAM_WD * w_ref[...])
        nm_ref[...] = nm
        nv_ref[...] = nv

    sds = jax.ShapeDtypeStruct((rows, PACK_W), F32)
    return pl.pallas_call(
        body, name=name, grid=(rows // tr,),
        in_specs=[pl.BlockSpec((N_DEV, tr, PACK_W), lambda i: (0, i, 0)), spec, spec, spec],
        out_specs=[spec] * 4, out_shape=[sds] * 4, compiler_params=_cparams(("parallel",)))(parts, w, m, v)


def _pack(arrs, dtype, row_mult=16):
    flat = jnp.concatenate([a.reshape(-1).astype(dtype) for a in arrs])
    unit = row_mult * PACK_W
    total = -(-flat.shape[0] // unit) * unit
    flat = jnp.pad(flat, (0, total - flat.shape[0]))
    return flat.reshape(-1, PACK_W)


def _pack_lead(arrs, dtype, row_mult):
    flat = jnp.concatenate([a.reshape(N_DEV, -1).astype(dtype) for a in arrs], axis=1)
    unit = row_mult * PACK_W
    total = -(-flat.shape[1] // unit) * unit
    flat = jnp.pad(flat, ((0, 0), (0, total - flat.shape[1])))
    return flat.reshape(N_DEV, -1, PACK_W)


def _unpack(buf, shapes, lead=()):
    flat = buf.reshape(lead + (-1,))
    out, off = [], 0
    for s in shapes:
        n = int(np.prod(s))
        out.append(flat[..., off:off + n].reshape(lead + tuple(s)))
        off += n
    return out


def _unshard(g, kind):
    if kind == "col":
        g = jnp.moveaxis(g, 0, -2)
        return g.reshape(g.shape[:-2] + (g.shape[-2] * g.shape[-1],))
    g = jnp.moveaxis(g, 0, 1)
    return g.reshape((g.shape[0], g.shape[1] * g.shape[2]) + g.shape[3:])


def _shard(full, kind):
    if kind == "col":
        s = full.reshape(full.shape[:-1] + (N_DEV, full.shape[-1] // N_DEV))
        return jnp.moveaxis(s, -2, 0)
    s = full.reshape((full.shape[0], N_DEV, full.shape[1] // N_DEV) + full.shape[2:])
    return jnp.moveaxis(s, 1, 0)


class _Geo:
    def __init__(self, bsz, seq):
        self.bsz, self.seq = bsz, seq
        self.pad = (-N_META) % SSM_CHUNK
        self.lp = self.pad + N_META + seq
        self.t0 = self.pad + N_META
        self.nq = 1 + seq // ATT_BLK
        assert self.t0 == LANE and seq % ATT_BLK == 0 and self.lp % SSM_CHUNK == 0
        self.nrows = bsz * self.lp
        self.nc = self.lp // SSM_CHUNK
        self.nh = SSM_D_INNER // SSM_HEAD_DIM
        self.gn = SSM_GROUPS * SSM_STATE
        self.cd = SSM_D_INNER + 2 * self.gn
        self.hq = MLA_HEADS * LANE
        order = (("z", SSM_D_INNER), ("g_ssm", D_MODEL), ("g_mla", D_MODEL), ("xs", SSM_D_INNER), ("bm", self.gn),
                 ("cm", self.gn), ("c_q", MLA_Q_LORA), ("c_kv", MLA_KV_LORA), ("dt", LANE), ("k_rope", LANE))
        self.col, off = {}, 0
        for nm, w in order:
            assert off % w == 0, (nm, off, w)
            self.col[nm] = (off, w)
            off += w
        self.pw = off
        assert self.nh <= LANE and MLA_ROPE == 64 and MLA_NOPE == LANE and MLA_V == LANE
        self.tr = _pick(self.lp, (1088, 768, 544, 512, 384, 272, 256, 128))
        self.tr_wide = _pick(self.lp, (544, 384, 272, 256, 128))

    def cb(self, nm):
        off, w = self.col[nm]
        return off // w

    def w_in_runs(self, shard_w):
        nh, half = self.nh, MLA_ROPE // 2
        src, pieces = 0, []
        for nm, n in (("z", SSM_D_INNER), ("xs", SSM_D_INNER), ("bm", self.gn), ("cm", self.gn), ("dt", nh),
                      ("c_q", MLA_Q_LORA), ("c_kv", MLA_KV_LORA), ("k_rope", MLA_ROPE), ("g_ssm", D_MODEL),
                      ("g_mla", D_MODEL)):
            dst = self.col[nm][0]
            if nm == "k_rope":
                pieces += [(src, half, dst), (src + half, half, dst + 2 * half)]
            else:
                pieces.append((src, n, dst))
            src += n
        assert src == shard_w * N_DEV
        runs = []
        for a, n, dst in pieces:
            for j in range(N_DEV):
                lo, hi = max(a, j * shard_w), min(a + n, (j + 1) * shard_w)
                if lo < hi:
                    runs.append((j, lo - j * shard_w, hi - lo, dst + lo - a))
        return runs


def _slot(a):
    h = MLA_ROPE // 2
    z = jnp.zeros(a.shape[:-1] + (h,), a.dtype)
    return jnp.concatenate([a[..., :h], z, a[..., h:], z], axis=-1)


def _unslot(a):
    h = MLA_ROPE // 2
    return jnp.concatenate([a[..., :h], a[..., 2 * h:3 * h]], axis=-1)


def _prep_layer(geo, wl):
    nh = geo.nh
    p = {}
    if "w_uq" in wl:
        uq = wl["w_uq"].reshape(MLA_Q_LORA, MLA_HEADS, MLA_NOPE + MLA_ROPE)
        p["w_qn"] = uq[..., :MLA_NOPE].reshape(MLA_Q_LORA, geo.hq)
        p["w_qp"] = _slot(uq[..., MLA_NOPE:]).reshape(MLA_Q_LORA, geo.hq)
    if "w_ukv" in wl:
        ukv = wl["w_ukv"].reshape(MLA_KV_LORA, MLA_HEADS, MLA_NOPE + MLA_V)
        p["w_k"] = ukv[..., :MLA_NOPE].reshape(MLA_KV_LORA, geo.hq)
        p["w_v"] = ukv[..., MLA_NOPE:].reshape(MLA_KV_LORA, geo.hq)
    for nm in ("w_in_pt", "conv_w", "w_branch_ssm", "w_branch_mla", "w_out", "w_mlp_up", "w_mlp_down"):
        if nm in wl:
            p[nm] = wl[nm]
    for nm in ("norm_mix_w", "conv_b", "ssm_norm_w", "q_norm_w", "kv_norm_w", "norm_mlp_w"):
        if nm in wl:
            p[nm] = wl[nm].reshape(1, -1)
    if "dt_bias" in wl:
        p["dt_bias"] = jnp.pad(wl["dt_bias"], (0, LANE - nh)).reshape(1, LANE)
        p["a_log"] = jnp.pad(wl["a_log"], (0, LANE - nh)).reshape(1, LANE)
        p["d_skip_full"] = jnp.repeat(wl["d_skip"], SSM_HEAD_DIM).reshape(1, SSM_D_INNER)
    return p


def _unprep_grads(geo, g):
    nh = geo.nh
    out = {}
    if "w_qn" in g:
        qn = g["w_qn"].reshape(MLA_Q_LORA, MLA_HEADS, MLA_NOPE)
        qp = _unslot(g["w_qp"].reshape(MLA_Q_LORA, MLA_HEADS, LANE))
        out["w_uq"] = jnp.concatenate([qn, qp], axis=-1).reshape(MLA_Q_LORA, -1)
    if "w_k" in g:
        wk = g["w_k"].reshape(MLA_KV_LORA, MLA_HEADS, MLA_NOPE)
        wv = g["w_v"].reshape(MLA_KV_LORA, MLA_HEADS, MLA_V)
        out["w_ukv"] = jnp.concatenate([wk, wv], axis=-1).reshape(MLA_KV_LORA, -1)
    for nm in ("w_in_pt", "w_branch_ssm", "w_branch_mla", "w_out", "w_mlp_up", "w_mlp_down", "conv_w"):
        if nm in g:
            out[nm] = g[nm]
    for nm in ("norm_mix_w", "conv_b", "ssm_norm_w", "q_norm_w", "kv_norm_w", "norm_mlp_w"):
        if nm in g:
            out[nm] = g[nm].reshape(-1)
    if "dt_bias" in g:
        out["dt_bias"] = g["dt_bias"].reshape(-1)[:nh]
        out["a_log"] = g["a_log"].reshape(-1)[:nh]
        out["d_skip"] = g["d_skip_full"].reshape(nh, SSM_HEAD_DIM).sum(-1)
    return out


def _tables(geo):
    pos = jnp.arange(geo.lp, dtype=F32) - geo.pad
    inv = ROPE_THETA ** (-jnp.arange(0, MLA_ROPE, 2, dtype=F32) / MLA_ROPE)
    ang = pos[:, None] * inv[None, :]
    cos, sin = jnp.cos(ang), jnp.sin(ang)
    z = jnp.zeros_like(cos)
    rows = jnp.arange(geo.lp)[:, None]
    return {"cos": jnp.concatenate([cos, z, cos, z], axis=-1), "sin": jnp.concatenate([-sin, z, sin, z], axis=-1),
            "valid": (rows >= geo.pad).astype(F32), "token": (rows >= geo.pad + N_META).astype(F32)}


def _w_in_assemble(geo, gathered):
    _, sw, d = gathered.shape
    runs = geo.w_in_runs(sw)
    tl = _pick(d, (256, 128))

    def body(x_ref, o_ref):
        o_ref[...] = jnp.zeros_like(o_ref)
        for j, s0, n, d0 in runs:
            o_ref[d0:d0 + n, :] = x_ref[j, s0:s0 + n, :]

    return pl.pallas_call(
        body, name="w_in_assemble", grid=(d // tl,), in_specs=[pl.BlockSpec((N_DEV, sw, tl), lambda i: (0, 0, i))],
        out_specs=pl.BlockSpec((geo.pw, tl), lambda i: (0, i)),
        out_shape=jax.ShapeDtypeStruct((geo.pw, d), gathered.dtype), compiler_params=_cparams(("parallel",)))(gathered)


def _w_in_split(geo, g_padded, sw):
    d = g_padded.shape[1]
    runs = geo.w_in_runs(sw)
    tl = _pick(d, (256, 128))

    def body(x_ref, o_ref):
        for j, s0, n, d0 in runs:
            o_ref[j, s0:s0 + n, :] = x_ref[d0:d0 + n, :]

    return pl.pallas_call(
        body, name="w_in_split", grid=(d // tl,), in_specs=[pl.BlockSpec((geo.pw, tl), lambda i: (0, i))],
        out_specs=pl.BlockSpec((N_DEV, sw, tl), lambda i: (0, 0, i)),
        out_shape=jax.ShapeDtypeStruct((N_DEV, sw, d), g_padded.dtype),
        compiler_params=_cparams(("parallel",)))(g_padded)


def _conv_cols(geo, cbw):
    x0 = geo.col["xs"][0]
    assert geo.col["bm"][0] == x0 + SSM_D_INNER and geo.col["cm"][0] == geo.col["bm"][0] + geo.gn and x0 % cbw == 0
    return lambda j: x0 // cbw + j


def _conv_taps(x):
    return [pltpu.roll(x, SSM_CONV - 1 - k, axis=0) for k in range(SSM_CONV - 1)] + [x]


def _conv_pre(x, w_ref, b_ref, taps=None):
    taps = _conv_taps(x) if taps is None else taps
    acc = b_ref[...]
    for k in range(SSM_CONV):
        acc = acc + taps[k] * w_ref[k:k + 1, :]
    return acc


def _conv_fwd(geo, proj, conv_w, conv_b):
    cbw = 256
    colmap = _conv_cols(geo, cbw)
    lp, pad = geo.lp, geo.pad

    def body(x_ref, w_ref, b_ref, o_ref):
        valid = (lax.broadcasted_iota(jnp.int32, (lp, 1), 0) >= pad).astype(F32)
        o_ref[...] = (_silu(_conv_pre(x_ref[...].astype(F32), w_ref, b_ref)) * valid).astype(o_ref.dtype)

    return pl.pallas_call(
        body, name="conv_fwd", grid=(geo.bsz, geo.cd // cbw),
        in_specs=[pl.BlockSpec((lp, cbw), lambda b, j: (b, colmap(j))),
                  pl.BlockSpec((SSM_CONV, cbw), lambda b, j: (0, j)), pl.BlockSpec((1, cbw), lambda b, j: (0, j))],
        out_specs=pl.BlockSpec((lp, cbw), lambda b, j: (b, j)),
        out_shape=jax.ShapeDtypeStruct((geo.nrows, geo.cd), MXU_DTYPE),
        compiler_params=_cparams(("parallel", "parallel")))(proj, conv_w, conv_b)


def _conv_bwd(geo, proj, conv_w, conv_b, dxc, dproj):
    cbw = 256
    colmap = _conv_cols(geo, cbw)
    lp, pad = geo.lp, geo.pad

    def body(x_ref, w_ref, b_ref, dy_ref, _, dx_ref, gw_ref, gb_ref):
        b = pl.program_id(1)
        valid = (lax.broadcasted_iota(jnp.int32, (lp, 1), 0) >= pad).astype(F32)
        taps = _conv_taps(x_ref[...].astype(F32))
        pre = _conv_pre(None, w_ref, b_ref, taps)
        sig = _sigmoid(pre)
        dpre = dy_ref[...] * (sig * (1.0 + pre * (1.0 - sig))) * valid
        dx = dpre * w_ref[SSM_CONV - 1:SSM_CONV, :]
        for k in range(SSM_CONV - 1):
            dx = dx + pltpu.roll(dpre, lp - (SSM_CONV - 1 - k), axis=0) * w_ref[k:k + 1, :]
        gws = [jnp.sum(dpre * taps[k], axis=0, keepdims=True) for k in range(SSM_CONV)]
        dx_ref[...] = (dx * valid).astype(dx_ref.dtype)

        @pl.when(b == 0)
        def _():
            gw_ref[...] = jnp.zeros_like(gw_ref)
            gb_ref[...] = jnp.zeros_like(gb_ref)

        for k in range(SSM_CONV):
            gw_ref[k:k + 1, :] += gws[k]
        gb_ref[...] += jnp.sum(dpre, axis=0, keepdims=True)

    return pl.pallas_call(
        body, name="conv_bwd", grid=(geo.cd // cbw, geo.bsz),
        in_specs=[pl.BlockSpec((lp, cbw), lambda j, b: (b, colmap(j))),
                  pl.BlockSpec((SSM_CONV, cbw), lambda j, b: (0, j)), pl.BlockSpec((1, cbw), lambda j, b: (0, j)),
                  pl.BlockSpec((lp, cbw), lambda j, b: (b, j)), pl.BlockSpec(memory_space=pl.ANY)],
        out_specs=[pl.BlockSpec((lp, cbw), lambda j, b: (b, colmap(j))),
                   pl.BlockSpec((SSM_CONV, cbw), lambda j, b: (0, j)), pl.BlockSpec((1, cbw), lambda j, b: (0, j))],
        out_shape=[jax.ShapeDtypeStruct(dproj.shape, dproj.dtype),
                   jax.ShapeDtypeStruct((SSM_CONV, geo.cd), F32), jax.ShapeDtypeStruct((1, geo.cd), F32)],
        input_output_aliases={4: 0},
        compiler_params=_cparams(("parallel", "arbitrary")))(proj, conv_w, conv_b, dxc, dproj)


def _tri(q):
    r = lax.broadcasted_iota(jnp.int32, (q, q), 0)
    c = lax.broadcasted_iota(jnp.int32, (q, q), 1)
    return r >= c


def _ssd_pre(dtr, dtb, alog, valid):
    dt = _softplus(dtr + dtb) * valid
    adt = dt * (-jnp.exp(alog))
    a_cs = _dot(_tri(SSM_CHUNK).astype(F32), adt, 1, 0, precision=lax.Precision.HIGHEST)
    return dt, a_cs


def _ssd_specs(geo, rev):
    nc, q = geo.nc, SSM_CHUNK
    ci = (lambda c: nc - 1 - c) if rev else (lambda c: c)
    nxb = SSM_D_INNER // geo.gn
    return [pl.BlockSpec((q, SSM_D_INNER), lambda b, c: (b * nc + ci(c), 0)),
            pl.BlockSpec((q, geo.gn), lambda b, c: (b * nc + ci(c), nxb)),
            pl.BlockSpec((q, geo.gn), lambda b, c: (b * nc + ci(c), nxb + 1)),
            pl.BlockSpec((q, LANE), lambda b, c: (b * nc + ci(c), 0)),
            pl.BlockSpec((1, LANE), lambda b, c: (0, 0)), pl.BlockSpec((1, LANE), lambda b, c: (0, 0))], ci


def _expand_heads(cols, nh):
    per = LANE // SSM_HEAD_DIM
    lane = lax.broadcasted_iota(jnp.int32, (1, LANE), 1)
    blocks = []
    for j in range(nh // per):
        blk = jnp.broadcast_to(cols[:, j * per:j * per + 1], (cols.shape[0], LANE))
        for k in range(1, per):
            blk = jnp.where(lane >= k * SSM_HEAD_DIM, cols[:, j * per + k:j * per + k + 1], blk)
        blocks.append(blk)
    return jnp.concatenate(blocks, axis=1)


def _head_maps(geo):
    e = (jnp.arange(SSM_D_INNER)[None, :] // SSM_HEAD_DIM == jnp.arange(LANE)[:, None]).astype(F32)
    return e, e.T


def _ssd_fwd_g(geo, xc, proj, dt_bias, a_log):
    q, p, n, e = SSM_CHUNK, SSM_HEAD_DIM, SSM_STATE, geo.nh // SSM_GROUPS
    nc, pad, gw = geo.nc, geo.pad, SSM_D_INNER // SSM_GROUPS
    in_specs, _ = _ssd_specs(geo, False)

    def body(xs_ref, b_ref, c_ref, dtr_ref, dtb_ref, alog_ref, y_ref, sp_ref, state, xdt_s, y_s):
        c = pl.program_id(1)

        @pl.when(c == 0)
        def _():
            state[...] = jnp.zeros_like(state)

        sp_ref[...] = state[...]
        inert = (c + 1) * q <= pad

        @pl.when(inert)
        def _():
            y_ref[...] = jnp.zeros_like(y_ref)

        @pl.when(jnp.logical_not(inert))
        def _():
            valid = (c * q + lax.broadcasted_iota(jnp.int32, (q, 1), 0) >= pad).astype(F32)
            dt, a_cs = _ssd_pre(dtr_ref[...], dtb_ref[...], alog_ref[...], valid)
            a_cst = a_cs.T
            dt_x, a_x = _expand_heads(dt, geo.nh), _expand_heads(a_cs, geo.nh)
            tri = _tri(q)
            for g in range(SSM_GROUPS):
                gs = slice(g * gw, (g + 1) * gw)
                bg, cg = b_ref[:, g * n:(g + 1) * n], c_ref[:, g * n:(g + 1) * n]
                a_g = a_x[:, gs]
                a_last = a_g[q - 1:q, :]
                xdt_g = xs_ref[:, gs] * dt_x[:, gs]
                xdt_s[:, gs] = xdt_g
                s_g = state[:, gs]
                y_s[:, gs] = _mxdot(cg, s_g, 1, 0) * jnp.exp(a_g)
                state[:, gs] = s_g * jnp.exp(a_last) + _mxdot(bg, xdt_g * jnp.exp(a_last - a_g), 0, 0)
                cb = _mxdot(cg, bg, 1, 1)
                for hh in range(e):
                    h = g * e + hh
                    hs = slice(h * p, (h + 1) * p)
                    ldec = jnp.exp(jnp.where(tri, a_cs[:, h:h + 1] - a_cst[h:h + 1, :], -jnp.inf))
                    y_s[:, hs] += _mxdot(cb * ldec, xdt_s[:, hs], 1, 0)
            y_ref[...] = y_s[...].astype(y_ref.dtype)

    return pl.pallas_call(
        body, name="ssd_fwd", grid=(geo.bsz, nc), in_specs=in_specs,
        out_specs=[pl.BlockSpec((q, SSM_D_INNER), lambda b, c: (b * nc + c, 0)),
                   pl.BlockSpec((n, SSM_D_INNER), lambda b, c: (b * nc + c, 0))],
        out_shape=[jax.ShapeDtypeStruct((geo.nrows, SSM_D_INNER), MXU_DTYPE),
                   jax.ShapeDtypeStruct((geo.bsz * nc * n, SSM_D_INNER), F32)],
        scratch_shapes=[pltpu.VMEM((n, SSM_D_INNER), F32), pltpu.VMEM((q, SSM_D_INNER), F32),
                        pltpu.VMEM((q, SSM_D_INNER), F32)],
        compiler_params=_cparams(("parallel", "arbitrary")))(xc, xc, xc, proj, dt_bias, a_log)


def _ssd_bwd_g(geo, xc, proj, dt_bias, a_log, s_prev_all, dy, dxs_skip, dproj):
    q, p, n, e = SSM_CHUNK, SSM_HEAD_DIM, SSM_STATE, geo.nh // SSM_GROUPS
    nc, pad, di, gn, gw = geo.nc, geo.pad, SSM_D_INNER, geo.gn, SSM_D_INNER // SSM_GROUPS
    in_specs, ci = _ssd_specs(geo, True)
    row_spec = pl.BlockSpec((q, di), lambda b, c: (b * nc + ci(c), 0))
    e_map, _ = _head_maps(geo)
    in_specs += [pl.BlockSpec((n, di), lambda b, c: (b * nc + ci(c), 0)), row_spec, row_spec,
                 pl.BlockSpec((LANE, di), lambda b, c: (0, 0)), pl.BlockSpec(memory_space=pl.ANY)]

    def body(xs_ref, b_ref, c_ref, dtr_ref, dtb_ref, alog_ref, sp_ref, dy_ref, dsk_ref, e_ref, _,
             dxc_ref, ddt_ref, gdtb_ref, galog_ref, dstate, xdt_s, dxdt_s):
        step = pl.program_id(1)
        first = jnp.logical_and(pl.program_id(0) == 0, step == 0)
        c = nc - 1 - step

        @pl.when(step == 0)
        def _():
            dstate[...] = jnp.zeros_like(dstate)

        @pl.when(first)
        def _():
            gdtb_ref[...] = jnp.zeros_like(gdtb_ref)
            galog_ref[...] = jnp.zeros_like(galog_ref)

        inert = (c + 1) * q <= pad

        @pl.when(inert)
        def _():
            dxc_ref[...] = jnp.zeros_like(dxc_ref)
            ddt_ref[...] = jnp.zeros_like(ddt_ref)

        @pl.when(jnp.logical_not(inert))
        def _():
            valid = (c * q + lax.broadcasted_iota(jnp.int32, (q, 1), 0) >= pad).astype(F32)
            dtr, dtb, alog = dtr_ref[...], dtb_ref[...], alog_ref[...]
            dt, a_cs = _ssd_pre(dtr, dtb, alog, valid)
            a_cst = a_cs.T
            dt_x, a_x = _expand_heads(dt, geo.nh), _expand_heads(a_cs, geo.nh)
            tri = _tri(q)
            lane = lax.broadcasted_iota(jnp.int32, (1, LANE), 1)
            sub = lax.broadcasted_iota(jnp.int32, (LANE, 1), 0)
            d_dt = jnp.zeros((q, LANE), F32)
            d_acs = jnp.zeros((q, LANE), F32)
            d_acst = jnp.zeros((LANE, q), F32)
            d_last = jnp.zeros((1, LANE), F32)
            for g in range(SSM_GROUPS):
                gs = slice(g * gw, (g + 1) * gw)
                bg, cg = b_ref[:, g * n:(g + 1) * n], c_ref[:, g * n:(g + 1) * n]
                seg = lambda v: _mxdot(v, e_ref[:, gs], 1, 1)
                a_g, dt_g, x_g, dy_g = a_x[:, gs], dt_x[:, gs], xs_ref[:, gs], dy_ref[:, gs]
                e_col, e_last, dec = jnp.exp(a_g), jnp.exp(a_g[q - 1:q, :]), jnp.exp(a_g[q - 1:q, :] - a_g)
                xdt_g = x_g * dt_g
                xdt_s[:, gs] = xdt_g
                s_g, ds_g = sp_ref[:, gs], dstate[:, gs]
                cs = _mxdot(cg, s_g, 1, 0)
                d_cs = dy_g * e_col
                d_acs = d_acs + seg(d_cs * cs)
                d_cg = _mxdot(d_cs, s_g, 1, 1)
                dstate[:, gs] = _mxdot(cg, d_cs, 0, 0) + ds_g * e_last
                dl_x = jnp.sum(ds_g * s_g, axis=0, keepdims=True) * e_last
                d_last = d_last + seg(jnp.broadcast_to(dl_x, (8, gw)))[:1]
                gmat = _mxdot(bg, ds_g, 1, 0)
                xd = xdt_g * dec
                d_bg = _mxdot(xd, ds_g, 1, 1)
                d_dec = seg(xd * gmat)
                d_acs = d_acs - d_dec
                d_last = d_last + jnp.sum(d_dec, axis=0, keepdims=True)
                dxdt_s[:, gs] = dec * gmat
                cb = _mxdot(cg, bg, 1, 1)
                d_cb = jnp.zeros((q, q), F32)
                for hh in range(e):
                    h = g * e + hh
                    hs = slice(h * p, (h + 1) * p)
                    ldec = jnp.exp(jnp.where(tri, a_cs[:, h:h + 1] - a_cst[h:h + 1, :], -jnp.inf))
                    dyh = dy_ref[:, hs]
                    d_m = _mxdot(dyh, xdt_s[:, hs], 1, 1)
                    dxdt_s[:, hs] += _mxdot(cb * ldec, dyh, 0, 0)
                    d_cb = d_cb + d_m * ldec
                    d_diff = d_m * cb * ldec
                    d_acs = d_acs + jnp.sum(d_diff, axis=1, keepdims=True) * (lane == h).astype(F32)
                    d_acst = d_acst - (sub == h).astype(F32) * jnp.sum(d_diff, axis=0, keepdims=True)
                d_xdt = dxdt_s[:, gs]
                dxc_ref[:, gs] = d_xdt * dt_g + dsk_ref[:, gs]
                d_dt = d_dt + seg(d_xdt * x_g)
                dxc_ref[:, di + g * n:di + (g + 1) * n] = d_bg + _mxdot(d_cb, cg, 0, 0)
                dxc_ref[:, di + gn + g * n:di + gn + (g + 1) * n] = d_cg + _mxdot(d_cb, bg, 1, 0)
            is_last = (lax.broadcasted_iota(jnp.int32, (q, 1), 0) == q - 1).astype(F32)
            d_acs = d_acs + d_acst.T + is_last * d_last
            d_adt = _dot(_tri(q).astype(F32), d_acs, 0, 0, precision=lax.Precision.HIGHEST)
            a = -jnp.exp(alog)
            d_dt = d_dt + d_adt * a
            d_dtr = d_dt * valid * _sigmoid(dtr + dtb)
            ddt_ref[...] = d_dtr.astype(ddt_ref.dtype)
            gdtb_ref[...] += jnp.sum(d_dtr, axis=0, keepdims=True)
            galog_ref[...] += jnp.sum(d_adt * dt, axis=0, keepdims=True) * a

    vec = pl.BlockSpec((1, LANE), lambda b, c: (0, 0))
    return pl.pallas_call(
        body, name="ssd_bwd", grid=(geo.bsz, nc), in_specs=in_specs,
        out_specs=[pl.BlockSpec((q, geo.cd), lambda b, c: (b * nc + ci(c), 0)),
                   pl.BlockSpec((q, LANE), lambda b, c: (b * nc + ci(c), geo.cb("dt"))), vec, vec],
        out_shape=[jax.ShapeDtypeStruct((geo.nrows, geo.cd), F32), jax.ShapeDtypeStruct(dproj.shape, dproj.dtype),
                   jax.ShapeDtypeStruct((1, LANE), F32), jax.ShapeDtypeStruct((1, LANE), F32)],
        scratch_shapes=[pltpu.VMEM((n, di), F32), pltpu.VMEM((q, di), F32), pltpu.VMEM((q, di), F32)],
        input_output_aliases={10: 1},
        compiler_params=_cparams(("arbitrary", "arbitrary")))(
            xc, xc, xc, proj, dt_bias, a_log, s_prev_all, dy, dxs_skip, e_map, dproj)


BIAS_LANE = MLA_ROPE // 2
KEY_OFF = -1e30
ATT_SCALE = (MLA_NOPE + MLA_ROPE) ** -0.5


def _row_t(col):
    return jnp.broadcast_to(col, (col.shape[0], LANE)).T[:8]


def _att_tile(geo, i):
    return (0, geo.t0) if i == 0 else (geo.t0 + (i - 1) * ATT_BLK, ATT_BLK)


def _attn_fwd4(geo, qn, qp, kn, kp, v):
    t, lp, nq = ATT_BLK, geo.lp, geo.nq

    def body(qn_ref, qp_ref, kn_ref, kp_ref, v_ref, o_ref, lse_ref, k_ref):
        qi = pl.program_id(2)

        @pl.when(qi == 0)
        def _():
            k_ref[:, :LANE] = kn_ref[...]
            k_ref[:, LANE:] = kp_ref[...]

        for i in range(nq):
            @pl.when(qi == i)
            def _(i=i):
                r0, rows = _att_tile(geo, i)
                w = r0 + rows
                q = jnp.concatenate([qn_ref[r0:w, :], qp_ref[r0:w, :]], axis=1)
                s = _mxdot(q, k_ref[0:w, :], 1, 1) * ATT_SCALE
                diag = jnp.where(_tri(rows), s[:, r0:], -jnp.inf)
                s = diag if r0 == 0 else jnp.concatenate([s[:, :r0], diag], axis=1)
                m = jnp.max(s, axis=1, keepdims=True)
                pr = jnp.exp(s - m)
                l = jnp.sum(pr, axis=1, keepdims=True)
                o_ref[r0:w, :] = (_mxdot(pr, v_ref[0:w, :], 1, 0) / l).astype(o_ref.dtype)
                lse_ref[0, 0, 0, :, 0:rows] = _row_t(m + jnp.log(l))

    seq = pl.BlockSpec((lp, LANE), lambda b, h, i: (b, h))
    return pl.pallas_call(
        body, name="attn_fwd", grid=(geo.bsz, MLA_HEADS, nq),
        in_specs=[seq, seq, seq, pl.BlockSpec((lp, LANE), lambda b, h, i: (b, 0)), seq],
        out_specs=[seq, pl.BlockSpec((1, 1, 1, 8, t), lambda b, h, i: (b, h, i, 0, 0))],
        out_shape=[jax.ShapeDtypeStruct((geo.nrows, geo.hq), MXU_DTYPE),
                   jax.ShapeDtypeStruct((geo.bsz, MLA_HEADS, nq, 8, t), F32)],
        scratch_shapes=[pltpu.VMEM((lp, 2 * LANE), MXU_DTYPE)],
        compiler_params=_cparams(("parallel", "parallel", "arbitrary")))(qn, qp, kn, kp, v)


def _attn_bwd4(geo, qn, qp, kn, kp, v, d_o, o, lse):
    t, lp, nq = ATT_BLK, geo.lp, geo.nq

    def body(qn_ref, qp_ref, kn_ref, kp_ref, v_ref, do_ref, o_ref, lse_ref,
             dqn_ref, dqp_ref, dkn_ref, dkp_ref, dv_ref, q_ref, dl_s):
        kj = pl.program_id(2)

        @pl.when(kj == 0)
        def _():
            q_ref[:, :LANE] = qn_ref[...]
            q_ref[:, LANE:] = qp_ref[...]
            dqn_ref[...] = jnp.zeros_like(dqn_ref)
            dqp_ref[...] = jnp.zeros_like(dqp_ref)
            for i in range(nq):
                r0, rows = _att_tile(geo, i)
                dl_s[i, :, 0:rows] = _row_t(jnp.sum(do_ref[r0:r0 + rows, :].astype(F32) * o_ref[r0:r0 + rows, :].astype(F32),
                                                    axis=1, keepdims=True))

        def per_query(ref, first):
            return jnp.concatenate([ref[i][:1, 0:_att_tile(geo, i)[1]] for i in range(first, nq)], axis=1)

        for i in range(nq):
            @pl.when(kj == i)
            def _(i=i):
                k0, kw = _att_tile(geo, i)
                k = jnp.concatenate([kn_ref[k0:k0 + kw, :], kp_ref[k0:k0 + kw, :]], axis=1)
                vv = v_ref[k0:k0 + kw, :]
                q, d_o_blk = q_ref[k0:lp, :], do_ref[k0:lp, :]
                wq = lp - k0
                st = _mxdot(k, q, 1, 1) * ATT_SCALE
                keys = lax.broadcasted_iota(jnp.int32, (kw, kw), 0)
                diag = jnp.where(keys <= lax.broadcasted_iota(jnp.int32, (kw, kw), 1), st[:, :kw], -jnp.inf)
                st = diag if wq == kw else jnp.concatenate([diag, st[:, kw:]], axis=1)
                pt = jnp.exp(st - per_query(lse_ref.at[0, 0], i))
                dst = pt * (_mxdot(vv, d_o_blk, 1, 1) - per_query(dl_s, i)) * ATT_SCALE
                dq = _mxdot(dst, k, 0, 0)
                dqn_ref[k0:lp, :] += dq[:, :LANE]
                dqp_ref[k0:lp, :] += dq[:, LANE:]
                dk = _mxdot(dst, q, 1, 0)
                dkn_ref[k0:k0 + kw, :] = dk[:, :LANE].astype(dkn_ref.dtype)
                dkp_ref[k0:k0 + kw, :] = dk[:, LANE:]
                dv_ref[k0:k0 + kw, :] = _mxdot(pt, d_o_blk, 1, 0).astype(dv_ref.dtype)

    seq = pl.BlockSpec((lp, LANE), lambda b, h, j: (b, h))
    return pl.pallas_call(
        body, name="attn_bwd", grid=(geo.bsz, MLA_HEADS, nq),
        in_specs=[seq, seq, seq, pl.BlockSpec((lp, LANE), lambda b, h, j: (b, 0)), seq, seq, seq,
                  pl.BlockSpec((1, 1, nq, 8, t), lambda b, h, j: (b, h, 0, 0, 0))],
        out_specs=[seq, seq, seq, seq, seq],
        out_shape=[jax.ShapeDtypeStruct((geo.nrows, geo.hq), F32), jax.ShapeDtypeStruct((geo.nrows, geo.hq), F32),
                   jax.ShapeDtypeStruct((geo.nrows, geo.hq), MXU_DTYPE), jax.ShapeDtypeStruct((geo.nrows, geo.hq), F32),
                   jax.ShapeDtypeStruct((geo.nrows, geo.hq), MXU_DTYPE)],
        scratch_shapes=[pltpu.VMEM((lp, 2 * LANE), MXU_DTYPE), pltpu.VMEM((nq, 8, t), F32)],
        compiler_params=_cparams(("parallel", "parallel", "arbitrary")))(qn, qp, kn, kp, v, d_o, o, lse)


def _rope(x, cos, sin):
    return x * cos + pltpu.roll(x, LANE // 2, axis=1) * sin


def _rope_t(dx, cos, sin):
    return dx * cos + pltpu.roll(dx * sin, LANE // 2, axis=1)


def _per_head(f):
    def fn(x, cos, sin):
        return (jnp.concatenate([f(x[:, h * LANE:(h + 1) * LANE], cos, sin) for h in range(MLA_HEADS)], axis=1),)
    return fn


def _layer_fwd(geo, h, w, tab, late=None):
    nr, tr, trw = geo.nrows, geo.tr, geo.tr_wide
    tb = geo.lp // tr
    rw = functools.partial(_rowwise, nrows=nr)
    s = {"h": h}
    (s["u"],) = rw("rms_mix", lambda x, g: (_rms(x, g),), tr=tr, rows=[(h, D_MODEL, 0)],
                   vecs=[(w["norm_mix_w"], D_MODEL, 0)], outs=[(D_MODEL, D_MODEL, MXU_DTYPE)])
    proj, s["proj_dt"] = _mm("mm_in", s["u"], w["w_in_pt"], tb=True, out_dtype=MXU_DTYPE,
                             side=(geo.col["dt"][0], LANE))
    s["proj"] = proj
    xc = s["xc"] = _conv_fwd(geo, proj, w["conv_w"], w["conv_b"])
    s["y_ssd"], s["s_prev"] = _ssd_fwd_g(geo, xc, s["proj_dt"], w["dt_bias"], w["a_log"])
    gw = SSM_D_INNER // SSM_GROUPS

    def gate_norm(y, x, z, dsk, nw):
        return (_rms((y + x * dsk) * _silu(z.astype(F32)), nw),)

    (s["y_ssm"],) = rw("ssm_gate_norm", gate_norm, tr=tr, ncb=SSM_GROUPS,
                       rows=[(s["y_ssd"], gw, 0), (xc, gw, 0), (proj, gw, geo.col["z"][0] // gw)],
                       vecs=[(w["d_skip_full"], gw, 0), (w["ssm_norm_w"], gw, 0)], outs=[(SSM_D_INNER, gw, MXU_DTYPE)])
    if late is not None:
        w = {**w, **late(s["y_ssm"])}
    (s["cq_n"],) = rw("rms_q", lambda x, g: (_rms(x, g),), tr=tr, rows=[(proj, MLA_Q_LORA, geo.cb("c_q"))],
                      vecs=[(w["q_norm_w"], MLA_Q_LORA, 0)], outs=[(MLA_Q_LORA, MLA_Q_LORA, MXU_DTYPE)])
    (s["ckv_n"],) = rw("rms_kv", lambda x, g: (_rms(x, g),), tr=tr, rows=[(proj, MLA_KV_LORA, geo.cb("c_kv"))],
                       vecs=[(w["kv_norm_w"], MLA_KV_LORA, 0)], outs=[(MLA_KV_LORA, MLA_KV_LORA, MXU_DTYPE)])
    s["qn"] = _mm("mm_qn", s["cq_n"], w["w_qn"], out_dtype=MXU_DTYPE)
    qp_raw = _mm("mm_qp", s["cq_n"], w["w_qp"])
    s["kn"] = _mm("mm_kn", s["ckv_n"], w["w_k"], out_dtype=MXU_DTYPE)
    s["v"] = _mm("mm_v", s["ckv_n"], w["w_v"], out_dtype=MXU_DTYPE)
    bias_lane = lambda: lax.broadcasted_iota(jnp.int32, (1, LANE), 1) == BIAS_LANE
    rope_tabs = [(tab["cos"], LANE, 0), (tab["sin"], LANE, 0)]
    (s["qp"],) = rw("rope_q", _per_head(lambda xp, c, sn: jnp.where(bias_lane(), 1.0, _rope(xp, c, sn))), tr=tr,
                    rows=[(qp_raw, geo.hq, 0)], tabs=rope_tabs, outs=[(geo.hq, geo.hq, MXU_DTYPE)], tab_blocks=tb)
    (s["kp"],) = rw("rope_k", lambda xp, c, sn, valid: (jnp.where(bias_lane(), KEY_OFF * (1.0 - valid),
                                                                 _rope(xp.astype(F32), c, sn)),),
                    tr=tr, rows=[(proj, LANE, geo.cb("k_rope"))], tabs=rope_tabs + [(tab["valid"], 1, 0)],
                    outs=[(LANE, LANE, MXU_DTYPE)], tab_blocks=tb)
    s["o"], s["lse"] = _attn_fwd4(geo, s["qn"], s["qp"], s["kn"], s["kp"], s["v"])
    s["ys_p"] = _mm("mm_bs", s["y_ssm"], w["w_branch_ssm"], out_dtype=MXU_DTYPE)
    s["ym_p"] = _mm("mm_bm", s["o"], w["w_branch_mla"], out_dtype=MXU_DTYPE)

    def gate(gs, gm, ys, ym):
        return (_sigmoid(gs.astype(F32)) * ys + _sigmoid(gm.astype(F32)) * ym,)

    (s["mixed"],) = rw("gate", gate, tr=tr, rows=[(proj, D_MODEL, geo.cb("g_ssm")), (proj, D_MODEL, geo.cb("g_mla")),
                                                  (s["ys_p"], D_MODEL, 0), (s["ym_p"], D_MODEL, 0)],
                       outs=[(D_MODEL, D_MODEL, MXU_DTYPE)])
    s["h2"] = _mm("mm_out", s["mixed"], w["w_out"], add=h)
    (s["vn"],) = rw("rms_mlp", lambda x, g: (_rms(x, g),), tr=tr, rows=[(s["h2"], D_MODEL, 0)],
                    vecs=[(w["norm_mlp_w"], D_MODEL, 0)], outs=[(D_MODEL, D_MODEL, MXU_DTYPE)])
    s["up"], s["act"] = _mm("mm_up", s["vn"], w["w_mlp_up"],
                            epi=(lambda r: (r, jnp.square(jnp.maximum(r, 0.0))), (MXU_DTYPE, MXU_DTYPE)))
    return _mm("mm_down", s["act"], w["w_mlp_down"], add=s["h2"]), s, w


def _layer_bwd(geo, dh3, s, w, tab, mid=None, tail=None, dep=None):
    nr, tr, trw = geo.nrows, geo.tr, geo.tr_wide
    tb = geo.lp // tr
    rw = functools.partial(_rowwise, nrows=nr)
    g = {}
    proj = s["proj"]

    def rms_bwd(x, dy, res, gw):
        _, vjp = jax.vjp(_rms, x.astype(F32), gw)
        dx, dgw = vjp(dy.astype(F32))
        return dx + res, dgw

    def rms_bwd_nores(x, dy, gw):
        _, vjp = jax.vjp(_rms, x.astype(F32), gw)
        return vjp(dy.astype(F32))

    (dup,) = _mm("mm_down_t", dh3, w["w_mlp_down"], tb=True, add=s["up"], dep=dep,
                 epi=(lambda r, up: (r * 2.0 * jnp.maximum(up, 0.0),), (MXU_DTYPE,)))
    g["w_mlp_down"] = _mm("mm_down_g", s["act"], dh3, ta=True, out_dtype=MXU_DTYPE)
    g["w_mlp_up"] = _mm("mm_up_g", s["vn"], dup, ta=True, out_dtype=MXU_DTYPE)
    dvn = _mm("mm_up_t", dup, w["w_mlp_up"], tb=True)
    dh2, g["norm_mlp_w"] = rw("rms_mlp_bwd", rms_bwd, tr=tr,
                              rows=[(s["h2"], D_MODEL, 0), (dvn, D_MODEL, 0), (dh3, D_MODEL, 0)],
                              vecs=[(w["norm_mlp_w"], D_MODEL, 0)], outs=[(D_MODEL, D_MODEL, F32)],
                              reds=[(D_MODEL, D_MODEL)])
    dmixed = _mm("mm_out_t", dh2, w["w_out"], tb=True, out_dtype=MXU_DTYPE)
    g["w_out"] = _mm("mm_out_g", s["mixed"], dh2, ta=True, out_dtype=MXU_DTYPE)

    def gate_bwd(gs, gm, ys, ym, dm):
        f = lambda a, b, c, d: _sigmoid(a) * c + _sigmoid(b) * d
        _, vjp = jax.vjp(f, gs.astype(F32), gm.astype(F32), ys.astype(F32), ym.astype(F32))
        dgs, dgm, dys, dym = vjp(dm.astype(F32))
        return dys, dym, jnp.concatenate([dgs, dgm], axis=1)

    assert geo.col["g_mla"][0] == geo.col["g_ssm"][0] + D_MODEL and geo.col["g_ssm"][0] % (2 * D_MODEL) == 0
    dys_p, dym_p, dproj = rw(
        "gate_bwd", gate_bwd, tr=tr,
        rows=[(proj, D_MODEL, geo.cb("g_ssm")), (proj, D_MODEL, geo.cb("g_mla")), (s["ys_p"], D_MODEL, 0),
              (s["ym_p"], D_MODEL, 0), (dmixed, D_MODEL, 0)],
        outs=[(D_MODEL, D_MODEL, MXU_DTYPE)] * 2 + [(geo.pw, 2 * D_MODEL, MXU_DTYPE, geo.col["g_ssm"][0] // (2 * D_MODEL))])
    g["w_branch_ssm"] = _mm("mm_bs_g", s["y_ssm"], dys_p, ta=True, out_dtype=MXU_DTYPE)
    dy_ssm = _mm("mm_bs_t", dys_p, w["w_branch_ssm"], tb=True, out_dtype=MXU_DTYPE)
    g["w_branch_mla"] = _mm("mm_bm_g", s["o"], dym_p, ta=True, out_dtype=MXU_DTYPE)
    d_o = _mm("mm_bm_t", dym_p, w["w_branch_mla"], tb=True, out_dtype=MXU_DTYPE)
    dqn, dqp, dkn, dkp_h, dv = _attn_bwd4(geo, s["qn"], s["qp"], s["kn"], s["kp"], s["v"], d_o, s["o"], s["lse"])
    rope_tabs = [(tab["cos"], LANE, 0), (tab["sin"], LANE, 0)]
    (dqp_raw,) = rw("rope_q_bwd", _per_head(_rope_t), tr=tr, rows=[(dqp, geo.hq, 0)], tabs=rope_tabs,
                    outs=[(geo.hq, geo.hq, MXU_DTYPE)], tab_blocks=tb)

    def rope_k_bwd(x, c, sn):
        tot = x[:, :LANE]
        for hd in range(1, MLA_HEADS):
            tot = tot + x[:, hd * LANE:(hd + 1) * LANE]
        return (_rope_t(tot, c, sn),)

    (dproj,) = rw("rope_k_bwd", rope_k_bwd, tr=tr, rows=[(dkp_h, geo.hq, 0)], tabs=rope_tabs,
                  outs=[(geo.pw, LANE, MXU_DTYPE, geo.cb("k_rope"), dproj)], tab_blocks=tb)
    g["w_qn"] = _mm("mm_qn_g", s["cq_n"], dqn, ta=True, out_dtype=MXU_DTYPE)
    g["w_qp"] = _mm("mm_qp_g", s["cq_n"], dqp_raw, ta=True, out_dtype=MXU_DTYPE)
    dcq_n = _mm("mm_qp_t", dqp_raw, w["w_qp"], tb=True, add=_mm("mm_qn_t", dqn, w["w_qn"], tb=True))
    g["w_k"] = _mm("mm_kn_g", s["ckv_n"], dkn, ta=True, out_dtype=MXU_DTYPE)
    g["w_v"] = _mm("mm_v_g", s["ckv_n"], dv, ta=True, out_dtype=MXU_DTYPE)
    dckv_n = _mm("mm_v_t", dv, w["w_v"], tb=True, add=_mm("mm_kn_t", dkn, w["w_k"], tb=True))
    dproj, g["q_norm_w"] = rw("rms_q_bwd", rms_bwd_nores, tr=tr,
                              rows=[(proj, MLA_Q_LORA, geo.cb("c_q")), (dcq_n, MLA_Q_LORA, 0)],
                              vecs=[(w["q_norm_w"], MLA_Q_LORA, 0)],
                              outs=[(geo.pw, MLA_Q_LORA, MXU_DTYPE, geo.cb("c_q"), dproj)], reds=[(MLA_Q_LORA, MLA_Q_LORA)])
    dproj, g["kv_norm_w"] = rw("rms_kv_bwd", rms_bwd_nores, tr=tr,
                               rows=[(proj, MLA_KV_LORA, geo.cb("c_kv")), (dckv_n, MLA_KV_LORA, 0)],
                               vecs=[(w["kv_norm_w"], MLA_KV_LORA, 0)],
                               outs=[(geo.pw, MLA_KV_LORA, MXU_DTYPE, geo.cb("c_kv"), dproj)],
                               reds=[(MLA_KV_LORA, MLA_KV_LORA)])
    gw_ = SSM_D_INNER // SSM_GROUPS
    d_skip_full = w["d_skip_full"] if mid is None else w["d_skip_full"] + mid(g)[0, 0]

    def gate_norm_bwd(y, x, z, dy, dsk, nw):
        f = lambda y_, x_, z_, dsk_, nw_: _rms((y_ + x_ * dsk_) * _silu(z_), nw_)
        _, vjp = jax.vjp(f, y.astype(F32), x.astype(F32), z.astype(F32), dsk, nw)
        dy_, dx_, dz_, ddsk, dnw = vjp(dy.astype(F32))
        return dy_, dx_, dz_, ddsk, dnw

    dy_ssd, dxs_skip, dproj, g["d_skip_full"], g["ssm_norm_w"] = rw(
        "ssm_gate_norm_bwd", gate_norm_bwd, tr=tr, ncb=SSM_GROUPS,
        rows=[(s["y_ssd"], gw_, 0), (s["xc"], gw_, 0), (proj, gw_, geo.col["z"][0] // gw_), (dy_ssm, gw_, 0)],
        vecs=[(d_skip_full, gw_, 0), (w["ssm_norm_w"], gw_, 0)],
        outs=[(SSM_D_INNER, gw_, MXU_DTYPE), (SSM_D_INNER, gw_, MXU_DTYPE),
              (geo.pw, gw_, MXU_DTYPE, geo.col["z"][0] // gw_, dproj)],
        reds=[(SSM_D_INNER, gw_), (SSM_D_INNER, gw_)])
    dxc, dproj, g["dt_bias"], g["a_log"] = _ssd_bwd_g(geo, s["xc"], s["proj_dt"], w["dt_bias"], w["a_log"], s["s_prev"],
                                                     dy_ssd, dxs_skip, dproj)
    dproj, g["conv_w"], g["conv_b"] = _conv_bwd(geo, proj, w["conv_w"], w["conv_b"], dxc, dproj)
    g["w_in_pt"] = _mm("mm_in_g", dproj, s["u"], ta=True, out_dtype=MXU_DTYPE)
    du = _mm("mm_in_t", dproj, w["w_in_pt"], dep=None if tail is None else tail(g))
    dh, g["norm_mix_w"] = rw("rms_mix_bwd", rms_bwd, tr=tr,
                             rows=[(s["h"], D_MODEL, 0), (du, D_MODEL, 0), (dh2, D_MODEL, 0)],
                             vecs=[(w["norm_mix_w"], D_MODEL, 0)], outs=[(D_MODEL, D_MODEL, F32)],
                             reds=[(D_MODEL, D_MODEL)])
    return dh, g


def _loss_bwd(geo, h, fw, target, tab):
    tr = geo.tr

    def fn(x, tgt, gw, tok):
        def lossf(x_, gw_):
            err = jnp.square(_rms(x_, gw_) - tgt)
            return 0.5 * jnp.sum(tok * jnp.mean(err, axis=-1, keepdims=True), axis=0, keepdims=True)

        val, vjp = jax.vjp(lossf, x, gw)
        dx, dgw = vjp(jnp.ones((1, 1), F32))
        return dx, jnp.broadcast_to(val, (1, LANE)), dgw

    return _rowwise("loss", fn, nrows=geo.nrows, tr=tr, rows=[(h, D_MODEL, 0), (target, D_MODEL, 0)],
                    vecs=[(fw, D_MODEL, 0)], tabs=[(tab["token"], 1, 0)], outs=[(D_MODEL, D_MODEL, F32)],
                    reds=[(LANE, LANE), (D_MODEL, D_MODEL)], tab_blocks=geo.lp // tr)


def kernel(x, meta_tokens, norm_mix_w, w_in, conv_w, conv_b, dt_bias, a_log, d_skip, ssm_norm_w, q_norm_w, kv_norm_w, w_uq, w_ukv, w_branch_ssm, w_branch_mla, w_out, norm_mlp_w, w_mlp_up, w_mlp_down, final_norm_w, loss_target, m_meta_tokens, m_norm_mix_w, m_w_in, m_conv_w, m_conv_b, m_dt_bias, m_a_log, m_d_skip, m_ssm_norm_w, m_q_norm_w, m_kv_norm_w, m_w_uq, m_w_ukv, m_w_branch_ssm, m_w_branch_mla, m_w_out, m_norm_mlp_w, m_w_mlp_up, m_w_mlp_down, m_final_norm_w, v_meta_tokens, v_norm_mix_w, v_w_in, v_conv_w, v_conv_b, v_dt_bias, v_a_log, v_d_skip, v_ssm_norm_w, v_q_norm_w, v_kv_norm_w, v_w_uq, v_w_ukv, v_w_branch_ssm, v_w_branch_mla, v_w_out, v_norm_mlp_w, v_w_mlp_up, v_w_mlp_down, v_final_norm_w):
    args = dict(locals())
    turn = lambda n, a: jnp.swapaxes(a, 1, 2) if n == "w_in" else a
    wts = {n: turn(n, args[n]) for n in WEIGHTS}
    mom = {n: turn(n, args["m_" + n]) for n in WEIGHTS}
    var = {n: turn(n, args["v_" + n]) for n in WEIGHTS}
    bsz, seq, _ = x.shape
    depth = w_in.shape[0]
    geo = _Geo(bsz, seq)
    tab = _tables(geo)

    big_names = [n for n, _ in BIG]
    sh_names = big_names + [n for n, _ in SHARDED_F32]
    kinds = dict(BIG + SHARDED_F32, w_in="row")
    shard3 = lambda a: a.reshape((1,) + a.shape) if a.ndim == 2 else a
    wire = {n: (MXU_DTYPE if n in big_names else F32) for n in sh_names}
    cast = {n: shard3(wts[n]).astype(wire[n]) for n in sh_names}
    per_layer = [n for n in sh_names if n != "meta_tokens"]
    small_names = ["norm_mix_w", "conv_b", "dt_bias", "a_log", "d_skip", "ssm_norm_w", "q_norm_w", "kv_norm_w",
                   "norm_mlp_w"]

    def gather_items(pairs):
        ins, outs, items, forms = [], [], [], []
        for n, i in pairs:
            a, b = cast[n].shape[1:]
            shape, dst, form = _gather_plan(a, b, kinds[n])
            items.append((len(ins), len(outs), (lambda ref, p, i=i: ref.at[i]), dst))
            ins.append(cast[n])
            outs.append(jax.ShapeDtypeStruct(shape, wire[n]))
            forms.append(form)
        return ins, outs, items, forms

    def whole_weights(pairs, forms, got):
        by_layer = {}
        for (n, i), form, g in zip(pairs, forms, got):
            if n == "w_in":
                n, g = "w_in_pt", _w_in_assemble(geo, g)
            elif form == "row":
                g = g.reshape(g.shape[0] * g.shape[1], g.shape[2])
            elif form == "stack":
                g = _unshard(g, "col")
            by_layer.setdefault(i, {})[n] = g
        return by_layer

    def prep(i, whole, token=None):
        wl = dict(whole)
        wl.update({n: wts[n][i] for n in small_names})
        if token is not None:
            wl["norm_mix_w"] = wl["norm_mix_w"] + token[0, 0]
        return _prep_layer(geo, wl)

    early = ("w_in", "conv_w")
    late_names = [n for n in per_layer if n not in early]
    pairs1 = [(n, i) for i in range(1, depth) for n in per_layer]
    groups = [[(n, 0) for n in early] + [("meta_tokens", 0)], [(n, 0) for n in late_names]] + ([pairs1] if pairs1 else [])
    started = {}

    def gather_start(gi, dep=None):
        ins, outs, items, forms = gather_items(groups[gi])
        sems, thru, landing, token = _exchange_start("gather_w%d_start" % gi, ins, outs, items, dep)
        started[gi] = (groups[gi], forms, sems, thru, landing, items)
        return token

    def gathered(gi, after):
        pairs, forms, sems, thru, landing, items = started[gi]
        return whole_weights(pairs, forms, _exchange_wait("gather_w%d_wait" % gi, sems, thru, landing, items, after))

    def late0(after):
        whole = gathered(1, after)[0]
        if pairs1:
            whole["q_norm_w"] = wts["q_norm_w"][0] + gather_start(2, whole["w_out"])[0, 0]
        return _prep_layer(geo, whole)

    token = gather_start(1, gather_start(0))
    whole0 = gathered(0, token)[0]
    meta_full = whole0.pop("meta_tokens")

    meta = jnp.broadcast_to(meta_full[None], (bsz, N_META, D_MODEL))
    h = jnp.concatenate([jnp.zeros((bsz, geo.pad, D_MODEL), F32), meta, x], axis=1).reshape(geo.nrows, D_MODEL)
    target = jnp.concatenate([jnp.zeros((bsz, geo.pad + N_META, D_MODEL), F32), loss_target], axis=1)
    target = target.reshape(geo.nrows, D_MODEL)
    layers, saved = [], []
    for i in range(depth):
        if i == 0:
            w, late = prep(0, whole0, token), late0
        else:
            if i == 1:
                whole1 = gathered(2, h)
            w, late = prep(i, whole1[i]), None
        h, s, w = _layer_fwd(geo, h, w, tab, late)
        layers.append(w)
        saved.append(s)
    dh, loss_part, g_final = _loss_bwd(geo, h, final_norm_w.reshape(1, -1), target, tab)

    def scatter_items(pairs):
        ins, outs, items = [], [], []
        for n, i in pairs:
            a, b = cast[n].shape[1:]
            arr = g_meta if n == "meta_tokens" else grads[i]["w_in_pt" if n == "w_in" else n]
            if n == "w_in":
                arr, src = _w_in_split(geo, arr, a), _entry
            elif kinds[n] == "row":
                src = lambda ref, p, a=a: ref.at[pl.ds(pl.multiple_of(p * a, a), a)]
            elif b % LANE == 0:
                src = lambda ref, p, b=b: ref.at[:, pl.ds(pl.multiple_of(p * b, b), b)]
            else:
                arr, src = _shard(arr, "col"), _entry
            items.append((len(ins), len(outs), src, _entry))
            ins.append(arr.astype(wire[n]))
            outs.append(jax.ShapeDtypeStruct((N_DEV, a, b), wire[n]))
        return ins, outs, items

    grads = [None] * depth
    landed, pending, res = {}, {}, {}

    def scatter_start(name, pairs):
        ins, outs, items = scatter_items(pairs)
        sems, thru, landing, token = _exchange_start(name + "_start", ins, outs, items)
        pending[name] = (pairs, sems, thru, landing, items)
        return token

    def scatter_wait(name, after):
        pairs, sems, thru, landing, items = pending[name]
        landed.update(zip(pairs, _exchange_wait(name + "_wait", sems, thru, landing, items, after)))

    def adam(n):
        parts = [landed[(n, i)] for i in range(cast[n].shape[0])]
        r = _adamw_nat("adamw_" + n, parts, shard3(wts[n]), shard3(mom[n]), shard3(var[n]))
        res[n] = [a.reshape(wts[n].shape) for a in r]

    def mid0(g):
        grads[0] = _unprep_grads(geo, g)
        return scatter_start("scatter_gb0", [(n, 0) for n in late_names])

    def tail0(g):
        grads[0] = _unprep_grads(geo, g)
        return scatter_start("scatter_ga0", [(n, 0) for n in early])

    dep = None
    for i in reversed(range(depth)):
        dh, gl = _layer_bwd(geo, dh, saved[i], layers[i], tab, *((mid0, tail0) if i == 0 else (None, None)), dep)
        grads[i] = _unprep_grads(geo, gl)
        if i == 1:
            dep = scatter_start("scatter_g1", pairs1)
    dh = dh.reshape(bsz, geo.lp, D_MODEL)
    grad_x = dh[:, geo.pad + N_META:]
    g_meta = jnp.sum(dh[:, geo.pad:geo.pad + N_META], axis=0)
    if pairs1:
        scatter_wait("scatter_g1", g_meta)
    scatter_wait("scatter_gb0", g_meta)
    for n in late_names:
        adam(n)
    g_small = {n: jnp.stack([grads[i][n] for i in range(depth)]) for n in SMALL if n != "final_norm_w"}
    g_small["final_norm_w"] = g_final.reshape(-1)
    zero = jnp.zeros((1,), F32)
    pk = lambda d, last: _pack([d[n] for n in SMALL] + [last], F32, row_mult=8)
    packed = pk(g_small, loss_part[0, :1])
    ins, outs, items = scatter_items([("meta_tokens", 0)])
    parts, landed[("meta_tokens", 0)] = _exchange(
        "gather_g", [packed] + ins + [res[n][1] for n in late_names],
        [jax.ShapeDtypeStruct((N_DEV,) + packed.shape, F32)] + outs,
        [(0, 0, _whole, _entry)] + [(1, 1, items[0][2], items[0][3])])
    adam("meta_tokens")
    scatter_wait("scatter_ga0", res["meta_tokens"][1])
    for n in early:
        adam(n)
    res_sm = _adamw("adamw_small", parts, pk(wts, zero), pk(mom, zero), pk(var, zero))
    res_sm = [_unpack(r, [wts[n].shape for n in SMALL] + [(1,)]) for r in res_sm]
    loss = res_sm[0][-1][0]

    out = [loss, grad_x]
    for k in range(4):
        named = {n: res[n][k] for n in sh_names}
        named.update(zip(SMALL, res_sm[k]))
        out += [turn(n, named[n]) for n in WEIGHTS]
    return tuple(out)
```

```python
import functools

import numpy as np
import jax
import jax.numpy as jnp
from jax import lax
from jax.experimental import pallas as pl
from jax.experimental.pallas import tpu as pltpu

F32 = jnp.float32
MXU_DTYPE = jnp.bfloat16

D_MODEL = 1024
N_META = 16
EPS = 1e-6
SSM_D_INNER = 2048
SSM_HEAD_DIM = 64
SSM_GROUPS = 4
SSM_STATE = 128
SSM_CONV = 4
SSM_CHUNK = 128
MLA_HEADS = 8
MLA_Q_LORA = 512
MLA_KV_LORA = 256
MLA_NOPE = 128
MLA_ROPE = 64
MLA_V = 128
ROPE_THETA = 10000.0
D_FF = 4096
ADAM_LR = 0.001
ADAM_B1 = 0.9
ADAM_B2 = 0.999
ADAM_EPS = 1e-08
ADAM_WD = 0.01
ADAM_STEP = 10

N_DEV = 8
ATT_BLK = 512
LANE = 128
PACK_W = 1024
VMEM_LIMIT = 56 * 1024 * 1024
MESH_ID = pl.DeviceIdType.MESH

BIG = (("w_in", "col"), ("w_uq", "col"), ("w_ukv", "col"), ("w_branch_ssm", "row"), ("w_branch_mla", "row"),
       ("w_out", "row"), ("w_mlp_up", "col"), ("w_mlp_down", "row"))
SHARDED_F32 = (("conv_w", "col"), ("meta_tokens", "col"))
SMALL = ("norm_mix_w", "conv_b", "dt_bias", "a_log", "d_skip", "ssm_norm_w", "q_norm_w", "kv_norm_w",
         "norm_mlp_w", "final_norm_w")
WEIGHTS = ("meta_tokens", "norm_mix_w", "w_in", "conv_w", "conv_b", "dt_bias", "a_log", "d_skip", "ssm_norm_w",
           "q_norm_w", "kv_norm_w", "w_uq", "w_ukv", "w_branch_ssm", "w_branch_mla", "w_out", "norm_mlp_w",
           "w_mlp_up", "w_mlp_down", "final_norm_w")


def _cparams(sem=None):
    return pltpu.CompilerParams(dimension_semantics=sem, vmem_limit_bytes=VMEM_LIMIT)


def _pick(n, cands):
    for c in cands:
        if n % c == 0:
            return c
    return n


def _sigmoid(x):
    return 1.0 / (1.0 + jnp.exp(-x))


def _silu(x):
    return x * _sigmoid(x)


def _softplus(x):
    t = jnp.exp(-jnp.abs(x))
    return jnp.maximum(x, 0.0) + jnp.where(t < 0.01, t * (1.0 - t * (0.5 - t * (1.0 / 3.0))), jnp.log(1.0 + t))


def _rms(x, w):
    x = x.astype(F32)
    return x * lax.rsqrt(jnp.mean(x * x, axis=-1, keepdims=True) + EPS) * w


def _dot(a, b, ca, cb, precision=None):
    return lax.dot_general(a, b, (((ca,), (cb,)), ((), ())), preferred_element_type=F32, precision=precision)


def _mxdot(a, b, ca, cb):
    return _dot(a.astype(MXU_DTYPE), b.astype(MXU_DTYPE), ca, cb)


def _mm(name, a, b, *, ta=False, tb=False, add=None, out_dtype=F32, dep=None, epi=None, side=None):
    (kdim, m) = a.shape if ta else a.shape[::-1]
    (n, k2) = b.shape if tb else b.shape[::-1]
    assert kdim == k2, (name, a.shape, b.shape)
    tm = _pick(m, (1152, 1088, 1024, 768, 544, 512, 384, 256, 128))
    tn = _pick(n, (1024, 512, 384, 256, 128))
    tk = _pick(kdim, (1152, 1088, 1024, 768, 544, 512, 384, 256, 128))
    nk = kdim // tk
    a_spec = pl.BlockSpec((tk, tm), lambda i, j, k: (k, i)) if ta else pl.BlockSpec((tm, tk), lambda i, j, k: (i, k))
    b_spec = pl.BlockSpec((tn, tk), lambda i, j, k: (j, k)) if tb else pl.BlockSpec((tk, tn), lambda i, j, k: (k, j))
    o_spec = pl.BlockSpec((tm, tn), lambda i, j, k: (i, j))
    ca, cb = (0 if ta else 1), (1 if tb else 0)

    out_dtypes = [out_dtype] if epi is None else list(epi[1])
    n_out = len(out_dtypes)
    n_side = 0 if side is None else 1

    def body(*refs):
        a_ref, b_ref = refs[:2]
        o_refs, acc = refs[-1 - n_side - n_out:-1 - n_side], refs[-1]
        k = pl.program_id(2)

        @pl.when(k == 0)
        def _():
            acc[...] = jnp.zeros_like(acc)

        acc[...] += _mxdot(a_ref[...], b_ref[...], ca, cb)

        @pl.when(k == nk - 1)
        def _():
            r = acc[...]
            if epi is not None:
                res = epi[0](r, refs[2][...]) if add is not None else epi[0](r)
            else:
                res = (r + refs[2][...].astype(F32) if add is not None else r,)
            for o_ref, val in zip(o_refs, res):
                o_ref[...] = val.astype(o_ref.dtype)

        if side is not None:
            @pl.when(jnp.logical_and(k == nk - 1, pl.program_id(1) == side[0] // tn))
            def _():
                refs[-2][...] = acc[:, side[0] % tn:side[0] % tn + side[1]]

    in_specs, args = [a_spec, b_spec], [a, b]
    if add is not None:
        in_specs.append(o_spec)
        args.append(add)
    if dep is not None:
        in_specs.append(pl.BlockSpec((8, LANE), lambda i, j, k: (0, 0)))
        args.append(dep)
    out_specs = [o_spec] * n_out
    out_shape = [jax.ShapeDtypeStruct((m, n), dt) for dt in out_dtypes]
    if side is not None:
        assert side[0] % tn + side[1] <= tn
        out_specs.append(pl.BlockSpec((tm, side[1]), lambda i, j, k: (i, 0)))
        out_shape.append(jax.ShapeDtypeStruct((m, side[1]), F32))
    res = pl.pallas_call(
        body, name=name, grid=(m // tm, n // tn, nk), in_specs=in_specs, out_specs=out_specs, out_shape=out_shape,
        scratch_shapes=[pltpu.VMEM((tm, tn), F32)],
        compiler_params=_cparams(("parallel", "arbitrary" if side is not None else "parallel", "arbitrary")))(*args)
    return res[0] if epi is None and side is None else res


def _rowwise(name, fn, *, nrows, tr, ncb=1, rows=(), fixed=(), vecs=(), tabs=(), outs=(), reds=(), tab_blocks=1):
    in_specs, args = [], []
    for arr, w, c0 in rows:
        in_specs.append(pl.BlockSpec((tr, w), lambda g, i, c0=c0: (i, c0 + g)))
        args.append(arr)
    for arr, w, c0 in fixed:
        in_specs.append(pl.BlockSpec((tr, w), lambda g, i, c0=c0: (i, c0)))
        args.append(arr)
    for arr, w, c0 in vecs:
        in_specs.append(pl.BlockSpec((1, w), lambda g, i, c0=c0: (0, c0 + g)))
        args.append(arr)
    for arr, w, c0 in tabs:
        in_specs.append(pl.BlockSpec((tr, w), lambda g, i, c0=c0: (i % tab_blocks, c0)))
        args.append(arr)
    n_in, n_out = len(args), len(outs)
    out_shape, out_specs, aliases = [], [], {}
    for k, o in enumerate(outs):
        c0 = o[3] if len(o) > 3 else 0
        out_shape.append(jax.ShapeDtypeStruct((nrows, o[0]), o[2]))
        out_specs.append(pl.BlockSpec((tr, o[1]), lambda g, i, c0=c0: (i, c0 + g)))
        if len(o) > 4:
            aliases[len(args)] = k
            in_specs.append(pl.BlockSpec(memory_space=pl.ANY))
            args.append(o[4])
    out_shape += [jax.ShapeDtypeStruct((1, wt), F32) for wt, w in reds]
    out_specs += [pl.BlockSpec((1, w), lambda g, i: (0, g)) for wt, w in reds]
    first_out = len(args)

    def body(*refs):
        res = fn(*[r[...] for r in refs[:n_in]])
        for o_ref, val in zip(refs[first_out:first_out + n_out], res[:n_out]):
            o_ref[...] = val.astype(o_ref.dtype)
        i = pl.program_id(1)
        for d_ref, val in zip(refs[first_out + n_out:], res[n_out:]):
            @pl.when(i == 0)
            def _(d_ref=d_ref, val=val):
                d_ref[...] = val

            @pl.when(i > 0)
            def _(d_ref=d_ref, val=val):
                d_ref[...] += val

    return pl.pallas_call(
        body, name=name, grid=(ncb, nrows // tr), in_specs=in_specs, out_specs=out_specs, out_shape=out_shape,
        input_output_aliases=aliases, compiler_params=_cparams(("parallel", "arbitrary")))(*args)


def _peer(k):
    x, y, c = lax.axis_index("x"), lax.axis_index("y"), lax.axis_index("c")
    px = jnp.where((k >> 2) & 1, 1 - x, x)
    py = jnp.where((k >> 1) & 1, 1 - y, y)
    pc = jnp.where(k & 1, 1 - c, c)
    return (px, py, pc), 4 * px + 2 * py + pc


def _my_index():
    return 4 * lax.axis_index("x") + 2 * lax.axis_index("y") + lax.axis_index("c")


def _exchange(name, ins, out_shapes, items):
    n_in, n_out, n_it = len(ins), len(out_shapes), len(items)

    def body(*refs):
        x, o = refs[:n_in], refs[n_in:n_in + n_out]
        send_sems, recv_sems, local_sems = refs[n_in + n_out:]
        me = _my_index()
        local, sends = [], []
        for t, (ii, io, src, dst) in enumerate(items):
            cp = pltpu.make_async_copy(src(x[ii], me), dst(o[io], me), local_sems.at[t])
            cp.start()
            local.append(cp)
        for k in range(1, N_DEV):
            dev, idx = _peer(k)
            for t, (ii, io, src, dst) in enumerate(items):
                s = (k - 1) * n_it + t
                cp = pltpu.make_async_remote_copy(
                    src_ref=src(x[ii], idx), dst_ref=dst(o[io], me), send_sem=send_sems.at[s],
                    recv_sem=recv_sems.at[s], device_id=dev, device_id_type=MESH_ID)
                cp.start()
                sends.append(cp)
        for k in range(1, N_DEV):
            dev, idx = _peer(k)
            for t, (ii, io, src, dst) in enumerate(items):
                s = (k - 1) * n_it + t
                pltpu.make_async_remote_copy(
                    src_ref=src(x[ii], idx), dst_ref=dst(o[io], idx), send_sem=send_sems.at[s],
                    recv_sem=recv_sems.at[s], device_id=dev, device_id_type=MESH_ID).wait_recv()
        for cp in sends:
            cp.wait_send()
        for cp in local:
            cp.wait()

    nsem = (N_DEV - 1) * n_it
    anyspec = pl.BlockSpec(memory_space=pl.ANY)
    return pl.pallas_call(
        body, name=name, out_shape=list(out_shapes), in_specs=[anyspec] * n_in, out_specs=[anyspec] * n_out,
        scratch_shapes=[pltpu.SemaphoreType.DMA((nsem,)), pltpu.SemaphoreType.DMA((nsem,)),
                        pltpu.SemaphoreType.DMA((n_it,))],
        compiler_params=pltpu.CompilerParams(has_side_effects=True))(*ins)


def _split_copies(x, land, send_sems, recv_sems, items, receive):
    me = _my_index()
    remote, n_it = [], len(items)
    for k in range(1, N_DEV):
        dev, idx = _peer(k)
        for t, (ii, io, src, dst) in enumerate(items):
            s = (k - 1) * n_it + t
            remote.append(pltpu.make_async_remote_copy(
                src_ref=src(x[ii], idx), dst_ref=dst(land[io], idx if receive else me), send_sem=send_sems.at[s],
                recv_sem=recv_sems.at[s], device_id=dev, device_id_type=MESH_ID))
    local = [pltpu.make_async_copy(src(x[ii], me), dst(land[io], me), send_sems.at[(N_DEV - 1) * n_it + t])
             for t, (ii, io, src, dst) in enumerate(items)]
    return remote, local


def _exchange_start(name, ins, out_shapes, items, dep=None):
    n_in, n_out, n_it = len(ins), len(out_shapes), len(items)

    def body(*refs):
        x, land = refs[:n_in], refs[n_in:n_in + n_out]
        first_out = n_in + n_out + (dep is not None)
        send_sems, recv_sems, token = refs[first_out], refs[first_out + 1], refs[-1]
        remote, local = _split_copies(x, land, send_sems, recv_sems, items, False)
        for cp in remote + local:
            cp.start()
        token[...] = jnp.zeros_like(token)

    hbm = pl.BlockSpec(memory_space=pltpu.HBM)
    sem = pl.BlockSpec(memory_space=pltpu.SEMAPHORE)
    arrs = [pltpu.with_memory_space_constraint(a, pltpu.HBM)
            for a in list(ins) + [lax.empty(s.shape, s.dtype) for s in out_shapes]]
    res = pl.pallas_call(
        body, name=name,
        out_shape=(pltpu.SemaphoreType.DMA((N_DEV * n_it,)), pltpu.SemaphoreType.DMA(((N_DEV - 1) * n_it,)),
                   *[pltpu.HBM(a.shape, a.dtype) for a in arrs], jax.ShapeDtypeStruct((8, LANE), F32)),
        in_specs=[hbm] * (n_in + n_out) + ([] if dep is None else [pl.BlockSpec(memory_space=pl.ANY)]),
        out_specs=(sem, sem, *[hbm] * (n_in + n_out), pl.BlockSpec(memory_space=pltpu.VMEM)),
        input_output_aliases={i: 2 + i for i in range(n_in + n_out)},
        compiler_params=pltpu.CompilerParams(has_side_effects=pltpu.SideEffectType.DATAFLOW_SIDE_EFFECTING))(
            *arrs, *([] if dep is None else [dep]))
    return res[:2], res[2:2 + n_in], res[2 + n_in:2 + n_in + n_out], res[-1]


def _exchange_wait(name, sems, ins, landing, items, after):
    n_in, n_out = len(ins), len(landing)

    def body(*refs):
        x, land = refs[:n_in], refs[n_in:n_in + n_out]
        send_sems, recv_sems = refs[n_in + n_out], refs[n_in + n_out + 1]
        remote, local = _split_copies(x, land, send_sems, recv_sems, items, True)
        for cp in remote:
            cp.wait_send()
            cp.wait_recv()
        for cp in local:
            cp.wait()

    hbm = pl.BlockSpec(memory_space=pltpu.HBM)
    sem = pl.BlockSpec(memory_space=pltpu.SEMAPHORE)
    arrs = list(ins) + list(landing)
    res = pl.pallas_call(
        body, name=name, out_shape=tuple(pltpu.HBM(a.shape, a.dtype) for a in arrs),
        in_specs=[hbm] * (n_in + n_out) + [sem, sem, pl.BlockSpec(memory_space=pl.ANY)],
        out_specs=tuple([hbm] * (n_in + n_out)), input_output_aliases={i: i for i in range(n_in + n_out)},
        compiler_params=pltpu.CompilerParams(has_side_effects=pltpu.SideEffectType.DATAFLOW_SIDE_EFFECTING))(
            *arrs, *sems, after)
    return res[n_in:]


def _whole(ref, p):
    return ref


def _entry(ref, p):
    return ref.at[p]


def _gather_plan(a, b, kind):
    if kind == "col" and b % LANE == 0:
        return (a, N_DEV * b), (lambda ref, p: ref.at[:, pl.ds(pl.multiple_of(p * b, b), b)]), "col"
    return (N_DEV, a, b), _entry, ("row" if kind == "row" else "stack")


def _adamw_nat(name, parts, w, m, v):
    depth, b, c = w.shape
    assert len(parts) == depth
    tb = _pick(b, (128, 64, 32, 16, 8))
    if tb == b and b > 256:
        tb = 256
    spec = pl.BlockSpec((1, tb, c), lambda i, j: (i, j, 0))

    def body(*refs):
        p_refs = refs[:depth]
        w_ref, m_ref, v_ref, g_ref, d_ref, nm_ref, nv_ref = refs[depth:]
        for layer, p_ref in enumerate(p_refs):
            @pl.when(pl.program_id(0) == layer)
            def _(p_ref=p_ref):
                g = p_ref[0].astype(F32)
                for j in range(1, N_DEV):
                    g = g + p_ref[j].astype(F32)
                nm = ADAM_B1 * m_ref[0] + (1.0 - ADAM_B1) * g
                nv = ADAM_B2 * v_ref[0] + (1.0 - ADAM_B2) * jnp.square(g)
                m_hat = nm / (1.0 - ADAM_B1 ** ADAM_STEP)
                v_hat = nv / (1.0 - ADAM_B2 ** ADAM_STEP)
                g_ref[0] = g
                d_ref[0] = -ADAM_LR * (m_hat / (jnp.sqrt(v_hat) + ADAM_EPS) + ADAM_WD * w_ref[0])
                nm_ref[0] = nm
                nv_ref[0] = nv

    sds = jax.ShapeDtypeStruct((depth, b, c), F32)
    return pl.pallas_call(
        body, name=name, grid=(depth, pl.cdiv(b, tb)),
        in_specs=[pl.BlockSpec((N_DEV, tb, c), lambda i, j: (0, j, 0))] * depth + [spec, spec, spec],
        out_specs=[spec] * 4, out_shape=[sds] * 4, compiler_params=_cparams(("parallel", "parallel")))(*parts, w, m, v)


def _adamw(name, parts, w, m, v):
    rows = w.shape[0]
    tr = _pick(rows, (256, 128, 64, 32, 16, 8))
    spec = pl.BlockSpec((tr, PACK_W), lambda i: (i, 0))

    def body(p_ref, w_ref, m_ref, v_ref, g_ref, d_ref, nm_ref, nv_ref):
        g = p_ref[0]
        for j in range(1, N_DEV):
            g = g + p_ref[j]
        nm = ADAM_B1 * m_ref[...] + (1.0 - ADAM_B1) * g
        nv = ADAM_B2 * v_ref[...] + (1.0 - ADAM_B2) * jnp.square(g)
        m_hat = nm / (1.0 - ADAM_B1 ** ADAM_STEP)
        v_hat = nv / (1.0 - ADAM_B2 ** ADAM_STEP)
        g_ref[...] = g
        d_ref[...] = -ADAM_LR * (m_hat / (jnp.sqrt(v_hat) + ADAM_EPS) + ADAM_WD * w_ref[...])
        nm_ref[...] = nm
        nv_ref[...] = nv

    sds = jax.ShapeDtypeStruct((rows, PACK_W), F32)
    return pl.pallas_call(
        body, name=name, grid=(rows // tr,),
        in_specs=[pl.BlockSpec((N_DEV, tr, PACK_W), lambda i: (0, i, 0)), spec, spec, spec],
        out_specs=[spec] * 4, out_shape=[sds] * 4, compiler_params=_cparams(("parallel",)))(parts, w, m, v)


def _pack(arrs, dtype, row_mult=16):
    flat = jnp.concatenate([a.reshape(-1).astype(dtype) for a in arrs])
    unit = row_mult * PACK_W
    total = -(-flat.shape[0] // unit) * unit
    flat = jnp.pad(flat, (0, total - flat.shape[0]))
    return flat.reshape(-1, PACK_W)


def _pack_lead(arrs, dtype, row_mult):
    flat = jnp.concatenate([a.reshape(N_DEV, -1).astype(dtype) for a in arrs], axis=1)
    unit = row_mult * PACK_W
    total = -(-flat.shape[1] // unit) * unit
    flat = jnp.pad(flat, ((0, 0), (0, total - flat.shape[1])))
    return flat.reshape(N_DEV, -1, PACK_W)


def _unpack(buf, shapes, lead=()):
    flat = buf.reshape(lead + (-1,))
    out, off = [], 0
    for s in shapes:
        n = int(np.prod(s))
        out.append(flat[..., off:off + n].reshape(lead + tuple(s)))
        off += n
    return out


def _unshard(g, kind):
    if kind == "col":
        g = jnp.moveaxis(g, 0, -2)
        return g.reshape(g.shape[:-2] + (g.shape[-2] * g.shape[-1],))
    g = jnp.moveaxis(g, 0, 1)
    return g.reshape((g.shape[0], g.shape[1] * g.shape[2]) + g.shape[3:])


def _shard(full, kind):
    if kind == "col":
        s = full.reshape(full.shape[:-1] + (N_DEV, full.shape[-1] // N_DEV))
        return jnp.moveaxis(s, -2, 0)
    s = full.reshape((full.shape[0], N_DEV, full.shape[1] // N_DEV) + full.shape[2:])
    return jnp.moveaxis(s, 1, 0)


class _Geo:
    def __init__(self, bsz, seq):
        self.bsz, self.seq = bsz, seq
        self.pad = (-N_META) % SSM_CHUNK
        self.lp = self.pad + N_META + seq
        self.t0 = self.pad + N_META
        self.nq = 1 + seq // ATT_BLK
        assert self.t0 == LANE and seq % ATT_BLK == 0 and self.lp % SSM_CHUNK == 0
        self.nrows = bsz * self.lp
        self.nc = self.lp // SSM_CHUNK
        self.nh = SSM_D_INNER // SSM_HEAD_DIM
        self.gn = SSM_GROUPS * SSM_STATE
        self.cd = SSM_D_INNER + 2 * self.gn
        self.hq = MLA_HEADS * LANE
        order = (("z", SSM_D_INNER), ("g_ssm", D_MODEL), ("g_mla", D_MODEL), ("xs", SSM_D_INNER), ("bm", self.gn),
                 ("cm", self.gn), ("c_q", MLA_Q_LORA), ("c_kv", MLA_KV_LORA), ("dt", LANE), ("k_rope", LANE))
        self.col, off = {}, 0
        for nm, w in order:
            assert off % w == 0, (nm, off, w)
            self.col[nm] = (off, w)
            off += w
        self.pw = off
        assert self.nh <= LANE and MLA_ROPE == 64 and MLA_NOPE == LANE and MLA_V == LANE
        self.tr = _pick(self.lp, (1088, 768, 544, 512, 384, 272, 256, 128))
        self.tr_wide = _pick(self.lp, (544, 384, 272, 256, 128))

    def cb(self, nm):
        off, w = self.col[nm]
        return off // w

    def w_in_runs(self, shard_w):
        nh, half = self.nh, MLA_ROPE // 2
        src, pieces = 0, []
        for nm, n in (("z", SSM_D_INNER), ("xs", SSM_D_INNER), ("bm", self.gn), ("cm", self.gn), ("dt", nh),
                      ("c_q", MLA_Q_LORA), ("c_kv", MLA_KV_LORA), ("k_rope", MLA_ROPE), ("g_ssm", D_MODEL),
                      ("g_mla", D_MODEL)):
            dst = self.col[nm][0]
            if nm == "k_rope":
                pieces += [(src, half, dst), (src + half, half, dst + 2 * half)]
            else:
                pieces.append((src, n, dst))
            src += n
        assert src == shard_w * N_DEV
        runs = []
        for a, n, dst in pieces:
            for j in range(N_DEV):
                lo, hi = max(a, j * shard_w), min(a + n, (j + 1) * shard_w)
                if lo < hi:
                    runs.append((j, lo - j * shard_w, hi - lo, dst + lo - a))
        return runs


def _slot(a):
    h = MLA_ROPE // 2
    z = jnp.zeros(a.shape[:-1] + (h,), a.dtype)
    return jnp.concatenate([a[..., :h], z, a[..., h:], z], axis=-1)


def _unslot(a):
    h = MLA_ROPE // 2
    return jnp.concatenate([a[..., :h], a[..., 2 * h:3 * h]], axis=-1)


def _prep_layer(geo, wl):
    nh = geo.nh
    p = {}
    if "w_uq" in wl:
        uq = wl["w_uq"].reshape(MLA_Q_LORA, MLA_HEADS, MLA_NOPE + MLA_ROPE)
        p["w_qn"] = uq[..., :MLA_NOPE].reshape(MLA_Q_LORA, geo.hq)
        p["w_qp"] = _slot(uq[..., MLA_NOPE:]).reshape(MLA_Q_LORA, geo.hq)
    if "w_ukv" in wl:
        ukv = wl["w_ukv"].reshape(MLA_KV_LORA, MLA_HEADS, MLA_NOPE + MLA_V)
        p["w_k"] = ukv[..., :MLA_NOPE].reshape(MLA_KV_LORA, geo.hq)
        p["w_v"] = ukv[..., MLA_NOPE:].reshape(MLA_KV_LORA, geo.hq)
    for nm in ("w_in_pt", "conv_w", "w_branch_ssm", "w_branch_mla", "w_out", "w_mlp_up", "w_mlp_down"):
        if nm in wl:
            p[nm] = wl[nm]
    for nm in ("norm_mix_w", "conv_b", "ssm_norm_w", "q_norm_w", "kv_norm_w", "norm_mlp_w"):
        if nm in wl:
            p[nm] = wl[nm].reshape(1, -1)
    if "dt_bias" in wl:
        p["dt_bias"] = jnp.pad(wl["dt_bias"], (0, LANE - nh)).reshape(1, LANE)
        p["a_log"] = jnp.pad(wl["a_log"], (0, LANE - nh)).reshape(1, LANE)
        p["d_skip_full"] = jnp.repeat(wl["d_skip"], SSM_HEAD_DIM).reshape(1, SSM_D_INNER)
    return p


def _unprep_grads(geo, g):
    nh = geo.nh
    out = {}
    if "w_qn" in g:
        qn = g["w_qn"].reshape(MLA_Q_LORA, MLA_HEADS, MLA_NOPE)
        qp = _unslot(g["w_qp"].reshape(MLA_Q_LORA, MLA_HEADS, LANE))
        out["w_uq"] = jnp.concatenate([qn, qp], axis=-1).reshape(MLA_Q_LORA, -1)
    if "w_k" in g:
        wk = g["w_k"].reshape(MLA_KV_LORA, MLA_HEADS, MLA_NOPE)
        wv = g["w_v"].reshape(MLA_KV_LORA, MLA_HEADS, MLA_V)
        out["w_ukv"] = jnp.concatenate([wk, wv], axis=-1).reshape(MLA_KV_LORA, -1)
    for nm in ("w_in_pt", "w_branch_ssm", "w_branch_mla", "w_out", "w_mlp_up", "w_mlp_down", "conv_w"):
        if nm in g:
            out[nm] = g[nm]
    for nm in ("norm_mix_w", "conv_b", "ssm_norm_w", "q_norm_w", "kv_norm_w", "norm_mlp_w"):
        if nm in g:
            out[nm] = g[nm].reshape(-1)
    if "dt_bias" in g:
        out["dt_bias"] = g["dt_bias"].reshape(-1)[:nh]
        out["a_log"] = g["a_log"].reshape(-1)[:nh]
        out["d_skip"] = g["d_skip_full"].reshape(nh, SSM_HEAD_DIM).sum(-1)
    return out


def _tables(geo):
    pos = jnp.arange(geo.lp, dtype=F32) - geo.pad
    inv = ROPE_THETA ** (-jnp.arange(0, MLA_ROPE, 2, dtype=F32) / MLA_ROPE)
    ang = pos[:, None] * inv[None, :]
    cos, sin = jnp.cos(ang), jnp.sin(ang)
    z = jnp.zeros_like(cos)
    rows = jnp.arange(geo.lp)[:, None]
    return {"cos": jnp.concatenate([cos, z, cos, z], axis=-1), "sin": jnp.concatenate([-sin, z, sin, z], axis=-1),
            "valid": (rows >= geo.pad).astype(F32), "token": (rows >= geo.pad + N_META).astype(F32)}


def _w_in_assemble(geo, gathered):
    _, sw, d = gathered.shape
    runs = geo.w_in_runs(sw)
    tl = _pick(d, (256, 128))

    def body(x_ref, o_ref):
        o_ref[...] = jnp.zeros_like(o_ref)
        for j, s0, n, d0 in runs:
            o_ref[d0:d0 + n, :] = x_ref[j, s0:s0 + n, :]

    return pl.pallas_call(
        body, name="w_in_assemble", grid=(d // tl,), in_specs=[pl.BlockSpec((N_DEV, sw, tl), lambda i: (0, 0, i))],
        out_specs=pl.BlockSpec((geo.pw, tl), lambda i: (0, i)),
        out_shape=jax.ShapeDtypeStruct((geo.pw, d), gathered.dtype), compiler_params=_cparams(("parallel",)))(gathered)


def _w_in_split(geo, g_padded, sw):
    d = g_padded.shape[1]
    runs = geo.w_in_runs(sw)
    tl = _pick(d, (256, 128))

    def body(x_ref, o_ref):
        for j, s0, n, d0 in runs:
            o_ref[j, s0:s0 + n, :] = x_ref[d0:d0 + n, :]

    return pl.pallas_call(
        body, name="w_in_split", grid=(d // tl,), in_specs=[pl.BlockSpec((geo.pw, tl), lambda i: (0, i))],
        out_specs=pl.BlockSpec((N_DEV, sw, tl), lambda i: (0, 0, i)),
        out_shape=jax.ShapeDtypeStruct((N_DEV, sw, d), g_padded.dtype),
        compiler_params=_cparams(("parallel",)))(g_padded)


def _conv_cols(geo, cbw):
    x0 = geo.col["xs"][0]
    assert geo.col["bm"][0] == x0 + SSM_D_INNER and geo.col["cm"][0] == geo.col["bm"][0] + geo.gn and x0 % cbw == 0
    return lambda j: x0 // cbw + j


def _conv_taps(x):
    return [pltpu.roll(x, SSM_CONV - 1 - k, axis=0) for k in range(SSM_CONV - 1)] + [x]


def _conv_pre(x, w_ref, b_ref, taps=None):
    taps = _conv_taps(x) if taps is None else taps
    acc = b_ref[...]
    for k in range(SSM_CONV):
        acc = acc + taps[k] * w_ref[k:k + 1, :]
    return acc


def _conv_fwd(geo, proj, conv_w, conv_b):
    cbw = 256
    colmap = _conv_cols(geo, cbw)
    lp, pad = geo.lp, geo.pad

    def body(x_ref, w_ref, b_ref, o_ref):
        valid = (lax.broadcasted_iota(jnp.int32, (lp, 1), 0) >= pad).astype(F32)
        o_ref[...] = (_silu(_conv_pre(x_ref[...].astype(F32), w_ref, b_ref)) * valid).astype(o_ref.dtype)

    return pl.pallas_call(
        body, name="conv_fwd", grid=(geo.bsz, geo.cd // cbw),
        in_specs=[pl.BlockSpec((lp, cbw), lambda b, j: (b, colmap(j))),
                  pl.BlockSpec((SSM_CONV, cbw), lambda b, j: (0, j)), pl.BlockSpec((1, cbw), lambda b, j: (0, j))],
        out_specs=pl.BlockSpec((lp, cbw), lambda b, j: (b, j)),
        out_shape=jax.ShapeDtypeStruct((geo.nrows, geo.cd), MXU_DTYPE),
        compiler_params=_cparams(("parallel", "parallel")))(proj, conv_w, conv_b)


def _conv_bwd(geo, proj, conv_w, conv_b, dxc, dproj):
    cbw = 256
    colmap = _conv_cols(geo, cbw)
    lp, pad = geo.lp, geo.pad

    def body(x_ref, w_ref, b_ref, dy_ref, _, dx_ref, gw_ref, gb_ref):
        b = pl.program_id(1)
        valid = (lax.broadcasted_iota(jnp.int32, (lp, 1), 0) >= pad).astype(F32)
        taps = _conv_taps(x_ref[...].astype(F32))
        pre = _conv_pre(None, w_ref, b_ref, taps)
        sig = _sigmoid(pre)
        dpre = dy_ref[...] * (sig * (1.0 + pre * (1.0 - sig))) * valid
        dx = dpre * w_ref[SSM_CONV - 1:SSM_CONV, :]
        for k in range(SSM_CONV - 1):
            dx = dx + pltpu.roll(dpre, lp - (SSM_CONV - 1 - k), axis=0) * w_ref[k:k + 1, :]
        gws = [jnp.sum(dpre * taps[k], axis=0, keepdims=True) for k in range(SSM_CONV)]
        dx_ref[...] = (dx * valid).astype(dx_ref.dtype)

        @pl.when(b == 0)
        def _():
            gw_ref[...] = jnp.zeros_like(gw_ref)
            gb_ref[...] = jnp.zeros_like(gb_ref)

        for k in range(SSM_CONV):
            gw_ref[k:k + 1, :] += gws[k]
        gb_ref[...] += jnp.sum(dpre, axis=0, keepdims=True)

    return pl.pallas_call(
        body, name="conv_bwd", grid=(geo.cd // cbw, geo.bsz),
        in_specs=[pl.BlockSpec((lp, cbw), lambda j, b: (b, colmap(j))),
                  pl.BlockSpec((SSM_CONV, cbw), lambda j, b: (0, j)), pl.BlockSpec((1, cbw), lambda j, b: (0, j)),
                  pl.BlockSpec((lp, cbw), lambda j, b: (b, j)), pl.BlockSpec(memory_space=pl.ANY)],
        out_specs=[pl.BlockSpec((lp, cbw), lambda j, b: (b, colmap(j))),
                   pl.BlockSpec((SSM_CONV, cbw), lambda j, b: (0, j)), pl.BlockSpec((1, cbw), lambda j, b: (0, j))],
        out_shape=[jax.ShapeDtypeStruct(dproj.shape, dproj.dtype),
                   jax.ShapeDtypeStruct((SSM_CONV, geo.cd), F32), jax.ShapeDtypeStruct((1, geo.cd), F32)],
        input_output_aliases={4: 0},
        compiler_params=_cparams(("parallel", "arbitrary")))(proj, conv_w, conv_b, dxc, dproj)


def _tri(q):
    r = lax.broadcasted_iota(jnp.int32, (q, q), 0)
    c = lax.broadcasted_iota(jnp.int32, (q, q), 1)
    return r >= c


def _ssd_pre(dtr, dtb, alog, valid):
    dt = _softplus(dtr + dtb) * valid
    adt = dt * (-jnp.exp(alog))
    a_cs = _dot(_tri(SSM_CHUNK).astype(F32), adt, 1, 0, precision=lax.Precision.HIGHEST)
    return dt, a_cs


def _ssd_specs(geo, rev):
    nc, q = geo.nc, SSM_CHUNK
    ci = (lambda c: nc - 1 - c) if rev else (lambda c: c)
    nxb = SSM_D_INNER // geo.gn
    return [pl.BlockSpec((q, SSM_D_INNER), lambda b, c: (b * nc + ci(c), 0)),
            pl.BlockSpec((q, geo.gn), lambda b, c: (b * nc + ci(c), nxb)),
            pl.BlockSpec((q, geo.gn), lambda b, c: (b * nc + ci(c), nxb + 1)),
            pl.BlockSpec((q, LANE), lambda b, c: (b * nc + ci(c), 0)),
            pl.BlockSpec((1, LANE), lambda b, c: (0, 0)), pl.BlockSpec((1, LANE), lambda b, c: (0, 0))], ci


def _expand_heads(cols, nh):
    per = LANE // SSM_HEAD_DIM
    lane = lax.broadcasted_iota(jnp.int32, (1, LANE), 1)
    blocks = []
    for j in range(nh // per):
        blk = jnp.broadcast_to(cols[:, j * per:j * per + 1], (cols.shape[0], LANE))
        for k in range(1, per):
            blk = jnp.where(lane >= k * SSM_HEAD_DIM, cols[:, j * per + k:j * per + k + 1], blk)
        blocks.append(blk)
    return jnp.concatenate(blocks, axis=1)


def _head_maps(geo):
    e = (jnp.arange(SSM_D_INNER)[None, :] // SSM_HEAD_DIM == jnp.arange(LANE)[:, None]).astype(F32)
    return e, e.T


def _ssd_fwd_g(geo, xc, proj, dt_bias, a_log):
    q, p, n, e = SSM_CHUNK, SSM_HEAD_DIM, SSM_STATE, geo.nh // SSM_GROUPS
    nc, pad, gw = geo.nc, geo.pad, SSM_D_INNER // SSM_GROUPS
    in_specs, _ = _ssd_specs(geo, False)

    def body(xs_ref, b_ref, c_ref, dtr_ref, dtb_ref, alog_ref, y_ref, sp_ref, state, xdt_s, y_s):
        c = pl.program_id(1)

        @pl.when(c == 0)
        def _():
            state[...] = jnp.zeros_like(state)

        sp_ref[...] = state[...]
        inert = (c + 1) * q <= pad

        @pl.when(inert)
        def _():
            y_ref[...] = jnp.zeros_like(y_ref)

        @pl.when(jnp.logical_not(inert))
        def _():
            valid = (c * q + lax.broadcasted_iota(jnp.int32, (q, 1), 0) >= pad).astype(F32)
            dt, a_cs = _ssd_pre(dtr_ref[...], dtb_ref[...], alog_ref[...], valid)
            a_cst = a_cs.T
            dt_x, a_x = _expand_heads(dt, geo.nh), _expand_heads(a_cs, geo.nh)
            tri = _tri(q)
            for g in range(SSM_GROUPS):
                gs = slice(g * gw, (g + 1) * gw)
                bg, cg = b_ref[:, g * n:(g + 1) * n], c_ref[:, g * n:(g + 1) * n]
                a_g = a_x[:, gs]
                a_last = a_g[q - 1:q, :]
                xdt_g = xs_ref[:, gs] * dt_x[:, gs]
                xdt_s[:, gs] = xdt_g
                s_g = state[:, gs]
                y_s[:, gs] = _mxdot(cg, s_g, 1, 0) * jnp.exp(a_g)
                state[:, gs] = s_g * jnp.exp(a_last) + _mxdot(bg, xdt_g * jnp.exp(a_last - a_g), 0, 0)
                cb = _mxdot(cg, bg, 1, 1)
                for hh in range(e):
                    h = g * e + hh
                    hs = slice(h * p, (h + 1) * p)
                    ldec = jnp.exp(jnp.where(tri, a_cs[:, h:h + 1] - a_cst[h:h + 1, :], -jnp.inf))
                    y_s[:, hs] += _mxdot(cb * ldec, xdt_s[:, hs], 1, 0)
            y_ref[...] = y_s[...].astype(y_ref.dtype)

    return pl.pallas_call(
        body, name="ssd_fwd", grid=(geo.bsz, nc), in_specs=in_specs,
        out_specs=[pl.BlockSpec((q, SSM_D_INNER), lambda b, c: (b * nc + c, 0)),
                   pl.BlockSpec((n, SSM_D_INNER), lambda b, c: (b * nc + c, 0))],
        out_shape=[jax.ShapeDtypeStruct((geo.nrows, SSM_D_INNER), MXU_DTYPE),
                   jax.ShapeDtypeStruct((geo.bsz * nc * n, SSM_D_INNER), F32)],
        scratch_shapes=[pltpu.VMEM((n, SSM_D_INNER), F32), pltpu.VMEM((q, SSM_D_INNER), F32),
                        pltpu.VMEM((q, SSM_D_INNER), F32)],
        compiler_params=_cparams(("parallel", "arbitrary")))(xc, xc, xc, proj, dt_bias, a_log)


def _ssd_bwd_g(geo, xc, proj, dt_bias, a_log, s_prev_all, dy, dxs_skip, dproj):
    q, p, n, e = SSM_CHUNK, SSM_HEAD_DIM, SSM_STATE, geo.nh // SSM_GROUPS
    nc, pad, di, gn, gw = geo.nc, geo.pad, SSM_D_INNER, geo.gn, SSM_D_INNER // SSM_GROUPS
    in_specs, ci = _ssd_specs(geo, True)
    row_spec = pl.BlockSpec((q, di), lambda b, c: (b * nc + ci(c), 0))
    e_map, _ = _head_maps(geo)
    in_specs += [pl.BlockSpec((n, di), lambda b, c: (b * nc + ci(c), 0)), row_spec, row_spec,
                 pl.BlockSpec((LANE, di), lambda b, c: (0, 0)), pl.BlockSpec(memory_space=pl.ANY)]

    def body(xs_ref, b_ref, c_ref, dtr_ref, dtb_ref, alog_ref, sp_ref, dy_ref, dsk_ref, e_ref, _,
             dxc_ref, ddt_ref, gdtb_ref, galog_ref, dstate, xdt_s, dxdt_s):
        step = pl.program_id(1)
        first = jnp.logical_and(pl.program_id(0) == 0, step == 0)
        c = nc - 1 - step

        @pl.when(step == 0)
        def _():
            dstate[...] = jnp.zeros_like(dstate)

        @pl.when(first)
        def _():
            gdtb_ref[...] = jnp.zeros_like(gdtb_ref)
            galog_ref[...] = jnp.zeros_like(galog_ref)

        inert = (c + 1) * q <= pad

        @pl.when(inert)
        def _():
            dxc_ref[...] = jnp.zeros_like(dxc_ref)
            ddt_ref[...] = jnp.zeros_like(ddt_ref)

        @pl.when(jnp.logical_not(inert))
        def _():
            valid = (c * q + lax.broadcasted_iota(jnp.int32, (q, 1), 0) >= pad).astype(F32)
            dtr, dtb, alog = dtr_ref[...], dtb_ref[...], alog_ref[...]
            dt, a_cs = _ssd_pre(dtr, dtb, alog, valid)
            a_cst = a_cs.T
            dt_x, a_x = _expand_heads(dt, geo.nh), _expand_heads(a_cs, geo.nh)
            tri = _tri(q)
            lane = lax.broadcasted_iota(jnp.int32, (1, LANE), 1)
            sub = lax.broadcasted_iota(jnp.int32, (LANE, 1), 0)
            d_dt = jnp.zeros((q, LANE), F32)
            d_acs = jnp.zeros((q, LANE), F32)
            d_acst = jnp.zeros((LANE, q), F32)
            d_last = jnp.zeros((1, LANE), F32)
            for g in range(SSM_GROUPS):
                gs = slice(g * gw, (g + 1) * gw)
                bg, cg = b_ref[:, g * n:(g + 1) * n], c_ref[:, g * n:(g + 1) * n]
                seg = lambda v: _mxdot(v, e_ref[:, gs], 1, 1)
                a_g, dt_g, x_g, dy_g = a_x[:, gs], dt_x[:, gs], xs_ref[:, gs], dy_ref[:, gs]
                e_col, e_last, dec = jnp.exp(a_g), jnp.exp(a_g[q - 1:q, :]), jnp.exp(a_g[q - 1:q, :] - a_g)
                xdt_g = x_g * dt_g
                xdt_s[:, gs] = xdt_g
                s_g, ds_g = sp_ref[:, gs], dstate[:, gs]
                cs = _mxdot(cg, s_g, 1, 0)
                d_cs = dy_g * e_col
                d_acs = d_acs + seg(d_cs * cs)
                d_cg = _mxdot(d_cs, s_g, 1, 1)
                dstate[:, gs] = _mxdot(cg, d_cs, 0, 0) + ds_g * e_last
                dl_x = jnp.sum(ds_g * s_g, axis=0, keepdims=True) * e_last
                d_last = d_last + seg(jnp.broadcast_to(dl_x, (8, gw)))[:1]
                gmat = _mxdot(bg, ds_g, 1, 0)
                xd = xdt_g * dec
                d_bg = _mxdot(xd, ds_g, 1, 1)
                d_dec = seg(xd * gmat)
                d_acs = d_acs - d_dec
                d_last = d_last + jnp.sum(d_dec, axis=0, keepdims=True)
                dxdt_s[:, gs] = dec * gmat
                cb = _mxdot(cg, bg, 1, 1)
                d_cb = jnp.zeros((q, q), F32)
                for hh in range(e):
                    h = g * e + hh
                    hs = slice(h * p, (h + 1) * p)
                    ldec = jnp.exp(jnp.where(tri, a_cs[:, h:h + 1] - a_cst[h:h + 1, :], -jnp.inf))
                    dyh = dy_ref[:, hs]
                    d_m = _mxdot(dyh, xdt_s[:, hs], 1, 1)
                    dxdt_s[:, hs] += _mxdot(cb * ldec, dyh, 0, 0)
                    d_cb = d_cb + d_m * ldec
                    d_diff = d_m * cb * ldec
                    d_acs = d_acs + jnp.sum(d_diff, axis=1, keepdims=True) * (lane == h).astype(F32)
                    d_acst = d_acst - (sub == h).astype(F32) * jnp.sum(d_diff, axis=0, keepdims=True)
                d_xdt = dxdt_s[:, gs]
                dxc_ref[:, gs] = d_xdt * dt_g + dsk_ref[:, gs]
                d_dt = d_dt + seg(d_xdt * x_g)
                dxc_ref[:, di + g * n:di + (g + 1) * n] = d_bg + _mxdot(d_cb, cg, 0, 0)
                dxc_ref[:, di + gn + g * n:di + gn + (g + 1) * n] = d_cg + _mxdot(d_cb, bg, 1, 0)
            is_last = (lax.broadcasted_iota(jnp.int32, (q, 1), 0) == q - 1).astype(F32)
            d_acs = d_acs + d_acst.T + is_last * d_last
            d_adt = _dot(_tri(q).astype(F32), d_acs, 0, 0, precision=lax.Precision.HIGHEST)
            a = -jnp.exp(alog)
            d_dt = d_dt + d_adt * a
            d_dtr = d_dt * valid * _sigmoid(dtr + dtb)
            ddt_ref[...] = d_dtr.astype(ddt_ref.dtype)
            gdtb_ref[...] += jnp.sum(d_dtr, axis=0, keepdims=True)
            galog_ref[...] += jnp.sum(d_adt * dt, axis=0, keepdims=True) * a

    vec = pl.BlockSpec((1, LANE), lambda b, c: (0, 0))
    return pl.pallas_call(
        body, name="ssd_bwd", grid=(geo.bsz, nc), in_specs=in_specs,
        out_specs=[pl.BlockSpec((q, geo.cd), lambda b, c: (b * nc + ci(c), 0)),
                   pl.BlockSpec((q, LANE), lambda b, c: (b * nc + ci(c), geo.cb("dt"))), vec, vec],
        out_shape=[jax.ShapeDtypeStruct((geo.nrows, geo.cd), F32), jax.ShapeDtypeStruct(dproj.shape, dproj.dtype),
                   jax.ShapeDtypeStruct((1, LANE), F32), jax.ShapeDtypeStruct((1, LANE), F32)],
        scratch_shapes=[pltpu.VMEM((n, di), F32), pltpu.VMEM((q, di), F32), pltpu.VMEM((q, di), F32)],
        input_output_aliases={10: 1},
        compiler_params=_cparams(("arbitrary", "arbitrary")))(
            xc, xc, xc, proj, dt_bias, a_log, s_prev_all, dy, dxs_skip, e_map, dproj)


BIAS_LANE = MLA_ROPE // 2
KEY_OFF = -1e30
ATT_SCALE = (MLA_NOPE + MLA_ROPE) ** -0.5


def _row_t(col):
    return jnp.broadcast_to(col, (col.shape[0], LANE)).T[:8]


def _att_tile(geo, i):
    return (0, geo.t0) if i == 0 else (geo.t0 + (i - 1) * ATT_BLK, ATT_BLK)


def _attn_fwd4(geo, qn, qp, kn, kp, v):
    t, lp, nq = ATT_BLK, geo.lp, geo.nq

    def body(qn_ref, qp_ref, kn_ref, kp_ref, v_ref, o_ref, lse_ref, k_ref):
        qi = pl.program_id(2)

        @pl.when(qi == 0)
        def _():
            k_ref[:, :LANE] = kn_ref[...]
            k_ref[:, LANE:] = kp_ref[...]

        for i in range(nq):
            @pl.when(qi == i)
            def _(i=i):
                r0, rows = _att_tile(geo, i)
                w = r0 + rows
                q = jnp.concatenate([qn_ref[r0:w, :], qp_ref[r0:w, :]], axis=1)
                s = _mxdot(q, k_ref[0:w, :], 1, 1) * ATT_SCALE
                diag = jnp.where(_tri(rows), s[:, r0:], -jnp.inf)
                s = diag if r0 == 0 else jnp.concatenate([s[:, :r0], diag], axis=1)
                m = jnp.max(s, axis=1, keepdims=True)
                pr = jnp.exp(s - m)
                l = jnp.sum(pr, axis=1, keepdims=True)
                o_ref[r0:w, :] = (_mxdot(pr, v_ref[0:w, :], 1, 0) / l).astype(o_ref.dtype)
                lse_ref[0, 0, 0, :, 0:rows] = _row_t(m + jnp.log(l))

    seq = pl.BlockSpec((lp, LANE), lambda b, h, i: (b, h))
    return pl.pallas_call(
        body, name="attn_fwd", grid=(geo.bsz, MLA_HEADS, nq),
        in_specs=[seq, seq, seq, pl.BlockSpec((lp, LANE), lambda b, h, i: (b, 0)), seq],
        out_specs=[seq, pl.BlockSpec((1, 1, 1, 8, t), lambda b, h, i: (b, h, i, 0, 0))],
        out_shape=[jax.ShapeDtypeStruct((geo.nrows, geo.hq), MXU_DTYPE),
                   jax.ShapeDtypeStruct((geo.bsz, MLA_HEADS, nq, 8, t), F32)],
        scratch_shapes=[pltpu.VMEM((lp, 2 * LANE), MXU_DTYPE)],
        compiler_params=_cparams(("parallel", "parallel", "arbitrary")))(qn, qp, kn, kp, v)


def _attn_bwd4(geo, qn, qp, kn, kp, v, d_o, o, lse):
    t, lp, nq = ATT_BLK, geo.lp, geo.nq

    def body(qn_ref, qp_ref, kn_ref, kp_ref, v_ref, do_ref, o_ref, lse_ref,
             dqn_ref, dqp_ref, dkn_ref, dkp_ref, dv_ref, q_ref, dl_s):
        kj = pl.program_id(2)

        @pl.when(kj == 0)
        def _():
            q_ref[:, :LANE] = qn_ref[...]
            q_ref[:, LANE:] = qp_ref[...]
            dqn_ref[...] = jnp.zeros_like(dqn_ref)
            dqp_ref[...] = jnp.zeros_like(dqp_ref)
            for i in range(nq):
                r0, rows = _att_tile(geo, i)
                dl_s[i, :, 0:rows] = _row_t(jnp.sum(do_ref[r0:r0 + rows, :].astype(F32) * o_ref[r0:r0 + rows, :].astype(F32),
                                                    axis=1, keepdims=True))

        def per_query(ref, first):
            return jnp.concatenate([ref[i][:1, 0:_att_tile(geo, i)[1]] for i in range(first, nq)], axis=1)

        for i in range(nq):
            @pl.when(kj == i)
            def _(i=i):
                k0, kw = _att_tile(geo, i)
                k = jnp.concatenate([kn_ref[k0:k0 + kw, :], kp_ref[k0:k0 + kw, :]], axis=1)
                vv = v_ref[k0:k0 + kw, :]
                q, d_o_blk = q_ref[k0:lp, :], do_ref[k0:lp, :]
                wq = lp - k0
                st = _mxdot(k, q, 1, 1) * ATT_SCALE
                keys = lax.broadcasted_iota(jnp.int32, (kw, kw), 0)
                diag = jnp.where(keys <= lax.broadcasted_iota(jnp.int32, (kw, kw), 1), st[:, :kw], -jnp.inf)
                st = diag if wq == kw else jnp.concatenate([diag, st[:, kw:]], axis=1)
                pt = jnp.exp(st - per_query(lse_ref.at[0, 0], i))
                dst = pt * (_mxdot(vv, d_o_blk, 1, 1) - per_query(dl_s, i)) * ATT_SCALE
                dq = _mxdot(dst, k, 0, 0)
                dqn_ref[k0:lp, :] += dq[:, :LANE]
                dqp_ref[k0:lp, :] += dq[:, LANE:]
                dk = _mxdot(dst, q, 1, 0)
                dkn_ref[k0:k0 + kw, :] = dk[:, :LANE].astype(dkn_ref.dtype)
                dkp_ref[k0:k0 + kw, :] = dk[:, LANE:]
                dv_ref[k0:k0 + kw, :] = _mxdot(pt, d_o_blk, 1, 0).astype(dv_ref.dtype)

    seq = pl.BlockSpec((lp, LANE), lambda b, h, j: (b, h))
    return pl.pallas_call(
        body, name="attn_bwd", grid=(geo.bsz, MLA_HEADS, nq),
        in_specs=[seq, seq, seq, pl.BlockSpec((lp, LANE), lambda b, h, j: (b, 0)), seq, seq, seq,
                  pl.BlockSpec((1, 1, nq, 8, t), lambda b, h, j: (b, h, 0, 0, 0))],
        out_specs=[seq, seq, seq, seq, seq],
        out_shape=[jax.ShapeDtypeStruct((geo.nrows, geo.hq), F32), jax.ShapeDtypeStruct((geo.nrows, geo.hq), F32),
                   jax.ShapeDtypeStruct((geo.nrows, geo.hq), MXU_DTYPE), jax.ShapeDtypeStruct((geo.nrows, geo.hq), F32),
                   jax.ShapeDtypeStruct((geo.nrows, geo.hq), MXU_DTYPE)],
        scratch_shapes=[pltpu.VMEM((lp, 2 * LANE), MXU_DTYPE), pltpu.VMEM((nq, 8, t), F32)],
        compiler_params=_cparams(("parallel", "parallel", "arbitrary")))(qn, qp, kn, kp, v, d_o, o, lse)


def _rope(x, cos, sin):
    return x * cos + pltpu.roll(x, LANE // 2, axis=1) * sin


def _rope_t(dx, cos, sin):
    return dx * cos + pltpu.roll(dx * sin, LANE // 2, axis=1)


def _per_head(f):
    def fn(x, cos, sin):
        return (jnp.concatenate([f(x[:, h * LANE:(h + 1) * LANE], cos, sin) for h in range(MLA_HEADS)], axis=1),)
    return fn


def _layer_fwd(geo, h, w, tab, late=None):
    nr, tr, trw = geo.nrows, geo.tr, geo.tr_wide
    tb = geo.lp // tr
    rw = functools.partial(_rowwise, nrows=nr)
    s = {"h": h}
    (s["u"],) = rw("rms_mix", lambda x, g: (_rms(x, g),), tr=tr, rows=[(h, D_MODEL, 0)],
                   vecs=[(w["norm_mix_w"], D_MODEL, 0)], outs=[(D_MODEL, D_MODEL, MXU_DTYPE)])
    proj, s["proj_dt"] = _mm("mm_in", s["u"], w["w_in_pt"], tb=True, out_dtype=MXU_DTYPE,
                             side=(geo.col["dt"][0], LANE))
    s["proj"] = proj
    xc = s["xc"] = _conv_fwd(geo, proj, w["conv_w"], w["conv_b"])
    s["y_ssd"], s["s_prev"] = _ssd_fwd_g(geo, xc, s["proj_dt"], w["dt_bias"], w["a_log"])
    gw = SSM_D_INNER // SSM_GROUPS

    def gate_norm(y, x, z, dsk, nw):
        return (_rms((y + x * dsk) * _silu(z.astype(F32)), nw),)

    (s["y_ssm"],) = rw("ssm_gate_norm", gate_norm, tr=tr, ncb=SSM_GROUPS,
                       rows=[(s["y_ssd"], gw, 0), (xc, gw, 0), (proj, gw, geo.col["z"][0] // gw)],
                       vecs=[(w["d_skip_full"], gw, 0), (w["ssm_norm_w"], gw, 0)], outs=[(SSM_D_INNER, gw, MXU_DTYPE)])
    if late is not None:
        w = {**w, **late(s["y_ssm"])}
    (s["cq_n"],) = rw("rms_q", lambda x, g: (_rms(x, g),), tr=tr, rows=[(proj, MLA_Q_LORA, geo.cb("c_q"))],
                      vecs=[(w["q_norm_w"], MLA_Q_LORA, 0)], outs=[(MLA_Q_LORA, MLA_Q_LORA, MXU_DTYPE)])
    (s["ckv_n"],) = rw("rms_kv", lambda x, g: (_rms(x, g),), tr=tr, rows=[(proj, MLA_KV_LORA, geo.cb("c_kv"))],
                       vecs=[(w["kv_norm_w"], MLA_KV_LORA, 0)], outs=[(MLA_KV_LORA, MLA_KV_LORA, MXU_DTYPE)])
    s["qn"] = _mm("mm_qn", s["cq_n"], w["w_qn"], out_dtype=MXU_DTYPE)
    qp_raw = _mm("mm_qp", s["cq_n"], w["w_qp"])
    s["kn"] = _mm("mm_kn", s["ckv_n"], w["w_k"], out_dtype=MXU_DTYPE)
    s["v"] = _mm("mm_v", s["ckv_n"], w["w_v"], out_dtype=MXU_DTYPE)
    bias_lane = lambda: lax.broadcasted_iota(jnp.int32, (1, LANE), 1) == BIAS_LANE
    rope_tabs = [(tab["cos"], LANE, 0), (tab["sin"], LANE, 0)]
    (s["qp"],) = rw("rope_q", _per_head(lambda xp, c, sn: jnp.where(bias_lane(), 1.0, _rope(xp, c, sn))), tr=tr,
                    rows=[(qp_raw, geo.hq, 0)], tabs=rope_tabs, outs=[(geo.hq, geo.hq, MXU_DTYPE)], tab_blocks=tb)
    (s["kp"],) = rw("rope_k", lambda xp, c, sn, valid: (jnp.where(bias_lane(), KEY_OFF * (1.0 - valid),
                                                                 _rope(xp.astype(F32), c, sn)),),
                    tr=tr, rows=[(proj, LANE, geo.cb("k_rope"))], tabs=rope_tabs + [(tab["valid"], 1, 0)],
                    outs=[(LANE, LANE, MXU_DTYPE)], tab_blocks=tb)
    s["o"], s["lse"] = _attn_fwd4(geo, s["qn"], s["qp"], s["kn"], s["kp"], s["v"])
    s["ys_p"] = _mm("mm_bs", s["y_ssm"], w["w_branch_ssm"], out_dtype=MXU_DTYPE)
    s["ym_p"] = _mm("mm_bm", s["o"], w["w_branch_mla"], out_dtype=MXU_DTYPE)

    def gate(gs, gm, ys, ym):
        return (_sigmoid(gs.astype(F32)) * ys + _sigmoid(gm.astype(F32)) * ym,)

    (s["mixed"],) = rw("gate", gate, tr=tr, rows=[(proj, D_MODEL, geo.cb("g_ssm")), (proj, D_MODEL, geo.cb("g_mla")),
                                                  (s["ys_p"], D_MODEL, 0), (s["ym_p"], D_MODEL, 0)],
                       outs=[(D_MODEL, D_MODEL, MXU_DTYPE)])
    s["h2"] = _mm("mm_out", s["mixed"], w["w_out"], add=h)
    (s["vn"],) = rw("rms_mlp", lambda x, g: (_rms(x, g),), tr=tr, rows=[(s["h2"], D_MODEL, 0)],
                    vecs=[(w["norm_mlp_w"], D_MODEL, 0)], outs=[(D_MODEL, D_MODEL, MXU_DTYPE)])
    s["up"], s["act"] = _mm("mm_up", s["vn"], w["w_mlp_up"],
                            epi=(lambda r: (r, jnp.square(jnp.maximum(r, 0.0))), (MXU_DTYPE, MXU_DTYPE)))
    return _mm("mm_down", s["act"], w["w_mlp_down"], add=s["h2"]), s, w


def _layer_bwd(geo, dh3, s, w, tab, mid=None, tail=None, dep=None):
    nr, tr, trw = geo.nrows, geo.tr, geo.tr_wide
    tb = geo.lp // tr
    rw = functools.partial(_rowwise, nrows=nr)
    g = {}
    proj = s["proj"]

    def rms_bwd(x, dy, res, gw):
        _, vjp = jax.vjp(_rms, x.astype(F32), gw)
        dx, dgw = vjp(dy.astype(F32))
        return dx + res, dgw

    def rms_bwd_nores(x, dy, gw):
        _, vjp = jax.vjp(_rms, x.astype(F32), gw)
        return vjp(dy.astype(F32))

    (dup,) = _mm("mm_down_t", dh3, w["w_mlp_down"], tb=True, add=s["up"], dep=dep,
                 epi=(lambda r, up: (r * 2.0 * jnp.maximum(up, 0.0),), (MXU_DTYPE,)))
    g["w_mlp_down"] = _mm("mm_down_g", s["act"], dh3, ta=True, out_dtype=MXU_DTYPE)
    g["w_mlp_up"] = _mm("mm_up_g", s["vn"], dup, ta=True, out_dtype=MXU_DTYPE)
    dvn = _mm("mm_up_t", dup, w["w_mlp_up"], tb=True)
    dh2, g["norm_mlp_w"] = rw("rms_mlp_bwd", rms_bwd, tr=tr,
                              rows=[(s["h2"], D_MODEL, 0), (dvn, D_MODEL, 0), (dh3, D_MODEL, 0)],
                              vecs=[(w["norm_mlp_w"], D_MODEL, 0)], outs=[(D_MODEL, D_MODEL, F32)],
                              reds=[(D_MODEL, D_MODEL)])
    dmixed = _mm("mm_out_t", dh2, w["w_out"], tb=True, out_dtype=MXU_DTYPE)
    g["w_out"] = _mm("mm_out_g", s["mixed"], dh2, ta=True, out_dtype=MXU_DTYPE)

    def gate_bwd(gs, gm, ys, ym, dm):
        f = lambda a, b, c, d: _sigmoid(a) * c + _sigmoid(b) * d
        _, vjp = jax.vjp(f, gs.astype(F32), gm.astype(F32), ys.astype(F32), ym.astype(F32))
        dgs, dgm, dys, dym = vjp(dm.astype(F32))
        return dys, dym, jnp.concatenate([dgs, dgm], axis=1)

    assert geo.col["g_mla"][0] == geo.col["g_ssm"][0] + D_MODEL and geo.col["g_ssm"][0] % (2 * D_MODEL) == 0
    dys_p, dym_p, dproj = rw(
        "gate_bwd", gate_bwd, tr=tr,
        rows=[(proj, D_MODEL, geo.cb("g_ssm")), (proj, D_MODEL, geo.cb("g_mla")), (s["ys_p"], D_MODEL, 0),
              (s["ym_p"], D_MODEL, 0), (dmixed, D_MODEL, 0)],
        outs=[(D_MODEL, D_MODEL, MXU_DTYPE)] * 2 + [(geo.pw, 2 * D_MODEL, MXU_DTYPE, geo.col["g_ssm"][0] // (2 * D_MODEL))])
    g["w_branch_ssm"] = _mm("mm_bs_g", s["y_ssm"], dys_p, ta=True, out_dtype=MXU_DTYPE)
    dy_ssm = _mm("mm_bs_t", dys_p, w["w_branch_ssm"], tb=True, out_dtype=MXU_DTYPE)
    g["w_branch_mla"] = _mm("mm_bm_g", s["o"], dym_p, ta=True, out_dtype=MXU_DTYPE)
    d_o = _mm("mm_bm_t", dym_p, w["w_branch_mla"], tb=True, out_dtype=MXU_DTYPE)
    dqn, dqp, dkn, dkp_h, dv = _attn_bwd4(geo, s["qn"], s["qp"], s["kn"], s["kp"], s["v"], d_o, s["o"], s["lse"])
    rope_tabs = [(tab["cos"], LANE, 0), (tab["sin"], LANE, 0)]
    (dqp_raw,) = rw("rope_q_bwd", _per_head(_rope_t), tr=tr, rows=[(dqp, geo.hq, 0)], tabs=rope_tabs,
                    outs=[(geo.hq, geo.hq, MXU_DTYPE)], tab_blocks=tb)

    def rope_k_bwd(x, c, sn):
        tot = x[:, :LANE]
        for hd in range(1, MLA_HEADS):
            tot = tot + x[:, hd * LANE:(hd + 1) * LANE]
        return (_rope_t(tot, c, sn),)

    (dproj,) = rw("rope_k_bwd", rope_k_bwd, tr=tr, rows=[(dkp_h, geo.hq, 0)], tabs=rope_tabs,
                  outs=[(geo.pw, LANE, MXU_DTYPE, geo.cb("k_rope"), dproj)], tab_blocks=tb)
    g["w_qn"] = _mm("mm_qn_g", s["cq_n"], dqn, ta=True, out_dtype=MXU_DTYPE)
    g["w_qp"] = _mm("mm_qp_g", s["cq_n"], dqp_raw, ta=True, out_dtype=MXU_DTYPE)
    dcq_n = _mm("mm_qp_t", dqp_raw, w["w_qp"], tb=True, add=_mm("mm_qn_t", dqn, w["w_qn"], tb=True))
    g["w_k"] = _mm("mm_kn_g", s["ckv_n"], dkn, ta=True, out_dtype=MXU_DTYPE)
    g["w_v"] = _mm("mm_v_g", s["ckv_n"], dv, ta=True, out_dtype=MXU_DTYPE)
    dckv_n = _mm("mm_v_t", dv, w["w_v"], tb=True, add=_mm("mm_kn_t", dkn, w["w_k"], tb=True))
    dproj, g["q_norm_w"] = rw("rms_q_bwd", rms_bwd_nores, tr=tr,
                              rows=[(proj, MLA_Q_LORA, geo.cb("c_q")), (dcq_n, MLA_Q_LORA, 0)],
                              vecs=[(w["q_norm_w"], MLA_Q_LORA, 0)],
                              outs=[(geo.pw, MLA_Q_LORA, MXU_DTYPE, geo.cb("c_q"), dproj)], reds=[(MLA_Q_LORA, MLA_Q_LORA)])
    dproj, g["kv_norm_w"] = rw("rms_kv_bwd", rms_bwd_nores, tr=tr,
                               rows=[(proj, MLA_KV_LORA, geo.cb("c_kv")), (dckv_n, MLA_KV_LORA, 0)],
                               vecs=[(w["kv_norm_w"], MLA_KV_LORA, 0)],
                               outs=[(geo.pw, MLA_KV_LORA, MXU_DTYPE, geo.cb("c_kv"), dproj)],
                               reds=[(MLA_KV_LORA, MLA_KV_LORA)])
    gw_ = SSM_D_INNER // SSM_GROUPS
    d_skip_full = w["d_skip_full"] if mid is None else w["d_skip_full"] + mid(g)[0, 0]

    def gate_norm_bwd(y, x, z, dy, dsk, nw):
        f = lambda y_, x_, z_, dsk_, nw_: _rms((y_ + x_ * dsk_) * _silu(z_), nw_)
        _, vjp = jax.vjp(f, y.astype(F32), x.astype(F32), z.astype(F32), dsk, nw)
        dy_, dx_, dz_, ddsk, dnw = vjp(dy.astype(F32))
        return dy_, dx_, dz_, ddsk, dnw

    dy_ssd, dxs_skip, dproj, g["d_skip_full"], g["ssm_norm_w"] = rw(
        "ssm_gate_norm_bwd", gate_norm_bwd, tr=tr, ncb=SSM_GROUPS,
        rows=[(s["y_ssd"], gw_, 0), (s["xc"], gw_, 0), (proj, gw_, geo.col["z"][0] // gw_), (dy_ssm, gw_, 0)],
        vecs=[(d_skip_full, gw_, 0), (w["ssm_norm_w"], gw_, 0)],
        outs=[(SSM_D_INNER, gw_, MXU_DTYPE), (SSM_D_INNER, gw_, MXU_DTYPE),
              (geo.pw, gw_, MXU_DTYPE, geo.col["z"][0] // gw_, dproj)],
        reds=[(SSM_D_INNER, gw_), (SSM_D_INNER, gw_)])
    dxc, dproj, g["dt_bias"], g["a_log"] = _ssd_bwd_g(geo, s["xc"], s["proj_dt"], w["dt_bias"], w["a_log"], s["s_prev"],
                                                     dy_ssd, dxs_skip, dproj)
    dproj, g["conv_w"], g["conv_b"] = _conv_bwd(geo, proj, w["conv_w"], w["conv_b"], dxc, dproj)
    g["w_in_pt"] = _mm("mm_in_g", dproj, s["u"], ta=True, out_dtype=MXU_DTYPE)
    du = _mm("mm_in_t", dproj, w["w_in_pt"], dep=None if tail is None else tail(g))
    dh, g["norm_mix_w"] = rw("rms_mix_bwd", rms_bwd, tr=tr,
                             rows=[(s["h"], D_MODEL, 0), (du, D_MODEL, 0), (dh2, D_MODEL, 0)],
                             vecs=[(w["norm_mix_w"], D_MODEL, 0)], outs=[(D_MODEL, D_MODEL, F32)],
                             reds=[(D_MODEL, D_MODEL)])
    return dh, g


def _loss_bwd(geo, h, fw, target, tab):
    tr = geo.tr

    def fn(x, tgt, gw, tok):
        def lossf(x_, gw_):
            err = jnp.square(_rms(x_, gw_) - tgt)
            return 0.5 * jnp.sum(tok * jnp.mean(err, axis=-1, keepdims=True), axis=0, keepdims=True)

        val, vjp = jax.vjp(lossf, x, gw)
        dx, dgw = vjp(jnp.ones((1, 1), F32))
        return dx, jnp.broadcast_to(val, (1, LANE)), dgw

    return _rowwise("loss", fn, nrows=geo.nrows, tr=tr, rows=[(h, D_MODEL, 0), (target, D_MODEL, 0)],
                    vecs=[(fw, D_MODEL, 0)], tabs=[(tab["token"], 1, 0)], outs=[(D_MODEL, D_MODEL, F32)],
                    reds=[(LANE, LANE), (D_MODEL, D_MODEL)], tab_blocks=geo.lp // tr)


def kernel(x, meta_tokens, norm_mix_w, w_in, conv_w, conv_b, dt_bias, a_log, d_skip, ssm_norm_w, q_norm_w, kv_norm_w, w_uq, w_ukv, w_branch_ssm, w_branch_mla, w_out, norm_mlp_w, w_mlp_up, w_mlp_down, final_norm_w, loss_target, m_meta_tokens, m_norm_mix_w, m_w_in, m_conv_w, m_conv_b, m_dt_bias, m_a_log, m_d_skip, m_ssm_norm_w, m_q_norm_w, m_kv_norm_w, m_w_uq, m_w_ukv, m_w_branch_ssm, m_w_branch_mla, m_w_out, m_norm_mlp_w, m_w_mlp_up, m_w_mlp_down, m_final_norm_w, v_meta_tokens, v_norm_mix_w, v_w_in, v_conv_w, v_conv_b, v_dt_bias, v_a_log, v_d_skip, v_ssm_norm_w, v_q_norm_w, v_kv_norm_w, v_w_uq, v_w_ukv, v_w_branch_ssm, v_w_branch_mla, v_w_out, v_norm_mlp_w, v_w_mlp_up, v_w_mlp_down, v_final_norm_w):
    args = dict(locals())
    turn = lambda n, a: jnp.swapaxes(a, 1, 2) if n == "w_in" else a
    wts = {n: turn(n, args[n]) for n in WEIGHTS}
    mom = {n: turn(n, args["m_" + n]) for n in WEIGHTS}
    var = {n: turn(n, args["v_" + n]) for n in WEIGHTS}
    bsz, seq, _ = x.shape
    depth = w_in.shape[0]
    geo = _Geo(bsz, seq)
    tab = _tables(geo)

    big_names = [n for n, _ in BIG]
    sh_names = big_names + [n for n, _ in SHARDED_F32]
    kinds = dict(BIG + SHARDED_F32, w_in="row")
    shard3 = lambda a: a.reshape((1,) + a.shape) if a.ndim == 2 else a
    wire = {n: (MXU_DTYPE if n in big_names else F32) for n in sh_names}
    cast = {n: shard3(wts[n]).astype(wire[n]) for n in sh_names}
    per_layer = [n for n in sh_names if n != "meta_tokens"]
    small_names = ["norm_mix_w", "conv_b", "dt_bias", "a_log", "d_skip", "ssm_norm_w", "q_norm_w", "kv_norm_w",
                   "norm_mlp_w"]

    def gather_items(pairs):
        ins, outs, items, forms = [], [], [], []
        for n, i in pairs:
            a, b = cast[n].shape[1:]
            shape, dst, form = _gather_plan(a, b, kinds[n])
            items.append((len(ins), len(outs), (lambda ref, p, i=i: ref.at[i]), dst))
            ins.append(cast[n])
            outs.append(jax.ShapeDtypeStruct(shape, wire[n]))
            forms.append(form)
        return ins, outs, items, forms

    def whole_weights(pairs, forms, got):
        by_layer = {}
        for (n, i), form, g in zip(pairs, forms, got):
            if n == "w_in":
                n, g = "w_in_pt", _w_in_assemble(geo, g)
            elif form == "row":
                g = g.reshape(g.shape[0] * g.shape[1], g.shape[2])
            elif form == "stack":
                g = _unshard(g, "col")
            by_layer.setdefault(i, {})[n] = g
        return by_layer

    def prep(i, whole, token=None):
        wl = dict(whole)
        wl.update({n: wts[n][i] for n in small_names})
        if token is not None:
            wl["norm_mix_w"] = wl["norm_mix_w"] + token[0, 0]
        return _prep_layer(geo, wl)

    early = ("w_in", "conv_w")
    late_names = [n for n in per_layer if n not in early]
    pairs1 = [(n, i) for i in range(1, depth) for n in per_layer]
    groups = [[(n, 0) for n in early] + [("meta_tokens", 0)], [(n, 0) for n in late_names]] + ([pairs1] if pairs1 else [])
    started = {}

    def gather_start(gi, dep=None):
        ins, outs, items, forms = gather_items(groups[gi])
        sems, thru, landing, token = _exchange_start("gather_w%d_start" % gi, ins, outs, items, dep)
        started[gi] = (groups[gi], forms, sems, thru, landing, items)
        return token

    def gathered(gi, after):
        pairs, forms, sems, thru, landing, items = started[gi]
        return whole_weights(pairs, forms, _exchange_wait("gather_w%d_wait" % gi, sems, thru, landing, items, after))

    def late0(after):
        whole = gathered(1, after)[0]
        if pairs1:
            whole["q_norm_w"] = wts["q_norm_w"][0] + gather_start(2, whole["w_out"])[0, 0]
        return _prep_layer(geo, whole)

    token = gather_start(1, gather_start(0))
    whole0 = gathered(0, token)[0]
    meta_full = whole0.pop("meta_tokens")

    meta = jnp.broadcast_to(meta_full[None], (bsz, N_META, D_MODEL))
    h = jnp.concatenate([jnp.zeros((bsz, geo.pad, D_MODEL), F32), meta, x], axis=1).reshape(geo.nrows, D_MODEL)
    target = jnp.concatenate([jnp.zeros((bsz, geo.pad + N_META, D_MODEL), F32), loss_target], axis=1)
    target = target.reshape(geo.nrows, D_MODEL)
    layers, saved = [], []
    for i in range(depth):
        if i == 0:
            w, late = prep(0, whole0, token), late0
        else:
            if i == 1:
                whole1 = gathered(2, h)
            w, late = prep(i, whole1[i]), None
        h, s, w = _layer_fwd(geo, h, w, tab, late)
        layers.append(w)
        saved.append(s)
    dh, loss_part, g_final = _loss_bwd(geo, h, final_norm_w.reshape(1, -1), target, tab)

    def scatter_items(pairs):
        ins, outs, items = [], [], []
        for n, i in pairs:
            a, b = cast[n].shape[1:]
            arr = g_meta if n == "meta_tokens" else grads[i]["w_in_pt" if n == "w_in" else n]
            if n == "w_in":
                arr, src = _w_in_split(geo, arr, a), _entry
            elif kinds[n] == "row":
                src = lambda ref, p, a=a: ref.at[pl.ds(pl.multiple_of(p * a, a), a)]
            elif b % LANE == 0:
                src = lambda ref, p, b=b: ref.at[:, pl.ds(pl.multiple_of(p * b, b), b)]
            else:
                arr, src = _shard(arr, "col"), _entry
            items.append((len(ins), len(outs), src, _entry))
            ins.append(arr.astype(wire[n]))
            outs.append(jax.ShapeDtypeStruct((N_DEV, a, b), wire[n]))
        return ins, outs, items

    grads = [None] * depth
    landed, pending, res = {}, {}, {}

    def scatter_start(name, pairs):
        ins, outs, items = scatter_items(pairs)
        sems, thru, landing, token = _exchange_start(name + "_start", ins, outs, items)
        pending[name] = (pairs, sems, thru, landing, items)
        return token

    def scatter_wait(name, after):
        pairs, sems, thru, landing, items = pending[name]
        landed.update(zip(pairs, _exchange_wait(name + "_wait", sems, thru, landing, items, after)))

    def adam(n):
        parts = [landed[(n, i)] for i in range(cast[n].shape[0])]
        r = _adamw_nat("adamw_" + n, parts, shard3(wts[n]), shard3(mom[n]), shard3(var[n]))
        res[n] = [a.reshape(wts[n].shape) for a in r]

    def mid0(g):
        grads[0] = _unprep_grads(geo, g)
        return scatter_start("scatter_gb0", [(n, 0) for n in late_names])

    def tail0(g):
        grads[0] = _unprep_grads(geo, g)
        return scatter_start("scatter_ga0", [(n, 0) for n in early])

    dep = None
    for i in reversed(range(depth)):
        dh, gl = _layer_bwd(geo, dh, saved[i], layers[i], tab, *((mid0, tail0) if i == 0 else (None, None)), dep)
        grads[i] = _unprep_grads(geo, gl)
        if i == 1:
            dep = scatter_start("scatter_g1", pairs1)
    dh = dh.reshape(bsz, geo.lp, D_MODEL)
    grad_x = dh[:, geo.pad + N_META:]
    g_meta = jnp.sum(dh[:, geo.pad:geo.pad + N_META], axis=0)
    if pairs1:
        scatter_wait("scatter_g1", g_meta)
    scatter_wait("scatter_gb0", g_meta)
    for n in late_names:
        adam(n)
    g_small = {n: jnp.stack([grads[i][n] for i in range(depth)]) for n in SMALL if n != "final_norm_w"}
    g_small["final_norm_w"] = g_final.reshape(-1)
    zero = jnp.zeros((1,), F32)
    pk = lambda d, last: _pack([d[n] for n in SMALL] + [last], F32, row_mult=8)
    packed = pk(g_small, loss_part[0, :1])
    ins, outs, items = scatter_items([("meta_tokens", 0)])
    parts, landed[("meta_tokens", 0)] = _exchange(
        "gather_g", [packed] + ins + [res[n][1] for n in late_names],
        [jax.ShapeDtypeStruct((N_DEV,) + packed.shape, F32)] + outs,
        [(0, 0, _whole, _entry)] + [(1, 1, items[0][2], items[0][3])])
    adam("meta_tokens")
    scatter_wait("scatter_ga0", res["meta_tokens"][1])
    for n in early:
        adam(n)
    res_sm = _adamw("adamw_small", parts, pk(wts, zero), pk(mom, zero), pk(var, zero))
    res_sm = [_unpack(r, [wts[n].shape for n in SMALL] + [(1,)]) for r in res_sm]
    loss = res_sm[0][-1][0]

    out = [loss, grad_x]
    for k in range(4):
        named = {n: res[n][k] for n in sh_names}
        named.update(zip(SMALL, res_sm[k]))
        out += [turn(n, named[n]) for n in WEIGHTS]
    return tuple(out)
```

```python
import functools

import numpy as np
import jax
import jax.numpy as jnp
from jax import lax
from jax.experimental import pallas as pl
from jax.experimental.pallas import tpu as pltpu

F32 = jnp.float32
MXU_DTYPE = jnp.bfloat16

D_MODEL = 1024
N_META = 16
EPS = 1e-6
SSM_D_INNER = 2048
SSM_HEAD_DIM = 64
SSM_GROUPS = 4
SSM_STATE = 128
SSM_CONV = 4
SSM_CHUNK = 128
MLA_HEADS = 8
MLA_Q_LORA = 512
MLA_KV_LORA = 256
MLA_NOPE = 128
MLA_ROPE = 64
MLA_V = 128
ROPE_THETA = 10000.0
D_FF = 4096
ADAM_LR = 0.001
ADAM_B1 = 0.9
ADAM_B2 = 0.999
ADAM_EPS = 1e-08
ADAM_WD = 0.01
ADAM_STEP = 10

N_DEV = 8
ATT_BLK = 1024
LANE = 128
PACK_W = 1024
VMEM_LIMIT = 56 * 1024 * 1024
MESH_ID = pl.DeviceIdType.MESH

BIG = (("w_in", "col"), ("w_uq", "col"), ("w_ukv", "col"), ("w_branch_ssm", "row"), ("w_branch_mla", "row"),
       ("w_out", "row"), ("w_mlp_up", "col"), ("w_mlp_down", "row"))
SHARDED_F32 = (("conv_w", "col"), ("meta_tokens", "col"))
SMALL = ("norm_mix_w", "conv_b", "dt_bias", "a_log", "d_skip", "ssm_norm_w", "q_norm_w", "kv_norm_w",
         "norm_mlp_w", "final_norm_w")
WEIGHTS = ("meta_tokens", "norm_mix_w", "w_in", "conv_w", "conv_b", "dt_bias", "a_log", "d_skip", "ssm_norm_w",
           "q_norm_w", "kv_norm_w", "w_uq", "w_ukv", "w_branch_ssm", "w_branch_mla", "w_out", "norm_mlp_w",
           "w_mlp_up", "w_mlp_down", "final_norm_w")


def _cparams(sem=None):
    return pltpu.CompilerParams(dimension_semantics=sem, vmem_limit_bytes=VMEM_LIMIT)


def _pick(n, cands):
    for c in cands:
        if n % c == 0:
            return c
    return n


def _sigmoid(x):
    return 1.0 / (1.0 + jnp.exp(-x))


def _silu(x):
    return x * _sigmoid(x)


def _softplus(x):
    t = jnp.exp(-jnp.abs(x))
    return jnp.maximum(x, 0.0) + jnp.where(t < 0.01, t * (1.0 - t * (0.5 - t * (1.0 / 3.0))), jnp.log(1.0 + t))


def _rms(x, w):
    x = x.astype(F32)
    return x * lax.rsqrt(jnp.mean(x * x, axis=-1, keepdims=True) + EPS) * w


def _dot(a, b, ca, cb, precision=None):
    return lax.dot_general(a, b, (((ca,), (cb,)), ((), ())), preferred_element_type=F32, precision=precision)


def _mxdot(a, b, ca, cb):
    return _dot(a.astype(MXU_DTYPE), b.astype(MXU_DTYPE), ca, cb)


def _mm(name, a, b, *, ta=False, tb=False, add=None, out_dtype=F32, dep=None, epi=None, side=None):
    (kdim, m) = a.shape if ta else a.shape[::-1]
    (n, k2) = b.shape if tb else b.shape[::-1]
    assert kdim == k2, (name, a.shape, b.shape)
    tm = _pick(m, (1152, 1088, 1024, 768, 544, 512, 384, 256, 128))
    tn = _pick(n, (1024, 512, 384, 256, 128))
    tk = _pick(kdim, (1152, 1088, 1024, 768, 544, 512, 384, 256, 128))
    nk = kdim // tk
    a_spec = pl.BlockSpec((tk, tm), lambda i, j, k: (k, i)) if ta else pl.BlockSpec((tm, tk), lambda i, j, k: (i, k))
    b_spec = pl.BlockSpec((tn, tk), lambda i, j, k: (j, k)) if tb else pl.BlockSpec((tk, tn), lambda i, j, k: (k, j))
    o_spec = pl.BlockSpec((tm, tn), lambda i, j, k: (i, j))
    ca, cb = (0 if ta else 1), (1 if tb else 0)

    out_dtypes = [out_dtype] if epi is None else list(epi[1])
    n_out = len(out_dtypes)
    n_side = 0 if side is None else 1

    def body(*refs):
        a_ref, b_ref = refs[:2]
        o_refs, acc = refs[-1 - n_side - n_out:-1 - n_side], refs[-1]
        k = pl.program_id(2)

        @pl.when(k == 0)
        def _():
            acc[...] = jnp.zeros_like(acc)

        acc[...] += _mxdot(a_ref[...], b_ref[...], ca, cb)

        @pl.when(k == nk - 1)
        def _():
            r = acc[...]
            if epi is not None:
                res = epi[0](r, refs[2][...]) if add is not None else epi[0](r)
            else:
                res = (r + refs[2][...].astype(F32) if add is not None else r,)
            for o_ref, val in zip(o_refs, res):
                o_ref[...] = val.astype(o_ref.dtype)

        if side is not None:
            @pl.when(jnp.logical_and(k == nk - 1, pl.program_id(1) == side[0] // tn))
            def _():
                refs[-2][...] = acc[:, side[0] % tn:side[0] % tn + side[1]]

    in_specs, args = [a_spec, b_spec], [a, b]
    if add is not None:
        in_specs.append(o_spec)
        args.append(add)
    if dep is not None:
        in_specs.append(pl.BlockSpec((8, LANE), lambda i, j, k: (0, 0)))
        args.append(dep)
    out_specs = [o_spec] * n_out
    out_shape = [jax.ShapeDtypeStruct((m, n), dt) for dt in out_dtypes]
    if side is not None:
        assert side[0] % tn + side[1] <= tn
        out_specs.append(pl.BlockSpec((tm, side[1]), lambda i, j, k: (i, 0)))
        out_shape.append(jax.ShapeDtypeStruct((m, side[1]), F32))
    res = pl.pallas_call(
        body, name=name, grid=(m // tm, n // tn, nk), in_specs=in_specs, out_specs=out_specs, out_shape=out_shape,
        scratch_shapes=[pltpu.VMEM((tm, tn), F32)],
        compiler_params=_cparams(("parallel", "arbitrary" if side is not None else "parallel", "arbitrary")))(*args)
    return res[0] if epi is None and side is None else res


def _rowwise(name, fn, *, nrows, tr, ncb=1, rows=(), fixed=(), vecs=(), tabs=(), outs=(), reds=(), tab_blocks=1):
    in_specs, args = [], []
    for arr, w, c0 in rows:
        in_specs.append(pl.BlockSpec((tr, w), lambda g, i, c0=c0: (i, c0 + g)))
        args.append(arr)
    for arr, w, c0 in fixed:
        in_specs.append(pl.BlockSpec((tr, w), lambda g, i, c0=c0: (i, c0)))
        args.append(arr)
    for arr, w, c0 in vecs:
        in_specs.append(pl.BlockSpec((1, w), lambda g, i, c0=c0: (0, c0 + g)))
        args.append(arr)
    for arr, w, c0 in tabs:
        in_specs.append(pl.BlockSpec((tr, w), lambda g, i, c0=c0: (i % tab_blocks, c0)))
        args.append(arr)
    n_in, n_out = len(args), len(outs)
    out_shape, out_specs, aliases = [], [], {}
    for k, o in enumerate(outs):
        c0 = o[3] if len(o) > 3 else 0
        out_shape.append(jax.ShapeDtypeStruct((nrows, o[0]), o[2]))
        out_specs.append(pl.BlockSpec((tr, o[1]), lambda g, i, c0=c0: (i, c0 + g)))
        if len(o) > 4:
            aliases[len(args)] = k
            in_specs.append(pl.BlockSpec(memory_space=pl.ANY))
            args.append(o[4])
    out_shape += [jax.ShapeDtypeStruct((1, wt), F32) for wt, w in reds]
    out_specs += [pl.BlockSpec((1, w), lambda g, i: (0, g)) for wt, w in reds]
    first_out = len(args)

    def body(*refs):
        res = fn(*[r[...] for r in refs[:n_in]])
        for o_ref, val in zip(refs[first_out:first_out + n_out], res[:n_out]):
            o_ref[...] = val.astype(o_ref.dtype)
        i = pl.program_id(1)
        for d_ref, val in zip(refs[first_out + n_out:], res[n_out:]):
            @pl.when(i == 0)
            def _(d_ref=d_ref, val=val):
                d_ref[...] = val

            @pl.when(i > 0)
            def _(d_ref=d_ref, val=val):
                d_ref[...] += val

    return pl.pallas_call(
        body, name=name, grid=(ncb, nrows // tr), in_specs=in_specs, out_specs=out_specs, out_shape=out_shape,
        input_output_aliases=aliases, compiler_params=_cparams(("parallel", "arbitrary")))(*args)


def _peer(k):
    x, y, c = lax.axis_index("x"), lax.axis_index("y"), lax.axis_index("c")
    px = jnp.where((k >> 2) & 1, 1 - x, x)
    py = jnp.where((k >> 1) & 1, 1 - y, y)
    pc = jnp.where(k & 1, 1 - c, c)
    return (px, py, pc), 4 * px + 2 * py + pc


def _my_index():
    return 4 * lax.axis_index("x") + 2 * lax.axis_index("y") + lax.axis_index("c")


def _exchange(name, ins, out_shapes, items):
    n_in, n_out, n_it = len(ins), len(out_shapes), len(items)

    def body(*refs):
        x, o = refs[:n_in], refs[n_in:n_in + n_out]
        send_sems, recv_sems, local_sems = refs[n_in + n_out:]
        me = _my_index()
        local, sends = [], []
        for t, (ii, io, src, dst) in enumerate(items):
            cp = pltpu.make_async_copy(src(x[ii], me), dst(o[io], me), local_sems.at[t])
            cp.start()
            local.append(cp)
        for k in range(1, N_DEV):
            dev, idx = _peer(k)
            for t, (ii, io, src, dst) in enumerate(items):
                s = (k - 1) * n_it + t
                cp = pltpu.make_async_remote_copy(
                    src_ref=src(x[ii], idx), dst_ref=dst(o[io], me), send_sem=send_sems.at[s],
                    recv_sem=recv_sems.at[s], device_id=dev, device_id_type=MESH_ID)
                cp.start()
                sends.append(cp)
        for k in range(1, N_DEV):
            dev, idx = _peer(k)
            for t, (ii, io, src, dst) in enumerate(items):
                s = (k - 1) * n_it + t
                pltpu.make_async_remote_copy(
                    src_ref=src(x[ii], idx), dst_ref=dst(o[io], idx), send_sem=send_sems.at[s],
                    recv_sem=recv_sems.at[s], device_id=dev, device_id_type=MESH_ID).wait_recv()
        for cp in sends:
            cp.wait_send()
        for cp in local:
            cp.wait()

    nsem = (N_DEV - 1) * n_it
    anyspec = pl.BlockSpec(memory_space=pl.ANY)
    return pl.pallas_call(
        body, name=name, out_shape=list(out_shapes), in_specs=[anyspec] * n_in, out_specs=[anyspec] * n_out,
        scratch_shapes=[pltpu.SemaphoreType.DMA((nsem,)), pltpu.SemaphoreType.DMA((nsem,)),
                        pltpu.SemaphoreType.DMA((n_it,))],
        compiler_params=pltpu.CompilerParams(has_side_effects=True))(*ins)


def _split_copies(x, land, send_sems, recv_sems, items, receive):
    me = _my_index()
    remote, n_it = [], len(items)
    for k in range(1, N_DEV):
        dev, idx = _peer(k)
        for t, (ii, io, src, dst) in enumerate(items):
            s = (k - 1) * n_it + t
            remote.append(pltpu.make_async_remote_copy(
                src_ref=src(x[ii], idx), dst_ref=dst(land[io], idx if receive else me), send_sem=send_sems.at[s],
                recv_sem=recv_sems.at[s], device_id=dev, device_id_type=MESH_ID))
    local = [pltpu.make_async_copy(src(x[ii], me), dst(land[io], me), send_sems.at[(N_DEV - 1) * n_it + t])
             for t, (ii, io, src, dst) in enumerate(items)]
    return remote, local


def _exchange_start(name, ins, out_shapes, items, dep=None):
    n_in, n_out, n_it = len(ins), len(out_shapes), len(items)

    def body(*refs):
        x, land = refs[:n_in], refs[n_in:n_in + n_out]
        first_out = n_in + n_out + (dep is not None)
        send_sems, recv_sems, token = refs[first_out], refs[first_out + 1], refs[-1]
        remote, local = _split_copies(x, land, send_sems, recv_sems, items, False)
        for cp in remote + local:
            cp.start()
        token[...] = jnp.zeros_like(token)

    hbm = pl.BlockSpec(memory_space=pltpu.HBM)
    sem = pl.BlockSpec(memory_space=pltpu.SEMAPHORE)
    arrs = [pltpu.with_memory_space_constraint(a, pltpu.HBM)
            for a in list(ins) + [lax.empty(s.shape, s.dtype) for s in out_shapes]]
    res = pl.pallas_call(
        body, name=name,
        out_shape=(pltpu.SemaphoreType.DMA((N_DEV * n_it,)), pltpu.SemaphoreType.DMA(((N_DEV - 1) * n_it,)),
                   *[pltpu.HBM(a.shape, a.dtype) for a in arrs], jax.ShapeDtypeStruct((8, LANE), F32)),
        in_specs=[hbm] * (n_in + n_out) + ([] if dep is None else [pl.BlockSpec(memory_space=pl.ANY)]),
        out_specs=(sem, sem, *[hbm] * (n_in + n_out), pl.BlockSpec(memory_space=pltpu.VMEM)),
        input_output_aliases={i: 2 + i for i in range(n_in + n_out)},
        compiler_params=pltpu.CompilerParams(has_side_effects=pltpu.SideEffectType.DATAFLOW_SIDE_EFFECTING))(
            *arrs, *([] if dep is None else [dep]))
    return res[:2], res[2:2 + n_in], res[2 + n_in:2 + n_in + n_out], res[-1]


def _exchange_wait(name, sems, ins, landing, items, after):
    n_in, n_out = len(ins), len(landing)

    def body(*refs):
        x, land = refs[:n_in], refs[n_in:n_in + n_out]
        send_sems, recv_sems = refs[n_in + n_out], refs[n_in + n_out + 1]
        remote, local = _split_copies(x, land, send_sems, recv_sems, items, True)
        for cp in remote:
            cp.wait_send()
            cp.wait_recv()
        for cp in local:
            cp.wait()

    hbm = pl.BlockSpec(memory_space=pltpu.HBM)
    sem = pl.BlockSpec(memory_space=pltpu.SEMAPHORE)
    arrs = list(ins) + list(landing)
    res = pl.pallas_call(
        body, name=name, out_shape=tuple(pltpu.HBM(a.shape, a.dtype) for a in arrs),
        in_specs=[hbm] * (n_in + n_out) + [sem, sem, pl.BlockSpec(memory_space=pl.ANY)],
        out_specs=tuple([hbm] * (n_in + n_out)), input_output_aliases={i: i for i in range(n_in + n_out)},
        compiler_params=pltpu.CompilerParams(has_side_effects=pltpu.SideEffectType.DATAFLOW_SIDE_EFFECTING))(
            *arrs, *sems, after)
    return res[n_in:]


def _whole(ref, p):
    return ref


def _entry(ref, p):
    return ref.at[p]


def _gather_plan(a, b, kind):
    if kind == "col" and b % LANE == 0:
        return (a, N_DEV * b), (lambda ref, p: ref.at[:, pl.ds(pl.multiple_of(p * b, b), b)]), "col"
    return (N_DEV, a, b), _entry, ("row" if kind == "row" else "stack")


def _adamw_nat(name, parts, w, m, v):
    depth, b, c = w.shape
    assert len(parts) == depth
    tb = _pick(b, (128, 64, 32, 16, 8))
    if tb == b and b > 256:
        tb = 256
    spec = pl.BlockSpec((1, tb, c), lambda i, j: (i, j, 0))

    def body(*refs):
        p_refs = refs[:depth]
        w_ref, m_ref, v_ref, g_ref, d_ref, nm_ref, nv_ref = refs[depth:]
        for layer, p_ref in enumerate(p_refs):
            @pl.when(pl.program_id(0) == layer)
            def _(p_ref=p_ref):
                g = p_ref[0].astype(F32)
                for j in range(1, N_DEV):
                    g = g + p_ref[j].astype(F32)
                nm = ADAM_B1 * m_ref[0] + (1.0 - ADAM_B1) * g
                nv = ADAM_B2 * v_ref[0] + (1.0 - ADAM_B2) * jnp.square(g)
                m_hat = nm / (1.0 - ADAM_B1 ** ADAM_STEP)
                v_hat = nv / (1.0 - ADAM_B2 ** ADAM_STEP)
                g_ref[0] = g
                d_ref[0] = -ADAM_LR * (m_hat / (jnp.sqrt(v_hat) + ADAM_EPS) + ADAM_WD * w_ref[0])
                nm_ref[0] = nm
                nv_ref[0] = nv

    sds = jax.ShapeDtypeStruct((depth, b, c), F32)
    return pl.pallas_call(
        body, name=name, grid=(depth, pl.cdiv(b, tb)),
        in_specs=[pl.BlockSpec((N_DEV, tb, c), lambda i, j: (0, j, 0))] * depth + [spec, spec, spec],
        out_specs=[spec] * 4, out_shape=[sds] * 4, compiler_params=_cparams(("parallel", "parallel")))(*parts, w, m, v)


def _adamw(name, parts, w, m, v):
    rows = w.shape[0]
    tr = _pick(rows, (256, 128, 64, 32, 16, 8))
    spec = pl.BlockSpec((tr, PACK_W), lambda i: (i, 0))

    def body(p_ref, w_ref, m_ref, v_ref, g_ref, d_ref, nm_ref, nv_ref):
        g = p_ref[0]
        for j in range(1, N_DEV):
            g = g + p_ref[j]
        nm = ADAM_B1 * m_ref[...] + (1.0 - ADAM_B1) * g
        nv = ADAM_B2 * v_ref[...] + (1.0 - ADAM_B2) * jnp.square(g)
        m_hat = nm / (1.0 - ADAM_B1 ** ADAM_STEP)
        v_hat = nv / (1.0 - ADAM_B2 ** ADAM_STEP)
        g_ref[...] = g
        d_ref[...] = -ADAM_LR * (m_hat / (jnp.sqrt(v_hat) + ADAM_EPS) + ADAM_WD * w_ref[...])
        nm_ref[...] = nm
        nv_ref[...] = nv

    sds = jax.ShapeDtypeStruct((rows, PACK_W), F32)
    return pl.pallas_call(
        body, name=name, grid=(rows // tr,),
        in_specs=[pl.BlockSpec((N_DEV, tr, PACK_W), lambda i: (0, i, 0)), spec, spec, spec],
        out_specs=[spec] * 4, out_shape=[sds] * 4, compiler_params=_cparams(("parallel",)))(parts, w, m, v)


def _pack(arrs, dtype, row_mult=16):
    flat = jnp.concatenate([a.reshape(-1).astype(dtype) for a in arrs])
    unit = row_mult * PACK_W
    total = -(-flat.shape[0] // unit) * unit
    flat = jnp.pad(flat, (0, total - flat.shape[0]))
    return flat.reshape(-1, PACK_W)


def _pack_lead(arrs, dtype, row_mult):
    flat = jnp.concatenate([a.reshape(N_DEV, -1).astype(dtype) for a in arrs], axis=1)
    unit = row_mult * PACK_W
    total = -(-flat.shape[1] // unit) * unit
    flat = jnp.pad(flat, ((0, 0), (0, total - flat.shape[1])))
    return flat.reshape(N_DEV, -1, PACK_W)


def _unpack(buf, shapes, lead=()):
    flat = buf.reshape(lead + (-1,))
    out, off = [], 0
    for s in shapes:
        n = int(np.prod(s))
        out.append(flat[..., off:off + n].reshape(lead + tuple(s)))
        off += n
    return out


def _unshard(g, kind):
    if kind == "col":
        g = jnp.moveaxis(g, 0, -2)
        return g.reshape(g.shape[:-2] + (g.shape[-2] * g.shape[-1],))
    g = jnp.moveaxis(g, 0, 1)
    return g.reshape((g.shape[0], g.shape[1] * g.shape[2]) + g.shape[3:])


def _shard(full, kind):
    if kind == "col":
        s = full.reshape(full.shape[:-1] + (N_DEV, full.shape[-1] // N_DEV))
        return jnp.moveaxis(s, -2, 0)
    s = full.reshape((full.shape[0], N_DEV, full.shape[1] // N_DEV) + full.shape[2:])
    return jnp.moveaxis(s, 1, 0)


class _Geo:
    def __init__(self, bsz, seq):
        self.bsz, self.seq = bsz, seq
        self.pad = (-N_META) % SSM_CHUNK
        self.lp = self.pad + N_META + seq
        self.t0 = self.pad + N_META
        self.nq = 1 + seq // ATT_BLK
        assert self.t0 == LANE and seq % ATT_BLK == 0 and self.lp % SSM_CHUNK == 0
        self.nrows = bsz * self.lp
        self.nc = self.lp // SSM_CHUNK
        self.nh = SSM_D_INNER // SSM_HEAD_DIM
        self.gn = SSM_GROUPS * SSM_STATE
        self.cd = SSM_D_INNER + 2 * self.gn
        self.hq = MLA_HEADS * LANE
        order = (("z", SSM_D_INNER), ("g_ssm", D_MODEL), ("g_mla", D_MODEL), ("xs", SSM_D_INNER), ("bm", self.gn),
                 ("cm", self.gn), ("c_q", MLA_Q_LORA), ("c_kv", MLA_KV_LORA), ("dt", LANE), ("k_rope", LANE))
        self.col, off = {}, 0
        for nm, w in order:
            assert off % w == 0, (nm, off, w)
            self.col[nm] = (off, w)
            off += w
        self.pw = off
        assert self.nh <= LANE and MLA_ROPE == 64 and MLA_NOPE == LANE and MLA_V == LANE
        self.tr = _pick(self.lp, (1088, 768, 544, 512, 384, 272, 256, 128))
        self.tr_wide = _pick(self.lp, (544, 384, 272, 256, 128))

    def cb(self, nm):
        off, w = self.col[nm]
        return off // w

    def w_in_runs(self, shard_w):
        nh, half = self.nh, MLA_ROPE // 2
        src, pieces = 0, []
        for nm, n in (("z", SSM_D_INNER), ("xs", SSM_D_INNER), ("bm", self.gn), ("cm", self.gn), ("dt", nh),
                      ("c_q", MLA_Q_LORA), ("c_kv", MLA_KV_LORA), ("k_rope", MLA_ROPE), ("g_ssm", D_MODEL),
                      ("g_mla", D_MODEL)):
            dst = self.col[nm][0]
            if nm == "k_rope":
                pieces += [(src, half, dst), (src + half, half, dst + 2 * half)]
            else:
                pieces.append((src, n, dst))
            src += n
        assert src == shard_w * N_DEV
        runs = []
        for a, n, dst in pieces:
            for j in range(N_DEV):
                lo, hi = max(a, j * shard_w), min(a + n, (j + 1) * shard_w)
                if lo < hi:
                    runs.append((j, lo - j * shard_w, hi - lo, dst + lo - a))
        return runs


def _slot(a):
    h = MLA_ROPE // 2
    z = jnp.zeros(a.shape[:-1] + (h,), a.dtype)
    return jnp.concatenate([a[..., :h], z, a[..., h:], z], axis=-1)


def _unslot(a):
    h = MLA_ROPE // 2
    return jnp.concatenate([a[..., :h], a[..., 2 * h:3 * h]], axis=-1)


def _prep_layer(geo, wl):
    nh = geo.nh
    p = {}
    if "w_uq" in wl:
        uq = wl["w_uq"].reshape(MLA_Q_LORA, MLA_HEADS, MLA_NOPE + MLA_ROPE)
        p["w_qn"] = uq[..., :MLA_NOPE].reshape(MLA_Q_LORA, geo.hq)
        p["w_qp"] = _slot(uq[..., MLA_NOPE:]).reshape(MLA_Q_LORA, geo.hq)
    if "w_ukv" in wl:
        ukv = wl["w_ukv"].reshape(MLA_KV_LORA, MLA_HEADS, MLA_NOPE + MLA_V)
        p["w_k"] = ukv[..., :MLA_NOPE].reshape(MLA_KV_LORA, geo.hq)
        p["w_v"] = ukv[..., MLA_NOPE:].reshape(MLA_KV_LORA, geo.hq)
    for nm in ("w_in_pt", "conv_w", "w_branch_ssm", "w_branch_mla", "w_out", "w_mlp_up", "w_mlp_down"):
        if nm in wl:
            p[nm] = wl[nm]
    for nm in ("norm_mix_w", "conv_b", "ssm_norm_w", "q_norm_w", "kv_norm_w", "norm_mlp_w"):
        if nm in wl:
            p[nm] = wl[nm].reshape(1, -1)
    if "dt_bias" in wl:
        p["dt_bias"] = jnp.pad(wl["dt_bias"], (0, LANE - nh)).reshape(1, LANE)
        p["a_log"] = jnp.pad(wl["a_log"], (0, LANE - nh)).reshape(1, LANE)
        p["d_skip_full"] = jnp.repeat(wl["d_skip"], SSM_HEAD_DIM).reshape(1, SSM_D_INNER)
    return p


def _unprep_grads(geo, g):
    nh = geo.nh
    out = {}
    if "w_qn" in g:
        qn = g["w_qn"].reshape(MLA_Q_LORA, MLA_HEADS, MLA_NOPE)
        qp = _unslot(g["w_qp"].reshape(MLA_Q_LORA, MLA_HEADS, LANE))
        out["w_uq"] = jnp.concatenate([qn, qp], axis=-1).reshape(MLA_Q_LORA, -1)
    if "w_k" in g:
        wk = g["w_k"].reshape(MLA_KV_LORA, MLA_HEADS, MLA_NOPE)
        wv = g["w_v"].reshape(MLA_KV_LORA, MLA_HEADS, MLA_V)
        out["w_ukv"] = jnp.concatenate([wk, wv], axis=-1).reshape(MLA_KV_LORA, -1)
    for nm in ("w_in_pt", "w_branch_ssm", "w_branch_mla", "w_out", "w_mlp_up", "w_mlp_down", "conv_w"):
        if nm in g:
            out[nm] = g[nm]
    for nm in ("norm_mix_w", "conv_b", "ssm_norm_w", "q_norm_w", "kv_norm_w", "norm_mlp_w"):
        if nm in g:
            out[nm] = g[nm].reshape(-1)
    if "dt_bias" in g:
        out["dt_bias"] = g["dt_bias"].reshape(-1)[:nh]
        out["a_log"] = g["a_log"].reshape(-1)[:nh]
        out["d_skip"] = g["d_skip_full"].reshape(nh, SSM_HEAD_DIM).sum(-1)
    return out


def _tables(geo):
    pos = jnp.arange(geo.lp, dtype=F32) - geo.pad
    inv = ROPE_THETA ** (-jnp.arange(0, MLA_ROPE, 2, dtype=F32) / MLA_ROPE)
    ang = pos[:, None] * inv[None, :]
    cos, sin = jnp.cos(ang), jnp.sin(ang)
    z = jnp.zeros_like(cos)
    rows = jnp.arange(geo.lp)[:, None]
    return {"cos": jnp.concatenate([cos, z, cos, z], axis=-1), "sin": jnp.concatenate([-sin, z, sin, z], axis=-1),
            "valid": (rows >= geo.pad).astype(F32), "token": (rows >= geo.pad + N_META).astype(F32)}


def _w_in_assemble(geo, gathered):
    _, sw, d = gathered.shape
    runs = geo.w_in_runs(sw)
    tl = _pick(d, (256, 128))

    def body(x_ref, o_ref):
        o_ref[...] = jnp.zeros_like(o_ref)
        for j, s0, n, d0 in runs:
            o_ref[d0:d0 + n, :] = x_ref[j, s0:s0 + n, :]

    return pl.pallas_call(
        body, name="w_in_assemble", grid=(d // tl,), in_specs=[pl.BlockSpec((N_DEV, sw, tl), lambda i: (0, 0, i))],
        out_specs=pl.BlockSpec((geo.pw, tl), lambda i: (0, i)),
        out_shape=jax.ShapeDtypeStruct((geo.pw, d), gathered.dtype), compiler_params=_cparams(("parallel",)))(gathered)


def _w_in_split(geo, g_padded, sw):
    d = g_padded.shape[1]
    runs = geo.w_in_runs(sw)
    tl = _pick(d, (256, 128))

    def body(x_ref, o_ref):
        for j, s0, n, d0 in runs:
            o_ref[j, s0:s0 + n, :] = x_ref[d0:d0 + n, :]

    return pl.pallas_call(
        body, name="w_in_split", grid=(d // tl,), in_specs=[pl.BlockSpec((geo.pw, tl), lambda i: (0, i))],
        out_specs=pl.BlockSpec((N_DEV, sw, tl), lambda i: (0, 0, i)),
        out_shape=jax.ShapeDtypeStruct((N_DEV, sw, d), g_padded.dtype),
        compiler_params=_cparams(("parallel",)))(g_padded)


def _conv_cols(geo, cbw):
    x0 = geo.col["xs"][0]
    assert geo.col["bm"][0] == x0 + SSM_D_INNER and geo.col["cm"][0] == geo.col["bm"][0] + geo.gn and x0 % cbw == 0
    return lambda j: x0 // cbw + j


def _conv_taps(x):
    return [pltpu.roll(x, SSM_CONV - 1 - k, axis=0) for k in range(SSM_CONV - 1)] + [x]


def _conv_pre(x, w_ref, b_ref, taps=None):
    taps = _conv_taps(x) if taps is None else taps
    acc = b_ref[...]
    for k in range(SSM_CONV):
        acc = acc + taps[k] * w_ref[k:k + 1, :]
    return acc


def _conv_fwd(geo, proj, conv_w, conv_b):
    cbw = 256
    colmap = _conv_cols(geo, cbw)
    lp, pad = geo.lp, geo.pad

    def body(x_ref, w_ref, b_ref, o_ref):
        valid = (lax.broadcasted_iota(jnp.int32, (lp, 1), 0) >= pad).astype(F32)
        o_ref[...] = (_silu(_conv_pre(x_ref[...].astype(F32), w_ref, b_ref)) * valid).astype(o_ref.dtype)

    return pl.pallas_call(
        body, name="conv_fwd", grid=(geo.bsz, geo.cd // cbw),
        in_specs=[pl.BlockSpec((lp, cbw), lambda b, j: (b, colmap(j))),
                  pl.BlockSpec((SSM_CONV, cbw), lambda b, j: (0, j)), pl.BlockSpec((1, cbw), lambda b, j: (0, j))],
        out_specs=pl.BlockSpec((lp, cbw), lambda b, j: (b, j)),
        out_shape=jax.ShapeDtypeStruct((geo.nrows, geo.cd), MXU_DTYPE),
        compiler_params=_cparams(("parallel", "parallel")))(proj, conv_w, conv_b)


def _conv_bwd(geo, proj, conv_w, conv_b, dxc, dproj):
    cbw = 256
    colmap = _conv_cols(geo, cbw)
    lp, pad = geo.lp, geo.pad

    def body(x_ref, w_ref, b_ref, dy_ref, _, dx_ref, gw_ref, gb_ref):
        b = pl.program_id(1)
        valid = (lax.broadcasted_iota(jnp.int32, (lp, 1), 0) >= pad).astype(F32)
        taps = _conv_taps(x_ref[...].astype(F32))
        pre = _conv_pre(None, w_ref, b_ref, taps)
        sig = _sigmoid(pre)
        dpre = dy_ref[...] * (sig * (1.0 + pre * (1.0 - sig))) * valid
        dx = dpre * w_ref[SSM_CONV - 1:SSM_CONV, :]
        for k in range(SSM_CONV - 1):
            dx = dx + pltpu.roll(dpre, lp - (SSM_CONV - 1 - k), axis=0) * w_ref[k:k + 1, :]
        gws = [jnp.sum(dpre * taps[k], axis=0, keepdims=True) for k in range(SSM_CONV)]
        dx_ref[...] = (dx * valid).astype(dx_ref.dtype)

        @pl.when(b == 0)
        def _():
            gw_ref[...] = jnp.zeros_like(gw_ref)
            gb_ref[...] = jnp.zeros_like(gb_ref)

        for k in range(SSM_CONV):
            gw_ref[k:k + 1, :] += gws[k]
        gb_ref[...] += jnp.sum(dpre, axis=0, keepdims=True)

    return pl.pallas_call(
        body, name="conv_bwd", grid=(geo.cd // cbw, geo.bsz),
        in_specs=[pl.BlockSpec((lp, cbw), lambda j, b: (b, colmap(j))),
                  pl.BlockSpec((SSM_CONV, cbw), lambda j, b: (0, j)), pl.BlockSpec((1, cbw), lambda j, b: (0, j)),
                  pl.BlockSpec((lp, cbw), lambda j, b: (b, j)), pl.BlockSpec(memory_space=pl.ANY)],
        out_specs=[pl.BlockSpec((lp, cbw), lambda j, b: (b, colmap(j))),
                   pl.BlockSpec((SSM_CONV, cbw), lambda j, b: (0, j)), pl.BlockSpec((1, cbw), lambda j, b: (0, j))],
        out_shape=[jax.ShapeDtypeStruct(dproj.shape, dproj.dtype),
                   jax.ShapeDtypeStruct((SSM_CONV, geo.cd), F32), jax.ShapeDtypeStruct((1, geo.cd), F32)],
        input_output_aliases={4: 0},
        compiler_params=_cparams(("parallel", "arbitrary")))(proj, conv_w, conv_b, dxc, dproj)


def _tri(q):
    r = lax.broadcasted_iota(jnp.int32, (q, q), 0)
    c = lax.broadcasted_iota(jnp.int32, (q, q), 1)
    return r >= c


def _ssd_pre(dtr, dtb, alog, valid):
    dt = _softplus(dtr + dtb) * valid
    adt = dt * (-jnp.exp(alog))
    a_cs = _dot(_tri(SSM_CHUNK).astype(F32), adt, 1, 0, precision=lax.Precision.HIGHEST)
    return dt, a_cs


def _ssd_specs(geo, rev):
    nc, q = geo.nc, SSM_CHUNK
    ci = (lambda c: nc - 1 - c) if rev else (lambda c: c)
    nxb = SSM_D_INNER // geo.gn
    return [pl.BlockSpec((q, SSM_D_INNER), lambda b, c: (b * nc + ci(c), 0)),
            pl.BlockSpec((q, geo.gn), lambda b, c: (b * nc + ci(c), nxb)),
            pl.BlockSpec((q, geo.gn), lambda b, c: (b * nc + ci(c), nxb + 1)),
            pl.BlockSpec((q, LANE), lambda b, c: (b * nc + ci(c), 0)),
            pl.BlockSpec((1, LANE), lambda b, c: (0, 0)), pl.BlockSpec((1, LANE), lambda b, c: (0, 0))], ci


def _expand_heads(cols, nh):
    per = LANE // SSM_HEAD_DIM
    lane = lax.broadcasted_iota(jnp.int32, (1, LANE), 1)
    blocks = []
    for j in range(nh // per):
        blk = jnp.broadcast_to(cols[:, j * per:j * per + 1], (cols.shape[0], LANE))
        for k in range(1, per):
            blk = jnp.where(lane >= k * SSM_HEAD_DIM, cols[:, j * per + k:j * per + k + 1], blk)
        blocks.append(blk)
    return jnp.concatenate(blocks, axis=1)


def _head_maps(geo):
    e = (jnp.arange(SSM_D_INNER)[None, :] // SSM_HEAD_DIM == jnp.arange(LANE)[:, None]).astype(F32)
    return e, e.T


def _ssd_fwd_g(geo, xc, proj, dt_bias, a_log):
    q, p, n, e = SSM_CHUNK, SSM_HEAD_DIM, SSM_STATE, geo.nh // SSM_GROUPS
    nc, pad, gw = geo.nc, geo.pad, SSM_D_INNER // SSM_GROUPS
    in_specs, _ = _ssd_specs(geo, False)

    def body(xs_ref, b_ref, c_ref, dtr_ref, dtb_ref, alog_ref, y_ref, sp_ref, state, xdt_s, y_s):
        c = pl.program_id(1)

        @pl.when(c == 0)
        def _():
            state[...] = jnp.zeros_like(state)

        sp_ref[...] = state[...]
        inert = (c + 1) * q <= pad

        @pl.when(inert)
        def _():
            y_ref[...] = jnp.zeros_like(y_ref)

        @pl.when(jnp.logical_not(inert))
        def _():
            valid = (c * q + lax.broadcasted_iota(jnp.int32, (q, 1), 0) >= pad).astype(F32)
            dt, a_cs = _ssd_pre(dtr_ref[...], dtb_ref[...], alog_ref[...], valid)
            a_cst = a_cs.T
            dt_x, a_x = _expand_heads(dt, geo.nh), _expand_heads(a_cs, geo.nh)
            tri = _tri(q)
            for g in range(SSM_GROUPS):
                gs = slice(g * gw, (g + 1) * gw)
                bg, cg = b_ref[:, g * n:(g + 1) * n], c_ref[:, g * n:(g + 1) * n]
                a_g = a_x[:, gs]
                a_last = a_g[q - 1:q, :]
                xdt_g = xs_ref[:, gs] * dt_x[:, gs]
                xdt_s[:, gs] = xdt_g
                s_g = state[:, gs]
                y_s[:, gs] = _mxdot(cg, s_g, 1, 0) * jnp.exp(a_g)
                state[:, gs] = s_g * jnp.exp(a_last) + _mxdot(bg, xdt_g * jnp.exp(a_last - a_g), 0, 0)
                cb = _mxdot(cg, bg, 1, 1)
                for hh in range(e):
                    h = g * e + hh
                    hs = slice(h * p, (h + 1) * p)
                    ldec = jnp.exp(jnp.where(tri, a_cs[:, h:h + 1] - a_cst[h:h + 1, :], -jnp.inf))
                    y_s[:, hs] += _mxdot(cb * ldec, xdt_s[:, hs], 1, 0)
            y_ref[...] = y_s[...].astype(y_ref.dtype)

    return pl.pallas_call(
        body, name="ssd_fwd", grid=(geo.bsz, nc), in_specs=in_specs,
        out_specs=[pl.BlockSpec((q, SSM_D_INNER), lambda b, c: (b * nc + c, 0)),
                   pl.BlockSpec((n, SSM_D_INNER), lambda b, c: (b * nc + c, 0))],
        out_shape=[jax.ShapeDtypeStruct((geo.nrows, SSM_D_INNER), MXU_DTYPE),
                   jax.ShapeDtypeStruct((geo.bsz * nc * n, SSM_D_INNER), F32)],
        scratch_shapes=[pltpu.VMEM((n, SSM_D_INNER), F32), pltpu.VMEM((q, SSM_D_INNER), F32),
                        pltpu.VMEM((q, SSM_D_INNER), F32)],
        compiler_params=_cparams(("parallel", "arbitrary")))(xc, xc, xc, proj, dt_bias, a_log)


def _ssd_bwd_g(geo, xc, proj, dt_bias, a_log, s_prev_all, dy, dxs_skip, dproj):
    q, p, n, e = SSM_CHUNK, SSM_HEAD_DIM, SSM_STATE, geo.nh // SSM_GROUPS
    nc, pad, di, gn, gw = geo.nc, geo.pad, SSM_D_INNER, geo.gn, SSM_D_INNER // SSM_GROUPS
    in_specs, ci = _ssd_specs(geo, True)
    row_spec = pl.BlockSpec((q, di), lambda b, c: (b * nc + ci(c), 0))
    e_map, _ = _head_maps(geo)
    in_specs += [pl.BlockSpec((n, di), lambda b, c: (b * nc + ci(c), 0)), row_spec, row_spec,
                 pl.BlockSpec((LANE, di), lambda b, c: (0, 0)), pl.BlockSpec(memory_space=pl.ANY)]

    def body(xs_ref, b_ref, c_ref, dtr_ref, dtb_ref, alog_ref, sp_ref, dy_ref, dsk_ref, e_ref, _,
             dxc_ref, ddt_ref, gdtb_ref, galog_ref, dstate, xdt_s, dxdt_s):
        step = pl.program_id(1)
        first = jnp.logical_and(pl.program_id(0) == 0, step == 0)
        c = nc - 1 - step

        @pl.when(step == 0)
        def _():
            dstate[...] = jnp.zeros_like(dstate)

        @pl.when(first)
        def _():
            gdtb_ref[...] = jnp.zeros_like(gdtb_ref)
            galog_ref[...] = jnp.zeros_like(galog_ref)

        inert = (c + 1) * q <= pad

        @pl.when(inert)
        def _():
            dxc_ref[...] = jnp.zeros_like(dxc_ref)
            ddt_ref[...] = jnp.zeros_like(ddt_ref)

        @pl.when(jnp.logical_not(inert))
        def _():
            valid = (c * q + lax.broadcasted_iota(jnp.int32, (q, 1), 0) >= pad).astype(F32)
            dtr, dtb, alog = dtr_ref[...], dtb_ref[...], alog_ref[...]
            dt, a_cs = _ssd_pre(dtr, dtb, alog, valid)
            a_cst = a_cs.T
            dt_x, a_x = _expand_heads(dt, geo.nh), _expand_heads(a_cs, geo.nh)
            tri = _tri(q)
            lane = lax.broadcasted_iota(jnp.int32, (1, LANE), 1)
            sub = lax.broadcasted_iota(jnp.int32, (LANE, 1), 0)
            d_dt = jnp.zeros((q, LANE), F32)
            d_acs = jnp.zeros((q, LANE), F32)
            d_acst = jnp.zeros((LANE, q), F32)
            d_last = jnp.zeros((1, LANE), F32)
            for g in range(SSM_GROUPS):
                gs = slice(g * gw, (g + 1) * gw)
                bg, cg = b_ref[:, g * n:(g + 1) * n], c_ref[:, g * n:(g + 1) * n]
                seg = lambda v: _mxdot(v, e_ref[:, gs], 1, 1)
                a_g, dt_g, x_g, dy_g = a_x[:, gs], dt_x[:, gs], xs_ref[:, gs], dy_ref[:, gs]
                e_col, e_last, dec = jnp.exp(a_g), jnp.exp(a_g[q - 1:q, :]), jnp.exp(a_g[q - 1:q, :] - a_g)
                xdt_g = x_g * dt_g
                xdt_s[:, gs] = xdt_g
                s_g, ds_g = sp_ref[:, gs], dstate[:, gs]
                cs = _mxdot(cg, s_g, 1, 0)
                d_cs = dy_g * e_col
                d_acs = d_acs + seg(d_cs * cs)
                d_cg = _mxdot(d_cs, s_g, 1, 1)
                dstate[:, gs] = _mxdot(cg, d_cs, 0, 0) + ds_g * e_last
                dl_x = jnp.sum(ds_g * s_g, axis=0, keepdims=True) * e_last
                d_last = d_last + seg(jnp.broadcast_to(dl_x, (8, gw)))[:1]
                gmat = _mxdot(bg, ds_g, 1, 0)
                xd = xdt_g * dec
                d_bg = _mxdot(xd, ds_g, 1, 1)
                d_dec = seg(xd * gmat)
                d_acs = d_acs - d_dec
                d_last = d_last + jnp.sum(d_dec, axis=0, keepdims=True)
                dxdt_s[:, gs] = dec * gmat
                cb = _mxdot(cg, bg, 1, 1)
                d_cb = jnp.zeros((q, q), F32)
                for hh in range(e):
                    h = g * e + hh
                    hs = slice(h * p, (h + 1) * p)
                    ldec = jnp.exp(jnp.where(tri, a_cs[:, h:h + 1] - a_cst[h:h + 1, :], -jnp.inf))
                    dyh = dy_ref[:, hs]
                    d_m = _mxdot(dyh, xdt_s[:, hs], 1, 1)
                    dxdt_s[:, hs] += _mxdot(cb * ldec, dyh, 0, 0)
                    d_cb = d_cb + d_m * ldec
                    d_diff = d_m * cb * ldec
                    d_acs = d_acs + jnp.sum(d_diff, axis=1, keepdims=True) * (lane == h).astype(F32)
                    d_acst = d_acst - (sub == h).astype(F32) * jnp.sum(d_diff, axis=0, keepdims=True)
                d_xdt = dxdt_s[:, gs]
                dxc_ref[:, gs] = d_xdt * dt_g + dsk_ref[:, gs]
                d_dt = d_dt + seg(d_xdt * x_g)
                dxc_ref[:, di + g * n:di + (g + 1) * n] = d_bg + _mxdot(d_cb, cg, 0, 0)
                dxc_ref[:, di + gn + g * n:di + gn + (g + 1) * n] = d_cg + _mxdot(d_cb, bg, 1, 0)
            is_last = (lax.broadcasted_iota(jnp.int32, (q, 1), 0) == q - 1).astype(F32)
            d_acs = d_acs + d_acst.T + is_last * d_last
            d_adt = _dot(_tri(q).astype(F32), d_acs, 0, 0, precision=lax.Precision.HIGHEST)
            a = -jnp.exp(alog)
            d_dt = d_dt + d_adt * a
            d_dtr = d_dt * valid * _sigmoid(dtr + dtb)
            ddt_ref[...] = d_dtr.astype(ddt_ref.dtype)
            gdtb_ref[...] += jnp.sum(d_dtr, axis=0, keepdims=True)
            galog_ref[...] += jnp.sum(d_adt * dt, axis=0, keepdims=True) * a

    vec = pl.BlockSpec((1, LANE), lambda b, c: (0, 0))
    return pl.pallas_call(
        body, name="ssd_bwd", grid=(geo.bsz, nc), in_specs=in_specs,
        out_specs=[pl.BlockSpec((q, geo.cd), lambda b, c: (b * nc + ci(c), 0)),
                   pl.BlockSpec((q, LANE), lambda b, c: (b * nc + ci(c), geo.cb("dt"))), vec, vec],
        out_shape=[jax.ShapeDtypeStruct((geo.nrows, geo.cd), F32), jax.ShapeDtypeStruct(dproj.shape, dproj.dtype),
                   jax.ShapeDtypeStruct((1, LANE), F32), jax.ShapeDtypeStruct((1, LANE), F32)],
        scratch_shapes=[pltpu.VMEM((n, di), F32), pltpu.VMEM((q, di), F32), pltpu.VMEM((q, di), F32)],
        input_output_aliases={10: 1},
        compiler_params=_cparams(("arbitrary", "arbitrary")))(
            xc, xc, xc, proj, dt_bias, a_log, s_prev_all, dy, dxs_skip, e_map, dproj)


BIAS_LANE = MLA_ROPE // 2
KEY_OFF = -1e30
ATT_SCALE = (MLA_NOPE + MLA_ROPE) ** -0.5


def _row_t(col):
    return jnp.broadcast_to(col, (col.shape[0], LANE)).T[:8]


def _att_tile(geo, i):
    return (0, geo.t0) if i == 0 else (geo.t0 + (i - 1) * ATT_BLK, ATT_BLK)


def _attn_fwd4(geo, qn, qp, kn, kp, v):
    t, lp, nq = ATT_BLK, geo.lp, geo.nq

    def body(qn_ref, qp_ref, kn_ref, kp_ref, v_ref, o_ref, lse_ref, k_ref):
        qi = pl.program_id(2)

        @pl.when(qi == 0)
        def _():
            k_ref[:, :LANE] = kn_ref[...]
            k_ref[:, LANE:] = kp_ref[...]

        for i in range(nq):
            @pl.when(qi == i)
            def _(i=i):
                r0, rows = _att_tile(geo, i)
                w = r0 + rows
                q = jnp.concatenate([qn_ref[r0:w, :], qp_ref[r0:w, :]], axis=1)
                s = _mxdot(q, k_ref[0:w, :], 1, 1) * ATT_SCALE
                diag = jnp.where(_tri(rows), s[:, r0:], -jnp.inf)
                s = diag if r0 == 0 else jnp.concatenate([s[:, :r0], diag], axis=1)
                m = jnp.max(s, axis=1, keepdims=True)
                pr = jnp.exp(s - m)
                l = jnp.sum(pr, axis=1, keepdims=True)
                o_ref[r0:w, :] = (_mxdot(pr, v_ref[0:w, :], 1, 0) / l).astype(o_ref.dtype)
                lse_ref[0, 0, 0, :, 0:rows] = _row_t(m + jnp.log(l))

    seq = pl.BlockSpec((lp, LANE), lambda b, h, i: (b, h))
    return pl.pallas_call(
        body, name="attn_fwd", grid=(geo.bsz, MLA_HEADS, nq),
        in_specs=[seq, seq, seq, pl.BlockSpec((lp, LANE), lambda b, h, i: (b, 0)), seq],
        out_specs=[seq, pl.BlockSpec((1, 1, 1, 8, t), lambda b, h, i: (b, h, i, 0, 0))],
        out_shape=[jax.ShapeDtypeStruct((geo.nrows, geo.hq), MXU_DTYPE),
                   jax.ShapeDtypeStruct((geo.bsz, MLA_HEADS, nq, 8, t), F32)],
        scratch_shapes=[pltpu.VMEM((lp, 2 * LANE), MXU_DTYPE)],
        compiler_params=_cparams(("parallel", "parallel", "arbitrary")))(qn, qp, kn, kp, v)


def _attn_bwd4(geo, qn, qp, kn, kp, v, d_o, o, lse):
    t, lp, nq = ATT_BLK, geo.lp, geo.nq

    def body(qn_ref, qp_ref, kn_ref, kp_ref, v_ref, do_ref, o_ref, lse_ref,
             dqn_ref, dqp_ref, dkn_ref, dkp_ref, dv_ref, q_ref, dl_s):
        kj = pl.program_id(2)

        @pl.when(kj == 0)
        def _():
            q_ref[:, :LANE] = qn_ref[...]
            q_ref[:, LANE:] = qp_ref[...]
            dqn_ref[...] = jnp.zeros_like(dqn_ref)
            dqp_ref[...] = jnp.zeros_like(dqp_ref)
            for i in range(nq):
                r0, rows = _att_tile(geo, i)
                dl_s[i, :, 0:rows] = _row_t(jnp.sum(do_ref[r0:r0 + rows, :].astype(F32) * o_ref[r0:r0 + rows, :].astype(F32),
                                                    axis=1, keepdims=True))

        def per_query(ref, first):
            return jnp.concatenate([ref[i][:1, 0:_att_tile(geo, i)[1]] for i in range(first, nq)], axis=1)

        for i in range(nq):
            @pl.when(kj == i)
            def _(i=i):
                k0, kw = _att_tile(geo, i)
                k = jnp.concatenate([kn_ref[k0:k0 + kw, :], kp_ref[k0:k0 + kw, :]], axis=1)
                vv = v_ref[k0:k0 + kw, :]
                q, d_o_blk = q_ref[k0:lp, :], do_ref[k0:lp, :]
                wq = lp - k0
                st = _mxdot(k, q, 1, 1) * ATT_SCALE
                keys = lax.broadcasted_iota(jnp.int32, (kw, kw), 0)
                diag = jnp.where(keys <= lax.broadcasted_iota(jnp.int32, (kw, kw), 1), st[:, :kw], -jnp.inf)
                st = diag if wq == kw else jnp.concatenate([diag, st[:, kw:]], axis=1)
                pt = jnp.exp(st - per_query(lse_ref.at[0, 0], i))
                dst = pt * (_mxdot(vv, d_o_blk, 1, 1) - per_query(dl_s, i)) * ATT_SCALE
                dq = _mxdot(dst, k, 0, 0)
                dqn_ref[k0:lp, :] += dq[:, :LANE]
                dqp_ref[k0:lp, :] += dq[:, LANE:]
                dk = _mxdot(dst, q, 1, 0)
                dkn_ref[k0:k0 + kw, :] = dk[:, :LANE].astype(dkn_ref.dtype)
                dkp_ref[k0:k0 + kw, :] = dk[:, LANE:]
                dv_ref[k0:k0 + kw, :] = _mxdot(pt, d_o_blk, 1, 0).astype(dv_ref.dtype)

    seq = pl.BlockSpec((lp, LANE), lambda b, h, j: (b, h))
    return pl.pallas_call(
        body, name="attn_bwd", grid=(geo.bsz, MLA_HEADS, nq),
        in_specs=[seq, seq, seq, pl.BlockSpec((lp, LANE), lambda b, h, j: (b, 0)), seq, seq, seq,
                  pl.BlockSpec((1, 1, nq, 8, t), lambda b, h, j: (b, h, 0, 0, 0))],
        out_specs=[seq, seq, seq, seq, seq],
        out_shape=[jax.ShapeDtypeStruct((geo.nrows, geo.hq), F32), jax.ShapeDtypeStruct((geo.nrows, geo.hq), F32),
                   jax.ShapeDtypeStruct((geo.nrows, geo.hq), MXU_DTYPE), jax.ShapeDtypeStruct((geo.nrows, geo.hq), F32),
                   jax.ShapeDtypeStruct((geo.nrows, geo.hq), MXU_DTYPE)],
        scratch_shapes=[pltpu.VMEM((lp, 2 * LANE), MXU_DTYPE), pltpu.VMEM((nq, 8, t), F32)],
        compiler_params=_cparams(("parallel", "parallel", "arbitrary")))(qn, qp, kn, kp, v, d_o, o, lse)


def _rope(x, cos, sin):
    return x * cos + pltpu.roll(x, LANE // 2, axis=1) * sin


def _rope_t(dx, cos, sin):
    return dx * cos + pltpu.roll(dx * sin, LANE // 2, axis=1)


def _per_head(f):
    def fn(x, cos, sin):
        return (jnp.concatenate([f(x[:, h * LANE:(h + 1) * LANE], cos, sin) for h in range(MLA_HEADS)], axis=1),)
    return fn


def _layer_fwd(geo, h, w, tab, late=None):
    nr, tr, trw = geo.nrows, geo.tr, geo.tr_wide
    tb = geo.lp // tr
    rw = functools.partial(_rowwise, nrows=nr)
    s = {"h": h}
    (s["u"],) = rw("rms_mix", lambda x, g: (_rms(x, g),), tr=tr, rows=[(h, D_MODEL, 0)],
                   vecs=[(w["norm_mix_w"], D_MODEL, 0)], outs=[(D_MODEL, D_MODEL, MXU_DTYPE)])
    proj, s["proj_dt"] = _mm("mm_in", s["u"], w["w_in_pt"], tb=True, out_dtype=MXU_DTYPE,
                             side=(geo.col["dt"][0], LANE))
    s["proj"] = proj
    xc = s["xc"] = _conv_fwd(geo, proj, w["conv_w"], w["conv_b"])
    s["y_ssd"], s["s_prev"] = _ssd_fwd_g(geo, xc, s["proj_dt"], w["dt_bias"], w["a_log"])
    gw = SSM_D_INNER // SSM_GROUPS

    def gate_norm(y, x, z, dsk, nw):
        return (_rms((y + x * dsk) * _silu(z.astype(F32)), nw),)

    (s["y_ssm"],) = rw("ssm_gate_norm", gate_norm, tr=tr, ncb=SSM_GROUPS,
                       rows=[(s["y_ssd"], gw, 0), (xc, gw, 0), (proj, gw, geo.col["z"][0] // gw)],
                       vecs=[(w["d_skip_full"], gw, 0), (w["ssm_norm_w"], gw, 0)], outs=[(SSM_D_INNER, gw, MXU_DTYPE)])
    if late is not None:
        w = {**w, **late(s["y_ssm"])}
    (s["cq_n"],) = rw("rms_q", lambda x, g: (_rms(x, g),), tr=tr, rows=[(proj, MLA_Q_LORA, geo.cb("c_q"))],
                      vecs=[(w["q_norm_w"], MLA_Q_LORA, 0)], outs=[(MLA_Q_LORA, MLA_Q_LORA, MXU_DTYPE)])
    (s["ckv_n"],) = rw("rms_kv", lambda x, g: (_rms(x, g),), tr=tr, rows=[(proj, MLA_KV_LORA, geo.cb("c_kv"))],
                       vecs=[(w["kv_norm_w"], MLA_KV_LORA, 0)], outs=[(MLA_KV_LORA, MLA_KV_LORA, MXU_DTYPE)])
    s["qn"] = _mm("mm_qn", s["cq_n"], w["w_qn"], out_dtype=MXU_DTYPE)
    qp_raw = _mm("mm_qp", s["cq_n"], w["w_qp"])
    s["kn"] = _mm("mm_kn", s["ckv_n"], w["w_k"], out_dtype=MXU_DTYPE)
    s["v"] = _mm("mm_v", s["ckv_n"], w["w_v"], out_dtype=MXU_DTYPE)
    bias_lane = lambda: lax.broadcasted_iota(jnp.int32, (1, LANE), 1) == BIAS_LANE
    rope_tabs = [(tab["cos"], LANE, 0), (tab["sin"], LANE, 0)]
    (s["qp"],) = rw("rope_q", _per_head(lambda xp, c, sn: jnp.where(bias_lane(), 1.0, _rope(xp, c, sn))), tr=tr,
                    rows=[(qp_raw, geo.hq, 0)], tabs=rope_tabs, outs=[(geo.hq, geo.hq, MXU_DTYPE)], tab_blocks=tb)
    (s["kp"],) = rw("rope_k", lambda xp, c, sn, valid: (jnp.where(bias_lane(), KEY_OFF * (1.0 - valid),
                                                                 _rope(xp.astype(F32), c, sn)),),
                    tr=tr, rows=[(proj, LANE, geo.cb("k_rope"))], tabs=rope_tabs + [(tab["valid"], 1, 0)],
                    outs=[(LANE, LANE, MXU_DTYPE)], tab_blocks=tb)
    s["o"], s["lse"] = _attn_fwd4(geo, s["qn"], s["qp"], s["kn"], s["kp"], s["v"])
    s["ys_p"] = _mm("mm_bs", s["y_ssm"], w["w_branch_ssm"], out_dtype=MXU_DTYPE)
    s["ym_p"] = _mm("mm_bm", s["o"], w["w_branch_mla"], out_dtype=MXU_DTYPE)

    def gate(gs, gm, ys, ym):
        return (_sigmoid(gs.astype(F32)) * ys + _sigmoid(gm.astype(F32)) * ym,)

    (s["mixed"],) = rw("gate", gate, tr=tr, rows=[(proj, D_MODEL, geo.cb("g_ssm")), (proj, D_MODEL, geo.cb("g_mla")),
                                                  (s["ys_p"], D_MODEL, 0), (s["ym_p"], D_MODEL, 0)],
                       outs=[(D_MODEL, D_MODEL, MXU_DTYPE)])
    s["h2"] = _mm("mm_out", s["mixed"], w["w_out"], add=h)
    (s["vn"],) = rw("rms_mlp", lambda x, g: (_rms(x, g),), tr=tr, rows=[(s["h2"], D_MODEL, 0)],
                    vecs=[(w["norm_mlp_w"], D_MODEL, 0)], outs=[(D_MODEL, D_MODEL, MXU_DTYPE)])
    s["up"], s["act"] = _mm("mm_up", s["vn"], w["w_mlp_up"],
                            epi=(lambda r: (r, jnp.square(jnp.maximum(r, 0.0))), (MXU_DTYPE, MXU_DTYPE)))
    return _mm("mm_down", s["act"], w["w_mlp_down"], add=s["h2"]), s, w


def _layer_bwd(geo, dh3, s, w, tab, mid=None, tail=None, dep=None):
    nr, tr, trw = geo.nrows, geo.tr, geo.tr_wide
    tb = geo.lp // tr
    rw = functools.partial(_rowwise, nrows=nr)
    g = {}
    proj = s["proj"]

    def rms_bwd(x, dy, res, gw):
        _, vjp = jax.vjp(_rms, x.astype(F32), gw)
        dx, dgw = vjp(dy.astype(F32))
        return dx + res, dgw

    def rms_bwd_nores(x, dy, gw):
        _, vjp = jax.vjp(_rms, x.astype(F32), gw)
        return vjp(dy.astype(F32))

    (dup,) = _mm("mm_down_t", dh3, w["w_mlp_down"], tb=True, add=s["up"], dep=dep,
                 epi=(lambda r, up: (r * 2.0 * jnp.maximum(up, 0.0),), (MXU_DTYPE,)))
    g["w_mlp_down"] = _mm("mm_down_g", s["act"], dh3, ta=True, out_dtype=MXU_DTYPE)
    g["w_mlp_up"] = _mm("mm_up_g", s["vn"], dup, ta=True, out_dtype=MXU_DTYPE)
    dvn = _mm("mm_up_t", dup, w["w_mlp_up"], tb=True)
    dh2, g["norm_mlp_w"] = rw("rms_mlp_bwd", rms_bwd, tr=tr,
                              rows=[(s["h2"], D_MODEL, 0), (dvn, D_MODEL, 0), (dh3, D_MODEL, 0)],
                              vecs=[(w["norm_mlp_w"], D_MODEL, 0)], outs=[(D_MODEL, D_MODEL, F32)],
                              reds=[(D_MODEL, D_MODEL)])
    dmixed = _mm("mm_out_t", dh2, w["w_out"], tb=True, out_dtype=MXU_DTYPE)
    g["w_out"] = _mm("mm_out_g", s["mixed"], dh2, ta=True, out_dtype=MXU_DTYPE)

    def gate_bwd(gs, gm, ys, ym, dm):
        f = lambda a, b, c, d: _sigmoid(a) * c + _sigmoid(b) * d
        _, vjp = jax.vjp(f, gs.astype(F32), gm.astype(F32), ys.astype(F32), ym.astype(F32))
        dgs, dgm, dys, dym = vjp(dm.astype(F32))
        return dys, dym, jnp.concatenate([dgs, dgm], axis=1)

    assert geo.col["g_mla"][0] == geo.col["g_ssm"][0] + D_MODEL and geo.col["g_ssm"][0] % (2 * D_MODEL) == 0
    dys_p, dym_p, dproj = rw(
        "gate_bwd", gate_bwd, tr=tr,
        rows=[(proj, D_MODEL, geo.cb("g_ssm")), (proj, D_MODEL, geo.cb("g_mla")), (s["ys_p"], D_MODEL, 0),
              (s["ym_p"], D_MODEL, 0), (dmixed, D_MODEL, 0)],
        outs=[(D_MODEL, D_MODEL, MXU_DTYPE)] * 2 + [(geo.pw, 2 * D_MODEL, MXU_DTYPE, geo.col["g_ssm"][0] // (2 * D_MODEL))])
    g["w_branch_ssm"] = _mm("mm_bs_g", s["y_ssm"], dys_p, ta=True, out_dtype=MXU_DTYPE)
    dy_ssm = _mm("mm_bs_t", dys_p, w["w_branch_ssm"], tb=True, out_dtype=MXU_DTYPE)
    g["w_branch_mla"] = _mm("mm_bm_g", s["o"], dym_p, ta=True, out_dtype=MXU_DTYPE)
    d_o = _mm("mm_bm_t", dym_p, w["w_branch_mla"], tb=True, out_dtype=MXU_DTYPE)
    dqn, dqp, dkn, dkp_h, dv = _attn_bwd4(geo, s["qn"], s["qp"], s["kn"], s["kp"], s["v"], d_o, s["o"], s["lse"])
    rope_tabs = [(tab["cos"], LANE, 0), (tab["sin"], LANE, 0)]
    (dqp_raw,) = rw("rope_q_bwd", _per_head(_rope_t), tr=tr, rows=[(dqp, geo.hq, 0)], tabs=rope_tabs,
                    outs=[(geo.hq, geo.hq, MXU_DTYPE)], tab_blocks=tb)

    def rope_k_bwd(x, c, sn):
        tot = x[:, :LANE]
        for hd in range(1, MLA_HEADS):
            tot = tot + x[:, hd * LANE:(hd + 1) * LANE]
        return (_rope_t(tot, c, sn),)

    (dproj,) = rw("rope_k_bwd", rope_k_bwd, tr=tr, rows=[(dkp_h, geo.hq, 0)], tabs=rope_tabs,
                  outs=[(geo.pw, LANE, MXU_DTYPE, geo.cb("k_rope"), dproj)], tab_blocks=tb)
    g["w_qn"] = _mm("mm_qn_g", s["cq_n"], dqn, ta=True, out_dtype=MXU_DTYPE)
    g["w_qp"] = _mm("mm_qp_g", s["cq_n"], dqp_raw, ta=True, out_dtype=MXU_DTYPE)
    dcq_n = _mm("mm_qp_t", dqp_raw, w["w_qp"], tb=True, add=_mm("mm_qn_t", dqn, w["w_qn"], tb=True))
    g["w_k"] = _mm("mm_kn_g", s["ckv_n"], dkn, ta=True, out_dtype=MXU_DTYPE)
    g["w_v"] = _mm("mm_v_g", s["ckv_n"], dv, ta=True, out_dtype=MXU_DTYPE)
    dckv_n = _mm("mm_v_t", dv, w["w_v"], tb=True, add=_mm("mm_kn_t", dkn, w["w_k"], tb=True))
    dproj, g["q_norm_w"] = rw("rms_q_bwd", rms_bwd_nores, tr=tr,
                              rows=[(proj, MLA_Q_LORA, geo.cb("c_q")), (dcq_n, MLA_Q_LORA, 0)],
                              vecs=[(w["q_norm_w"], MLA_Q_LORA, 0)],
                              outs=[(geo.pw, MLA_Q_LORA, MXU_DTYPE, geo.cb("c_q"), dproj)], reds=[(MLA_Q_LORA, MLA_Q_LORA)])
    dproj, g["kv_norm_w"] = rw("rms_kv_bwd", rms_bwd_nores, tr=tr,
                               rows=[(proj, MLA_KV_LORA, geo.cb("c_kv")), (dckv_n, MLA_KV_LORA, 0)],
                               vecs=[(w["kv_norm_w"], MLA_KV_LORA, 0)],
                               outs=[(geo.pw, MLA_KV_LORA, MXU_DTYPE, geo.cb("c_kv"), dproj)],
                               reds=[(MLA_KV_LORA, MLA_KV_LORA)])
    gw_ = SSM_D_INNER // SSM_GROUPS
    d_skip_full = w["d_skip_full"] if mid is None else w["d_skip_full"] + mid(g)[0, 0]

    def gate_norm_bwd(y, x, z, dy, dsk, nw):
        f = lambda y_, x_, z_, dsk_, nw_: _rms((y_ + x_ * dsk_) * _silu(z_), nw_)
        _, vjp = jax.vjp(f, y.astype(F32), x.astype(F32), z.astype(F32), dsk, nw)
        dy_, dx_, dz_, ddsk, dnw = vjp(dy.astype(F32))
        return dy_, dx_, dz_, ddsk, dnw

    dy_ssd, dxs_skip, dproj, g["d_skip_full"], g["ssm_norm_w"] = rw(
        "ssm_gate_norm_bwd", gate_norm_bwd, tr=tr, ncb=SSM_GROUPS,
        rows=[(s["y_ssd"], gw_, 0), (s["xc"], gw_, 0), (proj, gw_, geo.col["z"][0] // gw_), (dy_ssm, gw_, 0)],
        vecs=[(d_skip_full, gw_, 0), (w["ssm_norm_w"], gw_, 0)],
        outs=[(SSM_D_INNER, gw_, MXU_DTYPE), (SSM_D_INNER, gw_, MXU_DTYPE),
              (geo.pw, gw_, MXU_DTYPE, geo.col["z"][0] // gw_, dproj)],
        reds=[(SSM_D_INNER, gw_), (SSM_D_INNER, gw_)])
    dxc, dproj, g["dt_bias"], g["a_log"] = _ssd_bwd_g(geo, s["xc"], s["proj_dt"], w["dt_bias"], w["a_log"], s["s_prev"],
                                                     dy_ssd, dxs_skip, dproj)
    dproj, g["conv_w"], g["conv_b"] = _conv_bwd(geo, proj, w["conv_w"], w["conv_b"], dxc, dproj)
    g["w_in_pt"] = _mm("mm_in_g", dproj, s["u"], ta=True, out_dtype=MXU_DTYPE)
    du = _mm("mm_in_t", dproj, w["w_in_pt"], dep=None if tail is None else tail(g))
    dh, g["norm_mix_w"] = rw("rms_mix_bwd", rms_bwd, tr=tr,
                             rows=[(s["h"], D_MODEL, 0), (du, D_MODEL, 0), (dh2, D_MODEL, 0)],
                             vecs=[(w["norm_mix_w"], D_MODEL, 0)], outs=[(D_MODEL, D_MODEL, F32)],
                             reds=[(D_MODEL, D_MODEL)])
    return dh, g


def _loss_bwd(geo, h, fw, target, tab):
    tr = geo.tr

    def fn(x, tgt, gw, tok):
        def lossf(x_, gw_):
            err = jnp.square(_rms(x_, gw_) - tgt)
            return 0.5 * jnp.sum(tok * jnp.mean(err, axis=-1, keepdims=True), axis=0, keepdims=True)

        val, vjp = jax.vjp(lossf, x, gw)
        dx, dgw = vjp(jnp.ones((1, 1), F32))
        return dx, jnp.broadcast_to(val, (1, LANE)), dgw

    return _rowwise("loss", fn, nrows=geo.nrows, tr=tr, rows=[(h, D_MODEL, 0), (target, D_MODEL, 0)],
                    vecs=[(fw, D_MODEL, 0)], tabs=[(tab["token"], 1, 0)], outs=[(D_MODEL, D_MODEL, F32)],
                    reds=[(LANE, LANE), (D_MODEL, D_MODEL)], tab_blocks=geo.lp // tr)


def kernel(x, meta_tokens, norm_mix_w, w_in, conv_w, conv_b, dt_bias, a_log, d_skip, ssm_norm_w, q_norm_w, kv_norm_w, w_uq, w_ukv, w_branch_ssm, w_branch_mla, w_out, norm_mlp_w, w_mlp_up, w_mlp_down, final_norm_w, loss_target, m_meta_tokens, m_norm_mix_w, m_w_in, m_conv_w, m_conv_b, m_dt_bias, m_a_log, m_d_skip, m_ssm_norm_w, m_q_norm_w, m_kv_norm_w, m_w_uq, m_w_ukv, m_w_branch_ssm, m_w_branch_mla, m_w_out, m_norm_mlp_w, m_w_mlp_up, m_w_mlp_down, m_final_norm_w, v_meta_tokens, v_norm_mix_w, v_w_in, v_conv_w, v_conv_b, v_dt_bias, v_a_log, v_d_skip, v_ssm_norm_w, v_q_norm_w, v_kv_norm_w, v_w_uq, v_w_ukv, v_w_branch_ssm, v_w_branch_mla, v_w_out, v_norm_mlp_w, v_w_mlp_up, v_w_mlp_down, v_final_norm_w):
    args = dict(locals())
    turn = lambda n, a: jnp.swapaxes(a, 1, 2) if n == "w_in" else a
    wts = {n: turn(n, args[n]) for n in WEIGHTS}
    mom = {n: turn(n, args["m_" + n]) for n in WEIGHTS}
    var = {n: turn(n, args["v_" + n]) for n in WEIGHTS}
    bsz, seq, _ = x.shape
    depth = w_in.shape[0]
    geo = _Geo(bsz, seq)
    tab = _tables(geo)

    big_names = [n for n, _ in BIG]
    sh_names = big_names + [n for n, _ in SHARDED_F32]
    kinds = dict(BIG + SHARDED_F32, w_in="row")
    shard3 = lambda a: a.reshape((1,) + a.shape) if a.ndim == 2 else a
    wire = {n: (MXU_DTYPE if n in big_names else F32) for n in sh_names}
    cast = {n: shard3(wts[n]).astype(wire[n]) for n in sh_names}
    per_layer = [n for n in sh_names if n != "meta_tokens"]
    small_names = ["norm_mix_w", "conv_b", "dt_bias", "a_log", "d_skip", "ssm_norm_w", "q_norm_w", "kv_norm_w",
                   "norm_mlp_w"]

    def gather_items(pairs):
        ins, outs, items, forms = [], [], [], []
        for n, i in pairs:
            a, b = cast[n].shape[1:]
            shape, dst, form = _gather_plan(a, b, kinds[n])
            items.append((len(ins), len(outs), (lambda ref, p, i=i: ref.at[i]), dst))
            ins.append(cast[n])
            outs.append(jax.ShapeDtypeStruct(shape, wire[n]))
            forms.append(form)
        return ins, outs, items, forms

    def whole_weights(pairs, forms, got):
        by_layer = {}
        for (n, i), form, g in zip(pairs, forms, got):
            if n == "w_in":
                n, g = "w_in_pt", _w_in_assemble(geo, g)
            elif form == "row":
                g = g.reshape(g.shape[0] * g.shape[1], g.shape[2])
            elif form == "stack":
                g = _unshard(g, "col")
            by_layer.setdefault(i, {})[n] = g
        return by_layer

    def prep(i, whole, token=None):
        wl = dict(whole)
        wl.update({n: wts[n][i] for n in small_names})
        if token is not None:
            wl["norm_mix_w"] = wl["norm_mix_w"] + token[0, 0]
        return _prep_layer(geo, wl)

    early = ("w_in", "conv_w")
    late_names = [n for n in per_layer if n not in early]
    pairs1 = [(n, i) for i in range(1, depth) for n in per_layer]
    groups = [[(n, 0) for n in early] + [("meta_tokens", 0)], [(n, 0) for n in late_names]] + ([pairs1] if pairs1 else [])
    started = {}

    def gather_start(gi, dep=None):
        ins, outs, items, forms = gather_items(groups[gi])
        sems, thru, landing, token = _exchange_start("gather_w%d_start" % gi, ins, outs, items, dep)
        started[gi] = (groups[gi], forms, sems, thru, landing, items)
        return token

    def gathered(gi, after):
        pairs, forms, sems, thru, landing, items = started[gi]
        return whole_weights(pairs, forms, _exchange_wait("gather_w%d_wait" % gi, sems, thru, landing, items, after))

    def late0(after):
        whole = gathered(1, after)[0]
        if pairs1:
            whole["q_norm_w"] = wts["q_norm_w"][0] + gather_start(2, whole["w_out"])[0, 0]
        return _prep_layer(geo, whole)

    token = gather_start(1, gather_start(0))
    whole0 = gathered(0, token)[0]
    meta_full = whole0.pop("meta_tokens")

    meta = jnp.broadcast_to(meta_full[None], (bsz, N_META, D_MODEL))
    h = jnp.concatenate([jnp.zeros((bsz, geo.pad, D_MODEL), F32), meta, x], axis=1).reshape(geo.nrows, D_MODEL)
    target = jnp.concatenate([jnp.zeros((bsz, geo.pad + N_META, D_MODEL), F32), loss_target], axis=1)
    target = target.reshape(geo.nrows, D_MODEL)
    layers, saved = [], []
    for i in range(depth):
        if i == 0:
            w, late = prep(0, whole0, token), late0
        else:
            if i == 1:
                whole1 = gathered(2, h)
            w, late = prep(i, whole1[i]), None
        h, s, w = _layer_fwd(geo, h, w, tab, late)
        layers.append(w)
        saved.append(s)
    dh, loss_part, g_final = _loss_bwd(geo, h, final_norm_w.reshape(1, -1), target, tab)

    def scatter_items(pairs):
        ins, outs, items = [], [], []
        for n, i in pairs:
            a, b = cast[n].shape[1:]
            arr = g_meta if n == "meta_tokens" else grads[i]["w_in_pt" if n == "w_in" else n]
            if n == "w_in":
                arr, src = _w_in_split(geo, arr, a), _entry
            elif kinds[n] == "row":
                src = lambda ref, p, a=a: ref.at[pl.ds(pl.multiple_of(p * a, a), a)]
            elif b % LANE == 0:
                src = lambda ref, p, b=b: ref.at[:, pl.ds(pl.multiple_of(p * b, b), b)]
            else:
                arr, src = _shard(arr, "col"), _entry
            items.append((len(ins), len(outs), src, _entry))
            ins.append(arr.astype(wire[n]))
            outs.append(jax.ShapeDtypeStruct((N_DEV, a, b), wire[n]))
        return ins, outs, items

    grads = [None] * depth
    landed, pending, res = {}, {}, {}

    def scatter_start(name, pairs):
        ins, outs, items = scatter_items(pairs)
        sems, thru, landing, token = _exchange_start(name + "_start", ins, outs, items)
        pending[name] = (pairs, sems, thru, landing, items)
        return token

    def scatter_wait(name, after):
        pairs, sems, thru, landing, items = pending[name]
        landed.update(zip(pairs, _exchange_wait(name + "_wait", sems, thru, landing, items, after)))

    def adam(n):
        parts = [landed[(n, i)] for i in range(cast[n].shape[0])]
        r = _adamw_nat("adamw_" + n, parts, shard3(wts[n]), shard3(mom[n]), shard3(var[n]))
        res[n] = [a.reshape(wts[n].shape) for a in r]

    def mid0(g):
        grads[0] = _unprep_grads(geo, g)
        return scatter_start("scatter_gb0", [(n, 0) for n in late_names])

    def tail0(g):
        grads[0] = _unprep_grads(geo, g)
        return scatter_start("scatter_ga0", [(n, 0) for n in early])

    dep = None
    for i in reversed(range(depth)):
        dh, gl = _layer_bwd(geo, dh, saved[i], layers[i], tab, *((mid0, tail0) if i == 0 else (None, None)), dep)
        grads[i] = _unprep_grads(geo, gl)
        if i == 1:
            dep = scatter_start("scatter_g1", pairs1)
    dh = dh.reshape(bsz, geo.lp, D_MODEL)
    grad_x = dh[:, geo.pad + N_META:]
    g_meta = jnp.sum(dh[:, geo.pad:geo.pad + N_META], axis=0)
    if pairs1:
        scatter_wait("scatter_g1", g_meta)
    scatter_wait("scatter_gb0", g_meta)
    for n in late_names:
        adam(n)
    g_small = {n: jnp.stack([grads[i][n] for i in range(depth)]) for n in SMALL if n != "final_norm_w"}
    g_small["final_norm_w"] = g_final.reshape(-1)
    zero = jnp.zeros((1,), F32)
    pk = lambda d, last: _pack([d[n] for n in SMALL] + [last], F32, row_mult=8)
    packed = pk(g_small, loss_part[0, :1])
    ins, outs, items = scatter_items([("meta_tokens", 0)])
    parts, landed[("meta_tokens", 0)] = _exchange(
        "gather_g", [packed] + ins + [res[n][1] for n in late_names],
        [jax.ShapeDtypeStruct((N_DEV,) + packed.shape, F32)] + outs,
        [(0, 0, _whole, _entry)] + [(1, 1, items[0][2], items[0][3])])
    adam("meta_tokens")
    scatter_wait("scatter_ga0", res["meta_tokens"][1])
    for n in early:
        adam(n)
    res_sm = _adamw("adamw_small", parts, pk(wts, zero), pk(mom, zero), pk(var, zero))
    res_sm = [_unpack(r, [wts[n].shape for n in SMALL] + [(1,)]) for r in res_sm]
    loss = res_sm[0][-1][0]

    out = [loss, grad_x]
    for k in range(4):
        named = {n: res[n][k] for n in sh_names}
        named.update(zip(SMALL, res_sm[k]))
        out += [turn(n, named[n]) for n in WEIGHTS]
    return tuple(out)
```
